```python
import math
import jax, jax.numpy as jnp
from jax import lax
import numpy as np

D_MODEL = 1024
BATCH = 8
SEQ = 4096
DEPTH = 1

MIX_WIDTH = D_MODEL
HG_HEADS = 4
HG_DK = 128
HG_DV = 128
HG_KEY_WIDTH = HG_HEADS * HG_DK
HG_VAL_WIDTH = HG_HEADS * HG_DV
HG_CHUNK = 64
ATT_HEADS = 8
ATT_KV_HEADS = 2
ATT_GROUP = ATT_HEADS // ATT_KV_HEADS
ATT_HD = 64
ATT_Q_WIDTH = ATT_HEADS * ATT_HD
ATT_KV_WIDTH = ATT_KV_HEADS * ATT_HD
WINDOW = 128
ATT_BLOCK = 128
HG_COLS = 2 * HG_KEY_WIDTH + 2 * HG_VAL_WIDTH
ATT_COLS = ATT_Q_WIDTH + 2 * ATT_KV_WIDTH
IN_COLS = HG_COLS + ATT_COLS
D_FF = 2816
CONV_W = 3
EPS = 1e-6

kernel_name = "hymba_hgrn2_swa_sink_convffn"


def rmsnorm(x, w, eps=EPS):
    xf = x.astype(jnp.float32)
    inv = lax.rsqrt(jnp.mean(xf * xf, axis=-1, keepdims=True) + eps)
    return (xf * inv * w.astype(jnp.float32)).astype(x.dtype)


def hgrn2_chunkwise(q, k, v, log_f):
    B, S, H, DK = q.shape
    DV = v.shape[-1]
    n = S // HG_CHUNK

    def to_chunks(a):
        return a.astype(jnp.float32).reshape(B, n, HG_CHUNK, H, a.shape[-1]).transpose(1, 0, 3, 2, 4)

    qc, kc, vc, gc = to_chunks(q), to_chunks(k), to_chunks(v), to_chunks(log_f)
    causal = jnp.tril(jnp.ones((HG_CHUNK, HG_CHUNK), dtype=bool))[:, :, None]

    def step(state, inp):
        qi, ki, vi, gi = inp
        b = jnp.cumsum(gi, axis=2)
        rel = b[:, :, :, None, :] - b[:, :, None, :, :]
        decay = jnp.exp(jnp.where(causal, rel, -jnp.inf))
        scores = jnp.einsum('bhtk,bhsk,bhtsk->bhts', qi, ki, decay)
        intra = jnp.einsum('bhts,bhsv->bhtv', scores, vi)
        inter = jnp.einsum('bhtk,bhkv->bhtv', qi * jnp.exp(b), state)
        b_last = b[:, :, -1:, :]
        new_state = state * jnp.exp(b_last)[:, :, 0, :, None] + jnp.einsum(
            'bhsk,bhsv->bhkv', ki * jnp.exp(b_last - b), vi)
        return new_state, intra + inter

    state0 = jnp.zeros((B, H, DK, DV), jnp.float32)
    _, out = lax.scan(step, state0, (qc, kc, vc, gc))
    return out.transpose(1, 0, 3, 2, 4).reshape(B, S, H, DV)


def sliding_window_attention_with_sinks(q, k, v, sinks):
    B, S = q.shape[0], q.shape[1]
    nb = S // ATT_BLOCK
    scale = 1.0 / math.sqrt(ATT_HD)
    qb = q.astype(jnp.float32).reshape(B, nb, ATT_BLOCK, ATT_KV_HEADS, ATT_GROUP, ATT_HD)

    def band_keys(a):
        ap = jnp.pad(a.astype(jnp.float32), ((0, 0), (ATT_BLOCK, 0), (0, 0), (0, 0)))
        ap = ap.reshape(B, nb + 1, ATT_BLOCK, ATT_KV_HEADS, ATT_HD)
        return jnp.concatenate([ap[:, :-1], ap[:, 1:]], axis=2)

    kb, vb = band_keys(k), band_keys(v)
    scores = jnp.einsum('bnqhgd,bnkhd->bnhgqk', qb, kb) * scale
    qi = jnp.arange(ATT_BLOCK)[:, None]
    kj = jnp.arange(2 * ATT_BLOCK)[None, :]
    dist = qi + ATT_BLOCK - kj
    band = (dist >= 0) & (dist < WINDOW)
    key_pos = jnp.arange(nb)[:, None] * ATT_BLOCK + jnp.arange(2 * ATT_BLOCK)[None, :] - ATT_BLOCK
    mask = band[None] & (key_pos >= 0)[:, None, :]
    scores = jnp.where(mask[None, :, None, None], scores, -jnp.inf)
    sink = sinks.astype(jnp.float32).reshape(ATT_KV_HEADS, ATT_GROUP)[None, None, :, :, None, None]
    m = jnp.maximum(jnp.max(scores, axis=-1, keepdims=True), sink)
    p = jnp.exp(scores - m)
    denom = jnp.sum(p, axis=-1, keepdims=True) + jnp.exp(sink - m)
    out = jnp.einsum('bnhgqk,bnkhd->bnqhgd', p / denom, vb)
    return out.reshape(B, S, ATT_HEADS * ATT_HD)


def causal_depthwise_conv(a, w, b):
    C = a.shape[-1]
    y = lax.conv_general_dilated(
        a, w[:, None, :].astype(a.dtype), window_strides=(1,), padding=[(CONV_W - 1, 0)],
        dimension_numbers=('NWC', 'WIO', 'NWC'), feature_group_count=C)
    return y + b.astype(a.dtype)


def _fwd_setup_inputs(seed: int = 0) -> dict:
    key = jax.random.key(seed)
    ks = jax.random.split(key, 16)
    f32 = jnp.float32
    nrm = lambda k, shape, s: jax.random.normal(k, shape, f32) * s
    return {
        "x": jax.random.normal(ks[0], (BATCH, SEQ, D_MODEL), f32),
        "norm_mix_w": 1.0 + nrm(ks[1], (DEPTH, D_MODEL), 0.02),
        "w_in": nrm(ks[2], (DEPTH, D_MODEL, IN_COLS), D_MODEL ** -0.5),
        "b_attn": nrm(ks[3], (DEPTH, ATT_COLS), 0.02),
        "lb_logits": nrm(ks[4], (DEPTH + 1, HG_KEY_WIDTH), 0.1),
        "hg_norm_w": 1.0 + nrm(ks[5], (DEPTH, HG_DV), 0.02),
        "sinks": nrm(ks[6], (DEPTH, ATT_HEADS), 0.5),
        "w_out": nrm(ks[7], (DEPTH, MIX_WIDTH, D_MODEL), MIX_WIDTH ** -0.5),
        "norm_ffn_w": 1.0 + nrm(ks[8], (DEPTH, D_MODEL), 0.02),
        "w_gate": nrm(ks[9], (DEPTH, D_MODEL, D_FF), D_MODEL ** -0.5),
        "w_up": nrm(ks[10], (DEPTH, D_MODEL, D_FF), D_MODEL ** -0.5),
        "conv_w": nrm(ks[11], (DEPTH, CONV_W, D_FF), CONV_W ** -0.5),
        "conv_b": nrm(ks[12], (DEPTH, D_FF), 0.02),
        "w_down": nrm(ks[13], (DEPTH, D_FF, D_MODEL), D_FF ** -0.5),
        "final_norm_w": 1.0 + nrm(ks[14], (D_MODEL,), 0.02),
    }


def _fwd_reference(x, norm_mix_w, w_in, b_attn, lb_logits, hg_norm_w, sinks, w_out,
              norm_ffn_w, w_gate, w_up, conv_w, conv_b, w_down, final_norm_w):
    B, S, _ = x.shape
    lb_all = jnp.cumsum(jax.nn.softmax(lb_logits.astype(jnp.float32), axis=0), axis=0)[:DEPTH]
    h = x
    for l in range(DEPTH):
        u = rmsnorm(h, norm_mix_w[l])
        proj = u @ w_in[l]
        hq, hf, hi, hg, att = jnp.split(
            proj, [HG_KEY_WIDTH, 2 * HG_KEY_WIDTH, 2 * HG_KEY_WIDTH + HG_VAL_WIDTH, HG_COLS], axis=-1)
        lb = lb_all[l]
        f = lb + (1.0 - lb) * jax.nn.sigmoid(hf.astype(jnp.float32))
        log_f = jnp.log(f).reshape(B, S, HG_HEADS, HG_DK)
        k_hg = (1.0 - f).reshape(B, S, HG_HEADS, HG_DK)
        q_hg = hq.astype(jnp.float32).reshape(B, S, HG_HEADS, HG_DK) * (HG_DK ** -0.5)
        v_hg = hi.reshape(B, S, HG_HEADS, HG_DV)
        o_hg = hgrn2_chunkwise(q_hg, k_hg, v_hg, log_f)
        o_hg = rmsnorm(o_hg, hg_norm_w[l]).reshape(B, S, HG_VAL_WIDTH)
        o_hg = (o_hg * jax.nn.silu(hg.astype(jnp.float32))).astype(h.dtype)
        att = att + b_attn[l]
        aq, ak, av = jnp.split(att, [ATT_Q_WIDTH, ATT_Q_WIDTH + ATT_KV_WIDTH], axis=-1)
        o_att = sliding_window_attention_with_sinks(
            aq.reshape(B, S, ATT_HEADS, ATT_HD),
            ak.reshape(B, S, ATT_KV_HEADS, ATT_HD),
            av.reshape(B, S, ATT_KV_HEADS, ATT_HD),
            sinks[l]).astype(h.dtype)
        mix = jnp.concatenate([o_hg, o_att], axis=-1)
        h = h + mix @ w_out[l]
        v = rmsnorm(h, norm_ffn_w[l])
        gate = causal_depthwise_conv(v @ w_gate[l], conv_w[l], conv_b[l])
        h = h + (jax.nn.silu(gate) * (v @ w_up[l])) @ w_down[l]
    return rmsnorm(h, final_norm_w)


import jax as _jax
import jax.numpy as _jnp

TWIN_FORMAT = 'train_step'
FWD_PARAMS = ['x', 'norm_mix_w', 'w_in', 'b_attn', 'lb_logits', 'hg_norm_w', 'sinks', 'w_out', 'norm_ffn_w', 'w_gate', 'w_up', 'conv_w', 'conv_b', 'w_down', 'final_norm_w']
TWIN_WEIGHTS = ['norm_mix_w', 'w_in', 'b_attn', 'lb_logits', 'hg_norm_w', 'sinks', 'w_out', 'norm_ffn_w', 'w_gate', 'w_up', 'conv_w', 'conv_b', 'w_down', 'final_norm_w']
TWIN_DIFF_INPUT = 'x'
TWIN_INPUTS = ['x', 'norm_mix_w', 'w_in', 'b_attn', 'lb_logits', 'hg_norm_w', 'sinks', 'w_out', 'norm_ffn_w', 'w_gate', 'w_up', 'conv_w', 'conv_b', 'w_down', 'final_norm_w', 'loss_target', 'm_norm_mix_w', 'm_w_in', 'm_b_attn', 'm_lb_logits', 'm_hg_norm_w', 'm_sinks', 'm_w_out', 'm_norm_ffn_w', 'm_w_gate', 'm_w_up', 'm_conv_w', 'm_conv_b', 'm_w_down', 'm_final_norm_w', 'v_norm_mix_w', 'v_w_in', 'v_b_attn', 'v_lb_logits', 'v_hg_norm_w', 'v_sinks', 'v_w_out', 'v_norm_ffn_w', 'v_w_gate', 'v_w_up', 'v_conv_w', 'v_conv_b', 'v_w_down', 'v_final_norm_w']
TWIN_OUTPUTS = ['loss', 'grad_x', 'grad_norm_mix_w', 'grad_w_in', 'grad_b_attn', 'grad_lb_logits', 'grad_hg_norm_w', 'grad_sinks', 'grad_w_out', 'grad_norm_ffn_w', 'grad_w_gate', 'grad_w_up', 'grad_conv_w', 'grad_conv_b', 'grad_w_down', 'grad_final_norm_w', 'delta_norm_mix_w', 'delta_w_in', 'delta_b_attn', 'delta_lb_logits', 'delta_hg_norm_w', 'delta_sinks', 'delta_w_out', 'delta_norm_ffn_w', 'delta_w_gate', 'delta_w_up', 'delta_conv_w', 'delta_conv_b', 'delta_w_down', 'delta_final_norm_w', 'new_m_norm_mix_w', 'new_m_w_in', 'new_m_b_attn', 'new_m_lb_logits', 'new_m_hg_norm_w', 'new_m_sinks', 'new_m_w_out', 'new_m_norm_ffn_w', 'new_m_w_gate', 'new_m_w_up', 'new_m_conv_w', 'new_m_conv_b', 'new_m_w_down', 'new_m_final_norm_w', 'new_v_norm_mix_w', 'new_v_w_in', 'new_v_b_attn', 'new_v_lb_logits', 'new_v_hg_norm_w', 'new_v_sinks', 'new_v_w_out', 'new_v_norm_ffn_w', 'new_v_w_gate', 'new_v_w_up', 'new_v_conv_w', 'new_v_conv_b', 'new_v_w_down', 'new_v_final_norm_w']
TWIN_LEAF_KINDS = {'loss': 'loss', 'grad_x': 'grad_x', 'grad_norm_mix_w': 'grad_w', 'grad_w_in': 'grad_w', 'grad_b_attn': 'grad_w', 'grad_lb_logits': 'grad_w', 'grad_hg_norm_w': 'grad_w', 'grad_sinks': 'grad_w', 'grad_w_out': 'grad_w', 'grad_norm_ffn_w': 'grad_w', 'grad_w_gate': 'grad_w', 'grad_w_up': 'grad_w', 'grad_conv_w': 'grad_w', 'grad_conv_b': 'grad_w', 'grad_w_down': 'grad_w', 'grad_final_norm_w': 'grad_w', 'delta_norm_mix_w': 'delta_w', 'delta_w_in': 'delta_w', 'delta_b_attn': 'delta_w', 'delta_lb_logits': 'delta_w', 'delta_hg_norm_w': 'delta_w', 'delta_sinks': 'delta_w', 'delta_w_out': 'delta_w', 'delta_norm_ffn_w': 'delta_w', 'delta_w_gate': 'delta_w', 'delta_w_up': 'delta_w', 'delta_conv_w': 'delta_w', 'delta_conv_b': 'delta_w', 'delta_w_down': 'delta_w', 'delta_final_norm_w': 'delta_w', 'new_m_norm_mix_w': 'new_m', 'new_m_w_in': 'new_m', 'new_m_b_attn': 'new_m', 'new_m_lb_logits': 'new_m', 'new_m_hg_norm_w': 'new_m', 'new_m_sinks': 'new_m', 'new_m_w_out': 'new_m', 'new_m_norm_ffn_w': 'new_m', 'new_m_w_gate': 'new_m', 'new_m_w_up': 'new_m', 'new_m_conv_w': 'new_m', 'new_m_conv_b': 'new_m', 'new_m_w_down': 'new_m', 'new_m_final_norm_w': 'new_m', 'new_v_norm_mix_w': 'new_v', 'new_v_w_in': 'new_v', 'new_v_b_attn': 'new_v', 'new_v_lb_logits': 'new_v', 'new_v_hg_norm_w': 'new_v', 'new_v_sinks': 'new_v', 'new_v_w_out': 'new_v', 'new_v_norm_ffn_w': 'new_v', 'new_v_w_gate': 'new_v', 'new_v_w_up': 'new_v', 'new_v_conv_w': 'new_v', 'new_v_conv_b': 'new_v', 'new_v_w_down': 'new_v', 'new_v_final_norm_w': 'new_v'}


def _forward(args):
    return _fwd_reference(*[args[k] for k in FWD_PARAMS])


def _output_shape():
    out = _jax.eval_shape(lambda: _forward(_fwd_setup_inputs(0)))
    return out.shape, out.dtype

N_MICROBATCH = 1
ADAM_LR = 0.001
ADAM_B1 = 0.9
ADAM_B2 = 0.999
ADAM_EPS = 1e-08
ADAM_WD = 0.01
ADAM_STEP = 10
PER_EXAMPLE_BATCH_AXIS = {'x': 0, 'loss_target': 0}
SHARED_INPUTS = []
_WEIGHT_DTYPES = {'norm_mix_w': _jnp.float32, 'w_in': _jnp.float32, 'b_attn': _jnp.float32, 'lb_logits': _jnp.float32, 'hg_norm_w': _jnp.float32, 'sinks': _jnp.float32, 'w_out': _jnp.float32, 'norm_ffn_w': _jnp.float32, 'w_gate': _jnp.float32, 'w_up': _jnp.float32, 'conv_w': _jnp.float32, 'conv_b': _jnp.float32, 'w_down': _jnp.float32, 'final_norm_w': _jnp.float32}
MOMENT_SCALE = {'norm_mix_w': 1.733431e-01, 'w_in': 1.027343e-01, 'b_attn': 1.656286e-01, 'lb_logits': 7.219646e-02, 'hg_norm_w': 2.356749e-01, 'sinks': 3.556164e-02, 'w_out': 8.035383e-02, 'norm_ffn_w': 1.274762e-01, 'w_gate': 5.389734e-02, 'w_up': 5.252493e-02, 'conv_w': 5.544339e-02, 'conv_b': 5.183070e-02, 'w_down': 8.694764e-02, 'final_norm_w': 3.198261e+01}


def _to_microbatches(a, axis):
    t = _jnp.moveaxis(a, axis, 0)
    t = t.reshape((N_MICROBATCH, t.shape[0] // N_MICROBATCH) + t.shape[1:])
    return _jnp.moveaxis(t, 1, axis + 1)


def setup_inputs(seed: int = 0) -> dict:
    inp = _fwd_setup_inputs(seed)
    key = _jax.random.fold_in(_jax.random.key(seed), 7919)
    shape, _ = _output_shape()
    out = dict(inp)
    out["loss_target"] = _jax.random.normal(_jax.random.fold_in(key, 0), shape, _jnp.float32)
    for i, name in enumerate(TWIN_WEIGHTS):
        w = inp[name].astype(_jnp.float32)
        if MOMENT_SCALE is None:
            s = _jnp.sqrt(_jnp.mean(_jnp.square(w)) + 1e-30)
        else:
            s = MOMENT_SCALE[name]
        km, kv = _jax.random.split(_jax.random.fold_in(key, i + 1))
        out[name] = w
        out["m_" + name] = s * _jax.random.normal(km, w.shape, _jnp.float32)
        out["v_" + name] = (s * s) * _jax.random.uniform(kv, w.shape, _jnp.float32, 0.5, 1.5)
    if N_MICROBATCH > 1:
        for name, axis in PER_EXAMPLE_BATCH_AXIS.items():
            out[name] = _to_microbatches(out[name], axis)
    return {'x': out['x'], 'norm_mix_w': out['norm_mix_w'], 'w_in': out['w_in'], 'b_attn': out['b_attn'], 'lb_logits': out['lb_logits'], 'hg_norm_w': out['hg_norm_w'], 'sinks': out['sinks'], 'w_out': out['w_out'], 'norm_ffn_w': out['norm_ffn_w'], 'w_gate': out['w_gate'], 'w_up': out['w_up'], 'conv_w': out['conv_w'], 'conv_b': out['conv_b'], 'w_down': out['w_down'], 'final_norm_w': out['final_norm_w'], 'loss_target': out['loss_target'], 'm_norm_mix_w': out['m_norm_mix_w'], 'm_w_in': out['m_w_in'], 'm_b_attn': out['m_b_attn'], 'm_lb_logits': out['m_lb_logits'], 'm_hg_norm_w': out['m_hg_norm_w'], 'm_sinks': out['m_sinks'], 'm_w_out': out['m_w_out'], 'm_norm_ffn_w': out['m_norm_ffn_w'], 'm_w_gate': out['m_w_gate'], 'm_w_up': out['m_w_up'], 'm_conv_w': out['m_conv_w'], 'm_conv_b': out['m_conv_b'], 'm_w_down': out['m_w_down'], 'm_final_norm_w': out['m_final_norm_w'], 'v_norm_mix_w': out['v_norm_mix_w'], 'v_w_in': out['v_w_in'], 'v_b_attn': out['v_b_attn'], 'v_lb_logits': out['v_lb_logits'], 'v_hg_norm_w': out['v_hg_norm_w'], 'v_sinks': out['v_sinks'], 'v_w_out': out['v_w_out'], 'v_norm_ffn_w': out['v_norm_ffn_w'], 'v_w_gate': out['v_w_gate'], 'v_w_up': out['v_w_up'], 'v_conv_w': out['v_conv_w'], 'v_conv_b': out['v_conv_b'], 'v_w_down': out['v_w_down'], 'v_final_norm_w': out['v_final_norm_w']}


def _loss(weights, diff, rest, loss_target):
    with _jax.named_scope("forward"):
        args = {**rest, TWIN_DIFF_INPUT: diff, **{k: w.astype(_WEIGHT_DTYPES[k]) for k, w in weights.items()}}
        y = _forward(args)
    with _jax.named_scope("loss_head"):
        err = _jnp.square(y.astype(_jnp.float32) - loss_target)
        return 0.5 * _jnp.sum(_jnp.mean(err, axis=-1)) if err.ndim else 0.5 * err


def _adamw(w, g, m, v):
    m = ADAM_B1 * m + (1.0 - ADAM_B1) * g
    v = ADAM_B2 * v + (1.0 - ADAM_B2) * _jnp.square(g)
    m_hat = m / (1.0 - ADAM_B1 ** ADAM_STEP)
    v_hat = v / (1.0 - ADAM_B2 ** ADAM_STEP)
    delta = -ADAM_LR * (m_hat / (_jnp.sqrt(v_hat) + ADAM_EPS) + ADAM_WD * w)
    return delta, m, v


def reference(x, norm_mix_w, w_in, b_attn, lb_logits, hg_norm_w, sinks, w_out, norm_ffn_w, w_gate, w_up, conv_w, conv_b, w_down, final_norm_w, loss_target, m_norm_mix_w, m_w_in, m_b_attn, m_lb_logits, m_hg_norm_w, m_sinks, m_w_out, m_norm_ffn_w, m_w_gate, m_w_up, m_conv_w, m_conv_b, m_w_down, m_final_norm_w, v_norm_mix_w, v_w_in, v_b_attn, v_lb_logits, v_hg_norm_w, v_sinks, v_w_out, v_norm_ffn_w, v_w_gate, v_w_up, v_conv_w, v_conv_b, v_w_down, v_final_norm_w):
    given = dict(x=x, norm_mix_w=norm_mix_w, w_in=w_in, b_attn=b_attn, lb_logits=lb_logits, hg_norm_w=hg_norm_w, sinks=sinks, w_out=w_out, norm_ffn_w=norm_ffn_w, w_gate=w_gate, w_up=w_up, conv_w=conv_w, conv_b=conv_b, w_down=w_down, final_norm_w=final_norm_w, loss_target=loss_target, m_norm_mix_w=m_norm_mix_w, m_w_in=m_w_in, m_b_attn=m_b_attn, m_lb_logits=m_lb_logits, m_hg_norm_w=m_hg_norm_w, m_sinks=m_sinks, m_w_out=m_w_out, m_norm_ffn_w=m_norm_ffn_w, m_w_gate=m_w_gate, m_w_up=m_w_up, m_conv_w=m_conv_w, m_conv_b=m_conv_b, m_w_down=m_w_down, m_final_norm_w=m_final_norm_w, v_norm_mix_w=v_norm_mix_w, v_w_in=v_w_in, v_b_attn=v_b_attn, v_lb_logits=v_lb_logits, v_hg_norm_w=v_hg_norm_w, v_sinks=v_sinks, v_w_out=v_w_out, v_norm_ffn_w=v_norm_ffn_w, v_w_gate=v_w_gate, v_w_up=v_w_up, v_conv_w=v_conv_w, v_conv_b=v_conv_b, v_w_down=v_w_down, v_final_norm_w=v_final_norm_w)
    weights = {n: given[n] for n in TWIN_WEIGHTS}
    shared = {n: given[n] for n in SHARED_INPUTS}
    per_example = {n: given[n] for n in ['x']}
    grad_fn = _jax.value_and_grad(_loss, argnums=(0, 1))

    def one_microbatch(ex, loss_target):
        ex = dict(ex)
        diff = ex.pop(TWIN_DIFF_INPUT)
        return grad_fn(weights, diff, {**shared, **ex}, loss_target)

    if N_MICROBATCH == 1:
        loss, (grad_w, grad_x) = one_microbatch(per_example, given["loss_target"])
    else:
        def body(carry, xs):
            loss_sum, grad_sum = carry
            l_k, (gw_k, gx_k) = one_microbatch(xs[0], xs[1])
            with _jax.named_scope("update"):
                return (loss_sum + l_k, _jax.tree.map(_jnp.add, grad_sum, gw_k)), gx_k

        init = (_jnp.zeros((), _jnp.float32), _jax.tree.map(_jnp.zeros_like, weights))
        (loss, grad_w), grad_x = _jax.lax.scan(body, init, (per_example, given["loss_target"]))
    with _jax.named_scope("update"):
        delta_w, new_m, new_v = {}, {}, {}
        for n in TWIN_WEIGHTS:
            delta_w[n], new_m[n], new_v[n] = _adamw(weights[n], grad_w[n], given["m_" + n], given["v_" + n])
    return (loss, grad_x, *[grad_w[n] for n in TWIN_WEIGHTS], *[delta_w[n] for n in TWIN_WEIGHTS],
            *[new_m[n] for n in TWIN_WEIGHTS], *[new_v[n] for n in TWIN_WEIGHTS])
```

```python
import functools
import math

import jax
import jax.numpy as jnp
from jax import lax
from jax.experimental import pallas as pl
from jax.experimental.pallas import tpu as pltpu

F32 = jnp.float32
BF16 = jnp.bfloat16

D_MODEL = 1024
HG_HEADS = 4
HG_DK = 128
HG_W = HG_HEADS * HG_DK
HG_CHUNK = 64
HG_SUB = 16
ATT_HEADS = 8
ATT_KV = 2
ATT_GROUP = ATT_HEADS // ATT_KV
ATT_HD = 64
ATT_BLOCK = 128
ATT_Q_W = ATT_HEADS * ATT_HD
ATT_KV_W = ATT_KV * ATT_HD
ATT_COLS = ATT_Q_W + 2 * ATT_KV_W
IN_COLS = 4 * HG_W + ATT_COLS
D_FF = 2816
EPS = 1e-6
ADAM_LR, ADAM_B1, ADAM_B2, ADAM_EPS, ADAM_WD, ADAM_STEP = 0.001, 0.9, 0.999, 1e-08, 0.01, 10
NEG = -1e30

V7X_VMEM_BYTES = 64 * 1024 * 1024
VMEM_LIMIT = 48 * 1024 * 1024
SUBLANES = 8

N_CHIPS = 4


def _cp(sem=None, **kw):
    return pltpu.CompilerParams(dimension_semantics=sem, vmem_limit_bytes=VMEM_LIMIT, **kw)


def _sds(shape, dtype):
    return jax.ShapeDtypeStruct(shape, dtype)


def _wspec(w):
    arr, rows, blk = w
    return pl.BlockSpec((rows, arr.shape[1]), lambda i: (blk, 0))


def _mm_nt(a, w, *, splits, out_dtype, name, residual=None, tm=512):
    M, K = a.shape
    N = w[1]
    tm = min(tm, M)
    assert sum(splits) == N and M % tm == 0
    offs = [sum(splits[:i]) for i in range(len(splits))]

    def body(*refs):
        a_ref, w_ref = refs[0], refs[1]
        outs = refs[2 + (residual is not None):]
        acc = lax.dot_general(a_ref[...], w_ref[...], (((1,), (1,)), ((), ())), preferred_element_type=F32)
        if residual is not None:
            acc = acc + refs[2][...]
        for o_ref, c0, n in zip(outs, offs, splits):
            o_ref[...] = acc[:, c0:c0 + n].astype(out_dtype)

    in_specs = [pl.BlockSpec((tm, K), lambda i: (i, 0)), _wspec(w)]
    args = [a, w[0]]
    if residual is not None:
        assert len(splits) == 1
        in_specs.append(pl.BlockSpec((tm, N), lambda i: (i, 0)))
        args.append(residual)
    outs = pl.pallas_call(
        body, name=name, grid=(M // tm,), in_specs=in_specs,
        out_specs=[pl.BlockSpec((tm, n), lambda i: (i, 0)) for n in splits],
        out_shape=[_sds((M, n), out_dtype) for n in splits],
        compiler_params=_cp(("parallel",)),
    )(*args)
    return outs


def _mm_nn(pieces, ws, *, out_dtype, name, residual=None, tm=512):
    M = pieces[0][0].shape[0]
    K = ws[0][0].shape[1]
    tm = min(tm, M)
    flat = [p for grp in pieces for p in grp]
    n_p = len(flat)

    def body(*refs):
        p_refs = refs[:n_p]
        w_refs = refs[n_p:n_p + len(ws)]
        o_ref = refs[-1]
        acc = None if residual is None else refs[n_p + len(ws)][...]
        k = 0
        for gi, grp in enumerate(pieces):
            c0 = 0
            for p in grp:
                n = p.shape[1]
                t = jnp.dot(p_refs[k][...], w_refs[gi][c0:c0 + n, :], preferred_element_type=F32)
                acc = t if acc is None else acc + t
                c0 += n
                k += 1
        o_ref[...] = acc.astype(out_dtype)

    in_specs = [pl.BlockSpec((tm, p.shape[1]), lambda i: (i, 0)) for p in flat]
    in_specs += [_wspec(w) for w in ws]
    args = [*flat, *[w[0] for w in ws]]
    if residual is not None:
        in_specs.append(pl.BlockSpec((tm, K), lambda i: (i, 0)))
        args.append(residual)
    return pl.pallas_call(
        body, name=name, grid=(M // tm,), in_specs=in_specs,
        out_specs=pl.BlockSpec((tm, K), lambda i: (i, 0)),
        out_shape=_sds((M, K), out_dtype),
        compiler_params=_cp(("parallel",)),
    )(*args)


def _mm_tn(pieces, x, *, name, tt=512):
    M, K = x.shape
    tt = min(tt, M)
    ns = [p.shape[1] for p in pieces]
    offs = [sum(ns[:i]) for i in range(len(ns))]
    N = sum(ns)
    n_p = len(pieces)

    def body(*refs):
        p_refs = refs[:n_p]
        x_ref = refs[n_p]
        o_ref = refs[n_p + 1]

        @pl.when(pl.program_id(0) == 0)
        def _():
            o_ref[...] = jnp.zeros_like(o_ref)

        xv = x_ref[...]
        for p_ref, c0, n in zip(p_refs, offs, ns):
            o_ref[c0:c0 + n, :] += lax.dot_general(p_ref[...], xv, (((0,), (0,)), ((), ())),
                                                    preferred_element_type=F32)

    in_specs = [pl.BlockSpec((tt, n), lambda i: (i, 0)) for n in ns]
    in_specs.append(pl.BlockSpec((tt, K), lambda i: (i, 0)))
    return pl.pallas_call(
        body, name=name, grid=(M // tt,), in_specs=in_specs,
        out_specs=pl.BlockSpec((N, K), lambda i: (0, 0)),
        out_shape=_sds((N, K), F32),
        compiler_params=_cp(("arbitrary",)),
    )(*pieces, x)


def _rms_fwd(xf, w):
    inv = lax.rsqrt(jnp.mean(xf * xf, axis=-1, keepdims=True) + EPS)
    return xf * inv * w


def _rms_bwd(xf, w, dy):
    inv = lax.rsqrt(jnp.mean(xf * xf, axis=-1, keepdims=True) + EPS)
    xhat = xf * inv
    dxhat = dy * w
    dx = inv * (dxhat - xhat * jnp.mean(dxhat * xhat, axis=-1, keepdims=True))
    dw = jnp.sum(dy * xhat, axis=0, keepdims=True)
    return dx, dw


def _sigmoid(x):
    return 1.0 / (1.0 + jnp.exp(-x))


def _rowwise(fn, row_ins, bc_ins, row_outs, acc_outs, *, name, tm=256):
    M = row_outs[0].shape[0] if row_outs else row_ins[0][0].shape[0]
    assert M % tm == 0 and tm % SUBLANES == 0, (name, M, tm)
    n_r, n_b, n_o, n_a = len(row_ins), len(bc_ins), len(row_outs), len(acc_outs)

    def body(*refs):
        ins = [r[...] for r in refs[:n_r + n_b]]
        o_refs = refs[n_r + n_b:n_r + n_b + n_o]
        a_refs = refs[n_r + n_b + n_o:]
        res = fn(*ins)
        for o_ref, val in zip(o_refs, res[:n_o]):
            o_ref[...] = val.astype(o_ref.dtype)
        if n_a:
            @pl.when(pl.program_id(0) == 0)
            def _():
                for a_ref in a_refs:
                    a_ref[...] = jnp.zeros_like(a_ref)
            for a_ref, val in zip(a_refs, res[n_o:]):
                a_ref[...] += val

    in_specs = [pl.BlockSpec((tm, cw), functools.partial(lambda i, cb, r0: (i + r0, cb), cb=cb, r0=r0))
                for (_, cw, cb, r0) in row_ins]
    in_specs += [pl.BlockSpec(b.shape, lambda i: (0, 0)) for b in bc_ins]
    out_specs = [pl.BlockSpec((tm, s.shape[1]), lambda i: (i, 0)) for s in row_outs]
    out_specs += [pl.BlockSpec(s.shape, lambda i: (0, 0)) for s in acc_outs]
    return pl.pallas_call(
        body, name=name, grid=(M // tm,), in_specs=in_specs, out_specs=out_specs,
        out_shape=list(row_outs) + list(acc_outs),
        compiler_params=_cp(("arbitrary",) if n_a else ("parallel",)),
    )(*[r[0] for r in row_ins], *bc_ins)


def _full(a, first_row_block=0):
    return (a, a.shape[1], 0, first_row_block)


def _conv_rows(ext, w_ref_val, lo):
    s1 = pltpu.roll(ext, 1, 0)
    s2 = pltpu.roll(ext, 2, 0)
    y = w_ref_val[0:1, :] * s2 + w_ref_val[1:2, :] * s1 + w_ref_val[2:3, :] * ext
    return y[SUBLANES:, :]


def _convact_fwd(gp, up, conv_w8, conv_b, *, name, tr=512, tc=256):
    T, C = gp.shape
    tr = min(tr, T)
    hb = tr // SUBLANES

    def body(gp_ref, gph_ref, up_ref, w_ref, b_ref, act_ref):
        i = pl.program_id(1)
        halo = jnp.where(i > 0, gph_ref[...], 0.0)
        ext = jnp.concatenate([halo, gp_ref[...]], axis=0)
        gate = _conv_rows(ext, w_ref[...], 0) + b_ref[...]
        act_ref[...] = (gate * _sigmoid(gate) * up_ref[...]).astype(act_ref.dtype)

    return pl.pallas_call(
        body, name=name, grid=(C // tc, T // tr),
        in_specs=[pl.BlockSpec((tr, tc), lambda j, i: (i, j)),
                  pl.BlockSpec((SUBLANES, tc), lambda j, i: (jnp.maximum(i * hb - 1, 0), j)),
                  pl.BlockSpec((tr, tc), lambda j, i: (i, j)),
                  pl.BlockSpec((SUBLANES, tc), lambda j, i: (0, j)),
                  pl.BlockSpec((1, tc), lambda j, i: (0, j))],
        out_specs=pl.BlockSpec((tr, tc), lambda j, i: (i, j)),
        out_shape=_sds((T, C), BF16),
        compiler_params=_cp(("parallel", "parallel")),
    )(gp, gp, up, conv_w8, conv_b)


def _convact_bwd(gp, up, dact, conv_w8, conv_b, *, name, tr=512, tc=256):
    T, C = gp.shape
    tr = min(tr, T)
    hb = tr // SUBLANES
    nr = T // tr

    def body(gp_ref, gpp_ref, gpn_ref, up_ref, upn_ref, da_ref, dan_ref, w_ref, b_ref,
             dgp_ref, dup_ref, dw_ref, db_ref):
        i = pl.program_id(1)
        w = w_ref[...]
        prev = jnp.where(i > 0, gpp_ref[...], 0.0)
        last = i == nr - 1
        gp_ext = jnp.concatenate([prev, gp_ref[...], gpn_ref[...]], axis=0)
        gate = _conv_rows(gp_ext, w, 0) + b_ref[...]
        up_e = jnp.concatenate([up_ref[...], upn_ref[...]], axis=0)
        da_e = jnp.concatenate([da_ref[...], dan_ref[...]], axis=0)
        row = lax.broadcasted_iota(jnp.int32, gate.shape, 0)
        valid = jnp.logical_or(row < tr, jnp.logical_not(last))
        sg = _sigmoid(gate)
        silu = gate * sg
        dgate = jnp.where(valid, da_e * up_e * (sg * (1.0 + gate * (1.0 - sg))), 0.0)
        dup_ref[...] = (da_e[:tr] * silu[:tr]).astype(dup_ref.dtype)
        n = tr + SUBLANES
        g1 = pltpu.roll(dgate, n - 1, 0)
        g2 = pltpu.roll(dgate, n - 2, 0)
        dgp = w[2:3, :] * dgate + w[1:2, :] * g1 + w[0:1, :] * g2
        dgp_ref[...] = dgp[:tr].astype(dgp_ref.dtype)
        gpc = gp_ref[...]
        dw0 = jnp.sum(gpc * g2[:tr], axis=0, keepdims=True)
        dw1 = jnp.sum(gpc * g1[:tr], axis=0, keepdims=True)
        dw2 = jnp.sum(gpc * dgate[:tr], axis=0, keepdims=True)
        dbv = jnp.sum(dgate[:tr], axis=0, keepdims=True)
        z = jnp.zeros((SUBLANES - 3, gpc.shape[1]), F32)

        @pl.when(i == 0)
        def _():
            dw_ref[...] = jnp.zeros_like(dw_ref)
            db_ref[...] = jnp.zeros_like(db_ref)

        dw_ref[...] += jnp.concatenate([dw0, dw1, dw2, z], axis=0)
        db_ref[...] += dbv

    cur = pl.BlockSpec((tr, tc), lambda j, i: (i, j))
    prv = pl.BlockSpec((SUBLANES, tc), lambda j, i: (jnp.maximum(i * hb - 1, 0), j))
    nxt = pl.BlockSpec((SUBLANES, tc), lambda j, i: (jnp.minimum((i + 1) * hb, T // SUBLANES - 1), j))
    return pl.pallas_call(
        body, name=name, grid=(C // tc, nr),
        in_specs=[cur, prv, nxt, cur, nxt, cur, nxt,
                  pl.BlockSpec((SUBLANES, tc), lambda j, i: (0, j)),
                  pl.BlockSpec((1, tc), lambda j, i: (0, j))],
        out_specs=[cur, cur,
                   pl.BlockSpec((SUBLANES, tc), lambda j, i: (0, j)),
                   pl.BlockSpec((1, tc), lambda j, i: (0, j))],
        out_shape=[_sds((T, C), BF16), _sds((T, C), BF16), _sds((SUBLANES, C), F32), _sds((1, C), F32)],
        compiler_params=_cp(("parallel", "arbitrary")),
    )(gp, gp, gp, up, up, dact, dact, conv_w8, conv_b)


def _cumsum_rows(x):
    n = x.shape[0]
    row = lax.broadcasted_iota(jnp.int32, x.shape, 0)
    s = 1
    while s < n:
        x = x + jnp.where(row >= s, pltpu.roll(x, s, 0), 0.0)
        s *= 2
    return x


def _rcumsum_rows(x):
    n = x.shape[0]
    row = lax.broadcasted_iota(jnp.int32, x.shape, 0)
    s = 1
    while s < n:
        x = x + jnp.where(row < n - s, pltpu.roll(x, n - s, 0), 0.0)
        s *= 2
    return x


def _dot_nt(a, b):
    return lax.dot_general(a.astype(BF16), b.astype(BF16), (((1,), (1,)), ((), ())), preferred_element_type=F32)


def _dot_tn(a, b):
    return lax.dot_general(a.astype(BF16), b.astype(BF16), (((0,), (0,)), ((), ())), preferred_element_type=F32)


def _dot_nn(a, b):
    return jnp.dot(a.astype(BF16), b.astype(BF16), preferred_element_type=F32)


def _hg_gates(hq, hf, lbv):
    sig = _sigmoid(hf)
    f = lbv + (1.0 - lbv) * sig
    return sig, f, jnp.log(f), 1.0 - f, hq * (HG_DK ** -0.5)


def _hg_sel_rows(ref, sp):
    return jnp.concatenate(
        [jnp.broadcast_to(ref[pl.ds(HG_SUB * i + sp, 1), :], (HG_SUB, HG_DK)) for i in range(HG_CHUNK // HG_SUB)], axis=0)


def _hg_masks():
    C = HG_CHUNK
    row = lax.broadcasted_iota(jnp.int32, (C, C), 0)
    col = lax.broadcasted_iota(jnp.int32, (C, C), 1)
    d = col - (row // HG_SUB) * HG_SUB
    tmod = row % HG_SUB
    diag_valid = jnp.logical_and(d >= 0, d <= tmod)
    return row, col, d, diag_valid


def _hg_scores(q, k, b, b_sc, k_sc):
    C, S = HG_CHUNK, HG_SUB
    row, col, d, diag_valid = _hg_masks()
    blocks = [jnp.zeros((S, C), F32)]
    for i in range(1, C // S):
        r = b_sc[pl.ds(S * i - 1, 1), :]
        qi = q[S * i:S * (i + 1)] * jnp.exp(b[S * i:S * (i + 1)] - r)
        kk = k * jnp.exp(jnp.minimum(r - b, 0.0))
        blocks.append(_dot_nt(qi, kk))
    a_off = jnp.where(col < (row // S) * S, jnp.concatenate(blocks, axis=0), 0.0)
    a_d = jnp.zeros((C, C), F32)
    for sp in range(S):
        bs = _hg_sel_rows(b_sc, sp)
        ks = _hg_sel_rows(k_sc, sp)
        e = jnp.exp(jnp.minimum(b - bs, 0.0))
        colv = jnp.sum(q * ks * e, axis=-1, keepdims=True)
        a_d = jnp.where(d == sp, colv, a_d)
    return a_off + jnp.where(diag_valid, a_d, 0.0)


def _hgrn_fwd(hq, hf, hi, lb, *, name):
    T = hq.shape[0]
    C, H, K = HG_CHUNK, HG_HEADS, HG_DK
    NC = T // C

    def body(hq_ref, hf_ref, hi_ref, lb_ref, o_ref, st_ref, s_sc, b_sc, k_sc):
        @pl.when(pl.program_id(1) == 0)
        def _():
            s_sc[...] = jnp.zeros_like(s_sc)

        _, _, g, k, q = _hg_gates(hq_ref[...], hf_ref[...], lb_ref[...])
        v = hi_ref[...]
        b = _cumsum_rows(g)
        b_sc[...] = b
        k_sc[...] = k
        st0 = s_sc[...]
        st_ref[0, 0] = st0
        bc = b_sc[pl.ds(C - 1, 1), :]
        a = _hg_scores(q, k, b, b_sc, k_sc)
        o_ref[...] = _dot_nn(a, v) + _dot_nt(q * jnp.exp(b), st0)
        kb = k * jnp.exp(bc - b)
        s_sc[...] = st0 * jnp.exp(bc) + _dot_tn(v, kb)

    blk = pl.BlockSpec((C, K), lambda h, c: (c, h))
    return pl.pallas_call(
        body, name=name, grid=(H, NC),
        in_specs=[blk, blk, blk, pl.BlockSpec((1, K), lambda h, c: (0, h))],
        out_specs=[blk, pl.BlockSpec((1, 1, K, K), lambda h, c: (c, h, 0, 0))],
        out_shape=[_sds((T, H * K), F32), _sds((NC, H, K, K), F32)],
        scratch_shapes=[pltpu.VMEM((K, K), F32), pltpu.VMEM((C, K), F32), pltpu.VMEM((C, K), F32)],
        compiler_params=_cp(("parallel", "arbitrary")),
    )(hq, hf, hi, lb)


def _hgrn_bwd(hq, hf, hi, lb, states, do, *, name):
    T = hq.shape[0]
    C, H, K, S = HG_CHUNK, HG_HEADS, HG_DK, HG_SUB
    NC = T // C

    def body(hq_ref, hf_ref, hi_ref, lb_ref, st_ref, do_ref, dq_ref, dhf_ref, dv_ref, dlb_ref, ds_sc, b_sc, k_sc):
        @pl.when(pl.program_id(1) == 0)
        def _():
            ds_sc[...] = jnp.zeros_like(ds_sc)
            dlb_ref[...] = jnp.zeros_like(dlb_ref)

        lbv = lb_ref[...]
        sig, f, g, k, q = _hg_gates(hq_ref[...], hf_ref[...], lbv)
        v = hi_ref[...]
        dout = do_ref[...]
        b = _cumsum_rows(g)
        b_sc[...] = b
        k_sc[...] = k
        st0 = st_ref[0, 0]
        dst1 = ds_sc[...]
        bc = b_sc[pl.ds(C - 1, 1), :]
        ebc = jnp.exp(bc)
        eb = jnp.exp(b)
        ekb = jnp.exp(bc - b)
        qt = q * eb
        kb = k * ekb
        row, col, d, diag_valid = _hg_masks()
        a = _hg_scores(q, k, b, b_sc, k_sc)
        dv = _dot_tn(a, dout) + _dot_nt(kb, dst1)
        da = jnp.where(col <= row, _dot_nt(dout, v), 0.0)
        dqt = _dot_nn(dout, st0)
        dkb = _dot_nn(v, dst1)
        ds_sc[...] = _dot_tn(dout, qt) + dst1 * ebc
        dq = dqt * eb
        dk = dkb * ekb
        dq_blocks = [jnp.zeros((S, K), F32)]
        for i in range(1, C // S):
            r = b_sc[pl.ds(S * i - 1, 1), :]
            eq = jnp.exp(b[S * i:S * (i + 1)] - r)
            ek = jnp.exp(jnp.minimum(r - b, 0.0))
            qi = q[S * i:S * (i + 1)] * eq
            kk = k * ek
            dai = jnp.where(col[S * i:S * (i + 1)] < S * i, da[S * i:S * (i + 1)], 0.0)
            dq_blocks.append(_dot_nn(dai, kk) * eq)
            dk = dk + _dot_tn(dai, qi) * ek
        dq = dq + jnp.concatenate(dq_blocks, axis=0)
        same_blk = (row // S == col // S).astype(BF16)
        tmod = (lax.broadcasted_iota(jnp.int32, (C, K), 0)) % S
        for sp in range(S):
            bs = _hg_sel_rows(b_sc, sp)
            ks = _hg_sel_rows(k_sc, sp)
            e = jnp.where(tmod >= sp, jnp.exp(jnp.minimum(b - bs, 0.0)), 0.0)
            dacol = jnp.sum(jnp.where(d == sp, da, 0.0), axis=-1, keepdims=True)
            w = dacol * e
            dq = dq + w * ks
            blk_sum = jnp.dot(same_blk, (w * q).astype(BF16), preferred_element_type=F32)
            dk = dk + jnp.where(tmod == sp, blk_sum, 0.0)
        extra = jnp.sum(dkb * kb, axis=0, keepdims=True) + ebc * jnp.sum(st0 * dst1, axis=0, keepdims=True)
        rowk = lax.broadcasted_iota(jnp.int32, (C, K), 0)
        db = q * dq - k * dk + jnp.where(rowk == C - 1, extra, 0.0)
        dg = _rcumsum_rows(db)
        df = dg / f - dk
        dq_ref[...] = (dq * (K ** -0.5)).astype(dq_ref.dtype)
        dhf_ref[...] = (df * (1.0 - lbv) * sig * (1.0 - sig)).astype(dhf_ref.dtype)
        dv_ref[...] = dv.astype(dv_ref.dtype)
        dlb_ref[...] += jnp.sum(df * (1.0 - sig), axis=0, keepdims=True)

    blk = pl.BlockSpec((C, K), lambda h, c: (NC - 1 - c, h))
    return pl.pallas_call(
        body, name=name, grid=(H, NC),
        in_specs=[blk, blk, blk, pl.BlockSpec((1, K), lambda h, c: (0, h)),
                  pl.BlockSpec((1, 1, K, K), lambda h, c: (NC - 1 - c, h, 0, 0)), blk],
        out_specs=[blk, blk, blk, pl.BlockSpec((1, K), lambda h, c: (0, h))],
        out_shape=[_sds((T, H * K), BF16)] * 3 + [_sds((1, H * K), F32)],
        scratch_shapes=[pltpu.VMEM((K, K), F32), pltpu.VMEM((C, K), F32), pltpu.VMEM((C, K), F32)],
        compiler_params=_cp(("parallel", "arbitrary")),
    )(hq, hf, hi, lb, states, do)


def _att_valid(n):
    R, B = ATT_GROUP * ATT_BLOCK, ATT_BLOCK
    t = lax.broadcasted_iota(jnp.int32, (R, 2 * B), 0) % B
    j = lax.broadcasted_iota(jnp.int32, (R, 2 * B), 1)
    dist = t + B - j
    first_key = jnp.where(n > 0, 0, B)
    return jnp.logical_and(jnp.logical_and(dist >= 0, dist < B), j >= first_key)


def _att_load(cur_ref, prev_ref, ba_ref, kv):
    hd = ATT_HD
    def cols(ref, c0):
        return ref[:, c0:c0 + hd] + ba_ref[:, c0:c0 + hd]
    qs = jnp.concatenate([cols(cur_ref, hd * (ATT_GROUP * kv + g)) for g in range(ATT_GROUP)], axis=0)
    kc = jnp.concatenate([cols(prev_ref, ATT_Q_W + hd * kv), cols(cur_ref, ATT_Q_W + hd * kv)], axis=0)
    vc = jnp.concatenate([cols(prev_ref, ATT_Q_W + ATT_KV_W + hd * kv), cols(cur_ref, ATT_Q_W + ATT_KV_W + hd * kv)], axis=0)
    return qs, kc, vc


def _att_probs(qs, kc, valid, sink_ref, kv):
    scale = 1.0 / math.sqrt(ATT_HD)
    s = jnp.where(valid, _dot_nt(qs, kc) * scale, NEG)
    sink = jnp.concatenate([jnp.full((ATT_BLOCK, 1), sink_ref[0, ATT_GROUP * kv + g], F32) for g in range(ATT_GROUP)], axis=0)
    m = jnp.maximum(jnp.max(s, axis=-1, keepdims=True), sink)
    p = jnp.exp(s - m)
    ps = jnp.exp(sink - m)
    inv = 1.0 / (jnp.sum(p, axis=-1, keepdims=True) + ps)
    return p * inv, ps * inv


def _attn_fwd(att, b_attn, sinks, *, name):
    T = att.shape[0]
    B = ATT_BLOCK
    NB = T // B

    def body(sink_ref, cur_ref, prev_ref, ba_ref, o_ref):
        valid = _att_valid(pl.program_id(0))
        for kv in range(ATT_KV):
            qs, kc, vc = _att_load(cur_ref, prev_ref, ba_ref, kv)
            prob, _ = _att_probs(qs, kc, valid, sink_ref, kv)
            o = _dot_nn(prob, vc)
            for g in range(ATT_GROUP):
                c0 = ATT_HD * (ATT_GROUP * kv + g)
                o_ref[:, c0:c0 + ATT_HD] = o[B * g:B * (g + 1)]

    return pl.pallas_call(
        body, name=name, grid=(NB,),
        in_specs=[pl.BlockSpec(memory_space=pltpu.SMEM),
                  pl.BlockSpec((B, ATT_COLS), lambda n: (n, 0)),
                  pl.BlockSpec((B, ATT_COLS), lambda n: (jnp.maximum(n - 1, 0), 0)),
                  pl.BlockSpec((1, ATT_COLS), lambda n: (0, 0))],
        out_specs=pl.BlockSpec((B, ATT_Q_W), lambda n: (n, 0)),
        out_shape=_sds((T, ATT_Q_W), F32),
        compiler_params=_cp(("parallel",)),
    )(sinks, att, att, b_attn)


def _attn_bwd(att, b_attn, sinks, dmix, *, name):
    T = att.shape[0]
    B, hd = ATT_BLOCK, ATT_HD
    NB = T // B
    scale = 1.0 / math.sqrt(hd)

    def body(sink_ref, cur_ref, prev_ref, ba_ref, do_ref, daq_ref, dakv_ref, dsink_ref, dbq_ref, dbkv_ref,
             carry_sc, cprev_sc, ccur_sc):
        n = pl.program_id(0)

        @pl.when(n == 0)
        def _():
            carry_sc[...] = jnp.zeros_like(carry_sc)
            dsink_ref[...] = jnp.zeros_like(dsink_ref)
            dbq_ref[...] = jnp.zeros_like(dbq_ref)
            dbkv_ref[...] = jnp.zeros_like(dbkv_ref)

        @pl.when(n < NB)
        def _():
            valid = _att_valid(n)
            hrow = lax.broadcasted_iota(jnp.int32, (SUBLANES, 128), 0)
            dsink = jnp.zeros((SUBLANES, 128), F32)
            for kv in range(ATT_KV):
                qs, kc, vc = _att_load(cur_ref, prev_ref, ba_ref, kv)
                prob, psink = _att_probs(qs, kc, valid, sink_ref, kv)
                dout = jnp.concatenate(
                    [do_ref[:, hd * (ATT_GROUP * kv + g):hd * (ATT_GROUP * kv + g + 1)] for g in range(ATT_GROUP)], axis=0)
                dp = _dot_nt(dout, vc)
                delta = jnp.sum(prob * dp, axis=-1, keepdims=True)
                dsc = prob * (dp - delta) * scale
                dq = _dot_nn(dsc, kc)
                dk = _dot_tn(dsc, qs)
                dvv = _dot_tn(prob, dout)
                dsk = psink * delta
                for g in range(ATT_GROUP):
                    h = ATT_GROUP * kv + g
                    daq_ref[:, hd * h:hd * (h + 1)] = dq[B * g:B * (g + 1)].astype(daq_ref.dtype)
                    tot = jnp.sum(dsk[B * g:B * (g + 1)], axis=0, keepdims=True)
                    dsink = dsink - jnp.where(hrow == h, tot, 0.0)
                cprev_sc[:, hd * kv:hd * (kv + 1)] = dk[:B]
                ccur_sc[:, hd * kv:hd * (kv + 1)] = dk[B:]
                cprev_sc[:, ATT_KV_W + hd * kv:ATT_KV_W + hd * (kv + 1)] = dvv[:B]
                ccur_sc[:, ATT_KV_W + hd * kv:ATT_KV_W + hd * (kv + 1)] = dvv[B:]
            dsink_ref[...] += dsink
            dbq_ref[...] += jnp.sum(daq_ref[...].astype(F32), axis=0, keepdims=True)
            done = carry_sc[...] + cprev_sc[...]
            dakv_ref[...] = done.astype(dakv_ref.dtype)
            dbkv_ref[...] += jnp.sum(done.astype(dakv_ref.dtype).astype(F32), axis=0, keepdims=True)
            carry_sc[...] = ccur_sc[...]

        @pl.when(n == NB)
        def _():
            done = carry_sc[...]
            dakv_ref[...] = done.astype(dakv_ref.dtype)
            dbkv_ref[...] += jnp.sum(done.astype(dakv_ref.dtype).astype(F32), axis=0, keepdims=True)

    cl = lambda n: jnp.minimum(n, NB - 1)
    return pl.pallas_call(
        body, name=name, grid=(NB + 1,),
        in_specs=[pl.BlockSpec(memory_space=pltpu.SMEM),
                  pl.BlockSpec((B, ATT_COLS), lambda n: (cl(n), 0)),
                  pl.BlockSpec((B, ATT_COLS), lambda n: (jnp.maximum(cl(n) - 1, 0), 0)),
                  pl.BlockSpec((1, ATT_COLS), lambda n: (0, 0)),
                  pl.BlockSpec((B, ATT_Q_W), lambda n: (cl(n), 1))],
        out_specs=[pl.BlockSpec((B, ATT_Q_W), lambda n: (cl(n), 0)),
                   pl.BlockSpec((B, 2 * ATT_KV_W), lambda n: (jnp.maximum(n - 1, 0), 0)),
                   pl.BlockSpec((SUBLANES, 128), lambda n: (0, 0)),
                   pl.BlockSpec((1, ATT_Q_W), lambda n: (0, 0)),
                   pl.BlockSpec((1, 2 * ATT_KV_W), lambda n: (0, 0))],
        out_shape=[_sds((T, ATT_Q_W), BF16), _sds((T, 2 * ATT_KV_W), BF16), _sds((SUBLANES, 128), F32),
                   _sds((1, ATT_Q_W), F32), _sds((1, 2 * ATT_KV_W), F32)],
        scratch_shapes=[pltpu.VMEM((B, 2 * ATT_KV_W), F32)] * 3,
        compiler_params=_cp(("arbitrary",)),
    )(sinks, att, att, b_attn, dmix)


def _silu_and_grad(x):
    sg = _sigmoid(x)
    return x * sg, sg * (1.0 + x * (1.0 - sg))


def _mix_fwd_fn(o_raw, hg, o_att, hgw):
    outs = []
    for h in range(HG_HEADS):
        sl = slice(HG_DK * h, HG_DK * (h + 1))
        silu, _ = _silu_and_grad(hg[:, sl])
        outs.append(_rms_fwd(o_raw[:, sl], hgw) * silu)
    outs.append(o_att)
    return (jnp.concatenate(outs, axis=1),)


def _mix_bwd_fn(o_raw, hg, dmix, hgw):
    dos, dhgs = [], []
    dw = jnp.zeros((1, HG_DK), F32)
    for h in range(HG_HEADS):
        sl = slice(HG_DK * h, HG_DK * (h + 1))
        silu, dsilu = _silu_and_grad(hg[:, sl])
        dy = dmix[:, sl]
        dhgs.append(dy * _rms_fwd(o_raw[:, sl], hgw) * dsilu)
        dx, dwh = _rms_bwd(o_raw[:, sl], hgw, dy * silu)
        dos.append(dx)
        dw = dw + dwh
    return jnp.concatenate(dos, axis=1), jnp.concatenate(dhgs, axis=1), dw


def _final_fn(h2, tgt, wf):
    d = h2.shape[1]
    err = _rms_fwd(h2, wf) - tgt
    loss_cols = (0.5 / d) * jnp.sum(err * err, axis=0, keepdims=True)
    dh2, dwf = _rms_bwd(h2, wf, err * (1.0 / d))
    return dh2, dh2, loss_cols, dwf


def _local_step(x, tgt, p):
    T, D = x.shape
    row = lambda n, dt: _sds((T, n), dt)
    acc = lambda n: _sds((1, n), F32)

    (u,) = _rowwise(lambda xv, w: (_rms_fwd(xv, w),), [_full(x)], [p["norm_mix_w"]], [row(D, BF16)], [], name="rms_mix")
    hq, hf, hi, hg, att = _mm_nt(u, p["w_in_t"], splits=[HG_W] * 4 + [ATT_COLS], out_dtype=F32, name="in_proj")
    o_raw, states = _hgrn_fwd(hq, hf, hi, p["lb"], name="hgrn_fwd")
    o_att = _attn_fwd(att, p["b_attn"], p["sinks"], name="attn_fwd")
    (mix,) = _rowwise(_mix_fwd_fn, [_full(o_raw), _full(hg), _full(o_att)], [p["hg_norm_w"]], [row(D, BF16)], [],
                      name="mix_fwd")
    h1 = _mm_nn([[mix]], [p["w_out"]], out_dtype=F32, name="out_proj", residual=x)
    (v,) = _rowwise(lambda hv, w: (_rms_fwd(hv, w),), [_full(h1)], [p["norm_ffn_w"]], [row(D, BF16)], [], name="rms_ffn")
    (gp,) = _mm_nt(v, p["w_gate_t"], splits=[D_FF], out_dtype=F32, name="gate_proj")
    (up,) = _mm_nt(v, p["w_up_t"], splits=[D_FF], out_dtype=F32, name="up_proj")
    act = _convact_fwd(gp, up, p["conv_w8"], p["conv_b"], name="convact_fwd")
    h2 = _mm_nn([[act]], [p["w_down"]], out_dtype=F32, name="down_proj", residual=h1)
    dh2, dh2_b, loss_cols, d_final = _rowwise(_final_fn, [_full(h2), _full(tgt)], [p["final_norm_w"]],
                                              [row(D, F32), row(D, BF16)], [acc(D), acc(D)], name="final_loss")

    (dact,) = _mm_nt(dh2_b, p["w_down"], splits=[D_FF], out_dtype=F32, name="d_act")
    g_down = _mm_tn([act], dh2_b, name="g_down")
    dgp, dup, d_conv_w8, d_conv_b = _convact_bwd(gp, up, dact, p["conv_w8"], p["conv_b"], name="convact_bwd")
    dv = _mm_nn([[dgp], [dup]], [p["w_gate_t"], p["w_up_t"]], out_dtype=F32, name="d_v")
    g_gate_t = _mm_tn([dgp], v, name="g_gate")
    g_up_t = _mm_tn([dup], v, name="g_up")

    def ffn_norm_bwd(hv, dvv, dh2v, w):
        dx, dw = _rms_bwd(hv, w, dvv)
        dh1v = dx + dh2v
        return dh1v, dh1v, dw

    dh1, dh1_b, d_norm_ffn = _rowwise(ffn_norm_bwd, [_full(h1), _full(dv), _full(dh2)], [p["norm_ffn_w"]],
                                      [row(D, F32), row(D, BF16)], [acc(D)], name="rms_ffn_bwd")
    (dmix,) = _mm_nt(dh1_b, p["w_out"], splits=[D], out_dtype=F32, name="d_mix")
    g_out = _mm_tn([mix], dh1_b, name="g_out")
    do_raw, dhg, d_hg_norm = _rowwise(_mix_bwd_fn, [_full(o_raw), _full(hg), (dmix, HG_W, 0, 0)], [p["hg_norm_w"]],
                                      [row(HG_W, F32), row(HG_W, BF16)], [acc(HG_DK)], name="mix_bwd")
    daq, dakv, d_sinks8, d_bq, d_bkv = _attn_bwd(att, p["b_attn"], p["sinks"], dmix, name="attn_bwd")
    dhq, dhf, dhi, d_lb = _hgrn_bwd(hq, hf, hi, p["lb"], states, do_raw, name="hgrn_bwd")
    pieces = [dhq, dhf, dhi, dhg, daq, dakv]
    du = _mm_nn([pieces], [p["w_in_t"]], out_dtype=F32, name="d_u")
    g_in_t = _mm_tn(pieces, u, name="g_in")

    def mix_norm_bwd(xv, duv, dh1v, w):
        dx, dw = _rms_bwd(xv, w, duv)
        return dx + dh1v, dw

    dx, d_norm_mix = _rowwise(mix_norm_bwd, [_full(x), _full(du), _full(dh1)], [p["norm_mix_w"]], [row(D, F32)], [acc(D)],
                              name="rms_mix_bwd")
    grads = dict(g_in_t=g_in_t, g_out=g_out, g_gate_t=g_gate_t, g_up_t=g_up_t, g_down=g_down,
                 norm_mix_w=d_norm_mix, b_attn=jnp.concatenate([d_bq, d_bkv], axis=1), lb=d_lb, hg_norm_w=d_hg_norm,
                 sinks8=d_sinks8, norm_ffn_w=d_norm_ffn, conv_w8=d_conv_w8, conv_b=d_conv_b, final_norm_w=d_final)
    return loss_cols, dx, grads


SLAB = (IN_COLS // N_CHIPS, D_FF // N_CHIPS, D_FF // N_CHIPS, D_FF // N_CHIPS, D_MODEL // N_CHIPS)
N_W = len(SLAB)
PACK_OFF = tuple(sum(SLAB[:i]) for i in range(N_W))
PACK_ROWS = sum(SLAB)
FULL_OFF = tuple(N_CHIPS * o for o in PACK_OFF)
FULL_ROWS = N_CHIPS * PACK_ROWS
HALF = tuple(s // 2 for s in SLAB)
HPACK_OFF = tuple(sum(HALF[:i]) for i in range(N_W))
HPACK_ROWS = sum(HALF)
HFULL_OFF = tuple(N_CHIPS * o for o in HPACK_OFF)
HFULL_ROWS = N_CHIPS * HPACK_ROWS
CHIP_FLIPS = ((1, 0), (0, 1), (1, 1))
N_DEV = 8
BF16_ROWS = 16
ANY = pl.BlockSpec(memory_space=pl.ANY)


def _pos():
    return lax.axis_index("x"), lax.axis_index("y"), lax.axis_index("c")


def _flip(v, f):
    return 1 - v if f else v


def _rcopy(src, dst, ssem, rsem, dev):
    return pltpu.make_async_remote_copy(src_ref=src, dst_ref=dst, send_sem=ssem, recv_sem=rsem, device_id=dev,
                                        device_id_type=pl.DeviceIdType.MESH)


def _rows(ref, start, n, align=SUBLANES):
    if not isinstance(start, int):
        start = pl.multiple_of(start, align)
    return ref.at[pl.ds(start, n), :]


def _gather_weights(pack, cw8):
    D = pack.shape[1]

    def body(pack_ref, cw_ref, full_ref, cwall_ref, send_sems, recv_sems, loc_sems):
        x, y, c = _pos()
        q = 2 * x + y

        def dst(w):
            return _rows(full_ref, FULL_OFF[w] + q * SLAB[w], SLAB[w], BF16_ROWS)

        local = [pltpu.make_async_copy(_rows(pack_ref, PACK_OFF[w], SLAB[w]), dst(w), loc_sems.at[w]) for w in range(N_W)]
        local.append(pltpu.make_async_copy(cw_ref, cwall_ref.at[q], loc_sems.at[N_W]))
        for cp in local:
            cp.start()
        drains = []
        for k, (fx, fy) in enumerate(CHIP_FLIPS):
            peer = (_flip(x, fx), _flip(y, fy), c)
            for w in range(N_W):
                _rcopy(_rows(pack_ref, PACK_OFF[w], SLAB[w]), dst(w), send_sems.at[k], recv_sems.at[k], peer).start()
            cwc = _rcopy(cw_ref, cwall_ref.at[q], send_sems.at[3 + k], recv_sems.at[3 + k], peer)
            cwc.start()
            drains.append(_rcopy(pack_ref, _rows(full_ref, 0, PACK_ROWS), send_sems.at[k], recv_sems.at[k], peer))
            drains.append(cwc)
        for d in drains:
            d.wait_recv()
        for d in drains:
            d.wait_send()
        for cp in local:
            cp.wait()

    return pl.pallas_call(
        body, name="gather_weights", in_specs=[ANY, ANY], out_specs=[ANY, ANY],
        out_shape=[_sds((FULL_ROWS, D), pack.dtype), _sds((N_CHIPS,) + cw8.shape, cw8.dtype)],
        scratch_shapes=[pltpu.SemaphoreType.DMA((6,)), pltpu.SemaphoreType.DMA((6,)), pltpu.SemaphoreType.DMA((N_W + 1,))],
    )(pack, cw8)


def _exchange_halves(gs, small):
    D = gs[0].shape[1]

    def body(g0, g1, g2, g3, g4, small_ref, mine_ref, theirs_ref, sall_ref, d2d_send, d2d_recv, sm_send, sm_recv, loc_sems):
        g = (g0, g1, g2, g3, g4)
        x, y, c = _pos()
        sib = (x, y, 1 - c)
        me = 4 * x + 2 * y + c
        own_small = pltpu.make_async_copy(small_ref, sall_ref.at[me], loc_sems.at[1])
        own_small.start()
        for w in range(N_W):
            h = HALF[w]
            for qq in range(N_CHIPS):
                dst = pl.ds(HFULL_OFF[w] + qq * h, h)
                pltpu.make_async_copy(_rows(g[w], qq * SLAB[w] + c * h, h), mine_ref.at[dst, :], loc_sems.at[0]).start()
                _rcopy(_rows(g[w], qq * SLAB[w] + (1 - c) * h, h), theirs_ref.at[dst, :], d2d_send, d2d_recv, sib).start()
        smalls = []
        for f in range(1, N_DEV):
            peer = (_flip(x, f & 4), _flip(y, f & 2), _flip(c, f & 1))
            cp = _rcopy(small_ref, sall_ref.at[me], sm_send.at[f - 1], sm_recv.at[f - 1], peer)
            cp.start()
            smalls.append(cp)
        whole = _rcopy(mine_ref, theirs_ref, d2d_send, d2d_recv, sib)
        whole.wait_recv()
        for cp in smalls:
            cp.wait_recv()
        whole.wait_send()
        for cp in smalls:
            cp.wait_send()
        pltpu.make_async_copy(theirs_ref, mine_ref, loc_sems.at[0]).wait()
        own_small.wait()

    return pl.pallas_call(
        body, name="exchange_halves", in_specs=[ANY] * 6, out_specs=[ANY] * 3,
        out_shape=[_sds((HFULL_ROWS, D), F32), _sds((HFULL_ROWS, D), F32), _sds((N_DEV,) + small.shape, F32)],
        scratch_shapes=[pltpu.SemaphoreType.DMA, pltpu.SemaphoreType.DMA, pltpu.SemaphoreType.DMA((N_DEV - 1,)),
                        pltpu.SemaphoreType.DMA((N_DEV - 1,)), pltpu.SemaphoreType.DMA((2,))],
    )(*gs, small)


def _send_chip_partials(part):
    D = part.shape[1]

    def body(part_ref, own_ref, got_ref, send_sems, recv_sems, loc_sem):
        x, y, c = _pos()
        q = 2 * x + y
        for w in range(N_W):
            pltpu.make_async_copy(_rows(part_ref, HFULL_OFF[w] + q * HALF[w], HALF[w]),
                                  _rows(own_ref, HPACK_OFF[w], HALF[w]), loc_sem).start()
        drains = []
        for k, (fx, fy) in enumerate(CHIP_FLIPS):
            peer = (_flip(x, fx), _flip(y, fy), c)
            qp = 2 * _flip(x, fx) + _flip(y, fy)
            for w in range(N_W):
                _rcopy(_rows(part_ref, HFULL_OFF[w] + qp * HALF[w], HALF[w]),
                       _rows(got_ref, k * HPACK_ROWS + HPACK_OFF[w], HALF[w]), send_sems.at[k], recv_sems.at[k], peer).start()
            drains.append(_rcopy(own_ref, _rows(got_ref, k * HPACK_ROWS, HPACK_ROWS), send_sems.at[k], recv_sems.at[k], peer))
        for d in drains:
            d.wait_recv()
        for d in drains:
            d.wait_send()
        pltpu.make_async_copy(own_ref, own_ref, loc_sem).wait()

    return pl.pallas_call(
        body, name="send_chip_partials", in_specs=[ANY], out_specs=[ANY, ANY],
        out_shape=[_sds((HPACK_ROWS, D), F32), _sds((3 * HPACK_ROWS, D), F32)],
        scratch_shapes=[pltpu.SemaphoreType.DMA((3,)), pltpu.SemaphoreType.DMA((3,)), pltpu.SemaphoreType.DMA],
    )(part)


def _exchange_reduced(red):
    D = red.shape[1]

    def body(red_ref, out_ref, send_sem, recv_sem, loc_sem):
        x, y, c = _pos()
        sib = (x, y, 1 - c)
        for w in range(N_W):
            src = _rows(red_ref, HPACK_OFF[w], HALF[w])
            dst = _rows(out_ref, PACK_OFF[w] + c * HALF[w], HALF[w])
            pltpu.make_async_copy(src, dst, loc_sem).start()
            _rcopy(src, dst, send_sem, recv_sem, sib).start()
        whole = _rcopy(red_ref, _rows(out_ref, 0, HPACK_ROWS), send_sem, recv_sem, sib)
        whole.wait_recv()
        whole.wait_send()
        pltpu.make_async_copy(red_ref, red_ref, loc_sem).wait()

    return pl.pallas_call(
        body, name="exchange_reduced", in_specs=[ANY], out_specs=ANY,
        out_shape=_sds((PACK_ROWS, D), F32),
        scratch_shapes=[pltpu.SemaphoreType.DMA, pltpu.SemaphoreType.DMA, pltpu.SemaphoreType.DMA],
    )(red)


def _adamw_fn(w, g, m, v):
    m2 = ADAM_B1 * m + (1.0 - ADAM_B1) * g
    v2 = ADAM_B2 * v + (1.0 - ADAM_B2) * (g * g)
    m_hat = m2 / (1.0 - ADAM_B1 ** ADAM_STEP)
    v_hat = v2 / (1.0 - ADAM_B2 ** ADAM_STEP)
    return -ADAM_LR * (m_hat / (jnp.sqrt(v_hat) + ADAM_EPS) + ADAM_WD * w), m2, v2


def _adamw(w, g, m, v, *, name):
    shp = _sds(w.shape, F32)
    rows = w.shape[0]
    tm = max(t for t in range(SUBLANES, 512 + 1, SUBLANES) if rows % t == 0)
    return _rowwise(_adamw_fn, [_full(w), _full(g), _full(m), _full(v)], [], [shp] * 3, [], name=name, tm=tm)


SMALL_SEGS = (("loss", 8), ("norm_mix_w", 8), ("b_attn", 8), ("lb_logits", 8), ("hg_norm_w", 8), ("sinks", 8),
              ("norm_ffn_w", 8), ("conv_w", 72), ("conv_b", 24), ("final_norm_w", 8))
SMALL_OFF = {n: sum(r for _, r in SMALL_SEGS[:i]) for i, (n, _) in enumerate(SMALL_SEGS)}
SMALL_ROWS = sum(r for _, r in SMALL_SEGS)
LANES = 128


def _pack_small(parts):
    segs = []
    for n, r in SMALL_SEGS:
        a = parts.get(n)
        flat = jnp.zeros((0,), F32) if a is None else a.reshape(-1).astype(F32)
        segs.append(jnp.pad(flat, (0, r * LANES - flat.shape[0])).reshape(r, LANES))
    return jnp.concatenate(segs, axis=0)


def _unpack_small(pack, n, shape):
    size = math.prod(shape)
    r0 = SMALL_OFF[n]
    return pack[r0:r0 + dict(SMALL_SEGS)[n]].reshape(-1)[:size].reshape(shape)


def _small_update(sall, wp, mp, vp):
    R = SMALL_ROWS
    r_lb = SMALL_OFF["lb_logits"]

    def body(s_ref, w_ref, m_ref, v_ref, g_ref, d_ref, m2_ref, v2_ref, loss_ref):
        g = s_ref[0]
        for i in range(1, N_DEV):
            g = g + s_ref[i]
        tot = jnp.sum(jnp.sum(g[0:8], axis=1, keepdims=True), axis=0, keepdims=True)
        loss_ref[...] = jnp.broadcast_to(tot, loss_ref.shape)
        lg = w_ref[r_lb:r_lb + 8, :]
        p0 = _sigmoid(lg - pltpu.roll(lg, 4, 0))
        d = g[r_lb:r_lb + 8]
        d = d + pltpu.roll(d, 4, 0)
        sign = jnp.where(lax.broadcasted_iota(jnp.int32, d.shape, 0) < 4, 1.0, -1.0)
        g = jnp.concatenate([g[:r_lb], sign * d * p0 * (1.0 - p0), g[r_lb + 8:]], axis=0)
        g_ref[...] = g
        d_ref[...], m2_ref[...], v2_ref[...] = _adamw_fn(w_ref[...], g, m_ref[...], v_ref[...])

    full = pl.BlockSpec((R, LANES), lambda: (0, 0))
    return pl.pallas_call(
        body, name="small_update",
        in_specs=[pl.BlockSpec((N_DEV, R, LANES), lambda: (0, 0, 0)), full, full, full],
        out_specs=[full, full, full, full, pl.BlockSpec((8, LANES), lambda: (0, 0))],
        out_shape=[_sds((R, LANES), F32)] * 4 + [_sds((8, LANES), F32)],
        compiler_params=_cp(),
    )(sall, wp, mp, vp)


def _lb_fwd(lb_logits):
    n = lb_logits.shape[1]

    def body(l_ref, o_ref):
        o_ref[...] = _sigmoid(l_ref[0:1, :] - l_ref[1:2, :])

    return pl.pallas_call(body, name="lb_fwd", out_shape=_sds((1, n), F32), compiler_params=_cp())(lb_logits)


def kernel(x, norm_mix_w, w_in, b_attn, lb_logits, hg_norm_w, sinks, w_out, norm_ffn_w, w_gate, w_up, conv_w, conv_b, w_down, final_norm_w, loss_target, m_norm_mix_w, m_w_in, m_b_attn, m_lb_logits, m_hg_norm_w, m_sinks, m_w_out, m_norm_ffn_w, m_w_gate, m_w_up, m_conv_w, m_conv_b, m_w_down, m_final_norm_w, v_norm_mix_w, v_w_in, v_b_attn, v_lb_logits, v_hg_norm_w, v_sinks, v_w_out, v_norm_ffn_w, v_w_gate, v_w_up, v_conv_w, v_conv_b, v_w_down, v_final_norm_w):
    D = D_MODEL
    q = 2 * lax.axis_index("x") + lax.axis_index("y")
    ccols = D_FF // N_CHIPS

    pack = jnp.concatenate([w_in[0].T, w_gate[0].T, w_up[0].T, w_down[0], w_out[0]], axis=0).astype(BF16)
    cw8 = jnp.concatenate([conv_w[0], jnp.zeros((SUBLANES - 3, ccols), F32)], axis=0)
    wfull, cw_all = _gather_weights(pack, cw8)
    conv_w8 = jnp.concatenate([cw_all[i] for i in range(N_CHIPS)], axis=1)
    nblk = FULL_OFF[1] // (N_CHIPS * SLAB[0])
    p = dict(
        norm_mix_w=norm_mix_w, b_attn=b_attn, lb=_lb_fwd(lb_logits), hg_norm_w=hg_norm_w, sinks=sinks,
        norm_ffn_w=norm_ffn_w, conv_w8=conv_w8, conv_b=conv_b, final_norm_w=final_norm_w.reshape(1, D),
        w_in_t=(wfull, N_CHIPS * SLAB[0], 0), w_gate_t=(wfull, N_CHIPS * SLAB[1], 1), w_up_t=(wfull, N_CHIPS * SLAB[2], 2),
        w_down=(wfull, N_CHIPS * SLAB[3], 3), w_out=(wfull, N_CHIPS * SLAB[4], FULL_OFF[4] // (N_CHIPS * SLAB[4])),
    )
    assert nblk == 1

    loss_cols, dx, g = _local_step(x[0], loss_target[0], p)

    small = _pack_small(dict(loss=loss_cols, norm_mix_w=g["norm_mix_w"], b_attn=g["b_attn"], lb_logits=g["lb"],
                             hg_norm_w=g["hg_norm_w"], sinks=g["sinks8"], norm_ffn_w=g["norm_ffn_w"],
                             conv_w=g["conv_w8"][:3], conv_b=g["conv_b"], final_norm_w=g["final_norm_w"]))
    mine, theirs, sall = _exchange_halves([g["g_in_t"], g["g_gate_t"], g["g_up_t"], g["g_down"], g["g_out"]], small)
    (part,) = _rowwise(lambda a, b: (a + b,), [_full(mine), _full(theirs)], [], [_sds(mine.shape, F32)], [],
                       name="chip_partial")
    own, got = _send_chip_partials(part)
    nb = HPACK_ROWS // 256
    (red,) = _rowwise(lambda a, b, c, d: (a + b + c + d,),
                      [_full(own), _full(got, 0), _full(got, nb), _full(got, 2 * nb)], [], [_sds(own.shape, F32)], [],
                      name="chip_reduce")
    shard = _exchange_reduced(red)
    g_in = shard[PACK_OFF[0]:PACK_OFF[0] + SLAB[0]].T
    g_gate = shard[PACK_OFF[1]:PACK_OFF[1] + SLAB[1]].T
    g_up = shard[PACK_OFF[2]:PACK_OFF[2] + SLAB[2]].T
    g_down = shard[PACK_OFF[3]:PACK_OFF[3] + SLAB[3]]
    g_out = shard[PACK_OFF[4]:PACK_OFF[4] + SLAB[4]]
    big = {}
    for n, gw, w, m, v in (("w_in", g_in, w_in, m_w_in, v_w_in), ("w_out", g_out, w_out, m_w_out, v_w_out),
                           ("w_gate", g_gate, w_gate, m_w_gate, v_w_gate), ("w_up", g_up, w_up, m_w_up, v_w_up),
                           ("w_down", g_down, w_down, m_w_down, v_w_down)):
        d_, m_, v_ = _adamw(w[0], gw, m[0], v[0], name="adamw_" + n)
        big[n] = (gw[None], d_[None], m_[None], v_[None])

    def place(a):
        return lax.dynamic_update_slice(jnp.zeros((3, D_FF), F32), a[0], (0, q * ccols))

    def small_pack(ws, cw):
        nm, ba, lbl, hg, sk, nf, cb, fn = ws
        return _pack_small(dict(norm_mix_w=nm, b_attn=ba, lb_logits=lbl, hg_norm_w=hg,
                                sinks=jnp.broadcast_to(sk.reshape(ATT_HEADS, 1), (ATT_HEADS, LANES)), norm_ffn_w=nf,
                                conv_w=cw, conv_b=cb, final_norm_w=fn))

    wp = small_pack((norm_mix_w, b_attn, lb_logits, hg_norm_w, sinks, norm_ffn_w, conv_b, final_norm_w), conv_w8[:3])
    mp = small_pack((m_norm_mix_w, m_b_attn, m_lb_logits, m_hg_norm_w, m_sinks, m_norm_ffn_w, m_conv_b, m_final_norm_w),
                    place(m_conv_w))
    vp = small_pack((v_norm_mix_w, v_b_attn, v_lb_logits, v_hg_norm_w, v_sinks, v_norm_ffn_w, v_conv_b, v_final_norm_w),
                    place(v_conv_w))
    outs = _small_update(sall, wp, mp, vp)
    loss = outs[4][0, 0]

    def small_out(pk, n, ref):
        if n == "sinks":
            return pk[SMALL_OFF[n]:SMALL_OFF[n] + ATT_HEADS, 0].reshape(ref.shape)
        if n == "conv_w":
            full = _unpack_small(pk, n, (3, D_FF))
            return lax.dynamic_slice(full, (0, q * ccols), (3, ccols))[None]
        return _unpack_small(pk, n, ref.shape)

    refs = dict(norm_mix_w=norm_mix_w, b_attn=b_attn, lb_logits=lb_logits, hg_norm_w=hg_norm_w, sinks=sinks,
                norm_ffn_w=norm_ffn_w, conv_w=conv_w, conv_b=conv_b, final_norm_w=final_norm_w)
    order = ("norm_mix_w", "w_in", "b_attn", "lb_logits", "hg_norm_w", "sinks", "w_out", "norm_ffn_w", "w_gate", "w_up",
             "conv_w", "conv_b", "w_down", "final_norm_w")
    res = [loss, dx[None]]
    for k in range(4):
        for n in order:
            res.append(big[n][k] if n in big else small_out(outs[k], n, refs[n]))
    return tuple(res)
```

```python
import functools
import math

import jax
import jax.numpy as jnp
from jax import lax
from jax.experimental import pallas as pl
from jax.experimental.pallas import tpu as pltpu

F32 = jnp.float32
BF16 = jnp.bfloat16

D_MODEL = 1024
HG_HEADS = 4
HG_DK = 128
HG_W = HG_HEADS * HG_DK
HG_CHUNK = 64
HG_SUB = 16
ATT_HEADS = 8
ATT_KV = 2
ATT_GROUP = ATT_HEADS // ATT_KV
ATT_HD = 64
ATT_BLOCK = 128
ATT_Q_W = ATT_HEADS * ATT_HD
ATT_KV_W = ATT_KV * ATT_HD
ATT_COLS = ATT_Q_W + 2 * ATT_KV_W
IN_COLS = 4 * HG_W + ATT_COLS
D_FF = 2816
EPS = 1e-6
ADAM_LR, ADAM_B1, ADAM_B2, ADAM_EPS, ADAM_WD, ADAM_STEP = 0.001, 0.9, 0.999, 1e-08, 0.01, 10
NEG = -1e30

V7X_VMEM_BYTES = 64 * 1024 * 1024
VMEM_LIMIT = 48 * 1024 * 1024
SUBLANES = 8

N_CHIPS = 4


def _cp(sem=None, **kw):
    return pltpu.CompilerParams(dimension_semantics=sem, vmem_limit_bytes=VMEM_LIMIT, **kw)


def _sds(shape, dtype):
    return jax.ShapeDtypeStruct(shape, dtype)


def _wspec(w):
    arr, rows, blk = w
    return pl.BlockSpec((rows, arr.shape[1]), lambda i: (blk, 0))


def _mm_nt(a, w, *, splits, out_dtype, name, residual=None, tm=512):
    M, K = a.shape
    N = w[1]
    tm = min(tm, M)
    assert sum(splits) == N and M % tm == 0
    offs = [sum(splits[:i]) for i in range(len(splits))]

    def body(*refs):
        a_ref, w_ref = refs[0], refs[1]
        outs = refs[2 + (residual is not None):]
        acc = lax.dot_general(a_ref[...], w_ref[...], (((1,), (1,)), ((), ())), preferred_element_type=F32)
        if residual is not None:
            acc = acc + refs[2][...]
        for o_ref, c0, n in zip(outs, offs, splits):
            o_ref[...] = acc[:, c0:c0 + n].astype(out_dtype)

    in_specs = [pl.BlockSpec((tm, K), lambda i: (i, 0)), _wspec(w)]
    args = [a, w[0]]
    if residual is not None:
        assert len(splits) == 1
        in_specs.append(pl.BlockSpec((tm, N), lambda i: (i, 0)))
        args.append(residual)
    outs = pl.pallas_call(
        body, name=name, grid=(M // tm,), in_specs=in_specs,
        out_specs=[pl.BlockSpec((tm, n), lambda i: (i, 0)) for n in splits],
        out_shape=[_sds((M, n), out_dtype) for n in splits],
        compiler_params=_cp(("parallel",)),
    )(*args)
    return outs


def _mm_nn(pieces, ws, *, out_dtype, name, residual=None, tm=512):
    M = pieces[0][0].shape[0]
    K = ws[0][0].shape[1]
    tm = min(tm, M)
    flat = [p for grp in pieces for p in grp]
    n_p = len(flat)

    def body(*refs):
        p_refs = refs[:n_p]
        w_refs = refs[n_p:n_p + len(ws)]
        o_ref = refs[-1]
        acc = None if residual is None else refs[n_p + len(ws)][...]
        k = 0
        for gi, grp in enumerate(pieces):
            c0 = 0
            for p in grp:
                n = p.shape[1]
                t = jnp.dot(p_refs[k][...], w_refs[gi][c0:c0 + n, :], preferred_element_type=F32)
                acc = t if acc is None else acc + t
                c0 += n
                k += 1
        o_ref[...] = acc.astype(out_dtype)

    in_specs = [pl.BlockSpec((tm, p.shape[1]), lambda i: (i, 0)) for p in flat]
    in_specs += [_wspec(w) for w in ws]
    args = [*flat, *[w[0] for w in ws]]
    if residual is not None:
        in_specs.append(pl.BlockSpec((tm, K), lambda i: (i, 0)))
        args.append(residual)
    return pl.pallas_call(
        body, name=name, grid=(M // tm,), in_specs=in_specs,
        out_specs=pl.BlockSpec((tm, K), lambda i: (i, 0)),
        out_shape=_sds((M, K), out_dtype),
        compiler_params=_cp(("parallel",)),
    )(*args)


def _mm_tn(pieces, x, *, name, tt=512):
    M, K = x.shape
    tt = min(tt, M)
    ns = [p.shape[1] for p in pieces]
    offs = [sum(ns[:i]) for i in range(len(ns))]
    N = sum(ns)
    n_p = len(pieces)

    def body(*refs):
        p_refs = refs[:n_p]
        x_ref = refs[n_p]
        o_ref = refs[n_p + 1]

        @pl.when(pl.program_id(0) == 0)
        def _():
            o_ref[...] = jnp.zeros_like(o_ref)

        xv = x_ref[...]
        for p_ref, c0, n in zip(p_refs, offs, ns):
            o_ref[c0:c0 + n, :] += lax.dot_general(p_ref[...], xv, (((0,), (0,)), ((), ())),
                                                    preferred_element_type=F32)

    in_specs = [pl.BlockSpec((tt, n), lambda i: (i, 0)) for n in ns]
    in_specs.append(pl.BlockSpec((tt, K), lambda i: (i, 0)))
    return pl.pallas_call(
        body, name=name, grid=(M // tt,), in_specs=in_specs,
        out_specs=pl.BlockSpec((N, K), lambda i: (0, 0)),
        out_shape=_sds((N, K), F32),
        compiler_params=_cp(("arbitrary",)),
    )(*pieces, x)


def _rms_fwd(xf, w):
    inv = lax.rsqrt(jnp.mean(xf * xf, axis=-1, keepdims=True) + EPS)
    return xf * inv * w


def _rms_bwd(xf, w, dy):
    inv = lax.rsqrt(jnp.mean(xf * xf, axis=-1, keepdims=True) + EPS)
    xhat = xf * inv
    dxhat = dy * w
    dx = inv * (dxhat - xhat * jnp.mean(dxhat * xhat, axis=-1, keepdims=True))
    dw = jnp.sum(dy * xhat, axis=0, keepdims=True)
    return dx, dw


def _sigmoid(x):
    return 1.0 / (1.0 + jnp.exp(-x))


def _rowwise(fn, row_ins, bc_ins, row_outs, acc_outs, *, name, tm=256):
    M = row_outs[0].shape[0] if row_outs else row_ins[0][0].shape[0]
    assert M % tm == 0 and tm % SUBLANES == 0, (name, M, tm)
    n_r, n_b, n_o, n_a = len(row_ins), len(bc_ins), len(row_outs), len(acc_outs)

    def body(*refs):
        ins = [r[...] for r in refs[:n_r + n_b]]
        o_refs = refs[n_r + n_b:n_r + n_b + n_o]
        a_refs = refs[n_r + n_b + n_o:]
        res = fn(*ins)
        for o_ref, val in zip(o_refs, res[:n_o]):
            o_ref[...] = val.astype(o_ref.dtype)
        if n_a:
            @pl.when(pl.program_id(0) == 0)
            def _():
                for a_ref in a_refs:
                    a_ref[...] = jnp.zeros_like(a_ref)
            for a_ref, val in zip(a_refs, res[n_o:]):
                a_ref[...] += val

    in_specs = [pl.BlockSpec((tm, cw), functools.partial(lambda i, cb, r0: (i + r0, cb), cb=cb, r0=r0))
                for (_, cw, cb, r0) in row_ins]
    in_specs += [pl.BlockSpec(b.shape, lambda i: (0, 0)) for b in bc_ins]
    out_specs = [pl.BlockSpec((tm, s.shape[1]), lambda i: (i, 0)) for s in row_outs]
    out_specs += [pl.BlockSpec(s.shape, lambda i: (0, 0)) for s in acc_outs]
    return pl.pallas_call(
        body, name=name, grid=(M // tm,), in_specs=in_specs, out_specs=out_specs,
        out_shape=list(row_outs) + list(acc_outs),
        compiler_params=_cp(("arbitrary",) if n_a else ("parallel",)),
    )(*[r[0] for r in row_ins], *bc_ins)


def _full(a, first_row_block=0):
    return (a, a.shape[1], 0, first_row_block)


def _conv_rows(ext, w_ref_val, lo):
    s1 = pltpu.roll(ext, 1, 0)
    s2 = pltpu.roll(ext, 2, 0)
    y = w_ref_val[0:1, :] * s2 + w_ref_val[1:2, :] * s1 + w_ref_val[2:3, :] * ext
    return y[SUBLANES:, :]


def _convact_fwd(gp, up, conv_w8, conv_b, *, name, tr=512, tc=256):
    T, C = gp.shape
    tr = min(tr, T)
    hb = tr // SUBLANES

    def body(gp_ref, gph_ref, up_ref, w_ref, b_ref, act_ref):
        i = pl.program_id(1)
        halo = jnp.where(i > 0, gph_ref[...], 0.0)
        ext = jnp.concatenate([halo, gp_ref[...]], axis=0)
        gate = _conv_rows(ext, w_ref[...], 0) + b_ref[...]
        act_ref[...] = (gate * _sigmoid(gate) * up_ref[...]).astype(act_ref.dtype)

    return pl.pallas_call(
        body, name=name, grid=(C // tc, T // tr),
        in_specs=[pl.BlockSpec((tr, tc), lambda j, i: (i, j)),
                  pl.BlockSpec((SUBLANES, tc), lambda j, i: (jnp.maximum(i * hb - 1, 0), j)),
                  pl.BlockSpec((tr, tc), lambda j, i: (i, j)),
                  pl.BlockSpec((SUBLANES, tc), lambda j, i: (0, j)),
                  pl.BlockSpec((1, tc), lambda j, i: (0, j))],
        out_specs=pl.BlockSpec((tr, tc), lambda j, i: (i, j)),
        out_shape=_sds((T, C), BF16),
        compiler_params=_cp(("parallel", "parallel")),
    )(gp, gp, up, conv_w8, conv_b)


def _convact_bwd(gp, up, dact, conv_w8, conv_b, *, name, tr=512, tc=256):
    T, C = gp.shape
    tr = min(tr, T)
    hb = tr // SUBLANES
    nr = T // tr

    def body(gp_ref, gpp_ref, gpn_ref, up_ref, upn_ref, da_ref, dan_ref, w_ref, b_ref,
             dgp_ref, dup_ref, dw_ref, db_ref):
        i = pl.program_id(1)
        w = w_ref[...]
        prev = jnp.where(i > 0, gpp_ref[...], 0.0)
        last = i == nr - 1
        gp_ext = jnp.concatenate([prev, gp_ref[...], gpn_ref[...]], axis=0)
        gate = _conv_rows(gp_ext, w, 0) + b_ref[...]
        up_e = jnp.concatenate([up_ref[...], upn_ref[...]], axis=0)
        da_e = jnp.concatenate([da_ref[...], dan_ref[...]], axis=0)
        row = lax.broadcasted_iota(jnp.int32, gate.shape, 0)
        valid = jnp.logical_or(row < tr, jnp.logical_not(last))
        sg = _sigmoid(gate)
        silu = gate * sg
        dgate = jnp.where(valid, da_e * up_e * (sg * (1.0 + gate * (1.0 - sg))), 0.0)
        dup_ref[...] = (da_e[:tr] * silu[:tr]).astype(dup_ref.dtype)
        n = tr + SUBLANES
        g1 = pltpu.roll(dgate, n - 1, 0)
        g2 = pltpu.roll(dgate, n - 2, 0)
        dgp = w[2:3, :] * dgate + w[1:2, :] * g1 + w[0:1, :] * g2
        dgp_ref[...] = dgp[:tr].astype(dgp_ref.dtype)
        gpc = gp_ref[...]
        dw0 = jnp.sum(gpc * g2[:tr], axis=0, keepdims=True)
        dw1 = jnp.sum(gpc * g1[:tr], axis=0, keepdims=True)
        dw2 = jnp.sum(gpc * dgate[:tr], axis=0, keepdims=True)
        dbv = jnp.sum(dgate[:tr], axis=0, keepdims=True)
        z = jnp.zeros((SUBLANES - 3, gpc.shape[1]), F32)

        @pl.when(i == 0)
        def _():
            dw_ref[...] = jnp.zeros_like(dw_ref)
            db_ref[...] = jnp.zeros_like(db_ref)

        dw_ref[...] += jnp.concatenate([dw0, dw1, dw2, z], axis=0)
        db_ref[...] += dbv

    cur = pl.BlockSpec((tr, tc), lambda j, i: (i, j))
    prv = pl.BlockSpec((SUBLANES, tc), lambda j, i: (jnp.maximum(i * hb - 1, 0), j))
    nxt = pl.BlockSpec((SUBLANES, tc), lambda j, i: (jnp.minimum((i + 1) * hb, T // SUBLANES - 1), j))
    return pl.pallas_call(
        body, name=name, grid=(C // tc, nr),
        in_specs=[cur, prv, nxt, cur, nxt, cur, nxt,
                  pl.BlockSpec((SUBLANES, tc), lambda j, i: (0, j)),
                  pl.BlockSpec((1, tc), lambda j, i: (0, j))],
        out_specs=[cur, cur,
                   pl.BlockSpec((SUBLANES, tc), lambda j, i: (0, j)),
                   pl.BlockSpec((1, tc), lambda j, i: (0, j))],
        out_shape=[_sds((T, C), BF16), _sds((T, C), BF16), _sds((SUBLANES, C), F32), _sds((1, C), F32)],
        compiler_params=_cp(("parallel", "arbitrary")),
    )(gp, gp, gp, up, up, dact, dact, conv_w8, conv_b)


def _cumsum_rows(x):
    n = x.shape[0]
    row = lax.broadcasted_iota(jnp.int32, x.shape, 0)
    s = 1
    while s < n:
        x = x + jnp.where(row >= s, pltpu.roll(x, s, 0), 0.0)
        s *= 2
    return x


def _rcumsum_rows(x):
    n = x.shape[0]
    row = lax.broadcasted_iota(jnp.int32, x.shape, 0)
    s = 1
    while s < n:
        x = x + jnp.where(row < n - s, pltpu.roll(x, n - s, 0), 0.0)
        s *= 2
    return x


def _dot_nt(a, b):
    return lax.dot_general(a.astype(BF16), b.astype(BF16), (((1,), (1,)), ((), ())), preferred_element_type=F32)


def _dot_tn(a, b):
    return lax.dot_general(a.astype(BF16), b.astype(BF16), (((0,), (0,)), ((), ())), preferred_element_type=F32)


def _dot_nn(a, b):
    return jnp.dot(a.astype(BF16), b.astype(BF16), preferred_element_type=F32)


def _hg_gates(hq, hf, lbv):
    sig = _sigmoid(hf)
    f = lbv + (1.0 - lbv) * sig
    return sig, f, jnp.log(f), 1.0 - f, hq * (HG_DK ** -0.5)


def _hg_sel_rows(ref, sp):
    return jnp.concatenate(
        [jnp.broadcast_to(ref[pl.ds(HG_SUB * i + sp, 1), :], (HG_SUB, HG_DK)) for i in range(HG_CHUNK // HG_SUB)], axis=0)


def _hg_masks():
    C = HG_CHUNK
    row = lax.broadcasted_iota(jnp.int32, (C, C), 0)
    col = lax.broadcasted_iota(jnp.int32, (C, C), 1)
    d = col - (row // HG_SUB) * HG_SUB
    tmod = row % HG_SUB
    diag_valid = jnp.logical_and(d >= 0, d <= tmod)
    return row, col, d, diag_valid


def _hg_scores(q, k, b, b_sc, k_sc):
    C, S = HG_CHUNK, HG_SUB
    row, col, d, diag_valid = _hg_masks()
    blocks = [jnp.zeros((S, C), F32)]
    for i in range(1, C // S):
        r = b_sc[pl.ds(S * i - 1, 1), :]
        qi = q[S * i:S * (i + 1)] * jnp.exp(b[S * i:S * (i + 1)] - r)
        kk = k * jnp.exp(jnp.minimum(r - b, 0.0))
        blocks.append(_dot_nt(qi, kk))
    a_off = jnp.where(col < (row // S) * S, jnp.concatenate(blocks, axis=0), 0.0)
    a_d = jnp.zeros((C, C), F32)
    for sp in range(S):
        bs = _hg_sel_rows(b_sc, sp)
        ks = _hg_sel_rows(k_sc, sp)
        e = jnp.exp(jnp.minimum(b - bs, 0.0))
        colv = jnp.sum(q * ks * e, axis=-1, keepdims=True)
        a_d = jnp.where(d == sp, colv, a_d)
    return a_off + jnp.where(diag_valid, a_d, 0.0)


def _hgrn_fwd(hq, hf, hi, lb, *, name):
    T = hq.shape[0]
    C, H, K = HG_CHUNK, HG_HEADS, HG_DK
    NC = T // C

    def body(hq_ref, hf_ref, hi_ref, lb_ref, o_ref, st_ref, s_sc, b_sc, k_sc):
        @pl.when(pl.program_id(1) == 0)
        def _():
            s_sc[...] = jnp.zeros_like(s_sc)

        _, _, g, k, q = _hg_gates(hq_ref[...], hf_ref[...], lb_ref[...])
        v = hi_ref[...]
        b = _cumsum_rows(g)
        b_sc[...] = b
        k_sc[...] = k
        st0 = s_sc[...]
        st_ref[0, 0] = st0
        bc = b_sc[pl.ds(C - 1, 1), :]
        a = _hg_scores(q, k, b, b_sc, k_sc)
        o_ref[...] = _dot_nn(a, v) + _dot_nt(q * jnp.exp(b), st0)
        kb = k * jnp.exp(bc - b)
        s_sc[...] = st0 * jnp.exp(bc) + _dot_tn(v, kb)

    blk = pl.BlockSpec((C, K), lambda h, c: (c, h))
    return pl.pallas_call(
        body, name=name, grid=(H, NC),
        in_specs=[blk, blk, blk, pl.BlockSpec((1, K), lambda h, c: (0, h))],
        out_specs=[blk, pl.BlockSpec((1, 1, K, K), lambda h, c: (c, h, 0, 0))],
        out_shape=[_sds((T, H * K), F32), _sds((NC, H, K, K), F32)],
        scratch_shapes=[pltpu.VMEM((K, K), F32), pltpu.VMEM((C, K), F32), pltpu.VMEM((C, K), F32)],
        compiler_params=_cp(("parallel", "arbitrary")),
    )(hq, hf, hi, lb)


def _hgrn_bwd(hq, hf, hi, lb, states, do, *, name):
    T = hq.shape[0]
    C, H, K, S = HG_CHUNK, HG_HEADS, HG_DK, HG_SUB
    NC = T // C

    def body(hq_ref, hf_ref, hi_ref, lb_ref, st_ref, do_ref, dq_ref, dhf_ref, dv_ref, dlb_ref, ds_sc, b_sc, k_sc):
        @pl.when(pl.program_id(1) == 0)
        def _():
            ds_sc[...] = jnp.zeros_like(ds_sc)
            dlb_ref[...] = jnp.zeros_like(dlb_ref)

        lbv = lb_ref[...]
        sig, f, g, k, q = _hg_gates(hq_ref[...], hf_ref[...], lbv)
        v = hi_ref[...]
        dout = do_ref[...]
        b = _cumsum_rows(g)
        b_sc[...] = b
        k_sc[...] = k
        st0 = st_ref[0, 0]
        dst1 = ds_sc[...]
        bc = b_sc[pl.ds(C - 1, 1), :]
        ebc = jnp.exp(bc)
        eb = jnp.exp(b)
        ekb = jnp.exp(bc - b)
        qt = q * eb
        kb = k * ekb
        row, col, d, diag_valid = _hg_masks()
        a = _hg_scores(q, k, b, b_sc, k_sc)
        dv = _dot_tn(a, dout) + _dot_nt(kb, dst1)
        da = jnp.where(col <= row, _dot_nt(dout, v), 0.0)
        dqt = _dot_nn(dout, st0)
        dkb = _dot_nn(v, dst1)
        ds_sc[...] = _dot_tn(dout, qt) + dst1 * ebc
        dq = dqt * eb
        dk = dkb * ekb
        dq_blocks = [jnp.zeros((S, K), F32)]
        for i in range(1, C // S):
            r = b_sc[pl.ds(S * i - 1, 1), :]
            eq = jnp.exp(b[S * i:S * (i + 1)] - r)
            ek = jnp.exp(jnp.minimum(r - b, 0.0))
            qi = q[S * i:S * (i + 1)] * eq
            kk = k * ek
            dai = jnp.where(col[S * i:S * (i + 1)] < S * i, da[S * i:S * (i + 1)], 0.0)
            dq_blocks.append(_dot_nn(dai, kk) * eq)
            dk = dk + _dot_tn(dai, qi) * ek
        dq = dq + jnp.concatenate(dq_blocks, axis=0)
        same_blk = (row // S == col // S).astype(BF16)
        tmod = (lax.broadcasted_iota(jnp.int32, (C, K), 0)) % S
        for sp in range(S):
            bs = _hg_sel_rows(b_sc, sp)
            ks = _hg_sel_rows(k_sc, sp)
            e = jnp.where(tmod >= sp, jnp.exp(jnp.minimum(b - bs, 0.0)), 0.0)
            dacol = jnp.sum(jnp.where(d == sp, da, 0.0), axis=-1, keepdims=True)
            w = dacol * e
            dq = dq + w * ks
            blk_sum = jnp.dot(same_blk, (w * q).astype(BF16), preferred_element_type=F32)
            dk = dk + jnp.where(tmod == sp, blk_sum, 0.0)
        extra = jnp.sum(dkb * kb, axis=0, keepdims=True) + ebc * jnp.sum(st0 * dst1, axis=0, keepdims=True)
        rowk = lax.broadcasted_iota(jnp.int32, (C, K), 0)
        db = q * dq - k * dk + jnp.where(rowk == C - 1, extra, 0.0)
        dg = _rcumsum_rows(db)
        df = dg / f - dk
        dq_ref[...] = (dq * (K ** -0.5)).astype(dq_ref.dtype)
        dhf_ref[...] = (df * (1.0 - lbv) * sig * (1.0 - sig)).astype(dhf_ref.dtype)
        dv_ref[...] = dv.astype(dv_ref.dtype)
        dlb_ref[...] += jnp.sum(df * (1.0 - sig), axis=0, keepdims=True)

    blk = pl.BlockSpec((C, K), lambda h, c: (NC - 1 - c, h))
    return pl.pallas_call(
        body, name=name, grid=(H, NC),
        in_specs=[blk, blk, blk, pl.BlockSpec((1, K), lambda h, c: (0, h)),
                  pl.BlockSpec((1, 1, K, K), lambda h, c: (NC - 1 - c, h, 0, 0)), blk],
        out_specs=[blk, blk, blk, pl.BlockSpec((1, K), lambda h, c: (0, h))],
        out_shape=[_sds((T, H * K), BF16)] * 3 + [_sds((1, H * K), F32)],
        scratch_shapes=[pltpu.VMEM((K, K), F32), pltpu.VMEM((C, K), F32), pltpu.VMEM((C, K), F32)],
        compiler_params=_cp(("parallel", "arbitrary")),
    )(hq, hf, hi, lb, states, do)


def _att_valid(n):
    R, B = ATT_GROUP * ATT_BLOCK, ATT_BLOCK
    t = lax.broadcasted_iota(jnp.int32, (R, 2 * B), 0) % B
    j = lax.broadcasted_iota(jnp.int32, (R, 2 * B), 1)
    dist = t + B - j
    first_key = jnp.where(n > 0, 0, B)
    return jnp.logical_and(jnp.logical_and(dist >= 0, dist < B), j >= first_key)


def _att_load(cur_ref, prev_ref, ba_ref, kv):
    hd = ATT_HD
    def cols(ref, c0):
        return ref[:, c0:c0 + hd] + ba_ref[:, c0:c0 + hd]
    qs = jnp.concatenate([cols(cur_ref, hd * (ATT_GROUP * kv + g)) for g in range(ATT_GROUP)], axis=0)
    kc = jnp.concatenate([cols(prev_ref, ATT_Q_W + hd * kv), cols(cur_ref, ATT_Q_W + hd * kv)], axis=0)
    vc = jnp.concatenate([cols(prev_ref, ATT_Q_W + ATT_KV_W + hd * kv), cols(cur_ref, ATT_Q_W + ATT_KV_W + hd * kv)], axis=0)
    return qs, kc, vc


def _att_probs(qs, kc, valid, sink_ref, kv):
    scale = 1.0 / math.sqrt(ATT_HD)
    s = jnp.where(valid, _dot_nt(qs, kc) * scale, NEG)
    sink = jnp.concatenate([jnp.full((ATT_BLOCK, 1), sink_ref[0, ATT_GROUP * kv + g], F32) for g in range(ATT_GROUP)], axis=0)
    m = jnp.maximum(jnp.max(s, axis=-1, keepdims=True), sink)
    p = jnp.exp(s - m)
    ps = jnp.exp(sink - m)
    inv = 1.0 / (jnp.sum(p, axis=-1, keepdims=True) + ps)
    return p * inv, ps * inv


def _attn_fwd(att, b_attn, sinks, *, name):
    T = att.shape[0]
    B = ATT_BLOCK
    NB = T // B

    def body(sink_ref, cur_ref, prev_ref, ba_ref, o_ref):
        valid = _att_valid(pl.program_id(0))
        for kv in range(ATT_KV):
            qs, kc, vc = _att_load(cur_ref, prev_ref, ba_ref, kv)
            prob, _ = _att_probs(qs, kc, valid, sink_ref, kv)
            o = _dot_nn(prob, vc)
            for g in range(ATT_GROUP):
                c0 = ATT_HD * (ATT_GROUP * kv + g)
                o_ref[:, c0:c0 + ATT_HD] = o[B * g:B * (g + 1)]

    return pl.pallas_call(
        body, name=name, grid=(NB,),
        in_specs=[pl.BlockSpec(memory_space=pltpu.SMEM),
                  pl.BlockSpec((B, ATT_COLS), lambda n: (n, 0)),
                  pl.BlockSpec((B, ATT_COLS), lambda n: (jnp.maximum(n - 1, 0), 0)),
                  pl.BlockSpec((1, ATT_COLS), lambda n: (0, 0))],
        out_specs=pl.BlockSpec((B, ATT_Q_W), lambda n: (n, 0)),
        out_shape=_sds((T, ATT_Q_W), F32),
        compiler_params=_cp(("parallel",)),
    )(sinks, att, att, b_attn)


def _attn_bwd(att, b_attn, sinks, dmix, *, name):
    T = att.shape[0]
    B, hd = ATT_BLOCK, ATT_HD
    NB = T // B
    scale = 1.0 / math.sqrt(hd)

    def body(sink_ref, cur_ref, prev_ref, ba_ref, do_ref, daq_ref, dakv_ref, dsink_ref, dbq_ref, dbkv_ref,
             carry_sc, cprev_sc, ccur_sc):
        n = pl.program_id(0)

        @pl.when(n == 0)
        def _():
            carry_sc[...] = jnp.zeros_like(carry_sc)
            dsink_ref[...] = jnp.zeros_like(dsink_ref)
            dbq_ref[...] = jnp.zeros_like(dbq_ref)
            dbkv_ref[...] = jnp.zeros_like(dbkv_ref)

        @pl.when(n < NB)
        def _():
            valid = _att_valid(n)
            hrow = lax.broadcasted_iota(jnp.int32, (SUBLANES, 128), 0)
            dsink = jnp.zeros((SUBLANES, 128), F32)
            for kv in range(ATT_KV):
                qs, kc, vc = _att_load(cur_ref, prev_ref, ba_ref, kv)
                prob, psink = _att_probs(qs, kc, valid, sink_ref, kv)
                dout = jnp.concatenate(
                    [do_ref[:, hd * (ATT_GROUP * kv + g):hd * (ATT_GROUP * kv + g + 1)] for g in range(ATT_GROUP)], axis=0)
                dp = _dot_nt(dout, vc)
                delta = jnp.sum(prob * dp, axis=-1, keepdims=True)
                dsc = prob * (dp - delta) * scale
                dq = _dot_nn(dsc, kc)
                dk = _dot_tn(dsc, qs)
                dvv = _dot_tn(prob, dout)
                dsk = psink * delta
                for g in range(ATT_GROUP):
                    h = ATT_GROUP * kv + g
                    daq_ref[:, hd * h:hd * (h + 1)] = dq[B * g:B * (g + 1)].astype(daq_ref.dtype)
                    tot = jnp.sum(dsk[B * g:B * (g + 1)], axis=0, keepdims=True)
                    dsink = dsink - jnp.where(hrow == h, tot, 0.0)
                cprev_sc[:, hd * kv:hd * (kv + 1)] = dk[:B]
                ccur_sc[:, hd * kv:hd * (kv + 1)] = dk[B:]
                cprev_sc[:, ATT_KV_W + hd * kv:ATT_KV_W + hd * (kv + 1)] = dvv[:B]
                ccur_sc[:, ATT_KV_W + hd * kv:ATT_KV_W + hd * (kv + 1)] = dvv[B:]
            dsink_ref[...] += dsink
            dbq_ref[...] += jnp.sum(daq_ref[...].astype(F32), axis=0, keepdims=True)
            done = carry_sc[...] + cprev_sc[...]
            dakv_ref[...] = done.astype(dakv_ref.dtype)
            dbkv_ref[...] += jnp.sum(done.astype(dakv_ref.dtype).astype(F32), axis=0, keepdims=True)
            carry_sc[...] = ccur_sc[...]

        @pl.when(n == NB)
        def _():
            done = carry_sc[...]
            dakv_ref[...] = done.astype(dakv_ref.dtype)
            dbkv_ref[...] += jnp.sum(done.astype(dakv_ref.dtype).astype(F32), axis=0, keepdims=True)

    cl = lambda n: jnp.minimum(n, NB - 1)
    return pl.pallas_call(
        body, name=name, grid=(NB + 1,),
        in_specs=[pl.BlockSpec(memory_space=pltpu.SMEM),
                  pl.BlockSpec((B, ATT_COLS), lambda n: (cl(n), 0)),
                  pl.BlockSpec((B, ATT_COLS), lambda n: (jnp.maximum(cl(n) - 1, 0), 0)),
                  pl.BlockSpec((1, ATT_COLS), lambda n: (0, 0)),
                  pl.BlockSpec((B, ATT_Q_W), lambda n: (cl(n), 1))],
        out_specs=[pl.BlockSpec((B, ATT_Q_W), lambda n: (cl(n), 0)),
                   pl.BlockSpec((B, 2 * ATT_KV_W), lambda n: (jnp.maximum(n - 1, 0), 0)),
                   pl.BlockSpec((SUBLANES, 128), lambda n: (0, 0)),
                   pl.BlockSpec((1, ATT_Q_W), lambda n: (0, 0)),
                   pl.BlockSpec((1, 2 * ATT_KV_W), lambda n: (0, 0))],
        out_shape=[_sds((T, ATT_Q_W), BF16), _sds((T, 2 * ATT_KV_W), BF16), _sds((SUBLANES, 128), F32),
                   _sds((1, ATT_Q_W), F32), _sds((1, 2 * ATT_KV_W), F32)],
        scratch_shapes=[pltpu.VMEM((B, 2 * ATT_KV_W), F32)] * 3,
        compiler_params=_cp(("arbitrary",)),
    )(sinks, att, att, b_attn, dmix)


def _silu_and_grad(x):
    sg = _sigmoid(x)
    return x * sg, sg * (1.0 + x * (1.0 - sg))


def _mix_fwd_fn(o_raw, hg, o_att, hgw):
    outs = []
    for h in range(HG_HEADS):
        sl = slice(HG_DK * h, HG_DK * (h + 1))
        silu, _ = _silu_and_grad(hg[:, sl])
        outs.append(_rms_fwd(o_raw[:, sl], hgw) * silu)
    outs.append(o_att)
    return (jnp.concatenate(outs, axis=1),)


def _mix_bwd_fn(o_raw, hg, dmix, hgw):
    dos, dhgs = [], []
    dw = jnp.zeros((1, HG_DK), F32)
    for h in range(HG_HEADS):
        sl = slice(HG_DK * h, HG_DK * (h + 1))
        silu, dsilu = _silu_and_grad(hg[:, sl])
        dy = dmix[:, sl]
        dhgs.append(dy * _rms_fwd(o_raw[:, sl], hgw) * dsilu)
        dx, dwh = _rms_bwd(o_raw[:, sl], hgw, dy * silu)
        dos.append(dx)
        dw = dw + dwh
    return jnp.concatenate(dos, axis=1), jnp.concatenate(dhgs, axis=1), dw


def _final_fn(h2, tgt, wf):
    d = h2.shape[1]
    err = _rms_fwd(h2, wf) - tgt
    loss_cols = (0.5 / d) * jnp.sum(err * err, axis=0, keepdims=True)
    dh2, dwf = _rms_bwd(h2, wf, err * (1.0 / d))
    return dh2, dh2, loss_cols, dwf


def _local_step(x, tgt, p):
    T, D = x.shape
    row = lambda n, dt: _sds((T, n), dt)
    acc = lambda n: _sds((1, n), F32)

    (u,) = _rowwise(lambda xv, w: (_rms_fwd(xv, w),), [_full(x)], [p["norm_mix_w"]], [row(D, BF16)], [], name="rms_mix")
    hq, hf, hi, hg, att = _mm_nt(u, p["w_in_t"], splits=[HG_W] * 4 + [ATT_COLS], out_dtype=F32, name="in_proj")
    o_raw, states = _hgrn_fwd(hq, hf, hi, p["lb"], name="hgrn_fwd")
    o_att = _attn_fwd(att, p["b_attn"], p["sinks"], name="attn_fwd")
    (mix,) = _rowwise(_mix_fwd_fn, [_full(o_raw), _full(hg), _full(o_att)], [p["hg_norm_w"]], [row(D, BF16)], [],
                      name="mix_fwd")
    h1 = _mm_nn([[mix]], [p["w_out"]], out_dtype=F32, name="out_proj", residual=x)
    (v,) = _rowwise(lambda hv, w: (_rms_fwd(hv, w),), [_full(h1)], [p["norm_ffn_w"]], [row(D, BF16)], [], name="rms_ffn")
    (gp,) = _mm_nt(v, p["w_gate_t"], splits=[D_FF], out_dtype=F32, name="gate_proj")
    (up,) = _mm_nt(v, p["w_up_t"], splits=[D_FF], out_dtype=F32, name="up_proj")
    act = _convact_fwd(gp, up, p["conv_w8"], p["conv_b"], name="convact_fwd")
    h2 = _mm_nn([[act]], [p["w_down"]], out_dtype=F32, name="down_proj", residual=h1)
    dh2, dh2_b, loss_cols, d_final = _rowwise(_final_fn, [_full(h2), _full(tgt)], [p["final_norm_w"]],
                                              [row(D, F32), row(D, BF16)], [acc(D), acc(D)], name="final_loss")

    (dact,) = _mm_nt(dh2_b, p["w_down"], splits=[D_FF], out_dtype=F32, name="d_act")
    g_down = _mm_tn([act], dh2_b, name="g_down")
    dgp, dup, d_conv_w8, d_conv_b = _convact_bwd(gp, up, dact, p["conv_w8"], p["conv_b"], name="convact_bwd")
    dv = _mm_nn([[dgp], [dup]], [p["w_gate_t"], p["w_up_t"]], out_dtype=F32, name="d_v")
    g_gate_t = _mm_tn([dgp], v, name="g_gate")
    g_up_t = _mm_tn([dup], v, name="g_up")

    def ffn_norm_bwd(hv, dvv, dh2v, w):
        dx, dw = _rms_bwd(hv, w, dvv)
        dh1v = dx + dh2v
        return dh1v, dh1v, dw

    dh1, dh1_b, d_norm_ffn = _rowwise(ffn_norm_bwd, [_full(h1), _full(dv), _full(dh2)], [p["norm_ffn_w"]],
                                      [row(D, F32), row(D, BF16)], [acc(D)], name="rms_ffn_bwd")
    (dmix,) = _mm_nt(dh1_b, p["w_out"], splits=[D], out_dtype=F32, name="d_mix")
    g_out = _mm_tn([mix], dh1_b, name="g_out")
    do_raw, dhg, d_hg_norm = _rowwise(_mix_bwd_fn, [_full(o_raw), _full(hg), (dmix, HG_W, 0, 0)], [p["hg_norm_w"]],
                                      [row(HG_W, F32), row(HG_W, BF16)], [acc(HG_DK)], name="mix_bwd")
    daq, dakv, d_sinks8, d_bq, d_bkv = _attn_bwd(att, p["b_attn"], p["sinks"], dmix, name="attn_bwd")
    dhq, dhf, dhi, d_lb = _hgrn_bwd(hq, hf, hi, p["lb"], states, do_raw, name="hgrn_bwd")
    pieces = [dhq, dhf, dhi, dhg, daq, dakv]
    du = _mm_nn([pieces], [p["w_in_t"]], out_dtype=F32, name="d_u")
    g_in_t = _mm_tn(pieces, u, name="g_in")

    def mix_norm_bwd(xv, duv, dh1v, w):
        dx, dw = _rms_bwd(xv, w, duv)
        return dx + dh1v, dw

    dx, d_norm_mix = _rowwise(mix_norm_bwd, [_full(x), _full(du), _full(dh1)], [p["norm_mix_w"]], [row(D, F32)], [acc(D)],
                              name="rms_mix_bwd")
    grads = dict(g_in_t=g_in_t, g_out=g_out, g_gate_t=g_gate_t, g_up_t=g_up_t, g_down=g_down,
                 norm_mix_w=d_norm_mix, b_attn=jnp.concatenate([d_bq, d_bkv], axis=1), lb=d_lb, hg_norm_w=d_hg_norm,
                 sinks8=d_sinks8, norm_ffn_w=d_norm_ffn, conv_w8=d_conv_w8, conv_b=d_conv_b, final_norm_w=d_final)
    return loss_cols, dx, grads


SLAB = (IN_COLS // N_CHIPS, D_FF // N_CHIPS, D_FF // N_CHIPS, D_FF // N_CHIPS, D_MODEL // N_CHIPS)
N_W = len(SLAB)
PACK_OFF = tuple(sum(SLAB[:i]) for i in range(N_W))
PACK_ROWS = sum(SLAB)
FULL_OFF = tuple(N_CHIPS * o for o in PACK_OFF)
FULL_ROWS = N_CHIPS * PACK_ROWS
HALF = tuple(s // 2 for s in SLAB)
HPACK_OFF = tuple(sum(HALF[:i]) for i in range(N_W))
HPACK_ROWS = sum(HALF)
HFULL_OFF = tuple(N_CHIPS * o for o in HPACK_OFF)
HFULL_ROWS = N_CHIPS * HPACK_ROWS
CHIP_FLIPS = ((1, 0), (0, 1), (1, 1))
N_DEV = 8
BF16_ROWS = 16
ANY = pl.BlockSpec(memory_space=pl.ANY)


def _pos():
    return lax.axis_index("x"), lax.axis_index("y"), lax.axis_index("c")


def _flip(v, f):
    return 1 - v if f else v


def _rcopy(src, dst, ssem, rsem, dev):
    return pltpu.make_async_remote_copy(src_ref=src, dst_ref=dst, send_sem=ssem, recv_sem=rsem, device_id=dev,
                                        device_id_type=pl.DeviceIdType.MESH)


def _rows(ref, start, n, align=SUBLANES):
    if not isinstance(start, int):
        start = pl.multiple_of(start, align)
    return ref.at[pl.ds(start, n), :]


def _gather_weights(pack, cw8):
    D = pack.shape[1]
    n_peer = 1 + len(CHIP_FLIPS)

    def body(pack_ref, cw_ref, full_ref, cwall_ref, send_sems, recv_sems):
        x, y, c = _pos()
        q = 2 * x + y
        peers = [(x, y, 1 - c)] + [(_flip(x, fx), _flip(y, fy), c) for fx, fy in CHIP_FLIPS]
        drains = []
        for k, peer in enumerate(peers):
            for w in range(N_W):
                _rcopy(_rows(pack_ref, PACK_OFF[w], SLAB[w]), _rows(full_ref, FULL_OFF[w] + q * SLAB[w], SLAB[w], BF16_ROWS),
                       send_sems.at[k], recv_sems.at[k], peer).start()
            cwc = _rcopy(cw_ref, cwall_ref.at[q], send_sems.at[n_peer + k], recv_sems.at[n_peer + k], peer)
            cwc.start()
            drains.append(_rcopy(pack_ref, _rows(full_ref, 0, PACK_ROWS), send_sems.at[k], recv_sems.at[k], peer))
            drains.append(cwc)
        for d in drains:
            d.wait_recv()
        for d in drains:
            d.wait_send()

    return pl.pallas_call(
        body, name="gather_weights", in_specs=[ANY, ANY], out_specs=[ANY, ANY],
        out_shape=[_sds((FULL_ROWS, D), pack.dtype), _sds((N_CHIPS,) + cw8.shape, cw8.dtype)],
        scratch_shapes=[pltpu.SemaphoreType.DMA((2 * n_peer,)), pltpu.SemaphoreType.DMA((2 * n_peer,))],
    )(pack, cw8)


def _exchange_halves(gs, small):
    D = gs[0].shape[1]

    def body(g0, g1, g2, g3, g4, small_ref, t0, t1, t2, t3, t4, sall_ref, d2d_send, d2d_recv, sm_send, sm_recv, loc_sem):
        g = (g0, g1, g2, g3, g4)
        t = (t0, t1, t2, t3, t4)
        x, y, c = _pos()
        sib = (x, y, 1 - c)
        me = 4 * x + 2 * y + c
        own_small = pltpu.make_async_copy(small_ref, sall_ref.at[me], loc_sem)
        own_small.start()
        drains = []
        for w in range(N_W):
            h = HALF[w]
            for qq in range(N_CHIPS):
                _rcopy(_rows(g[w], qq * SLAB[w] + (1 - c) * h, h), _rows(t[w], qq * h, h),
                       d2d_send.at[w], d2d_recv.at[w], sib).start()
            drains.append(_rcopy(t[w], t[w], d2d_send.at[w], d2d_recv.at[w], sib))
        for f in range(1, N_DEV):
            peer = (_flip(x, f & 4), _flip(y, f & 2), _flip(c, f & 1))
            cp = _rcopy(small_ref, sall_ref.at[me], sm_send.at[f - 1], sm_recv.at[f - 1], peer)
            cp.start()
            drains.append(cp)
        for d in drains:
            d.wait_recv()
        for d in drains:
            d.wait_send()
        own_small.wait()

    return pl.pallas_call(
        body, name="exchange_halves", in_specs=[ANY] * (N_W + 1), out_specs=[ANY] * (N_W + 1),
        out_shape=[_sds((N_CHIPS * h, D), F32) for h in HALF] + [_sds((N_DEV,) + small.shape, F32)],
        scratch_shapes=[pltpu.SemaphoreType.DMA((N_W,)), pltpu.SemaphoreType.DMA((N_W,)), pltpu.SemaphoreType.DMA((N_DEV - 1,)),
                        pltpu.SemaphoreType.DMA((N_DEV - 1,)), pltpu.SemaphoreType.DMA],
    )(*gs, small)


REDUCE_SPLIT = 2


def _chip_partial(gs, theirs):
    D = gs[0].shape[1]

    def body(*refs):
        for w in range(N_W):
            refs[2 * N_W + w][...] = refs[w][...] + refs[N_W + w][...]

    blk = [h // REDUCE_SPLIT for h in HALF]
    mine = [pl.BlockSpec((b, D), lambda qq, j: ((2 * qq + lax.axis_index("c")) * REDUCE_SPLIT + j, 0)) for b in blk]
    flat = [pl.BlockSpec((b, D), lambda qq, j: (qq * REDUCE_SPLIT + j, 0)) for b in blk]
    return pl.pallas_call(
        body, name="chip_partial", grid=(N_CHIPS, REDUCE_SPLIT), in_specs=mine + flat, out_specs=flat,
        out_shape=[_sds((N_CHIPS * h, D), F32) for h in HALF],
        compiler_params=_cp(("parallel", "parallel")),
    )(*gs, *theirs)


def _send_chip_partials(parts):
    D = parts[0].shape[1]
    n = len(CHIP_FLIPS) * N_W

    def body(p0, p1, p2, p3, p4, g0, g1, g2, g3, g4, send_sems, recv_sems):
        part = (p0, p1, p2, p3, p4)
        got = (g0, g1, g2, g3, g4)
        x, y, c = _pos()
        cps = []
        for k, (fx, fy) in enumerate(CHIP_FLIPS):
            peer = (_flip(x, fx), _flip(y, fy), c)
            qp = 2 * _flip(x, fx) + _flip(y, fy)
            for w in range(N_W):
                cp = _rcopy(_rows(part[w], qp * HALF[w], HALF[w]), _rows(got[w], k * HALF[w], HALF[w]),
                            send_sems.at[N_W * k + w], recv_sems.at[N_W * k + w], peer)
                cp.start()
                cps.append(cp)
        for cp in cps:
            cp.wait_recv()
        for cp in cps:
            cp.wait_send()

    return pl.pallas_call(
        body, name="send_chip_partials", in_specs=[ANY] * N_W, out_specs=[ANY] * N_W,
        out_shape=[_sds((len(CHIP_FLIPS) * h, D), F32) for h in HALF],
        scratch_shapes=[pltpu.SemaphoreType.DMA((n,)), pltpu.SemaphoreType.DMA((n,))],
    )(*parts)


def _chip_reduce(parts, got):
    D = parts[0].shape[1]
    nk = len(CHIP_FLIPS)

    def body(*refs):
        outs = refs[(1 + nk) * N_W:]
        for w in range(N_W):
            acc = refs[w][...]
            for k in range(nk):
                acc = acc + refs[N_W * (1 + k) + w][...]
            outs[w][...] = acc

    blk = [h // REDUCE_SPLIT for h in HALF]

    def q_idx(j):
        return (2 * lax.axis_index("x") + lax.axis_index("y")) * REDUCE_SPLIT + j

    in_specs = [pl.BlockSpec((b, D), lambda j: (q_idx(j), 0)) for b in blk]
    for k in range(nk):
        in_specs += [pl.BlockSpec((b, D), functools.partial(lambda j, k: (k * REDUCE_SPLIT + j, 0), k=k)) for b in blk]
    out_specs = [pl.BlockSpec((b, D), lambda j: (lax.axis_index("c") * REDUCE_SPLIT + j, 0)) for b in blk]
    return pl.pallas_call(
        body, name="chip_reduce", grid=(REDUCE_SPLIT,), in_specs=in_specs, out_specs=out_specs,
        out_shape=[_sds((s, D), F32) for s in SLAB],
        compiler_params=_cp(("parallel",)),
    )(*parts, *[g for _ in range(nk) for g in got])


def _exchange_reduced(shards):
    def body(i0, i1, i2, i3, i4, o0, o1, o2, o3, o4, send_sems, recv_sems):
        ins = (i0, i1, i2, i3, i4)
        outs = (o0, o1, o2, o3, o4)
        x, y, c = _pos()
        sib = (x, y, 1 - c)
        cps = []
        for w in range(N_W):
            cp = _rcopy(_rows(ins[w], c * HALF[w], HALF[w]), _rows(outs[w], c * HALF[w], HALF[w]),
                        send_sems.at[w], recv_sems.at[w], sib)
            cp.start()
            cps.append(cp)
        for cp in cps:
            cp.wait_recv()
        for cp in cps:
            cp.wait_send()

    return pl.pallas_call(
        body, name="exchange_reduced", in_specs=[ANY] * N_W, out_specs=[ANY] * N_W,
        out_shape=[_sds(s.shape, s.dtype) for s in shards], input_output_aliases={w: w for w in range(N_W)},
        scratch_shapes=[pltpu.SemaphoreType.DMA((N_W,)), pltpu.SemaphoreType.DMA((N_W,))],
    )(*shards)


def _adamw_fn(w, g, m, v):
    m2 = ADAM_B1 * m + (1.0 - ADAM_B1) * g
    v2 = ADAM_B2 * v + (1.0 - ADAM_B2) * (g * g)
    m_hat = m2 / (1.0 - ADAM_B1 ** ADAM_STEP)
    v_hat = v2 / (1.0 - ADAM_B2 ** ADAM_STEP)
    return -ADAM_LR * (m_hat / (jnp.sqrt(v_hat) + ADAM_EPS) + ADAM_WD * w), m2, v2


def _adamw(w, g, m, v, *, name):
    shp = _sds(w.shape, F32)
    rows = w.shape[0]
    tm = max(t for t in range(SUBLANES, 512 + 1, SUBLANES) if rows % t == 0)
    return _rowwise(_adamw_fn, [_full(w), _full(g), _full(m), _full(v)], [], [shp] * 3, [], name=name, tm=tm)


SMALL_SEGS = (("loss", 8), ("norm_mix_w", 8), ("b_attn", 8), ("lb_logits", 8), ("hg_norm_w", 8), ("sinks", 8),
              ("norm_ffn_w", 8), ("conv_w", 72), ("conv_b", 24), ("final_norm_w", 8))
SMALL_OFF = {n: sum(r for _, r in SMALL_SEGS[:i]) for i, (n, _) in enumerate(SMALL_SEGS)}
SMALL_ROWS = sum(r for _, r in SMALL_SEGS)
LANES = 128


def _pack_small(parts):
    segs = []
    for n, r in SMALL_SEGS:
        a = parts.get(n)
        flat = jnp.zeros((0,), F32) if a is None else a.reshape(-1).astype(F32)
        segs.append(jnp.pad(flat, (0, r * LANES - flat.shape[0])).reshape(r, LANES))
    return jnp.concatenate(segs, axis=0)


def _unpack_small(pack, n, shape):
    size = math.prod(shape)
    r0 = SMALL_OFF[n]
    return pack[r0:r0 + dict(SMALL_SEGS)[n]].reshape(-1)[:size].reshape(shape)


def _small_update(sall, wp, mp, vp):
    R = SMALL_ROWS
    r_lb = SMALL_OFF["lb_logits"]

    def body(s_ref, w_ref, m_ref, v_ref, g_ref, d_ref, m2_ref, v2_ref, loss_ref):
        g = s_ref[0]
        for i in range(1, N_DEV):
            g = g + s_ref[i]
        tot = jnp.sum(jnp.sum(g[0:8], axis=1, keepdims=True), axis=0, keepdims=True)
        loss_ref[...] = jnp.broadcast_to(tot, loss_ref.shape)
        lg = w_ref[r_lb:r_lb + 8, :]
        p0 = _sigmoid(lg - pltpu.roll(lg, 4, 0))
        d = g[r_lb:r_lb + 8]
        d = d + pltpu.roll(d, 4, 0)
        sign = jnp.where(lax.broadcasted_iota(jnp.int32, d.shape, 0) < 4, 1.0, -1.0)
        g = jnp.concatenate([g[:r_lb], sign * d * p0 * (1.0 - p0), g[r_lb + 8:]], axis=0)
        g_ref[...] = g
        d_ref[...], m2_ref[...], v2_ref[...] = _adamw_fn(w_ref[...], g, m_ref[...], v_ref[...])

    full = pl.BlockSpec((R, LANES), lambda: (0, 0))
    return pl.pallas_call(
        body, name="small_update",
        in_specs=[pl.BlockSpec((N_DEV, R, LANES), lambda: (0, 0, 0)), full, full, full],
        out_specs=[full, full, full, full, pl.BlockSpec((8, LANES), lambda: (0, 0))],
        out_shape=[_sds((R, LANES), F32)] * 4 + [_sds((8, LANES), F32)],
        compiler_params=_cp(),
    )(sall, wp, mp, vp)


def _lb_fwd(lb_logits):
    n = lb_logits.shape[1]

    def body(l_ref, o_ref):
        o_ref[...] = _sigmoid(l_ref[0:1, :] - l_ref[1:2, :])

    return pl.pallas_call(body, name="lb_fwd", out_shape=_sds((1, n), F32), compiler_params=_cp())(lb_logits)


def kernel(x, norm_mix_w, w_in, b_attn, lb_logits, hg_norm_w, sinks, w_out, norm_ffn_w, w_gate, w_up, conv_w, conv_b, w_down, final_norm_w, loss_target, m_norm_mix_w, m_w_in, m_b_attn, m_lb_logits, m_hg_norm_w, m_sinks, m_w_out, m_norm_ffn_w, m_w_gate, m_w_up, m_conv_w, m_conv_b, m_w_down, m_final_norm_w, v_norm_mix_w, v_w_in, v_b_attn, v_lb_logits, v_hg_norm_w, v_sinks, v_w_out, v_norm_ffn_w, v_w_gate, v_w_up, v_conv_w, v_conv_b, v_w_down, v_final_norm_w):
    D = D_MODEL
    q = 2 * lax.axis_index("x") + lax.axis_index("y")
    ccols = D_FF // N_CHIPS

    pack = jnp.concatenate([w_in[0].T, w_gate[0].T, w_up[0].T, w_down[0], w_out[0]], axis=0).astype(BF16)
    cw8 = jnp.concatenate([conv_w[0], jnp.zeros((SUBLANES - 3, ccols), F32)], axis=0)
    wfull, cw_all = _gather_weights(pack, cw8)
    conv_w8 = jnp.concatenate([cw_all[i] for i in range(N_CHIPS)], axis=1)
    nblk = FULL_OFF[1] // (N_CHIPS * SLAB[0])
    p = dict(
        norm_mix_w=norm_mix_w, b_attn=b_attn, lb=_lb_fwd(lb_logits), hg_norm_w=hg_norm_w, sinks=sinks,
        norm_ffn_w=norm_ffn_w, conv_w8=conv_w8, conv_b=conv_b, final_norm_w=final_norm_w.reshape(1, D),
        w_in_t=(wfull, N_CHIPS * SLAB[0], 0), w_gate_t=(wfull, N_CHIPS * SLAB[1], 1), w_up_t=(wfull, N_CHIPS * SLAB[2], 2),
        w_down=(wfull, N_CHIPS * SLAB[3], 3), w_out=(wfull, N_CHIPS * SLAB[4], FULL_OFF[4] // (N_CHIPS * SLAB[4])),
    )
    assert nblk == 1

    loss_cols, dx, g = _local_step(x[0], loss_target[0], p)

    small = _pack_small(dict(loss=loss_cols, norm_mix_w=g["norm_mix_w"], b_attn=g["b_attn"], lb_logits=g["lb"],
                             hg_norm_w=g["hg_norm_w"], sinks=g["sinks8"], norm_ffn_w=g["norm_ffn_w"],
                             conv_w=g["conv_w8"][:3], conv_b=g["conv_b"], final_norm_w=g["final_norm_w"]))
    gs = [g["g_in_t"], g["g_gate_t"], g["g_up_t"], g["g_down"], g["g_out"]]
    *theirs, sall = _exchange_halves(gs, small)
    parts = _chip_partial(gs, theirs)
    got = _send_chip_partials(parts)
    shards = _exchange_reduced(_chip_reduce(parts, got))
    g_in, g_gate, g_up, g_down, g_out = shards[0].T, shards[1].T, shards[2].T, shards[3], shards[4]
    big = {}
    for n, gw, w, m, v in (("w_in", g_in, w_in, m_w_in, v_w_in), ("w_out", g_out, w_out, m_w_out, v_w_out),
                           ("w_gate", g_gate, w_gate, m_w_gate, v_w_gate), ("w_up", g_up, w_up, m_w_up, v_w_up),
                           ("w_down", g_down, w_down, m_w_down, v_w_down)):
        d_, m_, v_ = _adamw(w[0], gw, m[0], v[0], name="adamw_" + n)
        big[n] = (gw[None], d_[None], m_[None], v_[None])

    def place(a):
        return lax.dynamic_update_slice(jnp.zeros((3, D_FF), F32), a[0], (0, q * ccols))

    def small_pack(ws, cw):
        nm, ba, lbl, hg, sk, nf, cb, fn = ws
        return _pack_small(dict(norm_mix_w=nm, b_attn=ba, lb_logits=lbl, hg_norm_w=hg,
                                sinks=jnp.broadcast_to(sk.reshape(ATT_HEADS, 1), (ATT_HEADS, LANES)), norm_ffn_w=nf,
                                conv_w=cw, conv_b=cb, final_norm_w=fn))

    wp = small_pack((norm_mix_w, b_attn, lb_logits, hg_norm_w, sinks, norm_ffn_w, conv_b, final_norm_w), conv_w8[:3])
    mp = small_pack((m_norm_mix_w, m_b_attn, m_lb_logits, m_hg_norm_w, m_sinks, m_norm_ffn_w, m_conv_b, m_final_norm_w),
                    place(m_conv_w))
    vp = small_pack((v_norm_mix_w, v_b_attn, v_lb_logits, v_hg_norm_w, v_sinks, v_norm_ffn_w, v_conv_b, v_final_norm_w),
                    place(v_conv_w))
    outs = _small_update(sall, wp, mp, vp)
    loss = outs[4][0, 0]

    def small_out(pk, n, ref):
        if n == "sinks":
            return pk[SMALL_OFF[n]:SMALL_OFF[n] + ATT_HEADS, 0].reshape(ref.shape)
        if n == "conv_w":
            full = _unpack_small(pk, n, (3, D_FF))
            return lax.dynamic_slice(full, (0, q * ccols), (3, ccols))[None]
        return _unpack_small(pk, n, ref.shape)

    refs = dict(norm_mix_w=norm_mix_w, b_attn=b_attn, lb_logits=lb_logits, hg_norm_w=hg_norm_w, sinks=sinks,
                norm_ffn_w=norm_ffn_w, conv_w=conv_w, conv_b=conv_b, final_norm_w=final_norm_w)
    order = ("norm_mix_w", "w_in", "b_attn", "lb_logits", "hg_norm_w", "sinks", "w_out", "norm_ffn_w", "w_gate", "w_up",
             "conv_w", "conv_b", "w_down", "final_norm_w")
    res = [loss, dx[None]]
    for k in range(4):
        for n in order:
            res.append(big[n][k] if n in big else small_out(outs[k], n, refs[n]))
    return tuple(res)
```

```python
import functools
import math

import jax
import jax.numpy as jnp
from jax import lax
from jax.experimental import pallas as pl
from jax.experimental.pallas import tpu as pltpu

F32 = jnp.float32
BF16 = jnp.bfloat16

D_MODEL = 1024
HG_HEADS = 4
HG_DK = 128
HG_W = HG_HEADS * HG_DK
HG_CHUNK = 64
HG_SUB = 16
ATT_HEADS = 8
ATT_KV = 2
ATT_GROUP = ATT_HEADS // ATT_KV
ATT_HD = 64
ATT_BLOCK = 128
ATT_Q_W = ATT_HEADS * ATT_HD
ATT_KV_W = ATT_KV * ATT_HD
ATT_COLS = ATT_Q_W + 2 * ATT_KV_W
IN_COLS = 4 * HG_W + ATT_COLS
D_FF = 2816
EPS = 1e-6
ADAM_LR, ADAM_B1, ADAM_B2, ADAM_EPS, ADAM_WD, ADAM_STEP = 0.001, 0.9, 0.999, 1e-08, 0.01, 10
NEG = -1e30

V7X_VMEM_BYTES = 64 * 1024 * 1024
VMEM_LIMIT = 48 * 1024 * 1024
SUBLANES = 8

N_CHIPS = 4


def _cp(sem=None, **kw):
    return pltpu.CompilerParams(dimension_semantics=sem, vmem_limit_bytes=VMEM_LIMIT, **kw)


def _sds(shape, dtype):
    return jax.ShapeDtypeStruct(shape, dtype)


def _wspec(w):
    arr, rows, blk = w
    return pl.BlockSpec((rows, arr.shape[1]), lambda i: (blk, 0))


def _mm_nt(a, w, *, splits, out_dtype, name, residual=None, tm=512):
    M, K = a.shape
    N = w[1]
    tm = min(tm, M)
    assert sum(splits) == N and M % tm == 0
    offs = [sum(splits[:i]) for i in range(len(splits))]

    def body(*refs):
        a_ref, w_ref = refs[0], refs[1]
        outs = refs[2 + (residual is not None):]
        acc = lax.dot_general(a_ref[...], w_ref[...], (((1,), (1,)), ((), ())), preferred_element_type=F32)
        if residual is not None:
            acc = acc + refs[2][...]
        for o_ref, c0, n in zip(outs, offs, splits):
            o_ref[...] = acc[:, c0:c0 + n].astype(out_dtype)

    in_specs = [pl.BlockSpec((tm, K), lambda i: (i, 0)), _wspec(w)]
    args = [a, w[0]]
    if residual is not None:
        assert len(splits) == 1
        in_specs.append(pl.BlockSpec((tm, N), lambda i: (i, 0)))
        args.append(residual)
    outs = pl.pallas_call(
        body, name=name, grid=(M // tm,), in_specs=in_specs,
        out_specs=[pl.BlockSpec((tm, n), lambda i: (i, 0)) for n in splits],
        out_shape=[_sds((M, n), out_dtype) for n in splits],
        compiler_params=_cp(("parallel",)),
    )(*args)
    return outs


def _mm_nn(pieces, ws, *, out_dtype, name, residual=None, tm=512):
    M = pieces[0][0].shape[0]
    K = ws[0][0].shape[1]
    tm = min(tm, M)
    flat = [p for grp in pieces for p in grp]
    n_p = len(flat)

    def body(*refs):
        p_refs = refs[:n_p]
        w_refs = refs[n_p:n_p + len(ws)]
        o_ref = refs[-1]
        acc = None if residual is None else refs[n_p + len(ws)][...]
        k = 0
        for gi, grp in enumerate(pieces):
            c0 = 0
            for p in grp:
                n = p.shape[1]
                t = jnp.dot(p_refs[k][...], w_refs[gi][c0:c0 + n, :], preferred_element_type=F32)
                acc = t if acc is None else acc + t
                c0 += n
                k += 1
        o_ref[...] = acc.astype(out_dtype)

    in_specs = [pl.BlockSpec((tm, p.shape[1]), lambda i: (i, 0)) for p in flat]
    in_specs += [_wspec(w) for w in ws]
    args = [*flat, *[w[0] for w in ws]]
    if residual is not None:
        in_specs.append(pl.BlockSpec((tm, K), lambda i: (i, 0)))
        args.append(residual)
    return pl.pallas_call(
        body, name=name, grid=(M // tm,), in_specs=in_specs,
        out_specs=pl.BlockSpec((tm, K), lambda i: (i, 0)),
        out_shape=_sds((M, K), out_dtype),
        compiler_params=_cp(("parallel",)),
    )(*args)


def _mm_tn(pieces, x, *, name, tt=512):
    M, K = x.shape
    tt = min(tt, M)
    ns = [p.shape[1] for p in pieces]
    offs = [sum(ns[:i]) for i in range(len(ns))]
    N = sum(ns)
    n_p = len(pieces)

    def body(*refs):
        p_refs = refs[:n_p]
        x_ref = refs[n_p]
        o_ref = refs[n_p + 1]

        @pl.when(pl.program_id(0) == 0)
        def _():
            o_ref[...] = jnp.zeros_like(o_ref)

        xv = x_ref[...]
        for p_ref, c0, n in zip(p_refs, offs, ns):
            o_ref[c0:c0 + n, :] += lax.dot_general(p_ref[...], xv, (((0,), (0,)), ((), ())),
                                                    preferred_element_type=F32)

    in_specs = [pl.BlockSpec((tt, n), lambda i: (i, 0)) for n in ns]
    in_specs.append(pl.BlockSpec((tt, K), lambda i: (i, 0)))
    return pl.pallas_call(
        body, name=name, grid=(M // tt,), in_specs=in_specs,
        out_specs=pl.BlockSpec((N, K), lambda i: (0, 0)),
        out_shape=_sds((N, K), F32),
        compiler_params=_cp(("arbitrary",)),
    )(*pieces, x)


def _rms_fwd(xf, w):
    inv = lax.rsqrt(jnp.mean(xf * xf, axis=-1, keepdims=True) + EPS)
    return xf * inv * w


def _rms_bwd(xf, w, dy):
    inv = lax.rsqrt(jnp.mean(xf * xf, axis=-1, keepdims=True) + EPS)
    xhat = xf * inv
    dxhat = dy * w
    dx = inv * (dxhat - xhat * jnp.mean(dxhat * xhat, axis=-1, keepdims=True))
    dw = jnp.sum(dy * xhat, axis=0, keepdims=True)
    return dx, dw


def _sigmoid(x):
    return 1.0 / (1.0 + jnp.exp(-x))


def _rowwise(fn, row_ins, bc_ins, row_outs, acc_outs, *, name, tm=256, after=None):
    M = row_outs[0].shape[0] if row_outs else row_ins[0][0].shape[0]
    assert M % tm == 0 and tm % SUBLANES == 0, (name, M, tm)
    n_r, n_b, n_o, n_a = len(row_ins), len(bc_ins), len(row_outs), len(acc_outs)
    n_after = 0 if after is None else 1

    def body(*refs):
        refs = refs[n_after:]
        ins = [r[...] for r in refs[:n_r + n_b]]
        o_refs = refs[n_r + n_b:n_r + n_b + n_o]
        a_refs = refs[n_r + n_b + n_o:]
        res = fn(*ins)
        for o_ref, val in zip(o_refs, res[:n_o]):
            o_ref[...] = val.astype(o_ref.dtype)
        if n_a:
            @pl.when(pl.program_id(0) == 0)
            def _():
                for a_ref in a_refs:
                    a_ref[...] = jnp.zeros_like(a_ref)
            for a_ref, val in zip(a_refs, res[n_o:]):
                a_ref[...] += val

    in_specs = [pl.BlockSpec((tm, cw), functools.partial(lambda i, cb, r0: (i + r0, cb), cb=cb, r0=r0))
                for (_, cw, cb, r0) in row_ins]
    in_specs += [pl.BlockSpec(b.shape, lambda i: (0, 0)) for b in bc_ins]
    out_specs = [pl.BlockSpec((tm, s.shape[1]), lambda i: (i, 0)) for s in row_outs]
    out_specs += [pl.BlockSpec(s.shape, lambda i: (0, 0)) for s in acc_outs]
    if n_after:
        in_specs = [pl.BlockSpec(memory_space=pl.ANY)] + in_specs
    return pl.pallas_call(
        body, name=name, grid=(M // tm,), in_specs=in_specs, out_specs=out_specs,
        out_shape=list(row_outs) + list(acc_outs),
        compiler_params=_cp(("arbitrary",) if n_a else ("parallel",)),
    )(*([after] if n_after else []), *[r[0] for r in row_ins], *bc_ins)


def _full(a, first_row_block=0):
    return (a, a.shape[1], 0, first_row_block)


def _conv_rows(ext, w_ref_val, lo):
    s1 = pltpu.roll(ext, 1, 0)
    s2 = pltpu.roll(ext, 2, 0)
    y = w_ref_val[0:1, :] * s2 + w_ref_val[1:2, :] * s1 + w_ref_val[2:3, :] * ext
    return y[SUBLANES:, :]


def _convact_fwd(gp, up, conv_w8, conv_b, *, name, tr=512, tc=256):
    T, C = gp.shape
    tr = min(tr, T)
    hb = tr // SUBLANES

    def body(gp_ref, gph_ref, up_ref, w_ref, b_ref, act_ref):
        i = pl.program_id(1)
        halo = jnp.where(i > 0, gph_ref[...], 0.0)
        ext = jnp.concatenate([halo, gp_ref[...]], axis=0)
        gate = _conv_rows(ext, w_ref[...], 0) + b_ref[...]
        act_ref[...] = (gate * _sigmoid(gate) * up_ref[...]).astype(act_ref.dtype)

    return pl.pallas_call(
        body, name=name, grid=(C // tc, T // tr),
        in_specs=[pl.BlockSpec((tr, tc), lambda j, i: (i, j)),
                  pl.BlockSpec((SUBLANES, tc), lambda j, i: (jnp.maximum(i * hb - 1, 0), j)),
                  pl.BlockSpec((tr, tc), lambda j, i: (i, j)),
                  pl.BlockSpec((SUBLANES, tc), lambda j, i: (0, j)),
                  pl.BlockSpec((1, tc), lambda j, i: (0, j))],
        out_specs=pl.BlockSpec((tr, tc), lambda j, i: (i, j)),
        out_shape=_sds((T, C), BF16),
        compiler_params=_cp(("parallel", "parallel")),
    )(gp, gp, up, conv_w8, conv_b)


def _convact_bwd(gp, up, dact, conv_w8, conv_b, *, name, tr=512, tc=256):
    T, C = gp.shape
    tr = min(tr, T)
    hb = tr // SUBLANES
    nr = T // tr

    def body(gp_ref, gpp_ref, gpn_ref, up_ref, upn_ref, da_ref, dan_ref, w_ref, b_ref,
             dgp_ref, dup_ref, dw_ref, db_ref):
        i = pl.program_id(1)
        w = w_ref[...]
        prev = jnp.where(i > 0, gpp_ref[...], 0.0)
        last = i == nr - 1
        gp_ext = jnp.concatenate([prev, gp_ref[...], gpn_ref[...]], axis=0)
        gate = _conv_rows(gp_ext, w, 0) + b_ref[...]
        up_e = jnp.concatenate([up_ref[...], upn_ref[...]], axis=0)
        da_e = jnp.concatenate([da_ref[...], dan_ref[...]], axis=0)
        row = lax.broadcasted_iota(jnp.int32, gate.shape, 0)
        valid = jnp.logical_or(row < tr, jnp.logical_not(last))
        sg = _sigmoid(gate)
        silu = gate * sg
        dgate = jnp.where(valid, da_e * up_e * (sg * (1.0 + gate * (1.0 - sg))), 0.0)
        dup_ref[...] = (da_e[:tr] * silu[:tr]).astype(dup_ref.dtype)
        n = tr + SUBLANES
        g1 = pltpu.roll(dgate, n - 1, 0)
        g2 = pltpu.roll(dgate, n - 2, 0)
        dgp = w[2:3, :] * dgate + w[1:2, :] * g1 + w[0:1, :] * g2
        dgp_ref[...] = dgp[:tr].astype(dgp_ref.dtype)
        gpc = gp_ref[...]
        dw0 = jnp.sum(gpc * g2[:tr], axis=0, keepdims=True)
        dw1 = jnp.sum(gpc * g1[:tr], axis=0, keepdims=True)
        dw2 = jnp.sum(gpc * dgate[:tr], axis=0, keepdims=True)
        dbv = jnp.sum(dgate[:tr], axis=0, keepdims=True)
        z = jnp.zeros((SUBLANES - 3, gpc.shape[1]), F32)

        @pl.when(i == 0)
        def _():
            dw_ref[...] = jnp.zeros_like(dw_ref)
            db_ref[...] = jnp.zeros_like(db_ref)

        dw_ref[...] += jnp.concatenate([dw0, dw1, dw2, z], axis=0)
        db_ref[...] += dbv

    cur = pl.BlockSpec((tr, tc), lambda j, i: (i, j))
    prv = pl.BlockSpec((SUBLANES, tc), lambda j, i: (jnp.maximum(i * hb - 1, 0), j))
    nxt = pl.BlockSpec((SUBLANES, tc), lambda j, i: (jnp.minimum((i + 1) * hb, T // SUBLANES - 1), j))
    return pl.pallas_call(
        body, name=name, grid=(C // tc, nr),
        in_specs=[cur, prv, nxt, cur, nxt, cur, nxt,
                  pl.BlockSpec((SUBLANES, tc), lambda j, i: (0, j)),
                  pl.BlockSpec((1, tc), lambda j, i: (0, j))],
        out_specs=[cur, cur,
                   pl.BlockSpec((SUBLANES, tc), lambda j, i: (0, j)),
                   pl.BlockSpec((1, tc), lambda j, i: (0, j))],
        out_shape=[_sds((T, C), BF16), _sds((T, C), BF16), _sds((SUBLANES, C), F32), _sds((1, C), F32)],
        compiler_params=_cp(("parallel", "arbitrary")),
    )(gp, gp, gp, up, up, dact, dact, conv_w8, conv_b)


def _cumsum_rows(x):
    n = x.shape[0]
    row = lax.broadcasted_iota(jnp.int32, x.shape, 0)
    s = 1
    while s < n:
        x = x + jnp.where(row >= s, pltpu.roll(x, s, 0), 0.0)
        s *= 2
    return x


def _rcumsum_rows(x):
    n = x.shape[0]
    row = lax.broadcasted_iota(jnp.int32, x.shape, 0)
    s = 1
    while s < n:
        x = x + jnp.where(row < n - s, pltpu.roll(x, n - s, 0), 0.0)
        s *= 2
    return x


def _dot_nt(a, b):
    return lax.dot_general(a.astype(BF16), b.astype(BF16), (((1,), (1,)), ((), ())), preferred_element_type=F32)


def _dot_tn(a, b):
    return lax.dot_general(a.astype(BF16), b.astype(BF16), (((0,), (0,)), ((), ())), preferred_element_type=F32)


def _dot_nn(a, b):
    return jnp.dot(a.astype(BF16), b.astype(BF16), preferred_element_type=F32)


def _hg_gates(hq, hf, lbv):
    sig = _sigmoid(hf)
    f = lbv + (1.0 - lbv) * sig
    return sig, f, jnp.log(f), 1.0 - f, hq * (HG_DK ** -0.5)


def _hg_sel_rows(ref, sp):
    return jnp.concatenate(
        [jnp.broadcast_to(ref[pl.ds(HG_SUB * i + sp, 1), :], (HG_SUB, HG_DK)) for i in range(HG_CHUNK // HG_SUB)], axis=0)


def _hg_masks():
    C = HG_CHUNK
    row = lax.broadcasted_iota(jnp.int32, (C, C), 0)
    col = lax.broadcasted_iota(jnp.int32, (C, C), 1)
    d = col - (row // HG_SUB) * HG_SUB
    tmod = row % HG_SUB
    diag_valid = jnp.logical_and(d >= 0, d <= tmod)
    return row, col, d, diag_valid


def _hg_scores(q, k, b, b_sc, k_sc):
    C, S = HG_CHUNK, HG_SUB
    row, col, d, diag_valid = _hg_masks()
    blocks = [jnp.zeros((S, C), F32)]
    for i in range(1, C // S):
        r = b_sc[pl.ds(S * i - 1, 1), :]
        qi = q[S * i:S * (i + 1)] * jnp.exp(b[S * i:S * (i + 1)] - r)
        kk = k * jnp.exp(jnp.minimum(r - b, 0.0))
        blocks.append(_dot_nt(qi, kk))
    a_off = jnp.where(col < (row // S) * S, jnp.concatenate(blocks, axis=0), 0.0)
    a_d = jnp.zeros((C, C), F32)
    for sp in range(S):
        bs = _hg_sel_rows(b_sc, sp)
        ks = _hg_sel_rows(k_sc, sp)
        e = jnp.exp(jnp.minimum(b - bs, 0.0))
        colv = jnp.sum(q * ks * e, axis=-1, keepdims=True)
        a_d = jnp.where(d == sp, colv, a_d)
    return a_off + jnp.where(diag_valid, a_d, 0.0)


def _hgrn_fwd(hq, hf, hi, lb, *, name):
    T = hq.shape[0]
    C, H, K = HG_CHUNK, HG_HEADS, HG_DK
    NC = T // C

    def body(hq_ref, hf_ref, hi_ref, lb_ref, o_ref, st_ref, s_sc, b_sc, k_sc):
        @pl.when(pl.program_id(1) == 0)
        def _():
            s_sc[...] = jnp.zeros_like(s_sc)

        _, _, g, k, q = _hg_gates(hq_ref[...], hf_ref[...], lb_ref[...])
        v = hi_ref[...]
        b = _cumsum_rows(g)
        b_sc[...] = b
        k_sc[...] = k
        st0 = s_sc[...]
        st_ref[0, 0] = st0
        bc = b_sc[pl.ds(C - 1, 1), :]
        a = _hg_scores(q, k, b, b_sc, k_sc)
        o_ref[...] = _dot_nn(a, v) + _dot_nt(q * jnp.exp(b), st0)
        kb = k * jnp.exp(bc - b)
        s_sc[...] = st0 * jnp.exp(bc) + _dot_tn(v, kb)

    blk = pl.BlockSpec((C, K), lambda h, c: (c, h))
    return pl.pallas_call(
        body, name=name, grid=(H, NC),
        in_specs=[blk, blk, blk, pl.BlockSpec((1, K), lambda h, c: (0, h))],
        out_specs=[blk, pl.BlockSpec((1, 1, K, K), lambda h, c: (c, h, 0, 0))],
        out_shape=[_sds((T, H * K), F32), _sds((NC, H, K, K), F32)],
        scratch_shapes=[pltpu.VMEM((K, K), F32), pltpu.VMEM((C, K), F32), pltpu.VMEM((C, K), F32)],
        compiler_params=_cp(("parallel", "arbitrary")),
    )(hq, hf, hi, lb)


def _hgrn_bwd(hq, hf, hi, lb, states, do, *, name):
    T = hq.shape[0]
    C, H, K, S = HG_CHUNK, HG_HEADS, HG_DK, HG_SUB
    NC = T // C

    def body(hq_ref, hf_ref, hi_ref, lb_ref, st_ref, do_ref, dq_ref, dhf_ref, dv_ref, dlb_ref, ds_sc, b_sc, k_sc):
        @pl.when(pl.program_id(1) == 0)
        def _():
            ds_sc[...] = jnp.zeros_like(ds_sc)
            dlb_ref[...] = jnp.zeros_like(dlb_ref)

        lbv = lb_ref[...]
        sig, f, g, k, q = _hg_gates(hq_ref[...], hf_ref[...], lbv)
        v = hi_ref[...]
        dout = do_ref[...]
        b = _cumsum_rows(g)
        b_sc[...] = b
        k_sc[...] = k
        st0 = st_ref[0, 0]
        dst1 = ds_sc[...]
        bc = b_sc[pl.ds(C - 1, 1), :]
        ebc = jnp.exp(bc)
        eb = jnp.exp(b)
        ekb = jnp.exp(bc - b)
        qt = q * eb
        kb = k * ekb
        row, col, d, diag_valid = _hg_masks()
        a = _hg_scores(q, k, b, b_sc, k_sc)
        dv = _dot_tn(a, dout) + _dot_nt(kb, dst1)
        da = jnp.where(col <= row, _dot_nt(dout, v), 0.0)
        dqt = _dot_nn(dout, st0)
        dkb = _dot_nn(v, dst1)
        ds_sc[...] = _dot_tn(dout, qt) + dst1 * ebc
        dq = dqt * eb
        dk = dkb * ekb
        dq_blocks = [jnp.zeros((S, K), F32)]
        for i in range(1, C // S):
            r = b_sc[pl.ds(S * i - 1, 1), :]
            eq = jnp.exp(b[S * i:S * (i + 1)] - r)
            ek = jnp.exp(jnp.minimum(r - b, 0.0))
            qi = q[S * i:S * (i + 1)] * eq
            kk = k * ek
            dai = jnp.where(col[S * i:S * (i + 1)] < S * i, da[S * i:S * (i + 1)], 0.0)
            dq_blocks.append(_dot_nn(dai, kk) * eq)
            dk = dk + _dot_tn(dai, qi) * ek
        dq = dq + jnp.concatenate(dq_blocks, axis=0)
        same_blk = (row // S == col // S).astype(BF16)
        tmod = (lax.broadcasted_iota(jnp.int32, (C, K), 0)) % S
        for sp in range(S):
            bs = _hg_sel_rows(b_sc, sp)
            ks = _hg_sel_rows(k_sc, sp)
            e = jnp.where(tmod >= sp, jnp.exp(jnp.minimum(b - bs, 0.0)), 0.0)
            dacol = jnp.sum(jnp.where(d == sp, da, 0.0), axis=-1, keepdims=True)
            w = dacol * e
            dq = dq + w * ks
            blk_sum = jnp.dot(same_blk, (w * q).astype(BF16), preferred_element_type=F32)
            dk = dk + jnp.where(tmod == sp, blk_sum, 0.0)
        extra = jnp.sum(dkb * kb, axis=0, keepdims=True) + ebc * jnp.sum(st0 * dst1, axis=0, keepdims=True)
        rowk = lax.broadcasted_iota(jnp.int32, (C, K), 0)
        db = q * dq - k * dk + jnp.where(rowk == C - 1, extra, 0.0)
        dg = _rcumsum_rows(db)
        df = dg / f - dk
        dq_ref[...] = (dq * (K ** -0.5)).astype(dq_ref.dtype)
        dhf_ref[...] = (df * (1.0 - lbv) * sig * (1.0 - sig)).astype(dhf_ref.dtype)
        dv_ref[...] = dv.astype(dv_ref.dtype)
        dlb_ref[...] += jnp.sum(df * (1.0 - sig), axis=0, keepdims=True)

    blk = pl.BlockSpec((C, K), lambda h, c: (NC - 1 - c, h))
    return pl.pallas_call(
        body, name=name, grid=(H, NC),
        in_specs=[blk, blk, blk, pl.BlockSpec((1, K), lambda h, c: (0, h)),
                  pl.BlockSpec((1, 1, K, K), lambda h, c: (NC - 1 - c, h, 0, 0)), blk],
        out_specs=[blk, blk, blk, pl.BlockSpec((1, K), lambda h, c: (0, h))],
        out_shape=[_sds((T, H * K), BF16)] * 3 + [_sds((1, H * K), F32)],
        scratch_shapes=[pltpu.VMEM((K, K), F32), pltpu.VMEM((C, K), F32), pltpu.VMEM((C, K), F32)],
        compiler_params=_cp(("parallel", "arbitrary")),
    )(hq, hf, hi, lb, states, do)


def _att_valid(n):
    R, B = ATT_GROUP * ATT_BLOCK, ATT_BLOCK
    t = lax.broadcasted_iota(jnp.int32, (R, 2 * B), 0) % B
    j = lax.broadcasted_iota(jnp.int32, (R, 2 * B), 1)
    dist = t + B - j
    first_key = jnp.where(n > 0, 0, B)
    return jnp.logical_and(jnp.logical_and(dist >= 0, dist < B), j >= first_key)


def _att_load(cur_ref, prev_ref, ba_ref, kv):
    hd = ATT_HD
    def cols(ref, c0):
        return ref[:, c0:c0 + hd] + ba_ref[:, c0:c0 + hd]
    qs = jnp.concatenate([cols(cur_ref, hd * (ATT_GROUP * kv + g)) for g in range(ATT_GROUP)], axis=0)
    kc = jnp.concatenate([cols(prev_ref, ATT_Q_W + hd * kv), cols(cur_ref, ATT_Q_W + hd * kv)], axis=0)
    vc = jnp.concatenate([cols(prev_ref, ATT_Q_W + ATT_KV_W + hd * kv), cols(cur_ref, ATT_Q_W + ATT_KV_W + hd * kv)], axis=0)
    return qs, kc, vc


def _att_probs(qs, kc, valid, sink_ref, kv):
    scale = 1.0 / math.sqrt(ATT_HD)
    s = jnp.where(valid, _dot_nt(qs, kc) * scale, NEG)
    sink = jnp.concatenate([jnp.full((ATT_BLOCK, 1), sink_ref[0, ATT_GROUP * kv + g], F32) for g in range(ATT_GROUP)], axis=0)
    m = jnp.maximum(jnp.max(s, axis=-1, keepdims=True), sink)
    p = jnp.exp(s - m)
    ps = jnp.exp(sink - m)
    inv = 1.0 / (jnp.sum(p, axis=-1, keepdims=True) + ps)
    return p * inv, ps * inv


def _attn_fwd(att, b_attn, sinks, *, name):
    T = att.shape[0]
    B = ATT_BLOCK
    NB = T // B

    def body(sink_ref, cur_ref, prev_ref, ba_ref, o_ref):
        valid = _att_valid(pl.program_id(0))
        for kv in range(ATT_KV):
            qs, kc, vc = _att_load(cur_ref, prev_ref, ba_ref, kv)
            prob, _ = _att_probs(qs, kc, valid, sink_ref, kv)
            o = _dot_nn(prob, vc)
            for g in range(ATT_GROUP):
                c0 = ATT_HD * (ATT_GROUP * kv + g)
                o_ref[:, c0:c0 + ATT_HD] = o[B * g:B * (g + 1)]

    return pl.pallas_call(
        body, name=name, grid=(NB,),
        in_specs=[pl.BlockSpec(memory_space=pltpu.SMEM),
                  pl.BlockSpec((B, ATT_COLS), lambda n: (n, 0)),
                  pl.BlockSpec((B, ATT_COLS), lambda n: (jnp.maximum(n - 1, 0), 0)),
                  pl.BlockSpec((1, ATT_COLS), lambda n: (0, 0))],
        out_specs=pl.BlockSpec((B, ATT_Q_W), lambda n: (n, 0)),
        out_shape=_sds((T, ATT_Q_W), F32),
        compiler_params=_cp(("parallel",)),
    )(sinks, att, att, b_attn)


def _attn_bwd(att, b_attn, sinks, dmix, *, name):
    T = att.shape[0]
    B, hd = ATT_BLOCK, ATT_HD
    NB = T // B
    scale = 1.0 / math.sqrt(hd)

    def body(sink_ref, cur_ref, prev_ref, ba_ref, do_ref, daq_ref, dakv_ref, dsink_ref, dbq_ref, dbkv_ref,
             carry_sc, cprev_sc, ccur_sc):
        n = pl.program_id(0)

        @pl.when(n == 0)
        def _():
            carry_sc[...] = jnp.zeros_like(carry_sc)
            dsink_ref[...] = jnp.zeros_like(dsink_ref)
            dbq_ref[...] = jnp.zeros_like(dbq_ref)
            dbkv_ref[...] = jnp.zeros_like(dbkv_ref)

        @pl.when(n < NB)
        def _():
            valid = _att_valid(n)
            hrow = lax.broadcasted_iota(jnp.int32, (SUBLANES, 128), 0)
            dsink = jnp.zeros((SUBLANES, 128), F32)
            for kv in range(ATT_KV):
                qs, kc, vc = _att_load(cur_ref, prev_ref, ba_ref, kv)
                prob, psink = _att_probs(qs, kc, valid, sink_ref, kv)
                dout = jnp.concatenate(
                    [do_ref[:, hd * (ATT_GROUP * kv + g):hd * (ATT_GROUP * kv + g + 1)] for g in range(ATT_GROUP)], axis=0)
                dp = _dot_nt(dout, vc)
                delta = jnp.sum(prob * dp, axis=-1, keepdims=True)
                dsc = prob * (dp - delta) * scale
                dq = _dot_nn(dsc, kc)
                dk = _dot_tn(dsc, qs)
                dvv = _dot_tn(prob, dout)
                dsk = psink * delta
                for g in range(ATT_GROUP):
                    h = ATT_GROUP * kv + g
                    daq_ref[:, hd * h:hd * (h + 1)] = dq[B * g:B * (g + 1)].astype(daq_ref.dtype)
                    tot = jnp.sum(dsk[B * g:B * (g + 1)], axis=0, keepdims=True)
                    dsink = dsink - jnp.where(hrow == h, tot, 0.0)
                cprev_sc[:, hd * kv:hd * (kv + 1)] = dk[:B]
                ccur_sc[:, hd * kv:hd * (kv + 1)] = dk[B:]
                cprev_sc[:, ATT_KV_W + hd * kv:ATT_KV_W + hd * (kv + 1)] = dvv[:B]
                ccur_sc[:, ATT_KV_W + hd * kv:ATT_KV_W + hd * (kv + 1)] = dvv[B:]
            dsink_ref[...] += dsink
            dbq_ref[...] += jnp.sum(daq_ref[...].astype(F32), axis=0, keepdims=True)
            done = carry_sc[...] + cprev_sc[...]
            dakv_ref[...] = done.astype(dakv_ref.dtype)
            dbkv_ref[...] += jnp.sum(done.astype(dakv_ref.dtype).astype(F32), axis=0, keepdims=True)
            carry_sc[...] = ccur_sc[...]

        @pl.when(n == NB)
        def _():
            done = carry_sc[...]
            dakv_ref[...] = done.astype(dakv_ref.dtype)
            dbkv_ref[...] += jnp.sum(done.astype(dakv_ref.dtype).astype(F32), axis=0, keepdims=True)

    cl = lambda n: jnp.minimum(n, NB - 1)
    return pl.pallas_call(
        body, name=name, grid=(NB + 1,),
        in_specs=[pl.BlockSpec(memory_space=pltpu.SMEM),
                  pl.BlockSpec((B, ATT_COLS), lambda n: (cl(n), 0)),
                  pl.BlockSpec((B, ATT_COLS), lambda n: (jnp.maximum(cl(n) - 1, 0), 0)),
                  pl.BlockSpec((1, ATT_COLS), lambda n: (0, 0)),
                  pl.BlockSpec((B, ATT_Q_W), lambda n: (cl(n), 1))],
        out_specs=[pl.BlockSpec((B, ATT_Q_W), lambda n: (cl(n), 0)),
                   pl.BlockSpec((B, 2 * ATT_KV_W), lambda n: (jnp.maximum(n - 1, 0), 0)),
                   pl.BlockSpec((SUBLANES, 128), lambda n: (0, 0)),
                   pl.BlockSpec((1, ATT_Q_W), lambda n: (0, 0)),
                   pl.BlockSpec((1, 2 * ATT_KV_W), lambda n: (0, 0))],
        out_shape=[_sds((T, ATT_Q_W), BF16), _sds((T, 2 * ATT_KV_W), BF16), _sds((SUBLANES, 128), F32),
                   _sds((1, ATT_Q_W), F32), _sds((1, 2 * ATT_KV_W), F32)],
        scratch_shapes=[pltpu.VMEM((B, 2 * ATT_KV_W), F32)] * 3,
        compiler_params=_cp(("arbitrary",)),
    )(sinks, att, att, b_attn, dmix)


def _silu_and_grad(x):
    sg = _sigmoid(x)
    return x * sg, sg * (1.0 + x * (1.0 - sg))


def _mix_fwd_fn(o_raw, hg, o_att, hgw):
    outs = []
    for h in range(HG_HEADS):
        sl = slice(HG_DK * h, HG_DK * (h + 1))
        silu, _ = _silu_and_grad(hg[:, sl])
        outs.append(_rms_fwd(o_raw[:, sl], hgw) * silu)
    outs.append(o_att)
    return (jnp.concatenate(outs, axis=1),)


def _mix_bwd_fn(o_raw, hg, dmix, hgw):
    dos, dhgs = [], []
    dw = jnp.zeros((1, HG_DK), F32)
    for h in range(HG_HEADS):
        sl = slice(HG_DK * h, HG_DK * (h + 1))
        silu, dsilu = _silu_and_grad(hg[:, sl])
        dy = dmix[:, sl]
        dhgs.append(dy * _rms_fwd(o_raw[:, sl], hgw) * dsilu)
        dx, dwh = _rms_bwd(o_raw[:, sl], hgw, dy * silu)
        dos.append(dx)
        dw = dw + dwh
    return jnp.concatenate(dos, axis=1), jnp.concatenate(dhgs, axis=1), dw


def _final_fn(h2, tgt, wf):
    d = h2.shape[1]
    err = _rms_fwd(h2, wf) - tgt
    loss_cols = (0.5 / d) * jnp.sum(err * err, axis=0, keepdims=True)
    dh2, dwf = _rms_bwd(h2, wf, err * (1.0 / d))
    return dh2, dh2, loss_cols, dwf


class _NoExchange:
    def __init__(self, weights):
        self.weights = weights

    def start(self):
        return None

    def w_in(self, after):
        return self.weights["w_in_t"]

    def rest(self, after):
        return self.weights

    def ffn_grads(self, gs):
        return None


def _local_step(x, tgt, p, ex):
    T, D = x.shape
    row = lambda n, dt: _sds((T, n), dt)
    acc = lambda n: _sds((1, n), F32)

    (u,) = _rowwise(lambda xv, w: (_rms_fwd(xv, w),), [_full(x)], [p["norm_mix_w"]], [row(D, BF16)], [], name="rms_mix",
                    after=ex.start())
    p = dict(p, w_in_t=ex.w_in(u))
    hq, hf, hi, hg, att = _mm_nt(u, p["w_in_t"], splits=[HG_W] * 4 + [ATT_COLS], out_dtype=F32, name="in_proj")
    o_raw, states = _hgrn_fwd(hq, hf, hi, p["lb"], name="hgrn_fwd")
    o_att = _attn_fwd(att, p["b_attn"], p["sinks"], name="attn_fwd")
    (mix,) = _rowwise(_mix_fwd_fn, [_full(o_raw), _full(hg), _full(o_att)], [p["hg_norm_w"]], [row(D, BF16)], [],
                      name="mix_fwd")
    p = dict(p, **ex.rest(mix))
    h1 = _mm_nn([[mix]], [p["w_out"]], out_dtype=F32, name="out_proj", residual=x)
    (v,) = _rowwise(lambda hv, w: (_rms_fwd(hv, w),), [_full(h1)], [p["norm_ffn_w"]], [row(D, BF16)], [], name="rms_ffn")
    (gp,) = _mm_nt(v, p["w_gate_t"], splits=[D_FF], out_dtype=F32, name="gate_proj")
    (up,) = _mm_nt(v, p["w_up_t"], splits=[D_FF], out_dtype=F32, name="up_proj")
    act = _convact_fwd(gp, up, p["conv_w8"], p["conv_b"], name="convact_fwd")
    h2 = _mm_nn([[act]], [p["w_down"]], out_dtype=F32, name="down_proj", residual=h1)
    dh2, dh2_b, loss_cols, d_final = _rowwise(_final_fn, [_full(h2), _full(tgt)], [p["final_norm_w"]],
                                              [row(D, F32), row(D, BF16)], [acc(D), acc(D)], name="final_loss")

    (dact,) = _mm_nt(dh2_b, p["w_down"], splits=[D_FF], out_dtype=F32, name="d_act")
    g_down = _mm_tn([act], dh2_b, name="g_down")
    dgp, dup, d_conv_w8, d_conv_b = _convact_bwd(gp, up, dact, p["conv_w8"], p["conv_b"], name="convact_bwd")
    dv = _mm_nn([[dgp], [dup]], [p["w_gate_t"], p["w_up_t"]], out_dtype=F32, name="d_v")
    g_gate_t = _mm_tn([dgp], v, name="g_gate")
    g_up_t = _mm_tn([dup], v, name="g_up")
    sent = ex.ffn_grads([g_gate_t, g_up_t, g_down])

    def ffn_norm_bwd(hv, dvv, dh2v, w):
        dx, dw = _rms_bwd(hv, w, dvv)
        dh1v = dx + dh2v
        return dh1v, dh1v, dw

    dh1, dh1_b, d_norm_ffn = _rowwise(ffn_norm_bwd, [_full(h1), _full(dv), _full(dh2)], [p["norm_ffn_w"]],
                                      [row(D, F32), row(D, BF16)], [acc(D)], name="rms_ffn_bwd", after=sent)
    (dmix,) = _mm_nt(dh1_b, p["w_out"], splits=[D], out_dtype=F32, name="d_mix")
    g_out = _mm_tn([mix], dh1_b, name="g_out")
    do_raw, dhg, d_hg_norm = _rowwise(_mix_bwd_fn, [_full(o_raw), _full(hg), (dmix, HG_W, 0, 0)], [p["hg_norm_w"]],
                                      [row(HG_W, F32), row(HG_W, BF16)], [acc(HG_DK)], name="mix_bwd")
    daq, dakv, d_sinks8, d_bq, d_bkv = _attn_bwd(att, p["b_attn"], p["sinks"], dmix, name="attn_bwd")
    dhq, dhf, dhi, d_lb = _hgrn_bwd(hq, hf, hi, p["lb"], states, do_raw, name="hgrn_bwd")
    pieces = [dhq, dhf, dhi, dhg, daq, dakv]
    du = _mm_nn([pieces], [p["w_in_t"]], out_dtype=F32, name="d_u")
    g_in_t = _mm_tn(pieces, u, name="g_in")

    def mix_norm_bwd(xv, duv, dh1v, w):
        dx, dw = _rms_bwd(xv, w, duv)
        return dx + dh1v, dw

    dx, d_norm_mix = _rowwise(mix_norm_bwd, [_full(x), _full(du), _full(dh1)], [p["norm_mix_w"]], [row(D, F32)], [acc(D)],
                              name="rms_mix_bwd")
    grads = dict(g_in_t=g_in_t, g_out=g_out, g_gate_t=g_gate_t, g_up_t=g_up_t, g_down=g_down,
                 norm_mix_w=d_norm_mix, b_attn=jnp.concatenate([d_bq, d_bkv], axis=1), lb=d_lb, hg_norm_w=d_hg_norm,
                 sinks8=d_sinks8, norm_ffn_w=d_norm_ffn, conv_w8=d_conv_w8, conv_b=d_conv_b, final_norm_w=d_final)
    return loss_cols, dx, grads


SLAB = (IN_COLS // N_CHIPS, D_FF // N_CHIPS, D_FF // N_CHIPS, D_FF // N_CHIPS, D_MODEL // N_CHIPS)
N_W = len(SLAB)
PACK_OFF = tuple(sum(SLAB[:i]) for i in range(N_W))
PACK_ROWS = sum(SLAB)
FULL_OFF = tuple(N_CHIPS * o for o in PACK_OFF)
FULL_ROWS = N_CHIPS * PACK_ROWS
HALF = tuple(s // 2 for s in SLAB)
HPACK_OFF = tuple(sum(HALF[:i]) for i in range(N_W))
HPACK_ROWS = sum(HALF)
HFULL_OFF = tuple(N_CHIPS * o for o in HPACK_OFF)
HFULL_ROWS = N_CHIPS * HPACK_ROWS
CHIP_FLIPS = ((1, 0), (0, 1), (1, 1))
N_DEV = 8
BF16_ROWS = 16
ANY = pl.BlockSpec(memory_space=pl.ANY)


def _pos():
    return lax.axis_index("x"), lax.axis_index("y"), lax.axis_index("c")


def _flip(v, f):
    return 1 - v if f else v


def _rcopy(src, dst, ssem, rsem, dev):
    return pltpu.make_async_remote_copy(src_ref=src, dst_ref=dst, send_sem=ssem, recv_sem=rsem, device_id=dev,
                                        device_id_type=pl.DeviceIdType.MESH)


def _rows(ref, start, n, align=SUBLANES):
    if not isinstance(start, int):
        start = pl.multiple_of(start, align)
    return ref.at[pl.ds(start, n), :]


FFN_W = (1, 2, 3)
N_PEER = 1 + len(CHIP_FLIPS)
HBM = pl.BlockSpec(memory_space=pltpu.HBM)
SEM = pl.BlockSpec(memory_space=pltpu.SEMAPHORE)
EFFECT = pltpu.SideEffectType.DATAFLOW_SIDE_EFFECTING
LANES = 128


def _gather_start(pack, cw8):
    D = pack.shape[1]
    lands = [lax.empty((N_CHIPS * SLAB[0], D), pack.dtype), lax.empty((3 * N_CHIPS * SLAB[1], D), pack.dtype),
             lax.empty((N_CHIPS * SLAB[4], D), pack.dtype), lax.empty((N_CHIPS,) + cw8.shape, cw8.dtype)]
    bufs = [pack, cw8] + lands

    def body(pack_ref, cw_ref, l_in, l_ffn, l_out, l_cw, *rest):
        in_send, in_recv, rest_send, rest_recv = rest[:4]
        token = rest[-1]
        x, y, c = _pos()
        q = 2 * x + y
        peers = _gather_peers(x, y, c)
        for k, peer in enumerate(peers):
            _rcopy(_rows(pack_ref, PACK_OFF[0], SLAB[0]), _rows(l_in, q * SLAB[0], SLAB[0], BF16_ROWS),
                   in_send.at[k], in_recv.at[k], peer).start()
        for k, peer in enumerate(peers):
            for j, w in enumerate(FFN_W):
                _rcopy(_rows(pack_ref, PACK_OFF[w], SLAB[w]), _rows(l_ffn, (j * N_CHIPS + q) * SLAB[w], SLAB[w], BF16_ROWS),
                       rest_send.at[k], rest_recv.at[k], peer).start()
            _rcopy(_rows(pack_ref, PACK_OFF[4], SLAB[4]), _rows(l_out, q * SLAB[4], SLAB[4], BF16_ROWS),
                   rest_send.at[N_PEER + k], rest_recv.at[N_PEER + k], peer).start()
            _rcopy(cw_ref, l_cw.at[q], rest_send.at[2 * N_PEER + k], rest_recv.at[2 * N_PEER + k], peer).start()
        token[...] = jnp.zeros_like(token)

    outs = pl.pallas_call(
        body, name="gather_start", in_specs=[HBM] * len(bufs),
        out_specs=[SEM] * 4 + [HBM] * len(bufs) + [pl.BlockSpec(memory_space=pltpu.VMEM)],
        out_shape=[pltpu.SemaphoreType.DMA((N_PEER,)), pltpu.SemaphoreType.DMA((N_PEER,)),
                   pltpu.SemaphoreType.DMA((3 * N_PEER,)), pltpu.SemaphoreType.DMA((3 * N_PEER,))]
        + [pltpu.HBM(b.shape, b.dtype) for b in bufs] + [_sds((SUBLANES, LANES), F32)],
        input_output_aliases={i: 4 + i for i in range(len(bufs))},
        compiler_params=pltpu.CompilerParams(has_side_effects=EFFECT),
    )(*[pltpu.with_memory_space_constraint(b, pltpu.HBM) for b in bufs])
    return dict(in_sems=outs[0:2], rest_sems=outs[2:4], pack=outs[4], cw=outs[5], l_in=outs[6], l_ffn=outs[7],
                l_out=outs[8], l_cw=outs[9], token=outs[10])


def _gather_peers(x, y, c):
    return [(x, y, 1 - c)] + [(_flip(x, fx), _flip(y, fy), c) for fx, fy in CHIP_FLIPS]


def _gather_wait_in(g, after):
    def body(pack_ref, l_in, send, recv, after_ref, pack_out, l_out):
        for k, peer in enumerate(_gather_peers(*_pos())):
            cp = _rcopy(_rows(pack_ref, PACK_OFF[0], SLAB[0]), _rows(l_in, 0, SLAB[0]), send.at[k], recv.at[k], peer)
            cp.wait_send()
            cp.wait_recv()

    return pl.pallas_call(
        body, name="gather_wait_in", in_specs=[HBM, HBM, SEM, SEM, ANY], out_specs=[HBM, HBM],
        out_shape=[pltpu.HBM(g["pack"].shape, g["pack"].dtype), pltpu.HBM(g["l_in"].shape, g["l_in"].dtype)],
        input_output_aliases={0: 0, 1: 1}, compiler_params=pltpu.CompilerParams(has_side_effects=EFFECT),
    )(g["pack"], g["l_in"], *g["in_sems"], after)


def _gather_wait_rest(g, pack, after):
    n_ffn = len(FFN_W) * SLAB[FFN_W[0]]

    def body(pack_ref, cw_ref, l_ffn, l_out, l_cw, send, recv, after_ref, o_ffn, o_out, o_cw):
        for k, peer in enumerate(_gather_peers(*_pos())):
            for cp in (_rcopy(_rows(pack_ref, PACK_OFF[FFN_W[0]], n_ffn), _rows(l_ffn, 0, n_ffn), send.at[k], recv.at[k], peer),
                       _rcopy(_rows(pack_ref, PACK_OFF[4], SLAB[4]), _rows(l_out, 0, SLAB[4]),
                              send.at[N_PEER + k], recv.at[N_PEER + k], peer),
                       _rcopy(cw_ref, l_cw.at[0], send.at[2 * N_PEER + k], recv.at[2 * N_PEER + k], peer)):
                cp.wait_send()
                cp.wait_recv()

    ins = [pack, g["cw"], g["l_ffn"], g["l_out"], g["l_cw"]]
    return pl.pallas_call(
        body, name="gather_wait_rest", in_specs=[HBM] * 5 + [SEM, SEM, ANY], out_specs=[HBM] * 3,
        out_shape=[pltpu.HBM(b.shape, b.dtype) for b in ins[2:]],
        input_output_aliases={2: 0, 3: 1, 4: 2}, compiler_params=pltpu.CompilerParams(has_side_effects=EFFECT),
    )(*ins, *g["rest_sems"], after)


def _exchange_halves(ws, gs, small, *, name):
    D = gs[0].shape[1]
    n = len(ws)
    has_small = small is not None

    def body(*refs):
        g = refs[:n]
        t = refs[n + has_small:2 * n + has_small]
        sems = refs[2 * n + 2 * has_small:]
        d2d_send, d2d_recv = sems[0], sems[1]
        x, y, c = _pos()
        sib = (x, y, 1 - c)
        drains = []
        for i, w in enumerate(ws):
            h = HALF[w]
            for qq in range(N_CHIPS):
                _rcopy(_rows(g[i], qq * SLAB[w] + (1 - c) * h, h), _rows(t[i], qq * h, h),
                       d2d_send.at[i], d2d_recv.at[i], sib).start()
            drains.append(_rcopy(t[i], t[i], d2d_send.at[i], d2d_recv.at[i], sib))
        if has_small:
            small_ref, sall_ref = refs[n], refs[2 * n + 1]
            sm_send, sm_recv, loc_sem = sems[2], sems[3], sems[4]
            me = 4 * x + 2 * y + c
            own_small = pltpu.make_async_copy(small_ref, sall_ref.at[me], loc_sem)
            own_small.start()
            for f in range(1, N_DEV):
                peer = (_flip(x, f & 4), _flip(y, f & 2), _flip(c, f & 1))
                cp = _rcopy(small_ref, sall_ref.at[me], sm_send.at[f - 1], sm_recv.at[f - 1], peer)
                cp.start()
                drains.append(cp)
        for d in drains:
            d.wait_recv()
        for d in drains:
            d.wait_send()
        if has_small:
            own_small.wait()

    out_shape = [_sds((N_CHIPS * HALF[w], D), F32) for w in ws]
    scratch = [pltpu.SemaphoreType.DMA((n,)), pltpu.SemaphoreType.DMA((n,))]
    if has_small:
        out_shape.append(_sds((N_DEV,) + small.shape, F32))
        scratch += [pltpu.SemaphoreType.DMA((N_DEV - 1,)), pltpu.SemaphoreType.DMA((N_DEV - 1,)), pltpu.SemaphoreType.DMA]
    return pl.pallas_call(
        body, name=name, in_specs=[ANY] * (n + has_small), out_specs=[ANY] * (n + has_small),
        out_shape=out_shape, scratch_shapes=scratch,
    )(*gs, *([small] if has_small else []))


REDUCE_SPLIT = 2


def _chip_partial(ws, gs, theirs, *, name):
    D = gs[0].shape[1]
    n = len(ws)

    def body(*refs):
        for i in range(n):
            refs[2 * n + i][...] = refs[i][...] + refs[n + i][...]

    blk = [HALF[w] // REDUCE_SPLIT for w in ws]
    mine = [pl.BlockSpec((b, D), lambda qq, j: ((2 * qq + lax.axis_index("c")) * REDUCE_SPLIT + j, 0)) for b in blk]
    flat = [pl.BlockSpec((b, D), lambda qq, j: (qq * REDUCE_SPLIT + j, 0)) for b in blk]
    return pl.pallas_call(
        body, name=name, grid=(N_CHIPS, REDUCE_SPLIT), in_specs=mine + flat, out_specs=flat,
        out_shape=[_sds((N_CHIPS * HALF[w], D), F32) for w in ws],
        compiler_params=_cp(("parallel", "parallel")),
    )(*gs, *theirs)


def _partial_copies(ws, part, got, send_sems, recv_sems):
    x, y, c = _pos()
    cps = []
    for k, (fx, fy) in enumerate(CHIP_FLIPS):
        peer = (_flip(x, fx), _flip(y, fy), c)
        qp = 2 * _flip(x, fx) + _flip(y, fy)
        for i, w in enumerate(ws):
            cps.append(_rcopy(_rows(part[i], qp * HALF[w], HALF[w]), _rows(got[i], k * HALF[w], HALF[w]),
                              send_sems.at[len(ws) * k + i], recv_sems.at[len(ws) * k + i], peer))
    return cps


def _send_chip_partials(ws, parts, *, name):
    D = parts[0].shape[1]
    n = len(ws)

    def body(*refs):
        cps = _partial_copies(ws, refs[:n], refs[n:2 * n], refs[2 * n], refs[2 * n + 1])
        for cp in cps:
            cp.start()
        for cp in cps:
            cp.wait_recv()
        for cp in cps:
            cp.wait_send()

    return pl.pallas_call(
        body, name=name, in_specs=[ANY] * n, out_specs=[ANY] * n,
        out_shape=[_sds((len(CHIP_FLIPS) * HALF[w], D), F32) for w in ws],
        scratch_shapes=[pltpu.SemaphoreType.DMA((len(CHIP_FLIPS) * n,)), pltpu.SemaphoreType.DMA((len(CHIP_FLIPS) * n,))],
    )(*parts)


def _send_start(ws, parts, *, name):
    D = parts[0].shape[1]
    n = len(ws)
    bufs = list(parts) + [lax.empty((len(CHIP_FLIPS) * HALF[w], D), F32) for w in ws]

    def body(*refs):
        send_sems, recv_sems = refs[2 * n], refs[2 * n + 1]
        for cp in _partial_copies(ws, refs[:n], refs[n:2 * n], send_sems, recv_sems):
            cp.start()
        refs[-1][...] = jnp.zeros_like(refs[-1])

    outs = pl.pallas_call(
        body, name=name, in_specs=[HBM] * (2 * n),
        out_specs=[SEM, SEM] + [HBM] * (2 * n) + [pl.BlockSpec(memory_space=pltpu.VMEM)],
        out_shape=[pltpu.SemaphoreType.DMA((len(CHIP_FLIPS) * n,)), pltpu.SemaphoreType.DMA((len(CHIP_FLIPS) * n,))]
        + [pltpu.HBM(b.shape, b.dtype) for b in bufs] + [_sds((SUBLANES, LANES), F32)],
        input_output_aliases={i: 2 + i for i in range(2 * n)},
        compiler_params=pltpu.CompilerParams(has_side_effects=EFFECT),
    )(*[pltpu.with_memory_space_constraint(b, pltpu.HBM) for b in bufs])
    return dict(sems=outs[0:2], parts=outs[2:2 + n], got=outs[2 + n:2 + 2 * n], token=outs[-1])


def _send_wait(ws, s, after, *, name):
    n = len(ws)

    def body(*refs):
        for cp in _partial_copies(ws, refs[:n], refs[n:2 * n], refs[2 * n], refs[2 * n + 1]):
            cp.wait_send()
            cp.wait_recv()

    bufs = list(s["parts"]) + list(s["got"])
    outs = pl.pallas_call(
        body, name=name, in_specs=[HBM] * (2 * n) + [SEM, SEM, ANY], out_specs=[HBM] * (2 * n),
        out_shape=[pltpu.HBM(b.shape, b.dtype) for b in bufs],
        input_output_aliases={i: i for i in range(2 * n)},
        compiler_params=pltpu.CompilerParams(has_side_effects=EFFECT),
    )(*bufs, *s["sems"], after)
    return outs[:n], outs[n:]


def _chip_reduce(parts, got):
    D = parts[0].shape[1]
    nk = len(CHIP_FLIPS)

    def body(*refs):
        outs = refs[(1 + nk) * N_W:]
        for w in range(N_W):
            acc = refs[w][...]
            for k in range(nk):
                acc = acc + refs[N_W * (1 + k) + w][...]
            outs[w][...] = acc

    blk = [h // REDUCE_SPLIT for h in HALF]

    def q_idx(j):
        return (2 * lax.axis_index("x") + lax.axis_index("y")) * REDUCE_SPLIT + j

    in_specs = [pl.BlockSpec((b, D), lambda j: (q_idx(j), 0)) for b in blk]
    for k in range(nk):
        in_specs += [pl.BlockSpec((b, D), functools.partial(lambda j, k: (k * REDUCE_SPLIT + j, 0), k=k)) for b in blk]
    out_specs = [pl.BlockSpec((b, D), lambda j: (lax.axis_index("c") * REDUCE_SPLIT + j, 0)) for b in blk]
    return pl.pallas_call(
        body, name="chip_reduce", grid=(REDUCE_SPLIT,), in_specs=in_specs, out_specs=out_specs,
        out_shape=[_sds((s, D), F32) for s in SLAB],
        compiler_params=_cp(("parallel",)),
    )(*parts, *[g for _ in range(nk) for g in got])


def _exchange_reduced(shards):
    def body(i0, i1, i2, i3, i4, o0, o1, o2, o3, o4, send_sems, recv_sems):
        ins = (i0, i1, i2, i3, i4)
        outs = (o0, o1, o2, o3, o4)
        x, y, c = _pos()
        sib = (x, y, 1 - c)
        cps = []
        for w in range(N_W):
            cp = _rcopy(_rows(ins[w], c * HALF[w], HALF[w]), _rows(outs[w], c * HALF[w], HALF[w]),
                        send_sems.at[w], recv_sems.at[w], sib)
            cp.start()
            cps.append(cp)
        for cp in cps:
            cp.wait_recv()
        for cp in cps:
            cp.wait_send()

    return pl.pallas_call(
        body, name="exchange_reduced", in_specs=[ANY] * N_W, out_specs=[ANY] * N_W,
        out_shape=[_sds(s.shape, s.dtype) for s in shards], input_output_aliases={w: w for w in range(N_W)},
        scratch_shapes=[pltpu.SemaphoreType.DMA((N_W,)), pltpu.SemaphoreType.DMA((N_W,))],
    )(*shards)


def _adamw_fn(w, g, m, v):
    m2 = ADAM_B1 * m + (1.0 - ADAM_B1) * g
    v2 = ADAM_B2 * v + (1.0 - ADAM_B2) * (g * g)
    m_hat = m2 / (1.0 - ADAM_B1 ** ADAM_STEP)
    v_hat = v2 / (1.0 - ADAM_B2 ** ADAM_STEP)
    return -ADAM_LR * (m_hat / (jnp.sqrt(v_hat) + ADAM_EPS) + ADAM_WD * w), m2, v2


def _adamw(w, g, m, v, *, name):
    shp = _sds(w.shape, F32)
    rows = w.shape[0]
    tm = max(t for t in range(SUBLANES, 512 + 1, SUBLANES) if rows % t == 0)
    return _rowwise(_adamw_fn, [_full(w), _full(g), _full(m), _full(v)], [], [shp] * 3, [], name=name, tm=tm)


SMALL_SEGS = (("loss", 8), ("norm_mix_w", 8), ("b_attn", 8), ("lb_logits", 8), ("hg_norm_w", 8), ("sinks", 8),
              ("norm_ffn_w", 8), ("conv_w", 72), ("conv_b", 24), ("final_norm_w", 8))
SMALL_OFF = {n: sum(r for _, r in SMALL_SEGS[:i]) for i, (n, _) in enumerate(SMALL_SEGS)}
SMALL_ROWS = sum(r for _, r in SMALL_SEGS)
LANES = 128


def _pack_small(parts):
    segs = []
    for n, r in SMALL_SEGS:
        a = parts.get(n)
        flat = jnp.zeros((0,), F32) if a is None else a.reshape(-1).astype(F32)
        segs.append(jnp.pad(flat, (0, r * LANES - flat.shape[0])).reshape(r, LANES))
    return jnp.concatenate(segs, axis=0)


def _unpack_small(pack, n, shape):
    size = math.prod(shape)
    r0 = SMALL_OFF[n]
    return pack[r0:r0 + dict(SMALL_SEGS)[n]].reshape(-1)[:size].reshape(shape)


def _small_update(sall, wp, mp, vp):
    R = SMALL_ROWS
    r_lb = SMALL_OFF["lb_logits"]

    def body(s_ref, w_ref, m_ref, v_ref, g_ref, d_ref, m2_ref, v2_ref, loss_ref):
        g = s_ref[0]
        for i in range(1, N_DEV):
            g = g + s_ref[i]
        tot = jnp.sum(jnp.sum(g[0:8], axis=1, keepdims=True), axis=0, keepdims=True)
        loss_ref[...] = jnp.broadcast_to(tot, loss_ref.shape)
        lg = w_ref[r_lb:r_lb + 8, :]
        p0 = _sigmoid(lg - pltpu.roll(lg, 4, 0))
        d = g[r_lb:r_lb + 8]
        d = d + pltpu.roll(d, 4, 0)
        sign = jnp.where(lax.broadcasted_iota(jnp.int32, d.shape, 0) < 4, 1.0, -1.0)
        g = jnp.concatenate([g[:r_lb], sign * d * p0 * (1.0 - p0), g[r_lb + 8:]], axis=0)
        g_ref[...] = g
        d_ref[...], m2_ref[...], v2_ref[...] = _adamw_fn(w_ref[...], g, m_ref[...], v_ref[...])

    full = pl.BlockSpec((R, LANES), lambda: (0, 0))
    return pl.pallas_call(
        body, name="small_update",
        in_specs=[pl.BlockSpec((N_DEV, R, LANES), lambda: (0, 0, 0)), full, full, full],
        out_specs=[full, full, full, full, pl.BlockSpec((8, LANES), lambda: (0, 0))],
        out_shape=[_sds((R, LANES), F32)] * 4 + [_sds((8, LANES), F32)],
        compiler_params=_cp(),
    )(sall, wp, mp, vp)


def _lb_fwd(lb_logits):
    n = lb_logits.shape[1]

    def body(l_ref, o_ref):
        o_ref[...] = _sigmoid(l_ref[0:1, :] - l_ref[1:2, :])

    return pl.pallas_call(body, name="lb_fwd", out_shape=_sds((1, n), F32), compiler_params=_cp())(lb_logits)


class _MeshExchange:
    def __init__(self, pack, cw8):
        self.gather = _gather_start(pack, cw8)
        self.sent = None
        self.conv_w8 = None

    def start(self):
        return self.gather["token"]

    def w_in(self, after):
        self.pack, l_in = _gather_wait_in(self.gather, after)
        return (l_in, N_CHIPS * SLAB[0], 0)

    def rest(self, after):
        l_ffn, l_out, l_cw = _gather_wait_rest(self.gather, self.pack, after)
        self.conv_w8 = jnp.concatenate([l_cw[i] for i in range(N_CHIPS)], axis=1)
        rows = N_CHIPS * SLAB[FFN_W[0]]
        return dict(w_gate_t=(l_ffn, rows, 0), w_up_t=(l_ffn, rows, 1), w_down=(l_ffn, rows, 2),
                    w_out=(l_out, N_CHIPS * SLAB[4], 0), conv_w8=self.conv_w8)

    def ffn_grads(self, gs):
        theirs = _exchange_halves(FFN_W, gs, None, name="exchange_halves_ffn")
        parts = _chip_partial(FFN_W, gs, theirs, name="chip_partial_ffn")
        self.sent = _send_start(FFN_W, parts, name="send_ffn_start")
        return self.sent["token"]


def kernel(x, norm_mix_w, w_in, b_attn, lb_logits, hg_norm_w, sinks, w_out, norm_ffn_w, w_gate, w_up, conv_w, conv_b, w_down, final_norm_w, loss_target, m_norm_mix_w, m_w_in, m_b_attn, m_lb_logits, m_hg_norm_w, m_sinks, m_w_out, m_norm_ffn_w, m_w_gate, m_w_up, m_conv_w, m_conv_b, m_w_down, m_final_norm_w, v_norm_mix_w, v_w_in, v_b_attn, v_lb_logits, v_hg_norm_w, v_sinks, v_w_out, v_norm_ffn_w, v_w_gate, v_w_up, v_conv_w, v_conv_b, v_w_down, v_final_norm_w):
    D = D_MODEL
    q = 2 * lax.axis_index("x") + lax.axis_index("y")
    ccols = D_FF // N_CHIPS

    pack = jnp.concatenate([w_in[0].T, w_gate[0].T, w_up[0].T, w_down[0], w_out[0]], axis=0).astype(BF16)
    cw8 = jnp.concatenate([conv_w[0], jnp.zeros((SUBLANES - 3, ccols), F32)], axis=0)
    ex = _MeshExchange(pack, cw8)
    p = dict(norm_mix_w=norm_mix_w, b_attn=b_attn, lb=_lb_fwd(lb_logits), hg_norm_w=hg_norm_w, sinks=sinks,
             norm_ffn_w=norm_ffn_w, conv_b=conv_b, final_norm_w=final_norm_w.reshape(1, D))
    loss_cols, dx, g = _local_step(x[0], loss_target[0], p, ex)
    conv_w8 = ex.conv_w8

    small = _pack_small(dict(loss=loss_cols, norm_mix_w=g["norm_mix_w"], b_attn=g["b_attn"], lb_logits=g["lb"],
                             hg_norm_w=g["hg_norm_w"], sinks=g["sinks8"], norm_ffn_w=g["norm_ffn_w"],
                             conv_w=g["conv_w8"][:3], conv_b=g["conv_b"], final_norm_w=g["final_norm_w"]))
    parts_ffn, got_ffn = _send_wait(FFN_W, ex.sent, dx, name="send_ffn_wait")
    late = (0, 4)
    gs = [g["g_in_t"], g["g_out"]]
    *theirs, sall = _exchange_halves(late, gs, small, name="exchange_halves_late")
    parts_late = _chip_partial(late, gs, theirs, name="chip_partial_late")
    got_late = _send_chip_partials(late, parts_late, name="send_late")
    parts = [parts_late[0], *parts_ffn, parts_late[1]]
    got = [got_late[0], *got_ffn, got_late[1]]
    shards = _exchange_reduced(_chip_reduce(parts, got))
    g_in, g_gate, g_up, g_down, g_out = shards[0].T, shards[1].T, shards[2].T, shards[3], shards[4]
    big = {}
    for n, gw, w, m, v in (("w_in", g_in, w_in, m_w_in, v_w_in), ("w_out", g_out, w_out, m_w_out, v_w_out),
                           ("w_gate", g_gate, w_gate, m_w_gate, v_w_gate), ("w_up", g_up, w_up, m_w_up, v_w_up),
                           ("w_down", g_down, w_down, m_w_down, v_w_down)):
        d_, m_, v_ = _adamw(w[0], gw, m[0], v[0], name="adamw_" + n)
        big[n] = (gw[None], d_[None], m_[None], v_[None])

    def place(a):
        return lax.dynamic_update_slice(jnp.zeros((3, D_FF), F32), a[0], (0, q * ccols))

    def small_pack(ws, cw):
        nm, ba, lbl, hg, sk, nf, cb, fn = ws
        return _pack_small(dict(norm_mix_w=nm, b_attn=ba, lb_logits=lbl, hg_norm_w=hg,
                                sinks=jnp.broadcast_to(sk.reshape(ATT_HEADS, 1), (ATT_HEADS, LANES)), norm_ffn_w=nf,
                                conv_w=cw, conv_b=cb, final_norm_w=fn))

    wp = small_pack((norm_mix_w, b_attn, lb_logits, hg_norm_w, sinks, norm_ffn_w, conv_b, final_norm_w), conv_w8[:3])
    mp = small_pack((m_norm_mix_w, m_b_attn, m_lb_logits, m_hg_norm_w, m_sinks, m_norm_ffn_w, m_conv_b, m_final_norm_w),
                    place(m_conv_w))
    vp = small_pack((v_norm_mix_w, v_b_attn, v_lb_logits, v_hg_norm_w, v_sinks, v_norm_ffn_w, v_conv_b, v_final_norm_w),
                    place(v_conv_w))
    outs = _small_update(sall, wp, mp, vp)
    loss = outs[4][0, 0]

    def small_out(pk, n, ref):
        if n == "sinks":
            return pk[SMALL_OFF[n]:SMALL_OFF[n] + ATT_HEADS, 0].reshape(ref.shape)
        if n == "conv_w":
            full = _unpack_small(pk, n, (3, D_FF))
            return lax.dynamic_slice(full, (0, q * ccols), (3, ccols))[None]
        return _unpack_small(pk, n, ref.shape)

    refs = dict(norm_mix_w=norm_mix_w, b_attn=b_attn, lb_logits=lb_logits, hg_norm_w=hg_norm_w, sinks=sinks,
                norm_ffn_w=norm_ffn_w, conv_w=conv_w, conv_b=conv_b, final_norm_w=final_norm_w)
    order = ("norm_mix_w", "w_in", "b_attn", "lb_logits", "hg_norm_w", "sinks", "w_out", "norm_ffn_w", "w_gate", "w_up",
             "conv_w", "conv_b", "w_down", "final_norm_w")
    res = [loss, dx[None]]
    for k in range(4):
        for n in order:
            res.append(big[n][k] if n in big else small_out(outs[k], n, refs[n]))
    return tuple(res)
```

```python
import functools
import math

import jax
import jax.numpy as jnp
from jax import lax
from jax.experimental import pallas as pl
from jax.experimental.pallas import tpu as pltpu

F32 = jnp.float32
BF16 = jnp.bfloat16

D_MODEL = 1024
HG_HEADS = 4
HG_DK = 128
HG_W = HG_HEADS * HG_DK
HG_CHUNK = 64
HG_SUB = 16
ATT_HEADS = 8
ATT_KV = 2
ATT_GROUP = ATT_HEADS // ATT_KV
ATT_HD = 64
ATT_BLOCK = 128
ATT_Q_W = ATT_HEADS * ATT_HD
ATT_KV_W = ATT_KV * ATT_HD
ATT_COLS = ATT_Q_W + 2 * ATT_KV_W
IN_COLS = 4 * HG_W + ATT_COLS
D_FF = 2816
EPS = 1e-6
ADAM_LR, ADAM_B1, ADAM_B2, ADAM_EPS, ADAM_WD, ADAM_STEP = 0.001, 0.9, 0.999, 1e-08, 0.01, 10
NEG = -1e30

V7X_VMEM_BYTES = 64 * 1024 * 1024
VMEM_LIMIT = 48 * 1024 * 1024
SUBLANES = 8

N_CHIPS = 4


def _cp(sem=None, **kw):
    return pltpu.CompilerParams(dimension_semantics=sem, vmem_limit_bytes=VMEM_LIMIT, **kw)


def _sds(shape, dtype):
    return jax.ShapeDtypeStruct(shape, dtype)


def _wspec(w):
    arr, rows, blk = w
    return pl.BlockSpec((rows, arr.shape[1]), lambda i: (blk, 0))


def _mm_nt(a, w, *, splits, out_dtype, name, residual=None, tm=512):
    M, K = a.shape
    N = w[1]
    tm = min(tm, M)
    assert sum(splits) == N and M % tm == 0
    offs = [sum(splits[:i]) for i in range(len(splits))]

    def body(*refs):
        a_ref, w_ref = refs[0], refs[1]
        outs = refs[2 + (residual is not None):]
        acc = lax.dot_general(a_ref[...], w_ref[...], (((1,), (1,)), ((), ())), preferred_element_type=F32)
        if residual is not None:
            acc = acc + refs[2][...]
        for o_ref, c0, n in zip(outs, offs, splits):
            o_ref[...] = acc[:, c0:c0 + n].astype(out_dtype)

    in_specs = [pl.BlockSpec((tm, K), lambda i: (i, 0)), _wspec(w)]
    args = [a, w[0]]
    if residual is not None:
        assert len(splits) == 1
        in_specs.append(pl.BlockSpec((tm, N), lambda i: (i, 0)))
        args.append(residual)
    outs = pl.pallas_call(
        body, name=name, grid=(M // tm,), in_specs=in_specs,
        out_specs=[pl.BlockSpec((tm, n), lambda i: (i, 0)) for n in splits],
        out_shape=[_sds((M, n), out_dtype) for n in splits],
        compiler_params=_cp(("parallel",)),
    )(*args)
    return outs


def _mm_nn(pieces, ws, *, out_dtype, name, residual=None, tm=512):
    M = pieces[0][0].shape[0]
    K = ws[0][0].shape[1]
    tm = min(tm, M)
    flat = [p for grp in pieces for p in grp]
    n_p = len(flat)

    def body(*refs):
        p_refs = refs[:n_p]
        w_refs = refs[n_p:n_p + len(ws)]
        o_ref = refs[-1]
        acc = None if residual is None else refs[n_p + len(ws)][...]
        k = 0
        for gi, grp in enumerate(pieces):
            c0 = 0
            for p in grp:
                n = p.shape[1]
                t = jnp.dot(p_refs[k][...], w_refs[gi][c0:c0 + n, :], preferred_element_type=F32)
                acc = t if acc is None else acc + t
                c0 += n
                k += 1
        o_ref[...] = acc.astype(out_dtype)

    in_specs = [pl.BlockSpec((tm, p.shape[1]), lambda i: (i, 0)) for p in flat]
    in_specs += [_wspec(w) for w in ws]
    args = [*flat, *[w[0] for w in ws]]
    if residual is not None:
        in_specs.append(pl.BlockSpec((tm, K), lambda i: (i, 0)))
        args.append(residual)
    return pl.pallas_call(
        body, name=name, grid=(M // tm,), in_specs=in_specs,
        out_specs=pl.BlockSpec((tm, K), lambda i: (i, 0)),
        out_shape=_sds((M, K), out_dtype),
        compiler_params=_cp(("parallel",)),
    )(*args)


def _mm_tn(pieces, x, *, name, tt=512):
    M, K = x.shape
    tt = min(tt, M)
    ns = [p.shape[1] for p in pieces]
    offs = [sum(ns[:i]) for i in range(len(ns))]
    N = sum(ns)
    n_p = len(pieces)

    def body(*refs):
        p_refs = refs[:n_p]
        x_ref = refs[n_p]
        o_ref = refs[n_p + 1]

        @pl.when(pl.program_id(0) == 0)
        def _():
            o_ref[...] = jnp.zeros_like(o_ref)

        xv = x_ref[...]
        for p_ref, c0, n in zip(p_refs, offs, ns):
            o_ref[c0:c0 + n, :] += lax.dot_general(p_ref[...], xv, (((0,), (0,)), ((), ())),
                                                    preferred_element_type=F32)

    in_specs = [pl.BlockSpec((tt, n), lambda i: (i, 0)) for n in ns]
    in_specs.append(pl.BlockSpec((tt, K), lambda i: (i, 0)))
    return pl.pallas_call(
        body, name=name, grid=(M // tt,), in_specs=in_specs,
        out_specs=pl.BlockSpec((N, K), lambda i: (0, 0)),
        out_shape=_sds((N, K), F32),
        compiler_params=_cp(("arbitrary",)),
    )(*pieces, x)


def _rms_fwd(xf, w):
    inv = lax.rsqrt(jnp.mean(xf * xf, axis=-1, keepdims=True) + EPS)
    return xf * inv * w


def _rms_bwd(xf, w, dy):
    inv = lax.rsqrt(jnp.mean(xf * xf, axis=-1, keepdims=True) + EPS)
    xhat = xf * inv
    dxhat = dy * w
    dx = inv * (dxhat - xhat * jnp.mean(dxhat * xhat, axis=-1, keepdims=True))
    dw = jnp.sum(dy * xhat, axis=0, keepdims=True)
    return dx, dw


def _sigmoid(x):
    return 1.0 / (1.0 + jnp.exp(-x))


def _rowwise(fn, row_ins, bc_ins, row_outs, acc_outs, *, name, tm=256, after=None):
    M = row_outs[0].shape[0] if row_outs else row_ins[0][0].shape[0]
    assert M % tm == 0 and tm % SUBLANES == 0, (name, M, tm)
    n_r, n_b, n_o, n_a = len(row_ins), len(bc_ins), len(row_outs), len(acc_outs)
    n_after = 0 if after is None else 1

    def body(*refs):
        refs = refs[n_after:]
        ins = [r[...] for r in refs[:n_r + n_b]]
        o_refs = refs[n_r + n_b:n_r + n_b + n_o]
        a_refs = refs[n_r + n_b + n_o:]
        res = fn(*ins)
        for o_ref, val in zip(o_refs, res[:n_o]):
            o_ref[...] = val.astype(o_ref.dtype)
        if n_a:
            @pl.when(pl.program_id(0) == 0)
            def _():
                for a_ref in a_refs:
                    a_ref[...] = jnp.zeros_like(a_ref)
            for a_ref, val in zip(a_refs, res[n_o:]):
                a_ref[...] += val

    in_specs = [pl.BlockSpec((tm, cw), functools.partial(lambda i, cb, r0: (i + r0, cb), cb=cb, r0=r0))
                for (_, cw, cb, r0) in row_ins]
    in_specs += [pl.BlockSpec(b.shape, lambda i: (0, 0)) for b in bc_ins]
    out_specs = [pl.BlockSpec((tm, s.shape[1]), lambda i: (i, 0)) for s in row_outs]
    out_specs += [pl.BlockSpec(s.shape, lambda i: (0, 0)) for s in acc_outs]
    if n_after:
        in_specs = [pl.BlockSpec(memory_space=pl.ANY)] + in_specs
    return pl.pallas_call(
        body, name=name, grid=(M // tm,), in_specs=in_specs, out_specs=out_specs,
        out_shape=list(row_outs) + list(acc_outs),
        compiler_params=_cp(("arbitrary",) if n_a else ("parallel",)),
    )(*([after] if n_after else []), *[r[0] for r in row_ins], *bc_ins)


def _full(a, first_row_block=0):
    return (a, a.shape[1], 0, first_row_block)


def _conv_rows(ext, w_ref_val, lo):
    s1 = pltpu.roll(ext, 1, 0)
    s2 = pltpu.roll(ext, 2, 0)
    y = w_ref_val[0:1, :] * s2 + w_ref_val[1:2, :] * s1 + w_ref_val[2:3, :] * ext
    return y[SUBLANES:, :]


def _convact_fwd(gp, up, conv_w8, conv_b, *, name, tr=128, tc=1408):
    T, C = gp.shape
    tr = min(tr, T)
    hb = tr // SUBLANES

    def body(gp_ref, gph_ref, up_ref, w_ref, b_ref, act_ref):
        i = pl.program_id(1)
        halo = jnp.where(i > 0, gph_ref[...], 0.0)
        ext = jnp.concatenate([halo, gp_ref[...]], axis=0)
        gate = _conv_rows(ext, w_ref[...], 0) + b_ref[...]
        act_ref[...] = (gate * _sigmoid(gate) * up_ref[...]).astype(act_ref.dtype)

    return pl.pallas_call(
        body, name=name, grid=(C // tc, T // tr),
        in_specs=[pl.BlockSpec((tr, tc), lambda j, i: (i, j)),
                  pl.BlockSpec((SUBLANES, tc), lambda j, i: (jnp.maximum(i * hb - 1, 0), j)),
                  pl.BlockSpec((tr, tc), lambda j, i: (i, j)),
                  pl.BlockSpec((SUBLANES, tc), lambda j, i: (0, j)),
                  pl.BlockSpec((1, tc), lambda j, i: (0, j))],
        out_specs=pl.BlockSpec((tr, tc), lambda j, i: (i, j)),
        out_shape=_sds((T, C), BF16),
        compiler_params=_cp(("parallel", "parallel")),
    )(gp, gp, up, conv_w8, conv_b)


def _convact_bwd(gp, up, dact, conv_w8, conv_b, *, name, tr=128, tc=1408):
    T, C = gp.shape
    tr = min(tr, T)
    hb = tr // SUBLANES
    nr = T // tr

    def body(gp_ref, gpp_ref, gpn_ref, up_ref, upn_ref, da_ref, dan_ref, w_ref, b_ref,
             dgp_ref, dup_ref, dw_ref, db_ref):
        i = pl.program_id(1)
        w = w_ref[...]
        prev = jnp.where(i > 0, gpp_ref[...], 0.0)
        last = i == nr - 1
        gp_ext = jnp.concatenate([prev, gp_ref[...], gpn_ref[...]], axis=0)
        gate = _conv_rows(gp_ext, w, 0) + b_ref[...]
        up_e = jnp.concatenate([up_ref[...], upn_ref[...]], axis=0)
        da_e = jnp.concatenate([da_ref[...], dan_ref[...]], axis=0)
        row = lax.broadcasted_iota(jnp.int32, gate.shape, 0)
        valid = jnp.logical_or(row < tr, jnp.logical_not(last))
        sg = _sigmoid(gate)
        silu = gate * sg
        dgate = jnp.where(valid, da_e * up_e * (sg * (1.0 + gate * (1.0 - sg))), 0.0)
        dup_ref[...] = (da_e[:tr] * silu[:tr]).astype(dup_ref.dtype)
        n = tr + SUBLANES
        g1 = pltpu.roll(dgate, n - 1, 0)
        g2 = pltpu.roll(dgate, n - 2, 0)
        dgp = w[2:3, :] * dgate + w[1:2, :] * g1 + w[0:1, :] * g2
        dgp_ref[...] = dgp[:tr].astype(dgp_ref.dtype)
        gpc = gp_ref[...]
        dw0 = jnp.sum(gpc * g2[:tr], axis=0, keepdims=True)
        dw1 = jnp.sum(gpc * g1[:tr], axis=0, keepdims=True)
        dw2 = jnp.sum(gpc * dgate[:tr], axis=0, keepdims=True)
        dbv = jnp.sum(dgate[:tr], axis=0, keepdims=True)
        z = jnp.zeros((SUBLANES - 3, gpc.shape[1]), F32)

        @pl.when(i == 0)
        def _():
            dw_ref[...] = jnp.zeros_like(dw_ref)
            db_ref[...] = jnp.zeros_like(db_ref)

        dw_ref[...] += jnp.concatenate([dw0, dw1, dw2, z], axis=0)
        db_ref[...] += dbv

    cur = pl.BlockSpec((tr, tc), lambda j, i: (i, j))
    prv = pl.BlockSpec((SUBLANES, tc), lambda j, i: (jnp.maximum(i * hb - 1, 0), j))
    nxt = pl.BlockSpec((SUBLANES, tc), lambda j, i: (jnp.minimum((i + 1) * hb, T // SUBLANES - 1), j))
    return pl.pallas_call(
        body, name=name, grid=(C // tc, nr),
        in_specs=[cur, prv, nxt, cur, nxt, cur, nxt,
                  pl.BlockSpec((SUBLANES, tc), lambda j, i: (0, j)),
                  pl.BlockSpec((1, tc), lambda j, i: (0, j))],
        out_specs=[cur, cur,
                   pl.BlockSpec((SUBLANES, tc), lambda j, i: (0, j)),
                   pl.BlockSpec((1, tc), lambda j, i: (0, j))],
        out_shape=[_sds((T, C), BF16), _sds((T, C), BF16), _sds((SUBLANES, C), F32), _sds((1, C), F32)],
        compiler_params=_cp(("parallel", "arbitrary")),
    )(gp, gp, gp, up, up, dact, dact, conv_w8, conv_b)


def _cumsum_rows(x):
    n = x.shape[0]
    row = lax.broadcasted_iota(jnp.int32, x.shape, 0)
    s = 1
    while s < n:
        x = x + jnp.where(row >= s, pltpu.roll(x, s, 0), 0.0)
        s *= 2
    return x


def _rcumsum_rows(x):
    n = x.shape[0]
    row = lax.broadcasted_iota(jnp.int32, x.shape, 0)
    s = 1
    while s < n:
        x = x + jnp.where(row < n - s, pltpu.roll(x, n - s, 0), 0.0)
        s *= 2
    return x


def _dot_nt(a, b):
    return lax.dot_general(a.astype(BF16), b.astype(BF16), (((1,), (1,)), ((), ())), preferred_element_type=F32)


def _dot_tn(a, b):
    return lax.dot_general(a.astype(BF16), b.astype(BF16), (((0,), (0,)), ((), ())), preferred_element_type=F32)


def _dot_nn(a, b):
    return jnp.dot(a.astype(BF16), b.astype(BF16), preferred_element_type=F32)


def _hg_gates(hq, hf, lbv):
    sig = _sigmoid(hf)
    f = lbv + (1.0 - lbv) * sig
    return sig, f, jnp.log(f), 1.0 - f, hq * (HG_DK ** -0.5)


def _hg_sel_rows(ref, sp):
    return jnp.concatenate(
        [jnp.broadcast_to(ref[pl.ds(HG_SUB * i + sp, 1), :], (HG_SUB, HG_DK)) for i in range(HG_CHUNK // HG_SUB)], axis=0)


def _hg_masks():
    C = HG_CHUNK
    row = lax.broadcasted_iota(jnp.int32, (C, C), 0)
    col = lax.broadcasted_iota(jnp.int32, (C, C), 1)
    d = col - (row // HG_SUB) * HG_SUB
    tmod = row % HG_SUB
    diag_valid = jnp.logical_and(d >= 0, d <= tmod)
    return row, col, d, diag_valid


def _hg_scores(q, k, b, b_sc, k_sc):
    C, S = HG_CHUNK, HG_SUB
    row, col, d, diag_valid = _hg_masks()
    blocks = [jnp.zeros((S, C), F32)]
    for i in range(1, C // S):
        r = b_sc[pl.ds(S * i - 1, 1), :]
        qi = q[S * i:S * (i + 1)] * jnp.exp(b[S * i:S * (i + 1)] - r)
        kk = k * jnp.exp(jnp.minimum(r - b, 0.0))
        blocks.append(_dot_nt(qi, kk))
    a_off = jnp.where(col < (row // S) * S, jnp.concatenate(blocks, axis=0), 0.0)
    a_d = jnp.zeros((C, C), F32)
    for sp in range(S):
        bs = _hg_sel_rows(b_sc, sp)
        ks = _hg_sel_rows(k_sc, sp)
        e = jnp.exp(jnp.minimum(b - bs, 0.0))
        colv = jnp.sum(q * ks * e, axis=-1, keepdims=True)
        a_d = jnp.where(d == sp, colv, a_d)
    return a_off + jnp.where(diag_valid, a_d, 0.0)


def _hgrn_fwd(hq, hf, hi, lb, *, name):
    T = hq.shape[0]
    C, H, K = HG_CHUNK, HG_HEADS, HG_DK
    NC = T // C

    def body(hq_ref, hf_ref, hi_ref, lb_ref, o_ref, st_ref, s_sc, b_sc, k_sc):
        @pl.when(pl.program_id(0) == 0)
        def _():
            s_sc[...] = jnp.zeros_like(s_sc)

        st_all = s_sc[...]
        st_ref[0] = st_all
        outs, news = [], []
        for h in range(H):
            sl = slice(K * h, K * (h + 1))
            _, _, g, k, q = _hg_gates(hq_ref[:, sl], hf_ref[:, sl], lb_ref[:, sl])
            v = hi_ref[:, sl]
            b = _cumsum_rows(g)
            b_sc[h] = b
            k_sc[h] = k
            st0 = st_all[:, sl]
            bc = b_sc[h, pl.ds(C - 1, 1), :]
            a = _hg_scores(q, k, b, b_sc.at[h], k_sc.at[h])
            outs.append(_dot_nn(a, v) + _dot_nt(q * jnp.exp(b), st0))
            news.append(st0 * jnp.exp(bc) + _dot_tn(v, k * jnp.exp(bc - b)))
        o_ref[...] = jnp.concatenate(outs, axis=1)
        s_sc[...] = jnp.concatenate(news, axis=1)

    blk = pl.BlockSpec((C, H * K), lambda c: (c, 0))
    return pl.pallas_call(
        body, name=name, grid=(NC,),
        in_specs=[blk, blk, blk, pl.BlockSpec((1, H * K), lambda c: (0, 0))],
        out_specs=[blk, pl.BlockSpec((1, K, H * K), lambda c: (c, 0, 0))],
        out_shape=[_sds((T, H * K), F32), _sds((NC, K, H * K), F32)],
        scratch_shapes=[pltpu.VMEM((K, H * K), F32), pltpu.VMEM((H, C, K), F32), pltpu.VMEM((H, C, K), F32)],
        compiler_params=_cp(("arbitrary",)),
    )(hq, hf, hi, lb)


def _hgrn_bwd(hq, hf, hi, lb, states, do, *, name):
    T = hq.shape[0]
    C, H, K, S = HG_CHUNK, HG_HEADS, HG_DK, HG_SUB
    NC = T // C

    def one_head(hq_v, hf_v, v, lbv, st0, dst1, dout, b_sc, k_sc):
        sig, f, g, k, q = _hg_gates(hq_v, hf_v, lbv)
        b = _cumsum_rows(g)
        b_sc[...] = b
        k_sc[...] = k
        bc = b_sc[pl.ds(C - 1, 1), :]
        ebc = jnp.exp(bc)
        eb = jnp.exp(b)
        ekb = jnp.exp(bc - b)
        qt = q * eb
        kb = k * ekb
        row, col, d, diag_valid = _hg_masks()
        a = _hg_scores(q, k, b, b_sc, k_sc)
        dv = _dot_tn(a, dout) + _dot_nt(kb, dst1)
        da = jnp.where(col <= row, _dot_nt(dout, v), 0.0)
        dqt = _dot_nn(dout, st0)
        dkb = _dot_nn(v, dst1)
        new_ds = _dot_tn(dout, qt) + dst1 * ebc
        dq = dqt * eb
        dk = dkb * ekb
        dq_blocks = [jnp.zeros((S, K), F32)]
        for i in range(1, C // S):
            r = b_sc[pl.ds(S * i - 1, 1), :]
            eq = jnp.exp(b[S * i:S * (i + 1)] - r)
            ek = jnp.exp(jnp.minimum(r - b, 0.0))
            qi = q[S * i:S * (i + 1)] * eq
            kk = k * ek
            dai = jnp.where(col[S * i:S * (i + 1)] < S * i, da[S * i:S * (i + 1)], 0.0)
            dq_blocks.append(_dot_nn(dai, kk) * eq)
            dk = dk + _dot_tn(dai, qi) * ek
        dq = dq + jnp.concatenate(dq_blocks, axis=0)
        same_blk = (row // S == col // S).astype(BF16)
        tmod = (lax.broadcasted_iota(jnp.int32, (C, K), 0)) % S
        for sp in range(S):
            bs = _hg_sel_rows(b_sc, sp)
            ks = _hg_sel_rows(k_sc, sp)
            e = jnp.where(tmod >= sp, jnp.exp(jnp.minimum(b - bs, 0.0)), 0.0)
            dacol = jnp.sum(jnp.where(d == sp, da, 0.0), axis=-1, keepdims=True)
            w = dacol * e
            dq = dq + w * ks
            blk_sum = jnp.dot(same_blk, (w * q).astype(BF16), preferred_element_type=F32)
            dk = dk + jnp.where(tmod == sp, blk_sum, 0.0)
        extra = jnp.sum(dkb * kb, axis=0, keepdims=True) + ebc * jnp.sum(st0 * dst1, axis=0, keepdims=True)
        rowk = lax.broadcasted_iota(jnp.int32, (C, K), 0)
        db = q * dq - k * dk + jnp.where(rowk == C - 1, extra, 0.0)
        dg = _rcumsum_rows(db)
        df = dg / f - dk
        return (dq * (K ** -0.5), df * (1.0 - lbv) * sig * (1.0 - sig), dv,
                jnp.sum(df * (1.0 - sig), axis=0, keepdims=True), new_ds)

    def body(hq_ref, hf_ref, hi_ref, lb_ref, st_ref, do_ref, dq_ref, dhf_ref, dv_ref, dlb_ref, ds_sc, b_sc, k_sc):
        @pl.when(pl.program_id(0) == 0)
        def _():
            ds_sc[...] = jnp.zeros_like(ds_sc)
            dlb_ref[...] = jnp.zeros_like(dlb_ref)

        st_all = st_ref[0]
        ds_all = ds_sc[...]
        res = []
        for h in range(H):
            sl = slice(K * h, K * (h + 1))
            res.append(one_head(hq_ref[:, sl], hf_ref[:, sl], hi_ref[:, sl], lb_ref[:, sl], st_all[:, sl], ds_all[:, sl],
                                do_ref[:, sl], b_sc.at[h], k_sc.at[h]))
        cat = lambda j: jnp.concatenate([r[j] for r in res], axis=1)
        dq_ref[...] = cat(0).astype(dq_ref.dtype)
        dhf_ref[...] = cat(1).astype(dhf_ref.dtype)
        dv_ref[...] = cat(2).astype(dv_ref.dtype)
        dlb_ref[...] += cat(3)
        ds_sc[...] = cat(4)

    blk = pl.BlockSpec((C, H * K), lambda c: (NC - 1 - c, 0))
    par = pl.BlockSpec((1, H * K), lambda c: (0, 0))
    return pl.pallas_call(
        body, name=name, grid=(NC,),
        in_specs=[blk, blk, blk, par, pl.BlockSpec((1, K, H * K), lambda c: (NC - 1 - c, 0, 0)), blk],
        out_specs=[blk, blk, blk, par],
        out_shape=[_sds((T, H * K), BF16)] * 3 + [_sds((1, H * K), F32)],
        scratch_shapes=[pltpu.VMEM((K, H * K), F32), pltpu.VMEM((H, C, K), F32), pltpu.VMEM((H, C, K), F32)],
        compiler_params=_cp(("arbitrary",)),
    )(hq, hf, hi, lb, states, do)


def _att_valid(n):
    R, B = ATT_GROUP * ATT_BLOCK, ATT_BLOCK
    t = lax.broadcasted_iota(jnp.int32, (R, 2 * B), 0) % B
    j = lax.broadcasted_iota(jnp.int32, (R, 2 * B), 1)
    dist = t + B - j
    first_key = jnp.where(n > 0, 0, B)
    return jnp.logical_and(jnp.logical_and(dist >= 0, dist < B), j >= first_key)


def _att_load(cur_ref, prev_ref, ba_ref, kv):
    hd = ATT_HD
    def cols(ref, c0):
        return ref[:, c0:c0 + hd] + ba_ref[:, c0:c0 + hd]
    qs = jnp.concatenate([cols(cur_ref, hd * (ATT_GROUP * kv + g)) for g in range(ATT_GROUP)], axis=0)
    kc = jnp.concatenate([cols(prev_ref, ATT_Q_W + hd * kv), cols(cur_ref, ATT_Q_W + hd * kv)], axis=0)
    vc = jnp.concatenate([cols(prev_ref, ATT_Q_W + ATT_KV_W + hd * kv), cols(cur_ref, ATT_Q_W + ATT_KV_W + hd * kv)], axis=0)
    return qs, kc, vc


def _att_probs(qs, kc, valid, sink_ref, kv):
    scale = 1.0 / math.sqrt(ATT_HD)
    s = jnp.where(valid, _dot_nt(qs, kc) * scale, NEG)
    sink = jnp.concatenate([jnp.full((ATT_BLOCK, 1), sink_ref[0, ATT_GROUP * kv + g], F32) for g in range(ATT_GROUP)], axis=0)
    m = jnp.maximum(jnp.max(s, axis=-1, keepdims=True), sink)
    p = jnp.exp(s - m)
    ps = jnp.exp(sink - m)
    inv = 1.0 / (jnp.sum(p, axis=-1, keepdims=True) + ps)
    return p * inv, ps * inv


def _attn_fwd(att, b_attn, sinks, *, name):
    T = att.shape[0]
    B = ATT_BLOCK
    NB = T // B

    def body(sink_ref, cur_ref, prev_ref, ba_ref, o_ref):
        valid = _att_valid(pl.program_id(0))
        for kv in range(ATT_KV):
            qs, kc, vc = _att_load(cur_ref, prev_ref, ba_ref, kv)
            prob, _ = _att_probs(qs, kc, valid, sink_ref, kv)
            o = _dot_nn(prob, vc)
            for g in range(ATT_GROUP):
                c0 = ATT_HD * (ATT_GROUP * kv + g)
                o_ref[:, c0:c0 + ATT_HD] = o[B * g:B * (g + 1)]

    return pl.pallas_call(
        body, name=name, grid=(NB,),
        in_specs=[pl.BlockSpec(memory_space=pltpu.SMEM),
                  pl.BlockSpec((B, ATT_COLS), lambda n: (n, 0)),
                  pl.BlockSpec((B, ATT_COLS), lambda n: (jnp.maximum(n - 1, 0), 0)),
                  pl.BlockSpec((1, ATT_COLS), lambda n: (0, 0))],
        out_specs=pl.BlockSpec((B, ATT_Q_W), lambda n: (n, 0)),
        out_shape=_sds((T, ATT_Q_W), F32),
        compiler_params=_cp(("parallel",)),
    )(sinks, att, att, b_attn)


def _attn_bwd(att, b_attn, sinks, dmix, *, name):
    T = att.shape[0]
    B, hd = ATT_BLOCK, ATT_HD
    NB = T // B
    scale = 1.0 / math.sqrt(hd)

    def body(sink_ref, cur_ref, prev_ref, ba_ref, do_ref, daq_ref, dakv_ref, dsink_ref, dbq_ref, dbkv_ref,
             carry_sc, cprev_sc, ccur_sc):
        n = pl.program_id(0)

        @pl.when(n == 0)
        def _():
            carry_sc[...] = jnp.zeros_like(carry_sc)
            dsink_ref[...] = jnp.zeros_like(dsink_ref)
            dbq_ref[...] = jnp.zeros_like(dbq_ref)
            dbkv_ref[...] = jnp.zeros_like(dbkv_ref)

        @pl.when(n < NB)
        def _():
            valid = _att_valid(n)
            hrow = lax.broadcasted_iota(jnp.int32, (SUBLANES, 128), 0)
            dsink = jnp.zeros((SUBLANES, 128), F32)
            for kv in range(ATT_KV):
                qs, kc, vc = _att_load(cur_ref, prev_ref, ba_ref, kv)
                prob, psink = _att_probs(qs, kc, valid, sink_ref, kv)
                dout = jnp.concatenate(
                    [do_ref[:, hd * (ATT_GROUP * kv + g):hd * (ATT_GROUP * kv + g + 1)] for g in range(ATT_GROUP)], axis=0)
                dp = _dot_nt(dout, vc)
                delta = jnp.sum(prob * dp, axis=-1, keepdims=True)
                dsc = prob * (dp - delta) * scale
                dq = _dot_nn(dsc, kc)
                dk = _dot_tn(dsc, qs)
                dvv = _dot_tn(prob, dout)
                dsk = psink * delta
                for g in range(ATT_GROUP):
                    h = ATT_GROUP * kv + g
                    daq_ref[:, hd * h:hd * (h + 1)] = dq[B * g:B * (g + 1)].astype(daq_ref.dtype)
                    tot = jnp.sum(dsk[B * g:B * (g + 1)], axis=0, keepdims=True)
                    dsink = dsink - jnp.where(hrow == h, tot, 0.0)
                cprev_sc[:, hd * kv:hd * (kv + 1)] = dk[:B]
                ccur_sc[:, hd * kv:hd * (kv + 1)] = dk[B:]
                cprev_sc[:, ATT_KV_W + hd * kv:ATT_KV_W + hd * (kv + 1)] = dvv[:B]
                ccur_sc[:, ATT_KV_W + hd * kv:ATT_KV_W + hd * (kv + 1)] = dvv[B:]
            dsink_ref[...] += dsink
            dbq_ref[...] += jnp.sum(daq_ref[...].astype(F32), axis=0, keepdims=True)
            done = carry_sc[...] + cprev_sc[...]
            dakv_ref[...] = done.astype(dakv_ref.dtype)
            dbkv_ref[...] += jnp.sum(done.astype(dakv_ref.dtype).astype(F32), axis=0, keepdims=True)
            carry_sc[...] = ccur_sc[...]

        @pl.when(n == NB)
        def _():
            done = carry_sc[...]
            dakv_ref[...] = done.astype(dakv_ref.dtype)
            dbkv_ref[...] += jnp.sum(done.astype(dakv_ref.dtype).astype(F32), axis=0, keepdims=True)

    cl = lambda n: jnp.minimum(n, NB - 1)
    return pl.pallas_call(
        body, name=name, grid=(NB + 1,),
        in_specs=[pl.BlockSpec(memory_space=pltpu.SMEM),
                  pl.BlockSpec((B, ATT_COLS), lambda n: (cl(n), 0)),
                  pl.BlockSpec((B, ATT_COLS), lambda n: (jnp.maximum(cl(n) - 1, 0), 0)),
                  pl.BlockSpec((1, ATT_COLS), lambda n: (0, 0)),
                  pl.BlockSpec((B, ATT_Q_W), lambda n: (cl(n), 1))],
        out_specs=[pl.BlockSpec((B, ATT_Q_W), lambda n: (cl(n), 0)),
                   pl.BlockSpec((B, 2 * ATT_KV_W), lambda n: (jnp.maximum(n - 1, 0), 0)),
                   pl.BlockSpec((SUBLANES, 128), lambda n: (0, 0)),
                   pl.BlockSpec((1, ATT_Q_W), lambda n: (0, 0)),
                   pl.BlockSpec((1, 2 * ATT_KV_W), lambda n: (0, 0))],
        out_shape=[_sds((T, ATT_Q_W), BF16), _sds((T, 2 * ATT_KV_W), BF16), _sds((SUBLANES, 128), F32),
                   _sds((1, ATT_Q_W), F32), _sds((1, 2 * ATT_KV_W), F32)],
        scratch_shapes=[pltpu.VMEM((B, 2 * ATT_KV_W), F32)] * 3,
        compiler_params=_cp(("arbitrary",)),
    )(sinks, att, att, b_attn, dmix)


def _silu_and_grad(x):
    sg = _sigmoid(x)
    return x * sg, sg * (1.0 + x * (1.0 - sg))


def _mix_fwd_fn(o_raw, hg, o_att, hgw):
    outs = []
    for h in range(HG_HEADS):
        sl = slice(HG_DK * h, HG_DK * (h + 1))
        silu, _ = _silu_and_grad(hg[:, sl])
        outs.append(_rms_fwd(o_raw[:, sl], hgw) * silu)
    outs.append(o_att)
    return (jnp.concatenate(outs, axis=1),)


def _mix_bwd_fn(o_raw, hg, dmix, hgw):
    dos, dhgs = [], []
    dw = jnp.zeros((1, HG_DK), F32)
    for h in range(HG_HEADS):
        sl = slice(HG_DK * h, HG_DK * (h + 1))
        silu, dsilu = _silu_and_grad(hg[:, sl])
        dy = dmix[:, sl]
        dhgs.append(dy * _rms_fwd(o_raw[:, sl], hgw) * dsilu)
        dx, dwh = _rms_bwd(o_raw[:, sl], hgw, dy * silu)
        dos.append(dx)
        dw = dw + dwh
    return jnp.concatenate(dos, axis=1), jnp.concatenate(dhgs, axis=1), dw


def _final_fn(h2, tgt, wf):
    d = h2.shape[1]
    err = _rms_fwd(h2, wf) - tgt
    loss_cols = (0.5 / d) * jnp.sum(err * err, axis=0, keepdims=True)
    dh2, dwf = _rms_bwd(h2, wf, err * (1.0 / d))
    return dh2, dh2, loss_cols, dwf


class _NoExchange:
    def __init__(self, weights):
        self.weights = weights

    def start(self):
        return None

    def w_in(self, after):
        return self.weights["w_in_t"]

    def rest(self, after):
        return self.weights

    def ffn_grads(self, gs):
        return None


def _local_step(x, tgt, p, ex):
    T, D = x.shape
    row = lambda n, dt: _sds((T, n), dt)
    acc = lambda n: _sds((1, n), F32)

    (u,) = _rowwise(lambda xv, w: (_rms_fwd(xv, w),), [_full(x)], [p["norm_mix_w"]], [row(D, BF16)], [], name="rms_mix",
                    after=ex.start())
    p = dict(p, w_in_t=ex.w_in(u))
    hq, hf, hi, hg, att = _mm_nt(u, p["w_in_t"], splits=[HG_W] * 4 + [ATT_COLS], out_dtype=F32, name="in_proj")
    o_raw, states = _hgrn_fwd(hq, hf, hi, p["lb"], name="hgrn_fwd")
    o_att = _attn_fwd(att, p["b_attn"], p["sinks"], name="attn_fwd")
    (mix,) = _rowwise(_mix_fwd_fn, [_full(o_raw), _full(hg), _full(o_att)], [p["hg_norm_w"]], [row(D, BF16)], [],
                      name="mix_fwd")
    p = dict(p, **ex.rest(mix))
    h1 = _mm_nn([[mix]], [p["w_out"]], out_dtype=F32, name="out_proj", residual=x)
    (v,) = _rowwise(lambda hv, w: (_rms_fwd(hv, w),), [_full(h1)], [p["norm_ffn_w"]], [row(D, BF16)], [], name="rms_ffn")
    (gp,) = _mm_nt(v, p["w_gate_t"], splits=[D_FF], out_dtype=F32, name="gate_proj")
    (up,) = _mm_nt(v, p["w_up_t"], splits=[D_FF], out_dtype=F32, name="up_proj")
    act = _convact_fwd(gp, up, p["conv_w8"], p["conv_b"], name="convact_fwd")
    h2 = _mm_nn([[act]], [p["w_down"]], out_dtype=F32, name="down_proj", residual=h1)
    dh2, dh2_b, loss_cols, d_final = _rowwise(_final_fn, [_full(h2), _full(tgt)], [p["final_norm_w"]],
                                              [row(D, F32), row(D, BF16)], [acc(D), acc(D)], name="final_loss")

    (dact,) = _mm_nt(dh2_b, p["w_down"], splits=[D_FF], out_dtype=F32, name="d_act")
    g_down = _mm_tn([act], dh2_b, name="g_down")
    dgp, dup, d_conv_w8, d_conv_b = _convact_bwd(gp, up, dact, p["conv_w8"], p["conv_b"], name="convact_bwd")
    dv = _mm_nn([[dgp], [dup]], [p["w_gate_t"], p["w_up_t"]], out_dtype=F32, name="d_v")
    g_gate_t = _mm_tn([dgp], v, name="g_gate")
    g_up_t = _mm_tn([dup], v, name="g_up")
    sent = ex.ffn_grads([g_gate_t, g_up_t, g_down])

    def ffn_norm_bwd(hv, dvv, dh2v, w):
        dx, dw = _rms_bwd(hv, w, dvv)
        dh1v = dx + dh2v
        return dh1v, dh1v, dw

    dh1, dh1_b, d_norm_ffn = _rowwise(ffn_norm_bwd, [_full(h1), _full(dv), _full(dh2)], [p["norm_ffn_w"]],
                                      [row(D, F32), row(D, BF16)], [acc(D)], name="rms_ffn_bwd", after=sent)
    (dmix,) = _mm_nt(dh1_b, p["w_out"], splits=[D], out_dtype=F32, name="d_mix")
    g_out = _mm_tn([mix], dh1_b, name="g_out")
    do_raw, dhg, d_hg_norm = _rowwise(_mix_bwd_fn, [_full(o_raw), _full(hg), (dmix, HG_W, 0, 0)], [p["hg_norm_w"]],
                                      [row(HG_W, F32), row(HG_W, BF16)], [acc(HG_DK)], name="mix_bwd")
    daq, dakv, d_sinks8, d_bq, d_bkv = _attn_bwd(att, p["b_attn"], p["sinks"], dmix, name="attn_bwd")
    dhq, dhf, dhi, d_lb = _hgrn_bwd(hq, hf, hi, p["lb"], states, do_raw, name="hgrn_bwd")
    pieces = [dhq, dhf, dhi, dhg, daq, dakv]
    du = _mm_nn([pieces], [p["w_in_t"]], out_dtype=F32, name="d_u")
    g_in_t = _mm_tn(pieces, u, name="g_in")

    def mix_norm_bwd(xv, duv, dh1v, w):
        dx, dw = _rms_bwd(xv, w, duv)
        return dx + dh1v, dw

    dx, d_norm_mix = _rowwise(mix_norm_bwd, [_full(x), _full(du), _full(dh1)], [p["norm_mix_w"]], [row(D, F32)], [acc(D)],
                              name="rms_mix_bwd")
    grads = dict(g_in_t=g_in_t, g_out=g_out, g_gate_t=g_gate_t, g_up_t=g_up_t, g_down=g_down,
                 norm_mix_w=d_norm_mix, b_attn=jnp.concatenate([d_bq, d_bkv], axis=1), lb=d_lb, hg_norm_w=d_hg_norm,
                 sinks8=d_sinks8, norm_ffn_w=d_norm_ffn, conv_w8=d_conv_w8, conv_b=d_conv_b, final_norm_w=d_final)
    return loss_cols, dx, grads


SLAB = (IN_COLS // N_CHIPS, D_FF // N_CHIPS, D_FF // N_CHIPS, D_FF // N_CHIPS, D_MODEL // N_CHIPS)
N_W = len(SLAB)
PACK_OFF = tuple(sum(SLAB[:i]) for i in range(N_W))
PACK_ROWS = sum(SLAB)
FULL_OFF = tuple(N_CHIPS * o for o in PACK_OFF)
FULL_ROWS = N_CHIPS * PACK_ROWS
HALF = tuple(s // 2 for s in SLAB)
HPACK_OFF = tuple(sum(HALF[:i]) for i in range(N_W))
HPACK_ROWS = sum(HALF)
HFULL_OFF = tuple(N_CHIPS * o for o in HPACK_OFF)
HFULL_ROWS = N_CHIPS * HPACK_ROWS
CHIP_FLIPS = ((1, 0), (0, 1), (1, 1))
N_DEV = 8
BF16_ROWS = 16
ANY = pl.BlockSpec(memory_space=pl.ANY)


def _pos():
    return lax.axis_index("x"), lax.axis_index("y"), lax.axis_index("c")


def _flip(v, f):
    return 1 - v if f else v


def _rcopy(src, dst, ssem, rsem, dev):
    return pltpu.make_async_remote_copy(src_ref=src, dst_ref=dst, send_sem=ssem, recv_sem=rsem, device_id=dev,
                                        device_id_type=pl.DeviceIdType.MESH)


def _rows(ref, start, n, align=SUBLANES):
    if not isinstance(start, int):
        start = pl.multiple_of(start, align)
    return ref.at[pl.ds(start, n), :]


FFN_W = (1, 2, 3)
N_PEER = 1 + len(CHIP_FLIPS)
HBM = pl.BlockSpec(memory_space=pltpu.HBM)
SEM = pl.BlockSpec(memory_space=pltpu.SEMAPHORE)
EFFECT = pltpu.SideEffectType.DATAFLOW_SIDE_EFFECTING
LANES = 128


def _gather_start(pack, cw8):
    D = pack.shape[1]
    lands = [lax.empty((N_CHIPS * SLAB[0], D), pack.dtype), lax.empty((3 * N_CHIPS * SLAB[1], D), pack.dtype),
             lax.empty((N_CHIPS * SLAB[4], D), pack.dtype), lax.empty((N_CHIPS,) + cw8.shape, cw8.dtype)]
    bufs = [pack, cw8] + lands

    def body(pack_ref, cw_ref, l_in, l_ffn, l_out, l_cw, *rest):
        in_send, in_recv, rest_send, rest_recv = rest[:4]
        token = rest[-1]
        x, y, c = _pos()
        q = 2 * x + y
        peers = _gather_peers(x, y, c)
        for k, peer in enumerate(peers):
            _rcopy(_rows(pack_ref, PACK_OFF[0], SLAB[0]), _rows(l_in, q * SLAB[0], SLAB[0], BF16_ROWS),
                   in_send.at[k], in_recv.at[k], peer).start()
        for k, peer in enumerate(peers):
            for j, w in enumerate(FFN_W):
                _rcopy(_rows(pack_ref, PACK_OFF[w], SLAB[w]), _rows(l_ffn, (j * N_CHIPS + q) * SLAB[w], SLAB[w], BF16_ROWS),
                       rest_send.at[k], rest_recv.at[k], peer).start()
            _rcopy(_rows(pack_ref, PACK_OFF[4], SLAB[4]), _rows(l_out, q * SLAB[4], SLAB[4], BF16_ROWS),
                   rest_send.at[N_PEER + k], rest_recv.at[N_PEER + k], peer).start()
            _rcopy(cw_ref, l_cw.at[q], rest_send.at[2 * N_PEER + k], rest_recv.at[2 * N_PEER + k], peer).start()
        token[...] = jnp.zeros_like(token)

    outs = pl.pallas_call(
        body, name="gather_start", in_specs=[HBM] * len(bufs),
        out_specs=[SEM] * 4 + [HBM] * len(bufs) + [pl.BlockSpec(memory_space=pltpu.VMEM)],
        out_shape=[pltpu.SemaphoreType.DMA((N_PEER,)), pltpu.SemaphoreType.DMA((N_PEER,)),
                   pltpu.SemaphoreType.DMA((3 * N_PEER,)), pltpu.SemaphoreType.DMA((3 * N_PEER,))]
        + [pltpu.HBM(b.shape, b.dtype) for b in bufs] + [_sds((SUBLANES, LANES), F32)],
        input_output_aliases={i: 4 + i for i in range(len(bufs))},
        compiler_params=pltpu.CompilerParams(has_side_effects=EFFECT),
    )(*[pltpu.with_memory_space_constraint(b, pltpu.HBM) for b in bufs])
    return dict(in_sems=outs[0:2], rest_sems=outs[2:4], pack=outs[4], cw=outs[5], l_in=outs[6], l_ffn=outs[7],
                l_out=outs[8], l_cw=outs[9], token=outs[10])


def _gather_peers(x, y, c):
    return [(x, y, 1 - c)] + [(_flip(x, fx), _flip(y, fy), c) for fx, fy in CHIP_FLIPS]


def _gather_wait_in(g, after):
    def body(pack_ref, l_in, send, recv, after_ref, pack_out, l_out):
        for k, peer in enumerate(_gather_peers(*_pos())):
            cp = _rcopy(_rows(pack_ref, PACK_OFF[0], SLAB[0]), _rows(l_in, 0, SLAB[0]), send.at[k], recv.at[k], peer)
            cp.wait_send()
            cp.wait_recv()

    return pl.pallas_call(
        body, name="gather_wait_in", in_specs=[HBM, HBM, SEM, SEM, ANY], out_specs=[HBM, HBM],
        out_shape=[pltpu.HBM(g["pack"].shape, g["pack"].dtype), pltpu.HBM(g["l_in"].shape, g["l_in"].dtype)],
        input_output_aliases={0: 0, 1: 1}, compiler_params=pltpu.CompilerParams(has_side_effects=EFFECT),
    )(g["pack"], g["l_in"], *g["in_sems"], after)


def _gather_wait_rest(g, pack, after):
    n_ffn = len(FFN_W) * SLAB[FFN_W[0]]

    def body(pack_ref, cw_ref, l_ffn, l_out, l_cw, send, recv, after_ref, o_ffn, o_out, o_cw):
        for k, peer in enumerate(_gather_peers(*_pos())):
            for cp in (_rcopy(_rows(pack_ref, PACK_OFF[FFN_W[0]], n_ffn), _rows(l_ffn, 0, n_ffn), send.at[k], recv.at[k], peer),
                       _rcopy(_rows(pack_ref, PACK_OFF[4], SLAB[4]), _rows(l_out, 0, SLAB[4]),
                              send.at[N_PEER + k], recv.at[N_PEER + k], peer),
                       _rcopy(cw_ref, l_cw.at[0], send.at[2 * N_PEER + k], recv.at[2 * N_PEER + k], peer)):
                cp.wait_send()
                cp.wait_recv()

    ins = [pack, g["cw"], g["l_ffn"], g["l_out"], g["l_cw"]]
    return pl.pallas_call(
        body, name="gather_wait_rest", in_specs=[HBM] * 5 + [SEM, SEM, ANY], out_specs=[HBM] * 3,
        out_shape=[pltpu.HBM(b.shape, b.dtype) for b in ins[2:]],
        input_output_aliases={2: 0, 3: 1, 4: 2}, compiler_params=pltpu.CompilerParams(has_side_effects=EFFECT),
    )(*ins, *g["rest_sems"], after)


def _exchange_halves(ws, gs, small, *, name):
    D = gs[0].shape[1]
    n = len(ws)
    has_small = small is not None

    def body(*refs):
        g = refs[:n]
        t = refs[n + has_small:2 * n + has_small]
        sems = refs[2 * n + 2 * has_small:]
        d2d_send, d2d_recv = sems[0], sems[1]
        x, y, c = _pos()
        sib = (x, y, 1 - c)
        drains = []
        for i, w in enumerate(ws):
            h = HALF[w]
            for qq in range(N_CHIPS):
                _rcopy(_rows(g[i], qq * SLAB[w] + (1 - c) * h, h), _rows(t[i], qq * h, h),
                       d2d_send.at[i], d2d_recv.at[i], sib).start()
            drains.append(_rcopy(t[i], t[i], d2d_send.at[i], d2d_recv.at[i], sib))
        if has_small:
            small_ref, sall_ref = refs[n], refs[2 * n + 1]
            sm_send, sm_recv, loc_sem = sems[2], sems[3], sems[4]
            me = 4 * x + 2 * y + c
            own_small = pltpu.make_async_copy(small_ref, sall_ref.at[me], loc_sem)
            own_small.start()
            for f in range(1, N_DEV):
                peer = (_flip(x, f & 4), _flip(y, f & 2), _flip(c, f & 1))
                cp = _rcopy(small_ref, sall_ref.at[me], sm_send.at[f - 1], sm_recv.at[f - 1], peer)
                cp.start()
                drains.append(cp)
        for d in drains:
            d.wait_recv()
        for d in drains:
            d.wait_send()
        if has_small:
            own_small.wait()

    out_shape = [_sds((N_CHIPS * HALF[w], D), F32) for w in ws]
    scratch = [pltpu.SemaphoreType.DMA((n,)), pltpu.SemaphoreType.DMA((n,))]
    if has_small:
        out_shape.append(_sds((N_DEV,) + small.shape, F32))
        scratch += [pltpu.SemaphoreType.DMA((N_DEV - 1,)), pltpu.SemaphoreType.DMA((N_DEV - 1,)), pltpu.SemaphoreType.DMA]
    return pl.pallas_call(
        body, name=name, in_specs=[ANY] * (n + has_small), out_specs=[ANY] * (n + has_small),
        out_shape=out_shape, scratch_shapes=scratch,
    )(*gs, *([small] if has_small else []))


REDUCE_SPLIT = 2


def _chip_partial(ws, gs, theirs, *, name):
    D = gs[0].shape[1]
    n = len(ws)

    def body(*refs):
        for i in range(n):
            refs[2 * n + i][...] = refs[i][...] + refs[n + i][...]

    blk = [HALF[w] // REDUCE_SPLIT for w in ws]
    mine = [pl.BlockSpec((b, D), lambda qq, j: ((2 * qq + lax.axis_index("c")) * REDUCE_SPLIT + j, 0)) for b in blk]
    flat = [pl.BlockSpec((b, D), lambda qq, j: (qq * REDUCE_SPLIT + j, 0)) for b in blk]
    return pl.pallas_call(
        body, name=name, grid=(N_CHIPS, REDUCE_SPLIT), in_specs=mine + flat, out_specs=flat,
        out_shape=[_sds((N_CHIPS * HALF[w], D), F32) for w in ws],
        compiler_params=_cp(("parallel", "parallel")),
    )(*gs, *theirs)


def _partial_copies(ws, part, got, send_sems, recv_sems):
    x, y, c = _pos()
    cps = []
    for k, (fx, fy) in enumerate(CHIP_FLIPS):
        peer = (_flip(x, fx), _flip(y, fy), c)
        qp = 2 * _flip(x, fx) + _flip(y, fy)
        for i, w in enumerate(ws):
            cps.append(_rcopy(_rows(part[i], qp * HALF[w], HALF[w]), _rows(got[i], k * HALF[w], HALF[w]),
                              send_sems.at[len(ws) * k + i], recv_sems.at[len(ws) * k + i], peer))
    return cps


def _send_chip_partials(ws, parts, *, name):
    D = parts[0].shape[1]
    n = len(ws)

    def body(*refs):
        cps = _partial_copies(ws, refs[:n], refs[n:2 * n], refs[2 * n], refs[2 * n + 1])
        for cp in cps:
            cp.start()
        for cp in cps:
            cp.wait_recv()
        for cp in cps:
            cp.wait_send()

    return pl.pallas_call(
        body, name=name, in_specs=[ANY] * n, out_specs=[ANY] * n,
        out_shape=[_sds((len(CHIP_FLIPS) * HALF[w], D), F32) for w in ws],
        scratch_shapes=[pltpu.SemaphoreType.DMA((len(CHIP_FLIPS) * n,)), pltpu.SemaphoreType.DMA((len(CHIP_FLIPS) * n,))],
    )(*parts)


def _send_start(ws, parts, *, name):
    D = parts[0].shape[1]
    n = len(ws)
    bufs = list(parts) + [lax.empty((len(CHIP_FLIPS) * HALF[w], D), F32) for w in ws]

    def body(*refs):
        send_sems, recv_sems = refs[2 * n], refs[2 * n + 1]
        for cp in _partial_copies(ws, refs[:n], refs[n:2 * n], send_sems, recv_sems):
            cp.start()
        refs[-1][...] = jnp.zeros_like(refs[-1])

    outs = pl.pallas_call(
        body, name=name, in_specs=[HBM] * (2 * n),
        out_specs=[SEM, SEM] + [HBM] * (2 * n) + [pl.BlockSpec(memory_space=pltpu.VMEM)],
        out_shape=[pltpu.SemaphoreType.DMA((len(CHIP_FLIPS) * n,)), pltpu.SemaphoreType.DMA((len(CHIP_FLIPS) * n,))]
        + [pltpu.HBM(b.shape, b.dtype) for b in bufs] + [_sds((SUBLANES, LANES), F32)],
        input_output_aliases={i: 2 + i for i in range(2 * n)},
        compiler_params=pltpu.CompilerParams(has_side_effects=EFFECT),
    )(*[pltpu.with_memory_space_constraint(b, pltpu.HBM) for b in bufs])
    return dict(sems=outs[0:2], parts=outs[2:2 + n], got=outs[2 + n:2 + 2 * n], token=outs[-1])


def _send_wait(ws, s, after, *, name):
    n = len(ws)

    def body(*refs):
        for cp in _partial_copies(ws, refs[:n], refs[n:2 * n], refs[2 * n], refs[2 * n + 1]):
            cp.wait_send()
            cp.wait_recv()

    bufs = list(s["parts"]) + list(s["got"])
    outs = pl.pallas_call(
        body, name=name, in_specs=[HBM] * (2 * n) + [SEM, SEM, ANY], out_specs=[HBM] * (2 * n),
        out_shape=[pltpu.HBM(b.shape, b.dtype) for b in bufs],
        input_output_aliases={i: i for i in range(2 * n)},
        compiler_params=pltpu.CompilerParams(has_side_effects=EFFECT),
    )(*bufs, *s["sems"], after)
    return outs[:n], outs[n:]


def _chip_reduce(parts, got):
    D = parts[0].shape[1]
    nk = len(CHIP_FLIPS)

    def body(*refs):
        outs = refs[(1 + nk) * N_W:]
        for w in range(N_W):
            acc = refs[w][...]
            for k in range(nk):
                acc = acc + refs[N_W * (1 + k) + w][...]
            outs[w][...] = acc

    blk = [h // REDUCE_SPLIT for h in HALF]

    def q_idx(j):
        return (2 * lax.axis_index("x") + lax.axis_index("y")) * REDUCE_SPLIT + j

    in_specs = [pl.BlockSpec((b, D), lambda j: (q_idx(j), 0)) for b in blk]
    for k in range(nk):
        in_specs += [pl.BlockSpec((b, D), functools.partial(lambda j, k: (k * REDUCE_SPLIT + j, 0), k=k)) for b in blk]
    out_specs = [pl.BlockSpec((b, D), lambda j: (lax.axis_index("c") * REDUCE_SPLIT + j, 0)) for b in blk]
    return pl.pallas_call(
        body, name="chip_reduce", grid=(REDUCE_SPLIT,), in_specs=in_specs, out_specs=out_specs,
        out_shape=[_sds((s, D), F32) for s in SLAB],
        compiler_params=_cp(("parallel",)),
    )(*parts, *[g for _ in range(nk) for g in got])


def _exchange_reduced(shards):
    def body(i0, i1, i2, i3, i4, o0, o1, o2, o3, o4, send_sems, recv_sems):
        ins = (i0, i1, i2, i3, i4)
        outs = (o0, o1, o2, o3, o4)
        x, y, c = _pos()
        sib = (x, y, 1 - c)
        cps = []
        for w in range(N_W):
            cp = _rcopy(_rows(ins[w], c * HALF[w], HALF[w]), _rows(outs[w], c * HALF[w], HALF[w]),
                        send_sems.at[w], recv_sems.at[w], sib)
            cp.start()
            cps.append(cp)
        for cp in cps:
            cp.wait_recv()
        for cp in cps:
            cp.wait_send()

    return pl.pallas_call(
        body, name="exchange_reduced", in_specs=[ANY] * N_W, out_specs=[ANY] * N_W,
        out_shape=[_sds(s.shape, s.dtype) for s in shards], input_output_aliases={w: w for w in range(N_W)},
        scratch_shapes=[pltpu.SemaphoreType.DMA((N_W,)), pltpu.SemaphoreType.DMA((N_W,))],
    )(*shards)


def _adamw_fn(w, g, m, v):
    m2 = ADAM_B1 * m + (1.0 - ADAM_B1) * g
    v2 = ADAM_B2 * v + (1.0 - ADAM_B2) * (g * g)
    m_hat = m2 / (1.0 - ADAM_B1 ** ADAM_STEP)
    v_hat = v2 / (1.0 - ADAM_B2 ** ADAM_STEP)
    return -ADAM_LR * (m_hat / (jnp.sqrt(v_hat) + ADAM_EPS) + ADAM_WD * w), m2, v2


def _adamw(w, g, m, v, *, name):
    shp = _sds(w.shape, F32)
    rows = w.shape[0]
    tm = max(t for t in range(SUBLANES, 512 + 1, SUBLANES) if rows % t == 0)
    return _rowwise(_adamw_fn, [_full(w), _full(g), _full(m), _full(v)], [], [shp] * 3, [], name=name, tm=tm)


SMALL_SEGS = (("loss", 8), ("norm_mix_w", 8), ("b_attn", 8), ("lb_logits", 8), ("hg_norm_w", 8), ("sinks", 8),
              ("norm_ffn_w", 8), ("conv_w", 72), ("conv_b", 24), ("final_norm_w", 8))
SMALL_OFF = {n: sum(r for _, r in SMALL_SEGS[:i]) for i, (n, _) in enumerate(SMALL_SEGS)}
SMALL_ROWS = sum(r for _, r in SMALL_SEGS)
LANES = 128


def _pack_small(parts):
    segs = []
    for n, r in SMALL_SEGS:
        a = parts.get(n)
        flat = jnp.zeros((0,), F32) if a is None else a.reshape(-1).astype(F32)
        segs.append(jnp.pad(flat, (0, r * LANES - flat.shape[0])).reshape(r, LANES))
    return jnp.concatenate(segs, axis=0)


def _unpack_small(pack, n, shape):
    size = math.prod(shape)
    r0 = SMALL_OFF[n]
    return pack[r0:r0 + dict(SMALL_SEGS)[n]].reshape(-1)[:size].reshape(shape)


def _small_update(sall, wp, mp, vp):
    R = SMALL_ROWS
    r_lb = SMALL_OFF["lb_logits"]

    def body(s_ref, w_ref, m_ref, v_ref, g_ref, d_ref, m2_ref, v2_ref, loss_ref):
        g = s_ref[0]
        for i in range(1, N_DEV):
            g = g + s_ref[i]
        tot = jnp.sum(jnp.sum(g[0:8], axis=1, keepdims=True), axis=0, keepdims=True)
        loss_ref[...] = jnp.broadcast_to(tot, loss_ref.shape)
        lg = w_ref[r_lb:r_lb + 8, :]
        p0 = _sigmoid(lg - pltpu.roll(lg, 4, 0))
        d = g[r_lb:r_lb + 8]
        d = d + pltpu.roll(d, 4, 0)
        sign = jnp.where(lax.broadcasted_iota(jnp.int32, d.shape, 0) < 4, 1.0, -1.0)
        g = jnp.concatenate([g[:r_lb], sign * d * p0 * (1.0 - p0), g[r_lb + 8:]], axis=0)
        g_ref[...] = g
        d_ref[...], m2_ref[...], v2_ref[...] = _adamw_fn(w_ref[...], g, m_ref[...], v_ref[...])

    full = pl.BlockSpec((R, LANES), lambda: (0, 0))
    return pl.pallas_call(
        body, name="small_update",
        in_specs=[pl.BlockSpec((N_DEV, R, LANES), lambda: (0, 0, 0)), full, full, full],
        out_specs=[full, full, full, full, pl.BlockSpec((8, LANES), lambda: (0, 0))],
        out_shape=[_sds((R, LANES), F32)] * 4 + [_sds((8, LANES), F32)],
        compiler_params=_cp(),
    )(sall, wp, mp, vp)


def _lb_fwd(lb_logits):
    n = lb_logits.shape[1]

    def body(l_ref, o_ref):
        o_ref[...] = _sigmoid(l_ref[0:1, :] - l_ref[1:2, :])

    return pl.pallas_call(body, name="lb_fwd", out_shape=_sds((1, n), F32), compiler_params=_cp())(lb_logits)


class _MeshExchange:
    def __init__(self, pack, cw8):
        self.gather = _gather_start(pack, cw8)
        self.sent = None
        self.conv_w8 = None

    def start(self):
        return self.gather["token"]

    def w_in(self, after):
        self.pack, l_in = _gather_wait_in(self.gather, after)
        return (l_in, N_CHIPS * SLAB[0], 0)

    def rest(self, after):
        l_ffn, l_out, l_cw = _gather_wait_rest(self.gather, self.pack, after)
        self.conv_w8 = jnp.concatenate([l_cw[i] for i in range(N_CHIPS)], axis=1)
        rows = N_CHIPS * SLAB[FFN_W[0]]
        return dict(w_gate_t=(l_ffn, rows, 0), w_up_t=(l_ffn, rows, 1), w_down=(l_ffn, rows, 2),
                    w_out=(l_out, N_CHIPS * SLAB[4], 0), conv_w8=self.conv_w8)

    def ffn_grads(self, gs):
        theirs = _exchange_halves(FFN_W, gs, None, name="exchange_halves_ffn")
        parts = _chip_partial(FFN_W, gs, theirs, name="chip_partial_ffn")
        self.sent = _send_start(FFN_W, parts, name="send_ffn_start")
        return self.sent["token"]


def kernel(x, norm_mix_w, w_in, b_attn, lb_logits, hg_norm_w, sinks, w_out, norm_ffn_w, w_gate, w_up, conv_w, conv_b, w_down, final_norm_w, loss_target, m_norm_mix_w, m_w_in, m_b_attn, m_lb_logits, m_hg_norm_w, m_sinks, m_w_out, m_norm_ffn_w, m_w_gate, m_w_up, m_conv_w, m_conv_b, m_w_down, m_final_norm_w, v_norm_mix_w, v_w_in, v_b_attn, v_lb_logits, v_hg_norm_w, v_sinks, v_w_out, v_norm_ffn_w, v_w_gate, v_w_up, v_conv_w, v_conv_b, v_w_down, v_final_norm_w):
    D = D_MODEL
    q = 2 * lax.axis_index("x") + lax.axis_index("y")
    ccols = D_FF // N_CHIPS

    pack = jnp.concatenate([w_in[0].T, w_gate[0].T, w_up[0].T, w_down[0], w_out[0]], axis=0).astype(BF16)
    cw8 = jnp.concatenate([conv_w[0], jnp.zeros((SUBLANES - 3, ccols), F32)], axis=0)
    ex = _MeshExchange(pack, cw8)
    p = dict(norm_mix_w=norm_mix_w, b_attn=b_attn, lb=_lb_fwd(lb_logits), hg_norm_w=hg_norm_w, sinks=sinks,
             norm_ffn_w=norm_ffn_w, conv_b=conv_b, final_norm_w=final_norm_w.reshape(1, D))
    loss_cols, dx, g = _local_step(x[0], loss_target[0], p, ex)
    conv_w8 = ex.conv_w8

    small = _pack_small(dict(loss=loss_cols, norm_mix_w=g["norm_mix_w"], b_attn=g["b_attn"], lb_logits=g["lb"],
                             hg_norm_w=g["hg_norm_w"], sinks=g["sinks8"], norm_ffn_w=g["norm_ffn_w"],
                             conv_w=g["conv_w8"][:3], conv_b=g["conv_b"], final_norm_w=g["final_norm_w"]))
    parts_ffn, got_ffn = _send_wait(FFN_W, ex.sent, dx, name="send_ffn_wait")
    late = (0, 4)
    gs = [g["g_in_t"], g["g_out"]]
    *theirs, sall = _exchange_halves(late, gs, small, name="exchange_halves_late")
    parts_late = _chip_partial(late, gs, theirs, name="chip_partial_late")
    got_late = _send_chip_partials(late, parts_late, name="send_late")
    parts = [parts_late[0], *parts_ffn, parts_late[1]]
    got = [got_late[0], *got_ffn, got_late[1]]
    shards = _exchange_reduced(_chip_reduce(parts, got))
    big = {}
    for n, gw, w, m, v, tr in (("w_in", shards[0], w_in, m_w_in, v_w_in, True), ("w_gate", shards[1], w_gate, m_w_gate, v_w_gate, True),
                               ("w_up", shards[2], w_up, m_w_up, v_w_up, True), ("w_down", shards[3], w_down, m_w_down, v_w_down, False),
                               ("w_out", shards[4], w_out, m_w_out, v_w_out, False)):
        view = (lambda a: a[0].T) if tr else (lambda a: a[0])
        back = (lambda a: a.T[None]) if tr else (lambda a: a[None])
        d_, m_, v_ = _adamw(view(w), gw, view(m), view(v), name="adamw_" + n)
        big[n] = (back(gw), back(d_), back(m_), back(v_))

    def place(a):
        return lax.dynamic_update_slice(jnp.zeros((3, D_FF), F32), a[0], (0, q * ccols))

    def small_pack(ws, cw):
        nm, ba, lbl, hg, sk, nf, cb, fn = ws
        return _pack_small(dict(norm_mix_w=nm, b_attn=ba, lb_logits=lbl, hg_norm_w=hg,
                                sinks=jnp.broadcast_to(sk.reshape(ATT_HEADS, 1), (ATT_HEADS, LANES)), norm_ffn_w=nf,
                                conv_w=cw, conv_b=cb, final_norm_w=fn))

    wp = small_pack((norm_mix_w, b_attn, lb_logits, hg_norm_w, sinks, norm_ffn_w, conv_b, final_norm_w), conv_w8[:3])
    mp = small_pack((m_norm_mix_w, m_b_attn, m_lb_logits, m_hg_norm_w, m_sinks, m_norm_ffn_w, m_conv_b, m_final_norm_w),
                    place(m_conv_w))
    vp = small_pack((v_norm_mix_w, v_b_attn, v_lb_logits, v_hg_norm_w, v_sinks, v_norm_ffn_w, v_conv_b, v_final_norm_w),
                    place(v_conv_w))
    outs = _small_update(sall, wp, mp, vp)
    loss = outs[4][0, 0]

    def small_out(pk, n, ref):
        if n == "sinks":
            return pk[SMALL_OFF[n]:SMALL_OFF[n] + ATT_HEADS, 0].reshape(ref.shape)
        if n == "conv_w":
            full = _unpack_small(pk, n, (3, D_FF))
            return lax.dynamic_slice(full, (0, q * ccols), (3, ccols))[None]
        return _unpack_small(pk, n, ref.shape)

    refs = dict(norm_mix_w=norm_mix_w, b_attn=b_attn, lb_logits=lb_logits, hg_norm_w=hg_norm_w, sinks=sinks,
                norm_ffn_w=norm_ffn_w, conv_w=conv_w, conv_b=conv_b, final_norm_w=final_norm_w)
    order = ("norm_mix_w", "w_in", "b_attn", "lb_logits", "hg_norm_w", "sinks", "w_out", "norm_ffn_w", "w_gate", "w_up",
             "conv_w", "conv_b", "w_down", "final_norm_w")
    res = [loss, dx[None]]
    for k in range(4):
        for n in order:
            res.append(big[n][k] if n in big else small_out(outs[k], n, refs[n]))
    return tuple(res)
```

```python
import functools
import math

import jax
import jax.numpy as jnp
from jax import lax
from jax.experimental import pallas as pl
from jax.experimental.pallas import tpu as pltpu

F32 = jnp.float32
BF16 = jnp.bfloat16

D_MODEL = 1024
HG_HEADS = 4
HG_DK = 128
HG_W = HG_HEADS * HG_DK
HG_CHUNK = 64
HG_SUB = 16
ATT_HEADS = 8
ATT_KV = 2
ATT_GROUP = ATT_HEADS // ATT_KV
ATT_HD = 64
ATT_BLOCK = 128
ATT_Q_W = ATT_HEADS * ATT_HD
ATT_KV_W = ATT_KV * ATT_HD
ATT_COLS = ATT_Q_W + 2 * ATT_KV_W
IN_COLS = 4 * HG_W + ATT_COLS
D_FF = 2816
EPS = 1e-6
ADAM_LR, ADAM_B1, ADAM_B2, ADAM_EPS, ADAM_WD, ADAM_STEP = 0.001, 0.9, 0.999, 1e-08, 0.01, 10
NEG = -1e30

V7X_VMEM_BYTES = 64 * 1024 * 1024
VMEM_LIMIT = 48 * 1024 * 1024
SUBLANES = 8

N_CHIPS = 4


def _cp(sem=None, **kw):
    return pltpu.CompilerParams(dimension_semantics=sem, vmem_limit_bytes=VMEM_LIMIT, **kw)


def _sds(shape, dtype):
    return jax.ShapeDtypeStruct(shape, dtype)


def _wspec(w):
    arr, rows, blk = w
    return pl.BlockSpec((rows, arr.shape[1]), lambda i: (blk, 0))


def _mm_nt(a, w, *, splits, out_dtype, name, residual=None, tm=512):
    M, K = a.shape
    N = w[1]
    tm = min(tm, M)
    assert sum(splits) == N and M % tm == 0
    offs = [sum(splits[:i]) for i in range(len(splits))]

    def body(*refs):
        a_ref, w_ref = refs[0], refs[1]
        outs = refs[2 + (residual is not None):]
        acc = lax.dot_general(a_ref[...], w_ref[...], (((1,), (1,)), ((), ())), preferred_element_type=F32)
        if residual is not None:
            acc = acc + refs[2][...]
        for o_ref, c0, n in zip(outs, offs, splits):
            o_ref[...] = acc[:, c0:c0 + n].astype(out_dtype)

    in_specs = [pl.BlockSpec((tm, K), lambda i: (i, 0)), _wspec(w)]
    args = [a, w[0]]
    if residual is not None:
        assert len(splits) == 1
        in_specs.append(pl.BlockSpec((tm, N), lambda i: (i, 0)))
        args.append(residual)
    outs = pl.pallas_call(
        body, name=name, grid=(M // tm,), in_specs=in_specs,
        out_specs=[pl.BlockSpec((tm, n), lambda i: (i, 0)) for n in splits],
        out_shape=[_sds((M, n), out_dtype) for n in splits],
        compiler_params=_cp(("parallel",)),
    )(*args)
    return outs


def _mm_nn(pieces, ws, *, out_dtype, name, residual=None, tm=512):
    M = pieces[0][0].shape[0]
    K = ws[0][0].shape[1]
    tm = min(tm, M)
    flat = [p for grp in pieces for p in grp]
    n_p = len(flat)

    def body(*refs):
        p_refs = refs[:n_p]
        w_refs = refs[n_p:n_p + len(ws)]
        o_ref = refs[-1]
        acc = None if residual is None else refs[n_p + len(ws)][...]
        k = 0
        for gi, grp in enumerate(pieces):
            c0 = 0
            for p in grp:
                n = p.shape[1]
                t = jnp.dot(p_refs[k][...], w_refs[gi][c0:c0 + n, :], preferred_element_type=F32)
                acc = t if acc is None else acc + t
                c0 += n
                k += 1
        o_ref[...] = acc.astype(out_dtype)

    in_specs = [pl.BlockSpec((tm, p.shape[1]), lambda i: (i, 0)) for p in flat]
    in_specs += [_wspec(w) for w in ws]
    args = [*flat, *[w[0] for w in ws]]
    if residual is not None:
        in_specs.append(pl.BlockSpec((tm, K), lambda i: (i, 0)))
        args.append(residual)
    return pl.pallas_call(
        body, name=name, grid=(M // tm,), in_specs=in_specs,
        out_specs=pl.BlockSpec((tm, K), lambda i: (i, 0)),
        out_shape=_sds((M, K), out_dtype),
        compiler_params=_cp(("parallel",)),
    )(*args)


def _mm_tn(pieces, x, *, name, tt=512):
    M, K = x.shape
    tt = min(tt, M)
    ns = [p.shape[1] for p in pieces]
    offs = [sum(ns[:i]) for i in range(len(ns))]
    N = sum(ns)
    n_p = len(pieces)

    def body(*refs):
        p_refs = refs[:n_p]
        x_ref = refs[n_p]
        o_ref = refs[n_p + 1]

        @pl.when(pl.program_id(0) == 0)
        def _():
            o_ref[...] = jnp.zeros_like(o_ref)

        xv = x_ref[...]
        for p_ref, c0, n in zip(p_refs, offs, ns):
            o_ref[c0:c0 + n, :] += lax.dot_general(p_ref[...], xv, (((0,), (0,)), ((), ())),
                                                    preferred_element_type=F32)

    in_specs = [pl.BlockSpec((tt, n), lambda i: (i, 0)) for n in ns]
    in_specs.append(pl.BlockSpec((tt, K), lambda i: (i, 0)))
    return pl.pallas_call(
        body, name=name, grid=(M // tt,), in_specs=in_specs,
        out_specs=pl.BlockSpec((N, K), lambda i: (0, 0)),
        out_shape=_sds((N, K), F32),
        compiler_params=_cp(("arbitrary",)),
    )(*pieces, x)


def _rms_fwd(xf, w):
    inv = lax.rsqrt(jnp.mean(xf * xf, axis=-1, keepdims=True) + EPS)
    return xf * inv * w


def _rms_bwd(xf, w, dy):
    inv = lax.rsqrt(jnp.mean(xf * xf, axis=-1, keepdims=True) + EPS)
    xhat = xf * inv
    dxhat = dy * w
    dx = inv * (dxhat - xhat * jnp.mean(dxhat * xhat, axis=-1, keepdims=True))
    dw = jnp.sum(dy * xhat, axis=0, keepdims=True)
    return dx, dw


def _sigmoid(x):
    return 1.0 / (1.0 + jnp.exp(-x))


def _rowwise(fn, row_ins, bc_ins, row_outs, acc_outs, *, name, tm=256, after=None):
    M = row_outs[0].shape[0] if row_outs else row_ins[0][0].shape[0]
    assert M % tm == 0 and tm % SUBLANES == 0, (name, M, tm)
    n_r, n_b, n_o, n_a = len(row_ins), len(bc_ins), len(row_outs), len(acc_outs)
    n_after = 0 if after is None else 1

    def body(*refs):
        refs = refs[n_after:]
        ins = [r[...] for r in refs[:n_r + n_b]]
        o_refs = refs[n_r + n_b:n_r + n_b + n_o]
        a_refs = refs[n_r + n_b + n_o:]
        res = fn(*ins)
        for o_ref, val in zip(o_refs, res[:n_o]):
            o_ref[...] = val.astype(o_ref.dtype)
        if n_a:
            @pl.when(pl.program_id(0) == 0)
            def _():
                for a_ref in a_refs:
                    a_ref[...] = jnp.zeros_like(a_ref)
            for a_ref, val in zip(a_refs, res[n_o:]):
                a_ref[...] += val

    in_specs = [pl.BlockSpec((tm, cw), functools.partial(lambda i, cb, r0: (i + r0, cb), cb=cb, r0=r0))
                for (_, cw, cb, r0) in row_ins]
    in_specs += [pl.BlockSpec(b.shape, lambda i: (0, 0)) for b in bc_ins]
    out_specs = [pl.BlockSpec((tm, s.shape[1]), lambda i: (i, 0)) for s in row_outs]
    out_specs += [pl.BlockSpec(s.shape, lambda i: (0, 0)) for s in acc_outs]
    if n_after:
        in_specs = [pl.BlockSpec(memory_space=pl.ANY)] + in_specs
    return pl.pallas_call(
        body, name=name, grid=(M // tm,), in_specs=in_specs, out_specs=out_specs,
        out_shape=list(row_outs) + list(acc_outs),
        compiler_params=_cp(("arbitrary",) if n_a else ("parallel",)),
    )(*([after] if n_after else []), *[r[0] for r in row_ins], *bc_ins)


def _full(a, first_row_block=0):
    return (a, a.shape[1], 0, first_row_block)


def _conv_rows(ext, w_ref_val, lo):
    s1 = pltpu.roll(ext, 1, 0)
    s2 = pltpu.roll(ext, 2, 0)
    y = w_ref_val[0:1, :] * s2 + w_ref_val[1:2, :] * s1 + w_ref_val[2:3, :] * ext
    return y[SUBLANES:, :]


def _convact_fwd(gp, up, conv_w8, conv_b, *, name, tr=128, tc=1408):
    T, C = gp.shape
    tr = min(tr, T)
    hb = tr // SUBLANES

    def body(gp_ref, gph_ref, up_ref, w_ref, b_ref, act_ref):
        i = pl.program_id(1)
        halo = jnp.where(i > 0, gph_ref[...], 0.0)
        ext = jnp.concatenate([halo, gp_ref[...]], axis=0)
        gate = _conv_rows(ext, w_ref[...], 0) + b_ref[...]
        act_ref[...] = (gate * _sigmoid(gate) * up_ref[...]).astype(act_ref.dtype)

    return pl.pallas_call(
        body, name=name, grid=(C // tc, T // tr),
        in_specs=[pl.BlockSpec((tr, tc), lambda j, i: (i, j)),
                  pl.BlockSpec((SUBLANES, tc), lambda j, i: (jnp.maximum(i * hb - 1, 0), j)),
                  pl.BlockSpec((tr, tc), lambda j, i: (i, j)),
                  pl.BlockSpec((SUBLANES, tc), lambda j, i: (0, j)),
                  pl.BlockSpec((1, tc), lambda j, i: (0, j))],
        out_specs=pl.BlockSpec((tr, tc), lambda j, i: (i, j)),
        out_shape=_sds((T, C), BF16),
        compiler_params=_cp(("parallel", "parallel")),
    )(gp, gp, up, conv_w8, conv_b)


def _convact_bwd(gp, up, dact, conv_w8, conv_b, *, name, tr=128, tc=1408):
    T, C = gp.shape
    tr = min(tr, T)
    hb = tr // SUBLANES
    nr = T // tr

    def body(gp_ref, gpp_ref, gpn_ref, up_ref, upn_ref, da_ref, dan_ref, w_ref, b_ref,
             dgp_ref, dup_ref, dw_ref, db_ref):
        i = pl.program_id(1)
        w = w_ref[...]
        prev = jnp.where(i > 0, gpp_ref[...], 0.0)
        last = i == nr - 1
        gp_ext = jnp.concatenate([prev, gp_ref[...], gpn_ref[...]], axis=0)
        gate = _conv_rows(gp_ext, w, 0) + b_ref[...]
        up_e = jnp.concatenate([up_ref[...], upn_ref[...]], axis=0)
        da_e = jnp.concatenate([da_ref[...], dan_ref[...]], axis=0)
        row = lax.broadcasted_iota(jnp.int32, gate.shape, 0)
        valid = jnp.logical_or(row < tr, jnp.logical_not(last))
        sg = _sigmoid(gate)
        silu = gate * sg
        dgate = jnp.where(valid, da_e * up_e * (sg * (1.0 + gate * (1.0 - sg))), 0.0)
        dup_ref[...] = (da_e[:tr] * silu[:tr]).astype(dup_ref.dtype)
        n = tr + SUBLANES
        g1 = pltpu.roll(dgate, n - 1, 0)
        g2 = pltpu.roll(dgate, n - 2, 0)
        dgp = w[2:3, :] * dgate + w[1:2, :] * g1 + w[0:1, :] * g2
        dgp_ref[...] = dgp[:tr].astype(dgp_ref.dtype)
        gpc = gp_ref[...]
        dw0 = jnp.sum(gpc * g2[:tr], axis=0, keepdims=True)
        dw1 = jnp.sum(gpc * g1[:tr], axis=0, keepdims=True)
        dw2 = jnp.sum(gpc * dgate[:tr], axis=0, keepdims=True)
        dbv = jnp.sum(dgate[:tr], axis=0, keepdims=True)
        z = jnp.zeros((SUBLANES - 3, gpc.shape[1]), F32)

        @pl.when(i == 0)
        def _():
            dw_ref[...] = jnp.zeros_like(dw_ref)
            db_ref[...] = jnp.zeros_like(db_ref)

        dw_ref[...] += jnp.concatenate([dw0, dw1, dw2, z], axis=0)
        db_ref[...] += dbv

    cur = pl.BlockSpec((tr, tc), lambda j, i: (i, j))
    prv = pl.BlockSpec((SUBLANES, tc), lambda j, i: (jnp.maximum(i * hb - 1, 0), j))
    nxt = pl.BlockSpec((SUBLANES, tc), lambda j, i: (jnp.minimum((i + 1) * hb, T // SUBLANES - 1), j))
    return pl.pallas_call(
        body, name=name, grid=(C // tc, nr),
        in_specs=[cur, prv, nxt, cur, nxt, cur, nxt,
                  pl.BlockSpec((SUBLANES, tc), lambda j, i: (0, j)),
                  pl.BlockSpec((1, tc), lambda j, i: (0, j))],
        out_specs=[cur, cur,
                   pl.BlockSpec((SUBLANES, tc), lambda j, i: (0, j)),
                   pl.BlockSpec((1, tc), lambda j, i: (0, j))],
        out_shape=[_sds((T, C), BF16), _sds((T, C), BF16), _sds((SUBLANES, C), F32), _sds((1, C), F32)],
        compiler_params=_cp(("parallel", "arbitrary")),
    )(gp, gp, gp, up, up, dact, dact, conv_w8, conv_b)


def _cumsum_rows(x):
    n = x.shape[0]
    row = lax.broadcasted_iota(jnp.int32, x.shape, 0)
    s = 1
    while s < n:
        x = x + jnp.where(row >= s, pltpu.roll(x, s, 0), 0.0)
        s *= 2
    return x


def _rcumsum_rows(x):
    n = x.shape[0]
    row = lax.broadcasted_iota(jnp.int32, x.shape, 0)
    s = 1
    while s < n:
        x = x + jnp.where(row < n - s, pltpu.roll(x, n - s, 0), 0.0)
        s *= 2
    return x


def _dot_nt(a, b):
    return lax.dot_general(a.astype(BF16), b.astype(BF16), (((1,), (1,)), ((), ())), preferred_element_type=F32)


def _dot_tn(a, b):
    return lax.dot_general(a.astype(BF16), b.astype(BF16), (((0,), (0,)), ((), ())), preferred_element_type=F32)


def _dot_nn(a, b):
    return jnp.dot(a.astype(BF16), b.astype(BF16), preferred_element_type=F32)


def _hg_gates(hq, hf, lbv):
    sig = _sigmoid(hf)
    f = lbv + (1.0 - lbv) * sig
    return sig, f, jnp.log(f), 1.0 - f, hq * (HG_DK ** -0.5)


def _hg_sel_rows(ref, sp):
    return jnp.concatenate(
        [jnp.broadcast_to(ref[pl.ds(HG_SUB * i + sp, 1), :], (HG_SUB, HG_DK)) for i in range(HG_CHUNK // HG_SUB)], axis=0)


def _hg_masks():
    C = HG_CHUNK
    row = lax.broadcasted_iota(jnp.int32, (C, C), 0)
    col = lax.broadcasted_iota(jnp.int32, (C, C), 1)
    d = col - (row // HG_SUB) * HG_SUB
    tmod = row % HG_SUB
    diag_valid = jnp.logical_and(d >= 0, d <= tmod)
    return row, col, d, diag_valid


def _hg_scores(q, k, b, b_sc, k_sc):
    C, S = HG_CHUNK, HG_SUB
    row, col, d, diag_valid = _hg_masks()
    blocks = [jnp.zeros((S, C), F32)]
    for i in range(1, C // S):
        r = b_sc[pl.ds(S * i - 1, 1), :]
        qi = q[S * i:S * (i + 1)] * jnp.exp(b[S * i:S * (i + 1)] - r)
        kk = k * jnp.exp(jnp.minimum(r - b, 0.0))
        blocks.append(_dot_nt(qi, kk))
    a_off = jnp.where(col < (row // S) * S, jnp.concatenate(blocks, axis=0), 0.0)
    a_d = jnp.zeros((C, C), F32)
    for sp in range(S):
        bs = _hg_sel_rows(b_sc, sp)
        ks = _hg_sel_rows(k_sc, sp)
        e = jnp.exp(jnp.minimum(b - bs, 0.0))
        colv = jnp.sum(q * ks * e, axis=-1, keepdims=True)
        a_d = jnp.where(d == sp, colv, a_d)
    return a_off + jnp.where(diag_valid, a_d, 0.0)


def _hgrn_fwd(hq, hf, hi, lb, *, name):
    T = hq.shape[0]
    C, H, K = HG_CHUNK, HG_HEADS, HG_DK
    NC = T // C

    def body(hq_ref, hf_ref, hi_ref, lb_ref, o_ref, st_ref, s_sc, b_sc, k_sc):
        @pl.when(pl.program_id(0) == 0)
        def _():
            s_sc[...] = jnp.zeros_like(s_sc)

        st_all = s_sc[...]
        st_ref[0] = st_all
        outs, news = [], []
        for h in range(H):
            sl = slice(K * h, K * (h + 1))
            _, _, g, k, q = _hg_gates(hq_ref[:, sl], hf_ref[:, sl], lb_ref[:, sl])
            v = hi_ref[:, sl]
            b = _cumsum_rows(g)
            b_sc[h] = b
            k_sc[h] = k
            st0 = st_all[:, sl]
            bc = b_sc[h, pl.ds(C - 1, 1), :]
            a = _hg_scores(q, k, b, b_sc.at[h], k_sc.at[h])
            outs.append(_dot_nn(a, v) + _dot_nt(q * jnp.exp(b), st0))
            news.append(st0 * jnp.exp(bc) + _dot_tn(v, k * jnp.exp(bc - b)))
        o_ref[...] = jnp.concatenate(outs, axis=1)
        s_sc[...] = jnp.concatenate(news, axis=1)

    blk = pl.BlockSpec((C, H * K), lambda c: (c, 0))
    return pl.pallas_call(
        body, name=name, grid=(NC,),
        in_specs=[blk, blk, blk, pl.BlockSpec((1, H * K), lambda c: (0, 0))],
        out_specs=[blk, pl.BlockSpec((1, K, H * K), lambda c: (c, 0, 0))],
        out_shape=[_sds((T, H * K), F32), _sds((NC, K, H * K), F32)],
        scratch_shapes=[pltpu.VMEM((K, H * K), F32), pltpu.VMEM((H, C, K), F32), pltpu.VMEM((H, C, K), F32)],
        compiler_params=_cp(("arbitrary",)),
    )(hq, hf, hi, lb)


def _hgrn_bwd(hq, hf, hi, lb, states, do, *, name):
    T = hq.shape[0]
    C, H, K, S = HG_CHUNK, HG_HEADS, HG_DK, HG_SUB
    NC = T // C

    def one_head(hq_v, hf_v, v, lbv, st0, dst1, dout, b_sc, k_sc):
        sig, f, g, k, q = _hg_gates(hq_v, hf_v, lbv)
        b = _cumsum_rows(g)
        b_sc[...] = b
        k_sc[...] = k
        bc = b_sc[pl.ds(C - 1, 1), :]
        ebc = jnp.exp(bc)
        eb = jnp.exp(b)
        ekb = jnp.exp(bc - b)
        qt = q * eb
        kb = k * ekb
        row, col, d, diag_valid = _hg_masks()
        da = jnp.where(col <= row, _dot_nt(dout, v), 0.0)
        dqt = _dot_nn(dout, st0)
        dkb = _dot_nn(v, dst1)
        new_ds = _dot_tn(dout, qt) + dst1 * ebc
        dq = dqt * eb
        dk = dkb * ekb
        a_blocks = [jnp.zeros((S, C), F32)]
        dq_blocks = [jnp.zeros((S, K), F32)]
        for i in range(1, C // S):
            r = b_sc[pl.ds(S * i - 1, 1), :]
            eq = jnp.exp(b[S * i:S * (i + 1)] - r)
            ek = jnp.exp(jnp.minimum(r - b, 0.0))
            qi = q[S * i:S * (i + 1)] * eq
            kk = k * ek
            a_blocks.append(_dot_nt(qi, kk))
            dai = jnp.where(col[S * i:S * (i + 1)] < S * i, da[S * i:S * (i + 1)], 0.0)
            dq_blocks.append(_dot_nn(dai, kk) * eq)
            dk = dk + _dot_tn(dai, qi) * ek
        dq = dq + jnp.concatenate(dq_blocks, axis=0)
        a_off = jnp.where(col < (row // S) * S, jnp.concatenate(a_blocks, axis=0), 0.0)
        same_blk = (row // S == col // S).astype(BF16)
        tmod = (lax.broadcasted_iota(jnp.int32, (C, K), 0)) % S
        a_d = jnp.zeros((C, C), F32)
        for sp in range(S):
            bs = _hg_sel_rows(b_sc, sp)
            ks = _hg_sel_rows(k_sc, sp)
            e = jnp.where(tmod >= sp, jnp.exp(jnp.minimum(b - bs, 0.0)), 0.0)
            eks = e * ks
            a_d = jnp.where(d == sp, jnp.sum(q * eks, axis=-1, keepdims=True), a_d)
            dacol = jnp.sum(jnp.where(d == sp, da, 0.0), axis=-1, keepdims=True)
            dq = dq + dacol * eks
            blk_sum = jnp.dot(same_blk, (dacol * e * q).astype(BF16), preferred_element_type=F32)
            dk = dk + jnp.where(tmod == sp, blk_sum, 0.0)
        a = a_off + jnp.where(diag_valid, a_d, 0.0)
        dv = _dot_tn(a, dout) + _dot_nt(kb, dst1)
        extra =jnp.sum(dkb * kb, axis=0, keepdims=True) + ebc * jnp.sum(st0 * dst1, axis=0, keepdims=True)
        rowk = lax.broadcasted_iota(jnp.int32, (C, K), 0)
        db = q * dq - k * dk + jnp.where(rowk == C - 1, extra, 0.0)
        dg = _rcumsum_rows(db)
        df = dg / f - dk
        return (dq * (K ** -0.5), df * (1.0 - lbv) * sig * (1.0 - sig), dv,
                jnp.sum(df * (1.0 - sig), axis=0, keepdims=True), new_ds)

    def body(hq_ref, hf_ref, hi_ref, lb_ref, st_ref, do_ref, dq_ref, dhf_ref, dv_ref, dlb_ref, ds_sc, b_sc, k_sc):
        @pl.when(pl.program_id(0) == 0)
        def _():
            ds_sc[...] = jnp.zeros_like(ds_sc)
            dlb_ref[...] = jnp.zeros_like(dlb_ref)

        st_all = st_ref[0]
        ds_all = ds_sc[...]
        res = []
        for h in range(H):
            sl = slice(K * h, K * (h + 1))
            res.append(one_head(hq_ref[:, sl], hf_ref[:, sl], hi_ref[:, sl], lb_ref[:, sl], st_all[:, sl], ds_all[:, sl],
                                do_ref[:, sl], b_sc.at[h], k_sc.at[h]))
        cat = lambda j: jnp.concatenate([r[j] for r in res], axis=1)
        dq_ref[...] = cat(0).astype(dq_ref.dtype)
        dhf_ref[...] = cat(1).astype(dhf_ref.dtype)
        dv_ref[...] = cat(2).astype(dv_ref.dtype)
        dlb_ref[...] += cat(3)
        ds_sc[...] = cat(4)

    blk = pl.BlockSpec((C, H * K), lambda c: (NC - 1 - c, 0))
    par = pl.BlockSpec((1, H * K), lambda c: (0, 0))
    return pl.pallas_call(
        body, name=name, grid=(NC,),
        in_specs=[blk, blk, blk, par, pl.BlockSpec((1, K, H * K), lambda c: (NC - 1 - c, 0, 0)), blk],
        out_specs=[blk, blk, blk, par],
        out_shape=[_sds((T, H * K), BF16)] * 3 + [_sds((1, H * K), F32)],
        scratch_shapes=[pltpu.VMEM((K, H * K), F32), pltpu.VMEM((H, C, K), F32), pltpu.VMEM((H, C, K), F32)],
        compiler_params=_cp(("arbitrary",)),
    )(hq, hf, hi, lb, states, do)


def _att_valid(n):
    R, B = ATT_GROUP * ATT_BLOCK, ATT_BLOCK
    j = lax.broadcasted_iota(jnp.int32, (2 * B, R), 0)
    t = lax.broadcasted_iota(jnp.int32, (2 * B, R), 1) % B
    dist = t + B - j
    first_key = jnp.where(n > 0, 0, B)
    return jnp.logical_and(jnp.logical_and(dist >= 0, dist < B), j >= first_key)


def _att_load(cur_ref, prev_ref, ba_ref, kv):
    hd = ATT_HD
    def cols(ref, c0):
        return ref[:, c0:c0 + hd] + ba_ref[:, c0:c0 + hd]
    qs = jnp.concatenate([cols(cur_ref, hd * (ATT_GROUP * kv + g)) for g in range(ATT_GROUP)], axis=0)
    kc = jnp.concatenate([cols(prev_ref, ATT_Q_W + hd * kv), cols(cur_ref, ATT_Q_W + hd * kv)], axis=0)
    vc = jnp.concatenate([cols(prev_ref, ATT_Q_W + ATT_KV_W + hd * kv), cols(cur_ref, ATT_Q_W + ATT_KV_W + hd * kv)], axis=0)
    return qs, kc, vc


def _att_probs(qs, kc, valid, sink_ref, kv):
    scale = 1.0 / math.sqrt(ATT_HD)
    s = jnp.where(valid, _dot_nt(kc, qs) * scale, NEG)
    sink = jnp.concatenate([jnp.full((1, ATT_BLOCK), sink_ref[0, ATT_GROUP * kv + g], F32) for g in range(ATT_GROUP)], axis=1)
    m = jnp.maximum(jnp.max(s, axis=0, keepdims=True), sink)
    p = jnp.exp(s - m)
    ps = jnp.exp(sink - m)
    inv = 1.0 / (jnp.sum(p, axis=0, keepdims=True) + ps)
    return p * inv, ps * inv


def _attn_fwd(att, b_attn, sinks, *, name):
    T = att.shape[0]
    B = ATT_BLOCK
    NB = T // B

    def body(sink_ref, cur_ref, prev_ref, ba_ref, o_ref):
        valid = _att_valid(pl.program_id(0))
        for kv in range(ATT_KV):
            qs, kc, vc = _att_load(cur_ref, prev_ref, ba_ref, kv)
            prob, _ = _att_probs(qs, kc, valid, sink_ref, kv)
            o = _dot_tn(prob, vc)
            for g in range(ATT_GROUP):
                c0 = ATT_HD * (ATT_GROUP * kv + g)
                o_ref[:, c0:c0 + ATT_HD] = o[B * g:B * (g + 1)]

    return pl.pallas_call(
        body, name=name, grid=(NB,),
        in_specs=[pl.BlockSpec(memory_space=pltpu.SMEM),
                  pl.BlockSpec((B, ATT_COLS), lambda n: (n, 0)),
                  pl.BlockSpec((B, ATT_COLS), lambda n: (jnp.maximum(n - 1, 0), 0)),
                  pl.BlockSpec((1, ATT_COLS), lambda n: (0, 0))],
        out_specs=pl.BlockSpec((B, ATT_Q_W), lambda n: (n, 0)),
        out_shape=_sds((T, ATT_Q_W), F32),
        compiler_params=_cp(("parallel",)),
    )(sinks, att, att, b_attn)


def _attn_bwd(att, b_attn, sinks, dmix, *, name):
    T = att.shape[0]
    B, hd = ATT_BLOCK, ATT_HD
    NB = T // B
    scale = 1.0 / math.sqrt(hd)

    def body(sink_ref, cur_ref, prev_ref, ba_ref, do_ref, daq_ref, dakv_ref, dsink_ref, dbq_ref, dbkv_ref,
             carry_sc, cprev_sc, ccur_sc):
        n = pl.program_id(0)

        @pl.when(n == 0)
        def _():
            carry_sc[...] = jnp.zeros_like(carry_sc)
            dsink_ref[...] = jnp.zeros_like(dsink_ref)
            dbq_ref[...] = jnp.zeros_like(dbq_ref)
            dbkv_ref[...] = jnp.zeros_like(dbkv_ref)

        @pl.when(n < NB)
        def _():
            valid = _att_valid(n)
            hrow = lax.broadcasted_iota(jnp.int32, (SUBLANES, 128), 0)
            dsink = jnp.zeros((SUBLANES, 128), F32)
            for kv in range(ATT_KV):
                qs, kc, vc = _att_load(cur_ref, prev_ref, ba_ref, kv)
                prob, psink = _att_probs(qs, kc, valid, sink_ref, kv)
                dout = jnp.concatenate(
                    [do_ref[:, hd * (ATT_GROUP * kv + g):hd * (ATT_GROUP * kv + g + 1)] for g in range(ATT_GROUP)], axis=0)
                dp = _dot_nt(vc, dout)
                delta = jnp.sum(prob * dp, axis=0, keepdims=True)
                dsc = prob * (dp - delta) * scale
                dq = _dot_tn(dsc, kc)
                dk = _dot_nn(dsc, qs)
                dvv = _dot_nn(prob, dout)
                dsk = psink * delta
                for g in range(ATT_GROUP):
                    h = ATT_GROUP * kv + g
                    daq_ref[:, hd * h:hd * (h + 1)] = dq[B * g:B * (g + 1)].astype(daq_ref.dtype)
                    tot = jnp.sum(dsk[:, B * g:B * (g + 1)], axis=1, keepdims=True)
                    dsink = dsink - jnp.where(hrow == h, tot, 0.0)
                cprev_sc[:, hd * kv:hd * (kv + 1)] = dk[:B]
                ccur_sc[:, hd * kv:hd * (kv + 1)] = dk[B:]
                cprev_sc[:, ATT_KV_W + hd * kv:ATT_KV_W + hd * (kv + 1)] = dvv[:B]
                ccur_sc[:, ATT_KV_W + hd * kv:ATT_KV_W + hd * (kv + 1)] = dvv[B:]
            dsink_ref[...] += dsink
            dbq_ref[...] += jnp.sum(daq_ref[...].astype(F32), axis=0, keepdims=True)
            done = carry_sc[...] + cprev_sc[...]
            dakv_ref[...] = done.astype(dakv_ref.dtype)
            dbkv_ref[...] += jnp.sum(done.astype(dakv_ref.dtype).astype(F32), axis=0, keepdims=True)
            carry_sc[...] = ccur_sc[...]

        @pl.when(n == NB)
        def _():
            done = carry_sc[...]
            dakv_ref[...] = done.astype(dakv_ref.dtype)
            dbkv_ref[...] += jnp.sum(done.astype(dakv_ref.dtype).astype(F32), axis=0, keepdims=True)

    cl = lambda n: jnp.minimum(n, NB - 1)
    return pl.pallas_call(
        body, name=name, grid=(NB + 1,),
        in_specs=[pl.BlockSpec(memory_space=pltpu.SMEM),
                  pl.BlockSpec((B, ATT_COLS), lambda n: (cl(n), 0)),
                  pl.BlockSpec((B, ATT_COLS), lambda n: (jnp.maximum(cl(n) - 1, 0), 0)),
                  pl.BlockSpec((1, ATT_COLS), lambda n: (0, 0)),
                  pl.BlockSpec((B, ATT_Q_W), lambda n: (cl(n), 1))],
        out_specs=[pl.BlockSpec((B, ATT_Q_W), lambda n: (cl(n), 0)),
                   pl.BlockSpec((B, 2 * ATT_KV_W), lambda n: (jnp.maximum(n - 1, 0), 0)),
                   pl.BlockSpec((SUBLANES, 128), lambda n: (0, 0)),
                   pl.BlockSpec((1, ATT_Q_W), lambda n: (0, 0)),
                   pl.BlockSpec((1, 2 * ATT_KV_W), lambda n: (0, 0))],
        out_shape=[_sds((T, ATT_Q_W), BF16), _sds((T, 2 * ATT_KV_W), BF16), _sds((SUBLANES, 128), F32),
                   _sds((1, ATT_Q_W), F32), _sds((1, 2 * ATT_KV_W), F32)],
        scratch_shapes=[pltpu.VMEM((B, 2 * ATT_KV_W), F32)] * 3,
        compiler_params=_cp(("arbitrary",)),
    )(sinks, att, att, b_attn, dmix)


def _silu_and_grad(x):
    sg = _sigmoid(x)
    return x * sg, sg * (1.0 + x * (1.0 - sg))


def _mix_fwd_fn(o_raw, hg, o_att, hgw):
    outs = []
    for h in range(HG_HEADS):
        sl = slice(HG_DK * h, HG_DK * (h + 1))
        silu, _ = _silu_and_grad(hg[:, sl])
        outs.append(_rms_fwd(o_raw[:, sl], hgw) * silu)
    outs.append(o_att)
    return (jnp.concatenate(outs, axis=1),)


def _mix_bwd_fn(o_raw, hg, dmix, hgw):
    dos, dhgs = [], []
    dw = jnp.zeros((1, HG_DK), F32)
    for h in range(HG_HEADS):
        sl = slice(HG_DK * h, HG_DK * (h + 1))
        silu, dsilu = _silu_and_grad(hg[:, sl])
        dy = dmix[:, sl]
        dhgs.append(dy * _rms_fwd(o_raw[:, sl], hgw) * dsilu)
        dx, dwh = _rms_bwd(o_raw[:, sl], hgw, dy * silu)
        dos.append(dx)
        dw = dw + dwh
    return jnp.concatenate(dos, axis=1), jnp.concatenate(dhgs, axis=1), dw


def _final_fn(h2, tgt, wf):
    d = h2.shape[1]
    err = _rms_fwd(h2, wf) - tgt
    loss_cols = (0.5 / d) * jnp.sum(err * err, axis=0, keepdims=True)
    dh2, dwf = _rms_bwd(h2, wf, err * (1.0 / d))
    return dh2, dh2, loss_cols, dwf


class _NoExchange:
    def __init__(self, weights):
        self.weights = weights

    def start(self):
        return None

    def w_in(self, after):
        return self.weights["w_in_t"]

    def rest(self, after):
        return self.weights

    def ffn_grads(self, gs):
        return None


def _local_step(x, tgt, p, ex):
    T, D = x.shape
    row = lambda n, dt: _sds((T, n), dt)
    acc = lambda n: _sds((1, n), F32)

    (u,) = _rowwise(lambda xv, w: (_rms_fwd(xv, w),), [_full(x)], [p["norm_mix_w"]], [row(D, BF16)], [], name="rms_mix",
                    after=ex.start())
    p = dict(p, w_in_t=ex.w_in(u))
    hq, hf, hi, hg, att = _mm_nt(u, p["w_in_t"], splits=[HG_W] * 4 + [ATT_COLS], out_dtype=F32, name="in_proj")
    o_raw, states = _hgrn_fwd(hq, hf, hi, p["lb"], name="hgrn_fwd")
    o_att = _attn_fwd(att, p["b_attn"], p["sinks"], name="attn_fwd")
    (mix,) = _rowwise(_mix_fwd_fn, [_full(o_raw), _full(hg), _full(o_att)], [p["hg_norm_w"]], [row(D, BF16)], [],
                      name="mix_fwd")
    p = dict(p, **ex.rest(mix))
    h1 = _mm_nn([[mix]], [p["w_out"]], out_dtype=F32, name="out_proj", residual=x)
    (v,) = _rowwise(lambda hv, w: (_rms_fwd(hv, w),), [_full(h1)], [p["norm_ffn_w"]], [row(D, BF16)], [], name="rms_ffn")
    (gp,) = _mm_nt(v, p["w_gate_t"], splits=[D_FF], out_dtype=F32, name="gate_proj")
    (up,) = _mm_nt(v, p["w_up_t"], splits=[D_FF], out_dtype=F32, name="up_proj")
    act = _convact_fwd(gp, up, p["conv_w8"], p["conv_b"], name="convact_fwd")
    h2 = _mm_nn([[act]], [p["w_down"]], out_dtype=F32, name="down_proj", residual=h1)
    dh2, dh2_b, loss_cols, d_final = _rowwise(_final_fn, [_full(h2), _full(tgt)], [p["final_norm_w"]],
                                              [row(D, F32), row(D, BF16)], [acc(D), acc(D)], name="final_loss")

    (dact,) = _mm_nt(dh2_b, p["w_down"], splits=[D_FF], out_dtype=F32, name="d_act")
    g_down = _mm_tn([act], dh2_b, name="g_down")
    dgp, dup, d_conv_w8, d_conv_b = _convact_bwd(gp, up, dact, p["conv_w8"], p["conv_b"], name="convact_bwd")
    dv = _mm_nn([[dgp], [dup]], [p["w_gate_t"], p["w_up_t"]], out_dtype=F32, name="d_v")
    g_gate_t = _mm_tn([dgp], v, name="g_gate")
    g_up_t = _mm_tn([dup], v, name="g_up")
    sent = ex.ffn_grads([g_gate_t, g_up_t, g_down])

    def ffn_norm_bwd(hv, dvv, dh2v, w):
        dx, dw = _rms_bwd(hv, w, dvv)
        dh1v = dx + dh2v
        return dh1v, dh1v, dw

    dh1, dh1_b, d_norm_ffn = _rowwise(ffn_norm_bwd, [_full(h1), _full(dv), _full(dh2)], [p["norm_ffn_w"]],
                                      [row(D, F32), row(D, BF16)], [acc(D)], name="rms_ffn_bwd", after=sent)
    (dmix,) = _mm_nt(dh1_b, p["w_out"], splits=[D], out_dtype=F32, name="d_mix")
    g_out = _mm_tn([mix], dh1_b, name="g_out")
    do_raw, dhg, d_hg_norm = _rowwise(_mix_bwd_fn, [_full(o_raw), _full(hg), (dmix, HG_W, 0, 0)], [p["hg_norm_w"]],
                                      [row(HG_W, F32), row(HG_W, BF16)], [acc(HG_DK)], name="mix_bwd")
    daq, dakv, d_sinks8, d_bq, d_bkv = _attn_bwd(att, p["b_attn"], p["sinks"], dmix, name="attn_bwd")
    dhq, dhf, dhi, d_lb = _hgrn_bwd(hq, hf, hi, p["lb"], states, do_raw, name="hgrn_bwd")
    pieces = [dhq, dhf, dhi, dhg, daq, dakv]
    du = _mm_nn([pieces], [p["w_in_t"]], out_dtype=F32, name="d_u")
    g_in_t = _mm_tn(pieces, u, name="g_in")

    def mix_norm_bwd(xv, duv, dh1v, w):
        dx, dw = _rms_bwd(xv, w, duv)
        return dx + dh1v, dw

    dx, d_norm_mix = _rowwise(mix_norm_bwd, [_full(x), _full(du), _full(dh1)], [p["norm_mix_w"]], [row(D, F32)], [acc(D)],
                              name="rms_mix_bwd")
    grads = dict(g_in_t=g_in_t, g_out=g_out, g_gate_t=g_gate_t, g_up_t=g_up_t, g_down=g_down,
                 norm_mix_w=d_norm_mix, b_attn=jnp.concatenate([d_bq, d_bkv], axis=1), lb=d_lb, hg_norm_w=d_hg_norm,
                 sinks8=d_sinks8, norm_ffn_w=d_norm_ffn, conv_w8=d_conv_w8, conv_b=d_conv_b, final_norm_w=d_final)
    return loss_cols, dx, grads


SLAB = (IN_COLS // N_CHIPS, D_FF // N_CHIPS, D_FF // N_CHIPS, D_FF // N_CHIPS, D_MODEL // N_CHIPS)
N_W = len(SLAB)
PACK_OFF = tuple(sum(SLAB[:i]) for i in range(N_W))
PACK_ROWS = sum(SLAB)
FULL_OFF = tuple(N_CHIPS * o for o in PACK_OFF)
FULL_ROWS = N_CHIPS * PACK_ROWS
HALF = tuple(s // 2 for s in SLAB)
HPACK_OFF = tuple(sum(HALF[:i]) for i in range(N_W))
HPACK_ROWS = sum(HALF)
HFULL_OFF = tuple(N_CHIPS * o for o in HPACK_OFF)
HFULL_ROWS = N_CHIPS * HPACK_ROWS
CHIP_FLIPS = ((1, 0), (0, 1), (1, 1))
N_DEV = 8
BF16_ROWS = 16
ANY = pl.BlockSpec(memory_space=pl.ANY)


def _pos():
    return lax.axis_index("x"), lax.axis_index("y"), lax.axis_index("c")


def _flip(v, f):
    return 1 - v if f else v


def _rcopy(src, dst, ssem, rsem, dev):
    return pltpu.make_async_remote_copy(src_ref=src, dst_ref=dst, send_sem=ssem, recv_sem=rsem, device_id=dev,
                                        device_id_type=pl.DeviceIdType.MESH)


def _rows(ref, start, n, align=None):
    if not isinstance(start, int):
        if align is None:
            align = SUBLANES * (4 // jnp.dtype(ref.dtype).itemsize)
        start = pl.multiple_of(start, align)
    return ref.at[pl.ds(start, n), :]


FFN_W = (1, 2, 3)
N_PEER = 1 + len(CHIP_FLIPS)
HBM = pl.BlockSpec(memory_space=pltpu.HBM)
SEM = pl.BlockSpec(memory_space=pltpu.SEMAPHORE)
EFFECT = pltpu.SideEffectType.DATAFLOW_SIDE_EFFECTING
LANES = 128


def _gather_start(pack, cw8):
    D = pack.shape[1]
    lands = [lax.empty((N_CHIPS * SLAB[0], D), pack.dtype), lax.empty((3 * N_CHIPS * SLAB[1], D), pack.dtype),
             lax.empty((N_CHIPS * SLAB[4], D), pack.dtype), lax.empty((N_CHIPS,) + cw8.shape, cw8.dtype)]
    bufs = [pack, cw8] + lands

    def body(pack_ref, cw_ref, l_in, l_ffn, l_out, l_cw, *rest):
        in_send, in_recv, rest_send, rest_recv = rest[:4]
        token = rest[-1]
        x, y, c = _pos()
        q = 2 * x + y
        peers = _gather_peers(x, y, c)
        for k, peer in enumerate(peers):
            _rcopy(_rows(pack_ref, PACK_OFF[0], SLAB[0]), _rows(l_in, q * SLAB[0], SLAB[0], BF16_ROWS),
                   in_send.at[k], in_recv.at[k], peer).start()
        for k, peer in enumerate(peers):
            for j, w in enumerate(FFN_W):
                _rcopy(_rows(pack_ref, PACK_OFF[w], SLAB[w]), _rows(l_ffn, (j * N_CHIPS + q) * SLAB[w], SLAB[w], BF16_ROWS),
                       rest_send.at[k], rest_recv.at[k], peer).start()
            _rcopy(_rows(pack_ref, PACK_OFF[4], SLAB[4]), _rows(l_out, q * SLAB[4], SLAB[4], BF16_ROWS),
                   rest_send.at[N_PEER + k], rest_recv.at[N_PEER + k], peer).start()
            _rcopy(cw_ref, l_cw.at[q], rest_send.at[2 * N_PEER + k], rest_recv.at[2 * N_PEER + k], peer).start()
        token[...] = jnp.zeros_like(token)

    outs = pl.pallas_call(
        body, name="gather_start", in_specs=[HBM] * len(bufs),
        out_specs=[SEM] * 4 + [HBM] * len(bufs) + [pl.BlockSpec(memory_space=pltpu.VMEM)],
        out_shape=[pltpu.SemaphoreType.DMA((N_PEER,)), pltpu.SemaphoreType.DMA((N_PEER,)),
                   pltpu.SemaphoreType.DMA((3 * N_PEER,)), pltpu.SemaphoreType.DMA((3 * N_PEER,))]
        + [pltpu.HBM(b.shape, b.dtype) for b in bufs] + [_sds((SUBLANES, LANES), F32)],
        input_output_aliases={i: 4 + i for i in range(len(bufs))},
        compiler_params=pltpu.CompilerParams(has_side_effects=EFFECT),
    )(*[pltpu.with_memory_space_constraint(b, pltpu.HBM) for b in bufs])
    return dict(in_sems=outs[0:2], rest_sems=outs[2:4], pack=outs[4], cw=outs[5], l_in=outs[6], l_ffn=outs[7],
                l_out=outs[8], l_cw=outs[9], token=outs[10])


def _gather_peers(x, y, c):
    return [(x, y, 1 - c)] + [(_flip(x, fx), _flip(y, fy), c) for fx, fy in CHIP_FLIPS]


def _gather_wait_in(g, after):
    def body(pack_ref, l_in, send, recv, after_ref, pack_out, l_out):
        for k, peer in enumerate(_gather_peers(*_pos())):
            cp = _rcopy(_rows(pack_ref, PACK_OFF[0], SLAB[0]), _rows(l_in, 0, SLAB[0]), send.at[k], recv.at[k], peer)
            cp.wait_send()
            cp.wait_recv()

    return pl.pallas_call(
        body, name="gather_wait_in", in_specs=[HBM, HBM, SEM, SEM, ANY], out_specs=[HBM, HBM],
        out_shape=[pltpu.HBM(g["pack"].shape, g["pack"].dtype), pltpu.HBM(g["l_in"].shape, g["l_in"].dtype)],
        input_output_aliases={0: 0, 1: 1}, compiler_params=pltpu.CompilerParams(has_side_effects=EFFECT),
    )(g["pack"], g["l_in"], *g["in_sems"], after)


def _gather_wait_rest(g, pack, after):
    n_ffn = len(FFN_W) * SLAB[FFN_W[0]]

    def body(pack_ref, cw_ref, l_ffn, l_out, l_cw, send, recv, after_ref, o_ffn, o_out, o_cw):
        for k, peer in enumerate(_gather_peers(*_pos())):
            for cp in (_rcopy(_rows(pack_ref, PACK_OFF[FFN_W[0]], n_ffn), _rows(l_ffn, 0, n_ffn), send.at[k], recv.at[k], peer),
                       _rcopy(_rows(pack_ref, PACK_OFF[4], SLAB[4]), _rows(l_out, 0, SLAB[4]),
                              send.at[N_PEER + k], recv.at[N_PEER + k], peer),
                       _rcopy(cw_ref, l_cw.at[0], send.at[2 * N_PEER + k], recv.at[2 * N_PEER + k], peer)):
                cp.wait_send()
                cp.wait_recv()

    ins = [pack, g["cw"], g["l_ffn"], g["l_out"], g["l_cw"]]
    return pl.pallas_call(
        body, name="gather_wait_rest", in_specs=[HBM] * 5 + [SEM, SEM, ANY], out_specs=[HBM] * 3,
        out_shape=[pltpu.HBM(b.shape, b.dtype) for b in ins[2:]],
        input_output_aliases={2: 0, 3: 1, 4: 2}, compiler_params=pltpu.CompilerParams(has_side_effects=EFFECT),
    )(*ins, *g["rest_sems"], after)


def _exchange_halves(ws, gs, small, *, name):
    D = gs[0].shape[1]
    n = len(ws)
    has_small = small is not None

    def body(*refs):
        g = refs[:n]
        t = refs[n + has_small:2 * n + has_small]
        sems = refs[2 * n + 2 * has_small:]
        d2d_send, d2d_recv = sems[0], sems[1]
        x, y, c = _pos()
        sib = (x, y, 1 - c)
        drains = []
        for i, w in enumerate(ws):
            h = HALF[w]
            for qq in range(N_CHIPS):
                _rcopy(_rows(g[i], qq * SLAB[w] + (1 - c) * h, h), _rows(t[i], qq * h, h),
                       d2d_send.at[i], d2d_recv.at[i], sib).start()
            drains.append(_rcopy(t[i], t[i], d2d_send.at[i], d2d_recv.at[i], sib))
        if has_small:
            small_ref, sall_ref = refs[n], refs[2 * n + 1]
            sm_send, sm_recv, loc_sem = sems[2], sems[3], sems[4]
            me = 4 * x + 2 * y + c
            own_small = pltpu.make_async_copy(small_ref, sall_ref.at[me], loc_sem)
            own_small.start()
            for f in range(1, N_DEV):
                peer = (_flip(x, f & 4), _flip(y, f & 2), _flip(c, f & 1))
                cp = _rcopy(small_ref, sall_ref.at[me], sm_send.at[f - 1], sm_recv.at[f - 1], peer)
                cp.start()
                drains.append(cp)
        for d in drains:
            d.wait_recv()
        for d in drains:
            d.wait_send()
        if has_small:
            own_small.wait()

    out_shape = [_sds((N_CHIPS * HALF[w], D), F32) for w in ws]
    scratch = [pltpu.SemaphoreType.DMA((n,)), pltpu.SemaphoreType.DMA((n,))]
    if has_small:
        out_shape.append(_sds((N_DEV,) + small.shape, F32))
        scratch += [pltpu.SemaphoreType.DMA((N_DEV - 1,)), pltpu.SemaphoreType.DMA((N_DEV - 1,)), pltpu.SemaphoreType.DMA]
    return pl.pallas_call(
        body, name=name, in_specs=[ANY] * (n + has_small), out_specs=[ANY] * (n + has_small),
        out_shape=out_shape, scratch_shapes=scratch,
    )(*gs, *([small] if has_small else []))


REDUCE_SPLIT = 2


def _chip_partial(ws, gs, theirs, *, name, out_dtype=F32):
    D = gs[0].shape[1]
    n = len(ws)

    def body(*refs):
        for i in range(n):
            refs[2 * n + i][...] = (refs[i][...] + refs[n + i][...]).astype(out_dtype)

    blk = [HALF[w] // REDUCE_SPLIT for w in ws]
    mine = [pl.BlockSpec((b, D), lambda qq, j: ((2 * qq + lax.axis_index("c")) * REDUCE_SPLIT + j, 0)) for b in blk]
    flat = [pl.BlockSpec((b, D), lambda qq, j: (qq * REDUCE_SPLIT + j, 0)) for b in blk]
    return pl.pallas_call(
        body, name=name, grid=(N_CHIPS, REDUCE_SPLIT), in_specs=mine + flat, out_specs=flat,
        out_shape=[_sds((N_CHIPS * HALF[w], D), out_dtype) for w in ws],
        compiler_params=_cp(("parallel", "parallel")),
    )(*gs, *theirs)


def _partial_copies(ws, part, got, send_sems, recv_sems):
    x, y, c = _pos()
    cps = []
    for k, (fx, fy) in enumerate(CHIP_FLIPS):
        peer = (_flip(x, fx), _flip(y, fy), c)
        qp = 2 * _flip(x, fx) + _flip(y, fy)
        for i, w in enumerate(ws):
            cps.append(_rcopy(_rows(part[i], qp * HALF[w], HALF[w]), _rows(got[i], k * HALF[w], HALF[w]),
                              send_sems.at[len(ws) * k + i], recv_sems.at[len(ws) * k + i], peer))
    return cps


def _send_chip_partials(ws, parts, *, name):
    D = parts[0].shape[1]
    n = len(ws)

    def body(*refs):
        cps = _partial_copies(ws, refs[:n], refs[n:2 * n], refs[2 * n], refs[2 * n + 1])
        for cp in cps:
            cp.start()
        for cp in cps:
            cp.wait_recv()
        for cp in cps:
            cp.wait_send()

    return pl.pallas_call(
        body, name=name, in_specs=[ANY] * n, out_specs=[ANY] * n,
        out_shape=[_sds((len(CHIP_FLIPS) * HALF[w], D), parts[0].dtype) for w in ws],
        scratch_shapes=[pltpu.SemaphoreType.DMA((len(CHIP_FLIPS) * n,)), pltpu.SemaphoreType.DMA((len(CHIP_FLIPS) * n,))],
    )(*parts)


def _send_start(ws, parts, *, name):
    D = parts[0].shape[1]
    n = len(ws)
    bufs = list(parts) + [lax.empty((len(CHIP_FLIPS) * HALF[w], D), F32) for w in ws]

    def body(*refs):
        send_sems, recv_sems = refs[2 * n], refs[2 * n + 1]
        for cp in _partial_copies(ws, refs[:n], refs[n:2 * n], send_sems, recv_sems):
            cp.start()
        refs[-1][...] = jnp.zeros_like(refs[-1])

    outs = pl.pallas_call(
        body, name=name, in_specs=[HBM] * (2 * n),
        out_specs=[SEM, SEM] + [HBM] * (2 * n) + [pl.BlockSpec(memory_space=pltpu.VMEM)],
        out_shape=[pltpu.SemaphoreType.DMA((len(CHIP_FLIPS) * n,)), pltpu.SemaphoreType.DMA((len(CHIP_FLIPS) * n,))]
        + [pltpu.HBM(b.shape, b.dtype) for b in bufs] + [_sds((SUBLANES, LANES), F32)],
        input_output_aliases={i: 2 + i for i in range(2 * n)},
        compiler_params=pltpu.CompilerParams(has_side_effects=EFFECT),
    )(*[pltpu.with_memory_space_constraint(b, pltpu.HBM) for b in bufs])
    return dict(sems=outs[0:2], parts=outs[2:2 + n], got=outs[2 + n:2 + 2 * n], token=outs[-1])


def _send_wait(ws, s, after, *, name):
    n = len(ws)

    def body(*refs):
        for cp in _partial_copies(ws, refs[:n], refs[n:2 * n], refs[2 * n], refs[2 * n + 1]):
            cp.wait_send()
            cp.wait_recv()

    bufs = list(s["parts"]) + list(s["got"])
    outs = pl.pallas_call(
        body, name=name, in_specs=[HBM] * (2 * n) + [SEM, SEM, ANY], out_specs=[HBM] * (2 * n),
        out_shape=[pltpu.HBM(b.shape, b.dtype) for b in bufs],
        input_output_aliases={i: i for i in range(2 * n)},
        compiler_params=pltpu.CompilerParams(has_side_effects=EFFECT),
    )(*bufs, *s["sems"], after)
    return outs[:n], outs[n:]


def _chip_reduce(parts, got):
    D = parts[0].shape[1]
    nk = len(CHIP_FLIPS)

    def body(*refs):
        outs = refs[(1 + nk) * N_W:]
        for w in range(N_W):
            acc = refs[w][...].astype(F32)
            for k in range(nk):
                acc = acc + refs[N_W * (1 + k) + w][...].astype(F32)
            outs[w][...] = acc

    blk = [h // REDUCE_SPLIT for h in HALF]

    def q_idx(j):
        return (2 * lax.axis_index("x") + lax.axis_index("y")) * REDUCE_SPLIT + j

    in_specs = [pl.BlockSpec((b, D), lambda j: (q_idx(j), 0)) for b in blk]
    for k in range(nk):
        in_specs += [pl.BlockSpec((b, D), functools.partial(lambda j, k: (k * REDUCE_SPLIT + j, 0), k=k)) for b in blk]
    out_specs = [pl.BlockSpec((b, D), lambda j: (lax.axis_index("c") * REDUCE_SPLIT + j, 0)) for b in blk]
    return pl.pallas_call(
        body, name="chip_reduce", grid=(REDUCE_SPLIT,), in_specs=in_specs, out_specs=out_specs,
        out_shape=[_sds((s, D), F32) for s in SLAB],
        compiler_params=_cp(("parallel",)),
    )(*parts, *[g for _ in range(nk) for g in got])


def _exchange_reduced(shards):
    def body(i0, i1, i2, i3, i4, o0, o1, o2, o3, o4, send_sems, recv_sems):
        ins = (i0, i1, i2, i3, i4)
        outs = (o0, o1, o2, o3, o4)
        x, y, c = _pos()
        sib = (x, y, 1 - c)
        cps = []
        for w in range(N_W):
            cp = _rcopy(_rows(ins[w], c * HALF[w], HALF[w]), _rows(outs[w], c * HALF[w], HALF[w]),
                        send_sems.at[w], recv_sems.at[w], sib)
            cp.start()
            cps.append(cp)
        for cp in cps:
            cp.wait_recv()
        for cp in cps:
            cp.wait_send()

    return pl.pallas_call(
        body, name="exchange_reduced", in_specs=[ANY] * N_W, out_specs=[ANY] * N_W,
        out_shape=[_sds(s.shape, s.dtype) for s in shards], input_output_aliases={w: w for w in range(N_W)},
        scratch_shapes=[pltpu.SemaphoreType.DMA((N_W,)), pltpu.SemaphoreType.DMA((N_W,))],
    )(*shards)


def _adamw_fn(w, g, m, v):
    m2 = ADAM_B1 * m + (1.0 - ADAM_B1) * g
    v2 = ADAM_B2 * v + (1.0 - ADAM_B2) * (g * g)
    m_hat = m2 / (1.0 - ADAM_B1 ** ADAM_STEP)
    v_hat = v2 / (1.0 - ADAM_B2 ** ADAM_STEP)
    return -ADAM_LR * (m_hat / (jnp.sqrt(v_hat) + ADAM_EPS) + ADAM_WD * w), m2, v2


def _adamw(w, g, m, v, *, name):
    shp = _sds(w.shape, F32)
    rows = w.shape[0]
    tm = max(t for t in range(SUBLANES, 512 + 1, SUBLANES) if rows % t == 0)
    return _rowwise(_adamw_fn, [_full(w), _full(g), _full(m), _full(v)], [], [shp] * 3, [], name=name, tm=tm)


SMALL_SEGS = (("loss", 8), ("norm_mix_w", 8), ("b_attn", 8), ("lb_logits", 8), ("hg_norm_w", 8), ("sinks", 8),
              ("norm_ffn_w", 8), ("conv_w", 72), ("conv_b", 24), ("final_norm_w", 8))
SMALL_OFF = {n: sum(r for _, r in SMALL_SEGS[:i]) for i, (n, _) in enumerate(SMALL_SEGS)}
SMALL_ROWS = sum(r for _, r in SMALL_SEGS)
LANES = 128


def _pack_small(parts):
    segs = []
    for n, r in SMALL_SEGS:
        a = parts.get(n)
        flat = jnp.zeros((0,), F32) if a is None else a.reshape(-1).astype(F32)
        segs.append(jnp.pad(flat, (0, r * LANES - flat.shape[0])).reshape(r, LANES))
    return jnp.concatenate(segs, axis=0)


def _unpack_small(pack, n, shape):
    size = math.prod(shape)
    r0 = SMALL_OFF[n]
    return pack[r0:r0 + dict(SMALL_SEGS)[n]].reshape(-1)[:size].reshape(shape)


def _small_update(sall, wp, mp, vp):
    R = SMALL_ROWS
    r_lb = SMALL_OFF["lb_logits"]

    def body(s_ref, w_ref, m_ref, v_ref, g_ref, d_ref, m2_ref, v2_ref, loss_ref):
        g = s_ref[0]
        for i in range(1, N_DEV):
            g = g + s_ref[i]
        tot = jnp.sum(jnp.sum(g[0:8], axis=1, keepdims=True), axis=0, keepdims=True)
        loss_ref[...] = jnp.broadcast_to(tot, loss_ref.shape)
        lg = w_ref[r_lb:r_lb + 8, :]
        p0 = _sigmoid(lg - pltpu.roll(lg, 4, 0))
        d = g[r_lb:r_lb + 8]
        d = d + pltpu.roll(d, 4, 0)
        sign = jnp.where(lax.broadcasted_iota(jnp.int32, d.shape, 0) < 4, 1.0, -1.0)
        g = jnp.concatenate([g[:r_lb], sign * d * p0 * (1.0 - p0), g[r_lb + 8:]], axis=0)
        g_ref[...] = g
        d_ref[...], m2_ref[...], v2_ref[...] = _adamw_fn(w_ref[...], g, m_ref[...], v_ref[...])

    full = pl.BlockSpec((R, LANES), lambda: (0, 0))
    return pl.pallas_call(
        body, name="small_update",
        in_specs=[pl.BlockSpec((N_DEV, R, LANES), lambda: (0, 0, 0)), full, full, full],
        out_specs=[full, full, full, full, pl.BlockSpec((8, LANES), lambda: (0, 0))],
        out_shape=[_sds((R, LANES), F32)] * 4 + [_sds((8, LANES), F32)],
        compiler_params=_cp(),
    )(sall, wp, mp, vp)


def _lb_fwd(lb_logits):
    n = lb_logits.shape[1]

    def body(l_ref, o_ref):
        o_ref[...] = _sigmoid(l_ref[0:1, :] - l_ref[1:2, :])

    return pl.pallas_call(body, name="lb_fwd", out_shape=_sds((1, n), F32), compiler_params=_cp())(lb_logits)


class _MeshExchange:
    def __init__(self, pack, cw8):
        self.gather = _gather_start(pack, cw8)
        self.sent = None
        self.conv_w8 = None

    def start(self):
        return self.gather["token"]

    def w_in(self, after):
        self.pack, l_in = _gather_wait_in(self.gather, after)
        return (l_in, N_CHIPS * SLAB[0], 0)

    def rest(self, after):
        l_ffn, l_out, l_cw = _gather_wait_rest(self.gather, self.pack, after)
        self.conv_w8 = jnp.concatenate([l_cw[i] for i in range(N_CHIPS)], axis=1)
        rows = N_CHIPS * SLAB[FFN_W[0]]
        return dict(w_gate_t=(l_ffn, rows, 0), w_up_t=(l_ffn, rows, 1), w_down=(l_ffn, rows, 2),
                    w_out=(l_out, N_CHIPS * SLAB[4], 0), conv_w8=self.conv_w8)

    def ffn_grads(self, gs):
        theirs = _exchange_halves(FFN_W, gs, None, name="exchange_halves_ffn")
        parts = _chip_partial(FFN_W, gs, theirs, name="chip_partial_ffn")
        self.sent = _send_start(FFN_W, parts, name="send_ffn_start")
        return self.sent["token"]


def kernel(x, norm_mix_w, w_in, b_attn, lb_logits, hg_norm_w, sinks, w_out, norm_ffn_w, w_gate, w_up, conv_w, conv_b, w_down, final_norm_w, loss_target, m_norm_mix_w, m_w_in, m_b_attn, m_lb_logits, m_hg_norm_w, m_sinks, m_w_out, m_norm_ffn_w, m_w_gate, m_w_up, m_conv_w, m_conv_b, m_w_down, m_final_norm_w, v_norm_mix_w, v_w_in, v_b_attn, v_lb_logits, v_hg_norm_w, v_sinks, v_w_out, v_norm_ffn_w, v_w_gate, v_w_up, v_conv_w, v_conv_b, v_w_down, v_final_norm_w):
    D = D_MODEL
    q = 2 * lax.axis_index("x") + lax.axis_index("y")
    ccols = D_FF // N_CHIPS

    pack = jnp.concatenate([w_in[0].T, w_gate[0].T, w_up[0].T, w_down[0], w_out[0]], axis=0).astype(BF16)
    cw8 = jnp.concatenate([conv_w[0], jnp.zeros((SUBLANES - 3, ccols), F32)], axis=0)
    ex = _MeshExchange(pack, cw8)
    p = dict(norm_mix_w=norm_mix_w, b_attn=b_attn, lb=_lb_fwd(lb_logits), hg_norm_w=hg_norm_w, sinks=sinks,
             norm_ffn_w=norm_ffn_w, conv_b=conv_b, final_norm_w=final_norm_w.reshape(1, D))
    loss_cols, dx, g = _local_step(x[0], loss_target[0], p, ex)
    conv_w8 = ex.conv_w8

    small = _pack_small(dict(loss=loss_cols, norm_mix_w=g["norm_mix_w"], b_attn=g["b_attn"], lb_logits=g["lb"],
                             hg_norm_w=g["hg_norm_w"], sinks=g["sinks8"], norm_ffn_w=g["norm_ffn_w"],
                             conv_w=g["conv_w8"][:3], conv_b=g["conv_b"], final_norm_w=g["final_norm_w"]))
    parts_ffn, got_ffn = _send_wait(FFN_W, ex.sent, dx, name="send_ffn_wait")
    late = (0, 4)
    gs = [g["g_in_t"], g["g_out"]]
    *theirs, sall = _exchange_halves(late, gs, small, name="exchange_halves_late")
    parts_late = _chip_partial(late, gs, theirs, name="chip_partial_late", out_dtype=BF16)
    got_late = _send_chip_partials(late, parts_late, name="send_late")
    parts = [parts_late[0], *parts_ffn, parts_late[1]]
    got = [got_late[0], *got_ffn, got_late[1]]
    shards = _exchange_reduced(_chip_reduce(parts, got))
    big = {}
    for n, gw, w, m, v, tr in (("w_in", shards[0], w_in, m_w_in, v_w_in, True), ("w_gate", shards[1], w_gate, m_w_gate, v_w_gate, True),
                               ("w_up", shards[2], w_up, m_w_up, v_w_up, True), ("w_down", shards[3], w_down, m_w_down, v_w_down, False),
                               ("w_out", shards[4], w_out, m_w_out, v_w_out, False)):
        view = (lambda a: a[0].T) if tr else (lambda a: a[0])
        back = (lambda a: a.T[None]) if tr else (lambda a: a[None])
        d_, m_, v_ = _adamw(view(w), gw, view(m), view(v), name="adamw_" + n)
        big[n] = (back(gw), back(d_), back(m_), back(v_))

    def place(a):
        return lax.dynamic_update_slice(jnp.zeros((3, D_FF), F32), a[0], (0, q * ccols))

    def small_pack(ws, cw):
        nm, ba, lbl, hg, sk, nf, cb, fn = ws
        return _pack_small(dict(norm_mix_w=nm, b_attn=ba, lb_logits=lbl, hg_norm_w=hg,
                                sinks=jnp.broadcast_to(sk.reshape(ATT_HEADS, 1), (ATT_HEADS, LANES)), norm_ffn_w=nf,
                                conv_w=cw, conv_b=cb, final_norm_w=fn))

    wp = small_pack((norm_mix_w, b_attn, lb_logits, hg_norm_w, sinks, norm_ffn_w, conv_b, final_norm_w), conv_w8[:3])
    mp = small_pack((m_norm_mix_w, m_b_attn, m_lb_logits, m_hg_norm_w, m_sinks, m_norm_ffn_w, m_conv_b, m_final_norm_w),
                    place(m_conv_w))
    vp = small_pack((v_norm_mix_w, v_b_attn, v_lb_logits, v_hg_norm_w, v_sinks, v_norm_ffn_w, v_conv_b, v_final_norm_w),
                    place(v_conv_w))
    outs = _small_update(sall, wp, mp, vp)
    loss = outs[4][0, 0]

    def small_out(pk, n, ref):
        if n == "sinks":
            return pk[SMALL_OFF[n]:SMALL_OFF[n] + ATT_HEADS, 0].reshape(ref.shape)
        if n == "conv_w":
            full = _unpack_small(pk, n, (3, D_FF))
            return lax.dynamic_slice(full, (0, q * ccols), (3, ccols))[None]
        return _unpack_small(pk, n, ref.shape)

    refs = dict(norm_mix_w=norm_mix_w, b_attn=b_attn, lb_logits=lb_logits, hg_norm_w=hg_norm_w, sinks=sinks,
                norm_ffn_w=norm_ffn_w, conv_w=conv_w, conv_b=conv_b, final_norm_w=final_norm_w)
    order = ("norm_mix_w", "w_in", "b_attn", "lb_logits", "hg_norm_w", "sinks", "w_out", "norm_ffn_w", "w_gate", "w_up",
             "conv_w", "conv_b", "w_down", "final_norm_w")
    res = [loss, dx[None]]
    for k in range(4):
        for n in order:
            res.append(big[n][k] if n in big else small_out(outs[k], n, refs[n]))
    return tuple(res)
```

```python
import functools
import math

import jax
import jax.numpy as jnp
from jax import lax
from jax.experimental import pallas as pl
from jax.experimental.pallas import tpu as pltpu

F32 = jnp.float32
BF16 = jnp.bfloat16

D_MODEL = 1024
HG_HEADS = 4
HG_DK = 128
HG_W = HG_HEADS * HG_DK
HG_CHUNK = 64
HG_SUB = 8
ATT_HEADS = 8
ATT_KV = 2
ATT_GROUP = ATT_HEADS // ATT_KV
ATT_HD = 64
ATT_BLOCK = 128
ATT_Q_W = ATT_HEADS * ATT_HD
ATT_KV_W = ATT_KV * ATT_HD
ATT_COLS = ATT_Q_W + 2 * ATT_KV_W
IN_COLS = 4 * HG_W + ATT_COLS
D_FF = 2816
EPS = 1e-6
ADAM_LR, ADAM_B1, ADAM_B2, ADAM_EPS, ADAM_WD, ADAM_STEP = 0.001, 0.9, 0.999, 1e-08, 0.01, 10
NEG = -1e30

V7X_VMEM_BYTES = 64 * 1024 * 1024
VMEM_LIMIT = 48 * 1024 * 1024
SUBLANES = 8

N_CHIPS = 4


def _cp(sem=None, **kw):
    return pltpu.CompilerParams(dimension_semantics=sem, vmem_limit_bytes=VMEM_LIMIT, **kw)


def _sds(shape, dtype):
    return jax.ShapeDtypeStruct(shape, dtype)


def _wspec(w):
    arr, rows, blk = w
    return pl.BlockSpec((rows, arr.shape[1]), lambda i: (blk, 0))


def _mm_nt(a, w, *, splits, out_dtype, name, residual=None, tm=512):
    M, K = a.shape
    N = w[1]
    tm = min(tm, M)
    assert sum(splits) == N and M % tm == 0
    offs = [sum(splits[:i]) for i in range(len(splits))]

    def body(*refs):
        a_ref, w_ref = refs[0], refs[1]
        outs = refs[2 + (residual is not None):]
        acc = lax.dot_general(a_ref[...], w_ref[...], (((1,), (1,)), ((), ())), preferred_element_type=F32)
        if residual is not None:
            acc = acc + refs[2][...]
        for o_ref, c0, n in zip(outs, offs, splits):
            o_ref[...] = acc[:, c0:c0 + n].astype(out_dtype)

    in_specs = [pl.BlockSpec((tm, K), lambda i: (i, 0)), _wspec(w)]
    args = [a, w[0]]
    if residual is not None:
        assert len(splits) == 1
        in_specs.append(pl.BlockSpec((tm, N), lambda i: (i, 0)))
        args.append(residual)
    outs = pl.pallas_call(
        body, name=name, grid=(M // tm,), in_specs=in_specs,
        out_specs=[pl.BlockSpec((tm, n), lambda i: (i, 0)) for n in splits],
        out_shape=[_sds((M, n), out_dtype) for n in splits],
        compiler_params=_cp(("parallel",)),
    )(*args)
    return outs


def _mm_nn(pieces, ws, *, out_dtype, name, residual=None, tm=512):
    M = pieces[0][0].shape[0]
    K = ws[0][0].shape[1]
    tm = min(tm, M)
    flat = [p for grp in pieces for p in grp]
    n_p = len(flat)

    def body(*refs):
        p_refs = refs[:n_p]
        w_refs = refs[n_p:n_p + len(ws)]
        o_ref = refs[-1]
        acc = None if residual is None else refs[n_p + len(ws)][...]
        k = 0
        for gi, grp in enumerate(pieces):
            c0 = 0
            for p in grp:
                n = p.shape[1]
                t = jnp.dot(p_refs[k][...], w_refs[gi][c0:c0 + n, :], preferred_element_type=F32)
                acc = t if acc is None else acc + t
                c0 += n
                k += 1
        o_ref[...] = acc.astype(out_dtype)

    in_specs = [pl.BlockSpec((tm, p.shape[1]), lambda i: (i, 0)) for p in flat]
    in_specs += [_wspec(w) for w in ws]
    args = [*flat, *[w[0] for w in ws]]
    if residual is not None:
        in_specs.append(pl.BlockSpec((tm, K), lambda i: (i, 0)))
        args.append(residual)
    return pl.pallas_call(
        body, name=name, grid=(M // tm,), in_specs=in_specs,
        out_specs=pl.BlockSpec((tm, K), lambda i: (i, 0)),
        out_shape=_sds((M, K), out_dtype),
        compiler_params=_cp(("parallel",)),
    )(*args)


def _mm_tn(pieces, x, *, name, tt=512):
    M, K = x.shape
    tt = min(tt, M)
    ns = [p.shape[1] for p in pieces]
    offs = [sum(ns[:i]) for i in range(len(ns))]
    N = sum(ns)
    n_p = len(pieces)

    def body(*refs):
        p_refs = refs[:n_p]
        x_ref = refs[n_p]
        o_ref = refs[n_p + 1]

        @pl.when(pl.program_id(0) == 0)
        def _():
            o_ref[...] = jnp.zeros_like(o_ref)

        xv = x_ref[...]
        for p_ref, c0, n in zip(p_refs, offs, ns):
            o_ref[c0:c0 + n, :] += lax.dot_general(p_ref[...], xv, (((0,), (0,)), ((), ())),
                                                    preferred_element_type=F32)

    in_specs = [pl.BlockSpec((tt, n), lambda i: (i, 0)) for n in ns]
    in_specs.append(pl.BlockSpec((tt, K), lambda i: (i, 0)))
    return pl.pallas_call(
        body, name=name, grid=(M // tt,), in_specs=in_specs,
        out_specs=pl.BlockSpec((N, K), lambda i: (0, 0)),
        out_shape=_sds((N, K), F32),
        compiler_params=_cp(("arbitrary",)),
    )(*pieces, x)


def _rms_fwd(xf, w):
    inv = lax.rsqrt(jnp.mean(xf * xf, axis=-1, keepdims=True) + EPS)
    return xf * inv * w


def _rms_bwd(xf, w, dy):
    inv = lax.rsqrt(jnp.mean(xf * xf, axis=-1, keepdims=True) + EPS)
    xhat = xf * inv
    dxhat = dy * w
    dx = inv * (dxhat - xhat * jnp.mean(dxhat * xhat, axis=-1, keepdims=True))
    dw = jnp.sum(dy * xhat, axis=0, keepdims=True)
    return dx, dw


def _sigmoid(x):
    return 1.0 / (1.0 + jnp.exp(-x))


def _rowwise(fn, row_ins, bc_ins, row_outs, acc_outs, *, name, tm=256, after=None):
    M = row_outs[0].shape[0] if row_outs else row_ins[0][0].shape[0]
    assert M % tm == 0 and tm % SUBLANES == 0, (name, M, tm)
    n_r, n_b, n_o, n_a = len(row_ins), len(bc_ins), len(row_outs), len(acc_outs)
    n_after = 0 if after is None else 1

    def body(*refs):
        refs = refs[n_after:]
        ins = [r[...] for r in refs[:n_r + n_b]]
        o_refs = refs[n_r + n_b:n_r + n_b + n_o]
        a_refs = refs[n_r + n_b + n_o:]
        res = fn(*ins)
        for o_ref, val in zip(o_refs, res[:n_o]):
            o_ref[...] = val.astype(o_ref.dtype)
        if n_a:
            @pl.when(pl.program_id(0) == 0)
            def _():
                for a_ref in a_refs:
                    a_ref[...] = jnp.zeros_like(a_ref)
            for a_ref, val in zip(a_refs, res[n_o:]):
                a_ref[...] += val

    in_specs = [pl.BlockSpec((tm, cw), functools.partial(lambda i, cb, r0: (i + r0, cb), cb=cb, r0=r0))
                for (_, cw, cb, r0) in row_ins]
    in_specs += [pl.BlockSpec(b.shape, lambda i: (0, 0)) for b in bc_ins]
    out_specs = [pl.BlockSpec((tm, s.shape[1]), lambda i: (i, 0)) for s in row_outs]
    out_specs += [pl.BlockSpec(s.shape, lambda i: (0, 0)) for s in acc_outs]
    if n_after:
        in_specs = [pl.BlockSpec(memory_space=pl.ANY)] + in_specs
    return pl.pallas_call(
        body, name=name, grid=(M // tm,), in_specs=in_specs, out_specs=out_specs,
        out_shape=list(row_outs) + list(acc_outs),
        compiler_params=_cp(("arbitrary",) if n_a else ("parallel",)),
    )(*([after] if n_after else []), *[r[0] for r in row_ins], *bc_ins)


def _full(a, first_row_block=0):
    return (a, a.shape[1], 0, first_row_block)


def _conv_rows(ext, w_ref_val, lo):
    s1 = pltpu.roll(ext, 1, 0)
    s2 = pltpu.roll(ext, 2, 0)
    y = w_ref_val[0:1, :] * s2 + w_ref_val[1:2, :] * s1 + w_ref_val[2:3, :] * ext
    return y[SUBLANES:, :]


def _convact_fwd(gp, up, conv_w8, conv_b, *, name, tr=128, tc=1408):
    T, C = gp.shape
    tr = min(tr, T)
    hb = tr // SUBLANES

    def body(gp_ref, gph_ref, up_ref, w_ref, b_ref, act_ref):
        i = pl.program_id(1)
        halo = jnp.where(i > 0, gph_ref[...], 0.0)
        ext = jnp.concatenate([halo, gp_ref[...]], axis=0)
        gate = _conv_rows(ext, w_ref[...], 0) + b_ref[...]
        act_ref[...] = (gate * _sigmoid(gate) * up_ref[...]).astype(act_ref.dtype)

    return pl.pallas_call(
        body, name=name, grid=(C // tc, T // tr),
        in_specs=[pl.BlockSpec((tr, tc), lambda j, i: (i, j)),
                  pl.BlockSpec((SUBLANES, tc), lambda j, i: (jnp.maximum(i * hb - 1, 0), j)),
                  pl.BlockSpec((tr, tc), lambda j, i: (i, j)),
                  pl.BlockSpec((SUBLANES, tc), lambda j, i: (0, j)),
                  pl.BlockSpec((1, tc), lambda j, i: (0, j))],
        out_specs=pl.BlockSpec((tr, tc), lambda j, i: (i, j)),
        out_shape=_sds((T, C), BF16),
        compiler_params=_cp(("parallel", "parallel")),
    )(gp, gp, up, conv_w8, conv_b)


def _convact_bwd(gp, up, dact, conv_w8, conv_b, *, name, tr=128, tc=1408):
    T, C = gp.shape
    tr = min(tr, T)
    hb = tr // SUBLANES
    nr = T // tr

    def body(gp_ref, gpp_ref, gpn_ref, up_ref, upn_ref, da_ref, dan_ref, w_ref, b_ref,
             dgp_ref, dup_ref, dw_ref, db_ref):
        i = pl.program_id(1)
        w = w_ref[...]
        prev = jnp.where(i > 0, gpp_ref[...], 0.0)
        last = i == nr - 1
        gp_ext = jnp.concatenate([prev, gp_ref[...], gpn_ref[...]], axis=0)
        gate = _conv_rows(gp_ext, w, 0) + b_ref[...]
        up_e = jnp.concatenate([up_ref[...], upn_ref[...]], axis=0)
        da_e = jnp.concatenate([da_ref[...], dan_ref[...]], axis=0)
        row = lax.broadcasted_iota(jnp.int32, gate.shape, 0)
        valid = jnp.logical_or(row < tr, jnp.logical_not(last))
        sg = _sigmoid(gate)
        silu = gate * sg
        dgate = jnp.where(valid, da_e * up_e * (sg * (1.0 + gate * (1.0 - sg))), 0.0)
        dup_ref[...] = (da_e[:tr] * silu[:tr]).astype(dup_ref.dtype)
        n = tr + SUBLANES
        g1 = pltpu.roll(dgate, n - 1, 0)
        g2 = pltpu.roll(dgate, n - 2, 0)
        dgp = w[2:3, :] * dgate + w[1:2, :] * g1 + w[0:1, :] * g2
        dgp_ref[...] = dgp[:tr].astype(dgp_ref.dtype)
        gpc = gp_ref[...]
        dw0 = jnp.sum(gpc * g2[:tr], axis=0, keepdims=True)
        dw1 = jnp.sum(gpc * g1[:tr], axis=0, keepdims=True)
        dw2 = jnp.sum(gpc * dgate[:tr], axis=0, keepdims=True)
        dbv = jnp.sum(dgate[:tr], axis=0, keepdims=True)
        z = jnp.zeros((SUBLANES - 3, gpc.shape[1]), F32)

        @pl.when(i == 0)
        def _():
            dw_ref[...] = jnp.zeros_like(dw_ref)
            db_ref[...] = jnp.zeros_like(db_ref)

        dw_ref[...] += jnp.concatenate([dw0, dw1, dw2, z], axis=0)
        db_ref[...] += dbv

    cur = pl.BlockSpec((tr, tc), lambda j, i: (i, j))
    prv = pl.BlockSpec((SUBLANES, tc), lambda j, i: (jnp.maximum(i * hb - 1, 0), j))
    nxt = pl.BlockSpec((SUBLANES, tc), lambda j, i: (jnp.minimum((i + 1) * hb, T // SUBLANES - 1), j))
    return pl.pallas_call(
        body, name=name, grid=(C // tc, nr),
        in_specs=[cur, prv, nxt, cur, nxt, cur, nxt,
                  pl.BlockSpec((SUBLANES, tc), lambda j, i: (0, j)),
                  pl.BlockSpec((1, tc), lambda j, i: (0, j))],
        out_specs=[cur, cur,
                   pl.BlockSpec((SUBLANES, tc), lambda j, i: (0, j)),
                   pl.BlockSpec((1, tc), lambda j, i: (0, j))],
        out_shape=[_sds((T, C), BF16), _sds((T, C), BF16), _sds((SUBLANES, C), F32), _sds((1, C), F32)],
        compiler_params=_cp(("parallel", "arbitrary")),
    )(gp, gp, gp, up, up, dact, dact, conv_w8, conv_b)


def _cumsum_rows(x):
    n = x.shape[0]
    row = lax.broadcasted_iota(jnp.int32, x.shape, 0)
    s = 1
    while s < n:
        x = x + jnp.where(row >= s, pltpu.roll(x, s, 0), 0.0)
        s *= 2
    return x


def _rcumsum_rows(x):
    n = x.shape[0]
    row = lax.broadcasted_iota(jnp.int32, x.shape, 0)
    s = 1
    while s < n:
        x = x + jnp.where(row < n - s, pltpu.roll(x, n - s, 0), 0.0)
        s *= 2
    return x


def _dot_nt(a, b):
    return lax.dot_general(a.astype(BF16), b.astype(BF16), (((1,), (1,)), ((), ())), preferred_element_type=F32)


def _dot_tn(a, b):
    return lax.dot_general(a.astype(BF16), b.astype(BF16), (((0,), (0,)), ((), ())), preferred_element_type=F32)


def _dot_nn(a, b):
    return jnp.dot(a.astype(BF16), b.astype(BF16), preferred_element_type=F32)


def _hg_gates(hq, hf, lbv):
    sig = _sigmoid(hf)
    f = lbv + (1.0 - lbv) * sig
    return sig, f, jnp.log(f), 1.0 - f, hq * (HG_DK ** -0.5)


def _hg_sel_rows(ref, sp):
    return jnp.concatenate(
        [jnp.broadcast_to(ref[pl.ds(HG_SUB * i + sp, 1), :], (HG_SUB, HG_DK)) for i in range(HG_CHUNK // HG_SUB)], axis=0)


def _hg_masks():
    C = HG_CHUNK
    row = lax.broadcasted_iota(jnp.int32, (C, C), 0)
    col = lax.broadcasted_iota(jnp.int32, (C, C), 1)
    d = col - (row // HG_SUB) * HG_SUB
    tmod = row % HG_SUB
    diag_valid = jnp.logical_and(d >= 0, d <= tmod)
    return row, col, d, diag_valid


def _hg_scores(q, k, b, b_sc, k_sc):
    C, S = HG_CHUNK, HG_SUB
    row, col, d, diag_valid = _hg_masks()
    blocks = [jnp.zeros((S, C), F32)]
    for i in range(1, C // S):
        r = b_sc[pl.ds(S * i - 1, 1), :]
        qi = q[S * i:S * (i + 1)] * jnp.exp(b[S * i:S * (i + 1)] - r)
        kk = k * jnp.exp(jnp.minimum(r - b, 0.0))
        blocks.append(_dot_nt(qi, kk))
    a_off = jnp.where(col < (row // S) * S, jnp.concatenate(blocks, axis=0), 0.0)
    a_d = jnp.zeros((C, C), F32)
    for sp in range(S):
        bs = _hg_sel_rows(b_sc, sp)
        ks = _hg_sel_rows(k_sc, sp)
        e = jnp.exp(jnp.minimum(b - bs, 0.0))
        colv = jnp.sum(q * ks * e, axis=-1, keepdims=True)
        a_d = jnp.where(d == sp, colv, a_d)
    return a_off + jnp.where(diag_valid, a_d, 0.0)


def _hgrn_fwd(hq, hf, hi, lb, *, name):
    T = hq.shape[0]
    C, H, K = HG_CHUNK, HG_HEADS, HG_DK
    NC = T // C

    def body(hq_ref, hf_ref, hi_ref, lb_ref, o_ref, st_ref, s_sc, b_sc, k_sc):
        @pl.when(pl.program_id(0) == 0)
        def _():
            s_sc[...] = jnp.zeros_like(s_sc)

        st_all = s_sc[...]
        st_ref[0] = st_all
        outs, news = [], []
        for h in range(H):
            sl = slice(K * h, K * (h + 1))
            _, _, g, k, q = _hg_gates(hq_ref[:, sl], hf_ref[:, sl], lb_ref[:, sl])
            v = hi_ref[:, sl]
            b = _cumsum_rows(g)
            b_sc[h] = b
            k_sc[h] = k
            st0 = st_all[:, sl]
            bc = b_sc[h, pl.ds(C - 1, 1), :]
            a = _hg_scores(q, k, b, b_sc.at[h], k_sc.at[h])
            outs.append(_dot_nn(a, v) + _dot_nt(q * jnp.exp(b), st0))
            news.append(st0 * jnp.exp(bc) + _dot_tn(v, k * jnp.exp(bc - b)))
        o_ref[...] = jnp.concatenate(outs, axis=1)
        s_sc[...] = jnp.concatenate(news, axis=1)

    blk = pl.BlockSpec((C, H * K), lambda c: (c, 0))
    return pl.pallas_call(
        body, name=name, grid=(NC,),
        in_specs=[blk, blk, blk, pl.BlockSpec((1, H * K), lambda c: (0, 0))],
        out_specs=[blk, pl.BlockSpec((1, K, H * K), lambda c: (c, 0, 0))],
        out_shape=[_sds((T, H * K), F32), _sds((NC, K, H * K), F32)],
        scratch_shapes=[pltpu.VMEM((K, H * K), F32), pltpu.VMEM((H, C, K), F32), pltpu.VMEM((H, C, K), F32)],
        compiler_params=_cp(("arbitrary",)),
    )(hq, hf, hi, lb)


def _hgrn_bwd(hq, hf, hi, lb, states, do, *, name):
    T = hq.shape[0]
    C, H, K, S = HG_CHUNK, HG_HEADS, HG_DK, HG_SUB
    NC = T // C

    def one_head(hq_v, hf_v, v, lbv, st0, dst1, dout, b_sc, k_sc):
        sig, f, g, k, q = _hg_gates(hq_v, hf_v, lbv)
        b = _cumsum_rows(g)
        b_sc[...] = b
        k_sc[...] = k
        bc = b_sc[pl.ds(C - 1, 1), :]
        ebc = jnp.exp(bc)
        eb = jnp.exp(b)
        ekb = jnp.exp(bc - b)
        qt = q * eb
        kb = k * ekb
        row, col, d, diag_valid = _hg_masks()
        da = jnp.where(col <= row, _dot_nt(dout, v), 0.0)
        dqt = _dot_nn(dout, st0)
        dkb = _dot_nn(v, dst1)
        new_ds = _dot_tn(dout, qt) + dst1 * ebc
        dq = dqt * eb
        dk = dkb * ekb
        a_blocks = [jnp.zeros((S, C), F32)]
        dq_blocks = [jnp.zeros((S, K), F32)]
        for i in range(1, C // S):
            r = b_sc[pl.ds(S * i - 1, 1), :]
            eq = jnp.exp(b[S * i:S * (i + 1)] - r)
            ek = jnp.exp(jnp.minimum(r - b, 0.0))
            qi = q[S * i:S * (i + 1)] * eq
            kk = k * ek
            a_blocks.append(_dot_nt(qi, kk))
            dai = jnp.where(col[S * i:S * (i + 1)] < S * i, da[S * i:S * (i + 1)], 0.0)
            dq_blocks.append(_dot_nn(dai, kk) * eq)
            dk = dk + _dot_tn(dai, qi) * ek
        dq = dq + jnp.concatenate(dq_blocks, axis=0)
        a_off = jnp.where(col < (row // S) * S, jnp.concatenate(a_blocks, axis=0), 0.0)
        same_blk = (row // S == col // S).astype(BF16)
        tmod = (lax.broadcasted_iota(jnp.int32, (C, K), 0)) % S
        a_d = jnp.zeros((C, C), F32)
        for sp in range(S):
            bs = _hg_sel_rows(b_sc, sp)
            ks = _hg_sel_rows(k_sc, sp)
            e = jnp.where(tmod >= sp, jnp.exp(jnp.minimum(b - bs, 0.0)), 0.0)
            eks = e * ks
            a_d = jnp.where(d == sp, jnp.sum(q * eks, axis=-1, keepdims=True), a_d)
            dacol = jnp.sum(jnp.where(d == sp, da, 0.0), axis=-1, keepdims=True)
            dq = dq + dacol * eks
            blk_sum = jnp.dot(same_blk, (dacol * e * q).astype(BF16), preferred_element_type=F32)
            dk = dk + jnp.where(tmod == sp, blk_sum, 0.0)
        a = a_off + jnp.where(diag_valid, a_d, 0.0)
        dv = _dot_tn(a, dout) + _dot_nt(kb, dst1)
        extra =jnp.sum(dkb * kb, axis=0, keepdims=True) + ebc * jnp.sum(st0 * dst1, axis=0, keepdims=True)
        rowk = lax.broadcasted_iota(jnp.int32, (C, K), 0)
        db = q * dq - k * dk + jnp.where(rowk == C - 1, extra, 0.0)
        dg = _rcumsum_rows(db)
        df = dg / f - dk
        return (dq * (K ** -0.5), df * (1.0 - lbv) * sig * (1.0 - sig), dv,
                jnp.sum(df * (1.0 - sig), axis=0, keepdims=True), new_ds)

    def body(hq_ref, hf_ref, hi_ref, lb_ref, st_ref, do_ref, dq_ref, dhf_ref, dv_ref, dlb_ref, ds_sc, b_sc, k_sc):
        @pl.when(pl.program_id(0) == 0)
        def _():
            ds_sc[...] = jnp.zeros_like(ds_sc)
            dlb_ref[...] = jnp.zeros_like(dlb_ref)

        st_all = st_ref[0]
        ds_all = ds_sc[...]
        res = []
        for h in range(H):
            sl = slice(K * h, K * (h + 1))
            res.append(one_head(hq_ref[:, sl], hf_ref[:, sl], hi_ref[:, sl], lb_ref[:, sl], st_all[:, sl], ds_all[:, sl],
                                do_ref[:, sl], b_sc.at[h], k_sc.at[h]))
        cat = lambda j: jnp.concatenate([r[j] for r in res], axis=1)
        dq_ref[...] = cat(0).astype(dq_ref.dtype)
        dhf_ref[...] = cat(1).astype(dhf_ref.dtype)
        dv_ref[...] = cat(2).astype(dv_ref.dtype)
        dlb_ref[...] += cat(3)
        ds_sc[...] = cat(4)

    blk = pl.BlockSpec((C, H * K), lambda c: (NC - 1 - c, 0))
    par = pl.BlockSpec((1, H * K), lambda c: (0, 0))
    return pl.pallas_call(
        body, name=name, grid=(NC,),
        in_specs=[blk, blk, blk, par, pl.BlockSpec((1, K, H * K), lambda c: (NC - 1 - c, 0, 0)), blk],
        out_specs=[blk, blk, blk, par],
        out_shape=[_sds((T, H * K), BF16)] * 3 + [_sds((1, H * K), F32)],
        scratch_shapes=[pltpu.VMEM((K, H * K), F32), pltpu.VMEM((H, C, K), F32), pltpu.VMEM((H, C, K), F32)],
        compiler_params=_cp(("arbitrary",)),
    )(hq, hf, hi, lb, states, do)


def _att_valid(n):
    R, B = ATT_GROUP * ATT_BLOCK, ATT_BLOCK
    j = lax.broadcasted_iota(jnp.int32, (2 * B, R), 0)
    t = lax.broadcasted_iota(jnp.int32, (2 * B, R), 1) % B
    dist = t + B - j
    first_key = jnp.where(n > 0, 0, B)
    return jnp.logical_and(jnp.logical_and(dist >= 0, dist < B), j >= first_key)


def _att_load(cur_ref, prev_ref, ba_ref, kv):
    hd = ATT_HD
    def cols(ref, c0):
        return ref[:, c0:c0 + hd] + ba_ref[:, c0:c0 + hd]
    qs = jnp.concatenate([cols(cur_ref, hd * (ATT_GROUP * kv + g)) for g in range(ATT_GROUP)], axis=0)
    kc = jnp.concatenate([cols(prev_ref, ATT_Q_W + hd * kv), cols(cur_ref, ATT_Q_W + hd * kv)], axis=0)
    vc = jnp.concatenate([cols(prev_ref, ATT_Q_W + ATT_KV_W + hd * kv), cols(cur_ref, ATT_Q_W + ATT_KV_W + hd * kv)], axis=0)
    return qs, kc, vc


def _att_probs(qs, kc, valid, sink_ref, kv):
    scale = 1.0 / math.sqrt(ATT_HD)
    s = jnp.where(valid, _dot_nt(kc, qs) * scale, NEG)
    sink = jnp.concatenate([jnp.full((1, ATT_BLOCK), sink_ref[0, ATT_GROUP * kv + g], F32) for g in range(ATT_GROUP)], axis=1)
    m = jnp.maximum(jnp.max(s, axis=0, keepdims=True), sink)
    p = jnp.exp(s - m)
    ps = jnp.exp(sink - m)
    inv = 1.0 / (jnp.sum(p, axis=0, keepdims=True) + ps)
    return p * inv, ps * inv


def _attn_fwd(att, b_attn, sinks, *, name):
    T = att.shape[0]
    B = ATT_BLOCK
    NB = T // B

    def body(sink_ref, cur_ref, prev_ref, ba_ref, o_ref):
        valid = _att_valid(pl.program_id(0))
        for kv in range(ATT_KV):
            qs, kc, vc = _att_load(cur_ref, prev_ref, ba_ref, kv)
            prob, _ = _att_probs(qs, kc, valid, sink_ref, kv)
            o = _dot_tn(prob, vc)
            for g in range(ATT_GROUP):
                c0 = ATT_HD * (ATT_GROUP * kv + g)
                o_ref[:, c0:c0 + ATT_HD] = o[B * g:B * (g + 1)]

    return pl.pallas_call(
        body, name=name, grid=(NB,),
        in_specs=[pl.BlockSpec(memory_space=pltpu.SMEM),
                  pl.BlockSpec((B, ATT_COLS), lambda n: (n, 0)),
                  pl.BlockSpec((B, ATT_COLS), lambda n: (jnp.maximum(n - 1, 0), 0)),
                  pl.BlockSpec((1, ATT_COLS), lambda n: (0, 0))],
        out_specs=pl.BlockSpec((B, ATT_Q_W), lambda n: (n, 0)),
        out_shape=_sds((T, ATT_Q_W), F32),
        compiler_params=_cp(("parallel",)),
    )(sinks, att, att, b_attn)


def _attn_bwd(att, b_attn, sinks, dmix, *, name):
    T = att.shape[0]
    B, hd = ATT_BLOCK, ATT_HD
    NB = T // B
    scale = 1.0 / math.sqrt(hd)

    def body(sink_ref, cur_ref, prev_ref, ba_ref, do_ref, daq_ref, dakv_ref, dsink_ref, dbq_ref, dbkv_ref,
             carry_sc, cprev_sc, ccur_sc):
        n = pl.program_id(0)

        @pl.when(n == 0)
        def _():
            carry_sc[...] = jnp.zeros_like(carry_sc)
            dsink_ref[...] = jnp.zeros_like(dsink_ref)
            dbq_ref[...] = jnp.zeros_like(dbq_ref)
            dbkv_ref[...] = jnp.zeros_like(dbkv_ref)

        @pl.when(n < NB)
        def _():
            valid = _att_valid(n)
            hrow = lax.broadcasted_iota(jnp.int32, (SUBLANES, 128), 0)
            dsink = jnp.zeros((SUBLANES, 128), F32)
            for kv in range(ATT_KV):
                qs, kc, vc = _att_load(cur_ref, prev_ref, ba_ref, kv)
                prob, psink = _att_probs(qs, kc, valid, sink_ref, kv)
                dout = jnp.concatenate(
                    [do_ref[:, hd * (ATT_GROUP * kv + g):hd * (ATT_GROUP * kv + g + 1)] for g in range(ATT_GROUP)], axis=0)
                dp = _dot_nt(vc, dout)
                delta = jnp.sum(prob * dp, axis=0, keepdims=True)
                dsc = prob * (dp - delta) * scale
                dq = _dot_tn(dsc, kc)
                dk = _dot_nn(dsc, qs)
                dvv = _dot_nn(prob, dout)
                dsk = psink * delta
                for g in range(ATT_GROUP):
                    h = ATT_GROUP * kv + g
                    daq_ref[:, hd * h:hd * (h + 1)] = dq[B * g:B * (g + 1)].astype(daq_ref.dtype)
                    tot = jnp.sum(dsk[:, B * g:B * (g + 1)], axis=1, keepdims=True)
                    dsink = dsink - jnp.where(hrow == h, tot, 0.0)
                cprev_sc[:, hd * kv:hd * (kv + 1)] = dk[:B]
                ccur_sc[:, hd * kv:hd * (kv + 1)] = dk[B:]
                cprev_sc[:, ATT_KV_W + hd * kv:ATT_KV_W + hd * (kv + 1)] = dvv[:B]
                ccur_sc[:, ATT_KV_W + hd * kv:ATT_KV_W + hd * (kv + 1)] = dvv[B:]
            dsink_ref[...] += dsink
            dbq_ref[...] += jnp.sum(daq_ref[...].astype(F32), axis=0, keepdims=True)
            done = carry_sc[...] + cprev_sc[...]
            dakv_ref[...] = done.astype(dakv_ref.dtype)
            dbkv_ref[...] += jnp.sum(done.astype(dakv_ref.dtype).astype(F32), axis=0, keepdims=True)
            carry_sc[...] = ccur_sc[...]

        @pl.when(n == NB)
        def _():
            done = carry_sc[...]
            dakv_ref[...] = done.astype(dakv_ref.dtype)
            dbkv_ref[...] += jnp.sum(done.astype(dakv_ref.dtype).astype(F32), axis=0, keepdims=True)

    cl = lambda n: jnp.minimum(n, NB - 1)
    return pl.pallas_call(
        body, name=name, grid=(NB + 1,),
        in_specs=[pl.BlockSpec(memory_space=pltpu.SMEM),
                  pl.BlockSpec((B, ATT_COLS), lambda n: (cl(n), 0)),
                  pl.BlockSpec((B, ATT_COLS), lambda n: (jnp.maximum(cl(n) - 1, 0), 0)),
                  pl.BlockSpec((1, ATT_COLS), lambda n: (0, 0)),
                  pl.BlockSpec((B, ATT_Q_W), lambda n: (cl(n), 1))],
        out_specs=[pl.BlockSpec((B, ATT_Q_W), lambda n: (cl(n), 0)),
                   pl.BlockSpec((B, 2 * ATT_KV_W), lambda n: (jnp.maximum(n - 1, 0), 0)),
                   pl.BlockSpec((SUBLANES, 128), lambda n: (0, 0)),
                   pl.BlockSpec((1, ATT_Q_W), lambda n: (0, 0)),
                   pl.BlockSpec((1, 2 * ATT_KV_W), lambda n: (0, 0))],
        out_shape=[_sds((T, ATT_Q_W), BF16), _sds((T, 2 * ATT_KV_W), BF16), _sds((SUBLANES, 128), F32),
                   _sds((1, ATT_Q_W), F32), _sds((1, 2 * ATT_KV_W), F32)],
        scratch_shapes=[pltpu.VMEM((B, 2 * ATT_KV_W), F32)] * 3,
        compiler_params=_cp(("arbitrary",)),
    )(sinks, att, att, b_attn, dmix)


def _silu_and_grad(x):
    sg = _sigmoid(x)
    return x * sg, sg * (1.0 + x * (1.0 - sg))


def _mix_fwd_fn(o_raw, hg, o_att, hgw):
    outs = []
    for h in range(HG_HEADS):
        sl = slice(HG_DK * h, HG_DK * (h + 1))
        silu, _ = _silu_and_grad(hg[:, sl])
        outs.append(_rms_fwd(o_raw[:, sl], hgw) * silu)
    outs.append(o_att)
    return (jnp.concatenate(outs, axis=1),)


def _mix_bwd_fn(o_raw, hg, dmix, hgw):
    dos, dhgs = [], []
    dw = jnp.zeros((1, HG_DK), F32)
    for h in range(HG_HEADS):
        sl = slice(HG_DK * h, HG_DK * (h + 1))
        silu, dsilu = _silu_and_grad(hg[:, sl])
        dy = dmix[:, sl]
        dhgs.append(dy * _rms_fwd(o_raw[:, sl], hgw) * dsilu)
        dx, dwh = _rms_bwd(o_raw[:, sl], hgw, dy * silu)
        dos.append(dx)
        dw = dw + dwh
    return jnp.concatenate(dos, axis=1), jnp.concatenate(dhgs, axis=1), dw


def _final_fn(h2, tgt, wf):
    d = h2.shape[1]
    err = _rms_fwd(h2, wf) - tgt
    loss_cols = (0.5 / d) * jnp.sum(err * err, axis=0, keepdims=True)
    dh2, dwf = _rms_bwd(h2, wf, err * (1.0 / d))
    return dh2, dh2, loss_cols, dwf


class _NoExchange:
    def __init__(self, weights):
        self.weights = weights

    def start(self):
        return None

    def w_in(self, after):
        return self.weights["w_in_t"]

    def rest(self, after):
        return self.weights

    def ffn_grads(self, gs):
        return None


def _local_step(x, tgt, p, ex):
    T, D = x.shape
    row = lambda n, dt: _sds((T, n), dt)
    acc = lambda n: _sds((1, n), F32)

    (u,) = _rowwise(lambda xv, w: (_rms_fwd(xv, w),), [_full(x)], [p["norm_mix_w"]], [row(D, BF16)], [], name="rms_mix",
                    after=ex.start())
    p = dict(p, w_in_t=ex.w_in(u))
    hq, hf, hi, hg, att = _mm_nt(u, p["w_in_t"], splits=[HG_W] * 4 + [ATT_COLS], out_dtype=F32, name="in_proj")
    o_raw, states = _hgrn_fwd(hq, hf, hi, p["lb"], name="hgrn_fwd")
    o_att = _attn_fwd(att, p["b_attn"], p["sinks"], name="attn_fwd")
    (mix,) = _rowwise(_mix_fwd_fn, [_full(o_raw), _full(hg), _full(o_att)], [p["hg_norm_w"]], [row(D, BF16)], [],
                      name="mix_fwd")
    p = dict(p, **ex.rest(mix))
    h1 = _mm_nn([[mix]], [p["w_out"]], out_dtype=F32, name="out_proj", residual=x)
    (v,) = _rowwise(lambda hv, w: (_rms_fwd(hv, w),), [_full(h1)], [p["norm_ffn_w"]], [row(D, BF16)], [], name="rms_ffn")
    (gp,) = _mm_nt(v, p["w_gate_t"], splits=[D_FF], out_dtype=F32, name="gate_proj")
    (up,) = _mm_nt(v, p["w_up_t"], splits=[D_FF], out_dtype=F32, name="up_proj")
    act = _convact_fwd(gp, up, p["conv_w8"], p["conv_b"], name="convact_fwd")
    h2 = _mm_nn([[act]], [p["w_down"]], out_dtype=F32, name="down_proj", residual=h1)
    dh2, dh2_b, loss_cols, d_final = _rowwise(_final_fn, [_full(h2), _full(tgt)], [p["final_norm_w"]],
                                              [row(D, F32), row(D, BF16)], [acc(D), acc(D)], name="final_loss")

    (dact,) = _mm_nt(dh2_b, p["w_down"], splits=[D_FF], out_dtype=F32, name="d_act")
    g_down = _mm_tn([act], dh2_b, name="g_down")
    dgp, dup, d_conv_w8, d_conv_b = _convact_bwd(gp, up, dact, p["conv_w8"], p["conv_b"], name="convact_bwd")
    dv = _mm_nn([[dgp], [dup]], [p["w_gate_t"], p["w_up_t"]], out_dtype=F32, name="d_v")
    g_gate_t = _mm_tn([dgp], v, name="g_gate")
    g_up_t = _mm_tn([dup], v, name="g_up")
    sent = ex.ffn_grads([g_gate_t, g_up_t, g_down])

    def ffn_norm_bwd(hv, dvv, dh2v, w):
        dx, dw = _rms_bwd(hv, w, dvv)
        dh1v = dx + dh2v
        return dh1v, dh1v, dw

    dh1, dh1_b, d_norm_ffn = _rowwise(ffn_norm_bwd, [_full(h1), _full(dv), _full(dh2)], [p["norm_ffn_w"]],
                                      [row(D, F32), row(D, BF16)], [acc(D)], name="rms_ffn_bwd", after=sent)
    (dmix,) = _mm_nt(dh1_b, p["w_out"], splits=[D], out_dtype=F32, name="d_mix")
    g_out = _mm_tn([mix], dh1_b, name="g_out")
    do_raw, dhg, d_hg_norm = _rowwise(_mix_bwd_fn, [_full(o_raw), _full(hg), (dmix, HG_W, 0, 0)], [p["hg_norm_w"]],
                                      [row(HG_W, F32), row(HG_W, BF16)], [acc(HG_DK)], name="mix_bwd")
    daq, dakv, d_sinks8, d_bq, d_bkv = _attn_bwd(att, p["b_attn"], p["sinks"], dmix, name="attn_bwd")
    dhq, dhf, dhi, d_lb = _hgrn_bwd(hq, hf, hi, p["lb"], states, do_raw, name="hgrn_bwd")
    pieces = [dhq, dhf, dhi, dhg, daq, dakv]
    du = _mm_nn([pieces], [p["w_in_t"]], out_dtype=F32, name="d_u")
    g_in_t = _mm_tn(pieces, u, name="g_in")

    def mix_norm_bwd(xv, duv, dh1v, w):
        dx, dw = _rms_bwd(xv, w, duv)
        return dx + dh1v, dw

    dx, d_norm_mix = _rowwise(mix_norm_bwd, [_full(x), _full(du), _full(dh1)], [p["norm_mix_w"]], [row(D, F32)], [acc(D)],
                              name="rms_mix_bwd")
    grads = dict(g_in_t=g_in_t, g_out=g_out, g_gate_t=g_gate_t, g_up_t=g_up_t, g_down=g_down,
                 norm_mix_w=d_norm_mix, b_attn=jnp.concatenate([d_bq, d_bkv], axis=1), lb=d_lb, hg_norm_w=d_hg_norm,
                 sinks8=d_sinks8, norm_ffn_w=d_norm_ffn, conv_w8=d_conv_w8, conv_b=d_conv_b, final_norm_w=d_final)
    return loss_cols, dx, grads


SLAB = (IN_COLS // N_CHIPS, D_FF // N_CHIPS, D_FF // N_CHIPS, D_FF // N_CHIPS, D_MODEL // N_CHIPS)
N_W = len(SLAB)
PACK_OFF = tuple(sum(SLAB[:i]) for i in range(N_W))
PACK_ROWS = sum(SLAB)
FULL_OFF = tuple(N_CHIPS * o for o in PACK_OFF)
FULL_ROWS = N_CHIPS * PACK_ROWS
HALF = tuple(s // 2 for s in SLAB)
HPACK_OFF = tuple(sum(HALF[:i]) for i in range(N_W))
HPACK_ROWS = sum(HALF)
HFULL_OFF = tuple(N_CHIPS * o for o in HPACK_OFF)
HFULL_ROWS = N_CHIPS * HPACK_ROWS
CHIP_FLIPS = ((1, 0), (0, 1), (1, 1))
N_DEV = 8
BF16_ROWS = 16
ANY = pl.BlockSpec(memory_space=pl.ANY)


def _pos():
    return lax.axis_index("x"), lax.axis_index("y"), lax.axis_index("c")


def _flip(v, f):
    return 1 - v if f else v


def _rcopy(src, dst, ssem, rsem, dev):
    return pltpu.make_async_remote_copy(src_ref=src, dst_ref=dst, send_sem=ssem, recv_sem=rsem, device_id=dev,
                                        device_id_type=pl.DeviceIdType.MESH)


def _rows(ref, start, n, align=None):
    if not isinstance(start, int):
        if align is None:
            align = SUBLANES * (4 // jnp.dtype(ref.dtype).itemsize)
        start = pl.multiple_of(start, align)
    return ref.at[pl.ds(start, n), :]


FFN_W = (1, 2, 3)
N_PEER = 1 + len(CHIP_FLIPS)
HBM = pl.BlockSpec(memory_space=pltpu.HBM)
SEM = pl.BlockSpec(memory_space=pltpu.SEMAPHORE)
EFFECT = pltpu.SideEffectType.DATAFLOW_SIDE_EFFECTING
LANES = 128


def _gather_start(pack, cw8):
    D = pack.shape[1]
    lands = [lax.empty((N_CHIPS * SLAB[0], D), pack.dtype), lax.empty((3 * N_CHIPS * SLAB[1], D), pack.dtype),
             lax.empty((N_CHIPS * SLAB[4], D), pack.dtype), lax.empty((N_CHIPS,) + cw8.shape, cw8.dtype)]
    bufs = [pack, cw8] + lands

    def body(pack_ref, cw_ref, l_in, l_ffn, l_out, l_cw, *rest):
        in_send, in_recv, rest_send, rest_recv = rest[:4]
        token = rest[-1]
        x, y, c = _pos()
        q = 2 * x + y
        peers = _gather_peers(x, y, c)
        for k, peer in enumerate(peers):
            _rcopy(_rows(pack_ref, PACK_OFF[0], SLAB[0]), _rows(l_in, q * SLAB[0], SLAB[0], BF16_ROWS),
                   in_send.at[k], in_recv.at[k], peer).start()
        for k, peer in enumerate(peers):
            for j, w in enumerate(FFN_W):
                _rcopy(_rows(pack_ref, PACK_OFF[w], SLAB[w]), _rows(l_ffn, (j * N_CHIPS + q) * SLAB[w], SLAB[w], BF16_ROWS),
                       rest_send.at[k], rest_recv.at[k], peer).start()
            _rcopy(_rows(pack_ref, PACK_OFF[4], SLAB[4]), _rows(l_out, q * SLAB[4], SLAB[4], BF16_ROWS),
                   rest_send.at[N_PEER + k], rest_recv.at[N_PEER + k], peer).start()
            _rcopy(cw_ref, l_cw.at[q], rest_send.at[2 * N_PEER + k], rest_recv.at[2 * N_PEER + k], peer).start()
        token[...] = jnp.zeros_like(token)

    outs = pl.pallas_call(
        body, name="gather_start", in_specs=[HBM] * len(bufs),
        out_specs=[SEM] * 4 + [HBM] * len(bufs) + [pl.BlockSpec(memory_space=pltpu.VMEM)],
        out_shape=[pltpu.SemaphoreType.DMA((N_PEER,)), pltpu.SemaphoreType.DMA((N_PEER,)),
                   pltpu.SemaphoreType.DMA((3 * N_PEER,)), pltpu.SemaphoreType.DMA((3 * N_PEER,))]
        + [pltpu.HBM(b.shape, b.dtype) for b in bufs] + [_sds((SUBLANES, LANES), F32)],
        input_output_aliases={i: 4 + i for i in range(len(bufs))},
        compiler_params=pltpu.CompilerParams(has_side_effects=EFFECT),
    )(*[pltpu.with_memory_space_constraint(b, pltpu.HBM) for b in bufs])
    return dict(in_sems=outs[0:2], rest_sems=outs[2:4], pack=outs[4], cw=outs[5], l_in=outs[6], l_ffn=outs[7],
                l_out=outs[8], l_cw=outs[9], token=outs[10])


def _gather_peers(x, y, c):
    return [(x, y, 1 - c)] + [(_flip(x, fx), _flip(y, fy), c) for fx, fy in CHIP_FLIPS]


def _gather_wait_in(g, after):
    def body(pack_ref, l_in, send, recv, after_ref, pack_out, l_out):
        for k, peer in enumerate(_gather_peers(*_pos())):
            cp = _rcopy(_rows(pack_ref, PACK_OFF[0], SLAB[0]), _rows(l_in, 0, SLAB[0]), send.at[k], recv.at[k], peer)
            cp.wait_send()
            cp.wait_recv()

    return pl.pallas_call(
        body, name="gather_wait_in", in_specs=[HBM, HBM, SEM, SEM, ANY], out_specs=[HBM, HBM],
        out_shape=[pltpu.HBM(g["pack"].shape, g["pack"].dtype), pltpu.HBM(g["l_in"].shape, g["l_in"].dtype)],
        input_output_aliases={0: 0, 1: 1}, compiler_params=pltpu.CompilerParams(has_side_effects=EFFECT),
    )(g["pack"], g["l_in"], *g["in_sems"], after)


def _gather_wait_rest(g, pack, after):
    n_ffn = len(FFN_W) * SLAB[FFN_W[0]]

    def body(pack_ref, cw_ref, l_ffn, l_out, l_cw, send, recv, after_ref, o_ffn, o_out, o_cw):
        for k, peer in enumerate(_gather_peers(*_pos())):
            for cp in (_rcopy(_rows(pack_ref, PACK_OFF[FFN_W[0]], n_ffn), _rows(l_ffn, 0, n_ffn), send.at[k], recv.at[k], peer),
                       _rcopy(_rows(pack_ref, PACK_OFF[4], SLAB[4]), _rows(l_out, 0, SLAB[4]),
                              send.at[N_PEER + k], recv.at[N_PEER + k], peer),
                       _rcopy(cw_ref, l_cw.at[0], send.at[2 * N_PEER + k], recv.at[2 * N_PEER + k], peer)):
                cp.wait_send()
                cp.wait_recv()

    ins = [pack, g["cw"], g["l_ffn"], g["l_out"], g["l_cw"]]
    return pl.pallas_call(
        body, name="gather_wait_rest", in_specs=[HBM] * 5 + [SEM, SEM, ANY], out_specs=[HBM] * 3,
        out_shape=[pltpu.HBM(b.shape, b.dtype) for b in ins[2:]],
        input_output_aliases={2: 0, 3: 1, 4: 2}, compiler_params=pltpu.CompilerParams(has_side_effects=EFFECT),
    )(*ins, *g["rest_sems"], after)


def _exchange_halves(ws, gs, small, *, name):
    D = gs[0].shape[1]
    n = len(ws)
    has_small = small is not None

    def body(*refs):
        g = refs[:n]
        t = refs[n + has_small:2 * n + has_small]
        sems = refs[2 * n + 2 * has_small:]
        d2d_send, d2d_recv = sems[0], sems[1]
        x, y, c = _pos()
        sib = (x, y, 1 - c)
        drains = []
        for i, w in enumerate(ws):
            h = HALF[w]
            for qq in range(N_CHIPS):
                _rcopy(_rows(g[i], qq * SLAB[w] + (1 - c) * h, h), _rows(t[i], qq * h, h),
                       d2d_send.at[i], d2d_recv.at[i], sib).start()
            drains.append(_rcopy(t[i], t[i], d2d_send.at[i], d2d_recv.at[i], sib))
        if has_small:
            small_ref, sall_ref = refs[n], refs[2 * n + 1]
            sm_send, sm_recv, loc_sem = sems[2], sems[3], sems[4]
            me = 4 * x + 2 * y + c
            own_small = pltpu.make_async_copy(small_ref, sall_ref.at[me], loc_sem)
            own_small.start()
            for f in range(1, N_DEV):
                peer = (_flip(x, f & 4), _flip(y, f & 2), _flip(c, f & 1))
                cp = _rcopy(small_ref, sall_ref.at[me], sm_send.at[f - 1], sm_recv.at[f - 1], peer)
                cp.start()
                drains.append(cp)
        for d in drains:
            d.wait_recv()
        for d in drains:
            d.wait_send()
        if has_small:
            own_small.wait()

    out_shape = [_sds((N_CHIPS * HALF[w], D), F32) for w in ws]
    scratch = [pltpu.SemaphoreType.DMA((n,)), pltpu.SemaphoreType.DMA((n,))]
    if has_small:
        out_shape.append(_sds((N_DEV,) + small.shape, F32))
        scratch += [pltpu.SemaphoreType.DMA((N_DEV - 1,)), pltpu.SemaphoreType.DMA((N_DEV - 1,)), pltpu.SemaphoreType.DMA]
    return pl.pallas_call(
        body, name=name, in_specs=[ANY] * (n + has_small), out_specs=[ANY] * (n + has_small),
        out_shape=out_shape, scratch_shapes=scratch,
    )(*gs, *([small] if has_small else []))


REDUCE_SPLIT = 2


def _chip_partial(ws, gs, theirs, *, name, out_dtype=F32):
    D = gs[0].shape[1]
    n = len(ws)

    def body(*refs):
        for i in range(n):
            refs[2 * n + i][...] = (refs[i][...] + refs[n + i][...]).astype(out_dtype)

    blk = [HALF[w] // REDUCE_SPLIT for w in ws]
    mine = [pl.BlockSpec((b, D), lambda qq, j: ((2 * qq + lax.axis_index("c")) * REDUCE_SPLIT + j, 0)) for b in blk]
    flat = [pl.BlockSpec((b, D), lambda qq, j: (qq * REDUCE_SPLIT + j, 0)) for b in blk]
    return pl.pallas_call(
        body, name=name, grid=(N_CHIPS, REDUCE_SPLIT), in_specs=mine + flat, out_specs=flat,
        out_shape=[_sds((N_CHIPS * HALF[w], D), out_dtype) for w in ws],
        compiler_params=_cp(("parallel", "parallel")),
    )(*gs, *theirs)


def _partial_copies(ws, part, got, send_sems, recv_sems):
    x, y, c = _pos()
    cps = []
    for k, (fx, fy) in enumerate(CHIP_FLIPS):
        peer = (_flip(x, fx), _flip(y, fy), c)
        qp = 2 * _flip(x, fx) + _flip(y, fy)
        for i, w in enumerate(ws):
            cps.append(_rcopy(_rows(part[i], qp * HALF[w], HALF[w]), _rows(got[i], k * HALF[w], HALF[w]),
                              send_sems.at[len(ws) * k + i], recv_sems.at[len(ws) * k + i], peer))
    return cps


def _send_chip_partials(ws, parts, *, name):
    D = parts[0].shape[1]
    n = len(ws)

    def body(*refs):
        cps = _partial_copies(ws, refs[:n], refs[n:2 * n], refs[2 * n], refs[2 * n + 1])
        for cp in cps:
            cp.start()
        for cp in cps:
            cp.wait_recv()
        for cp in cps:
            cp.wait_send()

    return pl.pallas_call(
        body, name=name, in_specs=[ANY] * n, out_specs=[ANY] * n,
        out_shape=[_sds((len(CHIP_FLIPS) * HALF[w], D), parts[0].dtype) for w in ws],
        scratch_shapes=[pltpu.SemaphoreType.DMA((len(CHIP_FLIPS) * n,)), pltpu.SemaphoreType.DMA((len(CHIP_FLIPS) * n,))],
    )(*parts)


def _send_start(ws, parts, *, name):
    D = parts[0].shape[1]
    n = len(ws)
    bufs = list(parts) + [lax.empty((len(CHIP_FLIPS) * HALF[w], D), parts[0].dtype) for w in ws]

    def body(*refs):
        send_sems, recv_sems = refs[2 * n], refs[2 * n + 1]
        for cp in _partial_copies(ws, refs[:n], refs[n:2 * n], send_sems, recv_sems):
            cp.start()
        refs[-1][...] = jnp.zeros_like(refs[-1])

    outs = pl.pallas_call(
        body, name=name, in_specs=[HBM] * (2 * n),
        out_specs=[SEM, SEM] + [HBM] * (2 * n) + [pl.BlockSpec(memory_space=pltpu.VMEM)],
        out_shape=[pltpu.SemaphoreType.DMA((len(CHIP_FLIPS) * n,)), pltpu.SemaphoreType.DMA((len(CHIP_FLIPS) * n,))]
        + [pltpu.HBM(b.shape, b.dtype) for b in bufs] + [_sds((SUBLANES, LANES), F32)],
        input_output_aliases={i: 2 + i for i in range(2 * n)},
        compiler_params=pltpu.CompilerParams(has_side_effects=EFFECT),
    )(*[pltpu.with_memory_space_constraint(b, pltpu.HBM) for b in bufs])
    return dict(sems=outs[0:2], parts=outs[2:2 + n], got=outs[2 + n:2 + 2 * n], token=outs[-1])


def _send_wait(ws, s, after, *, name):
    n = len(ws)

    def body(*refs):
        for cp in _partial_copies(ws, refs[:n], refs[n:2 * n], refs[2 * n], refs[2 * n + 1]):
            cp.wait_send()
            cp.wait_recv()

    bufs = list(s["parts"]) + list(s["got"])
    outs = pl.pallas_call(
        body, name=name, in_specs=[HBM] * (2 * n) + [SEM, SEM, ANY], out_specs=[HBM] * (2 * n),
        out_shape=[pltpu.HBM(b.shape, b.dtype) for b in bufs],
        input_output_aliases={i: i for i in range(2 * n)},
        compiler_params=pltpu.CompilerParams(has_side_effects=EFFECT),
    )(*bufs, *s["sems"], after)
    return outs[:n], outs[n:]


def _chip_reduce(ws, parts, got, *, name, after=None):
    D = parts[0].shape[1]
    nk = len(CHIP_FLIPS)
    n = len(ws)
    extra = [] if after is None else [after]

    def body(*refs):
        refs = refs[len(extra):]
        outs = refs[(1 + nk) * n:]
        for i in range(n):
            acc = refs[i][...].astype(F32)
            for k in range(nk):
                acc = acc + refs[n * (1 + k) + i][...].astype(F32)
            outs[i][...] = acc

    blk = [HALF[w] // REDUCE_SPLIT for w in ws]

    def q_idx(j):
        return (2 * lax.axis_index("x") + lax.axis_index("y")) * REDUCE_SPLIT + j

    in_specs = [pl.BlockSpec((b, D), lambda j: (q_idx(j), 0)) for b in blk]
    for k in range(nk):
        in_specs += [pl.BlockSpec((b, D), functools.partial(lambda j, k: (k * REDUCE_SPLIT + j, 0), k=k)) for b in blk]
    out_specs = [pl.BlockSpec((b, D), lambda j: (lax.axis_index("c") * REDUCE_SPLIT + j, 0)) for b in blk]
    return pl.pallas_call(
        body, name=name, grid=(REDUCE_SPLIT,), in_specs=[ANY] * len(extra) + in_specs, out_specs=out_specs,
        out_shape=[_sds((SLAB[w], D), F32) for w in ws],
        compiler_params=_cp(("parallel",)),
    )(*extra, *parts, *[g for _ in range(nk) for g in got])


def _exchange_reduced(ws, shards, *, name):
    n = len(ws)

    def body(*refs):
        ins, outs = refs[:n], refs[n:2 * n]
        send_sems, recv_sems = refs[2 * n], refs[2 * n + 1]
        x, y, c = _pos()
        sib = (x, y, 1 - c)
        cps = []
        for i, w in enumerate(ws):
            cp = _rcopy(_rows(ins[i], c * HALF[w], HALF[w]), _rows(outs[i], c * HALF[w], HALF[w]),
                        send_sems.at[i], recv_sems.at[i], sib)
            cp.start()
            cps.append(cp)
        for cp in cps:
            cp.wait_recv()
        for cp in cps:
            cp.wait_send()

    return pl.pallas_call(
        body, name=name, in_specs=[ANY] * n, out_specs=[ANY] * n,
        out_shape=[_sds(s.shape, s.dtype) for s in shards], input_output_aliases={i: i for i in range(n)},
        scratch_shapes=[pltpu.SemaphoreType.DMA((n,)), pltpu.SemaphoreType.DMA((n,))],
    )(*shards)


def _adamw_fn(w, g, m, v):
    m2 = ADAM_B1 * m + (1.0 - ADAM_B1) * g
    v2 = ADAM_B2 * v + (1.0 - ADAM_B2) * (g * g)
    m_hat = m2 / (1.0 - ADAM_B1 ** ADAM_STEP)
    v_hat = v2 / (1.0 - ADAM_B2 ** ADAM_STEP)
    return -ADAM_LR * (m_hat / (jnp.sqrt(v_hat) + ADAM_EPS) + ADAM_WD * w), m2, v2


def _adamw(w, g, m, v, *, name):
    shp = _sds(w.shape, F32)
    rows = w.shape[0]
    tm = max(t for t in range(SUBLANES, 512 + 1, SUBLANES) if rows % t == 0)
    return _rowwise(_adamw_fn, [_full(w), _full(g), _full(m), _full(v)], [], [shp] * 3, [], name=name, tm=tm)


SMALL_SEGS = (("loss", 8), ("norm_mix_w", 8), ("b_attn", 8), ("lb_logits", 8), ("hg_norm_w", 8), ("sinks", 8),
              ("norm_ffn_w", 8), ("conv_w", 72), ("conv_b", 24), ("final_norm_w", 8))
SMALL_OFF = {n: sum(r for _, r in SMALL_SEGS[:i]) for i, (n, _) in enumerate(SMALL_SEGS)}
SMALL_ROWS = sum(r for _, r in SMALL_SEGS)
LANES = 128


def _pack_small(parts):
    segs = []
    for n, r in SMALL_SEGS:
        a = parts.get(n)
        flat = jnp.zeros((0,), F32) if a is None else a.reshape(-1).astype(F32)
        segs.append(jnp.pad(flat, (0, r * LANES - flat.shape[0])).reshape(r, LANES))
    return jnp.concatenate(segs, axis=0)


def _unpack_small(pack, n, shape):
    size = math.prod(shape)
    r0 = SMALL_OFF[n]
    return pack[r0:r0 + dict(SMALL_SEGS)[n]].reshape(-1)[:size].reshape(shape)


def _small_update(sall, wp, mp, vp):
    R = SMALL_ROWS
    r_lb = SMALL_OFF["lb_logits"]

    def body(s_ref, w_ref, m_ref, v_ref, g_ref, d_ref, m2_ref, v2_ref, loss_ref):
        g = s_ref[0]
        for i in range(1, N_DEV):
            g = g + s_ref[i]
        tot = jnp.sum(jnp.sum(g[0:8], axis=1, keepdims=True), axis=0, keepdims=True)
        loss_ref[...] = jnp.broadcast_to(tot, loss_ref.shape)
        lg = w_ref[r_lb:r_lb + 8, :]
        p0 = _sigmoid(lg - pltpu.roll(lg, 4, 0))
        d = g[r_lb:r_lb + 8]
        d = d + pltpu.roll(d, 4, 0)
        sign = jnp.where(lax.broadcasted_iota(jnp.int32, d.shape, 0) < 4, 1.0, -1.0)
        g = jnp.concatenate([g[:r_lb], sign * d * p0 * (1.0 - p0), g[r_lb + 8:]], axis=0)
        g_ref[...] = g
        d_ref[...], m2_ref[...], v2_ref[...] = _adamw_fn(w_ref[...], g, m_ref[...], v_ref[...])

    full = pl.BlockSpec((R, LANES), lambda: (0, 0))
    return pl.pallas_call(
        body, name="small_update",
        in_specs=[pl.BlockSpec((N_DEV, R, LANES), lambda: (0, 0, 0)), full, full, full],
        out_specs=[full, full, full, full, pl.BlockSpec((8, LANES), lambda: (0, 0))],
        out_shape=[_sds((R, LANES), F32)] * 4 + [_sds((8, LANES), F32)],
        compiler_params=_cp(),
    )(sall, wp, mp, vp)


def _lb_fwd(lb_logits):
    n = lb_logits.shape[1]

    def body(l_ref, o_ref):
        o_ref[...] = _sigmoid(l_ref[0:1, :] - l_ref[1:2, :])

    return pl.pallas_call(body, name="lb_fwd", out_shape=_sds((1, n), F32), compiler_params=_cp())(lb_logits)


class _MeshExchange:
    def __init__(self, pack, cw8):
        self.gather = _gather_start(pack, cw8)
        self.sent = None
        self.conv_w8 = None

    def start(self):
        return self.gather["token"]

    def w_in(self, after):
        self.pack, l_in = _gather_wait_in(self.gather, after)
        return (l_in, N_CHIPS * SLAB[0], 0)

    def rest(self, after):
        l_ffn, l_out, l_cw = _gather_wait_rest(self.gather, self.pack, after)
        self.conv_w8 = jnp.concatenate([l_cw[i] for i in range(N_CHIPS)], axis=1)
        rows = N_CHIPS * SLAB[FFN_W[0]]
        return dict(w_gate_t=(l_ffn, rows, 0), w_up_t=(l_ffn, rows, 1), w_down=(l_ffn, rows, 2),
                    w_out=(l_out, N_CHIPS * SLAB[4], 0), conv_w8=self.conv_w8)

    def ffn_grads(self, gs):
        theirs = _exchange_halves(FFN_W, gs, None, name="exchange_halves_ffn")
        parts = _chip_partial(FFN_W, gs, theirs, name="chip_partial_ffn")
        self.sent = _send_start(FFN_W, parts, name="send_ffn_start")
        return self.sent["token"]


def kernel(x, norm_mix_w, w_in, b_attn, lb_logits, hg_norm_w, sinks, w_out, norm_ffn_w, w_gate, w_up, conv_w, conv_b, w_down, final_norm_w, loss_target, m_norm_mix_w, m_w_in, m_b_attn, m_lb_logits, m_hg_norm_w, m_sinks, m_w_out, m_norm_ffn_w, m_w_gate, m_w_up, m_conv_w, m_conv_b, m_w_down, m_final_norm_w, v_norm_mix_w, v_w_in, v_b_attn, v_lb_logits, v_hg_norm_w, v_sinks, v_w_out, v_norm_ffn_w, v_w_gate, v_w_up, v_conv_w, v_conv_b, v_w_down, v_final_norm_w):
    D = D_MODEL
    q = 2 * lax.axis_index("x") + lax.axis_index("y")
    ccols = D_FF // N_CHIPS

    pack = jnp.concatenate([w_in[0].T, w_gate[0].T, w_up[0].T, w_down[0], w_out[0]], axis=0).astype(BF16)
    cw8 = jnp.concatenate([conv_w[0], jnp.zeros((SUBLANES - 3, ccols), F32)], axis=0)
    ex = _MeshExchange(pack, cw8)
    p = dict(norm_mix_w=norm_mix_w, b_attn=b_attn, lb=_lb_fwd(lb_logits), hg_norm_w=hg_norm_w, sinks=sinks,
             norm_ffn_w=norm_ffn_w, conv_b=conv_b, final_norm_w=final_norm_w.reshape(1, D))
    loss_cols, dx, g = _local_step(x[0], loss_target[0], p, ex)
    conv_w8 = ex.conv_w8

    small = _pack_small(dict(loss=loss_cols, norm_mix_w=g["norm_mix_w"], b_attn=g["b_attn"], lb_logits=g["lb"],
                             hg_norm_w=g["hg_norm_w"], sinks=g["sinks8"], norm_ffn_w=g["norm_ffn_w"],
                             conv_w=g["conv_w8"][:3], conv_b=g["conv_b"], final_norm_w=g["final_norm_w"]))
    parts_ffn, got_ffn = _send_wait(FFN_W, ex.sent, dx, name="send_ffn_wait")
    late = (0, 4)
    gs = [g["g_in_t"], g["g_out"]]
    *theirs, sall = _exchange_halves(late, gs, small, name="exchange_halves_late")
    parts_late = _chip_partial(late, gs, theirs, name="chip_partial_late", out_dtype=BF16)
    sent_late = _send_start(late, parts_late, name="send_late_start")
    big = {}

    def finish(ws, parts, got, specs, tag, after):
        shards = _exchange_reduced(ws, _chip_reduce(ws, parts, got, name="chip_reduce_" + tag, after=after),
                                   name="exchange_reduced_" + tag)
        for gw, (n, w, m, v, tr) in zip(shards, specs):
            view = (lambda a: a[0].T) if tr else (lambda a: a[0])
            back = (lambda a: a.T[None]) if tr else (lambda a: a[None])
            d_, m_, v_ = _adamw(view(w), gw, view(m), view(v), name="adamw_" + n)
            big[n] = (back(gw), back(d_), back(m_), back(v_))
        return d_

    last = finish(FFN_W, parts_ffn, got_ffn, (("w_gate", w_gate, m_w_gate, v_w_gate, True), ("w_up", w_up, m_w_up, v_w_up, True),
                                              ("w_down", w_down, m_w_down, v_w_down, False)), "ffn", sent_late["token"])
    parts_late, got_late = _send_wait(late, sent_late, last, name="send_late_wait")
    finish(late, parts_late, got_late, (("w_in", w_in, m_w_in, v_w_in, True), ("w_out", w_out, m_w_out, v_w_out, False)),
           "late", None)

    def place(a):
        return lax.dynamic_update_slice(jnp.zeros((3, D_FF), F32), a[0], (0, q * ccols))

    def small_pack(ws, cw):
        nm, ba, lbl, hg, sk, nf, cb, fn = ws
        return _pack_small(dict(norm_mix_w=nm, b_attn=ba, lb_logits=lbl, hg_norm_w=hg,
                                sinks=jnp.broadcast_to(sk.reshape(ATT_HEADS, 1), (ATT_HEADS, LANES)), norm_ffn_w=nf,
                                conv_w=cw, conv_b=cb, final_norm_w=fn))

    wp = small_pack((norm_mix_w, b_attn, lb_logits, hg_norm_w, sinks, norm_ffn_w, conv_b, final_norm_w), conv_w8[:3])
    mp = small_pack((m_norm_mix_w, m_b_attn, m_lb_logits, m_hg_norm_w, m_sinks, m_norm_ffn_w, m_conv_b, m_final_norm_w),
                    place(m_conv_w))
    vp = small_pack((v_norm_mix_w, v_b_attn, v_lb_logits, v_hg_norm_w, v_sinks, v_norm_ffn_w, v_conv_b, v_final_norm_w),
                    place(v_conv_w))
    outs = _small_update(sall, wp, mp, vp)
    loss = outs[4][0, 0]

    def small_out(pk, n, ref):
        if n == "sinks":
            return pk[SMALL_OFF[n]:SMALL_OFF[n] + ATT_HEADS, 0].reshape(ref.shape)
        if n == "conv_w":
            full = _unpack_small(pk, n, (3, D_FF))
            return lax.dynamic_slice(full, (0, q * ccols), (3, ccols))[None]
        return _unpack_small(pk, n, ref.shape)

    refs = dict(norm_mix_w=norm_mix_w, b_attn=b_attn, lb_logits=lb_logits, hg_norm_w=hg_norm_w, sinks=sinks,
                norm_ffn_w=norm_ffn_w, conv_w=conv_w, conv_b=conv_b, final_norm_w=final_norm_w)
    order = ("norm_mix_w", "w_in", "b_attn", "lb_logits", "hg_norm_w", "sinks", "w_out", "norm_ffn_w", "w_gate", "w_up",
             "conv_w", "conv_b", "w_down", "final_norm_w")
    res = [loss, dx[None]]
    for k in range(4):
        for n in order:
            res.append(big[n][k] if n in big else small_out(outs[k], n, refs[n]))
    return tuple(res)
```

```python
import functools
import math

import jax
import jax.numpy as jnp
from jax import lax
from jax.experimental import pallas as pl
from jax.experimental.pallas import tpu as pltpu

F32 = jnp.float32
BF16 = jnp.bfloat16

D_MODEL = 1024
HG_HEADS = 4
HG_DK = 128
HG_W = HG_HEADS * HG_DK
HG_CHUNK = 64
HG_SUB = 8
ATT_HEADS = 8
ATT_KV = 2
ATT_GROUP = ATT_HEADS // ATT_KV
ATT_HD = 64
ATT_BLOCK = 128
ATT_Q_W = ATT_HEADS * ATT_HD
ATT_KV_W = ATT_KV * ATT_HD
ATT_COLS = ATT_Q_W + 2 * ATT_KV_W
IN_COLS = 4 * HG_W + ATT_COLS
D_FF = 2816
EPS = 1e-6
ADAM_LR, ADAM_B1, ADAM_B2, ADAM_EPS, ADAM_WD, ADAM_STEP = 0.001, 0.9, 0.999, 1e-08, 0.01, 10
NEG = -1e30

V7X_VMEM_BYTES = 64 * 1024 * 1024
VMEM_LIMIT = 48 * 1024 * 1024
SUBLANES = 8

N_CHIPS = 4


def _cp(sem=None, **kw):
    return pltpu.CompilerParams(dimension_semantics=sem, vmem_limit_bytes=VMEM_LIMIT, **kw)


def _sds(shape, dtype):
    return jax.ShapeDtypeStruct(shape, dtype)


def _wspec(w):
    arr, rows, blk = w
    return pl.BlockSpec((rows, arr.shape[1]), lambda i: (blk, 0))


def _mm_nt(a, w, *, splits, out_dtype, name, after=None, tm=512):
    M, K = a.shape
    N = w[1]
    tm = min(tm, M)
    assert sum(splits) == N and M % tm == 0
    offs = [sum(splits[:i]) for i in range(len(splits))]
    n_in = 2 if after is None else 3

    def body(*refs):
        a_ref, w_ref = refs[0], refs[1]
        acc = lax.dot_general(a_ref[...], w_ref[...], (((1,), (1,)), ((), ())), preferred_element_type=F32)
        for o_ref, c0, n in zip(refs[n_in:], offs, splits):
            o_ref[...] = acc[:, c0:c0 + n].astype(out_dtype)

    in_specs = [pl.BlockSpec((tm, K), lambda i: (i, 0)), _wspec(w)]
    args = [a, w[0]]
    if after is not None:
        in_specs.append(pl.BlockSpec(memory_space=pl.ANY))
        args.append(after)
    outs = pl.pallas_call(
        body, name=name, grid=(M // tm,), in_specs=in_specs,
        out_specs=[pl.BlockSpec((tm, n), lambda i: (i, 0)) for n in splits],
        out_shape=[_sds((M, n), out_dtype) for n in splits],
        compiler_params=_cp(("parallel",)),
    )(*args)
    return outs


def _mm_nn(pieces, ws, *, name, out_dtype=F32, residual=None, epilogue=None, tm=512):
    M = pieces[0][0].shape[0]
    K = ws[0][0].shape[1]
    tm = min(tm, M)
    flat = [p for grp in pieces for p in grp]
    n_p = len(flat)
    n_w = len(ws)
    fn, row_ins, bc_ins, row_outs, acc_outs = epilogue or (None, [], [], [_sds((M, K), out_dtype)], [])
    if residual is not None:
        assert epilogue is None
        row_ins = [residual]
    n_r, n_b, n_o = len(row_ins), len(bc_ins), len(row_outs)

    def body(*refs):
        p_refs = refs[:n_p]
        w_refs = refs[n_p:n_p + n_w]
        extra = [r[...] for r in refs[n_p + n_w:n_p + n_w + n_r + n_b]]
        o_refs = refs[n_p + n_w + n_r + n_b:n_p + n_w + n_r + n_b + n_o]
        a_refs = refs[n_p + n_w + n_r + n_b + n_o:]
        acc = None
        k = 0
        for gi, grp in enumerate(pieces):
            c0 = 0
            for p in grp:
                n = p.shape[1]
                t = jnp.dot(p_refs[k][...], w_refs[gi][c0:c0 + n, :], preferred_element_type=F32)
                acc = t if acc is None else acc + t
                c0 += n
                k += 1
        if fn is None:
            res = (acc + extra[0] if residual is not None else acc,)
        else:
            res = fn(acc, *extra)
        for o_ref, val in zip(o_refs, res[:n_o]):
            o_ref[...] = val.astype(o_ref.dtype)
        if acc_outs:
            @pl.when(pl.program_id(0) == 0)
            def _():
                for a_ref in a_refs:
                    a_ref[...] = jnp.zeros_like(a_ref)
            for a_ref, val in zip(a_refs, res[n_o:]):
                a_ref[...] += val

    in_specs = [pl.BlockSpec((tm, p.shape[1]), lambda i: (i, 0)) for p in flat]
    in_specs += [_wspec(w) for w in ws]
    in_specs += [pl.BlockSpec((tm, r.shape[1]), lambda i: (i, 0)) for r in row_ins]
    in_specs += [pl.BlockSpec(b.shape, lambda i: (0, 0)) for b in bc_ins]
    out_specs = [pl.BlockSpec((tm, s.shape[1]), lambda i: (i, 0)) for s in row_outs]
    out_specs += [pl.BlockSpec(s.shape, lambda i: (0, 0)) for s in acc_outs]
    outs = pl.pallas_call(
        body, name=name, grid=(M // tm,), in_specs=in_specs, out_specs=out_specs,
        out_shape=list(row_outs) + list(acc_outs),
        compiler_params=_cp(("arbitrary",) if acc_outs else ("parallel",)),
    )(*flat, *[w[0] for w in ws], *row_ins, *bc_ins)
    return outs if epilogue is not None else outs[0]


def _mm_tn(pieces, x, *, name, tt=512):
    M, K = x.shape
    tt = min(tt, M)
    ns = [p.shape[1] for p in pieces]
    offs = [sum(ns[:i]) for i in range(len(ns))]
    N = sum(ns)
    n_p = len(pieces)

    def body(*refs):
        p_refs = refs[:n_p]
        x_ref = refs[n_p]
        o_ref = refs[n_p + 1]

        @pl.when(pl.program_id(0) == 0)
        def _():
            o_ref[...] = jnp.zeros_like(o_ref)

        xv = x_ref[...]
        for p_ref, c0, n in zip(p_refs, offs, ns):
            o_ref[c0:c0 + n, :] += lax.dot_general(p_ref[...], xv, (((0,), (0,)), ((), ())),
                                                    preferred_element_type=F32)

    in_specs = [pl.BlockSpec((tt, n), lambda i: (i, 0)) for n in ns]
    in_specs.append(pl.BlockSpec((tt, K), lambda i: (i, 0)))
    return pl.pallas_call(
        body, name=name, grid=(M // tt,), in_specs=in_specs,
        out_specs=pl.BlockSpec((N, K), lambda i: (0, 0)),
        out_shape=_sds((N, K), F32),
        compiler_params=_cp(("arbitrary",)),
    )(*pieces, x)


def _rms_fwd(xf, w):
    inv = lax.rsqrt(jnp.mean(xf * xf, axis=-1, keepdims=True) + EPS)
    return xf * inv * w


def _rms_bwd(xf, w, dy):
    inv = lax.rsqrt(jnp.mean(xf * xf, axis=-1, keepdims=True) + EPS)
    xhat = xf * inv
    dxhat = dy * w
    dx = inv * (dxhat - xhat * jnp.mean(dxhat * xhat, axis=-1, keepdims=True))
    dw = jnp.sum(dy * xhat, axis=0, keepdims=True)
    return dx, dw


def _sigmoid(x):
    return 1.0 / (1.0 + jnp.exp(-x))


def _rowwise(fn, row_ins, bc_ins, row_outs, acc_outs, *, name, tm=256, after=None):
    M = row_outs[0].shape[0] if row_outs else row_ins[0][0].shape[0]
    assert M % tm == 0 and tm % SUBLANES == 0, (name, M, tm)
    n_r, n_b, n_o, n_a = len(row_ins), len(bc_ins), len(row_outs), len(acc_outs)
    n_after = 0 if after is None else 1

    def body(*refs):
        refs = refs[n_after:]
        ins = [r[...] for r in refs[:n_r + n_b]]
        o_refs = refs[n_r + n_b:n_r + n_b + n_o]
        a_refs = refs[n_r + n_b + n_o:]
        res = fn(*ins)
        for o_ref, val in zip(o_refs, res[:n_o]):
            o_ref[...] = val.astype(o_ref.dtype)
        if n_a:
            @pl.when(pl.program_id(0) == 0)
            def _():
                for a_ref in a_refs:
                    a_ref[...] = jnp.zeros_like(a_ref)
            for a_ref, val in zip(a_refs, res[n_o:]):
                a_ref[...] += val

    in_specs = [pl.BlockSpec((tm, cw), functools.partial(lambda i, cb, r0: (i + r0, cb), cb=cb, r0=r0))
                for (_, cw, cb, r0) in row_ins]
    in_specs += [pl.BlockSpec(b.shape, lambda i: (0, 0)) for b in bc_ins]
    out_specs = [pl.BlockSpec((tm, s.shape[1]), lambda i: (i, 0)) for s in row_outs]
    out_specs += [pl.BlockSpec(s.shape, lambda i: (0, 0)) for s in acc_outs]
    if n_after:
        in_specs = [pl.BlockSpec(memory_space=pl.ANY)] + in_specs
    return pl.pallas_call(
        body, name=name, grid=(M // tm,), in_specs=in_specs, out_specs=out_specs,
        out_shape=list(row_outs) + list(acc_outs),
        compiler_params=_cp(("arbitrary",) if n_a else ("parallel",)),
    )(*([after] if n_after else []), *[r[0] for r in row_ins], *bc_ins)


def _full(a, first_row_block=0):
    return (a, a.shape[1], 0, first_row_block)


def _conv_rows(ext, w_ref_val, lo):
    s1 = pltpu.roll(ext, 1, 0)
    s2 = pltpu.roll(ext, 2, 0)
    y = w_ref_val[0:1, :] * s2 + w_ref_val[1:2, :] * s1 + w_ref_val[2:3, :] * ext
    return y[SUBLANES:, :]


def _convact_fwd(gp, up, conv_w8, conv_b, *, name, tr=128, tc=1408):
    T, C = gp.shape
    tr = min(tr, T)
    hb = tr // SUBLANES

    def body(gp_ref, gph_ref, up_ref, w_ref, b_ref, act_ref):
        i = pl.program_id(1)
        halo = jnp.where(i > 0, gph_ref[...], 0.0)
        ext = jnp.concatenate([halo, gp_ref[...]], axis=0)
        gate = _conv_rows(ext, w_ref[...], 0) + b_ref[...]
        act_ref[...] = (gate * _sigmoid(gate) * up_ref[...]).astype(act_ref.dtype)

    return pl.pallas_call(
        body, name=name, grid=(C // tc, T // tr),
        in_specs=[pl.BlockSpec((tr, tc), lambda j, i: (i, j)),
                  pl.BlockSpec((SUBLANES, tc), lambda j, i: (jnp.maximum(i * hb - 1, 0), j)),
                  pl.BlockSpec((tr, tc), lambda j, i: (i, j)),
                  pl.BlockSpec((SUBLANES, tc), lambda j, i: (0, j)),
                  pl.BlockSpec((1, tc), lambda j, i: (0, j))],
        out_specs=pl.BlockSpec((tr, tc), lambda j, i: (i, j)),
        out_shape=_sds((T, C), BF16),
        compiler_params=_cp(("parallel", "parallel")),
    )(gp, gp, up, conv_w8, conv_b)


def _convact_bwd(gp, up, dact, conv_w8, conv_b, *, name, tr=128, tc=1408):
    T, C = gp.shape
    tr = min(tr, T)
    hb = tr // SUBLANES
    nr = T // tr

    def body(gp_ref, gpp_ref, gpn_ref, up_ref, upn_ref, da_ref, dan_ref, w_ref, b_ref,
             dgp_ref, dup_ref, dw_ref, db_ref):
        i = pl.program_id(1)
        w = w_ref[...]
        prev = jnp.where(i > 0, gpp_ref[...], 0.0)
        last = i == nr - 1
        gp_ext = jnp.concatenate([prev, gp_ref[...], gpn_ref[...]], axis=0)
        gate = _conv_rows(gp_ext, w, 0) + b_ref[...]
        up_e = jnp.concatenate([up_ref[...], upn_ref[...]], axis=0)
        da_e = jnp.concatenate([da_ref[...], dan_ref[...]], axis=0)
        row = lax.broadcasted_iota(jnp.int32, gate.shape, 0)
        valid = jnp.logical_or(row < tr, jnp.logical_not(last))
        sg = _sigmoid(gate)
        silu = gate * sg
        dgate = jnp.where(valid, da_e * up_e * (sg * (1.0 + gate * (1.0 - sg))), 0.0)
        dup_ref[...] = (da_e[:tr] * silu[:tr]).astype(dup_ref.dtype)
        n = tr + SUBLANES
        g1 = pltpu.roll(dgate, n - 1, 0)
        g2 = pltpu.roll(dgate, n - 2, 0)
        dgp = w[2:3, :] * dgate + w[1:2, :] * g1 + w[0:1, :] * g2
        dgp_ref[...] = dgp[:tr].astype(dgp_ref.dtype)
        gpc = gp_ref[...]
        dw0 = jnp.sum(gpc * g2[:tr], axis=0, keepdims=True)
        dw1 = jnp.sum(gpc * g1[:tr], axis=0, keepdims=True)
        dw2 = jnp.sum(gpc * dgate[:tr], axis=0, keepdims=True)
        dbv = jnp.sum(dgate[:tr], axis=0, keepdims=True)
        z = jnp.zeros((SUBLANES - 3, gpc.shape[1]), F32)

        @pl.when(i == 0)
        def _():
            dw_ref[...] = jnp.zeros_like(dw_ref)
            db_ref[...] = jnp.zeros_like(db_ref)

        dw_ref[...] += jnp.concatenate([dw0, dw1, dw2, z], axis=0)
        db_ref[...] += dbv

    cur = pl.BlockSpec((tr, tc), lambda j, i: (i, j))
    prv = pl.BlockSpec((SUBLANES, tc), lambda j, i: (jnp.maximum(i * hb - 1, 0), j))
    nxt = pl.BlockSpec((SUBLANES, tc), lambda j, i: (jnp.minimum((i + 1) * hb, T // SUBLANES - 1), j))
    return pl.pallas_call(
        body, name=name, grid=(C // tc, nr),
        in_specs=[cur, prv, nxt, cur, nxt, cur, nxt,
                  pl.BlockSpec((SUBLANES, tc), lambda j, i: (0, j)),
                  pl.BlockSpec((1, tc), lambda j, i: (0, j))],
        out_specs=[cur, cur,
                   pl.BlockSpec((SUBLANES, tc), lambda j, i: (0, j)),
                   pl.BlockSpec((1, tc), lambda j, i: (0, j))],
        out_shape=[_sds((T, C), BF16), _sds((T, C), BF16), _sds((SUBLANES, C), F32), _sds((1, C), F32)],
        compiler_params=_cp(("parallel", "arbitrary")),
    )(gp, gp, gp, up, up, dact, dact, conv_w8, conv_b)


def _cumsum_rows(x):
    n = x.shape[0]
    row = lax.broadcasted_iota(jnp.int32, x.shape, 0)
    s = 1
    while s < n:
        x = x + jnp.where(row >= s, pltpu.roll(x, s, 0), 0.0)
        s *= 2
    return x


def _rcumsum_rows(x):
    n = x.shape[0]
    row = lax.broadcasted_iota(jnp.int32, x.shape, 0)
    s = 1
    while s < n:
        x = x + jnp.where(row < n - s, pltpu.roll(x, n - s, 0), 0.0)
        s *= 2
    return x


def _dot_nt(a, b):
    return lax.dot_general(a.astype(BF16), b.astype(BF16), (((1,), (1,)), ((), ())), preferred_element_type=F32)


def _dot_tn(a, b):
    return lax.dot_general(a.astype(BF16), b.astype(BF16), (((0,), (0,)), ((), ())), preferred_element_type=F32)


def _dot_nn(a, b):
    return jnp.dot(a.astype(BF16), b.astype(BF16), preferred_element_type=F32)


def _hg_gates(hq, hf, lbv):
    sig = _sigmoid(hf)
    f = lbv + (1.0 - lbv) * sig
    return sig, f, jnp.log(f), 1.0 - f, hq * (HG_DK ** -0.5)


def _hg_sel_rows(ref, sp):
    return jnp.concatenate(
        [jnp.broadcast_to(ref[pl.ds(HG_SUB * i + sp, 1), :], (HG_SUB, HG_DK)) for i in range(HG_CHUNK // HG_SUB)], axis=0)


def _hg_masks():
    C = HG_CHUNK
    row = lax.broadcasted_iota(jnp.int32, (C, C), 0)
    col = lax.broadcasted_iota(jnp.int32, (C, C), 1)
    d = col - (row // HG_SUB) * HG_SUB
    tmod = row % HG_SUB
    diag_valid = jnp.logical_and(d >= 0, d <= tmod)
    return row, col, d, diag_valid


def _hg_scores(q, k, b, b_sc, k_sc):
    C, S = HG_CHUNK, HG_SUB
    row, col, d, diag_valid = _hg_masks()
    blocks = [jnp.zeros((S, C), F32)]
    for i in range(1, C // S):
        r = b_sc[pl.ds(S * i - 1, 1), :]
        qi = q[S * i:S * (i + 1)] * jnp.exp(b[S * i:S * (i + 1)] - r)
        kk = k * jnp.exp(jnp.minimum(r - b, 0.0))
        blocks.append(_dot_nt(qi, kk))
    a_off = jnp.where(col < (row // S) * S, jnp.concatenate(blocks, axis=0), 0.0)
    a_d = jnp.zeros((C, C), F32)
    for sp in range(S):
        bs = _hg_sel_rows(b_sc, sp)
        ks = _hg_sel_rows(k_sc, sp)
        e = jnp.exp(jnp.minimum(b - bs, 0.0))
        colv = jnp.sum(q * ks * e, axis=-1, keepdims=True)
        a_d = jnp.where(d == sp, colv, a_d)
    return a_off + jnp.where(diag_valid, a_d, 0.0)


def _hgrn_fwd(hq, hf, hi, lb, *, name):
    T = hq.shape[0]
    C, H, K = HG_CHUNK, HG_HEADS, HG_DK
    NC = T // C

    def body(hq_ref, hf_ref, hi_ref, lb_ref, o_ref, st_ref, s_sc, b_sc, k_sc):
        @pl.when(pl.program_id(0) == 0)
        def _():
            s_sc[...] = jnp.zeros_like(s_sc)

        st_all = s_sc[...]
        st_ref[0] = st_all
        outs, news = [], []
        for h in range(H):
            sl = slice(K * h, K * (h + 1))
            _, _, g, k, q = _hg_gates(hq_ref[:, sl], hf_ref[:, sl], lb_ref[:, sl])
            v = hi_ref[:, sl]
            b = _cumsum_rows(g)
            b_sc[h] = b
            k_sc[h] = k
            st0 = st_all[:, sl]
            bc = b_sc[h, pl.ds(C - 1, 1), :]
            a = _hg_scores(q, k, b, b_sc.at[h], k_sc.at[h])
            outs.append(_dot_nn(a, v) + _dot_nt(q * jnp.exp(b), st0))
            news.append(st0 * jnp.exp(bc) + _dot_tn(v, k * jnp.exp(bc - b)))
        o_ref[...] = jnp.concatenate(outs, axis=1)
        s_sc[...] = jnp.concatenate(news, axis=1)

    blk = pl.BlockSpec((C, H * K), lambda c: (c, 0))
    return pl.pallas_call(
        body, name=name, grid=(NC,),
        in_specs=[blk, blk, blk, pl.BlockSpec((1, H * K), lambda c: (0, 0))],
        out_specs=[blk, pl.BlockSpec((1, K, H * K), lambda c: (c, 0, 0))],
        out_shape=[_sds((T, H * K), F32), _sds((NC, K, H * K), F32)],
        scratch_shapes=[pltpu.VMEM((K, H * K), F32), pltpu.VMEM((H, C, K), F32), pltpu.VMEM((H, C, K), F32)],
        compiler_params=_cp(("arbitrary",)),
    )(hq, hf, hi, lb)


def _hgrn_bwd(hq, hf, hi, lb, states, do, *, name):
    T = hq.shape[0]
    C, H, K, S = HG_CHUNK, HG_HEADS, HG_DK, HG_SUB
    NC = T // C

    def one_head(hq_v, hf_v, v, lbv, st0, dst1, dout, b_sc, k_sc):
        sig, f, g, k, q = _hg_gates(hq_v, hf_v, lbv)
        b = _cumsum_rows(g)
        b_sc[...] = b
        k_sc[...] = k
        bc = b_sc[pl.ds(C - 1, 1), :]
        ebc = jnp.exp(bc)
        eb = jnp.exp(b)
        ekb = jnp.exp(bc - b)
        qt = q * eb
        kb = k * ekb
        row, col, d, diag_valid = _hg_masks()
        da = jnp.where(col <= row, _dot_nt(dout, v), 0.0)
        dqt = _dot_nn(dout, st0)
        dkb = _dot_nn(v, dst1)
        new_ds = _dot_tn(dout, qt) + dst1 * ebc
        dq = dqt * eb
        dk = dkb * ekb
        a_blocks = [jnp.zeros((S, C), F32)]
        dq_blocks = [jnp.zeros((S, K), F32)]
        for i in range(1, C // S):
            r = b_sc[pl.ds(S * i - 1, 1), :]
            eq = jnp.exp(b[S * i:S * (i + 1)] - r)
            ek = jnp.exp(jnp.minimum(r - b, 0.0))
            qi = q[S * i:S * (i + 1)] * eq
            kk = k * ek
            a_blocks.append(_dot_nt(qi, kk))
            dai = jnp.where(col[S * i:S * (i + 1)] < S * i, da[S * i:S * (i + 1)], 0.0)
            dq_blocks.append(_dot_nn(dai, kk) * eq)
            dk = dk + _dot_tn(dai, qi) * ek
        dq = dq + jnp.concatenate(dq_blocks, axis=0)
        a_off = jnp.where(col < (row // S) * S, jnp.concatenate(a_blocks, axis=0), 0.0)
        same_blk = (row // S == col // S).astype(BF16)
        tmod = (lax.broadcasted_iota(jnp.int32, (C, K), 0)) % S
        a_d = jnp.zeros((C, C), F32)
        for sp in range(S):
            bs = _hg_sel_rows(b_sc, sp)
            ks = _hg_sel_rows(k_sc, sp)
            e = jnp.where(tmod >= sp, jnp.exp(jnp.minimum(b - bs, 0.0)), 0.0)
            eks = e * ks
            a_d = jnp.where(d == sp, jnp.sum(q * eks, axis=-1, keepdims=True), a_d)
            dacol = jnp.sum(jnp.where(d == sp, da, 0.0), axis=-1, keepdims=True)
            dq = dq + dacol * eks
            blk_sum = jnp.dot(same_blk, (dacol * e * q).astype(BF16), preferred_element_type=F32)
            dk = dk + jnp.where(tmod == sp, blk_sum, 0.0)
        a = a_off + jnp.where(diag_valid, a_d, 0.0)
        dv = _dot_tn(a, dout) + _dot_nt(kb, dst1)
        extra =jnp.sum(dkb * kb, axis=0, keepdims=True) + ebc * jnp.sum(st0 * dst1, axis=0, keepdims=True)
        rowk = lax.broadcasted_iota(jnp.int32, (C, K), 0)
        db = q * dq - k * dk + jnp.where(rowk == C - 1, extra, 0.0)
        dg = _rcumsum_rows(db)
        df = dg / f - dk
        return (dq * (K ** -0.5), df * (1.0 - lbv) * sig * (1.0 - sig), dv,
                jnp.sum(df * (1.0 - sig), axis=0, keepdims=True), new_ds)

    def body(hq_ref, hf_ref, hi_ref, lb_ref, st_ref, do_ref, dq_ref, dhf_ref, dv_ref, dlb_ref, ds_sc, b_sc, k_sc):
        @pl.when(pl.program_id(0) == 0)
        def _():
            ds_sc[...] = jnp.zeros_like(ds_sc)
            dlb_ref[...] = jnp.zeros_like(dlb_ref)

        st_all = st_ref[0]
        ds_all = ds_sc[...]
        res = []
        for h in range(H):
            sl = slice(K * h, K * (h + 1))
            res.append(one_head(hq_ref[:, sl], hf_ref[:, sl], hi_ref[:, sl], lb_ref[:, sl], st_all[:, sl], ds_all[:, sl],
                                do_ref[:, sl], b_sc.at[h], k_sc.at[h]))
        cat = lambda j: jnp.concatenate([r[j] for r in res], axis=1)
        dq_ref[...] = cat(0).astype(dq_ref.dtype)
        dhf_ref[...] = cat(1).astype(dhf_ref.dtype)
        dv_ref[...] = cat(2).astype(dv_ref.dtype)
        dlb_ref[...] += cat(3)
        ds_sc[...] = cat(4)

    blk = pl.BlockSpec((C, H * K), lambda c: (NC - 1 - c, 0))
    par = pl.BlockSpec((1, H * K), lambda c: (0, 0))
    return pl.pallas_call(
        body, name=name, grid=(NC,),
        in_specs=[blk, blk, blk, par, pl.BlockSpec((1, K, H * K), lambda c: (NC - 1 - c, 0, 0)), blk],
        out_specs=[blk, blk, blk, par],
        out_shape=[_sds((T, H * K), BF16)] * 3 + [_sds((1, H * K), F32)],
        scratch_shapes=[pltpu.VMEM((K, H * K), F32), pltpu.VMEM((H, C, K), F32), pltpu.VMEM((H, C, K), F32)],
        compiler_params=_cp(("arbitrary",)),
    )(hq, hf, hi, lb, states, do)


def _att_valid(n):
    R, B = ATT_GROUP * ATT_BLOCK, ATT_BLOCK
    j = lax.broadcasted_iota(jnp.int32, (2 * B, R), 0)
    t = lax.broadcasted_iota(jnp.int32, (2 * B, R), 1) % B
    dist = t + B - j
    first_key = jnp.where(n > 0, 0, B)
    return jnp.logical_and(jnp.logical_and(dist >= 0, dist < B), j >= first_key)


def _att_load(cur_ref, prev_ref, ba_ref, kv):
    hd = ATT_HD
    def cols(ref, c0):
        return ref[:, c0:c0 + hd] + ba_ref[:, c0:c0 + hd]
    qs = jnp.concatenate([cols(cur_ref, hd * (ATT_GROUP * kv + g)) for g in range(ATT_GROUP)], axis=0)
    kc = jnp.concatenate([cols(prev_ref, ATT_Q_W + hd * kv), cols(cur_ref, ATT_Q_W + hd * kv)], axis=0)
    vc = jnp.concatenate([cols(prev_ref, ATT_Q_W + ATT_KV_W + hd * kv), cols(cur_ref, ATT_Q_W + ATT_KV_W + hd * kv)], axis=0)
    return qs, kc, vc


def _att_probs(qs, kc, valid, sink_ref, kv):
    scale = 1.0 / math.sqrt(ATT_HD)
    s = jnp.where(valid, _dot_nt(kc, qs) * scale, NEG)
    sink = jnp.concatenate([jnp.full((1, ATT_BLOCK), sink_ref[0, ATT_GROUP * kv + g], F32) for g in range(ATT_GROUP)], axis=1)
    m = jnp.maximum(jnp.max(s, axis=0, keepdims=True), sink)
    p = jnp.exp(s - m)
    ps = jnp.exp(sink - m)
    inv = 1.0 / (jnp.sum(p, axis=0, keepdims=True) + ps)
    return p * inv, ps * inv


def _attn_fwd(att, b_attn, sinks, *, name):
    T = att.shape[0]
    B = ATT_BLOCK
    NB = T // B

    def body(sink_ref, cur_ref, prev_ref, ba_ref, o_ref):
        valid = _att_valid(pl.program_id(0))
        for kv in range(ATT_KV):
            qs, kc, vc = _att_load(cur_ref, prev_ref, ba_ref, kv)
            prob, _ = _att_probs(qs, kc, valid, sink_ref, kv)
            o = _dot_tn(prob, vc)
            for g in range(ATT_GROUP):
                c0 = ATT_HD * (ATT_GROUP * kv + g)
                o_ref[:, c0:c0 + ATT_HD] = o[B * g:B * (g + 1)]

    return pl.pallas_call(
        body, name=name, grid=(NB,),
        in_specs=[pl.BlockSpec(memory_space=pltpu.SMEM),
                  pl.BlockSpec((B, ATT_COLS), lambda n: (n, 0)),
                  pl.BlockSpec((B, ATT_COLS), lambda n: (jnp.maximum(n - 1, 0), 0)),
                  pl.BlockSpec((1, ATT_COLS), lambda n: (0, 0))],
        out_specs=pl.BlockSpec((B, ATT_Q_W), lambda n: (n, 0)),
        out_shape=_sds((T, ATT_Q_W), F32),
        compiler_params=_cp(("parallel",)),
    )(sinks, att, att, b_attn)


def _attn_bwd(att, b_attn, sinks, dmix, *, name):
    T = att.shape[0]
    B, hd = ATT_BLOCK, ATT_HD
    NB = T // B
    scale = 1.0 / math.sqrt(hd)

    def body(sink_ref, cur_ref, prev_ref, ba_ref, do_ref, daq_ref, dakv_ref, dsink_ref, dbq_ref, dbkv_ref,
             carry_sc, cprev_sc, ccur_sc):
        n = pl.program_id(0)

        @pl.when(n == 0)
        def _():
            carry_sc[...] = jnp.zeros_like(carry_sc)
            dsink_ref[...] = jnp.zeros_like(dsink_ref)
            dbq_ref[...] = jnp.zeros_like(dbq_ref)
            dbkv_ref[...] = jnp.zeros_like(dbkv_ref)

        @pl.when(n < NB)
        def _():
            valid = _att_valid(n)
            hrow = lax.broadcasted_iota(jnp.int32, (SUBLANES, 128), 0)
            dsink = jnp.zeros((SUBLANES, 128), F32)
            for kv in range(ATT_KV):
                qs, kc, vc = _att_load(cur_ref, prev_ref, ba_ref, kv)
                prob, psink = _att_probs(qs, kc, valid, sink_ref, kv)
                dout = jnp.concatenate(
                    [do_ref[:, hd * (ATT_GROUP * kv + g):hd * (ATT_GROUP * kv + g + 1)] for g in range(ATT_GROUP)], axis=0)
                dp = _dot_nt(vc, dout)
                delta = jnp.sum(prob * dp, axis=0, keepdims=True)
                dsc = prob * (dp - delta) * scale
                dq = _dot_tn(dsc, kc)
                dk = _dot_nn(dsc, qs)
                dvv = _dot_nn(prob, dout)
                dsk = psink * delta
                for g in range(ATT_GROUP):
                    h = ATT_GROUP * kv + g
                    daq_ref[:, hd * h:hd * (h + 1)] = dq[B * g:B * (g + 1)].astype(daq_ref.dtype)
                    tot = jnp.sum(dsk[:, B * g:B * (g + 1)], axis=1, keepdims=True)
                    dsink = dsink - jnp.where(hrow == h, tot, 0.0)
                cprev_sc[:, hd * kv:hd * (kv + 1)] = dk[:B]
                ccur_sc[:, hd * kv:hd * (kv + 1)] = dk[B:]
                cprev_sc[:, ATT_KV_W + hd * kv:ATT_KV_W + hd * (kv + 1)] = dvv[:B]
                ccur_sc[:, ATT_KV_W + hd * kv:ATT_KV_W + hd * (kv + 1)] = dvv[B:]
            dsink_ref[...] += dsink
            dbq_ref[...] += jnp.sum(daq_ref[...].astype(F32), axis=0, keepdims=True)
            done = carry_sc[...] + cprev_sc[...]
            dakv_ref[...] = done.astype(dakv_ref.dtype)
            dbkv_ref[...] += jnp.sum(done.astype(dakv_ref.dtype).astype(F32), axis=0, keepdims=True)
            carry_sc[...] = ccur_sc[...]

        @pl.when(n == NB)
        def _():
            done = carry_sc[...]
            dakv_ref[...] = done.astype(dakv_ref.dtype)
            dbkv_ref[...] += jnp.sum(done.astype(dakv_ref.dtype).astype(F32), axis=0, keepdims=True)

    cl = lambda n: jnp.minimum(n, NB - 1)
    return pl.pallas_call(
        body, name=name, grid=(NB + 1,),
        in_specs=[pl.BlockSpec(memory_space=pltpu.SMEM),
                  pl.BlockSpec((B, ATT_COLS), lambda n: (cl(n), 0)),
                  pl.BlockSpec((B, ATT_COLS), lambda n: (jnp.maximum(cl(n) - 1, 0), 0)),
                  pl.BlockSpec((1, ATT_COLS), lambda n: (0, 0)),
                  pl.BlockSpec((B, ATT_Q_W), lambda n: (cl(n), 1))],
        out_specs=[pl.BlockSpec((B, ATT_Q_W), lambda n: (cl(n), 0)),
                   pl.BlockSpec((B, 2 * ATT_KV_W), lambda n: (jnp.maximum(n - 1, 0), 0)),
                   pl.BlockSpec((SUBLANES, 128), lambda n: (0, 0)),
                   pl.BlockSpec((1, ATT_Q_W), lambda n: (0, 0)),
                   pl.BlockSpec((1, 2 * ATT_KV_W), lambda n: (0, 0))],
        out_shape=[_sds((T, ATT_Q_W), BF16), _sds((T, 2 * ATT_KV_W), BF16), _sds((SUBLANES, 128), F32),
                   _sds((1, ATT_Q_W), F32), _sds((1, 2 * ATT_KV_W), F32)],
        scratch_shapes=[pltpu.VMEM((B, 2 * ATT_KV_W), F32)] * 3,
        compiler_params=_cp(("arbitrary",)),
    )(sinks, att, att, b_attn, dmix)


def _silu_and_grad(x):
    sg = _sigmoid(x)
    return x * sg, sg * (1.0 + x * (1.0 - sg))


def _mix_fwd_fn(o_raw, hg, o_att, hgw):
    outs = []
    for h in range(HG_HEADS):
        sl = slice(HG_DK * h, HG_DK * (h + 1))
        silu, _ = _silu_and_grad(hg[:, sl])
        outs.append(_rms_fwd(o_raw[:, sl], hgw) * silu)
    outs.append(o_att)
    return (jnp.concatenate(outs, axis=1),)


def _mix_bwd_fn(o_raw, hg, dmix, hgw):
    dos, dhgs = [], []
    dw = jnp.zeros((1, HG_DK), F32)
    for h in range(HG_HEADS):
        sl = slice(HG_DK * h, HG_DK * (h + 1))
        silu, dsilu = _silu_and_grad(hg[:, sl])
        dy = dmix[:, sl]
        dhgs.append(dy * _rms_fwd(o_raw[:, sl], hgw) * dsilu)
        dx, dwh = _rms_bwd(o_raw[:, sl], hgw, dy * silu)
        dos.append(dx)
        dw = dw + dwh
    return jnp.concatenate(dos, axis=1), jnp.concatenate(dhgs, axis=1), dw


def _final_fn(h2, tgt, wf):
    d = h2.shape[1]
    err = _rms_fwd(h2, wf) - tgt
    loss_cols = (0.5 / d) * jnp.sum(err * err, axis=0, keepdims=True)
    dh2, dwf = _rms_bwd(h2, wf, err * (1.0 / d))
    return dh2, dh2, loss_cols, dwf


class _NoExchange:
    def __init__(self, weights):
        self.weights = weights

    def start(self):
        return None

    def w_in(self, after):
        return self.weights["w_in_t"]

    def w_out(self, after):
        return self.weights

    def ffn(self, after):
        return self.weights

    def ffn_grads(self, gs):
        return None


def _local_step(x, tgt, p, ex):
    T, D = x.shape
    row = lambda n, dt: _sds((T, n), dt)
    acc = lambda n: _sds((1, n), F32)

    (u,) = _rowwise(lambda xv, w: (_rms_fwd(xv, w),), [_full(x)], [p["norm_mix_w"]], [row(D, BF16)], [], name="rms_mix",
                    after=ex.start())
    p = dict(p, w_in_t=ex.w_in(u))
    hq, hf, hi, hg, att = _mm_nt(u, p["w_in_t"], splits=[HG_W] * 4 + [ATT_COLS], out_dtype=F32, name="in_proj")
    o_raw, states = _hgrn_fwd(hq, hf, hi, p["lb"], name="hgrn_fwd")
    o_att = _attn_fwd(att, p["b_attn"], p["sinks"], name="attn_fwd")
    (mix,) = _rowwise(_mix_fwd_fn, [_full(o_raw), _full(hg), _full(o_att)], [p["hg_norm_w"]], [row(D, BF16)], [],
                      name="mix_fwd")
    p = dict(p, **ex.w_out(mix))
    def out_epilogue(prod, xv, w):
        h1v = prod + xv
        return h1v, _rms_fwd(h1v, w)

    h1, v = _mm_nn([[mix]], [p["w_out"]], name="out_proj",
                   epilogue=(out_epilogue, [x], [p["norm_ffn_w"]], [row(D, F32), row(D, BF16)], []))
    p = dict(p, **ex.ffn(v))
    (gp,) = _mm_nt(v, p["w_gate_t"], splits=[D_FF], out_dtype=F32, name="gate_proj")
    (up,) = _mm_nt(v, p["w_up_t"], splits=[D_FF], out_dtype=F32, name="up_proj")
    act = _convact_fwd(gp, up, p["conv_w8"], p["conv_b"], name="convact_fwd")
    def down_epilogue(prod, h1v, tgtv, wf):
        return _final_fn(prod + h1v, tgtv, wf)

    dh2, dh2_b, loss_cols, d_final = _mm_nn(
        [[act]], [p["w_down"]], name="down_proj_loss",
        epilogue=(down_epilogue, [h1, tgt], [p["final_norm_w"]], [row(D, F32), row(D, BF16)], [acc(D), acc(D)]))

    (dact,) = _mm_nt(dh2_b, p["w_down"], splits=[D_FF], out_dtype=F32, name="d_act")
    g_down = _mm_tn([act], dh2_b, name="g_down")
    dgp, dup, d_conv_w8, d_conv_b = _convact_bwd(gp, up, dact, p["conv_w8"], p["conv_b"], name="convact_bwd")
    def ffn_norm_bwd(dvv, hv, dh2v, w):
        dx, dw = _rms_bwd(hv, w, dvv)
        dh1v = dx + dh2v
        return dh1v, dh1v, dw

    dh1, dh1_b, d_norm_ffn = _mm_nn(
        [[dgp], [dup]], [p["w_gate_t"], p["w_up_t"]], name="d_v_norm",
        epilogue=(ffn_norm_bwd, [h1, dh2], [p["norm_ffn_w"]], [row(D, F32), row(D, BF16)], [acc(D)]))
    g_gate_t = _mm_tn([dgp], v, name="g_gate")
    g_up_t = _mm_tn([dup], v, name="g_up")
    sent = ex.ffn_grads([g_gate_t, g_up_t, g_down])
    (dmix,) = _mm_nt(dh1_b, p["w_out"], splits=[D], out_dtype=F32, name="d_mix", after=sent)
    g_out = _mm_tn([mix], dh1_b, name="g_out")
    do_raw, dhg, d_hg_norm = _rowwise(_mix_bwd_fn, [_full(o_raw), _full(hg), (dmix, HG_W, 0, 0)], [p["hg_norm_w"]],
                                      [row(HG_W, F32), row(HG_W, BF16)], [acc(HG_DK)], name="mix_bwd")
    daq, dakv, d_sinks8, d_bq, d_bkv = _attn_bwd(att, p["b_attn"], p["sinks"], dmix, name="attn_bwd")
    dhq, dhf, dhi, d_lb = _hgrn_bwd(hq, hf, hi, p["lb"], states, do_raw, name="hgrn_bwd")
    pieces = [dhq, dhf, dhi, dhg, daq, dakv]
    g_in_t = _mm_tn(pieces, u, name="g_in")

    def mix_norm_bwd(duv, xv, dh1v, w):
        dx, dw = _rms_bwd(xv, w, duv)
        return dx + dh1v, dw

    dx, d_norm_mix = _mm_nn([pieces], [p["w_in_t"]], name="d_u_norm",
                            epilogue=(mix_norm_bwd, [x, dh1], [p["norm_mix_w"]], [row(D, F32)], [acc(D)]))
    grads = dict(g_in_t=g_in_t, g_out=g_out, g_gate_t=g_gate_t, g_up_t=g_up_t, g_down=g_down,
                 norm_mix_w=d_norm_mix, b_attn=jnp.concatenate([d_bq, d_bkv], axis=1), lb=d_lb, hg_norm_w=d_hg_norm,
                 sinks8=d_sinks8, norm_ffn_w=d_norm_ffn, conv_w8=d_conv_w8, conv_b=d_conv_b, final_norm_w=d_final)
    return loss_cols, dx, grads


SLAB = (IN_COLS // N_CHIPS, D_FF // N_CHIPS, D_FF // N_CHIPS, D_FF // N_CHIPS, D_MODEL // N_CHIPS)
N_W = len(SLAB)
PACK_OFF = tuple(sum(SLAB[:i]) for i in range(N_W))
PACK_ROWS = sum(SLAB)
FULL_OFF = tuple(N_CHIPS * o for o in PACK_OFF)
FULL_ROWS = N_CHIPS * PACK_ROWS
HALF = tuple(s // 2 for s in SLAB)
HPACK_OFF = tuple(sum(HALF[:i]) for i in range(N_W))
HPACK_ROWS = sum(HALF)
HFULL_OFF = tuple(N_CHIPS * o for o in HPACK_OFF)
HFULL_ROWS = N_CHIPS * HPACK_ROWS
CHIP_FLIPS = ((1, 0), (0, 1), (1, 1))
N_DEV = 8
BF16_ROWS = 16
ANY = pl.BlockSpec(memory_space=pl.ANY)


def _pos():
    return lax.axis_index("x"), lax.axis_index("y"), lax.axis_index("c")


def _flip(v, f):
    return 1 - v if f else v


def _rcopy(src, dst, ssem, rsem, dev):
    return pltpu.make_async_remote_copy(src_ref=src, dst_ref=dst, send_sem=ssem, recv_sem=rsem, device_id=dev,
                                        device_id_type=pl.DeviceIdType.MESH)


def _rows(ref, start, n, align=None):
    if not isinstance(start, int):
        if align is None:
            align = SUBLANES * (4 // jnp.dtype(ref.dtype).itemsize)
        start = pl.multiple_of(start, align)
    return ref.at[pl.ds(start, n), :]


FFN_W = (1, 2, 3)
N_PEER = 1 + len(CHIP_FLIPS)
HBM = pl.BlockSpec(memory_space=pltpu.HBM)
SEM = pl.BlockSpec(memory_space=pltpu.SEMAPHORE)
EFFECT = pltpu.SideEffectType.DATAFLOW_SIDE_EFFECTING
LANES = 128


def _gather_start(pack, cw8):
    D = pack.shape[1]
    lands = [lax.empty((N_CHIPS * SLAB[0], D), pack.dtype), lax.empty((3 * N_CHIPS * SLAB[1], D), pack.dtype),
             lax.empty((N_CHIPS * SLAB[4], D), pack.dtype), lax.empty((N_CHIPS,) + cw8.shape, cw8.dtype)]
    bufs = [pack, cw8] + lands

    def body(pack_ref, cw_ref, l_in, l_ffn, l_out, l_cw, *rest):
        in_send, in_recv, out_send, out_recv, ffn_send, ffn_recv = rest[:6]
        token = rest[-1]
        x, y, c = _pos()
        q = 2 * x + y
        peers = _gather_peers(x, y, c)
        for k, peer in enumerate(peers):
            _rcopy(_rows(pack_ref, PACK_OFF[0], SLAB[0]), _rows(l_in, q * SLAB[0], SLAB[0], BF16_ROWS),
                   in_send.at[k], in_recv.at[k], peer).start()
        for k, peer in enumerate(peers):
            _rcopy(_rows(pack_ref, PACK_OFF[4], SLAB[4]), _rows(l_out, q * SLAB[4], SLAB[4], BF16_ROWS),
                   out_send.at[k], out_recv.at[k], peer).start()
            _rcopy(cw_ref, l_cw.at[q], out_send.at[N_PEER + k], out_recv.at[N_PEER + k], peer).start()
        for j, w in enumerate(FFN_W):
            for k, peer in enumerate(peers):
                _rcopy(_rows(pack_ref, PACK_OFF[w], SLAB[w]), _rows(l_ffn, (j * N_CHIPS + q) * SLAB[w], SLAB[w], BF16_ROWS),
                       ffn_send.at[k], ffn_recv.at[k], peer).start()
        token[...] = jnp.zeros_like(token)

    n_sem = (N_PEER, N_PEER, 2 * N_PEER, 2 * N_PEER, N_PEER, N_PEER)
    outs = pl.pallas_call(
        body, name="gather_start", in_specs=[HBM] * len(bufs),
        out_specs=[SEM] * len(n_sem) + [HBM] * len(bufs) + [pl.BlockSpec(memory_space=pltpu.VMEM)],
        out_shape=[pltpu.SemaphoreType.DMA((n,)) for n in n_sem]
        + [pltpu.HBM(b.shape, b.dtype) for b in bufs] + [_sds((SUBLANES, LANES), F32)],
        input_output_aliases={i: len(n_sem) + i for i in range(len(bufs))},
        compiler_params=pltpu.CompilerParams(has_side_effects=EFFECT),
    )(*[pltpu.with_memory_space_constraint(b, pltpu.HBM) for b in bufs])
    bufs_out = outs[len(n_sem):]
    return dict(in_sems=outs[0:2], out_sems=outs[2:4], ffn_sems=outs[4:6], pack=bufs_out[0], cw=bufs_out[1], l_in=bufs_out[2],
                l_ffn=bufs_out[3], l_out=bufs_out[4], l_cw=bufs_out[5], token=bufs_out[6])


def _gather_peers(x, y, c):
    return [(x, y, 1 - c)] + [(_flip(x, fx), _flip(y, fy), c) for fx, fy in CHIP_FLIPS]


def _gather_wait_in(g, after):
    def body(pack_ref, l_in, send, recv, after_ref, pack_out, l_out):
        for k, peer in enumerate(_gather_peers(*_pos())):
            cp = _rcopy(_rows(pack_ref, PACK_OFF[0], SLAB[0]), _rows(l_in, 0, SLAB[0]), send.at[k], recv.at[k], peer)
            cp.wait_send()
            cp.wait_recv()

    return pl.pallas_call(
        body, name="gather_wait_in", in_specs=[HBM, HBM, SEM, SEM, ANY], out_specs=[HBM, HBM],
        out_shape=[pltpu.HBM(g["pack"].shape, g["pack"].dtype), pltpu.HBM(g["l_in"].shape, g["l_in"].dtype)],
        input_output_aliases={0: 0, 1: 1}, compiler_params=pltpu.CompilerParams(has_side_effects=EFFECT),
    )(g["pack"], g["l_in"], *g["in_sems"], after)


def _gather_wait_out(g, pack, after):
    def body(pack_ref, cw_ref, l_out, l_cw, send, recv, after_ref, o_pack, o_out, o_cw):
        for k, peer in enumerate(_gather_peers(*_pos())):
            for cp in (_rcopy(_rows(pack_ref, PACK_OFF[4], SLAB[4]), _rows(l_out, 0, SLAB[4]), send.at[k], recv.at[k], peer),
                       _rcopy(cw_ref, l_cw.at[0], send.at[N_PEER + k], recv.at[N_PEER + k], peer)):
                cp.wait_send()
                cp.wait_recv()

    ins = [pack, g["cw"], g["l_out"], g["l_cw"]]
    return pl.pallas_call(
        body, name="gather_wait_out", in_specs=[HBM] * 4 + [SEM, SEM, ANY], out_specs=[HBM] * 3,
        out_shape=[pltpu.HBM(b.shape, b.dtype) for b in (ins[0], ins[2], ins[3])],
        input_output_aliases={0: 0, 2: 1, 3: 2}, compiler_params=pltpu.CompilerParams(has_side_effects=EFFECT),
    )(*ins, *g["out_sems"], after)


def _gather_wait_ffn(g, pack, after):
    n_ffn = len(FFN_W) * SLAB[FFN_W[0]]

    def body(pack_ref, l_ffn, send, recv, after_ref, o_ffn):
        for k, peer in enumerate(_gather_peers(*_pos())):
            cp = _rcopy(_rows(pack_ref, PACK_OFF[FFN_W[0]], n_ffn), _rows(l_ffn, 0, n_ffn), send.at[k], recv.at[k], peer)
            cp.wait_send()
            cp.wait_recv()

    return pl.pallas_call(
        body, name="gather_wait_ffn", in_specs=[HBM] * 2 + [SEM, SEM, ANY], out_specs=HBM,
        out_shape=pltpu.HBM(g["l_ffn"].shape, g["l_ffn"].dtype),
        input_output_aliases={1: 0}, compiler_params=pltpu.CompilerParams(has_side_effects=EFFECT),
    )(pack, g["l_ffn"], *g["ffn_sems"], after)


def _exchange_halves(ws, gs, small, *, name):
    D = gs[0].shape[1]
    n = len(ws)
    has_small = small is not None

    def body(*refs):
        g = refs[:n]
        t = refs[n + has_small:2 * n + has_small]
        sems = refs[2 * n + 2 * has_small:]
        d2d_send, d2d_recv = sems[0], sems[1]
        x, y, c = _pos()
        sib = (x, y, 1 - c)
        drains = []
        for i, w in enumerate(ws):
            h = HALF[w]
            for qq in range(N_CHIPS):
                _rcopy(_rows(g[i], qq * SLAB[w] + (1 - c) * h, h), _rows(t[i], qq * h, h),
                       d2d_send.at[i], d2d_recv.at[i], sib).start()
            drains.append(_rcopy(t[i], t[i], d2d_send.at[i], d2d_recv.at[i], sib))
        if has_small:
            small_ref, sall_ref = refs[n], refs[2 * n + 1]
            sm_send, sm_recv, loc_sem = sems[2], sems[3], sems[4]
            me = 4 * x + 2 * y + c
            own_small = pltpu.make_async_copy(small_ref, sall_ref.at[me], loc_sem)
            own_small.start()
            for f in range(1, N_DEV):
                peer = (_flip(x, f & 4), _flip(y, f & 2), _flip(c, f & 1))
                cp = _rcopy(small_ref, sall_ref.at[me], sm_send.at[f - 1], sm_recv.at[f - 1], peer)
                cp.start()
                drains.append(cp)
        for d in drains:
            d.wait_recv()
        for d in drains:
            d.wait_send()
        if has_small:
            own_small.wait()

    out_shape = [_sds((N_CHIPS * HALF[w], D), F32) for w in ws]
    scratch = [pltpu.SemaphoreType.DMA((n,)), pltpu.SemaphoreType.DMA((n,))]
    if has_small:
        out_shape.append(_sds((N_DEV,) + small.shape, F32))
        scratch += [pltpu.SemaphoreType.DMA((N_DEV - 1,)), pltpu.SemaphoreType.DMA((N_DEV - 1,)), pltpu.SemaphoreType.DMA]
    return pl.pallas_call(
        body, name=name, in_specs=[ANY] * (n + has_small), out_specs=[ANY] * (n + has_small),
        out_shape=out_shape, scratch_shapes=scratch,
    )(*gs, *([small] if has_small else []))


REDUCE_SPLIT = 2


def _chip_partial(ws, gs, theirs, *, name, out_dtype=F32):
    D = gs[0].shape[1]
    n = len(ws)

    def body(*refs):
        for i in range(n):
            refs[2 * n + i][...] = (refs[i][...] + refs[n + i][...]).astype(out_dtype)

    blk = [HALF[w] // REDUCE_SPLIT for w in ws]
    mine = [pl.BlockSpec((b, D), lambda qq, j: ((2 * qq + lax.axis_index("c")) * REDUCE_SPLIT + j, 0)) for b in blk]
    flat = [pl.BlockSpec((b, D), lambda qq, j: (qq * REDUCE_SPLIT + j, 0)) for b in blk]
    return pl.pallas_call(
        body, name=name, grid=(N_CHIPS, REDUCE_SPLIT), in_specs=mine + flat, out_specs=flat,
        out_shape=[_sds((N_CHIPS * HALF[w], D), out_dtype) for w in ws],
        compiler_params=_cp(("parallel", "parallel")),
    )(*gs, *theirs)


def _partial_copies(ws, part, got, send_sems, recv_sems):
    x, y, c = _pos()
    cps = []
    for k, (fx, fy) in enumerate(CHIP_FLIPS):
        peer = (_flip(x, fx), _flip(y, fy), c)
        qp = 2 * _flip(x, fx) + _flip(y, fy)
        for i, w in enumerate(ws):
            cps.append(_rcopy(_rows(part[i], qp * HALF[w], HALF[w]), _rows(got[i], k * HALF[w], HALF[w]),
                              send_sems.at[len(ws) * k + i], recv_sems.at[len(ws) * k + i], peer))
    return cps


def _send_chip_partials(ws, parts, *, name):
    D = parts[0].shape[1]
    n = len(ws)

    def body(*refs):
        cps = _partial_copies(ws, refs[:n], refs[n:2 * n], refs[2 * n], refs[2 * n + 1])
        for cp in cps:
            cp.start()
        for cp in cps:
            cp.wait_recv()
        for cp in cps:
            cp.wait_send()

    return pl.pallas_call(
        body, name=name, in_specs=[ANY] * n, out_specs=[ANY] * n,
        out_shape=[_sds((len(CHIP_FLIPS) * HALF[w], D), parts[0].dtype) for w in ws],
        scratch_shapes=[pltpu.SemaphoreType.DMA((len(CHIP_FLIPS) * n,)), pltpu.SemaphoreType.DMA((len(CHIP_FLIPS) * n,))],
    )(*parts)


def _send_start(ws, parts, *, name):
    D = parts[0].shape[1]
    n = len(ws)
    bufs = list(parts) + [lax.empty((len(CHIP_FLIPS) * HALF[w], D), parts[0].dtype) for w in ws]

    def body(*refs):
        send_sems, recv_sems = refs[2 * n], refs[2 * n + 1]
        for cp in _partial_copies(ws, refs[:n], refs[n:2 * n], send_sems, recv_sems):
            cp.start()
        refs[-1][...] = jnp.zeros_like(refs[-1])

    outs = pl.pallas_call(
        body, name=name, in_specs=[HBM] * (2 * n),
        out_specs=[SEM, SEM] + [HBM] * (2 * n) + [pl.BlockSpec(memory_space=pltpu.VMEM)],
        out_shape=[pltpu.SemaphoreType.DMA((len(CHIP_FLIPS) * n,)), pltpu.SemaphoreType.DMA((len(CHIP_FLIPS) * n,))]
        + [pltpu.HBM(b.shape, b.dtype) for b in bufs] + [_sds((SUBLANES, LANES), F32)],
        input_output_aliases={i: 2 + i for i in range(2 * n)},
        compiler_params=pltpu.CompilerParams(has_side_effects=EFFECT),
    )(*[pltpu.with_memory_space_constraint(b, pltpu.HBM) for b in bufs])
    return dict(sems=outs[0:2], parts=outs[2:2 + n], got=outs[2 + n:2 + 2 * n], token=outs[-1])


def _send_wait(ws, s, after, *, name):
    n = len(ws)

    def body(*refs):
        for cp in _partial_copies(ws, refs[:n], refs[n:2 * n], refs[2 * n], refs[2 * n + 1]):
            cp.wait_send()
            cp.wait_recv()

    bufs = list(s["parts"]) + list(s["got"])
    outs = pl.pallas_call(
        body, name=name, in_specs=[HBM] * (2 * n) + [SEM, SEM, ANY], out_specs=[HBM] * (2 * n),
        out_shape=[pltpu.HBM(b.shape, b.dtype) for b in bufs],
        input_output_aliases={i: i for i in range(2 * n)},
        compiler_params=pltpu.CompilerParams(has_side_effects=EFFECT),
    )(*bufs, *s["sems"], after)
    return outs[:n], outs[n:]


def _chip_reduce(ws, parts, got, *, name, after=None):
    D = parts[0].shape[1]
    nk = len(CHIP_FLIPS)
    n = len(ws)
    extra = [] if after is None else [after]

    def body(*refs):
        refs = refs[len(extra):]
        outs = refs[(1 + nk) * n:]
        for i in range(n):
            acc = refs[i][...].astype(F32)
            for k in range(nk):
                acc = acc + refs[n * (1 + k) + i][...].astype(F32)
            outs[i][...] = acc

    blk = [HALF[w] // REDUCE_SPLIT for w in ws]

    def q_idx(j):
        return (2 * lax.axis_index("x") + lax.axis_index("y")) * REDUCE_SPLIT + j

    in_specs = [pl.BlockSpec((b, D), lambda j: (q_idx(j), 0)) for b in blk]
    for k in range(nk):
        in_specs += [pl.BlockSpec((b, D), functools.partial(lambda j, k: (k * REDUCE_SPLIT + j, 0), k=k)) for b in blk]
    out_specs = [pl.BlockSpec((b, D), lambda j: (lax.axis_index("c") * REDUCE_SPLIT + j, 0)) for b in blk]
    return pl.pallas_call(
        body, name=name, grid=(REDUCE_SPLIT,), in_specs=[ANY] * len(extra) + in_specs, out_specs=out_specs,
        out_shape=[_sds((SLAB[w], D), F32) for w in ws],
        compiler_params=_cp(("parallel",)),
    )(*extra, *parts, *[g for _ in range(nk) for g in got])


def _exchange_reduced(ws, shards, *, name):
    n = len(ws)

    def body(*refs):
        ins, outs = refs[:n], refs[n:2 * n]
        send_sems, recv_sems = refs[2 * n], refs[2 * n + 1]
        x, y, c = _pos()
        sib = (x, y, 1 - c)
        cps = []
        for i, w in enumerate(ws):
            cp = _rcopy(_rows(ins[i], c * HALF[w], HALF[w]), _rows(outs[i], c * HALF[w], HALF[w]),
                        send_sems.at[i], recv_sems.at[i], sib)
            cp.start()
            cps.append(cp)
        for cp in cps:
            cp.wait_recv()
        for cp in cps:
            cp.wait_send()

    return pl.pallas_call(
        body, name=name, in_specs=[ANY] * n, out_specs=[ANY] * n,
        out_shape=[_sds(s.shape, s.dtype) for s in shards], input_output_aliases={i: i for i in range(n)},
        scratch_shapes=[pltpu.SemaphoreType.DMA((n,)), pltpu.SemaphoreType.DMA((n,))],
    )(*shards)


def _adamw_fn(w, g, m, v):
    m2 = ADAM_B1 * m + (1.0 - ADAM_B1) * g
    v2 = ADAM_B2 * v + (1.0 - ADAM_B2) * (g * g)
    m_hat = m2 / (1.0 - ADAM_B1 ** ADAM_STEP)
    v_hat = v2 / (1.0 - ADAM_B2 ** ADAM_STEP)
    return -ADAM_LR * (m_hat / (jnp.sqrt(v_hat) + ADAM_EPS) + ADAM_WD * w), m2, v2


def _adamw(w, g, m, v, *, name):
    shp = _sds(w.shape, F32)
    rows = w.shape[0]
    tm = max(t for t in range(SUBLANES, 512 + 1, SUBLANES) if rows % t == 0)
    return _rowwise(_adamw_fn, [_full(w), _full(g), _full(m), _full(v)], [], [shp] * 3, [], name=name, tm=tm)


SMALL_SEGS = (("loss", 8), ("norm_mix_w", 8), ("b_attn", 8), ("lb_logits", 8), ("hg_norm_w", 8), ("sinks", 8),
              ("norm_ffn_w", 8), ("conv_w", 72), ("conv_b", 24), ("final_norm_w", 8))
SMALL_OFF = {n: sum(r for _, r in SMALL_SEGS[:i]) for i, (n, _) in enumerate(SMALL_SEGS)}
SMALL_ROWS = sum(r for _, r in SMALL_SEGS)
LANES = 128


def _pack_small(parts):
    segs = []
    for n, r in SMALL_SEGS:
        a = parts.get(n)
        flat = jnp.zeros((0,), F32) if a is None else a.reshape(-1).astype(F32)
        segs.append(jnp.pad(flat, (0, r * LANES - flat.shape[0])).reshape(r, LANES))
    return jnp.concatenate(segs, axis=0)


def _unpack_small(pack, n, shape):
    size = math.prod(shape)
    r0 = SMALL_OFF[n]
    return pack[r0:r0 + dict(SMALL_SEGS)[n]].reshape(-1)[:size].reshape(shape)


def _small_update(sall, wp, mp, vp):
    R = SMALL_ROWS
    r_lb = SMALL_OFF["lb_logits"]

    def body(s_ref, w_ref, m_ref, v_ref, g_ref, d_ref, m2_ref, v2_ref, loss_ref):
        g = s_ref[0]
        for i in range(1, N_DEV):
            g = g + s_ref[i]
        tot = jnp.sum(jnp.sum(g[0:8], axis=1, keepdims=True), axis=0, keepdims=True)
        loss_ref[...] = jnp.broadcast_to(tot, loss_ref.shape)
        lg = w_ref[r_lb:r_lb + 8, :]
        p0 = _sigmoid(lg - pltpu.roll(lg, 4, 0))
        d = g[r_lb:r_lb + 8]
        d = d + pltpu.roll(d, 4, 0)
        sign = jnp.where(lax.broadcasted_iota(jnp.int32, d.shape, 0) < 4, 1.0, -1.0)
        g = jnp.concatenate([g[:r_lb], sign * d * p0 * (1.0 - p0), g[r_lb + 8:]], axis=0)
        g_ref[...] = g
        d_ref[...], m2_ref[...], v2_ref[...] = _adamw_fn(w_ref[...], g, m_ref[...], v_ref[...])

    full = pl.BlockSpec((R, LANES), lambda: (0, 0))
    return pl.pallas_call(
        body, name="small_update",
        in_specs=[pl.BlockSpec((N_DEV, R, LANES), lambda: (0, 0, 0)), full, full, full],
        out_specs=[full, full, full, full, pl.BlockSpec((8, LANES), lambda: (0, 0))],
        out_shape=[_sds((R, LANES), F32)] * 4 + [_sds((8, LANES), F32)],
        compiler_params=_cp(),
    )(sall, wp, mp, vp)


def _lb_fwd(lb_logits):
    n = lb_logits.shape[1]

    def body(l_ref, o_ref):
        o_ref[...] = _sigmoid(l_ref[0:1, :] - l_ref[1:2, :])

    return pl.pallas_call(body, name="lb_fwd", out_shape=_sds((1, n), F32), compiler_params=_cp())(lb_logits)


class _MeshExchange:
    def __init__(self, pack, cw8):
        self.gather = _gather_start(pack, cw8)
        self.sent = None
        self.conv_w8 = None

    def start(self):
        return self.gather["token"]

    def w_in(self, after):
        self.pack, l_in = _gather_wait_in(self.gather, after)
        return (l_in, N_CHIPS * SLAB[0], 0)

    def w_out(self, after):
        self.pack, l_out, l_cw = _gather_wait_out(self.gather, self.pack, after)
        self.conv_w8 = jnp.concatenate([l_cw[i] for i in range(N_CHIPS)], axis=1)
        return dict(w_out=(l_out, N_CHIPS * SLAB[4], 0), conv_w8=self.conv_w8)

    def ffn(self, after):
        l_ffn = _gather_wait_ffn(self.gather, self.pack, after)
        rows = N_CHIPS * SLAB[FFN_W[0]]
        return dict(w_gate_t=(l_ffn, rows, 0), w_up_t=(l_ffn, rows, 1), w_down=(l_ffn, rows, 2))

    def ffn_grads(self, gs):
        theirs = _exchange_halves(FFN_W, gs, None, name="exchange_halves_ffn")
        parts = _chip_partial(FFN_W, gs, theirs, name="chip_partial_ffn")
        self.sent = _send_start(FFN_W, parts, name="send_ffn_start")
        return self.sent["token"]


def kernel(x, norm_mix_w, w_in, b_attn, lb_logits, hg_norm_w, sinks, w_out, norm_ffn_w, w_gate, w_up, conv_w, conv_b, w_down, final_norm_w, loss_target, m_norm_mix_w, m_w_in, m_b_attn, m_lb_logits, m_hg_norm_w, m_sinks, m_w_out, m_norm_ffn_w, m_w_gate, m_w_up, m_conv_w, m_conv_b, m_w_down, m_final_norm_w, v_norm_mix_w, v_w_in, v_b_attn, v_lb_logits, v_hg_norm_w, v_sinks, v_w_out, v_norm_ffn_w, v_w_gate, v_w_up, v_conv_w, v_conv_b, v_w_down, v_final_norm_w):
    D = D_MODEL
    q = 2 * lax.axis_index("x") + lax.axis_index("y")
    ccols = D_FF // N_CHIPS

    pack = jnp.concatenate([w_in[0].T, w_gate[0].T, w_up[0].T, w_down[0], w_out[0]], axis=0).astype(BF16)
    cw8 = jnp.concatenate([conv_w[0], jnp.zeros((SUBLANES - 3, ccols), F32)], axis=0)
    ex = _MeshExchange(pack, cw8)
    p = dict(norm_mix_w=norm_mix_w, b_attn=b_attn, lb=_lb_fwd(lb_logits), hg_norm_w=hg_norm_w, sinks=sinks,
             norm_ffn_w=norm_ffn_w, conv_b=conv_b, final_norm_w=final_norm_w.reshape(1, D))
    loss_cols, dx, g = _local_step(x[0], loss_target[0], p, ex)
    conv_w8 = ex.conv_w8

    small = _pack_small(dict(loss=loss_cols, norm_mix_w=g["norm_mix_w"], b_attn=g["b_attn"], lb_logits=g["lb"],
                             hg_norm_w=g["hg_norm_w"], sinks=g["sinks8"], norm_ffn_w=g["norm_ffn_w"],
                             conv_w=g["conv_w8"][:3], conv_b=g["conv_b"], final_norm_w=g["final_norm_w"]))
    parts_ffn, got_ffn = _send_wait(FFN_W, ex.sent, dx, name="send_ffn_wait")
    late = (0, 4)
    gs = [g["g_in_t"], g["g_out"]]
    *theirs, sall = _exchange_halves(late, gs, small, name="exchange_halves_late")
    parts_late = _chip_partial(late, gs, theirs, name="chip_partial_late", out_dtype=BF16)
    sent_late = _send_start(late, parts_late, name="send_late_start")
    big = {}

    def finish(ws, parts, got, specs, tag, after):
        shards = _exchange_reduced(ws, _chip_reduce(ws, parts, got, name="chip_reduce_" + tag, after=after),
                                   name="exchange_reduced_" + tag)
        for gw, (n, w, m, v, tr) in zip(shards, specs):
            view = (lambda a: a[0].T) if tr else (lambda a: a[0])
            back = (lambda a: a.T[None]) if tr else (lambda a: a[None])
            d_, m_, v_ = _adamw(view(w), gw, view(m), view(v), name="adamw_" + n)
            big[n] = (back(gw), back(d_), back(m_), back(v_))
        return d_

    last = finish(FFN_W, parts_ffn, got_ffn, (("w_gate", w_gate, m_w_gate, v_w_gate, True), ("w_up", w_up, m_w_up, v_w_up, True),
                                              ("w_down", w_down, m_w_down, v_w_down, False)), "ffn", sent_late["token"])
    parts_late, got_late = _send_wait(late, sent_late, last, name="send_late_wait")
    finish(late, parts_late, got_late, (("w_in", w_in, m_w_in, v_w_in, True), ("w_out", w_out, m_w_out, v_w_out, False)),
           "late", None)

    def place(a):
        return lax.dynamic_update_slice(jnp.zeros((3, D_FF), F32), a[0], (0, q * ccols))

    def small_pack(ws, cw):
        nm, ba, lbl, hg, sk, nf, cb, fn = ws
        return _pack_small(dict(norm_mix_w=nm, b_attn=ba, lb_logits=lbl, hg_norm_w=hg,
                                sinks=jnp.broadcast_to(sk.reshape(ATT_HEADS, 1), (ATT_HEADS, LANES)), norm_ffn_w=nf,
                                conv_w=cw, conv_b=cb, final_norm_w=fn))

    wp = small_pack((norm_mix_w, b_attn, lb_logits, hg_norm_w, sinks, norm_ffn_w, conv_b, final_norm_w), conv_w8[:3])
    mp = small_pack((m_norm_mix_w, m_b_attn, m_lb_logits, m_hg_norm_w, m_sinks, m_norm_ffn_w, m_conv_b, m_final_norm_w),
                    place(m_conv_w))
    vp = small_pack((v_norm_mix_w, v_b_attn, v_lb_logits, v_hg_norm_w, v_sinks, v_norm_ffn_w, v_conv_b, v_final_norm_w),
                    place(v_conv_w))
    outs = _small_update(sall, wp, mp, vp)
    loss = outs[4][0, 0]

    def small_out(pk, n, ref):
        if n == "sinks":
            return pk[SMALL_OFF[n]:SMALL_OFF[n] + ATT_HEADS, 0].reshape(ref.shape)
        if n == "conv_w":
            full = _unpack_small(pk, n, (3, D_FF))
            return lax.dynamic_slice(full, (0, q * ccols), (3, ccols))[None]
        return _unpack_small(pk, n, ref.shape)

    refs = dict(norm_mix_w=norm_mix_w, b_attn=b_attn, lb_logits=lb_logits, hg_norm_w=hg_norm_w, sinks=sinks,
                norm_ffn_w=norm_ffn_w, conv_w=conv_w, conv_b=conv_b, final_norm_w=final_norm_w)
    order = ("norm_mix_w", "w_in", "b_attn", "lb_logits", "hg_norm_w", "sinks", "w_out", "norm_ffn_w", "w_gate", "w_up",
             "conv_w", "conv_b", "w_down", "final_norm_w")
    res = [loss, dx[None]]
    for k in range(4):
        for n in order:
            res.append(big[n][k] if n in big else small_out(outs[k], n, refs[n]))
    return tuple(res)
```

```python
import functools
import math

import jax
import jax.numpy as jnp
from jax import lax
from jax.experimental import pallas as pl
from jax.experimental.pallas import tpu as pltpu

F32 = jnp.float32
BF16 = jnp.bfloat16

D_MODEL = 1024
HG_HEADS = 4
HG_DK = 128
HG_W = HG_HEADS * HG_DK
HG_CHUNK = 64
HG_SUB = 8
ATT_HEADS = 8
ATT_KV = 2
ATT_GROUP = ATT_HEADS // ATT_KV
ATT_HD = 64
ATT_BLOCK = 128
ATT_Q_W = ATT_HEADS * ATT_HD
ATT_KV_W = ATT_KV * ATT_HD
ATT_COLS = ATT_Q_W + 2 * ATT_KV_W
IN_COLS = 4 * HG_W + ATT_COLS
D_FF = 2816
EPS = 1e-6
ADAM_LR, ADAM_B1, ADAM_B2, ADAM_EPS, ADAM_WD, ADAM_STEP = 0.001, 0.9, 0.999, 1e-08, 0.01, 10
NEG = -1e30

V7X_VMEM_BYTES = 64 * 1024 * 1024
VMEM_LIMIT = 48 * 1024 * 1024
SUBLANES = 8

N_CHIPS = 4


def _cp(sem=None, **kw):
    return pltpu.CompilerParams(dimension_semantics=sem, vmem_limit_bytes=VMEM_LIMIT, **kw)


def _sds(shape, dtype):
    return jax.ShapeDtypeStruct(shape, dtype)


def _wspec(w):
    arr, rows, blk = w
    return pl.BlockSpec((rows, arr.shape[1]), lambda i: (blk, 0))


def _mm_nt(a, w, *, splits, out_dtype, name, after=None, tm=512):
    M, K = a.shape
    N = w[1]
    tm = min(tm, M)
    assert sum(splits) == N and M % tm == 0
    offs = [sum(splits[:i]) for i in range(len(splits))]
    n_in = 2 if after is None else 3

    def body(*refs):
        a_ref, w_ref = refs[0], refs[1]
        acc = lax.dot_general(a_ref[...], w_ref[...], (((1,), (1,)), ((), ())), preferred_element_type=F32)
        for o_ref, c0, n in zip(refs[n_in:], offs, splits):
            o_ref[...] = acc[:, c0:c0 + n].astype(out_dtype)

    in_specs = [pl.BlockSpec((tm, K), lambda i: (i, 0)), _wspec(w)]
    args = [a, w[0]]
    if after is not None:
        in_specs.append(pl.BlockSpec(memory_space=pl.ANY))
        args.append(after)
    outs = pl.pallas_call(
        body, name=name, grid=(M // tm,), in_specs=in_specs,
        out_specs=[pl.BlockSpec((tm, n), lambda i: (i, 0)) for n in splits],
        out_shape=[_sds((M, n), out_dtype) for n in splits],
        compiler_params=_cp(("parallel",)),
    )(*args)
    return outs


def _mm_nn(pieces, ws, *, name, out_dtype=F32, residual=None, epilogue=None, after=None, w_transposed=False, tm=512):
    M = pieces[0][0].shape[0]
    K = ws[0][1] if w_transposed else ws[0][0].shape[1]
    tm = min(tm, M)
    flat = [p for grp in pieces for p in grp]
    n_p = len(flat)
    n_w = len(ws)
    fn, row_ins, bc_ins, row_outs, acc_outs = epilogue or (None, [], [], [_sds((M, K), out_dtype)], [])
    if residual is not None:
        assert epilogue is None
        row_ins = [residual]
    n_r, n_b, n_o = len(row_ins), len(bc_ins), len(row_outs)
    lead = [] if after is None else [after]

    def body(*refs):
        refs = refs[len(lead):]
        p_refs = refs[:n_p]
        w_refs = refs[n_p:n_p + n_w]
        extra = [r[...] for r in refs[n_p + n_w:n_p + n_w + n_r + n_b]]
        o_refs = refs[n_p + n_w + n_r + n_b:n_p + n_w + n_r + n_b + n_o]
        a_refs = refs[n_p + n_w + n_r + n_b + n_o:]
        acc = None
        k = 0
        for gi, grp in enumerate(pieces):
            c0 = 0
            for p in grp:
                n = p.shape[1]
                if w_transposed:
                    t = lax.dot_general(p_refs[k][...], w_refs[gi][...], (((1,), (1,)), ((), ())), preferred_element_type=F32)
                else:
                    t = jnp.dot(p_refs[k][...], w_refs[gi][c0:c0 + n, :], preferred_element_type=F32)
                acc = t if acc is None else acc + t
                c0 += n
                k += 1
        if fn is None:
            res = (acc + extra[0] if residual is not None else acc,)
        else:
            res = fn(acc, *extra)
        for o_ref, val in zip(o_refs, res[:n_o]):
            o_ref[...] = val.astype(o_ref.dtype)
        if acc_outs:
            @pl.when(pl.program_id(0) == 0)
            def _():
                for a_ref in a_refs:
                    a_ref[...] = jnp.zeros_like(a_ref)
            for a_ref, val in zip(a_refs, res[n_o:]):
                a_ref[...] += val

    in_specs = [pl.BlockSpec((tm, p.shape[1]), lambda i: (i, 0)) for p in flat]
    in_specs += [_wspec(w) for w in ws]
    in_specs += [pl.BlockSpec((tm, r.shape[1]), lambda i: (i, 0)) for r in row_ins]
    in_specs += [pl.BlockSpec(b.shape, lambda i: (0, 0)) for b in bc_ins]
    out_specs = [pl.BlockSpec((tm, s.shape[1]), lambda i: (i, 0)) for s in row_outs]
    out_specs += [pl.BlockSpec(s.shape, lambda i: (0, 0)) for s in acc_outs]
    outs = pl.pallas_call(
        body, name=name, grid=(M // tm,), in_specs=[pl.BlockSpec(memory_space=pl.ANY)] * len(lead) + in_specs,
        out_specs=out_specs, out_shape=list(row_outs) + list(acc_outs),
        compiler_params=_cp(("arbitrary",) if acc_outs else ("parallel",)),
    )(*lead, *flat, *[w[0] for w in ws], *row_ins, *bc_ins)
    return outs if epilogue is not None else outs[0]


def _mm_tn(pieces, x, *, name, tt=512):
    M, K = x.shape
    tt = min(tt, M)
    ns = [p.shape[1] for p in pieces]
    offs = [sum(ns[:i]) for i in range(len(ns))]
    N = sum(ns)
    n_p = len(pieces)

    def body(*refs):
        p_refs = refs[:n_p]
        x_ref = refs[n_p]
        o_ref = refs[n_p + 1]

        @pl.when(pl.program_id(0) == 0)
        def _():
            o_ref[...] = jnp.zeros_like(o_ref)

        xv = x_ref[...]
        for p_ref, c0, n in zip(p_refs, offs, ns):
            o_ref[c0:c0 + n, :] += lax.dot_general(p_ref[...], xv, (((0,), (0,)), ((), ())),
                                                    preferred_element_type=F32)

    in_specs = [pl.BlockSpec((tt, n), lambda i: (i, 0)) for n in ns]
    in_specs.append(pl.BlockSpec((tt, K), lambda i: (i, 0)))
    return pl.pallas_call(
        body, name=name, grid=(M // tt,), in_specs=in_specs,
        out_specs=pl.BlockSpec((N, K), lambda i: (0, 0)),
        out_shape=_sds((N, K), F32),
        compiler_params=_cp(("arbitrary",)),
    )(*pieces, x)


def _rms_fwd(xf, w):
    inv = lax.rsqrt(jnp.mean(xf * xf, axis=-1, keepdims=True) + EPS)
    return xf * inv * w


def _rms_bwd(xf, w, dy):
    inv = lax.rsqrt(jnp.mean(xf * xf, axis=-1, keepdims=True) + EPS)
    xhat = xf * inv
    dxhat = dy * w
    dx = inv * (dxhat - xhat * jnp.mean(dxhat * xhat, axis=-1, keepdims=True))
    dw = jnp.sum(dy * xhat, axis=0, keepdims=True)
    return dx, dw


def _sigmoid(x):
    return 1.0 / (1.0 + jnp.exp(-x))


def _rowwise(fn, row_ins, bc_ins, row_outs, acc_outs, *, name, tm=256, after=None):
    M = row_outs[0].shape[0] if row_outs else row_ins[0][0].shape[0]
    assert M % tm == 0 and tm % SUBLANES == 0, (name, M, tm)
    n_r, n_b, n_o, n_a = len(row_ins), len(bc_ins), len(row_outs), len(acc_outs)
    n_after = 0 if after is None else 1

    def body(*refs):
        refs = refs[n_after:]
        ins = [r[...] for r in refs[:n_r + n_b]]
        o_refs = refs[n_r + n_b:n_r + n_b + n_o]
        a_refs = refs[n_r + n_b + n_o:]
        res = fn(*ins)
        for o_ref, val in zip(o_refs, res[:n_o]):
            o_ref[...] = val.astype(o_ref.dtype)
        if n_a:
            @pl.when(pl.program_id(0) == 0)
            def _():
                for a_ref in a_refs:
                    a_ref[...] = jnp.zeros_like(a_ref)
            for a_ref, val in zip(a_refs, res[n_o:]):
                a_ref[...] += val

    in_specs = [pl.BlockSpec((tm, cw), functools.partial(lambda i, cb, r0: (i + r0, cb), cb=cb, r0=r0))
                for (_, cw, cb, r0) in row_ins]
    in_specs += [pl.BlockSpec(b.shape, lambda i: (0, 0)) for b in bc_ins]
    out_specs = [pl.BlockSpec((tm, s.shape[1]), lambda i: (i, 0)) for s in row_outs]
    out_specs += [pl.BlockSpec(s.shape, lambda i: (0, 0)) for s in acc_outs]
    if n_after:
        in_specs = [pl.BlockSpec(memory_space=pl.ANY)] + in_specs
    return pl.pallas_call(
        body, name=name, grid=(M // tm,), in_specs=in_specs, out_specs=out_specs,
        out_shape=list(row_outs) + list(acc_outs),
        compiler_params=_cp(("arbitrary",) if n_a else ("parallel",)),
    )(*([after] if n_after else []), *[r[0] for r in row_ins], *bc_ins)


def _full(a, first_row_block=0):
    return (a, a.shape[1], 0, first_row_block)


def _conv_rows(ext, w_ref_val, lo):
    s1 = pltpu.roll(ext, 1, 0)
    s2 = pltpu.roll(ext, 2, 0)
    y = w_ref_val[0:1, :] * s2 + w_ref_val[1:2, :] * s1 + w_ref_val[2:3, :] * ext
    return y[SUBLANES:, :]


def _convact_fwd(gp, up, conv_w8, conv_b, *, name, tr=256, tc=1408):
    T, C = gp.shape
    tr = min(tr, T)
    hb = tr // SUBLANES

    def body(gp_ref, gph_ref, up_ref, w_ref, b_ref, act_ref):
        i = pl.program_id(1)
        halo = jnp.where(i > 0, gph_ref[...], 0.0)
        ext = jnp.concatenate([halo, gp_ref[...]], axis=0)
        gate = _conv_rows(ext, w_ref[...], 0) + b_ref[...]
        act_ref[...] = (gate * _sigmoid(gate) * up_ref[...]).astype(act_ref.dtype)

    return pl.pallas_call(
        body, name=name, grid=(C // tc, T // tr),
        in_specs=[pl.BlockSpec((tr, tc), lambda j, i: (i, j)),
                  pl.BlockSpec((SUBLANES, tc), lambda j, i: (jnp.maximum(i * hb - 1, 0), j)),
                  pl.BlockSpec((tr, tc), lambda j, i: (i, j)),
                  pl.BlockSpec((SUBLANES, tc), lambda j, i: (0, j)),
                  pl.BlockSpec((1, tc), lambda j, i: (0, j))],
        out_specs=pl.BlockSpec((tr, tc), lambda j, i: (i, j)),
        out_shape=_sds((T, C), BF16),
        compiler_params=_cp(("parallel", "parallel")),
    )(gp, gp, up, conv_w8, conv_b)


def _convact_bwd(gp, up, dact, conv_w8, conv_b, *, name, tr=256, tc=1408):
    T, C = gp.shape
    tr = min(tr, T)
    hb = tr // SUBLANES
    nr = T // tr

    def body(gp_ref, gpp_ref, gpn_ref, up_ref, upn_ref, da_ref, dan_ref, w_ref, b_ref,
             dgp_ref, dup_ref, dw_ref, db_ref):
        i = pl.program_id(1)
        w = w_ref[...]
        prev = jnp.where(i > 0, gpp_ref[...], 0.0)
        last = i == nr - 1
        gp_ext = jnp.concatenate([prev, gp_ref[...], gpn_ref[...]], axis=0)
        gate = _conv_rows(gp_ext, w, 0) + b_ref[...]
        up_e = jnp.concatenate([up_ref[...], upn_ref[...]], axis=0)
        da_e = jnp.concatenate([da_ref[...], dan_ref[...]], axis=0)
        row = lax.broadcasted_iota(jnp.int32, gate.shape, 0)
        valid = jnp.logical_or(row < tr, jnp.logical_not(last))
        sg = _sigmoid(gate)
        silu = gate * sg
        dgate = jnp.where(valid, da_e * up_e * (sg * (1.0 + gate * (1.0 - sg))), 0.0)
        dup_ref[...] = (da_e[:tr] * silu[:tr]).astype(dup_ref.dtype)
        n = tr + SUBLANES
        g1 = pltpu.roll(dgate, n - 1, 0)
        g2 = pltpu.roll(dgate, n - 2, 0)
        dgp = w[2:3, :] * dgate + w[1:2, :] * g1 + w[0:1, :] * g2
        dgp_ref[...] = dgp[:tr].astype(dgp_ref.dtype)
        gpc = gp_ref[...]
        dw0 = jnp.sum(gpc * g2[:tr], axis=0, keepdims=True)
        dw1 = jnp.sum(gpc * g1[:tr], axis=0, keepdims=True)
        dw2 = jnp.sum(gpc * dgate[:tr], axis=0, keepdims=True)
        dbv = jnp.sum(dgate[:tr], axis=0, keepdims=True)
        z = jnp.zeros((SUBLANES - 3, gpc.shape[1]), F32)

        @pl.when(i == 0)
        def _():
            dw_ref[...] = jnp.zeros_like(dw_ref)
            db_ref[...] = jnp.zeros_like(db_ref)

        dw_ref[...] += jnp.concatenate([dw0, dw1, dw2, z], axis=0)
        db_ref[...] += dbv

    cur = pl.BlockSpec((tr, tc), lambda j, i: (i, j))
    prv = pl.BlockSpec((SUBLANES, tc), lambda j, i: (jnp.maximum(i * hb - 1, 0), j))
    nxt = pl.BlockSpec((SUBLANES, tc), lambda j, i: (jnp.minimum((i + 1) * hb, T // SUBLANES - 1), j))
    return pl.pallas_call(
        body, name=name, grid=(C // tc, nr),
        in_specs=[cur, prv, nxt, cur, nxt, cur, nxt,
                  pl.BlockSpec((SUBLANES, tc), lambda j, i: (0, j)),
                  pl.BlockSpec((1, tc), lambda j, i: (0, j))],
        out_specs=[cur, cur,
                   pl.BlockSpec((SUBLANES, tc), lambda j, i: (0, j)),
                   pl.BlockSpec((1, tc), lambda j, i: (0, j))],
        out_shape=[_sds((T, C), BF16), _sds((T, C), BF16), _sds((SUBLANES, C), F32), _sds((1, C), F32)],
        compiler_params=_cp(("parallel", "arbitrary")),
    )(gp, gp, gp, up, up, dact, dact, conv_w8, conv_b)


def _cumsum_rows(x):
    n = x.shape[0]
    row = lax.broadcasted_iota(jnp.int32, x.shape, 0)
    s = 1
    while s < n:
        x = x + jnp.where(row >= s, pltpu.roll(x, s, 0), 0.0)
        s *= 2
    return x


def _rcumsum_rows(x):
    n = x.shape[0]
    row = lax.broadcasted_iota(jnp.int32, x.shape, 0)
    s = 1
    while s < n:
        x = x + jnp.where(row < n - s, pltpu.roll(x, n - s, 0), 0.0)
        s *= 2
    return x


def _dot_nt(a, b):
    return lax.dot_general(a.astype(BF16), b.astype(BF16), (((1,), (1,)), ((), ())), preferred_element_type=F32)


def _dot_tn(a, b):
    return lax.dot_general(a.astype(BF16), b.astype(BF16), (((0,), (0,)), ((), ())), preferred_element_type=F32)


def _dot_nn(a, b):
    return jnp.dot(a.astype(BF16), b.astype(BF16), preferred_element_type=F32)


def _hg_gates(hq, hf, lbv):
    sig = _sigmoid(hf)
    f = lbv + (1.0 - lbv) * sig
    return sig, f, jnp.log(f), 1.0 - f, hq * (HG_DK ** -0.5)


def _hg_sel_rows(ref, sp):
    return jnp.concatenate(
        [jnp.broadcast_to(ref[pl.ds(HG_SUB * i + sp, 1), :], (HG_SUB, HG_DK)) for i in range(HG_CHUNK // HG_SUB)], axis=0)


def _hg_masks():
    C = HG_CHUNK
    row = lax.broadcasted_iota(jnp.int32, (C, C), 0)
    col = lax.broadcasted_iota(jnp.int32, (C, C), 1)
    d = col - (row // HG_SUB) * HG_SUB
    tmod = row % HG_SUB
    diag_valid = jnp.logical_and(d >= 0, d <= tmod)
    return row, col, d, diag_valid


def _hg_scores(q, k, b, b_sc, k_sc):
    C, S = HG_CHUNK, HG_SUB
    row, col, d, diag_valid = _hg_masks()
    blocks = [jnp.zeros((S, C), F32)]
    for i in range(1, C // S):
        r = b_sc[pl.ds(S * i - 1, 1), :]
        qi = q[S * i:S * (i + 1)] * jnp.exp(b[S * i:S * (i + 1)] - r)
        kk = k * jnp.exp(jnp.minimum(r - b, 0.0))
        blocks.append(_dot_nt(qi, kk))
    a_off = jnp.where(col < (row // S) * S, jnp.concatenate(blocks, axis=0), 0.0)
    a_d = jnp.zeros((C, C), F32)
    for sp in range(S):
        bs = _hg_sel_rows(b_sc, sp)
        ks = _hg_sel_rows(k_sc, sp)
        e = jnp.exp(jnp.minimum(b - bs, 0.0))
        colv = jnp.sum(q * ks * e, axis=-1, keepdims=True)
        a_d = jnp.where(d == sp, colv, a_d)
    return a_off + jnp.where(diag_valid, a_d, 0.0)


def _hgrn_fwd(hq, hf, hi, lb, *, name):
    T = hq.shape[0]
    C, H, K = HG_CHUNK, HG_HEADS, HG_DK
    NC = T // C

    def body(hq_ref, hf_ref, hi_ref, lb_ref, o_ref, st_ref, s_sc, b_sc, k_sc):
        @pl.when(pl.program_id(0) == 0)
        def _():
            s_sc[...] = jnp.zeros_like(s_sc)

        st_all = s_sc[...]
        st_ref[0] = st_all
        outs, news = [], []
        for h in range(H):
            sl = slice(K * h, K * (h + 1))
            _, _, g, k, q = _hg_gates(hq_ref[:, sl], hf_ref[:, sl], lb_ref[:, sl])
            v = hi_ref[:, sl]
            b = _cumsum_rows(g)
            b_sc[h] = b
            k_sc[h] = k
            st0 = st_all[:, sl]
            bc = b_sc[h, pl.ds(C - 1, 1), :]
            a = _hg_scores(q, k, b, b_sc.at[h], k_sc.at[h])
            outs.append(_dot_nn(a, v) + _dot_nt(q * jnp.exp(b), st0))
            news.append(st0 * jnp.exp(bc) + _dot_tn(v, k * jnp.exp(bc - b)))
        o_ref[...] = jnp.concatenate(outs, axis=1)
        s_sc[...] = jnp.concatenate(news, axis=1)

    blk = pl.BlockSpec((C, H * K), lambda c: (c, 0))
    return pl.pallas_call(
        body, name=name, grid=(NC,),
        in_specs=[blk, blk, blk, pl.BlockSpec((1, H * K), lambda c: (0, 0))],
        out_specs=[blk, pl.BlockSpec((1, K, H * K), lambda c: (c, 0, 0))],
        out_shape=[_sds((T, H * K), F32), _sds((NC, K, H * K), F32)],
        scratch_shapes=[pltpu.VMEM((K, H * K), F32), pltpu.VMEM((H, C, K), F32), pltpu.VMEM((H, C, K), F32)],
        compiler_params=_cp(("arbitrary",)),
    )(hq, hf, hi, lb)


def _hgrn_bwd(hq, hf, hi, lb, states, do, *, name):
    T = hq.shape[0]
    C, H, K, S = HG_CHUNK, HG_HEADS, HG_DK, HG_SUB
    NC = T // C

    def one_head(hq_v, hf_v, v, lbv, st0, dst1, dout, b_sc, k_sc):
        sig, f, g, k, q = _hg_gates(hq_v, hf_v, lbv)
        b = _cumsum_rows(g)
        b_sc[...] = b
        k_sc[...] = k
        bc = b_sc[pl.ds(C - 1, 1), :]
        ebc = jnp.exp(bc)
        eb = jnp.exp(b)
        ekb = jnp.exp(bc - b)
        qt = q * eb
        kb = k * ekb
        row, col, d, diag_valid = _hg_masks()
        da = jnp.where(col <= row, _dot_nt(dout, v), 0.0)
        dqt = _dot_nn(dout, st0)
        dkb = _dot_nn(v, dst1)
        new_ds = _dot_tn(dout, qt) + dst1 * ebc
        dq = dqt * eb
        dk = dkb * ekb
        a_blocks = [jnp.zeros((S, C), F32)]
        dq_blocks = [jnp.zeros((S, K), F32)]
        for i in range(1, C // S):
            r = b_sc[pl.ds(S * i - 1, 1), :]
            eq = jnp.exp(b[S * i:S * (i + 1)] - r)
            ek = jnp.exp(jnp.minimum(r - b, 0.0))
            qi = q[S * i:S * (i + 1)] * eq
            kk = k * ek
            a_blocks.append(_dot_nt(qi, kk))
            dai = jnp.where(col[S * i:S * (i + 1)] < S * i, da[S * i:S * (i + 1)], 0.0)
            dq_blocks.append(_dot_nn(dai, kk) * eq)
            dk = dk + _dot_tn(dai, qi) * ek
        dq = dq + jnp.concatenate(dq_blocks, axis=0)
        a_off = jnp.where(col < (row // S) * S, jnp.concatenate(a_blocks, axis=0), 0.0)
        same_blk = (row // S == col // S).astype(BF16)
        tmod = (lax.broadcasted_iota(jnp.int32, (C, K), 0)) % S
        a_d = jnp.zeros((C, C), F32)
        for sp in range(S):
            bs = _hg_sel_rows(b_sc, sp)
            ks = _hg_sel_rows(k_sc, sp)
            e = jnp.where(tmod >= sp, jnp.exp(jnp.minimum(b - bs, 0.0)), 0.0)
            eks = e * ks
            a_d = jnp.where(d == sp, jnp.sum(q * eks, axis=-1, keepdims=True), a_d)
            dacol = jnp.sum(jnp.where(d == sp, da, 0.0), axis=-1, keepdims=True)
            dq = dq + dacol * eks
            blk_sum = jnp.dot(same_blk, (dacol * e * q).astype(BF16), preferred_element_type=F32)
            dk = dk + jnp.where(tmod == sp, blk_sum, 0.0)
        a = a_off + jnp.where(diag_valid, a_d, 0.0)
        dv = _dot_tn(a, dout) + _dot_nt(kb, dst1)
        extra =jnp.sum(dkb * kb, axis=0, keepdims=True) + ebc * jnp.sum(st0 * dst1, axis=0, keepdims=True)
        rowk = lax.broadcasted_iota(jnp.int32, (C, K), 0)
        db = q * dq - k * dk + jnp.where(rowk == C - 1, extra, 0.0)
        dg = _rcumsum_rows(db)
        df = dg / f - dk
        return (dq * (K ** -0.5), df * (1.0 - lbv) * sig * (1.0 - sig), dv,
                jnp.sum(df * (1.0 - sig), axis=0, keepdims=True), new_ds)

    def body(hq_ref, hf_ref, hi_ref, lb_ref, st_ref, do_ref, dq_ref, dhf_ref, dv_ref, dlb_ref, ds_sc, b_sc, k_sc):
        @pl.when(pl.program_id(0) == 0)
        def _():
            ds_sc[...] = jnp.zeros_like(ds_sc)
            dlb_ref[...] = jnp.zeros_like(dlb_ref)

        st_all = st_ref[0]
        ds_all = ds_sc[...]
        res = []
        for h in range(H):
            sl = slice(K * h, K * (h + 1))
            res.append(one_head(hq_ref[:, sl], hf_ref[:, sl], hi_ref[:, sl], lb_ref[:, sl], st_all[:, sl], ds_all[:, sl],
                                do_ref[:, sl], b_sc.at[h], k_sc.at[h]))
        cat = lambda j: jnp.concatenate([r[j] for r in res], axis=1)
        dq_ref[...] = cat(0).astype(dq_ref.dtype)
        dhf_ref[...] = cat(1).astype(dhf_ref.dtype)
        dv_ref[...] = cat(2).astype(dv_ref.dtype)
        dlb_ref[...] += cat(3)
        ds_sc[...] = cat(4)

    blk = pl.BlockSpec((C, H * K), lambda c: (NC - 1 - c, 0))
    par = pl.BlockSpec((1, H * K), lambda c: (0, 0))
    return pl.pallas_call(
        body, name=name, grid=(NC,),
        in_specs=[blk, blk, blk, par, pl.BlockSpec((1, K, H * K), lambda c: (NC - 1 - c, 0, 0)), blk],
        out_specs=[blk, blk, blk, par],
        out_shape=[_sds((T, H * K), BF16)] * 3 + [_sds((1, H * K), F32)],
        scratch_shapes=[pltpu.VMEM((K, H * K), F32), pltpu.VMEM((H, C, K), F32), pltpu.VMEM((H, C, K), F32)],
        compiler_params=_cp(("arbitrary",)),
    )(hq, hf, hi, lb, states, do)


def _att_valid(n):
    R, B = ATT_GROUP * ATT_BLOCK, ATT_BLOCK
    j = lax.broadcasted_iota(jnp.int32, (2 * B, R), 0)
    t = lax.broadcasted_iota(jnp.int32, (2 * B, R), 1) % B
    dist = t + B - j
    first_key = jnp.where(n > 0, 0, B)
    return jnp.logical_and(jnp.logical_and(dist >= 0, dist < B), j >= first_key)


def _att_load(cur_ref, prev_ref, ba_ref, kv):
    hd = ATT_HD
    def cols(ref, c0):
        return ref[:, c0:c0 + hd] + ba_ref[:, c0:c0 + hd]
    qs = jnp.concatenate([cols(cur_ref, hd * (ATT_GROUP * kv + g)) for g in range(ATT_GROUP)], axis=0)
    kc = jnp.concatenate([cols(prev_ref, ATT_Q_W + hd * kv), cols(cur_ref, ATT_Q_W + hd * kv)], axis=0)
    vc = jnp.concatenate([cols(prev_ref, ATT_Q_W + ATT_KV_W + hd * kv), cols(cur_ref, ATT_Q_W + ATT_KV_W + hd * kv)], axis=0)
    return qs, kc, vc


def _att_probs(qs, kc, valid, sink_ref, kv):
    scale = 1.0 / math.sqrt(ATT_HD)
    s = jnp.where(valid, _dot_nt(kc, qs) * scale, NEG)
    sink = jnp.concatenate([jnp.full((1, ATT_BLOCK), sink_ref[0, ATT_GROUP * kv + g], F32) for g in range(ATT_GROUP)], axis=1)
    m = jnp.maximum(jnp.max(s, axis=0, keepdims=True), sink)
    p = jnp.exp(s - m)
    ps = jnp.exp(sink - m)
    inv = 1.0 / (jnp.sum(p, axis=0, keepdims=True) + ps)
    return p * inv, ps * inv


def _attn_fwd(att, b_attn, sinks, *, name):
    T = att.shape[0]
    B = ATT_BLOCK
    NB = T // B

    def body(sink_ref, cur_ref, prev_ref, ba_ref, o_ref):
        valid = _att_valid(pl.program_id(0))
        for kv in range(ATT_KV):
            qs, kc, vc = _att_load(cur_ref, prev_ref, ba_ref, kv)
            prob, _ = _att_probs(qs, kc, valid, sink_ref, kv)
            o = _dot_tn(prob, vc)
            for g in range(ATT_GROUP):
                c0 = ATT_HD * (ATT_GROUP * kv + g)
                o_ref[:, c0:c0 + ATT_HD] = o[B * g:B * (g + 1)]

    return pl.pallas_call(
        body, name=name, grid=(NB,),
        in_specs=[pl.BlockSpec(memory_space=pltpu.SMEM),
                  pl.BlockSpec((B, ATT_COLS), lambda n: (n, 0)),
                  pl.BlockSpec((B, ATT_COLS), lambda n: (jnp.maximum(n - 1, 0), 0)),
                  pl.BlockSpec((1, ATT_COLS), lambda n: (0, 0))],
        out_specs=pl.BlockSpec((B, ATT_Q_W), lambda n: (n, 0)),
        out_shape=_sds((T, ATT_Q_W), F32),
        compiler_params=_cp(("parallel",)),
    )(sinks, att, att, b_attn)


def _attn_bwd(att, b_attn, sinks, dmix, *, name):
    T = att.shape[0]
    B, hd = ATT_BLOCK, ATT_HD
    NB = T // B
    scale = 1.0 / math.sqrt(hd)

    def body(sink_ref, cur_ref, prev_ref, ba_ref, do_ref, daq_ref, dakv_ref, dsink_ref, dbq_ref, dbkv_ref,
             carry_sc, cprev_sc, ccur_sc):
        n = pl.program_id(0)

        @pl.when(n == 0)
        def _():
            carry_sc[...] = jnp.zeros_like(carry_sc)
            dsink_ref[...] = jnp.zeros_like(dsink_ref)
            dbq_ref[...] = jnp.zeros_like(dbq_ref)
            dbkv_ref[...] = jnp.zeros_like(dbkv_ref)

        @pl.when(n < NB)
        def _():
            valid = _att_valid(n)
            hrow = lax.broadcasted_iota(jnp.int32, (SUBLANES, 128), 0)
            dsink = jnp.zeros((SUBLANES, 128), F32)
            for kv in range(ATT_KV):
                qs, kc, vc = _att_load(cur_ref, prev_ref, ba_ref, kv)
                prob, psink = _att_probs(qs, kc, valid, sink_ref, kv)
                dout = jnp.concatenate(
                    [do_ref[:, hd * (ATT_GROUP * kv + g):hd * (ATT_GROUP * kv + g + 1)] for g in range(ATT_GROUP)], axis=0)
                dp = _dot_nt(vc, dout)
                delta = jnp.sum(prob * dp, axis=0, keepdims=True)
                dsc = prob * (dp - delta) * scale
                dq = _dot_tn(dsc, kc)
                dk = _dot_nn(dsc, qs)
                dvv = _dot_nn(prob, dout)
                dsk = psink * delta
                for g in range(ATT_GROUP):
                    h = ATT_GROUP * kv + g
                    daq_ref[:, hd * h:hd * (h + 1)] = dq[B * g:B * (g + 1)].astype(daq_ref.dtype)
                    tot = jnp.sum(dsk[:, B * g:B * (g + 1)], axis=1, keepdims=True)
                    dsink = dsink - jnp.where(hrow == h, tot, 0.0)
                cprev_sc[:, hd * kv:hd * (kv + 1)] = dk[:B]
                ccur_sc[:, hd * kv:hd * (kv + 1)] = dk[B:]
                cprev_sc[:, ATT_KV_W + hd * kv:ATT_KV_W + hd * (kv + 1)] = dvv[:B]
                ccur_sc[:, ATT_KV_W + hd * kv:ATT_KV_W + hd * (kv + 1)] = dvv[B:]
            dsink_ref[...] += dsink
            dbq_ref[...] += jnp.sum(daq_ref[...].astype(F32), axis=0, keepdims=True)
            done = carry_sc[...] + cprev_sc[...]
            dakv_ref[...] = done.astype(dakv_ref.dtype)
            dbkv_ref[...] += jnp.sum(done.astype(dakv_ref.dtype).astype(F32), axis=0, keepdims=True)
            carry_sc[...] = ccur_sc[...]

        @pl.when(n == NB)
        def _():
            done = carry_sc[...]
            dakv_ref[...] = done.astype(dakv_ref.dtype)
            dbkv_ref[...] += jnp.sum(done.astype(dakv_ref.dtype).astype(F32), axis=0, keepdims=True)

    cl = lambda n: jnp.minimum(n, NB - 1)
    return pl.pallas_call(
        body, name=name, grid=(NB + 1,),
        in_specs=[pl.BlockSpec(memory_space=pltpu.SMEM),
                  pl.BlockSpec((B, ATT_COLS), lambda n: (cl(n), 0)),
                  pl.BlockSpec((B, ATT_COLS), lambda n: (jnp.maximum(cl(n) - 1, 0), 0)),
                  pl.BlockSpec((1, ATT_COLS), lambda n: (0, 0)),
                  pl.BlockSpec((B, ATT_Q_W), lambda n: (cl(n), 0))],
        out_specs=[pl.BlockSpec((B, ATT_Q_W), lambda n: (cl(n), 0)),
                   pl.BlockSpec((B, 2 * ATT_KV_W), lambda n: (jnp.maximum(n - 1, 0), 0)),
                   pl.BlockSpec((SUBLANES, 128), lambda n: (0, 0)),
                   pl.BlockSpec((1, ATT_Q_W), lambda n: (0, 0)),
                   pl.BlockSpec((1, 2 * ATT_KV_W), lambda n: (0, 0))],
        out_shape=[_sds((T, ATT_Q_W), BF16), _sds((T, 2 * ATT_KV_W), BF16), _sds((SUBLANES, 128), F32),
                   _sds((1, ATT_Q_W), F32), _sds((1, 2 * ATT_KV_W), F32)],
        scratch_shapes=[pltpu.VMEM((B, 2 * ATT_KV_W), F32)] * 3,
        compiler_params=_cp(("arbitrary",)),
    )(sinks, att, att, b_attn, dmix)


def _silu_and_grad(x):
    sg = _sigmoid(x)
    return x * sg, sg * (1.0 + x * (1.0 - sg))


def _mix_fwd_fn(o_raw, hg, o_att, hgw):
    outs = []
    for h in range(HG_HEADS):
        sl = slice(HG_DK * h, HG_DK * (h + 1))
        silu, _ = _silu_and_grad(hg[:, sl])
        outs.append(_rms_fwd(o_raw[:, sl], hgw) * silu)
    outs.append(o_att)
    return (jnp.concatenate(outs, axis=1),)


def _mix_bwd_fn(o_raw, hg, dmix, hgw):
    dos, dhgs = [], []
    dw = jnp.zeros((1, HG_DK), F32)
    for h in range(HG_HEADS):
        sl = slice(HG_DK * h, HG_DK * (h + 1))
        silu, dsilu = _silu_and_grad(hg[:, sl])
        dy = dmix[:, sl]
        dhgs.append(dy * _rms_fwd(o_raw[:, sl], hgw) * dsilu)
        dx, dwh = _rms_bwd(o_raw[:, sl], hgw, dy * silu)
        dos.append(dx)
        dw = dw + dwh
    return jnp.concatenate(dos, axis=1), jnp.concatenate(dhgs, axis=1), dw


def _final_fn(h2, tgt, wf):
    d = h2.shape[1]
    err = _rms_fwd(h2, wf) - tgt
    loss_cols = (0.5 / d) * jnp.sum(err * err, axis=0, keepdims=True)
    dh2, dwf = _rms_bwd(h2, wf, err * (1.0 / d))
    return dh2, dh2, loss_cols, dwf


class _NoExchange:
    def __init__(self, weights):
        self.weights = weights

    def start(self):
        return None

    def w_in(self, after):
        return self.weights["w_in_t"]

    def w_out(self, after):
        return self.weights

    def ffn(self, after):
        return self.weights

    def ffn_grads(self, gs):
        return None

    def ffn_grads_send(self, after):
        return None


def _local_step(x, tgt, p, ex):
    T, D = x.shape
    row = lambda n, dt: _sds((T, n), dt)
    acc = lambda n: _sds((1, n), F32)

    (u,) = _rowwise(lambda xv, w: (_rms_fwd(xv, w),), [_full(x)], [p["norm_mix_w"]], [row(D, BF16)], [], name="rms_mix",
                    after=ex.start())
    p = dict(p, w_in_t=ex.w_in(u))
    hq, hf, hi, hg, att = _mm_nt(u, p["w_in_t"], splits=[HG_W] * 4 + [ATT_COLS], out_dtype=F32, name="in_proj")
    o_raw, states = _hgrn_fwd(hq, hf, hi, p["lb"], name="hgrn_fwd")
    o_att = _attn_fwd(att, p["b_attn"], p["sinks"], name="attn_fwd")
    (mix,) = _rowwise(_mix_fwd_fn, [_full(o_raw), _full(hg), _full(o_att)], [p["hg_norm_w"]], [row(D, BF16)], [],
                      name="mix_fwd")
    p = dict(p, **ex.w_out(mix))
    def out_epilogue(prod, xv, w):
        h1v = prod + xv
        return h1v, _rms_fwd(h1v, w)

    h1, v = _mm_nn([[mix]], [p["w_out"]], name="out_proj",
                   epilogue=(out_epilogue, [x], [p["norm_ffn_w"]], [row(D, F32), row(D, BF16)], []))
    p = dict(p, **ex.ffn(v))
    (gp,) = _mm_nt(v, p["w_gate_t"], splits=[D_FF], out_dtype=F32, name="gate_proj")
    (up,) = _mm_nt(v, p["w_up_t"], splits=[D_FF], out_dtype=F32, name="up_proj")
    act = _convact_fwd(gp, up, p["conv_w8"], p["conv_b"], name="convact_fwd")
    def down_epilogue(prod, h1v, tgtv, wf):
        return _final_fn(prod + h1v, tgtv, wf)

    dh2, dh2_b, loss_cols, d_final = _mm_nn(
        [[act]], [p["w_down"]], name="down_proj_loss",
        epilogue=(down_epilogue, [h1, tgt], [p["final_norm_w"]], [row(D, F32), row(D, BF16)], [acc(D), acc(D)]))

    (dact,) = _mm_nt(dh2_b, p["w_down"], splits=[D_FF], out_dtype=F32, name="d_act")
    g_down = _mm_tn([act], dh2_b, name="g_down")
    dgp, dup, d_conv_w8, d_conv_b = _convact_bwd(gp, up, dact, p["conv_w8"], p["conv_b"], name="convact_bwd")
    g_gate_t = _mm_tn([dgp], v, name="g_gate")
    g_up_t = _mm_tn([dup], v, name="g_up")
    swapping = ex.ffn_grads([g_gate_t, g_up_t, g_down])

    def ffn_norm_bwd(dvv, hv, dh2v, w):
        dx, dw = _rms_bwd(hv, w, dvv)
        dh1v = dx + dh2v
        return dh1v, dh1v, dw

    dh1, dh1_b, d_norm_ffn = _mm_nn(
        [[dgp], [dup]], [p["w_gate_t"], p["w_up_t"]], name="d_v_norm", after=swapping,
        epilogue=(ffn_norm_bwd, [h1, dh2], [p["norm_ffn_w"]], [row(D, F32), row(D, BF16)], [acc(D)]))
    sent = ex.ffn_grads_send(dh1_b)
    def mix_bwd(dmixv, o_rawv, hgv, hgw):
        do_rawv, dhgv, dw = _mix_bwd_fn(o_rawv, hgv, dmixv[:, :HG_W], hgw)
        return do_rawv, dhgv, dmixv[:, HG_W:], dw

    do_raw, dhg, do_att, d_hg_norm = _mm_nn(
        [[dh1_b]], [p["w_out"]], name="d_mix_bwd", w_transposed=True, after=sent,
        epilogue=(mix_bwd, [o_raw, hg], [p["hg_norm_w"]], [row(HG_W, F32), row(HG_W, BF16), row(ATT_Q_W, F32)], [acc(HG_DK)]))
    g_out = _mm_tn([mix], dh1_b, name="g_out")
    daq, dakv, d_sinks8, d_bq, d_bkv = _attn_bwd(att, p["b_attn"], p["sinks"], do_att, name="attn_bwd")
    dhq, dhf, dhi, d_lb = _hgrn_bwd(hq, hf, hi, p["lb"], states, do_raw, name="hgrn_bwd")
    pieces = [dhq, dhf, dhi, dhg, daq, dakv]
    g_in_t = _mm_tn(pieces, u, name="g_in")

    def mix_norm_bwd(duv, xv, dh1v, w):
        dx, dw = _rms_bwd(xv, w, duv)
        return dx + dh1v, dw

    dx, d_norm_mix = _mm_nn([pieces], [p["w_in_t"]], name="d_u_norm",
                            epilogue=(mix_norm_bwd, [x, dh1], [p["norm_mix_w"]], [row(D, F32)], [acc(D)]))
    grads = dict(g_in_t=g_in_t, g_out=g_out, g_gate_t=g_gate_t, g_up_t=g_up_t, g_down=g_down,
                 norm_mix_w=d_norm_mix, b_attn=jnp.concatenate([d_bq, d_bkv], axis=1), lb=d_lb, hg_norm_w=d_hg_norm,
                 sinks8=d_sinks8, norm_ffn_w=d_norm_ffn, conv_w8=d_conv_w8, conv_b=d_conv_b, final_norm_w=d_final)
    return loss_cols, dx, grads


SLAB = (IN_COLS // N_CHIPS, D_FF // N_CHIPS, D_FF // N_CHIPS, D_FF // N_CHIPS, D_MODEL // N_CHIPS)
N_W = len(SLAB)
PACK_OFF = tuple(sum(SLAB[:i]) for i in range(N_W))
PACK_ROWS = sum(SLAB)
FULL_OFF = tuple(N_CHIPS * o for o in PACK_OFF)
FULL_ROWS = N_CHIPS * PACK_ROWS
HALF = tuple(s // 2 for s in SLAB)
HPACK_OFF = tuple(sum(HALF[:i]) for i in range(N_W))
HPACK_ROWS = sum(HALF)
HFULL_OFF = tuple(N_CHIPS * o for o in HPACK_OFF)
HFULL_ROWS = N_CHIPS * HPACK_ROWS
CHIP_FLIPS = ((1, 0), (0, 1), (1, 1))
N_DEV = 8
BF16_ROWS = 16
ANY = pl.BlockSpec(memory_space=pl.ANY)


def _pos():
    return lax.axis_index("x"), lax.axis_index("y"), lax.axis_index("c")


def _flip(v, f):
    return 1 - v if f else v


def _rcopy(src, dst, ssem, rsem, dev):
    return pltpu.make_async_remote_copy(src_ref=src, dst_ref=dst, send_sem=ssem, recv_sem=rsem, device_id=dev,
                                        device_id_type=pl.DeviceIdType.MESH)


def _rows(ref, start, n, align=None):
    if not isinstance(start, int):
        if align is None:
            align = SUBLANES * (4 // jnp.dtype(ref.dtype).itemsize)
        start = pl.multiple_of(start, align)
    return ref.at[pl.ds(start, n), :]


FFN_W = (1, 2, 3)
N_PEER = 1 + len(CHIP_FLIPS)
HBM = pl.BlockSpec(memory_space=pltpu.HBM)
SEM = pl.BlockSpec(memory_space=pltpu.SEMAPHORE)
EFFECT = pltpu.SideEffectType.DATAFLOW_SIDE_EFFECTING
LANES = 128


def _gather_start(pack, cw8):
    D = pack.shape[1]
    lands = [lax.empty((N_CHIPS * SLAB[0], D), pack.dtype), lax.empty((3 * N_CHIPS * SLAB[1], D), pack.dtype),
             lax.empty((N_CHIPS * SLAB[4], D), pack.dtype), lax.empty((N_CHIPS,) + cw8.shape, cw8.dtype)]
    bufs = [pack, cw8] + lands

    def body(pack_ref, cw_ref, l_in, l_ffn, l_out, l_cw, *rest):
        in_send, in_recv, out_send, out_recv, ffn_send, ffn_recv = rest[:6]
        token = rest[-1]
        x, y, c = _pos()
        q = 2 * x + y
        peers = _gather_peers(x, y, c)
        for k, peer in enumerate(peers):
            _rcopy(_rows(pack_ref, PACK_OFF[0], SLAB[0]), _rows(l_in, q * SLAB[0], SLAB[0], BF16_ROWS),
                   in_send.at[k], in_recv.at[k], peer).start()
        for k, peer in enumerate(peers):
            _rcopy(_rows(pack_ref, PACK_OFF[4], SLAB[4]), _rows(l_out, q * SLAB[4], SLAB[4], BF16_ROWS),
                   out_send.at[k], out_recv.at[k], peer).start()
            _rcopy(cw_ref, l_cw.at[q], out_send.at[N_PEER + k], out_recv.at[N_PEER + k], peer).start()
        for j, w in enumerate(FFN_W):
            for k, peer in enumerate(peers):
                _rcopy(_rows(pack_ref, PACK_OFF[w], SLAB[w]), _rows(l_ffn, (j * N_CHIPS + q) * SLAB[w], SLAB[w], BF16_ROWS),
                       ffn_send.at[k], ffn_recv.at[k], peer).start()
        token[...] = jnp.zeros_like(token)

    n_sem = (N_PEER, N_PEER, 2 * N_PEER, 2 * N_PEER, N_PEER, N_PEER)
    outs = pl.pallas_call(
        body, name="gather_start", in_specs=[HBM] * len(bufs),
        out_specs=[SEM] * len(n_sem) + [HBM] * len(bufs) + [pl.BlockSpec(memory_space=pltpu.VMEM)],
        out_shape=[pltpu.SemaphoreType.DMA((n,)) for n in n_sem]
        + [pltpu.HBM(b.shape, b.dtype) for b in bufs] + [_sds((SUBLANES, LANES), F32)],
        input_output_aliases={i: len(n_sem) + i for i in range(len(bufs))},
        compiler_params=pltpu.CompilerParams(has_side_effects=EFFECT),
    )(*[pltpu.with_memory_space_constraint(b, pltpu.HBM) for b in bufs])
    bufs_out = outs[len(n_sem):]
    return dict(in_sems=outs[0:2], out_sems=outs[2:4], ffn_sems=outs[4:6], pack=bufs_out[0], cw=bufs_out[1], l_in=bufs_out[2],
                l_ffn=bufs_out[3], l_out=bufs_out[4], l_cw=bufs_out[5], token=bufs_out[6])


def _gather_peers(x, y, c):
    return [(x, y, 1 - c)] + [(_flip(x, fx), _flip(y, fy), c) for fx, fy in CHIP_FLIPS]


def _gather_wait_in(g, after):
    def body(pack_ref, l_in, send, recv, after_ref, pack_out, l_out):
        for k, peer in enumerate(_gather_peers(*_pos())):
            cp = _rcopy(_rows(pack_ref, PACK_OFF[0], SLAB[0]), _rows(l_in, 0, SLAB[0]), send.at[k], recv.at[k], peer)
            cp.wait_send()
            cp.wait_recv()

    return pl.pallas_call(
        body, name="gather_wait_in", in_specs=[HBM, HBM, SEM, SEM, ANY], out_specs=[HBM, HBM],
        out_shape=[pltpu.HBM(g["pack"].shape, g["pack"].dtype), pltpu.HBM(g["l_in"].shape, g["l_in"].dtype)],
        input_output_aliases={0: 0, 1: 1}, compiler_params=pltpu.CompilerParams(has_side_effects=EFFECT),
    )(g["pack"], g["l_in"], *g["in_sems"], after)


def _gather_wait_out(g, pack, after):
    def body(pack_ref, cw_ref, l_out, l_cw, send, recv, after_ref, o_pack, o_out, o_cw):
        for k, peer in enumerate(_gather_peers(*_pos())):
            for cp in (_rcopy(_rows(pack_ref, PACK_OFF[4], SLAB[4]), _rows(l_out, 0, SLAB[4]), send.at[k], recv.at[k], peer),
                       _rcopy(cw_ref, l_cw.at[0], send.at[N_PEER + k], recv.at[N_PEER + k], peer)):
                cp.wait_send()
                cp.wait_recv()

    ins = [pack, g["cw"], g["l_out"], g["l_cw"]]
    return pl.pallas_call(
        body, name="gather_wait_out", in_specs=[HBM] * 4 + [SEM, SEM, ANY], out_specs=[HBM] * 3,
        out_shape=[pltpu.HBM(b.shape, b.dtype) for b in (ins[0], ins[2], ins[3])],
        input_output_aliases={0: 0, 2: 1, 3: 2}, compiler_params=pltpu.CompilerParams(has_side_effects=EFFECT),
    )(*ins, *g["out_sems"], after)


def _gather_wait_ffn(g, pack, after):
    n_ffn = len(FFN_W) * SLAB[FFN_W[0]]

    def body(pack_ref, l_ffn, send, recv, after_ref, o_ffn):
        for k, peer in enumerate(_gather_peers(*_pos())):
            cp = _rcopy(_rows(pack_ref, PACK_OFF[FFN_W[0]], n_ffn), _rows(l_ffn, 0, n_ffn), send.at[k], recv.at[k], peer)
            cp.wait_send()
            cp.wait_recv()

    return pl.pallas_call(
        body, name="gather_wait_ffn", in_specs=[HBM] * 2 + [SEM, SEM, ANY], out_specs=HBM,
        out_shape=pltpu.HBM(g["l_ffn"].shape, g["l_ffn"].dtype),
        input_output_aliases={1: 0}, compiler_params=pltpu.CompilerParams(has_side_effects=EFFECT),
    )(pack, g["l_ffn"], *g["ffn_sems"], after)


def _exchange_halves(ws, gs, small, *, name):
    D = gs[0].shape[1]
    n = len(ws)
    has_small = small is not None

    def body(*refs):
        g = refs[:n]
        t = refs[n + has_small:2 * n + has_small]
        sems = refs[2 * n + 2 * has_small:]
        d2d_send, d2d_recv = sems[0], sems[1]
        x, y, c = _pos()
        sib = (x, y, 1 - c)
        drains = []
        for i, w in enumerate(ws):
            h = HALF[w]
            for qq in range(N_CHIPS):
                _rcopy(_rows(g[i], qq * SLAB[w] + (1 - c) * h, h), _rows(t[i], qq * h, h),
                       d2d_send.at[i], d2d_recv.at[i], sib).start()
            drains.append(_rcopy(t[i], t[i], d2d_send.at[i], d2d_recv.at[i], sib))
        if has_small:
            small_ref, sall_ref = refs[n], refs[2 * n + 1]
            sm_send, sm_recv, loc_sem = sems[2], sems[3], sems[4]
            me = 4 * x + 2 * y + c
            own_small = pltpu.make_async_copy(small_ref, sall_ref.at[me], loc_sem)
            own_small.start()
            for f in range(1, N_DEV):
                peer = (_flip(x, f & 4), _flip(y, f & 2), _flip(c, f & 1))
                cp = _rcopy(small_ref, sall_ref.at[me], sm_send.at[f - 1], sm_recv.at[f - 1], peer)
                cp.start()
                drains.append(cp)
        for d in drains:
            d.wait_recv()
        for d in drains:
            d.wait_send()
        if has_small:
            own_small.wait()

    out_shape = [_sds((N_CHIPS * HALF[w], D), F32) for w in ws]
    scratch = [pltpu.SemaphoreType.DMA((n,)), pltpu.SemaphoreType.DMA((n,))]
    if has_small:
        out_shape.append(_sds((N_DEV,) + small.shape, F32))
        scratch += [pltpu.SemaphoreType.DMA((N_DEV - 1,)), pltpu.SemaphoreType.DMA((N_DEV - 1,)), pltpu.SemaphoreType.DMA]
    return pl.pallas_call(
        body, name=name, in_specs=[ANY] * (n + has_small), out_specs=[ANY] * (n + has_small),
        out_shape=out_shape, scratch_shapes=scratch,
    )(*gs, *([small] if has_small else []))


def _halves_copies(ws, g, t, send_sems, recv_sems):
    x, y, c = _pos()
    sib = (x, y, 1 - c)
    cps = []
    for i, w in enumerate(ws):
        h = HALF[w]
        for qq in range(N_CHIPS):
            cps.append(_rcopy(_rows(g[i], qq * SLAB[w] + (1 - c) * h, h), _rows(t[i], qq * h, h),
                              send_sems.at[N_CHIPS * i + qq], recv_sems.at[N_CHIPS * i + qq], sib))
    return cps


def _halves_start(ws, gs, *, name):
    D = gs[0].shape[1]
    n = len(ws)
    bufs = list(gs) + [lax.empty((N_CHIPS * HALF[w], D), F32) for w in ws]

    def body(*refs):
        for cp in _halves_copies(ws, refs[:n], refs[n:2 * n], refs[2 * n], refs[2 * n + 1]):
            cp.start()
        refs[-1][...] = jnp.zeros_like(refs[-1])

    outs = pl.pallas_call(
        body, name=name, in_specs=[HBM] * (2 * n),
        out_specs=[SEM, SEM] + [HBM] * (2 * n) + [pl.BlockSpec(memory_space=pltpu.VMEM)],
        out_shape=[pltpu.SemaphoreType.DMA((N_CHIPS * n,)), pltpu.SemaphoreType.DMA((N_CHIPS * n,))]
        + [pltpu.HBM(b.shape, b.dtype) for b in bufs] + [_sds((SUBLANES, LANES), F32)],
        input_output_aliases={i: 2 + i for i in range(2 * n)},
        compiler_params=pltpu.CompilerParams(has_side_effects=EFFECT),
    )(*[pltpu.with_memory_space_constraint(b, pltpu.HBM) for b in bufs])
    return dict(sems=outs[0:2], gs=outs[2:2 + n], theirs=outs[2 + n:2 + 2 * n], token=outs[-1])


def _halves_wait(ws, s, after, *, name):
    n = len(ws)

    def body(*refs):
        for cp in _halves_copies(ws, refs[:n], refs[n:2 * n], refs[2 * n], refs[2 * n + 1]):
            cp.wait_send()
            cp.wait_recv()

    bufs = list(s["gs"]) + list(s["theirs"])
    outs = pl.pallas_call(
        body, name=name, in_specs=[HBM] * (2 * n) + [SEM, SEM, ANY], out_specs=[HBM] * (2 * n),
        out_shape=[pltpu.HBM(b.shape, b.dtype) for b in bufs],
        input_output_aliases={i: i for i in range(2 * n)},
        compiler_params=pltpu.CompilerParams(has_side_effects=EFFECT),
    )(*bufs, *s["sems"], after)
    return outs[:n], outs[n:]


REDUCE_SPLIT = 2


def _chip_partial(ws, gs, theirs, *, name, out_dtype=F32):
    D = gs[0].shape[1]
    n = len(ws)

    def body(*refs):
        for i in range(n):
            refs[2 * n + i][...] = (refs[i][...] + refs[n + i][...]).astype(out_dtype)

    blk = [HALF[w] // REDUCE_SPLIT for w in ws]
    mine = [pl.BlockSpec((b, D), lambda qq, j: ((2 * qq + lax.axis_index("c")) * REDUCE_SPLIT + j, 0)) for b in blk]
    flat = [pl.BlockSpec((b, D), lambda qq, j: (qq * REDUCE_SPLIT + j, 0)) for b in blk]
    return pl.pallas_call(
        body, name=name, grid=(N_CHIPS, REDUCE_SPLIT), in_specs=mine + flat, out_specs=flat,
        out_shape=[_sds((N_CHIPS * HALF[w], D), out_dtype) for w in ws],
        compiler_params=_cp(("parallel", "parallel")),
    )(*gs, *theirs)


def _partial_copies(ws, part, got, send_sems, recv_sems):
    x, y, c = _pos()
    cps = []
    for k, (fx, fy) in enumerate(CHIP_FLIPS):
        peer = (_flip(x, fx), _flip(y, fy), c)
        qp = 2 * _flip(x, fx) + _flip(y, fy)
        for i, w in enumerate(ws):
            cps.append(_rcopy(_rows(part[i], qp * HALF[w], HALF[w]), _rows(got[i], k * HALF[w], HALF[w]),
                              send_sems.at[len(ws) * k + i], recv_sems.at[len(ws) * k + i], peer))
    return cps


def _send_chip_partials(ws, parts, *, name):
    D = parts[0].shape[1]
    n = len(ws)

    def body(*refs):
        cps = _partial_copies(ws, refs[:n], refs[n:2 * n], refs[2 * n], refs[2 * n + 1])
        for cp in cps:
            cp.start()
        for cp in cps:
            cp.wait_recv()
        for cp in cps:
            cp.wait_send()

    return pl.pallas_call(
        body, name=name, in_specs=[ANY] * n, out_specs=[ANY] * n,
        out_shape=[_sds((len(CHIP_FLIPS) * HALF[w], D), parts[0].dtype) for w in ws],
        scratch_shapes=[pltpu.SemaphoreType.DMA((len(CHIP_FLIPS) * n,)), pltpu.SemaphoreType.DMA((len(CHIP_FLIPS) * n,))],
    )(*parts)


def _send_start(ws, parts, *, name):
    D = parts[0].shape[1]
    n = len(ws)
    bufs = list(parts) + [lax.empty((len(CHIP_FLIPS) * HALF[w], D), parts[0].dtype) for w in ws]

    def body(*refs):
        send_sems, recv_sems = refs[2 * n], refs[2 * n + 1]
        for cp in _partial_copies(ws, refs[:n], refs[n:2 * n], send_sems, recv_sems):
            cp.start()
        refs[-1][...] = jnp.zeros_like(refs[-1])

    outs = pl.pallas_call(
        body, name=name, in_specs=[HBM] * (2 * n),
        out_specs=[SEM, SEM] + [HBM] * (2 * n) + [pl.BlockSpec(memory_space=pltpu.VMEM)],
        out_shape=[pltpu.SemaphoreType.DMA((len(CHIP_FLIPS) * n,)), pltpu.SemaphoreType.DMA((len(CHIP_FLIPS) * n,))]
        + [pltpu.HBM(b.shape, b.dtype) for b in bufs] + [_sds((SUBLANES, LANES), F32)],
        input_output_aliases={i: 2 + i for i in range(2 * n)},
        compiler_params=pltpu.CompilerParams(has_side_effects=EFFECT),
    )(*[pltpu.with_memory_space_constraint(b, pltpu.HBM) for b in bufs])
    return dict(sems=outs[0:2], parts=outs[2:2 + n], got=outs[2 + n:2 + 2 * n], token=outs[-1])


def _send_wait(ws, s, after, *, name):
    n = len(ws)

    def body(*refs):
        for cp in _partial_copies(ws, refs[:n], refs[n:2 * n], refs[2 * n], refs[2 * n + 1]):
            cp.wait_send()
            cp.wait_recv()

    bufs = list(s["parts"]) + list(s["got"])
    outs = pl.pallas_call(
        body, name=name, in_specs=[HBM] * (2 * n) + [SEM, SEM, ANY], out_specs=[HBM] * (2 * n),
        out_shape=[pltpu.HBM(b.shape, b.dtype) for b in bufs],
        input_output_aliases={i: i for i in range(2 * n)},
        compiler_params=pltpu.CompilerParams(has_side_effects=EFFECT),
    )(*bufs, *s["sems"], after)
    return outs[:n], outs[n:]


def _chip_reduce(ws, parts, got, *, name, after=None):
    D = parts[0].shape[1]
    nk = len(CHIP_FLIPS)
    n = len(ws)
    extra = [] if after is None else [after]

    def body(*refs):
        refs = refs[len(extra):]
        outs = refs[(1 + nk) * n:]
        for i in range(n):
            acc = refs[i][...].astype(F32)
            for k in range(nk):
                acc = acc + refs[n * (1 + k) + i][...].astype(F32)
            outs[i][...] = acc

    blk = [HALF[w] // REDUCE_SPLIT for w in ws]

    def q_idx(j):
        return (2 * lax.axis_index("x") + lax.axis_index("y")) * REDUCE_SPLIT + j

    in_specs = [pl.BlockSpec((b, D), lambda j: (q_idx(j), 0)) for b in blk]
    for k in range(nk):
        in_specs += [pl.BlockSpec((b, D), functools.partial(lambda j, k: (k * REDUCE_SPLIT + j, 0), k=k)) for b in blk]
    out_specs = [pl.BlockSpec((b, D), lambda j: (lax.axis_index("c") * REDUCE_SPLIT + j, 0)) for b in blk]
    return pl.pallas_call(
        body, name=name, grid=(REDUCE_SPLIT,), in_specs=[ANY] * len(extra) + in_specs, out_specs=out_specs,
        out_shape=[_sds((SLAB[w], D), F32) for w in ws],
        compiler_params=_cp(("parallel",)),
    )(*extra, *parts, *[g for _ in range(nk) for g in got])


def _exchange_reduced(ws, shards, *, name):
    n = len(ws)

    def body(*refs):
        ins, outs = refs[:n], refs[n:2 * n]
        send_sems, recv_sems = refs[2 * n], refs[2 * n + 1]
        x, y, c = _pos()
        sib = (x, y, 1 - c)
        cps = []
        for i, w in enumerate(ws):
            cp = _rcopy(_rows(ins[i], c * HALF[w], HALF[w]), _rows(outs[i], c * HALF[w], HALF[w]),
                        send_sems.at[i], recv_sems.at[i], sib)
            cp.start()
            cps.append(cp)
        for cp in cps:
            cp.wait_recv()
        for cp in cps:
            cp.wait_send()

    return pl.pallas_call(
        body, name=name, in_specs=[ANY] * n, out_specs=[ANY] * n,
        out_shape=[_sds(s.shape, s.dtype) for s in shards], input_output_aliases={i: i for i in range(n)},
        scratch_shapes=[pltpu.SemaphoreType.DMA((n,)), pltpu.SemaphoreType.DMA((n,))],
    )(*shards)


def _adamw_fn(w, g, m, v):
    m2 = ADAM_B1 * m + (1.0 - ADAM_B1) * g
    v2 = ADAM_B2 * v + (1.0 - ADAM_B2) * (g * g)
    m_hat = m2 / (1.0 - ADAM_B1 ** ADAM_STEP)
    v_hat = v2 / (1.0 - ADAM_B2 ** ADAM_STEP)
    return -ADAM_LR * (m_hat / (jnp.sqrt(v_hat) + ADAM_EPS) + ADAM_WD * w), m2, v2


def _adamw(w, g, m, v, *, name):
    shp = _sds(w.shape, F32)
    rows = w.shape[0]
    tm = max(t for t in range(SUBLANES, 512 + 1, SUBLANES) if rows % t == 0)
    return _rowwise(_adamw_fn, [_full(w), _full(g), _full(m), _full(v)], [], [shp] * 3, [], name=name, tm=tm)


SMALL_SEGS = (("loss", 8), ("norm_mix_w", 8), ("b_attn", 8), ("lb_logits", 8), ("hg_norm_w", 8), ("sinks", 8),
              ("norm_ffn_w", 8), ("conv_w", 72), ("conv_b", 24), ("final_norm_w", 8))
SMALL_OFF = {n: sum(r for _, r in SMALL_SEGS[:i]) for i, (n, _) in enumerate(SMALL_SEGS)}
SMALL_ROWS = sum(r for _, r in SMALL_SEGS)
LANES = 128


def _pack_small(parts):
    segs = []
    for n, r in SMALL_SEGS:
        a = parts.get(n)
        flat = jnp.zeros((0,), F32) if a is None else a.reshape(-1).astype(F32)
        segs.append(jnp.pad(flat, (0, r * LANES - flat.shape[0])).reshape(r, LANES))
    return jnp.concatenate(segs, axis=0)


def _unpack_small(pack, n, shape):
    size = math.prod(shape)
    r0 = SMALL_OFF[n]
    return pack[r0:r0 + dict(SMALL_SEGS)[n]].reshape(-1)[:size].reshape(shape)


def _small_update(sall, wp, mp, vp):
    R = SMALL_ROWS
    r_lb = SMALL_OFF["lb_logits"]

    def body(s_ref, w_ref, m_ref, v_ref, g_ref, d_ref, m2_ref, v2_ref, loss_ref):
        g = s_ref[0]
        for i in range(1, N_DEV):
            g = g + s_ref[i]
        tot = jnp.sum(jnp.sum(g[0:8], axis=1, keepdims=True), axis=0, keepdims=True)
        loss_ref[...] = jnp.broadcast_to(tot, loss_ref.shape)
        lg = w_ref[r_lb:r_lb + 8, :]
        p0 = _sigmoid(lg - pltpu.roll(lg, 4, 0))
        d = g[r_lb:r_lb + 8]
        d = d + pltpu.roll(d, 4, 0)
        sign = jnp.where(lax.broadcasted_iota(jnp.int32, d.shape, 0) < 4, 1.0, -1.0)
        g = jnp.concatenate([g[:r_lb], sign * d * p0 * (1.0 - p0), g[r_lb + 8:]], axis=0)
        g_ref[...] = g
        d_ref[...], m2_ref[...], v2_ref[...] = _adamw_fn(w_ref[...], g, m_ref[...], v_ref[...])

    full = pl.BlockSpec((R, LANES), lambda: (0, 0))
    return pl.pallas_call(
        body, name="small_update",
        in_specs=[pl.BlockSpec((N_DEV, R, LANES), lambda: (0, 0, 0)), full, full, full],
        out_specs=[full, full, full, full, pl.BlockSpec((8, LANES), lambda: (0, 0))],
        out_shape=[_sds((R, LANES), F32)] * 4 + [_sds((8, LANES), F32)],
        compiler_params=_cp(),
    )(sall, wp, mp, vp)


def _lb_fwd(lb_logits):
    n = lb_logits.shape[1]

    def body(l_ref, o_ref):
        o_ref[...] = _sigmoid(l_ref[0:1, :] - l_ref[1:2, :])

    return pl.pallas_call(body, name="lb_fwd", out_shape=_sds((1, n), F32), compiler_params=_cp())(lb_logits)


class _MeshExchange:
    def __init__(self, pack, cw8):
        self.gather = _gather_start(pack, cw8)
        self.sent = None
        self.conv_w8 = None

    def start(self):
        return self.gather["token"]

    def w_in(self, after):
        self.pack, l_in = _gather_wait_in(self.gather, after)
        return (l_in, N_CHIPS * SLAB[0], 0)

    def w_out(self, after):
        self.pack, l_out, l_cw = _gather_wait_out(self.gather, self.pack, after)
        self.conv_w8 = jnp.concatenate([l_cw[i] for i in range(N_CHIPS)], axis=1)
        return dict(w_out=(l_out, N_CHIPS * SLAB[4], 0), conv_w8=self.conv_w8)

    def ffn(self, after):
        l_ffn = _gather_wait_ffn(self.gather, self.pack, after)
        rows = N_CHIPS * SLAB[FFN_W[0]]
        return dict(w_gate_t=(l_ffn, rows, 0), w_up_t=(l_ffn, rows, 1), w_down=(l_ffn, rows, 2))

    def ffn_grads(self, gs):
        self.swap = _halves_start(FFN_W, gs, name="halves_ffn_start")
        return self.swap["token"]

    def ffn_grads_send(self, after):
        gs, theirs = _halves_wait(FFN_W, self.swap, after, name="halves_ffn_wait")
        parts = _chip_partial(FFN_W, gs, theirs, name="chip_partial_ffn")
        self.sent = _send_start(FFN_W, parts, name="send_ffn_start")
        return self.sent["token"]


def kernel(x, norm_mix_w, w_in, b_attn, lb_logits, hg_norm_w, sinks, w_out, norm_ffn_w, w_gate, w_up, conv_w, conv_b, w_down, final_norm_w, loss_target, m_norm_mix_w, m_w_in, m_b_attn, m_lb_logits, m_hg_norm_w, m_sinks, m_w_out, m_norm_ffn_w, m_w_gate, m_w_up, m_conv_w, m_conv_b, m_w_down, m_final_norm_w, v_norm_mix_w, v_w_in, v_b_attn, v_lb_logits, v_hg_norm_w, v_sinks, v_w_out, v_norm_ffn_w, v_w_gate, v_w_up, v_conv_w, v_conv_b, v_w_down, v_final_norm_w):
    D = D_MODEL
    q = 2 * lax.axis_index("x") + lax.axis_index("y")
    ccols = D_FF // N_CHIPS

    pack = jnp.concatenate([w_in[0].T, w_gate[0].T, w_up[0].T, w_down[0], w_out[0]], axis=0).astype(BF16)
    cw8 = jnp.concatenate([conv_w[0], jnp.zeros((SUBLANES - 3, ccols), F32)], axis=0)
    ex = _MeshExchange(pack, cw8)
    p = dict(norm_mix_w=norm_mix_w, b_attn=b_attn, lb=_lb_fwd(lb_logits), hg_norm_w=hg_norm_w, sinks=sinks,
             norm_ffn_w=norm_ffn_w, conv_b=conv_b, final_norm_w=final_norm_w.reshape(1, D))
    loss_cols, dx, g = _local_step(x[0], loss_target[0], p, ex)
    conv_w8 = ex.conv_w8

    small = _pack_small(dict(loss=loss_cols, norm_mix_w=g["norm_mix_w"], b_attn=g["b_attn"], lb_logits=g["lb"],
                             hg_norm_w=g["hg_norm_w"], sinks=g["sinks8"], norm_ffn_w=g["norm_ffn_w"],
                             conv_w=g["conv_w8"][:3], conv_b=g["conv_b"], final_norm_w=g["final_norm_w"]))
    parts_ffn, got_ffn = _send_wait(FFN_W, ex.sent, dx, name="send_ffn_wait")
    late = (0, 4)
    gs = [g["g_in_t"], g["g_out"]]
    *theirs, sall = _exchange_halves(late, gs, small, name="exchange_halves_late")
    parts_late = _chip_partial(late, gs, theirs, name="chip_partial_late", out_dtype=BF16)
    sent_late = _send_start(late, parts_late, name="send_late_start")
    big = {}

    def finish(ws, parts, got, specs, tag, after):
        shards = _exchange_reduced(ws, _chip_reduce(ws, parts, got, name="chip_reduce_" + tag, after=after),
                                   name="exchange_reduced_" + tag)
        for gw, (n, w, m, v, tr) in zip(shards, specs):
            view = (lambda a: a[0].T) if tr else (lambda a: a[0])
            back = (lambda a: a.T[None]) if tr else (lambda a: a[None])
            d_, m_, v_ = _adamw(view(w), gw, view(m), view(v), name="adamw_" + n)
            big[n] = (back(gw), back(d_), back(m_), back(v_))
        return d_

    last = finish(FFN_W, parts_ffn, got_ffn, (("w_gate", w_gate, m_w_gate, v_w_gate, True), ("w_up", w_up, m_w_up, v_w_up, True),
                                              ("w_down", w_down, m_w_down, v_w_down, False)), "ffn", sent_late["token"])
    parts_late, got_late = _send_wait(late, sent_late, last, name="send_late_wait")
    finish(late, parts_late, got_late, (("w_in", w_in, m_w_in, v_w_in, True), ("w_out", w_out, m_w_out, v_w_out, False)),
           "late", None)

    def place(a):
        return lax.dynamic_update_slice(jnp.zeros((3, D_FF), F32), a[0], (0, q * ccols))

    def small_pack(ws, cw):
        nm, ba, lbl, hg, sk, nf, cb, fn = ws
        return _pack_small(dict(norm_mix_w=nm, b_attn=ba, lb_logits=lbl, hg_norm_w=hg,
                                sinks=jnp.broadcast_to(sk.reshape(ATT_HEADS, 1), (ATT_HEADS, LANES)), norm_ffn_w=nf,
                                conv_w=cw, conv_b=cb, final_norm_w=fn))

    wp = small_pack((norm_mix_w, b_attn, lb_logits, hg_norm_w, sinks, norm_ffn_w, conv_b, final_norm_w), conv_w8[:3])
    mp = small_pack((m_norm_mix_w, m_b_attn, m_lb_logits, m_hg_norm_w, m_sinks, m_norm_ffn_w, m_conv_b, m_final_norm_w),
                    place(m_conv_w))
    vp = small_pack((v_norm_mix_w, v_b_attn, v_lb_logits, v_hg_norm_w, v_sinks, v_norm_ffn_w, v_conv_b, v_final_norm_w),
                    place(v_conv_w))
    outs = _small_update(sall, wp, mp, vp)
    loss = outs[4][0, 0]

    def small_out(pk, n, ref):
        if n == "sinks":
            return pk[SMALL_OFF[n]:SMALL_OFF[n] + ATT_HEADS, 0].reshape(ref.shape)
        if n == "conv_w":
            full = _unpack_small(pk, n, (3, D_FF))
            return lax.dynamic_slice(full, (0, q * ccols), (3, ccols))[None]
        return _unpack_small(pk, n, ref.shape)

    refs = dict(norm_mix_w=norm_mix_w, b_attn=b_attn, lb_logits=lb_logits, hg_norm_w=hg_norm_w, sinks=sinks,
                norm_ffn_w=norm_ffn_w, conv_w=conv_w, conv_b=conv_b, final_norm_w=final_norm_w)
    order = ("norm_mix_w", "w_in", "b_attn", "lb_logits", "hg_norm_w", "sinks", "w_out", "norm_ffn_w", "w_gate", "w_up",
             "conv_w", "conv_b", "w_down", "final_norm_w")
    res = [loss, dx[None]]
    for k in range(4):
        for n in order:
            res.append(big[n][k] if n in big else small_out(outs[k], n, refs[n]))
    return tuple(res)
```

```python
import functools
import math

import jax
import jax.numpy as jnp
from jax import lax
from jax.experimental import pallas as pl
from jax.experimental.pallas import tpu as pltpu

F32 = jnp.float32
BF16 = jnp.bfloat16

D_MODEL = 1024
HG_HEADS = 4
HG_DK = 128
HG_W = HG_HEADS * HG_DK
HG_CHUNK = 64
HG_SUB = 8
ATT_HEADS = 8
ATT_KV = 2
ATT_GROUP = ATT_HEADS // ATT_KV
ATT_HD = 64
ATT_BLOCK = 128
ATT_Q_W = ATT_HEADS * ATT_HD
ATT_KV_W = ATT_KV * ATT_HD
ATT_COLS = ATT_Q_W + 2 * ATT_KV_W
IN_COLS = 4 * HG_W + ATT_COLS
D_FF = 2816
EPS = 1e-6
ADAM_LR, ADAM_B1, ADAM_B2, ADAM_EPS, ADAM_WD, ADAM_STEP = 0.001, 0.9, 0.999, 1e-08, 0.01, 10
NEG = -1e30

V7X_VMEM_BYTES = 64 * 1024 * 1024
VMEM_LIMIT = 48 * 1024 * 1024
SUBLANES = 8

N_CHIPS = 4


def _cp(sem=None, **kw):
    return pltpu.CompilerParams(dimension_semantics=sem, vmem_limit_bytes=VMEM_LIMIT, **kw)


def _sds(shape, dtype):
    return jax.ShapeDtypeStruct(shape, dtype)


def _wspec(w):
    arr, rows, blk = w
    return pl.BlockSpec((rows, arr.shape[1]), lambda i: (blk, 0))


def _mm_nt(a, w, *, splits, out_dtype, name, after=None, tm=512):
    M, K = a.shape
    N = w[1]
    tm = min(tm, M)
    assert sum(splits) == N and M % tm == 0
    offs = [sum(splits[:i]) for i in range(len(splits))]
    n_in = 2 if after is None else 3

    def body(*refs):
        a_ref, w_ref = refs[0], refs[1]
        acc = lax.dot_general(a_ref[...], w_ref[...], (((1,), (1,)), ((), ())), preferred_element_type=F32)
        for o_ref, c0, n in zip(refs[n_in:], offs, splits):
            o_ref[...] = acc[:, c0:c0 + n].astype(out_dtype)

    in_specs = [pl.BlockSpec((tm, K), lambda i: (i, 0)), _wspec(w)]
    args = [a, w[0]]
    if after is not None:
        in_specs.append(pl.BlockSpec(memory_space=pl.ANY))
        args.append(after)
    outs = pl.pallas_call(
        body, name=name, grid=(M // tm,), in_specs=in_specs,
        out_specs=[pl.BlockSpec((tm, n), lambda i: (i, 0)) for n in splits],
        out_shape=[_sds((M, n), out_dtype) for n in splits],
        compiler_params=_cp(("parallel",)),
    )(*args)
    return outs


def _mm_nn(pieces, ws, *, name, out_dtype=F32, residual=None, epilogue=None, prologue=None, after=None,
           w_transposed=False, tm=512):
    pro_fn, pro_rows, pro_bc, pro_out = prologue or (None, [], [], None)
    if prologue is not None:
        assert pieces is None and len(ws) == 1
        pieces = [[pro_out]]
    M = pieces[0][0].shape[0]
    K = ws[0][1] if w_transposed else ws[0][0].shape[1]
    tm = min(tm, M)
    flat = [] if prologue is not None else [p for grp in pieces for p in grp]
    n_p = len(flat)
    n_w = len(ws)
    n_pr, n_pb = len(pro_rows), len(pro_bc)
    fn, row_ins, bc_ins, row_outs, acc_outs = epilogue or (None, [], [], [_sds((M, K), out_dtype)], [])
    if residual is not None:
        assert epilogue is None
        row_ins = [residual]
    n_r, n_b, n_o = len(row_ins), len(bc_ins), len(row_outs)
    lead = [] if after is None else [after]

    def body(*refs):
        refs = refs[len(lead):]
        p_refs = refs[:n_p]
        w_refs = refs[n_p:n_p + n_w]
        extra = [r[...] for r in refs[n_p + n_w:n_p + n_w + n_r + n_b]]
        base = n_p + n_w + n_r + n_b
        pro = [r[...] for r in refs[base:base + n_pr + n_pb]]
        base += n_pr + n_pb
        o_refs = refs[base:base + n_o]
        a_refs = refs[base + n_o:base + n_o + len(acc_outs)]
        if pro_fn is not None:
            lhs = pro_fn(*pro).astype(pro_out.dtype)
            refs[-1][...] = lhs
            tiles = [lhs]
        else:
            tiles = [r[...] for r in p_refs]
        acc = None
        k = 0
        for gi, grp in enumerate(pieces):
            c0 = 0
            for p in grp:
                n = p.shape[1]
                if w_transposed:
                    t = lax.dot_general(tiles[k], w_refs[gi][...], (((1,), (1,)), ((), ())), preferred_element_type=F32)
                else:
                    t = jnp.dot(tiles[k], w_refs[gi][c0:c0 + n, :], preferred_element_type=F32)
                acc = t if acc is None else acc + t
                c0 += n
                k += 1
        if fn is None:
            res = (acc + extra[0] if residual is not None else acc,)
        else:
            res = fn(acc, *extra)
        for o_ref, val in zip(o_refs, res[:n_o]):
            o_ref[...] = val.astype(o_ref.dtype)
        if acc_outs:
            @pl.when(pl.program_id(0) == 0)
            def _():
                for a_ref in a_refs:
                    a_ref[...] = jnp.zeros_like(a_ref)
            for a_ref, val in zip(a_refs, res[n_o:]):
                a_ref[...] += val

    in_specs = [pl.BlockSpec((tm, p.shape[1]), lambda i: (i, 0)) for p in flat]
    in_specs += [_wspec(w) for w in ws]
    in_specs += [pl.BlockSpec((tm, r.shape[1]), lambda i: (i, 0)) for r in row_ins]
    in_specs += [pl.BlockSpec(b.shape, lambda i: (0, 0)) for b in bc_ins]
    in_specs += [pl.BlockSpec((tm, r.shape[1]), lambda i: (i, 0)) for r in pro_rows]
    in_specs += [pl.BlockSpec(b.shape, lambda i: (0, 0)) for b in pro_bc]
    out_specs = [pl.BlockSpec((tm, s.shape[1]), lambda i: (i, 0)) for s in row_outs]
    out_specs += [pl.BlockSpec(s.shape, lambda i: (0, 0)) for s in acc_outs]
    pro_outs = [] if prologue is None else [pro_out]
    out_specs += [pl.BlockSpec((tm, s.shape[1]), lambda i: (i, 0)) for s in pro_outs]
    outs = pl.pallas_call(
        body, name=name, grid=(M // tm,), in_specs=[pl.BlockSpec(memory_space=pl.ANY)] * len(lead) + in_specs,
        out_specs=out_specs, out_shape=list(row_outs) + list(acc_outs) + pro_outs,
        compiler_params=_cp(("arbitrary",) if acc_outs else ("parallel",)),
    )(*lead, *flat, *[w[0] for w in ws], *row_ins, *bc_ins, *pro_rows, *pro_bc)
    return outs if (epilogue is not None or prologue is not None) else outs[0]


def _mm_tn(pieces, x, *, name, tt=1024):
    M, K = x.shape
    tt = min(tt, M)
    ns = [p.shape[1] for p in pieces]
    offs = [sum(ns[:i]) for i in range(len(ns))]
    N = sum(ns)
    n_p = len(pieces)

    def body(*refs):
        p_refs = refs[:n_p]
        x_ref = refs[n_p]
        o_ref = refs[n_p + 1]

        @pl.when(pl.program_id(0) == 0)
        def _():
            o_ref[...] = jnp.zeros_like(o_ref)

        xv = x_ref[...]
        for p_ref, c0, n in zip(p_refs, offs, ns):
            o_ref[c0:c0 + n, :] += lax.dot_general(p_ref[...], xv, (((0,), (0,)), ((), ())),
                                                    preferred_element_type=F32)

    in_specs = [pl.BlockSpec((tt, n), lambda i: (i, 0)) for n in ns]
    in_specs.append(pl.BlockSpec((tt, K), lambda i: (i, 0)))
    return pl.pallas_call(
        body, name=name, grid=(M // tt,), in_specs=in_specs,
        out_specs=pl.BlockSpec((N, K), lambda i: (0, 0)),
        out_shape=_sds((N, K), F32),
        compiler_params=_cp(("arbitrary",)),
    )(*pieces, x)


def _rms_fwd(xf, w):
    inv = lax.rsqrt(jnp.mean(xf * xf, axis=-1, keepdims=True) + EPS)
    return xf * inv * w


def _rms_bwd(xf, w, dy):
    inv = lax.rsqrt(jnp.mean(xf * xf, axis=-1, keepdims=True) + EPS)
    xhat = xf * inv
    dxhat = dy * w
    dx = inv * (dxhat - xhat * jnp.mean(dxhat * xhat, axis=-1, keepdims=True))
    dw = jnp.sum(dy * xhat, axis=0, keepdims=True)
    return dx, dw


def _sigmoid(x):
    return 1.0 / (1.0 + jnp.exp(-x))


def _rowwise(fn, row_ins, bc_ins, row_outs, acc_outs, *, name, tm=256, after=None):
    M = row_outs[0].shape[0] if row_outs else row_ins[0][0].shape[0]
    assert M % tm == 0 and tm % SUBLANES == 0, (name, M, tm)
    n_r, n_b, n_o, n_a = len(row_ins), len(bc_ins), len(row_outs), len(acc_outs)
    n_after = 0 if after is None else 1

    def body(*refs):
        refs = refs[n_after:]
        ins = [r[...] for r in refs[:n_r + n_b]]
        o_refs = refs[n_r + n_b:n_r + n_b + n_o]
        a_refs = refs[n_r + n_b + n_o:]
        res = fn(*ins)
        for o_ref, val in zip(o_refs, res[:n_o]):
            o_ref[...] = val.astype(o_ref.dtype)
        if n_a:
            @pl.when(pl.program_id(0) == 0)
            def _():
                for a_ref in a_refs:
                    a_ref[...] = jnp.zeros_like(a_ref)
            for a_ref, val in zip(a_refs, res[n_o:]):
                a_ref[...] += val

    in_specs = [pl.BlockSpec((tm, cw), functools.partial(lambda i, cb, r0: (i + r0, cb), cb=cb, r0=r0))
                for (_, cw, cb, r0) in row_ins]
    in_specs += [pl.BlockSpec(b.shape, lambda i: (0, 0)) for b in bc_ins]
    out_specs = [pl.BlockSpec((tm, s.shape[1]), lambda i: (i, 0)) for s in row_outs]
    out_specs += [pl.BlockSpec(s.shape, lambda i: (0, 0)) for s in acc_outs]
    if n_after:
        in_specs = [pl.BlockSpec(memory_space=pl.ANY)] + in_specs
    return pl.pallas_call(
        body, name=name, grid=(M // tm,), in_specs=in_specs, out_specs=out_specs,
        out_shape=list(row_outs) + list(acc_outs),
        compiler_params=_cp(("arbitrary",) if n_a else ("parallel",)),
    )(*([after] if n_after else []), *[r[0] for r in row_ins], *bc_ins)


def _full(a, first_row_block=0):
    return (a, a.shape[1], 0, first_row_block)


def _conv_rows(ext, w_ref_val, lo):
    s1 = pltpu.roll(ext, 1, 0)
    s2 = pltpu.roll(ext, 2, 0)
    y = w_ref_val[0:1, :] * s2 + w_ref_val[1:2, :] * s1 + w_ref_val[2:3, :] * ext
    return y[SUBLANES:, :]


def _convact_fwd(gp, up, conv_w8, conv_b, *, name, tr=512, tc=1408):
    T, C = gp.shape
    tr = min(tr, T)
    hb = tr // SUBLANES

    def body(gp_ref, gph_ref, up_ref, w_ref, b_ref, act_ref):
        i = pl.program_id(1)
        halo = jnp.where(i > 0, gph_ref[...], 0.0)
        ext = jnp.concatenate([halo, gp_ref[...]], axis=0)
        gate = _conv_rows(ext, w_ref[...], 0) + b_ref[...]
        act_ref[...] = (gate * _sigmoid(gate) * up_ref[...]).astype(act_ref.dtype)

    return pl.pallas_call(
        body, name=name, grid=(C // tc, T // tr),
        in_specs=[pl.BlockSpec((tr, tc), lambda j, i: (i, j)),
                  pl.BlockSpec((SUBLANES, tc), lambda j, i: (jnp.maximum(i * hb - 1, 0), j)),
                  pl.BlockSpec((tr, tc), lambda j, i: (i, j)),
                  pl.BlockSpec((SUBLANES, tc), lambda j, i: (0, j)),
                  pl.BlockSpec((1, tc), lambda j, i: (0, j))],
        out_specs=pl.BlockSpec((tr, tc), lambda j, i: (i, j)),
        out_shape=_sds((T, C), BF16),
        compiler_params=_cp(("parallel", "parallel")),
    )(gp, gp, up, conv_w8, conv_b)


def _convact_bwd(gp, up, dact, conv_w8, conv_b, *, name, tr=256, tc=1408):
    T, C = gp.shape
    tr = min(tr, T)
    hb = tr // SUBLANES
    nr = T // tr

    def body(gp_ref, gpp_ref, gpn_ref, up_ref, upn_ref, da_ref, dan_ref, w_ref, b_ref,
             dgp_ref, dup_ref, dw_ref, db_ref):
        i = pl.program_id(1)
        w = w_ref[...]
        prev = jnp.where(i > 0, gpp_ref[...], 0.0)
        last = i == nr - 1
        gp_ext = jnp.concatenate([prev, gp_ref[...], gpn_ref[...]], axis=0)
        gate = _conv_rows(gp_ext, w, 0) + b_ref[...]
        up_e = jnp.concatenate([up_ref[...], upn_ref[...]], axis=0)
        da_e = jnp.concatenate([da_ref[...], dan_ref[...]], axis=0)
        row = lax.broadcasted_iota(jnp.int32, gate.shape, 0)
        valid = jnp.logical_or(row < tr, jnp.logical_not(last))
        sg = _sigmoid(gate)
        silu = gate * sg
        dgate = jnp.where(valid, da_e * up_e * (sg * (1.0 + gate * (1.0 - sg))), 0.0)
        dup_ref[...] = (da_e[:tr] * silu[:tr]).astype(dup_ref.dtype)
        n = tr + SUBLANES
        g1 = pltpu.roll(dgate, n - 1, 0)
        g2 = pltpu.roll(dgate, n - 2, 0)
        dgp = w[2:3, :] * dgate + w[1:2, :] * g1 + w[0:1, :] * g2
        dgp_ref[...] = dgp[:tr].astype(dgp_ref.dtype)
        gpc = gp_ref[...]
        dw0 = jnp.sum(gpc * g2[:tr], axis=0, keepdims=True)
        dw1 = jnp.sum(gpc * g1[:tr], axis=0, keepdims=True)
        dw2 = jnp.sum(gpc * dgate[:tr], axis=0, keepdims=True)
        dbv = jnp.sum(dgate[:tr], axis=0, keepdims=True)
        z = jnp.zeros((SUBLANES - 3, gpc.shape[1]), F32)

        @pl.when(i == 0)
        def _():
            dw_ref[...] = jnp.zeros_like(dw_ref)
            db_ref[...] = jnp.zeros_like(db_ref)

        dw_ref[...] += jnp.concatenate([dw0, dw1, dw2, z], axis=0)
        db_ref[...] += dbv

    cur = pl.BlockSpec((tr, tc), lambda j, i: (i, j))
    prv = pl.BlockSpec((SUBLANES, tc), lambda j, i: (jnp.maximum(i * hb - 1, 0), j))
    nxt = pl.BlockSpec((SUBLANES, tc), lambda j, i: (jnp.minimum((i + 1) * hb, T // SUBLANES - 1), j))
    return pl.pallas_call(
        body, name=name, grid=(C // tc, nr),
        in_specs=[cur, prv, nxt, cur, nxt, cur, nxt,
                  pl.BlockSpec((SUBLANES, tc), lambda j, i: (0, j)),
                  pl.BlockSpec((1, tc), lambda j, i: (0, j))],
        out_specs=[cur, cur,
                   pl.BlockSpec((SUBLANES, tc), lambda j, i: (0, j)),
                   pl.BlockSpec((1, tc), lambda j, i: (0, j))],
        out_shape=[_sds((T, C), BF16), _sds((T, C), BF16), _sds((SUBLANES, C), F32), _sds((1, C), F32)],
        compiler_params=_cp(("parallel", "arbitrary")),
    )(gp, gp, gp, up, up, dact, dact, conv_w8, conv_b)


def _cumsum_rows(x):
    n = x.shape[0]
    row = lax.broadcasted_iota(jnp.int32, x.shape, 0)
    s = 1
    while s < n:
        x = x + jnp.where(row >= s, pltpu.roll(x, s, 0), 0.0)
        s *= 2
    return x


def _rcumsum_rows(x):
    n = x.shape[0]
    row = lax.broadcasted_iota(jnp.int32, x.shape, 0)
    s = 1
    while s < n:
        x = x + jnp.where(row < n - s, pltpu.roll(x, n - s, 0), 0.0)
        s *= 2
    return x


def _dot_nt(a, b):
    return lax.dot_general(a.astype(BF16), b.astype(BF16), (((1,), (1,)), ((), ())), preferred_element_type=F32)


def _dot_tn(a, b):
    return lax.dot_general(a.astype(BF16), b.astype(BF16), (((0,), (0,)), ((), ())), preferred_element_type=F32)


def _dot_nn(a, b):
    return jnp.dot(a.astype(BF16), b.astype(BF16), preferred_element_type=F32)


def _dot3(a, b, contract):
    def split(x):
        hi = x.astype(BF16)
        return hi, (x - hi.astype(F32)).astype(BF16)

    a_hi, a_lo = split(a)
    b_hi, b_lo = split(b)
    dot = lambda x, y: lax.dot_general(x, y, (contract, ((), ())), preferred_element_type=F32)
    return dot(a_hi, b_hi) + (dot(a_hi, b_lo) + dot(a_lo, b_hi))


NT, TN, NN = ((1,), (1,)), ((0,), (0,)), ((1,), (0,))


def _hg_gates(hq, hf, lbv):
    sig = _sigmoid(hf)
    f = lbv + (1.0 - lbv) * sig
    return sig, f, jnp.log(f), 1.0 - f, hq * (HG_DK ** -0.5)


def _hg_sel_rows(ref, sp):
    return jnp.concatenate(
        [jnp.broadcast_to(ref[pl.ds(HG_SUB * i + sp, 1), :], (HG_SUB, HG_DK)) for i in range(HG_CHUNK // HG_SUB)], axis=0)


def _hg_masks():
    C = HG_CHUNK
    row = lax.broadcasted_iota(jnp.int32, (C, C), 0)
    col = lax.broadcasted_iota(jnp.int32, (C, C), 1)
    d = col - (row // HG_SUB) * HG_SUB
    tmod = row % HG_SUB
    diag_valid = jnp.logical_and(d >= 0, d <= tmod)
    return row, col, d, diag_valid


def _hg_scores(q, k, b, b_sc, k_sc):
    C, S = HG_CHUNK, HG_SUB
    row, col, d, diag_valid = _hg_masks()
    blocks = [jnp.zeros((S, C), F32)]
    for i in range(1, C // S):
        r = b_sc[pl.ds(S * i - 1, 1), :]
        qi = q[S * i:S * (i + 1)] * jnp.exp(b[S * i:S * (i + 1)] - r)
        kk = k * jnp.exp(jnp.minimum(r - b, 0.0))
        blocks.append(_dot_nt(qi, kk))
    a_off = jnp.where(col < (row // S) * S, jnp.concatenate(blocks, axis=0), 0.0)
    a_d = jnp.zeros((C, C), F32)
    for sp in range(S):
        bs = _hg_sel_rows(b_sc, sp)
        ks = _hg_sel_rows(k_sc, sp)
        e = jnp.exp(jnp.minimum(b - bs, 0.0))
        colv = jnp.sum(q * ks * e, axis=-1, keepdims=True)
        a_d = jnp.where(d == sp, colv, a_d)
    return a_off + jnp.where(diag_valid, a_d, 0.0)


def _hg_prep(hq_v, hf_v, lbv, b_sc, k_sc):
    sig, f, g, k, q = _hg_gates(hq_v, hf_v, lbv)
    b = _cumsum_rows(g)
    b_sc[...] = b
    k_sc[...] = k
    return sig, f, k, q, b, b_sc[pl.ds(HG_CHUNK - 1, 1), :]


def _hgrn_fwd(hq, hf, hi, lb, *, name):
    T = hq.shape[0]
    C, H, K = HG_CHUNK, HG_HEADS, HG_DK
    NC = T // C

    def body(hq_ref, hf_ref, hi_ref, lb_ref, o_ref, st_ref, s_sc, b_sc, k_sc):
        @pl.when(pl.program_id(0) == 0)
        def _():
            s_sc[...] = jnp.zeros_like(s_sc)

        st_all = s_sc[...]
        st_ref[0] = st_all
        outs, news = [], []
        for h in range(H):
            sl = slice(K * h, K * (h + 1))
            _, _, k, q, b, bc = _hg_prep(hq_ref[:, sl], hf_ref[:, sl], lb_ref[:, sl], b_sc.at[h], k_sc.at[h])
            v = hi_ref[:, sl]
            st0 = st_all[:, sl]
            a = _hg_scores(q, k, b, b_sc.at[h], k_sc.at[h])
            outs.append(_dot_nn(a, v) + _dot_nt(q * jnp.exp(b), st0))
            news.append(st0 * jnp.exp(bc) + _dot_tn(v, k * jnp.exp(bc - b)))
        o_ref[...] = jnp.concatenate(outs, axis=1)
        s_sc[...] = jnp.concatenate(news, axis=1)

    blk = pl.BlockSpec((C, H * K), lambda c: (c, 0))
    return pl.pallas_call(
        body, name=name, grid=(NC,),
        in_specs=[blk, blk, blk, pl.BlockSpec((1, H * K), lambda c: (0, 0))],
        out_specs=[blk, pl.BlockSpec((1, K, H * K), lambda c: (c, 0, 0))],
        out_shape=[_sds((T, H * K), F32), _sds((NC, K, H * K), F32)],
        scratch_shapes=[pltpu.VMEM((K, H * K), F32), pltpu.VMEM((H, C, K), F32), pltpu.VMEM((H, C, K), F32)],
        compiler_params=_cp(("arbitrary",)),
    )(hq, hf, hi, lb)


def _hgrn_bwd(hq, hf, hi, lb, states, do, *, name):
    T = hq.shape[0]
    C, H, K, S = HG_CHUNK, HG_HEADS, HG_DK, HG_SUB
    NC = T // C

    def intra_slow(q, k, b, da, b_sc, k_sc):
        row, col, d, diag_valid = _hg_masks()
        a_blocks = [jnp.zeros((S, C), F32)]
        dq_blocks = [jnp.zeros((S, K), F32)]
        dk = jnp.zeros((C, K), F32)
        for i in range(1, C // S):
            r = b_sc[pl.ds(S * i - 1, 1), :]
            eq = jnp.exp(b[S * i:S * (i + 1)] - r)
            ek = jnp.exp(jnp.minimum(r - b, 0.0))
            qi = q[S * i:S * (i + 1)] * eq
            kk = k * ek
            a_blocks.append(_dot_nt(qi, kk))
            dai = jnp.where(col[S * i:S * (i + 1)] < S * i, da[S * i:S * (i + 1)], 0.0)
            dq_blocks.append(_dot_nn(dai, kk) * eq)
            dk = dk + _dot_tn(dai, qi) * ek
        dq = jnp.concatenate(dq_blocks, axis=0)
        a_off = jnp.where(col < (row // S) * S, jnp.concatenate(a_blocks, axis=0), 0.0)
        same_blk = (row // S == col // S).astype(BF16)
        tmod = (lax.broadcasted_iota(jnp.int32, (C, K), 0)) % S
        a_d = jnp.zeros((C, C), F32)
        for sp in range(S):
            bs = _hg_sel_rows(b_sc, sp)
            ks = _hg_sel_rows(k_sc, sp)
            e = jnp.where(tmod >= sp, jnp.exp(jnp.minimum(b - bs, 0.0)), 0.0)
            eks = e * ks
            a_d = jnp.where(d == sp, jnp.sum(q * eks, axis=-1, keepdims=True), a_d)
            dacol = jnp.sum(jnp.where(d == sp, da, 0.0), axis=-1, keepdims=True)
            dq = dq + dacol * eks
            wq = dacol * e * q
            wq_hi = wq.astype(BF16)
            wq_lo = (wq - wq_hi.astype(F32)).astype(BF16)
            blk_sum = (jnp.dot(same_blk, wq_hi, preferred_element_type=F32)
                       + jnp.dot(same_blk, wq_lo, preferred_element_type=F32))
            dk = dk + jnp.where(tmod == sp, blk_sum, 0.0)
        return a_off + jnp.where(diag_valid, a_d, 0.0), dq, dk

    def one_head(pre, v, lbv, st0, dst1, dout, b_sc, k_sc):
        sig, f, k, q, b, bc = pre
        ebc = jnp.exp(bc)
        eb = jnp.exp(b)
        ekb = jnp.exp(bc - b)
        qt = q * eb
        kb = k * ekb
        row = lax.broadcasted_iota(jnp.int32, (C, C), 0)
        col = lax.broadcasted_iota(jnp.int32, (C, C), 1)
        da = jnp.where(col <= row, _dot_nt(dout, v), 0.0)
        dkb = _dot_nn(v, dst1)
        new_ds = _dot_tn(dout, qt) + dst1 * ebc
        a, dq_i, dk_i = intra_slow(q, k, b, da, b_sc, k_sc)
        dq = _dot_nn(dout, st0) * eb + dq_i
        dk = dkb * ekb + dk_i
        dv = _dot_tn(a, dout) + _dot_nt(kb, dst1)
        extra = jnp.sum(dkb * kb, axis=0, keepdims=True) + ebc * jnp.sum(st0 * dst1, axis=0, keepdims=True)
        rowk = lax.broadcasted_iota(jnp.int32, (C, K), 0)
        db = q * dq - k * dk + jnp.where(rowk == C - 1, extra, 0.0)
        dg = _rcumsum_rows(db)
        df = dg / f - dk
        return (dq * (K ** -0.5), df * (1.0 - lbv) * sig * (1.0 - sig), dv,
                jnp.sum(df * (1.0 - sig), axis=0, keepdims=True), new_ds)

    def body(hq_ref, hf_ref, hi_ref, lb_ref, st_ref, do_ref, dq_ref, dhf_ref, dv_ref, dlb_ref, ds_sc, b_sc, k_sc):
        @pl.when(pl.program_id(0) == 0)
        def _():
            ds_sc[...] = jnp.zeros_like(ds_sc)
            dlb_ref[...] = jnp.zeros_like(dlb_ref)

        st_all = st_ref[0]
        ds_all = ds_sc[...]
        res = []
        for h in range(H):
            sl = slice(K * h, K * (h + 1))
            pre = _hg_prep(hq_ref[:, sl], hf_ref[:, sl], lb_ref[:, sl], b_sc.at[h], k_sc.at[h])
            res.append(one_head(pre, hi_ref[:, sl], lb_ref[:, sl], st_all[:, sl], ds_all[:, sl], do_ref[:, sl],
                                b_sc.at[h], k_sc.at[h]))
        cat = lambda j: jnp.concatenate([r[j] for r in res], axis=1)
        dq_ref[...] = cat(0).astype(dq_ref.dtype)
        dhf_ref[...] = cat(1).astype(dhf_ref.dtype)
        dv_ref[...] = cat(2).astype(dv_ref.dtype)
        dlb_ref[...] += cat(3)
        ds_sc[...] = cat(4)

    blk = pl.BlockSpec((C, H * K), lambda c: (NC - 1 - c, 0))
    par = pl.BlockSpec((1, H * K), lambda c: (0, 0))
    return pl.pallas_call(
        body, name=name, grid=(NC,),
        in_specs=[blk, blk, blk, par, pl.BlockSpec((1, K, H * K), lambda c: (NC - 1 - c, 0, 0)), blk],
        out_specs=[blk, blk, blk, par],
        out_shape=[_sds((T, H * K), BF16)] * 3 + [_sds((1, H * K), F32)],
        scratch_shapes=[pltpu.VMEM((K, H * K), F32), pltpu.VMEM((H, C, K), F32), pltpu.VMEM((H, C, K), F32)],
        compiler_params=_cp(("arbitrary",)),
    )(hq, hf, hi, lb, states, do)


def _att_valid(n):
    R, B = ATT_GROUP * ATT_BLOCK, ATT_BLOCK
    j = lax.broadcasted_iota(jnp.int32, (2 * B, R), 0)
    t = lax.broadcasted_iota(jnp.int32, (2 * B, R), 1) % B
    dist = t + B - j
    first_key = jnp.where(n > 0, 0, B)
    return jnp.logical_and(jnp.logical_and(dist >= 0, dist < B), j >= first_key)


def _att_load(cur_ref, prev_ref, ba_ref, kv):
    hd = ATT_HD
    def cols(ref, c0):
        return ref[:, c0:c0 + hd] + ba_ref[:, c0:c0 + hd]
    qs = jnp.concatenate([cols(cur_ref, hd * (ATT_GROUP * kv + g)) for g in range(ATT_GROUP)], axis=0)
    kc = jnp.concatenate([cols(prev_ref, ATT_Q_W + hd * kv), cols(cur_ref, ATT_Q_W + hd * kv)], axis=0)
    vc = jnp.concatenate([cols(prev_ref, ATT_Q_W + ATT_KV_W + hd * kv), cols(cur_ref, ATT_Q_W + ATT_KV_W + hd * kv)], axis=0)
    return qs, kc, vc


def _att_probs(qs, kc, valid, sink_ref, kv):
    scale = 1.0 / math.sqrt(ATT_HD)
    s = jnp.where(valid, _dot_nt(kc, qs) * scale, NEG)
    sink = jnp.concatenate([jnp.full((1, ATT_BLOCK), sink_ref[0, ATT_GROUP * kv + g], F32) for g in range(ATT_GROUP)], axis=1)
    m = jnp.maximum(jnp.max(s, axis=0, keepdims=True), sink)
    p = jnp.exp(s - m)
    ps = jnp.exp(sink - m)
    inv = 1.0 / (jnp.sum(p, axis=0, keepdims=True) + ps)
    return p * inv, ps * inv


def _attn_fwd(att, b_attn, sinks, *, name):
    T = att.shape[0]
    B = ATT_BLOCK
    NB = T // B

    def body(sink_ref, cur_ref, prev_ref, ba_ref, o_ref):
        valid = _att_valid(pl.program_id(0))
        for kv in range(ATT_KV):
            qs, kc, vc = _att_load(cur_ref, prev_ref, ba_ref, kv)
            prob, _ = _att_probs(qs, kc, valid, sink_ref, kv)
            o = _dot_tn(prob, vc)
            for g in range(ATT_GROUP):
                c0 = ATT_HD * (ATT_GROUP * kv + g)
                o_ref[:, c0:c0 + ATT_HD] = o[B * g:B * (g + 1)]

    return pl.pallas_call(
        body, name=name, grid=(NB,),
        in_specs=[pl.BlockSpec(memory_space=pltpu.SMEM),
                  pl.BlockSpec((B, ATT_COLS), lambda n: (n, 0)),
                  pl.BlockSpec((B, ATT_COLS), lambda n: (jnp.maximum(n - 1, 0), 0)),
                  pl.BlockSpec((1, ATT_COLS), lambda n: (0, 0))],
        out_specs=pl.BlockSpec((B, ATT_Q_W), lambda n: (n, 0)),
        out_shape=_sds((T, ATT_Q_W), F32),
        compiler_params=_cp(("parallel",)),
    )(sinks, att, att, b_attn)


def _attn_bwd(att, b_attn, sinks, dmix, *, name):
    T = att.shape[0]
    B, hd = ATT_BLOCK, ATT_HD
    NB = T // B
    scale = 1.0 / math.sqrt(hd)

    def body(sink_ref, cur_ref, prev_ref, ba_ref, do_ref, daq_ref, dakv_ref, dsink_ref, dbq_ref, dbkv_ref,
             carry_sc, cprev_sc, ccur_sc):
        n = pl.program_id(0)

        @pl.when(n == 0)
        def _():
            carry_sc[...] = jnp.zeros_like(carry_sc)
            dsink_ref[...] = jnp.zeros_like(dsink_ref)
            dbq_ref[...] = jnp.zeros_like(dbq_ref)
            dbkv_ref[...] = jnp.zeros_like(dbkv_ref)

        @pl.when(n < NB)
        def _():
            valid = _att_valid(n)
            hrow = lax.broadcasted_iota(jnp.int32, (SUBLANES, 128), 0)
            dsink = jnp.zeros((SUBLANES, 128), F32)
            for kv in range(ATT_KV):
                qs, kc, vc = _att_load(cur_ref, prev_ref, ba_ref, kv)
                prob, psink = _att_probs(qs, kc, valid, sink_ref, kv)
                dout = jnp.concatenate(
                    [do_ref[:, hd * (ATT_GROUP * kv + g):hd * (ATT_GROUP * kv + g + 1)] for g in range(ATT_GROUP)], axis=0)
                dp = _dot_nt(vc, dout)
                delta = jnp.sum(prob * dp, axis=0, keepdims=True)
                dsc = prob * (dp - delta) * scale
                dq = _dot_tn(dsc, kc)
                dk = _dot_nn(dsc, qs)
                dvv = _dot_nn(prob, dout)
                dsk = psink * delta
                for g in range(ATT_GROUP):
                    h = ATT_GROUP * kv + g
                    daq_ref[:, hd * h:hd * (h + 1)] = dq[B * g:B * (g + 1)].astype(daq_ref.dtype)
                    tot = jnp.sum(dsk[:, B * g:B * (g + 1)], axis=1, keepdims=True)
                    dsink = dsink - jnp.where(hrow == h, tot, 0.0)
                cprev_sc[:, hd * kv:hd * (kv + 1)] = dk[:B]
                ccur_sc[:, hd * kv:hd * (kv + 1)] = dk[B:]
                cprev_sc[:, ATT_KV_W + hd * kv:ATT_KV_W + hd * (kv + 1)] = dvv[:B]
                ccur_sc[:, ATT_KV_W + hd * kv:ATT_KV_W + hd * (kv + 1)] = dvv[B:]
            dsink_ref[...] += dsink
            dbq_ref[...] += jnp.sum(daq_ref[...].astype(F32), axis=0, keepdims=True)
            done = carry_sc[...] + cprev_sc[...]
            dakv_ref[...] = done.astype(dakv_ref.dtype)
            dbkv_ref[...] += jnp.sum(done.astype(dakv_ref.dtype).astype(F32), axis=0, keepdims=True)
            carry_sc[...] = ccur_sc[...]

        @pl.when(n == NB)
        def _():
            done = carry_sc[...]
            dakv_ref[...] = done.astype(dakv_ref.dtype)
            dbkv_ref[...] += jnp.sum(done.astype(dakv_ref.dtype).astype(F32), axis=0, keepdims=True)

    cl = lambda n: jnp.minimum(n, NB - 1)
    return pl.pallas_call(
        body, name=name, grid=(NB + 1,),
        in_specs=[pl.BlockSpec(memory_space=pltpu.SMEM),
                  pl.BlockSpec((B, ATT_COLS), lambda n: (cl(n), 0)),
                  pl.BlockSpec((B, ATT_COLS), lambda n: (jnp.maximum(cl(n) - 1, 0), 0)),
                  pl.BlockSpec((1, ATT_COLS), lambda n: (0, 0)),
                  pl.BlockSpec((B, ATT_Q_W), lambda n: (cl(n), 0))],
        out_specs=[pl.BlockSpec((B, ATT_Q_W), lambda n: (cl(n), 0)),
                   pl.BlockSpec((B, 2 * ATT_KV_W), lambda n: (jnp.maximum(n - 1, 0), 0)),
                   pl.BlockSpec((SUBLANES, 128), lambda n: (0, 0)),
                   pl.BlockSpec((1, ATT_Q_W), lambda n: (0, 0)),
                   pl.BlockSpec((1, 2 * ATT_KV_W), lambda n: (0, 0))],
        out_shape=[_sds((T, ATT_Q_W), BF16), _sds((T, 2 * ATT_KV_W), BF16), _sds((SUBLANES, 128), F32),
                   _sds((1, ATT_Q_W), F32), _sds((1, 2 * ATT_KV_W), F32)],
        scratch_shapes=[pltpu.VMEM((B, 2 * ATT_KV_W), F32)] * 3,
        compiler_params=_cp(("arbitrary",)),
    )(sinks, att, att, b_attn, dmix)


def _silu_and_grad(x):
    sg = _sigmoid(x)
    return x * sg, sg * (1.0 + x * (1.0 - sg))


def _mix_fwd_fn(o_raw, hg, o_att, hgw):
    outs = []
    for h in range(HG_HEADS):
        sl = slice(HG_DK * h, HG_DK * (h + 1))
        silu, _ = _silu_and_grad(hg[:, sl])
        outs.append(_rms_fwd(o_raw[:, sl], hgw) * silu)
    outs.append(o_att)
    return (jnp.concatenate(outs, axis=1),)


def _mix_bwd_fn(o_raw, hg, dmix, hgw):
    dos, dhgs = [], []
    dw = jnp.zeros((1, HG_DK), F32)
    for h in range(HG_HEADS):
        sl = slice(HG_DK * h, HG_DK * (h + 1))
        silu, dsilu = _silu_and_grad(hg[:, sl])
        dy = dmix[:, sl]
        dhgs.append(dy * _rms_fwd(o_raw[:, sl], hgw) * dsilu)
        dx, dwh = _rms_bwd(o_raw[:, sl], hgw, dy * silu)
        dos.append(dx)
        dw = dw + dwh
    return jnp.concatenate(dos, axis=1), jnp.concatenate(dhgs, axis=1), dw


def _final_fn(h2, tgt, wf):
    d = h2.shape[1]
    err = _rms_fwd(h2, wf) - tgt
    loss_cols = (0.5 / d) * jnp.sum(err * err, axis=0, keepdims=True)
    dh2, dwf = _rms_bwd(h2, wf, err * (1.0 / d))
    return dh2, dh2, loss_cols, dwf


class _NoExchange:
    def __init__(self, weights):
        self.weights = weights

    def start(self):
        return None

    def w_in(self, after):
        return self.weights["w_in_t"]

    def w_out(self, after):
        return self.weights

    def ffn(self, after):
        return self.weights

    def ffn_grads(self, gs):
        return None

    def ffn_grads_send(self, after):
        return None


def _local_step(x, tgt, p, ex):
    T, D = x.shape
    row = lambda n, dt: _sds((T, n), dt)
    acc = lambda n: _sds((1, n), F32)

    (u,) = _rowwise(lambda xv, w: (_rms_fwd(xv, w),), [_full(x)], [p["norm_mix_w"]], [row(D, BF16)], [], name="rms_mix",
                    after=ex.start())
    p = dict(p, w_in_t=ex.w_in(u))
    hq, hf, hi, hg, att = _mm_nt(u, p["w_in_t"], splits=[HG_W] * 4 + [ATT_COLS], out_dtype=F32, name="in_proj")
    o_raw, states = _hgrn_fwd(hq, hf, hi, p["lb"], name="hgrn_fwd")
    o_att = _attn_fwd(att, p["b_attn"], p["sinks"], name="attn_fwd")
    p = dict(p, **ex.w_out(o_att))
    def out_epilogue(prod, xv, w):
        h1v = prod + xv
        return h1v, _rms_fwd(h1v, w)

    h1, v, mix = _mm_nn(None, [p["w_out"]], name="mix_out_proj",
                        prologue=(lambda *a: _mix_fwd_fn(*a)[0], [o_raw, hg, o_att], [p["hg_norm_w"]], row(D, BF16)),
                        epilogue=(out_epilogue, [x], [p["norm_ffn_w"]], [row(D, F32), row(D, BF16)], []))
    p = dict(p, **ex.ffn(v))
    (gp,) = _mm_nt(v, p["w_gate_t"], splits=[D_FF], out_dtype=F32, name="gate_proj")
    (up,) = _mm_nt(v, p["w_up_t"], splits=[D_FF], out_dtype=F32, name="up_proj")
    act = _convact_fwd(gp, up, p["conv_w8"], p["conv_b"], name="convact_fwd")
    def down_epilogue(prod, h1v, tgtv, wf):
        return _final_fn(prod + h1v, tgtv, wf)

    dh2, dh2_b, loss_cols, d_final = _mm_nn(
        [[act]], [p["w_down"]], name="down_proj_loss",
        epilogue=(down_epilogue, [h1, tgt], [p["final_norm_w"]], [row(D, F32), row(D, BF16)], [acc(D), acc(D)]))

    (dact,) = _mm_nt(dh2_b, p["w_down"], splits=[D_FF], out_dtype=F32, name="d_act")
    g_down = _mm_tn([act], dh2_b, name="g_down")
    dgp, dup, d_conv_w8, d_conv_b = _convact_bwd(gp, up, dact, p["conv_w8"], p["conv_b"], name="convact_bwd")
    g_gate_t = _mm_tn([dgp], v, name="g_gate")
    g_up_t = _mm_tn([dup], v, name="g_up")
    swapping = ex.ffn_grads([g_gate_t, g_up_t, g_down])

    def ffn_norm_bwd(dvv, hv, dh2v, w):
        dx, dw = _rms_bwd(hv, w, dvv)
        dh1v = dx + dh2v
        return dh1v, dh1v, dw

    dh1, dh1_b, d_norm_ffn = _mm_nn(
        [[dgp], [dup]], [p["w_gate_t"], p["w_up_t"]], name="d_v_norm", after=swapping,
        epilogue=(ffn_norm_bwd, [h1, dh2], [p["norm_ffn_w"]], [row(D, F32), row(D, BF16)], [acc(D)]))
    sent = ex.ffn_grads_send(dh1_b)
    def mix_bwd(dmixv, o_rawv, hgv, hgw):
        do_rawv, dhgv, dw = _mix_bwd_fn(o_rawv, hgv, dmixv[:, :HG_W], hgw)
        return do_rawv, dhgv, dmixv[:, HG_W:], dw

    do_raw, dhg, do_att, d_hg_norm = _mm_nn(
        [[dh1_b]], [p["w_out"]], name="d_mix_bwd", w_transposed=True, after=sent,
        epilogue=(mix_bwd, [o_raw, hg], [p["hg_norm_w"]], [row(HG_W, F32), row(HG_W, BF16), row(ATT_Q_W, F32)], [acc(HG_DK)]))
    g_out = _mm_tn([mix], dh1_b, name="g_out")
    daq, dakv, d_sinks8, d_bq, d_bkv = _attn_bwd(att, p["b_attn"], p["sinks"], do_att, name="attn_bwd")
    dhq, dhf, dhi, d_lb = _hgrn_bwd(hq, hf, hi, p["lb"], states, do_raw, name="hgrn_bwd")
    pieces = [dhq, dhf, dhi, dhg, daq, dakv]
    g_in_t = _mm_tn(pieces, u, name="g_in")

    def mix_norm_bwd(duv, xv, dh1v, w):
        dx, dw = _rms_bwd(xv, w, duv)
        return dx + dh1v, dw

    dx, d_norm_mix = _mm_nn([pieces], [p["w_in_t"]], name="d_u_norm",
                            epilogue=(mix_norm_bwd, [x, dh1], [p["norm_mix_w"]], [row(D, F32)], [acc(D)]))
    grads = dict(g_in_t=g_in_t, g_out=g_out, g_gate_t=g_gate_t, g_up_t=g_up_t, g_down=g_down,
                 norm_mix_w=d_norm_mix, b_attn=jnp.concatenate([d_bq, d_bkv], axis=1), lb=d_lb, hg_norm_w=d_hg_norm,
                 sinks8=d_sinks8, norm_ffn_w=d_norm_ffn, conv_w8=d_conv_w8, conv_b=d_conv_b, final_norm_w=d_final)
    return loss_cols, dx, grads


SLAB = (IN_COLS // N_CHIPS, D_FF // N_CHIPS, D_FF // N_CHIPS, D_FF // N_CHIPS, D_MODEL // N_CHIPS)
N_W = len(SLAB)
PACK_OFF = tuple(sum(SLAB[:i]) for i in range(N_W))
PACK_ROWS = sum(SLAB)
FULL_OFF = tuple(N_CHIPS * o for o in PACK_OFF)
FULL_ROWS = N_CHIPS * PACK_ROWS
HALF = tuple(s // 2 for s in SLAB)
HPACK_OFF = tuple(sum(HALF[:i]) for i in range(N_W))
HPACK_ROWS = sum(HALF)
HFULL_OFF = tuple(N_CHIPS * o for o in HPACK_OFF)
HFULL_ROWS = N_CHIPS * HPACK_ROWS
CHIP_FLIPS = ((1, 0), (0, 1), (1, 1))
N_DEV = 8
BF16_ROWS = 16
ANY = pl.BlockSpec(memory_space=pl.ANY)


def _pos():
    return lax.axis_index("x"), lax.axis_index("y"), lax.axis_index("c")


def _flip(v, f):
    return 1 - v if f else v


def _rcopy(src, dst, ssem, rsem, dev):
    return pltpu.make_async_remote_copy(src_ref=src, dst_ref=dst, send_sem=ssem, recv_sem=rsem, device_id=dev,
                                        device_id_type=pl.DeviceIdType.MESH)


def _rows(ref, start, n, align=None):
    if not isinstance(start, int):
        if align is None:
            align = SUBLANES * (4 // jnp.dtype(ref.dtype).itemsize)
        start = pl.multiple_of(start, align)
    return ref.at[pl.ds(start, n), :]


FFN_W = (1, 2, 3)
N_PEER = 1 + len(CHIP_FLIPS)
HBM = pl.BlockSpec(memory_space=pltpu.HBM)
SEM = pl.BlockSpec(memory_space=pltpu.SEMAPHORE)
EFFECT = pltpu.SideEffectType.DATAFLOW_SIDE_EFFECTING
LANES = 128


def _gather_start(pack, cw8):
    D = pack.shape[1]
    lands = [lax.empty((N_CHIPS * SLAB[0], D), pack.dtype), lax.empty((3 * N_CHIPS * SLAB[1], D), pack.dtype),
             lax.empty((N_CHIPS * SLAB[4], D), pack.dtype), lax.empty((N_CHIPS,) + cw8.shape, cw8.dtype)]
    bufs = [pack, cw8] + lands

    def body(pack_ref, cw_ref, l_in, l_ffn, l_out, l_cw, *rest):
        in_send, in_recv, out_send, out_recv, ffn_send, ffn_recv = rest[:6]
        token = rest[-1]
        x, y, c = _pos()
        q = 2 * x + y
        peers = _gather_peers(x, y, c)
        for k, peer in enumerate(peers):
            _rcopy(_rows(pack_ref, PACK_OFF[0], SLAB[0]), _rows(l_in, q * SLAB[0], SLAB[0], BF16_ROWS),
                   in_send.at[k], in_recv.at[k], peer).start()
        for k, peer in enumerate(peers):
            _rcopy(_rows(pack_ref, PACK_OFF[4], SLAB[4]), _rows(l_out, q * SLAB[4], SLAB[4], BF16_ROWS),
                   out_send.at[k], out_recv.at[k], peer).start()
            _rcopy(cw_ref, l_cw.at[q], out_send.at[N_PEER + k], out_recv.at[N_PEER + k], peer).start()
        for j, w in enumerate(FFN_W):
            for k, peer in enumerate(peers):
                _rcopy(_rows(pack_ref, PACK_OFF[w], SLAB[w]), _rows(l_ffn, (j * N_CHIPS + q) * SLAB[w], SLAB[w], BF16_ROWS),
                       ffn_send.at[k], ffn_recv.at[k], peer).start()
        token[...] = jnp.zeros_like(token)

    n_sem = (N_PEER, N_PEER, 2 * N_PEER, 2 * N_PEER, N_PEER, N_PEER)
    outs = pl.pallas_call(
        body, name="gather_start", in_specs=[HBM] * len(bufs),
        out_specs=[SEM] * len(n_sem) + [HBM] * len(bufs) + [pl.BlockSpec(memory_space=pltpu.VMEM)],
        out_shape=[pltpu.SemaphoreType.DMA((n,)) for n in n_sem]
        + [pltpu.HBM(b.shape, b.dtype) for b in bufs] + [_sds((SUBLANES, LANES), F32)],
        input_output_aliases={i: len(n_sem) + i for i in range(len(bufs))},
        compiler_params=pltpu.CompilerParams(has_side_effects=EFFECT),
    )(*[pltpu.with_memory_space_constraint(b, pltpu.HBM) for b in bufs])
    bufs_out = outs[len(n_sem):]
    return dict(in_sems=outs[0:2], out_sems=outs[2:4], ffn_sems=outs[4:6], pack=bufs_out[0], cw=bufs_out[1], l_in=bufs_out[2],
                l_ffn=bufs_out[3], l_out=bufs_out[4], l_cw=bufs_out[5], token=bufs_out[6])


def _gather_peers(x, y, c):
    return [(x, y, 1 - c)] + [(_flip(x, fx), _flip(y, fy), c) for fx, fy in CHIP_FLIPS]


def _gather_wait_in(g, after):
    def body(pack_ref, l_in, send, recv, after_ref, pack_out, l_out):
        for k, peer in enumerate(_gather_peers(*_pos())):
            cp = _rcopy(_rows(pack_ref, PACK_OFF[0], SLAB[0]), _rows(l_in, 0, SLAB[0]), send.at[k], recv.at[k], peer)
            cp.wait_send()
            cp.wait_recv()

    return pl.pallas_call(
        body, name="gather_wait_in", in_specs=[HBM, HBM, SEM, SEM, ANY], out_specs=[HBM, HBM],
        out_shape=[pltpu.HBM(g["pack"].shape, g["pack"].dtype), pltpu.HBM(g["l_in"].shape, g["l_in"].dtype)],
        input_output_aliases={0: 0, 1: 1}, compiler_params=pltpu.CompilerParams(has_side_effects=EFFECT),
    )(g["pack"], g["l_in"], *g["in_sems"], after)


def _gather_wait_out(g, pack, after):
    def body(pack_ref, cw_ref, l_out, l_cw, send, recv, after_ref, o_pack, o_out, o_cw):
        for k, peer in enumerate(_gather_peers(*_pos())):
            for cp in (_rcopy(_rows(pack_ref, PACK_OFF[4], SLAB[4]), _rows(l_out, 0, SLAB[4]), send.at[k], recv.at[k], peer),
                       _rcopy(cw_ref, l_cw.at[0], send.at[N_PEER + k], recv.at[N_PEER + k], peer)):
                cp.wait_send()
                cp.wait_recv()

    ins = [pack, g["cw"], g["l_out"], g["l_cw"]]
    return pl.pallas_call(
        body, name="gather_wait_out", in_specs=[HBM] * 4 + [SEM, SEM, ANY], out_specs=[HBM] * 3,
        out_shape=[pltpu.HBM(b.shape, b.dtype) for b in (ins[0], ins[2], ins[3])],
        input_output_aliases={0: 0, 2: 1, 3: 2}, compiler_params=pltpu.CompilerParams(has_side_effects=EFFECT),
    )(*ins, *g["out_sems"], after)


def _gather_wait_ffn(g, pack, after):
    n_ffn = len(FFN_W) * SLAB[FFN_W[0]]

    def body(pack_ref, l_ffn, send, recv, after_ref, o_ffn):
        for k, peer in enumerate(_gather_peers(*_pos())):
            cp = _rcopy(_rows(pack_ref, PACK_OFF[FFN_W[0]], n_ffn), _rows(l_ffn, 0, n_ffn), send.at[k], recv.at[k], peer)
            cp.wait_send()
            cp.wait_recv()

    return pl.pallas_call(
        body, name="gather_wait_ffn", in_specs=[HBM] * 2 + [SEM, SEM, ANY], out_specs=HBM,
        out_shape=pltpu.HBM(g["l_ffn"].shape, g["l_ffn"].dtype),
        input_output_aliases={1: 0}, compiler_params=pltpu.CompilerParams(has_side_effects=EFFECT),
    )(pack, g["l_ffn"], *g["ffn_sems"], after)


def _exchange_halves(ws, gs, small, *, name):
    D = gs[0].shape[1]
    n = len(ws)
    has_small = small is not None

    def body(*refs):
        g = refs[:n]
        t = refs[n + has_small:2 * n + has_small]
        sems = refs[2 * n + 2 * has_small:]
        d2d_send, d2d_recv = sems[0], sems[1]
        x, y, c = _pos()
        sib = (x, y, 1 - c)
        drains = []
        for i, w in enumerate(ws):
            h = HALF[w]
            for qq in range(N_CHIPS):
                _rcopy(_rows(g[i], qq * SLAB[w] + (1 - c) * h, h), _rows(t[i], qq * h, h),
                       d2d_send.at[i], d2d_recv.at[i], sib).start()
            drains.append(_rcopy(t[i], t[i], d2d_send.at[i], d2d_recv.at[i], sib))
        if has_small:
            small_ref, sall_ref = refs[n], refs[2 * n + 1]
            sm_send, sm_recv, loc_sem = sems[2], sems[3], sems[4]
            me = 4 * x + 2 * y + c
            own_small = pltpu.make_async_copy(small_ref, sall_ref.at[me], loc_sem)
            own_small.start()
            for f in range(1, N_DEV):
                peer = (_flip(x, f & 4), _flip(y, f & 2), _flip(c, f & 1))
                cp = _rcopy(small_ref, sall_ref.at[me], sm_send.at[f - 1], sm_recv.at[f - 1], peer)
                cp.start()
                drains.append(cp)
        for d in drains:
            d.wait_recv()
        for d in drains:
            d.wait_send()
        if has_small:
            own_small.wait()

    out_shape = [_sds((N_CHIPS * HALF[w], D), F32) for w in ws]
    scratch = [pltpu.SemaphoreType.DMA((n,)), pltpu.SemaphoreType.DMA((n,))]
    if has_small:
        out_shape.append(_sds((N_DEV,) + small.shape, F32))
        scratch += [pltpu.SemaphoreType.DMA((N_DEV - 1,)), pltpu.SemaphoreType.DMA((N_DEV - 1,)), pltpu.SemaphoreType.DMA]
    return pl.pallas_call(
        body, name=name, in_specs=[ANY] * (n + has_small), out_specs=[ANY] * (n + has_small),
        out_shape=out_shape, scratch_shapes=scratch,
    )(*gs, *([small] if has_small else []))


def _halves_copies(ws, g, t, send_sems, recv_sems):
    x, y, c = _pos()
    sib = (x, y, 1 - c)
    cps = []
    for i, w in enumerate(ws):
        h = HALF[w]
        for qq in range(N_CHIPS):
            cps.append(_rcopy(_rows(g[i], qq * SLAB[w] + (1 - c) * h, h), _rows(t[i], qq * h, h),
                              send_sems.at[N_CHIPS * i + qq], recv_sems.at[N_CHIPS * i + qq], sib))
    return cps


def _halves_start(ws, gs, *, name):
    D = gs[0].shape[1]
    n = len(ws)
    bufs = list(gs) + [lax.empty((N_CHIPS * HALF[w], D), F32) for w in ws]

    def body(*refs):
        for cp in _halves_copies(ws, refs[:n], refs[n:2 * n], refs[2 * n], refs[2 * n + 1]):
            cp.start()
        refs[-1][...] = jnp.zeros_like(refs[-1])

    outs = pl.pallas_call(
        body, name=name, in_specs=[HBM] * (2 * n),
        out_specs=[SEM, SEM] + [HBM] * (2 * n) + [pl.BlockSpec(memory_space=pltpu.VMEM)],
        out_shape=[pltpu.SemaphoreType.DMA((N_CHIPS * n,)), pltpu.SemaphoreType.DMA((N_CHIPS * n,))]
        + [pltpu.HBM(b.shape, b.dtype) for b in bufs] + [_sds((SUBLANES, LANES), F32)],
        input_output_aliases={i: 2 + i for i in range(2 * n)},
        compiler_params=pltpu.CompilerParams(has_side_effects=EFFECT),
    )(*[pltpu.with_memory_space_constraint(b, pltpu.HBM) for b in bufs])
    return dict(sems=outs[0:2], gs=outs[2:2 + n], theirs=outs[2 + n:2 + 2 * n], token=outs[-1])


def _halves_wait(ws, s, after, *, name):
    n = len(ws)

    def body(*refs):
        for cp in _halves_copies(ws, refs[:n], refs[n:2 * n], refs[2 * n], refs[2 * n + 1]):
            cp.wait_send()
            cp.wait_recv()

    bufs = list(s["gs"]) + list(s["theirs"])
    outs = pl.pallas_call(
        body, name=name, in_specs=[HBM] * (2 * n) + [SEM, SEM, ANY], out_specs=[HBM] * (2 * n),
        out_shape=[pltpu.HBM(b.shape, b.dtype) for b in bufs],
        input_output_aliases={i: i for i in range(2 * n)},
        compiler_params=pltpu.CompilerParams(has_side_effects=EFFECT),
    )(*bufs, *s["sems"], after)
    return outs[:n], outs[n:]


REDUCE_SPLIT = 2


def _chip_partial(ws, gs, theirs, *, name, out_dtype=F32):
    D = gs[0].shape[1]
    n = len(ws)

    def body(*refs):
        for i in range(n):
            refs[2 * n + i][...] = (refs[i][...] + refs[n + i][...]).astype(out_dtype)

    blk = [HALF[w] // REDUCE_SPLIT for w in ws]
    mine = [pl.BlockSpec((b, D), lambda qq, j: ((2 * qq + lax.axis_index("c")) * REDUCE_SPLIT + j, 0)) for b in blk]
    flat = [pl.BlockSpec((b, D), lambda qq, j: (qq * REDUCE_SPLIT + j, 0)) for b in blk]
    return pl.pallas_call(
        body, name=name, grid=(N_CHIPS, REDUCE_SPLIT), in_specs=mine + flat, out_specs=flat,
        out_shape=[_sds((N_CHIPS * HALF[w], D), out_dtype) for w in ws],
        compiler_params=_cp(("parallel", "parallel")),
    )(*gs, *theirs)


def _partial_copies(ws, part, got, send_sems, recv_sems):
    x, y, c = _pos()
    cps = []
    for k, (fx, fy) in enumerate(CHIP_FLIPS):
        peer = (_flip(x, fx), _flip(y, fy), c)
        qp = 2 * _flip(x, fx) + _flip(y, fy)
        for i, w in enumerate(ws):
            cps.append(_rcopy(_rows(part[i], qp * HALF[w], HALF[w]), _rows(got[i], k * HALF[w], HALF[w]),
                              send_sems.at[len(ws) * k + i], recv_sems.at[len(ws) * k + i], peer))
    return cps


def _send_chip_partials(ws, parts, *, name):
    D = parts[0].shape[1]
    n = len(ws)

    def body(*refs):
        cps = _partial_copies(ws, refs[:n], refs[n:2 * n], refs[2 * n], refs[2 * n + 1])
        for cp in cps:
            cp.start()
        for cp in cps:
            cp.wait_recv()
        for cp in cps:
            cp.wait_send()

    return pl.pallas_call(
        body, name=name, in_specs=[ANY] * n, out_specs=[ANY] * n,
        out_shape=[_sds((len(CHIP_FLIPS) * HALF[w], D), parts[0].dtype) for w in ws],
        scratch_shapes=[pltpu.SemaphoreType.DMA((len(CHIP_FLIPS) * n,)), pltpu.SemaphoreType.DMA((len(CHIP_FLIPS) * n,))],
    )(*parts)


def _send_start(ws, parts, *, name):
    D = parts[0].shape[1]
    n = len(ws)
    bufs = list(parts) + [lax.empty((len(CHIP_FLIPS) * HALF[w], D), parts[0].dtype) for w in ws]

    def body(*refs):
        send_sems, recv_sems = refs[2 * n], refs[2 * n + 1]
        for cp in _partial_copies(ws, refs[:n], refs[n:2 * n], send_sems, recv_sems):
            cp.start()
        refs[-1][...] = jnp.zeros_like(refs[-1])

    outs = pl.pallas_call(
        body, name=name, in_specs=[HBM] * (2 * n),
        out_specs=[SEM, SEM] + [HBM] * (2 * n) + [pl.BlockSpec(memory_space=pltpu.VMEM)],
        out_shape=[pltpu.SemaphoreType.DMA((len(CHIP_FLIPS) * n,)), pltpu.SemaphoreType.DMA((len(CHIP_FLIPS) * n,))]
        + [pltpu.HBM(b.shape, b.dtype) for b in bufs] + [_sds((SUBLANES, LANES), F32)],
        input_output_aliases={i: 2 + i for i in range(2 * n)},
        compiler_params=pltpu.CompilerParams(has_side_effects=EFFECT),
    )(*[pltpu.with_memory_space_constraint(b, pltpu.HBM) for b in bufs])
    return dict(sems=outs[0:2], parts=outs[2:2 + n], got=outs[2 + n:2 + 2 * n], token=outs[-1])


def _send_wait(ws, s, after, *, name):
    n = len(ws)

    def body(*refs):
        for cp in _partial_copies(ws, refs[:n], refs[n:2 * n], refs[2 * n], refs[2 * n + 1]):
            cp.wait_send()
            cp.wait_recv()

    bufs = list(s["parts"]) + list(s["got"])
    outs = pl.pallas_call(
        body, name=name, in_specs=[HBM] * (2 * n) + [SEM, SEM, ANY], out_specs=[HBM] * (2 * n),
        out_shape=[pltpu.HBM(b.shape, b.dtype) for b in bufs],
        input_output_aliases={i: i for i in range(2 * n)},
        compiler_params=pltpu.CompilerParams(has_side_effects=EFFECT),
    )(*bufs, *s["sems"], after)
    return outs[:n], outs[n:]


def _chip_reduce(ws, parts, got, *, name, after=None):
    D = parts[0].shape[1]
    nk = len(CHIP_FLIPS)
    n = len(ws)
    extra = [] if after is None else [after]

    def body(*refs):
        refs = refs[len(extra):]
        outs = refs[(1 + nk) * n:]
        for i in range(n):
            acc = refs[i][...].astype(F32)
            for k in range(nk):
                acc = acc + refs[n * (1 + k) + i][...].astype(F32)
            outs[i][...] = acc

    blk = [HALF[w] // REDUCE_SPLIT for w in ws]

    def q_idx(j):
        return (2 * lax.axis_index("x") + lax.axis_index("y")) * REDUCE_SPLIT + j

    in_specs = [pl.BlockSpec((b, D), lambda j: (q_idx(j), 0)) for b in blk]
    for k in range(nk):
        in_specs += [pl.BlockSpec((b, D), functools.partial(lambda j, k: (k * REDUCE_SPLIT + j, 0), k=k)) for b in blk]
    out_specs = [pl.BlockSpec((b, D), lambda j: (lax.axis_index("c") * REDUCE_SPLIT + j, 0)) for b in blk]
    return pl.pallas_call(
        body, name=name, grid=(REDUCE_SPLIT,), in_specs=[ANY] * len(extra) + in_specs, out_specs=out_specs,
        out_shape=[_sds((SLAB[w], D), F32) for w in ws],
        compiler_params=_cp(("parallel",)),
    )(*extra, *parts, *[g for _ in range(nk) for g in got])


def _exchange_reduced(ws, shards, *, name):
    n = len(ws)

    def body(*refs):
        ins, outs = refs[:n], refs[n:2 * n]
        send_sems, recv_sems = refs[2 * n], refs[2 * n + 1]
        x, y, c = _pos()
        sib = (x, y, 1 - c)
        cps = []
        for i, w in enumerate(ws):
            cp = _rcopy(_rows(ins[i], c * HALF[w], HALF[w]), _rows(outs[i], c * HALF[w], HALF[w]),
                        send_sems.at[i], recv_sems.at[i], sib)
            cp.start()
            cps.append(cp)
        for cp in cps:
            cp.wait_recv()
        for cp in cps:
            cp.wait_send()

    return pl.pallas_call(
        body, name=name, in_specs=[ANY] * n, out_specs=[ANY] * n,
        out_shape=[_sds(s.shape, s.dtype) for s in shards], input_output_aliases={i: i for i in range(n)},
        scratch_shapes=[pltpu.SemaphoreType.DMA((n,)), pltpu.SemaphoreType.DMA((n,))],
    )(*shards)


def _adamw_fn(w, g, m, v):
    m2 = ADAM_B1 * m + (1.0 - ADAM_B1) * g
    v2 = ADAM_B2 * v + (1.0 - ADAM_B2) * (g * g)
    m_hat = m2 / (1.0 - ADAM_B1 ** ADAM_STEP)
    v_hat = v2 / (1.0 - ADAM_B2 ** ADAM_STEP)
    return -ADAM_LR * (m_hat / (jnp.sqrt(v_hat) + ADAM_EPS) + ADAM_WD * w), m2, v2


def _adamw(w, g, m, v, *, name):
    shp = _sds(w.shape, F32)
    rows = w.shape[0]
    tm = max(t for t in range(SUBLANES, 512 + 1, SUBLANES) if rows % t == 0)
    return _rowwise(_adamw_fn, [_full(w), _full(g), _full(m), _full(v)], [], [shp] * 3, [], name=name, tm=tm)


SMALL_SEGS = (("loss", 8), ("norm_mix_w", 8), ("b_attn", 8), ("lb_logits", 8), ("hg_norm_w", 8), ("sinks", 8),
              ("norm_ffn_w", 8), ("conv_w", 72), ("conv_b", 24), ("final_norm_w", 8))
SMALL_OFF = {n: sum(r for _, r in SMALL_SEGS[:i]) for i, (n, _) in enumerate(SMALL_SEGS)}
SMALL_ROWS = sum(r for _, r in SMALL_SEGS)
LANES = 128


def _pack_small(parts):
    segs = []
    for n, r in SMALL_SEGS:
        a = parts.get(n)
        flat = jnp.zeros((0,), F32) if a is None else a.reshape(-1).astype(F32)
        segs.append(jnp.pad(flat, (0, r * LANES - flat.shape[0])).reshape(r, LANES))
    return jnp.concatenate(segs, axis=0)


def _unpack_small(pack, n, shape):
    size = math.prod(shape)
    r0 = SMALL_OFF[n]
    return pack[r0:r0 + dict(SMALL_SEGS)[n]].reshape(-1)[:size].reshape(shape)


def _small_update(sall, wp, mp, vp):
    R = SMALL_ROWS
    r_lb = SMALL_OFF["lb_logits"]

    def body(s_ref, w_ref, m_ref, v_ref, g_ref, d_ref, m2_ref, v2_ref, loss_ref):
        g = s_ref[0]
        for i in range(1, N_DEV):
            g = g + s_ref[i]
        tot = jnp.sum(jnp.sum(g[0:8], axis=1, keepdims=True), axis=0, keepdims=True)
        loss_ref[...] = jnp.broadcast_to(tot, loss_ref.shape)
        lg = w_ref[r_lb:r_lb + 8, :]
        p0 = _sigmoid(lg - pltpu.roll(lg, 4, 0))
        d = g[r_lb:r_lb + 8]
        d = d + pltpu.roll(d, 4, 0)
        sign = jnp.where(lax.broadcasted_iota(jnp.int32, d.shape, 0) < 4, 1.0, -1.0)
        g = jnp.concatenate([g[:r_lb], sign * d * p0 * (1.0 - p0), g[r_lb + 8:]], axis=0)
        g_ref[...] = g
        d_ref[...], m2_ref[...], v2_ref[...] = _adamw_fn(w_ref[...], g, m_ref[...], v_ref[...])

    full = pl.BlockSpec((R, LANES), lambda: (0, 0))
    return pl.pallas_call(
        body, name="small_update",
        in_specs=[pl.BlockSpec((N_DEV, R, LANES), lambda: (0, 0, 0)), full, full, full],
        out_specs=[full, full, full, full, pl.BlockSpec((8, LANES), lambda: (0, 0))],
        out_shape=[_sds((R, LANES), F32)] * 4 + [_sds((8, LANES), F32)],
        compiler_params=_cp(),
    )(sall, wp, mp, vp)


def _lb_fwd(lb_logits):
    n = lb_logits.shape[1]

    def body(l_ref, o_ref):
        o_ref[...] = _sigmoid(l_ref[0:1, :] - l_ref[1:2, :])

    return pl.pallas_call(body, name="lb_fwd", out_shape=_sds((1, n), F32), compiler_params=_cp())(lb_logits)


class _MeshExchange:
    def __init__(self, pack, cw8):
        self.gather = _gather_start(pack, cw8)
        self.sent = None
        self.conv_w8 = None

    def start(self):
        return self.gather["token"]

    def w_in(self, after):
        self.pack, l_in = _gather_wait_in(self.gather, after)
        return (l_in, N_CHIPS * SLAB[0], 0)

    def w_out(self, after):
        self.pack, l_out, l_cw = _gather_wait_out(self.gather, self.pack, after)
        self.conv_w8 = jnp.concatenate([l_cw[i] for i in range(N_CHIPS)], axis=1)
        return dict(w_out=(l_out, N_CHIPS * SLAB[4], 0), conv_w8=self.conv_w8)

    def ffn(self, after):
        l_ffn = _gather_wait_ffn(self.gather, self.pack, after)
        rows = N_CHIPS * SLAB[FFN_W[0]]
        return dict(w_gate_t=(l_ffn, rows, 0), w_up_t=(l_ffn, rows, 1), w_down=(l_ffn, rows, 2))

    def ffn_grads(self, gs):
        self.swap = _halves_start(FFN_W, gs, name="halves_ffn_start")
        return self.swap["token"]

    def ffn_grads_send(self, after):
        gs, theirs = _halves_wait(FFN_W, self.swap, after, name="halves_ffn_wait")
        parts = _chip_partial(FFN_W, gs, theirs, name="chip_partial_ffn")
        self.sent = _send_start(FFN_W, parts, name="send_ffn_start")
        return self.sent["token"]


def kernel(x, norm_mix_w, w_in, b_attn, lb_logits, hg_norm_w, sinks, w_out, norm_ffn_w, w_gate, w_up, conv_w, conv_b, w_down, final_norm_w, loss_target, m_norm_mix_w, m_w_in, m_b_attn, m_lb_logits, m_hg_norm_w, m_sinks, m_w_out, m_norm_ffn_w, m_w_gate, m_w_up, m_conv_w, m_conv_b, m_w_down, m_final_norm_w, v_norm_mix_w, v_w_in, v_b_attn, v_lb_logits, v_hg_norm_w, v_sinks, v_w_out, v_norm_ffn_w, v_w_gate, v_w_up, v_conv_w, v_conv_b, v_w_down, v_final_norm_w):
    D = D_MODEL
    q = 2 * lax.axis_index("x") + lax.axis_index("y")
    ccols = D_FF // N_CHIPS

    pack = jnp.concatenate([w_in[0].T, w_gate[0].T, w_up[0].T, w_down[0], w_out[0]], axis=0).astype(BF16)
    cw8 = jnp.concatenate([conv_w[0], jnp.zeros((SUBLANES - 3, ccols), F32)], axis=0)
    ex = _MeshExchange(pack, cw8)
    p = dict(norm_mix_w=norm_mix_w, b_attn=b_attn, lb=_lb_fwd(lb_logits), hg_norm_w=hg_norm_w, sinks=sinks,
             norm_ffn_w=norm_ffn_w, conv_b=conv_b, final_norm_w=final_norm_w.reshape(1, D))
    loss_cols, dx, g = _local_step(x[0], loss_target[0], p, ex)
    conv_w8 = ex.conv_w8

    small = _pack_small(dict(loss=loss_cols, norm_mix_w=g["norm_mix_w"], b_attn=g["b_attn"], lb_logits=g["lb"],
                             hg_norm_w=g["hg_norm_w"], sinks=g["sinks8"], norm_ffn_w=g["norm_ffn_w"],
                             conv_w=g["conv_w8"][:3], conv_b=g["conv_b"], final_norm_w=g["final_norm_w"]))
    parts_ffn, got_ffn = _send_wait(FFN_W, ex.sent, dx, name="send_ffn_wait")
    late = (0, 4)
    gs = [g["g_in_t"], g["g_out"]]
    *theirs, sall = _exchange_halves(late, gs, small, name="exchange_halves_late")
    parts_late = _chip_partial(late, gs, theirs, name="chip_partial_late", out_dtype=BF16)
    sent_late = _send_start(late, parts_late, name="send_late_start")
    big = {}

    def finish(ws, parts, got, specs, tag, after):
        shards = _exchange_reduced(ws, _chip_reduce(ws, parts, got, name="chip_reduce_" + tag, after=after),
                                   name="exchange_reduced_" + tag)
        for gw, (n, w, m, v, tr) in zip(shards, specs):
            view = (lambda a: a[0].T) if tr else (lambda a: a[0])
            back = (lambda a: a.T[None]) if tr else (lambda a: a[None])
            d_, m_, v_ = _adamw(view(w), gw, view(m), view(v), name="adamw_" + n)
            big[n] = (back(gw), back(d_), back(m_), back(v_))
        return d_

    last = finish(FFN_W, parts_ffn, got_ffn, (("w_gate", w_gate, m_w_gate, v_w_gate, True), ("w_up", w_up, m_w_up, v_w_up, True),
                                              ("w_down", w_down, m_w_down, v_w_down, False)), "ffn", sent_late["token"])
    parts_late, got_late = _send_wait(late, sent_late, last, name="send_late_wait")
    finish(late, parts_late, got_late, (("w_in", w_in, m_w_in, v_w_in, True), ("w_out", w_out, m_w_out, v_w_out, False)),
           "late", None)

    def place(a):
        return lax.dynamic_update_slice(jnp.zeros((3, D_FF), F32), a[0], (0, q * ccols))

    def small_pack(ws, cw):
        nm, ba, lbl, hg, sk, nf, cb, fn = ws
        return _pack_small(dict(norm_mix_w=nm, b_attn=ba, lb_logits=lbl, hg_norm_w=hg,
                                sinks=jnp.broadcast_to(sk.reshape(ATT_HEADS, 1), (ATT_HEADS, LANES)), norm_ffn_w=nf,
                                conv_w=cw, conv_b=cb, final_norm_w=fn))

    wp = small_pack((norm_mix_w, b_attn, lb_logits, hg_norm_w, sinks, norm_ffn_w, conv_b, final_norm_w), conv_w8[:3])
    mp = small_pack((m_norm_mix_w, m_b_attn, m_lb_logits, m_hg_norm_w, m_sinks, m_norm_ffn_w, m_conv_b, m_final_norm_w),
                    place(m_conv_w))
    vp = small_pack((v_norm_mix_w, v_b_attn, v_lb_logits, v_hg_norm_w, v_sinks, v_norm_ffn_w, v_conv_b, v_final_norm_w),
                    place(v_conv_w))
    outs = _small_update(sall, wp, mp, vp)
    loss = outs[4][0, 0]

    def small_out(pk, n, ref):
        if n == "sinks":
            return pk[SMALL_OFF[n]:SMALL_OFF[n] + ATT_HEADS, 0].reshape(ref.shape)
        if n == "conv_w":
            full = _unpack_small(pk, n, (3, D_FF))
            return lax.dynamic_slice(full, (0, q * ccols), (3, ccols))[None]
        return _unpack_small(pk, n, ref.shape)

    refs = dict(norm_mix_w=norm_mix_w, b_attn=b_attn, lb_logits=lb_logits, hg_norm_w=hg_norm_w, sinks=sinks,
                norm_ffn_w=norm_ffn_w, conv_w=conv_w, conv_b=conv_b, final_norm_w=final_norm_w)
    order = ("norm_mix_w", "w_in", "b_attn", "lb_logits", "hg_norm_w", "sinks", "w_out", "norm_ffn_w", "w_gate", "w_up",
             "conv_w", "conv_b", "w_down", "final_norm_w")
    res = [loss, dx[None]]
    for k in range(4):
        for n in order:
            res.append(big[n][k] if n in big else small_out(outs[k], n, refs[n]))
    return tuple(res)
```

```python
import functools
import math

import jax
import jax.numpy as jnp
from jax import lax
from jax.experimental import pallas as pl
from jax.experimental.pallas import tpu as pltpu

F32 = jnp.float32
BF16 = jnp.bfloat16

D_MODEL = 1024
HG_HEADS = 4
HG_DK = 128
HG_W = HG_HEADS * HG_DK
HG_CHUNK = 64
HG_SUB = 8
ATT_HEADS = 8
ATT_KV = 2
ATT_GROUP = ATT_HEADS // ATT_KV
ATT_HD = 64
ATT_BLOCK = 128
ATT_Q_W = ATT_HEADS * ATT_HD
ATT_KV_W = ATT_KV * ATT_HD
ATT_COLS = ATT_Q_W + 2 * ATT_KV_W
IN_COLS = 4 * HG_W + ATT_COLS
D_FF = 2816
EPS = 1e-6
ADAM_LR, ADAM_B1, ADAM_B2, ADAM_EPS, ADAM_WD, ADAM_STEP = 0.001, 0.9, 0.999, 1e-08, 0.01, 10
NEG = -1e30

V7X_VMEM_BYTES = 64 * 1024 * 1024
VMEM_LIMIT = 48 * 1024 * 1024
SUBLANES = 8

N_CHIPS = 4


def _cp(sem=None, **kw):
    return pltpu.CompilerParams(dimension_semantics=sem, vmem_limit_bytes=VMEM_LIMIT, **kw)


def _sds(shape, dtype):
    return jax.ShapeDtypeStruct(shape, dtype)


def _wspec(w):
    arr, rows, blk = w
    return pl.BlockSpec((rows, arr.shape[1]), lambda i: (blk, 0))


def _mm_nt(a, w, *, splits, out_dtype, name, after=None, tm=512):
    M, K = a.shape
    N = w[1]
    tm = min(tm, M)
    assert sum(splits) == N and M % tm == 0
    offs = [sum(splits[:i]) for i in range(len(splits))]
    n_in = 2 if after is None else 3

    def body(*refs):
        a_ref, w_ref = refs[0], refs[1]
        acc = lax.dot_general(a_ref[...], w_ref[...], (((1,), (1,)), ((), ())), preferred_element_type=F32)
        for o_ref, c0, n in zip(refs[n_in:], offs, splits):
            o_ref[...] = acc[:, c0:c0 + n].astype(out_dtype)

    in_specs = [pl.BlockSpec((tm, K), lambda i: (i, 0)), _wspec(w)]
    args = [a, w[0]]
    if after is not None:
        in_specs.append(pl.BlockSpec(memory_space=pl.ANY))
        args.append(after)
    outs = pl.pallas_call(
        body, name=name, grid=(M // tm,), in_specs=in_specs,
        out_specs=[pl.BlockSpec((tm, n), lambda i: (i, 0)) for n in splits],
        out_shape=[_sds((M, n), out_dtype) for n in splits],
        compiler_params=_cp(("parallel",)),
    )(*args)
    return outs


def _mm_nn(pieces, ws, *, name, out_dtype=F32, residual=None, epilogue=None, prologue=None, after=None,
           w_transposed=False, tm=512):
    pro_fn, pro_rows, pro_bc, pro_out = prologue or (None, [], [], None)
    if prologue is not None:
        assert pieces is None and len(ws) == 1
        pieces = [[pro_out]]
    M = pieces[0][0].shape[0]
    K = ws[0][1] if w_transposed else ws[0][0].shape[1]
    tm = min(tm, M)
    flat = [] if prologue is not None else [p for grp in pieces for p in grp]
    n_p = len(flat)
    n_w = len(ws)
    n_pr, n_pb = len(pro_rows), len(pro_bc)
    fn, row_ins, bc_ins, row_outs, acc_outs = epilogue or (None, [], [], [_sds((M, K), out_dtype)], [])
    if residual is not None:
        assert epilogue is None
        row_ins = [residual]
    n_r, n_b, n_o = len(row_ins), len(bc_ins), len(row_outs)
    lead = [] if after is None else [after]

    def body(*refs):
        refs = refs[len(lead):]
        p_refs = refs[:n_p]
        w_refs = refs[n_p:n_p + n_w]
        extra = [r[...] for r in refs[n_p + n_w:n_p + n_w + n_r + n_b]]
        base = n_p + n_w + n_r + n_b
        pro = [r[...] for r in refs[base:base + n_pr + n_pb]]
        base += n_pr + n_pb
        o_refs = refs[base:base + n_o]
        a_refs = refs[base + n_o:base + n_o + len(acc_outs)]
        if pro_fn is not None:
            lhs = pro_fn(*pro).astype(pro_out.dtype)
            refs[-1][...] = lhs
            tiles = [lhs]
        else:
            tiles = [r[...] for r in p_refs]
        acc = None
        k = 0
        for gi, grp in enumerate(pieces):
            c0 = 0
            for p in grp:
                n = p.shape[1]
                if w_transposed:
                    t = lax.dot_general(tiles[k], w_refs[gi][...], (((1,), (1,)), ((), ())), preferred_element_type=F32)
                else:
                    t = jnp.dot(tiles[k], w_refs[gi][c0:c0 + n, :], preferred_element_type=F32)
                acc = t if acc is None else acc + t
                c0 += n
                k += 1
        if fn is None:
            res = (acc + extra[0] if residual is not None else acc,)
        else:
            res = fn(acc, *extra)
        for o_ref, val in zip(o_refs, res[:n_o]):
            o_ref[...] = val.astype(o_ref.dtype)
        if acc_outs:
            @pl.when(pl.program_id(0) == 0)
            def _():
                for a_ref in a_refs:
                    a_ref[...] = jnp.zeros_like(a_ref)
            for a_ref, val in zip(a_refs, res[n_o:]):
                a_ref[...] += val

    in_specs = [pl.BlockSpec((tm, p.shape[1]), lambda i: (i, 0)) for p in flat]
    in_specs += [_wspec(w) for w in ws]
    in_specs += [pl.BlockSpec((tm, r.shape[1]), lambda i: (i, 0)) for r in row_ins]
    in_specs += [pl.BlockSpec(b.shape, lambda i: (0, 0)) for b in bc_ins]
    in_specs += [pl.BlockSpec((tm, r.shape[1]), lambda i: (i, 0)) for r in pro_rows]
    in_specs += [pl.BlockSpec(b.shape, lambda i: (0, 0)) for b in pro_bc]
    out_specs = [pl.BlockSpec((tm, s.shape[1]), lambda i: (i, 0)) for s in row_outs]
    out_specs += [pl.BlockSpec(s.shape, lambda i: (0, 0)) for s in acc_outs]
    pro_outs = [] if prologue is None else [pro_out]
    out_specs += [pl.BlockSpec((tm, s.shape[1]), lambda i: (i, 0)) for s in pro_outs]
    outs = pl.pallas_call(
        body, name=name, grid=(M // tm,), in_specs=[pl.BlockSpec(memory_space=pl.ANY)] * len(lead) + in_specs,
        out_specs=out_specs, out_shape=list(row_outs) + list(acc_outs) + pro_outs,
        compiler_params=_cp(("arbitrary",) if acc_outs else ("parallel",)),
    )(*lead, *flat, *[w[0] for w in ws], *row_ins, *bc_ins, *pro_rows, *pro_bc)
    return outs if (epilogue is not None or prologue is not None) else outs[0]


def _mm_tn(pieces, x, *, name, tt=1024):
    M, K = x.shape
    tt = min(tt, M)
    ns = [p.shape[1] for p in pieces]
    offs = [sum(ns[:i]) for i in range(len(ns))]
    N = sum(ns)
    n_p = len(pieces)

    def body(*refs):
        p_refs = refs[:n_p]
        x_ref = refs[n_p]
        o_ref = refs[n_p + 1]

        @pl.when(pl.program_id(0) == 0)
        def _():
            o_ref[...] = jnp.zeros_like(o_ref)

        xv = x_ref[...]
        for p_ref, c0, n in zip(p_refs, offs, ns):
            o_ref[c0:c0 + n, :] += lax.dot_general(p_ref[...], xv, (((0,), (0,)), ((), ())),
                                                    preferred_element_type=F32)

    in_specs = [pl.BlockSpec((tt, n), lambda i: (i, 0)) for n in ns]
    in_specs.append(pl.BlockSpec((tt, K), lambda i: (i, 0)))
    return pl.pallas_call(
        body, name=name, grid=(M // tt,), in_specs=in_specs,
        out_specs=pl.BlockSpec((N, K), lambda i: (0, 0)),
        out_shape=_sds((N, K), F32),
        compiler_params=_cp(("arbitrary",)),
    )(*pieces, x)


def _rms_fwd(xf, w):
    inv = lax.rsqrt(jnp.mean(xf * xf, axis=-1, keepdims=True) + EPS)
    return xf * inv * w


def _rms_bwd(xf, w, dy):
    inv = lax.rsqrt(jnp.mean(xf * xf, axis=-1, keepdims=True) + EPS)
    xhat = xf * inv
    dxhat = dy * w
    dx = inv * (dxhat - xhat * jnp.mean(dxhat * xhat, axis=-1, keepdims=True))
    dw = jnp.sum(dy * xhat, axis=0, keepdims=True)
    return dx, dw


def _sigmoid(x):
    return 1.0 / (1.0 + jnp.exp(-x))


def _rowwise(fn, row_ins, bc_ins, row_outs, acc_outs, *, name, tm=256, after=None):
    M = row_outs[0].shape[0] if row_outs else row_ins[0][0].shape[0]
    assert M % tm == 0 and tm % SUBLANES == 0, (name, M, tm)
    n_r, n_b, n_o, n_a = len(row_ins), len(bc_ins), len(row_outs), len(acc_outs)
    n_after = 0 if after is None else 1

    def body(*refs):
        refs = refs[n_after:]
        ins = [r[...] for r in refs[:n_r + n_b]]
        o_refs = refs[n_r + n_b:n_r + n_b + n_o]
        a_refs = refs[n_r + n_b + n_o:]
        res = fn(*ins)
        for o_ref, val in zip(o_refs, res[:n_o]):
            o_ref[...] = val.astype(o_ref.dtype)
        if n_a:
            @pl.when(pl.program_id(0) == 0)
            def _():
                for a_ref in a_refs:
                    a_ref[...] = jnp.zeros_like(a_ref)
            for a_ref, val in zip(a_refs, res[n_o:]):
                a_ref[...] += val

    in_specs = [pl.BlockSpec((tm, cw), functools.partial(lambda i, cb, r0: (i + r0, cb), cb=cb, r0=r0))
                for (_, cw, cb, r0) in row_ins]
    in_specs += [pl.BlockSpec(b.shape, lambda i: (0, 0)) for b in bc_ins]
    out_specs = [pl.BlockSpec((tm, s.shape[1]), lambda i: (i, 0)) for s in row_outs]
    out_specs += [pl.BlockSpec(s.shape, lambda i: (0, 0)) for s in acc_outs]
    if n_after:
        in_specs = [pl.BlockSpec(memory_space=pl.ANY)] + in_specs
    return pl.pallas_call(
        body, name=name, grid=(M // tm,), in_specs=in_specs, out_specs=out_specs,
        out_shape=list(row_outs) + list(acc_outs),
        compiler_params=_cp(("arbitrary",) if n_a else ("parallel",)),
    )(*([after] if n_after else []), *[r[0] for r in row_ins], *bc_ins)


def _full(a, first_row_block=0):
    return (a, a.shape[1], 0, first_row_block)


def _conv_rows(ext, w_ref_val, lo):
    s1 = pltpu.roll(ext, 1, 0)
    s2 = pltpu.roll(ext, 2, 0)
    y = w_ref_val[0:1, :] * s2 + w_ref_val[1:2, :] * s1 + w_ref_val[2:3, :] * ext
    return y[SUBLANES:, :]


def _convact_fwd(gp, up, conv_w8, conv_b, *, name, tr=512, tc=1408):
    T, C = gp.shape
    tr = min(tr, T)
    hb = tr // SUBLANES

    def body(gp_ref, gph_ref, up_ref, w_ref, b_ref, act_ref):
        i = pl.program_id(1)
        halo = jnp.where(i > 0, gph_ref[...], 0.0)
        ext = jnp.concatenate([halo, gp_ref[...]], axis=0)
        gate = _conv_rows(ext, w_ref[...], 0) + b_ref[...]
        act_ref[...] = (gate * _sigmoid(gate) * up_ref[...]).astype(act_ref.dtype)

    return pl.pallas_call(
        body, name=name, grid=(C // tc, T // tr),
        in_specs=[pl.BlockSpec((tr, tc), lambda j, i: (i, j)),
                  pl.BlockSpec((SUBLANES, tc), lambda j, i: (jnp.maximum(i * hb - 1, 0), j)),
                  pl.BlockSpec((tr, tc), lambda j, i: (i, j)),
                  pl.BlockSpec((SUBLANES, tc), lambda j, i: (0, j)),
                  pl.BlockSpec((1, tc), lambda j, i: (0, j))],
        out_specs=pl.BlockSpec((tr, tc), lambda j, i: (i, j)),
        out_shape=_sds((T, C), BF16),
        compiler_params=_cp(("parallel", "parallel")),
    )(gp, gp, up, conv_w8, conv_b)


def _convact_bwd(gp, up, dact, conv_w8, conv_b, *, name, tr=256, tc=1408):
    T, C = gp.shape
    tr = min(tr, T)
    hb = tr // SUBLANES
    nr = T // tr

    def body(gp_ref, gpp_ref, gpn_ref, up_ref, upn_ref, da_ref, dan_ref, w_ref, b_ref,
             dgp_ref, dup_ref, dw_ref, db_ref):
        i = pl.program_id(1)
        w = w_ref[...]
        prev = jnp.where(i > 0, gpp_ref[...], 0.0)
        last = i == nr - 1
        gp_ext = jnp.concatenate([prev, gp_ref[...], gpn_ref[...]], axis=0)
        gate = _conv_rows(gp_ext, w, 0) + b_ref[...]
        up_e = jnp.concatenate([up_ref[...], upn_ref[...]], axis=0)
        da_e = jnp.concatenate([da_ref[...], dan_ref[...]], axis=0)
        row = lax.broadcasted_iota(jnp.int32, gate.shape, 0)
        valid = jnp.logical_or(row < tr, jnp.logical_not(last))
        sg = _sigmoid(gate)
        silu = gate * sg
        dgate = jnp.where(valid, da_e * up_e * (sg * (1.0 + gate * (1.0 - sg))), 0.0)
        dup_ref[...] = (da_e[:tr] * silu[:tr]).astype(dup_ref.dtype)
        n = tr + SUBLANES
        g1 = pltpu.roll(dgate, n - 1, 0)
        g2 = pltpu.roll(dgate, n - 2, 0)
        dgp = w[2:3, :] * dgate + w[1:2, :] * g1 + w[0:1, :] * g2
        dgp_ref[...] = dgp[:tr].astype(dgp_ref.dtype)
        gpc = gp_ref[...]
        dw0 = jnp.sum(gpc * g2[:tr], axis=0, keepdims=True)
        dw1 = jnp.sum(gpc * g1[:tr], axis=0, keepdims=True)
        dw2 = jnp.sum(gpc * dgate[:tr], axis=0, keepdims=True)
        dbv = jnp.sum(dgate[:tr], axis=0, keepdims=True)
        z = jnp.zeros((SUBLANES - 3, gpc.shape[1]), F32)

        @pl.when(i == 0)
        def _():
            dw_ref[...] = jnp.zeros_like(dw_ref)
            db_ref[...] = jnp.zeros_like(db_ref)

        dw_ref[...] += jnp.concatenate([dw0, dw1, dw2, z], axis=0)
        db_ref[...] += dbv

    cur = pl.BlockSpec((tr, tc), lambda j, i: (i, j))
    prv = pl.BlockSpec((SUBLANES, tc), lambda j, i: (jnp.maximum(i * hb - 1, 0), j))
    nxt = pl.BlockSpec((SUBLANES, tc), lambda j, i: (jnp.minimum((i + 1) * hb, T // SUBLANES - 1), j))
    return pl.pallas_call(
        body, name=name, grid=(C // tc, nr),
        in_specs=[cur, prv, nxt, cur, nxt, cur, nxt,
                  pl.BlockSpec((SUBLANES, tc), lambda j, i: (0, j)),
                  pl.BlockSpec((1, tc), lambda j, i: (0, j))],
        out_specs=[cur, cur,
                   pl.BlockSpec((SUBLANES, tc), lambda j, i: (0, j)),
                   pl.BlockSpec((1, tc), lambda j, i: (0, j))],
        out_shape=[_sds((T, C), BF16), _sds((T, C), BF16), _sds((SUBLANES, C), F32), _sds((1, C), F32)],
        compiler_params=_cp(("parallel", "arbitrary")),
    )(gp, gp, gp, up, up, dact, dact, conv_w8, conv_b)


def _cumsum_rows(x):
    n = x.shape[0]
    row = lax.broadcasted_iota(jnp.int32, x.shape, 0)
    s = 1
    while s < n:
        x = x + jnp.where(row >= s, pltpu.roll(x, s, 0), 0.0)
        s *= 2
    return x


def _rcumsum_rows(x):
    n = x.shape[0]
    row = lax.broadcasted_iota(jnp.int32, x.shape, 0)
    s = 1
    while s < n:
        x = x + jnp.where(row < n - s, pltpu.roll(x, n - s, 0), 0.0)
        s *= 2
    return x


def _dot_nt(a, b):
    return lax.dot_general(a.astype(BF16), b.astype(BF16), (((1,), (1,)), ((), ())), preferred_element_type=F32)


def _dot_tn(a, b):
    return lax.dot_general(a.astype(BF16), b.astype(BF16), (((0,), (0,)), ((), ())), preferred_element_type=F32)


def _dot_nn(a, b):
    return jnp.dot(a.astype(BF16), b.astype(BF16), preferred_element_type=F32)


def _dot3(a, b, contract):
    def split(x):
        hi = x.astype(BF16)
        return hi, (x - hi.astype(F32)).astype(BF16)

    a_hi, a_lo = split(a)
    b_hi, b_lo = split(b)
    dot = lambda x, y: lax.dot_general(x, y, (contract, ((), ())), preferred_element_type=F32)
    return dot(a_hi, b_hi) + (dot(a_hi, b_lo) + dot(a_lo, b_hi))


NT, TN, NN = ((1,), (1,)), ((0,), (0,)), ((1,), (0,))


def _hg_gates(hq, hf, lbv):
    sig = _sigmoid(hf)
    f = lbv + (1.0 - lbv) * sig
    return sig, f, jnp.log(f), 1.0 - f, hq * (HG_DK ** -0.5)


def _hg_sel_rows(ref, sp):
    return jnp.concatenate(
        [jnp.broadcast_to(ref[pl.ds(HG_SUB * i + sp, 1), :], (HG_SUB, HG_DK)) for i in range(HG_CHUNK // HG_SUB)], axis=0)


def _hg_masks():
    C = HG_CHUNK
    row = lax.broadcasted_iota(jnp.int32, (C, C), 0)
    col = lax.broadcasted_iota(jnp.int32, (C, C), 1)
    d = col - (row // HG_SUB) * HG_SUB
    tmod = row % HG_SUB
    diag_valid = jnp.logical_and(d >= 0, d <= tmod)
    return row, col, d, diag_valid


def _hg_scores(q, k, b, b_sc, k_sc):
    C, S = HG_CHUNK, HG_SUB
    row, col, d, diag_valid = _hg_masks()
    blocks = [jnp.zeros((S, C), F32)]
    for i in range(1, C // S):
        r = b_sc[pl.ds(S * i - 1, 1), :]
        qi = q[S * i:S * (i + 1)] * jnp.exp(b[S * i:S * (i + 1)] - r)
        kk = k * jnp.exp(jnp.minimum(r - b, 0.0))
        blocks.append(_dot_nt(qi, kk))
    a_off = jnp.where(col < (row // S) * S, jnp.concatenate(blocks, axis=0), 0.0)
    a_d = jnp.zeros((C, C), F32)
    for sp in range(S):
        bs = _hg_sel_rows(b_sc, sp)
        ks = _hg_sel_rows(k_sc, sp)
        e = jnp.exp(jnp.minimum(b - bs, 0.0))
        colv = jnp.sum(q * ks * e, axis=-1, keepdims=True)
        a_d = jnp.where(d == sp, colv, a_d)
    return a_off + jnp.where(diag_valid, a_d, 0.0)


def _hg_prep(hq_v, hf_v, lbv, b_sc, k_sc):
    sig, f, g, k, q = _hg_gates(hq_v, hf_v, lbv)
    b = _cumsum_rows(g)
    b_sc[...] = b
    k_sc[...] = k
    return sig, f, k, q, b, b_sc[pl.ds(HG_CHUNK - 1, 1), :]


def _hgrn_fwd(hq, hf, hi, lb, *, name):
    T = hq.shape[0]
    C, H, K = HG_CHUNK, HG_HEADS, HG_DK
    NC = T // C

    def body(hq_ref, hf_ref, hi_ref, lb_ref, o_ref, st_ref, s_sc, b_sc, k_sc):
        @pl.when(pl.program_id(0) == 0)
        def _():
            s_sc[...] = jnp.zeros_like(s_sc)

        st_all = s_sc[...]
        st_ref[0] = st_all
        outs, news = [], []
        for h in range(H):
            sl = slice(K * h, K * (h + 1))
            _, _, k, q, b, bc = _hg_prep(hq_ref[:, sl], hf_ref[:, sl], lb_ref[:, sl], b_sc.at[h], k_sc.at[h])
            v = hi_ref[:, sl]
            st0 = st_all[:, sl]
            a = _hg_scores(q, k, b, b_sc.at[h], k_sc.at[h])
            outs.append(_dot_nn(a, v) + _dot_nt(q * jnp.exp(b), st0))
            news.append(st0 * jnp.exp(bc) + _dot_tn(v, k * jnp.exp(bc - b)))
        o_ref[...] = jnp.concatenate(outs, axis=1)
        s_sc[...] = jnp.concatenate(news, axis=1)

    blk = pl.BlockSpec((C, H * K), lambda c: (c, 0))
    return pl.pallas_call(
        body, name=name, grid=(NC,),
        in_specs=[blk, blk, blk, pl.BlockSpec((1, H * K), lambda c: (0, 0))],
        out_specs=[blk, pl.BlockSpec((1, K, H * K), lambda c: (c, 0, 0))],
        out_shape=[_sds((T, H * K), F32), _sds((NC, K, H * K), F32)],
        scratch_shapes=[pltpu.VMEM((K, H * K), F32), pltpu.VMEM((H, C, K), F32), pltpu.VMEM((H, C, K), F32)],
        compiler_params=_cp(("arbitrary",)),
    )(hq, hf, hi, lb)


def _hgrn_bwd(hq, hf, hi, lb, states, do, *, name):
    T = hq.shape[0]
    C, H, K, S = HG_CHUNK, HG_HEADS, HG_DK, HG_SUB
    NC = T // C

    def intra_slow(q, k, b, da, b_sc, k_sc):
        row, col, d, diag_valid = _hg_masks()
        a_blocks = [jnp.zeros((S, C), F32)]
        dq_blocks = [jnp.zeros((S, K), F32)]
        dk = jnp.zeros((C, K), F32)
        for i in range(1, C // S):
            r = b_sc[pl.ds(S * i - 1, 1), :]
            eq = jnp.exp(b[S * i:S * (i + 1)] - r)
            ek = jnp.exp(jnp.minimum(r - b, 0.0))
            qi = q[S * i:S * (i + 1)] * eq
            kk = k * ek
            a_blocks.append(_dot_nt(qi, kk))
            dai = jnp.where(col[S * i:S * (i + 1)] < S * i, da[S * i:S * (i + 1)], 0.0)
            dq_blocks.append(_dot_nn(dai, kk) * eq)
            dk = dk + _dot_tn(dai, qi) * ek
        dq = jnp.concatenate(dq_blocks, axis=0)
        a_off = jnp.where(col < (row // S) * S, jnp.concatenate(a_blocks, axis=0), 0.0)
        same_blk = (row // S == col // S).astype(BF16)
        tmod = (lax.broadcasted_iota(jnp.int32, (C, K), 0)) % S
        a_d = jnp.zeros((C, C), F32)
        for sp in range(S):
            bs = _hg_sel_rows(b_sc, sp)
            ks = _hg_sel_rows(k_sc, sp)
            e = jnp.where(tmod >= sp, jnp.exp(jnp.minimum(b - bs, 0.0)), 0.0)
            eks = e * ks
            a_d = jnp.where(d == sp, jnp.sum(q * eks, axis=-1, keepdims=True), a_d)
            dacol = jnp.sum(jnp.where(d == sp, da, 0.0), axis=-1, keepdims=True)
            dq = dq + dacol * eks
            wq = dacol * e * q
            wq_hi = wq.astype(BF16)
            wq_lo = (wq - wq_hi.astype(F32)).astype(BF16)
            blk_sum = (jnp.dot(same_blk, wq_hi, preferred_element_type=F32)
                       + jnp.dot(same_blk, wq_lo, preferred_element_type=F32))
            dk = dk + jnp.where(tmod == sp, blk_sum, 0.0)
        return a_off + jnp.where(diag_valid, a_d, 0.0), dq, dk

    def one_head(pre, v, lbv, st0, dst1, dout, b_sc, k_sc):
        sig, f, k, q, b, bc = pre
        ebc = jnp.exp(bc)
        eb = jnp.exp(b)
        ekb = jnp.exp(bc - b)
        qt = q * eb
        kb = k * ekb
        row = lax.broadcasted_iota(jnp.int32, (C, C), 0)
        col = lax.broadcasted_iota(jnp.int32, (C, C), 1)
        da = jnp.where(col <= row, _dot_nt(dout, v), 0.0)
        dkb = _dot_nn(v, dst1)
        new_ds = _dot_tn(dout, qt) + dst1 * ebc
        a, dq_i, dk_i = intra_slow(q, k, b, da, b_sc, k_sc)
        dq = _dot_nn(dout, st0) * eb + dq_i
        dk = dkb * ekb + dk_i
        dv = _dot_tn(a, dout) + _dot_nt(kb, dst1)
        extra = jnp.sum(dkb * kb, axis=0, keepdims=True) + ebc * jnp.sum(st0 * dst1, axis=0, keepdims=True)
        rowk = lax.broadcasted_iota(jnp.int32, (C, K), 0)
        db = q * dq - k * dk + jnp.where(rowk == C - 1, extra, 0.0)
        dg = _rcumsum_rows(db)
        df = dg / f - dk
        return (dq * (K ** -0.5), df * (1.0 - lbv) * sig * (1.0 - sig), dv,
                jnp.sum(df * (1.0 - sig), axis=0, keepdims=True), new_ds)

    def body(hq_ref, hf_ref, hi_ref, lb_ref, st_ref, do_ref, dq_ref, dhf_ref, dv_ref, dlb_ref, ds_sc, b_sc, k_sc):
        @pl.when(pl.program_id(0) == 0)
        def _():
            ds_sc[...] = jnp.zeros_like(ds_sc)
            dlb_ref[...] = jnp.zeros_like(dlb_ref)

        st_all = st_ref[0]
        ds_all = ds_sc[...]
        res = []
        for h in range(H):
            sl = slice(K * h, K * (h + 1))
            pre = _hg_prep(hq_ref[:, sl], hf_ref[:, sl], lb_ref[:, sl], b_sc.at[h], k_sc.at[h])
            res.append(one_head(pre, hi_ref[:, sl], lb_ref[:, sl], st_all[:, sl], ds_all[:, sl], do_ref[:, sl],
                                b_sc.at[h], k_sc.at[h]))
        cat = lambda j: jnp.concatenate([r[j] for r in res], axis=1)
        dq_ref[...] = cat(0).astype(dq_ref.dtype)
        dhf_ref[...] = cat(1).astype(dhf_ref.dtype)
        dv_ref[...] = cat(2).astype(dv_ref.dtype)
        dlb_ref[...] += cat(3)
        ds_sc[...] = cat(4)

    blk = pl.BlockSpec((C, H * K), lambda c: (NC - 1 - c, 0))
    par = pl.BlockSpec((1, H * K), lambda c: (0, 0))
    return pl.pallas_call(
        body, name=name, grid=(NC,),
        in_specs=[blk, blk, blk, par, pl.BlockSpec((1, K, H * K), lambda c: (NC - 1 - c, 0, 0)), blk],
        out_specs=[blk, blk, blk, par],
        out_shape=[_sds((T, H * K), BF16)] * 3 + [_sds((1, H * K), F32)],
        scratch_shapes=[pltpu.VMEM((K, H * K), F32), pltpu.VMEM((H, C, K), F32), pltpu.VMEM((H, C, K), F32)],
        compiler_params=_cp(("arbitrary",)),
    )(hq, hf, hi, lb, states, do)


def _att_valid(n):
    R, B = ATT_GROUP * ATT_BLOCK, ATT_BLOCK
    j = lax.broadcasted_iota(jnp.int32, (2 * B, R), 0)
    t = lax.broadcasted_iota(jnp.int32, (2 * B, R), 1) % B
    dist = t + B - j
    first_key = jnp.where(n > 0, 0, B)
    return jnp.logical_and(jnp.logical_and(dist >= 0, dist < B), j >= first_key)


def _att_load(cur_ref, prev_ref, ba_ref, kv):
    hd = ATT_HD
    def cols(ref, c0):
        return ref[:, c0:c0 + hd] + ba_ref[:, c0:c0 + hd]
    qs = jnp.concatenate([cols(cur_ref, hd * (ATT_GROUP * kv + g)) for g in range(ATT_GROUP)], axis=0)
    kc = jnp.concatenate([cols(prev_ref, ATT_Q_W + hd * kv), cols(cur_ref, ATT_Q_W + hd * kv)], axis=0)
    vc = jnp.concatenate([cols(prev_ref, ATT_Q_W + ATT_KV_W + hd * kv), cols(cur_ref, ATT_Q_W + ATT_KV_W + hd * kv)], axis=0)
    return qs, kc, vc


def _att_probs(qs, kc, valid, sink_ref, kv):
    scale = 1.0 / math.sqrt(ATT_HD)
    s = jnp.where(valid, _dot_nt(kc, qs) * scale, NEG)
    sink = jnp.concatenate([jnp.full((1, ATT_BLOCK), sink_ref[0, ATT_GROUP * kv + g], F32) for g in range(ATT_GROUP)], axis=1)
    m = jnp.maximum(jnp.max(s, axis=0, keepdims=True), sink)
    p = jnp.exp(s - m)
    ps = jnp.exp(sink - m)
    inv = 1.0 / (jnp.sum(p, axis=0, keepdims=True) + ps)
    return p * inv, ps * inv


def _attn_fwd(att, b_attn, sinks, *, name, after=None):
    T = att.shape[0]
    B = ATT_BLOCK
    NB = T // B
    lead = [] if after is None else [after]

    def body(*refs):
        sink_ref, cur_ref, prev_ref, ba_ref, o_ref = refs[len(lead):]
        valid = _att_valid(pl.program_id(0))
        for kv in range(ATT_KV):
            qs, kc, vc = _att_load(cur_ref, prev_ref, ba_ref, kv)
            prob, _ = _att_probs(qs, kc, valid, sink_ref, kv)
            o = _dot_tn(prob, vc)
            for g in range(ATT_GROUP):
                c0 = ATT_HD * (ATT_GROUP * kv + g)
                o_ref[:, c0:c0 + ATT_HD] = o[B * g:B * (g + 1)]

    return pl.pallas_call(
        body, name=name, grid=(NB,),
        in_specs=[pl.BlockSpec(memory_space=pl.ANY)] * len(lead) + [
            pl.BlockSpec(memory_space=pltpu.SMEM),
            pl.BlockSpec((B, ATT_COLS), lambda n: (n, 0)),
            pl.BlockSpec((B, ATT_COLS), lambda n: (jnp.maximum(n - 1, 0), 0)),
            pl.BlockSpec((1, ATT_COLS), lambda n: (0, 0))],
        out_specs=pl.BlockSpec((B, ATT_Q_W), lambda n: (n, 0)),
        out_shape=_sds((T, ATT_Q_W), F32),
        compiler_params=_cp(("parallel",)),
    )(*lead, sinks, att, att, b_attn)


def _attn_bwd(att, b_attn, sinks, dmix, *, name):
    T = att.shape[0]
    B, hd = ATT_BLOCK, ATT_HD
    NB = T // B
    scale = 1.0 / math.sqrt(hd)

    def body(sink_ref, cur_ref, prev_ref, ba_ref, do_ref, daq_ref, dakv_ref, dsink_ref, dbq_ref, dbkv_ref,
             carry_sc, cprev_sc, ccur_sc):
        n = pl.program_id(0)

        @pl.when(n == 0)
        def _():
            carry_sc[...] = jnp.zeros_like(carry_sc)
            dsink_ref[...] = jnp.zeros_like(dsink_ref)
            dbq_ref[...] = jnp.zeros_like(dbq_ref)
            dbkv_ref[...] = jnp.zeros_like(dbkv_ref)

        @pl.when(n < NB)
        def _():
            valid = _att_valid(n)
            hrow = lax.broadcasted_iota(jnp.int32, (SUBLANES, 128), 0)
            dsink = jnp.zeros((SUBLANES, 128), F32)
            for kv in range(ATT_KV):
                qs, kc, vc = _att_load(cur_ref, prev_ref, ba_ref, kv)
                prob, psink = _att_probs(qs, kc, valid, sink_ref, kv)
                dout = jnp.concatenate(
                    [do_ref[:, hd * (ATT_GROUP * kv + g):hd * (ATT_GROUP * kv + g + 1)] for g in range(ATT_GROUP)], axis=0)
                dp = _dot_nt(vc, dout)
                delta = jnp.sum(prob * dp, axis=0, keepdims=True)
                dsc = prob * (dp - delta) * scale
                dq = _dot_tn(dsc, kc)
                dk = _dot_nn(dsc, qs)
                dvv = _dot_nn(prob, dout)
                dsk = psink * delta
                for g in range(ATT_GROUP):
                    h = ATT_GROUP * kv + g
                    daq_ref[:, hd * h:hd * (h + 1)] = dq[B * g:B * (g + 1)].astype(daq_ref.dtype)
                    tot = jnp.sum(dsk[:, B * g:B * (g + 1)], axis=1, keepdims=True)
                    dsink = dsink - jnp.where(hrow == h, tot, 0.0)
                cprev_sc[:, hd * kv:hd * (kv + 1)] = dk[:B]
                ccur_sc[:, hd * kv:hd * (kv + 1)] = dk[B:]
                cprev_sc[:, ATT_KV_W + hd * kv:ATT_KV_W + hd * (kv + 1)] = dvv[:B]
                ccur_sc[:, ATT_KV_W + hd * kv:ATT_KV_W + hd * (kv + 1)] = dvv[B:]
            dsink_ref[...] += dsink
            dbq_ref[...] += jnp.sum(daq_ref[...].astype(F32), axis=0, keepdims=True)
            done = carry_sc[...] + cprev_sc[...]
            dakv_ref[...] = done.astype(dakv_ref.dtype)
            dbkv_ref[...] += jnp.sum(done.astype(dakv_ref.dtype).astype(F32), axis=0, keepdims=True)
            carry_sc[...] = ccur_sc[...]

        @pl.when(n == NB)
        def _():
            done = carry_sc[...]
            dakv_ref[...] = done.astype(dakv_ref.dtype)
            dbkv_ref[...] += jnp.sum(done.astype(dakv_ref.dtype).astype(F32), axis=0, keepdims=True)

    cl = lambda n: jnp.minimum(n, NB - 1)
    return pl.pallas_call(
        body, name=name, grid=(NB + 1,),
        in_specs=[pl.BlockSpec(memory_space=pltpu.SMEM),
                  pl.BlockSpec((B, ATT_COLS), lambda n: (cl(n), 0)),
                  pl.BlockSpec((B, ATT_COLS), lambda n: (jnp.maximum(cl(n) - 1, 0), 0)),
                  pl.BlockSpec((1, ATT_COLS), lambda n: (0, 0)),
                  pl.BlockSpec((B, ATT_Q_W), lambda n: (cl(n), 0))],
        out_specs=[pl.BlockSpec((B, ATT_Q_W), lambda n: (cl(n), 0)),
                   pl.BlockSpec((B, 2 * ATT_KV_W), lambda n: (jnp.maximum(n - 1, 0), 0)),
                   pl.BlockSpec((SUBLANES, 128), lambda n: (0, 0)),
                   pl.BlockSpec((1, ATT_Q_W), lambda n: (0, 0)),
                   pl.BlockSpec((1, 2 * ATT_KV_W), lambda n: (0, 0))],
        out_shape=[_sds((T, ATT_Q_W), BF16), _sds((T, 2 * ATT_KV_W), BF16), _sds((SUBLANES, 128), F32),
                   _sds((1, ATT_Q_W), F32), _sds((1, 2 * ATT_KV_W), F32)],
        scratch_shapes=[pltpu.VMEM((B, 2 * ATT_KV_W), F32)] * 3,
        compiler_params=_cp(("arbitrary",)),
    )(sinks, att, att, b_attn, dmix)


def _silu_and_grad(x):
    sg = _sigmoid(x)
    return x * sg, sg * (1.0 + x * (1.0 - sg))


def _mix_fwd_fn(o_raw, hg, o_att, hgw):
    outs = []
    for h in range(HG_HEADS):
        sl = slice(HG_DK * h, HG_DK * (h + 1))
        silu, _ = _silu_and_grad(hg[:, sl])
        outs.append(_rms_fwd(o_raw[:, sl], hgw) * silu)
    outs.append(o_att)
    return (jnp.concatenate(outs, axis=1),)


def _mix_bwd_fn(o_raw, hg, dmix, hgw):
    dos, dhgs = [], []
    dw = jnp.zeros((1, HG_DK), F32)
    for h in range(HG_HEADS):
        sl = slice(HG_DK * h, HG_DK * (h + 1))
        silu, dsilu = _silu_and_grad(hg[:, sl])
        dy = dmix[:, sl]
        dhgs.append(dy * _rms_fwd(o_raw[:, sl], hgw) * dsilu)
        dx, dwh = _rms_bwd(o_raw[:, sl], hgw, dy * silu)
        dos.append(dx)
        dw = dw + dwh
    return jnp.concatenate(dos, axis=1), jnp.concatenate(dhgs, axis=1), dw


def _final_fn(h2, tgt, wf):
    d = h2.shape[1]
    err = _rms_fwd(h2, wf) - tgt
    loss_cols = (0.5 / d) * jnp.sum(err * err, axis=0, keepdims=True)
    dh2, dwf = _rms_bwd(h2, wf, err * (1.0 / d))
    return dh2, dh2, loss_cols, dwf


class _NoExchange:
    def __init__(self, weights):
        self.weights = weights

    def start(self):
        return None

    def w_in(self, after):
        return self.weights["w_in_t"]

    def mid(self, after):
        return None

    def rest(self, after):
        return self.weights

    def ffn_grads(self, gs):
        return None

    def ffn_grads_send(self, after):
        return None


def _local_step(x, tgt, p, ex):
    T, D = x.shape
    row = lambda n, dt: _sds((T, n), dt)
    acc = lambda n: _sds((1, n), F32)

    (u,) = _rowwise(lambda xv, w: (_rms_fwd(xv, w),), [_full(x)], [p["norm_mix_w"]], [row(D, BF16)], [], name="rms_mix",
                    after=ex.start())
    p = dict(p, w_in_t=ex.w_in(u))
    hq, hf, hi, hg, att = _mm_nt(u, p["w_in_t"], splits=[HG_W] * 4 + [ATT_COLS], out_dtype=F32, name="in_proj")
    o_raw, states = _hgrn_fwd(hq, hf, hi, p["lb"], name="hgrn_fwd")
    o_att = _attn_fwd(att, p["b_attn"], p["sinks"], name="attn_fwd", after=ex.mid(o_raw))
    p = dict(p, **ex.rest(o_att))
    def out_epilogue(prod, xv, w):
        h1v = prod + xv
        return h1v, _rms_fwd(h1v, w)

    h1, v, mix = _mm_nn(None, [p["w_out"]], name="mix_out_proj",
                        prologue=(lambda *a: _mix_fwd_fn(*a)[0], [o_raw, hg, o_att], [p["hg_norm_w"]], row(D, BF16)),
                        epilogue=(out_epilogue, [x], [p["norm_ffn_w"]], [row(D, F32), row(D, BF16)], []))
    (gp,) = _mm_nt(v, p["w_gate_t"], splits=[D_FF], out_dtype=F32, name="gate_proj")
    (up,) = _mm_nt(v, p["w_up_t"], splits=[D_FF], out_dtype=F32, name="up_proj")
    act = _convact_fwd(gp, up, p["conv_w8"], p["conv_b"], name="convact_fwd")
    def down_epilogue(prod, h1v, tgtv, wf):
        return _final_fn(prod + h1v, tgtv, wf)

    dh2, dh2_b, loss_cols, d_final = _mm_nn(
        [[act]], [p["w_down"]], name="down_proj_loss",
        epilogue=(down_epilogue, [h1, tgt], [p["final_norm_w"]], [row(D, F32), row(D, BF16)], [acc(D), acc(D)]))

    (dact,) = _mm_nt(dh2_b, p["w_down"], splits=[D_FF], out_dtype=F32, name="d_act")
    g_down = _mm_tn([act], dh2_b, name="g_down")
    dgp, dup, d_conv_w8, d_conv_b = _convact_bwd(gp, up, dact, p["conv_w8"], p["conv_b"], name="convact_bwd")
    g_gate_t = _mm_tn([dgp], v, name="g_gate")
    g_up_t = _mm_tn([dup], v, name="g_up")
    swapping = ex.ffn_grads([g_gate_t, g_up_t, g_down])

    def ffn_norm_bwd(dvv, hv, dh2v, w):
        dx, dw = _rms_bwd(hv, w, dvv)
        dh1v = dx + dh2v
        return dh1v, dh1v, dw

    dh1, dh1_b, d_norm_ffn = _mm_nn(
        [[dgp], [dup]], [p["w_gate_t"], p["w_up_t"]], name="d_v_norm", after=swapping,
        epilogue=(ffn_norm_bwd, [h1, dh2], [p["norm_ffn_w"]], [row(D, F32), row(D, BF16)], [acc(D)]))
    sent = ex.ffn_grads_send(dh1_b)
    def mix_bwd(dmixv, o_rawv, hgv, hgw):
        do_rawv, dhgv, dw = _mix_bwd_fn(o_rawv, hgv, dmixv[:, :HG_W], hgw)
        return do_rawv, dhgv, dmixv[:, HG_W:], dw

    do_raw, dhg, do_att, d_hg_norm = _mm_nn(
        [[dh1_b]], [p["w_out"]], name="d_mix_bwd", w_transposed=True, after=sent,
        epilogue=(mix_bwd, [o_raw, hg], [p["hg_norm_w"]], [row(HG_W, F32), row(HG_W, BF16), row(ATT_Q_W, F32)], [acc(HG_DK)]))
    g_out = _mm_tn([mix], dh1_b, name="g_out")
    daq, dakv, d_sinks8, d_bq, d_bkv = _attn_bwd(att, p["b_attn"], p["sinks"], do_att, name="attn_bwd")
    dhq, dhf, dhi, d_lb = _hgrn_bwd(hq, hf, hi, p["lb"], states, do_raw, name="hgrn_bwd")
    pieces = [dhq, dhf, dhi, dhg, daq, dakv]
    g_in_t = _mm_tn(pieces, u, name="g_in")

    def mix_norm_bwd(duv, xv, dh1v, w):
        dx, dw = _rms_bwd(xv, w, duv)
        return dx + dh1v, dw

    dx, d_norm_mix = _mm_nn([pieces], [p["w_in_t"]], name="d_u_norm",
                            epilogue=(mix_norm_bwd, [x, dh1], [p["norm_mix_w"]], [row(D, F32)], [acc(D)]))
    grads = dict(g_in_t=g_in_t, g_out=g_out, g_gate_t=g_gate_t, g_up_t=g_up_t, g_down=g_down,
                 norm_mix_w=d_norm_mix, b_attn=jnp.concatenate([d_bq, d_bkv], axis=1), lb=d_lb, hg_norm_w=d_hg_norm,
                 sinks8=d_sinks8, norm_ffn_w=d_norm_ffn, conv_w8=d_conv_w8, conv_b=d_conv_b, final_norm_w=d_final)
    return loss_cols, dx, grads


SLAB = (IN_COLS // N_CHIPS, D_FF // N_CHIPS, D_FF // N_CHIPS, D_FF // N_CHIPS, D_MODEL // N_CHIPS)
N_W = len(SLAB)
PACK_OFF = tuple(sum(SLAB[:i]) for i in range(N_W))
PACK_ROWS = sum(SLAB)
FULL_OFF = tuple(N_CHIPS * o for o in PACK_OFF)
FULL_ROWS = N_CHIPS * PACK_ROWS
HALF = tuple(s // 2 for s in SLAB)
HPACK_OFF = tuple(sum(HALF[:i]) for i in range(N_W))
HPACK_ROWS = sum(HALF)
HFULL_OFF = tuple(N_CHIPS * o for o in HPACK_OFF)
HFULL_ROWS = N_CHIPS * HPACK_ROWS
CHIP_FLIPS = ((1, 0), (0, 1), (1, 1))
N_DEV = 8
BF16_ROWS = 16
ANY = pl.BlockSpec(memory_space=pl.ANY)


def _pos():
    return lax.axis_index("x"), lax.axis_index("y"), lax.axis_index("c")


def _flip(v, f):
    return 1 - v if f else v


def _rcopy(src, dst, ssem, rsem, dev):
    return pltpu.make_async_remote_copy(src_ref=src, dst_ref=dst, send_sem=ssem, recv_sem=rsem, device_id=dev,
                                        device_id_type=pl.DeviceIdType.MESH)


def _rows(ref, start, n, align=None):
    if not isinstance(start, int):
        if align is None:
            align = SUBLANES * (4 // jnp.dtype(ref.dtype).itemsize)
        start = pl.multiple_of(start, align)
    return ref.at[pl.ds(start, n), :]


FFN_W = (1, 2, 3)
N_PEER = 1 + len(CHIP_FLIPS)
HBM = pl.BlockSpec(memory_space=pltpu.HBM)
SEM = pl.BlockSpec(memory_space=pltpu.SEMAPHORE)
EFFECT = pltpu.SideEffectType.DATAFLOW_SIDE_EFFECTING
LANES = 128


def _sent_rows(k, w, c):
    return (0, SLAB[w]) if k == 0 else (c * HALF[w], HALF[w])


def _gather_start(pack, cw8):
    D = pack.shape[1]
    lands = [lax.empty((N_CHIPS * SLAB[0], D), pack.dtype), lax.empty((3 * N_CHIPS * SLAB[1], D), pack.dtype),
             lax.empty((N_CHIPS * SLAB[4], D), pack.dtype), lax.empty((N_CHIPS,) + cw8.shape, cw8.dtype)]
    bufs = [pack, cw8] + lands

    def body(pack_ref, cw_ref, l_in, l_ffn, l_out, l_cw, *rest):
        in_send, in_recv, out_send, out_recv, ffn_send, ffn_recv = rest[:6]
        token = rest[-1]
        x, y, c = _pos()
        q = 2 * x + y
        peers = _gather_peers(x, y, c)

        def send(k, peer, w, land, base, ssem, rsem):
            r0, n = _sent_rows(k, w, c)
            _rcopy(_rows(pack_ref, PACK_OFF[w] + r0, n), _rows(land, base + q * SLAB[w] + r0, n), ssem, rsem, peer).start()

        for k, peer in enumerate(peers):
            send(k, peer, 0, l_in, 0, in_send.at[k], in_recv.at[k])
        for k, peer in enumerate(peers):
            send(k, peer, 4, l_out, 0, out_send.at[k], out_recv.at[k])
            _rcopy(cw_ref, l_cw.at[q], out_send.at[N_PEER + k], out_recv.at[N_PEER + k], peer).start()
        for j, w in enumerate(FFN_W):
            for k, peer in enumerate(peers):
                send(k, peer, w, l_ffn, j * N_CHIPS * SLAB[w], ffn_send.at[k], ffn_recv.at[k])
        token[...] = jnp.zeros_like(token)

    n_sem = (N_PEER, N_PEER, 2 * N_PEER, 2 * N_PEER, N_PEER, N_PEER)
    outs = pl.pallas_call(
        body, name="gather_start", in_specs=[HBM] * len(bufs),
        out_specs=[SEM] * len(n_sem) + [HBM] * len(bufs) + [pl.BlockSpec(memory_space=pltpu.VMEM)],
        out_shape=[pltpu.SemaphoreType.DMA((n,)) for n in n_sem]
        + [pltpu.HBM(b.shape, b.dtype) for b in bufs] + [_sds((SUBLANES, LANES), F32)],
        input_output_aliases={i: len(n_sem) + i for i in range(len(bufs))},
        compiler_params=pltpu.CompilerParams(has_side_effects=EFFECT),
    )(*[pltpu.with_memory_space_constraint(b, pltpu.HBM) for b in bufs])
    bufs_out = outs[len(n_sem):]
    return dict(in_sems=outs[0:2], out_sems=outs[2:4], ffn_sems=outs[4:6], pack=bufs_out[0], cw=bufs_out[1], l_in=bufs_out[2],
                l_ffn=bufs_out[3], l_out=bufs_out[4], l_cw=bufs_out[5], token=bufs_out[6])


def _gather_peers(x, y, c):
    return [(x, y, 1 - c)] + [(_flip(x, fx), _flip(y, fy), c) for fx, fy in CHIP_FLIPS]


def _gather_wait_in(g, after):
    def body(pack_ref, l_in, send, recv, after_ref, pack_out, l_out):
        for k, peer in enumerate(_gather_peers(*_pos())):
            n = _sent_rows(k, 0, 0)[1]
            cp = _rcopy(_rows(pack_ref, PACK_OFF[0], n), _rows(l_in, 0, n), send.at[k], recv.at[k], peer)
            cp.wait_send()
            cp.wait_recv()

    return pl.pallas_call(
        body, name="gather_wait_in", in_specs=[HBM, HBM, SEM, SEM, ANY], out_specs=[HBM, HBM],
        out_shape=[pltpu.HBM(g["pack"].shape, g["pack"].dtype), pltpu.HBM(g["l_in"].shape, g["l_in"].dtype)],
        input_output_aliases={0: 0, 1: 1}, compiler_params=pltpu.CompilerParams(has_side_effects=EFFECT),
    )(g["pack"], g["l_in"], *g["in_sems"], after)


def _gather_wait_rest(g, pack, after):
    def body(pack_ref, cw_ref, l_ffn, l_out, l_cw, o_send, o_recv, f_send, f_recv, after_ref, o_ffn, o_out, o_cw):
        for k, peer in enumerate(_gather_peers(*_pos())):
            n_out = _sent_rows(k, 4, 0)[1]
            n_ffn = len(FFN_W) * _sent_rows(k, FFN_W[0], 0)[1]
            for cp in (_rcopy(_rows(pack_ref, PACK_OFF[4], n_out), _rows(l_out, 0, n_out), o_send.at[k], o_recv.at[k], peer),
                       _rcopy(cw_ref, l_cw.at[0], o_send.at[N_PEER + k], o_recv.at[N_PEER + k], peer),
                       _rcopy(_rows(pack_ref, PACK_OFF[FFN_W[0]], n_ffn), _rows(l_ffn, 0, n_ffn), f_send.at[k], f_recv.at[k], peer)):
                cp.wait_send()
                cp.wait_recv()

    ins = [pack, g["cw"], g["l_ffn"], g["l_out"], g["l_cw"]]
    return pl.pallas_call(
        body, name="gather_wait_rest", in_specs=[HBM] * 5 + [SEM] * 4 + [ANY], out_specs=[HBM] * 3,
        out_shape=[pltpu.HBM(b.shape, b.dtype) for b in ins[2:]],
        input_output_aliases={2: 0, 3: 1, 4: 2}, compiler_params=pltpu.CompilerParams(has_side_effects=EFFECT),
    )(*ins, *g["out_sems"], *g["ffn_sems"], after)


FWD_IN = ((0, 0, 0),)
FWD_REST = tuple((0, w, j * N_CHIPS * SLAB[w]) for j, w in enumerate(FFN_W)) + ((1, 4, 0),)


def _forward_copies(layout, src, dst, send_sems, recv_sems):
    x, y, c = _pos()
    sib = (x, y, 1 - c)
    cps = []
    for fx, fy in CHIP_FLIPS:
        qa = 2 * _flip(x, fx) + _flip(y, fy)
        for bi, w, base in layout:
            r0 = base + qa * SLAB[w] + c * HALF[w]
            cps.append(_rcopy(_rows(src[bi], r0, HALF[w]), _rows(dst[bi], r0, HALF[w]),
                              send_sems.at[len(cps)], recv_sems.at[len(cps)], sib))
    return cps


def _forward_in(l_in):
    n = len(CHIP_FLIPS) * len(FWD_IN)

    def body(in_ref, out_ref, send_sems, recv_sems):
        cps = _forward_copies(FWD_IN, [in_ref], [out_ref], send_sems, recv_sems)
        for cp in cps:
            cp.start()
        for cp in cps:
            cp.wait_recv()
        for cp in cps:
            cp.wait_send()

    return pl.pallas_call(
        body, name="forward_in", in_specs=[ANY], out_specs=ANY, out_shape=_sds(l_in.shape, l_in.dtype),
        input_output_aliases={0: 0},
        scratch_shapes=[pltpu.SemaphoreType.DMA((n,)), pltpu.SemaphoreType.DMA((n,))],
    )(l_in)


def _forward_rest_start(l_ffn, l_out):
    n = len(CHIP_FLIPS) * len(FWD_REST)
    bufs = [l_ffn, l_out]

    def body(a_ref, b_ref, send_sems, recv_sems, a_out, b_out, token):
        for cp in _forward_copies(FWD_REST, [a_ref, b_ref], [a_ref, b_ref], send_sems, recv_sems):
            cp.start()
        token[...] = jnp.zeros_like(token)

    outs = pl.pallas_call(
        body, name="forward_rest_start", in_specs=[HBM] * 2,
        out_specs=[SEM, SEM, HBM, HBM, pl.BlockSpec(memory_space=pltpu.VMEM)],
        out_shape=[pltpu.SemaphoreType.DMA((n,)), pltpu.SemaphoreType.DMA((n,))]
        + [pltpu.HBM(b.shape, b.dtype) for b in bufs] + [_sds((SUBLANES, LANES), F32)],
        input_output_aliases={0: 2, 1: 3}, compiler_params=pltpu.CompilerParams(has_side_effects=EFFECT),
    )(*[pltpu.with_memory_space_constraint(b, pltpu.HBM) for b in bufs])
    return dict(sems=outs[0:2], bufs=outs[2:4], token=outs[4])


def _forward_rest_wait(s, after):
    def body(a_ref, b_ref, send_sems, recv_sems, after_ref, a_out, b_out):
        for cp in _forward_copies(FWD_REST, [a_ref, b_ref], [a_ref, b_ref], send_sems, recv_sems):
            cp.wait_send()
            cp.wait_recv()

    return pl.pallas_call(
        body, name="forward_rest_wait", in_specs=[HBM, HBM, SEM, SEM, ANY], out_specs=[HBM, HBM],
        out_shape=[pltpu.HBM(b.shape, b.dtype) for b in s["bufs"]],
        input_output_aliases={0: 0, 1: 1}, compiler_params=pltpu.CompilerParams(has_side_effects=EFFECT),
    )(*s["bufs"], *s["sems"], after)


def _exchange_halves(ws, gs, small, *, name):
    D = gs[0].shape[1]
    n = len(ws)
    has_small = small is not None

    def body(*refs):
        g = refs[:n]
        t = refs[n + has_small:2 * n + has_small]
        sems = refs[2 * n + 2 * has_small:]
        d2d_send, d2d_recv = sems[0], sems[1]
        x, y, c = _pos()
        sib = (x, y, 1 - c)
        drains = []
        for i, w in enumerate(ws):
            h = HALF[w]
            for qq in range(N_CHIPS):
                _rcopy(_rows(g[i], qq * SLAB[w] + (1 - c) * h, h), _rows(t[i], qq * h, h),
                       d2d_send.at[i], d2d_recv.at[i], sib).start()
            drains.append(_rcopy(t[i], t[i], d2d_send.at[i], d2d_recv.at[i], sib))
        if has_small:
            small_ref, sall_ref = refs[n], refs[2 * n + 1]
            sm_send, sm_recv, loc_sem = sems[2], sems[3], sems[4]
            me = 4 * x + 2 * y + c
            own_small = pltpu.make_async_copy(small_ref, sall_ref.at[me], loc_sem)
            own_small.start()
            for f in range(1, N_DEV):
                peer = (_flip(x, f & 4), _flip(y, f & 2), _flip(c, f & 1))
                cp = _rcopy(small_ref, sall_ref.at[me], sm_send.at[f - 1], sm_recv.at[f - 1], peer)
                cp.start()
                drains.append(cp)
        for d in drains:
            d.wait_recv()
        for d in drains:
            d.wait_send()
        if has_small:
            own_small.wait()

    out_shape = [_sds((N_CHIPS * HALF[w], D), F32) for w in ws]
    scratch = [pltpu.SemaphoreType.DMA((n,)), pltpu.SemaphoreType.DMA((n,))]
    if has_small:
        out_shape.append(_sds((N_DEV,) + small.shape, F32))
        scratch += [pltpu.SemaphoreType.DMA((N_DEV - 1,)), pltpu.SemaphoreType.DMA((N_DEV - 1,)), pltpu.SemaphoreType.DMA]
    return pl.pallas_call(
        body, name=name, in_specs=[ANY] * (n + has_small), out_specs=[ANY] * (n + has_small),
        out_shape=out_shape, scratch_shapes=scratch,
    )(*gs, *([small] if has_small else []))


def _halves_copies(ws, g, t, send_sems, recv_sems):
    x, y, c = _pos()
    sib = (x, y, 1 - c)
    cps = []
    for i, w in enumerate(ws):
        h = HALF[w]
        for qq in range(N_CHIPS):
            cps.append(_rcopy(_rows(g[i], qq * SLAB[w] + (1 - c) * h, h), _rows(t[i], qq * h, h),
                              send_sems.at[N_CHIPS * i + qq], recv_sems.at[N_CHIPS * i + qq], sib))
    return cps


def _halves_start(ws, gs, *, name):
    D = gs[0].shape[1]
    n = len(ws)
    bufs = list(gs) + [lax.empty((N_CHIPS * HALF[w], D), F32) for w in ws]

    def body(*refs):
        for cp in _halves_copies(ws, refs[:n], refs[n:2 * n], refs[2 * n], refs[2 * n + 1]):
            cp.start()
        refs[-1][...] = jnp.zeros_like(refs[-1])

    outs = pl.pallas_call(
        body, name=name, in_specs=[HBM] * (2 * n),
        out_specs=[SEM, SEM] + [HBM] * (2 * n) + [pl.BlockSpec(memory_space=pltpu.VMEM)],
        out_shape=[pltpu.SemaphoreType.DMA((N_CHIPS * n,)), pltpu.SemaphoreType.DMA((N_CHIPS * n,))]
        + [pltpu.HBM(b.shape, b.dtype) for b in bufs] + [_sds((SUBLANES, LANES), F32)],
        input_output_aliases={i: 2 + i for i in range(2 * n)},
        compiler_params=pltpu.CompilerParams(has_side_effects=EFFECT),
    )(*[pltpu.with_memory_space_constraint(b, pltpu.HBM) for b in bufs])
    return dict(sems=outs[0:2], gs=outs[2:2 + n], theirs=outs[2 + n:2 + 2 * n], token=outs[-1])


def _halves_wait(ws, s, after, *, name):
    n = len(ws)

    def body(*refs):
        for cp in _halves_copies(ws, refs[:n], refs[n:2 * n], refs[2 * n], refs[2 * n + 1]):
            cp.wait_send()
            cp.wait_recv()

    bufs = list(s["gs"]) + list(s["theirs"])
    outs = pl.pallas_call(
        body, name=name, in_specs=[HBM] * (2 * n) + [SEM, SEM, ANY], out_specs=[HBM] * (2 * n),
        out_shape=[pltpu.HBM(b.shape, b.dtype) for b in bufs],
        input_output_aliases={i: i for i in range(2 * n)},
        compiler_params=pltpu.CompilerParams(has_side_effects=EFFECT),
    )(*bufs, *s["sems"], after)
    return outs[:n], outs[n:]


REDUCE_SPLIT = 2


def _chip_partial(ws, gs, theirs, *, name, out_dtype=F32):
    D = gs[0].shape[1]
    n = len(ws)

    def body(*refs):
        for i in range(n):
            refs[2 * n + i][...] = (refs[i][...] + refs[n + i][...]).astype(out_dtype)

    blk = [HALF[w] // REDUCE_SPLIT for w in ws]
    mine = [pl.BlockSpec((b, D), lambda qq, j: ((2 * qq + lax.axis_index("c")) * REDUCE_SPLIT + j, 0)) for b in blk]
    flat = [pl.BlockSpec((b, D), lambda qq, j: (qq * REDUCE_SPLIT + j, 0)) for b in blk]
    return pl.pallas_call(
        body, name=name, grid=(N_CHIPS, REDUCE_SPLIT), in_specs=mine + flat, out_specs=flat,
        out_shape=[_sds((N_CHIPS * HALF[w], D), out_dtype) for w in ws],
        compiler_params=_cp(("parallel", "parallel")),
    )(*gs, *theirs)


def _partial_copies(ws, part, got, send_sems, recv_sems):
    x, y, c = _pos()
    cps = []
    for k, (fx, fy) in enumerate(CHIP_FLIPS):
        peer = (_flip(x, fx), _flip(y, fy), c)
        qp = 2 * _flip(x, fx) + _flip(y, fy)
        for i, w in enumerate(ws):
            cps.append(_rcopy(_rows(part[i], qp * HALF[w], HALF[w]), _rows(got[i], k * HALF[w], HALF[w]),
                              send_sems.at[len(ws) * k + i], recv_sems.at[len(ws) * k + i], peer))
    return cps


def _send_chip_partials(ws, parts, *, name):
    D = parts[0].shape[1]
    n = len(ws)

    def body(*refs):
        cps = _partial_copies(ws, refs[:n], refs[n:2 * n], refs[2 * n], refs[2 * n + 1])
        for cp in cps:
            cp.start()
        for cp in cps:
            cp.wait_recv()
        for cp in cps:
            cp.wait_send()

    return pl.pallas_call(
        body, name=name, in_specs=[ANY] * n, out_specs=[ANY] * n,
        out_shape=[_sds((len(CHIP_FLIPS) * HALF[w], D), parts[0].dtype) for w in ws],
        scratch_shapes=[pltpu.SemaphoreType.DMA((len(CHIP_FLIPS) * n,)), pltpu.SemaphoreType.DMA((len(CHIP_FLIPS) * n,))],
    )(*parts)


def _send_start(ws, parts, *, name):
    D = parts[0].shape[1]
    n = len(ws)
    bufs = list(parts) + [lax.empty((len(CHIP_FLIPS) * HALF[w], D), parts[0].dtype) for w in ws]

    def body(*refs):
        send_sems, recv_sems = refs[2 * n], refs[2 * n + 1]
        for cp in _partial_copies(ws, refs[:n], refs[n:2 * n], send_sems, recv_sems):
            cp.start()
        refs[-1][...] = jnp.zeros_like(refs[-1])

    outs = pl.pallas_call(
        body, name=name, in_specs=[HBM] * (2 * n),
        out_specs=[SEM, SEM] + [HBM] * (2 * n) + [pl.BlockSpec(memory_space=pltpu.VMEM)],
        out_shape=[pltpu.SemaphoreType.DMA((len(CHIP_FLIPS) * n,)), pltpu.SemaphoreType.DMA((len(CHIP_FLIPS) * n,))]
        + [pltpu.HBM(b.shape, b.dtype) for b in bufs] + [_sds((SUBLANES, LANES), F32)],
        input_output_aliases={i: 2 + i for i in range(2 * n)},
        compiler_params=pltpu.CompilerParams(has_side_effects=EFFECT),
    )(*[pltpu.with_memory_space_constraint(b, pltpu.HBM) for b in bufs])
    return dict(sems=outs[0:2], parts=outs[2:2 + n], got=outs[2 + n:2 + 2 * n], token=outs[-1])


def _send_wait(ws, s, after, *, name):
    n = len(ws)

    def body(*refs):
        for cp in _partial_copies(ws, refs[:n], refs[n:2 * n], refs[2 * n], refs[2 * n + 1]):
            cp.wait_send()
            cp.wait_recv()

    bufs = list(s["parts"]) + list(s["got"])
    outs = pl.pallas_call(
        body, name=name, in_specs=[HBM] * (2 * n) + [SEM, SEM, ANY], out_specs=[HBM] * (2 * n),
        out_shape=[pltpu.HBM(b.shape, b.dtype) for b in bufs],
        input_output_aliases={i: i for i in range(2 * n)},
        compiler_params=pltpu.CompilerParams(has_side_effects=EFFECT),
    )(*bufs, *s["sems"], after)
    return outs[:n], outs[n:]


def _chip_reduce(ws, parts, got, *, name, after=None):
    D = parts[0].shape[1]
    nk = len(CHIP_FLIPS)
    n = len(ws)
    extra = [] if after is None else [after]

    def body(*refs):
        refs = refs[len(extra):]
        outs = refs[(1 + nk) * n:]
        for i in range(n):
            acc = refs[i][...].astype(F32)
            for k in range(nk):
                acc = acc + refs[n * (1 + k) + i][...].astype(F32)
            outs[i][...] = acc

    blk = [HALF[w] // REDUCE_SPLIT for w in ws]

    def q_idx(j):
        return (2 * lax.axis_index("x") + lax.axis_index("y")) * REDUCE_SPLIT + j

    in_specs = [pl.BlockSpec((b, D), lambda j: (q_idx(j), 0)) for b in blk]
    for k in range(nk):
        in_specs += [pl.BlockSpec((b, D), functools.partial(lambda j, k: (k * REDUCE_SPLIT + j, 0), k=k)) for b in blk]
    out_specs = [pl.BlockSpec((b, D), lambda j: (lax.axis_index("c") * REDUCE_SPLIT + j, 0)) for b in blk]
    return pl.pallas_call(
        body, name=name, grid=(REDUCE_SPLIT,), in_specs=[ANY] * len(extra) + in_specs, out_specs=out_specs,
        out_shape=[_sds((SLAB[w], D), F32) for w in ws],
        compiler_params=_cp(("parallel",)),
    )(*extra, *parts, *[g for _ in range(nk) for g in got])


def _exchange_reduced(ws, shards, *, name):
    n = len(ws)

    def body(*refs):
        ins, outs = refs[:n], refs[n:2 * n]
        send_sems, recv_sems = refs[2 * n], refs[2 * n + 1]
        x, y, c = _pos()
        sib = (x, y, 1 - c)
        cps = []
        for i, w in enumerate(ws):
            cp = _rcopy(_rows(ins[i], c * HALF[w], HALF[w]), _rows(outs[i], c * HALF[w], HALF[w]),
                        send_sems.at[i], recv_sems.at[i], sib)
            cp.start()
            cps.append(cp)
        for cp in cps:
            cp.wait_recv()
        for cp in cps:
            cp.wait_send()

    return pl.pallas_call(
        body, name=name, in_specs=[ANY] * n, out_specs=[ANY] * n,
        out_shape=[_sds(s.shape, s.dtype) for s in shards], input_output_aliases={i: i for i in range(n)},
        scratch_shapes=[pltpu.SemaphoreType.DMA((n,)), pltpu.SemaphoreType.DMA((n,))],
    )(*shards)


def _adamw_fn(w, g, m, v):
    m2 = ADAM_B1 * m + (1.0 - ADAM_B1) * g
    v2 = ADAM_B2 * v + (1.0 - ADAM_B2) * (g * g)
    m_hat = m2 / (1.0 - ADAM_B1 ** ADAM_STEP)
    v_hat = v2 / (1.0 - ADAM_B2 ** ADAM_STEP)
    return -ADAM_LR * (m_hat / (jnp.sqrt(v_hat) + ADAM_EPS) + ADAM_WD * w), m2, v2


def _adamw(w, g, m, v, *, name):
    shp = _sds(w.shape, F32)
    rows = w.shape[0]
    tm = max(t for t in range(SUBLANES, 512 + 1, SUBLANES) if rows % t == 0)
    return _rowwise(_adamw_fn, [_full(w), _full(g), _full(m), _full(v)], [], [shp] * 3, [], name=name, tm=tm)


SMALL_SEGS = (("loss", 8), ("norm_mix_w", 8), ("b_attn", 8), ("lb_logits", 8), ("hg_norm_w", 8), ("sinks", 8),
              ("norm_ffn_w", 8), ("conv_w", 72), ("conv_b", 24), ("final_norm_w", 8))
SMALL_OFF = {n: sum(r for _, r in SMALL_SEGS[:i]) for i, (n, _) in enumerate(SMALL_SEGS)}
SMALL_ROWS = sum(r for _, r in SMALL_SEGS)
LANES = 128


def _pack_small(parts):
    segs = []
    for n, r in SMALL_SEGS:
        a = parts.get(n)
        flat = jnp.zeros((0,), F32) if a is None else a.reshape(-1).astype(F32)
        segs.append(jnp.pad(flat, (0, r * LANES - flat.shape[0])).reshape(r, LANES))
    return jnp.concatenate(segs, axis=0)


def _unpack_small(pack, n, shape):
    size = math.prod(shape)
    r0 = SMALL_OFF[n]
    return pack[r0:r0 + dict(SMALL_SEGS)[n]].reshape(-1)[:size].reshape(shape)


def _small_update(sall, wp, mp, vp):
    R = SMALL_ROWS
    r_lb = SMALL_OFF["lb_logits"]

    def body(s_ref, w_ref, m_ref, v_ref, g_ref, d_ref, m2_ref, v2_ref, loss_ref):
        g = s_ref[0]
        for i in range(1, N_DEV):
            g = g + s_ref[i]
        tot = jnp.sum(jnp.sum(g[0:8], axis=1, keepdims=True), axis=0, keepdims=True)
        loss_ref[...] = jnp.broadcast_to(tot, loss_ref.shape)
        lg = w_ref[r_lb:r_lb + 8, :]
        p0 = _sigmoid(lg - pltpu.roll(lg, 4, 0))
        d = g[r_lb:r_lb + 8]
        d = d + pltpu.roll(d, 4, 0)
        sign = jnp.where(lax.broadcasted_iota(jnp.int32, d.shape, 0) < 4, 1.0, -1.0)
        g = jnp.concatenate([g[:r_lb], sign * d * p0 * (1.0 - p0), g[r_lb + 8:]], axis=0)
        g_ref[...] = g
        d_ref[...], m2_ref[...], v2_ref[...] = _adamw_fn(w_ref[...], g, m_ref[...], v_ref[...])

    full = pl.BlockSpec((R, LANES), lambda: (0, 0))
    return pl.pallas_call(
        body, name="small_update",
        in_specs=[pl.BlockSpec((N_DEV, R, LANES), lambda: (0, 0, 0)), full, full, full],
        out_specs=[full, full, full, full, pl.BlockSpec((8, LANES), lambda: (0, 0))],
        out_shape=[_sds((R, LANES), F32)] * 4 + [_sds((8, LANES), F32)],
        compiler_params=_cp(),
    )(sall, wp, mp, vp)


def _lb_fwd(lb_logits):
    n = lb_logits.shape[1]

    def body(l_ref, o_ref):
        o_ref[...] = _sigmoid(l_ref[0:1, :] - l_ref[1:2, :])

    return pl.pallas_call(body, name="lb_fwd", out_shape=_sds((1, n), F32), compiler_params=_cp())(lb_logits)


class _MeshExchange:
    def __init__(self, pack, cw8):
        self.gather = _gather_start(pack, cw8)
        self.sent = None
        self.conv_w8 = None

    def start(self):
        return self.gather["token"]

    def w_in(self, after):
        self.pack, l_in = _gather_wait_in(self.gather, after)
        return (_forward_in(l_in), N_CHIPS * SLAB[0], 0)

    def mid(self, after):
        l_ffn, l_out, l_cw = _gather_wait_rest(self.gather, self.pack, after)
        self.conv_w8 = jnp.concatenate([l_cw[i] for i in range(N_CHIPS)], axis=1)
        self.passing = _forward_rest_start(l_ffn, l_out)
        return self.passing["token"]

    def rest(self, after):
        l_ffn, l_out = _forward_rest_wait(self.passing, after)
        rows = N_CHIPS * SLAB[FFN_W[0]]
        return dict(w_gate_t=(l_ffn, rows, 0), w_up_t=(l_ffn, rows, 1), w_down=(l_ffn, rows, 2),
                    w_out=(l_out, N_CHIPS * SLAB[4], 0), conv_w8=self.conv_w8)

    def ffn_grads(self, gs):
        self.swap = _halves_start(FFN_W, gs, name="halves_ffn_start")
        return self.swap["token"]

    def ffn_grads_send(self, after):
        gs, theirs = _halves_wait(FFN_W, self.swap, after, name="halves_ffn_wait")
        parts = _chip_partial(FFN_W, gs, theirs, name="chip_partial_ffn")
        self.sent = _send_start(FFN_W, parts, name="send_ffn_start")
        return self.sent["token"]


def kernel(x, norm_mix_w, w_in, b_attn, lb_logits, hg_norm_w, sinks, w_out, norm_ffn_w, w_gate, w_up, conv_w, conv_b, w_down, final_norm_w, loss_target, m_norm_mix_w, m_w_in, m_b_attn, m_lb_logits, m_hg_norm_w, m_sinks, m_w_out, m_norm_ffn_w, m_w_gate, m_w_up, m_conv_w, m_conv_b, m_w_down, m_final_norm_w, v_norm_mix_w, v_w_in, v_b_attn, v_lb_logits, v_hg_norm_w, v_sinks, v_w_out, v_norm_ffn_w, v_w_gate, v_w_up, v_conv_w, v_conv_b, v_w_down, v_final_norm_w):
    D = D_MODEL
    q = 2 * lax.axis_index("x") + lax.axis_index("y")
    ccols = D_FF // N_CHIPS

    pack = jnp.concatenate([w_in[0].T, w_gate[0].T, w_up[0].T, w_down[0], w_out[0]], axis=0).astype(BF16)
    cw8 = jnp.concatenate([conv_w[0], jnp.zeros((SUBLANES - 3, ccols), F32)], axis=0)
    ex = _MeshExchange(pack, cw8)
    p = dict(norm_mix_w=norm_mix_w, b_attn=b_attn, lb=_lb_fwd(lb_logits), hg_norm_w=hg_norm_w, sinks=sinks,
             norm_ffn_w=norm_ffn_w, conv_b=conv_b, final_norm_w=final_norm_w.reshape(1, D))
    loss_cols, dx, g = _local_step(x[0], loss_target[0], p, ex)
    conv_w8 = ex.conv_w8

    small = _pack_small(dict(loss=loss_cols, norm_mix_w=g["norm_mix_w"], b_attn=g["b_attn"], lb_logits=g["lb"],
                             hg_norm_w=g["hg_norm_w"], sinks=g["sinks8"], norm_ffn_w=g["norm_ffn_w"],
                             conv_w=g["conv_w8"][:3], conv_b=g["conv_b"], final_norm_w=g["final_norm_w"]))
    parts_ffn, got_ffn = _send_wait(FFN_W, ex.sent, dx, name="send_ffn_wait")
    late = (0, 4)
    gs = [g["g_in_t"], g["g_out"]]
    *theirs, sall = _exchange_halves(late, gs, small, name="exchange_halves_late")
    parts_late = _chip_partial(late, gs, theirs, name="chip_partial_late", out_dtype=BF16)
    sent_late = _send_start(late, parts_late, name="send_late_start")
    big = {}

    def finish(ws, parts, got, specs, tag, after):
        shards = _exchange_reduced(ws, _chip_reduce(ws, parts, got, name="chip_reduce_" + tag, after=after),
                                   name="exchange_reduced_" + tag)
        for gw, (n, w, m, v, tr) in zip(shards, specs):
            view = (lambda a: a[0].T) if tr else (lambda a: a[0])
            back = (lambda a: a.T[None]) if tr else (lambda a: a[None])
            d_, m_, v_ = _adamw(view(w), gw, view(m), view(v), name="adamw_" + n)
            big[n] = (back(gw), back(d_), back(m_), back(v_))
        return d_

    last = finish(FFN_W, parts_ffn, got_ffn, (("w_gate", w_gate, m_w_gate, v_w_gate, True), ("w_up", w_up, m_w_up, v_w_up, True),
                                              ("w_down", w_down, m_w_down, v_w_down, False)), "ffn", sent_late["token"])
    parts_late, got_late = _send_wait(late, sent_late, last, name="send_late_wait")
    finish(late, parts_late, got_late, (("w_in", w_in, m_w_in, v_w_in, True), ("w_out", w_out, m_w_out, v_w_out, False)),
           "late", None)

    def place(a):
        return lax.dynamic_update_slice(jnp.zeros((3, D_FF), F32), a[0], (0, q * ccols))

    def small_pack(ws, cw):
        nm, ba, lbl, hg, sk, nf, cb, fn = ws
        return _pack_small(dict(norm_mix_w=nm, b_attn=ba, lb_logits=lbl, hg_norm_w=hg,
                                sinks=jnp.broadcast_to(sk.reshape(ATT_HEADS, 1), (ATT_HEADS, LANES)), norm_ffn_w=nf,
                                conv_w=cw, conv_b=cb, final_norm_w=fn))

    wp = small_pack((norm_mix_w, b_attn, lb_logits, hg_norm_w, sinks, norm_ffn_w, conv_b, final_norm_w), conv_w8[:3])
    mp = small_pack((m_norm_mix_w, m_b_attn, m_lb_logits, m_hg_norm_w, m_sinks, m_norm_ffn_w, m_conv_b, m_final_norm_w),
                    place(m_conv_w))
    vp = small_pack((v_norm_mix_w, v_b_attn, v_lb_logits, v_hg_norm_w, v_sinks, v_norm_ffn_w, v_conv_b, v_final_norm_w),
                    place(v_conv_w))
    outs = _small_update(sall, wp, mp, vp)
    loss = outs[4][0, 0]

    def small_out(pk, n, ref):
        if n == "sinks":
            return pk[SMALL_OFF[n]:SMALL_OFF[n] + ATT_HEADS, 0].reshape(ref.shape)
        if n == "conv_w":
            full = _unpack_small(pk, n, (3, D_FF))
            return lax.dynamic_slice(full, (0, q * ccols), (3, ccols))[None]
        return _unpack_small(pk, n, ref.shape)

    refs = dict(norm_mix_w=norm_mix_w, b_attn=b_attn, lb_logits=lb_logits, hg_norm_w=hg_norm_w, sinks=sinks,
                norm_ffn_w=norm_ffn_w, conv_w=conv_w, conv_b=conv_b, final_norm_w=final_norm_w)
    order = ("norm_mix_w", "w_in", "b_attn", "lb_logits", "hg_norm_w", "sinks", "w_out", "norm_ffn_w", "w_gate", "w_up",
             "conv_w", "conv_b", "w_down", "final_norm_w")
    res = [loss, dx[None]]
    for k in range(4):
        for n in order:
            res.append(big[n][k] if n in big else small_out(outs[k], n, refs[n]))
    return tuple(res)
```

```python
import functools
import math

import jax
import jax.numpy as jnp
from jax import lax
from jax.experimental import pallas as pl
from jax.experimental.pallas import tpu as pltpu

F32 = jnp.float32
BF16 = jnp.bfloat16

D_MODEL = 1024
HG_HEADS = 4
HG_DK = 128
HG_W = HG_HEADS * HG_DK
HG_CHUNK = 64
HG_SUB = 8
HG_CHUNKS_PER_STEP = 2
ATT_HEADS = 8
ATT_KV = 2
ATT_GROUP = ATT_HEADS // ATT_KV
ATT_HD = 64
ATT_BLOCK = 128
ATT_Q_W = ATT_HEADS * ATT_HD
ATT_KV_W = ATT_KV * ATT_HD
ATT_COLS = ATT_Q_W + 2 * ATT_KV_W
IN_COLS = 4 * HG_W + ATT_COLS
D_FF = 2816
EPS = 1e-6
ADAM_LR, ADAM_B1, ADAM_B2, ADAM_EPS, ADAM_WD, ADAM_STEP = 0.001, 0.9, 0.999, 1e-08, 0.01, 10
NEG = -1e30

V7X_VMEM_BYTES = 64 * 1024 * 1024
VMEM_LIMIT = 48 * 1024 * 1024
SUBLANES = 8

N_CHIPS = 4


def _cp(sem=None, **kw):
    return pltpu.CompilerParams(dimension_semantics=sem, vmem_limit_bytes=VMEM_LIMIT, **kw)


def _sds(shape, dtype):
    return jax.ShapeDtypeStruct(shape, dtype)


def _wspec(w):
    arr, rows, blk = w
    return pl.BlockSpec((rows, arr.shape[1]), lambda i: (blk, 0))


def _mm_nt(a, w, *, splits, out_dtype, name, after=None, tm=512):
    M, K = a.shape
    N = w[1]
    tm = min(tm, M)
    assert sum(splits) == N and M % tm == 0
    offs = [sum(splits[:i]) for i in range(len(splits))]
    n_in = 2 if after is None else 3

    def body(*refs):
        a_ref, w_ref = refs[0], refs[1]
        acc = lax.dot_general(a_ref[...], w_ref[...], (((1,), (1,)), ((), ())), preferred_element_type=F32)
        for o_ref, c0, n in zip(refs[n_in:], offs, splits):
            o_ref[...] = acc[:, c0:c0 + n].astype(out_dtype)

    in_specs = [pl.BlockSpec((tm, K), lambda i: (i, 0)), _wspec(w)]
    args = [a, w[0]]
    if after is not None:
        in_specs.append(pl.BlockSpec(memory_space=pl.ANY))
        args.append(after)
    outs = pl.pallas_call(
        body, name=name, grid=(M // tm,), in_specs=in_specs,
        out_specs=[pl.BlockSpec((tm, n), lambda i: (i, 0)) for n in splits],
        out_shape=[_sds((M, n), out_dtype) for n in splits],
        compiler_params=_cp(("parallel",)),
    )(*args)
    return outs


def _mm_nn(pieces, ws, *, name, out_dtype=F32, residual=None, epilogue=None, prologue=None, after=None,
           w_transposed=False, tm=512):
    pro_fn, pro_rows, pro_bc, pro_out = prologue or (None, [], [], None)
    if prologue is not None:
        assert pieces is None and len(ws) == 1
        pieces = [[pro_out]]
    M = pieces[0][0].shape[0]
    K = ws[0][1] if w_transposed else ws[0][0].shape[1]
    tm = min(tm, M)
    flat = [] if prologue is not None else [p for grp in pieces for p in grp]
    n_p = len(flat)
    n_w = len(ws)
    n_pr, n_pb = len(pro_rows), len(pro_bc)
    fn, row_ins, bc_ins, row_outs, acc_outs = epilogue or (None, [], [], [_sds((M, K), out_dtype)], [])
    if residual is not None:
        assert epilogue is None
        row_ins = [residual]
    n_r, n_b, n_o = len(row_ins), len(bc_ins), len(row_outs)
    lead = [] if after is None else [after]

    def body(*refs):
        refs = refs[len(lead):]
        p_refs = refs[:n_p]
        w_refs = refs[n_p:n_p + n_w]
        extra = [r[...] for r in refs[n_p + n_w:n_p + n_w + n_r + n_b]]
        base = n_p + n_w + n_r + n_b
        pro = [r[...] for r in refs[base:base + n_pr + n_pb]]
        base += n_pr + n_pb
        o_refs = refs[base:base + n_o]
        a_refs = refs[base + n_o:base + n_o + len(acc_outs)]
        if pro_fn is not None:
            lhs = pro_fn(*pro).astype(pro_out.dtype)
            refs[-1][...] = lhs
            tiles = [lhs]
        else:
            tiles = [r[...] for r in p_refs]
        acc = None
        k = 0
        for gi, grp in enumerate(pieces):
            c0 = 0
            for p in grp:
                n = p.shape[1]
                if w_transposed:
                    t = lax.dot_general(tiles[k], w_refs[gi][...], (((1,), (1,)), ((), ())), preferred_element_type=F32)
                else:
                    t = jnp.dot(tiles[k], w_refs[gi][c0:c0 + n, :], preferred_element_type=F32)
                acc = t if acc is None else acc + t
                c0 += n
                k += 1
        if fn is None:
            res = (acc + extra[0] if residual is not None else acc,)
        else:
            res = fn(acc, *extra)
        for o_ref, val in zip(o_refs, res[:n_o]):
            o_ref[...] = val.astype(o_ref.dtype)
        if acc_outs:
            @pl.when(pl.program_id(0) == 0)
            def _():
                for a_ref in a_refs:
                    a_ref[...] = jnp.zeros_like(a_ref)
            for a_ref, val in zip(a_refs, res[n_o:]):
                a_ref[...] += val

    in_specs = [pl.BlockSpec((tm, p.shape[1]), lambda i: (i, 0)) for p in flat]
    in_specs += [_wspec(w) for w in ws]
    in_specs += [pl.BlockSpec((tm, r.shape[1]), lambda i: (i, 0)) for r in row_ins]
    in_specs += [pl.BlockSpec(b.shape, lambda i: (0, 0)) for b in bc_ins]
    in_specs += [pl.BlockSpec((tm, r.shape[1]), lambda i: (i, 0)) for r in pro_rows]
    in_specs += [pl.BlockSpec(b.shape, lambda i: (0, 0)) for b in pro_bc]
    out_specs = [pl.BlockSpec((tm, s.shape[1]), lambda i: (i, 0)) for s in row_outs]
    out_specs += [pl.BlockSpec(s.shape, lambda i: (0, 0)) for s in acc_outs]
    pro_outs = [] if prologue is None else [pro_out]
    out_specs += [pl.BlockSpec((tm, s.shape[1]), lambda i: (i, 0)) for s in pro_outs]
    outs = pl.pallas_call(
        body, name=name, grid=(M // tm,), in_specs=[pl.BlockSpec(memory_space=pl.ANY)] * len(lead) + in_specs,
        out_specs=out_specs, out_shape=list(row_outs) + list(acc_outs) + pro_outs,
        compiler_params=_cp(("arbitrary",) if acc_outs else ("parallel",)),
    )(*lead, *flat, *[w[0] for w in ws], *row_ins, *bc_ins, *pro_rows, *pro_bc)
    return outs if (epilogue is not None or prologue is not None) else outs[0]


def _mm_tn(pieces, x, *, name, tt=1024):
    M, K = x.shape
    tt = min(tt, M)
    ns = [p.shape[1] for p in pieces]
    offs = [sum(ns[:i]) for i in range(len(ns))]
    N = sum(ns)
    n_p = len(pieces)

    def body(*refs):
        p_refs = refs[:n_p]
        x_ref = refs[n_p]
        o_ref = refs[n_p + 1]

        @pl.when(pl.program_id(0) == 0)
        def _():
            o_ref[...] = jnp.zeros_like(o_ref)

        xv = x_ref[...]
        for p_ref, c0, n in zip(p_refs, offs, ns):
            o_ref[c0:c0 + n, :] += lax.dot_general(p_ref[...], xv, (((0,), (0,)), ((), ())),
                                                    preferred_element_type=F32)

    in_specs = [pl.BlockSpec((tt, n), lambda i: (i, 0)) for n in ns]
    in_specs.append(pl.BlockSpec((tt, K), lambda i: (i, 0)))
    return pl.pallas_call(
        body, name=name, grid=(M // tt,), in_specs=in_specs,
        out_specs=pl.BlockSpec((N, K), lambda i: (0, 0)),
        out_shape=_sds((N, K), F32),
        compiler_params=_cp(("arbitrary",)),
    )(*pieces, x)


def _rms_fwd(xf, w):
    inv = lax.rsqrt(jnp.mean(xf * xf, axis=-1, keepdims=True) + EPS)
    return xf * inv * w


def _rms_bwd(xf, w, dy):
    inv = lax.rsqrt(jnp.mean(xf * xf, axis=-1, keepdims=True) + EPS)
    xhat = xf * inv
    dxhat = dy * w
    dx = inv * (dxhat - xhat * jnp.mean(dxhat * xhat, axis=-1, keepdims=True))
    dw = jnp.sum(dy * xhat, axis=0, keepdims=True)
    return dx, dw


def _sigmoid(x):
    return 1.0 / (1.0 + jnp.exp(-x))


def _rowwise(fn, row_ins, bc_ins, row_outs, acc_outs, *, name, tm=256, after=None):
    M = row_outs[0].shape[0] if row_outs else row_ins[0][0].shape[0]
    assert M % tm == 0 and tm % SUBLANES == 0, (name, M, tm)
    n_r, n_b, n_o, n_a = len(row_ins), len(bc_ins), len(row_outs), len(acc_outs)
    n_after = 0 if after is None else 1

    def body(*refs):
        refs = refs[n_after:]
        ins = [r[...] for r in refs[:n_r + n_b]]
        o_refs = refs[n_r + n_b:n_r + n_b + n_o]
        a_refs = refs[n_r + n_b + n_o:]
        res = fn(*ins)
        for o_ref, val in zip(o_refs, res[:n_o]):
            o_ref[...] = val.astype(o_ref.dtype)
        if n_a:
            @pl.when(pl.program_id(0) == 0)
            def _():
                for a_ref in a_refs:
                    a_ref[...] = jnp.zeros_like(a_ref)
            for a_ref, val in zip(a_refs, res[n_o:]):
                a_ref[...] += val

    in_specs = [pl.BlockSpec((tm, cw), functools.partial(lambda i, cb, r0: (i + r0, cb), cb=cb, r0=r0))
                for (_, cw, cb, r0) in row_ins]
    in_specs += [pl.BlockSpec(b.shape, lambda i: (0, 0)) for b in bc_ins]
    out_specs = [pl.BlockSpec((tm, s.shape[1]), lambda i: (i, 0)) for s in row_outs]
    out_specs += [pl.BlockSpec(s.shape, lambda i: (0, 0)) for s in acc_outs]
    if n_after:
        in_specs = [pl.BlockSpec(memory_space=pl.ANY)] + in_specs
    return pl.pallas_call(
        body, name=name, grid=(M // tm,), in_specs=in_specs, out_specs=out_specs,
        out_shape=list(row_outs) + list(acc_outs),
        compiler_params=_cp(("arbitrary",) if n_a else ("parallel",)),
    )(*([after] if n_after else []), *[r[0] for r in row_ins], *bc_ins)


def _full(a, first_row_block=0):
    return (a, a.shape[1], 0, first_row_block)


def _conv_rows(ext, w_ref_val, lo):
    s1 = pltpu.roll(ext, 1, 0)
    s2 = pltpu.roll(ext, 2, 0)
    y = w_ref_val[0:1, :] * s2 + w_ref_val[1:2, :] * s1 + w_ref_val[2:3, :] * ext
    return y[SUBLANES:, :]


def _convact_fwd(gp, up, conv_w8, conv_b, *, name, tr=512, tc=1408):
    T, C = gp.shape
    tr = min(tr, T)
    hb = tr // SUBLANES

    def body(gp_ref, gph_ref, up_ref, w_ref, b_ref, act_ref):
        i = pl.program_id(1)
        halo = jnp.where(i > 0, gph_ref[...], 0.0)
        ext = jnp.concatenate([halo, gp_ref[...]], axis=0)
        gate = _conv_rows(ext, w_ref[...], 0) + b_ref[...]
        act_ref[...] = (gate * _sigmoid(gate) * up_ref[...]).astype(act_ref.dtype)

    return pl.pallas_call(
        body, name=name, grid=(C // tc, T // tr),
        in_specs=[pl.BlockSpec((tr, tc), lambda j, i: (i, j)),
                  pl.BlockSpec((SUBLANES, tc), lambda j, i: (jnp.maximum(i * hb - 1, 0), j)),
                  pl.BlockSpec((tr, tc), lambda j, i: (i, j)),
                  pl.BlockSpec((SUBLANES, tc), lambda j, i: (0, j)),
                  pl.BlockSpec((1, tc), lambda j, i: (0, j))],
        out_specs=pl.BlockSpec((tr, tc), lambda j, i: (i, j)),
        out_shape=_sds((T, C), BF16),
        compiler_params=_cp(("parallel", "parallel")),
    )(gp, gp, up, conv_w8, conv_b)


def _convact_bwd(gp, up, dact, conv_w8, conv_b, *, name, tr=256, tc=1408):
    T, C = gp.shape
    tr = min(tr, T)
    hb = tr // SUBLANES
    nr = T // tr

    def body(gp_ref, gpp_ref, gpn_ref, up_ref, upn_ref, da_ref, dan_ref, w_ref, b_ref,
             dgp_ref, dup_ref, dw_ref, db_ref):
        i = pl.program_id(1)
        w = w_ref[...]
        prev = jnp.where(i > 0, gpp_ref[...], 0.0)
        last = i == nr - 1
        gp_ext = jnp.concatenate([prev, gp_ref[...], gpn_ref[...]], axis=0)
        gate = _conv_rows(gp_ext, w, 0) + b_ref[...]
        up_e = jnp.concatenate([up_ref[...], upn_ref[...]], axis=0)
        da_e = jnp.concatenate([da_ref[...], dan_ref[...]], axis=0)
        row = lax.broadcasted_iota(jnp.int32, gate.shape, 0)
        valid = jnp.logical_or(row < tr, jnp.logical_not(last))
        sg = _sigmoid(gate)
        silu = gate * sg
        dgate = jnp.where(valid, da_e * up_e * (sg * (1.0 + gate * (1.0 - sg))), 0.0)
        dup_ref[...] = (da_e[:tr] * silu[:tr]).astype(dup_ref.dtype)
        n = tr + SUBLANES
        g1 = pltpu.roll(dgate, n - 1, 0)
        g2 = pltpu.roll(dgate, n - 2, 0)
        dgp = w[2:3, :] * dgate + w[1:2, :] * g1 + w[0:1, :] * g2
        dgp_ref[...] = dgp[:tr].astype(dgp_ref.dtype)
        gpc = gp_ref[...]
        dw0 = jnp.sum(gpc * g2[:tr], axis=0, keepdims=True)
        dw1 = jnp.sum(gpc * g1[:tr], axis=0, keepdims=True)
        dw2 = jnp.sum(gpc * dgate[:tr], axis=0, keepdims=True)
        dbv = jnp.sum(dgate[:tr], axis=0, keepdims=True)
        z = jnp.zeros((SUBLANES - 3, gpc.shape[1]), F32)

        @pl.when(i == 0)
        def _():
            dw_ref[...] = jnp.zeros_like(dw_ref)
            db_ref[...] = jnp.zeros_like(db_ref)

        dw_ref[...] += jnp.concatenate([dw0, dw1, dw2, z], axis=0)
        db_ref[...] += dbv

    cur = pl.BlockSpec((tr, tc), lambda j, i: (i, j))
    prv = pl.BlockSpec((SUBLANES, tc), lambda j, i: (jnp.maximum(i * hb - 1, 0), j))
    nxt = pl.BlockSpec((SUBLANES, tc), lambda j, i: (jnp.minimum((i + 1) * hb, T // SUBLANES - 1), j))
    return pl.pallas_call(
        body, name=name, grid=(C // tc, nr),
        in_specs=[cur, prv, nxt, cur, nxt, cur, nxt,
                  pl.BlockSpec((SUBLANES, tc), lambda j, i: (0, j)),
                  pl.BlockSpec((1, tc), lambda j, i: (0, j))],
        out_specs=[cur, cur,
                   pl.BlockSpec((SUBLANES, tc), lambda j, i: (0, j)),
                   pl.BlockSpec((1, tc), lambda j, i: (0, j))],
        out_shape=[_sds((T, C), BF16), _sds((T, C), BF16), _sds((SUBLANES, C), F32), _sds((1, C), F32)],
        compiler_params=_cp(("parallel", "arbitrary")),
    )(gp, gp, gp, up, up, dact, dact, conv_w8, conv_b)


def _cumsum_rows(x):
    n = x.shape[0]
    row = lax.broadcasted_iota(jnp.int32, x.shape, 0)
    s = 1
    while s < n:
        x = x + jnp.where(row >= s, pltpu.roll(x, s, 0), 0.0)
        s *= 2
    return x


def _rcumsum_rows(x):
    n = x.shape[0]
    row = lax.broadcasted_iota(jnp.int32, x.shape, 0)
    s = 1
    while s < n:
        x = x + jnp.where(row < n - s, pltpu.roll(x, n - s, 0), 0.0)
        s *= 2
    return x


def _dot_nt(a, b):
    return lax.dot_general(a.astype(BF16), b.astype(BF16), (((1,), (1,)), ((), ())), preferred_element_type=F32)


def _dot_tn(a, b):
    return lax.dot_general(a.astype(BF16), b.astype(BF16), (((0,), (0,)), ((), ())), preferred_element_type=F32)


def _dot_nn(a, b):
    return jnp.dot(a.astype(BF16), b.astype(BF16), preferred_element_type=F32)


def _dot3(a, b, contract):
    def split(x):
        hi = x.astype(BF16)
        return hi, (x - hi.astype(F32)).astype(BF16)

    a_hi, a_lo = split(a)
    b_hi, b_lo = split(b)
    dot = lambda x, y: lax.dot_general(x, y, (contract, ((), ())), preferred_element_type=F32)
    return dot(a_hi, b_hi) + (dot(a_hi, b_lo) + dot(a_lo, b_hi))


NT, TN, NN = ((1,), (1,)), ((0,), (0,)), ((1,), (0,))


def _hg_gates(hq, hf, lbv):
    sig = _sigmoid(hf)
    f = lbv + (1.0 - lbv) * sig
    return sig, f, jnp.log(f), 1.0 - f, hq * (HG_DK ** -0.5)


def _hg_sel_rows(ref, sp):
    return jnp.concatenate(
        [jnp.broadcast_to(ref[pl.ds(HG_SUB * i + sp, 1), :], (HG_SUB, HG_DK)) for i in range(HG_CHUNK // HG_SUB)], axis=0)


def _hg_masks():
    C = HG_CHUNK
    row = lax.broadcasted_iota(jnp.int32, (C, C), 0)
    col = lax.broadcasted_iota(jnp.int32, (C, C), 1)
    d = col - (row // HG_SUB) * HG_SUB
    tmod = row % HG_SUB
    diag_valid = jnp.logical_and(d >= 0, d <= tmod)
    return row, col, d, diag_valid


def _hg_scores(q, k, b, b_sc, k_sc):
    C, S = HG_CHUNK, HG_SUB
    row, col, d, diag_valid = _hg_masks()
    blocks = [jnp.zeros((S, C), F32)]
    for i in range(1, C // S):
        r = b_sc[pl.ds(S * i - 1, 1), :]
        qi = q[S * i:S * (i + 1)] * jnp.exp(b[S * i:S * (i + 1)] - r)
        kk = k * jnp.exp(jnp.minimum(r - b, 0.0))
        blocks.append(_dot_nt(qi, kk))
    a_off = jnp.where(col < (row // S) * S, jnp.concatenate(blocks, axis=0), 0.0)
    a_d = jnp.zeros((C, C), F32)
    for sp in range(S):
        bs = _hg_sel_rows(b_sc, sp)
        ks = _hg_sel_rows(k_sc, sp)
        e = jnp.exp(jnp.minimum(b - bs, 0.0))
        colv = jnp.sum(q * ks * e, axis=-1, keepdims=True)
        a_d = jnp.where(d == sp, colv, a_d)
    return a_off + jnp.where(diag_valid, a_d, 0.0)


def _hg_prep(hq_v, hf_v, lbv, b_sc, k_sc):
    sig, f, g, k, q = _hg_gates(hq_v, hf_v, lbv)
    b = _cumsum_rows(g)
    b_sc[...] = b
    k_sc[...] = k
    return sig, f, k, q, b, b_sc[pl.ds(HG_CHUNK - 1, 1), :]


def _hgrn_fwd(hq, hf, hi, lb, *, name):
    T = hq.shape[0]
    C, H, K = HG_CHUNK, HG_HEADS, HG_DK
    NC = T // C

    def body(hq_ref, hf_ref, hi_ref, lb_ref, o_ref, st_ref, s_sc, b_sc, k_sc):
        @pl.when(pl.program_id(0) == 0)
        def _():
            s_sc[...] = jnp.zeros_like(s_sc)

        st_all = s_sc[...]
        for j in range(P):
            rows = slice(C * j, C * (j + 1))
            st_ref[j] = st_all
            outs, news = [], []
            for h in range(H):
                sl = slice(K * h, K * (h + 1))
                _, _, k, q, b, bc = _hg_prep(hq_ref[rows, sl], hf_ref[rows, sl], lb_ref[:, sl], b_sc.at[j, h], k_sc.at[j, h])
                v = hi_ref[rows, sl]
                st0 = st_all[:, sl]
                a = _hg_scores(q, k, b, b_sc.at[j, h], k_sc.at[j, h])
                outs.append(_dot_nn(a, v) + _dot_nt(q * jnp.exp(b), st0))
                news.append(st0 * jnp.exp(bc) + _dot_tn(v, k * jnp.exp(bc - b)))
            o_ref[rows, :] = jnp.concatenate(outs, axis=1)
            st_all = jnp.concatenate(news, axis=1)
        s_sc[...] = st_all

    P = HG_CHUNKS_PER_STEP
    blk = pl.BlockSpec((P * C, H * K), lambda c: (c, 0))
    return pl.pallas_call(
        body, name=name, grid=(NC // P,),
        in_specs=[blk, blk, blk, pl.BlockSpec((1, H * K), lambda c: (0, 0))],
        out_specs=[blk, pl.BlockSpec((P, K, H * K), lambda c: (c, 0, 0))],
        out_shape=[_sds((T, H * K), F32), _sds((NC, K, H * K), F32)],
        scratch_shapes=[pltpu.VMEM((K, H * K), F32), pltpu.VMEM((P, H, C, K), F32), pltpu.VMEM((P, H, C, K), F32)],
        compiler_params=_cp(("arbitrary",)),
    )(hq, hf, hi, lb)


def _hgrn_bwd(hq, hf, hi, lb, states, do, *, name):
    T = hq.shape[0]
    C, H, K, S = HG_CHUNK, HG_HEADS, HG_DK, HG_SUB
    NC = T // C

    def intra_slow(q, k, b, da, b_sc, k_sc):
        row, col, d, diag_valid = _hg_masks()
        a_blocks = [jnp.zeros((S, C), F32)]
        dq_blocks = [jnp.zeros((S, K), F32)]
        dk = jnp.zeros((C, K), F32)
        for i in range(1, C // S):
            r = b_sc[pl.ds(S * i - 1, 1), :]
            eq = jnp.exp(b[S * i:S * (i + 1)] - r)
            ek = jnp.exp(jnp.minimum(r - b, 0.0))
            qi = q[S * i:S * (i + 1)] * eq
            kk = k * ek
            a_blocks.append(_dot_nt(qi, kk))
            dai = jnp.where(col[S * i:S * (i + 1)] < S * i, da[S * i:S * (i + 1)], 0.0)
            dq_blocks.append(_dot_nn(dai, kk) * eq)
            dk = dk + _dot_tn(dai, qi) * ek
        dq = jnp.concatenate(dq_blocks, axis=0)
        a_off = jnp.where(col < (row // S) * S, jnp.concatenate(a_blocks, axis=0), 0.0)
        same_blk = (row // S == col // S).astype(BF16)
        tmod = (lax.broadcasted_iota(jnp.int32, (C, K), 0)) % S
        a_d = jnp.zeros((C, C), F32)
        for sp in range(S):
            bs = _hg_sel_rows(b_sc, sp)
            ks = _hg_sel_rows(k_sc, sp)
            e = jnp.where(tmod >= sp, jnp.exp(jnp.minimum(b - bs, 0.0)), 0.0)
            eks = e * ks
            a_d = jnp.where(d == sp, jnp.sum(q * eks, axis=-1, keepdims=True), a_d)
            dacol = jnp.sum(jnp.where(d == sp, da, 0.0), axis=-1, keepdims=True)
            dq = dq + dacol * eks
            wq = dacol * e * q
            wq_hi = wq.astype(BF16)
            wq_lo = (wq - wq_hi.astype(F32)).astype(BF16)
            blk_sum = (jnp.dot(same_blk, wq_hi, preferred_element_type=F32)
                       + jnp.dot(same_blk, wq_lo, preferred_element_type=F32))
            dk = dk + jnp.where(tmod == sp, blk_sum, 0.0)
        return a_off + jnp.where(diag_valid, a_d, 0.0), dq, dk

    def one_head(pre, v, lbv, st0, dst1, dout, b_sc, k_sc):
        sig, f, k, q, b, bc = pre
        ebc = jnp.exp(bc)
        eb = jnp.exp(b)
        ekb = jnp.exp(bc - b)
        qt = q * eb
        kb = k * ekb
        row = lax.broadcasted_iota(jnp.int32, (C, C), 0)
        col = lax.broadcasted_iota(jnp.int32, (C, C), 1)
        da = jnp.where(col <= row, _dot_nt(dout, v), 0.0)
        dkb = _dot_nn(v, dst1)
        new_ds = _dot_tn(dout, qt) + dst1 * ebc
        a, dq_i, dk_i = intra_slow(q, k, b, da, b_sc, k_sc)
        dq = _dot_nn(dout, st0) * eb + dq_i
        dk = dkb * ekb + dk_i
        dv = _dot_tn(a, dout) + _dot_nt(kb, dst1)
        extra = jnp.sum(dkb * kb, axis=0, keepdims=True) + ebc * jnp.sum(st0 * dst1, axis=0, keepdims=True)
        rowk = lax.broadcasted_iota(jnp.int32, (C, K), 0)
        db = q * dq - k * dk + jnp.where(rowk == C - 1, extra, 0.0)
        dg = _rcumsum_rows(db)
        df = dg / f - dk
        return (dq * (K ** -0.5), df * (1.0 - lbv) * sig * (1.0 - sig), dv,
                jnp.sum(df * (1.0 - sig), axis=0, keepdims=True), new_ds)

    def body(hq_ref, hf_ref, hi_ref, lb_ref, st_ref, do_ref, dq_ref, dhf_ref, dv_ref, dlb_ref, ds_sc, b_sc, k_sc):
        @pl.when(pl.program_id(0) == 0)
        def _():
            ds_sc[...] = jnp.zeros_like(ds_sc)
            dlb_ref[...] = jnp.zeros_like(dlb_ref)

        ds_all = ds_sc[...]
        dlb = jnp.zeros((1, H * K), F32)
        for j in reversed(range(P)):
            rows = slice(C * j, C * (j + 1))
            st_all = st_ref[j]
            res = []
            for h in range(H):
                sl = slice(K * h, K * (h + 1))
                pre = _hg_prep(hq_ref[rows, sl], hf_ref[rows, sl], lb_ref[:, sl], b_sc.at[j, h], k_sc.at[j, h])
                res.append(one_head(pre, hi_ref[rows, sl], lb_ref[:, sl], st_all[:, sl], ds_all[:, sl], do_ref[rows, sl],
                                    b_sc.at[j, h], k_sc.at[j, h]))
            cat = lambda i: jnp.concatenate([r[i] for r in res], axis=1)
            dq_ref[rows, :] = cat(0).astype(dq_ref.dtype)
            dhf_ref[rows, :] = cat(1).astype(dhf_ref.dtype)
            dv_ref[rows, :] = cat(2).astype(dv_ref.dtype)
            dlb = dlb + cat(3)
            ds_all = cat(4)
        dlb_ref[...] += dlb
        ds_sc[...] = ds_all

    P = HG_CHUNKS_PER_STEP
    NS = NC // P
    blk = pl.BlockSpec((P * C, H * K), lambda c: (NS - 1 - c, 0))
    par = pl.BlockSpec((1, H * K), lambda c: (0, 0))
    return pl.pallas_call(
        body, name=name, grid=(NS,),
        in_specs=[blk, blk, blk, par, pl.BlockSpec((P, K, H * K), lambda c: (NS - 1 - c, 0, 0)), blk],
        out_specs=[blk, blk, blk, par],
        out_shape=[_sds((T, H * K), BF16)] * 3 + [_sds((1, H * K), F32)],
        scratch_shapes=[pltpu.VMEM((K, H * K), F32), pltpu.VMEM((P, H, C, K), F32), pltpu.VMEM((P, H, C, K), F32)],
        compiler_params=_cp(("arbitrary",)),
    )(hq, hf, hi, lb, states, do)


ATT_STACK = ATT_GROUP


def _att_valid(n):
    R, B = ATT_STACK * ATT_BLOCK, ATT_BLOCK
    j = lax.broadcasted_iota(jnp.int32, (2 * B, R), 0)
    t = lax.broadcasted_iota(jnp.int32, (2 * B, R), 1) % B
    dist = t + B - j
    first_key = jnp.where(n > 0, 0, B)
    return jnp.logical_and(jnp.logical_and(dist >= 0, dist < B), j >= first_key)


def _att_load(cur_ref, prev_ref, ba_ref, h0):
    hd = ATT_HD
    kv = h0 // ATT_GROUP
    def cols(ref, c0):
        return ref[:, c0:c0 + hd] + ba_ref[:, c0:c0 + hd]
    qs = jnp.concatenate([cols(cur_ref, hd * (h0 + g)) for g in range(ATT_STACK)], axis=0)
    kc = jnp.concatenate([cols(prev_ref, ATT_Q_W + hd * kv), cols(cur_ref, ATT_Q_W + hd * kv)], axis=0)
    vc = jnp.concatenate([cols(prev_ref, ATT_Q_W + ATT_KV_W + hd * kv), cols(cur_ref, ATT_Q_W + ATT_KV_W + hd * kv)], axis=0)
    return qs, kc, vc


def _att_probs(qs, kc, valid, sink_ref, h0):
    scale = 1.0 / math.sqrt(ATT_HD)
    s = jnp.where(valid, _dot_nt(kc, qs) * scale, NEG)
    sink = jnp.concatenate([jnp.full((1, ATT_BLOCK), sink_ref[0, h0 + g], F32) for g in range(ATT_STACK)], axis=1)
    m = jnp.maximum(jnp.max(s, axis=0, keepdims=True), sink)
    p = jnp.exp(s - m)
    ps = jnp.exp(sink - m)
    inv = 1.0 / (jnp.sum(p, axis=0, keepdims=True) + ps)
    return p * inv, ps * inv


def _attn_fwd(att, b_attn, sinks, *, name, after=None):
    T = att.shape[0]
    B = ATT_BLOCK
    NB = T // B
    lead = [] if after is None else [after]

    def body(*refs):
        sink_ref, cur_ref, prev_ref, ba_ref, o_ref = refs[len(lead):]
        valid = _att_valid(pl.program_id(0))
        outs = []
        for h0 in range(0, ATT_HEADS, ATT_STACK):
            qs, kc, vc = _att_load(cur_ref, prev_ref, ba_ref, h0)
            prob, _ = _att_probs(qs, kc, valid, sink_ref, h0)
            o = _dot_tn(prob, vc)
            outs += [o[B * g:B * (g + 1)] for g in range(ATT_STACK)]
        o_ref[...] = jnp.concatenate(outs, axis=1)

    return pl.pallas_call(
        body, name=name, grid=(NB,),
        in_specs=[pl.BlockSpec(memory_space=pl.ANY)] * len(lead) + [
            pl.BlockSpec(memory_space=pltpu.SMEM),
            pl.BlockSpec((B, ATT_COLS), lambda n: (n, 0)),
            pl.BlockSpec((B, ATT_COLS), lambda n: (jnp.maximum(n - 1, 0), 0)),
            pl.BlockSpec((1, ATT_COLS), lambda n: (0, 0))],
        out_specs=pl.BlockSpec((B, ATT_Q_W), lambda n: (n, 0)),
        out_shape=_sds((T, ATT_Q_W), F32),
        compiler_params=_cp(("parallel",)),
    )(*lead, sinks, att, att, b_attn)


def _attn_bwd(att, b_attn, sinks, dmix, *, name):
    T = att.shape[0]
    B, hd = ATT_BLOCK, ATT_HD
    NB = T // B
    scale = 1.0 / math.sqrt(hd)

    def body(sink_ref, cur_ref, prev_ref, ba_ref, do_ref, daq_ref, dakv_ref, dsink_ref, dbq_ref, dbkv_ref, carry_sc):
        n = pl.program_id(0)

        @pl.when(n == 0)
        def _():
            carry_sc[...] = jnp.zeros_like(carry_sc)
            dsink_ref[...] = jnp.zeros_like(dsink_ref)
            dbq_ref[...] = jnp.zeros_like(dbq_ref)
            dbkv_ref[...] = jnp.zeros_like(dbkv_ref)

        @pl.when(n < NB)
        def _():
            valid = _att_valid(n)
            hrow = lax.broadcasted_iota(jnp.int32, (SUBLANES, 128), 0)
            dsink = jnp.zeros((SUBLANES, 128), F32)
            dqs = []
            dks = [jnp.zeros((2 * B, hd), F32)] * ATT_KV
            dvs = [jnp.zeros((2 * B, hd), F32)] * ATT_KV
            for h0 in range(0, ATT_HEADS, ATT_STACK):
                kv = h0 // ATT_GROUP
                qs, kc, vc = _att_load(cur_ref, prev_ref, ba_ref, h0)
                prob, psink = _att_probs(qs, kc, valid, sink_ref, h0)
                dout = jnp.concatenate([do_ref[:, hd * (h0 + g):hd * (h0 + g + 1)] for g in range(ATT_STACK)], axis=0)
                dp = _dot_nt(vc, dout)
                delta = jnp.sum(prob * dp, axis=0, keepdims=True)
                dsc = prob * (dp - delta) * scale
                dq = _dot_tn(dsc, kc)
                dks[kv] = dks[kv] + _dot_nn(dsc, qs)
                dvs[kv] = dvs[kv] + _dot_nn(prob, dout)
                dsk = psink * delta
                for g in range(ATT_STACK):
                    dqs.append(dq[B * g:B * (g + 1)])
                    tot = jnp.sum(dsk[:, B * g:B * (g + 1)], axis=1, keepdims=True)
                    dsink = dsink - jnp.where(hrow == h0 + g, tot, 0.0)
            daq = jnp.concatenate(dqs, axis=1).astype(daq_ref.dtype)
            daq_ref[...] = daq
            dsink_ref[...] += dsink
            dbq_ref[...] += jnp.sum(daq.astype(F32), axis=0, keepdims=True)
            done = carry_sc[...] + jnp.concatenate([d[:B] for d in dks + dvs], axis=1)
            dakv_ref[...] = done.astype(dakv_ref.dtype)
            dbkv_ref[...] += jnp.sum(done.astype(dakv_ref.dtype).astype(F32), axis=0, keepdims=True)
            carry_sc[...] = jnp.concatenate([d[B:] for d in dks + dvs], axis=1)

        @pl.when(n == NB)
        def _():
            done = carry_sc[...]
            dakv_ref[...] = done.astype(dakv_ref.dtype)
            dbkv_ref[...] += jnp.sum(done.astype(dakv_ref.dtype).astype(F32), axis=0, keepdims=True)

    cl = lambda n: jnp.minimum(n, NB - 1)
    return pl.pallas_call(
        body, name=name, grid=(NB + 1,),
        in_specs=[pl.BlockSpec(memory_space=pltpu.SMEM),
                  pl.BlockSpec((B, ATT_COLS), lambda n: (cl(n), 0)),
                  pl.BlockSpec((B, ATT_COLS), lambda n: (jnp.maximum(cl(n) - 1, 0), 0)),
                  pl.BlockSpec((1, ATT_COLS), lambda n: (0, 0)),
                  pl.BlockSpec((B, ATT_Q_W), lambda n: (cl(n), 0))],
        out_specs=[pl.BlockSpec((B, ATT_Q_W), lambda n: (cl(n), 0)),
                   pl.BlockSpec((B, 2 * ATT_KV_W), lambda n: (jnp.maximum(n - 1, 0), 0)),
                   pl.BlockSpec((SUBLANES, 128), lambda n: (0, 0)),
                   pl.BlockSpec((1, ATT_Q_W), lambda n: (0, 0)),
                   pl.BlockSpec((1, 2 * ATT_KV_W), lambda n: (0, 0))],
        out_shape=[_sds((T, ATT_Q_W), BF16), _sds((T, 2 * ATT_KV_W), BF16), _sds((SUBLANES, 128), F32),
                   _sds((1, ATT_Q_W), F32), _sds((1, 2 * ATT_KV_W), F32)],
        scratch_shapes=[pltpu.VMEM((B, 2 * ATT_KV_W), F32)],
        compiler_params=_cp(("arbitrary",)),
    )(sinks, att, att, b_attn, dmix)


def _silu_and_grad(x):
    sg = _sigmoid(x)
    return x * sg, sg * (1.0 + x * (1.0 - sg))


def _mix_fwd_fn(o_raw, hg, o_att, hgw):
    outs = []
    for h in range(HG_HEADS):
        sl = slice(HG_DK * h, HG_DK * (h + 1))
        silu, _ = _silu_and_grad(hg[:, sl])
        outs.append(_rms_fwd(o_raw[:, sl], hgw) * silu)
    outs.append(o_att)
    return (jnp.concatenate(outs, axis=1),)


def _mix_bwd_fn(o_raw, hg, dmix, hgw):
    dos, dhgs = [], []
    dw = jnp.zeros((1, HG_DK), F32)
    for h in range(HG_HEADS):
        sl = slice(HG_DK * h, HG_DK * (h + 1))
        silu, dsilu = _silu_and_grad(hg[:, sl])
        dy = dmix[:, sl]
        dhgs.append(dy * _rms_fwd(o_raw[:, sl], hgw) * dsilu)
        dx, dwh = _rms_bwd(o_raw[:, sl], hgw, dy * silu)
        dos.append(dx)
        dw = dw + dwh
    return jnp.concatenate(dos, axis=1), jnp.concatenate(dhgs, axis=1), dw


def _final_fn(h2, tgt, wf):
    d = h2.shape[1]
    err = _rms_fwd(h2, wf) - tgt
    loss_cols = (0.5 / d) * jnp.sum(err * err, axis=0, keepdims=True)
    dh2, dwf = _rms_bwd(h2, wf, err * (1.0 / d))
    return dh2, dh2, loss_cols, dwf


class _NoExchange:
    def __init__(self, weights):
        self.weights = weights

    def start(self):
        return None

    def w_in(self, after):
        return self.weights["w_in_t"]

    def mid(self, after):
        return None

    def rest(self, after):
        return self.weights

    def ffn_grads(self, gs):
        return None

    def ffn_grads_send(self, after):
        return None


def _local_step(x, tgt, p, ex):
    T, D = x.shape
    row = lambda n, dt: _sds((T, n), dt)
    acc = lambda n: _sds((1, n), F32)

    (u,) = _rowwise(lambda xv, w: (_rms_fwd(xv, w),), [_full(x)], [p["norm_mix_w"]], [row(D, BF16)], [], name="rms_mix",
                    after=ex.start())
    p = dict(p, w_in_t=ex.w_in(u))
    hq, hf, hi, hg, att = _mm_nt(u, p["w_in_t"], splits=[HG_W] * 4 + [ATT_COLS], out_dtype=F32, name="in_proj")
    o_raw, states = _hgrn_fwd(hq, hf, hi, p["lb"], name="hgrn_fwd")
    o_att = _attn_fwd(att, p["b_attn"], p["sinks"], name="attn_fwd", after=ex.mid(o_raw))
    p = dict(p, **ex.rest(o_att))
    def out_epilogue(prod, xv, w):
        h1v = prod + xv
        return h1v, _rms_fwd(h1v, w)

    h1, v, mix = _mm_nn(None, [p["w_out"]], name="mix_out_proj",
                        prologue=(lambda *a: _mix_fwd_fn(*a)[0], [o_raw, hg, o_att], [p["hg_norm_w"]], row(D, BF16)),
                        epilogue=(out_epilogue, [x], [p["norm_ffn_w"]], [row(D, F32), row(D, BF16)], []))
    (gp,) = _mm_nt(v, p["w_gate_t"], splits=[D_FF], out_dtype=F32, name="gate_proj")
    (up,) = _mm_nt(v, p["w_up_t"], splits=[D_FF], out_dtype=F32, name="up_proj")
    act = _convact_fwd(gp, up, p["conv_w8"], p["conv_b"], name="convact_fwd")
    def down_epilogue(prod, h1v, tgtv, wf):
        return _final_fn(prod + h1v, tgtv, wf)

    dh2, dh2_b, loss_cols, d_final = _mm_nn(
        [[act]], [p["w_down"]], name="down_proj_loss",
        epilogue=(down_epilogue, [h1, tgt], [p["final_norm_w"]], [row(D, F32), row(D, BF16)], [acc(D), acc(D)]))

    (dact,) = _mm_nt(dh2_b, p["w_down"], splits=[D_FF], out_dtype=F32, name="d_act")
    g_down = _mm_tn([act], dh2_b, name="g_down")
    dgp, dup, d_conv_w8, d_conv_b = _convact_bwd(gp, up, dact, p["conv_w8"], p["conv_b"], name="convact_bwd")
    g_gate_t = _mm_tn([dgp], v, name="g_gate")
    g_up_t = _mm_tn([dup], v, name="g_up")
    swapping = ex.ffn_grads([g_gate_t, g_up_t, g_down])

    def ffn_norm_bwd(dvv, hv, dh2v, w):
        dx, dw = _rms_bwd(hv, w, dvv)
        dh1v = dx + dh2v
        return dh1v, dh1v, dw

    dh1, dh1_b, d_norm_ffn = _mm_nn(
        [[dgp], [dup]], [p["w_gate_t"], p["w_up_t"]], name="d_v_norm", after=swapping,
        epilogue=(ffn_norm_bwd, [h1, dh2], [p["norm_ffn_w"]], [row(D, F32), row(D, BF16)], [acc(D)]))
    sent = ex.ffn_grads_send(dh1_b)
    def mix_bwd(dmixv, o_rawv, hgv, hgw):
        do_rawv, dhgv, dw = _mix_bwd_fn(o_rawv, hgv, dmixv[:, :HG_W], hgw)
        return do_rawv, dhgv, dmixv[:, HG_W:], dw

    do_raw, dhg, do_att, d_hg_norm = _mm_nn(
        [[dh1_b]], [p["w_out"]], name="d_mix_bwd", w_transposed=True, after=sent,
        epilogue=(mix_bwd, [o_raw, hg], [p["hg_norm_w"]], [row(HG_W, F32), row(HG_W, BF16), row(ATT_Q_W, F32)], [acc(HG_DK)]))
    g_out = _mm_tn([mix], dh1_b, name="g_out")
    daq, dakv, d_sinks8, d_bq, d_bkv = _attn_bwd(att, p["b_attn"], p["sinks"], do_att, name="attn_bwd")
    dhq, dhf, dhi, d_lb = _hgrn_bwd(hq, hf, hi, p["lb"], states, do_raw, name="hgrn_bwd")
    pieces = [dhq, dhf, dhi, dhg, daq, dakv]
    g_in_t = _mm_tn(pieces, u, name="g_in")

    def mix_norm_bwd(duv, xv, dh1v, w):
        dx, dw = _rms_bwd(xv, w, duv)
        return dx + dh1v, dw

    dx, d_norm_mix = _mm_nn([pieces], [p["w_in_t"]], name="d_u_norm",
                            epilogue=(mix_norm_bwd, [x, dh1], [p["norm_mix_w"]], [row(D, F32)], [acc(D)]))
    grads = dict(g_in_t=g_in_t, g_out=g_out, g_gate_t=g_gate_t, g_up_t=g_up_t, g_down=g_down,
                 norm_mix_w=d_norm_mix, b_attn=jnp.concatenate([d_bq, d_bkv], axis=1), lb=d_lb, hg_norm_w=d_hg_norm,
                 sinks8=d_sinks8, norm_ffn_w=d_norm_ffn, conv_w8=d_conv_w8, conv_b=d_conv_b, final_norm_w=d_final)
    return loss_cols, dx, grads


SLAB = (IN_COLS // N_CHIPS, D_FF // N_CHIPS, D_FF // N_CHIPS, D_FF // N_CHIPS, D_MODEL // N_CHIPS)
N_W = len(SLAB)
PACK_OFF = tuple(sum(SLAB[:i]) for i in range(N_W))
PACK_ROWS = sum(SLAB)
FULL_OFF = tuple(N_CHIPS * o for o in PACK_OFF)
FULL_ROWS = N_CHIPS * PACK_ROWS
HALF = tuple(s // 2 for s in SLAB)
HPACK_OFF = tuple(sum(HALF[:i]) for i in range(N_W))
HPACK_ROWS = sum(HALF)
HFULL_OFF = tuple(N_CHIPS * o for o in HPACK_OFF)
HFULL_ROWS = N_CHIPS * HPACK_ROWS
CHIP_FLIPS = ((1, 0), (0, 1), (1, 1))
N_DEV = 8
BF16_ROWS = 16
ANY = pl.BlockSpec(memory_space=pl.ANY)


def _pos():
    return lax.axis_index("x"), lax.axis_index("y"), lax.axis_index("c")


def _flip(v, f):
    return 1 - v if f else v


def _rcopy(src, dst, ssem, rsem, dev):
    return pltpu.make_async_remote_copy(src_ref=src, dst_ref=dst, send_sem=ssem, recv_sem=rsem, device_id=dev,
                                        device_id_type=pl.DeviceIdType.MESH)


def _rows(ref, start, n, align=None):
    if not isinstance(start, int):
        if align is None:
            align = SUBLANES * (4 // jnp.dtype(ref.dtype).itemsize)
        start = pl.multiple_of(start, align)
    return ref.at[pl.ds(start, n), :]


FFN_W = (1, 2, 3)
N_PEER = 1 + len(CHIP_FLIPS)
HBM = pl.BlockSpec(memory_space=pltpu.HBM)
SEM = pl.BlockSpec(memory_space=pltpu.SEMAPHORE)
EFFECT = pltpu.SideEffectType.DATAFLOW_SIDE_EFFECTING
LANES = 128


def _sent_rows(k, w, c):
    return (0, SLAB[w]) if k == 0 else (c * HALF[w], HALF[w])


def _gather_start(pack, cw8):
    D = pack.shape[1]
    lands = [lax.empty((N_CHIPS * SLAB[0], D), pack.dtype), lax.empty((3 * N_CHIPS * SLAB[1], D), pack.dtype),
             lax.empty((N_CHIPS * SLAB[4], D), pack.dtype), lax.empty((N_CHIPS,) + cw8.shape, cw8.dtype)]
    bufs = [pack, cw8] + lands

    def body(pack_ref, cw_ref, l_in, l_ffn, l_out, l_cw, *rest):
        in_send, in_recv, out_send, out_recv, ffn_send, ffn_recv = rest[:6]
        token = rest[-1]
        x, y, c = _pos()
        q = 2 * x + y
        peers = _gather_peers(x, y, c)

        def send(k, peer, w, land, base, ssem, rsem):
            r0, n = _sent_rows(k, w, c)
            _rcopy(_rows(pack_ref, PACK_OFF[w] + r0, n), _rows(land, base + q * SLAB[w] + r0, n), ssem, rsem, peer).start()

        for k, peer in enumerate(peers):
            send(k, peer, 0, l_in, 0, in_send.at[k], in_recv.at[k])
        for k, peer in enumerate(peers):
            send(k, peer, 4, l_out, 0, out_send.at[k], out_recv.at[k])
            _rcopy(cw_ref, l_cw.at[q], out_send.at[N_PEER + k], out_recv.at[N_PEER + k], peer).start()
        for j, w in enumerate(FFN_W):
            for k, peer in enumerate(peers):
                send(k, peer, w, l_ffn, j * N_CHIPS * SLAB[w], ffn_send.at[k], ffn_recv.at[k])
        token[...] = jnp.zeros_like(token)

    n_sem = (N_PEER, N_PEER, 2 * N_PEER, 2 * N_PEER, N_PEER, N_PEER)
    outs = pl.pallas_call(
        body, name="gather_start", in_specs=[HBM] * len(bufs),
        out_specs=[SEM] * len(n_sem) + [HBM] * len(bufs) + [pl.BlockSpec(memory_space=pltpu.VMEM)],
        out_shape=[pltpu.SemaphoreType.DMA((n,)) for n in n_sem]
        + [pltpu.HBM(b.shape, b.dtype) for b in bufs] + [_sds((SUBLANES, LANES), F32)],
        input_output_aliases={i: len(n_sem) + i for i in range(len(bufs))},
        compiler_params=pltpu.CompilerParams(has_side_effects=EFFECT),
    )(*[pltpu.with_memory_space_constraint(b, pltpu.HBM) for b in bufs])
    bufs_out = outs[len(n_sem):]
    return dict(in_sems=outs[0:2], out_sems=outs[2:4], ffn_sems=outs[4:6], pack=bufs_out[0], cw=bufs_out[1], l_in=bufs_out[2],
                l_ffn=bufs_out[3], l_out=bufs_out[4], l_cw=bufs_out[5], token=bufs_out[6])


def _gather_peers(x, y, c):
    return [(x, y, 1 - c)] + [(_flip(x, fx), _flip(y, fy), c) for fx, fy in CHIP_FLIPS]


def _gather_wait_in(g, after):
    def body(pack_ref, l_in, send, recv, after_ref, pack_out, l_out):
        for k, peer in enumerate(_gather_peers(*_pos())):
            n = _sent_rows(k, 0, 0)[1]
            cp = _rcopy(_rows(pack_ref, PACK_OFF[0], n), _rows(l_in, 0, n), send.at[k], recv.at[k], peer)
            cp.wait_send()
            cp.wait_recv()

    return pl.pallas_call(
        body, name="gather_wait_in", in_specs=[HBM, HBM, SEM, SEM, ANY], out_specs=[HBM, HBM],
        out_shape=[pltpu.HBM(g["pack"].shape, g["pack"].dtype), pltpu.HBM(g["l_in"].shape, g["l_in"].dtype)],
        input_output_aliases={0: 0, 1: 1}, compiler_params=pltpu.CompilerParams(has_side_effects=EFFECT),
    )(g["pack"], g["l_in"], *g["in_sems"], after)


def _gather_wait_rest(g, pack, after):
    def body(pack_ref, cw_ref, l_ffn, l_out, l_cw, o_send, o_recv, f_send, f_recv, after_ref, o_ffn, o_out, o_cw):
        for k, peer in enumerate(_gather_peers(*_pos())):
            n_out = _sent_rows(k, 4, 0)[1]
            n_ffn = len(FFN_W) * _sent_rows(k, FFN_W[0], 0)[1]
            for cp in (_rcopy(_rows(pack_ref, PACK_OFF[4], n_out), _rows(l_out, 0, n_out), o_send.at[k], o_recv.at[k], peer),
                       _rcopy(cw_ref, l_cw.at[0], o_send.at[N_PEER + k], o_recv.at[N_PEER + k], peer),
                       _rcopy(_rows(pack_ref, PACK_OFF[FFN_W[0]], n_ffn), _rows(l_ffn, 0, n_ffn), f_send.at[k], f_recv.at[k], peer)):
                cp.wait_send()
                cp.wait_recv()

    ins = [pack, g["cw"], g["l_ffn"], g["l_out"], g["l_cw"]]
    return pl.pallas_call(
        body, name="gather_wait_rest", in_specs=[HBM] * 5 + [SEM] * 4 + [ANY], out_specs=[HBM] * 3,
        out_shape=[pltpu.HBM(b.shape, b.dtype) for b in ins[2:]],
        input_output_aliases={2: 0, 3: 1, 4: 2}, compiler_params=pltpu.CompilerParams(has_side_effects=EFFECT),
    )(*ins, *g["out_sems"], *g["ffn_sems"], after)


FWD_IN = ((0, 0, 0),)
FWD_REST = tuple((0, w, j * N_CHIPS * SLAB[w]) for j, w in enumerate(FFN_W)) + ((1, 4, 0),)


def _forward_copies(layout, src, dst, send_sems, recv_sems):
    x, y, c = _pos()
    sib = (x, y, 1 - c)
    cps = []
    for fx, fy in CHIP_FLIPS:
        qa = 2 * _flip(x, fx) + _flip(y, fy)
        for bi, w, base in layout:
            r0 = base + qa * SLAB[w] + c * HALF[w]
            cps.append(_rcopy(_rows(src[bi], r0, HALF[w]), _rows(dst[bi], r0, HALF[w]),
                              send_sems.at[len(cps)], recv_sems.at[len(cps)], sib))
    return cps


def _forward_in(l_in):
    n = len(CHIP_FLIPS) * len(FWD_IN)

    def body(in_ref, out_ref, send_sems, recv_sems):
        cps = _forward_copies(FWD_IN, [in_ref], [out_ref], send_sems, recv_sems)
        for cp in cps:
            cp.start()
        for cp in cps:
            cp.wait_recv()
        for cp in cps:
            cp.wait_send()

    return pl.pallas_call(
        body, name="forward_in", in_specs=[ANY], out_specs=ANY, out_shape=_sds(l_in.shape, l_in.dtype),
        input_output_aliases={0: 0},
        scratch_shapes=[pltpu.SemaphoreType.DMA((n,)), pltpu.SemaphoreType.DMA((n,))],
    )(l_in)


def _forward_rest_start(l_ffn, l_out):
    n = len(CHIP_FLIPS) * len(FWD_REST)
    bufs = [l_ffn, l_out]

    def body(a_ref, b_ref, send_sems, recv_sems, a_out, b_out, token):
        for cp in _forward_copies(FWD_REST, [a_ref, b_ref], [a_ref, b_ref], send_sems, recv_sems):
            cp.start()
        token[...] = jnp.zeros_like(token)

    outs = pl.pallas_call(
        body, name="forward_rest_start", in_specs=[HBM] * 2,
        out_specs=[SEM, SEM, HBM, HBM, pl.BlockSpec(memory_space=pltpu.VMEM)],
        out_shape=[pltpu.SemaphoreType.DMA((n,)), pltpu.SemaphoreType.DMA((n,))]
        + [pltpu.HBM(b.shape, b.dtype) for b in bufs] + [_sds((SUBLANES, LANES), F32)],
        input_output_aliases={0: 2, 1: 3}, compiler_params=pltpu.CompilerParams(has_side_effects=EFFECT),
    )(*[pltpu.with_memory_space_constraint(b, pltpu.HBM) for b in bufs])
    return dict(sems=outs[0:2], bufs=outs[2:4], token=outs[4])


def _forward_rest_wait(s, after):
    def body(a_ref, b_ref, send_sems, recv_sems, after_ref, a_out, b_out):
        for cp in _forward_copies(FWD_REST, [a_ref, b_ref], [a_ref, b_ref], send_sems, recv_sems):
            cp.wait_send()
            cp.wait_recv()

    return pl.pallas_call(
        body, name="forward_rest_wait", in_specs=[HBM, HBM, SEM, SEM, ANY], out_specs=[HBM, HBM],
        out_shape=[pltpu.HBM(b.shape, b.dtype) for b in s["bufs"]],
        input_output_aliases={0: 0, 1: 1}, compiler_params=pltpu.CompilerParams(has_side_effects=EFFECT),
    )(*s["bufs"], *s["sems"], after)


def _exchange_halves(ws, gs, small, *, name):
    D = gs[0].shape[1]
    n = len(ws)
    has_small = small is not None

    def body(*refs):
        g = refs[:n]
        t = refs[n + has_small:2 * n + has_small]
        sems = refs[2 * n + 2 * has_small:]
        d2d_send, d2d_recv = sems[0], sems[1]
        x, y, c = _pos()
        sib = (x, y, 1 - c)
        drains = []
        for i, w in enumerate(ws):
            h = HALF[w]
            for qq in range(N_CHIPS):
                _rcopy(_rows(g[i], qq * SLAB[w] + (1 - c) * h, h), _rows(t[i], qq * h, h),
                       d2d_send.at[i], d2d_recv.at[i], sib).start()
            drains.append(_rcopy(t[i], t[i], d2d_send.at[i], d2d_recv.at[i], sib))
        if has_small:
            small_ref, sall_ref = refs[n], refs[2 * n + 1]
            sm_send, sm_recv, loc_sem = sems[2], sems[3], sems[4]
            me = 4 * x + 2 * y + c
            own_small = pltpu.make_async_copy(small_ref, sall_ref.at[me], loc_sem)
            own_small.start()
            for f in range(1, N_DEV):
                peer = (_flip(x, f & 4), _flip(y, f & 2), _flip(c, f & 1))
                cp = _rcopy(small_ref, sall_ref.at[me], sm_send.at[f - 1], sm_recv.at[f - 1], peer)
                cp.start()
                drains.append(cp)
        for d in drains:
            d.wait_recv()
        for d in drains:
            d.wait_send()
        if has_small:
            own_small.wait()

    out_shape = [_sds((N_CHIPS * HALF[w], D), F32) for w in ws]
    scratch = [pltpu.SemaphoreType.DMA((n,)), pltpu.SemaphoreType.DMA((n,))]
    if has_small:
        out_shape.append(_sds((N_DEV,) + small.shape, F32))
        scratch += [pltpu.SemaphoreType.DMA((N_DEV - 1,)), pltpu.SemaphoreType.DMA((N_DEV - 1,)), pltpu.SemaphoreType.DMA]
    return pl.pallas_call(
        body, name=name, in_specs=[ANY] * (n + has_small), out_specs=[ANY] * (n + has_small),
        out_shape=out_shape, scratch_shapes=scratch,
    )(*gs, *([small] if has_small else []))


def _halves_copies(ws, g, t, send_sems, recv_sems):
    x, y, c = _pos()
    sib = (x, y, 1 - c)
    cps = []
    for i, w in enumerate(ws):
        h = HALF[w]
        for qq in range(N_CHIPS):
            cps.append(_rcopy(_rows(g[i], qq * SLAB[w] + (1 - c) * h, h), _rows(t[i], qq * h, h),
                              send_sems.at[N_CHIPS * i + qq], recv_sems.at[N_CHIPS * i + qq], sib))
    return cps


def _halves_start(ws, gs, *, name):
    D = gs[0].shape[1]
    n = len(ws)
    bufs = list(gs) + [lax.empty((N_CHIPS * HALF[w], D), F32) for w in ws]

    def body(*refs):
        for cp in _halves_copies(ws, refs[:n], refs[n:2 * n], refs[2 * n], refs[2 * n + 1]):
            cp.start()
        refs[-1][...] = jnp.zeros_like(refs[-1])

    outs = pl.pallas_call(
        body, name=name, in_specs=[HBM] * (2 * n),
        out_specs=[SEM, SEM] + [HBM] * (2 * n) + [pl.BlockSpec(memory_space=pltpu.VMEM)],
        out_shape=[pltpu.SemaphoreType.DMA((N_CHIPS * n,)), pltpu.SemaphoreType.DMA((N_CHIPS * n,))]
        + [pltpu.HBM(b.shape, b.dtype) for b in bufs] + [_sds((SUBLANES, LANES), F32)],
        input_output_aliases={i: 2 + i for i in range(2 * n)},
        compiler_params=pltpu.CompilerParams(has_side_effects=EFFECT),
    )(*[pltpu.with_memory_space_constraint(b, pltpu.HBM) for b in bufs])
    return dict(sems=outs[0:2], gs=outs[2:2 + n], theirs=outs[2 + n:2 + 2 * n], token=outs[-1])


def _halves_wait(ws, s, after, *, name):
    n = len(ws)

    def body(*refs):
        for cp in _halves_copies(ws, refs[:n], refs[n:2 * n], refs[2 * n], refs[2 * n + 1]):
            cp.wait_send()
            cp.wait_recv()

    bufs = list(s["gs"]) + list(s["theirs"])
    outs = pl.pallas_call(
        body, name=name, in_specs=[HBM] * (2 * n) + [SEM, SEM, ANY], out_specs=[HBM] * (2 * n),
        out_shape=[pltpu.HBM(b.shape, b.dtype) for b in bufs],
        input_output_aliases={i: i for i in range(2 * n)},
        compiler_params=pltpu.CompilerParams(has_side_effects=EFFECT),
    )(*bufs, *s["sems"], after)
    return outs[:n], outs[n:]


REDUCE_SPLIT = 2


def _chip_partial(ws, gs, theirs, *, name, out_dtype=F32):
    D = gs[0].shape[1]
    n = len(ws)

    def body(*refs):
        for i in range(n):
            refs[2 * n + i][...] = (refs[i][...] + refs[n + i][...]).astype(out_dtype)

    blk = [HALF[w] // REDUCE_SPLIT for w in ws]
    mine = [pl.BlockSpec((b, D), lambda qq, j: ((2 * qq + lax.axis_index("c")) * REDUCE_SPLIT + j, 0)) for b in blk]
    flat = [pl.BlockSpec((b, D), lambda qq, j: (qq * REDUCE_SPLIT + j, 0)) for b in blk]
    return pl.pallas_call(
        body, name=name, grid=(N_CHIPS, REDUCE_SPLIT), in_specs=mine + flat, out_specs=flat,
        out_shape=[_sds((N_CHIPS * HALF[w], D), out_dtype) for w in ws],
        compiler_params=_cp(("parallel", "parallel")),
    )(*gs, *theirs)


def _partial_copies(ws, part, got, send_sems, recv_sems):
    x, y, c = _pos()
    cps = []
    for k, (fx, fy) in enumerate(CHIP_FLIPS):
        peer = (_flip(x, fx), _flip(y, fy), c)
        qp = 2 * _flip(x, fx) + _flip(y, fy)
        for i, w in enumerate(ws):
            cps.append(_rcopy(_rows(part[i], qp * HALF[w], HALF[w]), _rows(got[i], k * HALF[w], HALF[w]),
                              send_sems.at[len(ws) * k + i], recv_sems.at[len(ws) * k + i], peer))
    return cps


def _send_chip_partials(ws, parts, *, name):
    D = parts[0].shape[1]
    n = len(ws)

    def body(*refs):
        cps = _partial_copies(ws, refs[:n], refs[n:2 * n], refs[2 * n], refs[2 * n + 1])
        for cp in cps:
            cp.start()
        for cp in cps:
            cp.wait_recv()
        for cp in cps:
            cp.wait_send()

    return pl.pallas_call(
        body, name=name, in_specs=[ANY] * n, out_specs=[ANY] * n,
        out_shape=[_sds((len(CHIP_FLIPS) * HALF[w], D), parts[0].dtype) for w in ws],
        scratch_shapes=[pltpu.SemaphoreType.DMA((len(CHIP_FLIPS) * n,)), pltpu.SemaphoreType.DMA((len(CHIP_FLIPS) * n,))],
    )(*parts)


def _send_start(ws, parts, *, name):
    D = parts[0].shape[1]
    n = len(ws)
    bufs = list(parts) + [lax.empty((len(CHIP_FLIPS) * HALF[w], D), parts[0].dtype) for w in ws]

    def body(*refs):
        send_sems, recv_sems = refs[2 * n], refs[2 * n + 1]
        for cp in _partial_copies(ws, refs[:n], refs[n:2 * n], send_sems, recv_sems):
            cp.start()
        refs[-1][...] = jnp.zeros_like(refs[-1])

    outs = pl.pallas_call(
        body, name=name, in_specs=[HBM] * (2 * n),
        out_specs=[SEM, SEM] + [HBM] * (2 * n) + [pl.BlockSpec(memory_space=pltpu.VMEM)],
        out_shape=[pltpu.SemaphoreType.DMA((len(CHIP_FLIPS) * n,)), pltpu.SemaphoreType.DMA((len(CHIP_FLIPS) * n,))]
        + [pltpu.HBM(b.shape, b.dtype) for b in bufs] + [_sds((SUBLANES, LANES), F32)],
        input_output_aliases={i: 2 + i for i in range(2 * n)},
        compiler_params=pltpu.CompilerParams(has_side_effects=EFFECT),
    )(*[pltpu.with_memory_space_constraint(b, pltpu.HBM) for b in bufs])
    return dict(sems=outs[0:2], parts=outs[2:2 + n], got=outs[2 + n:2 + 2 * n], token=outs[-1])


def _send_wait(ws, s, after, *, name):
    n = len(ws)

    def body(*refs):
        for cp in _partial_copies(ws, refs[:n], refs[n:2 * n], refs[2 * n], refs[2 * n + 1]):
            cp.wait_send()
            cp.wait_recv()

    bufs = list(s["parts"]) + list(s["got"])
    outs = pl.pallas_call(
        body, name=name, in_specs=[HBM] * (2 * n) + [SEM, SEM, ANY], out_specs=[HBM] * (2 * n),
        out_shape=[pltpu.HBM(b.shape, b.dtype) for b in bufs],
        input_output_aliases={i: i for i in range(2 * n)},
        compiler_params=pltpu.CompilerParams(has_side_effects=EFFECT),
    )(*bufs, *s["sems"], after)
    return outs[:n], outs[n:]


def _chip_reduce(ws, parts, got, *, name, after=None):
    D = parts[0].shape[1]
    nk = len(CHIP_FLIPS)
    n = len(ws)
    extra = [] if after is None else [after]

    def body(*refs):
        refs = refs[len(extra):]
        outs = refs[(1 + nk) * n:]
        for i in range(n):
            acc = refs[i][...].astype(F32)
            for k in range(nk):
                acc = acc + refs[n * (1 + k) + i][...].astype(F32)
            outs[i][...] = acc

    blk = [HALF[w] // REDUCE_SPLIT for w in ws]

    def q_idx(j):
        return (2 * lax.axis_index("x") + lax.axis_index("y")) * REDUCE_SPLIT + j

    in_specs = [pl.BlockSpec((b, D), lambda j: (q_idx(j), 0)) for b in blk]
    for k in range(nk):
        in_specs += [pl.BlockSpec((b, D), functools.partial(lambda j, k: (k * REDUCE_SPLIT + j, 0), k=k)) for b in blk]
    out_specs = [pl.BlockSpec((b, D), lambda j: (lax.axis_index("c") * REDUCE_SPLIT + j, 0)) for b in blk]
    return pl.pallas_call(
        body, name=name, grid=(REDUCE_SPLIT,), in_specs=[ANY] * len(extra) + in_specs, out_specs=out_specs,
        out_shape=[_sds((SLAB[w], D), F32) for w in ws],
        compiler_params=_cp(("parallel",)),
    )(*extra, *parts, *[g for _ in range(nk) for g in got])


def _exchange_reduced(ws, shards, *, name):
    n = len(ws)

    def body(*refs):
        ins, outs = refs[:n], refs[n:2 * n]
        send_sems, recv_sems = refs[2 * n], refs[2 * n + 1]
        x, y, c = _pos()
        sib = (x, y, 1 - c)
        cps = []
        for i, w in enumerate(ws):
            cp = _rcopy(_rows(ins[i], c * HALF[w], HALF[w]), _rows(outs[i], c * HALF[w], HALF[w]),
                        send_sems.at[i], recv_sems.at[i], sib)
            cp.start()
            cps.append(cp)
        for cp in cps:
            cp.wait_recv()
        for cp in cps:
            cp.wait_send()

    return pl.pallas_call(
        body, name=name, in_specs=[ANY] * n, out_specs=[ANY] * n,
        out_shape=[_sds(s.shape, s.dtype) for s in shards], input_output_aliases={i: i for i in range(n)},
        scratch_shapes=[pltpu.SemaphoreType.DMA((n,)), pltpu.SemaphoreType.DMA((n,))],
    )(*shards)


def _adamw_fn(w, g, m, v):
    m2 = ADAM_B1 * m + (1.0 - ADAM_B1) * g
    v2 = ADAM_B2 * v + (1.0 - ADAM_B2) * (g * g)
    m_hat = m2 / (1.0 - ADAM_B1 ** ADAM_STEP)
    v_hat = v2 / (1.0 - ADAM_B2 ** ADAM_STEP)
    return -ADAM_LR * (m_hat / (jnp.sqrt(v_hat) + ADAM_EPS) + ADAM_WD * w), m2, v2


def _adamw(w, g, m, v, *, name):
    shp = _sds(w.shape, F32)
    rows = w.shape[0]
    tm = max(t for t in range(SUBLANES, 512 + 1, SUBLANES) if rows % t == 0)
    return _rowwise(_adamw_fn, [_full(w), _full(g), _full(m), _full(v)], [], [shp] * 3, [], name=name, tm=tm)


SMALL_SEGS = (("loss", 8), ("norm_mix_w", 8), ("b_attn", 8), ("lb_logits", 8), ("hg_norm_w", 8), ("sinks", 8),
              ("norm_ffn_w", 8), ("conv_w", 72), ("conv_b", 24), ("final_norm_w", 8))
SMALL_OFF = {n: sum(r for _, r in SMALL_SEGS[:i]) for i, (n, _) in enumerate(SMALL_SEGS)}
SMALL_ROWS = sum(r for _, r in SMALL_SEGS)
LANES = 128


def _pack_small(parts):
    segs = []
    for n, r in SMALL_SEGS:
        a = parts.get(n)
        flat = jnp.zeros((0,), F32) if a is None else a.reshape(-1).astype(F32)
        segs.append(jnp.pad(flat, (0, r * LANES - flat.shape[0])).reshape(r, LANES))
    return jnp.concatenate(segs, axis=0)


def _unpack_small(pack, n, shape):
    size = math.prod(shape)
    r0 = SMALL_OFF[n]
    return pack[r0:r0 + dict(SMALL_SEGS)[n]].reshape(-1)[:size].reshape(shape)


def _small_update(sall, wp, mp, vp):
    R = SMALL_ROWS
    r_lb = SMALL_OFF["lb_logits"]

    def body(s_ref, w_ref, m_ref, v_ref, g_ref, d_ref, m2_ref, v2_ref, loss_ref):
        g = s_ref[0]
        for i in range(1, N_DEV):
            g = g + s_ref[i]
        tot = jnp.sum(jnp.sum(g[0:8], axis=1, keepdims=True), axis=0, keepdims=True)
        loss_ref[...] = jnp.broadcast_to(tot, loss_ref.shape)
        lg = w_ref[r_lb:r_lb + 8, :]
        p0 = _sigmoid(lg - pltpu.roll(lg, 4, 0))
        d = g[r_lb:r_lb + 8]
        d = d + pltpu.roll(d, 4, 0)
        sign = jnp.where(lax.broadcasted_iota(jnp.int32, d.shape, 0) < 4, 1.0, -1.0)
        g = jnp.concatenate([g[:r_lb], sign * d * p0 * (1.0 - p0), g[r_lb + 8:]], axis=0)
        g_ref[...] = g
        d_ref[...], m2_ref[...], v2_ref[...] = _adamw_fn(w_ref[...], g, m_ref[...], v_ref[...])

    full = pl.BlockSpec((R, LANES), lambda: (0, 0))
    return pl.pallas_call(
        body, name="small_update",
        in_specs=[pl.BlockSpec((N_DEV, R, LANES), lambda: (0, 0, 0)), full, full, full],
        out_specs=[full, full, full, full, pl.BlockSpec((8, LANES), lambda: (0, 0))],
        out_shape=[_sds((R, LANES), F32)] * 4 + [_sds((8, LANES), F32)],
        compiler_params=_cp(),
    )(sall, wp, mp, vp)


def _lb_fwd(lb_logits):
    n = lb_logits.shape[1]

    def body(l_ref, o_ref):
        o_ref[...] = _sigmoid(l_ref[0:1, :] - l_ref[1:2, :])

    return pl.pallas_call(body, name="lb_fwd", out_shape=_sds((1, n), F32), compiler_params=_cp())(lb_logits)


class _MeshExchange:
    def __init__(self, pack, cw8):
        self.gather = _gather_start(pack, cw8)
        self.sent = None
        self.conv_w8 = None

    def start(self):
        return self.gather["token"]

    def w_in(self, after):
        self.pack, l_in = _gather_wait_in(self.gather, after)
        return (_forward_in(l_in), N_CHIPS * SLAB[0], 0)

    def mid(self, after):
        l_ffn, l_out, l_cw = _gather_wait_rest(self.gather, self.pack, after)
        self.conv_w8 = jnp.concatenate([l_cw[i] for i in range(N_CHIPS)], axis=1)
        self.passing = _forward_rest_start(l_ffn, l_out)
        return self.passing["token"]

    def rest(self, after):
        l_ffn, l_out = _forward_rest_wait(self.passing, after)
        rows = N_CHIPS * SLAB[FFN_W[0]]
        return dict(w_gate_t=(l_ffn, rows, 0), w_up_t=(l_ffn, rows, 1), w_down=(l_ffn, rows, 2),
                    w_out=(l_out, N_CHIPS * SLAB[4], 0), conv_w8=self.conv_w8)

    def ffn_grads(self, gs):
        self.swap = _halves_start(FFN_W, gs, name="halves_ffn_start")
        return self.swap["token"]

    def ffn_grads_send(self, after):
        gs, theirs = _halves_wait(FFN_W, self.swap, after, name="halves_ffn_wait")
        parts = _chip_partial(FFN_W, gs, theirs, name="chip_partial_ffn")
        self.sent = _send_start(FFN_W, parts, name="send_ffn_start")
        return self.sent["token"]


def kernel(x, norm_mix_w, w_in, b_attn, lb_logits, hg_norm_w, sinks, w_out, norm_ffn_w, w_gate, w_up, conv_w, conv_b, w_down, final_norm_w, loss_target, m_norm_mix_w, m_w_in, m_b_attn, m_lb_logits, m_hg_norm_w, m_sinks, m_w_out, m_norm_ffn_w, m_w_gate, m_w_up, m_conv_w, m_conv_b, m_w_down, m_final_norm_w, v_norm_mix_w, v_w_in, v_b_attn, v_lb_logits, v_hg_norm_w, v_sinks, v_w_out, v_norm_ffn_w, v_w_gate, v_w_up, v_conv_w, v_conv_b, v_w_down, v_final_norm_w):
    D = D_MODEL
    q = 2 * lax.axis_index("x") + lax.axis_index("y")
    ccols = D_FF // N_CHIPS

    pack = jnp.concatenate([w_in[0].T, w_gate[0].T, w_up[0].T, w_down[0], w_out[0]], axis=0).astype(BF16)
    cw8 = jnp.concatenate([conv_w[0], jnp.zeros((SUBLANES - 3, ccols), F32)], axis=0)
    ex = _MeshExchange(pack, cw8)
    p = dict(norm_mix_w=norm_mix_w, b_attn=b_attn, lb=_lb_fwd(lb_logits), hg_norm_w=hg_norm_w, sinks=sinks,
             norm_ffn_w=norm_ffn_w, conv_b=conv_b, final_norm_w=final_norm_w.reshape(1, D))
    loss_cols, dx, g = _local_step(x[0], loss_target[0], p, ex)
    conv_w8 = ex.conv_w8

    small = _pack_small(dict(loss=loss_cols, norm_mix_w=g["norm_mix_w"], b_attn=g["b_attn"], lb_logits=g["lb"],
                             hg_norm_w=g["hg_norm_w"], sinks=g["sinks8"], norm_ffn_w=g["norm_ffn_w"],
                             conv_w=g["conv_w8"][:3], conv_b=g["conv_b"], final_norm_w=g["final_norm_w"]))
    parts_ffn, got_ffn = _send_wait(FFN_W, ex.sent, dx, name="send_ffn_wait")
    late = (0, 4)
    gs = [g["g_in_t"], g["g_out"]]
    *theirs, sall = _exchange_halves(late, gs, small, name="exchange_halves_late")
    parts_late = _chip_partial(late, gs, theirs, name="chip_partial_late", out_dtype=BF16)
    sent_late = _send_start(late, parts_late, name="send_late_start")
    big = {}

    def finish(ws, parts, got, specs, tag, after):
        shards = _exchange_reduced(ws, _chip_reduce(ws, parts, got, name="chip_reduce_" + tag, after=after),
                                   name="exchange_reduced_" + tag)
        for gw, (n, w, m, v, tr) in zip(shards, specs):
            view = (lambda a: a[0].T) if tr else (lambda a: a[0])
            back = (lambda a: a.T[None]) if tr else (lambda a: a[None])
            d_, m_, v_ = _adamw(view(w), gw, view(m), view(v), name="adamw_" + n)
            big[n] = (back(gw), back(d_), back(m_), back(v_))
        return d_

    last = finish(FFN_W, parts_ffn, got_ffn, (("w_gate", w_gate, m_w_gate, v_w_gate, True), ("w_up", w_up, m_w_up, v_w_up, True),
                                              ("w_down", w_down, m_w_down, v_w_down, False)), "ffn", sent_late["token"])
    parts_late, got_late = _send_wait(late, sent_late, last, name="send_late_wait")
    finish(late, parts_late, got_late, (("w_in", w_in, m_w_in, v_w_in, True), ("w_out", w_out, m_w_out, v_w_out, False)),
           "late", None)

    def place(a):
        return lax.dynamic_update_slice(jnp.zeros((3, D_FF), F32), a[0], (0, q * ccols))

    def small_pack(ws, cw):
        nm, ba, lbl, hg, sk, nf, cb, fn = ws
        return _pack_small(dict(norm_mix_w=nm, b_attn=ba, lb_logits=lbl, hg_norm_w=hg,
                                sinks=jnp.broadcast_to(sk.reshape(ATT_HEADS, 1), (ATT_HEADS, LANES)), norm_ffn_w=nf,
                                conv_w=cw, conv_b=cb, final_norm_w=fn))

    wp = small_pack((norm_mix_w, b_attn, lb_logits, hg_norm_w, sinks, norm_ffn_w, conv_b, final_norm_w), conv_w8[:3])
    mp = small_pack((m_norm_mix_w, m_b_attn, m_lb_logits, m_hg_norm_w, m_sinks, m_norm_ffn_w, m_conv_b, m_final_norm_w),
                    place(m_conv_w))
    vp = small_pack((v_norm_mix_w, v_b_attn, v_lb_logits, v_hg_norm_w, v_sinks, v_norm_ffn_w, v_conv_b, v_final_norm_w),
                    place(v_conv_w))
    outs = _small_update(sall, wp, mp, vp)
    loss = outs[4][0, 0]

    def small_out(pk, n, ref):
        if n == "sinks":
            return pk[SMALL_OFF[n]:SMALL_OFF[n] + ATT_HEADS, 0].reshape(ref.shape)
        if n == "conv_w":
            full = _unpack_small(pk, n, (3, D_FF))
            return lax.dynamic_slice(full, (0, q * ccols), (3, ccols))[None]
        return _unpack_small(pk, n, ref.shape)

    refs = dict(norm_mix_w=norm_mix_w, b_attn=b_attn, lb_logits=lb_logits, hg_norm_w=hg_norm_w, sinks=sinks,
                norm_ffn_w=norm_ffn_w, conv_w=conv_w, conv_b=conv_b, final_norm_w=final_norm_w)
    order = ("norm_mix_w", "w_in", "b_attn", "lb_logits", "hg_norm_w", "sinks", "w_out", "norm_ffn_w", "w_gate", "w_up",
             "conv_w", "conv_b", "w_down", "final_norm_w")
    res = [loss, dx[None]]
    for k in range(4):
        for n in order:
            res.append(big[n][k] if n in big else small_out(outs[k], n, refs[n]))
    return tuple(res)
```

```python
import functools
import math

import jax
import jax.numpy as jnp
from jax import lax
from jax.experimental import pallas as pl
from jax.experimental.pallas import tpu as pltpu

F32 = jnp.float32
BF16 = jnp.bfloat16

D_MODEL = 1024
HG_HEADS = 4
HG_DK = 128
HG_W = HG_HEADS * HG_DK
HG_CHUNK = 64
HG_SUB = 8
HG_FWD_CHUNKS_PER_STEP = 4
HG_CHUNKS_PER_STEP = 2
ATT_HEADS = 8
ATT_KV = 2
ATT_GROUP = ATT_HEADS // ATT_KV
ATT_HD = 64
ATT_BLOCK = 128
ATT_Q_W = ATT_HEADS * ATT_HD
ATT_KV_W = ATT_KV * ATT_HD
ATT_COLS = ATT_Q_W + 2 * ATT_KV_W
IN_COLS = 4 * HG_W + ATT_COLS
D_FF = 2816
EPS = 1e-6
ADAM_LR, ADAM_B1, ADAM_B2, ADAM_EPS, ADAM_WD, ADAM_STEP = 0.001, 0.9, 0.999, 1e-08, 0.01, 10
NEG = -1e30

V7X_VMEM_BYTES = 64 * 1024 * 1024
VMEM_LIMIT = 48 * 1024 * 1024
SUBLANES = 8

N_CHIPS = 4


def _cp(sem=None, **kw):
    return pltpu.CompilerParams(dimension_semantics=sem, vmem_limit_bytes=VMEM_LIMIT, **kw)


def _sds(shape, dtype):
    return jax.ShapeDtypeStruct(shape, dtype)


def _wspec(w):
    arr, rows, blk = w
    return pl.BlockSpec((rows, arr.shape[1]), lambda i: (blk, 0))


def _mm_nt(a, w, *, splits, out_dtype, name, after=None, tm=512):
    M, K = a.shape
    N = w[1]
    tm = min(tm, M)
    assert sum(splits) == N and M % tm == 0
    offs = [sum(splits[:i]) for i in range(len(splits))]
    n_in = 2 if after is None else 3

    def body(*refs):
        a_ref, w_ref = refs[0], refs[1]
        acc = lax.dot_general(a_ref[...], w_ref[...], (((1,), (1,)), ((), ())), preferred_element_type=F32)
        for o_ref, c0, n in zip(refs[n_in:], offs, splits):
            o_ref[...] = acc[:, c0:c0 + n].astype(out_dtype)

    in_specs = [pl.BlockSpec((tm, K), lambda i: (i, 0)), _wspec(w)]
    args = [a, w[0]]
    if after is not None:
        in_specs.append(pl.BlockSpec(memory_space=pl.ANY))
        args.append(after)
    outs = pl.pallas_call(
        body, name=name, grid=(M // tm,), in_specs=in_specs,
        out_specs=[pl.BlockSpec((tm, n), lambda i: (i, 0)) for n in splits],
        out_shape=[_sds((M, n), out_dtype) for n in splits],
        compiler_params=_cp(("parallel",)),
    )(*args)
    return outs


def _mm_nn(pieces, ws, *, name, out_dtype=F32, residual=None, epilogue=None, prologue=None, after=None,
           w_transposed=False, tm=512):
    pro_fn, pro_rows, pro_bc, pro_out = prologue or (None, [], [], None)
    if prologue is not None:
        assert pieces is None and len(ws) == 1
        pieces = [[pro_out]]
    M = pieces[0][0].shape[0]
    K = ws[0][1] if w_transposed else ws[0][0].shape[1]
    tm = min(tm, M)
    flat = [] if prologue is not None else [p for grp in pieces for p in grp]
    n_p = len(flat)
    n_w = len(ws)
    n_pr, n_pb = len(pro_rows), len(pro_bc)
    fn, row_ins, bc_ins, row_outs, acc_outs = epilogue or (None, [], [], [_sds((M, K), out_dtype)], [])
    if residual is not None:
        assert epilogue is None
        row_ins = [residual]
    n_r, n_b, n_o = len(row_ins), len(bc_ins), len(row_outs)
    lead = [] if after is None else [after]

    def body(*refs):
        refs = refs[len(lead):]
        p_refs = refs[:n_p]
        w_refs = refs[n_p:n_p + n_w]
        extra = [r[...] for r in refs[n_p + n_w:n_p + n_w + n_r + n_b]]
        base = n_p + n_w + n_r + n_b
        pro = [r[...] for r in refs[base:base + n_pr + n_pb]]
        base += n_pr + n_pb
        o_refs = refs[base:base + n_o]
        a_refs = refs[base + n_o:base + n_o + len(acc_outs)]
        if pro_fn is not None:
            lhs = pro_fn(*pro).astype(pro_out.dtype)
            refs[-1][...] = lhs
            tiles = [lhs]
        else:
            tiles = [r[...] for r in p_refs]
        acc = None
        k = 0
        for gi, grp in enumerate(pieces):
            c0 = 0
            for p in grp:
                n = p.shape[1]
                if w_transposed:
                    t = lax.dot_general(tiles[k], w_refs[gi][...], (((1,), (1,)), ((), ())), preferred_element_type=F32)
                else:
                    t = jnp.dot(tiles[k], w_refs[gi][c0:c0 + n, :], preferred_element_type=F32)
                acc = t if acc is None else acc + t
                c0 += n
                k += 1
        if fn is None:
            res = (acc + extra[0] if residual is not None else acc,)
        else:
            res = fn(acc, *extra)
        for o_ref, val in zip(o_refs, res[:n_o]):
            o_ref[...] = val.astype(o_ref.dtype)
        if acc_outs:
            @pl.when(pl.program_id(0) == 0)
            def _():
                for a_ref in a_refs:
                    a_ref[...] = jnp.zeros_like(a_ref)
            for a_ref, val in zip(a_refs, res[n_o:]):
                a_ref[...] += val

    in_specs = [pl.BlockSpec((tm, p.shape[1]), lambda i: (i, 0)) for p in flat]
    in_specs += [_wspec(w) for w in ws]
    in_specs += [pl.BlockSpec((tm, r.shape[1]), lambda i: (i, 0)) for r in row_ins]
    in_specs += [pl.BlockSpec(b.shape, lambda i: (0, 0)) for b in bc_ins]
    in_specs += [pl.BlockSpec((tm, r.shape[1]), lambda i: (i, 0)) for r in pro_rows]
    in_specs += [pl.BlockSpec(b.shape, lambda i: (0, 0)) for b in pro_bc]
    out_specs = [pl.BlockSpec((tm, s.shape[1]), lambda i: (i, 0)) for s in row_outs]
    out_specs += [pl.BlockSpec(s.shape, lambda i: (0, 0)) for s in acc_outs]
    pro_outs = [] if prologue is None else [pro_out]
    out_specs += [pl.BlockSpec((tm, s.shape[1]), lambda i: (i, 0)) for s in pro_outs]
    outs = pl.pallas_call(
        body, name=name, grid=(M // tm,), in_specs=[pl.BlockSpec(memory_space=pl.ANY)] * len(lead) + in_specs,
        out_specs=out_specs, out_shape=list(row_outs) + list(acc_outs) + pro_outs,
        compiler_params=_cp(("arbitrary",) if acc_outs else ("parallel",)),
    )(*lead, *flat, *[w[0] for w in ws], *row_ins, *bc_ins, *pro_rows, *pro_bc)
    return outs if (epilogue is not None or prologue is not None) else outs[0]


def _mm_tn(pieces, x, *, name, out_dtype=BF16, tt=1024):
    M, K = x.shape
    tt = min(tt, M)
    ns = [p.shape[1] for p in pieces]
    offs = [sum(ns[:i]) for i in range(len(ns))]
    N = sum(ns)
    n_p = len(pieces)
    last = M // tt - 1

    def body(*refs):
        p_refs = refs[:n_p]
        x_ref = refs[n_p]
        o_ref, acc_ref = refs[n_p + 1], refs[n_p + 2]

        @pl.when(pl.program_id(0) == 0)
        def _():
            acc_ref[...] = jnp.zeros_like(acc_ref)

        xv = x_ref[...]
        for p_ref, c0, n in zip(p_refs, offs, ns):
            acc_ref[c0:c0 + n, :] += lax.dot_general(p_ref[...], xv, (((0,), (0,)), ((), ())),
                                                      preferred_element_type=F32)

        @pl.when(pl.program_id(0) == last)
        def _():
            o_ref[...] = acc_ref[...].astype(o_ref.dtype)

    in_specs = [pl.BlockSpec((tt, n), lambda i: (i, 0)) for n in ns]
    in_specs.append(pl.BlockSpec((tt, K), lambda i: (i, 0)))
    return pl.pallas_call(
        body, name=name, grid=(M // tt,), in_specs=in_specs,
        out_specs=pl.BlockSpec((N, K), lambda i: (0, 0)),
        out_shape=_sds((N, K), out_dtype),
        scratch_shapes=[pltpu.VMEM((N, K), F32)],
        compiler_params=_cp(("arbitrary",)),
    )(*pieces, x)


def _rms_fwd(xf, w):
    inv = lax.rsqrt(jnp.mean(xf * xf, axis=-1, keepdims=True) + EPS)
    return xf * inv * w


def _rms_bwd(xf, w, dy):
    inv = lax.rsqrt(jnp.mean(xf * xf, axis=-1, keepdims=True) + EPS)
    xhat = xf * inv
    dxhat = dy * w
    dx = inv * (dxhat - xhat * jnp.mean(dxhat * xhat, axis=-1, keepdims=True))
    dw = jnp.sum(dy * xhat, axis=0, keepdims=True)
    return dx, dw


def _sigmoid(x):
    return 1.0 / (1.0 + jnp.exp(-x))


def _rowwise(fn, row_ins, bc_ins, row_outs, acc_outs, *, name, tm=256, after=None):
    M = row_outs[0].shape[0] if row_outs else row_ins[0][0].shape[0]
    assert M % tm == 0 and tm % SUBLANES == 0, (name, M, tm)
    n_r, n_b, n_o, n_a = len(row_ins), len(bc_ins), len(row_outs), len(acc_outs)
    n_after = 0 if after is None else 1

    def body(*refs):
        refs = refs[n_after:]
        ins = [r[...] for r in refs[:n_r + n_b]]
        o_refs = refs[n_r + n_b:n_r + n_b + n_o]
        a_refs = refs[n_r + n_b + n_o:]
        res = fn(*ins)
        for o_ref, val in zip(o_refs, res[:n_o]):
            o_ref[...] = val.astype(o_ref.dtype)
        if n_a:
            @pl.when(pl.program_id(0) == 0)
            def _():
                for a_ref in a_refs:
                    a_ref[...] = jnp.zeros_like(a_ref)
            for a_ref, val in zip(a_refs, res[n_o:]):
                a_ref[...] += val

    in_specs = [pl.BlockSpec((tm, cw), functools.partial(lambda i, cb, r0: (i + r0, cb), cb=cb, r0=r0))
                for (_, cw, cb, r0) in row_ins]
    in_specs += [pl.BlockSpec(b.shape, lambda i: (0, 0)) for b in bc_ins]
    out_specs = [pl.BlockSpec((tm, s.shape[1]), lambda i: (i, 0)) for s in row_outs]
    out_specs += [pl.BlockSpec(s.shape, lambda i: (0, 0)) for s in acc_outs]
    if n_after:
        in_specs = [pl.BlockSpec(memory_space=pl.ANY)] + in_specs
    return pl.pallas_call(
        body, name=name, grid=(M // tm,), in_specs=in_specs, out_specs=out_specs,
        out_shape=list(row_outs) + list(acc_outs),
        compiler_params=_cp(("arbitrary",) if n_a else ("parallel",)),
    )(*([after] if n_after else []), *[r[0] for r in row_ins], *bc_ins)


def _full(a, first_row_block=0):
    return (a, a.shape[1], 0, first_row_block)


def _conv_rows(ext, w_ref_val, lo):
    s1 = pltpu.roll(ext, 1, 0)
    s2 = pltpu.roll(ext, 2, 0)
    y = w_ref_val[0:1, :] * s2 + w_ref_val[1:2, :] * s1 + w_ref_val[2:3, :] * ext
    return y[SUBLANES:, :]


def _convact_fwd(gp, up, conv_w8, conv_b, *, name, tr=512, tc=1408):
    T, C = gp.shape
    tr = min(tr, T)
    hb = tr // SUBLANES

    def body(gp_ref, gph_ref, up_ref, w_ref, b_ref, act_ref):
        i = pl.program_id(1)
        halo = jnp.where(i > 0, gph_ref[...], 0.0)
        ext = jnp.concatenate([halo, gp_ref[...]], axis=0)
        gate = _conv_rows(ext, w_ref[...], 0) + b_ref[...]
        act_ref[...] = (gate * _sigmoid(gate) * up_ref[...]).astype(act_ref.dtype)

    return pl.pallas_call(
        body, name=name, grid=(C // tc, T // tr),
        in_specs=[pl.BlockSpec((tr, tc), lambda j, i: (i, j)),
                  pl.BlockSpec((SUBLANES, tc), lambda j, i: (jnp.maximum(i * hb - 1, 0), j)),
                  pl.BlockSpec((tr, tc), lambda j, i: (i, j)),
                  pl.BlockSpec((SUBLANES, tc), lambda j, i: (0, j)),
                  pl.BlockSpec((1, tc), lambda j, i: (0, j))],
        out_specs=pl.BlockSpec((tr, tc), lambda j, i: (i, j)),
        out_shape=_sds((T, C), BF16),
        compiler_params=_cp(("parallel", "parallel")),
    )(gp, gp, up, conv_w8, conv_b)


def _convact_bwd(gp, up, dact, conv_w8, conv_b, *, name, tr=256, tc=1408):
    T, C = gp.shape
    tr = min(tr, T)
    hb = tr // SUBLANES
    nr = T // tr

    def body(gp_ref, gpp_ref, gpn_ref, up_ref, upn_ref, da_ref, dan_ref, w_ref, b_ref,
             dgp_ref, dup_ref, dw_ref, db_ref):
        i = pl.program_id(1)
        w = w_ref[...]
        prev = jnp.where(i > 0, gpp_ref[...], 0.0)
        last = i == nr - 1
        gp_ext = jnp.concatenate([prev, gp_ref[...], gpn_ref[...]], axis=0)
        gate = _conv_rows(gp_ext, w, 0) + b_ref[...]
        up_e = jnp.concatenate([up_ref[...], upn_ref[...]], axis=0)
        da_e = jnp.concatenate([da_ref[...], dan_ref[...]], axis=0)
        row = lax.broadcasted_iota(jnp.int32, gate.shape, 0)
        valid = jnp.logical_or(row < tr, jnp.logical_not(last))
        sg = _sigmoid(gate)
        silu = gate * sg
        dgate = jnp.where(valid, da_e * up_e * (sg * (1.0 + gate * (1.0 - sg))), 0.0)
        dup_ref[...] = (da_e[:tr] * silu[:tr]).astype(dup_ref.dtype)
        n = tr + SUBLANES
        g1 = pltpu.roll(dgate, n - 1, 0)
        g2 = pltpu.roll(dgate, n - 2, 0)
        dgp = w[2:3, :] * dgate + w[1:2, :] * g1 + w[0:1, :] * g2
        dgp_ref[...] = dgp[:tr].astype(dgp_ref.dtype)
        gpc = gp_ref[...]
        dw0 = jnp.sum(gpc * g2[:tr], axis=0, keepdims=True)
        dw1 = jnp.sum(gpc * g1[:tr], axis=0, keepdims=True)
        dw2 = jnp.sum(gpc * dgate[:tr], axis=0, keepdims=True)
        dbv = jnp.sum(dgate[:tr], axis=0, keepdims=True)
        z = jnp.zeros((SUBLANES - 3, gpc.shape[1]), F32)

        @pl.when(i == 0)
        def _():
            dw_ref[...] = jnp.zeros_like(dw_ref)
            db_ref[...] = jnp.zeros_like(db_ref)

        dw_ref[...] += jnp.concatenate([dw0, dw1, dw2, z], axis=0)
        db_ref[...] += dbv

    cur = pl.BlockSpec((tr, tc), lambda j, i: (i, j))
    prv = pl.BlockSpec((SUBLANES, tc), lambda j, i: (jnp.maximum(i * hb - 1, 0), j))
    nxt = pl.BlockSpec((SUBLANES, tc), lambda j, i: (jnp.minimum((i + 1) * hb, T // SUBLANES - 1), j))
    return pl.pallas_call(
        body, name=name, grid=(C // tc, nr),
        in_specs=[cur, prv, nxt, cur, nxt, cur, nxt,
                  pl.BlockSpec((SUBLANES, tc), lambda j, i: (0, j)),
                  pl.BlockSpec((1, tc), lambda j, i: (0, j))],
        out_specs=[cur, cur,
                   pl.BlockSpec((SUBLANES, tc), lambda j, i: (0, j)),
                   pl.BlockSpec((1, tc), lambda j, i: (0, j))],
        out_shape=[_sds((T, C), BF16), _sds((T, C), BF16), _sds((SUBLANES, C), F32), _sds((1, C), F32)],
        compiler_params=_cp(("parallel", "arbitrary")),
    )(gp, gp, gp, up, up, dact, dact, conv_w8, conv_b)


def _cumsum_rows(x):
    n = x.shape[0]
    row = lax.broadcasted_iota(jnp.int32, x.shape, 0)
    s = 1
    while s < n:
        x = x + jnp.where(row >= s, pltpu.roll(x, s, 0), 0.0)
        s *= 2
    return x


def _rcumsum_rows(x):
    n = x.shape[0]
    row = lax.broadcasted_iota(jnp.int32, x.shape, 0)
    s = 1
    while s < n:
        x = x + jnp.where(row < n - s, pltpu.roll(x, n - s, 0), 0.0)
        s *= 2
    return x


def _dot_nt(a, b):
    return lax.dot_general(a.astype(BF16), b.astype(BF16), (((1,), (1,)), ((), ())), preferred_element_type=F32)


def _dot_tn(a, b):
    return lax.dot_general(a.astype(BF16), b.astype(BF16), (((0,), (0,)), ((), ())), preferred_element_type=F32)


def _dot_nn(a, b):
    return jnp.dot(a.astype(BF16), b.astype(BF16), preferred_element_type=F32)


def _dot3(a, b, contract):
    def split(x):
        hi = x.astype(BF16)
        return hi, (x - hi.astype(F32)).astype(BF16)

    a_hi, a_lo = split(a)
    b_hi, b_lo = split(b)
    dot = lambda x, y: lax.dot_general(x, y, (contract, ((), ())), preferred_element_type=F32)
    return dot(a_hi, b_hi) + (dot(a_hi, b_lo) + dot(a_lo, b_hi))


NT, TN, NN = ((1,), (1,)), ((0,), (0,)), ((1,), (0,))


def _hg_gates(hq, hf, lbv):
    sig = _sigmoid(hf)
    f = lbv + (1.0 - lbv) * sig
    return sig, f, jnp.log(f), 1.0 - f, hq * (HG_DK ** -0.5)


def _hg_sel_rows(ref, sp):
    return jnp.concatenate(
        [jnp.broadcast_to(ref[pl.ds(HG_SUB * i + sp, 1), :], (HG_SUB, HG_DK)) for i in range(HG_CHUNK // HG_SUB)], axis=0)


def _hg_masks():
    C = HG_CHUNK
    row = lax.broadcasted_iota(jnp.int32, (C, C), 0)
    col = lax.broadcasted_iota(jnp.int32, (C, C), 1)
    d = col - (row // HG_SUB) * HG_SUB
    tmod = row % HG_SUB
    diag_valid = jnp.logical_and(d >= 0, d <= tmod)
    return row, col, d, diag_valid


def _hg_scores(q, k, b, b_sc, k_sc):
    C, S = HG_CHUNK, HG_SUB
    row, col, d, diag_valid = _hg_masks()
    blocks = [jnp.zeros((S, C), F32)]
    for i in range(1, C // S):
        r = b_sc[pl.ds(S * i - 1, 1), :]
        qi = q[S * i:S * (i + 1)] * jnp.exp(b[S * i:S * (i + 1)] - r)
        kk = k * jnp.exp(jnp.minimum(r - b, 0.0))
        blocks.append(_dot_nt(qi, kk))
    a_off = jnp.where(col < (row // S) * S, jnp.concatenate(blocks, axis=0), 0.0)
    a_d = jnp.zeros((C, C), F32)
    for sp in range(S):
        bs = _hg_sel_rows(b_sc, sp)
        ks = _hg_sel_rows(k_sc, sp)
        e = jnp.exp(jnp.minimum(b - bs, 0.0))
        colv = jnp.sum(q * ks * e, axis=-1, keepdims=True)
        a_d = jnp.where(d == sp, colv, a_d)
    return a_off + jnp.where(diag_valid, a_d, 0.0)


def _hg_prep(hq_v, hf_v, lbv, b_sc, k_sc):
    sig, f, g, k, q = _hg_gates(hq_v, hf_v, lbv)
    b = _cumsum_rows(g)
    b_sc[...] = b
    k_sc[...] = k
    return sig, f, k, q, b, b_sc[pl.ds(HG_CHUNK - 1, 1), :]


def _hgrn_fwd(hq, hf, hi, lb, *, name):
    T = hq.shape[0]
    C, H, K = HG_CHUNK, HG_HEADS, HG_DK
    NC = T // C

    def body(hq_ref, hf_ref, hi_ref, lb_ref, o_ref, st_ref, s_sc, b_sc, k_sc):
        @pl.when(pl.program_id(0) == 0)
        def _():
            s_sc[...] = jnp.zeros_like(s_sc)

        st_all = s_sc[...]
        for j in range(P):
            rows = slice(C * j, C * (j + 1))
            st_ref[j] = st_all
            outs, news = [], []
            for h in range(H):
                sl = slice(K * h, K * (h + 1))
                _, _, k, q, b, bc = _hg_prep(hq_ref[rows, sl], hf_ref[rows, sl], lb_ref[:, sl], b_sc.at[j, h], k_sc.at[j, h])
                v = hi_ref[rows, sl]
                st0 = st_all[:, sl]
                a = _hg_scores(q, k, b, b_sc.at[j, h], k_sc.at[j, h])
                outs.append(_dot_nn(a, v) + _dot_nt(q * jnp.exp(b), st0))
                news.append(st0 * jnp.exp(bc) + _dot_tn(v, k * jnp.exp(bc - b)))
            o_ref[rows, :] = jnp.concatenate(outs, axis=1)
            st_all = jnp.concatenate(news, axis=1)
        s_sc[...] = st_all

    P = HG_FWD_CHUNKS_PER_STEP
    blk = pl.BlockSpec((P * C, H * K), lambda c: (c, 0))
    return pl.pallas_call(
        body, name=name, grid=(NC // P,),
        in_specs=[blk, blk, blk, pl.BlockSpec((1, H * K), lambda c: (0, 0))],
        out_specs=[blk, pl.BlockSpec((P, K, H * K), lambda c: (c, 0, 0))],
        out_shape=[_sds((T, H * K), F32), _sds((NC, K, H * K), F32)],
        scratch_shapes=[pltpu.VMEM((K, H * K), F32), pltpu.VMEM((P, H, C, K), F32), pltpu.VMEM((P, H, C, K), F32)],
        compiler_params=_cp(("arbitrary",)),
    )(hq, hf, hi, lb)


def _hgrn_bwd(hq, hf, hi, lb, states, do, *, name):
    T = hq.shape[0]
    C, H, K, S = HG_CHUNK, HG_HEADS, HG_DK, HG_SUB
    NC = T // C

    def intra_slow(q, k, b, da, b_sc, k_sc):
        row, col, d, diag_valid = _hg_masks()
        a_blocks = [jnp.zeros((S, C), F32)]
        dq_blocks = [jnp.zeros((S, K), F32)]
        dk = jnp.zeros((C, K), F32)
        for i in range(1, C // S):
            r = b_sc[pl.ds(S * i - 1, 1), :]
            eq = jnp.exp(b[S * i:S * (i + 1)] - r)
            ek = jnp.exp(jnp.minimum(r - b, 0.0))
            qi = q[S * i:S * (i + 1)] * eq
            kk = k * ek
            a_blocks.append(_dot_nt(qi, kk))
            dai = jnp.where(col[S * i:S * (i + 1)] < S * i, da[S * i:S * (i + 1)], 0.0)
            dq_blocks.append(_dot_nn(dai, kk) * eq)
            dk = dk + _dot_tn(dai, qi) * ek
        dq = jnp.concatenate(dq_blocks, axis=0)
        a_off = jnp.where(col < (row // S) * S, jnp.concatenate(a_blocks, axis=0), 0.0)
        same_blk = (row // S == col // S).astype(BF16)
        tmod = (lax.broadcasted_iota(jnp.int32, (C, K), 0)) % S
        a_d = jnp.zeros((C, C), F32)
        for sp in range(S):
            bs = _hg_sel_rows(b_sc, sp)
            ks = _hg_sel_rows(k_sc, sp)
            e = jnp.where(tmod >= sp, jnp.exp(jnp.minimum(b - bs, 0.0)), 0.0)
            eks = e * ks
            a_d = jnp.where(d == sp, jnp.sum(q * eks, axis=-1, keepdims=True), a_d)
            dacol = jnp.sum(jnp.where(d == sp, da, 0.0), axis=-1, keepdims=True)
            dq = dq + dacol * eks
            wq = dacol * e * q
            wq_hi = wq.astype(BF16)
            wq_lo = (wq - wq_hi.astype(F32)).astype(BF16)
            blk_sum = (jnp.dot(same_blk, wq_hi, preferred_element_type=F32)
                       + jnp.dot(same_blk, wq_lo, preferred_element_type=F32))
            dk = dk + jnp.where(tmod == sp, blk_sum, 0.0)
        return a_off + jnp.where(diag_valid, a_d, 0.0), dq, dk

    def one_head(pre, v, lbv, st0, dst1, dout, b_sc, k_sc):
        sig, f, k, q, b, bc = pre
        ebc = jnp.exp(bc)
        eb = jnp.exp(b)
        ekb = jnp.exp(bc - b)
        qt = q * eb
        kb = k * ekb
        row = lax.broadcasted_iota(jnp.int32, (C, C), 0)
        col = lax.broadcasted_iota(jnp.int32, (C, C), 1)
        da = jnp.where(col <= row, _dot_nt(dout, v), 0.0)
        dkb = _dot_nn(v, dst1)
        new_ds = _dot_tn(dout, qt) + dst1 * ebc
        a, dq_i, dk_i = intra_slow(q, k, b, da, b_sc, k_sc)
        dq = _dot_nn(dout, st0) * eb + dq_i
        dk = dkb * ekb + dk_i
        dv = _dot_tn(a, dout) + _dot_nt(kb, dst1)
        extra = jnp.sum(dkb * kb, axis=0, keepdims=True) + ebc * jnp.sum(st0 * dst1, axis=0, keepdims=True)
        rowk = lax.broadcasted_iota(jnp.int32, (C, K), 0)
        db = q * dq - k * dk + jnp.where(rowk == C - 1, extra, 0.0)
        dg = _rcumsum_rows(db)
        df = dg / f - dk
        return (dq * (K ** -0.5), df * (1.0 - lbv) * sig * (1.0 - sig), dv,
                jnp.sum(df * (1.0 - sig), axis=0, keepdims=True), new_ds)

    def body(hq_ref, hf_ref, hi_ref, lb_ref, st_ref, do_ref, dq_ref, dhf_ref, dv_ref, dlb_ref, ds_sc, b_sc, k_sc):
        @pl.when(pl.program_id(0) == 0)
        def _():
            ds_sc[...] = jnp.zeros_like(ds_sc)
            dlb_ref[...] = jnp.zeros_like(dlb_ref)

        ds_all = ds_sc[...]
        dlb = jnp.zeros((1, H * K), F32)
        for j in reversed(range(P)):
            rows = slice(C * j, C * (j + 1))
            st_all = st_ref[j]
            res = []
            for h in range(H):
                sl = slice(K * h, K * (h + 1))
                pre = _hg_prep(hq_ref[rows, sl], hf_ref[rows, sl], lb_ref[:, sl], b_sc.at[j, h], k_sc.at[j, h])
                res.append(one_head(pre, hi_ref[rows, sl], lb_ref[:, sl], st_all[:, sl], ds_all[:, sl], do_ref[rows, sl],
                                    b_sc.at[j, h], k_sc.at[j, h]))
            cat = lambda i: jnp.concatenate([r[i] for r in res], axis=1)
            dq_ref[rows, :] = cat(0).astype(dq_ref.dtype)
            dhf_ref[rows, :] = cat(1).astype(dhf_ref.dtype)
            dv_ref[rows, :] = cat(2).astype(dv_ref.dtype)
            dlb = dlb + cat(3)
            ds_all = cat(4)
        dlb_ref[...] += dlb
        ds_sc[...] = ds_all

    P = HG_CHUNKS_PER_STEP
    NS = NC // P
    blk = pl.BlockSpec((P * C, H * K), lambda c: (NS - 1 - c, 0))
    par = pl.BlockSpec((1, H * K), lambda c: (0, 0))
    return pl.pallas_call(
        body, name=name, grid=(NS,),
        in_specs=[blk, blk, blk, par, pl.BlockSpec((P, K, H * K), lambda c: (NS - 1 - c, 0, 0)), blk],
        out_specs=[blk, blk, blk, par],
        out_shape=[_sds((T, H * K), BF16)] * 3 + [_sds((1, H * K), F32)],
        scratch_shapes=[pltpu.VMEM((K, H * K), F32), pltpu.VMEM((P, H, C, K), F32), pltpu.VMEM((P, H, C, K), F32)],
        compiler_params=_cp(("arbitrary",)),
    )(hq, hf, hi, lb, states, do)


ATT_STACK = ATT_GROUP


def _att_valid(n):
    R, B = ATT_STACK * ATT_BLOCK, ATT_BLOCK
    j = lax.broadcasted_iota(jnp.int32, (2 * B, R), 0)
    t = lax.broadcasted_iota(jnp.int32, (2 * B, R), 1) % B
    dist = t + B - j
    first_key = jnp.where(n > 0, 0, B)
    return jnp.logical_and(jnp.logical_and(dist >= 0, dist < B), j >= first_key)


def _att_load(cur_ref, prev_ref, ba_ref, h0):
    hd = ATT_HD
    kv = h0 // ATT_GROUP
    def cols(ref, c0):
        return ref[:, c0:c0 + hd] + ba_ref[:, c0:c0 + hd]
    qs = jnp.concatenate([cols(cur_ref, hd * (h0 + g)) for g in range(ATT_STACK)], axis=0)
    kc = jnp.concatenate([cols(prev_ref, ATT_Q_W + hd * kv), cols(cur_ref, ATT_Q_W + hd * kv)], axis=0)
    vc = jnp.concatenate([cols(prev_ref, ATT_Q_W + ATT_KV_W + hd * kv), cols(cur_ref, ATT_Q_W + ATT_KV_W + hd * kv)], axis=0)
    return qs, kc, vc


def _att_probs(qs, kc, valid, sink_ref, h0):
    scale = 1.0 / math.sqrt(ATT_HD)
    s = jnp.where(valid, _dot_nt(kc, qs) * scale, NEG)
    sink = jnp.concatenate([jnp.full((1, ATT_BLOCK), sink_ref[0, h0 + g], F32) for g in range(ATT_STACK)], axis=1)
    m = jnp.maximum(jnp.max(s, axis=0, keepdims=True), sink)
    p = jnp.exp(s - m)
    ps = jnp.exp(sink - m)
    inv = 1.0 / (jnp.sum(p, axis=0, keepdims=True) + ps)
    return p * inv, ps * inv


def _attn_fwd(att, b_attn, sinks, *, name, after=None):
    T = att.shape[0]
    B = ATT_BLOCK
    NB = T // B
    lead = [] if after is None else [after]

    def body(*refs):
        sink_ref, cur_ref, prev_ref, ba_ref, o_ref = refs[len(lead):]
        valid = _att_valid(pl.program_id(0))
        outs = []
        for h0 in range(0, ATT_HEADS, ATT_STACK):
            qs, kc, vc = _att_load(cur_ref, prev_ref, ba_ref, h0)
            prob, _ = _att_probs(qs, kc, valid, sink_ref, h0)
            o = _dot_tn(prob, vc)
            outs += [o[B * g:B * (g + 1)] for g in range(ATT_STACK)]
        o_ref[...] = jnp.concatenate(outs, axis=1)

    return pl.pallas_call(
        body, name=name, grid=(NB,),
        in_specs=[pl.BlockSpec(memory_space=pl.ANY)] * len(lead) + [
            pl.BlockSpec(memory_space=pltpu.SMEM),
            pl.BlockSpec((B, ATT_COLS), lambda n: (n, 0)),
            pl.BlockSpec((B, ATT_COLS), lambda n: (jnp.maximum(n - 1, 0), 0)),
            pl.BlockSpec((1, ATT_COLS), lambda n: (0, 0))],
        out_specs=pl.BlockSpec((B, ATT_Q_W), lambda n: (n, 0)),
        out_shape=_sds((T, ATT_Q_W), F32),
        compiler_params=_cp(("parallel",)),
    )(*lead, sinks, att, att, b_attn)


def _attn_bwd(att, b_attn, sinks, dmix, *, name):
    T = att.shape[0]
    B, hd = ATT_BLOCK, ATT_HD
    NB = T // B
    scale = 1.0 / math.sqrt(hd)

    def body(sink_ref, cur_ref, prev_ref, ba_ref, do_ref, daq_ref, dakv_ref, dsink_ref, dbq_ref, dbkv_ref, carry_sc):
        n = pl.program_id(0)

        @pl.when(n == 0)
        def _():
            carry_sc[...] = jnp.zeros_like(carry_sc)
            dsink_ref[...] = jnp.zeros_like(dsink_ref)
            dbq_ref[...] = jnp.zeros_like(dbq_ref)
            dbkv_ref[...] = jnp.zeros_like(dbkv_ref)

        @pl.when(n < NB)
        def _():
            valid = _att_valid(n)
            hrow = lax.broadcasted_iota(jnp.int32, (SUBLANES, 128), 0)
            dsink = jnp.zeros((SUBLANES, 128), F32)
            dqs = []
            dks = [jnp.zeros((2 * B, hd), F32)] * ATT_KV
            dvs = [jnp.zeros((2 * B, hd), F32)] * ATT_KV
            for h0 in range(0, ATT_HEADS, ATT_STACK):
                kv = h0 // ATT_GROUP
                qs, kc, vc = _att_load(cur_ref, prev_ref, ba_ref, h0)
                prob, psink = _att_probs(qs, kc, valid, sink_ref, h0)
                dout = jnp.concatenate([do_ref[:, hd * (h0 + g):hd * (h0 + g + 1)] for g in range(ATT_STACK)], axis=0)
                dp = _dot_nt(vc, dout)
                delta = jnp.sum(prob * dp, axis=0, keepdims=True)
                dsc = prob * (dp - delta) * scale
                dq = _dot_tn(dsc, kc)
                dks[kv] = dks[kv] + _dot_nn(dsc, qs)
                dvs[kv] = dvs[kv] + _dot_nn(prob, dout)
                dsk = psink * delta
                for g in range(ATT_STACK):
                    dqs.append(dq[B * g:B * (g + 1)])
                    tot = jnp.sum(dsk[:, B * g:B * (g + 1)], axis=1, keepdims=True)
                    dsink = dsink - jnp.where(hrow == h0 + g, tot, 0.0)
            daq = jnp.concatenate(dqs, axis=1).astype(daq_ref.dtype)
            daq_ref[...] = daq
            dsink_ref[...] += dsink
            dbq_ref[...] += jnp.sum(daq.astype(F32), axis=0, keepdims=True)
            done = carry_sc[...] + jnp.concatenate([d[:B] for d in dks + dvs], axis=1)
            dakv_ref[...] = done.astype(dakv_ref.dtype)
            dbkv_ref[...] += jnp.sum(done.astype(dakv_ref.dtype).astype(F32), axis=0, keepdims=True)
            carry_sc[...] = jnp.concatenate([d[B:] for d in dks + dvs], axis=1)

        @pl.when(n == NB)
        def _():
            done = carry_sc[...]
            dakv_ref[...] = done.astype(dakv_ref.dtype)
            dbkv_ref[...] += jnp.sum(done.astype(dakv_ref.dtype).astype(F32), axis=0, keepdims=True)

    cl = lambda n: jnp.minimum(n, NB - 1)
    return pl.pallas_call(
        body, name=name, grid=(NB + 1,),
        in_specs=[pl.BlockSpec(memory_space=pltpu.SMEM),
                  pl.BlockSpec((B, ATT_COLS), lambda n: (cl(n), 0)),
                  pl.BlockSpec((B, ATT_COLS), lambda n: (jnp.maximum(cl(n) - 1, 0), 0)),
                  pl.BlockSpec((1, ATT_COLS), lambda n: (0, 0)),
                  pl.BlockSpec((B, ATT_Q_W), lambda n: (cl(n), 0))],
        out_specs=[pl.BlockSpec((B, ATT_Q_W), lambda n: (cl(n), 0)),
                   pl.BlockSpec((B, 2 * ATT_KV_W), lambda n: (jnp.maximum(n - 1, 0), 0)),
                   pl.BlockSpec((SUBLANES, 128), lambda n: (0, 0)),
                   pl.BlockSpec((1, ATT_Q_W), lambda n: (0, 0)),
                   pl.BlockSpec((1, 2 * ATT_KV_W), lambda n: (0, 0))],
        out_shape=[_sds((T, ATT_Q_W), BF16), _sds((T, 2 * ATT_KV_W), BF16), _sds((SUBLANES, 128), F32),
                   _sds((1, ATT_Q_W), F32), _sds((1, 2 * ATT_KV_W), F32)],
        scratch_shapes=[pltpu.VMEM((B, 2 * ATT_KV_W), F32)],
        compiler_params=_cp(("arbitrary",)),
    )(sinks, att, att, b_attn, dmix)


def _silu_and_grad(x):
    sg = _sigmoid(x)
    return x * sg, sg * (1.0 + x * (1.0 - sg))


def _mix_fwd_fn(o_raw, hg, o_att, hgw):
    outs = []
    for h in range(HG_HEADS):
        sl = slice(HG_DK * h, HG_DK * (h + 1))
        silu, _ = _silu_and_grad(hg[:, sl])
        outs.append(_rms_fwd(o_raw[:, sl], hgw) * silu)
    outs.append(o_att)
    return (jnp.concatenate(outs, axis=1),)


def _mix_bwd_fn(o_raw, hg, dmix, hgw):
    dos, dhgs = [], []
    dw = jnp.zeros((1, HG_DK), F32)
    for h in range(HG_HEADS):
        sl = slice(HG_DK * h, HG_DK * (h + 1))
        silu, dsilu = _silu_and_grad(hg[:, sl])
        dy = dmix[:, sl]
        dhgs.append(dy * _rms_fwd(o_raw[:, sl], hgw) * dsilu)
        dx, dwh = _rms_bwd(o_raw[:, sl], hgw, dy * silu)
        dos.append(dx)
        dw = dw + dwh
    return jnp.concatenate(dos, axis=1), jnp.concatenate(dhgs, axis=1), dw


def _final_fn(h2, tgt, wf):
    d = h2.shape[1]
    err = _rms_fwd(h2, wf) - tgt
    loss_cols = (0.5 / d) * jnp.sum(err * err, axis=0, keepdims=True)
    dh2, dwf = _rms_bwd(h2, wf, err * (1.0 / d))
    return dh2, dh2, loss_cols, dwf


class _NoExchange:
    def __init__(self, weights):
        self.weights = weights

    def start(self):
        return None

    def w_in(self, after):
        return self.weights["w_in_t"]

    def mid(self, after):
        return None

    def rest(self, after):
        return self.weights

    def ffn_grads(self, gs):
        return None

    def ffn_grads_send(self, after):
        return None


def _local_step(x, tgt, p, ex):
    T, D = x.shape
    row = lambda n, dt: _sds((T, n), dt)
    acc = lambda n: _sds((1, n), F32)

    (u,) = _rowwise(lambda xv, w: (_rms_fwd(xv, w),), [_full(x)], [p["norm_mix_w"]], [row(D, BF16)], [], name="rms_mix",
                    after=ex.start())
    p = dict(p, w_in_t=ex.w_in(u))
    hq, hf, hi, hg, att = _mm_nt(u, p["w_in_t"], splits=[HG_W] * 4 + [ATT_COLS], out_dtype=F32, name="in_proj")
    o_raw, states = _hgrn_fwd(hq, hf, hi, p["lb"], name="hgrn_fwd")
    o_att = _attn_fwd(att, p["b_attn"], p["sinks"], name="attn_fwd", after=ex.mid(o_raw))
    p = dict(p, **ex.rest(o_att))
    def out_epilogue(prod, xv, w):
        h1v = prod + xv
        return h1v, _rms_fwd(h1v, w)

    h1, v, mix = _mm_nn(None, [p["w_out"]], name="mix_out_proj",
                        prologue=(lambda *a: _mix_fwd_fn(*a)[0], [o_raw, hg, o_att], [p["hg_norm_w"]], row(D, BF16)),
                        epilogue=(out_epilogue, [x], [p["norm_ffn_w"]], [row(D, F32), row(D, BF16)], []))
    (gp,) = _mm_nt(v, p["w_gate_t"], splits=[D_FF], out_dtype=F32, name="gate_proj")
    (up,) = _mm_nt(v, p["w_up_t"], splits=[D_FF], out_dtype=F32, name="up_proj")
    act = _convact_fwd(gp, up, p["conv_w8"], p["conv_b"], name="convact_fwd")
    def down_epilogue(prod, h1v, tgtv, wf):
        return _final_fn(prod + h1v, tgtv, wf)

    dh2, dh2_b, loss_cols, d_final = _mm_nn(
        [[act]], [p["w_down"]], name="down_proj_loss",
        epilogue=(down_epilogue, [h1, tgt], [p["final_norm_w"]], [row(D, F32), row(D, BF16)], [acc(D), acc(D)]))

    (dact,) = _mm_nt(dh2_b, p["w_down"], splits=[D_FF], out_dtype=F32, name="d_act")
    g_down = _mm_tn([act], dh2_b, name="g_down")
    dgp, dup, d_conv_w8, d_conv_b = _convact_bwd(gp, up, dact, p["conv_w8"], p["conv_b"], name="convact_bwd")
    g_gate_t = _mm_tn([dgp], v, name="g_gate")
    g_up_t = _mm_tn([dup], v, name="g_up")
    swapping = ex.ffn_grads([g_gate_t, g_up_t, g_down])

    def ffn_norm_bwd(dvv, hv, dh2v, w):
        dx, dw = _rms_bwd(hv, w, dvv)
        dh1v = dx + dh2v
        return dh1v, dh1v, dw

    dh1, dh1_b, d_norm_ffn = _mm_nn(
        [[dgp], [dup]], [p["w_gate_t"], p["w_up_t"]], name="d_v_norm", after=swapping,
        epilogue=(ffn_norm_bwd, [h1, dh2], [p["norm_ffn_w"]], [row(D, F32), row(D, BF16)], [acc(D)]))
    sent = ex.ffn_grads_send(dh1_b)
    def mix_bwd(dmixv, o_rawv, hgv, hgw):
        do_rawv, dhgv, dw = _mix_bwd_fn(o_rawv, hgv, dmixv[:, :HG_W], hgw)
        return do_rawv, dhgv, dmixv[:, HG_W:], dw

    do_raw, dhg, do_att, d_hg_norm = _mm_nn(
        [[dh1_b]], [p["w_out"]], name="d_mix_bwd", w_transposed=True, after=sent,
        epilogue=(mix_bwd, [o_raw, hg], [p["hg_norm_w"]], [row(HG_W, F32), row(HG_W, BF16), row(ATT_Q_W, F32)], [acc(HG_DK)]))
    g_out = _mm_tn([mix], dh1_b, name="g_out")
    daq, dakv, d_sinks8, d_bq, d_bkv = _attn_bwd(att, p["b_attn"], p["sinks"], do_att, name="attn_bwd")
    dhq, dhf, dhi, d_lb = _hgrn_bwd(hq, hf, hi, p["lb"], states, do_raw, name="hgrn_bwd")
    pieces = [dhq, dhf, dhi, dhg, daq, dakv]
    g_in_t = _mm_tn(pieces, u, name="g_in")

    def mix_norm_bwd(duv, xv, dh1v, w):
        dx, dw = _rms_bwd(xv, w, duv)
        return dx + dh1v, dw

    dx, d_norm_mix = _mm_nn([pieces], [p["w_in_t"]], name="d_u_norm",
                            epilogue=(mix_norm_bwd, [x, dh1], [p["norm_mix_w"]], [row(D, F32)], [acc(D)]))
    grads = dict(g_in_t=g_in_t, g_out=g_out, g_gate_t=g_gate_t, g_up_t=g_up_t, g_down=g_down,
                 norm_mix_w=d_norm_mix, b_attn=jnp.concatenate([d_bq, d_bkv], axis=1), lb=d_lb, hg_norm_w=d_hg_norm,
                 sinks8=d_sinks8, norm_ffn_w=d_norm_ffn, conv_w8=d_conv_w8, conv_b=d_conv_b, final_norm_w=d_final)
    return loss_cols, dx, grads


SLAB = (IN_COLS // N_CHIPS, D_FF // N_CHIPS, D_FF // N_CHIPS, D_FF // N_CHIPS, D_MODEL // N_CHIPS)
N_W = len(SLAB)
PACK_OFF = tuple(sum(SLAB[:i]) for i in range(N_W))
PACK_ROWS = sum(SLAB)
FULL_OFF = tuple(N_CHIPS * o for o in PACK_OFF)
FULL_ROWS = N_CHIPS * PACK_ROWS
HALF = tuple(s // 2 for s in SLAB)
HPACK_OFF = tuple(sum(HALF[:i]) for i in range(N_W))
HPACK_ROWS = sum(HALF)
HFULL_OFF = tuple(N_CHIPS * o for o in HPACK_OFF)
HFULL_ROWS = N_CHIPS * HPACK_ROWS
CHIP_FLIPS = ((1, 0), (0, 1), (1, 1))
N_DEV = 8
BF16_ROWS = 16
ANY = pl.BlockSpec(memory_space=pl.ANY)


def _pos():
    return lax.axis_index("x"), lax.axis_index("y"), lax.axis_index("c")


def _flip(v, f):
    return 1 - v if f else v


def _rcopy(src, dst, ssem, rsem, dev):
    return pltpu.make_async_remote_copy(src_ref=src, dst_ref=dst, send_sem=ssem, recv_sem=rsem, device_id=dev,
                                        device_id_type=pl.DeviceIdType.MESH)


def _rows(ref, start, n, align=None):
    if not isinstance(start, int):
        if align is None:
            align = SUBLANES * (4 // jnp.dtype(ref.dtype).itemsize)
        start = pl.multiple_of(start, align)
    return ref.at[pl.ds(start, n), :]


FFN_W = (1, 2, 3)
N_PEER = 1 + len(CHIP_FLIPS)
HBM = pl.BlockSpec(memory_space=pltpu.HBM)
SEM = pl.BlockSpec(memory_space=pltpu.SEMAPHORE)
EFFECT = pltpu.SideEffectType.DATAFLOW_SIDE_EFFECTING
LANES = 128


def _sent_rows(k, w, c):
    return (0, SLAB[w]) if k == 0 else (c * HALF[w], HALF[w])


def _gather_start(pack, cw8):
    D = pack.shape[1]
    lands = [lax.empty((N_CHIPS * SLAB[0], D), pack.dtype), lax.empty((3 * N_CHIPS * SLAB[1], D), pack.dtype),
             lax.empty((N_CHIPS * SLAB[4], D), pack.dtype), lax.empty((N_CHIPS,) + cw8.shape, cw8.dtype)]
    bufs = [pack, cw8] + lands

    def body(pack_ref, cw_ref, l_in, l_ffn, l_out, l_cw, *rest):
        in_send, in_recv, out_send, out_recv, ffn_send, ffn_recv = rest[:6]
        token = rest[-1]
        x, y, c = _pos()
        q = 2 * x + y
        peers = _gather_peers(x, y, c)

        def send(k, peer, w, land, base, ssem, rsem):
            r0, n = _sent_rows(k, w, c)
            _rcopy(_rows(pack_ref, PACK_OFF[w] + r0, n), _rows(land, base + q * SLAB[w] + r0, n), ssem, rsem, peer).start()

        for k, peer in enumerate(peers):
            send(k, peer, 0, l_in, 0, in_send.at[k], in_recv.at[k])
        for k, peer in enumerate(peers):
            send(k, peer, 4, l_out, 0, out_send.at[k], out_recv.at[k])
            _rcopy(cw_ref, l_cw.at[q], out_send.at[N_PEER + k], out_recv.at[N_PEER + k], peer).start()
        for j, w in enumerate(FFN_W):
            for k, peer in enumerate(peers):
                send(k, peer, w, l_ffn, j * N_CHIPS * SLAB[w], ffn_send.at[k], ffn_recv.at[k])
        token[...] = jnp.zeros_like(token)

    n_sem = (N_PEER, N_PEER, 2 * N_PEER, 2 * N_PEER, N_PEER, N_PEER)
    outs = pl.pallas_call(
        body, name="gather_start", in_specs=[HBM] * len(bufs),
        out_specs=[SEM] * len(n_sem) + [HBM] * len(bufs) + [pl.BlockSpec(memory_space=pltpu.VMEM)],
        out_shape=[pltpu.SemaphoreType.DMA((n,)) for n in n_sem]
        + [pltpu.HBM(b.shape, b.dtype) for b in bufs] + [_sds((SUBLANES, LANES), F32)],
        input_output_aliases={i: len(n_sem) + i for i in range(len(bufs))},
        compiler_params=pltpu.CompilerParams(has_side_effects=EFFECT),
    )(*[pltpu.with_memory_space_constraint(b, pltpu.HBM) for b in bufs])
    bufs_out = outs[len(n_sem):]
    return dict(in_sems=outs[0:2], out_sems=outs[2:4], ffn_sems=outs[4:6], pack=bufs_out[0], cw=bufs_out[1], l_in=bufs_out[2],
                l_ffn=bufs_out[3], l_out=bufs_out[4], l_cw=bufs_out[5], token=bufs_out[6])


def _gather_peers(x, y, c):
    return [(x, y, 1 - c)] + [(_flip(x, fx), _flip(y, fy), c) for fx, fy in CHIP_FLIPS]


def _gather_wait_in(g, after):
    def body(pack_ref, l_in, send, recv, after_ref, pack_out, l_out):
        for k, peer in enumerate(_gather_peers(*_pos())):
            n = _sent_rows(k, 0, 0)[1]
            cp = _rcopy(_rows(pack_ref, PACK_OFF[0], n), _rows(l_in, 0, n), send.at[k], recv.at[k], peer)
            cp.wait_send()
            cp.wait_recv()

    return pl.pallas_call(
        body, name="gather_wait_in", in_specs=[HBM, HBM, SEM, SEM, ANY], out_specs=[HBM, HBM],
        out_shape=[pltpu.HBM(g["pack"].shape, g["pack"].dtype), pltpu.HBM(g["l_in"].shape, g["l_in"].dtype)],
        input_output_aliases={0: 0, 1: 1}, compiler_params=pltpu.CompilerParams(has_side_effects=EFFECT),
    )(g["pack"], g["l_in"], *g["in_sems"], after)


def _gather_wait_rest(g, pack, after):
    def body(pack_ref, cw_ref, l_ffn, l_out, l_cw, o_send, o_recv, f_send, f_recv, after_ref, o_ffn, o_out, o_cw):
        for k, peer in enumerate(_gather_peers(*_pos())):
            n_out = _sent_rows(k, 4, 0)[1]
            n_ffn = len(FFN_W) * _sent_rows(k, FFN_W[0], 0)[1]
            for cp in (_rcopy(_rows(pack_ref, PACK_OFF[4], n_out), _rows(l_out, 0, n_out), o_send.at[k], o_recv.at[k], peer),
                       _rcopy(cw_ref, l_cw.at[0], o_send.at[N_PEER + k], o_recv.at[N_PEER + k], peer),
                       _rcopy(_rows(pack_ref, PACK_OFF[FFN_W[0]], n_ffn), _rows(l_ffn, 0, n_ffn), f_send.at[k], f_recv.at[k], peer)):
                cp.wait_send()
                cp.wait_recv()

    ins = [pack, g["cw"], g["l_ffn"], g["l_out"], g["l_cw"]]
    return pl.pallas_call(
        body, name="gather_wait_rest", in_specs=[HBM] * 5 + [SEM] * 4 + [ANY], out_specs=[HBM] * 3,
        out_shape=[pltpu.HBM(b.shape, b.dtype) for b in ins[2:]],
        input_output_aliases={2: 0, 3: 1, 4: 2}, compiler_params=pltpu.CompilerParams(has_side_effects=EFFECT),
    )(*ins, *g["out_sems"], *g["ffn_sems"], after)


FWD_IN = ((0, 0, 0),)
FWD_REST = tuple((0, w, j * N_CHIPS * SLAB[w]) for j, w in enumerate(FFN_W)) + ((1, 4, 0),)


def _forward_copies(layout, src, dst, send_sems, recv_sems):
    x, y, c = _pos()
    sib = (x, y, 1 - c)
    cps = []
    for fx, fy in CHIP_FLIPS:
        qa = 2 * _flip(x, fx) + _flip(y, fy)
        for bi, w, base in layout:
            r0 = base + qa * SLAB[w] + c * HALF[w]
            cps.append(_rcopy(_rows(src[bi], r0, HALF[w]), _rows(dst[bi], r0, HALF[w]),
                              send_sems.at[len(cps)], recv_sems.at[len(cps)], sib))
    return cps


def _forward_in(l_in):
    n = len(CHIP_FLIPS) * len(FWD_IN)

    def body(in_ref, out_ref, send_sems, recv_sems):
        cps = _forward_copies(FWD_IN, [in_ref], [out_ref], send_sems, recv_sems)
        for cp in cps:
            cp.start()
        for cp in cps:
            cp.wait_recv()
        for cp in cps:
            cp.wait_send()

    return pl.pallas_call(
        body, name="forward_in", in_specs=[ANY], out_specs=ANY, out_shape=_sds(l_in.shape, l_in.dtype),
        input_output_aliases={0: 0},
        scratch_shapes=[pltpu.SemaphoreType.DMA((n,)), pltpu.SemaphoreType.DMA((n,))],
    )(l_in)


def _forward_rest_start(l_ffn, l_out):
    n = len(CHIP_FLIPS) * len(FWD_REST)
    bufs = [l_ffn, l_out]

    def body(a_ref, b_ref, send_sems, recv_sems, a_out, b_out, token):
        for cp in _forward_copies(FWD_REST, [a_ref, b_ref], [a_ref, b_ref], send_sems, recv_sems):
            cp.start()
        token[...] = jnp.zeros_like(token)

    outs = pl.pallas_call(
        body, name="forward_rest_start", in_specs=[HBM] * 2,
        out_specs=[SEM, SEM, HBM, HBM, pl.BlockSpec(memory_space=pltpu.VMEM)],
        out_shape=[pltpu.SemaphoreType.DMA((n,)), pltpu.SemaphoreType.DMA((n,))]
        + [pltpu.HBM(b.shape, b.dtype) for b in bufs] + [_sds((SUBLANES, LANES), F32)],
        input_output_aliases={0: 2, 1: 3}, compiler_params=pltpu.CompilerParams(has_side_effects=EFFECT),
    )(*[pltpu.with_memory_space_constraint(b, pltpu.HBM) for b in bufs])
    return dict(sems=outs[0:2], bufs=outs[2:4], token=outs[4])


def _forward_rest_wait(s, after):
    def body(a_ref, b_ref, send_sems, recv_sems, after_ref, a_out, b_out):
        for cp in _forward_copies(FWD_REST, [a_ref, b_ref], [a_ref, b_ref], send_sems, recv_sems):
            cp.wait_send()
            cp.wait_recv()

    return pl.pallas_call(
        body, name="forward_rest_wait", in_specs=[HBM, HBM, SEM, SEM, ANY], out_specs=[HBM, HBM],
        out_shape=[pltpu.HBM(b.shape, b.dtype) for b in s["bufs"]],
        input_output_aliases={0: 0, 1: 1}, compiler_params=pltpu.CompilerParams(has_side_effects=EFFECT),
    )(*s["bufs"], *s["sems"], after)


def _exchange_halves(ws, gs, small, *, name):
    D = gs[0].shape[1]
    n = len(ws)
    has_small = small is not None

    def body(*refs):
        g = refs[:n]
        t = refs[n + has_small:2 * n + has_small]
        sems = refs[2 * n + 2 * has_small:]
        d2d_send, d2d_recv = sems[0], sems[1]
        x, y, c = _pos()
        sib = (x, y, 1 - c)
        drains = []
        for i, w in enumerate(ws):
            h = HALF[w]
            for qq in range(N_CHIPS):
                _rcopy(_rows(g[i], qq * SLAB[w] + (1 - c) * h, h), _rows(t[i], qq * h, h),
                       d2d_send.at[i], d2d_recv.at[i], sib).start()
            drains.append(_rcopy(t[i], t[i], d2d_send.at[i], d2d_recv.at[i], sib))
        if has_small:
            small_ref, sall_ref = refs[n], refs[2 * n + 1]
            sm_send, sm_recv, loc_sem = sems[2], sems[3], sems[4]
            me = 4 * x + 2 * y + c
            own_small = pltpu.make_async_copy(small_ref, sall_ref.at[me], loc_sem)
            own_small.start()
            for f in range(1, N_DEV):
                peer = (_flip(x, f & 4), _flip(y, f & 2), _flip(c, f & 1))
                cp = _rcopy(small_ref, sall_ref.at[me], sm_send.at[f - 1], sm_recv.at[f - 1], peer)
                cp.start()
                drains.append(cp)
        for d in drains:
            d.wait_recv()
        for d in drains:
            d.wait_send()
        if has_small:
            own_small.wait()

    out_shape = [_sds((N_CHIPS * HALF[w], D), gs[0].dtype) for w in ws]
    scratch = [pltpu.SemaphoreType.DMA((n,)), pltpu.SemaphoreType.DMA((n,))]
    if has_small:
        out_shape.append(_sds((N_DEV,) + small.shape, F32))
        scratch += [pltpu.SemaphoreType.DMA((N_DEV - 1,)), pltpu.SemaphoreType.DMA((N_DEV - 1,)), pltpu.SemaphoreType.DMA]
    return pl.pallas_call(
        body, name=name, in_specs=[ANY] * (n + has_small), out_specs=[ANY] * (n + has_small),
        out_shape=out_shape, scratch_shapes=scratch,
    )(*gs, *([small] if has_small else []))


def _halves_copies(ws, g, t, send_sems, recv_sems):
    x, y, c = _pos()
    sib = (x, y, 1 - c)
    cps = []
    for i, w in enumerate(ws):
        h = HALF[w]
        for qq in range(N_CHIPS):
            cps.append(_rcopy(_rows(g[i], qq * SLAB[w] + (1 - c) * h, h), _rows(t[i], qq * h, h),
                              send_sems.at[N_CHIPS * i + qq], recv_sems.at[N_CHIPS * i + qq], sib))
    return cps


def _halves_start(ws, gs, *, name):
    D = gs[0].shape[1]
    n = len(ws)
    bufs = list(gs) + [lax.empty((N_CHIPS * HALF[w], D), gs[0].dtype) for w in ws]

    def body(*refs):
        for cp in _halves_copies(ws, refs[:n], refs[n:2 * n], refs[2 * n], refs[2 * n + 1]):
            cp.start()
        refs[-1][...] = jnp.zeros_like(refs[-1])

    outs = pl.pallas_call(
        body, name=name, in_specs=[HBM] * (2 * n),
        out_specs=[SEM, SEM] + [HBM] * (2 * n) + [pl.BlockSpec(memory_space=pltpu.VMEM)],
        out_shape=[pltpu.SemaphoreType.DMA((N_CHIPS * n,)), pltpu.SemaphoreType.DMA((N_CHIPS * n,))]
        + [pltpu.HBM(b.shape, b.dtype) for b in bufs] + [_sds((SUBLANES, LANES), F32)],
        input_output_aliases={i: 2 + i for i in range(2 * n)},
        compiler_params=pltpu.CompilerParams(has_side_effects=EFFECT),
    )(*[pltpu.with_memory_space_constraint(b, pltpu.HBM) for b in bufs])
    return dict(sems=outs[0:2], gs=outs[2:2 + n], theirs=outs[2 + n:2 + 2 * n], token=outs[-1])


def _halves_wait(ws, s, after, *, name):
    n = len(ws)

    def body(*refs):
        for cp in _halves_copies(ws, refs[:n], refs[n:2 * n], refs[2 * n], refs[2 * n + 1]):
            cp.wait_send()
            cp.wait_recv()

    bufs = list(s["gs"]) + list(s["theirs"])
    outs = pl.pallas_call(
        body, name=name, in_specs=[HBM] * (2 * n) + [SEM, SEM, ANY], out_specs=[HBM] * (2 * n),
        out_shape=[pltpu.HBM(b.shape, b.dtype) for b in bufs],
        input_output_aliases={i: i for i in range(2 * n)},
        compiler_params=pltpu.CompilerParams(has_side_effects=EFFECT),
    )(*bufs, *s["sems"], after)
    return outs[:n], outs[n:]


REDUCE_SPLIT = 2


def _chip_partial(ws, gs, theirs, *, name, out_dtype=F32):
    D = gs[0].shape[1]
    n = len(ws)

    def body(*refs):
        for i in range(n):
            refs[2 * n + i][...] = (refs[i][...].astype(F32) + refs[n + i][...].astype(F32)).astype(out_dtype)

    blk = [HALF[w] // REDUCE_SPLIT for w in ws]
    mine = [pl.BlockSpec((b, D), lambda qq, j: ((2 * qq + lax.axis_index("c")) * REDUCE_SPLIT + j, 0)) for b in blk]
    flat = [pl.BlockSpec((b, D), lambda qq, j: (qq * REDUCE_SPLIT + j, 0)) for b in blk]
    return pl.pallas_call(
        body, name=name, grid=(N_CHIPS, REDUCE_SPLIT), in_specs=mine + flat, out_specs=flat,
        out_shape=[_sds((N_CHIPS * HALF[w], D), out_dtype) for w in ws],
        compiler_params=_cp(("parallel", "parallel")),
    )(*gs, *theirs)


def _partial_copies(ws, part, got, send_sems, recv_sems):
    x, y, c = _pos()
    cps = []
    for k, (fx, fy) in enumerate(CHIP_FLIPS):
        peer = (_flip(x, fx), _flip(y, fy), c)
        qp = 2 * _flip(x, fx) + _flip(y, fy)
        for i, w in enumerate(ws):
            cps.append(_rcopy(_rows(part[i], qp * HALF[w], HALF[w]), _rows(got[i], k * HALF[w], HALF[w]),
                              send_sems.at[len(ws) * k + i], recv_sems.at[len(ws) * k + i], peer))
    return cps


def _send_chip_partials(ws, parts, *, name):
    D = parts[0].shape[1]
    n = len(ws)

    def body(*refs):
        cps = _partial_copies(ws, refs[:n], refs[n:2 * n], refs[2 * n], refs[2 * n + 1])
        for cp in cps:
            cp.start()
        for cp in cps:
            cp.wait_recv()
        for cp in cps:
            cp.wait_send()

    return pl.pallas_call(
        body, name=name, in_specs=[ANY] * n, out_specs=[ANY] * n,
        out_shape=[_sds((len(CHIP_FLIPS) * HALF[w], D), parts[0].dtype) for w in ws],
        scratch_shapes=[pltpu.SemaphoreType.DMA((len(CHIP_FLIPS) * n,)), pltpu.SemaphoreType.DMA((len(CHIP_FLIPS) * n,))],
    )(*parts)


def _send_start(ws, parts, *, name):
    D = parts[0].shape[1]
    n = len(ws)
    bufs = list(parts) + [lax.empty((len(CHIP_FLIPS) * HALF[w], D), parts[0].dtype) for w in ws]

    def body(*refs):
        send_sems, recv_sems = refs[2 * n], refs[2 * n + 1]
        for cp in _partial_copies(ws, refs[:n], refs[n:2 * n], send_sems, recv_sems):
            cp.start()
        refs[-1][...] = jnp.zeros_like(refs[-1])

    outs = pl.pallas_call(
        body, name=name, in_specs=[HBM] * (2 * n),
        out_specs=[SEM, SEM] + [HBM] * (2 * n) + [pl.BlockSpec(memory_space=pltpu.VMEM)],
        out_shape=[pltpu.SemaphoreType.DMA((len(CHIP_FLIPS) * n,)), pltpu.SemaphoreType.DMA((len(CHIP_FLIPS) * n,))]
        + [pltpu.HBM(b.shape, b.dtype) for b in bufs] + [_sds((SUBLANES, LANES), F32)],
        input_output_aliases={i: 2 + i for i in range(2 * n)},
        compiler_params=pltpu.CompilerParams(has_side_effects=EFFECT),
    )(*[pltpu.with_memory_space_constraint(b, pltpu.HBM) for b in bufs])
    return dict(sems=outs[0:2], parts=outs[2:2 + n], got=outs[2 + n:2 + 2 * n], token=outs[-1])


def _send_wait(ws, s, after, *, name):
    n = len(ws)

    def body(*refs):
        for cp in _partial_copies(ws, refs[:n], refs[n:2 * n], refs[2 * n], refs[2 * n + 1]):
            cp.wait_send()
            cp.wait_recv()

    bufs = list(s["parts"]) + list(s["got"])
    outs = pl.pallas_call(
        body, name=name, in_specs=[HBM] * (2 * n) + [SEM, SEM, ANY], out_specs=[HBM] * (2 * n),
        out_shape=[pltpu.HBM(b.shape, b.dtype) for b in bufs],
        input_output_aliases={i: i for i in range(2 * n)},
        compiler_params=pltpu.CompilerParams(has_side_effects=EFFECT),
    )(*bufs, *s["sems"], after)
    return outs[:n], outs[n:]


def _chip_reduce(ws, parts, got, *, name, after=None):
    D = parts[0].shape[1]
    nk = len(CHIP_FLIPS)
    n = len(ws)
    extra = [] if after is None else [after]

    def body(*refs):
        refs = refs[len(extra):]
        outs = refs[(1 + nk) * n:]
        for i in range(n):
            acc = refs[i][...].astype(F32)
            for k in range(nk):
                acc = acc + refs[n * (1 + k) + i][...].astype(F32)
            outs[i][...] = acc

    blk = [HALF[w] // REDUCE_SPLIT for w in ws]

    def q_idx(j):
        return (2 * lax.axis_index("x") + lax.axis_index("y")) * REDUCE_SPLIT + j

    in_specs = [pl.BlockSpec((b, D), lambda j: (q_idx(j), 0)) for b in blk]
    for k in range(nk):
        in_specs += [pl.BlockSpec((b, D), functools.partial(lambda j, k: (k * REDUCE_SPLIT + j, 0), k=k)) for b in blk]
    out_specs = [pl.BlockSpec((b, D), lambda j: (lax.axis_index("c") * REDUCE_SPLIT + j, 0)) for b in blk]
    return pl.pallas_call(
        body, name=name, grid=(REDUCE_SPLIT,), in_specs=[ANY] * len(extra) + in_specs, out_specs=out_specs,
        out_shape=[_sds((SLAB[w], D), F32) for w in ws],
        compiler_params=_cp(("parallel",)),
    )(*extra, *parts, *[g for _ in range(nk) for g in got])


def _exchange_reduced(ws, shards, *, name):
    n = len(ws)

    def body(*refs):
        ins, outs = refs[:n], refs[n:2 * n]
        send_sems, recv_sems = refs[2 * n], refs[2 * n + 1]
        x, y, c = _pos()
        sib = (x, y, 1 - c)
        cps = []
        for i, w in enumerate(ws):
            cp = _rcopy(_rows(ins[i], c * HALF[w], HALF[w]), _rows(outs[i], c * HALF[w], HALF[w]),
                        send_sems.at[i], recv_sems.at[i], sib)
            cp.start()
            cps.append(cp)
        for cp in cps:
            cp.wait_recv()
        for cp in cps:
            cp.wait_send()

    return pl.pallas_call(
        body, name=name, in_specs=[ANY] * n, out_specs=[ANY] * n,
        out_shape=[_sds(s.shape, s.dtype) for s in shards], input_output_aliases={i: i for i in range(n)},
        scratch_shapes=[pltpu.SemaphoreType.DMA((n,)), pltpu.SemaphoreType.DMA((n,))],
    )(*shards)


def _adamw_fn(w, g, m, v):
    m2 = ADAM_B1 * m + (1.0 - ADAM_B1) * g
    v2 = ADAM_B2 * v + (1.0 - ADAM_B2) * (g * g)
    m_hat = m2 / (1.0 - ADAM_B1 ** ADAM_STEP)
    v_hat = v2 / (1.0 - ADAM_B2 ** ADAM_STEP)
    return -ADAM_LR * (m_hat / (jnp.sqrt(v_hat) + ADAM_EPS) + ADAM_WD * w), m2, v2


def _adamw(w, g, m, v, *, name):
    shp = _sds(w.shape, F32)
    rows = w.shape[0]
    tm = max(t for t in range(SUBLANES, 512 + 1, SUBLANES) if rows % t == 0)
    return _rowwise(_adamw_fn, [_full(w), _full(g), _full(m), _full(v)], [], [shp] * 3, [], name=name, tm=tm)


SMALL_SEGS = (("loss", 8), ("norm_mix_w", 8), ("b_attn", 8), ("lb_logits", 8), ("hg_norm_w", 8), ("sinks", 8),
              ("norm_ffn_w", 8), ("conv_w", 72), ("conv_b", 24), ("final_norm_w", 8))
SMALL_OFF = {n: sum(r for _, r in SMALL_SEGS[:i]) for i, (n, _) in enumerate(SMALL_SEGS)}
SMALL_ROWS = sum(r for _, r in SMALL_SEGS)
LANES = 128


def _pack_small(parts):
    segs = []
    for n, r in SMALL_SEGS:
        a = parts.get(n)
        flat = jnp.zeros((0,), F32) if a is None else a.reshape(-1).astype(F32)
        segs.append(jnp.pad(flat, (0, r * LANES - flat.shape[0])).reshape(r, LANES))
    return jnp.concatenate(segs, axis=0)


def _unpack_small(pack, n, shape):
    size = math.prod(shape)
    r0 = SMALL_OFF[n]
    return pack[r0:r0 + dict(SMALL_SEGS)[n]].reshape(-1)[:size].reshape(shape)


def _small_update(sall, wp, mp, vp):
    R = SMALL_ROWS
    r_lb = SMALL_OFF["lb_logits"]

    def body(s_ref, w_ref, m_ref, v_ref, g_ref, d_ref, m2_ref, v2_ref, loss_ref):
        g = s_ref[0]
        for i in range(1, N_DEV):
            g = g + s_ref[i]
        tot = jnp.sum(jnp.sum(g[0:8], axis=1, keepdims=True), axis=0, keepdims=True)
        loss_ref[...] = jnp.broadcast_to(tot, loss_ref.shape)
        lg = w_ref[r_lb:r_lb + 8, :]
        p0 = _sigmoid(lg - pltpu.roll(lg, 4, 0))
        d = g[r_lb:r_lb + 8]
        d = d + pltpu.roll(d, 4, 0)
        sign = jnp.where(lax.broadcasted_iota(jnp.int32, d.shape, 0) < 4, 1.0, -1.0)
        g = jnp.concatenate([g[:r_lb], sign * d * p0 * (1.0 - p0), g[r_lb + 8:]], axis=0)
        g_ref[...] = g
        d_ref[...], m2_ref[...], v2_ref[...] = _adamw_fn(w_ref[...], g, m_ref[...], v_ref[...])

    full = pl.BlockSpec((R, LANES), lambda: (0, 0))
    return pl.pallas_call(
        body, name="small_update",
        in_specs=[pl.BlockSpec((N_DEV, R, LANES), lambda: (0, 0, 0)), full, full, full],
        out_specs=[full, full, full, full, pl.BlockSpec((8, LANES), lambda: (0, 0))],
        out_shape=[_sds((R, LANES), F32)] * 4 + [_sds((8, LANES), F32)],
        compiler_params=_cp(),
    )(sall, wp, mp, vp)


def _lb_fwd(lb_logits):
    n = lb_logits.shape[1]

    def body(l_ref, o_ref):
        o_ref[...] = _sigmoid(l_ref[0:1, :] - l_ref[1:2, :])

    return pl.pallas_call(body, name="lb_fwd", out_shape=_sds((1, n), F32), compiler_params=_cp())(lb_logits)


class _MeshExchange:
    def __init__(self, pack, cw8):
        self.gather = _gather_start(pack, cw8)
        self.sent = None
        self.conv_w8 = None

    def start(self):
        return self.gather["token"]

    def w_in(self, after):
        self.pack, l_in = _gather_wait_in(self.gather, after)
        return (_forward_in(l_in), N_CHIPS * SLAB[0], 0)

    def mid(self, after):
        l_ffn, l_out, l_cw = _gather_wait_rest(self.gather, self.pack, after)
        self.conv_w8 = jnp.concatenate([l_cw[i] for i in range(N_CHIPS)], axis=1)
        self.passing = _forward_rest_start(l_ffn, l_out)
        return self.passing["token"]

    def rest(self, after):
        l_ffn, l_out = _forward_rest_wait(self.passing, after)
        rows = N_CHIPS * SLAB[FFN_W[0]]
        return dict(w_gate_t=(l_ffn, rows, 0), w_up_t=(l_ffn, rows, 1), w_down=(l_ffn, rows, 2),
                    w_out=(l_out, N_CHIPS * SLAB[4], 0), conv_w8=self.conv_w8)

    def ffn_grads(self, gs):
        self.swap = _halves_start(FFN_W, gs, name="halves_ffn_start")
        return self.swap["token"]

    def ffn_grads_send(self, after):
        gs, theirs = _halves_wait(FFN_W, self.swap, after, name="halves_ffn_wait")
        parts = _chip_partial(FFN_W, gs, theirs, name="chip_partial_ffn", out_dtype=BF16)
        self.sent = _send_start(FFN_W, parts, name="send_ffn_start")
        return self.sent["token"]


def kernel(x, norm_mix_w, w_in, b_attn, lb_logits, hg_norm_w, sinks, w_out, norm_ffn_w, w_gate, w_up, conv_w, conv_b, w_down, final_norm_w, loss_target, m_norm_mix_w, m_w_in, m_b_attn, m_lb_logits, m_hg_norm_w, m_sinks, m_w_out, m_norm_ffn_w, m_w_gate, m_w_up, m_conv_w, m_conv_b, m_w_down, m_final_norm_w, v_norm_mix_w, v_w_in, v_b_attn, v_lb_logits, v_hg_norm_w, v_sinks, v_w_out, v_norm_ffn_w, v_w_gate, v_w_up, v_conv_w, v_conv_b, v_w_down, v_final_norm_w):
    D = D_MODEL
    q = 2 * lax.axis_index("x") + lax.axis_index("y")
    ccols = D_FF // N_CHIPS

    pack = jnp.concatenate([w_in[0].T, w_gate[0].T, w_up[0].T, w_down[0], w_out[0]], axis=0).astype(BF16)
    cw8 = jnp.concatenate([conv_w[0], jnp.zeros((SUBLANES - 3, ccols), F32)], axis=0)
    ex = _MeshExchange(pack, cw8)
    p = dict(norm_mix_w=norm_mix_w, b_attn=b_attn, lb=_lb_fwd(lb_logits), hg_norm_w=hg_norm_w, sinks=sinks,
             norm_ffn_w=norm_ffn_w, conv_b=conv_b, final_norm_w=final_norm_w.reshape(1, D))
    loss_cols, dx, g = _local_step(x[0], loss_target[0], p, ex)
    conv_w8 = ex.conv_w8

    small = _pack_small(dict(loss=loss_cols, norm_mix_w=g["norm_mix_w"], b_attn=g["b_attn"], lb_logits=g["lb"],
                             hg_norm_w=g["hg_norm_w"], sinks=g["sinks8"], norm_ffn_w=g["norm_ffn_w"],
                             conv_w=g["conv_w8"][:3], conv_b=g["conv_b"], final_norm_w=g["final_norm_w"]))
    parts_ffn, got_ffn = _send_wait(FFN_W, ex.sent, dx, name="send_ffn_wait")
    late = (0, 4)
    gs = [g["g_in_t"], g["g_out"]]
    *theirs, sall = _exchange_halves(late, gs, small, name="exchange_halves_late")
    parts_late = _chip_partial(late, gs, theirs, name="chip_partial_late", out_dtype=BF16)
    sent_late = _send_start(late, parts_late, name="send_late_start")
    big = {}

    def finish(ws, parts, got, specs, tag, after):
        shards = _exchange_reduced(ws, _chip_reduce(ws, parts, got, name="chip_reduce_" + tag, after=after),
                                   name="exchange_reduced_" + tag)
        for gw, (n, w, m, v, tr) in zip(shards, specs):
            view = (lambda a: a[0].T) if tr else (lambda a: a[0])
            back = (lambda a: a.T[None]) if tr else (lambda a: a[None])
            d_, m_, v_ = _adamw(view(w), gw, view(m), view(v), name="adamw_" + n)
            big[n] = (back(gw), back(d_), back(m_), back(v_))
        return d_

    last = finish(FFN_W, parts_ffn, got_ffn, (("w_gate", w_gate, m_w_gate, v_w_gate, True), ("w_up", w_up, m_w_up, v_w_up, True),
                                              ("w_down", w_down, m_w_down, v_w_down, False)), "ffn", sent_late["token"])
    parts_late, got_late = _send_wait(late, sent_late, last, name="send_late_wait")
    finish(late, parts_late, got_late, (("w_in", w_in, m_w_in, v_w_in, True), ("w_out", w_out, m_w_out, v_w_out, False)),
           "late", None)

    def place(a):
        return lax.dynamic_update_slice(jnp.zeros((3, D_FF), F32), a[0], (0, q * ccols))

    def small_pack(ws, cw):
        nm, ba, lbl, hg, sk, nf, cb, fn = ws
        return _pack_small(dict(norm_mix_w=nm, b_attn=ba, lb_logits=lbl, hg_norm_w=hg,
                                sinks=jnp.broadcast_to(sk.reshape(ATT_HEADS, 1), (ATT_HEADS, LANES)), norm_ffn_w=nf,
                                conv_w=cw, conv_b=cb, final_norm_w=fn))

    wp = small_pack((norm_mix_w, b_attn, lb_logits, hg_norm_w, sinks, norm_ffn_w, conv_b, final_norm_w), conv_w8[:3])
    mp = small_pack((m_norm_mix_w, m_b_attn, m_lb_logits, m_hg_norm_w, m_sinks, m_norm_ffn_w, m_conv_b, m_final_norm_w),
                    place(m_conv_w))
    vp = small_pack((v_norm_mix_w, v_b_attn, v_lb_logits, v_hg_norm_w, v_sinks, v_norm_ffn_w, v_conv_b, v_final_norm_w),
                    place(v_conv_w))
    outs = _small_update(sall, wp, mp, vp)
    loss = outs[4][0, 0]

    def small_out(pk, n, ref):
        if n == "sinks":
            return pk[SMALL_OFF[n]:SMALL_OFF[n] + ATT_HEADS, 0].reshape(ref.shape)
        if n == "conv_w":
            full = _unpack_small(pk, n, (3, D_FF))
            return lax.dynamic_slice(full, (0, q * ccols), (3, ccols))[None]
        return _unpack_small(pk, n, ref.shape)

    refs = dict(norm_mix_w=norm_mix_w, b_attn=b_attn, lb_logits=lb_logits, hg_norm_w=hg_norm_w, sinks=sinks,
                norm_ffn_w=norm_ffn_w, conv_w=conv_w, conv_b=conv_b, final_norm_w=final_norm_w)
    order = ("norm_mix_w", "w_in", "b_attn", "lb_logits", "hg_norm_w", "sinks", "w_out", "norm_ffn_w", "w_gate", "w_up",
             "conv_w", "conv_b", "w_down", "final_norm_w")
    res = [loss, dx[None]]
    for k in range(4):
        for n in order:
            res.append(big[n][k] if n in big else small_out(outs[k], n, refs[n]))
    return tuple(res)
```

```python
import functools
import math

import jax
import jax.numpy as jnp
from jax import lax
from jax.experimental import pallas as pl
from jax.experimental.pallas import tpu as pltpu

F32 = jnp.float32
BF16 = jnp.bfloat16

D_MODEL = 1024
HG_HEADS = 4
HG_DK = 128
HG_W = HG_HEADS * HG_DK
HG_CHUNK = 64
HG_SUB = 8
HG_FWD_CHUNKS_PER_STEP = 4
HG_CHUNKS_PER_STEP = 2
ATT_HEADS = 8
ATT_KV = 2
ATT_GROUP = ATT_HEADS // ATT_KV
ATT_HD = 64
ATT_BLOCK = 128
ATT_Q_W = ATT_HEADS * ATT_HD
ATT_KV_W = ATT_KV * ATT_HD
ATT_COLS = ATT_Q_W + 2 * ATT_KV_W
IN_COLS = 4 * HG_W + ATT_COLS
D_FF = 2816
EPS = 1e-6
ADAM_LR, ADAM_B1, ADAM_B2, ADAM_EPS, ADAM_WD, ADAM_STEP = 0.001, 0.9, 0.999, 1e-08, 0.01, 10
NEG = -1e30

V7X_VMEM_BYTES = 64 * 1024 * 1024
VMEM_LIMIT = 48 * 1024 * 1024
SUBLANES = 8

N_CHIPS = 4


def _cp(sem=None, **kw):
    return pltpu.CompilerParams(dimension_semantics=sem, vmem_limit_bytes=VMEM_LIMIT, **kw)


def _sds(shape, dtype):
    return pltpu.HBM(shape, dtype)


TOKEN = jax.ShapeDtypeStruct((8, 128), jnp.float32)


def _hbm(*arrays):
    return [pltpu.with_memory_space_constraint(a, pltpu.HBM) for a in arrays]


def _wspec(w):
    arr, rows, blk = w
    return pl.BlockSpec((rows, arr.shape[1]), lambda i: (blk, 0))


def _mm_nt(a, w, *, splits, out_dtype, name, after=None, tm=512):
    M, K = a.shape
    N = w[1]
    tm = min(tm, M)
    assert sum(splits) == N and M % tm == 0
    offs = [sum(splits[:i]) for i in range(len(splits))]
    n_in = 2 if after is None else 3

    def body(*refs):
        a_ref, w_ref = refs[0], refs[1]
        acc = lax.dot_general(a_ref[...], w_ref[...], (((1,), (1,)), ((), ())), preferred_element_type=F32)
        for o_ref, c0, n in zip(refs[n_in:], offs, splits):
            o_ref[...] = acc[:, c0:c0 + n].astype(out_dtype)

    in_specs = [pl.BlockSpec((tm, K), lambda i: (i, 0)), _wspec(w)]
    args = [a, w[0]]
    if after is not None:
        in_specs.append(pl.BlockSpec(memory_space=pl.ANY))
        args.append(after)
    outs = pl.pallas_call(
        body, name=name, grid=(M // tm,), in_specs=in_specs,
        out_specs=[pl.BlockSpec((tm, n), lambda i: (i, 0)) for n in splits],
        out_shape=[_sds((M, n), out_dtype) for n in splits],
        compiler_params=_cp(("parallel",)),
    )(*_hbm(*args[:2]), *args[2:])
    return outs


def _mm_nn(pieces, ws, *, name, out_dtype=F32, residual=None, epilogue=None, prologue=None, after=None,
           w_transposed=False, tm=512):
    pro_fn, pro_rows, pro_bc, pro_out = prologue or (None, [], [], None)
    if prologue is not None:
        assert pieces is None and len(ws) == 1
        pieces = [[pro_out]]
    M = pieces[0][0].shape[0]
    K = ws[0][1] if w_transposed else ws[0][0].shape[1]
    tm = min(tm, M)
    flat = [] if prologue is not None else [p for grp in pieces for p in grp]
    n_p = len(flat)
    n_w = len(ws)
    n_pr, n_pb = len(pro_rows), len(pro_bc)
    fn, row_ins, bc_ins, row_outs, acc_outs = epilogue or (None, [], [], [_sds((M, K), out_dtype)], [])
    if residual is not None:
        assert epilogue is None
        row_ins = [residual]
    n_r, n_b, n_o = len(row_ins), len(bc_ins), len(row_outs)
    lead = [] if after is None else [after]

    def body(*refs):
        refs = refs[len(lead):]
        p_refs = refs[:n_p]
        w_refs = refs[n_p:n_p + n_w]
        extra = [r[...] for r in refs[n_p + n_w:n_p + n_w + n_r + n_b]]
        base = n_p + n_w + n_r + n_b
        pro = [r[...] for r in refs[base:base + n_pr + n_pb]]
        base += n_pr + n_pb
        o_refs = refs[base:base + n_o]
        a_refs = refs[base + n_o:base + n_o + len(acc_outs)]
        if pro_fn is not None:
            lhs = pro_fn(*pro).astype(pro_out.dtype)
            refs[-1][...] = lhs
            tiles = [lhs]
        else:
            tiles = [r[...] for r in p_refs]
        acc = None
        k = 0
        for gi, grp in enumerate(pieces):
            c0 = 0
            for p in grp:
                n = p.shape[1]
                if w_transposed:
                    t = lax.dot_general(tiles[k], w_refs[gi][...], (((1,), (1,)), ((), ())), preferred_element_type=F32)
                else:
                    t = jnp.dot(tiles[k], w_refs[gi][c0:c0 + n, :], preferred_element_type=F32)
                acc = t if acc is None else acc + t
                c0 += n
                k += 1
        if fn is None:
            res = (acc + extra[0] if residual is not None else acc,)
        else:
            res = fn(acc, *extra)
        for o_ref, val in zip(o_refs, res[:n_o]):
            o_ref[...] = val.astype(o_ref.dtype)
        if acc_outs:
            @pl.when(pl.program_id(0) == 0)
            def _():
                for a_ref in a_refs:
                    a_ref[...] = jnp.zeros_like(a_ref)
            for a_ref, val in zip(a_refs, res[n_o:]):
                a_ref[...] += val

    in_specs = [pl.BlockSpec((tm, p.shape[1]), lambda i: (i, 0)) for p in flat]
    in_specs += [_wspec(w) for w in ws]
    in_specs += [pl.BlockSpec((tm, r.shape[1]), lambda i: (i, 0)) for r in row_ins]
    in_specs += [pl.BlockSpec(b.shape, lambda i: (0, 0)) for b in bc_ins]
    in_specs += [pl.BlockSpec((tm, r.shape[1]), lambda i: (i, 0)) for r in pro_rows]
    in_specs += [pl.BlockSpec(b.shape, lambda i: (0, 0)) for b in pro_bc]
    out_specs = [pl.BlockSpec((tm, s.shape[1]), lambda i: (i, 0)) for s in row_outs]
    out_specs += [pl.BlockSpec(s.shape, lambda i: (0, 0)) for s in acc_outs]
    pro_outs = [] if prologue is None else [pro_out]
    out_specs += [pl.BlockSpec((tm, s.shape[1]), lambda i: (i, 0)) for s in pro_outs]
    outs = pl.pallas_call(
        body, name=name, grid=(M // tm,), in_specs=[pl.BlockSpec(memory_space=pl.ANY)] * len(lead) + in_specs,
        out_specs=out_specs, out_shape=list(row_outs) + list(acc_outs) + pro_outs,
        compiler_params=_cp(("arbitrary",) if acc_outs else ("parallel",)),
    )(*lead, *_hbm(*flat, *[w[0] for w in ws], *row_ins, *bc_ins, *pro_rows, *pro_bc))
    return outs if (epilogue is not None or prologue is not None) else outs[0]


def _mm_tn(pieces, x, *, name, out_dtype=BF16, tt=1024):
    M, K = x.shape
    tt = min(tt, M)
    ns = [p.shape[1] for p in pieces]
    offs = [sum(ns[:i]) for i in range(len(ns))]
    N = sum(ns)
    n_p = len(pieces)
    last = M // tt - 1

    def body(*refs):
        p_refs = refs[:n_p]
        x_ref = refs[n_p]
        o_ref, acc_ref = refs[n_p + 1], refs[n_p + 2]

        @pl.when(pl.program_id(0) == 0)
        def _():
            acc_ref[...] = jnp.zeros_like(acc_ref)

        xv = x_ref[...]
        for p_ref, c0, n in zip(p_refs, offs, ns):
            acc_ref[c0:c0 + n, :] += lax.dot_general(p_ref[...], xv, (((0,), (0,)), ((), ())),
                                                      preferred_element_type=F32)

        @pl.when(pl.program_id(0) == last)
        def _():
            o_ref[...] = acc_ref[...].astype(o_ref.dtype)

    in_specs = [pl.BlockSpec((tt, n), lambda i: (i, 0)) for n in ns]
    in_specs.append(pl.BlockSpec((tt, K), lambda i: (i, 0)))
    return pl.pallas_call(
        body, name=name, grid=(M // tt,), in_specs=in_specs,
        out_specs=pl.BlockSpec((N, K), lambda i: (0, 0)),
        out_shape=_sds((N, K), out_dtype),
        scratch_shapes=[pltpu.VMEM((N, K), F32)],
        compiler_params=_cp(("arbitrary",)),
    )(*_hbm(*pieces, x))


def _rms_fwd(xf, w):
    inv = lax.rsqrt(jnp.mean(xf * xf, axis=-1, keepdims=True) + EPS)
    return xf * inv * w


def _rms_bwd(xf, w, dy):
    inv = lax.rsqrt(jnp.mean(xf * xf, axis=-1, keepdims=True) + EPS)
    xhat = xf * inv
    dxhat = dy * w
    dx = inv * (dxhat - xhat * jnp.mean(dxhat * xhat, axis=-1, keepdims=True))
    dw = jnp.sum(dy * xhat, axis=0, keepdims=True)
    return dx, dw


def _sigmoid(x):
    return 1.0 / (1.0 + jnp.exp(-x))


def _rowwise(fn, row_ins, bc_ins, row_outs, acc_outs, *, name, tm=256, after=None):
    M = row_outs[0].shape[0] if row_outs else row_ins[0][0].shape[0]
    assert M % tm == 0 and tm % SUBLANES == 0, (name, M, tm)
    n_r, n_b, n_o, n_a = len(row_ins), len(bc_ins), len(row_outs), len(acc_outs)
    n_after = 0 if after is None else 1

    def body(*refs):
        refs = refs[n_after:]
        ins = [r[...] for r in refs[:n_r + n_b]]
        o_refs = refs[n_r + n_b:n_r + n_b + n_o]
        a_refs = refs[n_r + n_b + n_o:]
        res = fn(*ins)
        for o_ref, val in zip(o_refs, res[:n_o]):
            o_ref[...] = val.astype(o_ref.dtype)
        if n_a:
            @pl.when(pl.program_id(0) == 0)
            def _():
                for a_ref in a_refs:
                    a_ref[...] = jnp.zeros_like(a_ref)
            for a_ref, val in zip(a_refs, res[n_o:]):
                a_ref[...] += val

    in_specs = [pl.BlockSpec((tm, cw), functools.partial(lambda i, cb, r0: (i + r0, cb), cb=cb, r0=r0))
                for (_, cw, cb, r0) in row_ins]
    in_specs += [pl.BlockSpec(b.shape, lambda i: (0, 0)) for b in bc_ins]
    out_specs = [pl.BlockSpec((tm, s.shape[1]), lambda i: (i, 0)) for s in row_outs]
    out_specs += [pl.BlockSpec(s.shape, lambda i: (0, 0)) for s in acc_outs]
    if n_after:
        in_specs = [pl.BlockSpec(memory_space=pl.ANY)] + in_specs
    return pl.pallas_call(
        body, name=name, grid=(M // tm,), in_specs=in_specs, out_specs=out_specs,
        out_shape=list(row_outs) + list(acc_outs),
        compiler_params=_cp(("arbitrary",) if n_a else ("parallel",)),
    )(*([after] if n_after else []), *_hbm(*[r[0] for r in row_ins], *bc_ins))


def _full(a, first_row_block=0):
    return (a, a.shape[1], 0, first_row_block)


def _conv_rows(ext, w_ref_val, lo):
    s1 = pltpu.roll(ext, 1, 0)
    s2 = pltpu.roll(ext, 2, 0)
    y = w_ref_val[0:1, :] * s2 + w_ref_val[1:2, :] * s1 + w_ref_val[2:3, :] * ext
    return y[SUBLANES:, :]


def _convact_fwd(gp, up, conv_w8, conv_b, *, name, tr=512, tc=1408):
    T, C = gp.shape
    tr = min(tr, T)
    hb = tr // SUBLANES

    def body(gp_ref, gph_ref, up_ref, w_ref, b_ref, act_ref):
        i = pl.program_id(1)
        halo = jnp.where(i > 0, gph_ref[...], 0.0)
        ext = jnp.concatenate([halo, gp_ref[...]], axis=0)
        gate = _conv_rows(ext, w_ref[...], 0) + b_ref[...]
        act_ref[...] = (gate * _sigmoid(gate) * up_ref[...]).astype(act_ref.dtype)

    return pl.pallas_call(
        body, name=name, grid=(C // tc, T // tr),
        in_specs=[pl.BlockSpec((tr, tc), lambda j, i: (i, j)),
                  pl.BlockSpec((SUBLANES, tc), lambda j, i: (jnp.maximum(i * hb - 1, 0), j)),
                  pl.BlockSpec((tr, tc), lambda j, i: (i, j)),
                  pl.BlockSpec((SUBLANES, tc), lambda j, i: (0, j)),
                  pl.BlockSpec((1, tc), lambda j, i: (0, j))],
        out_specs=pl.BlockSpec((tr, tc), lambda j, i: (i, j)),
        out_shape=_sds((T, C), BF16),
        compiler_params=_cp(("parallel", "parallel")),
    )(*_hbm(gp, gp, up, conv_w8, conv_b))


def _convact_bwd(gp, up, dact, conv_w8, conv_b, *, name, tr=256, tc=1408):
    T, C = gp.shape
    tr = min(tr, T)
    hb = tr // SUBLANES
    nr = T // tr

    def body(gp_ref, gpp_ref, gpn_ref, up_ref, upn_ref, da_ref, dan_ref, w_ref, b_ref,
             dgp_ref, dup_ref, dw_ref, db_ref):
        i = pl.program_id(1)
        w = w_ref[...]
        prev = jnp.where(i > 0, gpp_ref[...], 0.0)
        last = i == nr - 1
        gp_ext = jnp.concatenate([prev, gp_ref[...], gpn_ref[...]], axis=0)
        gate = _conv_rows(gp_ext, w, 0) + b_ref[...]
        up_e = jnp.concatenate([up_ref[...], upn_ref[...]], axis=0)
        da_e = jnp.concatenate([da_ref[...], dan_ref[...]], axis=0)
        row = lax.broadcasted_iota(jnp.int32, gate.shape, 0)
        valid = jnp.logical_or(row < tr, jnp.logical_not(last))
        sg = _sigmoid(gate)
        silu = gate * sg
        dgate = jnp.where(valid, da_e * up_e * (sg * (1.0 + gate * (1.0 - sg))), 0.0)
        dup_ref[...] = (da_e[:tr] * silu[:tr]).astype(dup_ref.dtype)
        n = tr + SUBLANES
        g1 = pltpu.roll(dgate, n - 1, 0)
        g2 = pltpu.roll(dgate, n - 2, 0)
        dgp = w[2:3, :] * dgate + w[1:2, :] * g1 + w[0:1, :] * g2
        dgp_ref[...] = dgp[:tr].astype(dgp_ref.dtype)
        gpc = gp_ref[...]
        dw0 = jnp.sum(gpc * g2[:tr], axis=0, keepdims=True)
        dw1 = jnp.sum(gpc * g1[:tr], axis=0, keepdims=True)
        dw2 = jnp.sum(gpc * dgate[:tr], axis=0, keepdims=True)
        dbv = jnp.sum(dgate[:tr], axis=0, keepdims=True)
        z = jnp.zeros((SUBLANES - 3, gpc.shape[1]), F32)

        @pl.when(i == 0)
        def _():
            dw_ref[...] = jnp.zeros_like(dw_ref)
            db_ref[...] = jnp.zeros_like(db_ref)

        dw_ref[...] += jnp.concatenate([dw0, dw1, dw2, z], axis=0)
        db_ref[...] += dbv

    cur = pl.BlockSpec((tr, tc), lambda j, i: (i, j))
    prv = pl.BlockSpec((SUBLANES, tc), lambda j, i: (jnp.maximum(i * hb - 1, 0), j))
    nxt = pl.BlockSpec((SUBLANES, tc), lambda j, i: (jnp.minimum((i + 1) * hb, T // SUBLANES - 1), j))
    return pl.pallas_call(
        body, name=name, grid=(C // tc, nr),
        in_specs=[cur, prv, nxt, cur, nxt, cur, nxt,
                  pl.BlockSpec((SUBLANES, tc), lambda j, i: (0, j)),
                  pl.BlockSpec((1, tc), lambda j, i: (0, j))],
        out_specs=[cur, cur,
                   pl.BlockSpec((SUBLANES, tc), lambda j, i: (0, j)),
                   pl.BlockSpec((1, tc), lambda j, i: (0, j))],
        out_shape=[_sds((T, C), BF16), _sds((T, C), BF16), _sds((SUBLANES, C), F32), _sds((1, C), F32)],
        compiler_params=_cp(("parallel", "arbitrary")),
    )(*_hbm(gp, gp, gp, up, up, dact, dact, conv_w8, conv_b))


def _cumsum_rows(x):
    n = x.shape[0]
    row = lax.broadcasted_iota(jnp.int32, x.shape, 0)
    s = 1
    while s < n:
        x = x + jnp.where(row >= s, pltpu.roll(x, s, 0), 0.0)
        s *= 2
    return x


def _rcumsum_rows(x):
    n = x.shape[0]
    row = lax.broadcasted_iota(jnp.int32, x.shape, 0)
    s = 1
    while s < n:
        x = x + jnp.where(row < n - s, pltpu.roll(x, n - s, 0), 0.0)
        s *= 2
    return x


def _dot_nt(a, b):
    return lax.dot_general(a.astype(BF16), b.astype(BF16), (((1,), (1,)), ((), ())), preferred_element_type=F32)


def _dot_tn(a, b):
    return lax.dot_general(a.astype(BF16), b.astype(BF16), (((0,), (0,)), ((), ())), preferred_element_type=F32)


def _dot_nn(a, b):
    return jnp.dot(a.astype(BF16), b.astype(BF16), preferred_element_type=F32)


def _dot3(a, b, contract):
    def split(x):
        hi = x.astype(BF16)
        return hi, (x - hi.astype(F32)).astype(BF16)

    a_hi, a_lo = split(a)
    b_hi, b_lo = split(b)
    dot = lambda x, y: lax.dot_general(x, y, (contract, ((), ())), preferred_element_type=F32)
    return dot(a_hi, b_hi) + (dot(a_hi, b_lo) + dot(a_lo, b_hi))


NT, TN, NN = ((1,), (1,)), ((0,), (0,)), ((1,), (0,))


def _hg_gates(hq, hf, lbv):
    sig = _sigmoid(hf)
    f = lbv + (1.0 - lbv) * sig
    return sig, f, jnp.log(f), 1.0 - f, hq * (HG_DK ** -0.5)


def _hg_sel_rows(ref, sp):
    return jnp.concatenate(
        [jnp.broadcast_to(ref[pl.ds(HG_SUB * i + sp, 1), :], (HG_SUB, HG_DK)) for i in range(HG_CHUNK // HG_SUB)], axis=0)


def _hg_masks():
    C = HG_CHUNK
    row = lax.broadcasted_iota(jnp.int32, (C, C), 0)
    col = lax.broadcasted_iota(jnp.int32, (C, C), 1)
    d = col - (row // HG_SUB) * HG_SUB
    tmod = row % HG_SUB
    diag_valid = jnp.logical_and(d >= 0, d <= tmod)
    return row, col, d, diag_valid


def _hg_scores(q, k, b, b_sc, k_sc):
    C, S = HG_CHUNK, HG_SUB
    row, col, d, diag_valid = _hg_masks()
    blocks = [jnp.zeros((S, C), F32)]
    for i in range(1, C // S):
        r = b_sc[pl.ds(S * i - 1, 1), :]
        qi = q[S * i:S * (i + 1)] * jnp.exp(b[S * i:S * (i + 1)] - r)
        kk = k * jnp.exp(jnp.minimum(r - b, 0.0))
        blocks.append(_dot_nt(qi, kk))
    a_off = jnp.where(col < (row // S) * S, jnp.concatenate(blocks, axis=0), 0.0)
    a_d = jnp.zeros((C, C), F32)
    for sp in range(S):
        bs = _hg_sel_rows(b_sc, sp)
        ks = _hg_sel_rows(k_sc, sp)
        e = jnp.exp(jnp.minimum(b - bs, 0.0))
        colv = jnp.sum(q * ks * e, axis=-1, keepdims=True)
        a_d = jnp.where(d == sp, colv, a_d)
    return a_off + jnp.where(diag_valid, a_d, 0.0)


def _hg_prep(hq_v, hf_v, lbv, b_sc, k_sc):
    sig, f, g, k, q = _hg_gates(hq_v, hf_v, lbv)
    b = _cumsum_rows(g)
    b_sc[...] = b
    k_sc[...] = k
    return sig, f, k, q, b, b_sc[pl.ds(HG_CHUNK - 1, 1), :]


def _hgrn_fwd(hq, hf, hi, lb, *, name):
    T = hq.shape[0]
    C, H, K = HG_CHUNK, HG_HEADS, HG_DK
    NC = T // C

    def body(hq_ref, hf_ref, hi_ref, lb_ref, o_ref, st_ref, s_sc, b_sc, k_sc):
        @pl.when(pl.program_id(0) == 0)
        def _():
            s_sc[...] = jnp.zeros_like(s_sc)

        st_all = s_sc[...]
        for j in range(P):
            rows = slice(C * j, C * (j + 1))
            st_ref[j] = st_all
            outs, news = [], []
            for h in range(H):
                sl = slice(K * h, K * (h + 1))
                _, _, k, q, b, bc = _hg_prep(hq_ref[rows, sl], hf_ref[rows, sl], lb_ref[:, sl], b_sc.at[j, h], k_sc.at[j, h])
                v = hi_ref[rows, sl]
                st0 = st_all[:, sl]
                a = _hg_scores(q, k, b, b_sc.at[j, h], k_sc.at[j, h])
                outs.append(_dot_nn(a, v) + _dot_nt(q * jnp.exp(b), st0))
                news.append(st0 * jnp.exp(bc) + _dot_tn(v, k * jnp.exp(bc - b)))
            o_ref[rows, :] = jnp.concatenate(outs, axis=1)
            st_all = jnp.concatenate(news, axis=1)
        s_sc[...] = st_all

    P = HG_FWD_CHUNKS_PER_STEP
    blk = pl.BlockSpec((P * C, H * K), lambda c: (c, 0))
    return pl.pallas_call(
        body, name=name, grid=(NC // P,),
        in_specs=[blk, blk, blk, pl.BlockSpec((1, H * K), lambda c: (0, 0))],
        out_specs=[blk, pl.BlockSpec((P, K, H * K), lambda c: (c, 0, 0))],
        out_shape=[_sds((T, H * K), F32), _sds((NC, K, H * K), F32)],
        scratch_shapes=[pltpu.VMEM((K, H * K), F32), pltpu.VMEM((P, H, C, K), F32), pltpu.VMEM((P, H, C, K), F32)],
        compiler_params=_cp(("arbitrary",)),
    )(*_hbm(hq, hf, hi, lb))


def _hgrn_bwd(hq, hf, hi, lb, states, do, *, name):
    T = hq.shape[0]
    C, H, K, S = HG_CHUNK, HG_HEADS, HG_DK, HG_SUB
    NC = T // C

    def intra_slow(q, k, b, da, b_sc, k_sc):
        row, col, d, diag_valid = _hg_masks()
        a_blocks = [jnp.zeros((S, C), F32)]
        dq_blocks = [jnp.zeros((S, K), F32)]
        dk = jnp.zeros((C, K), F32)
        for i in range(1, C // S):
            r = b_sc[pl.ds(S * i - 1, 1), :]
            eq = jnp.exp(b[S * i:S * (i + 1)] - r)
            ek = jnp.exp(jnp.minimum(r - b, 0.0))
            qi = q[S * i:S * (i + 1)] * eq
            kk = k * ek
            a_blocks.append(_dot_nt(qi, kk))
            dai = jnp.where(col[S * i:S * (i + 1)] < S * i, da[S * i:S * (i + 1)], 0.0)
            dq_blocks.append(_dot_nn(dai, kk) * eq)
            dk = dk + _dot_tn(dai, qi) * ek
        dq = jnp.concatenate(dq_blocks, axis=0)
        a_off = jnp.where(col < (row // S) * S, jnp.concatenate(a_blocks, axis=0), 0.0)
        same_blk = (row // S == col // S).astype(BF16)
        tmod = (lax.broadcasted_iota(jnp.int32, (C, K), 0)) % S
        a_d = jnp.zeros((C, C), F32)
        for sp in range(S):
            bs = _hg_sel_rows(b_sc, sp)
            ks = _hg_sel_rows(k_sc, sp)
            e = jnp.where(tmod >= sp, jnp.exp(jnp.minimum(b - bs, 0.0)), 0.0)
            eks = e * ks
            a_d = jnp.where(d == sp, jnp.sum(q * eks, axis=-1, keepdims=True), a_d)
            dacol = jnp.sum(jnp.where(d == sp, da, 0.0), axis=-1, keepdims=True)
            dq = dq + dacol * eks
            wq = dacol * e * q
            wq_hi = wq.astype(BF16)
            wq_lo = (wq - wq_hi.astype(F32)).astype(BF16)
            blk_sum = (jnp.dot(same_blk, wq_hi, preferred_element_type=F32)
                       + jnp.dot(same_blk, wq_lo, preferred_element_type=F32))
            dk = dk + jnp.where(tmod == sp, blk_sum, 0.0)
        return a_off + jnp.where(diag_valid, a_d, 0.0), dq, dk

    def one_head(pre, v, lbv, st0, dst1, dout, b_sc, k_sc):
        sig, f, k, q, b, bc = pre
        ebc = jnp.exp(bc)
        eb = jnp.exp(b)
        ekb = jnp.exp(bc - b)
        qt = q * eb
        kb = k * ekb
        row = lax.broadcasted_iota(jnp.int32, (C, C), 0)
        col = lax.broadcasted_iota(jnp.int32, (C, C), 1)
        da = jnp.where(col <= row, _dot_nt(dout, v), 0.0)
        dkb = _dot_nn(v, dst1)
        new_ds = _dot_tn(dout, qt) + dst1 * ebc
        a, dq_i, dk_i = intra_slow(q, k, b, da, b_sc, k_sc)
        dq = _dot_nn(dout, st0) * eb + dq_i
        dk = dkb * ekb + dk_i
        dv = _dot_tn(a, dout) + _dot_nt(kb, dst1)
        extra = jnp.sum(dkb * kb, axis=0, keepdims=True) + ebc * jnp.sum(st0 * dst1, axis=0, keepdims=True)
        rowk = lax.broadcasted_iota(jnp.int32, (C, K), 0)
        db = q * dq - k * dk + jnp.where(rowk == C - 1, extra, 0.0)
        dg = _rcumsum_rows(db)
        df = dg / f - dk
        return (dq * (K ** -0.5), df * (1.0 - lbv) * sig * (1.0 - sig), dv,
                jnp.sum(df * (1.0 - sig), axis=0, keepdims=True), new_ds)

    def body(hq_ref, hf_ref, hi_ref, lb_ref, st_ref, do_ref, dq_ref, dhf_ref, dv_ref, dlb_ref, ds_sc, b_sc, k_sc):
        @pl.when(pl.program_id(0) == 0)
        def _():
            ds_sc[...] = jnp.zeros_like(ds_sc)
            dlb_ref[...] = jnp.zeros_like(dlb_ref)

        ds_all = ds_sc[...]
        dlb = jnp.zeros((1, H * K), F32)
        for j in reversed(range(P)):
            rows = slice(C * j, C * (j + 1))
            st_all = st_ref[j]
            res = []
            for h in range(H):
                sl = slice(K * h, K * (h + 1))
                pre = _hg_prep(hq_ref[rows, sl], hf_ref[rows, sl], lb_ref[:, sl], b_sc.at[j, h], k_sc.at[j, h])
                res.append(one_head(pre, hi_ref[rows, sl], lb_ref[:, sl], st_all[:, sl], ds_all[:, sl], do_ref[rows, sl],
                                    b_sc.at[j, h], k_sc.at[j, h]))
            cat = lambda i: jnp.concatenate([r[i] for r in res], axis=1)
            dq_ref[rows, :] = cat(0).astype(dq_ref.dtype)
            dhf_ref[rows, :] = cat(1).astype(dhf_ref.dtype)
            dv_ref[rows, :] = cat(2).astype(dv_ref.dtype)
            dlb = dlb + cat(3)
            ds_all = cat(4)
        dlb_ref[...] += dlb
        ds_sc[...] = ds_all

    P = HG_CHUNKS_PER_STEP
    NS = NC // P
    blk = pl.BlockSpec((P * C, H * K), lambda c: (NS - 1 - c, 0))
    par = pl.BlockSpec((1, H * K), lambda c: (0, 0))
    return pl.pallas_call(
        body, name=name, grid=(NS,),
        in_specs=[blk, blk, blk, par, pl.BlockSpec((P, K, H * K), lambda c: (NS - 1 - c, 0, 0)), blk],
        out_specs=[blk, blk, blk, par],
        out_shape=[_sds((T, H * K), BF16)] * 3 + [_sds((1, H * K), F32)],
        scratch_shapes=[pltpu.VMEM((K, H * K), F32), pltpu.VMEM((P, H, C, K), F32), pltpu.VMEM((P, H, C, K), F32)],
        compiler_params=_cp(("arbitrary",)),
    )(*_hbm(hq, hf, hi, lb, states, do))


ATT_STACK = ATT_GROUP


def _att_valid(n):
    R, B = ATT_STACK * ATT_BLOCK, ATT_BLOCK
    j = lax.broadcasted_iota(jnp.int32, (2 * B, R), 0)
    t = lax.broadcasted_iota(jnp.int32, (2 * B, R), 1) % B
    dist = t + B - j
    first_key = jnp.where(n > 0, 0, B)
    return jnp.logical_and(jnp.logical_and(dist >= 0, dist < B), j >= first_key)


def _att_load(cur_ref, prev_ref, ba_ref, h0):
    hd = ATT_HD
    kv = h0 // ATT_GROUP
    def cols(ref, c0):
        return ref[:, c0:c0 + hd] + ba_ref[:, c0:c0 + hd]
    qs = jnp.concatenate([cols(cur_ref, hd * (h0 + g)) for g in range(ATT_STACK)], axis=0)
    kc = jnp.concatenate([cols(prev_ref, ATT_Q_W + hd * kv), cols(cur_ref, ATT_Q_W + hd * kv)], axis=0)
    vc = jnp.concatenate([cols(prev_ref, ATT_Q_W + ATT_KV_W + hd * kv), cols(cur_ref, ATT_Q_W + ATT_KV_W + hd * kv)], axis=0)
    return qs, kc, vc


def _att_probs(qs, kc, valid, sink_ref, h0):
    scale = 1.0 / math.sqrt(ATT_HD)
    s = jnp.where(valid, _dot_nt(kc, qs) * scale, NEG)
    sink = jnp.concatenate([jnp.full((1, ATT_BLOCK), sink_ref[0, h0 + g], F32) for g in range(ATT_STACK)], axis=1)
    m = jnp.maximum(jnp.max(s, axis=0, keepdims=True), sink)
    p = jnp.exp(s - m)
    ps = jnp.exp(sink - m)
    inv = 1.0 / (jnp.sum(p, axis=0, keepdims=True) + ps)
    return p * inv, ps * inv


def _attn_fwd(att, b_attn, sinks, *, name, after=None):
    T = att.shape[0]
    B = ATT_BLOCK
    NB = T // B
    lead = [] if after is None else [after]

    def body(*refs):
        sink_ref, cur_ref, prev_ref, ba_ref, o_ref = refs[len(lead):]
        valid = _att_valid(pl.program_id(0))
        outs = []
        for h0 in range(0, ATT_HEADS, ATT_STACK):
            qs, kc, vc = _att_load(cur_ref, prev_ref, ba_ref, h0)
            prob, _ = _att_probs(qs, kc, valid, sink_ref, h0)
            o = _dot_tn(prob, vc)
            outs += [o[B * g:B * (g + 1)] for g in range(ATT_STACK)]
        o_ref[...] = jnp.concatenate(outs, axis=1)

    return pl.pallas_call(
        body, name=name, grid=(NB,),
        in_specs=[pl.BlockSpec(memory_space=pl.ANY)] * len(lead) + [
            pl.BlockSpec(memory_space=pltpu.SMEM),
            pl.BlockSpec((B, ATT_COLS), lambda n: (n, 0)),
            pl.BlockSpec((B, ATT_COLS), lambda n: (jnp.maximum(n - 1, 0), 0)),
            pl.BlockSpec((1, ATT_COLS), lambda n: (0, 0))],
        out_specs=pl.BlockSpec((B, ATT_Q_W), lambda n: (n, 0)),
        out_shape=_sds((T, ATT_Q_W), F32),
        compiler_params=_cp(("parallel",)),
    )(*lead, sinks, *_hbm(att, att, b_attn))


def _attn_bwd(att, b_attn, sinks, dmix, *, name):
    T = att.shape[0]
    B, hd = ATT_BLOCK, ATT_HD
    NB = T // B
    scale = 1.0 / math.sqrt(hd)

    def body(sink_ref, cur_ref, prev_ref, ba_ref, do_ref, daq_ref, dakv_ref, dsink_ref, dbq_ref, dbkv_ref, carry_sc):
        n = pl.program_id(0)

        @pl.when(n == 0)
        def _():
            carry_sc[...] = jnp.zeros_like(carry_sc)
            dsink_ref[...] = jnp.zeros_like(dsink_ref)
            dbq_ref[...] = jnp.zeros_like(dbq_ref)
            dbkv_ref[...] = jnp.zeros_like(dbkv_ref)

        @pl.when(n < NB)
        def _():
            valid = _att_valid(n)
            hrow = lax.broadcasted_iota(jnp.int32, (SUBLANES, 128), 0)
            dsink = jnp.zeros((SUBLANES, 128), F32)
            dqs = []
            dks = [jnp.zeros((2 * B, hd), F32)] * ATT_KV
            dvs = [jnp.zeros((2 * B, hd), F32)] * ATT_KV
            for h0 in range(0, ATT_HEADS, ATT_STACK):
                kv = h0 // ATT_GROUP
                qs, kc, vc = _att_load(cur_ref, prev_ref, ba_ref, h0)
                prob, psink = _att_probs(qs, kc, valid, sink_ref, h0)
                dout = jnp.concatenate([do_ref[:, hd * (h0 + g):hd * (h0 + g + 1)] for g in range(ATT_STACK)], axis=0)
                dp = _dot_nt(vc, dout)
                delta = jnp.sum(prob * dp, axis=0, keepdims=True)
                dsc = prob * (dp - delta) * scale
                dq = _dot_tn(dsc, kc)
                dks[kv] = dks[kv] + _dot_nn(dsc, qs)
                dvs[kv] = dvs[kv] + _dot_nn(prob, dout)
                dsk = psink * delta
                for g in range(ATT_STACK):
                    dqs.append(dq[B * g:B * (g + 1)])
                    tot = jnp.sum(dsk[:, B * g:B * (g + 1)], axis=1, keepdims=True)
                    dsink = dsink - jnp.where(hrow == h0 + g, tot, 0.0)
            daq = jnp.concatenate(dqs, axis=1).astype(daq_ref.dtype)
            daq_ref[...] = daq
            dsink_ref[...] += dsink
            dbq_ref[...] += jnp.sum(daq.astype(F32), axis=0, keepdims=True)
            done = carry_sc[...] + jnp.concatenate([d[:B] for d in dks + dvs], axis=1)
            dakv_ref[...] = done.astype(dakv_ref.dtype)
            dbkv_ref[...] += jnp.sum(done.astype(dakv_ref.dtype).astype(F32), axis=0, keepdims=True)
            carry_sc[...] = jnp.concatenate([d[B:] for d in dks + dvs], axis=1)

        @pl.when(n == NB)
        def _():
            done = carry_sc[...]
            dakv_ref[...] = done.astype(dakv_ref.dtype)
            dbkv_ref[...] += jnp.sum(done.astype(dakv_ref.dtype).astype(F32), axis=0, keepdims=True)

    cl = lambda n: jnp.minimum(n, NB - 1)
    return pl.pallas_call(
        body, name=name, grid=(NB + 1,),
        in_specs=[pl.BlockSpec(memory_space=pltpu.SMEM),
                  pl.BlockSpec((B, ATT_COLS), lambda n: (cl(n), 0)),
                  pl.BlockSpec((B, ATT_COLS), lambda n: (jnp.maximum(cl(n) - 1, 0), 0)),
                  pl.BlockSpec((1, ATT_COLS), lambda n: (0, 0)),
                  pl.BlockSpec((B, ATT_Q_W), lambda n: (cl(n), 0))],
        out_specs=[pl.BlockSpec((B, ATT_Q_W), lambda n: (cl(n), 0)),
                   pl.BlockSpec((B, 2 * ATT_KV_W), lambda n: (jnp.maximum(n - 1, 0), 0)),
                   pl.BlockSpec((SUBLANES, 128), lambda n: (0, 0)),
                   pl.BlockSpec((1, ATT_Q_W), lambda n: (0, 0)),
                   pl.BlockSpec((1, 2 * ATT_KV_W), lambda n: (0, 0))],
        out_shape=[_sds((T, ATT_Q_W), BF16), _sds((T, 2 * ATT_KV_W), BF16), _sds((SUBLANES, 128), F32),
                   _sds((1, ATT_Q_W), F32), _sds((1, 2 * ATT_KV_W), F32)],
        scratch_shapes=[pltpu.VMEM((B, 2 * ATT_KV_W), F32)],
        compiler_params=_cp(("arbitrary",)),
    )(sinks, *_hbm(att, att, b_attn, dmix))


def _silu_and_grad(x):
    sg = _sigmoid(x)
    return x * sg, sg * (1.0 + x * (1.0 - sg))


def _mix_fwd_fn(o_raw, hg, o_att, hgw):
    outs = []
    for h in range(HG_HEADS):
        sl = slice(HG_DK * h, HG_DK * (h + 1))
        silu, _ = _silu_and_grad(hg[:, sl])
        outs.append(_rms_fwd(o_raw[:, sl], hgw) * silu)
    outs.append(o_att)
    return (jnp.concatenate(outs, axis=1),)


def _mix_bwd_fn(o_raw, hg, dmix, hgw):
    dos, dhgs = [], []
    dw = jnp.zeros((1, HG_DK), F32)
    for h in range(HG_HEADS):
        sl = slice(HG_DK * h, HG_DK * (h + 1))
        silu, dsilu = _silu_and_grad(hg[:, sl])
        dy = dmix[:, sl]
        dhgs.append(dy * _rms_fwd(o_raw[:, sl], hgw) * dsilu)
        dx, dwh = _rms_bwd(o_raw[:, sl], hgw, dy * silu)
        dos.append(dx)
        dw = dw + dwh
    return jnp.concatenate(dos, axis=1), jnp.concatenate(dhgs, axis=1), dw


def _final_fn(h2, tgt, wf):
    d = h2.shape[1]
    err = _rms_fwd(h2, wf) - tgt
    loss_cols = (0.5 / d) * jnp.sum(err * err, axis=0, keepdims=True)
    dh2, dwf = _rms_bwd(h2, wf, err * (1.0 / d))
    return dh2, dh2, loss_cols, dwf


class _NoExchange:
    def __init__(self, weights):
        self.weights = weights

    def start(self):
        return None

    def w_in(self, after):
        return self.weights["w_in_t"]

    def mid(self, after):
        return None

    def rest(self, after):
        return self.weights

    def ffn_grads(self, gs):
        return None

    def ffn_grads_send(self, after):
        return None


def _local_step(x, tgt, p, ex):
    T, D = x.shape
    row = lambda n, dt: _sds((T, n), dt)
    acc = lambda n: _sds((1, n), F32)

    (u,) = _rowwise(lambda xv, w: (_rms_fwd(xv, w),), [_full(x)], [p["norm_mix_w"]], [row(D, BF16)], [], name="rms_mix",
                    after=ex.start())
    p = dict(p, w_in_t=ex.w_in(u))
    hq, hf, hi, hg, att = _mm_nt(u, p["w_in_t"], splits=[HG_W] * 4 + [ATT_COLS], out_dtype=F32, name="in_proj")
    o_raw, states = _hgrn_fwd(hq, hf, hi, p["lb"], name="hgrn_fwd")
    o_att = _attn_fwd(att, p["b_attn"], p["sinks"], name="attn_fwd", after=ex.mid(o_raw))
    p = dict(p, **ex.rest(o_att))
    def out_epilogue(prod, xv, w):
        h1v = prod + xv
        return h1v, _rms_fwd(h1v, w)

    h1, v, mix = _mm_nn(None, [p["w_out"]], name="mix_out_proj",
                        prologue=(lambda *a: _mix_fwd_fn(*a)[0], [o_raw, hg, o_att], [p["hg_norm_w"]], row(D, BF16)),
                        epilogue=(out_epilogue, [x], [p["norm_ffn_w"]], [row(D, F32), row(D, BF16)], []))
    (gp,) = _mm_nt(v, p["w_gate_t"], splits=[D_FF], out_dtype=F32, name="gate_proj")
    (up,) = _mm_nt(v, p["w_up_t"], splits=[D_FF], out_dtype=F32, name="up_proj")
    act = _convact_fwd(gp, up, p["conv_w8"], p["conv_b"], name="convact_fwd")
    def down_epilogue(prod, h1v, tgtv, wf):
        return _final_fn(prod + h1v, tgtv, wf)

    dh2, dh2_b, loss_cols, d_final = _mm_nn(
        [[act]], [p["w_down"]], name="down_proj_loss",
        epilogue=(down_epilogue, [h1, tgt], [p["final_norm_w"]], [row(D, F32), row(D, BF16)], [acc(D), acc(D)]))

    (dact,) = _mm_nt(dh2_b, p["w_down"], splits=[D_FF], out_dtype=F32, name="d_act")
    g_down = _mm_tn([act], dh2_b, name="g_down")
    dgp, dup, d_conv_w8, d_conv_b = _convact_bwd(gp, up, dact, p["conv_w8"], p["conv_b"], name="convact_bwd")
    g_gate_t = _mm_tn([dgp], v, name="g_gate")
    g_up_t = _mm_tn([dup], v, name="g_up")
    swapping = ex.ffn_grads([g_gate_t, g_up_t, g_down])

    def ffn_norm_bwd(dvv, hv, dh2v, w):
        dx, dw = _rms_bwd(hv, w, dvv)
        dh1v = dx + dh2v
        return dh1v, dh1v, dw

    dh1, dh1_b, d_norm_ffn = _mm_nn(
        [[dgp], [dup]], [p["w_gate_t"], p["w_up_t"]], name="d_v_norm", after=swapping,
        epilogue=(ffn_norm_bwd, [h1, dh2], [p["norm_ffn_w"]], [row(D, F32), row(D, BF16)], [acc(D)]))
    sent = ex.ffn_grads_send(dh1_b)
    def mix_bwd(dmixv, o_rawv, hgv, hgw):
        do_rawv, dhgv, dw = _mix_bwd_fn(o_rawv, hgv, dmixv[:, :HG_W], hgw)
        return do_rawv, dhgv, dmixv[:, HG_W:], dw

    do_raw, dhg, do_att, d_hg_norm = _mm_nn(
        [[dh1_b]], [p["w_out"]], name="d_mix_bwd", w_transposed=True, after=sent,
        epilogue=(mix_bwd, [o_raw, hg], [p["hg_norm_w"]], [row(HG_W, F32), row(HG_W, BF16), row(ATT_Q_W, F32)], [acc(HG_DK)]))
    g_out = _mm_tn([mix], dh1_b, name="g_out")
    daq, dakv, d_sinks8, d_bq, d_bkv = _attn_bwd(att, p["b_attn"], p["sinks"], do_att, name="attn_bwd")
    dhq, dhf, dhi, d_lb = _hgrn_bwd(hq, hf, hi, p["lb"], states, do_raw, name="hgrn_bwd")
    pieces = [dhq, dhf, dhi, dhg, daq, dakv]
    g_in_t = _mm_tn(pieces, u, name="g_in")

    def mix_norm_bwd(duv, xv, dh1v, w):
        dx, dw = _rms_bwd(xv, w, duv)
        return dx + dh1v, dw

    dx, d_norm_mix = _mm_nn([pieces], [p["w_in_t"]], name="d_u_norm",
                            epilogue=(mix_norm_bwd, [x, dh1], [p["norm_mix_w"]], [row(D, F32)], [acc(D)]))
    grads = dict(g_in_t=g_in_t, g_out=g_out, g_gate_t=g_gate_t, g_up_t=g_up_t, g_down=g_down,
                 norm_mix_w=d_norm_mix, b_attn=jnp.concatenate([d_bq, d_bkv], axis=1), lb=d_lb, hg_norm_w=d_hg_norm,
                 sinks8=d_sinks8, norm_ffn_w=d_norm_ffn, conv_w8=d_conv_w8, conv_b=d_conv_b, final_norm_w=d_final)
    return loss_cols, dx, grads


SLAB = (IN_COLS // N_CHIPS, D_FF // N_CHIPS, D_FF // N_CHIPS, D_FF // N_CHIPS, D_MODEL // N_CHIPS)
N_W = len(SLAB)
PACK_OFF = tuple(sum(SLAB[:i]) for i in range(N_W))
PACK_ROWS = sum(SLAB)
FULL_OFF = tuple(N_CHIPS * o for o in PACK_OFF)
FULL_ROWS = N_CHIPS * PACK_ROWS
HALF = tuple(s // 2 for s in SLAB)
HPACK_OFF = tuple(sum(HALF[:i]) for i in range(N_W))
HPACK_ROWS = sum(HALF)
HFULL_OFF = tuple(N_CHIPS * o for o in HPACK_OFF)
HFULL_ROWS = N_CHIPS * HPACK_ROWS
CHIP_FLIPS = ((1, 0), (0, 1), (1, 1))
N_DEV = 8
BF16_ROWS = 16
ANY = pl.BlockSpec(memory_space=pl.ANY)


def _pos():
    return lax.axis_index("x"), lax.axis_index("y"), lax.axis_index("c")


def _flip(v, f):
    return 1 - v if f else v


def _rcopy(src, dst, ssem, rsem, dev):
    return pltpu.make_async_remote_copy(src_ref=src, dst_ref=dst, send_sem=ssem, recv_sem=rsem, device_id=dev,
                                        device_id_type=pl.DeviceIdType.MESH)


def _rows(ref, start, n, align=None):
    if not isinstance(start, int):
        if align is None:
            align = SUBLANES * (4 // jnp.dtype(ref.dtype).itemsize)
        start = pl.multiple_of(start, align)
    return ref.at[pl.ds(start, n), :]


FFN_W = (1, 2, 3)
N_PEER = 1 + len(CHIP_FLIPS)
HBM = pl.BlockSpec(memory_space=pltpu.HBM)
SEM = pl.BlockSpec(memory_space=pltpu.SEMAPHORE)
EFFECT = pltpu.SideEffectType.DATAFLOW_SIDE_EFFECTING
LANES = 128


def _sent_rows(k, w, c):
    return (0, SLAB[w]) if k == 0 else (c * HALF[w], HALF[w])


def _gather_start(pack, cw8):
    D = pack.shape[1]
    lands = [lax.empty((N_CHIPS * SLAB[0], D), pack.dtype), lax.empty((3 * N_CHIPS * SLAB[1], D), pack.dtype),
             lax.empty((N_CHIPS * SLAB[4], D), pack.dtype), lax.empty((N_CHIPS,) + cw8.shape, cw8.dtype)]
    bufs = [pack, cw8] + lands

    def body(pack_ref, cw_ref, l_in, l_ffn, l_out, l_cw, *rest):
        in_send, in_recv, out_send, out_recv, ffn_send, ffn_recv = rest[:6]
        token = rest[-1]
        x, y, c = _pos()
        q = 2 * x + y
        peers = _gather_peers(x, y, c)

        def send(k, peer, w, land, base, ssem, rsem):
            r0, n = _sent_rows(k, w, c)
            _rcopy(_rows(pack_ref, PACK_OFF[w] + r0, n), _rows(land, base + q * SLAB[w] + r0, n), ssem, rsem, peer).start()

        for k, peer in enumerate(peers):
            send(k, peer, 0, l_in, 0, in_send.at[k], in_recv.at[k])
        for k, peer in enumerate(peers):
            send(k, peer, 4, l_out, 0, out_send.at[k], out_recv.at[k])
            _rcopy(cw_ref, l_cw.at[q], out_send.at[N_PEER + k], out_recv.at[N_PEER + k], peer).start()
        for j, w in enumerate(FFN_W):
            for k, peer in enumerate(peers):
                send(k, peer, w, l_ffn, j * N_CHIPS * SLAB[w], ffn_send.at[k], ffn_recv.at[k])
        token[...] = jnp.zeros_like(token)

    n_sem = (N_PEER, N_PEER, 2 * N_PEER, 2 * N_PEER, N_PEER, N_PEER)
    outs = pl.pallas_call(
        body, name="gather_start", in_specs=[HBM] * len(bufs),
        out_specs=[SEM] * len(n_sem) + [HBM] * len(bufs) + [pl.BlockSpec(memory_space=pltpu.VMEM)],
        out_shape=[pltpu.SemaphoreType.DMA((n,)) for n in n_sem]
        + [pltpu.HBM(b.shape, b.dtype) for b in bufs] + [TOKEN],
        input_output_aliases={i: len(n_sem) + i for i in range(len(bufs))},
        compiler_params=pltpu.CompilerParams(has_side_effects=EFFECT),
    )(*[pltpu.with_memory_space_constraint(b, pltpu.HBM) for b in bufs])
    bufs_out = outs[len(n_sem):]
    return dict(in_sems=outs[0:2], out_sems=outs[2:4], ffn_sems=outs[4:6], pack=bufs_out[0], cw=bufs_out[1], l_in=bufs_out[2],
                l_ffn=bufs_out[3], l_out=bufs_out[4], l_cw=bufs_out[5], token=bufs_out[6])


def _gather_peers(x, y, c):
    return [(x, y, 1 - c)] + [(_flip(x, fx), _flip(y, fy), c) for fx, fy in CHIP_FLIPS]


def _gather_wait_in(g, after):
    def body(pack_ref, l_in, send, recv, after_ref, pack_out, l_out):
        for k, peer in enumerate(_gather_peers(*_pos())):
            n = _sent_rows(k, 0, 0)[1]
            cp = _rcopy(_rows(pack_ref, PACK_OFF[0], n), _rows(l_in, 0, n), send.at[k], recv.at[k], peer)
            cp.wait_send()
            cp.wait_recv()

    return pl.pallas_call(
        body, name="gather_wait_in", in_specs=[HBM, HBM, SEM, SEM, ANY], out_specs=[HBM, HBM],
        out_shape=[pltpu.HBM(g["pack"].shape, g["pack"].dtype), pltpu.HBM(g["l_in"].shape, g["l_in"].dtype)],
        input_output_aliases={0: 0, 1: 1}, compiler_params=pltpu.CompilerParams(has_side_effects=EFFECT),
    )(g["pack"], g["l_in"], *g["in_sems"], after)


def _gather_wait_rest(g, pack, after):
    def body(pack_ref, cw_ref, l_ffn, l_out, l_cw, o_send, o_recv, f_send, f_recv, after_ref, o_ffn, o_out, o_cw):
        for k, peer in enumerate(_gather_peers(*_pos())):
            n_out = _sent_rows(k, 4, 0)[1]
            n_ffn = len(FFN_W) * _sent_rows(k, FFN_W[0], 0)[1]
            for cp in (_rcopy(_rows(pack_ref, PACK_OFF[4], n_out), _rows(l_out, 0, n_out), o_send.at[k], o_recv.at[k], peer),
                       _rcopy(cw_ref, l_cw.at[0], o_send.at[N_PEER + k], o_recv.at[N_PEER + k], peer),
                       _rcopy(_rows(pack_ref, PACK_OFF[FFN_W[0]], n_ffn), _rows(l_ffn, 0, n_ffn), f_send.at[k], f_recv.at[k], peer)):
                cp.wait_send()
                cp.wait_recv()

    ins = [pack, g["cw"], g["l_ffn"], g["l_out"], g["l_cw"]]
    return pl.pallas_call(
        body, name="gather_wait_rest", in_specs=[HBM] * 5 + [SEM] * 4 + [ANY], out_specs=[HBM] * 3,
        out_shape=[pltpu.HBM(b.shape, b.dtype) for b in ins[2:]],
        input_output_aliases={2: 0, 3: 1, 4: 2}, compiler_params=pltpu.CompilerParams(has_side_effects=EFFECT),
    )(*ins, *g["out_sems"], *g["ffn_sems"], after)


FWD_IN = ((0, 0, 0),)
FWD_REST = tuple((0, w, j * N_CHIPS * SLAB[w]) for j, w in enumerate(FFN_W)) + ((1, 4, 0),)


def _forward_copies(layout, src, dst, send_sems, recv_sems):
    x, y, c = _pos()
    sib = (x, y, 1 - c)
    cps = []
    for fx, fy in CHIP_FLIPS:
        qa = 2 * _flip(x, fx) + _flip(y, fy)
        for bi, w, base in layout:
            r0 = base + qa * SLAB[w] + c * HALF[w]
            cps.append(_rcopy(_rows(src[bi], r0, HALF[w]), _rows(dst[bi], r0, HALF[w]),
                              send_sems.at[len(cps)], recv_sems.at[len(cps)], sib))
    return cps


def _forward_in(l_in):
    n = len(CHIP_FLIPS) * len(FWD_IN)

    def body(in_ref, out_ref, send_sems, recv_sems):
        cps = _forward_copies(FWD_IN, [in_ref], [out_ref], send_sems, recv_sems)
        for cp in cps:
            cp.start()
        for cp in cps:
            cp.wait_recv()
        for cp in cps:
            cp.wait_send()

    return pl.pallas_call(
        body, name="forward_in", in_specs=[ANY], out_specs=ANY, out_shape=_sds(l_in.shape, l_in.dtype),
        input_output_aliases={0: 0},
        scratch_shapes=[pltpu.SemaphoreType.DMA((n,)), pltpu.SemaphoreType.DMA((n,))],
    )(l_in)


def _forward_rest_start(l_ffn, l_out):
    n = len(CHIP_FLIPS) * len(FWD_REST)
    bufs = [l_ffn, l_out]

    def body(a_ref, b_ref, send_sems, recv_sems, a_out, b_out, token):
        for cp in _forward_copies(FWD_REST, [a_ref, b_ref], [a_ref, b_ref], send_sems, recv_sems):
            cp.start()
        token[...] = jnp.zeros_like(token)

    outs = pl.pallas_call(
        body, name="forward_rest_start", in_specs=[HBM] * 2,
        out_specs=[SEM, SEM, HBM, HBM, pl.BlockSpec(memory_space=pltpu.VMEM)],
        out_shape=[pltpu.SemaphoreType.DMA((n,)), pltpu.SemaphoreType.DMA((n,))]
        + [pltpu.HBM(b.shape, b.dtype) for b in bufs] + [TOKEN],
        input_output_aliases={0: 2, 1: 3}, compiler_params=pltpu.CompilerParams(has_side_effects=EFFECT),
    )(*[pltpu.with_memory_space_constraint(b, pltpu.HBM) for b in bufs])
    return dict(sems=outs[0:2], bufs=outs[2:4], token=outs[4])


def _forward_rest_wait(s, after):
    def body(a_ref, b_ref, send_sems, recv_sems, after_ref, a_out, b_out):
        for cp in _forward_copies(FWD_REST, [a_ref, b_ref], [a_ref, b_ref], send_sems, recv_sems):
            cp.wait_send()
            cp.wait_recv()

    return pl.pallas_call(
        body, name="forward_rest_wait", in_specs=[HBM, HBM, SEM, SEM, ANY], out_specs=[HBM, HBM],
        out_shape=[pltpu.HBM(b.shape, b.dtype) for b in s["bufs"]],
        input_output_aliases={0: 0, 1: 1}, compiler_params=pltpu.CompilerParams(has_side_effects=EFFECT),
    )(*s["bufs"], *s["sems"], after)


def _exchange_halves(ws, gs, small, *, name):
    D = gs[0].shape[1]
    n = len(ws)
    has_small = small is not None

    def body(*refs):
        g = refs[:n]
        t = refs[n + has_small:2 * n + has_small]
        sems = refs[2 * n + 2 * has_small:]
        d2d_send, d2d_recv = sems[0], sems[1]
        x, y, c = _pos()
        sib = (x, y, 1 - c)
        drains = []
        for i, w in enumerate(ws):
            h = HALF[w]
            for qq in range(N_CHIPS):
                _rcopy(_rows(g[i], qq * SLAB[w] + (1 - c) * h, h), _rows(t[i], qq * h, h),
                       d2d_send.at[i], d2d_recv.at[i], sib).start()
            drains.append(_rcopy(t[i], t[i], d2d_send.at[i], d2d_recv.at[i], sib))
        if has_small:
            small_ref, sall_ref = refs[n], refs[2 * n + 1]
            sm_send, sm_recv, loc_sem = sems[2], sems[3], sems[4]
            me = 4 * x + 2 * y + c
            own_small = pltpu.make_async_copy(small_ref, sall_ref.at[me], loc_sem)
            own_small.start()
            for f in range(1, N_DEV):
                peer = (_flip(x, f & 4), _flip(y, f & 2), _flip(c, f & 1))
                cp = _rcopy(small_ref, sall_ref.at[me], sm_send.at[f - 1], sm_recv.at[f - 1], peer)
                cp.start()
                drains.append(cp)
        for d in drains:
            d.wait_recv()
        for d in drains:
            d.wait_send()
        if has_small:
            own_small.wait()

    out_shape = [_sds((N_CHIPS * HALF[w], D), gs[0].dtype) for w in ws]
    scratch = [pltpu.SemaphoreType.DMA((n,)), pltpu.SemaphoreType.DMA((n,))]
    if has_small:
        out_shape.append(_sds((N_DEV,) + small.shape, F32))
        scratch += [pltpu.SemaphoreType.DMA((N_DEV - 1,)), pltpu.SemaphoreType.DMA((N_DEV - 1,)), pltpu.SemaphoreType.DMA]
    return pl.pallas_call(
        body, name=name, in_specs=[ANY] * (n + has_small), out_specs=[ANY] * (n + has_small),
        out_shape=out_shape, scratch_shapes=scratch,
    )(*gs, *([small] if has_small else []))


def _halves_copies(ws, g, t, send_sems, recv_sems):
    x, y, c = _pos()
    sib = (x, y, 1 - c)
    cps = []
    for i, w in enumerate(ws):
        h = HALF[w]
        for qq in range(N_CHIPS):
            cps.append(_rcopy(_rows(g[i], qq * SLAB[w] + (1 - c) * h, h), _rows(t[i], qq * h, h),
                              send_sems.at[N_CHIPS * i + qq], recv_sems.at[N_CHIPS * i + qq], sib))
    return cps


def _halves_start(ws, gs, *, name):
    D = gs[0].shape[1]
    n = len(ws)
    bufs = list(gs) + [lax.empty((N_CHIPS * HALF[w], D), gs[0].dtype) for w in ws]

    def body(*refs):
        for cp in _halves_copies(ws, refs[:n], refs[n:2 * n], refs[2 * n], refs[2 * n + 1]):
            cp.start()
        refs[-1][...] = jnp.zeros_like(refs[-1])

    outs = pl.pallas_call(
        body, name=name, in_specs=[HBM] * (2 * n),
        out_specs=[SEM, SEM] + [HBM] * (2 * n) + [pl.BlockSpec(memory_space=pltpu.VMEM)],
        out_shape=[pltpu.SemaphoreType.DMA((N_CHIPS * n,)), pltpu.SemaphoreType.DMA((N_CHIPS * n,))]
        + [pltpu.HBM(b.shape, b.dtype) for b in bufs] + [TOKEN],
        input_output_aliases={i: 2 + i for i in range(2 * n)},
        compiler_params=pltpu.CompilerParams(has_side_effects=EFFECT),
    )(*[pltpu.with_memory_space_constraint(b, pltpu.HBM) for b in bufs])
    return dict(sems=outs[0:2], gs=outs[2:2 + n], theirs=outs[2 + n:2 + 2 * n], token=outs[-1])


def _halves_wait(ws, s, after, *, name):
    n = len(ws)

    def body(*refs):
        for cp in _halves_copies(ws, refs[:n], refs[n:2 * n], refs[2 * n], refs[2 * n + 1]):
            cp.wait_send()
            cp.wait_recv()

    bufs = list(s["gs"]) + list(s["theirs"])
    outs = pl.pallas_call(
        body, name=name, in_specs=[HBM] * (2 * n) + [SEM, SEM, ANY], out_specs=[HBM] * (2 * n),
        out_shape=[pltpu.HBM(b.shape, b.dtype) for b in bufs],
        input_output_aliases={i: i for i in range(2 * n)},
        compiler_params=pltpu.CompilerParams(has_side_effects=EFFECT),
    )(*bufs, *s["sems"], after)
    return outs[:n], outs[n:]


REDUCE_SPLIT = 2


def _chip_partial(ws, gs, theirs, *, name, out_dtype=F32):
    D = gs[0].shape[1]
    n = len(ws)

    def body(*refs):
        for i in range(n):
            refs[2 * n + i][...] = (refs[i][...].astype(F32) + refs[n + i][...].astype(F32)).astype(out_dtype)

    blk = [HALF[w] // REDUCE_SPLIT for w in ws]
    mine = [pl.BlockSpec((b, D), lambda qq, j: ((2 * qq + lax.axis_index("c")) * REDUCE_SPLIT + j, 0)) for b in blk]
    flat = [pl.BlockSpec((b, D), lambda qq, j: (qq * REDUCE_SPLIT + j, 0)) for b in blk]
    return pl.pallas_call(
        body, name=name, grid=(N_CHIPS, REDUCE_SPLIT), in_specs=mine + flat, out_specs=flat,
        out_shape=[_sds((N_CHIPS * HALF[w], D), out_dtype) for w in ws],
        compiler_params=_cp(("parallel", "parallel")),
    )(*_hbm(*gs, *theirs))


def _partial_copies(ws, part, got, send_sems, recv_sems):
    x, y, c = _pos()
    cps = []
    for k, (fx, fy) in enumerate(CHIP_FLIPS):
        peer = (_flip(x, fx), _flip(y, fy), c)
        qp = 2 * _flip(x, fx) + _flip(y, fy)
        for i, w in enumerate(ws):
            cps.append(_rcopy(_rows(part[i], qp * HALF[w], HALF[w]), _rows(got[i], k * HALF[w], HALF[w]),
                              send_sems.at[len(ws) * k + i], recv_sems.at[len(ws) * k + i], peer))
    return cps


def _send_chip_partials(ws, parts, *, name):
    D = parts[0].shape[1]
    n = len(ws)

    def body(*refs):
        cps = _partial_copies(ws, refs[:n], refs[n:2 * n], refs[2 * n], refs[2 * n + 1])
        for cp in cps:
            cp.start()
        for cp in cps:
            cp.wait_recv()
        for cp in cps:
            cp.wait_send()

    return pl.pallas_call(
        body, name=name, in_specs=[ANY] * n, out_specs=[ANY] * n,
        out_shape=[_sds((len(CHIP_FLIPS) * HALF[w], D), parts[0].dtype) for w in ws],
        scratch_shapes=[pltpu.SemaphoreType.DMA((len(CHIP_FLIPS) * n,)), pltpu.SemaphoreType.DMA((len(CHIP_FLIPS) * n,))],
    )(*parts)


def _send_start(ws, parts, *, name):
    D = parts[0].shape[1]
    n = len(ws)
    bufs = list(parts) + [lax.empty((len(CHIP_FLIPS) * HALF[w], D), parts[0].dtype) for w in ws]

    def body(*refs):
        send_sems, recv_sems = refs[2 * n], refs[2 * n + 1]
        for cp in _partial_copies(ws, refs[:n], refs[n:2 * n], send_sems, recv_sems):
            cp.start()
        refs[-1][...] = jnp.zeros_like(refs[-1])

    outs = pl.pallas_call(
        body, name=name, in_specs=[HBM] * (2 * n),
        out_specs=[SEM, SEM] + [HBM] * (2 * n) + [pl.BlockSpec(memory_space=pltpu.VMEM)],
        out_shape=[pltpu.SemaphoreType.DMA((len(CHIP_FLIPS) * n,)), pltpu.SemaphoreType.DMA((len(CHIP_FLIPS) * n,))]
        + [pltpu.HBM(b.shape, b.dtype) for b in bufs] + [TOKEN],
        input_output_aliases={i: 2 + i for i in range(2 * n)},
        compiler_params=pltpu.CompilerParams(has_side_effects=EFFECT),
    )(*[pltpu.with_memory_space_constraint(b, pltpu.HBM) for b in bufs])
    return dict(sems=outs[0:2], parts=outs[2:2 + n], got=outs[2 + n:2 + 2 * n], token=outs[-1])


def _send_wait(ws, s, after, *, name):
    n = len(ws)

    def body(*refs):
        for cp in _partial_copies(ws, refs[:n], refs[n:2 * n], refs[2 * n], refs[2 * n + 1]):
            cp.wait_send()
            cp.wait_recv()

    bufs = list(s["parts"]) + list(s["got"])
    outs = pl.pallas_call(
        body, name=name, in_specs=[HBM] * (2 * n) + [SEM, SEM, ANY], out_specs=[HBM] * (2 * n),
        out_shape=[pltpu.HBM(b.shape, b.dtype) for b in bufs],
        input_output_aliases={i: i for i in range(2 * n)},
        compiler_params=pltpu.CompilerParams(has_side_effects=EFFECT),
    )(*bufs, *s["sems"], after)
    return outs[:n], outs[n:]


def _chip_reduce(ws, parts, got, *, name, after=None):
    D = parts[0].shape[1]
    nk = len(CHIP_FLIPS)
    n = len(ws)
    extra = [] if after is None else [after]

    def body(*refs):
        refs = refs[len(extra):]
        outs = refs[(1 + nk) * n:]
        for i in range(n):
            acc = refs[i][...].astype(F32)
            for k in range(nk):
                acc = acc + refs[n * (1 + k) + i][...].astype(F32)
            outs[i][...] = acc

    blk = [HALF[w] // REDUCE_SPLIT for w in ws]

    def q_idx(j):
        return (2 * lax.axis_index("x") + lax.axis_index("y")) * REDUCE_SPLIT + j

    in_specs = [pl.BlockSpec((b, D), lambda j: (q_idx(j), 0)) for b in blk]
    for k in range(nk):
        in_specs += [pl.BlockSpec((b, D), functools.partial(lambda j, k: (k * REDUCE_SPLIT + j, 0), k=k)) for b in blk]
    out_specs = [pl.BlockSpec((b, D), lambda j: (lax.axis_index("c") * REDUCE_SPLIT + j, 0)) for b in blk]
    return pl.pallas_call(
        body, name=name, grid=(REDUCE_SPLIT,), in_specs=[ANY] * len(extra) + in_specs, out_specs=out_specs,
        out_shape=[_sds((SLAB[w], D), F32) for w in ws],
        compiler_params=_cp(("parallel",)),
    )(*extra, *_hbm(*parts, *[g for _ in range(nk) for g in got]))


def _exchange_reduced(ws, shards, *, name):
    n = len(ws)

    def body(*refs):
        ins, outs = refs[:n], refs[n:2 * n]
        send_sems, recv_sems = refs[2 * n], refs[2 * n + 1]
        x, y, c = _pos()
        sib = (x, y, 1 - c)
        cps = []
        for i, w in enumerate(ws):
            cp = _rcopy(_rows(ins[i], c * HALF[w], HALF[w]), _rows(outs[i], c * HALF[w], HALF[w]),
                        send_sems.at[i], recv_sems.at[i], sib)
            cp.start()
            cps.append(cp)
        for cp in cps:
            cp.wait_recv()
        for cp in cps:
            cp.wait_send()

    return pl.pallas_call(
        body, name=name, in_specs=[ANY] * n, out_specs=[ANY] * n,
        out_shape=[_sds(s.shape, s.dtype) for s in shards], input_output_aliases={i: i for i in range(n)},
        scratch_shapes=[pltpu.SemaphoreType.DMA((n,)), pltpu.SemaphoreType.DMA((n,))],
    )(*shards)


def _adamw_fn(w, g, m, v):
    m2 = ADAM_B1 * m + (1.0 - ADAM_B1) * g
    v2 = ADAM_B2 * v + (1.0 - ADAM_B2) * (g * g)
    m_hat = m2 / (1.0 - ADAM_B1 ** ADAM_STEP)
    v_hat = v2 / (1.0 - ADAM_B2 ** ADAM_STEP)
    return -ADAM_LR * (m_hat / (jnp.sqrt(v_hat) + ADAM_EPS) + ADAM_WD * w), m2, v2


def _adamw(w, g, m, v, *, name):
    shp = _sds(w.shape, F32)
    rows = w.shape[0]
    tm = max(t for t in range(SUBLANES, 512 + 1, SUBLANES) if rows % t == 0)
    return _rowwise(_adamw_fn, [_full(w), _full(g), _full(m), _full(v)], [], [shp] * 3, [], name=name, tm=tm)


SMALL_SEGS = (("loss", 8), ("norm_mix_w", 8), ("b_attn", 8), ("lb_logits", 8), ("hg_norm_w", 8), ("sinks", 8),
              ("norm_ffn_w", 8), ("conv_w", 72), ("conv_b", 24), ("final_norm_w", 8))
SMALL_OFF = {n: sum(r for _, r in SMALL_SEGS[:i]) for i, (n, _) in enumerate(SMALL_SEGS)}
SMALL_ROWS = sum(r for _, r in SMALL_SEGS)
LANES = 128


def _pack_small(parts):
    segs = []
    for n, r in SMALL_SEGS:
        a = parts.get(n)
        flat = jnp.zeros((0,), F32) if a is None else a.reshape(-1).astype(F32)
        segs.append(jnp.pad(flat, (0, r * LANES - flat.shape[0])).reshape(r, LANES))
    return jnp.concatenate(segs, axis=0)


def _unpack_small(pack, n, shape):
    size = math.prod(shape)
    r0 = SMALL_OFF[n]
    return pack[r0:r0 + dict(SMALL_SEGS)[n]].reshape(-1)[:size].reshape(shape)


def _small_update(sall, wp, mp, vp):
    R = SMALL_ROWS
    r_lb = SMALL_OFF["lb_logits"]

    def body(s_ref, w_ref, m_ref, v_ref, g_ref, d_ref, m2_ref, v2_ref, loss_ref):
        g = s_ref[0]
        for i in range(1, N_DEV):
            g = g + s_ref[i]
        tot = jnp.sum(jnp.sum(g[0:8], axis=1, keepdims=True), axis=0, keepdims=True)
        loss_ref[...] = jnp.broadcast_to(tot, loss_ref.shape)
        lg = w_ref[r_lb:r_lb + 8, :]
        p0 = _sigmoid(lg - pltpu.roll(lg, 4, 0))
        d = g[r_lb:r_lb + 8]
        d = d + pltpu.roll(d, 4, 0)
        sign = jnp.where(lax.broadcasted_iota(jnp.int32, d.shape, 0) < 4, 1.0, -1.0)
        g = jnp.concatenate([g[:r_lb], sign * d * p0 * (1.0 - p0), g[r_lb + 8:]], axis=0)
        g_ref[...] = g
        d_ref[...], m2_ref[...], v2_ref[...] = _adamw_fn(w_ref[...], g, m_ref[...], v_ref[...])

    full = pl.BlockSpec((R, LANES), lambda: (0, 0))
    return pl.pallas_call(
        body, name="small_update",
        in_specs=[pl.BlockSpec((N_DEV, R, LANES), lambda: (0, 0, 0)), full, full, full],
        out_specs=[full, full, full, full, pl.BlockSpec((8, LANES), lambda: (0, 0))],
        out_shape=[_sds((R, LANES), F32)] * 4 + [_sds((8, LANES), F32)],
        compiler_params=_cp(),
    )(sall, wp, mp, vp)


def _lb_fwd(lb_logits):
    n = lb_logits.shape[1]

    def body(l_ref, o_ref):
        o_ref[...] = _sigmoid(l_ref[0:1, :] - l_ref[1:2, :])

    return pl.pallas_call(body, name="lb_fwd", out_shape=jax.ShapeDtypeStruct((1, n), F32), compiler_params=_cp())(lb_logits)


class _MeshExchange:
    def __init__(self, pack, cw8):
        self.gather = _gather_start(pack, cw8)
        self.sent = None
        self.conv_w8 = None

    def start(self):
        return self.gather["token"]

    def w_in(self, after):
        self.pack, l_in = _gather_wait_in(self.gather, after)
        return (_forward_in(l_in), N_CHIPS * SLAB[0], 0)

    def mid(self, after):
        l_ffn, l_out, l_cw = _gather_wait_rest(self.gather, self.pack, after)
        self.conv_w8 = jnp.concatenate([l_cw[i] for i in range(N_CHIPS)], axis=1)
        self.passing = _forward_rest_start(l_ffn, l_out)
        return self.passing["token"]

    def rest(self, after):
        l_ffn, l_out = _forward_rest_wait(self.passing, after)
        rows = N_CHIPS * SLAB[FFN_W[0]]
        return dict(w_gate_t=(l_ffn, rows, 0), w_up_t=(l_ffn, rows, 1), w_down=(l_ffn, rows, 2),
                    w_out=(l_out, N_CHIPS * SLAB[4], 0), conv_w8=self.conv_w8)

    def ffn_grads(self, gs):
        self.swap = _halves_start(FFN_W, gs, name="halves_ffn_start")
        return self.swap["token"]

    def ffn_grads_send(self, after):
        gs, theirs = _halves_wait(FFN_W, self.swap, after, name="halves_ffn_wait")
        parts = _chip_partial(FFN_W, gs, theirs, name="chip_partial_ffn", out_dtype=BF16)
        self.sent = _send_start(FFN_W, parts, name="send_ffn_start")
        return self.sent["token"]


def kernel(x, norm_mix_w, w_in, b_attn, lb_logits, hg_norm_w, sinks, w_out, norm_ffn_w, w_gate, w_up, conv_w, conv_b, w_down, final_norm_w, loss_target, m_norm_mix_w, m_w_in, m_b_attn, m_lb_logits, m_hg_norm_w, m_sinks, m_w_out, m_norm_ffn_w, m_w_gate, m_w_up, m_conv_w, m_conv_b, m_w_down, m_final_norm_w, v_norm_mix_w, v_w_in, v_b_attn, v_lb_logits, v_hg_norm_w, v_sinks, v_w_out, v_norm_ffn_w, v_w_gate, v_w_up, v_conv_w, v_conv_b, v_w_down, v_final_norm_w):
    D = D_MODEL
    q = 2 * lax.axis_index("x") + lax.axis_index("y")
    ccols = D_FF // N_CHIPS

    pack = jnp.concatenate([w_in[0].T, w_gate[0].T, w_up[0].T, w_down[0], w_out[0]], axis=0).astype(BF16)
    cw8 = jnp.concatenate([conv_w[0], jnp.zeros((SUBLANES - 3, ccols), F32)], axis=0)
    ex = _MeshExchange(pack, cw8)
    p = dict(norm_mix_w=norm_mix_w, b_attn=b_attn, lb=_lb_fwd(lb_logits), hg_norm_w=hg_norm_w, sinks=sinks,
             norm_ffn_w=norm_ffn_w, conv_b=conv_b, final_norm_w=final_norm_w.reshape(1, D))
    loss_cols, dx, g = _local_step(x[0], loss_target[0], p, ex)
    conv_w8 = ex.conv_w8

    small = _pack_small(dict(loss=loss_cols, norm_mix_w=g["norm_mix_w"], b_attn=g["b_attn"], lb_logits=g["lb"],
                             hg_norm_w=g["hg_norm_w"], sinks=g["sinks8"], norm_ffn_w=g["norm_ffn_w"],
                             conv_w=g["conv_w8"][:3], conv_b=g["conv_b"], final_norm_w=g["final_norm_w"]))
    parts_ffn, got_ffn = _send_wait(FFN_W, ex.sent, dx, name="send_ffn_wait")
    late = (0, 4)
    gs = [g["g_in_t"], g["g_out"]]
    *theirs, sall = _exchange_halves(late, gs, small, name="exchange_halves_late")
    parts_late = _chip_partial(late, gs, theirs, name="chip_partial_late", out_dtype=BF16)
    sent_late = _send_start(late, parts_late, name="send_late_start")
    big = {}

    def finish(ws, parts, got, specs, tag, after):
        shards = _exchange_reduced(ws, _chip_reduce(ws, parts, got, name="chip_reduce_" + tag, after=after),
                                   name="exchange_reduced_" + tag)
        for gw, (n, w, m, v, tr) in zip(shards, specs):
            view = (lambda a: a[0].T) if tr else (lambda a: a[0])
            back = (lambda a: a.T[None]) if tr else (lambda a: a[None])
            d_, m_, v_ = _adamw(view(w), gw, view(m), view(v), name="adamw_" + n)
            big[n] = (back(gw), back(d_), back(m_), back(v_))
        return d_

    last = finish(FFN_W, parts_ffn, got_ffn, (("w_gate", w_gate, m_w_gate, v_w_gate, True), ("w_up", w_up, m_w_up, v_w_up, True),
                                              ("w_down", w_down, m_w_down, v_w_down, False)), "ffn", sent_late["token"])
    parts_late, got_late = _send_wait(late, sent_late, last, name="send_late_wait")
    finish(late, parts_late, got_late, (("w_in", w_in, m_w_in, v_w_in, True), ("w_out", w_out, m_w_out, v_w_out, False)),
           "late", None)

    def place(a):
        return lax.dynamic_update_slice(jnp.zeros((3, D_FF), F32), a[0], (0, q * ccols))

    def small_pack(ws, cw):
        nm, ba, lbl, hg, sk, nf, cb, fn = ws
        return _pack_small(dict(norm_mix_w=nm, b_attn=ba, lb_logits=lbl, hg_norm_w=hg,
                                sinks=jnp.broadcast_to(sk.reshape(ATT_HEADS, 1), (ATT_HEADS, LANES)), norm_ffn_w=nf,
                                conv_w=cw, conv_b=cb, final_norm_w=fn))

    wp = small_pack((norm_mix_w, b_attn, lb_logits, hg_norm_w, sinks, norm_ffn_w, conv_b, final_norm_w), conv_w8[:3])
    mp = small_pack((m_norm_mix_w, m_b_attn, m_lb_logits, m_hg_norm_w, m_sinks, m_norm_ffn_w, m_conv_b, m_final_norm_w),
                    place(m_conv_w))
    vp = small_pack((v_norm_mix_w, v_b_attn, v_lb_logits, v_hg_norm_w, v_sinks, v_norm_ffn_w, v_conv_b, v_final_norm_w),
                    place(v_conv_w))
    outs = _small_update(sall, wp, mp, vp)
    loss = outs[4][0, 0]

    def small_out(pk, n, ref):
        if n == "sinks":
            return pk[SMALL_OFF[n]:SMALL_OFF[n] + ATT_HEADS, 0].reshape(ref.shape)
        if n == "conv_w":
            full = _unpack_small(pk, n, (3, D_FF))
            return lax.dynamic_slice(full, (0, q * ccols), (3, ccols))[None]
        return _unpack_small(pk, n, ref.shape)

    refs = dict(norm_mix_w=norm_mix_w, b_attn=b_attn, lb_logits=lb_logits, hg_norm_w=hg_norm_w, sinks=sinks,
                norm_ffn_w=norm_ffn_w, conv_w=conv_w, conv_b=conv_b, final_norm_w=final_norm_w)
    order = ("norm_mix_w", "w_in", "b_attn", "lb_logits", "hg_norm_w", "sinks", "w_out", "norm_ffn_w", "w_gate", "w_up",
             "conv_w", "conv_b", "w_down", "final_norm_w")
    res = [loss, dx[None]]
    for k in range(4):
        for n in order:
            res.append(big[n][k] if n in big else small_out(outs[k], n, refs[n]))
    return tuple(res)
```

```python
import functools
import math

import jax
import jax.numpy as jnp
from jax import lax
from jax.experimental import pallas as pl
from jax.experimental.pallas import tpu as pltpu

F32 = jnp.float32
BF16 = jnp.bfloat16

D_MODEL = 1024
HG_HEADS = 4
HG_DK = 128
HG_W = HG_HEADS * HG_DK
HG_CHUNK = 64
HG_SUB = 8
HG_FWD_CHUNKS_PER_STEP = 4
HG_CHUNKS_PER_STEP = 2
ATT_HEADS = 8
ATT_KV = 2
ATT_GROUP = ATT_HEADS // ATT_KV
ATT_HD = 64
ATT_BLOCK = 128
ATT_Q_W = ATT_HEADS * ATT_HD
ATT_KV_W = ATT_KV * ATT_HD
ATT_COLS = ATT_Q_W + 2 * ATT_KV_W
IN_COLS = 4 * HG_W + ATT_COLS
D_FF = 2816
EPS = 1e-6
ADAM_LR, ADAM_B1, ADAM_B2, ADAM_EPS, ADAM_WD, ADAM_STEP = 0.001, 0.9, 0.999, 1e-08, 0.01, 10
NEG = -1e30

V7X_VMEM_BYTES = 64 * 1024 * 1024
VMEM_LIMIT = 48 * 1024 * 1024
SUBLANES = 8

N_CHIPS = 4


def _cp(sem=None, **kw):
    return pltpu.CompilerParams(dimension_semantics=sem, vmem_limit_bytes=VMEM_LIMIT, **kw)


def _sds(shape, dtype):
    return jax.ShapeDtypeStruct(shape, dtype)


TOKEN = jax.ShapeDtypeStruct((8, 128), jnp.float32)


def _wspec(w):
    arr, rows, blk = w
    return pl.BlockSpec((rows, arr.shape[1]), lambda i: (blk, 0))


def _mm_nt(a, w, *, splits, out_dtype, name, after=None, tm=512):
    M, K = a.shape
    N = w[1]
    tm = min(tm, M)
    assert sum(splits) == N and M % tm == 0
    offs = [sum(splits[:i]) for i in range(len(splits))]
    n_in = 2 if after is None else 3

    def body(*refs):
        a_ref, w_ref = refs[0], refs[1]
        acc = lax.dot_general(a_ref[...], w_ref[...], (((1,), (1,)), ((), ())), preferred_element_type=F32)
        for o_ref, c0, n in zip(refs[n_in:], offs, splits):
            o_ref[...] = acc[:, c0:c0 + n].astype(out_dtype)

    in_specs = [pl.BlockSpec((tm, K), lambda i: (i, 0)), _wspec(w)]
    args = [a, w[0]]
    if after is not None:
        in_specs.append(pl.BlockSpec(memory_space=pl.ANY))
        args.append(after)
    outs = pl.pallas_call(
        body, name=name, grid=(M // tm,), in_specs=in_specs,
        out_specs=[pl.BlockSpec((tm, n), lambda i: (i, 0)) for n in splits],
        out_shape=[_sds((M, n), out_dtype) for n in splits],
        compiler_params=_cp(("parallel",)),
    )(*args)
    return outs


def _mm_nn(pieces, ws, *, name, out_dtype=F32, residual=None, epilogue=None, prologue=None, after=None,
           w_transposed=False, tm=512):
    pro_fn, pro_rows, pro_bc, pro_out = prologue or (None, [], [], None)
    if prologue is not None:
        assert pieces is None and len(ws) == 1
        pieces = [[pro_out]]
    M = pieces[0][0].shape[0]
    K = ws[0][1] if w_transposed else ws[0][0].shape[1]
    tm = min(tm, M)
    flat = [] if prologue is not None else [p for grp in pieces for p in grp]
    n_p = len(flat)
    n_w = len(ws)
    n_pr, n_pb = len(pro_rows), len(pro_bc)
    fn, row_ins, bc_ins, row_outs, acc_outs = epilogue or (None, [], [], [_sds((M, K), out_dtype)], [])
    if residual is not None:
        assert epilogue is None
        row_ins = [residual]
    n_r, n_b, n_o = len(row_ins), len(bc_ins), len(row_outs)
    lead = [] if after is None else [after]

    def body(*refs):
        refs = refs[len(lead):]
        p_refs = refs[:n_p]
        w_refs = refs[n_p:n_p + n_w]
        extra = [r[...] for r in refs[n_p + n_w:n_p + n_w + n_r + n_b]]
        base = n_p + n_w + n_r + n_b
        pro = [r[...] for r in refs[base:base + n_pr + n_pb]]
        base += n_pr + n_pb
        o_refs = refs[base:base + n_o]
        a_refs = refs[base + n_o:base + n_o + len(acc_outs)]
        if pro_fn is not None:
            lhs = pro_fn(*pro).astype(pro_out.dtype)
            refs[-1][...] = lhs
            tiles = [lhs]
        else:
            tiles = [r[...] for r in p_refs]
        acc = None
        k = 0
        for gi, grp in enumerate(pieces):
            c0 = 0
            for p in grp:
                n = p.shape[1]
                if w_transposed:
                    t = lax.dot_general(tiles[k], w_refs[gi][...], (((1,), (1,)), ((), ())), preferred_element_type=F32)
                else:
                    t = jnp.dot(tiles[k], w_refs[gi][c0:c0 + n, :], preferred_element_type=F32)
                acc = t if acc is None else acc + t
                c0 += n
                k += 1
        if fn is None:
            res = (acc + extra[0] if residual is not None else acc,)
        else:
            res = fn(acc, *extra)
        for o_ref, val in zip(o_refs, res[:n_o]):
            o_ref[...] = val.astype(o_ref.dtype)
        if acc_outs:
            @pl.when(pl.program_id(0) == 0)
            def _():
                for a_ref in a_refs:
                    a_ref[...] = jnp.zeros_like(a_ref)
            for a_ref, val in zip(a_refs, res[n_o:]):
                a_ref[...] += val

    in_specs = [pl.BlockSpec((tm, p.shape[1]), lambda i: (i, 0)) for p in flat]
    in_specs += [_wspec(w) for w in ws]
    in_specs += [pl.BlockSpec((tm, r.shape[1]), lambda i: (i, 0)) for r in row_ins]
    in_specs += [pl.BlockSpec(b.shape, lambda i: (0, 0)) for b in bc_ins]
    in_specs += [pl.BlockSpec((tm, r.shape[1]), lambda i: (i, 0)) for r in pro_rows]
    in_specs += [pl.BlockSpec(b.shape, lambda i: (0, 0)) for b in pro_bc]
    out_specs = [pl.BlockSpec((tm, s.shape[1]), lambda i: (i, 0)) for s in row_outs]
    out_specs += [pl.BlockSpec(s.shape, lambda i: (0, 0)) for s in acc_outs]
    pro_outs = [] if prologue is None else [pro_out]
    out_specs += [pl.BlockSpec((tm, s.shape[1]), lambda i: (i, 0)) for s in pro_outs]
    outs = pl.pallas_call(
        body, name=name, grid=(M // tm,), in_specs=[pl.BlockSpec(memory_space=pl.ANY)] * len(lead) + in_specs,
        out_specs=out_specs, out_shape=list(row_outs) + list(acc_outs) + pro_outs,
        compiler_params=_cp(("arbitrary",) if acc_outs else ("parallel",)),
    )(*lead, *flat, *[w[0] for w in ws], *row_ins, *bc_ins, *pro_rows, *pro_bc)
    return outs if (epilogue is not None or prologue is not None) else outs[0]


def _mm_tn(pieces, x, *, name, out_dtype=BF16, tt=1024):
    M, K = x.shape
    tt = min(tt, M)
    ns = [p.shape[1] for p in pieces]
    offs = [sum(ns[:i]) for i in range(len(ns))]
    N = sum(ns)
    n_p = len(pieces)
    last = M // tt - 1

    def body(*refs):
        p_refs = refs[:n_p]
        x_ref = refs[n_p]
        o_ref, acc_ref = refs[n_p + 1], refs[n_p + 2]

        @pl.when(pl.program_id(0) == 0)
        def _():
            acc_ref[...] = jnp.zeros_like(acc_ref)

        xv = x_ref[...]
        for p_ref, c0, n in zip(p_refs, offs, ns):
            acc_ref[c0:c0 + n, :] += lax.dot_general(p_ref[...], xv, (((0,), (0,)), ((), ())),
                                                      preferred_element_type=F32)

        @pl.when(pl.program_id(0) == last)
        def _():
            o_ref[...] = acc_ref[...].astype(o_ref.dtype)

    in_specs = [pl.BlockSpec((tt, n), lambda i: (i, 0)) for n in ns]
    in_specs.append(pl.BlockSpec((tt, K), lambda i: (i, 0)))
    return pl.pallas_call(
        body, name=name, grid=(M // tt,), in_specs=in_specs,
        out_specs=pl.BlockSpec((N, K), lambda i: (0, 0)),
        out_shape=_sds((N, K), out_dtype),
        scratch_shapes=[pltpu.VMEM((N, K), F32)],
        compiler_params=_cp(("arbitrary",)),
    )(*pieces, x)


def _rms_fwd(xf, w):
    inv = lax.rsqrt(jnp.mean(xf * xf, axis=-1, keepdims=True) + EPS)
    return xf * inv * w


def _rms_bwd(xf, w, dy):
    inv = lax.rsqrt(jnp.mean(xf * xf, axis=-1, keepdims=True) + EPS)
    xhat = xf * inv
    dxhat = dy * w
    dx = inv * (dxhat - xhat * jnp.mean(dxhat * xhat, axis=-1, keepdims=True))
    dw = jnp.sum(dy * xhat, axis=0, keepdims=True)
    return dx, dw


def _sigmoid(x):
    return 1.0 / (1.0 + jnp.exp(-x))


def _rowwise(fn, row_ins, bc_ins, row_outs, acc_outs, *, name, tm=256, after=None):
    M = row_outs[0].shape[0] if row_outs else row_ins[0][0].shape[0]
    assert M % tm == 0 and tm % SUBLANES == 0, (name, M, tm)
    n_r, n_b, n_o, n_a = len(row_ins), len(bc_ins), len(row_outs), len(acc_outs)
    n_after = 0 if after is None else 1

    def body(*refs):
        refs = refs[n_after:]
        ins = [r[...] for r in refs[:n_r + n_b]]
        o_refs = refs[n_r + n_b:n_r + n_b + n_o]
        a_refs = refs[n_r + n_b + n_o:]
        res = fn(*ins)
        for o_ref, val in zip(o_refs, res[:n_o]):
            o_ref[...] = val.astype(o_ref.dtype)
        if n_a:
            @pl.when(pl.program_id(0) == 0)
            def _():
                for a_ref in a_refs:
                    a_ref[...] = jnp.zeros_like(a_ref)
            for a_ref, val in zip(a_refs, res[n_o:]):
                a_ref[...] += val

    in_specs = [pl.BlockSpec((tm, cw), functools.partial(lambda i, cb, r0: (i + r0, cb), cb=cb, r0=r0))
                for (_, cw, cb, r0) in row_ins]
    in_specs += [pl.BlockSpec(b.shape, lambda i: (0, 0)) for b in bc_ins]
    out_specs = [pl.BlockSpec((tm, s.shape[1]), lambda i: (i, 0)) for s in row_outs]
    out_specs += [pl.BlockSpec(s.shape, lambda i: (0, 0)) for s in acc_outs]
    if n_after:
        in_specs = [pl.BlockSpec(memory_space=pl.ANY)] + in_specs
    return pl.pallas_call(
        body, name=name, grid=(M // tm,), in_specs=in_specs, out_specs=out_specs,
        out_shape=list(row_outs) + list(acc_outs),
        compiler_params=_cp(("arbitrary",) if n_a else ("parallel",)),
    )(*([after] if n_after else []), *[r[0] for r in row_ins], *bc_ins)


def _full(a, first_row_block=0):
    return (a, a.shape[1], 0, first_row_block)


def _conv_rows(ext, w_ref_val, lo):
    s1 = pltpu.roll(ext, 1, 0)
    s2 = pltpu.roll(ext, 2, 0)
    y = w_ref_val[0:1, :] * s2 + w_ref_val[1:2, :] * s1 + w_ref_val[2:3, :] * ext
    return y[SUBLANES:, :]


def _convact_fwd(gp, up, conv_w8, conv_b, *, name, tr=512, tc=1408):
    T, C = gp.shape
    tr = min(tr, T)
    hb = tr // SUBLANES

    def body(gp_ref, gph_ref, up_ref, w_ref, b_ref, act_ref):
        i = pl.program_id(1)
        halo = jnp.where(i > 0, gph_ref[...], 0.0)
        ext = jnp.concatenate([halo, gp_ref[...]], axis=0)
        gate = _conv_rows(ext, w_ref[...], 0) + b_ref[...]
        act_ref[...] = (gate * _sigmoid(gate) * up_ref[...]).astype(act_ref.dtype)

    return pl.pallas_call(
        body, name=name, grid=(C // tc, T // tr),
        in_specs=[pl.BlockSpec((tr, tc), lambda j, i: (i, j)),
                  pl.BlockSpec((SUBLANES, tc), lambda j, i: (jnp.maximum(i * hb - 1, 0), j)),
                  pl.BlockSpec((tr, tc), lambda j, i: (i, j)),
                  pl.BlockSpec((SUBLANES, tc), lambda j, i: (0, j)),
                  pl.BlockSpec((1, tc), lambda j, i: (0, j))],
        out_specs=pl.BlockSpec((tr, tc), lambda j, i: (i, j)),
        out_shape=_sds((T, C), BF16),
        compiler_params=_cp(("parallel", "parallel")),
    )(gp, gp, up, conv_w8, conv_b)


def _convact_bwd(gp, up, dact, conv_w8, conv_b, *, name, tr=256, tc=1408):
    T, C = gp.shape
    tr = min(tr, T)
    hb = tr // SUBLANES
    nr = T // tr

    def body(gp_ref, gpp_ref, gpn_ref, up_ref, upn_ref, da_ref, dan_ref, w_ref, b_ref,
             dgp_ref, dup_ref, dw_ref, db_ref):
        i = pl.program_id(1)
        w = w_ref[...]
        prev = jnp.where(i > 0, gpp_ref[...], 0.0)
        last = i == nr - 1
        gp_ext = jnp.concatenate([prev, gp_ref[...], gpn_ref[...]], axis=0)
        gate = _conv_rows(gp_ext, w, 0) + b_ref[...]
        up_e = jnp.concatenate([up_ref[...], upn_ref[...]], axis=0)
        da_e = jnp.concatenate([da_ref[...], dan_ref[...]], axis=0)
        row = lax.broadcasted_iota(jnp.int32, gate.shape, 0)
        valid = jnp.logical_or(row < tr, jnp.logical_not(last))
        sg = _sigmoid(gate)
        silu = gate * sg
        dgate = jnp.where(valid, da_e * up_e * (sg * (1.0 + gate * (1.0 - sg))), 0.0)
        dup_ref[...] = (da_e[:tr] * silu[:tr]).astype(dup_ref.dtype)
        n = tr + SUBLANES
        g1 = pltpu.roll(dgate, n - 1, 0)
        g2 = pltpu.roll(dgate, n - 2, 0)
        dgp = w[2:3, :] * dgate + w[1:2, :] * g1 + w[0:1, :] * g2
        dgp_ref[...] = dgp[:tr].astype(dgp_ref.dtype)
        gpc = gp_ref[...]
        dw0 = jnp.sum(gpc * g2[:tr], axis=0, keepdims=True)
        dw1 = jnp.sum(gpc * g1[:tr], axis=0, keepdims=True)
        dw2 = jnp.sum(gpc * dgate[:tr], axis=0, keepdims=True)
        dbv = jnp.sum(dgate[:tr], axis=0, keepdims=True)
        z = jnp.zeros((SUBLANES - 3, gpc.shape[1]), F32)

        @pl.when(i == 0)
        def _():
            dw_ref[...] = jnp.zeros_like(dw_ref)
            db_ref[...] = jnp.zeros_like(db_ref)

        dw_ref[...] += jnp.concatenate([dw0, dw1, dw2, z], axis=0)
        db_ref[...] += dbv

    cur = pl.BlockSpec((tr, tc), lambda j, i: (i, j))
    prv = pl.BlockSpec((SUBLANES, tc), lambda j, i: (jnp.maximum(i * hb - 1, 0), j))
    nxt = pl.BlockSpec((SUBLANES, tc), lambda j, i: (jnp.minimum((i + 1) * hb, T // SUBLANES - 1), j))
    return pl.pallas_call(
        body, name=name, grid=(C // tc, nr),
        in_specs=[cur, prv, nxt, cur, nxt, cur, nxt,
                  pl.BlockSpec((SUBLANES, tc), lambda j, i: (0, j)),
                  pl.BlockSpec((1, tc), lambda j, i: (0, j))],
        out_specs=[cur, cur,
                   pl.BlockSpec((SUBLANES, tc), lambda j, i: (0, j)),
                   pl.BlockSpec((1, tc), lambda j, i: (0, j))],
        out_shape=[_sds((T, C), BF16), _sds((T, C), BF16), _sds((SUBLANES, C), F32), _sds((1, C), F32)],
        compiler_params=_cp(("parallel", "arbitrary")),
    )(gp, gp, gp, up, up, dact, dact, conv_w8, conv_b)


def _cumsum_rows(x):
    n = x.shape[0]
    row = lax.broadcasted_iota(jnp.int32, x.shape, 0)
    s = 1
    while s < n:
        x = x + jnp.where(row >= s, pltpu.roll(x, s, 0), 0.0)
        s *= 2
    return x


def _rcumsum_rows(x):
    n = x.shape[0]
    row = lax.broadcasted_iota(jnp.int32, x.shape, 0)
    s = 1
    while s < n:
        x = x + jnp.where(row < n - s, pltpu.roll(x, n - s, 0), 0.0)
        s *= 2
    return x


def _dot_nt(a, b):
    return lax.dot_general(a.astype(BF16), b.astype(BF16), (((1,), (1,)), ((), ())), preferred_element_type=F32)


def _dot_tn(a, b):
    return lax.dot_general(a.astype(BF16), b.astype(BF16), (((0,), (0,)), ((), ())), preferred_element_type=F32)


def _dot_nn(a, b):
    return jnp.dot(a.astype(BF16), b.astype(BF16), preferred_element_type=F32)


def _dot3(a, b, contract):
    def split(x):
        hi = x.astype(BF16)
        return hi, (x - hi.astype(F32)).astype(BF16)

    a_hi, a_lo = split(a)
    b_hi, b_lo = split(b)
    dot = lambda x, y: lax.dot_general(x, y, (contract, ((), ())), preferred_element_type=F32)
    return dot(a_hi, b_hi) + (dot(a_hi, b_lo) + dot(a_lo, b_hi))


NT, TN, NN = ((1,), (1,)), ((0,), (0,)), ((1,), (0,))


def _hg_gates(hq, hf, lbv):
    sig = _sigmoid(hf)
    f = lbv + (1.0 - lbv) * sig
    return sig, f, jnp.log(f), 1.0 - f, hq * (HG_DK ** -0.5)


def _hg_sel_rows(ref, sp):
    return jnp.concatenate(
        [jnp.broadcast_to(ref[pl.ds(HG_SUB * i + sp, 1), :], (HG_SUB, HG_DK)) for i in range(HG_CHUNK // HG_SUB)], axis=0)


def _hg_masks():
    C = HG_CHUNK
    row = lax.broadcasted_iota(jnp.int32, (C, C), 0)
    col = lax.broadcasted_iota(jnp.int32, (C, C), 1)
    d = col - (row // HG_SUB) * HG_SUB
    tmod = row % HG_SUB
    diag_valid = jnp.logical_and(d >= 0, d <= tmod)
    return row, col, d, diag_valid


def _hg_scores(q, k, b, b_sc, k_sc):
    C, S = HG_CHUNK, HG_SUB
    row, col, d, diag_valid = _hg_masks()
    blocks = [jnp.zeros((S, C), F32)]
    for i in range(1, C // S):
        r = b_sc[pl.ds(S * i - 1, 1), :]
        qi = q[S * i:S * (i + 1)] * jnp.exp(b[S * i:S * (i + 1)] - r)
        kk = k * jnp.exp(jnp.minimum(r - b, 0.0))
        blocks.append(_dot_nt(qi, kk))
    a_off = jnp.where(col < (row // S) * S, jnp.concatenate(blocks, axis=0), 0.0)
    a_d = jnp.zeros((C, C), F32)
    for sp in range(S):
        bs = _hg_sel_rows(b_sc, sp)
        ks = _hg_sel_rows(k_sc, sp)
        e = jnp.exp(jnp.minimum(b - bs, 0.0))
        colv = jnp.sum(q * ks * e, axis=-1, keepdims=True)
        a_d = jnp.where(d == sp, colv, a_d)
    return a_off + jnp.where(diag_valid, a_d, 0.0)


def _hg_prep(hq_v, hf_v, lbv, b_sc, k_sc):
    sig, f, g, k, q = _hg_gates(hq_v, hf_v, lbv)
    b = _cumsum_rows(g)
    b_sc[...] = b
    k_sc[...] = k
    return sig, f, k, q, b, b_sc[pl.ds(HG_CHUNK - 1, 1), :]


def _hgrn_fwd(hq, hf, hi, lb, *, name):
    T = hq.shape[0]
    C, H, K = HG_CHUNK, HG_HEADS, HG_DK
    NC = T // C

    def body(hq_ref, hf_ref, hi_ref, lb_ref, o_ref, st_ref, s_sc, b_sc, k_sc):
        @pl.when(pl.program_id(0) == 0)
        def _():
            s_sc[...] = jnp.zeros_like(s_sc)

        st_all = s_sc[...]
        for j in range(P):
            rows = slice(C * j, C * (j + 1))
            st_ref[j] = st_all
            outs, news = [], []
            for h in range(H):
                sl = slice(K * h, K * (h + 1))
                _, _, k, q, b, bc = _hg_prep(hq_ref[rows, sl], hf_ref[rows, sl], lb_ref[:, sl], b_sc.at[j, h], k_sc.at[j, h])
                v = hi_ref[rows, sl]
                st0 = st_all[:, sl]
                a = _hg_scores(q, k, b, b_sc.at[j, h], k_sc.at[j, h])
                outs.append(_dot_nn(a, v) + _dot_nt(q * jnp.exp(b), st0))
                news.append(st0 * jnp.exp(bc) + _dot_tn(v, k * jnp.exp(bc - b)))
            o_ref[rows, :] = jnp.concatenate(outs, axis=1)
            st_all = jnp.concatenate(news, axis=1)
        s_sc[...] = st_all

    P = HG_FWD_CHUNKS_PER_STEP
    blk = pl.BlockSpec((P * C, H * K), lambda c: (c, 0))
    return pl.pallas_call(
        body, name=name, grid=(NC // P,),
        in_specs=[blk, blk, blk, pl.BlockSpec((1, H * K), lambda c: (0, 0))],
        out_specs=[blk, pl.BlockSpec((P, K, H * K), lambda c: (c, 0, 0))],
        out_shape=[_sds((T, H * K), F32), _sds((NC, K, H * K), F32)],
        scratch_shapes=[pltpu.VMEM((K, H * K), F32), pltpu.VMEM((P, H, C, K), F32), pltpu.VMEM((P, H, C, K), F32)],
        compiler_params=_cp(("arbitrary",)),
    )(hq, hf, hi, lb)


def _hgrn_bwd(hq, hf, hi, lb, states, do, *, name):
    T = hq.shape[0]
    C, H, K, S = HG_CHUNK, HG_HEADS, HG_DK, HG_SUB
    NC = T // C

    def intra_slow(q, k, b, da, b_sc, k_sc):
        row, col, d, diag_valid = _hg_masks()
        a_blocks = [jnp.zeros((S, C), F32)]
        dq_blocks = [jnp.zeros((S, K), F32)]
        dk = jnp.zeros((C, K), F32)
        for i in range(1, C // S):
            r = b_sc[pl.ds(S * i - 1, 1), :]
            eq = jnp.exp(b[S * i:S * (i + 1)] - r)
            ek = jnp.exp(jnp.minimum(r - b, 0.0))
            qi = q[S * i:S * (i + 1)] * eq
            kk = k * ek
            a_blocks.append(_dot_nt(qi, kk))
            dai = jnp.where(col[S * i:S * (i + 1)] < S * i, da[S * i:S * (i + 1)], 0.0)
            dq_blocks.append(_dot_nn(dai, kk) * eq)
            dk = dk + _dot_tn(dai, qi) * ek
        dq = jnp.concatenate(dq_blocks, axis=0)
        a_off = jnp.where(col < (row // S) * S, jnp.concatenate(a_blocks, axis=0), 0.0)
        same_blk = (row // S == col // S).astype(BF16)
        tmod = (lax.broadcasted_iota(jnp.int32, (C, K), 0)) % S
        a_d = jnp.zeros((C, C), F32)
        for sp in range(S):
            bs = _hg_sel_rows(b_sc, sp)
            ks = _hg_sel_rows(k_sc, sp)
            e = jnp.where(tmod >= sp, jnp.exp(jnp.minimum(b - bs, 0.0)), 0.0)
            eks = e * ks
            a_d = jnp.where(d == sp, jnp.sum(q * eks, axis=-1, keepdims=True), a_d)
            dacol = jnp.sum(jnp.where(d == sp, da, 0.0), axis=-1, keepdims=True)
            dq = dq + dacol * eks
            wq = dacol * e * q
            wq_hi = wq.astype(BF16)
            wq_lo = (wq - wq_hi.astype(F32)).astype(BF16)
            blk_sum = (jnp.dot(same_blk, wq_hi, preferred_element_type=F32)
                       + jnp.dot(same_blk, wq_lo, preferred_element_type=F32))
            dk = dk + jnp.where(tmod == sp, blk_sum, 0.0)
        return a_off + jnp.where(diag_valid, a_d, 0.0), dq, dk

    def one_head(pre, v, lbv, st0, dst1, dout, b_sc, k_sc):
        sig, f, k, q, b, bc = pre
        ebc = jnp.exp(bc)
        eb = jnp.exp(b)
        ekb = jnp.exp(bc - b)
        qt = q * eb
        kb = k * ekb
        row = lax.broadcasted_iota(jnp.int32, (C, C), 0)
        col = lax.broadcasted_iota(jnp.int32, (C, C), 1)
        da = jnp.where(col <= row, _dot_nt(dout, v), 0.0)
        dkb = _dot_nn(v, dst1)
        new_ds = _dot_tn(dout, qt) + dst1 * ebc
        a, dq_i, dk_i = intra_slow(q, k, b, da, b_sc, k_sc)
        dq = _dot_nn(dout, st0) * eb + dq_i
        dk = dkb * ekb + dk_i
        dv = _dot_tn(a, dout) + _dot_nt(kb, dst1)
        extra = jnp.sum(dkb * kb, axis=0, keepdims=True) + ebc * jnp.sum(st0 * dst1, axis=0, keepdims=True)
        rowk = lax.broadcasted_iota(jnp.int32, (C, K), 0)
        db = q * dq - k * dk + jnp.where(rowk == C - 1, extra, 0.0)
        dg = _rcumsum_rows(db)
        df = dg / f - dk
        return (dq * (K ** -0.5), df * (1.0 - lbv) * sig * (1.0 - sig), dv,
                jnp.sum(df * (1.0 - sig), axis=0, keepdims=True), new_ds)

    def body(hq_ref, hf_ref, hi_ref, lb_ref, st_ref, do_ref, dq_ref, dhf_ref, dv_ref, dlb_ref, ds_sc, b_sc, k_sc):
        @pl.when(pl.program_id(0) == 0)
        def _():
            ds_sc[...] = jnp.zeros_like(ds_sc)
            dlb_ref[...] = jnp.zeros_like(dlb_ref)

        ds_all = ds_sc[...]
        dlb = jnp.zeros((1, H * K), F32)
        for j in reversed(range(P)):
            rows = slice(C * j, C * (j + 1))
            st_all = st_ref[j]
            res = []
            for h in range(H):
                sl = slice(K * h, K * (h + 1))
                pre = _hg_prep(hq_ref[rows, sl], hf_ref[rows, sl], lb_ref[:, sl], b_sc.at[j, h], k_sc.at[j, h])
                res.append(one_head(pre, hi_ref[rows, sl], lb_ref[:, sl], st_all[:, sl], ds_all[:, sl], do_ref[rows, sl],
                                    b_sc.at[j, h], k_sc.at[j, h]))
            cat = lambda i: jnp.concatenate([r[i] for r in res], axis=1)
            dq_ref[rows, :] = cat(0).astype(dq_ref.dtype)
            dhf_ref[rows, :] = cat(1).astype(dhf_ref.dtype)
            dv_ref[rows, :] = cat(2).astype(dv_ref.dtype)
            dlb = dlb + cat(3)
            ds_all = cat(4)
        dlb_ref[...] += dlb
        ds_sc[...] = ds_all

    P = HG_CHUNKS_PER_STEP
    NS = NC // P
    blk = pl.BlockSpec((P * C, H * K), lambda c: (NS - 1 - c, 0))
    par = pl.BlockSpec((1, H * K), lambda c: (0, 0))
    return pl.pallas_call(
        body, name=name, grid=(NS,),
        in_specs=[blk, blk, blk, par, pl.BlockSpec((P, K, H * K), lambda c: (NS - 1 - c, 0, 0)), blk],
        out_specs=[blk, blk, blk, par],
        out_shape=[_sds((T, H * K), BF16)] * 3 + [_sds((1, H * K), F32)],
        scratch_shapes=[pltpu.VMEM((K, H * K), F32), pltpu.VMEM((P, H, C, K), F32), pltpu.VMEM((P, H, C, K), F32)],
        compiler_params=_cp(("arbitrary",)),
    )(hq, hf, hi, lb, states, do)


ATT_STACK = ATT_GROUP


def _att_valid(n):
    R, B = ATT_STACK * ATT_BLOCK, ATT_BLOCK
    j = lax.broadcasted_iota(jnp.int32, (2 * B, R), 0)
    t = lax.broadcasted_iota(jnp.int32, (2 * B, R), 1) % B
    dist = t + B - j
    first_key = jnp.where(n > 0, 0, B)
    return jnp.logical_and(jnp.logical_and(dist >= 0, dist < B), j >= first_key)


def _att_load(cur_ref, prev_ref, ba_ref, h0):
    hd = ATT_HD
    kv = h0 // ATT_GROUP
    def cols(ref, c0):
        return ref[:, c0:c0 + hd] + ba_ref[:, c0:c0 + hd]
    qs = jnp.concatenate([cols(cur_ref, hd * (h0 + g)) for g in range(ATT_STACK)], axis=0)
    kc = jnp.concatenate([cols(prev_ref, ATT_Q_W + hd * kv), cols(cur_ref, ATT_Q_W + hd * kv)], axis=0)
    vc = jnp.concatenate([cols(prev_ref, ATT_Q_W + ATT_KV_W + hd * kv), cols(cur_ref, ATT_Q_W + ATT_KV_W + hd * kv)], axis=0)
    return qs, kc, vc


def _att_probs(qs, kc, valid, sink_ref, h0):
    scale = 1.0 / math.sqrt(ATT_HD)
    s = jnp.where(valid, _dot_nt(kc, qs) * scale, NEG)
    sink = jnp.concatenate([jnp.full((1, ATT_BLOCK), sink_ref[0, h0 + g], F32) for g in range(ATT_STACK)], axis=1)
    m = jnp.maximum(jnp.max(s, axis=0, keepdims=True), sink)
    p = jnp.exp(s - m)
    ps = jnp.exp(sink - m)
    inv = 1.0 / (jnp.sum(p, axis=0, keepdims=True) + ps)
    return p * inv, ps * inv


def _attn_fwd(att, b_attn, sinks, *, name, after=None):
    T = att.shape[0]
    B = ATT_BLOCK
    NB = T // B
    lead = [] if after is None else [after]

    def body(*refs):
        sink_ref, cur_ref, prev_ref, ba_ref, o_ref = refs[len(lead):]
        valid = _att_valid(pl.program_id(0))
        outs = []
        for h0 in range(0, ATT_HEADS, ATT_STACK):
            qs, kc, vc = _att_load(cur_ref, prev_ref, ba_ref, h0)
            prob, _ = _att_probs(qs, kc, valid, sink_ref, h0)
            o = _dot_tn(prob, vc)
            outs += [o[B * g:B * (g + 1)] for g in range(ATT_STACK)]
        o_ref[...] = jnp.concatenate(outs, axis=1)

    return pl.pallas_call(
        body, name=name, grid=(NB,),
        in_specs=[pl.BlockSpec(memory_space=pl.ANY)] * len(lead) + [
            pl.BlockSpec(memory_space=pltpu.SMEM),
            pl.BlockSpec((B, ATT_COLS), lambda n: (n, 0)),
            pl.BlockSpec((B, ATT_COLS), lambda n: (jnp.maximum(n - 1, 0), 0)),
            pl.BlockSpec((1, ATT_COLS), lambda n: (0, 0))],
        out_specs=pl.BlockSpec((B, ATT_Q_W), lambda n: (n, 0)),
        out_shape=_sds((T, ATT_Q_W), F32),
        compiler_params=_cp(("parallel",)),
    )(*lead, sinks, att, att, b_attn)


def _attn_bwd(att, b_attn, sinks, dmix, *, name):
    T = att.shape[0]
    B, hd = ATT_BLOCK, ATT_HD
    NB = T // B
    scale = 1.0 / math.sqrt(hd)

    def body(sink_ref, cur_ref, prev_ref, ba_ref, do_ref, daq_ref, dakv_ref, dsink_ref, dbq_ref, dbkv_ref, carry_sc):
        n = pl.program_id(0)

        @pl.when(n == 0)
        def _():
            carry_sc[...] = jnp.zeros_like(carry_sc)
            dsink_ref[...] = jnp.zeros_like(dsink_ref)
            dbq_ref[...] = jnp.zeros_like(dbq_ref)
            dbkv_ref[...] = jnp.zeros_like(dbkv_ref)

        @pl.when(n < NB)
        def _():
            valid = _att_valid(n)
            hrow = lax.broadcasted_iota(jnp.int32, (SUBLANES, 128), 0)
            dsink = jnp.zeros((SUBLANES, 128), F32)
            dqs = []
            dks = [jnp.zeros((2 * B, hd), F32)] * ATT_KV
            dvs = [jnp.zeros((2 * B, hd), F32)] * ATT_KV
            for h0 in range(0, ATT_HEADS, ATT_STACK):
                kv = h0 // ATT_GROUP
                qs, kc, vc = _att_load(cur_ref, prev_ref, ba_ref, h0)
                prob, psink = _att_probs(qs, kc, valid, sink_ref, h0)
                dout = jnp.concatenate([do_ref[:, hd * (h0 + g):hd * (h0 + g + 1)] for g in range(ATT_STACK)], axis=0)
                dp = _dot_nt(vc, dout)
                delta = jnp.sum(prob * dp, axis=0, keepdims=True)
                dsc = prob * (dp - delta) * scale
                dq = _dot_tn(dsc, kc)
                dks[kv] = dks[kv] + _dot_nn(dsc, qs)
                dvs[kv] = dvs[kv] + _dot_nn(prob, dout)
                dsk = psink * delta
                for g in range(ATT_STACK):
                    dqs.append(dq[B * g:B * (g + 1)])
                    tot = jnp.sum(dsk[:, B * g:B * (g + 1)], axis=1, keepdims=True)
                    dsink = dsink - jnp.where(hrow == h0 + g, tot, 0.0)
            daq = jnp.concatenate(dqs, axis=1).astype(daq_ref.dtype)
            daq_ref[...] = daq
            dsink_ref[...] += dsink
            dbq_ref[...] += jnp.sum(daq.astype(F32), axis=0, keepdims=True)
            done = carry_sc[...] + jnp.concatenate([d[:B] for d in dks + dvs], axis=1)
            dakv_ref[...] = done.astype(dakv_ref.dtype)
            dbkv_ref[...] += jnp.sum(done.astype(dakv_ref.dtype).astype(F32), axis=0, keepdims=True)
            carry_sc[...] = jnp.concatenate([d[B:] for d in dks + dvs], axis=1)

        @pl.when(n == NB)
        def _():
            done = carry_sc[...]
            dakv_ref[...] = done.astype(dakv_ref.dtype)
            dbkv_ref[...] += jnp.sum(done.astype(dakv_ref.dtype).astype(F32), axis=0, keepdims=True)

    cl = lambda n: jnp.minimum(n, NB - 1)
    return pl.pallas_call(
        body, name=name, grid=(NB + 1,),
        in_specs=[pl.BlockSpec(memory_space=pltpu.SMEM),
                  pl.BlockSpec((B, ATT_COLS), lambda n: (cl(n), 0)),
                  pl.BlockSpec((B, ATT_COLS), lambda n: (jnp.maximum(cl(n) - 1, 0), 0)),
                  pl.BlockSpec((1, ATT_COLS), lambda n: (0, 0)),
                  pl.BlockSpec((B, ATT_Q_W), lambda n: (cl(n), 0))],
        out_specs=[pl.BlockSpec((B, ATT_Q_W), lambda n: (cl(n), 0)),
                   pl.BlockSpec((B, 2 * ATT_KV_W), lambda n: (jnp.maximum(n - 1, 0), 0)),
                   pl.BlockSpec((SUBLANES, 128), lambda n: (0, 0)),
                   pl.BlockSpec((1, ATT_Q_W), lambda n: (0, 0)),
                   pl.BlockSpec((1, 2 * ATT_KV_W), lambda n: (0, 0))],
        out_shape=[_sds((T, ATT_Q_W), BF16), _sds((T, 2 * ATT_KV_W), BF16), _sds((SUBLANES, 128), F32),
                   _sds((1, ATT_Q_W), F32), _sds((1, 2 * ATT_KV_W), F32)],
        scratch_shapes=[pltpu.VMEM((B, 2 * ATT_KV_W), F32)],
        compiler_params=_cp(("arbitrary",)),
    )(sinks, att, att, b_attn, dmix)


def _silu_and_grad(x):
    sg = _sigmoid(x)
    return x * sg, sg * (1.0 + x * (1.0 - sg))


def _mix_fwd_fn(o_raw, hg, o_att, hgw):
    outs = []
    for h in range(HG_HEADS):
        sl = slice(HG_DK * h, HG_DK * (h + 1))
        silu, _ = _silu_and_grad(hg[:, sl])
        outs.append(_rms_fwd(o_raw[:, sl], hgw) * silu)
    outs.append(o_att)
    return (jnp.concatenate(outs, axis=1),)


def _mix_bwd_fn(o_raw, hg, dmix, hgw):
    dos, dhgs = [], []
    dw = jnp.zeros((1, HG_DK), F32)
    for h in range(HG_HEADS):
        sl = slice(HG_DK * h, HG_DK * (h + 1))
        silu, dsilu = _silu_and_grad(hg[:, sl])
        dy = dmix[:, sl]
        dhgs.append(dy * _rms_fwd(o_raw[:, sl], hgw) * dsilu)
        dx, dwh = _rms_bwd(o_raw[:, sl], hgw, dy * silu)
        dos.append(dx)
        dw = dw + dwh
    return jnp.concatenate(dos, axis=1), jnp.concatenate(dhgs, axis=1), dw


def _final_fn(h2, tgt, wf):
    d = h2.shape[1]
    err = _rms_fwd(h2, wf) - tgt
    loss_cols = (0.5 / d) * jnp.sum(err * err, axis=0, keepdims=True)
    dh2, dwf = _rms_bwd(h2, wf, err * (1.0 / d))
    return dh2, dh2, loss_cols, dwf


class _NoExchange:
    def __init__(self, weights):
        self.weights = weights

    def start(self):
        return None

    def w_in(self, after):
        return self.weights["w_in_t"]

    def mid(self, after):
        return None

    def rest(self, after):
        return self.weights

    def ffn_grads(self, gs):
        return None

    def ffn_grads_send(self, after):
        return None


def _local_step(x, tgt, p, ex):
    T, D = x.shape
    row = lambda n, dt: _sds((T, n), dt)
    acc = lambda n: _sds((1, n), F32)

    (u,) = _rowwise(lambda xv, w: (_rms_fwd(xv, w),), [_full(x)], [p["norm_mix_w"]], [row(D, BF16)], [], name="rms_mix",
                    after=ex.start())
    p = dict(p, w_in_t=ex.w_in(u))
    hq, hf, hi, hg, att = _mm_nt(u, p["w_in_t"], splits=[HG_W] * 4 + [ATT_COLS], out_dtype=F32, name="in_proj")
    o_raw, states = _hgrn_fwd(hq, hf, hi, p["lb"], name="hgrn_fwd")
    o_att = _attn_fwd(att, p["b_attn"], p["sinks"], name="attn_fwd", after=ex.mid(o_raw))
    p = dict(p, **ex.rest(o_att))
    def out_epilogue(prod, xv, w):
        h1v = prod + xv
        return h1v, _rms_fwd(h1v, w)

    h1, v, mix = _mm_nn(None, [p["w_out"]], name="mix_out_proj",
                        prologue=(lambda *a: _mix_fwd_fn(*a)[0], [o_raw, hg, o_att], [p["hg_norm_w"]], row(D, BF16)),
                        epilogue=(out_epilogue, [x], [p["norm_ffn_w"]], [row(D, F32), row(D, BF16)], []))
    (gp,) = _mm_nt(v, p["w_gate_t"], splits=[D_FF], out_dtype=F32, name="gate_proj")
    (up,) = _mm_nt(v, p["w_up_t"], splits=[D_FF], out_dtype=F32, name="up_proj")
    act = _convact_fwd(gp, up, p["conv_w8"], p["conv_b"], name="convact_fwd")
    def down_epilogue(prod, h1v, tgtv, wf):
        return _final_fn(prod + h1v, tgtv, wf)

    dh2, dh2_b, loss_cols, d_final = _mm_nn(
        [[act]], [p["w_down"]], name="down_proj_loss",
        epilogue=(down_epilogue, [h1, tgt], [p["final_norm_w"]], [row(D, F32), row(D, BF16)], [acc(D), acc(D)]))

    (dact,) = _mm_nt(dh2_b, p["w_down"], splits=[D_FF], out_dtype=F32, name="d_act")
    g_down = _mm_tn([act], dh2_b, name="g_down")
    dgp, dup, d_conv_w8, d_conv_b = _convact_bwd(gp, up, dact, p["conv_w8"], p["conv_b"], name="convact_bwd")
    g_gate_t = _mm_tn([dgp], v, name="g_gate")
    g_up_t = _mm_tn([dup], v, name="g_up")
    swapping = ex.ffn_grads([g_gate_t, g_up_t, g_down])

    def ffn_norm_bwd(dvv, hv, dh2v, w):
        dx, dw = _rms_bwd(hv, w, dvv)
        dh1v = dx + dh2v
        return dh1v, dh1v, dw

    dh1, dh1_b, d_norm_ffn = _mm_nn(
        [[dgp], [dup]], [p["w_gate_t"], p["w_up_t"]], name="d_v_norm", after=swapping,
        epilogue=(ffn_norm_bwd, [h1, dh2], [p["norm_ffn_w"]], [row(D, F32), row(D, BF16)], [acc(D)]))
    sent = ex.ffn_grads_send(dh1_b)
    def mix_bwd(dmixv, o_rawv, hgv, hgw):
        do_rawv, dhgv, dw = _mix_bwd_fn(o_rawv, hgv, dmixv[:, :HG_W], hgw)
        return do_rawv, dhgv, dmixv[:, HG_W:], dw

    do_raw, dhg, do_att, d_hg_norm = _mm_nn(
        [[dh1_b]], [p["w_out"]], name="d_mix_bwd", w_transposed=True, after=sent,
        epilogue=(mix_bwd, [o_raw, hg], [p["hg_norm_w"]], [row(HG_W, F32), row(HG_W, BF16), row(ATT_Q_W, F32)], [acc(HG_DK)]))
    g_out = _mm_tn([mix], dh1_b, name="g_out")
    daq, dakv, d_sinks8, d_bq, d_bkv = _attn_bwd(att, p["b_attn"], p["sinks"], do_att, name="attn_bwd")
    dhq, dhf, dhi, d_lb = _hgrn_bwd(hq, hf, hi, p["lb"], states, do_raw, name="hgrn_bwd")
    pieces = [dhq, dhf, dhi, dhg, daq, dakv]
    g_in_t = _mm_tn(pieces, u, name="g_in")

    def mix_norm_bwd(duv, xv, dh1v, w):
        dx, dw = _rms_bwd(xv, w, duv)
        return dx + dh1v, dw

    dx, d_norm_mix = _mm_nn([pieces], [p["w_in_t"]], name="d_u_norm",
                            epilogue=(mix_norm_bwd, [x, dh1], [p["norm_mix_w"]], [row(D, F32)], [acc(D)]))
    grads = dict(g_in_t=g_in_t, g_out=g_out, g_gate_t=g_gate_t, g_up_t=g_up_t, g_down=g_down,
                 norm_mix_w=d_norm_mix, b_attn=jnp.concatenate([d_bq, d_bkv], axis=1), lb=d_lb, hg_norm_w=d_hg_norm,
                 sinks8=d_sinks8, norm_ffn_w=d_norm_ffn, conv_w8=d_conv_w8, conv_b=d_conv_b, final_norm_w=d_final)
    return loss_cols, dx, grads


SLAB = (IN_COLS // N_CHIPS, D_FF // N_CHIPS, D_FF // N_CHIPS, D_FF // N_CHIPS, D_MODEL // N_CHIPS)
N_W = len(SLAB)
PACK_OFF = tuple(sum(SLAB[:i]) for i in range(N_W))
PACK_ROWS = sum(SLAB)
FULL_OFF = tuple(N_CHIPS * o for o in PACK_OFF)
FULL_ROWS = N_CHIPS * PACK_ROWS
HALF = tuple(s // 2 for s in SLAB)
HPACK_OFF = tuple(sum(HALF[:i]) for i in range(N_W))
HPACK_ROWS = sum(HALF)
HFULL_OFF = tuple(N_CHIPS * o for o in HPACK_OFF)
HFULL_ROWS = N_CHIPS * HPACK_ROWS
CHIP_FLIPS = ((1, 0), (0, 1), (1, 1))
N_DEV = 8
BF16_ROWS = 16
ANY = pl.BlockSpec(memory_space=pl.ANY)


def _pos():
    return lax.axis_index("x"), lax.axis_index("y"), lax.axis_index("c")


def _flip(v, f):
    return 1 - v if f else v


def _rcopy(src, dst, ssem, rsem, dev):
    return pltpu.make_async_remote_copy(src_ref=src, dst_ref=dst, send_sem=ssem, recv_sem=rsem, device_id=dev,
                                        device_id_type=pl.DeviceIdType.MESH)


def _rows(ref, start, n, align=None):
    if not isinstance(start, int):
        if align is None:
            align = SUBLANES * (4 // jnp.dtype(ref.dtype).itemsize)
        start = pl.multiple_of(start, align)
    return ref.at[pl.ds(start, n), :]


FFN_W = (1, 2, 3)
N_PEER = 1 + len(CHIP_FLIPS)
HBM = pl.BlockSpec(memory_space=pltpu.HBM)
SEM = pl.BlockSpec(memory_space=pltpu.SEMAPHORE)
EFFECT = pltpu.SideEffectType.DATAFLOW_SIDE_EFFECTING
LANES = 128


def _sent_rows(k, w, c):
    return (0, SLAB[w]) if k == 0 else (c * HALF[w], HALF[w])


def _gather_start(pack, cw8):
    D = pack.shape[1]
    lands = [lax.empty((N_CHIPS * SLAB[0], D), pack.dtype), lax.empty((3 * N_CHIPS * SLAB[1], D), pack.dtype),
             lax.empty((N_CHIPS * SLAB[4], D), pack.dtype), lax.empty((N_CHIPS,) + cw8.shape, cw8.dtype)]
    bufs = [pack, cw8] + lands

    def body(pack_ref, cw_ref, l_in, l_ffn, l_out, l_cw, *rest):
        in_send, in_recv, out_send, out_recv, ffn_send, ffn_recv = rest[:6]
        token = rest[-1]
        x, y, c = _pos()
        q = 2 * x + y
        peers = _gather_peers(x, y, c)

        def send(k, peer, w, land, base, ssem, rsem):
            r0, n = _sent_rows(k, w, c)
            _rcopy(_rows(pack_ref, PACK_OFF[w] + r0, n), _rows(land, base + q * SLAB[w] + r0, n), ssem, rsem, peer).start()

        for k, peer in enumerate(peers):
            send(k, peer, 0, l_in, 0, in_send.at[k], in_recv.at[k])
        for k, peer in enumerate(peers):
            send(k, peer, 4, l_out, 0, out_send.at[k], out_recv.at[k])
            _rcopy(cw_ref, l_cw.at[q], out_send.at[N_PEER + k], out_recv.at[N_PEER + k], peer).start()
        for j, w in enumerate(FFN_W):
            for k, peer in enumerate(peers):
                send(k, peer, w, l_ffn, j * N_CHIPS * SLAB[w], ffn_send.at[k], ffn_recv.at[k])
        token[...] = jnp.zeros_like(token)

    n_sem = (N_PEER, N_PEER, 2 * N_PEER, 2 * N_PEER, N_PEER, N_PEER)
    outs = pl.pallas_call(
        body, name="gather_start", in_specs=[HBM] * len(bufs),
        out_specs=[SEM] * len(n_sem) + [HBM] * len(bufs) + [pl.BlockSpec(memory_space=pltpu.VMEM)],
        out_shape=[pltpu.SemaphoreType.DMA((n,)) for n in n_sem]
        + [pltpu.HBM(b.shape, b.dtype) for b in bufs] + [TOKEN],
        input_output_aliases={i: len(n_sem) + i for i in range(len(bufs))},
        compiler_params=pltpu.CompilerParams(has_side_effects=EFFECT),
    )(*[pltpu.with_memory_space_constraint(b, pltpu.HBM) for b in bufs])
    bufs_out = outs[len(n_sem):]
    return dict(in_sems=outs[0:2], out_sems=outs[2:4], ffn_sems=outs[4:6], pack=bufs_out[0], cw=bufs_out[1], l_in=bufs_out[2],
                l_ffn=bufs_out[3], l_out=bufs_out[4], l_cw=bufs_out[5], token=bufs_out[6])


def _gather_peers(x, y, c):
    return [(x, y, 1 - c)] + [(_flip(x, fx), _flip(y, fy), c) for fx, fy in CHIP_FLIPS]


def _gather_wait_in(g, after):
    def body(pack_ref, l_in, send, recv, after_ref, pack_out, l_out):
        for k, peer in enumerate(_gather_peers(*_pos())):
            n = _sent_rows(k, 0, 0)[1]
            cp = _rcopy(_rows(pack_ref, PACK_OFF[0], n), _rows(l_in, 0, n), send.at[k], recv.at[k], peer)
            cp.wait_send()
            cp.wait_recv()

    return pl.pallas_call(
        body, name="gather_wait_in", in_specs=[HBM, HBM, SEM, SEM, ANY], out_specs=[HBM, HBM],
        out_shape=[pltpu.HBM(g["pack"].shape, g["pack"].dtype), pltpu.HBM(g["l_in"].shape, g["l_in"].dtype)],
        input_output_aliases={0: 0, 1: 1}, compiler_params=pltpu.CompilerParams(has_side_effects=EFFECT),
    )(g["pack"], g["l_in"], *g["in_sems"], after)


def _gather_wait_rest(g, pack, after):
    def body(pack_ref, cw_ref, l_ffn, l_out, l_cw, o_send, o_recv, f_send, f_recv, after_ref, o_ffn, o_out, o_cw):
        for k, peer in enumerate(_gather_peers(*_pos())):
            n_out = _sent_rows(k, 4, 0)[1]
            n_ffn = len(FFN_W) * _sent_rows(k, FFN_W[0], 0)[1]
            for cp in (_rcopy(_rows(pack_ref, PACK_OFF[4], n_out), _rows(l_out, 0, n_out), o_send.at[k], o_recv.at[k], peer),
                       _rcopy(cw_ref, l_cw.at[0], o_send.at[N_PEER + k], o_recv.at[N_PEER + k], peer),
                       _rcopy(_rows(pack_ref, PACK_OFF[FFN_W[0]], n_ffn), _rows(l_ffn, 0, n_ffn), f_send.at[k], f_recv.at[k], peer)):
                cp.wait_send()
                cp.wait_recv()

    ins = [pack, g["cw"], g["l_ffn"], g["l_out"], g["l_cw"]]
    return pl.pallas_call(
        body, name="gather_wait_rest", in_specs=[HBM] * 5 + [SEM] * 4 + [ANY], out_specs=[HBM] * 3,
        out_shape=[pltpu.HBM(b.shape, b.dtype) for b in ins[2:]],
        input_output_aliases={2: 0, 3: 1, 4: 2}, compiler_params=pltpu.CompilerParams(has_side_effects=EFFECT),
    )(*ins, *g["out_sems"], *g["ffn_sems"], after)


FWD_IN = ((0, 0, 0),)
FWD_REST = tuple((0, w, j * N_CHIPS * SLAB[w]) for j, w in enumerate(FFN_W)) + ((1, 4, 0),)


def _forward_copies(layout, src, dst, send_sems, recv_sems):
    x, y, c = _pos()
    sib = (x, y, 1 - c)
    cps = []
    for fx, fy in CHIP_FLIPS:
        qa = 2 * _flip(x, fx) + _flip(y, fy)
        for bi, w, base in layout:
            r0 = base + qa * SLAB[w] + c * HALF[w]
            cps.append(_rcopy(_rows(src[bi], r0, HALF[w]), _rows(dst[bi], r0, HALF[w]),
                              send_sems.at[len(cps)], recv_sems.at[len(cps)], sib))
    return cps


def _forward_in(l_in):
    n = len(CHIP_FLIPS) * len(FWD_IN)

    def body(in_ref, out_ref, send_sems, recv_sems):
        cps = _forward_copies(FWD_IN, [in_ref], [out_ref], send_sems, recv_sems)
        for cp in cps:
            cp.start()
        for cp in cps:
            cp.wait_recv()
        for cp in cps:
            cp.wait_send()

    return pl.pallas_call(
        body, name="forward_in", in_specs=[ANY], out_specs=ANY, out_shape=_sds(l_in.shape, l_in.dtype),
        input_output_aliases={0: 0},
        scratch_shapes=[pltpu.SemaphoreType.DMA((n,)), pltpu.SemaphoreType.DMA((n,))],
    )(l_in)


def _forward_rest_start(l_ffn, l_out):
    n = len(CHIP_FLIPS) * len(FWD_REST)
    bufs = [l_ffn, l_out]

    def body(a_ref, b_ref, send_sems, recv_sems, a_out, b_out, token):
        for cp in _forward_copies(FWD_REST, [a_ref, b_ref], [a_ref, b_ref], send_sems, recv_sems):
            cp.start()
        token[...] = jnp.zeros_like(token)

    outs = pl.pallas_call(
        body, name="forward_rest_start", in_specs=[HBM] * 2,
        out_specs=[SEM, SEM, HBM, HBM, pl.BlockSpec(memory_space=pltpu.VMEM)],
        out_shape=[pltpu.SemaphoreType.DMA((n,)), pltpu.SemaphoreType.DMA((n,))]
        + [pltpu.HBM(b.shape, b.dtype) for b in bufs] + [TOKEN],
        input_output_aliases={0: 2, 1: 3}, compiler_params=pltpu.CompilerParams(has_side_effects=EFFECT),
    )(*[pltpu.with_memory_space_constraint(b, pltpu.HBM) for b in bufs])
    return dict(sems=outs[0:2], bufs=outs[2:4], token=outs[4])


def _forward_rest_wait(s, after):
    def body(a_ref, b_ref, send_sems, recv_sems, after_ref, a_out, b_out):
        for cp in _forward_copies(FWD_REST, [a_ref, b_ref], [a_ref, b_ref], send_sems, recv_sems):
            cp.wait_send()
            cp.wait_recv()

    return pl.pallas_call(
        body, name="forward_rest_wait", in_specs=[HBM, HBM, SEM, SEM, ANY], out_specs=[HBM, HBM],
        out_shape=[pltpu.HBM(b.shape, b.dtype) for b in s["bufs"]],
        input_output_aliases={0: 0, 1: 1}, compiler_params=pltpu.CompilerParams(has_side_effects=EFFECT),
    )(*s["bufs"], *s["sems"], after)


def _exchange_halves(ws, gs, small, *, name):
    D = gs[0].shape[1]
    n = len(ws)
    has_small = small is not None

    def body(*refs):
        g = refs[:n]
        t = refs[n + has_small:2 * n + has_small]
        sems = refs[2 * n + 2 * has_small:]
        d2d_send, d2d_recv = sems[0], sems[1]
        x, y, c = _pos()
        sib = (x, y, 1 - c)
        drains = []
        for i, w in enumerate(ws):
            h = HALF[w]
            for qq in range(N_CHIPS):
                _rcopy(_rows(g[i], qq * SLAB[w] + (1 - c) * h, h), _rows(t[i], qq * h, h),
                       d2d_send.at[i], d2d_recv.at[i], sib).start()
            drains.append(_rcopy(t[i], t[i], d2d_send.at[i], d2d_recv.at[i], sib))
        if has_small:
            small_ref, sall_ref = refs[n], refs[2 * n + 1]
            sm_send, sm_recv, loc_sem = sems[2], sems[3], sems[4]
            me = 4 * x + 2 * y + c
            own_small = pltpu.make_async_copy(small_ref, sall_ref.at[me], loc_sem)
            own_small.start()
            for f in range(1, N_DEV):
                peer = (_flip(x, f & 4), _flip(y, f & 2), _flip(c, f & 1))
                cp = _rcopy(small_ref, sall_ref.at[me], sm_send.at[f - 1], sm_recv.at[f - 1], peer)
                cp.start()
                drains.append(cp)
        for d in drains:
            d.wait_recv()
        for d in drains:
            d.wait_send()
        if has_small:
            own_small.wait()

    out_shape = [_sds((N_CHIPS * HALF[w], D), gs[0].dtype) for w in ws]
    scratch = [pltpu.SemaphoreType.DMA((n,)), pltpu.SemaphoreType.DMA((n,))]
    if has_small:
        out_shape.append(_sds((N_DEV,) + small.shape, F32))
        scratch += [pltpu.SemaphoreType.DMA((N_DEV - 1,)), pltpu.SemaphoreType.DMA((N_DEV - 1,)), pltpu.SemaphoreType.DMA]
    return pl.pallas_call(
        body, name=name, in_specs=[ANY] * (n + has_small), out_specs=[ANY] * (n + has_small),
        out_shape=out_shape, scratch_shapes=scratch,
    )(*gs, *([small] if has_small else []))


def _halves_copies(ws, g, t, send_sems, recv_sems):
    x, y, c = _pos()
    sib = (x, y, 1 - c)
    cps = []
    for i, w in enumerate(ws):
        h = HALF[w]
        for qq in range(N_CHIPS):
            cps.append(_rcopy(_rows(g[i], qq * SLAB[w] + (1 - c) * h, h), _rows(t[i], qq * h, h),
                              send_sems.at[N_CHIPS * i + qq], recv_sems.at[N_CHIPS * i + qq], sib))
    return cps


def _halves_start(ws, gs, *, name):
    D = gs[0].shape[1]
    n = len(ws)
    bufs = list(gs) + [lax.empty((N_CHIPS * HALF[w], D), gs[0].dtype) for w in ws]

    def body(*refs):
        for cp in _halves_copies(ws, refs[:n], refs[n:2 * n], refs[2 * n], refs[2 * n + 1]):
            cp.start()
        refs[-1][...] = jnp.zeros_like(refs[-1])

    outs = pl.pallas_call(
        body, name=name, in_specs=[HBM] * (2 * n),
        out_specs=[SEM, SEM] + [HBM] * (2 * n) + [pl.BlockSpec(memory_space=pltpu.VMEM)],
        out_shape=[pltpu.SemaphoreType.DMA((N_CHIPS * n,)), pltpu.SemaphoreType.DMA((N_CHIPS * n,))]
        + [pltpu.HBM(b.shape, b.dtype) for b in bufs] + [TOKEN],
        input_output_aliases={i: 2 + i for i in range(2 * n)},
        compiler_params=pltpu.CompilerParams(has_side_effects=EFFECT),
    )(*[pltpu.with_memory_space_constraint(b, pltpu.HBM) for b in bufs])
    return dict(sems=outs[0:2], gs=outs[2:2 + n], theirs=outs[2 + n:2 + 2 * n], token=outs[-1])


def _halves_wait(ws, s, after, *, name):
    n = len(ws)

    def body(*refs):
        for cp in _halves_copies(ws, refs[:n], refs[n:2 * n], refs[2 * n], refs[2 * n + 1]):
            cp.wait_send()
            cp.wait_recv()

    bufs = list(s["gs"]) + list(s["theirs"])
    outs = pl.pallas_call(
        body, name=name, in_specs=[HBM] * (2 * n) + [SEM, SEM, ANY], out_specs=[HBM] * (2 * n),
        out_shape=[pltpu.HBM(b.shape, b.dtype) for b in bufs],
        input_output_aliases={i: i for i in range(2 * n)},
        compiler_params=pltpu.CompilerParams(has_side_effects=EFFECT),
    )(*bufs, *s["sems"], after)
    return outs[:n], outs[n:]


REDUCE_SPLIT = 2


def _chip_partial(ws, gs, theirs, *, name, out_dtype=F32):
    D = gs[0].shape[1]
    n = len(ws)

    def body(*refs):
        for i in range(n):
            refs[2 * n + i][...] = (refs[i][...].astype(F32) + refs[n + i][...].astype(F32)).astype(out_dtype)

    blk = [HALF[w] // REDUCE_SPLIT for w in ws]
    mine = [pl.BlockSpec((b, D), lambda qq, j: ((2 * qq + lax.axis_index("c")) * REDUCE_SPLIT + j, 0)) for b in blk]
    flat = [pl.BlockSpec((b, D), lambda qq, j: (qq * REDUCE_SPLIT + j, 0)) for b in blk]
    return pl.pallas_call(
        body, name=name, grid=(N_CHIPS, REDUCE_SPLIT), in_specs=mine + flat, out_specs=flat,
        out_shape=[_sds((N_CHIPS * HALF[w], D), out_dtype) for w in ws],
        compiler_params=_cp(("parallel", "parallel")),
    )(*gs, *theirs)


def _partial_copies(ws, part, got, send_sems, recv_sems):
    x, y, c = _pos()
    cps = []
    for k, (fx, fy) in enumerate(CHIP_FLIPS):
        peer = (_flip(x, fx), _flip(y, fy), c)
        qp = 2 * _flip(x, fx) + _flip(y, fy)
        for i, w in enumerate(ws):
            cps.append(_rcopy(_rows(part[i], qp * HALF[w], HALF[w]), _rows(got[i], k * HALF[w], HALF[w]),
                              send_sems.at[len(ws) * k + i], recv_sems.at[len(ws) * k + i], peer))
    return cps


def _send_chip_partials(ws, parts, *, name):
    D = parts[0].shape[1]
    n = len(ws)

    def body(*refs):
        cps = _partial_copies(ws, refs[:n], refs[n:2 * n], refs[2 * n], refs[2 * n + 1])
        for cp in cps:
            cp.start()
        for cp in cps:
            cp.wait_recv()
        for cp in cps:
            cp.wait_send()

    return pl.pallas_call(
        body, name=name, in_specs=[ANY] * n, out_specs=[ANY] * n,
        out_shape=[_sds((len(CHIP_FLIPS) * HALF[w], D), parts[0].dtype) for w in ws],
        scratch_shapes=[pltpu.SemaphoreType.DMA((len(CHIP_FLIPS) * n,)), pltpu.SemaphoreType.DMA((len(CHIP_FLIPS) * n,))],
    )(*parts)


def _send_start(ws, parts, *, name):
    D = parts[0].shape[1]
    n = len(ws)
    bufs = list(parts) + [lax.empty((len(CHIP_FLIPS) * HALF[w], D), parts[0].dtype) for w in ws]

    def body(*refs):
        send_sems, recv_sems = refs[2 * n], refs[2 * n + 1]
        for cp in _partial_copies(ws, refs[:n], refs[n:2 * n], send_sems, recv_sems):
            cp.start()
        refs[-1][...] = jnp.zeros_like(refs[-1])

    outs = pl.pallas_call(
        body, name=name, in_specs=[HBM] * (2 * n),
        out_specs=[SEM, SEM] + [HBM] * (2 * n) + [pl.BlockSpec(memory_space=pltpu.VMEM)],
        out_shape=[pltpu.SemaphoreType.DMA((len(CHIP_FLIPS) * n,)), pltpu.SemaphoreType.DMA((len(CHIP_FLIPS) * n,))]
        + [pltpu.HBM(b.shape, b.dtype) for b in bufs] + [TOKEN],
        input_output_aliases={i: 2 + i for i in range(2 * n)},
        compiler_params=pltpu.CompilerParams(has_side_effects=EFFECT),
    )(*[pltpu.with_memory_space_constraint(b, pltpu.HBM) for b in bufs])
    return dict(sems=outs[0:2], parts=outs[2:2 + n], got=outs[2 + n:2 + 2 * n], token=outs[-1])


def _send_wait(ws, s, after, *, name):
    n = len(ws)

    def body(*refs):
        for cp in _partial_copies(ws, refs[:n], refs[n:2 * n], refs[2 * n], refs[2 * n + 1]):
            cp.wait_send()
            cp.wait_recv()

    bufs = list(s["parts"]) + list(s["got"])
    outs = pl.pallas_call(
        body, name=name, in_specs=[HBM] * (2 * n) + [SEM, SEM] + [ANY] * len(after), out_specs=[HBM] * (2 * n),
        out_shape=[pltpu.HBM(b.shape, b.dtype) for b in bufs],
        input_output_aliases={i: i for i in range(2 * n)},
        compiler_params=pltpu.CompilerParams(has_side_effects=EFFECT),
    )(*bufs, *s["sems"], *after)
    return outs[:n], outs[n:]


def _chip_reduce(ws, parts, got, *, name, after=None):
    D = parts[0].shape[1]
    nk = len(CHIP_FLIPS)
    n = len(ws)
    extra = [] if after is None else [after]

    def body(*refs):
        refs = refs[len(extra):]
        outs = refs[(1 + nk) * n:]
        for i in range(n):
            acc = refs[i][...].astype(F32)
            for k in range(nk):
                acc = acc + refs[n * (1 + k) + i][...].astype(F32)
            outs[i][...] = acc

    blk = [HALF[w] // REDUCE_SPLIT for w in ws]

    def q_idx(j):
        return (2 * lax.axis_index("x") + lax.axis_index("y")) * REDUCE_SPLIT + j

    in_specs = [pl.BlockSpec((b, D), lambda j: (q_idx(j), 0)) for b in blk]
    for k in range(nk):
        in_specs += [pl.BlockSpec((b, D), functools.partial(lambda j, k: (k * REDUCE_SPLIT + j, 0), k=k)) for b in blk]
    out_specs = [pl.BlockSpec((b, D), lambda j: (lax.axis_index("c") * REDUCE_SPLIT + j, 0)) for b in blk]
    return pl.pallas_call(
        body, name=name, grid=(REDUCE_SPLIT,), in_specs=[ANY] * len(extra) + in_specs, out_specs=out_specs,
        out_shape=[_sds((SLAB[w], D), F32) for w in ws],
        compiler_params=_cp(("parallel",)),
    )(*extra, *parts, *[g for _ in range(nk) for g in got])


def _exchange_reduced(ws, shards, *, name):
    n = len(ws)

    def body(*refs):
        ins, outs = refs[:n], refs[n:2 * n]
        send_sems, recv_sems = refs[2 * n], refs[2 * n + 1]
        x, y, c = _pos()
        sib = (x, y, 1 - c)
        cps = []
        for i, w in enumerate(ws):
            cp = _rcopy(_rows(ins[i], c * HALF[w], HALF[w]), _rows(outs[i], c * HALF[w], HALF[w]),
                        send_sems.at[i], recv_sems.at[i], sib)
            cp.start()
            cps.append(cp)
        for cp in cps:
            cp.wait_recv()
        for cp in cps:
            cp.wait_send()

    return pl.pallas_call(
        body, name=name, in_specs=[ANY] * n, out_specs=[ANY] * n,
        out_shape=[_sds(s.shape, s.dtype) for s in shards], input_output_aliases={i: i for i in range(n)},
        scratch_shapes=[pltpu.SemaphoreType.DMA((n,)), pltpu.SemaphoreType.DMA((n,))],
    )(*shards)


def _adamw_fn(w, g, m, v):
    m2 = ADAM_B1 * m + (1.0 - ADAM_B1) * g
    v2 = ADAM_B2 * v + (1.0 - ADAM_B2) * (g * g)
    m_hat = m2 / (1.0 - ADAM_B1 ** ADAM_STEP)
    v_hat = v2 / (1.0 - ADAM_B2 ** ADAM_STEP)
    return -ADAM_LR * (m_hat / (jnp.sqrt(v_hat) + ADAM_EPS) + ADAM_WD * w), m2, v2


def _adamw(w, g, m, v, *, name):
    shp = _sds(w.shape, F32)
    rows = w.shape[0]
    tm = max(t for t in range(SUBLANES, 512 + 1, SUBLANES) if rows % t == 0)
    return _rowwise(lambda wv, gv, mv, vv: (gv, *_adamw_fn(wv, gv, mv, vv)), [_full(w), _full(g), _full(m), _full(v)], [],
                    [shp] * 4, [], name=name, tm=tm)


SMALL_SEGS = (("loss", 8), ("norm_mix_w", 8), ("b_attn", 8), ("lb_logits", 8), ("hg_norm_w", 8), ("sinks", 8),
              ("norm_ffn_w", 8), ("conv_w", 72), ("conv_b", 24), ("final_norm_w", 8))
SMALL_OFF = {n: sum(r for _, r in SMALL_SEGS[:i]) for i, (n, _) in enumerate(SMALL_SEGS)}
SMALL_ROWS = sum(r for _, r in SMALL_SEGS)
LANES = 128


def _pack_small(parts):
    segs = []
    for n, r in SMALL_SEGS:
        a = parts.get(n)
        flat = jnp.zeros((0,), F32) if a is None else a.reshape(-1).astype(F32)
        segs.append(jnp.pad(flat, (0, r * LANES - flat.shape[0])).reshape(r, LANES))
    return jnp.concatenate(segs, axis=0)


def _unpack_small(pack, n, shape):
    size = math.prod(shape)
    r0 = SMALL_OFF[n]
    return pack[r0:r0 + dict(SMALL_SEGS)[n]].reshape(-1)[:size].reshape(shape)


def _small_update(sall, wp, mp, vp, *, after):
    R = SMALL_ROWS
    r_lb = SMALL_OFF["lb_logits"]

    def body(after_ref, s_ref, w_ref, m_ref, v_ref, g_ref, d_ref, m2_ref, v2_ref, loss_ref):
        g = s_ref[0]
        for i in range(1, N_DEV):
            g = g + s_ref[i]
        tot = jnp.sum(jnp.sum(g[0:8], axis=1, keepdims=True), axis=0, keepdims=True)
        loss_ref[...] = jnp.broadcast_to(tot, loss_ref.shape)
        lg = w_ref[r_lb:r_lb + 8, :]
        p0 = _sigmoid(lg - pltpu.roll(lg, 4, 0))
        d = g[r_lb:r_lb + 8]
        d = d + pltpu.roll(d, 4, 0)
        sign = jnp.where(lax.broadcasted_iota(jnp.int32, d.shape, 0) < 4, 1.0, -1.0)
        g = jnp.concatenate([g[:r_lb], sign * d * p0 * (1.0 - p0), g[r_lb + 8:]], axis=0)
        g_ref[...] = g
        d_ref[...], m2_ref[...], v2_ref[...] = _adamw_fn(w_ref[...], g, m_ref[...], v_ref[...])

    full = pl.BlockSpec((R, LANES), lambda: (0, 0))
    return pl.pallas_call(
        body, name="small_update",
        in_specs=[ANY, pl.BlockSpec((N_DEV, R, LANES), lambda: (0, 0, 0)), full, full, full],
        out_specs=[full, full, full, full, pl.BlockSpec((8, LANES), lambda: (0, 0))],
        out_shape=[_sds((R, LANES), F32)] * 4 + [_sds((8, LANES), F32)],
        compiler_params=_cp(),
    )(after, sall, wp, mp, vp)


def _lb_fwd(lb_logits):
    n = lb_logits.shape[1]

    def body(l_ref, o_ref):
        o_ref[...] = _sigmoid(l_ref[0:1, :] - l_ref[1:2, :])

    return pl.pallas_call(body, name="lb_fwd", out_shape=jax.ShapeDtypeStruct((1, n), F32), compiler_params=_cp())(lb_logits)


class _MeshExchange:
    def __init__(self, pack, cw8):
        self.gather = _gather_start(pack, cw8)
        self.sent = None
        self.conv_w8 = None

    def start(self):
        return self.gather["token"]

    def w_in(self, after):
        self.pack, l_in = _gather_wait_in(self.gather, after)
        return (_forward_in(l_in), N_CHIPS * SLAB[0], 0)

    def mid(self, after):
        l_ffn, l_out, l_cw = _gather_wait_rest(self.gather, self.pack, after)
        self.conv_w8 = jnp.concatenate([l_cw[i] for i in range(N_CHIPS)], axis=1)
        self.passing = _forward_rest_start(l_ffn, l_out)
        return self.passing["token"]

    def rest(self, after):
        l_ffn, l_out = _forward_rest_wait(self.passing, after)
        rows = N_CHIPS * SLAB[FFN_W[0]]
        return dict(w_gate_t=(l_ffn, rows, 0), w_up_t=(l_ffn, rows, 1), w_down=(l_ffn, rows, 2),
                    w_out=(l_out, N_CHIPS * SLAB[4], 0), conv_w8=self.conv_w8)

    def ffn_grads(self, gs):
        self.swap = _halves_start(FFN_W, gs, name="halves_ffn_start")
        return self.swap["token"]

    def ffn_grads_send(self, after):
        gs, theirs = _halves_wait(FFN_W, self.swap, after, name="halves_ffn_wait")
        parts = _chip_partial(FFN_W, gs, theirs, name="chip_partial_ffn", out_dtype=BF16)
        self.sent = _send_start(FFN_W, parts, name="send_ffn_start")
        return self.sent["token"]


def kernel(x, norm_mix_w, w_in, b_attn, lb_logits, hg_norm_w, sinks, w_out, norm_ffn_w, w_gate, w_up, conv_w, conv_b, w_down, final_norm_w, loss_target, m_norm_mix_w, m_w_in, m_b_attn, m_lb_logits, m_hg_norm_w, m_sinks, m_w_out, m_norm_ffn_w, m_w_gate, m_w_up, m_conv_w, m_conv_b, m_w_down, m_final_norm_w, v_norm_mix_w, v_w_in, v_b_attn, v_lb_logits, v_hg_norm_w, v_sinks, v_w_out, v_norm_ffn_w, v_w_gate, v_w_up, v_conv_w, v_conv_b, v_w_down, v_final_norm_w):
    D = D_MODEL
    q = 2 * lax.axis_index("x") + lax.axis_index("y")
    ccols = D_FF // N_CHIPS

    pack = jnp.concatenate([w_in[0].T, w_gate[0].T, w_up[0].T, w_down[0], w_out[0]], axis=0).astype(BF16)
    cw8 = jnp.concatenate([conv_w[0], jnp.zeros((SUBLANES - 3, ccols), F32)], axis=0)
    ex = _MeshExchange(pack, cw8)
    p = dict(norm_mix_w=norm_mix_w, b_attn=b_attn, lb=_lb_fwd(lb_logits), hg_norm_w=hg_norm_w, sinks=sinks,
             norm_ffn_w=norm_ffn_w, conv_b=conv_b, final_norm_w=final_norm_w.reshape(1, D))
    loss_cols, dx, g = _local_step(x[0], loss_target[0], p, ex)
    conv_w8 = ex.conv_w8

    small = _pack_small(dict(loss=loss_cols, norm_mix_w=g["norm_mix_w"], b_attn=g["b_attn"], lb_logits=g["lb"],
                             hg_norm_w=g["hg_norm_w"], sinks=g["sinks8"], norm_ffn_w=g["norm_ffn_w"],
                             conv_w=g["conv_w8"][:3], conv_b=g["conv_b"], final_norm_w=g["final_norm_w"]))
    parts_ffn, got_ffn = _send_wait(FFN_W, ex.sent, [dx], name="send_ffn_wait")
    late = (0, 4)
    gs = [g["g_in_t"], g["g_out"]]
    *theirs, sall = _exchange_halves(late, gs, small, name="exchange_halves_late")
    parts_late = _chip_partial(late, gs, theirs, name="chip_partial_late", out_dtype=BF16)
    sent_late = _send_start(late, parts_late, name="send_late_start")
    big = {}

    def finish(ws, parts, got, specs, tag, after):
        shards = _exchange_reduced(ws, _chip_reduce(ws, parts, got, name="chip_reduce_" + tag, after=after),
                                   name="exchange_reduced_" + tag)
        deltas = []
        for gw, (n, w, m, v, tr) in zip(shards, specs):
            view = (lambda a: a[0].T) if tr else (lambda a: a[0])
            back = (lambda a: a.T[None]) if tr else (lambda a: a[None])
            res = _adamw(view(w), gw, view(m), view(v), name="adamw_" + n)
            big[n] = tuple(back(r) for r in res)
            deltas.append(res[1])
        return deltas

    done_ffn = finish(FFN_W, parts_ffn, got_ffn, (("w_gate", w_gate, m_w_gate, v_w_gate, True),
                                                  ("w_up", w_up, m_w_up, v_w_up, True),
                                                  ("w_down", w_down, m_w_down, v_w_down, False)), "ffn", sent_late["token"])

    def place(a):
        return lax.dynamic_update_slice(jnp.zeros((3, D_FF), F32), a[0], (0, q * ccols))

    def small_pack(ws, cw):
        nm, ba, lbl, hg, sk, nf, cb, fn = ws
        return _pack_small(dict(norm_mix_w=nm, b_attn=ba, lb_logits=lbl, hg_norm_w=hg,
                                sinks=jnp.broadcast_to(sk.reshape(ATT_HEADS, 1), (ATT_HEADS, LANES)), norm_ffn_w=nf,
                                conv_w=cw, conv_b=cb, final_norm_w=fn))

    wp = small_pack((norm_mix_w, b_attn, lb_logits, hg_norm_w, sinks, norm_ffn_w, conv_b, final_norm_w), conv_w8[:3])
    mp = small_pack((m_norm_mix_w, m_b_attn, m_lb_logits, m_hg_norm_w, m_sinks, m_norm_ffn_w, m_conv_b, m_final_norm_w),
                    place(m_conv_w))
    vp = small_pack((v_norm_mix_w, v_b_attn, v_lb_logits, v_hg_norm_w, v_sinks, v_norm_ffn_w, v_conv_b, v_final_norm_w),
                    place(v_conv_w))
    outs = _small_update(sall, wp, mp, vp, after=sent_late["token"])
    loss = outs[4][0, 0]
    parts_late, got_late = _send_wait(late, sent_late, [*done_ffn, outs[4]], name="send_late_wait")
    finish(late, parts_late, got_late, (("w_in", w_in, m_w_in, v_w_in, True), ("w_out", w_out, m_w_out, v_w_out, False)),
           "late", None)

    def small_out(pk, n, ref):
        if n == "sinks":
            return pk[SMALL_OFF[n]:SMALL_OFF[n] + ATT_HEADS, 0].reshape(ref.shape)
        if n == "conv_w":
            full = _unpack_small(pk, n, (3, D_FF))
            return lax.dynamic_slice(full, (0, q * ccols), (3, ccols))[None]
        return _unpack_small(pk, n, ref.shape)

    refs = dict(norm_mix_w=norm_mix_w, b_attn=b_attn, lb_logits=lb_logits, hg_norm_w=hg_norm_w, sinks=sinks,
                norm_ffn_w=norm_ffn_w, conv_w=conv_w, conv_b=conv_b, final_norm_w=final_norm_w)
    order = ("norm_mix_w", "w_in", "b_attn", "lb_logits", "hg_norm_w", "sinks", "w_out", "norm_ffn_w", "w_gate", "w_up",
             "conv_w", "conv_b", "w_down", "final_norm_w")
    res = [loss, dx[None]]
    for k in range(4):
        for n in order:
            res.append(big[n][k] if n in big else small_out(outs[k], n, refs[n]))
    return tuple(res)
```

```python
import functools
import math

import jax
import jax.numpy as jnp
from jax import lax
from jax.experimental import pallas as pl
from jax.experimental.pallas import tpu as pltpu

F32 = jnp.float32
BF16 = jnp.bfloat16

D_MODEL = 1024
HG_HEADS = 4
HG_DK = 128
HG_W = HG_HEADS * HG_DK
HG_CHUNK = 64
HG_SUB = 8
HG_FWD_CHUNKS_PER_STEP = 4
HG_CHUNKS_PER_STEP = 2
ATT_HEADS = 8
ATT_KV = 2
ATT_GROUP = ATT_HEADS // ATT_KV
ATT_HD = 64
ATT_BLOCK = 128
ATT_Q_W = ATT_HEADS * ATT_HD
ATT_KV_W = ATT_KV * ATT_HD
ATT_COLS = ATT_Q_W + 2 * ATT_KV_W
IN_COLS = 4 * HG_W + ATT_COLS
D_FF = 2816
EPS = 1e-6
ADAM_LR, ADAM_B1, ADAM_B2, ADAM_EPS, ADAM_WD, ADAM_STEP = 0.001, 0.9, 0.999, 1e-08, 0.01, 10
NEG = -1e30

V7X_VMEM_BYTES = 64 * 1024 * 1024
VMEM_LIMIT = 48 * 1024 * 1024
SUBLANES = 8

N_CHIPS = 4


def _cp(sem=None, **kw):
    return pltpu.CompilerParams(dimension_semantics=sem, vmem_limit_bytes=VMEM_LIMIT, **kw)


def _sds(shape, dtype):
    return jax.ShapeDtypeStruct(shape, dtype)


TOKEN = jax.ShapeDtypeStruct((8, 128), jnp.float32)


def _wspec(w):
    arr, rows, blk = w
    return pl.BlockSpec((rows, arr.shape[1]), lambda i: (blk, 0))


def _mm_nt(a, w, *, splits, out_dtype, name, after=None, tm=512):
    M, K = a.shape
    N = w[1]
    tm = min(tm, M)
    assert sum(splits) == N and M % tm == 0
    offs = [sum(splits[:i]) for i in range(len(splits))]
    n_in = 2 if after is None else 3

    def body(*refs):
        a_ref, w_ref = refs[0], refs[1]
        acc = lax.dot_general(a_ref[...], w_ref[...], (((1,), (1,)), ((), ())), preferred_element_type=F32)
        for o_ref, c0, n in zip(refs[n_in:], offs, splits):
            o_ref[...] = acc[:, c0:c0 + n].astype(out_dtype)

    in_specs = [pl.BlockSpec((tm, K), lambda i: (i, 0)), _wspec(w)]
    args = [a, w[0]]
    if after is not None:
        in_specs.append(pl.BlockSpec(memory_space=pl.ANY))
        args.append(after)
    outs = pl.pallas_call(
        body, name=name, grid=(M // tm,), in_specs=in_specs,
        out_specs=[pl.BlockSpec((tm, n), lambda i: (i, 0)) for n in splits],
        out_shape=[_sds((M, n), out_dtype) for n in splits],
        compiler_params=_cp(("parallel",)),
    )(*args)
    return outs


def _mm_nn(pieces, ws, *, name, out_dtype=F32, residual=None, epilogue=None, prologue=None, after=None,
           w_transposed=False, tm=512):
    pro_fn, pro_rows, pro_bc, pro_out = prologue or (None, [], [], None)
    if prologue is not None:
        assert pieces is None and len(ws) == 1
        pieces = [[pro_out]]
    M = pieces[0][0].shape[0]
    K = ws[0][1] if w_transposed else ws[0][0].shape[1]
    tm = min(tm, M)
    flat = [] if prologue is not None else [p for grp in pieces for p in grp]
    n_p = len(flat)
    n_w = len(ws)
    n_pr, n_pb = len(pro_rows), len(pro_bc)
    fn, row_ins, bc_ins, row_outs, acc_outs = epilogue or (None, [], [], [_sds((M, K), out_dtype)], [])
    if residual is not None:
        assert epilogue is None
        row_ins = [residual]
    n_r, n_b, n_o = len(row_ins), len(bc_ins), len(row_outs)
    lead = [] if after is None else [after]

    def body(*refs):
        refs = refs[len(lead):]
        p_refs = refs[:n_p]
        w_refs = refs[n_p:n_p + n_w]
        extra = [r[...] for r in refs[n_p + n_w:n_p + n_w + n_r + n_b]]
        base = n_p + n_w + n_r + n_b
        pro = [r[...] for r in refs[base:base + n_pr + n_pb]]
        base += n_pr + n_pb
        o_refs = refs[base:base + n_o]
        a_refs = refs[base + n_o:base + n_o + len(acc_outs)]
        if pro_fn is not None:
            lhs = pro_fn(*pro).astype(pro_out.dtype)
            refs[-1][...] = lhs
            tiles = [lhs]
        else:
            tiles = [r[...] for r in p_refs]
        acc = None
        k = 0
        for gi, grp in enumerate(pieces):
            c0 = 0
            for p in grp:
                n = p.shape[1]
                if w_transposed:
                    t = lax.dot_general(tiles[k], w_refs[gi][...], (((1,), (1,)), ((), ())), preferred_element_type=F32)
                else:
                    t = jnp.dot(tiles[k], w_refs[gi][c0:c0 + n, :], preferred_element_type=F32)
                acc = t if acc is None else acc + t
                c0 += n
                k += 1
        if fn is None:
            res = (acc + extra[0] if residual is not None else acc,)
        else:
            res = fn(acc, *extra)
        for o_ref, val in zip(o_refs, res[:n_o]):
            o_ref[...] = val.astype(o_ref.dtype)
        if acc_outs:
            @pl.when(pl.program_id(0) == 0)
            def _():
                for a_ref in a_refs:
                    a_ref[...] = jnp.zeros_like(a_ref)
            for a_ref, val in zip(a_refs, res[n_o:]):
                a_ref[...] += val

    in_specs = [pl.BlockSpec((tm, p.shape[1]), lambda i: (i, 0)) for p in flat]
    in_specs += [_wspec(w) for w in ws]
    in_specs += [pl.BlockSpec((tm, r.shape[1]), lambda i: (i, 0)) for r in row_ins]
    in_specs += [pl.BlockSpec(b.shape, lambda i: (0, 0)) for b in bc_ins]
    in_specs += [pl.BlockSpec((tm, r.shape[1]), lambda i: (i, 0)) for r in pro_rows]
    in_specs += [pl.BlockSpec(b.shape, lambda i: (0, 0)) for b in pro_bc]
    out_specs = [pl.BlockSpec((tm, s.shape[1]), lambda i: (i, 0)) for s in row_outs]
    out_specs += [pl.BlockSpec(s.shape, lambda i: (0, 0)) for s in acc_outs]
    pro_outs = [] if prologue is None else [pro_out]
    out_specs += [pl.BlockSpec((tm, s.shape[1]), lambda i: (i, 0)) for s in pro_outs]
    outs = pl.pallas_call(
        body, name=name, grid=(M // tm,), in_specs=[pl.BlockSpec(memory_space=pl.ANY)] * len(lead) + in_specs,
        out_specs=out_specs, out_shape=list(row_outs) + list(acc_outs) + pro_outs,
        compiler_params=_cp(("arbitrary",) if acc_outs else ("parallel",)),
    )(*lead, *flat, *[w[0] for w in ws], *row_ins, *bc_ins, *pro_rows, *pro_bc)
    return outs if (epilogue is not None or prologue is not None) else outs[0]


def _mm_tn(pieces, x, *, name, out_dtype=BF16, tt=1024):
    M, K = x.shape
    tt = min(tt, M)
    ns = [p.shape[1] for p in pieces]
    offs = [sum(ns[:i]) for i in range(len(ns))]
    N = sum(ns)
    n_p = len(pieces)
    last = M // tt - 1

    def body(*refs):
        p_refs = refs[:n_p]
        x_ref = refs[n_p]
        o_ref, acc_ref = refs[n_p + 1], refs[n_p + 2]

        @pl.when(pl.program_id(0) == 0)
        def _():
            acc_ref[...] = jnp.zeros_like(acc_ref)

        xv = x_ref[...]
        for p_ref, c0, n in zip(p_refs, offs, ns):
            acc_ref[c0:c0 + n, :] += lax.dot_general(p_ref[...], xv, (((0,), (0,)), ((), ())),
                                                      preferred_element_type=F32)

        @pl.when(pl.program_id(0) == last)
        def _():
            o_ref[...] = acc_ref[...].astype(o_ref.dtype)

    in_specs = [pl.BlockSpec((tt, n), lambda i: (i, 0)) for n in ns]
    in_specs.append(pl.BlockSpec((tt, K), lambda i: (i, 0)))
    return pl.pallas_call(
        body, name=name, grid=(M // tt,), in_specs=in_specs,
        out_specs=pl.BlockSpec((N, K), lambda i: (0, 0)),
        out_shape=_sds((N, K), out_dtype),
        scratch_shapes=[pltpu.VMEM((N, K), F32)],
        compiler_params=_cp(("arbitrary",)),
    )(*pieces, x)


def _rms_fwd(xf, w):
    inv = lax.rsqrt(jnp.mean(xf * xf, axis=-1, keepdims=True) + EPS)
    return xf * inv * w


def _rms_bwd(xf, w, dy):
    inv = lax.rsqrt(jnp.mean(xf * xf, axis=-1, keepdims=True) + EPS)
    xhat = xf * inv
    dxhat = dy * w
    dx = inv * (dxhat - xhat * jnp.mean(dxhat * xhat, axis=-1, keepdims=True))
    dw = jnp.sum(dy * xhat, axis=0, keepdims=True)
    return dx, dw


def _sigmoid(x):
    return 1.0 / (1.0 + jnp.exp(-x))


def _rowwise(fn, row_ins, bc_ins, row_outs, acc_outs, *, name, tm=256, after=None):
    M = row_outs[0].shape[0] if row_outs else row_ins[0][0].shape[0]
    assert M % tm == 0 and tm % SUBLANES == 0, (name, M, tm)
    n_r, n_b, n_o, n_a = len(row_ins), len(bc_ins), len(row_outs), len(acc_outs)
    n_after = 0 if after is None else 1

    def body(*refs):
        refs = refs[n_after:]
        ins = [r[...] for r in refs[:n_r + n_b]]
        o_refs = refs[n_r + n_b:n_r + n_b + n_o]
        a_refs = refs[n_r + n_b + n_o:]
        res = fn(*ins)
        for o_ref, val in zip(o_refs, res[:n_o]):
            o_ref[...] = val.astype(o_ref.dtype)
        if n_a:
            @pl.when(pl.program_id(0) == 0)
            def _():
                for a_ref in a_refs:
                    a_ref[...] = jnp.zeros_like(a_ref)
            for a_ref, val in zip(a_refs, res[n_o:]):
                a_ref[...] += val

    in_specs = [pl.BlockSpec((tm, cw), functools.partial(lambda i, cb, r0: (i + r0, cb), cb=cb, r0=r0))
                for (_, cw, cb, r0) in row_ins]
    in_specs += [pl.BlockSpec(b.shape, lambda i: (0, 0)) for b in bc_ins]
    out_specs = [pl.BlockSpec((tm, s.shape[1]), lambda i: (i, 0)) for s in row_outs]
    out_specs += [pl.BlockSpec(s.shape, lambda i: (0, 0)) for s in acc_outs]
    if n_after:
        in_specs = [pl.BlockSpec(memory_space=pl.ANY)] + in_specs
    return pl.pallas_call(
        body, name=name, grid=(M // tm,), in_specs=in_specs, out_specs=out_specs,
        out_shape=list(row_outs) + list(acc_outs),
        compiler_params=_cp(("arbitrary",) if n_a else ("parallel",)),
    )(*([after] if n_after else []), *[r[0] for r in row_ins], *bc_ins)


def _full(a, first_row_block=0):
    return (a, a.shape[1], 0, first_row_block)


def _conv_rows(ext, w_ref_val, lo):
    s1 = pltpu.roll(ext, 1, 0)
    s2 = pltpu.roll(ext, 2, 0)
    y = w_ref_val[0:1, :] * s2 + w_ref_val[1:2, :] * s1 + w_ref_val[2:3, :] * ext
    return y[SUBLANES:, :]


def _convact_fwd(gp, up, conv_w8, conv_b, *, name, tr=512, tc=1408):
    T, C = gp.shape
    tr = min(tr, T)
    hb = tr // SUBLANES

    def body(gp_ref, gph_ref, up_ref, w_ref, b_ref, act_ref):
        i = pl.program_id(1)
        halo = jnp.where(i > 0, gph_ref[...], 0.0)
        ext = jnp.concatenate([halo, gp_ref[...]], axis=0)
        gate = _conv_rows(ext, w_ref[...], 0) + b_ref[...]
        act_ref[...] = (gate * _sigmoid(gate) * up_ref[...]).astype(act_ref.dtype)

    return pl.pallas_call(
        body, name=name, grid=(C // tc, T // tr),
        in_specs=[pl.BlockSpec((tr, tc), lambda j, i: (i, j)),
                  pl.BlockSpec((SUBLANES, tc), lambda j, i: (jnp.maximum(i * hb - 1, 0), j)),
                  pl.BlockSpec((tr, tc), lambda j, i: (i, j)),
                  pl.BlockSpec((SUBLANES, tc), lambda j, i: (0, j)),
                  pl.BlockSpec((1, tc), lambda j, i: (0, j))],
        out_specs=pl.BlockSpec((tr, tc), lambda j, i: (i, j)),
        out_shape=_sds((T, C), BF16),
        compiler_params=_cp(("parallel", "parallel")),
    )(gp, gp, up, conv_w8, conv_b)


def _ffn_in(v, w_gate, w_up, conv_w8, conv_b, *, name, tm=256):
    T, K = v.shape
    N = w_gate[1]
    tm = min(tm, T)

    def body(v_ref, wg_ref, wu_ref, cw_ref, cb_ref, gp_ref, up_ref, act_ref, carry_sc):
        @pl.when(pl.program_id(0) == 0)
        def _():
            carry_sc[...] = jnp.zeros_like(carry_sc)

        vv = v_ref[...]
        dn = (((1,), (1,)), ((), ()))
        gp = lax.dot_general(vv, wg_ref[...], dn, preferred_element_type=F32)
        up = lax.dot_general(vv, wu_ref[...], dn, preferred_element_type=F32)
        gp_ref[...] = gp
        up_ref[...] = up
        gate = _conv_rows(jnp.concatenate([carry_sc[...], gp], axis=0), cw_ref[...], 0) + cb_ref[...]
        act_ref[...] = (gate * _sigmoid(gate) * up).astype(act_ref.dtype)
        carry_sc[...] = gp[tm - SUBLANES:, :]

    tile = lambda dt: pl.BlockSpec((tm, N), lambda i: (i, 0))
    return pl.pallas_call(
        body, name=name, grid=(T // tm,),
        in_specs=[pl.BlockSpec((tm, K), lambda i: (i, 0)), _wspec(w_gate), _wspec(w_up),
                  pl.BlockSpec((SUBLANES, N), lambda i: (0, 0)), pl.BlockSpec((1, N), lambda i: (0, 0))],
        out_specs=[tile(F32), tile(F32), tile(BF16)],
        out_shape=[_sds((T, N), F32), _sds((T, N), F32), _sds((T, N), BF16)],
        scratch_shapes=[pltpu.VMEM((SUBLANES, N), F32)],
        compiler_params=_cp(("arbitrary",)),
    )(v, w_gate[0], w_up[0], conv_w8, conv_b)


def _convact_bwd(gp, up, dact, conv_w8, conv_b, *, name, tr=256, tc=1408):
    T, C = gp.shape
    tr = min(tr, T)
    hb = tr // SUBLANES
    nr = T // tr

    def body(gp_ref, gpp_ref, gpn_ref, up_ref, upn_ref, da_ref, dan_ref, w_ref, b_ref,
             dgp_ref, dup_ref, dw_ref, db_ref):
        i = pl.program_id(1)
        w = w_ref[...]
        prev = jnp.where(i > 0, gpp_ref[...], 0.0)
        last = i == nr - 1
        gp_ext = jnp.concatenate([prev, gp_ref[...], gpn_ref[...]], axis=0)
        gate = _conv_rows(gp_ext, w, 0) + b_ref[...]
        up_e = jnp.concatenate([up_ref[...], upn_ref[...]], axis=0)
        da_e = jnp.concatenate([da_ref[...], dan_ref[...]], axis=0)
        row = lax.broadcasted_iota(jnp.int32, gate.shape, 0)
        valid = jnp.logical_or(row < tr, jnp.logical_not(last))
        sg = _sigmoid(gate)
        silu = gate * sg
        dgate = jnp.where(valid, da_e * up_e * (sg * (1.0 + gate * (1.0 - sg))), 0.0)
        dup_ref[...] = (da_e[:tr] * silu[:tr]).astype(dup_ref.dtype)
        n = tr + SUBLANES
        g1 = pltpu.roll(dgate, n - 1, 0)
        g2 = pltpu.roll(dgate, n - 2, 0)
        dgp = w[2:3, :] * dgate + w[1:2, :] * g1 + w[0:1, :] * g2
        dgp_ref[...] = dgp[:tr].astype(dgp_ref.dtype)
        gpc = gp_ref[...]
        dw0 = jnp.sum(gpc * g2[:tr], axis=0, keepdims=True)
        dw1 = jnp.sum(gpc * g1[:tr], axis=0, keepdims=True)
        dw2 = jnp.sum(gpc * dgate[:tr], axis=0, keepdims=True)
        dbv = jnp.sum(dgate[:tr], axis=0, keepdims=True)
        z = jnp.zeros((SUBLANES - 3, gpc.shape[1]), F32)

        @pl.when(i == 0)
        def _():
            dw_ref[...] = jnp.zeros_like(dw_ref)
            db_ref[...] = jnp.zeros_like(db_ref)

        dw_ref[...] += jnp.concatenate([dw0, dw1, dw2, z], axis=0)
        db_ref[...] += dbv

    cur = pl.BlockSpec((tr, tc), lambda j, i: (i, j))
    prv = pl.BlockSpec((SUBLANES, tc), lambda j, i: (jnp.maximum(i * hb - 1, 0), j))
    nxt = pl.BlockSpec((SUBLANES, tc), lambda j, i: (jnp.minimum((i + 1) * hb, T // SUBLANES - 1), j))
    return pl.pallas_call(
        body, name=name, grid=(C // tc, nr),
        in_specs=[cur, prv, nxt, cur, nxt, cur, nxt,
                  pl.BlockSpec((SUBLANES, tc), lambda j, i: (0, j)),
                  pl.BlockSpec((1, tc), lambda j, i: (0, j))],
        out_specs=[cur, cur,
                   pl.BlockSpec((SUBLANES, tc), lambda j, i: (0, j)),
                   pl.BlockSpec((1, tc), lambda j, i: (0, j))],
        out_shape=[_sds((T, C), BF16), _sds((T, C), BF16), _sds((SUBLANES, C), F32), _sds((1, C), F32)],
        compiler_params=_cp(("parallel", "arbitrary")),
    )(gp, gp, gp, up, up, dact, dact, conv_w8, conv_b)


def _cumsum_rows(x):
    n = x.shape[0]
    row = lax.broadcasted_iota(jnp.int32, x.shape, 0)
    s = 1
    while s < n:
        x = x + jnp.where(row >= s, pltpu.roll(x, s, 0), 0.0)
        s *= 2
    return x


def _rcumsum_rows(x):
    n = x.shape[0]
    row = lax.broadcasted_iota(jnp.int32, x.shape, 0)
    s = 1
    while s < n:
        x = x + jnp.where(row < n - s, pltpu.roll(x, n - s, 0), 0.0)
        s *= 2
    return x


def _dot_nt(a, b):
    return lax.dot_general(a.astype(BF16), b.astype(BF16), (((1,), (1,)), ((), ())), preferred_element_type=F32)


def _dot_tn(a, b):
    return lax.dot_general(a.astype(BF16), b.astype(BF16), (((0,), (0,)), ((), ())), preferred_element_type=F32)


def _dot_nn(a, b):
    return jnp.dot(a.astype(BF16), b.astype(BF16), preferred_element_type=F32)


def _dot3(a, b, contract):
    def split(x):
        hi = x.astype(BF16)
        return hi, (x - hi.astype(F32)).astype(BF16)

    a_hi, a_lo = split(a)
    b_hi, b_lo = split(b)
    dot = lambda x, y: lax.dot_general(x, y, (contract, ((), ())), preferred_element_type=F32)
    return dot(a_hi, b_hi) + (dot(a_hi, b_lo) + dot(a_lo, b_hi))


NT, TN, NN = ((1,), (1,)), ((0,), (0,)), ((1,), (0,))


def _hg_gates(hq, hf, lbv):
    sig = _sigmoid(hf)
    f = lbv + (1.0 - lbv) * sig
    return sig, f, jnp.log(f), 1.0 - f, hq * (HG_DK ** -0.5)


def _hg_sel_rows(ref, sp):
    return jnp.concatenate(
        [jnp.broadcast_to(ref[pl.ds(HG_SUB * i + sp, 1), :], (HG_SUB, HG_DK)) for i in range(HG_CHUNK // HG_SUB)], axis=0)


def _hg_masks():
    C = HG_CHUNK
    row = lax.broadcasted_iota(jnp.int32, (C, C), 0)
    col = lax.broadcasted_iota(jnp.int32, (C, C), 1)
    d = col - (row // HG_SUB) * HG_SUB
    tmod = row % HG_SUB
    diag_valid = jnp.logical_and(d >= 0, d <= tmod)
    return row, col, d, diag_valid


def _hg_scores(q, k, b, b_sc, k_sc):
    C, S = HG_CHUNK, HG_SUB
    row, col, d, diag_valid = _hg_masks()
    blocks = [jnp.zeros((S, C), F32)]
    for i in range(1, C // S):
        r = b_sc[pl.ds(S * i - 1, 1), :]
        qi = q[S * i:S * (i + 1)] * jnp.exp(b[S * i:S * (i + 1)] - r)
        kk = k * jnp.exp(jnp.minimum(r - b, 0.0))
        blocks.append(_dot_nt(qi, kk))
    a_off = jnp.where(col < (row // S) * S, jnp.concatenate(blocks, axis=0), 0.0)
    a_d = jnp.zeros((C, C), F32)
    for sp in range(S):
        bs = _hg_sel_rows(b_sc, sp)
        ks = _hg_sel_rows(k_sc, sp)
        e = jnp.exp(jnp.minimum(b - bs, 0.0))
        colv = jnp.sum(q * ks * e, axis=-1, keepdims=True)
        a_d = jnp.where(d == sp, colv, a_d)
    return a_off + jnp.where(diag_valid, a_d, 0.0)


def _hg_prep(hq_v, hf_v, lbv, b_sc, k_sc):
    sig, f, g, k, q = _hg_gates(hq_v, hf_v, lbv)
    b = _cumsum_rows(g)
    b_sc[...] = b
    k_sc[...] = k
    return sig, f, k, q, b, b_sc[pl.ds(HG_CHUNK - 1, 1), :]


def _hgrn_fwd(hq, hf, hi, lb, *, name):
    T = hq.shape[0]
    C, H, K = HG_CHUNK, HG_HEADS, HG_DK
    NC = T // C

    def body(hq_ref, hf_ref, hi_ref, lb_ref, o_ref, st_ref, s_sc, b_sc, k_sc):
        @pl.when(pl.program_id(0) == 0)
        def _():
            s_sc[...] = jnp.zeros_like(s_sc)

        st_all = s_sc[...]
        for j in range(P):
            rows = slice(C * j, C * (j + 1))
            st_ref[j] = st_all
            outs, news = [], []
            for h in range(H):
                sl = slice(K * h, K * (h + 1))
                _, _, k, q, b, bc = _hg_prep(hq_ref[rows, sl], hf_ref[rows, sl], lb_ref[:, sl], b_sc.at[j, h], k_sc.at[j, h])
                v = hi_ref[rows, sl]
                st0 = st_all[:, sl]
                a = _hg_scores(q, k, b, b_sc.at[j, h], k_sc.at[j, h])
                outs.append(_dot_nn(a, v) + _dot_nt(q * jnp.exp(b), st0))
                news.append(st0 * jnp.exp(bc) + _dot_tn(v, k * jnp.exp(bc - b)))
            o_ref[rows, :] = jnp.concatenate(outs, axis=1)
            st_all = jnp.concatenate(news, axis=1)
        s_sc[...] = st_all

    P = HG_FWD_CHUNKS_PER_STEP
    blk = pl.BlockSpec((P * C, H * K), lambda c: (c, 0))
    return pl.pallas_call(
        body, name=name, grid=(NC // P,),
        in_specs=[blk, blk, blk, pl.BlockSpec((1, H * K), lambda c: (0, 0))],
        out_specs=[blk, pl.BlockSpec((P, K, H * K), lambda c: (c, 0, 0))],
        out_shape=[_sds((T, H * K), F32), _sds((NC, K, H * K), F32)],
        scratch_shapes=[pltpu.VMEM((K, H * K), F32), pltpu.VMEM((P, H, C, K), F32), pltpu.VMEM((P, H, C, K), F32)],
        compiler_params=_cp(("arbitrary",)),
    )(hq, hf, hi, lb)


def _hgrn_bwd(hq, hf, hi, lb, states, do, *, name):
    T = hq.shape[0]
    C, H, K, S = HG_CHUNK, HG_HEADS, HG_DK, HG_SUB
    NC = T // C

    def intra_slow(q, k, b, da, b_sc, k_sc):
        row, col, d, diag_valid = _hg_masks()
        a_blocks = [jnp.zeros((S, C), F32)]
        dq_blocks = [jnp.zeros((S, K), F32)]
        dk = jnp.zeros((C, K), F32)
        for i in range(1, C // S):
            r = b_sc[pl.ds(S * i - 1, 1), :]
            eq = jnp.exp(b[S * i:S * (i + 1)] - r)
            ek = jnp.exp(jnp.minimum(r - b, 0.0))
            qi = q[S * i:S * (i + 1)] * eq
            kk = k * ek
            a_blocks.append(_dot_nt(qi, kk))
            dai = jnp.where(col[S * i:S * (i + 1)] < S * i, da[S * i:S * (i + 1)], 0.0)
            dq_blocks.append(_dot_nn(dai, kk) * eq)
            dk = dk + _dot_tn(dai, qi) * ek
        dq = jnp.concatenate(dq_blocks, axis=0)
        a_off = jnp.where(col < (row // S) * S, jnp.concatenate(a_blocks, axis=0), 0.0)
        same_blk = (row // S == col // S).astype(BF16)
        tmod = (lax.broadcasted_iota(jnp.int32, (C, K), 0)) % S
        a_d = jnp.zeros((C, C), F32)
        for sp in range(S):
            bs = _hg_sel_rows(b_sc, sp)
            ks = _hg_sel_rows(k_sc, sp)
            e = jnp.where(tmod >= sp, jnp.exp(jnp.minimum(b - bs, 0.0)), 0.0)
            eks = e * ks
            a_d = jnp.where(d == sp, jnp.sum(q * eks, axis=-1, keepdims=True), a_d)
            dacol = jnp.sum(jnp.where(d == sp, da, 0.0), axis=-1, keepdims=True)
            dq = dq + dacol * eks
            wq = dacol * e * q
            wq_hi = wq.astype(BF16)
            wq_lo = (wq - wq_hi.astype(F32)).astype(BF16)
            blk_sum = (jnp.dot(same_blk, wq_hi, preferred_element_type=F32)
                       + jnp.dot(same_blk, wq_lo, preferred_element_type=F32))
            dk = dk + jnp.where(tmod == sp, blk_sum, 0.0)
        return a_off + jnp.where(diag_valid, a_d, 0.0), dq, dk

    def one_head(pre, v, lbv, st0, dst1, dout, b_sc, k_sc):
        sig, f, k, q, b, bc = pre
        ebc = jnp.exp(bc)
        eb = jnp.exp(b)
        ekb = jnp.exp(bc - b)
        qt = q * eb
        kb = k * ekb
        row = lax.broadcasted_iota(jnp.int32, (C, C), 0)
        col = lax.broadcasted_iota(jnp.int32, (C, C), 1)
        da = jnp.where(col <= row, _dot_nt(dout, v), 0.0)
        dkb = _dot_nn(v, dst1)
        new_ds = _dot_tn(dout, qt) + dst1 * ebc
        a, dq_i, dk_i = intra_slow(q, k, b, da, b_sc, k_sc)
        dq = _dot_nn(dout, st0) * eb + dq_i
        dk = dkb * ekb + dk_i
        dv = _dot_tn(a, dout) + _dot_nt(kb, dst1)
        extra = jnp.sum(dkb * kb, axis=0, keepdims=True) + ebc * jnp.sum(st0 * dst1, axis=0, keepdims=True)
        rowk = lax.broadcasted_iota(jnp.int32, (C, K), 0)
        db = q * dq - k * dk + jnp.where(rowk == C - 1, extra, 0.0)
        dg = _rcumsum_rows(db)
        df = dg / f - dk
        return (dq * (K ** -0.5), df * (1.0 - lbv) * sig * (1.0 - sig), dv,
                jnp.sum(df * (1.0 - sig), axis=0, keepdims=True), new_ds)

    def body(hq_ref, hf_ref, hi_ref, lb_ref, st_ref, do_ref, dq_ref, dhf_ref, dv_ref, dlb_ref, ds_sc, b_sc, k_sc):
        @pl.when(pl.program_id(0) == 0)
        def _():
            ds_sc[...] = jnp.zeros_like(ds_sc)
            dlb_ref[...] = jnp.zeros_like(dlb_ref)

        ds_all = ds_sc[...]
        dlb = jnp.zeros((1, H * K), F32)
        for j in reversed(range(P)):
            rows = slice(C * j, C * (j + 1))
            st_all = st_ref[j]
            res = []
            for h in range(H):
                sl = slice(K * h, K * (h + 1))
                pre = _hg_prep(hq_ref[rows, sl], hf_ref[rows, sl], lb_ref[:, sl], b_sc.at[j, h], k_sc.at[j, h])
                res.append(one_head(pre, hi_ref[rows, sl], lb_ref[:, sl], st_all[:, sl], ds_all[:, sl], do_ref[rows, sl],
                                    b_sc.at[j, h], k_sc.at[j, h]))
            cat = lambda i: jnp.concatenate([r[i] for r in res], axis=1)
            dq_ref[rows, :] = cat(0).astype(dq_ref.dtype)
            dhf_ref[rows, :] = cat(1).astype(dhf_ref.dtype)
            dv_ref[rows, :] = cat(2).astype(dv_ref.dtype)
            dlb = dlb + cat(3)
            ds_all = cat(4)
        dlb_ref[...] += dlb
        ds_sc[...] = ds_all

    P = HG_CHUNKS_PER_STEP
    NS = NC // P
    blk = pl.BlockSpec((P * C, H * K), lambda c: (NS - 1 - c, 0))
    par = pl.BlockSpec((1, H * K), lambda c: (0, 0))
    return pl.pallas_call(
        body, name=name, grid=(NS,),
        in_specs=[blk, blk, blk, par, pl.BlockSpec((P, K, H * K), lambda c: (NS - 1 - c, 0, 0)), blk],
        out_specs=[blk, blk, blk, par],
        out_shape=[_sds((T, H * K), BF16)] * 3 + [_sds((1, H * K), F32)],
        scratch_shapes=[pltpu.VMEM((K, H * K), F32), pltpu.VMEM((P, H, C, K), F32), pltpu.VMEM((P, H, C, K), F32)],
        compiler_params=_cp(("arbitrary",)),
    )(hq, hf, hi, lb, states, do)


ATT_STACK = ATT_GROUP


def _att_valid(n):
    R, B = ATT_STACK * ATT_BLOCK, ATT_BLOCK
    j = lax.broadcasted_iota(jnp.int32, (2 * B, R), 0)
    t = lax.broadcasted_iota(jnp.int32, (2 * B, R), 1) % B
    dist = t + B - j
    first_key = jnp.where(n > 0, 0, B)
    return jnp.logical_and(jnp.logical_and(dist >= 0, dist < B), j >= first_key)


def _att_load(cur_ref, prev_ref, ba_ref, h0):
    hd = ATT_HD
    kv = h0 // ATT_GROUP
    def cols(ref, c0):
        return ref[:, c0:c0 + hd] + ba_ref[:, c0:c0 + hd]
    qs = jnp.concatenate([cols(cur_ref, hd * (h0 + g)) for g in range(ATT_STACK)], axis=0)
    kc = jnp.concatenate([cols(prev_ref, ATT_Q_W + hd * kv), cols(cur_ref, ATT_Q_W + hd * kv)], axis=0)
    vc = jnp.concatenate([cols(prev_ref, ATT_Q_W + ATT_KV_W + hd * kv), cols(cur_ref, ATT_Q_W + ATT_KV_W + hd * kv)], axis=0)
    return qs, kc, vc


def _att_probs(qs, kc, valid, sink_ref, h0):
    scale = 1.0 / math.sqrt(ATT_HD)
    s = jnp.where(valid, _dot_nt(kc, qs) * scale, NEG)
    sink = jnp.concatenate([jnp.full((1, ATT_BLOCK), sink_ref[0, h0 + g], F32) for g in range(ATT_STACK)], axis=1)
    m = jnp.maximum(jnp.max(s, axis=0, keepdims=True), sink)
    p = jnp.exp(s - m)
    ps = jnp.exp(sink - m)
    inv = 1.0 / (jnp.sum(p, axis=0, keepdims=True) + ps)
    return p * inv, ps * inv


def _attn_fwd(att, b_attn, sinks, *, name, after=None):
    T = att.shape[0]
    B = ATT_BLOCK
    NB = T // B
    lead = [] if after is None else [after]

    def body(*refs):
        sink_ref, cur_ref, prev_ref, ba_ref, o_ref = refs[len(lead):]
        valid = _att_valid(pl.program_id(0))
        outs = []
        for h0 in range(0, ATT_HEADS, ATT_STACK):
            qs, kc, vc = _att_load(cur_ref, prev_ref, ba_ref, h0)
            prob, _ = _att_probs(qs, kc, valid, sink_ref, h0)
            o = _dot_tn(prob, vc)
            outs += [o[B * g:B * (g + 1)] for g in range(ATT_STACK)]
        o_ref[...] = jnp.concatenate(outs, axis=1)

    return pl.pallas_call(
        body, name=name, grid=(NB,),
        in_specs=[pl.BlockSpec(memory_space=pl.ANY)] * len(lead) + [
            pl.BlockSpec(memory_space=pltpu.SMEM),
            pl.BlockSpec((B, ATT_COLS), lambda n: (n, 0)),
            pl.BlockSpec((B, ATT_COLS), lambda n: (jnp.maximum(n - 1, 0), 0)),
            pl.BlockSpec((1, ATT_COLS), lambda n: (0, 0))],
        out_specs=pl.BlockSpec((B, ATT_Q_W), lambda n: (n, 0)),
        out_shape=_sds((T, ATT_Q_W), F32),
        compiler_params=_cp(("parallel",)),
    )(*lead, sinks, att, att, b_attn)


def _attn_bwd(att, b_attn, sinks, dmix, *, name):
    T = att.shape[0]
    B, hd = ATT_BLOCK, ATT_HD
    NB = T // B
    scale = 1.0 / math.sqrt(hd)

    def body(sink_ref, cur_ref, prev_ref, ba_ref, do_ref, daq_ref, dakv_ref, dsink_ref, dbq_ref, dbkv_ref, carry_sc):
        n = pl.program_id(0)

        @pl.when(n == 0)
        def _():
            carry_sc[...] = jnp.zeros_like(carry_sc)
            dsink_ref[...] = jnp.zeros_like(dsink_ref)
            dbq_ref[...] = jnp.zeros_like(dbq_ref)
            dbkv_ref[...] = jnp.zeros_like(dbkv_ref)

        @pl.when(n < NB)
        def _():
            valid = _att_valid(n)
            hrow = lax.broadcasted_iota(jnp.int32, (SUBLANES, 128), 0)
            dsink = jnp.zeros((SUBLANES, 128), F32)
            dqs = []
            dks = [jnp.zeros((2 * B, hd), F32)] * ATT_KV
            dvs = [jnp.zeros((2 * B, hd), F32)] * ATT_KV
            for h0 in range(0, ATT_HEADS, ATT_STACK):
                kv = h0 // ATT_GROUP
                qs, kc, vc = _att_load(cur_ref, prev_ref, ba_ref, h0)
                prob, psink = _att_probs(qs, kc, valid, sink_ref, h0)
                dout = jnp.concatenate([do_ref[:, hd * (h0 + g):hd * (h0 + g + 1)] for g in range(ATT_STACK)], axis=0)
                dp = _dot_nt(vc, dout)
                delta = jnp.sum(prob * dp, axis=0, keepdims=True)
                dsc = prob * (dp - delta) * scale
                dq = _dot_tn(dsc, kc)
                dks[kv] = dks[kv] + _dot_nn(dsc, qs)
                dvs[kv] = dvs[kv] + _dot_nn(prob, dout)
                dsk = psink * delta
                for g in range(ATT_STACK):
                    dqs.append(dq[B * g:B * (g + 1)])
                    tot = jnp.sum(dsk[:, B * g:B * (g + 1)], axis=1, keepdims=True)
                    dsink = dsink - jnp.where(hrow == h0 + g, tot, 0.0)
            daq = jnp.concatenate(dqs, axis=1).astype(daq_ref.dtype)
            daq_ref[...] = daq
            dsink_ref[...] += dsink
            dbq_ref[...] += jnp.sum(daq.astype(F32), axis=0, keepdims=True)
            done = carry_sc[...] + jnp.concatenate([d[:B] for d in dks + dvs], axis=1)
            dakv_ref[...] = done.astype(dakv_ref.dtype)
            dbkv_ref[...] += jnp.sum(done.astype(dakv_ref.dtype).astype(F32), axis=0, keepdims=True)
            carry_sc[...] = jnp.concatenate([d[B:] for d in dks + dvs], axis=1)

        @pl.when(n == NB)
        def _():
            done = carry_sc[...]
            dakv_ref[...] = done.astype(dakv_ref.dtype)
            dbkv_ref[...] += jnp.sum(done.astype(dakv_ref.dtype).astype(F32), axis=0, keepdims=True)

    cl = lambda n: jnp.minimum(n, NB - 1)
    return pl.pallas_call(
        body, name=name, grid=(NB + 1,),
        in_specs=[pl.BlockSpec(memory_space=pltpu.SMEM),
                  pl.BlockSpec((B, ATT_COLS), lambda n: (cl(n), 0)),
                  pl.BlockSpec((B, ATT_COLS), lambda n: (jnp.maximum(cl(n) - 1, 0), 0)),
                  pl.BlockSpec((1, ATT_COLS), lambda n: (0, 0)),
                  pl.BlockSpec((B, ATT_Q_W), lambda n: (cl(n), 0))],
        out_specs=[pl.BlockSpec((B, ATT_Q_W), lambda n: (cl(n), 0)),
                   pl.BlockSpec((B, 2 * ATT_KV_W), lambda n: (jnp.maximum(n - 1, 0), 0)),
                   pl.BlockSpec((SUBLANES, 128), lambda n: (0, 0)),
                   pl.BlockSpec((1, ATT_Q_W), lambda n: (0, 0)),
                   pl.BlockSpec((1, 2 * ATT_KV_W), lambda n: (0, 0))],
        out_shape=[_sds((T, ATT_Q_W), BF16), _sds((T, 2 * ATT_KV_W), BF16), _sds((SUBLANES, 128), F32),
                   _sds((1, ATT_Q_W), F32), _sds((1, 2 * ATT_KV_W), F32)],
        scratch_shapes=[pltpu.VMEM((B, 2 * ATT_KV_W), F32)],
        compiler_params=_cp(("arbitrary",)),
    )(sinks, att, att, b_attn, dmix)


def _silu_and_grad(x):
    sg = _sigmoid(x)
    return x * sg, sg * (1.0 + x * (1.0 - sg))


def _mix_fwd_fn(o_raw, hg, o_att, hgw):
    outs = []
    for h in range(HG_HEADS):
        sl = slice(HG_DK * h, HG_DK * (h + 1))
        silu, _ = _silu_and_grad(hg[:, sl])
        outs.append(_rms_fwd(o_raw[:, sl], hgw) * silu)
    outs.append(o_att)
    return (jnp.concatenate(outs, axis=1),)


def _mix_bwd_fn(o_raw, hg, dmix, hgw):
    dos, dhgs = [], []
    dw = jnp.zeros((1, HG_DK), F32)
    for h in range(HG_HEADS):
        sl = slice(HG_DK * h, HG_DK * (h + 1))
        silu, dsilu = _silu_and_grad(hg[:, sl])
        dy = dmix[:, sl]
        dhgs.append(dy * _rms_fwd(o_raw[:, sl], hgw) * dsilu)
        dx, dwh = _rms_bwd(o_raw[:, sl], hgw, dy * silu)
        dos.append(dx)
        dw = dw + dwh
    return jnp.concatenate(dos, axis=1), jnp.concatenate(dhgs, axis=1), dw


def _final_fn(h2, tgt, wf):
    d = h2.shape[1]
    err = _rms_fwd(h2, wf) - tgt
    loss_cols = (0.5 / d) * jnp.sum(err * err, axis=0, keepdims=True)
    dh2, dwf = _rms_bwd(h2, wf, err * (1.0 / d))
    return dh2, dh2, loss_cols, dwf


class _NoExchange:
    def __init__(self, weights):
        self.weights = weights

    def start(self):
        return None

    def w_in(self, after):
        return self.weights["w_in_t"]

    def mid(self, after):
        return None

    def rest(self, after):
        return self.weights

    def ffn_grads(self, gs):
        return None

    def ffn_grads_send(self, after):
        return None


def _local_step(x, tgt, p, ex):
    T, D = x.shape
    row = lambda n, dt: _sds((T, n), dt)
    acc = lambda n: _sds((1, n), F32)

    (u,) = _rowwise(lambda xv, w: (_rms_fwd(xv, w),), [_full(x)], [p["norm_mix_w"]], [row(D, BF16)], [], name="rms_mix",
                    after=ex.start())
    p = dict(p, w_in_t=ex.w_in(u))
    hq, hf, hi, hg, att = _mm_nt(u, p["w_in_t"], splits=[HG_W] * 4 + [ATT_COLS], out_dtype=F32, name="in_proj")
    o_raw, states = _hgrn_fwd(hq, hf, hi, p["lb"], name="hgrn_fwd")
    o_att = _attn_fwd(att, p["b_attn"], p["sinks"], name="attn_fwd", after=ex.mid(o_raw))
    p = dict(p, **ex.rest(o_att))
    def out_epilogue(prod, xv, w):
        h1v = prod + xv
        return h1v, _rms_fwd(h1v, w)

    h1, v, mix = _mm_nn(None, [p["w_out"]], name="mix_out_proj",
                        prologue=(lambda *a: _mix_fwd_fn(*a)[0], [o_raw, hg, o_att], [p["hg_norm_w"]], row(D, BF16)),
                        epilogue=(out_epilogue, [x], [p["norm_ffn_w"]], [row(D, F32), row(D, BF16)], []))
    gp, up, act = _ffn_in(v, p["w_gate_t"], p["w_up_t"], p["conv_w8"], p["conv_b"], name="ffn_in")
    def down_epilogue(prod, h1v, tgtv, wf):
        return _final_fn(prod + h1v, tgtv, wf)

    dh2, dh2_b, loss_cols, d_final = _mm_nn(
        [[act]], [p["w_down"]], name="down_proj_loss",
        epilogue=(down_epilogue, [h1, tgt], [p["final_norm_w"]], [row(D, F32), row(D, BF16)], [acc(D), acc(D)]))

    (dact,) = _mm_nt(dh2_b, p["w_down"], splits=[D_FF], out_dtype=F32, name="d_act")
    g_down = _mm_tn([act], dh2_b, name="g_down")
    dgp, dup, d_conv_w8, d_conv_b = _convact_bwd(gp, up, dact, p["conv_w8"], p["conv_b"], name="convact_bwd")
    g_gate_t = _mm_tn([dgp], v, name="g_gate")
    g_up_t = _mm_tn([dup], v, name="g_up")
    swapping = ex.ffn_grads([g_gate_t, g_up_t, g_down])

    def ffn_norm_bwd(dvv, hv, dh2v, w):
        dx, dw = _rms_bwd(hv, w, dvv)
        dh1v = dx + dh2v
        return dh1v, dh1v, dw

    dh1, dh1_b, d_norm_ffn = _mm_nn(
        [[dgp], [dup]], [p["w_gate_t"], p["w_up_t"]], name="d_v_norm", after=swapping,
        epilogue=(ffn_norm_bwd, [h1, dh2], [p["norm_ffn_w"]], [row(D, F32), row(D, BF16)], [acc(D)]))
    sent = ex.ffn_grads_send(dh1_b)
    def mix_bwd(dmixv, o_rawv, hgv, hgw):
        do_rawv, dhgv, dw = _mix_bwd_fn(o_rawv, hgv, dmixv[:, :HG_W], hgw)
        return do_rawv, dhgv, dmixv[:, HG_W:], dw

    do_raw, dhg, do_att, d_hg_norm = _mm_nn(
        [[dh1_b]], [p["w_out"]], name="d_mix_bwd", w_transposed=True, after=sent,
        epilogue=(mix_bwd, [o_raw, hg], [p["hg_norm_w"]], [row(HG_W, F32), row(HG_W, BF16), row(ATT_Q_W, F32)], [acc(HG_DK)]))
    g_out = _mm_tn([mix], dh1_b, name="g_out")
    daq, dakv, d_sinks8, d_bq, d_bkv = _attn_bwd(att, p["b_attn"], p["sinks"], do_att, name="attn_bwd")
    dhq, dhf, dhi, d_lb = _hgrn_bwd(hq, hf, hi, p["lb"], states, do_raw, name="hgrn_bwd")
    pieces = [dhq, dhf, dhi, dhg, daq, dakv]
    g_in_t = _mm_tn(pieces, u, name="g_in")

    def mix_norm_bwd(duv, xv, dh1v, w):
        dx, dw = _rms_bwd(xv, w, duv)
        return dx + dh1v, dw

    dx, d_norm_mix = _mm_nn([pieces], [p["w_in_t"]], name="d_u_norm",
                            epilogue=(mix_norm_bwd, [x, dh1], [p["norm_mix_w"]], [row(D, F32)], [acc(D)]))
    grads = dict(g_in_t=g_in_t, g_out=g_out, g_gate_t=g_gate_t, g_up_t=g_up_t, g_down=g_down,
                 norm_mix_w=d_norm_mix, b_attn=jnp.concatenate([d_bq, d_bkv], axis=1), lb=d_lb, hg_norm_w=d_hg_norm,
                 sinks8=d_sinks8, norm_ffn_w=d_norm_ffn, conv_w8=d_conv_w8, conv_b=d_conv_b, final_norm_w=d_final)
    return loss_cols, dx, grads


SLAB = (IN_COLS // N_CHIPS, D_FF // N_CHIPS, D_FF // N_CHIPS, D_FF // N_CHIPS, D_MODEL // N_CHIPS)
N_W = len(SLAB)
PACK_OFF = tuple(sum(SLAB[:i]) for i in range(N_W))
PACK_ROWS = sum(SLAB)
FULL_OFF = tuple(N_CHIPS * o for o in PACK_OFF)
FULL_ROWS = N_CHIPS * PACK_ROWS
HALF = tuple(s // 2 for s in SLAB)
HPACK_OFF = tuple(sum(HALF[:i]) for i in range(N_W))
HPACK_ROWS = sum(HALF)
HFULL_OFF = tuple(N_CHIPS * o for o in HPACK_OFF)
HFULL_ROWS = N_CHIPS * HPACK_ROWS
CHIP_FLIPS = ((1, 0), (0, 1), (1, 1))
N_DEV = 8
BF16_ROWS = 16
ANY = pl.BlockSpec(memory_space=pl.ANY)


def _pos():
    return lax.axis_index("x"), lax.axis_index("y"), lax.axis_index("c")


def _flip(v, f):
    return 1 - v if f else v


def _rcopy(src, dst, ssem, rsem, dev):
    return pltpu.make_async_remote_copy(src_ref=src, dst_ref=dst, send_sem=ssem, recv_sem=rsem, device_id=dev,
                                        device_id_type=pl.DeviceIdType.MESH)


def _rows(ref, start, n, align=None):
    if not isinstance(start, int):
        if align is None:
            align = SUBLANES * (4 // jnp.dtype(ref.dtype).itemsize)
        start = pl.multiple_of(start, align)
    return ref.at[pl.ds(start, n), :]


FFN_W = (1, 2, 3)
N_PEER = 1 + len(CHIP_FLIPS)
HBM = pl.BlockSpec(memory_space=pltpu.HBM)
SEM = pl.BlockSpec(memory_space=pltpu.SEMAPHORE)
EFFECT = pltpu.SideEffectType.DATAFLOW_SIDE_EFFECTING
LANES = 128


def _sent_rows(k, w, c):
    return (0, SLAB[w]) if k == 0 else (c * HALF[w], HALF[w])


def _gather_start(pack, cw8):
    D = pack.shape[1]
    lands = [lax.empty((N_CHIPS * SLAB[0], D), pack.dtype), lax.empty((3 * N_CHIPS * SLAB[1], D), pack.dtype),
             lax.empty((N_CHIPS * SLAB[4], D), pack.dtype), lax.empty((N_CHIPS,) + cw8.shape, cw8.dtype)]
    bufs = [pack, cw8] + lands

    def body(pack_ref, cw_ref, l_in, l_ffn, l_out, l_cw, *rest):
        in_send, in_recv, out_send, out_recv, ffn_send, ffn_recv = rest[:6]
        token = rest[-1]
        x, y, c = _pos()
        q = 2 * x + y
        peers = _gather_peers(x, y, c)

        def send(k, peer, w, land, base, ssem, rsem):
            r0, n = _sent_rows(k, w, c)
            _rcopy(_rows(pack_ref, PACK_OFF[w] + r0, n), _rows(land, base + q * SLAB[w] + r0, n), ssem, rsem, peer).start()

        for k, peer in enumerate(peers):
            send(k, peer, 0, l_in, 0, in_send.at[k], in_recv.at[k])
        for k, peer in enumerate(peers):
            send(k, peer, 4, l_out, 0, out_send.at[k], out_recv.at[k])
            _rcopy(cw_ref, l_cw.at[q], out_send.at[N_PEER + k], out_recv.at[N_PEER + k], peer).start()
        for j, w in enumerate(FFN_W):
            for k, peer in enumerate(peers):
                send(k, peer, w, l_ffn, j * N_CHIPS * SLAB[w], ffn_send.at[k], ffn_recv.at[k])
        token[...] = jnp.zeros_like(token)

    n_sem = (N_PEER, N_PEER, 2 * N_PEER, 2 * N_PEER, N_PEER, N_PEER)
    outs = pl.pallas_call(
        body, name="gather_start", in_specs=[HBM] * len(bufs),
        out_specs=[SEM] * len(n_sem) + [HBM] * len(bufs) + [pl.BlockSpec(memory_space=pltpu.VMEM)],
        out_shape=[pltpu.SemaphoreType.DMA((n,)) for n in n_sem]
        + [pltpu.HBM(b.shape, b.dtype) for b in bufs] + [TOKEN],
        input_output_aliases={i: len(n_sem) + i for i in range(len(bufs))},
        compiler_params=pltpu.CompilerParams(has_side_effects=EFFECT),
    )(*[pltpu.with_memory_space_constraint(b, pltpu.HBM) for b in bufs])
    bufs_out = outs[len(n_sem):]
    return dict(in_sems=outs[0:2], out_sems=outs[2:4], ffn_sems=outs[4:6], pack=bufs_out[0], cw=bufs_out[1], l_in=bufs_out[2],
                l_ffn=bufs_out[3], l_out=bufs_out[4], l_cw=bufs_out[5], token=bufs_out[6])


def _gather_peers(x, y, c):
    return [(x, y, 1 - c)] + [(_flip(x, fx), _flip(y, fy), c) for fx, fy in CHIP_FLIPS]


def _gather_wait_in(g, after):
    def body(pack_ref, l_in, send, recv, after_ref, pack_out, l_out):
        for k, peer in enumerate(_gather_peers(*_pos())):
            n = _sent_rows(k, 0, 0)[1]
            cp = _rcopy(_rows(pack_ref, PACK_OFF[0], n), _rows(l_in, 0, n), send.at[k], recv.at[k], peer)
            cp.wait_send()
            cp.wait_recv()

    return pl.pallas_call(
        body, name="gather_wait_in", in_specs=[HBM, HBM, SEM, SEM, ANY], out_specs=[HBM, HBM],
        out_shape=[pltpu.HBM(g["pack"].shape, g["pack"].dtype), pltpu.HBM(g["l_in"].shape, g["l_in"].dtype)],
        input_output_aliases={0: 0, 1: 1}, compiler_params=pltpu.CompilerParams(has_side_effects=EFFECT),
    )(g["pack"], g["l_in"], *g["in_sems"], after)


def _gather_wait_rest(g, pack, after):
    def body(pack_ref, cw_ref, l_ffn, l_out, l_cw, o_send, o_recv, f_send, f_recv, after_ref, o_ffn, o_out, o_cw):
        for k, peer in enumerate(_gather_peers(*_pos())):
            n_out = _sent_rows(k, 4, 0)[1]
            n_ffn = len(FFN_W) * _sent_rows(k, FFN_W[0], 0)[1]
            for cp in (_rcopy(_rows(pack_ref, PACK_OFF[4], n_out), _rows(l_out, 0, n_out), o_send.at[k], o_recv.at[k], peer),
                       _rcopy(cw_ref, l_cw.at[0], o_send.at[N_PEER + k], o_recv.at[N_PEER + k], peer),
                       _rcopy(_rows(pack_ref, PACK_OFF[FFN_W[0]], n_ffn), _rows(l_ffn, 0, n_ffn), f_send.at[k], f_recv.at[k], peer)):
                cp.wait_send()
                cp.wait_recv()

    ins = [pack, g["cw"], g["l_ffn"], g["l_out"], g["l_cw"]]
    return pl.pallas_call(
        body, name="gather_wait_rest", in_specs=[HBM] * 5 + [SEM] * 4 + [ANY], out_specs=[HBM] * 3,
        out_shape=[pltpu.HBM(b.shape, b.dtype) for b in ins[2:]],
        input_output_aliases={2: 0, 3: 1, 4: 2}, compiler_params=pltpu.CompilerParams(has_side_effects=EFFECT),
    )(*ins, *g["out_sems"], *g["ffn_sems"], after)


FWD_IN = ((0, 0, 0),)
FWD_REST = tuple((0, w, j * N_CHIPS * SLAB[w]) for j, w in enumerate(FFN_W)) + ((1, 4, 0),)


def _forward_copies(layout, src, dst, send_sems, recv_sems):
    x, y, c = _pos()
    sib = (x, y, 1 - c)
    cps = []
    for fx, fy in CHIP_FLIPS:
        qa = 2 * _flip(x, fx) + _flip(y, fy)
        for bi, w, base in layout:
            r0 = base + qa * SLAB[w] + c * HALF[w]
            cps.append(_rcopy(_rows(src[bi], r0, HALF[w]), _rows(dst[bi], r0, HALF[w]),
                              send_sems.at[len(cps)], recv_sems.at[len(cps)], sib))
    return cps


def _forward_in(l_in):
    n = len(CHIP_FLIPS) * len(FWD_IN)

    def body(in_ref, out_ref, send_sems, recv_sems):
        cps = _forward_copies(FWD_IN, [in_ref], [out_ref], send_sems, recv_sems)
        for cp in cps:
            cp.start()
        for cp in cps:
            cp.wait_recv()
        for cp in cps:
            cp.wait_send()

    return pl.pallas_call(
        body, name="forward_in", in_specs=[ANY], out_specs=ANY, out_shape=_sds(l_in.shape, l_in.dtype),
        input_output_aliases={0: 0},
        scratch_shapes=[pltpu.SemaphoreType.DMA((n,)), pltpu.SemaphoreType.DMA((n,))],
    )(l_in)


def _forward_rest_start(l_ffn, l_out):
    n = len(CHIP_FLIPS) * len(FWD_REST)
    bufs = [l_ffn, l_out]

    def body(a_ref, b_ref, send_sems, recv_sems, a_out, b_out, token):
        for cp in _forward_copies(FWD_REST, [a_ref, b_ref], [a_ref, b_ref], send_sems, recv_sems):
            cp.start()
        token[...] = jnp.zeros_like(token)

    outs = pl.pallas_call(
        body, name="forward_rest_start", in_specs=[HBM] * 2,
        out_specs=[SEM, SEM, HBM, HBM, pl.BlockSpec(memory_space=pltpu.VMEM)],
        out_shape=[pltpu.SemaphoreType.DMA((n,)), pltpu.SemaphoreType.DMA((n,))]
        + [pltpu.HBM(b.shape, b.dtype) for b in bufs] + [TOKEN],
        input_output_aliases={0: 2, 1: 3}, compiler_params=pltpu.CompilerParams(has_side_effects=EFFECT),
    )(*[pltpu.with_memory_space_constraint(b, pltpu.HBM) for b in bufs])
    return dict(sems=outs[0:2], bufs=outs[2:4], token=outs[4])


def _forward_rest_wait(s, after):
    def body(a_ref, b_ref, send_sems, recv_sems, after_ref, a_out, b_out):
        for cp in _forward_copies(FWD_REST, [a_ref, b_ref], [a_ref, b_ref], send_sems, recv_sems):
            cp.wait_send()
            cp.wait_recv()

    return pl.pallas_call(
        body, name="forward_rest_wait", in_specs=[HBM, HBM, SEM, SEM, ANY], out_specs=[HBM, HBM],
        out_shape=[pltpu.HBM(b.shape, b.dtype) for b in s["bufs"]],
        input_output_aliases={0: 0, 1: 1}, compiler_params=pltpu.CompilerParams(has_side_effects=EFFECT),
    )(*s["bufs"], *s["sems"], after)


def _exchange_halves(ws, gs, small, *, name):
    D = gs[0].shape[1]
    n = len(ws)
    has_small = small is not None

    def body(*refs):
        g = refs[:n]
        t = refs[n + has_small:2 * n + has_small]
        sems = refs[2 * n + 2 * has_small:]
        d2d_send, d2d_recv = sems[0], sems[1]
        x, y, c = _pos()
        sib = (x, y, 1 - c)
        drains = []
        for i, w in enumerate(ws):
            h = HALF[w]
            for qq in range(N_CHIPS):
                _rcopy(_rows(g[i], qq * SLAB[w] + (1 - c) * h, h), _rows(t[i], qq * h, h),
                       d2d_send.at[i], d2d_recv.at[i], sib).start()
            drains.append(_rcopy(t[i], t[i], d2d_send.at[i], d2d_recv.at[i], sib))
        if has_small:
            small_ref, sall_ref = refs[n], refs[2 * n + 1]
            sm_send, sm_recv, loc_sem = sems[2], sems[3], sems[4]
            me = 4 * x + 2 * y + c
            own_small = pltpu.make_async_copy(small_ref, sall_ref.at[me], loc_sem)
            own_small.start()
            for f in range(1, N_DEV):
                peer = (_flip(x, f & 4), _flip(y, f & 2), _flip(c, f & 1))
                cp = _rcopy(small_ref, sall_ref.at[me], sm_send.at[f - 1], sm_recv.at[f - 1], peer)
                cp.start()
                drains.append(cp)
        for d in drains:
            d.wait_recv()
        for d in drains:
            d.wait_send()
        if has_small:
            own_small.wait()

    out_shape = [_sds((N_CHIPS * HALF[w], D), gs[0].dtype) for w in ws]
    scratch = [pltpu.SemaphoreType.DMA((n,)), pltpu.SemaphoreType.DMA((n,))]
    if has_small:
        out_shape.append(_sds((N_DEV,) + small.shape, F32))
        scratch += [pltpu.SemaphoreType.DMA((N_DEV - 1,)), pltpu.SemaphoreType.DMA((N_DEV - 1,)), pltpu.SemaphoreType.DMA]
    return pl.pallas_call(
        body, name=name, in_specs=[ANY] * (n + has_small), out_specs=[ANY] * (n + has_small),
        out_shape=out_shape, scratch_shapes=scratch,
    )(*gs, *([small] if has_small else []))


def _halves_copies(ws, g, t, send_sems, recv_sems):
    x, y, c = _pos()
    sib = (x, y, 1 - c)
    cps = []
    for i, w in enumerate(ws):
        h = HALF[w]
        for qq in range(N_CHIPS):
            cps.append(_rcopy(_rows(g[i], qq * SLAB[w] + (1 - c) * h, h), _rows(t[i], qq * h, h),
                              send_sems.at[N_CHIPS * i + qq], recv_sems.at[N_CHIPS * i + qq], sib))
    return cps


def _halves_start(ws, gs, *, name):
    D = gs[0].shape[1]
    n = len(ws)
    bufs = list(gs) + [lax.empty((N_CHIPS * HALF[w], D), gs[0].dtype) for w in ws]

    def body(*refs):
        for cp in _halves_copies(ws, refs[:n], refs[n:2 * n], refs[2 * n], refs[2 * n + 1]):
            cp.start()
        refs[-1][...] = jnp.zeros_like(refs[-1])

    outs = pl.pallas_call(
        body, name=name, in_specs=[HBM] * (2 * n),
        out_specs=[SEM, SEM] + [HBM] * (2 * n) + [pl.BlockSpec(memory_space=pltpu.VMEM)],
        out_shape=[pltpu.SemaphoreType.DMA((N_CHIPS * n,)), pltpu.SemaphoreType.DMA((N_CHIPS * n,))]
        + [pltpu.HBM(b.shape, b.dtype) for b in bufs] + [TOKEN],
        input_output_aliases={i: 2 + i for i in range(2 * n)},
        compiler_params=pltpu.CompilerParams(has_side_effects=EFFECT),
    )(*[pltpu.with_memory_space_constraint(b, pltpu.HBM) for b in bufs])
    return dict(sems=outs[0:2], gs=outs[2:2 + n], theirs=outs[2 + n:2 + 2 * n], token=outs[-1])


def _halves_wait(ws, s, after, *, name):
    n = len(ws)

    def body(*refs):
        for cp in _halves_copies(ws, refs[:n], refs[n:2 * n], refs[2 * n], refs[2 * n + 1]):
            cp.wait_send()
            cp.wait_recv()

    bufs = list(s["gs"]) + list(s["theirs"])
    outs = pl.pallas_call(
        body, name=name, in_specs=[HBM] * (2 * n) + [SEM, SEM, ANY], out_specs=[HBM] * (2 * n),
        out_shape=[pltpu.HBM(b.shape, b.dtype) for b in bufs],
        input_output_aliases={i: i for i in range(2 * n)},
        compiler_params=pltpu.CompilerParams(has_side_effects=EFFECT),
    )(*bufs, *s["sems"], after)
    return outs[:n], outs[n:]


REDUCE_SPLIT = 2


def _chip_partial(ws, gs, theirs, *, name, out_dtype=F32):
    D = gs[0].shape[1]
    n = len(ws)

    def body(*refs):
        for i in range(n):
            refs[2 * n + i][...] = (refs[i][...].astype(F32) + refs[n + i][...].astype(F32)).astype(out_dtype)

    blk = [HALF[w] // REDUCE_SPLIT for w in ws]
    mine = [pl.BlockSpec((b, D), lambda qq, j: ((2 * qq + lax.axis_index("c")) * REDUCE_SPLIT + j, 0)) for b in blk]
    flat = [pl.BlockSpec((b, D), lambda qq, j: (qq * REDUCE_SPLIT + j, 0)) for b in blk]
    return pl.pallas_call(
        body, name=name, grid=(N_CHIPS, REDUCE_SPLIT), in_specs=mine + flat, out_specs=flat,
        out_shape=[_sds((N_CHIPS * HALF[w], D), out_dtype) for w in ws],
        compiler_params=_cp(("parallel", "parallel")),
    )(*gs, *theirs)


def _partial_copies(ws, part, got, send_sems, recv_sems):
    x, y, c = _pos()
    cps = []
    for k, (fx, fy) in enumerate(CHIP_FLIPS):
        peer = (_flip(x, fx), _flip(y, fy), c)
        qp = 2 * _flip(x, fx) + _flip(y, fy)
        for i, w in enumerate(ws):
            cps.append(_rcopy(_rows(part[i], qp * HALF[w], HALF[w]), _rows(got[i], k * HALF[w], HALF[w]),
                              send_sems.at[len(ws) * k + i], recv_sems.at[len(ws) * k + i], peer))
    return cps


def _send_chip_partials(ws, parts, *, name):
    D = parts[0].shape[1]
    n = len(ws)

    def body(*refs):
        cps = _partial_copies(ws, refs[:n], refs[n:2 * n], refs[2 * n], refs[2 * n + 1])
        for cp in cps:
            cp.start()
        for cp in cps:
            cp.wait_recv()
        for cp in cps:
            cp.wait_send()

    return pl.pallas_call(
        body, name=name, in_specs=[ANY] * n, out_specs=[ANY] * n,
        out_shape=[_sds((len(CHIP_FLIPS) * HALF[w], D), parts[0].dtype) for w in ws],
        scratch_shapes=[pltpu.SemaphoreType.DMA((len(CHIP_FLIPS) * n,)), pltpu.SemaphoreType.DMA((len(CHIP_FLIPS) * n,))],
    )(*parts)


def _send_start(ws, parts, *, name):
    D = parts[0].shape[1]
    n = len(ws)
    bufs = list(parts) + [lax.empty((len(CHIP_FLIPS) * HALF[w], D), parts[0].dtype) for w in ws]

    def body(*refs):
        send_sems, recv_sems = refs[2 * n], refs[2 * n + 1]
        for cp in _partial_copies(ws, refs[:n], refs[n:2 * n], send_sems, recv_sems):
            cp.start()
        refs[-1][...] = jnp.zeros_like(refs[-1])

    outs = pl.pallas_call(
        body, name=name, in_specs=[HBM] * (2 * n),
        out_specs=[SEM, SEM] + [HBM] * (2 * n) + [pl.BlockSpec(memory_space=pltpu.VMEM)],
        out_shape=[pltpu.SemaphoreType.DMA((len(CHIP_FLIPS) * n,)), pltpu.SemaphoreType.DMA((len(CHIP_FLIPS) * n,))]
        + [pltpu.HBM(b.shape, b.dtype) for b in bufs] + [TOKEN],
        input_output_aliases={i: 2 + i for i in range(2 * n)},
        compiler_params=pltpu.CompilerParams(has_side_effects=EFFECT),
    )(*[pltpu.with_memory_space_constraint(b, pltpu.HBM) for b in bufs])
    return dict(sems=outs[0:2], parts=outs[2:2 + n], got=outs[2 + n:2 + 2 * n], token=outs[-1])


def _send_wait(ws, s, after, *, name):
    n = len(ws)

    def body(*refs):
        for cp in _partial_copies(ws, refs[:n], refs[n:2 * n], refs[2 * n], refs[2 * n + 1]):
            cp.wait_send()
            cp.wait_recv()

    bufs = list(s["parts"]) + list(s["got"])
    outs = pl.pallas_call(
        body, name=name, in_specs=[HBM] * (2 * n) + [SEM, SEM] + [ANY] * len(after), out_specs=[HBM] * (2 * n),
        out_shape=[pltpu.HBM(b.shape, b.dtype) for b in bufs],
        input_output_aliases={i: i for i in range(2 * n)},
        compiler_params=pltpu.CompilerParams(has_side_effects=EFFECT),
    )(*bufs, *s["sems"], *after)
    return outs[:n], outs[n:]


def _chip_reduce(ws, parts, got, *, name, after=None):
    D = parts[0].shape[1]
    nk = len(CHIP_FLIPS)
    n = len(ws)
    extra = [] if after is None else [after]

    def body(*refs):
        refs = refs[len(extra):]
        outs = refs[(1 + nk) * n:]
        for i in range(n):
            acc = refs[i][...].astype(F32)
            for k in range(nk):
                acc = acc + refs[n * (1 + k) + i][...].astype(F32)
            outs[i][...] = acc

    blk = [HALF[w] // REDUCE_SPLIT for w in ws]

    def q_idx(j):
        return (2 * lax.axis_index("x") + lax.axis_index("y")) * REDUCE_SPLIT + j

    in_specs = [pl.BlockSpec((b, D), lambda j: (q_idx(j), 0)) for b in blk]
    for k in range(nk):
        in_specs += [pl.BlockSpec((b, D), functools.partial(lambda j, k: (k * REDUCE_SPLIT + j, 0), k=k)) for b in blk]
    out_specs = [pl.BlockSpec((b, D), lambda j: (lax.axis_index("c") * REDUCE_SPLIT + j, 0)) for b in blk]
    return pl.pallas_call(
        body, name=name, grid=(REDUCE_SPLIT,), in_specs=[ANY] * len(extra) + in_specs, out_specs=out_specs,
        out_shape=[_sds((SLAB[w], D), F32) for w in ws],
        compiler_params=_cp(("parallel",)),
    )(*extra, *parts, *[g for _ in range(nk) for g in got])


def _exchange_reduced(ws, shards, *, name):
    n = len(ws)

    def body(*refs):
        ins, outs = refs[:n], refs[n:2 * n]
        send_sems, recv_sems = refs[2 * n], refs[2 * n + 1]
        x, y, c = _pos()
        sib = (x, y, 1 - c)
        cps = []
        for i, w in enumerate(ws):
            cp = _rcopy(_rows(ins[i], c * HALF[w], HALF[w]), _rows(outs[i], c * HALF[w], HALF[w]),
                        send_sems.at[i], recv_sems.at[i], sib)
            cp.start()
            cps.append(cp)
        for cp in cps:
            cp.wait_recv()
        for cp in cps:
            cp.wait_send()

    return pl.pallas_call(
        body, name=name, in_specs=[ANY] * n, out_specs=[ANY] * n,
        out_shape=[_sds(s.shape, s.dtype) for s in shards], input_output_aliases={i: i for i in range(n)},
        scratch_shapes=[pltpu.SemaphoreType.DMA((n,)), pltpu.SemaphoreType.DMA((n,))],
    )(*shards)


def _adamw_fn(w, g, m, v):
    m2 = ADAM_B1 * m + (1.0 - ADAM_B1) * g
    v2 = ADAM_B2 * v + (1.0 - ADAM_B2) * (g * g)
    m_hat = m2 / (1.0 - ADAM_B1 ** ADAM_STEP)
    v_hat = v2 / (1.0 - ADAM_B2 ** ADAM_STEP)
    return -ADAM_LR * (m_hat / (jnp.sqrt(v_hat) + ADAM_EPS) + ADAM_WD * w), m2, v2


def _adamw(w, g, m, v, *, name):
    shp = _sds(w.shape, F32)
    rows = w.shape[0]
    tm = max(t for t in range(SUBLANES, 512 + 1, SUBLANES) if rows % t == 0)
    return _rowwise(lambda wv, gv, mv, vv: (gv, *_adamw_fn(wv, gv, mv, vv)), [_full(w), _full(g), _full(m), _full(v)], [],
                    [shp] * 4, [], name=name, tm=tm)


SMALL_SEGS = (("loss", 8), ("norm_mix_w", 8), ("b_attn", 8), ("lb_logits", 8), ("hg_norm_w", 8), ("sinks", 8),
              ("norm_ffn_w", 8), ("conv_w", 72), ("conv_b", 24), ("final_norm_w", 8))
SMALL_OFF = {n: sum(r for _, r in SMALL_SEGS[:i]) for i, (n, _) in enumerate(SMALL_SEGS)}
SMALL_ROWS = sum(r for _, r in SMALL_SEGS)
LANES = 128


def _pack_small(parts):
    segs = []
    for n, r in SMALL_SEGS:
        a = parts.get(n)
        flat = jnp.zeros((0,), F32) if a is None else a.reshape(-1).astype(F32)
        segs.append(jnp.pad(flat, (0, r * LANES - flat.shape[0])).reshape(r, LANES))
    return jnp.concatenate(segs, axis=0)


def _unpack_small(pack, n, shape):
    size = math.prod(shape)
    r0 = SMALL_OFF[n]
    return pack[r0:r0 + dict(SMALL_SEGS)[n]].reshape(-1)[:size].reshape(shape)


def _small_update(sall, wp, mp, vp, *, after):
    R = SMALL_ROWS
    r_lb = SMALL_OFF["lb_logits"]

    def body(after_ref, s_ref, w_ref, m_ref, v_ref, g_ref, d_ref, m2_ref, v2_ref, loss_ref):
        g = s_ref[0]
        for i in range(1, N_DEV):
            g = g + s_ref[i]
        tot = jnp.sum(jnp.sum(g[0:8], axis=1, keepdims=True), axis=0, keepdims=True)
        loss_ref[...] = jnp.broadcast_to(tot, loss_ref.shape)
        lg = w_ref[r_lb:r_lb + 8, :]
        p0 = _sigmoid(lg - pltpu.roll(lg, 4, 0))
        d = g[r_lb:r_lb + 8]
        d = d + pltpu.roll(d, 4, 0)
        sign = jnp.where(lax.broadcasted_iota(jnp.int32, d.shape, 0) < 4, 1.0, -1.0)
        g = jnp.concatenate([g[:r_lb], sign * d * p0 * (1.0 - p0), g[r_lb + 8:]], axis=0)
        g_ref[...] = g
        d_ref[...], m2_ref[...], v2_ref[...] = _adamw_fn(w_ref[...], g, m_ref[...], v_ref[...])

    full = pl.BlockSpec((R, LANES), lambda: (0, 0))
    return pl.pallas_call(
        body, name="small_update",
        in_specs=[ANY, pl.BlockSpec((N_DEV, R, LANES), lambda: (0, 0, 0)), full, full, full],
        out_specs=[full, full, full, full, pl.BlockSpec((8, LANES), lambda: (0, 0))],
        out_shape=[_sds((R, LANES), F32)] * 4 + [_sds((8, LANES), F32)],
        compiler_params=_cp(),
    )(after, sall, wp, mp, vp)


def _lb_fwd(lb_logits):
    n = lb_logits.shape[1]

    def body(l_ref, o_ref):
        o_ref[...] = _sigmoid(l_ref[0:1, :] - l_ref[1:2, :])

    return pl.pallas_call(body, name="lb_fwd", out_shape=jax.ShapeDtypeStruct((1, n), F32), compiler_params=_cp())(lb_logits)


class _MeshExchange:
    def __init__(self, pack, cw8):
        self.gather = _gather_start(pack, cw8)
        self.sent = None
        self.conv_w8 = None

    def start(self):
        return self.gather["token"]

    def w_in(self, after):
        self.pack, l_in = _gather_wait_in(self.gather, after)
        return (_forward_in(l_in), N_CHIPS * SLAB[0], 0)

    def mid(self, after):
        l_ffn, l_out, l_cw = _gather_wait_rest(self.gather, self.pack, after)
        self.conv_w8 = jnp.concatenate([l_cw[i] for i in range(N_CHIPS)], axis=1)
        self.passing = _forward_rest_start(l_ffn, l_out)
        return self.passing["token"]

    def rest(self, after):
        l_ffn, l_out = _forward_rest_wait(self.passing, after)
        rows = N_CHIPS * SLAB[FFN_W[0]]
        return dict(w_gate_t=(l_ffn, rows, 0), w_up_t=(l_ffn, rows, 1), w_down=(l_ffn, rows, 2),
                    w_out=(l_out, N_CHIPS * SLAB[4], 0), conv_w8=self.conv_w8)

    def ffn_grads(self, gs):
        self.swap = _halves_start(FFN_W, gs, name="halves_ffn_start")
        return self.swap["token"]

    def ffn_grads_send(self, after):
        gs, theirs = _halves_wait(FFN_W, self.swap, after, name="halves_ffn_wait")
        parts = _chip_partial(FFN_W, gs, theirs, name="chip_partial_ffn", out_dtype=BF16)
        self.sent = _send_start(FFN_W, parts, name="send_ffn_start")
        return self.sent["token"]


def kernel(x, norm_mix_w, w_in, b_attn, lb_logits, hg_norm_w, sinks, w_out, norm_ffn_w, w_gate, w_up, conv_w, conv_b, w_down, final_norm_w, loss_target, m_norm_mix_w, m_w_in, m_b_attn, m_lb_logits, m_hg_norm_w, m_sinks, m_w_out, m_norm_ffn_w, m_w_gate, m_w_up, m_conv_w, m_conv_b, m_w_down, m_final_norm_w, v_norm_mix_w, v_w_in, v_b_attn, v_lb_logits, v_hg_norm_w, v_sinks, v_w_out, v_norm_ffn_w, v_w_gate, v_w_up, v_conv_w, v_conv_b, v_w_down, v_final_norm_w):
    D = D_MODEL
    q = 2 * lax.axis_index("x") + lax.axis_index("y")
    ccols = D_FF // N_CHIPS

    pack = jnp.concatenate([w_in[0].T, w_gate[0].T, w_up[0].T, w_down[0], w_out[0]], axis=0).astype(BF16)
    cw8 = jnp.concatenate([conv_w[0], jnp.zeros((SUBLANES - 3, ccols), F32)], axis=0)
    ex = _MeshExchange(pack, cw8)
    p = dict(norm_mix_w=norm_mix_w, b_attn=b_attn, lb=_lb_fwd(lb_logits), hg_norm_w=hg_norm_w, sinks=sinks,
             norm_ffn_w=norm_ffn_w, conv_b=conv_b, final_norm_w=final_norm_w.reshape(1, D))
    loss_cols, dx, g = _local_step(x[0], loss_target[0], p, ex)
    conv_w8 = ex.conv_w8

    small = _pack_small(dict(loss=loss_cols, norm_mix_w=g["norm_mix_w"], b_attn=g["b_attn"], lb_logits=g["lb"],
                             hg_norm_w=g["hg_norm_w"], sinks=g["sinks8"], norm_ffn_w=g["norm_ffn_w"],
                             conv_w=g["conv_w8"][:3], conv_b=g["conv_b"], final_norm_w=g["final_norm_w"]))
    parts_ffn, got_ffn = _send_wait(FFN_W, ex.sent, [dx], name="send_ffn_wait")
    late = (0, 4)
    gs = [g["g_in_t"], g["g_out"]]
    *theirs, sall = _exchange_halves(late, gs, small, name="exchange_halves_late")
    parts_late = _chip_partial(late, gs, theirs, name="chip_partial_late", out_dtype=BF16)
    sent_late = _send_start(late, parts_late, name="send_late_start")
    big = {}

    def finish(ws, parts, got, specs, tag, after):
        shards = _exchange_reduced(ws, _chip_reduce(ws, parts, got, name="chip_reduce_" + tag, after=after),
                                   name="exchange_reduced_" + tag)
        deltas = []
        for gw, (n, w, m, v, tr) in zip(shards, specs):
            view = (lambda a: a[0].T) if tr else (lambda a: a[0])
            back = (lambda a: a.T[None]) if tr else (lambda a: a[None])
            res = _adamw(view(w), gw, view(m), view(v), name="adamw_" + n)
            big[n] = tuple(back(r) for r in res)
            deltas.append(res[1])
        return deltas

    done_ffn = finish(FFN_W, parts_ffn, got_ffn, (("w_gate", w_gate, m_w_gate, v_w_gate, True),
                                                  ("w_up", w_up, m_w_up, v_w_up, True),
                                                  ("w_down", w_down, m_w_down, v_w_down, False)), "ffn", sent_late["token"])

    def place(a):
        return lax.dynamic_update_slice(jnp.zeros((3, D_FF), F32), a[0], (0, q * ccols))

    def small_pack(ws, cw):
        nm, ba, lbl, hg, sk, nf, cb, fn = ws
        return _pack_small(dict(norm_mix_w=nm, b_attn=ba, lb_logits=lbl, hg_norm_w=hg,
                                sinks=jnp.broadcast_to(sk.reshape(ATT_HEADS, 1), (ATT_HEADS, LANES)), norm_ffn_w=nf,
                                conv_w=cw, conv_b=cb, final_norm_w=fn))

    wp = small_pack((norm_mix_w, b_attn, lb_logits, hg_norm_w, sinks, norm_ffn_w, conv_b, final_norm_w), conv_w8[:3])
    mp = small_pack((m_norm_mix_w, m_b_attn, m_lb_logits, m_hg_norm_w, m_sinks, m_norm_ffn_w, m_conv_b, m_final_norm_w),
                    place(m_conv_w))
    vp = small_pack((v_norm_mix_w, v_b_attn, v_lb_logits, v_hg_norm_w, v_sinks, v_norm_ffn_w, v_conv_b, v_final_norm_w),
                    place(v_conv_w))
    outs = _small_update(sall, wp, mp, vp, after=sent_late["token"])
    loss = outs[4][0, 0]
    parts_late, got_late = _send_wait(late, sent_late, [*done_ffn, outs[4]], name="send_late_wait")
    finish(late, parts_late, got_late, (("w_in", w_in, m_w_in, v_w_in, True), ("w_out", w_out, m_w_out, v_w_out, False)),
           "late", None)

    def small_out(pk, n, ref):
        if n == "sinks":
            return pk[SMALL_OFF[n]:SMALL_OFF[n] + ATT_HEADS, 0].reshape(ref.shape)
        if n == "conv_w":
            full = _unpack_small(pk, n, (3, D_FF))
            return lax.dynamic_slice(full, (0, q * ccols), (3, ccols))[None]
        return _unpack_small(pk, n, ref.shape)

    refs = dict(norm_mix_w=norm_mix_w, b_attn=b_attn, lb_logits=lb_logits, hg_norm_w=hg_norm_w, sinks=sinks,
                norm_ffn_w=norm_ffn_w, conv_w=conv_w, conv_b=conv_b, final_norm_w=final_norm_w)
    order = ("norm_mix_w", "w_in", "b_attn", "lb_logits", "hg_norm_w", "sinks", "w_out", "norm_ffn_w", "w_gate", "w_up",
             "conv_w", "conv_b", "w_down", "final_norm_w")
    res = [loss, dx[None]]
    for k in range(4):
        for n in order:
            res.append(big[n][k] if n in big else small_out(outs[k], n, refs[n]))
    return tuple(res)
```

```python
import functools
import math

import jax
import jax.numpy as jnp
from jax import lax
from jax.experimental import pallas as pl
from jax.experimental.pallas import tpu as pltpu

F32 = jnp.float32
BF16 = jnp.bfloat16

D_MODEL = 1024
HG_HEADS = 4
HG_DK = 128
HG_W = HG_HEADS * HG_DK
HG_CHUNK = 64
HG_SUB = 8
HG_FWD_CHUNKS_PER_STEP = 4
HG_CHUNKS_PER_STEP = 2
ATT_HEADS = 8
ATT_KV = 2
ATT_GROUP = ATT_HEADS // ATT_KV
ATT_HD = 64
ATT_BLOCK = 128
ATT_Q_W = ATT_HEADS * ATT_HD
ATT_KV_W = ATT_KV * ATT_HD
ATT_COLS = ATT_Q_W + 2 * ATT_KV_W
IN_COLS = 4 * HG_W + ATT_COLS
D_FF = 2816
EPS = 1e-6
ADAM_LR, ADAM_B1, ADAM_B2, ADAM_EPS, ADAM_WD, ADAM_STEP = 0.001, 0.9, 0.999, 1e-08, 0.01, 10
NEG = -1e30

V7X_VMEM_BYTES = 64 * 1024 * 1024
VMEM_LIMIT = 48 * 1024 * 1024
SUBLANES = 8

N_CHIPS = 4


def _cp(sem=None, **kw):
    return pltpu.CompilerParams(dimension_semantics=sem, vmem_limit_bytes=VMEM_LIMIT, **kw)


def _sds(shape, dtype):
    return jax.ShapeDtypeStruct(shape, dtype)


TOKEN = jax.ShapeDtypeStruct((8, 128), jnp.float32)


def _wspec(w):
    arr, rows, blk = w
    return pl.BlockSpec((rows, arr.shape[1]), lambda i: (blk, 0))


def _mm_nt(a, w, *, splits, out_dtype, name, after=None, tm=512):
    M, K = a.shape
    N = w[1]
    tm = min(tm, M)
    assert sum(splits) == N and M % tm == 0
    offs = [sum(splits[:i]) for i in range(len(splits))]
    n_in = 2 if after is None else 3

    def body(*refs):
        a_ref, w_ref = refs[0], refs[1]
        acc = lax.dot_general(a_ref[...], w_ref[...], (((1,), (1,)), ((), ())), preferred_element_type=F32)
        for o_ref, c0, n in zip(refs[n_in:], offs, splits):
            o_ref[...] = acc[:, c0:c0 + n].astype(out_dtype)

    in_specs = [pl.BlockSpec((tm, K), lambda i: (i, 0)), _wspec(w)]
    args = [a, w[0]]
    if after is not None:
        in_specs.append(pl.BlockSpec(memory_space=pl.ANY))
        args.append(after)
    outs = pl.pallas_call(
        body, name=name, grid=(M // tm,), in_specs=in_specs,
        out_specs=[pl.BlockSpec((tm, n), lambda i: (i, 0)) for n in splits],
        out_shape=[_sds((M, n), out_dtype) for n in splits],
        compiler_params=_cp(("parallel",)),
    )(*args)
    return outs


def _mm_nn(pieces, ws, *, name, out_dtype=F32, residual=None, epilogue=None, prologue=None, after=None,
           w_transposed=False, tm=512):
    pro_fn, pro_rows, pro_bc, pro_out = prologue or (None, [], [], None)
    if prologue is not None:
        assert pieces is None and len(ws) == 1
        pieces = [[pro_out]]
    M = pieces[0][0].shape[0]
    K = ws[0][1] if w_transposed else ws[0][0].shape[1]
    tm = min(tm, M)
    flat = [] if prologue is not None else [p for grp in pieces for p in grp]
    n_p = len(flat)
    n_w = len(ws)
    n_pr, n_pb = len(pro_rows), len(pro_bc)
    fn, row_ins, bc_ins, row_outs, acc_outs = epilogue or (None, [], [], [_sds((M, K), out_dtype)], [])
    if residual is not None:
        assert epilogue is None
        row_ins = [residual]
    n_r, n_b, n_o = len(row_ins), len(bc_ins), len(row_outs)
    lead = [] if after is None else [after]

    def body(*refs):
        refs = refs[len(lead):]
        p_refs = refs[:n_p]
        w_refs = refs[n_p:n_p + n_w]
        extra = [r[...] for r in refs[n_p + n_w:n_p + n_w + n_r + n_b]]
        base = n_p + n_w + n_r + n_b
        pro = [r[...] for r in refs[base:base + n_pr + n_pb]]
        base += n_pr + n_pb
        o_refs = refs[base:base + n_o]
        a_refs = refs[base + n_o:base + n_o + len(acc_outs)]
        if pro_fn is not None:
            lhs = pro_fn(*pro).astype(pro_out.dtype)
            refs[-1][...] = lhs
            tiles = [lhs]
        else:
            tiles = [r[...] for r in p_refs]
        acc = None
        k = 0
        for gi, grp in enumerate(pieces):
            c0 = 0
            for p in grp:
                n = p.shape[1]
                if w_transposed:
                    t = lax.dot_general(tiles[k], w_refs[gi][...], (((1,), (1,)), ((), ())), preferred_element_type=F32)
                else:
                    t = jnp.dot(tiles[k], w_refs[gi][c0:c0 + n, :], preferred_element_type=F32)
                acc = t if acc is None else acc + t
                c0 += n
                k += 1
        if fn is None:
            res = (acc + extra[0] if residual is not None else acc,)
        else:
            res = fn(acc, *extra)
        for o_ref, val in zip(o_refs, res[:n_o]):
            o_ref[...] = val.astype(o_ref.dtype)
        if acc_outs:
            @pl.when(pl.program_id(0) == 0)
            def _():
                for a_ref in a_refs:
                    a_ref[...] = jnp.zeros_like(a_ref)
            for a_ref, val in zip(a_refs, res[n_o:]):
                a_ref[...] += val

    in_specs = [pl.BlockSpec((tm, p.shape[1]), lambda i: (i, 0)) for p in flat]
    in_specs += [_wspec(w) for w in ws]
    in_specs += [pl.BlockSpec((tm, r.shape[1]), lambda i: (i, 0)) for r in row_ins]
    in_specs += [pl.BlockSpec(b.shape, lambda i: (0, 0)) for b in bc_ins]
    in_specs += [pl.BlockSpec((tm, r.shape[1]), lambda i: (i, 0)) for r in pro_rows]
    in_specs += [pl.BlockSpec(b.shape, lambda i: (0, 0)) for b in pro_bc]
    out_specs = [pl.BlockSpec((tm, s.shape[1]), lambda i: (i, 0)) for s in row_outs]
    out_specs += [pl.BlockSpec(s.shape, lambda i: (0, 0)) for s in acc_outs]
    pro_outs = [] if prologue is None else [pro_out]
    out_specs += [pl.BlockSpec((tm, s.shape[1]), lambda i: (i, 0)) for s in pro_outs]
    outs = pl.pallas_call(
        body, name=name, grid=(M // tm,), in_specs=[pl.BlockSpec(memory_space=pl.ANY)] * len(lead) + in_specs,
        out_specs=out_specs, out_shape=list(row_outs) + list(acc_outs) + pro_outs,
        compiler_params=_cp(("arbitrary",) if acc_outs else ("parallel",)),
    )(*lead, *flat, *[w[0] for w in ws], *row_ins, *bc_ins, *pro_rows, *pro_bc)
    return outs if (epilogue is not None or prologue is not None) else outs[0]


def _mm_tn(pieces, x, *, name, out_dtype=BF16, tt=1024):
    M, K = x.shape
    tt = min(tt, M)
    ns = [p.shape[1] for p in pieces]
    offs = [sum(ns[:i]) for i in range(len(ns))]
    N = sum(ns)
    n_p = len(pieces)
    last = M // tt - 1

    def body(*refs):
        p_refs = refs[:n_p]
        x_ref = refs[n_p]
        o_ref, acc_ref = refs[n_p + 1], refs[n_p + 2]

        @pl.when(pl.program_id(0) == 0)
        def _():
            acc_ref[...] = jnp.zeros_like(acc_ref)

        xv = x_ref[...]
        for p_ref, c0, n in zip(p_refs, offs, ns):
            acc_ref[c0:c0 + n, :] += lax.dot_general(p_ref[...], xv, (((0,), (0,)), ((), ())),
                                                      preferred_element_type=F32)

        @pl.when(pl.program_id(0) == last)
        def _():
            o_ref[...] = acc_ref[...].astype(o_ref.dtype)

    in_specs = [pl.BlockSpec((tt, n), lambda i: (i, 0)) for n in ns]
    in_specs.append(pl.BlockSpec((tt, K), lambda i: (i, 0)))
    return pl.pallas_call(
        body, name=name, grid=(M // tt,), in_specs=in_specs,
        out_specs=pl.BlockSpec((N, K), lambda i: (0, 0)),
        out_shape=_sds((N, K), out_dtype),
        scratch_shapes=[pltpu.VMEM((N, K), F32)],
        compiler_params=_cp(("arbitrary",)),
    )(*pieces, x)


def _rms_fwd(xf, w):
    inv = lax.rsqrt(jnp.mean(xf * xf, axis=-1, keepdims=True) + EPS)
    return xf * inv * w


def _rms_bwd(xf, w, dy):
    inv = lax.rsqrt(jnp.mean(xf * xf, axis=-1, keepdims=True) + EPS)
    xhat = xf * inv
    dxhat = dy * w
    dx = inv * (dxhat - xhat * jnp.mean(dxhat * xhat, axis=-1, keepdims=True))
    dw = jnp.sum(dy * xhat, axis=0, keepdims=True)
    return dx, dw


def _sigmoid(x):
    return 1.0 / (1.0 + jnp.exp(-x))


def _rowwise(fn, row_ins, bc_ins, row_outs, acc_outs, *, name, tm=256, after=None):
    M = row_outs[0].shape[0] if row_outs else row_ins[0][0].shape[0]
    assert M % tm == 0 and tm % SUBLANES == 0, (name, M, tm)
    n_r, n_b, n_o, n_a = len(row_ins), len(bc_ins), len(row_outs), len(acc_outs)
    n_after = 0 if after is None else 1

    def body(*refs):
        refs = refs[n_after:]
        ins = [r[...] for r in refs[:n_r + n_b]]
        o_refs = refs[n_r + n_b:n_r + n_b + n_o]
        a_refs = refs[n_r + n_b + n_o:]
        res = fn(*ins)
        for o_ref, val in zip(o_refs, res[:n_o]):
            o_ref[...] = val.astype(o_ref.dtype)
        if n_a:
            @pl.when(pl.program_id(0) == 0)
            def _():
                for a_ref in a_refs:
                    a_ref[...] = jnp.zeros_like(a_ref)
            for a_ref, val in zip(a_refs, res[n_o:]):
                a_ref[...] += val

    in_specs = [pl.BlockSpec((tm, cw), functools.partial(lambda i, cb, r0: (i + r0, cb), cb=cb, r0=r0))
                for (_, cw, cb, r0) in row_ins]
    in_specs += [pl.BlockSpec(b.shape, lambda i: (0, 0)) for b in bc_ins]
    out_specs = [pl.BlockSpec((tm, s.shape[1]), lambda i: (i, 0)) for s in row_outs]
    out_specs += [pl.BlockSpec(s.shape, lambda i: (0, 0)) for s in acc_outs]
    if n_after:
        in_specs = [pl.BlockSpec(memory_space=pl.ANY)] + in_specs
    return pl.pallas_call(
        body, name=name, grid=(M // tm,), in_specs=in_specs, out_specs=out_specs,
        out_shape=list(row_outs) + list(acc_outs),
        compiler_params=_cp(("arbitrary",) if n_a else ("parallel",)),
    )(*([after] if n_after else []), *[r[0] for r in row_ins], *bc_ins)


def _full(a, first_row_block=0):
    return (a, a.shape[1], 0, first_row_block)


def _conv_rows(ext, w_ref_val, lo):
    s1 = pltpu.roll(ext, 1, 0)
    s2 = pltpu.roll(ext, 2, 0)
    y = w_ref_val[0:1, :] * s2 + w_ref_val[1:2, :] * s1 + w_ref_val[2:3, :] * ext
    return y[SUBLANES:, :]


def _ffn_in(v, w_gate, w_up, conv_w8, conv_b, *, name, tm=256):
    T, K = v.shape
    N = w_gate[1]
    tm = min(tm, T)

    def body(v_ref, wg_ref, wu_ref, cw_ref, cb_ref, gp_ref, up_ref, act_ref, carry_sc):
        @pl.when(pl.program_id(0) == 0)
        def _():
            carry_sc[...] = jnp.zeros_like(carry_sc)

        vv = v_ref[...]
        dn = (((1,), (1,)), ((), ()))
        gp = lax.dot_general(vv, wg_ref[...], dn, preferred_element_type=F32)
        up = lax.dot_general(vv, wu_ref[...], dn, preferred_element_type=F32)
        gp_ref[...] = gp
        up_ref[...] = up
        gate = _conv_rows(jnp.concatenate([carry_sc[...], gp], axis=0), cw_ref[...], 0) + cb_ref[...]
        act_ref[...] = (gate * _sigmoid(gate) * up).astype(act_ref.dtype)
        carry_sc[...] = gp[tm - SUBLANES:, :]

    tile = lambda dt: pl.BlockSpec((tm, N), lambda i: (i, 0))
    return pl.pallas_call(
        body, name=name, grid=(T // tm,),
        in_specs=[pl.BlockSpec((tm, K), lambda i: (i, 0)), _wspec(w_gate), _wspec(w_up),
                  pl.BlockSpec((SUBLANES, N), lambda i: (0, 0)), pl.BlockSpec((1, N), lambda i: (0, 0))],
        out_specs=[tile(F32), tile(F32), tile(BF16)],
        out_shape=[_sds((T, N), F32), _sds((T, N), F32), _sds((T, N), BF16)],
        scratch_shapes=[pltpu.VMEM((SUBLANES, N), F32)],
        compiler_params=_cp(("arbitrary",)),
    )(v, w_gate[0], w_up[0], conv_w8, conv_b)


def _ffn_back(dh2, w_down, gp, up, conv_w8, conv_b, *, name, tr=256, tc=1408):
    T, C = gp.shape
    K = dh2.shape[1]
    warr, _, wblk = w_down
    tr = min(tr, T)
    hb = tr // SUBLANES
    nr = T // tr
    ncb = C // tc

    def body(dh_ref, wd_ref, gp_ref, gpp_ref, up_ref, w_ref, b_ref, dgp_ref, dup_ref, dw_ref, db_ref, carry_sc):
        i = pl.program_id(1)

        @pl.when(i == 0)
        def _():
            carry_sc[...] = jnp.zeros_like(carry_sc)
            dw_ref[...] = jnp.zeros_like(dw_ref)
            db_ref[...] = jnp.zeros_like(db_ref)

        w = w_ref[...]
        dact = lax.dot_general(dh_ref[...], wd_ref[...], (((1,), (1,)), ((), ())), preferred_element_type=F32)
        gpc = gp_ref[...]
        prev = jnp.where(i < nr - 1, gpp_ref[...], 0.0)
        gate = _conv_rows(jnp.concatenate([prev, gpc], axis=0), w, 0) + b_ref[...]
        sg = _sigmoid(gate)
        silu = gate * sg
        dup_ref[...] = (dact * silu).astype(dup_ref.dtype)
        dgate = dact * up_ref[...] * (sg + silu * (1.0 - sg))
        ext = jnp.concatenate([dgate, carry_sc[...]], axis=0)
        n = tr + SUBLANES
        g1 = pltpu.roll(ext, n - 1, 0)[:tr]
        g2 = pltpu.roll(ext, n - 2, 0)[:tr]
        dgp_ref[...] = (w[2:3, :] * dgate + w[1:2, :] * g1 + w[0:1, :] * g2).astype(dgp_ref.dtype)
        dw0 = jnp.sum(gpc * g2, axis=0, keepdims=True)
        dw1 = jnp.sum(gpc * g1, axis=0, keepdims=True)
        dw2 = jnp.sum(gpc * dgate, axis=0, keepdims=True)
        z = jnp.zeros((SUBLANES - 3, gpc.shape[1]), F32)
        dw_ref[...] += jnp.concatenate([dw0, dw1, dw2, z], axis=0)
        db_ref[...] += jnp.sum(dgate, axis=0, keepdims=True)
        carry_sc[...] = dgate[:SUBLANES]

    rev = lambda i: nr - 1 - i
    cur = pl.BlockSpec((tr, tc), lambda j, i: (rev(i), j))
    prv = pl.BlockSpec((SUBLANES, tc), lambda j, i: (jnp.maximum(rev(i) * hb - 1, 0), j))
    return pl.pallas_call(
        body, name=name, grid=(ncb, nr),
        in_specs=[pl.BlockSpec((tr, K), lambda j, i: (rev(i), 0)),
                  pl.BlockSpec((tc, K), lambda j, i: (wblk * ncb + j, 0)),
                  cur, prv, cur,
                  pl.BlockSpec((SUBLANES, tc), lambda j, i: (0, j)),
                  pl.BlockSpec((1, tc), lambda j, i: (0, j))],
        out_specs=[cur, cur,
                   pl.BlockSpec((SUBLANES, tc), lambda j, i: (0, j)),
                   pl.BlockSpec((1, tc), lambda j, i: (0, j))],
        out_shape=[_sds((T, C), BF16), _sds((T, C), BF16), _sds((SUBLANES, C), F32), _sds((1, C), F32)],
        scratch_shapes=[pltpu.VMEM((SUBLANES, tc), F32)],
        compiler_params=_cp(("parallel", "arbitrary")),
    )(dh2, warr, gp, gp, up, conv_w8, conv_b)


def _cumsum_rows(x):
    n = x.shape[0]
    row = lax.broadcasted_iota(jnp.int32, x.shape, 0)
    s = 1
    while s < n:
        x = x + jnp.where(row >= s, pltpu.roll(x, s, 0), 0.0)
        s *= 2
    return x


def _rcumsum_rows(x):
    n = x.shape[0]
    row = lax.broadcasted_iota(jnp.int32, x.shape, 0)
    s = 1
    while s < n:
        x = x + jnp.where(row < n - s, pltpu.roll(x, n - s, 0), 0.0)
        s *= 2
    return x


def _dot_nt(a, b):
    return lax.dot_general(a.astype(BF16), b.astype(BF16), (((1,), (1,)), ((), ())), preferred_element_type=F32)


def _dot_tn(a, b):
    return lax.dot_general(a.astype(BF16), b.astype(BF16), (((0,), (0,)), ((), ())), preferred_element_type=F32)


def _dot_nn(a, b):
    return jnp.dot(a.astype(BF16), b.astype(BF16), preferred_element_type=F32)


def _dot3(a, b, contract):
    def split(x):
        hi = x.astype(BF16)
        return hi, (x - hi.astype(F32)).astype(BF16)

    a_hi, a_lo = split(a)
    b_hi, b_lo = split(b)
    dot = lambda x, y: lax.dot_general(x, y, (contract, ((), ())), preferred_element_type=F32)
    return dot(a_hi, b_hi) + (dot(a_hi, b_lo) + dot(a_lo, b_hi))


NT, TN, NN = ((1,), (1,)), ((0,), (0,)), ((1,), (0,))


def _hg_gates(hq, hf, lbv):
    sig = _sigmoid(hf)
    f = lbv + (1.0 - lbv) * sig
    return sig, f, jnp.log(f), 1.0 - f, hq * (HG_DK ** -0.5)


def _hg_sel_rows(ref, sp):
    return jnp.concatenate(
        [jnp.broadcast_to(ref[pl.ds(HG_SUB * i + sp, 1), :], (HG_SUB, HG_DK)) for i in range(HG_CHUNK // HG_SUB)], axis=0)


def _hg_masks():
    C = HG_CHUNK
    row = lax.broadcasted_iota(jnp.int32, (C, C), 0)
    col = lax.broadcasted_iota(jnp.int32, (C, C), 1)
    d = col - (row // HG_SUB) * HG_SUB
    tmod = row % HG_SUB
    diag_valid = jnp.logical_and(d >= 0, d <= tmod)
    return row, col, d, diag_valid


def _hg_scores(q, k, b, b_sc, k_sc):
    C, S = HG_CHUNK, HG_SUB
    row, col, d, diag_valid = _hg_masks()
    blocks = [jnp.zeros((S, C), F32)]
    for i in range(1, C // S):
        r = b_sc[pl.ds(S * i - 1, 1), :]
        qi = q[S * i:S * (i + 1)] * jnp.exp(b[S * i:S * (i + 1)] - r)
        kk = k * jnp.exp(jnp.minimum(r - b, 0.0))
        blocks.append(_dot_nt(qi, kk))
    a_off = jnp.where(col < (row // S) * S, jnp.concatenate(blocks, axis=0), 0.0)
    a_d = jnp.zeros((C, C), F32)
    for sp in range(S):
        bs = _hg_sel_rows(b_sc, sp)
        ks = _hg_sel_rows(k_sc, sp)
        e = jnp.exp(jnp.minimum(b - bs, 0.0))
        colv = jnp.sum(q * ks * e, axis=-1, keepdims=True)
        a_d = jnp.where(d == sp, colv, a_d)
    return a_off + jnp.where(diag_valid, a_d, 0.0)


def _hg_prep(hq_v, hf_v, lbv, b_sc, k_sc):
    sig, f, g, k, q = _hg_gates(hq_v, hf_v, lbv)
    b = _cumsum_rows(g)
    b_sc[...] = b
    k_sc[...] = k
    return sig, f, k, q, b, b_sc[pl.ds(HG_CHUNK - 1, 1), :]


def _hgrn_fwd(hq, hf, hi, lb, *, name):
    T = hq.shape[0]
    C, H, K = HG_CHUNK, HG_HEADS, HG_DK
    NC = T // C

    def body(hq_ref, hf_ref, hi_ref, lb_ref, o_ref, st_ref, s_sc, b_sc, k_sc):
        @pl.when(pl.program_id(0) == 0)
        def _():
            s_sc[...] = jnp.zeros_like(s_sc)

        st_all = s_sc[...]
        for j in range(P):
            rows = slice(C * j, C * (j + 1))
            st_ref[j] = st_all
            outs, news = [], []
            for h in range(H):
                sl = slice(K * h, K * (h + 1))
                _, _, k, q, b, bc = _hg_prep(hq_ref[rows, sl], hf_ref[rows, sl], lb_ref[:, sl], b_sc.at[j, h], k_sc.at[j, h])
                v = hi_ref[rows, sl]
                st0 = st_all[:, sl]
                a = _hg_scores(q, k, b, b_sc.at[j, h], k_sc.at[j, h])
                outs.append(_dot_nn(a, v) + _dot_nt(q * jnp.exp(b), st0))
                news.append(st0 * jnp.exp(bc) + _dot_tn(v, k * jnp.exp(bc - b)))
            o_ref[rows, :] = jnp.concatenate(outs, axis=1)
            st_all = jnp.concatenate(news, axis=1)
        s_sc[...] = st_all

    P = HG_FWD_CHUNKS_PER_STEP
    blk = pl.BlockSpec((P * C, H * K), lambda c: (c, 0))
    return pl.pallas_call(
        body, name=name, grid=(NC // P,),
        in_specs=[blk, blk, blk, pl.BlockSpec((1, H * K), lambda c: (0, 0))],
        out_specs=[blk, pl.BlockSpec((P, K, H * K), lambda c: (c, 0, 0))],
        out_shape=[_sds((T, H * K), F32), _sds((NC, K, H * K), F32)],
        scratch_shapes=[pltpu.VMEM((K, H * K), F32), pltpu.VMEM((P, H, C, K), F32), pltpu.VMEM((P, H, C, K), F32)],
        compiler_params=_cp(("arbitrary",)),
    )(hq, hf, hi, lb)


def _hgrn_bwd(hq, hf, hi, lb, states, do, *, name):
    T = hq.shape[0]
    C, H, K, S = HG_CHUNK, HG_HEADS, HG_DK, HG_SUB
    NC = T // C

    def intra_slow(q, k, b, da, b_sc, k_sc):
        row, col, d, diag_valid = _hg_masks()
        a_blocks = [jnp.zeros((S, C), F32)]
        dq_blocks = [jnp.zeros((S, K), F32)]
        dk = jnp.zeros((C, K), F32)
        for i in range(1, C // S):
            r = b_sc[pl.ds(S * i - 1, 1), :]
            eq = jnp.exp(b[S * i:S * (i + 1)] - r)
            ek = jnp.exp(jnp.minimum(r - b, 0.0))
            qi = q[S * i:S * (i + 1)] * eq
            kk = k * ek
            a_blocks.append(_dot_nt(qi, kk))
            dai = jnp.where(col[S * i:S * (i + 1)] < S * i, da[S * i:S * (i + 1)], 0.0)
            dq_blocks.append(_dot_nn(dai, kk) * eq)
            dk = dk + _dot_tn(dai, qi) * ek
        dq = jnp.concatenate(dq_blocks, axis=0)
        a_off = jnp.where(col < (row // S) * S, jnp.concatenate(a_blocks, axis=0), 0.0)
        same_blk = (row // S == col // S).astype(BF16)
        tmod = (lax.broadcasted_iota(jnp.int32, (C, K), 0)) % S
        a_d = jnp.zeros((C, C), F32)
        for sp in range(S):
            bs = _hg_sel_rows(b_sc, sp)
            ks = _hg_sel_rows(k_sc, sp)
            e = jnp.where(tmod >= sp, jnp.exp(jnp.minimum(b - bs, 0.0)), 0.0)
            eks = e * ks
            a_d = jnp.where(d == sp, jnp.sum(q * eks, axis=-1, keepdims=True), a_d)
            dacol = jnp.sum(jnp.where(d == sp, da, 0.0), axis=-1, keepdims=True)
            dq = dq + dacol * eks
            wq = dacol * e * q
            wq_hi = wq.astype(BF16)
            wq_lo = (wq - wq_hi.astype(F32)).astype(BF16)
            blk_sum = (jnp.dot(same_blk, wq_hi, preferred_element_type=F32)
                       + jnp.dot(same_blk, wq_lo, preferred_element_type=F32))
            dk = dk + jnp.where(tmod == sp, blk_sum, 0.0)
        return a_off + jnp.where(diag_valid, a_d, 0.0), dq, dk

    def one_head(pre, v, lbv, st0, dst1, dout, b_sc, k_sc):
        sig, f, k, q, b, bc = pre
        ebc = jnp.exp(bc)
        eb = jnp.exp(b)
        ekb = jnp.exp(bc - b)
        qt = q * eb
        kb = k * ekb
        row = lax.broadcasted_iota(jnp.int32, (C, C), 0)
        col = lax.broadcasted_iota(jnp.int32, (C, C), 1)
        da = jnp.where(col <= row, _dot_nt(dout, v), 0.0)
        dkb = _dot_nn(v, dst1)
        new_ds = _dot_tn(dout, qt) + dst1 * ebc
        a, dq_i, dk_i = intra_slow(q, k, b, da, b_sc, k_sc)
        dq = _dot_nn(dout, st0) * eb + dq_i
        dk = dkb * ekb + dk_i
        dv = _dot_tn(a, dout) + _dot_nt(kb, dst1)
        extra = jnp.sum(dkb * kb, axis=0, keepdims=True) + ebc * jnp.sum(st0 * dst1, axis=0, keepdims=True)
        rowk = lax.broadcasted_iota(jnp.int32, (C, K), 0)
        db = q * dq - k * dk + jnp.where(rowk == C - 1, extra, 0.0)
        dg = _rcumsum_rows(db)
        df = dg / f - dk
        return (dq * (K ** -0.5), df * (1.0 - lbv) * sig * (1.0 - sig), dv,
                jnp.sum(df * (1.0 - sig), axis=0, keepdims=True), new_ds)

    def body(hq_ref, hf_ref, hi_ref, lb_ref, st_ref, do_ref, dq_ref, dhf_ref, dv_ref, dlb_ref, ds_sc, b_sc, k_sc):
        @pl.when(pl.program_id(0) == 0)
        def _():
            ds_sc[...] = jnp.zeros_like(ds_sc)
            dlb_ref[...] = jnp.zeros_like(dlb_ref)

        ds_all = ds_sc[...]
        dlb = jnp.zeros((1, H * K), F32)
        for j in reversed(range(P)):
            rows = slice(C * j, C * (j + 1))
            st_all = st_ref[j]
            res = []
            for h in range(H):
                sl = slice(K * h, K * (h + 1))
                pre = _hg_prep(hq_ref[rows, sl], hf_ref[rows, sl], lb_ref[:, sl], b_sc.at[j, h], k_sc.at[j, h])
                res.append(one_head(pre, hi_ref[rows, sl], lb_ref[:, sl], st_all[:, sl], ds_all[:, sl], do_ref[rows, sl],
                                    b_sc.at[j, h], k_sc.at[j, h]))
            cat = lambda i: jnp.concatenate([r[i] for r in res], axis=1)
            dq_ref[rows, :] = cat(0).astype(dq_ref.dtype)
            dhf_ref[rows, :] = cat(1).astype(dhf_ref.dtype)
            dv_ref[rows, :] = cat(2).astype(dv_ref.dtype)
            dlb = dlb + cat(3)
            ds_all = cat(4)
        dlb_ref[...] += dlb
        ds_sc[...] = ds_all

    P = HG_CHUNKS_PER_STEP
    NS = NC // P
    blk = pl.BlockSpec((P * C, H * K), lambda c: (NS - 1 - c, 0))
    par = pl.BlockSpec((1, H * K), lambda c: (0, 0))
    return pl.pallas_call(
        body, name=name, grid=(NS,),
        in_specs=[blk, blk, blk, par, pl.BlockSpec((P, K, H * K), lambda c: (NS - 1 - c, 0, 0)), blk],
        out_specs=[blk, blk, blk, par],
        out_shape=[_sds((T, H * K), BF16)] * 3 + [_sds((1, H * K), F32)],
        scratch_shapes=[pltpu.VMEM((K, H * K), F32), pltpu.VMEM((P, H, C, K), F32), pltpu.VMEM((P, H, C, K), F32)],
        compiler_params=_cp(("arbitrary",)),
    )(hq, hf, hi, lb, states, do)


ATT_STACK = ATT_GROUP


def _att_valid(n):
    R, B = ATT_STACK * ATT_BLOCK, ATT_BLOCK
    j = lax.broadcasted_iota(jnp.int32, (2 * B, R), 0)
    t = lax.broadcasted_iota(jnp.int32, (2 * B, R), 1) % B
    dist = t + B - j
    first_key = jnp.where(n > 0, 0, B)
    return jnp.logical_and(jnp.logical_and(dist >= 0, dist < B), j >= first_key)


def _att_load(cur_ref, prev_ref, ba_ref, h0):
    hd = ATT_HD
    kv = h0 // ATT_GROUP
    def cols(ref, c0):
        return ref[:, c0:c0 + hd] + ba_ref[:, c0:c0 + hd]
    qs = jnp.concatenate([cols(cur_ref, hd * (h0 + g)) for g in range(ATT_STACK)], axis=0)
    kc = jnp.concatenate([cols(prev_ref, ATT_Q_W + hd * kv), cols(cur_ref, ATT_Q_W + hd * kv)], axis=0)
    vc = jnp.concatenate([cols(prev_ref, ATT_Q_W + ATT_KV_W + hd * kv), cols(cur_ref, ATT_Q_W + ATT_KV_W + hd * kv)], axis=0)
    return qs, kc, vc


def _att_probs(qs, kc, valid, sink_ref, h0):
    scale = 1.0 / math.sqrt(ATT_HD)
    s = jnp.where(valid, _dot_nt(kc, qs) * scale, NEG)
    sink = jnp.concatenate([jnp.full((1, ATT_BLOCK), sink_ref[0, h0 + g], F32) for g in range(ATT_STACK)], axis=1)
    m = jnp.maximum(jnp.max(s, axis=0, keepdims=True), sink)
    p = jnp.exp(s - m)
    ps = jnp.exp(sink - m)
    inv = 1.0 / (jnp.sum(p, axis=0, keepdims=True) + ps)
    return p * inv, ps * inv


def _attn_fwd(att, b_attn, sinks, *, name, after=None):
    T = att.shape[0]
    B = ATT_BLOCK
    NB = T // B
    lead = [] if after is None else [after]

    def body(*refs):
        sink_ref, cur_ref, prev_ref, ba_ref, o_ref = refs[len(lead):]
        valid = _att_valid(pl.program_id(0))
        outs = []
        for h0 in range(0, ATT_HEADS, ATT_STACK):
            qs, kc, vc = _att_load(cur_ref, prev_ref, ba_ref, h0)
            prob, _ = _att_probs(qs, kc, valid, sink_ref, h0)
            o = _dot_tn(prob, vc)
            outs += [o[B * g:B * (g + 1)] for g in range(ATT_STACK)]
        o_ref[...] = jnp.concatenate(outs, axis=1)

    return pl.pallas_call(
        body, name=name, grid=(NB,),
        in_specs=[pl.BlockSpec(memory_space=pl.ANY)] * len(lead) + [
            pl.BlockSpec(memory_space=pltpu.SMEM),
            pl.BlockSpec((B, ATT_COLS), lambda n: (n, 0)),
            pl.BlockSpec((B, ATT_COLS), lambda n: (jnp.maximum(n - 1, 0), 0)),
            pl.BlockSpec((1, ATT_COLS), lambda n: (0, 0))],
        out_specs=pl.BlockSpec((B, ATT_Q_W), lambda n: (n, 0)),
        out_shape=_sds((T, ATT_Q_W), F32),
        compiler_params=_cp(("parallel",)),
    )(*lead, sinks, att, att, b_attn)


def _attn_bwd(att, b_attn, sinks, dmix, *, name):
    T = att.shape[0]
    B, hd = ATT_BLOCK, ATT_HD
    NB = T // B
    scale = 1.0 / math.sqrt(hd)

    def body(sink_ref, cur_ref, prev_ref, ba_ref, do_ref, daq_ref, dakv_ref, dsink_ref, dbq_ref, dbkv_ref, carry_sc):
        n = pl.program_id(0)

        @pl.when(n == 0)
        def _():
            carry_sc[...] = jnp.zeros_like(carry_sc)
            dsink_ref[...] = jnp.zeros_like(dsink_ref)
            dbq_ref[...] = jnp.zeros_like(dbq_ref)
            dbkv_ref[...] = jnp.zeros_like(dbkv_ref)

        @pl.when(n < NB)
        def _():
            valid = _att_valid(n)
            hrow = lax.broadcasted_iota(jnp.int32, (SUBLANES, 128), 0)
            dsink = jnp.zeros((SUBLANES, 128), F32)
            dqs = []
            dks = [jnp.zeros((2 * B, hd), F32)] * ATT_KV
            dvs = [jnp.zeros((2 * B, hd), F32)] * ATT_KV
            for h0 in range(0, ATT_HEADS, ATT_STACK):
                kv = h0 // ATT_GROUP
                qs, kc, vc = _att_load(cur_ref, prev_ref, ba_ref, h0)
                prob, psink = _att_probs(qs, kc, valid, sink_ref, h0)
                dout = jnp.concatenate([do_ref[:, hd * (h0 + g):hd * (h0 + g + 1)] for g in range(ATT_STACK)], axis=0)
                dp = _dot_nt(vc, dout)
                delta = jnp.sum(prob * dp, axis=0, keepdims=True)
                dsc = prob * (dp - delta) * scale
                dq = _dot_tn(dsc, kc)
                dks[kv] = dks[kv] + _dot_nn(dsc, qs)
                dvs[kv] = dvs[kv] + _dot_nn(prob, dout)
                dsk = psink * delta
                for g in range(ATT_STACK):
                    dqs.append(dq[B * g:B * (g + 1)])
                    tot = jnp.sum(dsk[:, B * g:B * (g + 1)], axis=1, keepdims=True)
                    dsink = dsink - jnp.where(hrow == h0 + g, tot, 0.0)
            daq = jnp.concatenate(dqs, axis=1).astype(daq_ref.dtype)
            daq_ref[...] = daq
            dsink_ref[...] += dsink
            dbq_ref[...] += jnp.sum(daq.astype(F32), axis=0, keepdims=True)
            done = carry_sc[...] + jnp.concatenate([d[:B] for d in dks + dvs], axis=1)
            dakv_ref[...] = done.astype(dakv_ref.dtype)
            dbkv_ref[...] += jnp.sum(done.astype(dakv_ref.dtype).astype(F32), axis=0, keepdims=True)
            carry_sc[...] = jnp.concatenate([d[B:] for d in dks + dvs], axis=1)

        @pl.when(n == NB)
        def _():
            done = carry_sc[...]
            dakv_ref[...] = done.astype(dakv_ref.dtype)
            dbkv_ref[...] += jnp.sum(done.astype(dakv_ref.dtype).astype(F32), axis=0, keepdims=True)

    cl = lambda n: jnp.minimum(n, NB - 1)
    return pl.pallas_call(
        body, name=name, grid=(NB + 1,),
        in_specs=[pl.BlockSpec(memory_space=pltpu.SMEM),
                  pl.BlockSpec((B, ATT_COLS), lambda n: (cl(n), 0)),
                  pl.BlockSpec((B, ATT_COLS), lambda n: (jnp.maximum(cl(n) - 1, 0), 0)),
                  pl.BlockSpec((1, ATT_COLS), lambda n: (0, 0)),
                  pl.BlockSpec((B, ATT_Q_W), lambda n: (cl(n), 0))],
        out_specs=[pl.BlockSpec((B, ATT_Q_W), lambda n: (cl(n), 0)),
                   pl.BlockSpec((B, 2 * ATT_KV_W), lambda n: (jnp.maximum(n - 1, 0), 0)),
                   pl.BlockSpec((SUBLANES, 128), lambda n: (0, 0)),
                   pl.BlockSpec((1, ATT_Q_W), lambda n: (0, 0)),
                   pl.BlockSpec((1, 2 * ATT_KV_W), lambda n: (0, 0))],
        out_shape=[_sds((T, ATT_Q_W), BF16), _sds((T, 2 * ATT_KV_W), BF16), _sds((SUBLANES, 128), F32),
                   _sds((1, ATT_Q_W), F32), _sds((1, 2 * ATT_KV_W), F32)],
        scratch_shapes=[pltpu.VMEM((B, 2 * ATT_KV_W), F32)],
        compiler_params=_cp(("arbitrary",)),
    )(sinks, att, att, b_attn, dmix)


def _silu_and_grad(x):
    sg = _sigmoid(x)
    return x * sg, sg * (1.0 + x * (1.0 - sg))


def _mix_fwd_fn(o_raw, hg, o_att, hgw):
    outs = []
    for h in range(HG_HEADS):
        sl = slice(HG_DK * h, HG_DK * (h + 1))
        silu, _ = _silu_and_grad(hg[:, sl])
        outs.append(_rms_fwd(o_raw[:, sl], hgw) * silu)
    outs.append(o_att)
    return (jnp.concatenate(outs, axis=1),)


def _mix_bwd_fn(o_raw, hg, dmix, hgw):
    dos, dhgs = [], []
    dw = jnp.zeros((1, HG_DK), F32)
    for h in range(HG_HEADS):
        sl = slice(HG_DK * h, HG_DK * (h + 1))
        silu, dsilu = _silu_and_grad(hg[:, sl])
        dy = dmix[:, sl]
        dhgs.append(dy * _rms_fwd(o_raw[:, sl], hgw) * dsilu)
        dx, dwh = _rms_bwd(o_raw[:, sl], hgw, dy * silu)
        dos.append(dx)
        dw = dw + dwh
    return jnp.concatenate(dos, axis=1), jnp.concatenate(dhgs, axis=1), dw


def _final_fn(h2, tgt, wf):
    d = h2.shape[1]
    err = _rms_fwd(h2, wf) - tgt
    loss_cols = (0.5 / d) * jnp.sum(err * err, axis=0, keepdims=True)
    dh2, dwf = _rms_bwd(h2, wf, err * (1.0 / d))
    return dh2, dh2, loss_cols, dwf


class _NoExchange:
    def __init__(self, weights):
        self.weights = weights

    def start(self):
        return None

    def w_in(self, after):
        return self.weights["w_in_t"]

    def mid(self, after):
        return None

    def rest(self, after):
        return self.weights

    def ffn_grads(self, gs):
        return None

    def ffn_grads_send(self, after):
        return None


def _local_step(x, tgt, p, ex):
    T, D = x.shape
    row = lambda n, dt: _sds((T, n), dt)
    acc = lambda n: _sds((1, n), F32)

    (u,) = _rowwise(lambda xv, w: (_rms_fwd(xv, w),), [_full(x)], [p["norm_mix_w"]], [row(D, BF16)], [], name="rms_mix",
                    after=ex.start())
    p = dict(p, w_in_t=ex.w_in(u))
    hq, hf, hi, hg, att = _mm_nt(u, p["w_in_t"], splits=[HG_W] * 4 + [ATT_COLS], out_dtype=F32, name="in_proj")
    o_raw, states = _hgrn_fwd(hq, hf, hi, p["lb"], name="hgrn_fwd")
    o_att = _attn_fwd(att, p["b_attn"], p["sinks"], name="attn_fwd", after=ex.mid(o_raw))
    p = dict(p, **ex.rest(o_att))
    def out_epilogue(prod, xv, w):
        h1v = prod + xv
        return h1v, _rms_fwd(h1v, w)

    h1, v, mix = _mm_nn(None, [p["w_out"]], name="mix_out_proj",
                        prologue=(lambda *a: _mix_fwd_fn(*a)[0], [o_raw, hg, o_att], [p["hg_norm_w"]], row(D, BF16)),
                        epilogue=(out_epilogue, [x], [p["norm_ffn_w"]], [row(D, F32), row(D, BF16)], []))
    gp, up, act = _ffn_in(v, p["w_gate_t"], p["w_up_t"], p["conv_w8"], p["conv_b"], name="ffn_in")
    def down_epilogue(prod, h1v, tgtv, wf):
        return _final_fn(prod + h1v, tgtv, wf)

    dh2, dh2_b, loss_cols, d_final = _mm_nn(
        [[act]], [p["w_down"]], name="down_proj_loss",
        epilogue=(down_epilogue, [h1, tgt], [p["final_norm_w"]], [row(D, F32), row(D, BF16)], [acc(D), acc(D)]))

    g_down = _mm_tn([act], dh2_b, name="g_down")
    dgp, dup, d_conv_w8, d_conv_b = _ffn_back(dh2_b, p["w_down"], gp, up, p["conv_w8"], p["conv_b"], name="ffn_back")
    g_gate_t = _mm_tn([dgp], v, name="g_gate")
    g_up_t = _mm_tn([dup], v, name="g_up")
    swapping = ex.ffn_grads([g_gate_t, g_up_t, g_down])

    def ffn_norm_bwd(dvv, hv, dh2v, w):
        dx, dw = _rms_bwd(hv, w, dvv)
        dh1v = dx + dh2v
        return dh1v, dh1v, dw

    dh1, dh1_b, d_norm_ffn = _mm_nn(
        [[dgp], [dup]], [p["w_gate_t"], p["w_up_t"]], name="d_v_norm", after=swapping,
        epilogue=(ffn_norm_bwd, [h1, dh2], [p["norm_ffn_w"]], [row(D, F32), row(D, BF16)], [acc(D)]))
    sent = ex.ffn_grads_send(dh1_b)
    def mix_bwd(dmixv, o_rawv, hgv, hgw):
        do_rawv, dhgv, dw = _mix_bwd_fn(o_rawv, hgv, dmixv[:, :HG_W], hgw)
        return do_rawv, dhgv, dmixv[:, HG_W:], dw

    do_raw, dhg, do_att, d_hg_norm = _mm_nn(
        [[dh1_b]], [p["w_out"]], name="d_mix_bwd", w_transposed=True, after=sent,
        epilogue=(mix_bwd, [o_raw, hg], [p["hg_norm_w"]], [row(HG_W, F32), row(HG_W, BF16), row(ATT_Q_W, F32)], [acc(HG_DK)]))
    g_out = _mm_tn([mix], dh1_b, name="g_out")
    daq, dakv, d_sinks8, d_bq, d_bkv = _attn_bwd(att, p["b_attn"], p["sinks"], do_att, name="attn_bwd")
    dhq, dhf, dhi, d_lb = _hgrn_bwd(hq, hf, hi, p["lb"], states, do_raw, name="hgrn_bwd")
    pieces = [dhq, dhf, dhi, dhg, daq, dakv]
    g_in_t = _mm_tn(pieces, u, name="g_in")

    def mix_norm_bwd(duv, xv, dh1v, w):
        dx, dw = _rms_bwd(xv, w, duv)
        return dx + dh1v, dw

    dx, d_norm_mix = _mm_nn([pieces], [p["w_in_t"]], name="d_u_norm",
                            epilogue=(mix_norm_bwd, [x, dh1], [p["norm_mix_w"]], [row(D, F32)], [acc(D)]))
    grads = dict(g_in_t=g_in_t, g_out=g_out, g_gate_t=g_gate_t, g_up_t=g_up_t, g_down=g_down,
                 norm_mix_w=d_norm_mix, b_attn=jnp.concatenate([d_bq, d_bkv], axis=1), lb=d_lb, hg_norm_w=d_hg_norm,
                 sinks8=d_sinks8, norm_ffn_w=d_norm_ffn, conv_w8=d_conv_w8, conv_b=d_conv_b, final_norm_w=d_final)
    return loss_cols, dx, grads


SLAB = (IN_COLS // N_CHIPS, D_FF // N_CHIPS, D_FF // N_CHIPS, D_FF // N_CHIPS, D_MODEL // N_CHIPS)
N_W = len(SLAB)
PACK_OFF = tuple(sum(SLAB[:i]) for i in range(N_W))
PACK_ROWS = sum(SLAB)
FULL_OFF = tuple(N_CHIPS * o for o in PACK_OFF)
FULL_ROWS = N_CHIPS * PACK_ROWS
HALF = tuple(s // 2 for s in SLAB)
HPACK_OFF = tuple(sum(HALF[:i]) for i in range(N_W))
HPACK_ROWS = sum(HALF)
HFULL_OFF = tuple(N_CHIPS * o for o in HPACK_OFF)
HFULL_ROWS = N_CHIPS * HPACK_ROWS
CHIP_FLIPS = ((1, 0), (0, 1), (1, 1))
N_DEV = 8
BF16_ROWS = 16
ANY = pl.BlockSpec(memory_space=pl.ANY)


def _pos():
    return lax.axis_index("x"), lax.axis_index("y"), lax.axis_index("c")


def _flip(v, f):
    return 1 - v if f else v


def _rcopy(src, dst, ssem, rsem, dev):
    return pltpu.make_async_remote_copy(src_ref=src, dst_ref=dst, send_sem=ssem, recv_sem=rsem, device_id=dev,
                                        device_id_type=pl.DeviceIdType.MESH)


def _rows(ref, start, n, align=None):
    if not isinstance(start, int):
        if align is None:
            align = SUBLANES * (4 // jnp.dtype(ref.dtype).itemsize)
        start = pl.multiple_of(start, align)
    return ref.at[pl.ds(start, n), :]


FFN_W = (1, 2, 3)
N_PEER = 1 + len(CHIP_FLIPS)
HBM = pl.BlockSpec(memory_space=pltpu.HBM)
SEM = pl.BlockSpec(memory_space=pltpu.SEMAPHORE)
EFFECT = pltpu.SideEffectType.DATAFLOW_SIDE_EFFECTING
LANES = 128


def _sent_rows(k, w, c):
    return (0, SLAB[w]) if k == 0 else (c * HALF[w], HALF[w])


def _gather_start(pack, cw8):
    D = pack.shape[1]
    lands = [lax.empty((N_CHIPS * SLAB[0], D), pack.dtype), lax.empty((3 * N_CHIPS * SLAB[1], D), pack.dtype),
             lax.empty((N_CHIPS * SLAB[4], D), pack.dtype), lax.empty((N_CHIPS,) + cw8.shape, cw8.dtype)]
    bufs = [pack, cw8] + lands

    def body(pack_ref, cw_ref, l_in, l_ffn, l_out, l_cw, *rest):
        in_send, in_recv, out_send, out_recv, ffn_send, ffn_recv = rest[:6]
        token = rest[-1]
        x, y, c = _pos()
        q = 2 * x + y
        peers = _gather_peers(x, y, c)

        def send(k, peer, w, land, base, ssem, rsem):
            r0, n = _sent_rows(k, w, c)
            _rcopy(_rows(pack_ref, PACK_OFF[w] + r0, n), _rows(land, base + q * SLAB[w] + r0, n), ssem, rsem, peer).start()

        for k, peer in enumerate(peers):
            send(k, peer, 0, l_in, 0, in_send.at[k], in_recv.at[k])
        for k, peer in enumerate(peers):
            send(k, peer, 4, l_out, 0, out_send.at[k], out_recv.at[k])
            _rcopy(cw_ref, l_cw.at[q], out_send.at[N_PEER + k], out_recv.at[N_PEER + k], peer).start()
        for j, w in enumerate(FFN_W):
            for k, peer in enumerate(peers):
                send(k, peer, w, l_ffn, j * N_CHIPS * SLAB[w], ffn_send.at[k], ffn_recv.at[k])
        token[...] = jnp.zeros_like(token)

    n_sem = (N_PEER, N_PEER, 2 * N_PEER, 2 * N_PEER, N_PEER, N_PEER)
    outs = pl.pallas_call(
        body, name="gather_start", in_specs=[HBM] * len(bufs),
        out_specs=[SEM] * len(n_sem) + [HBM] * len(bufs) + [pl.BlockSpec(memory_space=pltpu.VMEM)],
        out_shape=[pltpu.SemaphoreType.DMA((n,)) for n in n_sem]
        + [pltpu.HBM(b.shape, b.dtype) for b in bufs] + [TOKEN],
        input_output_aliases={i: len(n_sem) + i for i in range(len(bufs))},
        compiler_params=pltpu.CompilerParams(has_side_effects=EFFECT),
    )(*[pltpu.with_memory_space_constraint(b, pltpu.HBM) for b in bufs])
    bufs_out = outs[len(n_sem):]
    return dict(in_sems=outs[0:2], out_sems=outs[2:4], ffn_sems=outs[4:6], pack=bufs_out[0], cw=bufs_out[1], l_in=bufs_out[2],
                l_ffn=bufs_out[3], l_out=bufs_out[4], l_cw=bufs_out[5], token=bufs_out[6])


def _gather_peers(x, y, c):
    return [(x, y, 1 - c)] + [(_flip(x, fx), _flip(y, fy), c) for fx, fy in CHIP_FLIPS]


def _gather_wait_in(g, after):
    def body(pack_ref, l_in, send, recv, after_ref, pack_out, l_out):
        for k, peer in enumerate(_gather_peers(*_pos())):
            n = _sent_rows(k, 0, 0)[1]
            cp = _rcopy(_rows(pack_ref, PACK_OFF[0], n), _rows(l_in, 0, n), send.at[k], recv.at[k], peer)
            cp.wait_send()
            cp.wait_recv()

    return pl.pallas_call(
        body, name="gather_wait_in", in_specs=[HBM, HBM, SEM, SEM, ANY], out_specs=[HBM, HBM],
        out_shape=[pltpu.HBM(g["pack"].shape, g["pack"].dtype), pltpu.HBM(g["l_in"].shape, g["l_in"].dtype)],
        input_output_aliases={0: 0, 1: 1}, compiler_params=pltpu.CompilerParams(has_side_effects=EFFECT),
    )(g["pack"], g["l_in"], *g["in_sems"], after)


def _gather_wait_rest(g, pack, after):
    def body(pack_ref, cw_ref, l_ffn, l_out, l_cw, o_send, o_recv, f_send, f_recv, after_ref, o_ffn, o_out, o_cw):
        for k, peer in enumerate(_gather_peers(*_pos())):
            n_out = _sent_rows(k, 4, 0)[1]
            n_ffn = len(FFN_W) * _sent_rows(k, FFN_W[0], 0)[1]
            for cp in (_rcopy(_rows(pack_ref, PACK_OFF[4], n_out), _rows(l_out, 0, n_out), o_send.at[k], o_recv.at[k], peer),
                       _rcopy(cw_ref, l_cw.at[0], o_send.at[N_PEER + k], o_recv.at[N_PEER + k], peer),
                       _rcopy(_rows(pack_ref, PACK_OFF[FFN_W[0]], n_ffn), _rows(l_ffn, 0, n_ffn), f_send.at[k], f_recv.at[k], peer)):
                cp.wait_send()
                cp.wait_recv()

    ins = [pack, g["cw"], g["l_ffn"], g["l_out"], g["l_cw"]]
    return pl.pallas_call(
        body, name="gather_wait_rest", in_specs=[HBM] * 5 + [SEM] * 4 + [ANY], out_specs=[HBM] * 3,
        out_shape=[pltpu.HBM(b.shape, b.dtype) for b in ins[2:]],
        input_output_aliases={2: 0, 3: 1, 4: 2}, compiler_params=pltpu.CompilerParams(has_side_effects=EFFECT),
    )(*ins, *g["out_sems"], *g["ffn_sems"], after)


FWD_IN = ((0, 0, 0),)
FWD_REST = tuple((0, w, j * N_CHIPS * SLAB[w]) for j, w in enumerate(FFN_W)) + ((1, 4, 0),)


def _forward_copies(layout, src, dst, send_sems, recv_sems):
    x, y, c = _pos()
    sib = (x, y, 1 - c)
    cps = []
    for fx, fy in CHIP_FLIPS:
        qa = 2 * _flip(x, fx) + _flip(y, fy)
        for bi, w, base in layout:
            r0 = base + qa * SLAB[w] + c * HALF[w]
            cps.append(_rcopy(_rows(src[bi], r0, HALF[w]), _rows(dst[bi], r0, HALF[w]),
                              send_sems.at[len(cps)], recv_sems.at[len(cps)], sib))
    return cps


def _forward_in(l_in):
    n = len(CHIP_FLIPS) * len(FWD_IN)

    def body(in_ref, out_ref, send_sems, recv_sems):
        cps = _forward_copies(FWD_IN, [in_ref], [out_ref], send_sems, recv_sems)
        for cp in cps:
            cp.start()
        for cp in cps:
            cp.wait_recv()
        for cp in cps:
            cp.wait_send()

    return pl.pallas_call(
        body, name="forward_in", in_specs=[ANY], out_specs=ANY, out_shape=_sds(l_in.shape, l_in.dtype),
        input_output_aliases={0: 0},
        scratch_shapes=[pltpu.SemaphoreType.DMA((n,)), pltpu.SemaphoreType.DMA((n,))],
    )(l_in)


def _forward_rest_start(l_ffn, l_out):
    n = len(CHIP_FLIPS) * len(FWD_REST)
    bufs = [l_ffn, l_out]

    def body(a_ref, b_ref, send_sems, recv_sems, a_out, b_out, token):
        for cp in _forward_copies(FWD_REST, [a_ref, b_ref], [a_ref, b_ref], send_sems, recv_sems):
            cp.start()
        token[...] = jnp.zeros_like(token)

    outs = pl.pallas_call(
        body, name="forward_rest_start", in_specs=[HBM] * 2,
        out_specs=[SEM, SEM, HBM, HBM, pl.BlockSpec(memory_space=pltpu.VMEM)],
        out_shape=[pltpu.SemaphoreType.DMA((n,)), pltpu.SemaphoreType.DMA((n,))]
        + [pltpu.HBM(b.shape, b.dtype) for b in bufs] + [TOKEN],
        input_output_aliases={0: 2, 1: 3}, compiler_params=pltpu.CompilerParams(has_side_effects=EFFECT),
    )(*[pltpu.with_memory_space_constraint(b, pltpu.HBM) for b in bufs])
    return dict(sems=outs[0:2], bufs=outs[2:4], token=outs[4])


def _forward_rest_wait(s, after):
    def body(a_ref, b_ref, send_sems, recv_sems, after_ref, a_out, b_out):
        for cp in _forward_copies(FWD_REST, [a_ref, b_ref], [a_ref, b_ref], send_sems, recv_sems):
            cp.wait_send()
            cp.wait_recv()

    return pl.pallas_call(
        body, name="forward_rest_wait", in_specs=[HBM, HBM, SEM, SEM, ANY], out_specs=[HBM, HBM],
        out_shape=[pltpu.HBM(b.shape, b.dtype) for b in s["bufs"]],
        input_output_aliases={0: 0, 1: 1}, compiler_params=pltpu.CompilerParams(has_side_effects=EFFECT),
    )(*s["bufs"], *s["sems"], after)


def _exchange_halves(ws, gs, small, *, name):
    D = gs[0].shape[1]
    n = len(ws)
    has_small = small is not None

    def body(*refs):
        g = refs[:n]
        t = refs[n + has_small:2 * n + has_small]
        sems = refs[2 * n + 2 * has_small:]
        d2d_send, d2d_recv = sems[0], sems[1]
        x, y, c = _pos()
        sib = (x, y, 1 - c)
        drains = []
        for i, w in enumerate(ws):
            h = HALF[w]
            for qq in range(N_CHIPS):
                _rcopy(_rows(g[i], qq * SLAB[w] + (1 - c) * h, h), _rows(t[i], qq * h, h),
                       d2d_send.at[i], d2d_recv.at[i], sib).start()
            drains.append(_rcopy(t[i], t[i], d2d_send.at[i], d2d_recv.at[i], sib))
        if has_small:
            small_ref, sall_ref = refs[n], refs[2 * n + 1]
            sm_send, sm_recv, loc_sem = sems[2], sems[3], sems[4]
            me = 4 * x + 2 * y + c
            own_small = pltpu.make_async_copy(small_ref, sall_ref.at[me], loc_sem)
            own_small.start()
            for f in range(1, N_DEV):
                peer = (_flip(x, f & 4), _flip(y, f & 2), _flip(c, f & 1))
                cp = _rcopy(small_ref, sall_ref.at[me], sm_send.at[f - 1], sm_recv.at[f - 1], peer)
                cp.start()
                drains.append(cp)
        for d in drains:
            d.wait_recv()
        for d in drains:
            d.wait_send()
        if has_small:
            own_small.wait()

    out_shape = [_sds((N_CHIPS * HALF[w], D), gs[0].dtype) for w in ws]
    scratch = [pltpu.SemaphoreType.DMA((n,)), pltpu.SemaphoreType.DMA((n,))]
    if has_small:
        out_shape.append(_sds((N_DEV,) + small.shape, F32))
        scratch += [pltpu.SemaphoreType.DMA((N_DEV - 1,)), pltpu.SemaphoreType.DMA((N_DEV - 1,)), pltpu.SemaphoreType.DMA]
    return pl.pallas_call(
        body, name=name, in_specs=[ANY] * (n + has_small), out_specs=[ANY] * (n + has_small),
        out_shape=out_shape, scratch_shapes=scratch,
    )(*gs, *([small] if has_small else []))


def _halves_copies(ws, g, t, send_sems, recv_sems):
    x, y, c = _pos()
    sib = (x, y, 1 - c)
    cps = []
    for i, w in enumerate(ws):
        h = HALF[w]
        for qq in range(N_CHIPS):
            cps.append(_rcopy(_rows(g[i], qq * SLAB[w] + (1 - c) * h, h), _rows(t[i], qq * h, h),
                              send_sems.at[N_CHIPS * i + qq], recv_sems.at[N_CHIPS * i + qq], sib))
    return cps


def _halves_start(ws, gs, *, name):
    D = gs[0].shape[1]
    n = len(ws)
    bufs = list(gs) + [lax.empty((N_CHIPS * HALF[w], D), gs[0].dtype) for w in ws]

    def body(*refs):
        for cp in _halves_copies(ws, refs[:n], refs[n:2 * n], refs[2 * n], refs[2 * n + 1]):
            cp.start()
        refs[-1][...] = jnp.zeros_like(refs[-1])

    outs = pl.pallas_call(
        body, name=name, in_specs=[HBM] * (2 * n),
        out_specs=[SEM, SEM] + [HBM] * (2 * n) + [pl.BlockSpec(memory_space=pltpu.VMEM)],
        out_shape=[pltpu.SemaphoreType.DMA((N_CHIPS * n,)), pltpu.SemaphoreType.DMA((N_CHIPS * n,))]
        + [pltpu.HBM(b.shape, b.dtype) for b in bufs] + [TOKEN],
        input_output_aliases={i: 2 + i for i in range(2 * n)},
        compiler_params=pltpu.CompilerParams(has_side_effects=EFFECT),
    )(*[pltpu.with_memory_space_constraint(b, pltpu.HBM) for b in bufs])
    return dict(sems=outs[0:2], gs=outs[2:2 + n], theirs=outs[2 + n:2 + 2 * n], token=outs[-1])


def _halves_wait(ws, s, after, *, name):
    n = len(ws)

    def body(*refs):
        for cp in _halves_copies(ws, refs[:n], refs[n:2 * n], refs[2 * n], refs[2 * n + 1]):
            cp.wait_send()
            cp.wait_recv()

    bufs = list(s["gs"]) + list(s["theirs"])
    outs = pl.pallas_call(
        body, name=name, in_specs=[HBM] * (2 * n) + [SEM, SEM, ANY], out_specs=[HBM] * (2 * n),
        out_shape=[pltpu.HBM(b.shape, b.dtype) for b in bufs],
        input_output_aliases={i: i for i in range(2 * n)},
        compiler_params=pltpu.CompilerParams(has_side_effects=EFFECT),
    )(*bufs, *s["sems"], after)
    return outs[:n], outs[n:]


REDUCE_SPLIT = 2


def _chip_partial(ws, gs, theirs, *, name, out_dtype=F32):
    D = gs[0].shape[1]
    n = len(ws)

    def body(*refs):
        for i in range(n):
            refs[2 * n + i][...] = (refs[i][...].astype(F32) + refs[n + i][...].astype(F32)).astype(out_dtype)

    blk = [HALF[w] // REDUCE_SPLIT for w in ws]
    mine = [pl.BlockSpec((b, D), lambda qq, j: ((2 * qq + lax.axis_index("c")) * REDUCE_SPLIT + j, 0)) for b in blk]
    flat = [pl.BlockSpec((b, D), lambda qq, j: (qq * REDUCE_SPLIT + j, 0)) for b in blk]
    return pl.pallas_call(
        body, name=name, grid=(N_CHIPS, REDUCE_SPLIT), in_specs=mine + flat, out_specs=flat,
        out_shape=[_sds((N_CHIPS * HALF[w], D), out_dtype) for w in ws],
        compiler_params=_cp(("parallel", "parallel")),
    )(*gs, *theirs)


def _partial_copies(ws, part, got, send_sems, recv_sems):
    x, y, c = _pos()
    cps = []
    for k, (fx, fy) in enumerate(CHIP_FLIPS):
        peer = (_flip(x, fx), _flip(y, fy), c)
        qp = 2 * _flip(x, fx) + _flip(y, fy)
        for i, w in enumerate(ws):
            cps.append(_rcopy(_rows(part[i], qp * HALF[w], HALF[w]), _rows(got[i], k * HALF[w], HALF[w]),
                              send_sems.at[len(ws) * k + i], recv_sems.at[len(ws) * k + i], peer))
    return cps


def _send_chip_partials(ws, parts, *, name):
    D = parts[0].shape[1]
    n = len(ws)

    def body(*refs):
        cps = _partial_copies(ws, refs[:n], refs[n:2 * n], refs[2 * n], refs[2 * n + 1])
        for cp in cps:
            cp.start()
        for cp in cps:
            cp.wait_recv()
        for cp in cps:
            cp.wait_send()

    return pl.pallas_call(
        body, name=name, in_specs=[ANY] * n, out_specs=[ANY] * n,
        out_shape=[_sds((len(CHIP_FLIPS) * HALF[w], D), parts[0].dtype) for w in ws],
        scratch_shapes=[pltpu.SemaphoreType.DMA((len(CHIP_FLIPS) * n,)), pltpu.SemaphoreType.DMA((len(CHIP_FLIPS) * n,))],
    )(*parts)


def _send_start(ws, parts, *, name):
    D = parts[0].shape[1]
    n = len(ws)
    bufs = list(parts) + [lax.empty((len(CHIP_FLIPS) * HALF[w], D), parts[0].dtype) for w in ws]

    def body(*refs):
        send_sems, recv_sems = refs[2 * n], refs[2 * n + 1]
        for cp in _partial_copies(ws, refs[:n], refs[n:2 * n], send_sems, recv_sems):
            cp.start()
        refs[-1][...] = jnp.zeros_like(refs[-1])

    outs = pl.pallas_call(
        body, name=name, in_specs=[HBM] * (2 * n),
        out_specs=[SEM, SEM] + [HBM] * (2 * n) + [pl.BlockSpec(memory_space=pltpu.VMEM)],
        out_shape=[pltpu.SemaphoreType.DMA((len(CHIP_FLIPS) * n,)), pltpu.SemaphoreType.DMA((len(CHIP_FLIPS) * n,))]
        + [pltpu.HBM(b.shape, b.dtype) for b in bufs] + [TOKEN],
        input_output_aliases={i: 2 + i for i in range(2 * n)},
        compiler_params=pltpu.CompilerParams(has_side_effects=EFFECT),
    )(*[pltpu.with_memory_space_constraint(b, pltpu.HBM) for b in bufs])
    return dict(sems=outs[0:2], parts=outs[2:2 + n], got=outs[2 + n:2 + 2 * n], token=outs[-1])


def _send_wait(ws, s, after, *, name):
    n = len(ws)

    def body(*refs):
        for cp in _partial_copies(ws, refs[:n], refs[n:2 * n], refs[2 * n], refs[2 * n + 1]):
            cp.wait_send()
            cp.wait_recv()

    bufs = list(s["parts"]) + list(s["got"])
    outs = pl.pallas_call(
        body, name=name, in_specs=[HBM] * (2 * n) + [SEM, SEM] + [ANY] * len(after), out_specs=[HBM] * (2 * n),
        out_shape=[pltpu.HBM(b.shape, b.dtype) for b in bufs],
        input_output_aliases={i: i for i in range(2 * n)},
        compiler_params=pltpu.CompilerParams(has_side_effects=EFFECT),
    )(*bufs, *s["sems"], *after)
    return outs[:n], outs[n:]


def _chip_reduce(ws, parts, got, *, name, after=None):
    D = parts[0].shape[1]
    nk = len(CHIP_FLIPS)
    n = len(ws)
    extra = [] if after is None else [after]

    def body(*refs):
        refs = refs[len(extra):]
        outs = refs[(1 + nk) * n:]
        for i in range(n):
            acc = refs[i][...].astype(F32)
            for k in range(nk):
                acc = acc + refs[n * (1 + k) + i][...].astype(F32)
            outs[i][...] = acc

    blk = [HALF[w] // REDUCE_SPLIT for w in ws]

    def q_idx(j):
        return (2 * lax.axis_index("x") + lax.axis_index("y")) * REDUCE_SPLIT + j

    in_specs = [pl.BlockSpec((b, D), lambda j: (q_idx(j), 0)) for b in blk]
    for k in range(nk):
        in_specs += [pl.BlockSpec((b, D), functools.partial(lambda j, k: (k * REDUCE_SPLIT + j, 0), k=k)) for b in blk]
    out_specs = [pl.BlockSpec((b, D), lambda j: (lax.axis_index("c") * REDUCE_SPLIT + j, 0)) for b in blk]
    return pl.pallas_call(
        body, name=name, grid=(REDUCE_SPLIT,), in_specs=[ANY] * len(extra) + in_specs, out_specs=out_specs,
        out_shape=[_sds((SLAB[w], D), F32) for w in ws],
        compiler_params=_cp(("parallel",)),
    )(*extra, *parts, *[g for _ in range(nk) for g in got])


def _exchange_reduced(ws, shards, *, name):
    n = len(ws)

    def body(*refs):
        ins, outs = refs[:n], refs[n:2 * n]
        send_sems, recv_sems = refs[2 * n], refs[2 * n + 1]
        x, y, c = _pos()
        sib = (x, y, 1 - c)
        cps = []
        for i, w in enumerate(ws):
            cp = _rcopy(_rows(ins[i], c * HALF[w], HALF[w]), _rows(outs[i], c * HALF[w], HALF[w]),
                        send_sems.at[i], recv_sems.at[i], sib)
            cp.start()
            cps.append(cp)
        for cp in cps:
            cp.wait_recv()
        for cp in cps:
            cp.wait_send()

    return pl.pallas_call(
        body, name=name, in_specs=[ANY] * n, out_specs=[ANY] * n,
        out_shape=[_sds(s.shape, s.dtype) for s in shards], input_output_aliases={i: i for i in range(n)},
        scratch_shapes=[pltpu.SemaphoreType.DMA((n,)), pltpu.SemaphoreType.DMA((n,))],
    )(*shards)


def _adamw_fn(w, g, m, v):
    m2 = ADAM_B1 * m + (1.0 - ADAM_B1) * g
    v2 = ADAM_B2 * v + (1.0 - ADAM_B2) * (g * g)
    m_hat = m2 / (1.0 - ADAM_B1 ** ADAM_STEP)
    v_hat = v2 / (1.0 - ADAM_B2 ** ADAM_STEP)
    return -ADAM_LR * (m_hat / (jnp.sqrt(v_hat) + ADAM_EPS) + ADAM_WD * w), m2, v2


def _adamw(w, g, m, v, *, name):
    shp = _sds(w.shape, F32)
    rows = w.shape[0]
    tm = max(t for t in range(SUBLANES, 512 + 1, SUBLANES) if rows % t == 0)
    return _rowwise(lambda wv, gv, mv, vv: (gv, *_adamw_fn(wv, gv, mv, vv)), [_full(w), _full(g), _full(m), _full(v)], [],
                    [shp] * 4, [], name=name, tm=tm)


SMALL_SEGS = (("loss", 8), ("norm_mix_w", 8), ("b_attn", 8), ("lb_logits", 8), ("hg_norm_w", 8), ("sinks", 8),
              ("norm_ffn_w", 8), ("conv_w", 72), ("conv_b", 24), ("final_norm_w", 8))
SMALL_OFF = {n: sum(r for _, r in SMALL_SEGS[:i]) for i, (n, _) in enumerate(SMALL_SEGS)}
SMALL_ROWS = sum(r for _, r in SMALL_SEGS)
LANES = 128


def _pack_small(parts):
    segs = []
    for n, r in SMALL_SEGS:
        a = parts.get(n)
        flat = jnp.zeros((0,), F32) if a is None else a.reshape(-1).astype(F32)
        segs.append(jnp.pad(flat, (0, r * LANES - flat.shape[0])).reshape(r, LANES))
    return jnp.concatenate(segs, axis=0)


def _unpack_small(pack, n, shape):
    size = math.prod(shape)
    r0 = SMALL_OFF[n]
    return pack[r0:r0 + dict(SMALL_SEGS)[n]].reshape(-1)[:size].reshape(shape)


def _small_update(sall, wp, mp, vp, *, after):
    R = SMALL_ROWS
    r_lb = SMALL_OFF["lb_logits"]

    def body(after_ref, s_ref, w_ref, m_ref, v_ref, g_ref, d_ref, m2_ref, v2_ref, loss_ref):
        g = s_ref[0]
        for i in range(1, N_DEV):
            g = g + s_ref[i]
        tot = jnp.sum(jnp.sum(g[0:8], axis=1, keepdims=True), axis=0, keepdims=True)
        loss_ref[...] = jnp.broadcast_to(tot, loss_ref.shape)
        lg = w_ref[r_lb:r_lb + 8, :]
        p0 = _sigmoid(lg - pltpu.roll(lg, 4, 0))
        d = g[r_lb:r_lb + 8]
        d = d + pltpu.roll(d, 4, 0)
        sign = jnp.where(lax.broadcasted_iota(jnp.int32, d.shape, 0) < 4, 1.0, -1.0)
        g = jnp.concatenate([g[:r_lb], sign * d * p0 * (1.0 - p0), g[r_lb + 8:]], axis=0)
        g_ref[...] = g
        d_ref[...], m2_ref[...], v2_ref[...] = _adamw_fn(w_ref[...], g, m_ref[...], v_ref[...])

    full = pl.BlockSpec((R, LANES), lambda: (0, 0))
    return pl.pallas_call(
        body, name="small_update",
        in_specs=[ANY, pl.BlockSpec((N_DEV, R, LANES), lambda: (0, 0, 0)), full, full, full],
        out_specs=[full, full, full, full, pl.BlockSpec((8, LANES), lambda: (0, 0))],
        out_shape=[_sds((R, LANES), F32)] * 4 + [_sds((8, LANES), F32)],
        compiler_params=_cp(),
    )(after, sall, wp, mp, vp)


def _lb_fwd(lb_logits):
    n = lb_logits.shape[1]

    def body(l_ref, o_ref):
        o_ref[...] = _sigmoid(l_ref[0:1, :] - l_ref[1:2, :])

    return pl.pallas_call(body, name="lb_fwd", out_shape=jax.ShapeDtypeStruct((1, n), F32), compiler_params=_cp())(lb_logits)


class _MeshExchange:
    def __init__(self, pack, cw8):
        self.gather = _gather_start(pack, cw8)
        self.sent = None
        self.conv_w8 = None

    def start(self):
        return self.gather["token"]

    def w_in(self, after):
        self.pack, l_in = _gather_wait_in(self.gather, after)
        return (_forward_in(l_in), N_CHIPS * SLAB[0], 0)

    def mid(self, after):
        l_ffn, l_out, l_cw = _gather_wait_rest(self.gather, self.pack, after)
        self.conv_w8 = jnp.concatenate([l_cw[i] for i in range(N_CHIPS)], axis=1)
        self.passing = _forward_rest_start(l_ffn, l_out)
        return self.passing["token"]

    def rest(self, after):
        l_ffn, l_out = _forward_rest_wait(self.passing, after)
        rows = N_CHIPS * SLAB[FFN_W[0]]
        return dict(w_gate_t=(l_ffn, rows, 0), w_up_t=(l_ffn, rows, 1), w_down=(l_ffn, rows, 2),
                    w_out=(l_out, N_CHIPS * SLAB[4], 0), conv_w8=self.conv_w8)

    def ffn_grads(self, gs):
        self.swap = _halves_start(FFN_W, gs, name="halves_ffn_start")
        return self.swap["token"]

    def ffn_grads_send(self, after):
        gs, theirs = _halves_wait(FFN_W, self.swap, after, name="halves_ffn_wait")
        parts = _chip_partial(FFN_W, gs, theirs, name="chip_partial_ffn", out_dtype=BF16)
        self.sent = _send_start(FFN_W, parts, name="send_ffn_start")
        return self.sent["token"]


def kernel(x, norm_mix_w, w_in, b_attn, lb_logits, hg_norm_w, sinks, w_out, norm_ffn_w, w_gate, w_up, conv_w, conv_b, w_down, final_norm_w, loss_target, m_norm_mix_w, m_w_in, m_b_attn, m_lb_logits, m_hg_norm_w, m_sinks, m_w_out, m_norm_ffn_w, m_w_gate, m_w_up, m_conv_w, m_conv_b, m_w_down, m_final_norm_w, v_norm_mix_w, v_w_in, v_b_attn, v_lb_logits, v_hg_norm_w, v_sinks, v_w_out, v_norm_ffn_w, v_w_gate, v_w_up, v_conv_w, v_conv_b, v_w_down, v_final_norm_w):
    D = D_MODEL
    q = 2 * lax.axis_index("x") + lax.axis_index("y")
    ccols = D_FF // N_CHIPS

    pack = jnp.concatenate([w_in[0].T, w_gate[0].T, w_up[0].T, w_down[0], w_out[0]], axis=0).astype(BF16)
    cw8 = jnp.concatenate([conv_w[0], jnp.zeros((SUBLANES - 3, ccols), F32)], axis=0)
    ex = _MeshExchange(pack, cw8)
    p = dict(norm_mix_w=norm_mix_w, b_attn=b_attn, lb=_lb_fwd(lb_logits), hg_norm_w=hg_norm_w, sinks=sinks,
             norm_ffn_w=norm_ffn_w, conv_b=conv_b, final_norm_w=final_norm_w.reshape(1, D))
    loss_cols, dx, g = _local_step(x[0], loss_target[0], p, ex)
    conv_w8 = ex.conv_w8

    small = _pack_small(dict(loss=loss_cols, norm_mix_w=g["norm_mix_w"], b_attn=g["b_attn"], lb_logits=g["lb"],
                             hg_norm_w=g["hg_norm_w"], sinks=g["sinks8"], norm_ffn_w=g["norm_ffn_w"],
                             conv_w=g["conv_w8"][:3], conv_b=g["conv_b"], final_norm_w=g["final_norm_w"]))
    parts_ffn, got_ffn = _send_wait(FFN_W, ex.sent, [dx], name="send_ffn_wait")
    late = (0, 4)
    gs = [g["g_in_t"], g["g_out"]]
    *theirs, sall = _exchange_halves(late, gs, small, name="exchange_halves_late")
    parts_late = _chip_partial(late, gs, theirs, name="chip_partial_late", out_dtype=BF16)
    sent_late = _send_start(late, parts_late, name="send_late_start")
    big = {}

    def finish(ws, parts, got, specs, tag, after):
        shards = _exchange_reduced(ws, _chip_reduce(ws, parts, got, name="chip_reduce_" + tag, after=after),
                                   name="exchange_reduced_" + tag)
        deltas = []
        for gw, (n, w, m, v, tr) in zip(shards, specs):
            view = (lambda a: a[0].T) if tr else (lambda a: a[0])
            back = (lambda a: a.T[None]) if tr else (lambda a: a[None])
            res = _adamw(view(w), gw, view(m), view(v), name="adamw_" + n)
            big[n] = tuple(back(r) for r in res)
            deltas.append(res[1])
        return deltas

    done_ffn = finish(FFN_W, parts_ffn, got_ffn, (("w_gate", w_gate, m_w_gate, v_w_gate, True),
                                                  ("w_up", w_up, m_w_up, v_w_up, True),
                                                  ("w_down", w_down, m_w_down, v_w_down, False)), "ffn", sent_late["token"])

    def place(a):
        return lax.dynamic_update_slice(jnp.zeros((3, D_FF), F32), a[0], (0, q * ccols))

    def small_pack(ws, cw):
        nm, ba, lbl, hg, sk, nf, cb, fn = ws
        return _pack_small(dict(norm_mix_w=nm, b_attn=ba, lb_logits=lbl, hg_norm_w=hg,
                                sinks=jnp.broadcast_to(sk.reshape(ATT_HEADS, 1), (ATT_HEADS, LANES)), norm_ffn_w=nf,
                                conv_w=cw, conv_b=cb, final_norm_w=fn))

    wp = small_pack((norm_mix_w, b_attn, lb_logits, hg_norm_w, sinks, norm_ffn_w, conv_b, final_norm_w), conv_w8[:3])
    mp = small_pack((m_norm_mix_w, m_b_attn, m_lb_logits, m_hg_norm_w, m_sinks, m_norm_ffn_w, m_conv_b, m_final_norm_w),
                    place(m_conv_w))
    vp = small_pack((v_norm_mix_w, v_b_attn, v_lb_logits, v_hg_norm_w, v_sinks, v_norm_ffn_w, v_conv_b, v_final_norm_w),
                    place(v_conv_w))
    outs = _small_update(sall, wp, mp, vp, after=sent_late["token"])
    loss = outs[4][0, 0]
    parts_late, got_late = _send_wait(late, sent_late, [*done_ffn, outs[4]], name="send_late_wait")
    finish(late, parts_late, got_late, (("w_in", w_in, m_w_in, v_w_in, True), ("w_out", w_out, m_w_out, v_w_out, False)),
           "late", None)

    def small_out(pk, n, ref):
        if n == "sinks":
            return pk[SMALL_OFF[n]:SMALL_OFF[n] + ATT_HEADS, 0].reshape(ref.shape)
        if n == "conv_w":
            full = _unpack_small(pk, n, (3, D_FF))
            return lax.dynamic_slice(full, (0, q * ccols), (3, ccols))[None]
        return _unpack_small(pk, n, ref.shape)

    refs = dict(norm_mix_w=norm_mix_w, b_attn=b_attn, lb_logits=lb_logits, hg_norm_w=hg_norm_w, sinks=sinks,
                norm_ffn_w=norm_ffn_w, conv_w=conv_w, conv_b=conv_b, final_norm_w=final_norm_w)
    order = ("norm_mix_w", "w_in", "b_attn", "lb_logits", "hg_norm_w", "sinks", "w_out", "norm_ffn_w", "w_gate", "w_up",
             "conv_w", "conv_b", "w_down", "final_norm_w")
    res = [loss, dx[None]]
    for k in range(4):
        for n in order:
            res.append(big[n][k] if n in big else small_out(outs[k], n, refs[n]))
    return tuple(res)
```

```python
import functools
import math

import jax
import jax.numpy as jnp
from jax import lax
from jax.experimental import pallas as pl
from jax.experimental.pallas import tpu as pltpu

F32 = jnp.float32
BF16 = jnp.bfloat16

D_MODEL = 1024
HG_HEADS = 4
HG_DK = 128
HG_W = HG_HEADS * HG_DK
HG_CHUNK = 64
HG_SUB = 8
HG_FWD_CHUNKS_PER_STEP = 4
HG_CHUNKS_PER_STEP = 2
ATT_HEADS = 8
ATT_KV = 2
ATT_GROUP = ATT_HEADS // ATT_KV
ATT_HD = 64
ATT_BLOCK = 128
ATT_Q_W = ATT_HEADS * ATT_HD
ATT_KV_W = ATT_KV * ATT_HD
ATT_COLS = ATT_Q_W + 2 * ATT_KV_W
IN_COLS = 4 * HG_W + ATT_COLS
D_FF = 2816
EPS = 1e-6
ADAM_LR, ADAM_B1, ADAM_B2, ADAM_EPS, ADAM_WD, ADAM_STEP = 0.001, 0.9, 0.999, 1e-08, 0.01, 10
NEG = -1e30

V7X_VMEM_BYTES = 64 * 1024 * 1024
VMEM_LIMIT = 48 * 1024 * 1024
SUBLANES = 8

N_CHIPS = 4


def _cp(sem=None, **kw):
    return pltpu.CompilerParams(dimension_semantics=sem, vmem_limit_bytes=VMEM_LIMIT, **kw)


def _sds(shape, dtype):
    return jax.ShapeDtypeStruct(shape, dtype)


TOKEN = jax.ShapeDtypeStruct((8, 128), jnp.float32)


def _wspec(w):
    arr, rows, blk = w
    return pl.BlockSpec((rows, arr.shape[1]), lambda i: (blk, 0))


def _mm_nt(a, w, *, splits, out_dtype, name, after=None, tm=512):
    M, K = a.shape
    N = w[1]
    tm = min(tm, M)
    assert sum(splits) == N and M % tm == 0
    offs = [sum(splits[:i]) for i in range(len(splits))]
    n_in = 2 if after is None else 3

    def body(*refs):
        a_ref, w_ref = refs[0], refs[1]
        acc = lax.dot_general(a_ref[...], w_ref[...], (((1,), (1,)), ((), ())), preferred_element_type=F32)
        for o_ref, c0, n in zip(refs[n_in:], offs, splits):
            o_ref[...] = acc[:, c0:c0 + n].astype(out_dtype)

    in_specs = [pl.BlockSpec((tm, K), lambda i: (i, 0)), _wspec(w)]
    args = [a, w[0]]
    if after is not None:
        in_specs.append(pl.BlockSpec(memory_space=pl.ANY))
        args.append(after)
    outs = pl.pallas_call(
        body, name=name, grid=(M // tm,), in_specs=in_specs,
        out_specs=[pl.BlockSpec((tm, n), lambda i: (i, 0)) for n in splits],
        out_shape=[_sds((M, n), out_dtype) for n in splits],
        compiler_params=_cp(("parallel",)),
    )(*args)
    return outs


def _mm_nn(pieces, ws, *, name, out_dtype=F32, residual=None, epilogue=None, prologue=None, after=None,
           w_transposed=False, tm=512):
    pro_fn, pro_rows, pro_bc, pro_out = prologue or (None, [], [], None)
    if prologue is not None:
        assert pieces is None and len(ws) == 1
        pieces = [[pro_out]]
    M = pieces[0][0].shape[0]
    K = ws[0][1] if w_transposed else ws[0][0].shape[1]
    tm = min(tm, M)
    flat = [] if prologue is not None else [p for grp in pieces for p in grp]
    n_p = len(flat)
    n_w = len(ws)
    n_pr, n_pb = len(pro_rows), len(pro_bc)
    fn, row_ins, bc_ins, row_outs, acc_outs = epilogue or (None, [], [], [_sds((M, K), out_dtype)], [])
    if residual is not None:
        assert epilogue is None
        row_ins = [residual]
    n_r, n_b, n_o = len(row_ins), len(bc_ins), len(row_outs)
    lead = [] if after is None else [after]

    def body(*refs):
        refs = refs[len(lead):]
        p_refs = refs[:n_p]
        w_refs = refs[n_p:n_p + n_w]
        extra = [r[...] for r in refs[n_p + n_w:n_p + n_w + n_r + n_b]]
        base = n_p + n_w + n_r + n_b
        pro = [r[...] for r in refs[base:base + n_pr + n_pb]]
        base += n_pr + n_pb
        o_refs = refs[base:base + n_o]
        a_refs = refs[base + n_o:base + n_o + len(acc_outs)]
        if pro_fn is not None:
            lhs = pro_fn(*pro).astype(pro_out.dtype)
            refs[-1][...] = lhs
            tiles = [lhs]
        else:
            tiles = [r[...] for r in p_refs]
        acc = None
        k = 0
        for gi, grp in enumerate(pieces):
            c0 = 0
            for p in grp:
                n = p.shape[1]
                if w_transposed:
                    t = lax.dot_general(tiles[k], w_refs[gi][...], (((1,), (1,)), ((), ())), preferred_element_type=F32)
                else:
                    t = jnp.dot(tiles[k], w_refs[gi][c0:c0 + n, :], preferred_element_type=F32)
                acc = t if acc is None else acc + t
                c0 += n
                k += 1
        if fn is None:
            res = (acc + extra[0] if residual is not None else acc,)
        else:
            res = fn(acc, *extra)
        for o_ref, val in zip(o_refs, res[:n_o]):
            o_ref[...] = val.astype(o_ref.dtype)
        if acc_outs:
            @pl.when(pl.program_id(0) == 0)
            def _():
                for a_ref in a_refs:
                    a_ref[...] = jnp.zeros_like(a_ref)
            for a_ref, val in zip(a_refs, res[n_o:]):
                a_ref[...] += val

    in_specs = [pl.BlockSpec((tm, p.shape[1]), lambda i: (i, 0)) for p in flat]
    in_specs += [_wspec(w) for w in ws]
    in_specs += [pl.BlockSpec((tm, r.shape[1]), lambda i: (i, 0)) for r in row_ins]
    in_specs += [pl.BlockSpec(b.shape, lambda i: (0, 0)) for b in bc_ins]
    in_specs += [pl.BlockSpec((tm, r.shape[1]), lambda i: (i, 0)) for r in pro_rows]
    in_specs += [pl.BlockSpec(b.shape, lambda i: (0, 0)) for b in pro_bc]
    out_specs = [pl.BlockSpec((tm, s.shape[1]), lambda i: (i, 0)) for s in row_outs]
    out_specs += [pl.BlockSpec(s.shape, lambda i: (0, 0)) for s in acc_outs]
    pro_outs = [] if prologue is None else [pro_out]
    out_specs += [pl.BlockSpec((tm, s.shape[1]), lambda i: (i, 0)) for s in pro_outs]
    outs = pl.pallas_call(
        body, name=name, grid=(M // tm,), in_specs=[pl.BlockSpec(memory_space=pl.ANY)] * len(lead) + in_specs,
        out_specs=out_specs, out_shape=list(row_outs) + list(acc_outs) + pro_outs,
        compiler_params=_cp(("arbitrary",) if acc_outs else ("parallel",)),
    )(*lead, *flat, *[w[0] for w in ws], *row_ins, *bc_ins, *pro_rows, *pro_bc)
    return outs if (epilogue is not None or prologue is not None) else outs[0]


def _mm_tn(pieces, x, *, name, out_dtype=BF16, tt=1024):
    M, K = x.shape
    tt = min(tt, M)
    ns = [p.shape[1] for p in pieces]
    offs = [sum(ns[:i]) for i in range(len(ns))]
    N = sum(ns)
    n_p = len(pieces)
    last = M // tt - 1

    def body(*refs):
        p_refs = refs[:n_p]
        x_ref = refs[n_p]
        o_ref, acc_ref = refs[n_p + 1], refs[n_p + 2]

        @pl.when(pl.program_id(0) == 0)
        def _():
            acc_ref[...] = jnp.zeros_like(acc_ref)

        xv = x_ref[...]
        for p_ref, c0, n in zip(p_refs, offs, ns):
            acc_ref[c0:c0 + n, :] += lax.dot_general(p_ref[...], xv, (((0,), (0,)), ((), ())),
                                                      preferred_element_type=F32)

        @pl.when(pl.program_id(0) == last)
        def _():
            o_ref[...] = acc_ref[...].astype(o_ref.dtype)

    in_specs = [pl.BlockSpec((tt, n), lambda i: (i, 0)) for n in ns]
    in_specs.append(pl.BlockSpec((tt, K), lambda i: (i, 0)))
    return pl.pallas_call(
        body, name=name, grid=(M // tt,), in_specs=in_specs,
        out_specs=pl.BlockSpec((N, K), lambda i: (0, 0)),
        out_shape=_sds((N, K), out_dtype),
        scratch_shapes=[pltpu.VMEM((N, K), F32)],
        compiler_params=_cp(("arbitrary",)),
    )(*pieces, x)


def _rms_fwd(xf, w):
    inv = lax.rsqrt(jnp.mean(xf * xf, axis=-1, keepdims=True) + EPS)
    return xf * inv * w


def _rms_bwd(xf, w, dy):
    inv = lax.rsqrt(jnp.mean(xf * xf, axis=-1, keepdims=True) + EPS)
    xhat = xf * inv
    dxhat = dy * w
    dx = inv * (dxhat - xhat * jnp.mean(dxhat * xhat, axis=-1, keepdims=True))
    dw = jnp.sum(dy * xhat, axis=0, keepdims=True)
    return dx, dw


def _sigmoid(x):
    return 1.0 / (1.0 + jnp.exp(-x))


def _rowwise(fn, row_ins, bc_ins, row_outs, acc_outs, *, name, tm=256, after=None):
    M = row_outs[0].shape[0] if row_outs else row_ins[0][0].shape[0]
    assert M % tm == 0 and tm % SUBLANES == 0, (name, M, tm)
    n_r, n_b, n_o, n_a = len(row_ins), len(bc_ins), len(row_outs), len(acc_outs)
    n_after = 0 if after is None else 1

    def body(*refs):
        refs = refs[n_after:]
        ins = [r[...] for r in refs[:n_r + n_b]]
        o_refs = refs[n_r + n_b:n_r + n_b + n_o]
        a_refs = refs[n_r + n_b + n_o:]
        res = fn(*ins)
        for o_ref, val in zip(o_refs, res[:n_o]):
            o_ref[...] = val.astype(o_ref.dtype)
        if n_a:
            @pl.when(pl.program_id(0) == 0)
            def _():
                for a_ref in a_refs:
                    a_ref[...] = jnp.zeros_like(a_ref)
            for a_ref, val in zip(a_refs, res[n_o:]):
                a_ref[...] += val

    in_specs = [pl.BlockSpec((tm, cw), functools.partial(lambda i, cb, r0: (i + r0, cb), cb=cb, r0=r0))
                for (_, cw, cb, r0) in row_ins]
    in_specs += [pl.BlockSpec(b.shape, lambda i: (0, 0)) for b in bc_ins]
    out_specs = [pl.BlockSpec((tm, s.shape[1]), lambda i: (i, 0)) for s in row_outs]
    out_specs += [pl.BlockSpec(s.shape, lambda i: (0, 0)) for s in acc_outs]
    if n_after:
        in_specs = [pl.BlockSpec(memory_space=pl.ANY)] + in_specs
    return pl.pallas_call(
        body, name=name, grid=(M // tm,), in_specs=in_specs, out_specs=out_specs,
        out_shape=list(row_outs) + list(acc_outs),
        compiler_params=_cp(("arbitrary",) if n_a else ("parallel",)),
    )(*([after] if n_after else []), *[r[0] for r in row_ins], *bc_ins)


def _full(a, first_row_block=0):
    return (a, a.shape[1], 0, first_row_block)


def _conv_rows(ext, w_ref_val, lo):
    s1 = pltpu.roll(ext, 1, 0)
    s2 = pltpu.roll(ext, 2, 0)
    y = w_ref_val[0:1, :] * s2 + w_ref_val[1:2, :] * s1 + w_ref_val[2:3, :] * ext
    return y[SUBLANES:, :]


def _ffn_in(v, w_gate, w_up, conv_w8, conv_b, *, name, tm=256):
    T, K = v.shape
    N = w_gate[1]
    tm = min(tm, T)

    def body(v_ref, wg_ref, wu_ref, cw_ref, cb_ref, gp_ref, up_ref, gate_ref, act_ref, carry_sc):
        @pl.when(pl.program_id(0) == 0)
        def _():
            carry_sc[...] = jnp.zeros_like(carry_sc)

        vv = v_ref[...]
        dn = (((1,), (1,)), ((), ()))
        gp = lax.dot_general(vv, wg_ref[...], dn, preferred_element_type=F32)
        up = lax.dot_general(vv, wu_ref[...], dn, preferred_element_type=F32)
        gp_ref[...] = gp
        up_ref[...] = up
        gate = _conv_rows(jnp.concatenate([carry_sc[...], gp], axis=0), cw_ref[...], 0) + cb_ref[...]
        gate_ref[...] = gate
        act_ref[...] = (gate * _sigmoid(gate) * up).astype(act_ref.dtype)
        carry_sc[...] = gp[tm - SUBLANES:, :]

    tile = pl.BlockSpec((tm, N), lambda i: (i, 0))
    return pl.pallas_call(
        body, name=name, grid=(T // tm,),
        in_specs=[pl.BlockSpec((tm, K), lambda i: (i, 0)), _wspec(w_gate), _wspec(w_up),
                  pl.BlockSpec((SUBLANES, N), lambda i: (0, 0)), pl.BlockSpec((1, N), lambda i: (0, 0))],
        out_specs=[tile] * 4,
        out_shape=[_sds((T, N), F32)] * 3 + [_sds((T, N), BF16)],
        scratch_shapes=[pltpu.VMEM((SUBLANES, N), F32)],
        compiler_params=_cp(("arbitrary",)),
    )(v, w_gate[0], w_up[0], conv_w8, conv_b)


def _ffn_back(dh2, w_down, gp, up, gate, conv_w8, *, name, tr=256, tc=1408):
    T, C = gp.shape
    K = dh2.shape[1]
    warr, _, wblk = w_down
    tr = min(tr, T)
    nr = T // tr
    ncb = C // tc

    def body(dh_ref, wd_ref, gp_ref, up_ref, gate_ref, w_ref, dgp_ref, dup_ref, dw_ref, db_ref, carry_sc):
        @pl.when(pl.program_id(1) == 0)
        def _():
            carry_sc[...] = jnp.zeros_like(carry_sc)
            dw_ref[...] = jnp.zeros_like(dw_ref)
            db_ref[...] = jnp.zeros_like(db_ref)

        w = w_ref[...]
        dact = lax.dot_general(dh_ref[...], wd_ref[...], (((1,), (1,)), ((), ())), preferred_element_type=F32)
        gpc = gp_ref[...]
        gate = gate_ref[...]
        sg = _sigmoid(gate)
        silu = gate * sg
        dup_ref[...] = (dact * silu).astype(dup_ref.dtype)
        dgate = dact * up_ref[...] * (sg + silu * (1.0 - sg))
        ext = jnp.concatenate([dgate, carry_sc[...]], axis=0)
        n = tr + SUBLANES
        g1 = pltpu.roll(ext, n - 1, 0)[:tr]
        g2 = pltpu.roll(ext, n - 2, 0)[:tr]
        dgp_ref[...] = (w[2:3, :] * dgate + w[1:2, :] * g1 + w[0:1, :] * g2).astype(dgp_ref.dtype)
        dw0 = jnp.sum(gpc * g2, axis=0, keepdims=True)
        dw1 = jnp.sum(gpc * g1, axis=0, keepdims=True)
        dw2 = jnp.sum(gpc * dgate, axis=0, keepdims=True)
        z = jnp.zeros((SUBLANES - 3, gpc.shape[1]), F32)
        dw_ref[...] += jnp.concatenate([dw0, dw1, dw2, z], axis=0)
        db_ref[...] += jnp.sum(dgate, axis=0, keepdims=True)
        carry_sc[...] = dgate[:SUBLANES]

    rev = lambda i: nr - 1 - i
    cur = pl.BlockSpec((tr, tc), lambda j, i: (rev(i), j))
    return pl.pallas_call(
        body, name=name, grid=(ncb, nr),
        in_specs=[pl.BlockSpec((tr, K), lambda j, i: (rev(i), 0)),
                  pl.BlockSpec((tc, K), lambda j, i: (wblk * ncb + j, 0)),
                  cur, cur, cur,
                  pl.BlockSpec((SUBLANES, tc), lambda j, i: (0, j))],
        out_specs=[cur, cur,
                   pl.BlockSpec((SUBLANES, tc), lambda j, i: (0, j)),
                   pl.BlockSpec((1, tc), lambda j, i: (0, j))],
        out_shape=[_sds((T, C), BF16), _sds((T, C), BF16), _sds((SUBLANES, C), F32), _sds((1, C), F32)],
        scratch_shapes=[pltpu.VMEM((SUBLANES, tc), F32)],
        compiler_params=_cp(("parallel", "arbitrary")),
    )(dh2, warr, gp, up, gate, conv_w8)


def _cumsum_rows(x):
    n = x.shape[0]
    row = lax.broadcasted_iota(jnp.int32, x.shape, 0)
    s = 1
    while s < n:
        x = x + jnp.where(row >= s, pltpu.roll(x, s, 0), 0.0)
        s *= 2
    return x


def _rcumsum_rows(x):
    n = x.shape[0]
    row = lax.broadcasted_iota(jnp.int32, x.shape, 0)
    s = 1
    while s < n:
        x = x + jnp.where(row < n - s, pltpu.roll(x, n - s, 0), 0.0)
        s *= 2
    return x


def _dot_nt(a, b):
    return lax.dot_general(a.astype(BF16), b.astype(BF16), (((1,), (1,)), ((), ())), preferred_element_type=F32)


def _dot_tn(a, b):
    return lax.dot_general(a.astype(BF16), b.astype(BF16), (((0,), (0,)), ((), ())), preferred_element_type=F32)


def _dot_nn(a, b):
    return jnp.dot(a.astype(BF16), b.astype(BF16), preferred_element_type=F32)


def _dot3(a, b, contract):
    def split(x):
        hi = x.astype(BF16)
        return hi, (x - hi.astype(F32)).astype(BF16)

    a_hi, a_lo = split(a)
    b_hi, b_lo = split(b)
    dot = lambda x, y: lax.dot_general(x, y, (contract, ((), ())), preferred_element_type=F32)
    return dot(a_hi, b_hi) + (dot(a_hi, b_lo) + dot(a_lo, b_hi))


NT, TN, NN = ((1,), (1,)), ((0,), (0,)), ((1,), (0,))


def _hg_gates(hq, hf, lbv):
    sig = _sigmoid(hf)
    f = lbv + (1.0 - lbv) * sig
    return sig, f, jnp.log(f), 1.0 - f, hq * (HG_DK ** -0.5)


def _hg_sel_rows(ref, sp):
    return jnp.concatenate(
        [jnp.broadcast_to(ref[pl.ds(HG_SUB * i + sp, 1), :], (HG_SUB, HG_DK)) for i in range(HG_CHUNK // HG_SUB)], axis=0)


def _hg_masks():
    C = HG_CHUNK
    row = lax.broadcasted_iota(jnp.int32, (C, C), 0)
    col = lax.broadcasted_iota(jnp.int32, (C, C), 1)
    d = col - (row // HG_SUB) * HG_SUB
    tmod = row % HG_SUB
    diag_valid = jnp.logical_and(d >= 0, d <= tmod)
    return row, col, d, diag_valid


def _hg_scores(q, k, b, b_sc, k_sc):
    C, S = HG_CHUNK, HG_SUB
    row, col, d, diag_valid = _hg_masks()
    blocks = [jnp.zeros((S, C), F32)]
    for i in range(1, C // S):
        r = b_sc[pl.ds(S * i - 1, 1), :]
        qi = q[S * i:S * (i + 1)] * jnp.exp(b[S * i:S * (i + 1)] - r)
        kk = k * jnp.exp(jnp.minimum(r - b, 0.0))
        blocks.append(_dot_nt(qi, kk))
    a_off = jnp.where(col < (row // S) * S, jnp.concatenate(blocks, axis=0), 0.0)
    a_d = jnp.zeros((C, C), F32)
    for sp in range(S):
        bs = _hg_sel_rows(b_sc, sp)
        ks = _hg_sel_rows(k_sc, sp)
        e = jnp.exp(jnp.minimum(b - bs, 0.0))
        colv = jnp.sum(q * ks * e, axis=-1, keepdims=True)
        a_d = jnp.where(d == sp, colv, a_d)
    return a_off + jnp.where(diag_valid, a_d, 0.0)


def _hg_prep(hq_v, hf_v, lbv, b_sc, k_sc):
    sig, f, g, k, q = _hg_gates(hq_v, hf_v, lbv)
    b = _cumsum_rows(g)
    b_sc[...] = b
    k_sc[...] = k
    return sig, f, k, q, b, b_sc[pl.ds(HG_CHUNK - 1, 1), :]


def _hgrn_fwd(u, w_in, lb, *, name):
    T, D = u.shape
    C, H, K = HG_CHUNK, HG_HEADS, HG_DK
    NC = T // C

    def body(u_ref, w_ref, lb_ref, hq_ref, hf_ref, hi_ref, hg_ref, att_ref, o_ref, st_ref, s_sc, b_sc, k_sc):
        @pl.when(pl.program_id(0) == 0)
        def _():
            s_sc[...] = jnp.zeros_like(s_sc)

        proj = lax.dot_general(u_ref[...], w_ref[...], (((1,), (1,)), ((), ())), preferred_element_type=F32)
        for i, ref in enumerate((hq_ref, hf_ref, hi_ref, hg_ref)):
            ref[...] = proj[:, H * K * i:H * K * (i + 1)]
        att_ref[...] = proj[:, 4 * H * K:]
        st_all = s_sc[...]
        for j in range(P):
            rows = slice(C * j, C * (j + 1))
            st_ref[j] = st_all
            outs, news = [], []
            for h in range(H):
                sl = slice(K * h, K * (h + 1))
                _, _, k, q, b, bc = _hg_prep(hq_ref[rows, sl], hf_ref[rows, sl], lb_ref[:, sl], b_sc.at[j, h], k_sc.at[j, h])
                v = hi_ref[rows, sl]
                st0 = st_all[:, sl]
                a = _hg_scores(q, k, b, b_sc.at[j, h], k_sc.at[j, h])
                outs.append(_dot_nn(a, v) + _dot_nt(q * jnp.exp(b), st0))
                news.append(st0 * jnp.exp(bc) + _dot_tn(v, k * jnp.exp(bc - b)))
            o_ref[rows, :] = jnp.concatenate(outs, axis=1)
            st_all = jnp.concatenate(news, axis=1)
        s_sc[...] = st_all

    P = HG_FWD_CHUNKS_PER_STEP
    tile = lambda n: pl.BlockSpec((P * C, n), lambda c: (c, 0))
    blk = tile(H * K)
    return pl.pallas_call(
        body, name=name, grid=(NC // P,),
        in_specs=[tile(D), _wspec(w_in), pl.BlockSpec((1, H * K), lambda c: (0, 0))],
        out_specs=[blk, blk, blk, blk, tile(ATT_COLS), blk, pl.BlockSpec((P, K, H * K), lambda c: (c, 0, 0))],
        out_shape=[_sds((T, H * K), F32)] * 4 + [_sds((T, ATT_COLS), F32), _sds((T, H * K), F32), _sds((NC, K, H * K), F32)],
        scratch_shapes=[pltpu.VMEM((K, H * K), F32), pltpu.VMEM((P, H, C, K), F32), pltpu.VMEM((P, H, C, K), F32)],
        compiler_params=_cp(("arbitrary",)),
    )(u, w_in[0], lb)


def _hgrn_bwd(hq, hf, hi, lb, states, do, *, name):
    T = hq.shape[0]
    C, H, K, S = HG_CHUNK, HG_HEADS, HG_DK, HG_SUB
    NC = T // C

    def intra_slow(q, k, b, da, b_sc, k_sc):
        row, col, d, diag_valid = _hg_masks()
        a_blocks = [jnp.zeros((S, C), F32)]
        dq_blocks = [jnp.zeros((S, K), F32)]
        dk = jnp.zeros((C, K), F32)
        for i in range(1, C // S):
            r = b_sc[pl.ds(S * i - 1, 1), :]
            eq = jnp.exp(b[S * i:S * (i + 1)] - r)
            ek = jnp.exp(jnp.minimum(r - b, 0.0))
            qi = q[S * i:S * (i + 1)] * eq
            kk = k * ek
            a_blocks.append(_dot_nt(qi, kk))
            dai = jnp.where(col[S * i:S * (i + 1)] < S * i, da[S * i:S * (i + 1)], 0.0)
            dq_blocks.append(_dot_nn(dai, kk) * eq)
            dk = dk + _dot_tn(dai, qi) * ek
        dq = jnp.concatenate(dq_blocks, axis=0)
        a_off = jnp.where(col < (row // S) * S, jnp.concatenate(a_blocks, axis=0), 0.0)
        same_blk = (row // S == col // S).astype(BF16)
        tmod = (lax.broadcasted_iota(jnp.int32, (C, K), 0)) % S
        a_d = jnp.zeros((C, C), F32)
        for sp in range(S):
            bs = _hg_sel_rows(b_sc, sp)
            ks = _hg_sel_rows(k_sc, sp)
            e = jnp.where(tmod >= sp, jnp.exp(jnp.minimum(b - bs, 0.0)), 0.0)
            eks = e * ks
            a_d = jnp.where(d == sp, jnp.sum(q * eks, axis=-1, keepdims=True), a_d)
            dacol = jnp.sum(jnp.where(d == sp, da, 0.0), axis=-1, keepdims=True)
            dq = dq + dacol * eks
            wq = dacol * e * q
            wq_hi = wq.astype(BF16)
            wq_lo = (wq - wq_hi.astype(F32)).astype(BF16)
            blk_sum = (jnp.dot(same_blk, wq_hi, preferred_element_type=F32)
                       + jnp.dot(same_blk, wq_lo, preferred_element_type=F32))
            dk = dk + jnp.where(tmod == sp, blk_sum, 0.0)
        return a_off + jnp.where(diag_valid, a_d, 0.0), dq, dk

    def one_head(pre, v, lbv, st0, dst1, dout, b_sc, k_sc):
        sig, f, k, q, b, bc = pre
        ebc = jnp.exp(bc)
        eb = jnp.exp(b)
        ekb = jnp.exp(bc - b)
        qt = q * eb
        kb = k * ekb
        row = lax.broadcasted_iota(jnp.int32, (C, C), 0)
        col = lax.broadcasted_iota(jnp.int32, (C, C), 1)
        da = jnp.where(col <= row, _dot_nt(dout, v), 0.0)
        dkb = _dot_nn(v, dst1)
        new_ds = _dot_tn(dout, qt) + dst1 * ebc
        a, dq_i, dk_i = intra_slow(q, k, b, da, b_sc, k_sc)
        dq = _dot_nn(dout, st0) * eb + dq_i
        dk = dkb * ekb + dk_i
        dv = _dot_tn(a, dout) + _dot_nt(kb, dst1)
        extra = jnp.sum(dkb * kb, axis=0, keepdims=True) + ebc * jnp.sum(st0 * dst1, axis=0, keepdims=True)
        rowk = lax.broadcasted_iota(jnp.int32, (C, K), 0)
        db = q * dq - k * dk + jnp.where(rowk == C - 1, extra, 0.0)
        dg = _rcumsum_rows(db)
        df = dg / f - dk
        return (dq * (K ** -0.5), df * (1.0 - lbv) * sig * (1.0 - sig), dv,
                jnp.sum(df * (1.0 - sig), axis=0, keepdims=True), new_ds)

    def body(hq_ref, hf_ref, hi_ref, lb_ref, st_ref, do_ref, dq_ref, dhf_ref, dv_ref, dlb_ref, ds_sc, b_sc, k_sc):
        @pl.when(pl.program_id(0) == 0)
        def _():
            ds_sc[...] = jnp.zeros_like(ds_sc)
            dlb_ref[...] = jnp.zeros_like(dlb_ref)

        ds_all = ds_sc[...]
        dlb = jnp.zeros((1, H * K), F32)
        for j in reversed(range(P)):
            rows = slice(C * j, C * (j + 1))
            st_all = st_ref[j]
            res = []
            for h in range(H):
                sl = slice(K * h, K * (h + 1))
                pre = _hg_prep(hq_ref[rows, sl], hf_ref[rows, sl], lb_ref[:, sl], b_sc.at[j, h], k_sc.at[j, h])
                res.append(one_head(pre, hi_ref[rows, sl], lb_ref[:, sl], st_all[:, sl], ds_all[:, sl], do_ref[rows, sl],
                                    b_sc.at[j, h], k_sc.at[j, h]))
            cat = lambda i: jnp.concatenate([r[i] for r in res], axis=1)
            dq_ref[rows, :] = cat(0).astype(dq_ref.dtype)
            dhf_ref[rows, :] = cat(1).astype(dhf_ref.dtype)
            dv_ref[rows, :] = cat(2).astype(dv_ref.dtype)
            dlb = dlb + cat(3)
            ds_all = cat(4)
        dlb_ref[...] += dlb
        ds_sc[...] = ds_all

    P = HG_CHUNKS_PER_STEP
    NS = NC // P
    blk = pl.BlockSpec((P * C, H * K), lambda c: (NS - 1 - c, 0))
    par = pl.BlockSpec((1, H * K), lambda c: (0, 0))
    return pl.pallas_call(
        body, name=name, grid=(NS,),
        in_specs=[blk, blk, blk, par, pl.BlockSpec((P, K, H * K), lambda c: (NS - 1 - c, 0, 0)), blk],
        out_specs=[blk, blk, blk, par],
        out_shape=[_sds((T, H * K), BF16)] * 3 + [_sds((1, H * K), F32)],
        scratch_shapes=[pltpu.VMEM((K, H * K), F32), pltpu.VMEM((P, H, C, K), F32), pltpu.VMEM((P, H, C, K), F32)],
        compiler_params=_cp(("arbitrary",)),
    )(hq, hf, hi, lb, states, do)


ATT_STACK = ATT_GROUP


def _att_valid(n):
    R, B = ATT_STACK * ATT_BLOCK, ATT_BLOCK
    j = lax.broadcasted_iota(jnp.int32, (2 * B, R), 0)
    t = lax.broadcasted_iota(jnp.int32, (2 * B, R), 1) % B
    dist = t + B - j
    first_key = jnp.where(n > 0, 0, B)
    return jnp.logical_and(jnp.logical_and(dist >= 0, dist < B), j >= first_key)


def _att_load(cur_ref, prev_ref, ba_ref, h0):
    hd = ATT_HD
    kv = h0 // ATT_GROUP
    def cols(ref, c0):
        return ref[:, c0:c0 + hd] + ba_ref[:, c0:c0 + hd]
    qs = jnp.concatenate([cols(cur_ref, hd * (h0 + g)) for g in range(ATT_STACK)], axis=0)
    kc = jnp.concatenate([cols(prev_ref, ATT_Q_W + hd * kv), cols(cur_ref, ATT_Q_W + hd * kv)], axis=0)
    vc = jnp.concatenate([cols(prev_ref, ATT_Q_W + ATT_KV_W + hd * kv), cols(cur_ref, ATT_Q_W + ATT_KV_W + hd * kv)], axis=0)
    return qs, kc, vc


def _att_probs(qs, kc, valid, sink_ref, h0):
    scale = 1.0 / math.sqrt(ATT_HD)
    s = jnp.where(valid, _dot_nt(kc, qs) * scale, NEG)
    sink = jnp.concatenate([jnp.full((1, ATT_BLOCK), sink_ref[0, h0 + g], F32) for g in range(ATT_STACK)], axis=1)
    m = jnp.maximum(jnp.max(s, axis=0, keepdims=True), sink)
    p = jnp.exp(s - m)
    ps = jnp.exp(sink - m)
    inv = 1.0 / (jnp.sum(p, axis=0, keepdims=True) + ps)
    return p * inv, ps * inv


def _attn_fwd(att, b_attn, sinks, *, name, after=None):
    T = att.shape[0]
    B = ATT_BLOCK
    NB = T // B
    lead = [] if after is None else [after]

    def body(*refs):
        sink_ref, cur_ref, prev_ref, ba_ref, o_ref = refs[len(lead):]
        valid = _att_valid(pl.program_id(0))
        outs = []
        for h0 in range(0, ATT_HEADS, ATT_STACK):
            qs, kc, vc = _att_load(cur_ref, prev_ref, ba_ref, h0)
            prob, _ = _att_probs(qs, kc, valid, sink_ref, h0)
            o = _dot_tn(prob, vc)
            outs += [o[B * g:B * (g + 1)] for g in range(ATT_STACK)]
        o_ref[...] = jnp.concatenate(outs, axis=1)

    return pl.pallas_call(
        body, name=name, grid=(NB,),
        in_specs=[pl.BlockSpec(memory_space=pl.ANY)] * len(lead) + [
            pl.BlockSpec(memory_space=pltpu.SMEM),
            pl.BlockSpec((B, ATT_COLS), lambda n: (n, 0)),
            pl.BlockSpec((B, ATT_COLS), lambda n: (jnp.maximum(n - 1, 0), 0)),
            pl.BlockSpec((1, ATT_COLS), lambda n: (0, 0))],
        out_specs=pl.BlockSpec((B, ATT_Q_W), lambda n: (n, 0)),
        out_shape=_sds((T, ATT_Q_W), F32),
        compiler_params=_cp(("parallel",)),
    )(*lead, sinks, att, att, b_attn)


def _attn_bwd(att, b_attn, sinks, dmix, *, name):
    T = att.shape[0]
    B, hd = ATT_BLOCK, ATT_HD
    NB = T // B
    scale = 1.0 / math.sqrt(hd)

    def body(sink_ref, cur_ref, prev_ref, ba_ref, do_ref, daq_ref, dakv_ref, dsink_ref, dbq_ref, dbkv_ref, carry_sc):
        n = pl.program_id(0)

        @pl.when(n == 0)
        def _():
            carry_sc[...] = jnp.zeros_like(carry_sc)
            dsink_ref[...] = jnp.zeros_like(dsink_ref)
            dbq_ref[...] = jnp.zeros_like(dbq_ref)
            dbkv_ref[...] = jnp.zeros_like(dbkv_ref)

        @pl.when(n < NB)
        def _():
            valid = _att_valid(n)
            hrow = lax.broadcasted_iota(jnp.int32, (SUBLANES, 128), 0)
            dsink = jnp.zeros((SUBLANES, 128), F32)
            dqs = []
            dks = [jnp.zeros((2 * B, hd), F32)] * ATT_KV
            dvs = [jnp.zeros((2 * B, hd), F32)] * ATT_KV
            for h0 in range(0, ATT_HEADS, ATT_STACK):
                kv = h0 // ATT_GROUP
                qs, kc, vc = _att_load(cur_ref, prev_ref, ba_ref, h0)
                prob, psink = _att_probs(qs, kc, valid, sink_ref, h0)
                dout = jnp.concatenate([do_ref[:, hd * (h0 + g):hd * (h0 + g + 1)] for g in range(ATT_STACK)], axis=0)
                dp = _dot_nt(vc, dout)
                delta = jnp.sum(prob * dp, axis=0, keepdims=True)
                dsc = prob * (dp - delta) * scale
                dq = _dot_tn(dsc, kc)
                dks[kv] = dks[kv] + _dot_nn(dsc, qs)
                dvs[kv] = dvs[kv] + _dot_nn(prob, dout)
                dsk = psink * delta
                for g in range(ATT_STACK):
                    dqs.append(dq[B * g:B * (g + 1)])
                    tot = jnp.sum(dsk[:, B * g:B * (g + 1)], axis=1, keepdims=True)
                    dsink = dsink - jnp.where(hrow == h0 + g, tot, 0.0)
            daq = jnp.concatenate(dqs, axis=1).astype(daq_ref.dtype)
            daq_ref[...] = daq
            dsink_ref[...] += dsink
            dbq_ref[...] += jnp.sum(daq.astype(F32), axis=0, keepdims=True)
            done = carry_sc[...] + jnp.concatenate([d[:B] for d in dks + dvs], axis=1)
            dakv_ref[...] = done.astype(dakv_ref.dtype)
            dbkv_ref[...] += jnp.sum(done.astype(dakv_ref.dtype).astype(F32), axis=0, keepdims=True)
            carry_sc[...] = jnp.concatenate([d[B:] for d in dks + dvs], axis=1)

        @pl.when(n == NB)
        def _():
            done = carry_sc[...]
            dakv_ref[...] = done.astype(dakv_ref.dtype)
            dbkv_ref[...] += jnp.sum(done.astype(dakv_ref.dtype).astype(F32), axis=0, keepdims=True)

    cl = lambda n: jnp.minimum(n, NB - 1)
    return pl.pallas_call(
        body, name=name, grid=(NB + 1,),
        in_specs=[pl.BlockSpec(memory_space=pltpu.SMEM),
                  pl.BlockSpec((B, ATT_COLS), lambda n: (cl(n), 0)),
                  pl.BlockSpec((B, ATT_COLS), lambda n: (jnp.maximum(cl(n) - 1, 0), 0)),
                  pl.BlockSpec((1, ATT_COLS), lambda n: (0, 0)),
                  pl.BlockSpec((B, ATT_Q_W), lambda n: (cl(n), 0))],
        out_specs=[pl.BlockSpec((B, ATT_Q_W), lambda n: (cl(n), 0)),
                   pl.BlockSpec((B, 2 * ATT_KV_W), lambda n: (jnp.maximum(n - 1, 0), 0)),
                   pl.BlockSpec((SUBLANES, 128), lambda n: (0, 0)),
                   pl.BlockSpec((1, ATT_Q_W), lambda n: (0, 0)),
                   pl.BlockSpec((1, 2 * ATT_KV_W), lambda n: (0, 0))],
        out_shape=[_sds((T, ATT_Q_W), BF16), _sds((T, 2 * ATT_KV_W), BF16), _sds((SUBLANES, 128), F32),
                   _sds((1, ATT_Q_W), F32), _sds((1, 2 * ATT_KV_W), F32)],
        scratch_shapes=[pltpu.VMEM((B, 2 * ATT_KV_W), F32)],
        compiler_params=_cp(("arbitrary",)),
    )(sinks, att, att, b_attn, dmix)


def _silu_and_grad(x):
    sg = _sigmoid(x)
    return x * sg, sg * (1.0 + x * (1.0 - sg))


def _mix_fwd_fn(o_raw, hg, o_att, hgw):
    outs = []
    for h in range(HG_HEADS):
        sl = slice(HG_DK * h, HG_DK * (h + 1))
        silu, _ = _silu_and_grad(hg[:, sl])
        outs.append(_rms_fwd(o_raw[:, sl], hgw) * silu)
    outs.append(o_att)
    return (jnp.concatenate(outs, axis=1),)


def _mix_bwd_fn(o_raw, hg, dmix, hgw):
    dos, dhgs = [], []
    dw = jnp.zeros((1, HG_DK), F32)
    for h in range(HG_HEADS):
        sl = slice(HG_DK * h, HG_DK * (h + 1))
        silu, dsilu = _silu_and_grad(hg[:, sl])
        dy = dmix[:, sl]
        dhgs.append(dy * _rms_fwd(o_raw[:, sl], hgw) * dsilu)
        dx, dwh = _rms_bwd(o_raw[:, sl], hgw, dy * silu)
        dos.append(dx)
        dw = dw + dwh
    return jnp.concatenate(dos, axis=1), jnp.concatenate(dhgs, axis=1), dw


def _final_fn(h2, tgt, wf):
    d = h2.shape[1]
    err = _rms_fwd(h2, wf) - tgt
    loss_cols = (0.5 / d) * jnp.sum(err * err, axis=0, keepdims=True)
    dh2, dwf = _rms_bwd(h2, wf, err * (1.0 / d))
    return dh2, dh2, loss_cols, dwf


class _NoExchange:
    def __init__(self, weights):
        self.weights = weights

    def start(self):
        return None

    def w_in(self, after):
        return self.weights["w_in_t"]

    def mid(self, after):
        return None

    def rest(self, after):
        return self.weights

    def ffn_grads(self, gs):
        return None

    def ffn_grads_send(self, after):
        return None


def _local_step(x, tgt, p, ex):
    T, D = x.shape
    row = lambda n, dt: _sds((T, n), dt)
    acc = lambda n: _sds((1, n), F32)

    (u,) = _rowwise(lambda xv, w: (_rms_fwd(xv, w),), [_full(x)], [p["norm_mix_w"]], [row(D, BF16)], [], name="rms_mix",
                    after=ex.start())
    p = dict(p, w_in_t=ex.w_in(u))
    hq, hf, hi, hg, att, o_raw, states = _hgrn_fwd(u, p["w_in_t"], p["lb"], name="proj_hgrn_fwd")
    o_att = _attn_fwd(att, p["b_attn"], p["sinks"], name="attn_fwd", after=ex.mid(o_raw))
    p = dict(p, **ex.rest(o_att))
    def out_epilogue(prod, xv, w):
        h1v = prod + xv
        return h1v, _rms_fwd(h1v, w)

    h1, v, mix = _mm_nn(None, [p["w_out"]], name="mix_out_proj",
                        prologue=(lambda *a: _mix_fwd_fn(*a)[0], [o_raw, hg, o_att], [p["hg_norm_w"]], row(D, BF16)),
                        epilogue=(out_epilogue, [x], [p["norm_ffn_w"]], [row(D, F32), row(D, BF16)], []))
    gp, up, gate, act = _ffn_in(v, p["w_gate_t"], p["w_up_t"], p["conv_w8"], p["conv_b"], name="ffn_in")
    def down_epilogue(prod, h1v, tgtv, wf):
        return _final_fn(prod + h1v, tgtv, wf)

    dh2, dh2_b, loss_cols, d_final = _mm_nn(
        [[act]], [p["w_down"]], name="down_proj_loss",
        epilogue=(down_epilogue, [h1, tgt], [p["final_norm_w"]], [row(D, F32), row(D, BF16)], [acc(D), acc(D)]))

    g_down = _mm_tn([act], dh2_b, name="g_down")
    dgp, dup, d_conv_w8, d_conv_b = _ffn_back(dh2_b, p["w_down"], gp, up, gate, p["conv_w8"], name="ffn_back")
    g_gate_t = _mm_tn([dgp], v, name="g_gate")
    g_up_t = _mm_tn([dup], v, name="g_up")
    swapping = ex.ffn_grads([g_gate_t, g_up_t, g_down])

    def ffn_norm_bwd(dvv, hv, dh2v, w):
        dx, dw = _rms_bwd(hv, w, dvv)
        dh1v = dx + dh2v
        return dh1v, dh1v, dw

    dh1, dh1_b, d_norm_ffn = _mm_nn(
        [[dgp], [dup]], [p["w_gate_t"], p["w_up_t"]], name="d_v_norm", after=swapping,
        epilogue=(ffn_norm_bwd, [h1, dh2], [p["norm_ffn_w"]], [row(D, F32), row(D, BF16)], [acc(D)]))
    sent = ex.ffn_grads_send(dh1_b)
    def mix_bwd(dmixv, o_rawv, hgv, hgw):
        do_rawv, dhgv, dw = _mix_bwd_fn(o_rawv, hgv, dmixv[:, :HG_W], hgw)
        return do_rawv, dhgv, dmixv[:, HG_W:], dw

    do_raw, dhg, do_att, d_hg_norm = _mm_nn(
        [[dh1_b]], [p["w_out"]], name="d_mix_bwd", w_transposed=True, after=sent,
        epilogue=(mix_bwd, [o_raw, hg], [p["hg_norm_w"]], [row(HG_W, F32), row(HG_W, BF16), row(ATT_Q_W, F32)], [acc(HG_DK)]))
    g_out = _mm_tn([mix], dh1_b, name="g_out")
    daq, dakv, d_sinks8, d_bq, d_bkv = _attn_bwd(att, p["b_attn"], p["sinks"], do_att, name="attn_bwd")
    dhq, dhf, dhi, d_lb = _hgrn_bwd(hq, hf, hi, p["lb"], states, do_raw, name="hgrn_bwd")
    pieces = [dhq, dhf, dhi, dhg, daq, dakv]
    g_in_t = _mm_tn(pieces, u, name="g_in")

    def mix_norm_bwd(duv, xv, dh1v, w):
        dx, dw = _rms_bwd(xv, w, duv)
        return dx + dh1v, dw

    dx, d_norm_mix = _mm_nn([pieces], [p["w_in_t"]], name="d_u_norm",
                            epilogue=(mix_norm_bwd, [x, dh1], [p["norm_mix_w"]], [row(D, F32)], [acc(D)]))
    grads = dict(g_in_t=g_in_t, g_out=g_out, g_gate_t=g_gate_t, g_up_t=g_up_t, g_down=g_down,
                 norm_mix_w=d_norm_mix, b_attn=jnp.concatenate([d_bq, d_bkv], axis=1), lb=d_lb, hg_norm_w=d_hg_norm,
                 sinks8=d_sinks8, norm_ffn_w=d_norm_ffn, conv_w8=d_conv_w8, conv_b=d_conv_b, final_norm_w=d_final)
    return loss_cols, dx, grads


SLAB = (IN_COLS // N_CHIPS, D_FF // N_CHIPS, D_FF // N_CHIPS, D_FF // N_CHIPS, D_MODEL // N_CHIPS)
N_W = len(SLAB)
PACK_OFF = tuple(sum(SLAB[:i]) for i in range(N_W))
PACK_ROWS = sum(SLAB)
FULL_OFF = tuple(N_CHIPS * o for o in PACK_OFF)
FULL_ROWS = N_CHIPS * PACK_ROWS
HALF = tuple(s // 2 for s in SLAB)
HPACK_OFF = tuple(sum(HALF[:i]) for i in range(N_W))
HPACK_ROWS = sum(HALF)
HFULL_OFF = tuple(N_CHIPS * o for o in HPACK_OFF)
HFULL_ROWS = N_CHIPS * HPACK_ROWS
CHIP_FLIPS = ((1, 0), (0, 1), (1, 1))
N_DEV = 8
BF16_ROWS = 16
ANY = pl.BlockSpec(memory_space=pl.ANY)


def _pos():
    return lax.axis_index("x"), lax.axis_index("y"), lax.axis_index("c")


def _flip(v, f):
    return 1 - v if f else v


def _rcopy(src, dst, ssem, rsem, dev):
    return pltpu.make_async_remote_copy(src_ref=src, dst_ref=dst, send_sem=ssem, recv_sem=rsem, device_id=dev,
                                        device_id_type=pl.DeviceIdType.MESH)


def _rows(ref, start, n, align=None):
    if not isinstance(start, int):
        if align is None:
            align = SUBLANES * (4 // jnp.dtype(ref.dtype).itemsize)
        start = pl.multiple_of(start, align)
    return ref.at[pl.ds(start, n), :]


FFN_W = (1, 2, 3)
N_PEER = 1 + len(CHIP_FLIPS)
HBM = pl.BlockSpec(memory_space=pltpu.HBM)
SEM = pl.BlockSpec(memory_space=pltpu.SEMAPHORE)
EFFECT = pltpu.SideEffectType.DATAFLOW_SIDE_EFFECTING
LANES = 128


def _sent_rows(k, w, c):
    return (0, SLAB[w]) if k == 0 else (c * HALF[w], HALF[w])


def _gather_start(pack, cw8):
    D = pack.shape[1]
    lands = [lax.empty((N_CHIPS * SLAB[0], D), pack.dtype), lax.empty((3 * N_CHIPS * SLAB[1], D), pack.dtype),
             lax.empty((N_CHIPS * SLAB[4], D), pack.dtype), lax.empty((N_CHIPS,) + cw8.shape, cw8.dtype)]
    bufs = [pack, cw8] + lands

    def body(pack_ref, cw_ref, l_in, l_ffn, l_out, l_cw, *rest):
        in_send, in_recv, out_send, out_recv, ffn_send, ffn_recv = rest[:6]
        token = rest[-1]
        x, y, c = _pos()
        q = 2 * x + y
        peers = _gather_peers(x, y, c)

        def send(k, peer, w, land, base, ssem, rsem):
            r0, n = _sent_rows(k, w, c)
            _rcopy(_rows(pack_ref, PACK_OFF[w] + r0, n), _rows(land, base + q * SLAB[w] + r0, n), ssem, rsem, peer).start()

        for k, peer in enumerate(peers):
            send(k, peer, 0, l_in, 0, in_send.at[k], in_recv.at[k])
        for k, peer in enumerate(peers):
            send(k, peer, 4, l_out, 0, out_send.at[k], out_recv.at[k])
            _rcopy(cw_ref, l_cw.at[q], out_send.at[N_PEER + k], out_recv.at[N_PEER + k], peer).start()
        for j, w in enumerate(FFN_W):
            for k, peer in enumerate(peers):
                send(k, peer, w, l_ffn, j * N_CHIPS * SLAB[w], ffn_send.at[k], ffn_recv.at[k])
        token[...] = jnp.zeros_like(token)

    n_sem = (N_PEER, N_PEER, 2 * N_PEER, 2 * N_PEER, N_PEER, N_PEER)
    outs = pl.pallas_call(
        body, name="gather_start", in_specs=[HBM] * len(bufs),
        out_specs=[SEM] * len(n_sem) + [HBM] * len(bufs) + [pl.BlockSpec(memory_space=pltpu.VMEM)],
        out_shape=[pltpu.SemaphoreType.DMA((n,)) for n in n_sem]
        + [pltpu.HBM(b.shape, b.dtype) for b in bufs] + [TOKEN],
        input_output_aliases={i: len(n_sem) + i for i in range(len(bufs))},
        compiler_params=pltpu.CompilerParams(has_side_effects=EFFECT),
    )(*[pltpu.with_memory_space_constraint(b, pltpu.HBM) for b in bufs])
    bufs_out = outs[len(n_sem):]
    return dict(in_sems=outs[0:2], out_sems=outs[2:4], ffn_sems=outs[4:6], pack=bufs_out[0], cw=bufs_out[1], l_in=bufs_out[2],
                l_ffn=bufs_out[3], l_out=bufs_out[4], l_cw=bufs_out[5], token=bufs_out[6])


def _gather_peers(x, y, c):
    return [(x, y, 1 - c)] + [(_flip(x, fx), _flip(y, fy), c) for fx, fy in CHIP_FLIPS]


def _gather_wait_in(g, after):
    def body(pack_ref, l_in, send, recv, after_ref, pack_out, l_out):
        for k, peer in enumerate(_gather_peers(*_pos())):
            n = _sent_rows(k, 0, 0)[1]
            cp = _rcopy(_rows(pack_ref, PACK_OFF[0], n), _rows(l_in, 0, n), send.at[k], recv.at[k], peer)
            cp.wait_send()
            cp.wait_recv()

    return pl.pallas_call(
        body, name="gather_wait_in", in_specs=[HBM, HBM, SEM, SEM, ANY], out_specs=[HBM, HBM],
        out_shape=[pltpu.HBM(g["pack"].shape, g["pack"].dtype), pltpu.HBM(g["l_in"].shape, g["l_in"].dtype)],
        input_output_aliases={0: 0, 1: 1}, compiler_params=pltpu.CompilerParams(has_side_effects=EFFECT),
    )(g["pack"], g["l_in"], *g["in_sems"], after)


def _gather_wait_rest(g, pack, after):
    def body(pack_ref, cw_ref, l_ffn, l_out, l_cw, o_send, o_recv, f_send, f_recv, after_ref, o_ffn, o_out, o_cw):
        for k, peer in enumerate(_gather_peers(*_pos())):
            n_out = _sent_rows(k, 4, 0)[1]
            n_ffn = len(FFN_W) * _sent_rows(k, FFN_W[0], 0)[1]
            for cp in (_rcopy(_rows(pack_ref, PACK_OFF[4], n_out), _rows(l_out, 0, n_out), o_send.at[k], o_recv.at[k], peer),
                       _rcopy(cw_ref, l_cw.at[0], o_send.at[N_PEER + k], o_recv.at[N_PEER + k], peer),
                       _rcopy(_rows(pack_ref, PACK_OFF[FFN_W[0]], n_ffn), _rows(l_ffn, 0, n_ffn), f_send.at[k], f_recv.at[k], peer)):
                cp.wait_send()
                cp.wait_recv()

    ins = [pack, g["cw"], g["l_ffn"], g["l_out"], g["l_cw"]]
    return pl.pallas_call(
        body, name="gather_wait_rest", in_specs=[HBM] * 5 + [SEM] * 4 + [ANY], out_specs=[HBM] * 3,
        out_shape=[pltpu.HBM(b.shape, b.dtype) for b in ins[2:]],
        input_output_aliases={2: 0, 3: 1, 4: 2}, compiler_params=pltpu.CompilerParams(has_side_effects=EFFECT),
    )(*ins, *g["out_sems"], *g["ffn_sems"], after)


FWD_IN = ((0, 0, 0),)
FWD_REST = tuple((0, w, j * N_CHIPS * SLAB[w]) for j, w in enumerate(FFN_W)) + ((1, 4, 0),)


def _forward_copies(layout, src, dst, send_sems, recv_sems):
    x, y, c = _pos()
    sib = (x, y, 1 - c)
    cps = []
    for fx, fy in CHIP_FLIPS:
        qa = 2 * _flip(x, fx) + _flip(y, fy)
        for bi, w, base in layout:
            r0 = base + qa * SLAB[w] + c * HALF[w]
            cps.append(_rcopy(_rows(src[bi], r0, HALF[w]), _rows(dst[bi], r0, HALF[w]),
                              send_sems.at[len(cps)], recv_sems.at[len(cps)], sib))
    return cps


def _forward_in(l_in):
    n = len(CHIP_FLIPS) * len(FWD_IN)

    def body(in_ref, out_ref, send_sems, recv_sems):
        cps = _forward_copies(FWD_IN, [in_ref], [out_ref], send_sems, recv_sems)
        for cp in cps:
            cp.start()
        for cp in cps:
            cp.wait_recv()
        for cp in cps:
            cp.wait_send()

    return pl.pallas_call(
        body, name="forward_in", in_specs=[ANY], out_specs=ANY, out_shape=_sds(l_in.shape, l_in.dtype),
        input_output_aliases={0: 0},
        scratch_shapes=[pltpu.SemaphoreType.DMA((n,)), pltpu.SemaphoreType.DMA((n,))],
    )(l_in)


def _forward_rest_start(l_ffn, l_out):
    n = len(CHIP_FLIPS) * len(FWD_REST)
    bufs = [l_ffn, l_out]

    def body(a_ref, b_ref, send_sems, recv_sems, a_out, b_out, token):
        for cp in _forward_copies(FWD_REST, [a_ref, b_ref], [a_ref, b_ref], send_sems, recv_sems):
            cp.start()
        token[...] = jnp.zeros_like(token)

    outs = pl.pallas_call(
        body, name="forward_rest_start", in_specs=[HBM] * 2,
        out_specs=[SEM, SEM, HBM, HBM, pl.BlockSpec(memory_space=pltpu.VMEM)],
        out_shape=[pltpu.SemaphoreType.DMA((n,)), pltpu.SemaphoreType.DMA((n,))]
        + [pltpu.HBM(b.shape, b.dtype) for b in bufs] + [TOKEN],
        input_output_aliases={0: 2, 1: 3}, compiler_params=pltpu.CompilerParams(has_side_effects=EFFECT),
    )(*[pltpu.with_memory_space_constraint(b, pltpu.HBM) for b in bufs])
    return dict(sems=outs[0:2], bufs=outs[2:4], token=outs[4])


def _forward_rest_wait(s, after):
    def body(a_ref, b_ref, send_sems, recv_sems, after_ref, a_out, b_out):
        for cp in _forward_copies(FWD_REST, [a_ref, b_ref], [a_ref, b_ref], send_sems, recv_sems):
            cp.wait_send()
            cp.wait_recv()

    return pl.pallas_call(
        body, name="forward_rest_wait", in_specs=[HBM, HBM, SEM, SEM, ANY], out_specs=[HBM, HBM],
        out_shape=[pltpu.HBM(b.shape, b.dtype) for b in s["bufs"]],
        input_output_aliases={0: 0, 1: 1}, compiler_params=pltpu.CompilerParams(has_side_effects=EFFECT),
    )(*s["bufs"], *s["sems"], after)


def _exchange_halves(ws, gs, small, *, name):
    D = gs[0].shape[1]
    n = len(ws)
    has_small = small is not None

    def body(*refs):
        g = refs[:n]
        t = refs[n + has_small:2 * n + has_small]
        sems = refs[2 * n + 2 * has_small:]
        d2d_send, d2d_recv = sems[0], sems[1]
        x, y, c = _pos()
        sib = (x, y, 1 - c)
        drains = []
        for i, w in enumerate(ws):
            h = HALF[w]
            for qq in range(N_CHIPS):
                _rcopy(_rows(g[i], qq * SLAB[w] + (1 - c) * h, h), _rows(t[i], qq * h, h),
                       d2d_send.at[i], d2d_recv.at[i], sib).start()
            drains.append(_rcopy(t[i], t[i], d2d_send.at[i], d2d_recv.at[i], sib))
        if has_small:
            small_ref, sall_ref = refs[n], refs[2 * n + 1]
            sm_send, sm_recv, loc_sem = sems[2], sems[3], sems[4]
            me = 4 * x + 2 * y + c
            own_small = pltpu.make_async_copy(small_ref, sall_ref.at[me], loc_sem)
            own_small.start()
            for f in range(1, N_DEV):
                peer = (_flip(x, f & 4), _flip(y, f & 2), _flip(c, f & 1))
                cp = _rcopy(small_ref, sall_ref.at[me], sm_send.at[f - 1], sm_recv.at[f - 1], peer)
                cp.start()
                drains.append(cp)
        for d in drains:
            d.wait_recv()
        for d in drains:
            d.wait_send()
        if has_small:
            own_small.wait()

    out_shape = [_sds((N_CHIPS * HALF[w], D), gs[0].dtype) for w in ws]
    scratch = [pltpu.SemaphoreType.DMA((n,)), pltpu.SemaphoreType.DMA((n,))]
    if has_small:
        out_shape.append(_sds((N_DEV,) + small.shape, F32))
        scratch += [pltpu.SemaphoreType.DMA((N_DEV - 1,)), pltpu.SemaphoreType.DMA((N_DEV - 1,)), pltpu.SemaphoreType.DMA]
    return pl.pallas_call(
        body, name=name, in_specs=[ANY] * (n + has_small), out_specs=[ANY] * (n + has_small),
        out_shape=out_shape, scratch_shapes=scratch,
    )(*gs, *([small] if has_small else []))


def _halves_copies(ws, g, t, send_sems, recv_sems):
    x, y, c = _pos()
    sib = (x, y, 1 - c)
    cps = []
    for i, w in enumerate(ws):
        h = HALF[w]
        for qq in range(N_CHIPS):
            cps.append(_rcopy(_rows(g[i], qq * SLAB[w] + (1 - c) * h, h), _rows(t[i], qq * h, h),
                              send_sems.at[N_CHIPS * i + qq], recv_sems.at[N_CHIPS * i + qq], sib))
    return cps


def _halves_start(ws, gs, *, name):
    D = gs[0].shape[1]
    n = len(ws)
    bufs = list(gs) + [lax.empty((N_CHIPS * HALF[w], D), gs[0].dtype) for w in ws]

    def body(*refs):
        for cp in _halves_copies(ws, refs[:n], refs[n:2 * n], refs[2 * n], refs[2 * n + 1]):
            cp.start()
        refs[-1][...] = jnp.zeros_like(refs[-1])

    outs = pl.pallas_call(
        body, name=name, in_specs=[HBM] * (2 * n),
        out_specs=[SEM, SEM] + [HBM] * (2 * n) + [pl.BlockSpec(memory_space=pltpu.VMEM)],
        out_shape=[pltpu.SemaphoreType.DMA((N_CHIPS * n,)), pltpu.SemaphoreType.DMA((N_CHIPS * n,))]
        + [pltpu.HBM(b.shape, b.dtype) for b in bufs] + [TOKEN],
        input_output_aliases={i: 2 + i for i in range(2 * n)},
        compiler_params=pltpu.CompilerParams(has_side_effects=EFFECT),
    )(*[pltpu.with_memory_space_constraint(b, pltpu.HBM) for b in bufs])
    return dict(sems=outs[0:2], gs=outs[2:2 + n], theirs=outs[2 + n:2 + 2 * n], token=outs[-1])


def _halves_wait(ws, s, after, *, name):
    n = len(ws)

    def body(*refs):
        for cp in _halves_copies(ws, refs[:n], refs[n:2 * n], refs[2 * n], refs[2 * n + 1]):
            cp.wait_send()
            cp.wait_recv()

    bufs = list(s["gs"]) + list(s["theirs"])
    outs = pl.pallas_call(
        body, name=name, in_specs=[HBM] * (2 * n) + [SEM, SEM, ANY], out_specs=[HBM] * (2 * n),
        out_shape=[pltpu.HBM(b.shape, b.dtype) for b in bufs],
        input_output_aliases={i: i for i in range(2 * n)},
        compiler_params=pltpu.CompilerParams(has_side_effects=EFFECT),
    )(*bufs, *s["sems"], after)
    return outs[:n], outs[n:]


REDUCE_SPLIT = 2


def _chip_partial(ws, gs, theirs, *, name, out_dtype=F32):
    D = gs[0].shape[1]
    n = len(ws)

    def body(*refs):
        for i in range(n):
            refs[2 * n + i][...] = (refs[i][...].astype(F32) + refs[n + i][...].astype(F32)).astype(out_dtype)

    blk = [HALF[w] // REDUCE_SPLIT for w in ws]
    mine = [pl.BlockSpec((b, D), lambda qq, j: ((2 * qq + lax.axis_index("c")) * REDUCE_SPLIT + j, 0)) for b in blk]
    flat = [pl.BlockSpec((b, D), lambda qq, j: (qq * REDUCE_SPLIT + j, 0)) for b in blk]
    return pl.pallas_call(
        body, name=name, grid=(N_CHIPS, REDUCE_SPLIT), in_specs=mine + flat, out_specs=flat,
        out_shape=[_sds((N_CHIPS * HALF[w], D), out_dtype) for w in ws],
        compiler_params=_cp(("parallel", "parallel")),
    )(*gs, *theirs)


def _partial_copies(ws, part, got, send_sems, recv_sems):
    x, y, c = _pos()
    cps = []
    for k, (fx, fy) in enumerate(CHIP_FLIPS):
        peer = (_flip(x, fx), _flip(y, fy), c)
        qp = 2 * _flip(x, fx) + _flip(y, fy)
        for i, w in enumerate(ws):
            cps.append(_rcopy(_rows(part[i], qp * HALF[w], HALF[w]), _rows(got[i], k * HALF[w], HALF[w]),
                              send_sems.at[len(ws) * k + i], recv_sems.at[len(ws) * k + i], peer))
    return cps


def _send_chip_partials(ws, parts, *, name):
    D = parts[0].shape[1]
    n = len(ws)

    def body(*refs):
        cps = _partial_copies(ws, refs[:n], refs[n:2 * n], refs[2 * n], refs[2 * n + 1])
        for cp in cps:
            cp.start()
        for cp in cps:
            cp.wait_recv()
        for cp in cps:
            cp.wait_send()

    return pl.pallas_call(
        body, name=name, in_specs=[ANY] * n, out_specs=[ANY] * n,
        out_shape=[_sds((len(CHIP_FLIPS) * HALF[w], D), parts[0].dtype) for w in ws],
        scratch_shapes=[pltpu.SemaphoreType.DMA((len(CHIP_FLIPS) * n,)), pltpu.SemaphoreType.DMA((len(CHIP_FLIPS) * n,))],
    )(*parts)


def _send_start(ws, parts, *, name):
    D = parts[0].shape[1]
    n = len(ws)
    bufs = list(parts) + [lax.empty((len(CHIP_FLIPS) * HALF[w], D), parts[0].dtype) for w in ws]

    def body(*refs):
        send_sems, recv_sems = refs[2 * n], refs[2 * n + 1]
        for cp in _partial_copies(ws, refs[:n], refs[n:2 * n], send_sems, recv_sems):
            cp.start()
        refs[-1][...] = jnp.zeros_like(refs[-1])

    outs = pl.pallas_call(
        body, name=name, in_specs=[HBM] * (2 * n),
        out_specs=[SEM, SEM] + [HBM] * (2 * n) + [pl.BlockSpec(memory_space=pltpu.VMEM)],
        out_shape=[pltpu.SemaphoreType.DMA((len(CHIP_FLIPS) * n,)), pltpu.SemaphoreType.DMA((len(CHIP_FLIPS) * n,))]
        + [pltpu.HBM(b.shape, b.dtype) for b in bufs] + [TOKEN],
        input_output_aliases={i: 2 + i for i in range(2 * n)},
        compiler_params=pltpu.CompilerParams(has_side_effects=EFFECT),
    )(*[pltpu.with_memory_space_constraint(b, pltpu.HBM) for b in bufs])
    return dict(sems=outs[0:2], parts=outs[2:2 + n], got=outs[2 + n:2 + 2 * n], token=outs[-1])


def _send_wait(ws, s, after, *, name):
    n = len(ws)

    def body(*refs):
        for cp in _partial_copies(ws, refs[:n], refs[n:2 * n], refs[2 * n], refs[2 * n + 1]):
            cp.wait_send()
            cp.wait_recv()

    bufs = list(s["parts"]) + list(s["got"])
    outs = pl.pallas_call(
        body, name=name, in_specs=[HBM] * (2 * n) + [SEM, SEM] + [ANY] * len(after), out_specs=[HBM] * (2 * n),
        out_shape=[pltpu.HBM(b.shape, b.dtype) for b in bufs],
        input_output_aliases={i: i for i in range(2 * n)},
        compiler_params=pltpu.CompilerParams(has_side_effects=EFFECT),
    )(*bufs, *s["sems"], *after)
    return outs[:n], outs[n:]


def _chip_reduce(ws, parts, got, *, name, after=None):
    D = parts[0].shape[1]
    nk = len(CHIP_FLIPS)
    n = len(ws)
    extra = [] if after is None else [after]

    def body(*refs):
        refs = refs[len(extra):]
        outs = refs[(1 + nk) * n:]
        for i in range(n):
            acc = refs[i][...].astype(F32)
            for k in range(nk):
                acc = acc + refs[n * (1 + k) + i][...].astype(F32)
            outs[i][...] = acc

    blk = [HALF[w] // REDUCE_SPLIT for w in ws]

    def q_idx(j):
        return (2 * lax.axis_index("x") + lax.axis_index("y")) * REDUCE_SPLIT + j

    in_specs = [pl.BlockSpec((b, D), lambda j: (q_idx(j), 0)) for b in blk]
    for k in range(nk):
        in_specs += [pl.BlockSpec((b, D), functools.partial(lambda j, k: (k * REDUCE_SPLIT + j, 0), k=k)) for b in blk]
    out_specs = [pl.BlockSpec((b, D), lambda j: (lax.axis_index("c") * REDUCE_SPLIT + j, 0)) for b in blk]
    return pl.pallas_call(
        body, name=name, grid=(REDUCE_SPLIT,), in_specs=[ANY] * len(extra) + in_specs, out_specs=out_specs,
        out_shape=[_sds((SLAB[w], D), F32) for w in ws],
        compiler_params=_cp(("parallel",)),
    )(*extra, *parts, *[g for _ in range(nk) for g in got])


def _exchange_reduced(ws, shards, *, name):
    n = len(ws)

    def body(*refs):
        ins, outs = refs[:n], refs[n:2 * n]
        send_sems, recv_sems = refs[2 * n], refs[2 * n + 1]
        x, y, c = _pos()
        sib = (x, y, 1 - c)
        cps = []
        for i, w in enumerate(ws):
            cp = _rcopy(_rows(ins[i], c * HALF[w], HALF[w]), _rows(outs[i], c * HALF[w], HALF[w]),
                        send_sems.at[i], recv_sems.at[i], sib)
            cp.start()
            cps.append(cp)
        for cp in cps:
            cp.wait_recv()
        for cp in cps:
            cp.wait_send()

    return pl.pallas_call(
        body, name=name, in_specs=[ANY] * n, out_specs=[ANY] * n,
        out_shape=[_sds(s.shape, s.dtype) for s in shards], input_output_aliases={i: i for i in range(n)},
        scratch_shapes=[pltpu.SemaphoreType.DMA((n,)), pltpu.SemaphoreType.DMA((n,))],
    )(*shards)


def _adamw_fn(w, g, m, v):
    m2 = ADAM_B1 * m + (1.0 - ADAM_B1) * g
    v2 = ADAM_B2 * v + (1.0 - ADAM_B2) * (g * g)
    m_hat = m2 / (1.0 - ADAM_B1 ** ADAM_STEP)
    v_hat = v2 / (1.0 - ADAM_B2 ** ADAM_STEP)
    return -ADAM_LR * (m_hat / (jnp.sqrt(v_hat) + ADAM_EPS) + ADAM_WD * w), m2, v2


def _adamw(w, g, m, v, *, name):
    shp = _sds(w.shape, F32)
    rows = w.shape[0]
    tm = max(t for t in range(SUBLANES, 512 + 1, SUBLANES) if rows % t == 0)
    return _rowwise(lambda wv, gv, mv, vv: (gv, *_adamw_fn(wv, gv, mv, vv)), [_full(w), _full(g), _full(m), _full(v)], [],
                    [shp] * 4, [], name=name, tm=tm)


SMALL_SEGS = (("loss", 8), ("norm_mix_w", 8), ("b_attn", 8), ("lb_logits", 8), ("hg_norm_w", 8), ("sinks", 8),
              ("norm_ffn_w", 8), ("conv_w", 72), ("conv_b", 24), ("final_norm_w", 8))
SMALL_OFF = {n: sum(r for _, r in SMALL_SEGS[:i]) for i, (n, _) in enumerate(SMALL_SEGS)}
SMALL_ROWS = sum(r for _, r in SMALL_SEGS)
LANES = 128


def _pack_small(parts):
    segs = []
    for n, r in SMALL_SEGS:
        a = parts.get(n)
        flat = jnp.zeros((0,), F32) if a is None else a.reshape(-1).astype(F32)
        segs.append(jnp.pad(flat, (0, r * LANES - flat.shape[0])).reshape(r, LANES))
    return jnp.concatenate(segs, axis=0)


def _unpack_small(pack, n, shape):
    size = math.prod(shape)
    r0 = SMALL_OFF[n]
    return pack[r0:r0 + dict(SMALL_SEGS)[n]].reshape(-1)[:size].reshape(shape)


def _small_update(sall, wp, mp, vp, *, after):
    R = SMALL_ROWS
    r_lb = SMALL_OFF["lb_logits"]

    def body(after_ref, s_ref, w_ref, m_ref, v_ref, g_ref, d_ref, m2_ref, v2_ref, loss_ref):
        g = s_ref[0]
        for i in range(1, N_DEV):
            g = g + s_ref[i]
        tot = jnp.sum(jnp.sum(g[0:8], axis=1, keepdims=True), axis=0, keepdims=True)
        loss_ref[...] = jnp.broadcast_to(tot, loss_ref.shape)
        lg = w_ref[r_lb:r_lb + 8, :]
        p0 = _sigmoid(lg - pltpu.roll(lg, 4, 0))
        d = g[r_lb:r_lb + 8]
        d = d + pltpu.roll(d, 4, 0)
        sign = jnp.where(lax.broadcasted_iota(jnp.int32, d.shape, 0) < 4, 1.0, -1.0)
        g = jnp.concatenate([g[:r_lb], sign * d * p0 * (1.0 - p0), g[r_lb + 8:]], axis=0)
        g_ref[...] = g
        d_ref[...], m2_ref[...], v2_ref[...] = _adamw_fn(w_ref[...], g, m_ref[...], v_ref[...])

    full = pl.BlockSpec((R, LANES), lambda: (0, 0))
    return pl.pallas_call(
        body, name="small_update",
        in_specs=[ANY, pl.BlockSpec((N_DEV, R, LANES), lambda: (0, 0, 0)), full, full, full],
        out_specs=[full, full, full, full, pl.BlockSpec((8, LANES), lambda: (0, 0))],
        out_shape=[_sds((R, LANES), F32)] * 4 + [_sds((8, LANES), F32)],
        compiler_params=_cp(),
    )(after, sall, wp, mp, vp)


def _lb_fwd(lb_logits):
    n = lb_logits.shape[1]

    def body(l_ref, o_ref):
        o_ref[...] = _sigmoid(l_ref[0:1, :] - l_ref[1:2, :])

    return pl.pallas_call(body, name="lb_fwd", out_shape=jax.ShapeDtypeStruct((1, n), F32), compiler_params=_cp())(lb_logits)


class _MeshExchange:
    def __init__(self, pack, cw8):
        self.gather = _gather_start(pack, cw8)
        self.sent = None
        self.conv_w8 = None

    def start(self):
        return self.gather["token"]

    def w_in(self, after):
        self.pack, l_in = _gather_wait_in(self.gather, after)
        return (_forward_in(l_in), N_CHIPS * SLAB[0], 0)

    def mid(self, after):
        l_ffn, l_out, l_cw = _gather_wait_rest(self.gather, self.pack, after)
        self.conv_w8 = jnp.concatenate([l_cw[i] for i in range(N_CHIPS)], axis=1)
        self.passing = _forward_rest_start(l_ffn, l_out)
        return self.passing["token"]

    def rest(self, after):
        l_ffn, l_out = _forward_rest_wait(self.passing, after)
        rows = N_CHIPS * SLAB[FFN_W[0]]
        return dict(w_gate_t=(l_ffn, rows, 0), w_up_t=(l_ffn, rows, 1), w_down=(l_ffn, rows, 2),
                    w_out=(l_out, N_CHIPS * SLAB[4], 0), conv_w8=self.conv_w8)

    def ffn_grads(self, gs):
        self.swap = _halves_start(FFN_W, gs, name="halves_ffn_start")
        return self.swap["token"]

    def ffn_grads_send(self, after):
        gs, theirs = _halves_wait(FFN_W, self.swap, after, name="halves_ffn_wait")
        parts = _chip_partial(FFN_W, gs, theirs, name="chip_partial_ffn", out_dtype=BF16)
        self.sent = _send_start(FFN_W, parts, name="send_ffn_start")
        return self.sent["token"]


def kernel(x, norm_mix_w, w_in, b_attn, lb_logits, hg_norm_w, sinks, w_out, norm_ffn_w, w_gate, w_up, conv_w, conv_b, w_down, final_norm_w, loss_target, m_norm_mix_w, m_w_in, m_b_attn, m_lb_logits, m_hg_norm_w, m_sinks, m_w_out, m_norm_ffn_w, m_w_gate, m_w_up, m_conv_w, m_conv_b, m_w_down, m_final_norm_w, v_norm_mix_w, v_w_in, v_b_attn, v_lb_logits, v_hg_norm_w, v_sinks, v_w_out, v_norm_ffn_w, v_w_gate, v_w_up, v_conv_w, v_conv_b, v_w_down, v_final_norm_w):
    D = D_MODEL
    q = 2 * lax.axis_index("x") + lax.axis_index("y")
    ccols = D_FF // N_CHIPS

    pack = jnp.concatenate([w_in[0].T, w_gate[0].T, w_up[0].T, w_down[0], w_out[0]], axis=0).astype(BF16)
    cw8 = jnp.concatenate([conv_w[0], jnp.zeros((SUBLANES - 3, ccols), F32)], axis=0)
    ex = _MeshExchange(pack, cw8)
    p = dict(norm_mix_w=norm_mix_w, b_attn=b_attn, lb=_lb_fwd(lb_logits), hg_norm_w=hg_norm_w, sinks=sinks,
             norm_ffn_w=norm_ffn_w, conv_b=conv_b, final_norm_w=final_norm_w.reshape(1, D))
    loss_cols, dx, g = _local_step(x[0], loss_target[0], p, ex)
    conv_w8 = ex.conv_w8

    small = _pack_small(dict(loss=loss_cols, norm_mix_w=g["norm_mix_w"], b_attn=g["b_attn"], lb_logits=g["lb"],
                             hg_norm_w=g["hg_norm_w"], sinks=g["sinks8"], norm_ffn_w=g["norm_ffn_w"],
                             conv_w=g["conv_w8"][:3], conv_b=g["conv_b"], final_norm_w=g["final_norm_w"]))
    parts_ffn, got_ffn = _send_wait(FFN_W, ex.sent, [dx], name="send_ffn_wait")
    late = (0, 4)
    gs = [g["g_in_t"], g["g_out"]]
    *theirs, sall = _exchange_halves(late, gs, small, name="exchange_halves_late")
    parts_late = _chip_partial(late, gs, theirs, name="chip_partial_late", out_dtype=BF16)
    sent_late = _send_start(late, parts_late, name="send_late_start")
    big = {}

    def finish(ws, parts, got, specs, tag, after):
        shards = _exchange_reduced(ws, _chip_reduce(ws, parts, got, name="chip_reduce_" + tag, after=after),
                                   name="exchange_reduced_" + tag)
        deltas = []
        for gw, (n, w, m, v, tr) in zip(shards, specs):
            view = (lambda a: a[0].T) if tr else (lambda a: a[0])
            back = (lambda a: a.T[None]) if tr else (lambda a: a[None])
            res = _adamw(view(w), gw, view(m), view(v), name="adamw_" + n)
            big[n] = tuple(back(r) for r in res)
            deltas.append(res[1])
        return deltas

    done_ffn = finish(FFN_W, parts_ffn, got_ffn, (("w_gate", w_gate, m_w_gate, v_w_gate, True),
                                                  ("w_up", w_up, m_w_up, v_w_up, True),
                                                  ("w_down", w_down, m_w_down, v_w_down, False)), "ffn", sent_late["token"])

    def place(a):
        return lax.dynamic_update_slice(jnp.zeros((3, D_FF), F32), a[0], (0, q * ccols))

    def small_pack(ws, cw):
        nm, ba, lbl, hg, sk, nf, cb, fn = ws
        return _pack_small(dict(norm_mix_w=nm, b_attn=ba, lb_logits=lbl, hg_norm_w=hg,
                                sinks=jnp.broadcast_to(sk.reshape(ATT_HEADS, 1), (ATT_HEADS, LANES)), norm_ffn_w=nf,
                                conv_w=cw, conv_b=cb, final_norm_w=fn))

    wp = small_pack((norm_mix_w, b_attn, lb_logits, hg_norm_w, sinks, norm_ffn_w, conv_b, final_norm_w), conv_w8[:3])
    mp = small_pack((m_norm_mix_w, m_b_attn, m_lb_logits, m_hg_norm_w, m_sinks, m_norm_ffn_w, m_conv_b, m_final_norm_w),
                    place(m_conv_w))
    vp = small_pack((v_norm_mix_w, v_b_attn, v_lb_logits, v_hg_norm_w, v_sinks, v_norm_ffn_w, v_conv_b, v_final_norm_w),
                    place(v_conv_w))
    outs = _small_update(sall, wp, mp, vp, after=sent_late["token"])
    loss = outs[4][0, 0]
    parts_late, got_late = _send_wait(late, sent_late, [*done_ffn, outs[4]], name="send_late_wait")
    finish(late, parts_late, got_late, (("w_in", w_in, m_w_in, v_w_in, True), ("w_out", w_out, m_w_out, v_w_out, False)),
           "late", None)

    def small_out(pk, n, ref):
        if n == "sinks":
            return pk[SMALL_OFF[n]:SMALL_OFF[n] + ATT_HEADS, 0].reshape(ref.shape)
        if n == "conv_w":
            full = _unpack_small(pk, n, (3, D_FF))
            return lax.dynamic_slice(full, (0, q * ccols), (3, ccols))[None]
        return _unpack_small(pk, n, ref.shape)

    refs = dict(norm_mix_w=norm_mix_w, b_attn=b_attn, lb_logits=lb_logits, hg_norm_w=hg_norm_w, sinks=sinks,
                norm_ffn_w=norm_ffn_w, conv_w=conv_w, conv_b=conv_b, final_norm_w=final_norm_w)
    order = ("norm_mix_w", "w_in", "b_attn", "lb_logits", "hg_norm_w", "sinks", "w_out", "norm_ffn_w", "w_gate", "w_up",
             "conv_w", "conv_b", "w_down", "final_norm_w")
    res = [loss, dx[None]]
    for k in range(4):
        for n in order:
            res.append(big[n][k] if n in big else small_out(outs[k], n, refs[n]))
    return tuple(res)
```

```python
import functools
import math

import jax
import jax.numpy as jnp
from jax import lax
from jax.experimental import pallas as pl
from jax.experimental.pallas import tpu as pltpu

F32 = jnp.float32
BF16 = jnp.bfloat16

D_MODEL = 1024
HG_HEADS = 4
HG_DK = 128
HG_W = HG_HEADS * HG_DK
HG_CHUNK = 64
HG_SUB = 8
HG_FWD_CHUNKS_PER_STEP = 4
HG_CHUNKS_PER_STEP = 2
ATT_HEADS = 8
ATT_KV = 2
ATT_GROUP = ATT_HEADS // ATT_KV
ATT_HD = 64
ATT_BLOCK = 128
ATT_Q_W = ATT_HEADS * ATT_HD
ATT_KV_W = ATT_KV * ATT_HD
ATT_COLS = ATT_Q_W + 2 * ATT_KV_W
IN_COLS = 4 * HG_W + ATT_COLS
D_FF = 2816
EPS = 1e-6
ADAM_LR, ADAM_B1, ADAM_B2, ADAM_EPS, ADAM_WD, ADAM_STEP = 0.001, 0.9, 0.999, 1e-08, 0.01, 10
NEG = -1e30

V7X_VMEM_BYTES = 64 * 1024 * 1024
VMEM_LIMIT = 48 * 1024 * 1024
SUBLANES = 8

N_CHIPS = 4


def _cp(sem=None, **kw):
    return pltpu.CompilerParams(dimension_semantics=sem, vmem_limit_bytes=VMEM_LIMIT, **kw)


def _sds(shape, dtype):
    return jax.ShapeDtypeStruct(shape, dtype)


TOKEN = jax.ShapeDtypeStruct((8, 128), jnp.float32)


def _wspec(w):
    arr, rows, blk = w
    return pl.BlockSpec((rows, arr.shape[1]), lambda i: (blk, 0))


def _mm_nt(a, w, *, splits, out_dtype, name, after=None, tm=512):
    M, K = a.shape
    N = w[1]
    tm = min(tm, M)
    assert sum(splits) == N and M % tm == 0
    offs = [sum(splits[:i]) for i in range(len(splits))]
    n_in = 2 if after is None else 3

    def body(*refs):
        a_ref, w_ref = refs[0], refs[1]
        acc = lax.dot_general(a_ref[...], w_ref[...], (((1,), (1,)), ((), ())), preferred_element_type=F32)
        for o_ref, c0, n in zip(refs[n_in:], offs, splits):
            o_ref[...] = acc[:, c0:c0 + n].astype(out_dtype)

    in_specs = [pl.BlockSpec((tm, K), lambda i: (i, 0)), _wspec(w)]
    args = [a, w[0]]
    if after is not None:
        in_specs.append(pl.BlockSpec(memory_space=pl.ANY))
        args.append(after)
    outs = pl.pallas_call(
        body, name=name, grid=(M // tm,), in_specs=in_specs,
        out_specs=[pl.BlockSpec((tm, n), lambda i: (i, 0)) for n in splits],
        out_shape=[_sds((M, n), out_dtype) for n in splits],
        compiler_params=_cp(("parallel",)),
    )(*args)
    return outs


def _mm_nn(pieces, ws, *, name, out_dtype=F32, residual=None, epilogue=None, prologue=None, after=None,
           w_transposed=False, tm=512):
    pro_fn, pro_rows, pro_bc, pro_out = prologue or (None, [], [], None)
    if prologue is not None:
        assert pieces is None and len(ws) == 1
        pieces = [[pro_out]]
    M = pieces[0][0].shape[0]
    K = ws[0][1] if w_transposed else ws[0][0].shape[1]
    tm = min(tm, M)
    flat = [] if prologue is not None else [p for grp in pieces for p in grp]
    n_p = len(flat)
    n_w = len(ws)
    n_pr, n_pb = len(pro_rows), len(pro_bc)
    fn, row_ins, bc_ins, row_outs, acc_outs = epilogue or (None, [], [], [_sds((M, K), out_dtype)], [])
    if residual is not None:
        assert epilogue is None
        row_ins = [residual]
    n_r, n_b, n_o = len(row_ins), len(bc_ins), len(row_outs)
    lead = [] if after is None else [after]

    def body(*refs):
        refs = refs[len(lead):]
        p_refs = refs[:n_p]
        w_refs = refs[n_p:n_p + n_w]
        extra = [r[...] for r in refs[n_p + n_w:n_p + n_w + n_r + n_b]]
        base = n_p + n_w + n_r + n_b
        pro = [r[...] for r in refs[base:base + n_pr + n_pb]]
        base += n_pr + n_pb
        o_refs = refs[base:base + n_o]
        a_refs = refs[base + n_o:base + n_o + len(acc_outs)]
        if pro_fn is not None:
            lhs = pro_fn(*pro).astype(pro_out.dtype)
            refs[-1][...] = lhs
            tiles = [lhs]
        else:
            tiles = [r[...] for r in p_refs]
        acc = None
        k = 0
        for gi, grp in enumerate(pieces):
            c0 = 0
            for p in grp:
                n = p.shape[1]
                if w_transposed:
                    t = lax.dot_general(tiles[k], w_refs[gi][...], (((1,), (1,)), ((), ())), preferred_element_type=F32)
                else:
                    t = jnp.dot(tiles[k], w_refs[gi][c0:c0 + n, :], preferred_element_type=F32)
                acc = t if acc is None else acc + t
                c0 += n
                k += 1
        if fn is None:
            res = (acc + extra[0] if residual is not None else acc,)
        else:
            res = fn(acc, *extra)
        for o_ref, val in zip(o_refs, res[:n_o]):
            o_ref[...] = val.astype(o_ref.dtype)
        if acc_outs:
            @pl.when(pl.program_id(0) == 0)
            def _():
                for a_ref in a_refs:
                    a_ref[...] = jnp.zeros_like(a_ref)
            for a_ref, val in zip(a_refs, res[n_o:]):
                a_ref[...] += val

    in_specs = [pl.BlockSpec((tm, p.shape[1]), lambda i: (i, 0)) for p in flat]
    in_specs += [_wspec(w) for w in ws]
    in_specs += [pl.BlockSpec((tm, r.shape[1]), lambda i: (i, 0)) for r in row_ins]
    in_specs += [pl.BlockSpec(b.shape, lambda i: (0, 0)) for b in bc_ins]
    in_specs += [pl.BlockSpec((tm, r.shape[1]), lambda i: (i, 0)) for r in pro_rows]
    in_specs += [pl.BlockSpec(b.shape, lambda i: (0, 0)) for b in pro_bc]
    out_specs = [pl.BlockSpec((tm, s.shape[1]), lambda i: (i, 0)) for s in row_outs]
    out_specs += [pl.BlockSpec(s.shape, lambda i: (0, 0)) for s in acc_outs]
    pro_outs = [] if prologue is None else [pro_out]
    out_specs += [pl.BlockSpec((tm, s.shape[1]), lambda i: (i, 0)) for s in pro_outs]
    outs = pl.pallas_call(
        body, name=name, grid=(M // tm,), in_specs=[pl.BlockSpec(memory_space=pl.ANY)] * len(lead) + in_specs,
        out_specs=out_specs, out_shape=list(row_outs) + list(acc_outs) + pro_outs,
        compiler_params=_cp(("arbitrary",) if acc_outs else ("parallel",)),
    )(*lead, *flat, *[w[0] for w in ws], *row_ins, *bc_ins, *pro_rows, *pro_bc)
    return outs if (epilogue is not None or prologue is not None) else outs[0]


def _mm_tn(pieces, x, *, name, out_dtype=BF16, tt=1024):
    M, K = x.shape
    tt = min(tt, M)
    ns = [p.shape[1] for p in pieces]
    offs = [sum(ns[:i]) for i in range(len(ns))]
    N = sum(ns)
    n_p = len(pieces)
    last = M // tt - 1

    def body(*refs):
        p_refs = refs[:n_p]
        x_ref = refs[n_p]
        o_ref, acc_ref = refs[n_p + 1], refs[n_p + 2]

        @pl.when(pl.program_id(0) == 0)
        def _():
            acc_ref[...] = jnp.zeros_like(acc_ref)

        xv = x_ref[...]
        for p_ref, c0, n in zip(p_refs, offs, ns):
            acc_ref[c0:c0 + n, :] += lax.dot_general(p_ref[...], xv, (((0,), (0,)), ((), ())),
                                                      preferred_element_type=F32)

        @pl.when(pl.program_id(0) == last)
        def _():
            o_ref[...] = acc_ref[...].astype(o_ref.dtype)

    in_specs = [pl.BlockSpec((tt, n), lambda i: (i, 0)) for n in ns]
    in_specs.append(pl.BlockSpec((tt, K), lambda i: (i, 0)))
    return pl.pallas_call(
        body, name=name, grid=(M // tt,), in_specs=in_specs,
        out_specs=pl.BlockSpec((N, K), lambda i: (0, 0)),
        out_shape=_sds((N, K), out_dtype),
        scratch_shapes=[pltpu.VMEM((N, K), F32)],
        compiler_params=_cp(("arbitrary",)),
    )(*pieces, x)


def _rms_fwd(xf, w):
    inv = lax.rsqrt(jnp.mean(xf * xf, axis=-1, keepdims=True) + EPS)
    return xf * inv * w


def _rms_bwd(xf, w, dy):
    inv = lax.rsqrt(jnp.mean(xf * xf, axis=-1, keepdims=True) + EPS)
    xhat = xf * inv
    dxhat = dy * w
    dx = inv * (dxhat - xhat * jnp.mean(dxhat * xhat, axis=-1, keepdims=True))
    dw = jnp.sum(dy * xhat, axis=0, keepdims=True)
    return dx, dw


def _sigmoid(x):
    return 1.0 / (1.0 + jnp.exp(-x))


def _rowwise(fn, row_ins, bc_ins, row_outs, acc_outs, *, name, tm=256, after=None):
    M = row_outs[0].shape[0] if row_outs else row_ins[0][0].shape[0]
    assert M % tm == 0 and tm % SUBLANES == 0, (name, M, tm)
    n_r, n_b, n_o, n_a = len(row_ins), len(bc_ins), len(row_outs), len(acc_outs)
    n_after = 0 if after is None else 1

    def body(*refs):
        refs = refs[n_after:]
        ins = [r[...] for r in refs[:n_r + n_b]]
        o_refs = refs[n_r + n_b:n_r + n_b + n_o]
        a_refs = refs[n_r + n_b + n_o:]
        res = fn(*ins)
        for o_ref, val in zip(o_refs, res[:n_o]):
            o_ref[...] = val.astype(o_ref.dtype)
        if n_a:
            @pl.when(pl.program_id(0) == 0)
            def _():
                for a_ref in a_refs:
                    a_ref[...] = jnp.zeros_like(a_ref)
            for a_ref, val in zip(a_refs, res[n_o:]):
                a_ref[...] += val

    in_specs = [pl.BlockSpec((tm, cw), functools.partial(lambda i, cb, r0: (i + r0, cb), cb=cb, r0=r0))
                for (_, cw, cb, r0) in row_ins]
    in_specs += [pl.BlockSpec(b.shape, lambda i: (0, 0)) for b in bc_ins]
    out_specs = [pl.BlockSpec((tm, s.shape[1]), lambda i: (i, 0)) for s in row_outs]
    out_specs += [pl.BlockSpec(s.shape, lambda i: (0, 0)) for s in acc_outs]
    if n_after:
        in_specs = [pl.BlockSpec(memory_space=pl.ANY)] + in_specs
    return pl.pallas_call(
        body, name=name, grid=(M // tm,), in_specs=in_specs, out_specs=out_specs,
        out_shape=list(row_outs) + list(acc_outs),
        compiler_params=_cp(("arbitrary",) if n_a else ("parallel",)),
    )(*([after] if n_after else []), *[r[0] for r in row_ins], *bc_ins)


def _full(a, first_row_block=0):
    return (a, a.shape[1], 0, first_row_block)


def _conv_rows(ext, w_ref_val, lo):
    s1 = pltpu.roll(ext, 1, 0)
    s2 = pltpu.roll(ext, 2, 0)
    y = w_ref_val[0:1, :] * s2 + w_ref_val[1:2, :] * s1 + w_ref_val[2:3, :] * ext
    return y[SUBLANES:, :]


def _ffn_in(v, w_gate, w_up, conv_w8, conv_b, *, name, tm=256):
    T, K = v.shape
    N = w_gate[1]
    tm = min(tm, T)

    def body(v_ref, wg_ref, wu_ref, cw_ref, cb_ref, gp_ref, up_ref, gate_ref, act_ref, carry_sc):
        @pl.when(pl.program_id(0) == 0)
        def _():
            carry_sc[...] = jnp.zeros_like(carry_sc)

        vv = v_ref[...]
        dn = (((1,), (1,)), ((), ()))
        gp = lax.dot_general(vv, wg_ref[...], dn, preferred_element_type=F32)
        up = lax.dot_general(vv, wu_ref[...], dn, preferred_element_type=F32)
        gp_ref[...] = gp
        up_ref[...] = up
        gate = _conv_rows(jnp.concatenate([carry_sc[...], gp], axis=0), cw_ref[...], 0) + cb_ref[...]
        gate_ref[...] = gate
        act_ref[...] = (gate * _sigmoid(gate) * up).astype(act_ref.dtype)
        carry_sc[...] = gp[tm - SUBLANES:, :]

    tile = pl.BlockSpec((tm, N), lambda i: (i, 0))
    return pl.pallas_call(
        body, name=name, grid=(T // tm,),
        in_specs=[pl.BlockSpec((tm, K), lambda i: (i, 0)), _wspec(w_gate), _wspec(w_up),
                  pl.BlockSpec((SUBLANES, N), lambda i: (0, 0)), pl.BlockSpec((1, N), lambda i: (0, 0))],
        out_specs=[tile] * 4,
        out_shape=[_sds((T, N), F32)] * 3 + [_sds((T, N), BF16)],
        scratch_shapes=[pltpu.VMEM((SUBLANES, N), F32)],
        compiler_params=_cp(("arbitrary",)),
    )(v, w_gate[0], w_up[0], conv_w8, conv_b)


def _ffn_back(dh2, w_down, gp, up, gate, conv_w8, *, name, tr=256, tc=1408):
    T, C = gp.shape
    K = dh2.shape[1]
    warr, _, wblk = w_down
    tr = min(tr, T)
    nr = T // tr
    ncb = C // tc

    def body(dh_ref, wd_ref, gp_ref, up_ref, gate_ref, w_ref, dgp_ref, dup_ref, dw_ref, db_ref, carry_sc):
        @pl.when(pl.program_id(1) == 0)
        def _():
            carry_sc[...] = jnp.zeros_like(carry_sc)
            dw_ref[...] = jnp.zeros_like(dw_ref)
            db_ref[...] = jnp.zeros_like(db_ref)

        w = w_ref[...]
        dact = lax.dot_general(dh_ref[...], wd_ref[...], (((1,), (1,)), ((), ())), preferred_element_type=F32)
        gpc = gp_ref[...]
        gate = gate_ref[...]
        sg = _sigmoid(gate)
        silu = gate * sg
        dup_ref[...] = (dact * silu).astype(dup_ref.dtype)
        dgate = dact * up_ref[...] * (sg + silu * (1.0 - sg))
        ext = jnp.concatenate([dgate, carry_sc[...]], axis=0)
        n = tr + SUBLANES
        g1 = pltpu.roll(ext, n - 1, 0)[:tr]
        g2 = pltpu.roll(ext, n - 2, 0)[:tr]
        dgp_ref[...] = (w[2:3, :] * dgate + w[1:2, :] * g1 + w[0:1, :] * g2).astype(dgp_ref.dtype)
        dw0 = jnp.sum(gpc * g2, axis=0, keepdims=True)
        dw1 = jnp.sum(gpc * g1, axis=0, keepdims=True)
        dw2 = jnp.sum(gpc * dgate, axis=0, keepdims=True)
        z = jnp.zeros((SUBLANES - 3, gpc.shape[1]), F32)
        dw_ref[...] += jnp.concatenate([dw0, dw1, dw2, z], axis=0)
        db_ref[...] += jnp.sum(dgate, axis=0, keepdims=True)
        carry_sc[...] = dgate[:SUBLANES]

    rev = lambda i: nr - 1 - i
    cur = pl.BlockSpec((tr, tc), lambda j, i: (rev(i), j))
    return pl.pallas_call(
        body, name=name, grid=(ncb, nr),
        in_specs=[pl.BlockSpec((tr, K), lambda j, i: (rev(i), 0)),
                  pl.BlockSpec((tc, K), lambda j, i: (wblk * ncb + j, 0)),
                  cur, cur, cur,
                  pl.BlockSpec((SUBLANES, tc), lambda j, i: (0, j))],
        out_specs=[cur, cur,
                   pl.BlockSpec((SUBLANES, tc), lambda j, i: (0, j)),
                   pl.BlockSpec((1, tc), lambda j, i: (0, j))],
        out_shape=[_sds((T, C), BF16), _sds((T, C), BF16), _sds((SUBLANES, C), F32), _sds((1, C), F32)],
        scratch_shapes=[pltpu.VMEM((SUBLANES, tc), F32)],
        compiler_params=_cp(("parallel", "arbitrary")),
    )(dh2, warr, gp, up, gate, conv_w8)


def _cumsum_rows(x):
    n = x.shape[0]
    row = lax.broadcasted_iota(jnp.int32, x.shape, 0)
    s = 1
    while s < n:
        x = x + jnp.where(row >= s, pltpu.roll(x, s, 0), 0.0)
        s *= 2
    return x


def _rcumsum_rows(x):
    n = x.shape[0]
    row = lax.broadcasted_iota(jnp.int32, x.shape, 0)
    s = 1
    while s < n:
        x = x + jnp.where(row < n - s, pltpu.roll(x, n - s, 0), 0.0)
        s *= 2
    return x


def _dot_nt(a, b):
    return lax.dot_general(a.astype(BF16), b.astype(BF16), (((1,), (1,)), ((), ())), preferred_element_type=F32)


def _dot_tn(a, b):
    return lax.dot_general(a.astype(BF16), b.astype(BF16), (((0,), (0,)), ((), ())), preferred_element_type=F32)


def _dot_nn(a, b):
    return jnp.dot(a.astype(BF16), b.astype(BF16), preferred_element_type=F32)


def _dot3(a, b, contract):
    def split(x):
        hi = x.astype(BF16)
        return hi, (x - hi.astype(F32)).astype(BF16)

    a_hi, a_lo = split(a)
    b_hi, b_lo = split(b)
    dot = lambda x, y: lax.dot_general(x, y, (contract, ((), ())), preferred_element_type=F32)
    return dot(a_hi, b_hi) + (dot(a_hi, b_lo) + dot(a_lo, b_hi))


NT, TN, NN = ((1,), (1,)), ((0,), (0,)), ((1,), (0,))


def _hg_gates(hq, hf, lbv):
    sig = _sigmoid(hf)
    f = lbv + (1.0 - lbv) * sig
    return sig, f, jnp.log(f), 1.0 - f, hq * (HG_DK ** -0.5)


def _hg_sel_rows(ref, sp):
    return jnp.concatenate(
        [jnp.broadcast_to(ref[pl.ds(HG_SUB * i + sp, 1), :], (HG_SUB, HG_DK)) for i in range(HG_CHUNK // HG_SUB)], axis=0)


def _hg_masks():
    C = HG_CHUNK
    row = lax.broadcasted_iota(jnp.int32, (C, C), 0)
    col = lax.broadcasted_iota(jnp.int32, (C, C), 1)
    d = col - (row // HG_SUB) * HG_SUB
    tmod = row % HG_SUB
    diag_valid = jnp.logical_and(d >= 0, d <= tmod)
    return row, col, d, diag_valid


def _hg_scores(q, k, b, b_sc, k_sc):
    C, S = HG_CHUNK, HG_SUB
    row, col, d, diag_valid = _hg_masks()
    blocks = [jnp.zeros((S, C), F32)]
    for i in range(1, C // S):
        r = b_sc[pl.ds(S * i - 1, 1), :]
        qi = q[S * i:S * (i + 1)] * jnp.exp(b[S * i:S * (i + 1)] - r)
        kk = k * jnp.exp(jnp.minimum(r - b, 0.0))
        blocks.append(_dot_nt(qi, kk))
    a_off = jnp.where(col < (row // S) * S, jnp.concatenate(blocks, axis=0), 0.0)
    a_d = jnp.zeros((C, C), F32)
    for sp in range(S):
        bs = _hg_sel_rows(b_sc, sp)
        ks = _hg_sel_rows(k_sc, sp)
        e = jnp.exp(jnp.minimum(b - bs, 0.0))
        colv = jnp.sum(q * ks * e, axis=-1, keepdims=True)
        a_d = jnp.where(d == sp, colv, a_d)
    return a_off + jnp.where(diag_valid, a_d, 0.0)


def _hg_prep(hq_v, hf_v, lbv, b_sc, k_sc):
    sig, f, g, k, q = _hg_gates(hq_v, hf_v, lbv)
    b = _cumsum_rows(g)
    b_sc[...] = b
    k_sc[...] = k
    return sig, f, k, q, b, b_sc[pl.ds(HG_CHUNK - 1, 1), :]


def _hgrn_fwd(hq, hf, hi, lb, *, name):
    T = hq.shape[0]
    C, H, K = HG_CHUNK, HG_HEADS, HG_DK
    NC = T // C

    def body(hq_ref, hf_ref, hi_ref, lb_ref, o_ref, st_ref, s_sc, b_sc, k_sc):
        @pl.when(pl.program_id(0) == 0)
        def _():
            s_sc[...] = jnp.zeros_like(s_sc)

        st_all = s_sc[...]
        for j in range(P):
            rows = slice(C * j, C * (j + 1))
            st_ref[j] = st_all
            outs, news = [], []
            for h in range(H):
                sl = slice(K * h, K * (h + 1))
                _, _, k, q, b, bc = _hg_prep(hq_ref[rows, sl], hf_ref[rows, sl], lb_ref[:, sl], b_sc.at[j, h], k_sc.at[j, h])
                v = hi_ref[rows, sl]
                st0 = st_all[:, sl]
                a = _hg_scores(q, k, b, b_sc.at[j, h], k_sc.at[j, h])
                outs.append(_dot_nn(a, v) + _dot_nt(q * jnp.exp(b), st0))
                news.append(st0 * jnp.exp(bc) + _dot_tn(v, k * jnp.exp(bc - b)))
            o_ref[rows, :] = jnp.concatenate(outs, axis=1)
            st_all = jnp.concatenate(news, axis=1)
        s_sc[...] = st_all

    P = HG_FWD_CHUNKS_PER_STEP
    blk = pl.BlockSpec((P * C, H * K), lambda c: (c, 0))
    return pl.pallas_call(
        body, name=name, grid=(NC // P,),
        in_specs=[blk, blk, blk, pl.BlockSpec((1, H * K), lambda c: (0, 0))],
        out_specs=[blk, pl.BlockSpec((P, K, H * K), lambda c: (c, 0, 0))],
        out_shape=[_sds((T, H * K), F32), _sds((NC, K, H * K), F32)],
        scratch_shapes=[pltpu.VMEM((K, H * K), F32), pltpu.VMEM((P, H, C, K), F32), pltpu.VMEM((P, H, C, K), F32)],
        compiler_params=_cp(("arbitrary",)),
    )(hq, hf, hi, lb)


def _hgrn_bwd(hq, hf, hi, lb, states, do, *, name):
    T = hq.shape[0]
    C, H, K, S = HG_CHUNK, HG_HEADS, HG_DK, HG_SUB
    NC = T // C

    def intra_slow(q, k, b, da, b_sc, k_sc):
        row, col, d, diag_valid = _hg_masks()
        a_blocks = [jnp.zeros((S, C), F32)]
        dq_blocks = [jnp.zeros((S, K), F32)]
        dk = jnp.zeros((C, K), F32)
        for i in range(1, C // S):
            r = b_sc[pl.ds(S * i - 1, 1), :]
            eq = jnp.exp(b[S * i:S * (i + 1)] - r)
            ek = jnp.exp(jnp.minimum(r - b, 0.0))
            qi = q[S * i:S * (i + 1)] * eq
            kk = k * ek
            a_blocks.append(_dot_nt(qi, kk))
            dai = jnp.where(col[S * i:S * (i + 1)] < S * i, da[S * i:S * (i + 1)], 0.0)
            dq_blocks.append(_dot_nn(dai, kk) * eq)
            dk = dk + _dot_tn(dai, qi) * ek
        dq = jnp.concatenate(dq_blocks, axis=0)
        a_off = jnp.where(col < (row // S) * S, jnp.concatenate(a_blocks, axis=0), 0.0)
        same_blk = (row // S == col // S).astype(BF16)
        tmod = (lax.broadcasted_iota(jnp.int32, (C, K), 0)) % S
        a_d = jnp.zeros((C, C), F32)
        dk_d = jnp.zeros((C, K), F32)
        for sp in range(S):
            bs = _hg_sel_rows(b_sc, sp)
            ks = _hg_sel_rows(k_sc, sp)
            e = jnp.exp(jnp.minimum(b - bs, 0.0))
            eks = e * ks
            a_d = jnp.where(d == sp, jnp.sum(q * eks, axis=-1, keepdims=True), a_d)
            dacol = jnp.sum(jnp.where(d == sp, da, 0.0), axis=-1, keepdims=True)
            dq = dq + dacol * eks
            wq = dacol * e * q
            wq_hi = wq.astype(BF16)
            wq_lo = (wq - wq_hi.astype(F32)).astype(BF16)
            blk_sum = (jnp.dot(same_blk, wq_hi, preferred_element_type=F32)
                       + jnp.dot(same_blk, wq_lo, preferred_element_type=F32))
            dk_d = jnp.where(tmod == sp, blk_sum, dk_d)
        return a_off + jnp.where(diag_valid, a_d, 0.0), dq, dk + dk_d

    def one_head(pre, v, lbv, st0, dst1, dout, b_sc, k_sc):
        sig, f, k, q, b, bc = pre
        ebc = jnp.exp(bc)
        eb = jnp.exp(b)
        ekb = jnp.exp(bc - b)
        qt = q * eb
        kb = k * ekb
        row = lax.broadcasted_iota(jnp.int32, (C, C), 0)
        col = lax.broadcasted_iota(jnp.int32, (C, C), 1)
        da = jnp.where(col <= row, _dot_nt(dout, v), 0.0)
        dkb = _dot_nn(v, dst1)
        new_ds = _dot_tn(dout, qt) + dst1 * ebc
        a, dq_i, dk_i = intra_slow(q, k, b, da, b_sc, k_sc)
        dq = _dot_nn(dout, st0) * eb + dq_i
        dk = dkb * ekb + dk_i
        dv = _dot_tn(a, dout) + _dot_nt(kb, dst1)
        extra = jnp.sum(dkb * kb, axis=0, keepdims=True) + ebc * jnp.sum(st0 * dst1, axis=0, keepdims=True)
        rowk = lax.broadcasted_iota(jnp.int32, (C, K), 0)
        db = q * dq - k * dk + jnp.where(rowk == C - 1, extra, 0.0)
        dg = _rcumsum_rows(db)
        df = dg / f - dk
        return (dq * (K ** -0.5), df * (1.0 - lbv) * sig * (1.0 - sig), dv,
                jnp.sum(df * (1.0 - sig), axis=0, keepdims=True), new_ds)

    def body(hq_ref, hf_ref, hi_ref, lb_ref, st_ref, do_ref, dq_ref, dhf_ref, dv_ref, dlb_ref, ds_sc, b_sc, k_sc):
        @pl.when(pl.program_id(0) == 0)
        def _():
            ds_sc[...] = jnp.zeros_like(ds_sc)
            dlb_ref[...] = jnp.zeros_like(dlb_ref)

        ds_all = ds_sc[...]
        dlb = jnp.zeros((1, H * K), F32)
        for j in reversed(range(P)):
            rows = slice(C * j, C * (j + 1))
            st_all = st_ref[j]
            res = []
            for h in range(H):
                sl = slice(K * h, K * (h + 1))
                pre = _hg_prep(hq_ref[rows, sl], hf_ref[rows, sl], lb_ref[:, sl], b_sc.at[j, h], k_sc.at[j, h])
                res.append(one_head(pre, hi_ref[rows, sl], lb_ref[:, sl], st_all[:, sl], ds_all[:, sl], do_ref[rows, sl],
                                    b_sc.at[j, h], k_sc.at[j, h]))
            cat = lambda i: jnp.concatenate([r[i] for r in res], axis=1)
            dq_ref[rows, :] = cat(0).astype(dq_ref.dtype)
            dhf_ref[rows, :] = cat(1).astype(dhf_ref.dtype)
            dv_ref[rows, :] = cat(2).astype(dv_ref.dtype)
            dlb = dlb + cat(3)
            ds_all = cat(4)
        dlb_ref[...] += dlb
        ds_sc[...] = ds_all

    P = HG_CHUNKS_PER_STEP
    NS = NC // P
    blk = pl.BlockSpec((P * C, H * K), lambda c: (NS - 1 - c, 0))
    par = pl.BlockSpec((1, H * K), lambda c: (0, 0))
    return pl.pallas_call(
        body, name=name, grid=(NS,),
        in_specs=[blk, blk, blk, par, pl.BlockSpec((P, K, H * K), lambda c: (NS - 1 - c, 0, 0)), blk],
        out_specs=[blk, blk, blk, par],
        out_shape=[_sds((T, H * K), BF16)] * 3 + [_sds((1, H * K), F32)],
        scratch_shapes=[pltpu.VMEM((K, H * K), F32), pltpu.VMEM((P, H, C, K), F32), pltpu.VMEM((P, H, C, K), F32)],
        compiler_params=_cp(("arbitrary",)),
    )(hq, hf, hi, lb, states, do)


ATT_STACK = ATT_GROUP


def _att_valid(n):
    R, B = ATT_STACK * ATT_BLOCK, ATT_BLOCK
    j = lax.broadcasted_iota(jnp.int32, (2 * B, R), 0)
    t = lax.broadcasted_iota(jnp.int32, (2 * B, R), 1) % B
    dist = t + B - j
    first_key = jnp.where(n > 0, 0, B)
    return jnp.logical_and(jnp.logical_and(dist >= 0, dist < B), j >= first_key)


def _att_load(cur_ref, prev_ref, ba_ref, h0):
    hd = ATT_HD
    kv = h0 // ATT_GROUP
    def cols(ref, c0):
        return ref[:, c0:c0 + hd] + ba_ref[:, c0:c0 + hd]
    qs = jnp.concatenate([cols(cur_ref, hd * (h0 + g)) for g in range(ATT_STACK)], axis=0)
    kc = jnp.concatenate([cols(prev_ref, ATT_Q_W + hd * kv), cols(cur_ref, ATT_Q_W + hd * kv)], axis=0)
    vc = jnp.concatenate([cols(prev_ref, ATT_Q_W + ATT_KV_W + hd * kv), cols(cur_ref, ATT_Q_W + ATT_KV_W + hd * kv)], axis=0)
    return qs, kc, vc


def _att_probs(qs, kc, valid, sink_ref, h0):
    scale = 1.0 / math.sqrt(ATT_HD)
    s = jnp.where(valid, _dot_nt(kc, qs) * scale, NEG)
    sink = jnp.concatenate([jnp.full((1, ATT_BLOCK), sink_ref[0, h0 + g], F32) for g in range(ATT_STACK)], axis=1)
    m = jnp.maximum(jnp.max(s, axis=0, keepdims=True), sink)
    p = jnp.exp(s - m)
    ps = jnp.exp(sink - m)
    inv = 1.0 / (jnp.sum(p, axis=0, keepdims=True) + ps)
    return p * inv, ps * inv


def _attn_fwd(att, b_attn, sinks, *, name, after=None):
    T = att.shape[0]
    B = ATT_BLOCK
    NB = T // B
    lead = [] if after is None else [after]

    def body(*refs):
        sink_ref, cur_ref, prev_ref, ba_ref, o_ref = refs[len(lead):]
        valid = _att_valid(pl.program_id(0))
        outs = []
        for h0 in range(0, ATT_HEADS, ATT_STACK):
            qs, kc, vc = _att_load(cur_ref, prev_ref, ba_ref, h0)
            prob, _ = _att_probs(qs, kc, valid, sink_ref, h0)
            o = _dot_tn(prob, vc)
            outs += [o[B * g:B * (g + 1)] for g in range(ATT_STACK)]
        o_ref[...] = jnp.concatenate(outs, axis=1)

    return pl.pallas_call(
        body, name=name, grid=(NB,),
        in_specs=[pl.BlockSpec(memory_space=pl.ANY)] * len(lead) + [
            pl.BlockSpec(memory_space=pltpu.SMEM),
            pl.BlockSpec((B, ATT_COLS), lambda n: (n, 0)),
            pl.BlockSpec((B, ATT_COLS), lambda n: (jnp.maximum(n - 1, 0), 0)),
            pl.BlockSpec((1, ATT_COLS), lambda n: (0, 0))],
        out_specs=pl.BlockSpec((B, ATT_Q_W), lambda n: (n, 0)),
        out_shape=_sds((T, ATT_Q_W), F32),
        compiler_params=_cp(("parallel",)),
    )(*lead, sinks, att, att, b_attn)


def _attn_bwd(att, b_attn, sinks, dmix, *, name):
    T = att.shape[0]
    B, hd = ATT_BLOCK, ATT_HD
    NB = T // B
    scale = 1.0 / math.sqrt(hd)

    def body(sink_ref, cur_ref, prev_ref, ba_ref, do_ref, daq_ref, dakv_ref, dsink_ref, dbq_ref, dbkv_ref, carry_sc):
        n = pl.program_id(0)

        @pl.when(n == 0)
        def _():
            carry_sc[...] = jnp.zeros_like(carry_sc)
            dsink_ref[...] = jnp.zeros_like(dsink_ref)
            dbq_ref[...] = jnp.zeros_like(dbq_ref)
            dbkv_ref[...] = jnp.zeros_like(dbkv_ref)

        @pl.when(n < NB)
        def _():
            valid = _att_valid(n)
            hrow = lax.broadcasted_iota(jnp.int32, (SUBLANES, 128), 0)
            dsink = jnp.zeros((SUBLANES, 128), F32)
            dqs = []
            dks = [jnp.zeros((2 * B, hd), F32)] * ATT_KV
            dvs = [jnp.zeros((2 * B, hd), F32)] * ATT_KV
            for h0 in range(0, ATT_HEADS, ATT_STACK):
                kv = h0 // ATT_GROUP
                qs, kc, vc = _att_load(cur_ref, prev_ref, ba_ref, h0)
                prob, psink = _att_probs(qs, kc, valid, sink_ref, h0)
                dout = jnp.concatenate([do_ref[:, hd * (h0 + g):hd * (h0 + g + 1)] for g in range(ATT_STACK)], axis=0)
                dp = _dot_nt(vc, dout)
                delta = jnp.sum(prob * dp, axis=0, keepdims=True)
                dsc = prob * (dp - delta) * scale
                dq = _dot_tn(dsc, kc)
                dks[kv] = dks[kv] + _dot_nn(dsc, qs)
                dvs[kv] = dvs[kv] + _dot_nn(prob, dout)
                dsk = psink * delta
                for g in range(ATT_STACK):
                    dqs.append(dq[B * g:B * (g + 1)])
                    tot = jnp.sum(dsk[:, B * g:B * (g + 1)], axis=1, keepdims=True)
                    dsink = dsink - jnp.where(hrow == h0 + g, tot, 0.0)
            daq = jnp.concatenate(dqs, axis=1).astype(daq_ref.dtype)
            daq_ref[...] = daq
            dsink_ref[...] += dsink
            dbq_ref[...] += jnp.sum(daq.astype(F32), axis=0, keepdims=True)
            done = carry_sc[...] + jnp.concatenate([d[:B] for d in dks + dvs], axis=1)
            dakv_ref[...] = done.astype(dakv_ref.dtype)
            dbkv_ref[...] += jnp.sum(done.astype(dakv_ref.dtype).astype(F32), axis=0, keepdims=True)
            carry_sc[...] = jnp.concatenate([d[B:] for d in dks + dvs], axis=1)

        @pl.when(n == NB)
        def _():
            done = carry_sc[...]
            dakv_ref[...] = done.astype(dakv_ref.dtype)
            dbkv_ref[...] += jnp.sum(done.astype(dakv_ref.dtype).astype(F32), axis=0, keepdims=True)

    cl = lambda n: jnp.minimum(n, NB - 1)
    return pl.pallas_call(
        body, name=name, grid=(NB + 1,),
        in_specs=[pl.BlockSpec(memory_space=pltpu.SMEM),
                  pl.BlockSpec((B, ATT_COLS), lambda n: (cl(n), 0)),
                  pl.BlockSpec((B, ATT_COLS), lambda n: (jnp.maximum(cl(n) - 1, 0), 0)),
                  pl.BlockSpec((1, ATT_COLS), lambda n: (0, 0)),
                  pl.BlockSpec((B, ATT_Q_W), lambda n: (cl(n), 0))],
        out_specs=[pl.BlockSpec((B, ATT_Q_W), lambda n: (cl(n), 0)),
                   pl.BlockSpec((B, 2 * ATT_KV_W), lambda n: (jnp.maximum(n - 1, 0), 0)),
                   pl.BlockSpec((SUBLANES, 128), lambda n: (0, 0)),
                   pl.BlockSpec((1, ATT_Q_W), lambda n: (0, 0)),
                   pl.BlockSpec((1, 2 * ATT_KV_W), lambda n: (0, 0))],
        out_shape=[_sds((T, ATT_Q_W), BF16), _sds((T, 2 * ATT_KV_W), BF16), _sds((SUBLANES, 128), F32),
                   _sds((1, ATT_Q_W), F32), _sds((1, 2 * ATT_KV_W), F32)],
        scratch_shapes=[pltpu.VMEM((B, 2 * ATT_KV_W), F32)],
        compiler_params=_cp(("arbitrary",)),
    )(sinks, att, att, b_attn, dmix)


def _silu_and_grad(x):
    sg = _sigmoid(x)
    return x * sg, sg * (1.0 + x * (1.0 - sg))


def _mix_fwd_fn(o_raw, hg, o_att, hgw):
    outs = []
    for h in range(HG_HEADS):
        sl = slice(HG_DK * h, HG_DK * (h + 1))
        silu, _ = _silu_and_grad(hg[:, sl])
        outs.append(_rms_fwd(o_raw[:, sl], hgw) * silu)
    outs.append(o_att)
    return (jnp.concatenate(outs, axis=1),)


def _mix_bwd_fn(o_raw, hg, dmix, hgw):
    dos, dhgs = [], []
    dw = jnp.zeros((1, HG_DK), F32)
    for h in range(HG_HEADS):
        sl = slice(HG_DK * h, HG_DK * (h + 1))
        silu, dsilu = _silu_and_grad(hg[:, sl])
        dy = dmix[:, sl]
        dhgs.append(dy * _rms_fwd(o_raw[:, sl], hgw) * dsilu)
        dx, dwh = _rms_bwd(o_raw[:, sl], hgw, dy * silu)
        dos.append(dx)
        dw = dw + dwh
    return jnp.concatenate(dos, axis=1), jnp.concatenate(dhgs, axis=1), dw


def _final_fn(h2, tgt, wf):
    d = h2.shape[1]
    err = _rms_fwd(h2, wf) - tgt
    loss_cols = (0.5 / d) * jnp.sum(err * err, axis=0, keepdims=True)
    dh2, dwf = _rms_bwd(h2, wf, err * (1.0 / d))
    return dh2, dh2, loss_cols, dwf


class _NoExchange:
    def __init__(self, weights):
        self.weights = weights

    def start(self):
        return None

    def w_in(self, after):
        return self.weights["w_in_t"]

    def mid(self, after):
        return None

    def rest(self, after):
        return self.weights

    def ffn_grads(self, gs):
        return None

    def ffn_grads_send(self, after):
        return None


def _local_step(x, tgt, p, ex):
    T, D = x.shape
    row = lambda n, dt: _sds((T, n), dt)
    acc = lambda n: _sds((1, n), F32)

    (u,) = _rowwise(lambda xv, w: (_rms_fwd(xv, w),), [_full(x)], [p["norm_mix_w"]], [row(D, BF16)], [], name="rms_mix",
                    after=ex.start())
    p = dict(p, w_in_t=ex.w_in(u))
    hq, hf, hi, hg, att = _mm_nt(u, p["w_in_t"], splits=[HG_W] * 4 + [ATT_COLS], out_dtype=F32, name="in_proj")
    o_raw, states = _hgrn_fwd(hq, hf, hi, p["lb"], name="hgrn_fwd")
    o_att = _attn_fwd(att, p["b_attn"], p["sinks"], name="attn_fwd", after=ex.mid(o_raw))
    p = dict(p, **ex.rest(o_att))
    def out_epilogue(prod, xv, w):
        h1v = prod + xv
        return h1v, _rms_fwd(h1v, w)

    h1, v, mix = _mm_nn(None, [p["w_out"]], name="mix_out_proj",
                        prologue=(lambda *a: _mix_fwd_fn(*a)[0], [o_raw, hg, o_att], [p["hg_norm_w"]], row(D, BF16)),
                        epilogue=(out_epilogue, [x], [p["norm_ffn_w"]], [row(D, F32), row(D, BF16)], []))
    gp, up, gate, act = _ffn_in(v, p["w_gate_t"], p["w_up_t"], p["conv_w8"], p["conv_b"], name="ffn_in")
    def down_epilogue(prod, h1v, tgtv, wf):
        return _final_fn(prod + h1v, tgtv, wf)

    dh2, dh2_b, loss_cols, d_final = _mm_nn(
        [[act]], [p["w_down"]], name="down_proj_loss",
        epilogue=(down_epilogue, [h1, tgt], [p["final_norm_w"]], [row(D, F32), row(D, BF16)], [acc(D), acc(D)]))

    g_down = _mm_tn([act], dh2_b, name="g_down")
    dgp, dup, d_conv_w8, d_conv_b = _ffn_back(dh2_b, p["w_down"], gp, up, gate, p["conv_w8"], name="ffn_back")
    g_gate_t = _mm_tn([dgp], v, name="g_gate")
    g_up_t = _mm_tn([dup], v, name="g_up")
    swapping = ex.ffn_grads([g_gate_t, g_up_t, g_down])

    def ffn_norm_bwd(dvv, hv, dh2v, w):
        dx, dw = _rms_bwd(hv, w, dvv)
        dh1v = dx + dh2v
        return dh1v, dh1v, dw

    dh1, dh1_b, d_norm_ffn = _mm_nn(
        [[dgp], [dup]], [p["w_gate_t"], p["w_up_t"]], name="d_v_norm", after=swapping,
        epilogue=(ffn_norm_bwd, [h1, dh2], [p["norm_ffn_w"]], [row(D, F32), row(D, BF16)], [acc(D)]))
    sent = ex.ffn_grads_send(dh1_b)
    def mix_bwd(dmixv, o_rawv, hgv, hgw):
        do_rawv, dhgv, dw = _mix_bwd_fn(o_rawv, hgv, dmixv[:, :HG_W], hgw)
        return do_rawv, dhgv, dmixv[:, HG_W:], dw

    do_raw, dhg, do_att, d_hg_norm = _mm_nn(
        [[dh1_b]], [p["w_out"]], name="d_mix_bwd", w_transposed=True, after=sent,
        epilogue=(mix_bwd, [o_raw, hg], [p["hg_norm_w"]], [row(HG_W, F32), row(HG_W, BF16), row(ATT_Q_W, F32)], [acc(HG_DK)]))
    g_out = _mm_tn([mix], dh1_b, name="g_out")
    daq, dakv, d_sinks8, d_bq, d_bkv = _attn_bwd(att, p["b_attn"], p["sinks"], do_att, name="attn_bwd")
    dhq, dhf, dhi, d_lb = _hgrn_bwd(hq, hf, hi, p["lb"], states, do_raw, name="hgrn_bwd")
    pieces = [dhq, dhf, dhi, dhg, daq, dakv]
    g_in_t = _mm_tn(pieces, u, name="g_in")

    def mix_norm_bwd(duv, xv, dh1v, w):
        dx, dw = _rms_bwd(xv, w, duv)
        return dx + dh1v, dw

    dx, d_norm_mix = _mm_nn([pieces], [p["w_in_t"]], name="d_u_norm",
                            epilogue=(mix_norm_bwd, [x, dh1], [p["norm_mix_w"]], [row(D, F32)], [acc(D)]))
    grads = dict(g_in_t=g_in_t, g_out=g_out, g_gate_t=g_gate_t, g_up_t=g_up_t, g_down=g_down,
                 norm_mix_w=d_norm_mix, b_attn=jnp.concatenate([d_bq, d_bkv], axis=1), lb=d_lb, hg_norm_w=d_hg_norm,
                 sinks8=d_sinks8, norm_ffn_w=d_norm_ffn, conv_w8=d_conv_w8, conv_b=d_conv_b, final_norm_w=d_final)
    return loss_cols, dx, grads


SLAB = (IN_COLS // N_CHIPS, D_FF // N_CHIPS, D_FF // N_CHIPS, D_FF // N_CHIPS, D_MODEL // N_CHIPS)
N_W = len(SLAB)
PACK_OFF = tuple(sum(SLAB[:i]) for i in range(N_W))
PACK_ROWS = sum(SLAB)
FULL_OFF = tuple(N_CHIPS * o for o in PACK_OFF)
FULL_ROWS = N_CHIPS * PACK_ROWS
HALF = tuple(s // 2 for s in SLAB)
HPACK_OFF = tuple(sum(HALF[:i]) for i in range(N_W))
HPACK_ROWS = sum(HALF)
HFULL_OFF = tuple(N_CHIPS * o for o in HPACK_OFF)
HFULL_ROWS = N_CHIPS * HPACK_ROWS
CHIP_FLIPS = ((1, 0), (0, 1), (1, 1))
N_DEV = 8
BF16_ROWS = 16
ANY = pl.BlockSpec(memory_space=pl.ANY)


def _pos():
    return lax.axis_index("x"), lax.axis_index("y"), lax.axis_index("c")


def _flip(v, f):
    return 1 - v if f else v


def _rcopy(src, dst, ssem, rsem, dev):
    return pltpu.make_async_remote_copy(src_ref=src, dst_ref=dst, send_sem=ssem, recv_sem=rsem, device_id=dev,
                                        device_id_type=pl.DeviceIdType.MESH)


def _rows(ref, start, n, align=None):
    if not isinstance(start, int):
        if align is None:
            align = SUBLANES * (4 // jnp.dtype(ref.dtype).itemsize)
        start = pl.multiple_of(start, align)
    return ref.at[pl.ds(start, n), :]


FFN_W = (1, 2, 3)
N_PEER = 1 + len(CHIP_FLIPS)
HBM = pl.BlockSpec(memory_space=pltpu.HBM)
SEM = pl.BlockSpec(memory_space=pltpu.SEMAPHORE)
EFFECT = pltpu.SideEffectType.DATAFLOW_SIDE_EFFECTING
LANES = 128


def _sent_rows(k, w, c):
    return (0, SLAB[w]) if k == 0 else (c * HALF[w], HALF[w])


def _gather_start(pack, cw8):
    D = pack.shape[1]
    lands = [lax.empty((N_CHIPS * SLAB[0], D), pack.dtype), lax.empty((3 * N_CHIPS * SLAB[1], D), pack.dtype),
             lax.empty((N_CHIPS * SLAB[4], D), pack.dtype), lax.empty((N_CHIPS,) + cw8.shape, cw8.dtype)]
    bufs = [pack, cw8] + lands

    def body(pack_ref, cw_ref, l_in, l_ffn, l_out, l_cw, *rest):
        in_send, in_recv, out_send, out_recv, ffn_send, ffn_recv = rest[:6]
        token = rest[-1]
        x, y, c = _pos()
        q = 2 * x + y
        peers = _gather_peers(x, y, c)

        def send(k, peer, w, land, base, ssem, rsem):
            r0, n = _sent_rows(k, w, c)
            _rcopy(_rows(pack_ref, PACK_OFF[w] + r0, n), _rows(land, base + q * SLAB[w] + r0, n), ssem, rsem, peer).start()

        for k, peer in enumerate(peers):
            send(k, peer, 0, l_in, 0, in_send.at[k], in_recv.at[k])
        for k, peer in enumerate(peers):
            send(k, peer, 4, l_out, 0, out_send.at[k], out_recv.at[k])
            _rcopy(cw_ref, l_cw.at[q], out_send.at[N_PEER + k], out_recv.at[N_PEER + k], peer).start()
        for j, w in enumerate(FFN_W):
            for k, peer in enumerate(peers):
                send(k, peer, w, l_ffn, j * N_CHIPS * SLAB[w], ffn_send.at[k], ffn_recv.at[k])
        token[...] = jnp.zeros_like(token)

    n_sem = (N_PEER, N_PEER, 2 * N_PEER, 2 * N_PEER, N_PEER, N_PEER)
    outs = pl.pallas_call(
        body, name="gather_start", in_specs=[HBM] * len(bufs),
        out_specs=[SEM] * len(n_sem) + [HBM] * len(bufs) + [pl.BlockSpec(memory_space=pltpu.VMEM)],
        out_shape=[pltpu.SemaphoreType.DMA((n,)) for n in n_sem]
        + [pltpu.HBM(b.shape, b.dtype) for b in bufs] + [TOKEN],
        input_output_aliases={i: len(n_sem) + i for i in range(len(bufs))},
        compiler_params=pltpu.CompilerParams(has_side_effects=EFFECT),
    )(*[pltpu.with_memory_space_constraint(b, pltpu.HBM) for b in bufs])
    bufs_out = outs[len(n_sem):]
    return dict(in_sems=outs[0:2], out_sems=outs[2:4], ffn_sems=outs[4:6], pack=bufs_out[0], cw=bufs_out[1], l_in=bufs_out[2],
                l_ffn=bufs_out[3], l_out=bufs_out[4], l_cw=bufs_out[5], token=bufs_out[6])


def _gather_peers(x, y, c):
    return [(x, y, 1 - c)] + [(_flip(x, fx), _flip(y, fy), c) for fx, fy in CHIP_FLIPS]


def _gather_wait_in(g, after):
    def body(pack_ref, l_in, send, recv, after_ref, pack_out, l_out):
        for k, peer in enumerate(_gather_peers(*_pos())):
            n = _sent_rows(k, 0, 0)[1]
            cp = _rcopy(_rows(pack_ref, PACK_OFF[0], n), _rows(l_in, 0, n), send.at[k], recv.at[k], peer)
            cp.wait_send()
            cp.wait_recv()

    return pl.pallas_call(
        body, name="gather_wait_in", in_specs=[HBM, HBM, SEM, SEM, ANY], out_specs=[HBM, HBM],
        out_shape=[pltpu.HBM(g["pack"].shape, g["pack"].dtype), pltpu.HBM(g["l_in"].shape, g["l_in"].dtype)],
        input_output_aliases={0: 0, 1: 1}, compiler_params=pltpu.CompilerParams(has_side_effects=EFFECT),
    )(g["pack"], g["l_in"], *g["in_sems"], after)


def _gather_wait_rest(g, pack, after):
    def body(pack_ref, cw_ref, l_ffn, l_out, l_cw, o_send, o_recv, f_send, f_recv, after_ref, o_ffn, o_out, o_cw):
        for k, peer in enumerate(_gather_peers(*_pos())):
            n_out = _sent_rows(k, 4, 0)[1]
            n_ffn = len(FFN_W) * _sent_rows(k, FFN_W[0], 0)[1]
            for cp in (_rcopy(_rows(pack_ref, PACK_OFF[4], n_out), _rows(l_out, 0, n_out), o_send.at[k], o_recv.at[k], peer),
                       _rcopy(cw_ref, l_cw.at[0], o_send.at[N_PEER + k], o_recv.at[N_PEER + k], peer),
                       _rcopy(_rows(pack_ref, PACK_OFF[FFN_W[0]], n_ffn), _rows(l_ffn, 0, n_ffn), f_send.at[k], f_recv.at[k], peer)):
                cp.wait_send()
                cp.wait_recv()

    ins = [pack, g["cw"], g["l_ffn"], g["l_out"], g["l_cw"]]
    return pl.pallas_call(
        body, name="gather_wait_rest", in_specs=[HBM] * 5 + [SEM] * 4 + [ANY], out_specs=[HBM] * 3,
        out_shape=[pltpu.HBM(b.shape, b.dtype) for b in ins[2:]],
        input_output_aliases={2: 0, 3: 1, 4: 2}, compiler_params=pltpu.CompilerParams(has_side_effects=EFFECT),
    )(*ins, *g["out_sems"], *g["ffn_sems"], after)


FWD_IN = ((0, 0, 0),)
FWD_REST = tuple((0, w, j * N_CHIPS * SLAB[w]) for j, w in enumerate(FFN_W)) + ((1, 4, 0),)


def _forward_copies(layout, src, dst, send_sems, recv_sems):
    x, y, c = _pos()
    sib = (x, y, 1 - c)
    cps = []
    for fx, fy in CHIP_FLIPS:
        qa = 2 * _flip(x, fx) + _flip(y, fy)
        for bi, w, base in layout:
            r0 = base + qa * SLAB[w] + c * HALF[w]
            cps.append(_rcopy(_rows(src[bi], r0, HALF[w]), _rows(dst[bi], r0, HALF[w]),
                              send_sems.at[len(cps)], recv_sems.at[len(cps)], sib))
    return cps


def _forward_in(l_in):
    n = len(CHIP_FLIPS) * len(FWD_IN)

    def body(in_ref, out_ref, send_sems, recv_sems):
        cps = _forward_copies(FWD_IN, [in_ref], [out_ref], send_sems, recv_sems)
        for cp in cps:
            cp.start()
        for cp in cps:
            cp.wait_recv()
        for cp in cps:
            cp.wait_send()

    return pl.pallas_call(
        body, name="forward_in", in_specs=[ANY], out_specs=ANY, out_shape=_sds(l_in.shape, l_in.dtype),
        input_output_aliases={0: 0},
        scratch_shapes=[pltpu.SemaphoreType.DMA((n,)), pltpu.SemaphoreType.DMA((n,))],
    )(l_in)


def _forward_rest_start(l_ffn, l_out):
    n = len(CHIP_FLIPS) * len(FWD_REST)
    bufs = [l_ffn, l_out]

    def body(a_ref, b_ref, send_sems, recv_sems, a_out, b_out, token):
        for cp in _forward_copies(FWD_REST, [a_ref, b_ref], [a_ref, b_ref], send_sems, recv_sems):
            cp.start()
        token[...] = jnp.zeros_like(token)

    outs = pl.pallas_call(
        body, name="forward_rest_start", in_specs=[HBM] * 2,
        out_specs=[SEM, SEM, HBM, HBM, pl.BlockSpec(memory_space=pltpu.VMEM)],
        out_shape=[pltpu.SemaphoreType.DMA((n,)), pltpu.SemaphoreType.DMA((n,))]
        + [pltpu.HBM(b.shape, b.dtype) for b in bufs] + [TOKEN],
        input_output_aliases={0: 2, 1: 3}, compiler_params=pltpu.CompilerParams(has_side_effects=EFFECT),
    )(*[pltpu.with_memory_space_constraint(b, pltpu.HBM) for b in bufs])
    return dict(sems=outs[0:2], bufs=outs[2:4], token=outs[4])


def _forward_rest_wait(s, after):
    def body(a_ref, b_ref, send_sems, recv_sems, after_ref, a_out, b_out):
        for cp in _forward_copies(FWD_REST, [a_ref, b_ref], [a_ref, b_ref], send_sems, recv_sems):
            cp.wait_send()
            cp.wait_recv()

    return pl.pallas_call(
        body, name="forward_rest_wait", in_specs=[HBM, HBM, SEM, SEM, ANY], out_specs=[HBM, HBM],
        out_shape=[pltpu.HBM(b.shape, b.dtype) for b in s["bufs"]],
        input_output_aliases={0: 0, 1: 1}, compiler_params=pltpu.CompilerParams(has_side_effects=EFFECT),
    )(*s["bufs"], *s["sems"], after)


def _exchange_halves(ws, gs, small, *, name):
    D = gs[0].shape[1]
    n = len(ws)
    has_small = small is not None

    def body(*refs):
        g = refs[:n]
        t = refs[n + has_small:2 * n + has_small]
        sems = refs[2 * n + 2 * has_small:]
        d2d_send, d2d_recv = sems[0], sems[1]
        x, y, c = _pos()
        sib = (x, y, 1 - c)
        drains = []
        for i, w in enumerate(ws):
            h = HALF[w]
            for qq in range(N_CHIPS):
                _rcopy(_rows(g[i], qq * SLAB[w] + (1 - c) * h, h), _rows(t[i], qq * h, h),
                       d2d_send.at[i], d2d_recv.at[i], sib).start()
            drains.append(_rcopy(t[i], t[i], d2d_send.at[i], d2d_recv.at[i], sib))
        if has_small:
            small_ref, sall_ref = refs[n], refs[2 * n + 1]
            sm_send, sm_recv, loc_sem = sems[2], sems[3], sems[4]
            me = 4 * x + 2 * y + c
            own_small = pltpu.make_async_copy(small_ref, sall_ref.at[me], loc_sem)
            own_small.start()
            for f in range(1, N_DEV):
                peer = (_flip(x, f & 4), _flip(y, f & 2), _flip(c, f & 1))
                cp = _rcopy(small_ref, sall_ref.at[me], sm_send.at[f - 1], sm_recv.at[f - 1], peer)
                cp.start()
                drains.append(cp)
        for d in drains:
            d.wait_recv()
        for d in drains:
            d.wait_send()
        if has_small:
            own_small.wait()

    out_shape = [_sds((N_CHIPS * HALF[w], D), gs[0].dtype) for w in ws]
    scratch = [pltpu.SemaphoreType.DMA((n,)), pltpu.SemaphoreType.DMA((n,))]
    if has_small:
        out_shape.append(_sds((N_DEV,) + small.shape, F32))
        scratch += [pltpu.SemaphoreType.DMA((N_DEV - 1,)), pltpu.SemaphoreType.DMA((N_DEV - 1,)), pltpu.SemaphoreType.DMA]
    return pl.pallas_call(
        body, name=name, in_specs=[ANY] * (n + has_small), out_specs=[ANY] * (n + has_small),
        out_shape=out_shape, scratch_shapes=scratch,
    )(*gs, *([small] if has_small else []))


def _halves_copies(ws, g, t, send_sems, recv_sems):
    x, y, c = _pos()
    sib = (x, y, 1 - c)
    cps = []
    for i, w in enumerate(ws):
        h = HALF[w]
        for qq in range(N_CHIPS):
            cps.append(_rcopy(_rows(g[i], qq * SLAB[w] + (1 - c) * h, h), _rows(t[i], qq * h, h),
                              send_sems.at[N_CHIPS * i + qq], recv_sems.at[N_CHIPS * i + qq], sib))
    return cps


def _halves_start(ws, gs, *, name):
    D = gs[0].shape[1]
    n = len(ws)
    bufs = list(gs) + [lax.empty((N_CHIPS * HALF[w], D), gs[0].dtype) for w in ws]

    def body(*refs):
        for cp in _halves_copies(ws, refs[:n], refs[n:2 * n], refs[2 * n], refs[2 * n + 1]):
            cp.start()
        refs[-1][...] = jnp.zeros_like(refs[-1])

    outs = pl.pallas_call(
        body, name=name, in_specs=[HBM] * (2 * n),
        out_specs=[SEM, SEM] + [HBM] * (2 * n) + [pl.BlockSpec(memory_space=pltpu.VMEM)],
        out_shape=[pltpu.SemaphoreType.DMA((N_CHIPS * n,)), pltpu.SemaphoreType.DMA((N_CHIPS * n,))]
        + [pltpu.HBM(b.shape, b.dtype) for b in bufs] + [TOKEN],
        input_output_aliases={i: 2 + i for i in range(2 * n)},
        compiler_params=pltpu.CompilerParams(has_side_effects=EFFECT),
    )(*[pltpu.with_memory_space_constraint(b, pltpu.HBM) for b in bufs])
    return dict(sems=outs[0:2], gs=outs[2:2 + n], theirs=outs[2 + n:2 + 2 * n], token=outs[-1])


def _halves_wait(ws, s, after, *, name):
    n = len(ws)

    def body(*refs):
        for cp in _halves_copies(ws, refs[:n], refs[n:2 * n], refs[2 * n], refs[2 * n + 1]):
            cp.wait_send()
            cp.wait_recv()

    bufs = list(s["gs"]) + list(s["theirs"])
    outs = pl.pallas_call(
        body, name=name, in_specs=[HBM] * (2 * n) + [SEM, SEM, ANY], out_specs=[HBM] * (2 * n),
        out_shape=[pltpu.HBM(b.shape, b.dtype) for b in bufs],
        input_output_aliases={i: i for i in range(2 * n)},
        compiler_params=pltpu.CompilerParams(has_side_effects=EFFECT),
    )(*bufs, *s["sems"], after)
    return outs[:n], outs[n:]


REDUCE_SPLIT = 2


def _chip_partial(ws, gs, theirs, *, name, out_dtype=F32):
    D = gs[0].shape[1]
    n = len(ws)

    def body(*refs):
        for i in range(n):
            refs[2 * n + i][...] = (refs[i][...].astype(F32) + refs[n + i][...].astype(F32)).astype(out_dtype)

    blk = [HALF[w] // REDUCE_SPLIT for w in ws]
    mine = [pl.BlockSpec((b, D), lambda qq, j: ((2 * qq + lax.axis_index("c")) * REDUCE_SPLIT + j, 0)) for b in blk]
    flat = [pl.BlockSpec((b, D), lambda qq, j: (qq * REDUCE_SPLIT + j, 0)) for b in blk]
    return pl.pallas_call(
        body, name=name, grid=(N_CHIPS, REDUCE_SPLIT), in_specs=mine + flat, out_specs=flat,
        out_shape=[_sds((N_CHIPS * HALF[w], D), out_dtype) for w in ws],
        compiler_params=_cp(("parallel", "parallel")),
    )(*gs, *theirs)


def _partial_copies(ws, part, got, send_sems, recv_sems):
    x, y, c = _pos()
    cps = []
    for k, (fx, fy) in enumerate(CHIP_FLIPS):
        peer = (_flip(x, fx), _flip(y, fy), c)
        qp = 2 * _flip(x, fx) + _flip(y, fy)
        for i, w in enumerate(ws):
            cps.append(_rcopy(_rows(part[i], qp * HALF[w], HALF[w]), _rows(got[i], k * HALF[w], HALF[w]),
                              send_sems.at[len(ws) * k + i], recv_sems.at[len(ws) * k + i], peer))
    return cps


def _send_chip_partials(ws, parts, *, name):
    D = parts[0].shape[1]
    n = len(ws)

    def body(*refs):
        cps = _partial_copies(ws, refs[:n], refs[n:2 * n], refs[2 * n], refs[2 * n + 1])
        for cp in cps:
            cp.start()
        for cp in cps:
            cp.wait_recv()
        for cp in cps:
            cp.wait_send()

    return pl.pallas_call(
        body, name=name, in_specs=[ANY] * n, out_specs=[ANY] * n,
        out_shape=[_sds((len(CHIP_FLIPS) * HALF[w], D), parts[0].dtype) for w in ws],
        scratch_shapes=[pltpu.SemaphoreType.DMA((len(CHIP_FLIPS) * n,)), pltpu.SemaphoreType.DMA((len(CHIP_FLIPS) * n,))],
    )(*parts)


def _send_start(ws, parts, *, name):
    D = parts[0].shape[1]
    n = len(ws)
    bufs = list(parts) + [lax.empty((len(CHIP_FLIPS) * HALF[w], D), parts[0].dtype) for w in ws]

    def body(*refs):
        send_sems, recv_sems = refs[2 * n], refs[2 * n + 1]
        for cp in _partial_copies(ws, refs[:n], refs[n:2 * n], send_sems, recv_sems):
            cp.start()
        refs[-1][...] = jnp.zeros_like(refs[-1])

    outs = pl.pallas_call(
        body, name=name, in_specs=[HBM] * (2 * n),
        out_specs=[SEM, SEM] + [HBM] * (2 * n) + [pl.BlockSpec(memory_space=pltpu.VMEM)],
        out_shape=[pltpu.SemaphoreType.DMA((len(CHIP_FLIPS) * n,)), pltpu.SemaphoreType.DMA((len(CHIP_FLIPS) * n,))]
        + [pltpu.HBM(b.shape, b.dtype) for b in bufs] + [TOKEN],
        input_output_aliases={i: 2 + i for i in range(2 * n)},
        compiler_params=pltpu.CompilerParams(has_side_effects=EFFECT),
    )(*[pltpu.with_memory_space_constraint(b, pltpu.HBM) for b in bufs])
    return dict(sems=outs[0:2], parts=outs[2:2 + n], got=outs[2 + n:2 + 2 * n], token=outs[-1])


def _send_wait(ws, s, after, *, name):
    n = len(ws)

    def body(*refs):
        for cp in _partial_copies(ws, refs[:n], refs[n:2 * n], refs[2 * n], refs[2 * n + 1]):
            cp.wait_send()
            cp.wait_recv()

    bufs = list(s["parts"]) + list(s["got"])
    outs = pl.pallas_call(
        body, name=name, in_specs=[HBM] * (2 * n) + [SEM, SEM] + [ANY] * len(after), out_specs=[HBM] * (2 * n),
        out_shape=[pltpu.HBM(b.shape, b.dtype) for b in bufs],
        input_output_aliases={i: i for i in range(2 * n)},
        compiler_params=pltpu.CompilerParams(has_side_effects=EFFECT),
    )(*bufs, *s["sems"], *after)
    return outs[:n], outs[n:]


def _chip_reduce(ws, parts, got, *, name, after=None):
    D = parts[0].shape[1]
    nk = len(CHIP_FLIPS)
    n = len(ws)
    extra = [] if after is None else [after]

    def body(*refs):
        refs = refs[len(extra):]
        outs = refs[(1 + nk) * n:]
        for i in range(n):
            acc = refs[i][...].astype(F32)
            for k in range(nk):
                acc = acc + refs[n * (1 + k) + i][...].astype(F32)
            outs[i][...] = acc

    blk = [HALF[w] // REDUCE_SPLIT for w in ws]

    def q_idx(j):
        return (2 * lax.axis_index("x") + lax.axis_index("y")) * REDUCE_SPLIT + j

    in_specs = [pl.BlockSpec((b, D), lambda j: (q_idx(j), 0)) for b in blk]
    for k in range(nk):
        in_specs += [pl.BlockSpec((b, D), functools.partial(lambda j, k: (k * REDUCE_SPLIT + j, 0), k=k)) for b in blk]
    out_specs = [pl.BlockSpec((b, D), lambda j: (lax.axis_index("c") * REDUCE_SPLIT + j, 0)) for b in blk]
    return pl.pallas_call(
        body, name=name, grid=(REDUCE_SPLIT,), in_specs=[ANY] * len(extra) + in_specs, out_specs=out_specs,
        out_shape=[_sds((SLAB[w], D), F32) for w in ws],
        compiler_params=_cp(("parallel",)),
    )(*extra, *parts, *[g for _ in range(nk) for g in got])


def _exchange_reduced(ws, shards, *, name):
    n = len(ws)

    def body(*refs):
        ins, outs = refs[:n], refs[n:2 * n]
        send_sems, recv_sems = refs[2 * n], refs[2 * n + 1]
        x, y, c = _pos()
        sib = (x, y, 1 - c)
        cps = []
        for i, w in enumerate(ws):
            cp = _rcopy(_rows(ins[i], c * HALF[w], HALF[w]), _rows(outs[i], c * HALF[w], HALF[w]),
                        send_sems.at[i], recv_sems.at[i], sib)
            cp.start()
            cps.append(cp)
        for cp in cps:
            cp.wait_recv()
        for cp in cps:
            cp.wait_send()

    return pl.pallas_call(
        body, name=name, in_specs=[ANY] * n, out_specs=[ANY] * n,
        out_shape=[_sds(s.shape, s.dtype) for s in shards], input_output_aliases={i: i for i in range(n)},
        scratch_shapes=[pltpu.SemaphoreType.DMA((n,)), pltpu.SemaphoreType.DMA((n,))],
    )(*shards)


def _adamw_fn(w, g, m, v):
    m2 = ADAM_B1 * m + (1.0 - ADAM_B1) * g
    v2 = ADAM_B2 * v + (1.0 - ADAM_B2) * (g * g)
    m_hat = m2 / (1.0 - ADAM_B1 ** ADAM_STEP)
    v_hat = v2 / (1.0 - ADAM_B2 ** ADAM_STEP)
    return -ADAM_LR * (m_hat / (jnp.sqrt(v_hat) + ADAM_EPS) + ADAM_WD * w), m2, v2


def _adamw(w, g, m, v, *, name):
    shp = _sds(w.shape, F32)
    rows = w.shape[0]
    tm = max(t for t in range(SUBLANES, 512 + 1, SUBLANES) if rows % t == 0)
    return _rowwise(lambda wv, gv, mv, vv: (gv, *_adamw_fn(wv, gv, mv, vv)), [_full(w), _full(g), _full(m), _full(v)], [],
                    [shp] * 4, [], name=name, tm=tm)


SMALL_SEGS = (("loss", 8), ("norm_mix_w", 8), ("b_attn", 8), ("lb_logits", 8), ("hg_norm_w", 8), ("sinks", 8),
              ("norm_ffn_w", 8), ("conv_w", 72), ("conv_b", 24), ("final_norm_w", 8))
SMALL_OFF = {n: sum(r for _, r in SMALL_SEGS[:i]) for i, (n, _) in enumerate(SMALL_SEGS)}
SMALL_ROWS = sum(r for _, r in SMALL_SEGS)
LANES = 128


def _pack_small(parts):
    segs = []
    for n, r in SMALL_SEGS:
        a = parts.get(n)
        flat = jnp.zeros((0,), F32) if a is None else a.reshape(-1).astype(F32)
        segs.append(jnp.pad(flat, (0, r * LANES - flat.shape[0])).reshape(r, LANES))
    return jnp.concatenate(segs, axis=0)


def _unpack_small(pack, n, shape):
    size = math.prod(shape)
    r0 = SMALL_OFF[n]
    return pack[r0:r0 + dict(SMALL_SEGS)[n]].reshape(-1)[:size].reshape(shape)


def _small_update(sall, wp, mp, vp, *, after):
    R = SMALL_ROWS
    r_lb = SMALL_OFF["lb_logits"]

    def body(after_ref, s_ref, w_ref, m_ref, v_ref, g_ref, d_ref, m2_ref, v2_ref, loss_ref):
        g = s_ref[0]
        for i in range(1, N_DEV):
            g = g + s_ref[i]
        tot = jnp.sum(jnp.sum(g[0:8], axis=1, keepdims=True), axis=0, keepdims=True)
        loss_ref[...] = jnp.broadcast_to(tot, loss_ref.shape)
        lg = w_ref[r_lb:r_lb + 8, :]
        p0 = _sigmoid(lg - pltpu.roll(lg, 4, 0))
        d = g[r_lb:r_lb + 8]
        d = d + pltpu.roll(d, 4, 0)
        sign = jnp.where(lax.broadcasted_iota(jnp.int32, d.shape, 0) < 4, 1.0, -1.0)
        g = jnp.concatenate([g[:r_lb], sign * d * p0 * (1.0 - p0), g[r_lb + 8:]], axis=0)
        g_ref[...] = g
        d_ref[...], m2_ref[...], v2_ref[...] = _adamw_fn(w_ref[...], g, m_ref[...], v_ref[...])

    full = pl.BlockSpec((R, LANES), lambda: (0, 0))
    return pl.pallas_call(
        body, name="small_update",
        in_specs=[ANY, pl.BlockSpec((N_DEV, R, LANES), lambda: (0, 0, 0)), full, full, full],
        out_specs=[full, full, full, full, pl.BlockSpec((8, LANES), lambda: (0, 0))],
        out_shape=[_sds((R, LANES), F32)] * 4 + [_sds((8, LANES), F32)],
        compiler_params=_cp(),
    )(after, sall, wp, mp, vp)


def _lb_fwd(lb_logits):
    n = lb_logits.shape[1]

    def body(l_ref, o_ref):
        o_ref[...] = _sigmoid(l_ref[0:1, :] - l_ref[1:2, :])

    return pl.pallas_call(body, name="lb_fwd", out_shape=jax.ShapeDtypeStruct((1, n), F32), compiler_params=_cp())(lb_logits)


class _MeshExchange:
    def __init__(self, pack, cw8):
        self.gather = _gather_start(pack, cw8)
        self.sent = None
        self.conv_w8 = None

    def start(self):
        return self.gather["token"]

    def w_in(self, after):
        self.pack, l_in = _gather_wait_in(self.gather, after)
        return (_forward_in(l_in), N_CHIPS * SLAB[0], 0)

    def mid(self, after):
        l_ffn, l_out, l_cw = _gather_wait_rest(self.gather, self.pack, after)
        self.conv_w8 = jnp.concatenate([l_cw[i] for i in range(N_CHIPS)], axis=1)
        self.passing = _forward_rest_start(l_ffn, l_out)
        return self.passing["token"]

    def rest(self, after):
        l_ffn, l_out = _forward_rest_wait(self.passing, after)
        rows = N_CHIPS * SLAB[FFN_W[0]]
        return dict(w_gate_t=(l_ffn, rows, 0), w_up_t=(l_ffn, rows, 1), w_down=(l_ffn, rows, 2),
                    w_out=(l_out, N_CHIPS * SLAB[4], 0), conv_w8=self.conv_w8)

    def ffn_grads(self, gs):
        self.swap = _halves_start(FFN_W, gs, name="halves_ffn_start")
        return self.swap["token"]

    def ffn_grads_send(self, after):
        gs, theirs = _halves_wait(FFN_W, self.swap, after, name="halves_ffn_wait")
        parts = _chip_partial(FFN_W, gs, theirs, name="chip_partial_ffn", out_dtype=BF16)
        self.sent = _send_start(FFN_W, parts, name="send_ffn_start")
        return self.sent["token"]


def kernel(x, norm_mix_w, w_in, b_attn, lb_logits, hg_norm_w, sinks, w_out, norm_ffn_w, w_gate, w_up, conv_w, conv_b, w_down, final_norm_w, loss_target, m_norm_mix_w, m_w_in, m_b_attn, m_lb_logits, m_hg_norm_w, m_sinks, m_w_out, m_norm_ffn_w, m_w_gate, m_w_up, m_conv_w, m_conv_b, m_w_down, m_final_norm_w, v_norm_mix_w, v_w_in, v_b_attn, v_lb_logits, v_hg_norm_w, v_sinks, v_w_out, v_norm_ffn_w, v_w_gate, v_w_up, v_conv_w, v_conv_b, v_w_down, v_final_norm_w):
    D = D_MODEL
    q = 2 * lax.axis_index("x") + lax.axis_index("y")
    ccols = D_FF // N_CHIPS

    pack = jnp.concatenate([w_in[0].T, w_gate[0].T, w_up[0].T, w_down[0], w_out[0]], axis=0).astype(BF16)
    cw8 = jnp.concatenate([conv_w[0], jnp.zeros((SUBLANES - 3, ccols), F32)], axis=0)
    ex = _MeshExchange(pack, cw8)
    p = dict(norm_mix_w=norm_mix_w, b_attn=b_attn, lb=_lb_fwd(lb_logits), hg_norm_w=hg_norm_w, sinks=sinks,
             norm_ffn_w=norm_ffn_w, conv_b=conv_b, final_norm_w=final_norm_w.reshape(1, D))
    loss_cols, dx, g = _local_step(x[0], loss_target[0], p, ex)
    conv_w8 = ex.conv_w8

    small = _pack_small(dict(loss=loss_cols, norm_mix_w=g["norm_mix_w"], b_attn=g["b_attn"], lb_logits=g["lb"],
                             hg_norm_w=g["hg_norm_w"], sinks=g["sinks8"], norm_ffn_w=g["norm_ffn_w"],
                             conv_w=g["conv_w8"][:3], conv_b=g["conv_b"], final_norm_w=g["final_norm_w"]))
    parts_ffn, got_ffn = _send_wait(FFN_W, ex.sent, [dx], name="send_ffn_wait")
    late = (0, 4)
    gs = [g["g_in_t"], g["g_out"]]
    *theirs, sall = _exchange_halves(late, gs, small, name="exchange_halves_late")
    parts_late = _chip_partial(late, gs, theirs, name="chip_partial_late", out_dtype=BF16)
    sent_late = _send_start(late, parts_late, name="send_late_start")
    big = {}

    def finish(ws, parts, got, specs, tag, after):
        shards = _exchange_reduced(ws, _chip_reduce(ws, parts, got, name="chip_reduce_" + tag, after=after),
                                   name="exchange_reduced_" + tag)
        deltas = []
        for gw, (n, w, m, v, tr) in zip(shards, specs):
            view = (lambda a: a[0].T) if tr else (lambda a: a[0])
            back = (lambda a: a.T[None]) if tr else (lambda a: a[None])
            res = _adamw(view(w), gw, view(m), view(v), name="adamw_" + n)
            big[n] = tuple(back(r) for r in res)
            deltas.append(res[1])
        return deltas

    done_ffn = finish(FFN_W, parts_ffn, got_ffn, (("w_gate", w_gate, m_w_gate, v_w_gate, True),
                                                  ("w_up", w_up, m_w_up, v_w_up, True),
                                                  ("w_down", w_down, m_w_down, v_w_down, False)), "ffn", sent_late["token"])

    def place(a):
        return lax.dynamic_update_slice(jnp.zeros((3, D_FF), F32), a[0], (0, q * ccols))

    def small_pack(ws, cw):
        nm, ba, lbl, hg, sk, nf, cb, fn = ws
        return _pack_small(dict(norm_mix_w=nm, b_attn=ba, lb_logits=lbl, hg_norm_w=hg,
                                sinks=jnp.broadcast_to(sk.reshape(ATT_HEADS, 1), (ATT_HEADS, LANES)), norm_ffn_w=nf,
                                conv_w=cw, conv_b=cb, final_norm_w=fn))

    wp = small_pack((norm_mix_w, b_attn, lb_logits, hg_norm_w, sinks, norm_ffn_w, conv_b, final_norm_w), conv_w8[:3])
    mp = small_pack((m_norm_mix_w, m_b_attn, m_lb_logits, m_hg_norm_w, m_sinks, m_norm_ffn_w, m_conv_b, m_final_norm_w),
                    place(m_conv_w))
    vp = small_pack((v_norm_mix_w, v_b_attn, v_lb_logits, v_hg_norm_w, v_sinks, v_norm_ffn_w, v_conv_b, v_final_norm_w),
                    place(v_conv_w))
    outs = _small_update(sall, wp, mp, vp, after=sent_late["token"])
    loss = outs[4][0, 0]
    parts_late, got_late = _send_wait(late, sent_late, [*done_ffn, outs[4]], name="send_late_wait")
    finish(late, parts_late, got_late, (("w_in", w_in, m_w_in, v_w_in, True), ("w_out", w_out, m_w_out, v_w_out, False)),
           "late", None)

    def small_out(pk, n, ref):
        if n == "sinks":
            return pk[SMALL_OFF[n]:SMALL_OFF[n] + ATT_HEADS, 0].reshape(ref.shape)
        if n == "conv_w":
            full = _unpack_small(pk, n, (3, D_FF))
            return lax.dynamic_slice(full, (0, q * ccols), (3, ccols))[None]
        return _unpack_small(pk, n, ref.shape)

    refs = dict(norm_mix_w=norm_mix_w, b_attn=b_attn, lb_logits=lb_logits, hg_norm_w=hg_norm_w, sinks=sinks,
                norm_ffn_w=norm_ffn_w, conv_w=conv_w, conv_b=conv_b, final_norm_w=final_norm_w)
    order = ("norm_mix_w", "w_in", "b_attn", "lb_logits", "hg_norm_w", "sinks", "w_out", "norm_ffn_w", "w_gate", "w_up",
             "conv_w", "conv_b", "w_down", "final_norm_w")
    res = [loss, dx[None]]
    for k in range(4):
        for n in order:
            res.append(big[n][k] if n in big else small_out(outs[k], n, refs[n]))
    return tuple(res)
```

```python
import functools
import math

import jax
import jax.numpy as jnp
from jax import lax
from jax.experimental import pallas as pl
from jax.experimental.pallas import tpu as pltpu

F32 = jnp.float32
BF16 = jnp.bfloat16

D_MODEL = 1024
HG_HEADS = 4
HG_DK = 128
HG_W = HG_HEADS * HG_DK
HG_CHUNK = 64
HG_SUB = 8
HG_FWD_CHUNKS_PER_STEP = 4
HG_CHUNKS_PER_STEP = 2
ATT_HEADS = 8
ATT_KV = 2
ATT_GROUP = ATT_HEADS // ATT_KV
ATT_HD = 64
ATT_BLOCK = 128
ATT_Q_W = ATT_HEADS * ATT_HD
ATT_KV_W = ATT_KV * ATT_HD
ATT_COLS = ATT_Q_W + 2 * ATT_KV_W
IN_COLS = 4 * HG_W + ATT_COLS
D_FF = 2816
EPS = 1e-6
ADAM_LR, ADAM_B1, ADAM_B2, ADAM_EPS, ADAM_WD, ADAM_STEP = 0.001, 0.9, 0.999, 1e-08, 0.01, 10
NEG = -1e30

V7X_VMEM_BYTES = 64 * 1024 * 1024
VMEM_LIMIT = 48 * 1024 * 1024
SUBLANES = 8

N_CHIPS = 4


def _cp(sem=None, **kw):
    return pltpu.CompilerParams(dimension_semantics=sem, vmem_limit_bytes=VMEM_LIMIT, **kw)


def _sds(shape, dtype):
    return jax.ShapeDtypeStruct(shape, dtype)


TOKEN = jax.ShapeDtypeStruct((8, 128), jnp.float32)


def _wspec(w):
    arr, rows, blk = w
    return pl.BlockSpec((rows, arr.shape[1]), lambda i: (blk, 0))


def _mm_nt(a, w, *, splits, out_dtype, name, after=None, tm=512):
    M, K = a.shape
    N = w[1]
    tm = min(tm, M)
    assert sum(splits) == N and M % tm == 0
    offs = [sum(splits[:i]) for i in range(len(splits))]
    n_in = 2 if after is None else 3

    def body(*refs):
        a_ref, w_ref = refs[0], refs[1]
        acc = lax.dot_general(a_ref[...], w_ref[...], (((1,), (1,)), ((), ())), preferred_element_type=F32)
        for o_ref, c0, n in zip(refs[n_in:], offs, splits):
            o_ref[...] = acc[:, c0:c0 + n].astype(out_dtype)

    in_specs = [pl.BlockSpec((tm, K), lambda i: (i, 0)), _wspec(w)]
    args = [a, w[0]]
    if after is not None:
        in_specs.append(pl.BlockSpec(memory_space=pl.ANY))
        args.append(after)
    outs = pl.pallas_call(
        body, name=name, grid=(M // tm,), in_specs=in_specs,
        out_specs=[pl.BlockSpec((tm, n), lambda i: (i, 0)) for n in splits],
        out_shape=[_sds((M, n), out_dtype) for n in splits],
        compiler_params=_cp(("parallel",)),
    )(*args)
    return outs


def _mm_nn(pieces, ws, *, name, out_dtype=F32, residual=None, epilogue=None, prologue=None, after=None,
           w_transposed=False, tm=512):
    pro_fn, pro_rows, pro_bc, pro_out = prologue or (None, [], [], None)
    if prologue is not None:
        assert pieces is None and len(ws) == 1
        pieces = [[pro_out]]
    M = pieces[0][0].shape[0]
    K = ws[0][1] if w_transposed else ws[0][0].shape[1]
    tm = min(tm, M)
    flat = [] if prologue is not None else [p for grp in pieces for p in grp]
    n_p = len(flat)
    n_w = len(ws)
    n_pr, n_pb = len(pro_rows), len(pro_bc)
    fn, row_ins, bc_ins, row_outs, acc_outs = epilogue or (None, [], [], [_sds((M, K), out_dtype)], [])
    if residual is not None:
        assert epilogue is None
        row_ins = [residual]
    n_r, n_b, n_o = len(row_ins), len(bc_ins), len(row_outs)
    lead = [] if after is None else [after]

    def body(*refs):
        refs = refs[len(lead):]
        p_refs = refs[:n_p]
        w_refs = refs[n_p:n_p + n_w]
        extra = [r[...] for r in refs[n_p + n_w:n_p + n_w + n_r + n_b]]
        base = n_p + n_w + n_r + n_b
        pro = [r[...] for r in refs[base:base + n_pr + n_pb]]
        base += n_pr + n_pb
        o_refs = refs[base:base + n_o]
        a_refs = refs[base + n_o:base + n_o + len(acc_outs)]
        if pro_fn is not None:
            lhs = pro_fn(*pro).astype(pro_out.dtype)
            refs[-1][...] = lhs
            tiles = [lhs]
        else:
            tiles = [r[...] for r in p_refs]
        acc = None
        k = 0
        for gi, grp in enumerate(pieces):
            c0 = 0
            for p in grp:
                n = p.shape[1]
                if w_transposed:
                    t = lax.dot_general(tiles[k], w_refs[gi][...], (((1,), (1,)), ((), ())), preferred_element_type=F32)
                else:
                    t = jnp.dot(tiles[k], w_refs[gi][c0:c0 + n, :], preferred_element_type=F32)
                acc = t if acc is None else acc + t
                c0 += n
                k += 1
        if fn is None:
            res = (acc + extra[0] if residual is not None else acc,)
        else:
            res = fn(acc, *extra)
        for o_ref, val in zip(o_refs, res[:n_o]):
            o_ref[...] = val.astype(o_ref.dtype)
        if acc_outs:
            @pl.when(pl.program_id(0) == 0)
            def _():
                for a_ref in a_refs:
                    a_ref[...] = jnp.zeros_like(a_ref)
            for a_ref, val in zip(a_refs, res[n_o:]):
                a_ref[...] += val

    in_specs = [pl.BlockSpec((tm, p.shape[1]), lambda i: (i, 0)) for p in flat]
    in_specs += [_wspec(w) for w in ws]
    in_specs += [pl.BlockSpec((tm, r.shape[1]), lambda i: (i, 0)) for r in row_ins]
    in_specs += [pl.BlockSpec(b.shape, lambda i: (0, 0)) for b in bc_ins]
    in_specs += [pl.BlockSpec((tm, r.shape[1]), lambda i: (i, 0)) for r in pro_rows]
    in_specs += [pl.BlockSpec(b.shape, lambda i: (0, 0)) for b in pro_bc]
    out_specs = [pl.BlockSpec((tm, s.shape[1]), lambda i: (i, 0)) for s in row_outs]
    out_specs += [pl.BlockSpec(s.shape, lambda i: (0, 0)) for s in acc_outs]
    pro_outs = [] if prologue is None else [pro_out]
    out_specs += [pl.BlockSpec((tm, s.shape[1]), lambda i: (i, 0)) for s in pro_outs]
    outs = pl.pallas_call(
        body, name=name, grid=(M // tm,), in_specs=[pl.BlockSpec(memory_space=pl.ANY)] * len(lead) + in_specs,
        out_specs=out_specs, out_shape=list(row_outs) + list(acc_outs) + pro_outs,
        compiler_params=_cp(("arbitrary",) if acc_outs else ("parallel",)),
    )(*lead, *flat, *[w[0] for w in ws], *row_ins, *bc_ins, *pro_rows, *pro_bc)
    return outs if (epilogue is not None or prologue is not None) else outs[0]


def _mm_tn(pieces, x, *, name, out_dtype=BF16, tt=1024):
    M, K = x.shape
    tt = min(tt, M)
    ns = [p.shape[1] for p in pieces]
    offs = [sum(ns[:i]) for i in range(len(ns))]
    N = sum(ns)
    n_p = len(pieces)
    last = M // tt - 1

    def body(*refs):
        p_refs = refs[:n_p]
        x_ref = refs[n_p]
        o_ref, acc_ref = refs[n_p + 1], refs[n_p + 2]

        @pl.when(pl.program_id(0) == 0)
        def _():
            acc_ref[...] = jnp.zeros_like(acc_ref)

        xv = x_ref[...]
        for p_ref, c0, n in zip(p_refs, offs, ns):
            acc_ref[c0:c0 + n, :] += lax.dot_general(p_ref[...], xv, (((0,), (0,)), ((), ())),
                                                      preferred_element_type=F32)

        @pl.when(pl.program_id(0) == last)
        def _():
            o_ref[...] = acc_ref[...].astype(o_ref.dtype)

    in_specs = [pl.BlockSpec((tt, n), lambda i: (i, 0)) for n in ns]
    in_specs.append(pl.BlockSpec((tt, K), lambda i: (i, 0)))
    return pl.pallas_call(
        body, name=name, grid=(M // tt,), in_specs=in_specs,
        out_specs=pl.BlockSpec((N, K), lambda i: (0, 0)),
        out_shape=_sds((N, K), out_dtype),
        scratch_shapes=[pltpu.VMEM((N, K), F32)],
        compiler_params=_cp(("arbitrary",)),
    )(*pieces, x)


def _rms_fwd(xf, w):
    inv = lax.rsqrt(jnp.mean(xf * xf, axis=-1, keepdims=True) + EPS)
    return xf * inv * w


def _rms_bwd(xf, w, dy):
    inv = lax.rsqrt(jnp.mean(xf * xf, axis=-1, keepdims=True) + EPS)
    xhat = xf * inv
    dxhat = dy * w
    dx = inv * (dxhat - xhat * jnp.mean(dxhat * xhat, axis=-1, keepdims=True))
    dw = jnp.sum(dy * xhat, axis=0, keepdims=True)
    return dx, dw


def _sigmoid(x):
    return 1.0 / (1.0 + jnp.exp(-x))


def _rowwise(fn, row_ins, bc_ins, row_outs, acc_outs, *, name, tm=256, after=None):
    M = row_outs[0].shape[0] if row_outs else row_ins[0][0].shape[0]
    assert M % tm == 0 and tm % SUBLANES == 0, (name, M, tm)
    n_r, n_b, n_o, n_a = len(row_ins), len(bc_ins), len(row_outs), len(acc_outs)
    n_after = 0 if after is None else 1

    def body(*refs):
        refs = refs[n_after:]
        ins = [r[...] for r in refs[:n_r + n_b]]
        o_refs = refs[n_r + n_b:n_r + n_b + n_o]
        a_refs = refs[n_r + n_b + n_o:]
        res = fn(*ins)
        for o_ref, val in zip(o_refs, res[:n_o]):
            o_ref[...] = val.astype(o_ref.dtype)
        if n_a:
            @pl.when(pl.program_id(0) == 0)
            def _():
                for a_ref in a_refs:
                    a_ref[...] = jnp.zeros_like(a_ref)
            for a_ref, val in zip(a_refs, res[n_o:]):
                a_ref[...] += val

    in_specs = [pl.BlockSpec((tm, cw), functools.partial(lambda i, cb, r0: (i + r0, cb), cb=cb, r0=r0))
                for (_, cw, cb, r0) in row_ins]
    in_specs += [pl.BlockSpec(b.shape, lambda i: (0, 0)) for b in bc_ins]
    out_specs = [pl.BlockSpec((tm, s.shape[1]), lambda i: (i, 0)) for s in row_outs]
    out_specs += [pl.BlockSpec(s.shape, lambda i: (0, 0)) for s in acc_outs]
    if n_after:
        in_specs = [pl.BlockSpec(memory_space=pl.ANY)] + in_specs
    return pl.pallas_call(
        body, name=name, grid=(M // tm,), in_specs=in_specs, out_specs=out_specs,
        out_shape=list(row_outs) + list(acc_outs),
        compiler_params=_cp(("arbitrary",) if n_a else ("parallel",)),
    )(*([after] if n_after else []), *[r[0] for r in row_ins], *bc_ins)


def _full(a, first_row_block=0):
    return (a, a.shape[1], 0, first_row_block)


def _conv_rows(ext, w_ref_val, lo):
    s1 = pltpu.roll(ext, 1, 0)
    s2 = pltpu.roll(ext, 2, 0)
    y = w_ref_val[0:1, :] * s2 + w_ref_val[1:2, :] * s1 + w_ref_val[2:3, :] * ext
    return y[SUBLANES:, :]


def _ffn_in(v, w_gate, w_up, conv_w8, conv_b, *, name, tm=256):
    T, K = v.shape
    N = w_gate[1]
    tm = min(tm, T)

    def body(v_ref, wg_ref, wu_ref, cw_ref, cb_ref, gp_ref, up_ref, gate_ref, act_ref, carry_sc):
        @pl.when(pl.program_id(0) == 0)
        def _():
            carry_sc[...] = jnp.zeros_like(carry_sc)

        vv = v_ref[...]
        dn = (((1,), (1,)), ((), ()))
        gp = lax.dot_general(vv, wg_ref[...], dn, preferred_element_type=F32)
        up = lax.dot_general(vv, wu_ref[...], dn, preferred_element_type=F32)
        gp_ref[...] = gp.astype(gp_ref.dtype)
        up_ref[...] = up.astype(up_ref.dtype)
        gate = _conv_rows(jnp.concatenate([carry_sc[...], gp], axis=0), cw_ref[...], 0) + cb_ref[...]
        gate_ref[...] = gate
        act_ref[...] = (gate * _sigmoid(gate) * up).astype(act_ref.dtype)
        carry_sc[...] = gp[tm - SUBLANES:, :]

    tile = pl.BlockSpec((tm, N), lambda i: (i, 0))
    return pl.pallas_call(
        body, name=name, grid=(T // tm,),
        in_specs=[pl.BlockSpec((tm, K), lambda i: (i, 0)), _wspec(w_gate), _wspec(w_up),
                  pl.BlockSpec((SUBLANES, N), lambda i: (0, 0)), pl.BlockSpec((1, N), lambda i: (0, 0))],
        out_specs=[tile] * 4,
        out_shape=[_sds((T, N), BF16), _sds((T, N), BF16), _sds((T, N), F32), _sds((T, N), BF16)],
        scratch_shapes=[pltpu.VMEM((SUBLANES, N), F32)],
        compiler_params=_cp(("arbitrary",)),
    )(v, w_gate[0], w_up[0], conv_w8, conv_b)


def _ffn_back(dh2, w_down, gp, up, gate, conv_w8, *, name, tr=256, tc=1408):
    T, C = gp.shape
    K = dh2.shape[1]
    warr, _, wblk = w_down
    tr = min(tr, T)
    nr = T // tr
    ncb = C // tc

    def body(dh_ref, wd_ref, gp_ref, up_ref, gate_ref, w_ref, dgp_ref, dup_ref, dw_ref, db_ref, carry_sc):
        @pl.when(pl.program_id(1) == 0)
        def _():
            carry_sc[...] = jnp.zeros_like(carry_sc)
            dw_ref[...] = jnp.zeros_like(dw_ref)
            db_ref[...] = jnp.zeros_like(db_ref)

        w = w_ref[...]
        dact = lax.dot_general(dh_ref[...], wd_ref[...], (((1,), (1,)), ((), ())), preferred_element_type=F32)
        gpc = gp_ref[...].astype(F32)
        gate = gate_ref[...]
        sg = _sigmoid(gate)
        silu = gate * sg
        dup_ref[...] = (dact * silu).astype(dup_ref.dtype)
        dgate = dact * up_ref[...].astype(F32) * (sg + silu * (1.0 - sg))
        ext = jnp.concatenate([dgate, carry_sc[...]], axis=0)
        n = tr + SUBLANES
        g1 = pltpu.roll(ext, n - 1, 0)[:tr]
        g2 = pltpu.roll(ext, n - 2, 0)[:tr]
        dgp_ref[...] = (w[2:3, :] * dgate + w[1:2, :] * g1 + w[0:1, :] * g2).astype(dgp_ref.dtype)
        dw0 = jnp.sum(gpc * g2, axis=0, keepdims=True)
        dw1 = jnp.sum(gpc * g1, axis=0, keepdims=True)
        dw2 = jnp.sum(gpc * dgate, axis=0, keepdims=True)
        z = jnp.zeros((SUBLANES - 3, gpc.shape[1]), F32)
        dw_ref[...] += jnp.concatenate([dw0, dw1, dw2, z], axis=0)
        db_ref[...] += jnp.sum(dgate, axis=0, keepdims=True)
        carry_sc[...] = dgate[:SUBLANES]

    rev = lambda i: nr - 1 - i
    cur = pl.BlockSpec((tr, tc), lambda j, i: (rev(i), j))
    return pl.pallas_call(
        body, name=name, grid=(ncb, nr),
        in_specs=[pl.BlockSpec((tr, K), lambda j, i: (rev(i), 0)),
                  pl.BlockSpec((tc, K), lambda j, i: (wblk * ncb + j, 0)),
                  cur, cur, cur,
                  pl.BlockSpec((SUBLANES, tc), lambda j, i: (0, j))],
        out_specs=[cur, cur,
                   pl.BlockSpec((SUBLANES, tc), lambda j, i: (0, j)),
                   pl.BlockSpec((1, tc), lambda j, i: (0, j))],
        out_shape=[_sds((T, C), BF16), _sds((T, C), BF16), _sds((SUBLANES, C), F32), _sds((1, C), F32)],
        scratch_shapes=[pltpu.VMEM((SUBLANES, tc), F32)],
        compiler_params=_cp(("parallel", "arbitrary")),
    )(dh2, warr, gp, up, gate, conv_w8)


def _cumsum_rows(x):
    n = x.shape[0]
    row = lax.broadcasted_iota(jnp.int32, x.shape, 0)
    s = 1
    while s < n:
        x = x + jnp.where(row >= s, pltpu.roll(x, s, 0), 0.0)
        s *= 2
    return x


def _rcumsum_rows(x):
    n = x.shape[0]
    row = lax.broadcasted_iota(jnp.int32, x.shape, 0)
    s = 1
    while s < n:
        x = x + jnp.where(row < n - s, pltpu.roll(x, n - s, 0), 0.0)
        s *= 2
    return x


def _dot_nt(a, b):
    return lax.dot_general(a.astype(BF16), b.astype(BF16), (((1,), (1,)), ((), ())), preferred_element_type=F32)


def _dot_tn(a, b):
    return lax.dot_general(a.astype(BF16), b.astype(BF16), (((0,), (0,)), ((), ())), preferred_element_type=F32)


def _dot_nn(a, b):
    return jnp.dot(a.astype(BF16), b.astype(BF16), preferred_element_type=F32)


def _dot3(a, b, contract):
    def split(x):
        hi = x.astype(BF16)
        return hi, (x - hi.astype(F32)).astype(BF16)

    a_hi, a_lo = split(a)
    b_hi, b_lo = split(b)
    dot = lambda x, y: lax.dot_general(x, y, (contract, ((), ())), preferred_element_type=F32)
    return dot(a_hi, b_hi) + (dot(a_hi, b_lo) + dot(a_lo, b_hi))


NT, TN, NN = ((1,), (1,)), ((0,), (0,)), ((1,), (0,))


def _hg_gates(hq, hf, lbv):
    sig = _sigmoid(hf)
    f = lbv + (1.0 - lbv) * sig
    return sig, f, jnp.log(f), 1.0 - f, hq * (HG_DK ** -0.5)


def _hg_sel_rows(ref, sp):
    return jnp.concatenate(
        [jnp.broadcast_to(ref[pl.ds(HG_SUB * i + sp, 1), :], (HG_SUB, HG_DK)) for i in range(HG_CHUNK // HG_SUB)], axis=0)


def _hg_masks():
    C = HG_CHUNK
    row = lax.broadcasted_iota(jnp.int32, (C, C), 0)
    col = lax.broadcasted_iota(jnp.int32, (C, C), 1)
    d = col - (row // HG_SUB) * HG_SUB
    tmod = row % HG_SUB
    diag_valid = jnp.logical_and(d >= 0, d <= tmod)
    return row, col, d, diag_valid


def _hg_scores(q, k, b, b_sc, k_sc):
    C, S = HG_CHUNK, HG_SUB
    row, col, d, diag_valid = _hg_masks()
    blocks = [jnp.zeros((S, C), F32)]
    for i in range(1, C // S):
        r = b_sc[pl.ds(S * i - 1, 1), :]
        qi = q[S * i:S * (i + 1)] * jnp.exp(b[S * i:S * (i + 1)] - r)
        kk = k * jnp.exp(jnp.minimum(r - b, 0.0))
        blocks.append(_dot_nt(qi, kk))
    a_off = jnp.where(col < (row // S) * S, jnp.concatenate(blocks, axis=0), 0.0)
    a_d = jnp.zeros((C, C), F32)
    for sp in range(S):
        bs = _hg_sel_rows(b_sc, sp)
        ks = _hg_sel_rows(k_sc, sp)
        e = jnp.exp(jnp.minimum(b - bs, 0.0))
        colv = jnp.sum(q * ks * e, axis=-1, keepdims=True)
        a_d = jnp.where(d == sp, colv, a_d)
    return a_off + jnp.where(diag_valid, a_d, 0.0)


def _hg_prep(hq_v, hf_v, lbv, b_sc, k_sc):
    sig, f, g, k, q = _hg_gates(hq_v, hf_v, lbv)
    b = _cumsum_rows(g)
    b_sc[...] = b
    k_sc[...] = k
    return sig, f, k, q, b, b_sc[pl.ds(HG_CHUNK - 1, 1), :]


def _hgrn_fwd(hq, hf, hi, lb, *, name):
    T = hq.shape[0]
    C, H, K = HG_CHUNK, HG_HEADS, HG_DK
    NC = T // C

    def body(hq_ref, hf_ref, hi_ref, lb_ref, o_ref, st_ref, s_sc, b_sc, k_sc):
        @pl.when(pl.program_id(0) == 0)
        def _():
            s_sc[...] = jnp.zeros_like(s_sc)

        st_all = s_sc[...]
        for j in range(P):
            rows = slice(C * j, C * (j + 1))
            st_ref[j] = st_all
            outs, news = [], []
            for h in range(H):
                sl = slice(K * h, K * (h + 1))
                _, _, k, q, b, bc = _hg_prep(hq_ref[rows, sl], hf_ref[rows, sl], lb_ref[:, sl], b_sc.at[j, h], k_sc.at[j, h])
                v = hi_ref[rows, sl]
                st0 = st_all[:, sl]
                a = _hg_scores(q, k, b, b_sc.at[j, h], k_sc.at[j, h])
                outs.append(_dot_nn(a, v) + _dot_nt(q * jnp.exp(b), st0))
                news.append(st0 * jnp.exp(bc) + _dot_tn(v, k * jnp.exp(bc - b)))
            o_ref[rows, :] = jnp.concatenate(outs, axis=1)
            st_all = jnp.concatenate(news, axis=1)
        s_sc[...] = st_all

    P = HG_FWD_CHUNKS_PER_STEP
    blk = pl.BlockSpec((P * C, H * K), lambda c: (c, 0))
    return pl.pallas_call(
        body, name=name, grid=(NC // P,),
        in_specs=[blk, blk, blk, pl.BlockSpec((1, H * K), lambda c: (0, 0))],
        out_specs=[blk, pl.BlockSpec((P, K, H * K), lambda c: (c, 0, 0))],
        out_shape=[_sds((T, H * K), F32), _sds((NC, K, H * K), F32)],
        scratch_shapes=[pltpu.VMEM((K, H * K), F32), pltpu.VMEM((P, H, C, K), F32), pltpu.VMEM((P, H, C, K), F32)],
        compiler_params=_cp(("arbitrary",)),
    )(hq, hf, hi, lb)


def _hgrn_bwd(hq, hf, hi, lb, states, do, *, name):
    T = hq.shape[0]
    C, H, K, S = HG_CHUNK, HG_HEADS, HG_DK, HG_SUB
    NC = T // C

    def intra_slow(q, k, b, da, b_sc, k_sc):
        row, col, d, diag_valid = _hg_masks()
        a_blocks = [jnp.zeros((S, C), F32)]
        dq_blocks = [jnp.zeros((S, K), F32)]
        dk = jnp.zeros((C, K), F32)
        for i in range(1, C // S):
            r = b_sc[pl.ds(S * i - 1, 1), :]
            eq = jnp.exp(b[S * i:S * (i + 1)] - r)
            ek = jnp.exp(jnp.minimum(r - b, 0.0))
            qi = q[S * i:S * (i + 1)] * eq
            kk = k * ek
            a_blocks.append(_dot_nt(qi, kk))
            dai = jnp.where(col[S * i:S * (i + 1)] < S * i, da[S * i:S * (i + 1)], 0.0)
            dq_blocks.append(_dot_nn(dai, kk) * eq)
            dk = dk + _dot_tn(dai, qi) * ek
        dq = jnp.concatenate(dq_blocks, axis=0)
        a_off = jnp.where(col < (row // S) * S, jnp.concatenate(a_blocks, axis=0), 0.0)
        same_blk = (row // S == col // S).astype(BF16)
        tmod = (lax.broadcasted_iota(jnp.int32, (C, K), 0)) % S
        a_d = jnp.zeros((C, C), F32)
        dk_d = jnp.zeros((C, K), F32)
        for sp in range(S):
            bs = _hg_sel_rows(b_sc, sp)
            ks = _hg_sel_rows(k_sc, sp)
            e = jnp.exp(jnp.minimum(b - bs, 0.0))
            eks = e * ks
            a_d = jnp.where(d == sp, jnp.sum(q * eks, axis=-1, keepdims=True), a_d)
            dacol = jnp.sum(jnp.where(d == sp, da, 0.0), axis=-1, keepdims=True)
            dq = dq + dacol * eks
            wq = dacol * e * q
            wq_hi = wq.astype(BF16)
            wq_lo = (wq - wq_hi.astype(F32)).astype(BF16)
            blk_sum = (jnp.dot(same_blk, wq_hi, preferred_element_type=F32)
                       + jnp.dot(same_blk, wq_lo, preferred_element_type=F32))
            dk_d = jnp.where(tmod == sp, blk_sum, dk_d)
        return a_off + jnp.where(diag_valid, a_d, 0.0), dq, dk + dk_d

    def one_head(pre, v, lbv, st0, dst1, dout, b_sc, k_sc):
        sig, f, k, q, b, bc = pre
        ebc = jnp.exp(bc)
        eb = jnp.exp(b)
        ekb = jnp.exp(bc - b)
        qt = q * eb
        kb = k * ekb
        row = lax.broadcasted_iota(jnp.int32, (C, C), 0)
        col = lax.broadcasted_iota(jnp.int32, (C, C), 1)
        da = jnp.where(col <= row, _dot_nt(dout, v), 0.0)
        dkb = _dot_nn(v, dst1)
        new_ds = _dot_tn(dout, qt) + dst1 * ebc
        a, dq_i, dk_i = intra_slow(q, k, b, da, b_sc, k_sc)
        dq = _dot_nn(dout, st0) * eb + dq_i
        dk = dkb * ekb + dk_i
        dv = _dot_tn(a, dout) + _dot_nt(kb, dst1)
        extra = jnp.sum(dkb * kb, axis=0, keepdims=True) + ebc * jnp.sum(st0 * dst1, axis=0, keepdims=True)
        rowk = lax.broadcasted_iota(jnp.int32, (C, K), 0)
        db = q * dq - k * dk + jnp.where(rowk == C - 1, extra, 0.0)
        dg = _rcumsum_rows(db)
        df = dg / f - dk
        return (dq * (K ** -0.5), df * (1.0 - lbv) * sig * (1.0 - sig), dv,
                jnp.sum(df * (1.0 - sig), axis=0, keepdims=True), new_ds)

    def body(hq_ref, hf_ref, hi_ref, lb_ref, st_ref, do_ref, dq_ref, dhf_ref, dv_ref, dlb_ref, ds_sc, b_sc, k_sc):
        @pl.when(pl.program_id(0) == 0)
        def _():
            ds_sc[...] = jnp.zeros_like(ds_sc)
            dlb_ref[...] = jnp.zeros_like(dlb_ref)

        ds_all = ds_sc[...]
        dlb = jnp.zeros((1, H * K), F32)
        for j in reversed(range(P)):
            rows = slice(C * j, C * (j + 1))
            st_all = st_ref[j]
            res = []
            for h in range(H):
                sl = slice(K * h, K * (h + 1))
                pre = _hg_prep(hq_ref[rows, sl], hf_ref[rows, sl], lb_ref[:, sl], b_sc.at[j, h], k_sc.at[j, h])
                res.append(one_head(pre, hi_ref[rows, sl], lb_ref[:, sl], st_all[:, sl], ds_all[:, sl], do_ref[rows, sl],
                                    b_sc.at[j, h], k_sc.at[j, h]))
            cat = lambda i: jnp.concatenate([r[i] for r in res], axis=1)
            dq_ref[rows, :] = cat(0).astype(dq_ref.dtype)
            dhf_ref[rows, :] = cat(1).astype(dhf_ref.dtype)
            dv_ref[rows, :] = cat(2).astype(dv_ref.dtype)
            dlb = dlb + cat(3)
            ds_all = cat(4)
        dlb_ref[...] += dlb
        ds_sc[...] = ds_all

    P = HG_CHUNKS_PER_STEP
    NS = NC // P
    blk = pl.BlockSpec((P * C, H * K), lambda c: (NS - 1 - c, 0))
    par = pl.BlockSpec((1, H * K), lambda c: (0, 0))
    return pl.pallas_call(
        body, name=name, grid=(NS,),
        in_specs=[blk, blk, blk, par, pl.BlockSpec((P, K, H * K), lambda c: (NS - 1 - c, 0, 0)), blk],
        out_specs=[blk, blk, blk, par],
        out_shape=[_sds((T, H * K), BF16)] * 3 + [_sds((1, H * K), F32)],
        scratch_shapes=[pltpu.VMEM((K, H * K), F32), pltpu.VMEM((P, H, C, K), F32), pltpu.VMEM((P, H, C, K), F32)],
        compiler_params=_cp(("arbitrary",)),
    )(hq, hf, hi, lb, states, do)


ATT_STACK = ATT_GROUP


def _att_valid(n):
    R, B = ATT_STACK * ATT_BLOCK, ATT_BLOCK
    j = lax.broadcasted_iota(jnp.int32, (2 * B, R), 0)
    t = lax.broadcasted_iota(jnp.int32, (2 * B, R), 1) % B
    dist = t + B - j
    first_key = jnp.where(n > 0, 0, B)
    return jnp.logical_and(jnp.logical_and(dist >= 0, dist < B), j >= first_key)


def _att_load(cur_ref, prev_ref, ba_ref, h0):
    hd = ATT_HD
    kv = h0 // ATT_GROUP
    def cols(ref, c0):
        return ref[:, c0:c0 + hd] + ba_ref[:, c0:c0 + hd]
    qs = jnp.concatenate([cols(cur_ref, hd * (h0 + g)) for g in range(ATT_STACK)], axis=0)
    kc = jnp.concatenate([cols(prev_ref, ATT_Q_W + hd * kv), cols(cur_ref, ATT_Q_W + hd * kv)], axis=0)
    vc = jnp.concatenate([cols(prev_ref, ATT_Q_W + ATT_KV_W + hd * kv), cols(cur_ref, ATT_Q_W + ATT_KV_W + hd * kv)], axis=0)
    return qs, kc, vc


def _att_probs(qs, kc, valid, sink_ref, h0):
    scale = 1.0 / math.sqrt(ATT_HD)
    s = jnp.where(valid, _dot_nt(kc, qs) * scale, NEG)
    sink = jnp.concatenate([jnp.full((1, ATT_BLOCK), sink_ref[0, h0 + g], F32) for g in range(ATT_STACK)], axis=1)
    m = jnp.maximum(jnp.max(s, axis=0, keepdims=True), sink)
    p = jnp.exp(s - m)
    ps = jnp.exp(sink - m)
    inv = 1.0 / (jnp.sum(p, axis=0, keepdims=True) + ps)
    return p * inv, ps * inv


def _attn_fwd(att, b_attn, sinks, *, name, after=None):
    T = att.shape[0]
    B = ATT_BLOCK
    NB = T // B
    lead = [] if after is None else [after]

    def body(*refs):
        sink_ref, cur_ref, prev_ref, ba_ref, o_ref = refs[len(lead):]
        valid = _att_valid(pl.program_id(0))
        outs = []
        for h0 in range(0, ATT_HEADS, ATT_STACK):
            qs, kc, vc = _att_load(cur_ref, prev_ref, ba_ref, h0)
            prob, _ = _att_probs(qs, kc, valid, sink_ref, h0)
            o = _dot_tn(prob, vc)
            outs += [o[B * g:B * (g + 1)] for g in range(ATT_STACK)]
        o_ref[...] = jnp.concatenate(outs, axis=1)

    return pl.pallas_call(
        body, name=name, grid=(NB,),
        in_specs=[pl.BlockSpec(memory_space=pl.ANY)] * len(lead) + [
            pl.BlockSpec(memory_space=pltpu.SMEM),
            pl.BlockSpec((B, ATT_COLS), lambda n: (n, 0)),
            pl.BlockSpec((B, ATT_COLS), lambda n: (jnp.maximum(n - 1, 0), 0)),
            pl.BlockSpec((1, ATT_COLS), lambda n: (0, 0))],
        out_specs=pl.BlockSpec((B, ATT_Q_W), lambda n: (n, 0)),
        out_shape=_sds((T, ATT_Q_W), F32),
        compiler_params=_cp(("parallel",)),
    )(*lead, sinks, att, att, b_attn)


def _attn_bwd(att, b_attn, sinks, dmix, *, name):
    T = att.shape[0]
    B, hd = ATT_BLOCK, ATT_HD
    NB = T // B
    scale = 1.0 / math.sqrt(hd)

    def body(sink_ref, cur_ref, prev_ref, ba_ref, do_ref, daq_ref, dakv_ref, dsink_ref, dbq_ref, dbkv_ref, carry_sc):
        n = pl.program_id(0)

        @pl.when(n == 0)
        def _():
            carry_sc[...] = jnp.zeros_like(carry_sc)
            dsink_ref[...] = jnp.zeros_like(dsink_ref)
            dbq_ref[...] = jnp.zeros_like(dbq_ref)
            dbkv_ref[...] = jnp.zeros_like(dbkv_ref)

        @pl.when(n < NB)
        def _():
            valid = _att_valid(n)
            hrow = lax.broadcasted_iota(jnp.int32, (SUBLANES, 128), 0)
            dsink = jnp.zeros((SUBLANES, 128), F32)
            dqs = []
            dks = [jnp.zeros((2 * B, hd), F32)] * ATT_KV
            dvs = [jnp.zeros((2 * B, hd), F32)] * ATT_KV
            for h0 in range(0, ATT_HEADS, ATT_STACK):
                kv = h0 // ATT_GROUP
                qs, kc, vc = _att_load(cur_ref, prev_ref, ba_ref, h0)
                prob, psink = _att_probs(qs, kc, valid, sink_ref, h0)
                dout = jnp.concatenate([do_ref[:, hd * (h0 + g):hd * (h0 + g + 1)] for g in range(ATT_STACK)], axis=0)
                dp = _dot_nt(vc, dout)
                delta = jnp.sum(prob * dp, axis=0, keepdims=True)
                dsc = prob * (dp - delta) * scale
                dq = _dot_tn(dsc, kc)
                dks[kv] = dks[kv] + _dot_nn(dsc, qs)
                dvs[kv] = dvs[kv] + _dot_nn(prob, dout)
                dsk = psink * delta
                for g in range(ATT_STACK):
                    dqs.append(dq[B * g:B * (g + 1)])
                    tot = jnp.sum(dsk[:, B * g:B * (g + 1)], axis=1, keepdims=True)
                    dsink = dsink - jnp.where(hrow == h0 + g, tot, 0.0)
            daq = jnp.concatenate(dqs, axis=1).astype(daq_ref.dtype)
            daq_ref[...] = daq
            dsink_ref[...] += dsink
            dbq_ref[...] += jnp.sum(daq.astype(F32), axis=0, keepdims=True)
            done = carry_sc[...] + jnp.concatenate([d[:B] for d in dks + dvs], axis=1)
            dakv_ref[...] = done.astype(dakv_ref.dtype)
            dbkv_ref[...] += jnp.sum(done.astype(dakv_ref.dtype).astype(F32), axis=0, keepdims=True)
            carry_sc[...] = jnp.concatenate([d[B:] for d in dks + dvs], axis=1)

        @pl.when(n == NB)
        def _():
            done = carry_sc[...]
            dakv_ref[...] = done.astype(dakv_ref.dtype)
            dbkv_ref[...] += jnp.sum(done.astype(dakv_ref.dtype).astype(F32), axis=0, keepdims=True)

    cl = lambda n: jnp.minimum(n, NB - 1)
    return pl.pallas_call(
        body, name=name, grid=(NB + 1,),
        in_specs=[pl.BlockSpec(memory_space=pltpu.SMEM),
                  pl.BlockSpec((B, ATT_COLS), lambda n: (cl(n), 0)),
                  pl.BlockSpec((B, ATT_COLS), lambda n: (jnp.maximum(cl(n) - 1, 0), 0)),
                  pl.BlockSpec((1, ATT_COLS), lambda n: (0, 0)),
                  pl.BlockSpec((B, ATT_Q_W), lambda n: (cl(n), 0))],
        out_specs=[pl.BlockSpec((B, ATT_Q_W), lambda n: (cl(n), 0)),
                   pl.BlockSpec((B, 2 * ATT_KV_W), lambda n: (jnp.maximum(n - 1, 0), 0)),
                   pl.BlockSpec((SUBLANES, 128), lambda n: (0, 0)),
                   pl.BlockSpec((1, ATT_Q_W), lambda n: (0, 0)),
                   pl.BlockSpec((1, 2 * ATT_KV_W), lambda n: (0, 0))],
        out_shape=[_sds((T, ATT_Q_W), BF16), _sds((T, 2 * ATT_KV_W), BF16), _sds((SUBLANES, 128), F32),
                   _sds((1, ATT_Q_W), F32), _sds((1, 2 * ATT_KV_W), F32)],
        scratch_shapes=[pltpu.VMEM((B, 2 * ATT_KV_W), F32)],
        compiler_params=_cp(("arbitrary",)),
    )(sinks, att, att, b_attn, dmix)


def _silu_and_grad(x):
    sg = _sigmoid(x)
    return x * sg, sg * (1.0 + x * (1.0 - sg))


def _mix_fwd_fn(o_raw, hg, o_att, hgw):
    outs = []
    for h in range(HG_HEADS):
        sl = slice(HG_DK * h, HG_DK * (h + 1))
        silu, _ = _silu_and_grad(hg[:, sl])
        outs.append(_rms_fwd(o_raw[:, sl], hgw) * silu)
    outs.append(o_att)
    return (jnp.concatenate(outs, axis=1),)


def _mix_bwd_fn(o_raw, hg, dmix, hgw):
    dos, dhgs = [], []
    dw = jnp.zeros((1, HG_DK), F32)
    for h in range(HG_HEADS):
        sl = slice(HG_DK * h, HG_DK * (h + 1))
        silu, dsilu = _silu_and_grad(hg[:, sl])
        dy = dmix[:, sl]
        dhgs.append(dy * _rms_fwd(o_raw[:, sl], hgw) * dsilu)
        dx, dwh = _rms_bwd(o_raw[:, sl], hgw, dy * silu)
        dos.append(dx)
        dw = dw + dwh
    return jnp.concatenate(dos, axis=1), jnp.concatenate(dhgs, axis=1), dw


def _final_fn(h2, tgt, wf):
    d = h2.shape[1]
    err = _rms_fwd(h2, wf) - tgt
    loss_cols = (0.5 / d) * jnp.sum(err * err, axis=0, keepdims=True)
    dh2, dwf = _rms_bwd(h2, wf, err * (1.0 / d))
    return dh2, dh2, loss_cols, dwf


class _NoExchange:
    def __init__(self, weights):
        self.weights = weights

    def start(self):
        return None

    def w_in(self, after):
        return self.weights["w_in_t"]

    def mid(self, after):
        return None

    def rest(self, after):
        return self.weights

    def ffn_grads(self, gs):
        return None

    def ffn_grads_send(self, after):
        return None


def _local_step(x, tgt, p, ex):
    T, D = x.shape
    row = lambda n, dt: _sds((T, n), dt)
    acc = lambda n: _sds((1, n), F32)

    (u,) = _rowwise(lambda xv, w: (_rms_fwd(xv, w),), [_full(x)], [p["norm_mix_w"]], [row(D, BF16)], [], name="rms_mix",
                    after=ex.start())
    p = dict(p, w_in_t=ex.w_in(u))
    hq, hf, hi, hg, att = _mm_nt(u, p["w_in_t"], splits=[HG_W] * 4 + [ATT_COLS], out_dtype=F32, name="in_proj")
    o_raw, states = _hgrn_fwd(hq, hf, hi, p["lb"], name="hgrn_fwd")
    o_att = _attn_fwd(att, p["b_attn"], p["sinks"], name="attn_fwd", after=ex.mid(o_raw))
    p = dict(p, **ex.rest(o_att))
    def out_epilogue(prod, xv, w):
        h1v = prod + xv
        return h1v, _rms_fwd(h1v, w)

    h1, v, mix = _mm_nn(None, [p["w_out"]], name="mix_out_proj",
                        prologue=(lambda *a: _mix_fwd_fn(*a)[0], [o_raw, hg, o_att], [p["hg_norm_w"]], row(D, BF16)),
                        epilogue=(out_epilogue, [x], [p["norm_ffn_w"]], [row(D, F32), row(D, BF16)], []))
    gp, up, gate, act = _ffn_in(v, p["w_gate_t"], p["w_up_t"], p["conv_w8"], p["conv_b"], name="ffn_in")
    def down_epilogue(prod, h1v, tgtv, wf):
        return _final_fn(prod + h1v, tgtv, wf)

    dh2, dh2_b, loss_cols, d_final = _mm_nn(
        [[act]], [p["w_down"]], name="down_proj_loss",
        epilogue=(down_epilogue, [h1, tgt], [p["final_norm_w"]], [row(D, F32), row(D, BF16)], [acc(D), acc(D)]))

    g_down = _mm_tn([act], dh2_b, name="g_down")
    dgp, dup, d_conv_w8, d_conv_b = _ffn_back(dh2_b, p["w_down"], gp, up, gate, p["conv_w8"], name="ffn_back")
    g_gate_t = _mm_tn([dgp], v, name="g_gate")
    g_up_t = _mm_tn([dup], v, name="g_up")
    swapping = ex.ffn_grads([g_gate_t, g_up_t, g_down])

    def ffn_norm_bwd(dvv, hv, dh2v, w):
        dx, dw = _rms_bwd(hv, w, dvv)
        dh1v = dx + dh2v
        return dh1v, dh1v, dw

    dh1, dh1_b, d_norm_ffn = _mm_nn(
        [[dgp], [dup]], [p["w_gate_t"], p["w_up_t"]], name="d_v_norm", after=swapping,
        epilogue=(ffn_norm_bwd, [h1, dh2], [p["norm_ffn_w"]], [row(D, F32), row(D, BF16)], [acc(D)]))
    sent = ex.ffn_grads_send(dh1_b)
    def mix_bwd(dmixv, o_rawv, hgv, hgw):
        do_rawv, dhgv, dw = _mix_bwd_fn(o_rawv, hgv, dmixv[:, :HG_W], hgw)
        return do_rawv, dhgv, dmixv[:, HG_W:], dw

    do_raw, dhg, do_att, d_hg_norm = _mm_nn(
        [[dh1_b]], [p["w_out"]], name="d_mix_bwd", w_transposed=True, after=sent,
        epilogue=(mix_bwd, [o_raw, hg], [p["hg_norm_w"]], [row(HG_W, F32), row(HG_W, BF16), row(ATT_Q_W, F32)], [acc(HG_DK)]))
    g_out = _mm_tn([mix], dh1_b, name="g_out")
    daq, dakv, d_sinks8, d_bq, d_bkv = _attn_bwd(att, p["b_attn"], p["sinks"], do_att, name="attn_bwd")
    dhq, dhf, dhi, d_lb = _hgrn_bwd(hq, hf, hi, p["lb"], states, do_raw, name="hgrn_bwd")
    pieces = [dhq, dhf, dhi, dhg, daq, dakv]
    g_in_t = _mm_tn(pieces, u, name="g_in")

    def mix_norm_bwd(duv, xv, dh1v, w):
        dx, dw = _rms_bwd(xv, w, duv)
        return dx + dh1v, dw

    dx, d_norm_mix = _mm_nn([pieces], [p["w_in_t"]], name="d_u_norm",
                            epilogue=(mix_norm_bwd, [x, dh1], [p["norm_mix_w"]], [row(D, F32)], [acc(D)]))
    grads = dict(g_in_t=g_in_t, g_out=g_out, g_gate_t=g_gate_t, g_up_t=g_up_t, g_down=g_down,
                 norm_mix_w=d_norm_mix, b_attn=jnp.concatenate([d_bq, d_bkv], axis=1), lb=d_lb, hg_norm_w=d_hg_norm,
                 sinks8=d_sinks8, norm_ffn_w=d_norm_ffn, conv_w8=d_conv_w8, conv_b=d_conv_b, final_norm_w=d_final)
    return loss_cols, dx, grads


SLAB = (IN_COLS // N_CHIPS, D_FF // N_CHIPS, D_FF // N_CHIPS, D_FF // N_CHIPS, D_MODEL // N_CHIPS)
N_W = len(SLAB)
PACK_OFF = tuple(sum(SLAB[:i]) for i in range(N_W))
PACK_ROWS = sum(SLAB)
FULL_OFF = tuple(N_CHIPS * o for o in PACK_OFF)
FULL_ROWS = N_CHIPS * PACK_ROWS
HALF = tuple(s // 2 for s in SLAB)
HPACK_OFF = tuple(sum(HALF[:i]) for i in range(N_W))
HPACK_ROWS = sum(HALF)
HFULL_OFF = tuple(N_CHIPS * o for o in HPACK_OFF)
HFULL_ROWS = N_CHIPS * HPACK_ROWS
CHIP_FLIPS = ((1, 0), (0, 1), (1, 1))
N_DEV = 8
BF16_ROWS = 16
ANY = pl.BlockSpec(memory_space=pl.ANY)


def _pos():
    return lax.axis_index("x"), lax.axis_index("y"), lax.axis_index("c")


def _flip(v, f):
    return 1 - v if f else v


def _rcopy(src, dst, ssem, rsem, dev):
    return pltpu.make_async_remote_copy(src_ref=src, dst_ref=dst, send_sem=ssem, recv_sem=rsem, device_id=dev,
                                        device_id_type=pl.DeviceIdType.MESH)


def _rows(ref, start, n, align=None):
    if not isinstance(start, int):
        if align is None:
            align = SUBLANES * (4 // jnp.dtype(ref.dtype).itemsize)
        start = pl.multiple_of(start, align)
    return ref.at[pl.ds(start, n), :]


FFN_W = (1, 2, 3)
N_PEER = 1 + len(CHIP_FLIPS)
HBM = pl.BlockSpec(memory_space=pltpu.HBM)
SEM = pl.BlockSpec(memory_space=pltpu.SEMAPHORE)
EFFECT = pltpu.SideEffectType.DATAFLOW_SIDE_EFFECTING
LANES = 128


def _sent_rows(k, w, c):
    return (0, SLAB[w]) if k == 0 else (c * HALF[w], HALF[w])


def _gather_start(pack, cw8):
    D = pack.shape[1]
    lands = [lax.empty((N_CHIPS * SLAB[0], D), pack.dtype), lax.empty((3 * N_CHIPS * SLAB[1], D), pack.dtype),
             lax.empty((N_CHIPS * SLAB[4], D), pack.dtype), lax.empty((N_CHIPS,) + cw8.shape, cw8.dtype)]
    bufs = [pack, cw8] + lands

    def body(pack_ref, cw_ref, l_in, l_ffn, l_out, l_cw, *rest):
        in_send, in_recv, out_send, out_recv, ffn_send, ffn_recv = rest[:6]
        token = rest[-1]
        x, y, c = _pos()
        q = 2 * x + y
        peers = _gather_peers(x, y, c)

        def send(k, peer, w, land, base, ssem, rsem):
            r0, n = _sent_rows(k, w, c)
            _rcopy(_rows(pack_ref, PACK_OFF[w] + r0, n), _rows(land, base + q * SLAB[w] + r0, n), ssem, rsem, peer).start()

        for k, peer in enumerate(peers):
            send(k, peer, 0, l_in, 0, in_send.at[k], in_recv.at[k])
        for k, peer in enumerate(peers):
            send(k, peer, 4, l_out, 0, out_send.at[k], out_recv.at[k])
            _rcopy(cw_ref, l_cw.at[q], out_send.at[N_PEER + k], out_recv.at[N_PEER + k], peer).start()
        for j, w in enumerate(FFN_W):
            for k, peer in enumerate(peers):
                send(k, peer, w, l_ffn, j * N_CHIPS * SLAB[w], ffn_send.at[k], ffn_recv.at[k])
        token[...] = jnp.zeros_like(token)

    n_sem = (N_PEER, N_PEER, 2 * N_PEER, 2 * N_PEER, N_PEER, N_PEER)
    outs = pl.pallas_call(
        body, name="gather_start", in_specs=[HBM] * len(bufs),
        out_specs=[SEM] * len(n_sem) + [HBM] * len(bufs) + [pl.BlockSpec(memory_space=pltpu.VMEM)],
        out_shape=[pltpu.SemaphoreType.DMA((n,)) for n in n_sem]
        + [pltpu.HBM(b.shape, b.dtype) for b in bufs] + [TOKEN],
        input_output_aliases={i: len(n_sem) + i for i in range(len(bufs))},
        compiler_params=pltpu.CompilerParams(has_side_effects=EFFECT),
    )(*[pltpu.with_memory_space_constraint(b, pltpu.HBM) for b in bufs])
    bufs_out = outs[len(n_sem):]
    return dict(in_sems=outs[0:2], out_sems=outs[2:4], ffn_sems=outs[4:6], pack=bufs_out[0], cw=bufs_out[1], l_in=bufs_out[2],
                l_ffn=bufs_out[3], l_out=bufs_out[4], l_cw=bufs_out[5], token=bufs_out[6])


def _gather_peers(x, y, c):
    return [(x, y, 1 - c)] + [(_flip(x, fx), _flip(y, fy), c) for fx, fy in CHIP_FLIPS]


def _gather_wait_in(g, after):
    def body(pack_ref, l_in, send, recv, after_ref, pack_out, l_out):
        for k, peer in enumerate(_gather_peers(*_pos())):
            n = _sent_rows(k, 0, 0)[1]
            cp = _rcopy(_rows(pack_ref, PACK_OFF[0], n), _rows(l_in, 0, n), send.at[k], recv.at[k], peer)
            cp.wait_send()
            cp.wait_recv()

    return pl.pallas_call(
        body, name="gather_wait_in", in_specs=[HBM, HBM, SEM, SEM, ANY], out_specs=[HBM, HBM],
        out_shape=[pltpu.HBM(g["pack"].shape, g["pack"].dtype), pltpu.HBM(g["l_in"].shape, g["l_in"].dtype)],
        input_output_aliases={0: 0, 1: 1}, compiler_params=pltpu.CompilerParams(has_side_effects=EFFECT),
    )(g["pack"], g["l_in"], *g["in_sems"], after)


def _gather_wait_rest(g, pack, after):
    def body(pack_ref, cw_ref, l_ffn, l_out, l_cw, o_send, o_recv, f_send, f_recv, after_ref, o_ffn, o_out, o_cw):
        for k, peer in enumerate(_gather_peers(*_pos())):
            n_out = _sent_rows(k, 4, 0)[1]
            n_ffn = len(FFN_W) * _sent_rows(k, FFN_W[0], 0)[1]
            for cp in (_rcopy(_rows(pack_ref, PACK_OFF[4], n_out), _rows(l_out, 0, n_out), o_send.at[k], o_recv.at[k], peer),
                       _rcopy(cw_ref, l_cw.at[0], o_send.at[N_PEER + k], o_recv.at[N_PEER + k], peer),
                       _rcopy(_rows(pack_ref, PACK_OFF[FFN_W[0]], n_ffn), _rows(l_ffn, 0, n_ffn), f_send.at[k], f_recv.at[k], peer)):
                cp.wait_send()
                cp.wait_recv()

    ins = [pack, g["cw"], g["l_ffn"], g["l_out"], g["l_cw"]]
    return pl.pallas_call(
        body, name="gather_wait_rest", in_specs=[HBM] * 5 + [SEM] * 4 + [ANY], out_specs=[HBM] * 3,
        out_shape=[pltpu.HBM(b.shape, b.dtype) for b in ins[2:]],
        input_output_aliases={2: 0, 3: 1, 4: 2}, compiler_params=pltpu.CompilerParams(has_side_effects=EFFECT),
    )(*ins, *g["out_sems"], *g["ffn_sems"], after)


FWD_IN = ((0, 0, 0),)
FWD_REST = tuple((0, w, j * N_CHIPS * SLAB[w]) for j, w in enumerate(FFN_W)) + ((1, 4, 0),)


def _forward_copies(layout, src, dst, send_sems, recv_sems):
    x, y, c = _pos()
    sib = (x, y, 1 - c)
    cps = []
    for fx, fy in CHIP_FLIPS:
        qa = 2 * _flip(x, fx) + _flip(y, fy)
        for bi, w, base in layout:
            r0 = base + qa * SLAB[w] + c * HALF[w]
            cps.append(_rcopy(_rows(src[bi], r0, HALF[w]), _rows(dst[bi], r0, HALF[w]),
                              send_sems.at[len(cps)], recv_sems.at[len(cps)], sib))
    return cps


def _forward_in(l_in):
    n = len(CHIP_FLIPS) * len(FWD_IN)

    def body(in_ref, out_ref, send_sems, recv_sems):
        cps = _forward_copies(FWD_IN, [in_ref], [out_ref], send_sems, recv_sems)
        for cp in cps:
            cp.start()
        for cp in cps:
            cp.wait_recv()
        for cp in cps:
            cp.wait_send()

    return pl.pallas_call(
        body, name="forward_in", in_specs=[ANY], out_specs=ANY, out_shape=_sds(l_in.shape, l_in.dtype),
        input_output_aliases={0: 0},
        scratch_shapes=[pltpu.SemaphoreType.DMA((n,)), pltpu.SemaphoreType.DMA((n,))],
    )(l_in)


def _forward_rest_start(l_ffn, l_out):
    n = len(CHIP_FLIPS) * len(FWD_REST)
    bufs = [l_ffn, l_out]

    def body(a_ref, b_ref, send_sems, recv_sems, a_out, b_out, token):
        for cp in _forward_copies(FWD_REST, [a_ref, b_ref], [a_ref, b_ref], send_sems, recv_sems):
            cp.start()
        token[...] = jnp.zeros_like(token)

    outs = pl.pallas_call(
        body, name="forward_rest_start", in_specs=[HBM] * 2,
        out_specs=[SEM, SEM, HBM, HBM, pl.BlockSpec(memory_space=pltpu.VMEM)],
        out_shape=[pltpu.SemaphoreType.DMA((n,)), pltpu.SemaphoreType.DMA((n,))]
        + [pltpu.HBM(b.shape, b.dtype) for b in bufs] + [TOKEN],
        input_output_aliases={0: 2, 1: 3}, compiler_params=pltpu.CompilerParams(has_side_effects=EFFECT),
    )(*[pltpu.with_memory_space_constraint(b, pltpu.HBM) for b in bufs])
    return dict(sems=outs[0:2], bufs=outs[2:4], token=outs[4])


def _forward_rest_wait(s, after):
    def body(a_ref, b_ref, send_sems, recv_sems, after_ref, a_out, b_out):
        for cp in _forward_copies(FWD_REST, [a_ref, b_ref], [a_ref, b_ref], send_sems, recv_sems):
            cp.wait_send()
            cp.wait_recv()

    return pl.pallas_call(
        body, name="forward_rest_wait", in_specs=[HBM, HBM, SEM, SEM, ANY], out_specs=[HBM, HBM],
        out_shape=[pltpu.HBM(b.shape, b.dtype) for b in s["bufs"]],
        input_output_aliases={0: 0, 1: 1}, compiler_params=pltpu.CompilerParams(has_side_effects=EFFECT),
    )(*s["bufs"], *s["sems"], after)


def _exchange_halves(ws, gs, small, *, name):
    D = gs[0].shape[1]
    n = len(ws)
    has_small = small is not None

    def body(*refs):
        g = refs[:n]
        t = refs[n + has_small:2 * n + has_small]
        sems = refs[2 * n + 2 * has_small:]
        d2d_send, d2d_recv = sems[0], sems[1]
        x, y, c = _pos()
        sib = (x, y, 1 - c)
        drains = []
        for i, w in enumerate(ws):
            h = HALF[w]
            for qq in range(N_CHIPS):
                _rcopy(_rows(g[i], qq * SLAB[w] + (1 - c) * h, h), _rows(t[i], qq * h, h),
                       d2d_send.at[i], d2d_recv.at[i], sib).start()
            drains.append(_rcopy(t[i], t[i], d2d_send.at[i], d2d_recv.at[i], sib))
        if has_small:
            small_ref, sall_ref = refs[n], refs[2 * n + 1]
            sm_send, sm_recv, loc_sem = sems[2], sems[3], sems[4]
            me = 4 * x + 2 * y + c
            own_small = pltpu.make_async_copy(small_ref, sall_ref.at[me], loc_sem)
            own_small.start()
            for f in range(1, N_DEV):
                peer = (_flip(x, f & 4), _flip(y, f & 2), _flip(c, f & 1))
                cp = _rcopy(small_ref, sall_ref.at[me], sm_send.at[f - 1], sm_recv.at[f - 1], peer)
                cp.start()
                drains.append(cp)
        for d in drains:
            d.wait_recv()
        for d in drains:
            d.wait_send()
        if has_small:
            own_small.wait()

    out_shape = [_sds((N_CHIPS * HALF[w], D), gs[0].dtype) for w in ws]
    scratch = [pltpu.SemaphoreType.DMA((n,)), pltpu.SemaphoreType.DMA((n,))]
    if has_small:
        out_shape.append(_sds((N_DEV,) + small.shape, F32))
        scratch += [pltpu.SemaphoreType.DMA((N_DEV - 1,)), pltpu.SemaphoreType.DMA((N_DEV - 1,)), pltpu.SemaphoreType.DMA]
    return pl.pallas_call(
        body, name=name, in_specs=[ANY] * (n + has_small), out_specs=[ANY] * (n + has_small),
        out_shape=out_shape, scratch_shapes=scratch,
    )(*gs, *([small] if has_small else []))


def _halves_copies(ws, g, t, send_sems, recv_sems):
    x, y, c = _pos()
    sib = (x, y, 1 - c)
    cps = []
    for i, w in enumerate(ws):
        h = HALF[w]
        for qq in range(N_CHIPS):
            cps.append(_rcopy(_rows(g[i], qq * SLAB[w] + (1 - c) * h, h), _rows(t[i], qq * h, h),
                              send_sems.at[N_CHIPS * i + qq], recv_sems.at[N_CHIPS * i + qq], sib))
    return cps


def _halves_start(ws, gs, *, name):
    D = gs[0].shape[1]
    n = len(ws)
    bufs = list(gs) + [lax.empty((N_CHIPS * HALF[w], D), gs[0].dtype) for w in ws]

    def body(*refs):
        for cp in _halves_copies(ws, refs[:n], refs[n:2 * n], refs[2 * n], refs[2 * n + 1]):
            cp.start()
        refs[-1][...] = jnp.zeros_like(refs[-1])

    outs = pl.pallas_call(
        body, name=name, in_specs=[HBM] * (2 * n),
        out_specs=[SEM, SEM] + [HBM] * (2 * n) + [pl.BlockSpec(memory_space=pltpu.VMEM)],
        out_shape=[pltpu.SemaphoreType.DMA((N_CHIPS * n,)), pltpu.SemaphoreType.DMA((N_CHIPS * n,))]
        + [pltpu.HBM(b.shape, b.dtype) for b in bufs] + [TOKEN],
        input_output_aliases={i: 2 + i for i in range(2 * n)},
        compiler_params=pltpu.CompilerParams(has_side_effects=EFFECT),
    )(*[pltpu.with_memory_space_constraint(b, pltpu.HBM) for b in bufs])
    return dict(sems=outs[0:2], gs=outs[2:2 + n], theirs=outs[2 + n:2 + 2 * n], token=outs[-1])


def _halves_wait(ws, s, after, *, name):
    n = len(ws)

    def body(*refs):
        for cp in _halves_copies(ws, refs[:n], refs[n:2 * n], refs[2 * n], refs[2 * n + 1]):
            cp.wait_send()
            cp.wait_recv()

    bufs = list(s["gs"]) + list(s["theirs"])
    outs = pl.pallas_call(
        body, name=name, in_specs=[HBM] * (2 * n) + [SEM, SEM, ANY], out_specs=[HBM] * (2 * n),
        out_shape=[pltpu.HBM(b.shape, b.dtype) for b in bufs],
        input_output_aliases={i: i for i in range(2 * n)},
        compiler_params=pltpu.CompilerParams(has_side_effects=EFFECT),
    )(*bufs, *s["sems"], after)
    return outs[:n], outs[n:]


REDUCE_SPLIT = 2


def _chip_partial(ws, gs, theirs, *, name, out_dtype=F32):
    D = gs[0].shape[1]
    n = len(ws)

    def body(*refs):
        for i in range(n):
            refs[2 * n + i][...] = (refs[i][...].astype(F32) + refs[n + i][...].astype(F32)).astype(out_dtype)

    blk = [HALF[w] // REDUCE_SPLIT for w in ws]
    mine = [pl.BlockSpec((b, D), lambda qq, j: ((2 * qq + lax.axis_index("c")) * REDUCE_SPLIT + j, 0)) for b in blk]
    flat = [pl.BlockSpec((b, D), lambda qq, j: (qq * REDUCE_SPLIT + j, 0)) for b in blk]
    return pl.pallas_call(
        body, name=name, grid=(N_CHIPS, REDUCE_SPLIT), in_specs=mine + flat, out_specs=flat,
        out_shape=[_sds((N_CHIPS * HALF[w], D), out_dtype) for w in ws],
        compiler_params=_cp(("parallel", "parallel")),
    )(*gs, *theirs)


def _partial_copies(ws, part, got, send_sems, recv_sems):
    x, y, c = _pos()
    cps = []
    for k, (fx, fy) in enumerate(CHIP_FLIPS):
        peer = (_flip(x, fx), _flip(y, fy), c)
        qp = 2 * _flip(x, fx) + _flip(y, fy)
        for i, w in enumerate(ws):
            cps.append(_rcopy(_rows(part[i], qp * HALF[w], HALF[w]), _rows(got[i], k * HALF[w], HALF[w]),
                              send_sems.at[len(ws) * k + i], recv_sems.at[len(ws) * k + i], peer))
    return cps


def _send_chip_partials(ws, parts, *, name):
    D = parts[0].shape[1]
    n = len(ws)

    def body(*refs):
        cps = _partial_copies(ws, refs[:n], refs[n:2 * n], refs[2 * n], refs[2 * n + 1])
        for cp in cps:
            cp.start()
        for cp in cps:
            cp.wait_recv()
        for cp in cps:
            cp.wait_send()

    return pl.pallas_call(
        body, name=name, in_specs=[ANY] * n, out_specs=[ANY] * n,
        out_shape=[_sds((len(CHIP_FLIPS) * HALF[w], D), parts[0].dtype) for w in ws],
        scratch_shapes=[pltpu.SemaphoreType.DMA((len(CHIP_FLIPS) * n,)), pltpu.SemaphoreType.DMA((len(CHIP_FLIPS) * n,))],
    )(*parts)


def _send_start(ws, parts, *, name):
    D = parts[0].shape[1]
    n = len(ws)
    bufs = list(parts) + [lax.empty((len(CHIP_FLIPS) * HALF[w], D), parts[0].dtype) for w in ws]

    def body(*refs):
        send_sems, recv_sems = refs[2 * n], refs[2 * n + 1]
        for cp in _partial_copies(ws, refs[:n], refs[n:2 * n], send_sems, recv_sems):
            cp.start()
        refs[-1][...] = jnp.zeros_like(refs[-1])

    outs = pl.pallas_call(
        body, name=name, in_specs=[HBM] * (2 * n),
        out_specs=[SEM, SEM] + [HBM] * (2 * n) + [pl.BlockSpec(memory_space=pltpu.VMEM)],
        out_shape=[pltpu.SemaphoreType.DMA((len(CHIP_FLIPS) * n,)), pltpu.SemaphoreType.DMA((len(CHIP_FLIPS) * n,))]
        + [pltpu.HBM(b.shape, b.dtype) for b in bufs] + [TOKEN],
        input_output_aliases={i: 2 + i for i in range(2 * n)},
        compiler_params=pltpu.CompilerParams(has_side_effects=EFFECT),
    )(*[pltpu.with_memory_space_constraint(b, pltpu.HBM) for b in bufs])
    return dict(sems=outs[0:2], parts=outs[2:2 + n], got=outs[2 + n:2 + 2 * n], token=outs[-1])


def _send_wait(ws, s, after, *, name):
    n = len(ws)

    def body(*refs):
        for cp in _partial_copies(ws, refs[:n], refs[n:2 * n], refs[2 * n], refs[2 * n + 1]):
            cp.wait_send()
            cp.wait_recv()

    bufs = list(s["parts"]) + list(s["got"])
    outs = pl.pallas_call(
        body, name=name, in_specs=[HBM] * (2 * n) + [SEM, SEM] + [ANY] * len(after), out_specs=[HBM] * (2 * n),
        out_shape=[pltpu.HBM(b.shape, b.dtype) for b in bufs],
        input_output_aliases={i: i for i in range(2 * n)},
        compiler_params=pltpu.CompilerParams(has_side_effects=EFFECT),
    )(*bufs, *s["sems"], *after)
    return outs[:n], outs[n:]


def _chip_reduce(ws, parts, got, *, name, after=None):
    D = parts[0].shape[1]
    nk = len(CHIP_FLIPS)
    n = len(ws)
    extra = [] if after is None else [after]

    def body(*refs):
        refs = refs[len(extra):]
        outs = refs[(1 + nk) * n:]
        for i in range(n):
            acc = refs[i][...].astype(F32)
            for k in range(nk):
                acc = acc + refs[n * (1 + k) + i][...].astype(F32)
            outs[i][...] = acc

    blk = [HALF[w] // REDUCE_SPLIT for w in ws]

    def q_idx(j):
        return (2 * lax.axis_index("x") + lax.axis_index("y")) * REDUCE_SPLIT + j

    in_specs = [pl.BlockSpec((b, D), lambda j: (q_idx(j), 0)) for b in blk]
    for k in range(nk):
        in_specs += [pl.BlockSpec((b, D), functools.partial(lambda j, k: (k * REDUCE_SPLIT + j, 0), k=k)) for b in blk]
    out_specs = [pl.BlockSpec((b, D), lambda j: (lax.axis_index("c") * REDUCE_SPLIT + j, 0)) for b in blk]
    return pl.pallas_call(
        body, name=name, grid=(REDUCE_SPLIT,), in_specs=[ANY] * len(extra) + in_specs, out_specs=out_specs,
        out_shape=[_sds((SLAB[w], D), F32) for w in ws],
        compiler_params=_cp(("parallel",)),
    )(*extra, *parts, *[g for _ in range(nk) for g in got])


def _exchange_reduced(ws, shards, *, name):
    n = len(ws)

    def body(*refs):
        ins, outs = refs[:n], refs[n:2 * n]
        send_sems, recv_sems = refs[2 * n], refs[2 * n + 1]
        x, y, c = _pos()
        sib = (x, y, 1 - c)
        cps = []
        for i, w in enumerate(ws):
            cp = _rcopy(_rows(ins[i], c * HALF[w], HALF[w]), _rows(outs[i], c * HALF[w], HALF[w]),
                        send_sems.at[i], recv_sems.at[i], sib)
            cp.start()
            cps.append(cp)
        for cp in cps:
            cp.wait_recv()
        for cp in cps:
            cp.wait_send()

    return pl.pallas_call(
        body, name=name, in_specs=[ANY] * n, out_specs=[ANY] * n,
        out_shape=[_sds(s.shape, s.dtype) for s in shards], input_output_aliases={i: i for i in range(n)},
        scratch_shapes=[pltpu.SemaphoreType.DMA((n,)), pltpu.SemaphoreType.DMA((n,))],
    )(*shards)


def _adamw_fn(w, g, m, v):
    m2 = ADAM_B1 * m + (1.0 - ADAM_B1) * g
    v2 = ADAM_B2 * v + (1.0 - ADAM_B2) * (g * g)
    m_hat = m2 / (1.0 - ADAM_B1 ** ADAM_STEP)
    v_hat = v2 / (1.0 - ADAM_B2 ** ADAM_STEP)
    return -ADAM_LR * (m_hat / (jnp.sqrt(v_hat) + ADAM_EPS) + ADAM_WD * w), m2, v2


def _adamw(w, g, m, v, *, name):
    shp = _sds(w.shape, F32)
    rows = w.shape[0]
    tm = max(t for t in range(SUBLANES, 512 + 1, SUBLANES) if rows % t == 0)
    return _rowwise(lambda wv, gv, mv, vv: (gv, *_adamw_fn(wv, gv, mv, vv)), [_full(w), _full(g), _full(m), _full(v)], [],
                    [shp] * 4, [], name=name, tm=tm)


SMALL_SEGS = (("loss", 8), ("norm_mix_w", 8), ("b_attn", 8), ("lb_logits", 8), ("hg_norm_w", 8), ("sinks", 8),
              ("norm_ffn_w", 8), ("conv_w", 72), ("conv_b", 24), ("final_norm_w", 8))
SMALL_OFF = {n: sum(r for _, r in SMALL_SEGS[:i]) for i, (n, _) in enumerate(SMALL_SEGS)}
SMALL_ROWS = sum(r for _, r in SMALL_SEGS)
LANES = 128


def _pack_small(parts):
    segs = []
    for n, r in SMALL_SEGS:
        a = parts.get(n)
        flat = jnp.zeros((0,), F32) if a is None else a.reshape(-1).astype(F32)
        segs.append(jnp.pad(flat, (0, r * LANES - flat.shape[0])).reshape(r, LANES))
    return jnp.concatenate(segs, axis=0)


def _unpack_small(pack, n, shape):
    size = math.prod(shape)
    r0 = SMALL_OFF[n]
    return pack[r0:r0 + dict(SMALL_SEGS)[n]].reshape(-1)[:size].reshape(shape)


def _small_update(sall, wp, mp, vp, *, after):
    R = SMALL_ROWS
    r_lb = SMALL_OFF["lb_logits"]

    def body(after_ref, s_ref, w_ref, m_ref, v_ref, g_ref, d_ref, m2_ref, v2_ref, loss_ref):
        g = s_ref[0]
        for i in range(1, N_DEV):
            g = g + s_ref[i]
        tot = jnp.sum(jnp.sum(g[0:8], axis=1, keepdims=True), axis=0, keepdims=True)
        loss_ref[...] = jnp.broadcast_to(tot, loss_ref.shape)
        lg = w_ref[r_lb:r_lb + 8, :]
        p0 = _sigmoid(lg - pltpu.roll(lg, 4, 0))
        d = g[r_lb:r_lb + 8]
        d = d + pltpu.roll(d, 4, 0)
        sign = jnp.where(lax.broadcasted_iota(jnp.int32, d.shape, 0) < 4, 1.0, -1.0)
        g = jnp.concatenate([g[:r_lb], sign * d * p0 * (1.0 - p0), g[r_lb + 8:]], axis=0)
        g_ref[...] = g
        d_ref[...], m2_ref[...], v2_ref[...] = _adamw_fn(w_ref[...], g, m_ref[...], v_ref[...])

    full = pl.BlockSpec((R, LANES), lambda: (0, 0))
    return pl.pallas_call(
        body, name="small_update",
        in_specs=[ANY, pl.BlockSpec((N_DEV, R, LANES), lambda: (0, 0, 0)), full, full, full],
        out_specs=[full, full, full, full, pl.BlockSpec((8, LANES), lambda: (0, 0))],
        out_shape=[_sds((R, LANES), F32)] * 4 + [_sds((8, LANES), F32)],
        compiler_params=_cp(),
    )(after, sall, wp, mp, vp)


def _lb_fwd(lb_logits):
    n = lb_logits.shape[1]

    def body(l_ref, o_ref):
        o_ref[...] = _sigmoid(l_ref[0:1, :] - l_ref[1:2, :])

    return pl.pallas_call(body, name="lb_fwd", out_shape=jax.ShapeDtypeStruct((1, n), F32), compiler_params=_cp())(lb_logits)


class _MeshExchange:
    def __init__(self, pack, cw8):
        self.gather = _gather_start(pack, cw8)
        self.sent = None
        self.conv_w8 = None

    def start(self):
        return self.gather["token"]

    def w_in(self, after):
        self.pack, l_in = _gather_wait_in(self.gather, after)
        return (_forward_in(l_in), N_CHIPS * SLAB[0], 0)

    def mid(self, after):
        l_ffn, l_out, l_cw = _gather_wait_rest(self.gather, self.pack, after)
        self.conv_w8 = jnp.concatenate([l_cw[i] for i in range(N_CHIPS)], axis=1)
        self.passing = _forward_rest_start(l_ffn, l_out)
        return self.passing["token"]

    def rest(self, after):
        l_ffn, l_out = _forward_rest_wait(self.passing, after)
        rows = N_CHIPS * SLAB[FFN_W[0]]
        return dict(w_gate_t=(l_ffn, rows, 0), w_up_t=(l_ffn, rows, 1), w_down=(l_ffn, rows, 2),
                    w_out=(l_out, N_CHIPS * SLAB[4], 0), conv_w8=self.conv_w8)

    def ffn_grads(self, gs):
        self.swap = _halves_start(FFN_W, gs, name="halves_ffn_start")
        return self.swap["token"]

    def ffn_grads_send(self, after):
        gs, theirs = _halves_wait(FFN_W, self.swap, after, name="halves_ffn_wait")
        parts = _chip_partial(FFN_W, gs, theirs, name="chip_partial_ffn", out_dtype=BF16)
        self.sent = _send_start(FFN_W, parts, name="send_ffn_start")
        return self.sent["token"]


def kernel(x, norm_mix_w, w_in, b_attn, lb_logits, hg_norm_w, sinks, w_out, norm_ffn_w, w_gate, w_up, conv_w, conv_b, w_down, final_norm_w, loss_target, m_norm_mix_w, m_w_in, m_b_attn, m_lb_logits, m_hg_norm_w, m_sinks, m_w_out, m_norm_ffn_w, m_w_gate, m_w_up, m_conv_w, m_conv_b, m_w_down, m_final_norm_w, v_norm_mix_w, v_w_in, v_b_attn, v_lb_logits, v_hg_norm_w, v_sinks, v_w_out, v_norm_ffn_w, v_w_gate, v_w_up, v_conv_w, v_conv_b, v_w_down, v_final_norm_w):
    D = D_MODEL
    q = 2 * lax.axis_index("x") + lax.axis_index("y")
    ccols = D_FF // N_CHIPS

    pack = jnp.concatenate([w_in[0].T, w_gate[0].T, w_up[0].T, w_down[0], w_out[0]], axis=0).astype(BF16)
    cw8 = jnp.concatenate([conv_w[0], jnp.zeros((SUBLANES - 3, ccols), F32)], axis=0)
    ex = _MeshExchange(pack, cw8)
    p = dict(norm_mix_w=norm_mix_w, b_attn=b_attn, lb=_lb_fwd(lb_logits), hg_norm_w=hg_norm_w, sinks=sinks,
             norm_ffn_w=norm_ffn_w, conv_b=conv_b, final_norm_w=final_norm_w.reshape(1, D))
    loss_cols, dx, g = _local_step(x[0], loss_target[0], p, ex)
    conv_w8 = ex.conv_w8

    small = _pack_small(dict(loss=loss_cols, norm_mix_w=g["norm_mix_w"], b_attn=g["b_attn"], lb_logits=g["lb"],
                             hg_norm_w=g["hg_norm_w"], sinks=g["sinks8"], norm_ffn_w=g["norm_ffn_w"],
                             conv_w=g["conv_w8"][:3], conv_b=g["conv_b"], final_norm_w=g["final_norm_w"]))
    parts_ffn, got_ffn = _send_wait(FFN_W, ex.sent, [dx], name="send_ffn_wait")
    late = (0, 4)
    gs = [g["g_in_t"], g["g_out"]]
    *theirs, sall = _exchange_halves(late, gs, small, name="exchange_halves_late")
    parts_late = _chip_partial(late, gs, theirs, name="chip_partial_late", out_dtype=BF16)
    sent_late = _send_start(late, parts_late, name="send_late_start")
    big = {}

    def finish(ws, parts, got, specs, tag, after):
        shards = _exchange_reduced(ws, _chip_reduce(ws, parts, got, name="chip_reduce_" + tag, after=after),
                                   name="exchange_reduced_" + tag)
        deltas = []
        for gw, (n, w, m, v, tr) in zip(shards, specs):
            view = (lambda a: a[0].T) if tr else (lambda a: a[0])
            back = (lambda a: a.T[None]) if tr else (lambda a: a[None])
            res = _adamw(view(w), gw, view(m), view(v), name="adamw_" + n)
            big[n] = tuple(back(r) for r in res)
            deltas.append(res[1])
        return deltas

    done_ffn = finish(FFN_W, parts_ffn, got_ffn, (("w_gate", w_gate, m_w_gate, v_w_gate, True),
                                                  ("w_up", w_up, m_w_up, v_w_up, True),
                                                  ("w_down", w_down, m_w_down, v_w_down, False)), "ffn", sent_late["token"])

    def place(a):
        return lax.dynamic_update_slice(jnp.zeros((3, D_FF), F32), a[0], (0, q * ccols))

    def small_pack(ws, cw):
        nm, ba, lbl, hg, sk, nf, cb, fn = ws
        return _pack_small(dict(norm_mix_w=nm, b_attn=ba, lb_logits=lbl, hg_norm_w=hg,
                                sinks=jnp.broadcast_to(sk.reshape(ATT_HEADS, 1), (ATT_HEADS, LANES)), norm_ffn_w=nf,
                                conv_w=cw, conv_b=cb, final_norm_w=fn))

    wp = small_pack((norm_mix_w, b_attn, lb_logits, hg_norm_w, sinks, norm_ffn_w, conv_b, final_norm_w), conv_w8[:3])
    mp = small_pack((m_norm_mix_w, m_b_attn, m_lb_logits, m_hg_norm_w, m_sinks, m_norm_ffn_w, m_conv_b, m_final_norm_w),
                    place(m_conv_w))
    vp = small_pack((v_norm_mix_w, v_b_attn, v_lb_logits, v_hg_norm_w, v_sinks, v_norm_ffn_w, v_conv_b, v_final_norm_w),
                    place(v_conv_w))
    outs = _small_update(sall, wp, mp, vp, after=sent_late["token"])
    loss = outs[4][0, 0]
    parts_late, got_late = _send_wait(late, sent_late, [*done_ffn, outs[4]], name="send_late_wait")
    finish(late, parts_late, got_late, (("w_in", w_in, m_w_in, v_w_in, True), ("w_out", w_out, m_w_out, v_w_out, False)),
           "late", None)

    def small_out(pk, n, ref):
        if n == "sinks":
            return pk[SMALL_OFF[n]:SMALL_OFF[n] + ATT_HEADS, 0].reshape(ref.shape)
        if n == "conv_w":
            full = _unpack_small(pk, n, (3, D_FF))
            return lax.dynamic_slice(full, (0, q * ccols), (3, ccols))[None]
        return _unpack_small(pk, n, ref.shape)

    refs = dict(norm_mix_w=norm_mix_w, b_attn=b_attn, lb_logits=lb_logits, hg_norm_w=hg_norm_w, sinks=sinks,
                norm_ffn_w=norm_ffn_w, conv_w=conv_w, conv_b=conv_b, final_norm_w=final_norm_w)
    order = ("norm_mix_w", "w_in", "b_attn", "lb_logits", "hg_norm_w", "sinks", "w_out", "norm_ffn_w", "w_gate", "w_up",
             "conv_w", "conv_b", "w_down", "final_norm_w")
    res = [loss, dx[None]]
    for k in range(4):
        for n in order:
            res.append(big[n][k] if n in big else small_out(outs[k], n, refs[n]))
    return tuple(res)
```

```python
import functools
import math

import jax
import jax.numpy as jnp
from jax import lax
from jax.experimental import pallas as pl
from jax.experimental.pallas import tpu as pltpu

F32 = jnp.float32
BF16 = jnp.bfloat16

D_MODEL = 1024
HG_HEADS = 4
HG_DK = 128
HG_W = HG_HEADS * HG_DK
HG_CHUNK = 64
HG_SUB = 8
HG_FWD_CHUNKS_PER_STEP = 4
HG_CHUNKS_PER_STEP = 2
ATT_HEADS = 8
ATT_KV = 2
ATT_GROUP = ATT_HEADS // ATT_KV
ATT_HD = 64
ATT_BLOCK = 128
ATT_Q_W = ATT_HEADS * ATT_HD
ATT_KV_W = ATT_KV * ATT_HD
ATT_COLS = ATT_Q_W + 2 * ATT_KV_W
IN_COLS = 4 * HG_W + ATT_COLS
D_FF = 2816
EPS = 1e-6
ADAM_LR, ADAM_B1, ADAM_B2, ADAM_EPS, ADAM_WD, ADAM_STEP = 0.001, 0.9, 0.999, 1e-08, 0.01, 10
NEG = -1e30

V7X_VMEM_BYTES = 64 * 1024 * 1024
VMEM_LIMIT = 48 * 1024 * 1024
SUBLANES = 8

N_CHIPS = 4


def _cp(sem=None, **kw):
    return pltpu.CompilerParams(dimension_semantics=sem, vmem_limit_bytes=VMEM_LIMIT, **kw)


def _sds(shape, dtype):
    return jax.ShapeDtypeStruct(shape, dtype)


TOKEN = jax.ShapeDtypeStruct((8, 128), jnp.float32)


def _wspec(w):
    arr, rows, blk = w
    return pl.BlockSpec((rows, arr.shape[1]), lambda i: (blk, 0))


def _mm_nt(a, w, *, splits, out_dtype, name, after=None, tm=512):
    M, K = a.shape
    N = w[1]
    tm = min(tm, M)
    assert sum(splits) == N and M % tm == 0
    offs = [sum(splits[:i]) for i in range(len(splits))]
    n_in = 2 if after is None else 3

    def body(*refs):
        a_ref, w_ref = refs[0], refs[1]
        acc = lax.dot_general(a_ref[...], w_ref[...], (((1,), (1,)), ((), ())), preferred_element_type=F32)
        for o_ref, c0, n in zip(refs[n_in:], offs, splits):
            o_ref[...] = acc[:, c0:c0 + n].astype(out_dtype)

    in_specs = [pl.BlockSpec((tm, K), lambda i: (i, 0)), _wspec(w)]
    args = [a, w[0]]
    if after is not None:
        in_specs.append(pl.BlockSpec(memory_space=pl.ANY))
        args.append(after)
    outs = pl.pallas_call(
        body, name=name, grid=(M // tm,), in_specs=in_specs,
        out_specs=[pl.BlockSpec((tm, n), lambda i: (i, 0)) for n in splits],
        out_shape=[_sds((M, n), out_dtype) for n in splits],
        compiler_params=_cp(("parallel",)),
    )(*args)
    return outs


def _mm_nn(pieces, ws, *, name, out_dtype=F32, residual=None, epilogue=None, prologue=None, after=None,
           w_transposed=False, tm=512):
    pro_fn, pro_rows, pro_bc, pro_out = prologue or (None, [], [], None)
    if prologue is not None:
        assert pieces is None and len(ws) == 1
        pieces = [[pro_out]]
    M = pieces[0][0].shape[0]
    K = ws[0][1] if w_transposed else ws[0][0].shape[1]
    tm = min(tm, M)
    flat = [] if prologue is not None else [p for grp in pieces for p in grp]
    n_p = len(flat)
    n_w = len(ws)
    n_pr, n_pb = len(pro_rows), len(pro_bc)
    fn, row_ins, bc_ins, row_outs, acc_outs = epilogue or (None, [], [], [_sds((M, K), out_dtype)], [])
    if residual is not None:
        assert epilogue is None
        row_ins = [residual]
    n_r, n_b, n_o = len(row_ins), len(bc_ins), len(row_outs)
    lead = [] if after is None else [after]

    def body(*refs):
        refs = refs[len(lead):]
        p_refs = refs[:n_p]
        w_refs = refs[n_p:n_p + n_w]
        extra = [r[...] for r in refs[n_p + n_w:n_p + n_w + n_r + n_b]]
        base = n_p + n_w + n_r + n_b
        pro = [r[...] for r in refs[base:base + n_pr + n_pb]]
        base += n_pr + n_pb
        o_refs = refs[base:base + n_o]
        a_refs = refs[base + n_o:base + n_o + len(acc_outs)]
        if pro_fn is not None:
            lhs = pro_fn(*pro).astype(pro_out.dtype)
            refs[-1][...] = lhs
            tiles = [lhs]
        else:
            tiles = [r[...] for r in p_refs]
        acc = None
        k = 0
        for gi, grp in enumerate(pieces):
            c0 = 0
            for p in grp:
                n = p.shape[1]
                if w_transposed:
                    t = lax.dot_general(tiles[k], w_refs[gi][...], (((1,), (1,)), ((), ())), preferred_element_type=F32)
                else:
                    t = jnp.dot(tiles[k], w_refs[gi][c0:c0 + n, :], preferred_element_type=F32)
                acc = t if acc is None else acc + t
                c0 += n
                k += 1
        if fn is None:
            res = (acc + extra[0] if residual is not None else acc,)
        else:
            res = fn(acc, *extra)
        for o_ref, val in zip(o_refs, res[:n_o]):
            o_ref[...] = val.astype(o_ref.dtype)
        if acc_outs:
            @pl.when(pl.program_id(0) == 0)
            def _():
                for a_ref in a_refs:
                    a_ref[...] = jnp.zeros_like(a_ref)
            for a_ref, val in zip(a_refs, res[n_o:]):
                a_ref[...] += val

    in_specs = [pl.BlockSpec((tm, p.shape[1]), lambda i: (i, 0)) for p in flat]
    in_specs += [_wspec(w) for w in ws]
    in_specs += [pl.BlockSpec((tm, r.shape[1]), lambda i: (i, 0)) for r in row_ins]
    in_specs += [pl.BlockSpec(b.shape, lambda i: (0, 0)) for b in bc_ins]
    in_specs += [pl.BlockSpec((tm, r.shape[1]), lambda i: (i, 0)) for r in pro_rows]
    in_specs += [pl.BlockSpec(b.shape, lambda i: (0, 0)) for b in pro_bc]
    out_specs = [pl.BlockSpec((tm, s.shape[1]), lambda i: (i, 0)) for s in row_outs]
    out_specs += [pl.BlockSpec(s.shape, lambda i: (0, 0)) for s in acc_outs]
    pro_outs = [] if prologue is None else [pro_out]
    out_specs += [pl.BlockSpec((tm, s.shape[1]), lambda i: (i, 0)) for s in pro_outs]
    outs = pl.pallas_call(
        body, name=name, grid=(M // tm,), in_specs=[pl.BlockSpec(memory_space=pl.ANY)] * len(lead) + in_specs,
        out_specs=out_specs, out_shape=list(row_outs) + list(acc_outs) + pro_outs,
        compiler_params=_cp(("arbitrary",) if acc_outs else ("parallel",)),
    )(*lead, *flat, *[w[0] for w in ws], *row_ins, *bc_ins, *pro_rows, *pro_bc)
    return outs if (epilogue is not None or prologue is not None) else outs[0]


def _mm_tn(pieces, x, *, name, out_dtype=BF16, tt=1024):
    M, K = x.shape
    tt = min(tt, M)
    ns = [p.shape[1] for p in pieces]
    offs = [sum(ns[:i]) for i in range(len(ns))]
    N = sum(ns)
    n_p = len(pieces)
    last = M // tt - 1

    def body(*refs):
        p_refs = refs[:n_p]
        x_ref = refs[n_p]
        o_ref, acc_ref = refs[n_p + 1], refs[n_p + 2]

        @pl.when(pl.program_id(0) == 0)
        def _():
            acc_ref[...] = jnp.zeros_like(acc_ref)

        xv = x_ref[...]
        for p_ref, c0, n in zip(p_refs, offs, ns):
            acc_ref[c0:c0 + n, :] += lax.dot_general(p_ref[...], xv, (((0,), (0,)), ((), ())),
                                                      preferred_element_type=F32)

        @pl.when(pl.program_id(0) == last)
        def _():
            o_ref[...] = acc_ref[...].astype(o_ref.dtype)

    in_specs = [pl.BlockSpec((tt, n), lambda i: (i, 0)) for n in ns]
    in_specs.append(pl.BlockSpec((tt, K), lambda i: (i, 0)))
    return pl.pallas_call(
        body, name=name, grid=(M // tt,), in_specs=in_specs,
        out_specs=pl.BlockSpec((N, K), lambda i: (0, 0)),
        out_shape=_sds((N, K), out_dtype),
        scratch_shapes=[pltpu.VMEM((N, K), F32)],
        compiler_params=_cp(("arbitrary",)),
    )(*pieces, x)


def _rms_fwd(xf, w):
    inv = lax.rsqrt(jnp.mean(xf * xf, axis=-1, keepdims=True) + EPS)
    return xf * inv * w


def _rms_bwd(xf, w, dy):
    inv = lax.rsqrt(jnp.mean(xf * xf, axis=-1, keepdims=True) + EPS)
    xhat = xf * inv
    dxhat = dy * w
    dx = inv * (dxhat - xhat * jnp.mean(dxhat * xhat, axis=-1, keepdims=True))
    dw = jnp.sum(dy * xhat, axis=0, keepdims=True)
    return dx, dw


def _sigmoid(x):
    return 1.0 / (1.0 + jnp.exp(-x))


def _rowwise(fn, row_ins, bc_ins, row_outs, acc_outs, *, name, tm=256, after=None):
    M = row_outs[0].shape[0] if row_outs else row_ins[0][0].shape[0]
    assert M % tm == 0 and tm % SUBLANES == 0, (name, M, tm)
    n_r, n_b, n_o, n_a = len(row_ins), len(bc_ins), len(row_outs), len(acc_outs)
    n_after = 0 if after is None else 1

    def body(*refs):
        refs = refs[n_after:]
        ins = [r[...] for r in refs[:n_r + n_b]]
        o_refs = refs[n_r + n_b:n_r + n_b + n_o]
        a_refs = refs[n_r + n_b + n_o:]
        res = fn(*ins)
        for o_ref, val in zip(o_refs, res[:n_o]):
            o_ref[...] = val.astype(o_ref.dtype)
        if n_a:
            @pl.when(pl.program_id(0) == 0)
            def _():
                for a_ref in a_refs:
                    a_ref[...] = jnp.zeros_like(a_ref)
            for a_ref, val in zip(a_refs, res[n_o:]):
                a_ref[...] += val

    in_specs = [pl.BlockSpec((tm, cw), functools.partial(lambda i, cb, r0: (i + r0, cb), cb=cb, r0=r0))
                for (_, cw, cb, r0) in row_ins]
    in_specs += [pl.BlockSpec(b.shape, lambda i: (0, 0)) for b in bc_ins]
    out_specs = [pl.BlockSpec((tm, s.shape[1]), lambda i: (i, 0)) for s in row_outs]
    out_specs += [pl.BlockSpec(s.shape, lambda i: (0, 0)) for s in acc_outs]
    if n_after:
        in_specs = [pl.BlockSpec(memory_space=pl.ANY)] + in_specs
    return pl.pallas_call(
        body, name=name, grid=(M // tm,), in_specs=in_specs, out_specs=out_specs,
        out_shape=list(row_outs) + list(acc_outs),
        compiler_params=_cp(("arbitrary",) if n_a else ("parallel",)),
    )(*([after] if n_after else []), *[r[0] for r in row_ins], *bc_ins)


def _full(a, first_row_block=0):
    return (a, a.shape[1], 0, first_row_block)


def _conv_rows(ext, w_ref_val, lo):
    s1 = pltpu.roll(ext, 1, 0)
    s2 = pltpu.roll(ext, 2, 0)
    y = w_ref_val[0:1, :] * s2 + w_ref_val[1:2, :] * s1 + w_ref_val[2:3, :] * ext
    return y[SUBLANES:, :]


def _ffn_in(v, w_gate, w_up, conv_w8, conv_b, *, name, tm=256):
    T, K = v.shape
    N = w_gate[1]
    tm = min(tm, T)

    def body(v_ref, wg_ref, wu_ref, cw_ref, cb_ref, gp_ref, up_ref, gate_ref, act_ref, carry_sc):
        @pl.when(pl.program_id(0) == 0)
        def _():
            carry_sc[...] = jnp.zeros_like(carry_sc)

        vv = v_ref[...]
        dn = (((1,), (1,)), ((), ()))
        gp = lax.dot_general(vv, wg_ref[...], dn, preferred_element_type=F32)
        up = lax.dot_general(vv, wu_ref[...], dn, preferred_element_type=F32)
        gp_ref[...] = gp.astype(gp_ref.dtype)
        up_ref[...] = up.astype(up_ref.dtype)
        gate = _conv_rows(jnp.concatenate([carry_sc[...], gp], axis=0), cw_ref[...], 0) + cb_ref[...]
        gate_ref[...] = gate
        act_ref[...] = (gate * _sigmoid(gate) * up).astype(act_ref.dtype)
        carry_sc[...] = gp[tm - SUBLANES:, :]

    tile = pl.BlockSpec((tm, N), lambda i: (i, 0))
    return pl.pallas_call(
        body, name=name, grid=(T // tm,),
        in_specs=[pl.BlockSpec((tm, K), lambda i: (i, 0)), _wspec(w_gate), _wspec(w_up),
                  pl.BlockSpec((SUBLANES, N), lambda i: (0, 0)), pl.BlockSpec((1, N), lambda i: (0, 0))],
        out_specs=[tile] * 4,
        out_shape=[_sds((T, N), BF16), _sds((T, N), BF16), _sds((T, N), F32), _sds((T, N), BF16)],
        scratch_shapes=[pltpu.VMEM((SUBLANES, N), F32)],
        compiler_params=_cp(("arbitrary",)),
    )(v, w_gate[0], w_up[0], conv_w8, conv_b)


def _ffn_back(dh2, w_down, gp, up, gate, conv_w8, *, name, tr=512, tc=1408):
    T, C = gp.shape
    K = dh2.shape[1]
    warr, _, wblk = w_down
    tr = min(tr, T)
    nr = T // tr
    ncb = C // tc

    def body(dh_ref, wd_ref, gp_ref, up_ref, gate_ref, w_ref, dgp_ref, dup_ref, dw_ref, db_ref, carry_sc):
        @pl.when(pl.program_id(1) == 0)
        def _():
            carry_sc[...] = jnp.zeros_like(carry_sc)
            dw_ref[...] = jnp.zeros_like(dw_ref)
            db_ref[...] = jnp.zeros_like(db_ref)

        w = w_ref[...]
        dact = lax.dot_general(dh_ref[...], wd_ref[...], (((1,), (1,)), ((), ())), preferred_element_type=F32)
        gpc = gp_ref[...].astype(F32)
        gate = gate_ref[...]
        sg = _sigmoid(gate)
        silu = gate * sg
        dup_ref[...] = (dact * silu).astype(dup_ref.dtype)
        dgate = dact * up_ref[...].astype(F32) * (sg + silu * (1.0 - sg))
        ext = jnp.concatenate([dgate, carry_sc[...]], axis=0)
        n = tr + SUBLANES
        g1 = pltpu.roll(ext, n - 1, 0)[:tr]
        g2 = pltpu.roll(ext, n - 2, 0)[:tr]
        dgp_ref[...] = (w[2:3, :] * dgate + w[1:2, :] * g1 + w[0:1, :] * g2).astype(dgp_ref.dtype)
        dw0 = jnp.sum(gpc * g2, axis=0, keepdims=True)
        dw1 = jnp.sum(gpc * g1, axis=0, keepdims=True)
        dw2 = jnp.sum(gpc * dgate, axis=0, keepdims=True)
        z = jnp.zeros((SUBLANES - 3, gpc.shape[1]), F32)
        dw_ref[...] += jnp.concatenate([dw0, dw1, dw2, z], axis=0)
        db_ref[...] += jnp.sum(dgate, axis=0, keepdims=True)
        carry_sc[...] = dgate[:SUBLANES]

    rev = lambda i: nr - 1 - i
    cur = pl.BlockSpec((tr, tc), lambda j, i: (rev(i), j))
    return pl.pallas_call(
        body, name=name, grid=(ncb, nr),
        in_specs=[pl.BlockSpec((tr, K), lambda j, i: (rev(i), 0)),
                  pl.BlockSpec((tc, K), lambda j, i: (wblk * ncb + j, 0)),
                  cur, cur, cur,
                  pl.BlockSpec((SUBLANES, tc), lambda j, i: (0, j))],
        out_specs=[cur, cur,
                   pl.BlockSpec((SUBLANES, tc), lambda j, i: (0, j)),
                   pl.BlockSpec((1, tc), lambda j, i: (0, j))],
        out_shape=[_sds((T, C), BF16), _sds((T, C), BF16), _sds((SUBLANES, C), F32), _sds((1, C), F32)],
        scratch_shapes=[pltpu.VMEM((SUBLANES, tc), F32)],
        compiler_params=_cp(("parallel", "arbitrary")),
    )(dh2, warr, gp, up, gate, conv_w8)


def _cumsum_rows(x):
    n = x.shape[0]
    row = lax.broadcasted_iota(jnp.int32, x.shape, 0)
    s = 1
    while s < n:
        x = x + jnp.where(row >= s, pltpu.roll(x, s, 0), 0.0)
        s *= 2
    return x


def _rcumsum_rows(x):
    n = x.shape[0]
    row = lax.broadcasted_iota(jnp.int32, x.shape, 0)
    s = 1
    while s < n:
        x = x + jnp.where(row < n - s, pltpu.roll(x, n - s, 0), 0.0)
        s *= 2
    return x


def _dot_nt(a, b):
    return lax.dot_general(a.astype(BF16), b.astype(BF16), (((1,), (1,)), ((), ())), preferred_element_type=F32)


def _dot_tn(a, b):
    return lax.dot_general(a.astype(BF16), b.astype(BF16), (((0,), (0,)), ((), ())), preferred_element_type=F32)


def _dot_nn(a, b):
    return jnp.dot(a.astype(BF16), b.astype(BF16), preferred_element_type=F32)


def _dot3(a, b, contract):
    def split(x):
        hi = x.astype(BF16)
        return hi, (x - hi.astype(F32)).astype(BF16)

    a_hi, a_lo = split(a)
    b_hi, b_lo = split(b)
    dot = lambda x, y: lax.dot_general(x, y, (contract, ((), ())), preferred_element_type=F32)
    return dot(a_hi, b_hi) + (dot(a_hi, b_lo) + dot(a_lo, b_hi))


NT, TN, NN = ((1,), (1,)), ((0,), (0,)), ((1,), (0,))


def _hg_gates(hq, hf, lbv):
    sig = _sigmoid(hf)
    f = lbv + (1.0 - lbv) * sig
    return sig, f, jnp.log(f), 1.0 - f, hq * (HG_DK ** -0.5)


def _hg_sel_rows(ref, sp):
    return jnp.concatenate(
        [jnp.broadcast_to(ref[pl.ds(HG_SUB * i + sp, 1), :], (HG_SUB, HG_DK)) for i in range(HG_CHUNK // HG_SUB)], axis=0)


def _hg_masks():
    C = HG_CHUNK
    row = lax.broadcasted_iota(jnp.int32, (C, C), 0)
    col = lax.broadcasted_iota(jnp.int32, (C, C), 1)
    d = col - (row // HG_SUB) * HG_SUB
    tmod = row % HG_SUB
    diag_valid = jnp.logical_and(d >= 0, d <= tmod)
    return row, col, d, diag_valid


def _hg_strip_keys(k, b, r, n):
    ek = jnp.exp(r - b[:n])
    return ek, jnp.concatenate([k[:n] * ek, jnp.zeros((HG_CHUNK - n, k.shape[1]), F32)], axis=0)


def _hg_scores(q, k, b, b_sc, k_sc):
    C, S = HG_CHUNK, HG_SUB
    row, col, d, diag_valid = _hg_masks()
    blocks = [jnp.zeros((S, C), F32)]
    for i in range(1, C // S):
        r = b_sc[pl.ds(S * i - 1, 1), :]
        qi = q[S * i:S * (i + 1)] * jnp.exp(b[S * i:S * (i + 1)] - r)
        blocks.append(_dot_nt(qi, _hg_strip_keys(k, b, r, S * i)[1]))
    a_off = jnp.concatenate(blocks, axis=0)
    a_d = jnp.zeros((C, C), F32)
    for sp in range(S):
        bs = _hg_sel_rows(b_sc, sp)
        ks = _hg_sel_rows(k_sc, sp)
        e = jnp.exp(jnp.minimum(b - bs, 0.0))
        colv = jnp.sum(q * ks * e, axis=-1, keepdims=True)
        a_d = jnp.where(d == sp, colv, a_d)
    return a_off + jnp.where(diag_valid, a_d, 0.0)


def _hg_prep(hq_v, hf_v, lbv, b_sc, k_sc):
    sig, f, g, k, q = _hg_gates(hq_v, hf_v, lbv)
    b = _cumsum_rows(g)
    b_sc[...] = b
    k_sc[...] = k
    return sig, f, k, q, b, b_sc[pl.ds(HG_CHUNK - 1, 1), :]


def _hgrn_fwd(hq, hf, hi, lb, *, name):
    T = hq.shape[0]
    C, H, K = HG_CHUNK, HG_HEADS, HG_DK
    NC = T // C

    def body(hq_ref, hf_ref, hi_ref, lb_ref, o_ref, st_ref, s_sc, b_sc, k_sc):
        @pl.when(pl.program_id(0) == 0)
        def _():
            s_sc[...] = jnp.zeros_like(s_sc)

        st_all = s_sc[...]
        for j in range(P):
            rows = slice(C * j, C * (j + 1))
            st_ref[j] = st_all
            outs, news = [], []
            for h in range(H):
                sl = slice(K * h, K * (h + 1))
                _, _, k, q, b, bc = _hg_prep(hq_ref[rows, sl], hf_ref[rows, sl], lb_ref[:, sl], b_sc.at[j, h], k_sc.at[j, h])
                v = hi_ref[rows, sl]
                st0 = st_all[:, sl]
                a = _hg_scores(q, k, b, b_sc.at[j, h], k_sc.at[j, h])
                outs.append(_dot_nn(a, v) + _dot_nt(q * jnp.exp(b), st0))
                news.append(st0 * jnp.exp(bc) + _dot_tn(v, k * jnp.exp(bc - b)))
            o_ref[rows, :] = jnp.concatenate(outs, axis=1)
            st_all = jnp.concatenate(news, axis=1)
        s_sc[...] = st_all

    P = HG_FWD_CHUNKS_PER_STEP
    blk = pl.BlockSpec((P * C, H * K), lambda c: (c, 0))
    return pl.pallas_call(
        body, name=name, grid=(NC // P,),
        in_specs=[blk, blk, blk, pl.BlockSpec((1, H * K), lambda c: (0, 0))],
        out_specs=[blk, pl.BlockSpec((P, K, H * K), lambda c: (c, 0, 0))],
        out_shape=[_sds((T, H * K), F32), _sds((NC, K, H * K), F32)],
        scratch_shapes=[pltpu.VMEM((K, H * K), F32), pltpu.VMEM((P, H, C, K), F32), pltpu.VMEM((P, H, C, K), F32)],
        compiler_params=_cp(("arbitrary",)),
    )(hq, hf, hi, lb)


def _hgrn_bwd(hq, hf, hi, lb, states, do, *, name):
    T = hq.shape[0]
    C, H, K, S = HG_CHUNK, HG_HEADS, HG_DK, HG_SUB
    NC = T // C

    def intra_slow(q, k, b, da, b_sc, k_sc):
        row, col, d, diag_valid = _hg_masks()
        a_blocks = [jnp.zeros((S, C), F32)]
        dq_blocks = [jnp.zeros((S, K), F32)]
        dk = jnp.zeros((C, K), F32)
        for i in range(1, C // S):
            r = b_sc[pl.ds(S * i - 1, 1), :]
            eq = jnp.exp(b[S * i:S * (i + 1)] - r)
            ek = jnp.exp(jnp.minimum(r - b, 0.0))
            qi = q[S * i:S * (i + 1)] * eq
            kk = k * ek
            a_blocks.append(_dot_nt(qi, kk))
            dai = jnp.where(col[S * i:S * (i + 1)] < S * i, da[S * i:S * (i + 1)], 0.0)
            dq_blocks.append(_dot_nn(dai, kk) * eq)
            dk = dk + _dot_tn(dai, qi) * ek
        dq = jnp.concatenate(dq_blocks, axis=0)
        a_off = jnp.where(col < (row // S) * S, jnp.concatenate(a_blocks, axis=0), 0.0)
        same_blk = (row // S == col // S).astype(BF16)
        tmod = (lax.broadcasted_iota(jnp.int32, (C, K), 0)) % S
        a_d = jnp.zeros((C, C), F32)
        dk_d = jnp.zeros((C, K), F32)
        for sp in range(S):
            bs = _hg_sel_rows(b_sc, sp)
            ks = _hg_sel_rows(k_sc, sp)
            e = jnp.exp(jnp.minimum(b - bs, 0.0))
            eks = e * ks
            a_d = jnp.where(d == sp, jnp.sum(q * eks, axis=-1, keepdims=True), a_d)
            dacol = jnp.sum(jnp.where(d == sp, da, 0.0), axis=-1, keepdims=True)
            dq = dq + dacol * eks
            wq = dacol * e * q
            wq_hi = wq.astype(BF16)
            wq_lo = (wq - wq_hi.astype(F32)).astype(BF16)
            blk_sum = (jnp.dot(same_blk, wq_hi, preferred_element_type=F32)
                       + jnp.dot(same_blk, wq_lo, preferred_element_type=F32))
            dk_d = jnp.where(tmod == sp, blk_sum, dk_d)
        return a_off + jnp.where(diag_valid, a_d, 0.0), dq, dk + dk_d

    def one_head(pre, v, lbv, st0, dst1, dout, b_sc, k_sc):
        sig, f, k, q, b, bc = pre
        ebc = jnp.exp(bc)
        eb = jnp.exp(b)
        ekb = jnp.exp(bc - b)
        qt = q * eb
        kb = k * ekb
        row = lax.broadcasted_iota(jnp.int32, (C, C), 0)
        col = lax.broadcasted_iota(jnp.int32, (C, C), 1)
        da = jnp.where(col <= row, _dot_nt(dout, v), 0.0)
        dkb = _dot_nn(v, dst1)
        new_ds = _dot_tn(dout, qt) + dst1 * ebc
        a, dq_i, dk_i = intra_slow(q, k, b, da, b_sc, k_sc)
        dq = _dot_nn(dout, st0) * eb + dq_i
        dk = dkb * ekb + dk_i
        dv = _dot_tn(a, dout) + _dot_nt(kb, dst1)
        extra = jnp.sum(dkb * kb, axis=0, keepdims=True) + ebc * jnp.sum(st0 * dst1, axis=0, keepdims=True)
        rowk = lax.broadcasted_iota(jnp.int32, (C, K), 0)
        db = q * dq - k * dk + jnp.where(rowk == C - 1, extra, 0.0)
        dg = _rcumsum_rows(db)
        df = dg / f - dk
        return (dq * (K ** -0.5), df * (1.0 - lbv) * sig * (1.0 - sig), dv,
                jnp.sum(df * (1.0 - sig), axis=0, keepdims=True), new_ds)

    def body(hq_ref, hf_ref, hi_ref, lb_ref, st_ref, do_ref, dq_ref, dhf_ref, dv_ref, dlb_ref, ds_sc, b_sc, k_sc):
        @pl.when(pl.program_id(0) == 0)
        def _():
            ds_sc[...] = jnp.zeros_like(ds_sc)
            dlb_ref[...] = jnp.zeros_like(dlb_ref)

        ds_all = ds_sc[...]
        dlb = jnp.zeros((1, H * K), F32)
        for j in reversed(range(P)):
            rows = slice(C * j, C * (j + 1))
            st_all = st_ref[j]
            res = []
            for h in range(H):
                sl = slice(K * h, K * (h + 1))
                pre = _hg_prep(hq_ref[rows, sl], hf_ref[rows, sl], lb_ref[:, sl], b_sc.at[j, h], k_sc.at[j, h])
                res.append(one_head(pre, hi_ref[rows, sl], lb_ref[:, sl], st_all[:, sl], ds_all[:, sl], do_ref[rows, sl],
                                    b_sc.at[j, h], k_sc.at[j, h]))
            cat = lambda i: jnp.concatenate([r[i] for r in res], axis=1)
            dq_ref[rows, :] = cat(0).astype(dq_ref.dtype)
            dhf_ref[rows, :] = cat(1).astype(dhf_ref.dtype)
            dv_ref[rows, :] = cat(2).astype(dv_ref.dtype)
            dlb = dlb + cat(3)
            ds_all = cat(4)
        dlb_ref[...] += dlb
        ds_sc[...] = ds_all

    P = HG_CHUNKS_PER_STEP
    NS = NC // P
    blk = pl.BlockSpec((P * C, H * K), lambda c: (NS - 1 - c, 0))
    par = pl.BlockSpec((1, H * K), lambda c: (0, 0))
    return pl.pallas_call(
        body, name=name, grid=(NS,),
        in_specs=[blk, blk, blk, par, pl.BlockSpec((P, K, H * K), lambda c: (NS - 1 - c, 0, 0)), blk],
        out_specs=[blk, blk, blk, par],
        out_shape=[_sds((T, H * K), BF16)] * 3 + [_sds((1, H * K), F32)],
        scratch_shapes=[pltpu.VMEM((K, H * K), F32), pltpu.VMEM((P, H, C, K), F32), pltpu.VMEM((P, H, C, K), F32)],
        compiler_params=_cp(("arbitrary",)),
    )(hq, hf, hi, lb, states, do)


ATT_STACK = ATT_GROUP


def _att_valid(n):
    R, B = ATT_STACK * ATT_BLOCK, ATT_BLOCK
    j = lax.broadcasted_iota(jnp.int32, (2 * B, R), 0)
    t = lax.broadcasted_iota(jnp.int32, (2 * B, R), 1) % B
    dist = t + B - j
    first_key = jnp.where(n > 0, 0, B)
    return jnp.logical_and(jnp.logical_and(dist >= 0, dist < B), j >= first_key)


def _att_load(cur_ref, prev_ref, ba_ref, h0):
    hd = ATT_HD
    kv = h0 // ATT_GROUP
    def cols(ref, c0):
        return ref[:, c0:c0 + hd] + ba_ref[:, c0:c0 + hd]
    qs = jnp.concatenate([cols(cur_ref, hd * (h0 + g)) for g in range(ATT_STACK)], axis=0)
    kc = jnp.concatenate([cols(prev_ref, ATT_Q_W + hd * kv), cols(cur_ref, ATT_Q_W + hd * kv)], axis=0)
    vc = jnp.concatenate([cols(prev_ref, ATT_Q_W + ATT_KV_W + hd * kv), cols(cur_ref, ATT_Q_W + ATT_KV_W + hd * kv)], axis=0)
    return qs, kc, vc


def _att_probs(qs, kc, valid, sink_ref, h0):
    scale = 1.0 / math.sqrt(ATT_HD)
    s = jnp.where(valid, _dot_nt(kc, qs) * scale, NEG)
    sink = jnp.concatenate([jnp.full((1, ATT_BLOCK), sink_ref[0, h0 + g], F32) for g in range(ATT_STACK)], axis=1)
    m = jnp.maximum(jnp.max(s, axis=0, keepdims=True), sink)
    p = jnp.exp(s - m)
    ps = jnp.exp(sink - m)
    inv = 1.0 / (jnp.sum(p, axis=0, keepdims=True) + ps)
    return p * inv, ps * inv


def _attn_fwd(att, b_attn, sinks, *, name, after=None):
    T = att.shape[0]
    B = ATT_BLOCK
    NB = T // B
    lead = [] if after is None else [after]

    def body(*refs):
        sink_ref, cur_ref, prev_ref, ba_ref, o_ref = refs[len(lead):]
        valid = _att_valid(pl.program_id(0))
        outs = []
        for h0 in range(0, ATT_HEADS, ATT_STACK):
            qs, kc, vc = _att_load(cur_ref, prev_ref, ba_ref, h0)
            prob, _ = _att_probs(qs, kc, valid, sink_ref, h0)
            o = _dot_tn(prob, vc)
            outs += [o[B * g:B * (g + 1)] for g in range(ATT_STACK)]
        o_ref[...] = jnp.concatenate(outs, axis=1)

    return pl.pallas_call(
        body, name=name, grid=(NB,),
        in_specs=[pl.BlockSpec(memory_space=pl.ANY)] * len(lead) + [
            pl.BlockSpec(memory_space=pltpu.SMEM),
            pl.BlockSpec((B, ATT_COLS), lambda n: (n, 0)),
            pl.BlockSpec((B, ATT_COLS), lambda n: (jnp.maximum(n - 1, 0), 0)),
            pl.BlockSpec((1, ATT_COLS), lambda n: (0, 0))],
        out_specs=pl.BlockSpec((B, ATT_Q_W), lambda n: (n, 0)),
        out_shape=_sds((T, ATT_Q_W), F32),
        compiler_params=_cp(("parallel",)),
    )(*lead, sinks, att, att, b_attn)


def _attn_bwd(att, b_attn, sinks, dmix, *, name):
    T = att.shape[0]
    B, hd = ATT_BLOCK, ATT_HD
    NB = T // B
    scale = 1.0 / math.sqrt(hd)

    def body(sink_ref, cur_ref, prev_ref, ba_ref, do_ref, daq_ref, dakv_ref, dsink_ref, dbq_ref, dbkv_ref, carry_sc):
        n = pl.program_id(0)

        @pl.when(n == 0)
        def _():
            carry_sc[...] = jnp.zeros_like(carry_sc)
            dsink_ref[...] = jnp.zeros_like(dsink_ref)
            dbq_ref[...] = jnp.zeros_like(dbq_ref)
            dbkv_ref[...] = jnp.zeros_like(dbkv_ref)

        @pl.when(n < NB)
        def _():
            valid = _att_valid(n)
            hrow = lax.broadcasted_iota(jnp.int32, (SUBLANES, 128), 0)
            dsink = jnp.zeros((SUBLANES, 128), F32)
            dqs = []
            dks = [jnp.zeros((2 * B, hd), F32)] * ATT_KV
            dvs = [jnp.zeros((2 * B, hd), F32)] * ATT_KV
            for h0 in range(0, ATT_HEADS, ATT_STACK):
                kv = h0 // ATT_GROUP
                qs, kc, vc = _att_load(cur_ref, prev_ref, ba_ref, h0)
                prob, psink = _att_probs(qs, kc, valid, sink_ref, h0)
                dout = jnp.concatenate([do_ref[:, hd * (h0 + g):hd * (h0 + g + 1)] for g in range(ATT_STACK)], axis=0)
                dp = _dot_nt(vc, dout)
                delta = jnp.sum(prob * dp, axis=0, keepdims=True)
                dsc = prob * (dp - delta) * scale
                dq = _dot_tn(dsc, kc)
                dks[kv] = dks[kv] + _dot_nn(dsc, qs)
                dvs[kv] = dvs[kv] + _dot_nn(prob, dout)
                dsk = psink * delta
                for g in range(ATT_STACK):
                    dqs.append(dq[B * g:B * (g + 1)])
                    tot = jnp.sum(dsk[:, B * g:B * (g + 1)], axis=1, keepdims=True)
                    dsink = dsink - jnp.where(hrow == h0 + g, tot, 0.0)
            daq = jnp.concatenate(dqs, axis=1).astype(daq_ref.dtype)
            daq_ref[...] = daq
            dsink_ref[...] += dsink
            dbq_ref[...] += jnp.sum(daq.astype(F32), axis=0, keepdims=True)
            done = carry_sc[...] + jnp.concatenate([d[:B] for d in dks + dvs], axis=1)
            dakv_ref[...] = done.astype(dakv_ref.dtype)
            dbkv_ref[...] += jnp.sum(done.astype(dakv_ref.dtype).astype(F32), axis=0, keepdims=True)
            carry_sc[...] = jnp.concatenate([d[B:] for d in dks + dvs], axis=1)

        @pl.when(n == NB)
        def _():
            done = carry_sc[...]
            dakv_ref[...] = done.astype(dakv_ref.dtype)
            dbkv_ref[...] += jnp.sum(done.astype(dakv_ref.dtype).astype(F32), axis=0, keepdims=True)

    cl = lambda n: jnp.minimum(n, NB - 1)
    return pl.pallas_call(
        body, name=name, grid=(NB + 1,),
        in_specs=[pl.BlockSpec(memory_space=pltpu.SMEM),
                  pl.BlockSpec((B, ATT_COLS), lambda n: (cl(n), 0)),
                  pl.BlockSpec((B, ATT_COLS), lambda n: (jnp.maximum(cl(n) - 1, 0), 0)),
                  pl.BlockSpec((1, ATT_COLS), lambda n: (0, 0)),
                  pl.BlockSpec((B, ATT_Q_W), lambda n: (cl(n), 0))],
        out_specs=[pl.BlockSpec((B, ATT_Q_W), lambda n: (cl(n), 0)),
                   pl.BlockSpec((B, 2 * ATT_KV_W), lambda n: (jnp.maximum(n - 1, 0), 0)),
                   pl.BlockSpec((SUBLANES, 128), lambda n: (0, 0)),
                   pl.BlockSpec((1, ATT_Q_W), lambda n: (0, 0)),
                   pl.BlockSpec((1, 2 * ATT_KV_W), lambda n: (0, 0))],
        out_shape=[_sds((T, ATT_Q_W), BF16), _sds((T, 2 * ATT_KV_W), BF16), _sds((SUBLANES, 128), F32),
                   _sds((1, ATT_Q_W), F32), _sds((1, 2 * ATT_KV_W), F32)],
        scratch_shapes=[pltpu.VMEM((B, 2 * ATT_KV_W), F32)],
        compiler_params=_cp(("arbitrary",)),
    )(sinks, att, att, b_attn, dmix)


def _silu_and_grad(x):
    sg = _sigmoid(x)
    return x * sg, sg * (1.0 + x * (1.0 - sg))


def _mix_fwd_fn(o_raw, hg, o_att, hgw):
    outs = []
    for h in range(HG_HEADS):
        sl = slice(HG_DK * h, HG_DK * (h + 1))
        silu, _ = _silu_and_grad(hg[:, sl])
        outs.append(_rms_fwd(o_raw[:, sl], hgw) * silu)
    outs.append(o_att)
    return (jnp.concatenate(outs, axis=1),)


def _mix_bwd_fn(o_raw, hg, dmix, hgw):
    dos, dhgs = [], []
    dw = jnp.zeros((1, HG_DK), F32)
    for h in range(HG_HEADS):
        sl = slice(HG_DK * h, HG_DK * (h + 1))
        silu, dsilu = _silu_and_grad(hg[:, sl])
        dy = dmix[:, sl]
        dhgs.append(dy * _rms_fwd(o_raw[:, sl], hgw) * dsilu)
        dx, dwh = _rms_bwd(o_raw[:, sl], hgw, dy * silu)
        dos.append(dx)
        dw = dw + dwh
    return jnp.concatenate(dos, axis=1), jnp.concatenate(dhgs, axis=1), dw


def _final_fn(h2, tgt, wf):
    d = h2.shape[1]
    err = _rms_fwd(h2, wf) - tgt
    loss_cols = (0.5 / d) * jnp.sum(err * err, axis=0, keepdims=True)
    dh2, dwf = _rms_bwd(h2, wf, err * (1.0 / d))
    return dh2, dh2, loss_cols, dwf


class _NoExchange:
    def __init__(self, weights):
        self.weights = weights

    def start(self):
        return None

    def w_in(self, after):
        return self.weights["w_in_t"]

    def mid(self, after):
        return None

    def rest(self, after):
        return self.weights

    def ffn_grads(self, gs):
        return None

    def ffn_grads_send(self, after):
        return None


def _local_step(x, tgt, p, ex):
    T, D = x.shape
    row = lambda n, dt: _sds((T, n), dt)
    acc = lambda n: _sds((1, n), F32)

    (u,) = _rowwise(lambda xv, w: (_rms_fwd(xv, w),), [_full(x)], [p["norm_mix_w"]], [row(D, BF16)], [], name="rms_mix",
                    after=ex.start())
    p = dict(p, w_in_t=ex.w_in(u))
    hq, hf, hi, hg, att = _mm_nt(u, p["w_in_t"], splits=[HG_W] * 4 + [ATT_COLS], out_dtype=F32, name="in_proj")
    o_raw, states = _hgrn_fwd(hq, hf, hi, p["lb"], name="hgrn_fwd")
    o_att = _attn_fwd(att, p["b_attn"], p["sinks"], name="attn_fwd", after=ex.mid(o_raw))
    p = dict(p, **ex.rest(o_att))
    def out_epilogue(prod, xv, w):
        h1v = prod + xv
        return h1v, _rms_fwd(h1v, w)

    h1, v, mix = _mm_nn(None, [p["w_out"]], name="mix_out_proj",
                        prologue=(lambda *a: _mix_fwd_fn(*a)[0], [o_raw, hg, o_att], [p["hg_norm_w"]], row(D, BF16)),
                        epilogue=(out_epilogue, [x], [p["norm_ffn_w"]], [row(D, F32), row(D, BF16)], []))
    gp, up, gate, act = _ffn_in(v, p["w_gate_t"], p["w_up_t"], p["conv_w8"], p["conv_b"], name="ffn_in")
    def down_epilogue(prod, h1v, tgtv, wf):
        return _final_fn(prod + h1v, tgtv, wf)

    dh2, dh2_b, loss_cols, d_final = _mm_nn(
        [[act]], [p["w_down"]], name="down_proj_loss",
        epilogue=(down_epilogue, [h1, tgt], [p["final_norm_w"]], [row(D, F32), row(D, BF16)], [acc(D), acc(D)]))

    g_down = _mm_tn([act], dh2_b, name="g_down")
    dgp, dup, d_conv_w8, d_conv_b = _ffn_back(dh2_b, p["w_down"], gp, up, gate, p["conv_w8"], name="ffn_back")
    g_gate_t = _mm_tn([dgp], v, name="g_gate")
    g_up_t = _mm_tn([dup], v, name="g_up")
    swapping = ex.ffn_grads([g_gate_t, g_up_t, g_down])

    def ffn_norm_bwd(dvv, hv, dh2v, w):
        dx, dw = _rms_bwd(hv, w, dvv)
        dh1v = dx + dh2v
        return dh1v, dh1v, dw

    dh1, dh1_b, d_norm_ffn = _mm_nn(
        [[dgp], [dup]], [p["w_gate_t"], p["w_up_t"]], name="d_v_norm", after=swapping,
        epilogue=(ffn_norm_bwd, [h1, dh2], [p["norm_ffn_w"]], [row(D, F32), row(D, BF16)], [acc(D)]))
    sent = ex.ffn_grads_send(dh1_b)
    def mix_bwd(dmixv, o_rawv, hgv, hgw):
        do_rawv, dhgv, dw = _mix_bwd_fn(o_rawv, hgv, dmixv[:, :HG_W], hgw)
        return do_rawv, dhgv, dmixv[:, HG_W:], dw

    do_raw, dhg, do_att, d_hg_norm = _mm_nn(
        [[dh1_b]], [p["w_out"]], name="d_mix_bwd", w_transposed=True, after=sent,
        epilogue=(mix_bwd, [o_raw, hg], [p["hg_norm_w"]], [row(HG_W, F32), row(HG_W, BF16), row(ATT_Q_W, F32)], [acc(HG_DK)]))
    g_out = _mm_tn([mix], dh1_b, name="g_out")
    daq, dakv, d_sinks8, d_bq, d_bkv = _attn_bwd(att, p["b_attn"], p["sinks"], do_att, name="attn_bwd")
    dhq, dhf, dhi, d_lb = _hgrn_bwd(hq, hf, hi, p["lb"], states, do_raw, name="hgrn_bwd")
    pieces = [dhq, dhf, dhi, dhg, daq, dakv]
    g_in_t = _mm_tn(pieces, u, name="g_in")

    def mix_norm_bwd(duv, xv, dh1v, w):
        dx, dw = _rms_bwd(xv, w, duv)
        return dx + dh1v, dw

    dx, d_norm_mix = _mm_nn([pieces], [p["w_in_t"]], name="d_u_norm",
                            epilogue=(mix_norm_bwd, [x, dh1], [p["norm_mix_w"]], [row(D, F32)], [acc(D)]))
    grads = dict(g_in_t=g_in_t, g_out=g_out, g_gate_t=g_gate_t, g_up_t=g_up_t, g_down=g_down,
                 norm_mix_w=d_norm_mix, b_attn=jnp.concatenate([d_bq, d_bkv], axis=1), lb=d_lb, hg_norm_w=d_hg_norm,
                 sinks8=d_sinks8, norm_ffn_w=d_norm_ffn, conv_w8=d_conv_w8, conv_b=d_conv_b, final_norm_w=d_final)
    return loss_cols, dx, grads


SLAB = (IN_COLS // N_CHIPS, D_FF // N_CHIPS, D_FF // N_CHIPS, D_FF // N_CHIPS, D_MODEL // N_CHIPS)
N_W = len(SLAB)
PACK_OFF = tuple(sum(SLAB[:i]) for i in range(N_W))
PACK_ROWS = sum(SLAB)
FULL_OFF = tuple(N_CHIPS * o for o in PACK_OFF)
FULL_ROWS = N_CHIPS * PACK_ROWS
HALF = tuple(s // 2 for s in SLAB)
HPACK_OFF = tuple(sum(HALF[:i]) for i in range(N_W))
HPACK_ROWS = sum(HALF)
HFULL_OFF = tuple(N_CHIPS * o for o in HPACK_OFF)
HFULL_ROWS = N_CHIPS * HPACK_ROWS
CHIP_FLIPS = ((1, 0), (0, 1), (1, 1))
N_DEV = 8
BF16_ROWS = 16
ANY = pl.BlockSpec(memory_space=pl.ANY)


def _pos():
    return lax.axis_index("x"), lax.axis_index("y"), lax.axis_index("c")


def _flip(v, f):
    return 1 - v if f else v


def _rcopy(src, dst, ssem, rsem, dev):
    return pltpu.make_async_remote_copy(src_ref=src, dst_ref=dst, send_sem=ssem, recv_sem=rsem, device_id=dev,
                                        device_id_type=pl.DeviceIdType.MESH)


def _rows(ref, start, n, align=None):
    if not isinstance(start, int):
        if align is None:
            align = SUBLANES * (4 // jnp.dtype(ref.dtype).itemsize)
        start = pl.multiple_of(start, align)
    return ref.at[pl.ds(start, n), :]


FFN_W = (1, 2, 3)
N_PEER = 1 + len(CHIP_FLIPS)
HBM = pl.BlockSpec(memory_space=pltpu.HBM)
SEM = pl.BlockSpec(memory_space=pltpu.SEMAPHORE)
EFFECT = pltpu.SideEffectType.DATAFLOW_SIDE_EFFECTING
LANES = 128


def _sent_rows(k, w, c):
    return (0, SLAB[w]) if k == 0 else (c * HALF[w], HALF[w])


def _gather_start(pack, cw8):
    D = pack.shape[1]
    lands = [lax.empty((N_CHIPS * SLAB[0], D), pack.dtype), lax.empty((3 * N_CHIPS * SLAB[1], D), pack.dtype),
             lax.empty((N_CHIPS * SLAB[4], D), pack.dtype), lax.empty((N_CHIPS,) + cw8.shape, cw8.dtype)]
    bufs = [pack, cw8] + lands

    def body(pack_ref, cw_ref, l_in, l_ffn, l_out, l_cw, *rest):
        in_send, in_recv, out_send, out_recv, ffn_send, ffn_recv = rest[:6]
        token = rest[-1]
        x, y, c = _pos()
        q = 2 * x + y
        peers = _gather_peers(x, y, c)

        def send(k, peer, w, land, base, ssem, rsem):
            r0, n = _sent_rows(k, w, c)
            _rcopy(_rows(pack_ref, PACK_OFF[w] + r0, n), _rows(land, base + q * SLAB[w] + r0, n), ssem, rsem, peer).start()

        for k, peer in enumerate(peers):
            send(k, peer, 0, l_in, 0, in_send.at[k], in_recv.at[k])
        for k, peer in enumerate(peers):
            send(k, peer, 4, l_out, 0, out_send.at[k], out_recv.at[k])
            _rcopy(cw_ref, l_cw.at[q], out_send.at[N_PEER + k], out_recv.at[N_PEER + k], peer).start()
        for j, w in enumerate(FFN_W):
            for k, peer in enumerate(peers):
                send(k, peer, w, l_ffn, j * N_CHIPS * SLAB[w], ffn_send.at[k], ffn_recv.at[k])
        token[...] = jnp.zeros_like(token)

    n_sem = (N_PEER, N_PEER, 2 * N_PEER, 2 * N_PEER, N_PEER, N_PEER)
    outs = pl.pallas_call(
        body, name="gather_start", in_specs=[HBM] * len(bufs),
        out_specs=[SEM] * len(n_sem) + [HBM] * len(bufs) + [pl.BlockSpec(memory_space=pltpu.VMEM)],
        out_shape=[pltpu.SemaphoreType.DMA((n,)) for n in n_sem]
        + [pltpu.HBM(b.shape, b.dtype) for b in bufs] + [TOKEN],
        input_output_aliases={i: len(n_sem) + i for i in range(len(bufs))},
        compiler_params=pltpu.CompilerParams(has_side_effects=EFFECT),
    )(*[pltpu.with_memory_space_constraint(b, pltpu.HBM) for b in bufs])
    bufs_out = outs[len(n_sem):]
    return dict(in_sems=outs[0:2], out_sems=outs[2:4], ffn_sems=outs[4:6], pack=bufs_out[0], cw=bufs_out[1], l_in=bufs_out[2],
                l_ffn=bufs_out[3], l_out=bufs_out[4], l_cw=bufs_out[5], token=bufs_out[6])


def _gather_peers(x, y, c):
    return [(x, y, 1 - c)] + [(_flip(x, fx), _flip(y, fy), c) for fx, fy in CHIP_FLIPS]


def _gather_wait_in(g, after):
    def body(pack_ref, l_in, send, recv, after_ref, pack_out, l_out):
        for k, peer in enumerate(_gather_peers(*_pos())):
            n = _sent_rows(k, 0, 0)[1]
            cp = _rcopy(_rows(pack_ref, PACK_OFF[0], n), _rows(l_in, 0, n), send.at[k], recv.at[k], peer)
            cp.wait_send()
            cp.wait_recv()

    return pl.pallas_call(
        body, name="gather_wait_in", in_specs=[HBM, HBM, SEM, SEM, ANY], out_specs=[HBM, HBM],
        out_shape=[pltpu.HBM(g["pack"].shape, g["pack"].dtype), pltpu.HBM(g["l_in"].shape, g["l_in"].dtype)],
        input_output_aliases={0: 0, 1: 1}, compiler_params=pltpu.CompilerParams(has_side_effects=EFFECT),
    )(g["pack"], g["l_in"], *g["in_sems"], after)


def _gather_wait_rest(g, pack, after):
    def body(pack_ref, cw_ref, l_ffn, l_out, l_cw, o_send, o_recv, f_send, f_recv, after_ref, o_ffn, o_out, o_cw):
        for k, peer in enumerate(_gather_peers(*_pos())):
            n_out = _sent_rows(k, 4, 0)[1]
            n_ffn = len(FFN_W) * _sent_rows(k, FFN_W[0], 0)[1]
            for cp in (_rcopy(_rows(pack_ref, PACK_OFF[4], n_out), _rows(l_out, 0, n_out), o_send.at[k], o_recv.at[k], peer),
                       _rcopy(cw_ref, l_cw.at[0], o_send.at[N_PEER + k], o_recv.at[N_PEER + k], peer),
                       _rcopy(_rows(pack_ref, PACK_OFF[FFN_W[0]], n_ffn), _rows(l_ffn, 0, n_ffn), f_send.at[k], f_recv.at[k], peer)):
                cp.wait_send()
                cp.wait_recv()

    ins = [pack, g["cw"], g["l_ffn"], g["l_out"], g["l_cw"]]
    return pl.pallas_call(
        body, name="gather_wait_rest", in_specs=[HBM] * 5 + [SEM] * 4 + [ANY], out_specs=[HBM] * 3,
        out_shape=[pltpu.HBM(b.shape, b.dtype) for b in ins[2:]],
        input_output_aliases={2: 0, 3: 1, 4: 2}, compiler_params=pltpu.CompilerParams(has_side_effects=EFFECT),
    )(*ins, *g["out_sems"], *g["ffn_sems"], after)


FWD_IN = ((0, 0, 0),)
FWD_REST = tuple((0, w, j * N_CHIPS * SLAB[w]) for j, w in enumerate(FFN_W)) + ((1, 4, 0),)


def _forward_copies(layout, src, dst, send_sems, recv_sems):
    x, y, c = _pos()
    sib = (x, y, 1 - c)
    cps = []
    for fx, fy in CHIP_FLIPS:
        qa = 2 * _flip(x, fx) + _flip(y, fy)
        for bi, w, base in layout:
            r0 = base + qa * SLAB[w] + c * HALF[w]
            cps.append(_rcopy(_rows(src[bi], r0, HALF[w]), _rows(dst[bi], r0, HALF[w]),
                              send_sems.at[len(cps)], recv_sems.at[len(cps)], sib))
    return cps


def _forward_in(l_in):
    n = len(CHIP_FLIPS) * len(FWD_IN)

    def body(in_ref, out_ref, send_sems, recv_sems):
        cps = _forward_copies(FWD_IN, [in_ref], [out_ref], send_sems, recv_sems)
        for cp in cps:
            cp.start()
        for cp in cps:
            cp.wait_recv()
        for cp in cps:
            cp.wait_send()

    return pl.pallas_call(
        body, name="forward_in", in_specs=[ANY], out_specs=ANY, out_shape=_sds(l_in.shape, l_in.dtype),
        input_output_aliases={0: 0},
        scratch_shapes=[pltpu.SemaphoreType.DMA((n,)), pltpu.SemaphoreType.DMA((n,))],
    )(l_in)


def _forward_rest_start(l_ffn, l_out):
    n = len(CHIP_FLIPS) * len(FWD_REST)
    bufs = [l_ffn, l_out]

    def body(a_ref, b_ref, send_sems, recv_sems, a_out, b_out, token):
        for cp in _forward_copies(FWD_REST, [a_ref, b_ref], [a_ref, b_ref], send_sems, recv_sems):
            cp.start()
        token[...] = jnp.zeros_like(token)

    outs = pl.pallas_call(
        body, name="forward_rest_start", in_specs=[HBM] * 2,
        out_specs=[SEM, SEM, HBM, HBM, pl.BlockSpec(memory_space=pltpu.VMEM)],
        out_shape=[pltpu.SemaphoreType.DMA((n,)), pltpu.SemaphoreType.DMA((n,))]
        + [pltpu.HBM(b.shape, b.dtype) for b in bufs] + [TOKEN],
        input_output_aliases={0: 2, 1: 3}, compiler_params=pltpu.CompilerParams(has_side_effects=EFFECT),
    )(*[pltpu.with_memory_space_constraint(b, pltpu.HBM) for b in bufs])
    return dict(sems=outs[0:2], bufs=outs[2:4], token=outs[4])


def _forward_rest_wait(s, after):
    def body(a_ref, b_ref, send_sems, recv_sems, after_ref, a_out, b_out):
        for cp in _forward_copies(FWD_REST, [a_ref, b_ref], [a_ref, b_ref], send_sems, recv_sems):
            cp.wait_send()
            cp.wait_recv()

    return pl.pallas_call(
        body, name="forward_rest_wait", in_specs=[HBM, HBM, SEM, SEM, ANY], out_specs=[HBM, HBM],
        out_shape=[pltpu.HBM(b.shape, b.dtype) for b in s["bufs"]],
        input_output_aliases={0: 0, 1: 1}, compiler_params=pltpu.CompilerParams(has_side_effects=EFFECT),
    )(*s["bufs"], *s["sems"], after)


def _exchange_halves(ws, gs, small, *, name):
    D = gs[0].shape[1]
    n = len(ws)
    has_small = small is not None

    def body(*refs):
        g = refs[:n]
        t = refs[n + has_small:2 * n + has_small]
        sems = refs[2 * n + 2 * has_small:]
        d2d_send, d2d_recv = sems[0], sems[1]
        x, y, c = _pos()
        sib = (x, y, 1 - c)
        drains = []
        for i, w in enumerate(ws):
            h = HALF[w]
            for qq in range(N_CHIPS):
                _rcopy(_rows(g[i], qq * SLAB[w] + (1 - c) * h, h), _rows(t[i], qq * h, h),
                       d2d_send.at[i], d2d_recv.at[i], sib).start()
            drains.append(_rcopy(t[i], t[i], d2d_send.at[i], d2d_recv.at[i], sib))
        if has_small:
            small_ref, sall_ref = refs[n], refs[2 * n + 1]
            sm_send, sm_recv, loc_sem = sems[2], sems[3], sems[4]
            me = 4 * x + 2 * y + c
            own_small = pltpu.make_async_copy(small_ref, sall_ref.at[me], loc_sem)
            own_small.start()
            for f in range(1, N_DEV):
                peer = (_flip(x, f & 4), _flip(y, f & 2), _flip(c, f & 1))
                cp = _rcopy(small_ref, sall_ref.at[me], sm_send.at[f - 1], sm_recv.at[f - 1], peer)
                cp.start()
                drains.append(cp)
        for d in drains:
            d.wait_recv()
        for d in drains:
            d.wait_send()
        if has_small:
            own_small.wait()

    out_shape = [_sds((N_CHIPS * HALF[w], D), gs[0].dtype) for w in ws]
    scratch = [pltpu.SemaphoreType.DMA((n,)), pltpu.SemaphoreType.DMA((n,))]
    if has_small:
        out_shape.append(_sds((N_DEV,) + small.shape, F32))
        scratch += [pltpu.SemaphoreType.DMA((N_DEV - 1,)), pltpu.SemaphoreType.DMA((N_DEV - 1,)), pltpu.SemaphoreType.DMA]
    return pl.pallas_call(
        body, name=name, in_specs=[ANY] * (n + has_small), out_specs=[ANY] * (n + has_small),
        out_shape=out_shape, scratch_shapes=scratch,
    )(*gs, *([small] if has_small else []))


def _halves_copies(ws, g, t, send_sems, recv_sems):
    x, y, c = _pos()
    sib = (x, y, 1 - c)
    cps = []
    for i, w in enumerate(ws):
        h = HALF[w]
        for qq in range(N_CHIPS):
            cps.append(_rcopy(_rows(g[i], qq * SLAB[w] + (1 - c) * h, h), _rows(t[i], qq * h, h),
                              send_sems.at[N_CHIPS * i + qq], recv_sems.at[N_CHIPS * i + qq], sib))
    return cps


def _halves_start(ws, gs, *, name):
    D = gs[0].shape[1]
    n = len(ws)
    bufs = list(gs) + [lax.empty((N_CHIPS * HALF[w], D), gs[0].dtype) for w in ws]

    def body(*refs):
        for cp in _halves_copies(ws, refs[:n], refs[n:2 * n], refs[2 * n], refs[2 * n + 1]):
            cp.start()
        refs[-1][...] = jnp.zeros_like(refs[-1])

    outs = pl.pallas_call(
        body, name=name, in_specs=[HBM] * (2 * n),
        out_specs=[SEM, SEM] + [HBM] * (2 * n) + [pl.BlockSpec(memory_space=pltpu.VMEM)],
        out_shape=[pltpu.SemaphoreType.DMA((N_CHIPS * n,)), pltpu.SemaphoreType.DMA((N_CHIPS * n,))]
        + [pltpu.HBM(b.shape, b.dtype) for b in bufs] + [TOKEN],
        input_output_aliases={i: 2 + i for i in range(2 * n)},
        compiler_params=pltpu.CompilerParams(has_side_effects=EFFECT),
    )(*[pltpu.with_memory_space_constraint(b, pltpu.HBM) for b in bufs])
    return dict(sems=outs[0:2], gs=outs[2:2 + n], theirs=outs[2 + n:2 + 2 * n], token=outs[-1])


def _halves_wait(ws, s, after, *, name):
    n = len(ws)

    def body(*refs):
        for cp in _halves_copies(ws, refs[:n], refs[n:2 * n], refs[2 * n], refs[2 * n + 1]):
            cp.wait_send()
            cp.wait_recv()

    bufs = list(s["gs"]) + list(s["theirs"])
    outs = pl.pallas_call(
        body, name=name, in_specs=[HBM] * (2 * n) + [SEM, SEM, ANY], out_specs=[HBM] * (2 * n),
        out_shape=[pltpu.HBM(b.shape, b.dtype) for b in bufs],
        input_output_aliases={i: i for i in range(2 * n)},
        compiler_params=pltpu.CompilerParams(has_side_effects=EFFECT),
    )(*bufs, *s["sems"], after)
    return outs[:n], outs[n:]


REDUCE_SPLIT = 2


def _chip_partial(ws, gs, theirs, *, name, out_dtype=F32):
    D = gs[0].shape[1]
    n = len(ws)

    def body(*refs):
        for i in range(n):
            refs[2 * n + i][...] = (refs[i][...].astype(F32) + refs[n + i][...].astype(F32)).astype(out_dtype)

    blk = [HALF[w] // REDUCE_SPLIT for w in ws]
    mine = [pl.BlockSpec((b, D), lambda qq, j: ((2 * qq + lax.axis_index("c")) * REDUCE_SPLIT + j, 0)) for b in blk]
    flat = [pl.BlockSpec((b, D), lambda qq, j: (qq * REDUCE_SPLIT + j, 0)) for b in blk]
    return pl.pallas_call(
        body, name=name, grid=(N_CHIPS, REDUCE_SPLIT), in_specs=mine + flat, out_specs=flat,
        out_shape=[_sds((N_CHIPS * HALF[w], D), out_dtype) for w in ws],
        compiler_params=_cp(("parallel", "parallel")),
    )(*gs, *theirs)


def _partial_copies(ws, part, got, send_sems, recv_sems):
    x, y, c = _pos()
    cps = []
    for k, (fx, fy) in enumerate(CHIP_FLIPS):
        peer = (_flip(x, fx), _flip(y, fy), c)
        qp = 2 * _flip(x, fx) + _flip(y, fy)
        for i, w in enumerate(ws):
            cps.append(_rcopy(_rows(part[i], qp * HALF[w], HALF[w]), _rows(got[i], k * HALF[w], HALF[w]),
                              send_sems.at[len(ws) * k + i], recv_sems.at[len(ws) * k + i], peer))
    return cps


def _send_chip_partials(ws, parts, *, name):
    D = parts[0].shape[1]
    n = len(ws)

    def body(*refs):
        cps = _partial_copies(ws, refs[:n], refs[n:2 * n], refs[2 * n], refs[2 * n + 1])
        for cp in cps:
            cp.start()
        for cp in cps:
            cp.wait_recv()
        for cp in cps:
            cp.wait_send()

    return pl.pallas_call(
        body, name=name, in_specs=[ANY] * n, out_specs=[ANY] * n,
        out_shape=[_sds((len(CHIP_FLIPS) * HALF[w], D), parts[0].dtype) for w in ws],
        scratch_shapes=[pltpu.SemaphoreType.DMA((len(CHIP_FLIPS) * n,)), pltpu.SemaphoreType.DMA((len(CHIP_FLIPS) * n,))],
    )(*parts)


def _send_start(ws, parts, *, name):
    D = parts[0].shape[1]
    n = len(ws)
    bufs = list(parts) + [lax.empty((len(CHIP_FLIPS) * HALF[w], D), parts[0].dtype) for w in ws]

    def body(*refs):
        send_sems, recv_sems = refs[2 * n], refs[2 * n + 1]
        for cp in _partial_copies(ws, refs[:n], refs[n:2 * n], send_sems, recv_sems):
            cp.start()
        refs[-1][...] = jnp.zeros_like(refs[-1])

    outs = pl.pallas_call(
        body, name=name, in_specs=[HBM] * (2 * n),
        out_specs=[SEM, SEM] + [HBM] * (2 * n) + [pl.BlockSpec(memory_space=pltpu.VMEM)],
        out_shape=[pltpu.SemaphoreType.DMA((len(CHIP_FLIPS) * n,)), pltpu.SemaphoreType.DMA((len(CHIP_FLIPS) * n,))]
        + [pltpu.HBM(b.shape, b.dtype) for b in bufs] + [TOKEN],
        input_output_aliases={i: 2 + i for i in range(2 * n)},
        compiler_params=pltpu.CompilerParams(has_side_effects=EFFECT),
    )(*[pltpu.with_memory_space_constraint(b, pltpu.HBM) for b in bufs])
    return dict(sems=outs[0:2], parts=outs[2:2 + n], got=outs[2 + n:2 + 2 * n], token=outs[-1])


def _send_wait(ws, s, after, *, name):
    n = len(ws)

    def body(*refs):
        for cp in _partial_copies(ws, refs[:n], refs[n:2 * n], refs[2 * n], refs[2 * n + 1]):
            cp.wait_send()
            cp.wait_recv()

    bufs = list(s["parts"]) + list(s["got"])
    outs = pl.pallas_call(
        body, name=name, in_specs=[HBM] * (2 * n) + [SEM, SEM] + [ANY] * len(after), out_specs=[HBM] * (2 * n),
        out_shape=[pltpu.HBM(b.shape, b.dtype) for b in bufs],
        input_output_aliases={i: i for i in range(2 * n)},
        compiler_params=pltpu.CompilerParams(has_side_effects=EFFECT),
    )(*bufs, *s["sems"], *after)
    return outs[:n], outs[n:]


def _chip_reduce(ws, parts, got, *, name, after=None):
    D = parts[0].shape[1]
    nk = len(CHIP_FLIPS)
    n = len(ws)
    extra = [] if after is None else [after]

    def body(*refs):
        refs = refs[len(extra):]
        outs = refs[(1 + nk) * n:]
        for i in range(n):
            acc = refs[i][...].astype(F32)
            for k in range(nk):
                acc = acc + refs[n * (1 + k) + i][...].astype(F32)
            outs[i][...] = acc

    blk = [HALF[w] // REDUCE_SPLIT for w in ws]

    def q_idx(j):
        return (2 * lax.axis_index("x") + lax.axis_index("y")) * REDUCE_SPLIT + j

    in_specs = [pl.BlockSpec((b, D), lambda j: (q_idx(j), 0)) for b in blk]
    for k in range(nk):
        in_specs += [pl.BlockSpec((b, D), functools.partial(lambda j, k: (k * REDUCE_SPLIT + j, 0), k=k)) for b in blk]
    out_specs = [pl.BlockSpec((b, D), lambda j: (lax.axis_index("c") * REDUCE_SPLIT + j, 0)) for b in blk]
    return pl.pallas_call(
        body, name=name, grid=(REDUCE_SPLIT,), in_specs=[ANY] * len(extra) + in_specs, out_specs=out_specs,
        out_shape=[_sds((SLAB[w], D), F32) for w in ws],
        compiler_params=_cp(("parallel",)),
    )(*extra, *parts, *[g for _ in range(nk) for g in got])


def _exchange_reduced(ws, shards, *, name):
    n = len(ws)

    def body(*refs):
        ins, outs = refs[:n], refs[n:2 * n]
        send_sems, recv_sems = refs[2 * n], refs[2 * n + 1]
        x, y, c = _pos()
        sib = (x, y, 1 - c)
        cps = []
        for i, w in enumerate(ws):
            cp = _rcopy(_rows(ins[i], c * HALF[w], HALF[w]), _rows(outs[i], c * HALF[w], HALF[w]),
                        send_sems.at[i], recv_sems.at[i], sib)
            cp.start()
            cps.append(cp)
        for cp in cps:
            cp.wait_recv()
        for cp in cps:
            cp.wait_send()

    return pl.pallas_call(
        body, name=name, in_specs=[ANY] * n, out_specs=[ANY] * n,
        out_shape=[_sds(s.shape, s.dtype) for s in shards], input_output_aliases={i: i for i in range(n)},
        scratch_shapes=[pltpu.SemaphoreType.DMA((n,)), pltpu.SemaphoreType.DMA((n,))],
    )(*shards)


def _adamw_fn(w, g, m, v):
    m2 = ADAM_B1 * m + (1.0 - ADAM_B1) * g
    v2 = ADAM_B2 * v + (1.0 - ADAM_B2) * (g * g)
    m_hat = m2 / (1.0 - ADAM_B1 ** ADAM_STEP)
    v_hat = v2 / (1.0 - ADAM_B2 ** ADAM_STEP)
    return -ADAM_LR * (m_hat / (jnp.sqrt(v_hat) + ADAM_EPS) + ADAM_WD * w), m2, v2


def _adamw(w, g, m, v, *, name):
    shp = _sds(w.shape, F32)
    rows = w.shape[0]
    tm = max(t for t in range(SUBLANES, 512 + 1, SUBLANES) if rows % t == 0)
    return _rowwise(lambda wv, gv, mv, vv: (gv, *_adamw_fn(wv, gv, mv, vv)), [_full(w), _full(g), _full(m), _full(v)], [],
                    [shp] * 4, [], name=name, tm=tm)


SMALL_SEGS = (("loss", 8), ("norm_mix_w", 8), ("b_attn", 8), ("lb_logits", 8), ("hg_norm_w", 8), ("sinks", 8),
              ("norm_ffn_w", 8), ("conv_w", 72), ("conv_b", 24), ("final_norm_w", 8))
SMALL_OFF = {n: sum(r for _, r in SMALL_SEGS[:i]) for i, (n, _) in enumerate(SMALL_SEGS)}
SMALL_ROWS = sum(r for _, r in SMALL_SEGS)
LANES = 128


def _pack_small(parts):
    segs = []
    for n, r in SMALL_SEGS:
        a = parts.get(n)
        flat = jnp.zeros((0,), F32) if a is None else a.reshape(-1).astype(F32)
        segs.append(jnp.pad(flat, (0, r * LANES - flat.shape[0])).reshape(r, LANES))
    return jnp.concatenate(segs, axis=0)


def _unpack_small(pack, n, shape):
    size = math.prod(shape)
    r0 = SMALL_OFF[n]
    return pack[r0:r0 + dict(SMALL_SEGS)[n]].reshape(-1)[:size].reshape(shape)


def _small_update(sall, wp, mp, vp, *, after):
    R = SMALL_ROWS
    r_lb = SMALL_OFF["lb_logits"]

    def body(after_ref, s_ref, w_ref, m_ref, v_ref, g_ref, d_ref, m2_ref, v2_ref, loss_ref):
        g = s_ref[0]
        for i in range(1, N_DEV):
            g = g + s_ref[i]
        tot = jnp.sum(jnp.sum(g[0:8], axis=1, keepdims=True), axis=0, keepdims=True)
        loss_ref[...] = jnp.broadcast_to(tot, loss_ref.shape)
        lg = w_ref[r_lb:r_lb + 8, :]
        p0 = _sigmoid(lg - pltpu.roll(lg, 4, 0))
        d = g[r_lb:r_lb + 8]
        d = d + pltpu.roll(d, 4, 0)
        sign = jnp.where(lax.broadcasted_iota(jnp.int32, d.shape, 0) < 4, 1.0, -1.0)
        g = jnp.concatenate([g[:r_lb], sign * d * p0 * (1.0 - p0), g[r_lb + 8:]], axis=0)
        g_ref[...] = g
        d_ref[...], m2_ref[...], v2_ref[...] = _adamw_fn(w_ref[...], g, m_ref[...], v_ref[...])

    full = pl.BlockSpec((R, LANES), lambda: (0, 0))
    return pl.pallas_call(
        body, name="small_update",
        in_specs=[ANY, pl.BlockSpec((N_DEV, R, LANES), lambda: (0, 0, 0)), full, full, full],
        out_specs=[full, full, full, full, pl.BlockSpec((8, LANES), lambda: (0, 0))],
        out_shape=[_sds((R, LANES), F32)] * 4 + [_sds((8, LANES), F32)],
        compiler_params=_cp(),
    )(after, sall, wp, mp, vp)


def _lb_fwd(lb_logits):
    n = lb_logits.shape[1]

    def body(l_ref, o_ref):
        o_ref[...] = _sigmoid(l_ref[0:1, :] - l_ref[1:2, :])

    return pl.pallas_call(body, name="lb_fwd", out_shape=jax.ShapeDtypeStruct((1, n), F32), compiler_params=_cp())(lb_logits)


class _MeshExchange:
    def __init__(self, pack, cw8):
        self.gather = _gather_start(pack, cw8)
        self.sent = None
        self.conv_w8 = None

    def start(self):
        return self.gather["token"]

    def w_in(self, after):
        self.pack, l_in = _gather_wait_in(self.gather, after)
        return (_forward_in(l_in), N_CHIPS * SLAB[0], 0)

    def mid(self, after):
        l_ffn, l_out, l_cw = _gather_wait_rest(self.gather, self.pack, after)
        self.conv_w8 = jnp.concatenate([l_cw[i] for i in range(N_CHIPS)], axis=1)
        self.passing = _forward_rest_start(l_ffn, l_out)
        return self.passing["token"]

    def rest(self, after):
        l_ffn, l_out = _forward_rest_wait(self.passing, after)
        rows = N_CHIPS * SLAB[FFN_W[0]]
        return dict(w_gate_t=(l_ffn, rows, 0), w_up_t=(l_ffn, rows, 1), w_down=(l_ffn, rows, 2),
                    w_out=(l_out, N_CHIPS * SLAB[4], 0), conv_w8=self.conv_w8)

    def ffn_grads(self, gs):
        self.swap = _halves_start(FFN_W, gs, name="halves_ffn_start")
        return self.swap["token"]

    def ffn_grads_send(self, after):
        gs, theirs = _halves_wait(FFN_W, self.swap, after, name="halves_ffn_wait")
        parts = _chip_partial(FFN_W, gs, theirs, name="chip_partial_ffn", out_dtype=BF16)
        self.sent = _send_start(FFN_W, parts, name="send_ffn_start")
        return self.sent["token"]


def kernel(x, norm_mix_w, w_in, b_attn, lb_logits, hg_norm_w, sinks, w_out, norm_ffn_w, w_gate, w_up, conv_w, conv_b, w_down, final_norm_w, loss_target, m_norm_mix_w, m_w_in, m_b_attn, m_lb_logits, m_hg_norm_w, m_sinks, m_w_out, m_norm_ffn_w, m_w_gate, m_w_up, m_conv_w, m_conv_b, m_w_down, m_final_norm_w, v_norm_mix_w, v_w_in, v_b_attn, v_lb_logits, v_hg_norm_w, v_sinks, v_w_out, v_norm_ffn_w, v_w_gate, v_w_up, v_conv_w, v_conv_b, v_w_down, v_final_norm_w):
    D = D_MODEL
    q = 2 * lax.axis_index("x") + lax.axis_index("y")
    ccols = D_FF // N_CHIPS

    pack = jnp.concatenate([w_in[0].T, w_gate[0].T, w_up[0].T, w_down[0], w_out[0]], axis=0).astype(BF16)
    cw8 = jnp.concatenate([conv_w[0], jnp.zeros((SUBLANES - 3, ccols), F32)], axis=0)
    ex = _MeshExchange(pack, cw8)
    p = dict(norm_mix_w=norm_mix_w, b_attn=b_attn, lb=_lb_fwd(lb_logits), hg_norm_w=hg_norm_w, sinks=sinks,
             norm_ffn_w=norm_ffn_w, conv_b=conv_b, final_norm_w=final_norm_w.reshape(1, D))
    loss_cols, dx, g = _local_step(x[0], loss_target[0], p, ex)
    conv_w8 = ex.conv_w8

    small = _pack_small(dict(loss=loss_cols, norm_mix_w=g["norm_mix_w"], b_attn=g["b_attn"], lb_logits=g["lb"],
                             hg_norm_w=g["hg_norm_w"], sinks=g["sinks8"], norm_ffn_w=g["norm_ffn_w"],
                             conv_w=g["conv_w8"][:3], conv_b=g["conv_b"], final_norm_w=g["final_norm_w"]))
    parts_ffn, got_ffn = _send_wait(FFN_W, ex.sent, [dx], name="send_ffn_wait")
    late = (0, 4)
    gs = [g["g_in_t"], g["g_out"]]
    *theirs, sall = _exchange_halves(late, gs, small, name="exchange_halves_late")
    parts_late = _chip_partial(late, gs, theirs, name="chip_partial_late", out_dtype=BF16)
    sent_late = _send_start(late, parts_late, name="send_late_start")
    big = {}

    def finish(ws, parts, got, specs, tag, after):
        shards = _exchange_reduced(ws, _chip_reduce(ws, parts, got, name="chip_reduce_" + tag, after=after),
                                   name="exchange_reduced_" + tag)
        deltas = []
        for gw, (n, w, m, v, tr) in zip(shards, specs):
            view = (lambda a: a[0].T) if tr else (lambda a: a[0])
            back = (lambda a: a.T[None]) if tr else (lambda a: a[None])
            res = _adamw(view(w), gw, view(m), view(v), name="adamw_" + n)
            big[n] = tuple(back(r) for r in res)
            deltas.append(res[1])
        return deltas

    done_ffn = finish(FFN_W, parts_ffn, got_ffn, (("w_gate", w_gate, m_w_gate, v_w_gate, True),
                                                  ("w_up", w_up, m_w_up, v_w_up, True),
                                                  ("w_down", w_down, m_w_down, v_w_down, False)), "ffn", sent_late["token"])

    def place(a):
        return lax.dynamic_update_slice(jnp.zeros((3, D_FF), F32), a[0], (0, q * ccols))

    def small_pack(ws, cw):
        nm, ba, lbl, hg, sk, nf, cb, fn = ws
        return _pack_small(dict(norm_mix_w=nm, b_attn=ba, lb_logits=lbl, hg_norm_w=hg,
                                sinks=jnp.broadcast_to(sk.reshape(ATT_HEADS, 1), (ATT_HEADS, LANES)), norm_ffn_w=nf,
                                conv_w=cw, conv_b=cb, final_norm_w=fn))

    wp = small_pack((norm_mix_w, b_attn, lb_logits, hg_norm_w, sinks, norm_ffn_w, conv_b, final_norm_w), conv_w8[:3])
    mp = small_pack((m_norm_mix_w, m_b_attn, m_lb_logits, m_hg_norm_w, m_sinks, m_norm_ffn_w, m_conv_b, m_final_norm_w),
                    place(m_conv_w))
    vp = small_pack((v_norm_mix_w, v_b_attn, v_lb_logits, v_hg_norm_w, v_sinks, v_norm_ffn_w, v_conv_b, v_final_norm_w),
                    place(v_conv_w))
    outs = _small_update(sall, wp, mp, vp, after=sent_late["token"])
    loss = outs[4][0, 0]
    parts_late, got_late = _send_wait(late, sent_late, [*done_ffn, outs[4]], name="send_late_wait")
    finish(late, parts_late, got_late, (("w_in", w_in, m_w_in, v_w_in, True), ("w_out", w_out, m_w_out, v_w_out, False)),
           "late", None)

    def small_out(pk, n, ref):
        if n == "sinks":
            return pk[SMALL_OFF[n]:SMALL_OFF[n] + ATT_HEADS, 0].reshape(ref.shape)
        if n == "conv_w":
            full = _unpack_small(pk, n, (3, D_FF))
            return lax.dynamic_slice(full, (0, q * ccols), (3, ccols))[None]
        return _unpack_small(pk, n, ref.shape)

    refs = dict(norm_mix_w=norm_mix_w, b_attn=b_attn, lb_logits=lb_logits, hg_norm_w=hg_norm_w, sinks=sinks,
                norm_ffn_w=norm_ffn_w, conv_w=conv_w, conv_b=conv_b, final_norm_w=final_norm_w)
    order = ("norm_mix_w", "w_in", "b_attn", "lb_logits", "hg_norm_w", "sinks", "w_out", "norm_ffn_w", "w_gate", "w_up",
             "conv_w", "conv_b", "w_down", "final_norm_w")
    res = [loss, dx[None]]
    for k in range(4):
        for n in order:
            res.append(big[n][k] if n in big else small_out(outs[k], n, refs[n]))
    return tuple(res)
```

```python
import functools
import math

import jax
import jax.numpy as jnp
from jax import lax
from jax.experimental import pallas as pl
from jax.experimental.pallas import tpu as pltpu

F32 = jnp.float32
BF16 = jnp.bfloat16

D_MODEL = 1024
HG_HEADS = 4
HG_DK = 128
HG_W = HG_HEADS * HG_DK
HG_CHUNK = 64
HG_SUB = 8
HG_FWD_CHUNKS_PER_STEP = 4
HG_CHUNKS_PER_STEP = 2
ATT_HEADS = 8
ATT_KV = 2
ATT_GROUP = ATT_HEADS // ATT_KV
ATT_HD = 64
ATT_BLOCK = 128
ATT_Q_W = ATT_HEADS * ATT_HD
ATT_KV_W = ATT_KV * ATT_HD
ATT_COLS = ATT_Q_W + 2 * ATT_KV_W
IN_COLS = 4 * HG_W + ATT_COLS
D_FF = 2816
EPS = 1e-6
ADAM_LR, ADAM_B1, ADAM_B2, ADAM_EPS, ADAM_WD, ADAM_STEP = 0.001, 0.9, 0.999, 1e-08, 0.01, 10
NEG = -1e30

V7X_VMEM_BYTES = 64 * 1024 * 1024
VMEM_LIMIT = 48 * 1024 * 1024
SUBLANES = 8

N_CHIPS = 4


def _cp(sem=None, **kw):
    return pltpu.CompilerParams(dimension_semantics=sem, vmem_limit_bytes=VMEM_LIMIT, **kw)


def _sds(shape, dtype):
    return jax.ShapeDtypeStruct(shape, dtype)


TOKEN = jax.ShapeDtypeStruct((8, 128), jnp.float32)


def _wspec(w):
    arr, rows, blk = w
    return pl.BlockSpec((rows, arr.shape[1]), lambda i: (blk, 0))


def _mm_nt(a, w, *, splits, out_dtype, name, after=None, tm=512):
    M, K = a.shape
    N = w[1]
    tm = min(tm, M)
    assert sum(splits) == N and M % tm == 0
    offs = [sum(splits[:i]) for i in range(len(splits))]
    n_in = 2 if after is None else 3

    def body(*refs):
        a_ref, w_ref = refs[0], refs[1]
        acc = lax.dot_general(a_ref[...], w_ref[...], (((1,), (1,)), ((), ())), preferred_element_type=F32)
        for o_ref, c0, n in zip(refs[n_in:], offs, splits):
            o_ref[...] = acc[:, c0:c0 + n].astype(out_dtype)

    in_specs = [pl.BlockSpec((tm, K), lambda i: (i, 0)), _wspec(w)]
    args = [a, w[0]]
    if after is not None:
        in_specs.append(pl.BlockSpec(memory_space=pl.ANY))
        args.append(after)
    outs = pl.pallas_call(
        body, name=name, grid=(M // tm,), in_specs=in_specs,
        out_specs=[pl.BlockSpec((tm, n), lambda i: (i, 0)) for n in splits],
        out_shape=[_sds((M, n), out_dtype) for n in splits],
        compiler_params=_cp(("parallel",)),
    )(*args)
    return outs


def _mm_nn(pieces, ws, *, name, out_dtype=F32, residual=None, epilogue=None, prologue=None, after=None,
           w_transposed=False, tm=512):
    pro_fn, pro_rows, pro_bc, pro_out = prologue or (None, [], [], None)
    if prologue is not None:
        assert pieces is None and len(ws) == 1
        pieces = [[pro_out]]
    M = pieces[0][0].shape[0]
    K = ws[0][1] if w_transposed else ws[0][0].shape[1]
    tm = min(tm, M)
    flat = [] if prologue is not None else [p for grp in pieces for p in grp]
    n_p = len(flat)
    n_w = len(ws)
    n_pr, n_pb = len(pro_rows), len(pro_bc)
    fn, row_ins, bc_ins, row_outs, acc_outs = epilogue or (None, [], [], [_sds((M, K), out_dtype)], [])
    if residual is not None:
        assert epilogue is None
        row_ins = [residual]
    n_r, n_b, n_o = len(row_ins), len(bc_ins), len(row_outs)
    lead = [] if after is None else [after]

    def body(*refs):
        refs = refs[len(lead):]
        p_refs = refs[:n_p]
        w_refs = refs[n_p:n_p + n_w]
        extra = [r[...] for r in refs[n_p + n_w:n_p + n_w + n_r + n_b]]
        base = n_p + n_w + n_r + n_b
        pro = [r[...] for r in refs[base:base + n_pr + n_pb]]
        base += n_pr + n_pb
        o_refs = refs[base:base + n_o]
        a_refs = refs[base + n_o:base + n_o + len(acc_outs)]
        if pro_fn is not None:
            lhs = pro_fn(*pro).astype(pro_out.dtype)
            refs[-1][...] = lhs
            tiles = [lhs]
        else:
            tiles = [r[...] for r in p_refs]
        acc = None
        k = 0
        for gi, grp in enumerate(pieces):
            c0 = 0
            for p in grp:
                n = p.shape[1]
                if w_transposed:
                    t = lax.dot_general(tiles[k], w_refs[gi][...], (((1,), (1,)), ((), ())), preferred_element_type=F32)
                else:
                    t = jnp.dot(tiles[k], w_refs[gi][c0:c0 + n, :], preferred_element_type=F32)
                acc = t if acc is None else acc + t
                c0 += n
                k += 1
        if fn is None:
            res = (acc + extra[0] if residual is not None else acc,)
        else:
            res = fn(acc, *extra)
        for o_ref, val in zip(o_refs, res[:n_o]):
            o_ref[...] = val.astype(o_ref.dtype)
        if acc_outs:
            @pl.when(pl.program_id(0) == 0)
            def _():
                for a_ref in a_refs:
                    a_ref[...] = jnp.zeros_like(a_ref)
            for a_ref, val in zip(a_refs, res[n_o:]):
                a_ref[...] += val

    in_specs = [pl.BlockSpec((tm, p.shape[1]), lambda i: (i, 0)) for p in flat]
    in_specs += [_wspec(w) for w in ws]
    in_specs += [pl.BlockSpec((tm, r.shape[1]), lambda i: (i, 0)) for r in row_ins]
    in_specs += [pl.BlockSpec(b.shape, lambda i: (0, 0)) for b in bc_ins]
    in_specs += [pl.BlockSpec((tm, r.shape[1]), lambda i: (i, 0)) for r in pro_rows]
    in_specs += [pl.BlockSpec(b.shape, lambda i: (0, 0)) for b in pro_bc]
    out_specs = [pl.BlockSpec((tm, s.shape[1]), lambda i: (i, 0)) for s in row_outs]
    out_specs += [pl.BlockSpec(s.shape, lambda i: (0, 0)) for s in acc_outs]
    pro_outs = [] if prologue is None else [pro_out]
    out_specs += [pl.BlockSpec((tm, s.shape[1]), lambda i: (i, 0)) for s in pro_outs]
    outs = pl.pallas_call(
        body, name=name, grid=(M // tm,), in_specs=[pl.BlockSpec(memory_space=pl.ANY)] * len(lead) + in_specs,
        out_specs=out_specs, out_shape=list(row_outs) + list(acc_outs) + pro_outs,
        compiler_params=_cp(("arbitrary",) if acc_outs else ("parallel",)),
    )(*lead, *flat, *[w[0] for w in ws], *row_ins, *bc_ins, *pro_rows, *pro_bc)
    return outs if (epilogue is not None or prologue is not None) else outs[0]


def _mm_tn(pieces, x, *, name, out_dtype=BF16, tt=1024):
    M, K = x.shape
    tt = min(tt, M)
    ns = [p.shape[1] for p in pieces]
    offs = [sum(ns[:i]) for i in range(len(ns))]
    N = sum(ns)
    n_p = len(pieces)
    last = M // tt - 1

    def body(*refs):
        p_refs = refs[:n_p]
        x_ref = refs[n_p]
        o_ref, acc_ref = refs[n_p + 1], refs[n_p + 2]

        @pl.when(pl.program_id(0) == 0)
        def _():
            acc_ref[...] = jnp.zeros_like(acc_ref)

        xv = x_ref[...]
        for p_ref, c0, n in zip(p_refs, offs, ns):
            acc_ref[c0:c0 + n, :] += lax.dot_general(p_ref[...], xv, (((0,), (0,)), ((), ())),
                                                      preferred_element_type=F32)

        @pl.when(pl.program_id(0) == last)
        def _():
            o_ref[...] = acc_ref[...].astype(o_ref.dtype)

    in_specs = [pl.BlockSpec((tt, n), lambda i: (i, 0)) for n in ns]
    in_specs.append(pl.BlockSpec((tt, K), lambda i: (i, 0)))
    return pl.pallas_call(
        body, name=name, grid=(M // tt,), in_specs=in_specs,
        out_specs=pl.BlockSpec((N, K), lambda i: (0, 0)),
        out_shape=_sds((N, K), out_dtype),
        scratch_shapes=[pltpu.VMEM((N, K), F32)],
        compiler_params=_cp(("arbitrary",)),
    )(*pieces, x)


def _rms_fwd(xf, w):
    inv = lax.rsqrt(jnp.mean(xf * xf, axis=-1, keepdims=True) + EPS)
    return xf * inv * w


def _rms_bwd(xf, w, dy):
    inv = lax.rsqrt(jnp.mean(xf * xf, axis=-1, keepdims=True) + EPS)
    xhat = xf * inv
    dxhat = dy * w
    dx = inv * (dxhat - xhat * jnp.mean(dxhat * xhat, axis=-1, keepdims=True))
    dw = jnp.sum(dy * xhat, axis=0, keepdims=True)
    return dx, dw


def _sigmoid(x):
    return 1.0 / (1.0 + jnp.exp(-x))


def _rowwise(fn, row_ins, bc_ins, row_outs, acc_outs, *, name, tm=256, after=None):
    M = row_outs[0].shape[0] if row_outs else row_ins[0][0].shape[0]
    assert M % tm == 0 and tm % SUBLANES == 0, (name, M, tm)
    n_r, n_b, n_o, n_a = len(row_ins), len(bc_ins), len(row_outs), len(acc_outs)
    n_after = 0 if after is None else 1

    def body(*refs):
        refs = refs[n_after:]
        ins = [r[...] for r in refs[:n_r + n_b]]
        o_refs = refs[n_r + n_b:n_r + n_b + n_o]
        a_refs = refs[n_r + n_b + n_o:]
        res = fn(*ins)
        for o_ref, val in zip(o_refs, res[:n_o]):
            o_ref[...] = val.astype(o_ref.dtype)
        if n_a:
            @pl.when(pl.program_id(0) == 0)
            def _():
                for a_ref in a_refs:
                    a_ref[...] = jnp.zeros_like(a_ref)
            for a_ref, val in zip(a_refs, res[n_o:]):
                a_ref[...] += val

    in_specs = [pl.BlockSpec((tm, cw), functools.partial(lambda i, cb, r0: (i + r0, cb), cb=cb, r0=r0))
                for (_, cw, cb, r0) in row_ins]
    in_specs += [pl.BlockSpec(b.shape, lambda i: (0, 0)) for b in bc_ins]
    out_specs = [pl.BlockSpec((tm, s.shape[1]), lambda i: (i, 0)) for s in row_outs]
    out_specs += [pl.BlockSpec(s.shape, lambda i: (0, 0)) for s in acc_outs]
    if n_after:
        in_specs = [pl.BlockSpec(memory_space=pl.ANY)] + in_specs
    return pl.pallas_call(
        body, name=name, grid=(M // tm,), in_specs=in_specs, out_specs=out_specs,
        out_shape=list(row_outs) + list(acc_outs),
        compiler_params=_cp(("arbitrary",) if n_a else ("parallel",)),
    )(*([after] if n_after else []), *[r[0] for r in row_ins], *bc_ins)


def _full(a, first_row_block=0):
    return (a, a.shape[1], 0, first_row_block)


def _conv_rows(ext, w_ref_val, lo):
    s1 = pltpu.roll(ext, 1, 0)
    s2 = pltpu.roll(ext, 2, 0)
    y = w_ref_val[0:1, :] * s2 + w_ref_val[1:2, :] * s1 + w_ref_val[2:3, :] * ext
    return y[SUBLANES:, :]


def _ffn_in(v, w_gate, w_up, conv_w8, conv_b, *, name, tm=256):
    T, K = v.shape
    N = w_gate[1]
    tm = min(tm, T)

    def body(v_ref, wg_ref, wu_ref, cw_ref, cb_ref, gp_ref, up_ref, gate_ref, act_ref, carry_sc):
        @pl.when(pl.program_id(0) == 0)
        def _():
            carry_sc[...] = jnp.zeros_like(carry_sc)

        vv = v_ref[...]
        dn = (((1,), (1,)), ((), ()))
        gp = lax.dot_general(vv, wg_ref[...], dn, preferred_element_type=F32)
        up = lax.dot_general(vv, wu_ref[...], dn, preferred_element_type=F32)
        gp_ref[...] = gp.astype(gp_ref.dtype)
        up_ref[...] = up.astype(up_ref.dtype)
        gate = _conv_rows(jnp.concatenate([carry_sc[...], gp], axis=0), cw_ref[...], 0) + cb_ref[...]
        gate_ref[...] = gate
        act_ref[...] = (gate * _sigmoid(gate) * up).astype(act_ref.dtype)
        carry_sc[...] = gp[tm - SUBLANES:, :]

    tile = pl.BlockSpec((tm, N), lambda i: (i, 0))
    return pl.pallas_call(
        body, name=name, grid=(T // tm,),
        in_specs=[pl.BlockSpec((tm, K), lambda i: (i, 0)), _wspec(w_gate), _wspec(w_up),
                  pl.BlockSpec((SUBLANES, N), lambda i: (0, 0)), pl.BlockSpec((1, N), lambda i: (0, 0))],
        out_specs=[tile] * 4,
        out_shape=[_sds((T, N), BF16), _sds((T, N), BF16), _sds((T, N), F32), _sds((T, N), BF16)],
        scratch_shapes=[pltpu.VMEM((SUBLANES, N), F32)],
        compiler_params=_cp(("arbitrary",)),
    )(v, w_gate[0], w_up[0], conv_w8, conv_b)


def _ffn_back(dh2, w_down, gp, up, gate, conv_w8, *, name, tr=512, tc=1408):
    T, C = gp.shape
    K = dh2.shape[1]
    warr, _, wblk = w_down
    tr = min(tr, T)
    nr = T // tr
    ncb = C // tc

    def body(dh_ref, wd_ref, gp_ref, up_ref, gate_ref, w_ref, dgp_ref, dup_ref, dw_ref, db_ref, carry_sc):
        @pl.when(pl.program_id(1) == 0)
        def _():
            carry_sc[...] = jnp.zeros_like(carry_sc)
            dw_ref[...] = jnp.zeros_like(dw_ref)
            db_ref[...] = jnp.zeros_like(db_ref)

        w = w_ref[...]
        dact = lax.dot_general(dh_ref[...], wd_ref[...], (((1,), (1,)), ((), ())), preferred_element_type=F32)
        gpc = gp_ref[...].astype(F32)
        gate = gate_ref[...]
        sg = _sigmoid(gate)
        silu = gate * sg
        dup_ref[...] = (dact * silu).astype(dup_ref.dtype)
        dgate = dact * up_ref[...].astype(F32) * (sg + silu * (1.0 - sg))
        ext = jnp.concatenate([dgate, carry_sc[...]], axis=0)
        n = tr + SUBLANES
        g1 = pltpu.roll(ext, n - 1, 0)[:tr]
        g2 = pltpu.roll(ext, n - 2, 0)[:tr]
        dgp_ref[...] = (w[2:3, :] * dgate + w[1:2, :] * g1 + w[0:1, :] * g2).astype(dgp_ref.dtype)
        dw0 = jnp.sum(gpc * g2, axis=0, keepdims=True)
        dw1 = jnp.sum(gpc * g1, axis=0, keepdims=True)
        dw2 = jnp.sum(gpc * dgate, axis=0, keepdims=True)
        z = jnp.zeros((SUBLANES - 3, gpc.shape[1]), F32)
        dw_ref[...] += jnp.concatenate([dw0, dw1, dw2, z], axis=0)
        db_ref[...] += jnp.sum(dgate, axis=0, keepdims=True)
        carry_sc[...] = dgate[:SUBLANES]

    rev = lambda i: nr - 1 - i
    cur = pl.BlockSpec((tr, tc), lambda j, i: (rev(i), j))
    return pl.pallas_call(
        body, name=name, grid=(ncb, nr),
        in_specs=[pl.BlockSpec((tr, K), lambda j, i: (rev(i), 0)),
                  pl.BlockSpec((tc, K), lambda j, i: (wblk * ncb + j, 0)),
                  cur, cur, cur,
                  pl.BlockSpec((SUBLANES, tc), lambda j, i: (0, j))],
        out_specs=[cur, cur,
                   pl.BlockSpec((SUBLANES, tc), lambda j, i: (0, j)),
                   pl.BlockSpec((1, tc), lambda j, i: (0, j))],
        out_shape=[_sds((T, C), BF16), _sds((T, C), BF16), _sds((SUBLANES, C), F32), _sds((1, C), F32)],
        scratch_shapes=[pltpu.VMEM((SUBLANES, tc), F32)],
        compiler_params=_cp(("parallel", "arbitrary")),
    )(dh2, warr, gp, up, gate, conv_w8)


def _cumsum_rows(x):
    n = x.shape[0]
    row = lax.broadcasted_iota(jnp.int32, x.shape, 0)
    s = 1
    while s < n:
        x = x + jnp.where(row >= s, pltpu.roll(x, s, 0), 0.0)
        s *= 2
    return x


def _rcumsum_rows(x):
    n = x.shape[0]
    row = lax.broadcasted_iota(jnp.int32, x.shape, 0)
    s = 1
    while s < n:
        x = x + jnp.where(row < n - s, pltpu.roll(x, n - s, 0), 0.0)
        s *= 2
    return x


def _dot_nt(a, b):
    return lax.dot_general(a.astype(BF16), b.astype(BF16), (((1,), (1,)), ((), ())), preferred_element_type=F32)


def _dot_tn(a, b):
    return lax.dot_general(a.astype(BF16), b.astype(BF16), (((0,), (0,)), ((), ())), preferred_element_type=F32)


def _dot_nn(a, b):
    return jnp.dot(a.astype(BF16), b.astype(BF16), preferred_element_type=F32)


def _dot3(a, b, contract):
    def split(x):
        hi = x.astype(BF16)
        return hi, (x - hi.astype(F32)).astype(BF16)

    a_hi, a_lo = split(a)
    b_hi, b_lo = split(b)
    dot = lambda x, y: lax.dot_general(x, y, (contract, ((), ())), preferred_element_type=F32)
    return dot(a_hi, b_hi) + (dot(a_hi, b_lo) + dot(a_lo, b_hi))


NT, TN, NN = ((1,), (1,)), ((0,), (0,)), ((1,), (0,))


def _hg_gates(hq, hf, lbv):
    sig = _sigmoid(hf)
    f = lbv + (1.0 - lbv) * sig
    return sig, f, jnp.log(f), 1.0 - f, hq * (HG_DK ** -0.5)


def _hg_sel_rows(ref, sp):
    return jnp.concatenate(
        [jnp.broadcast_to(ref[pl.ds(HG_SUB * i + sp, 1), :], (HG_SUB, HG_DK)) for i in range(HG_CHUNK // HG_SUB)], axis=0)


def _hg_masks():
    C = HG_CHUNK
    row = lax.broadcasted_iota(jnp.int32, (C, C), 0)
    col = lax.broadcasted_iota(jnp.int32, (C, C), 1)
    d = col - (row // HG_SUB) * HG_SUB
    tmod = row % HG_SUB
    diag_valid = jnp.logical_and(d >= 0, d <= tmod)
    return row, col, d, diag_valid


def _hg_strip_keys(k, b, r, n):
    ek = jnp.exp(r - b[:n])
    return ek, jnp.concatenate([k[:n] * ek, jnp.zeros((HG_CHUNK - n, k.shape[1]), F32)], axis=0)


def _hg_scores(q, k, b, b_sc, k_sc):
    C, S = HG_CHUNK, HG_SUB
    row, col, d, diag_valid = _hg_masks()
    blocks = [jnp.zeros((S, C), F32)]
    for i in range(1, C // S):
        r = b_sc[pl.ds(S * i - 1, 1), :]
        qi = q[S * i:S * (i + 1)] * jnp.exp(b[S * i:S * (i + 1)] - r)
        blocks.append(_dot_nt(qi, _hg_strip_keys(k, b, r, S * i)[1]))
    a_off = jnp.concatenate(blocks, axis=0)
    a_d = jnp.zeros((C, C), F32)
    for sp in range(S):
        bs = _hg_sel_rows(b_sc, sp)
        ks = _hg_sel_rows(k_sc, sp)
        e = jnp.exp(jnp.minimum(b - bs, 0.0))
        colv = jnp.sum(q * ks * e, axis=-1, keepdims=True)
        a_d = jnp.where(d == sp, colv, a_d)
    return a_off + jnp.where(diag_valid, a_d, 0.0)


def _hg_prep(hq_v, hf_v, lbv, b_sc, k_sc):
    sig, f, g, k, q = _hg_gates(hq_v, hf_v, lbv)
    b = _cumsum_rows(g)
    b_sc[...] = b
    k_sc[...] = k
    return sig, f, k, q, b, b_sc[pl.ds(HG_CHUNK - 1, 1), :]


def _hgrn_fwd(hq, hf, hi, lb, *, name):
    T = hq.shape[0]
    C, H, K = HG_CHUNK, HG_HEADS, HG_DK
    NC = T // C

    def body(hq_ref, hf_ref, hi_ref, lb_ref, o_ref, st_ref, s_sc, b_sc, k_sc):
        @pl.when(pl.program_id(0) == 0)
        def _():
            s_sc[...] = jnp.zeros_like(s_sc)

        st_all = s_sc[...]
        for j in range(P):
            rows = slice(C * j, C * (j + 1))
            st_ref[j] = st_all
            outs, news = [], []
            for h in range(H):
                sl = slice(K * h, K * (h + 1))
                _, _, k, q, b, bc = _hg_prep(hq_ref[rows, sl], hf_ref[rows, sl], lb_ref[:, sl], b_sc.at[j, h], k_sc.at[j, h])
                v = hi_ref[rows, sl]
                st0 = st_all[:, sl]
                a = _hg_scores(q, k, b, b_sc.at[j, h], k_sc.at[j, h])
                outs.append(_dot_nn(a, v) + _dot_nt(q * jnp.exp(b), st0))
                news.append(st0 * jnp.exp(bc) + _dot_tn(v, k * jnp.exp(bc - b)))
            o_ref[rows, :] = jnp.concatenate(outs, axis=1)
            st_all = jnp.concatenate(news, axis=1)
        s_sc[...] = st_all

    P = HG_FWD_CHUNKS_PER_STEP
    blk = pl.BlockSpec((P * C, H * K), lambda c: (c, 0))
    return pl.pallas_call(
        body, name=name, grid=(NC // P,),
        in_specs=[blk, blk, blk, pl.BlockSpec((1, H * K), lambda c: (0, 0))],
        out_specs=[blk, pl.BlockSpec((P, K, H * K), lambda c: (c, 0, 0))],
        out_shape=[_sds((T, H * K), F32), _sds((NC, K, H * K), F32)],
        scratch_shapes=[pltpu.VMEM((K, H * K), F32), pltpu.VMEM((P, H, C, K), F32), pltpu.VMEM((P, H, C, K), F32)],
        compiler_params=_cp(("arbitrary",)),
    )(hq, hf, hi, lb)


def _hgrn_bwd(hq, hf, hi, lb, states, do, *, name):
    T = hq.shape[0]
    C, H, K, S = HG_CHUNK, HG_HEADS, HG_DK, HG_SUB
    NC = T // C

    def intra_slow(q, k, b, da, b_sc, k_sc):
        row, col, d, diag_valid = _hg_masks()
        a_blocks = [jnp.zeros((S, C), F32)]
        dq_blocks = [jnp.zeros((S, K), F32)]
        dk = jnp.zeros((C, K), F32)
        for i in range(1, C // S):
            r = b_sc[pl.ds(S * i - 1, 1), :]
            eq = jnp.exp(b[S * i:S * (i + 1)] - r)
            ek = jnp.exp(jnp.minimum(r - b, 0.0))
            qi = q[S * i:S * (i + 1)] * eq
            kk = k * ek
            a_blocks.append(_dot_nt(qi, kk))
            dai = jnp.where(col[S * i:S * (i + 1)] < S * i, da[S * i:S * (i + 1)], 0.0)
            dq_blocks.append(_dot_nn(dai, kk) * eq)
            dk = dk + _dot_tn(dai, qi) * ek
        dq = jnp.concatenate(dq_blocks, axis=0)
        a_off = jnp.where(col < (row // S) * S, jnp.concatenate(a_blocks, axis=0), 0.0)
        same_blk = (row // S == col // S).astype(BF16)
        tmod = (lax.broadcasted_iota(jnp.int32, (C, K), 0)) % S
        a_d = jnp.zeros((C, C), F32)
        dk_d = jnp.zeros((C, K), F32)
        for sp in range(S):
            bs = _hg_sel_rows(b_sc, sp)
            ks = _hg_sel_rows(k_sc, sp)
            e = jnp.exp(jnp.minimum(b - bs, 0.0))
            eks = e * ks
            a_d = jnp.where(d == sp, jnp.sum(q * eks, axis=-1, keepdims=True), a_d)
            dacol = jnp.sum(jnp.where(d == sp, da, 0.0), axis=-1, keepdims=True)
            dq = dq + dacol * eks
            wq = dacol * e * q
            wq_hi = wq.astype(BF16)
            wq_lo = (wq - wq_hi.astype(F32)).astype(BF16)
            blk_sum = (jnp.dot(same_blk, wq_hi, preferred_element_type=F32)
                       + jnp.dot(same_blk, wq_lo, preferred_element_type=F32))
            dk_d = jnp.where(tmod == sp, blk_sum, dk_d)
        return a_off + jnp.where(diag_valid, a_d, 0.0), dq, dk + dk_d

    def one_head(pre, v, lbv, st0, dst1, dout, b_sc, k_sc):
        sig, f, k, q, b, bc = pre
        ebc = jnp.exp(bc)
        eb = jnp.exp(b)
        ekb = jnp.exp(bc - b)
        qt = q * eb
        kb = k * ekb
        row = lax.broadcasted_iota(jnp.int32, (C, C), 0)
        col = lax.broadcasted_iota(jnp.int32, (C, C), 1)
        da = jnp.where(col <= row, _dot_nt(dout, v), 0.0)
        dkb = _dot_nn(v, dst1)
        new_ds = _dot_tn(dout, qt) + dst1 * ebc
        a, dq_i, dk_i = intra_slow(q, k, b, da, b_sc, k_sc)
        dq = _dot_nn(dout, st0) * eb + dq_i
        dk = dkb * ekb + dk_i
        dv = _dot_tn(a, dout) + _dot_nt(kb, dst1)
        extra = jnp.sum(dkb * kb, axis=0, keepdims=True) + ebc * jnp.sum(st0 * dst1, axis=0, keepdims=True)
        rowk = lax.broadcasted_iota(jnp.int32, (C, K), 0)
        db = q * dq - k * dk + jnp.where(rowk == C - 1, extra, 0.0)
        dg = _rcumsum_rows(db)
        df = dg / f - dk
        return (dq * (K ** -0.5), df * (1.0 - lbv) * sig * (1.0 - sig), dv,
                jnp.sum(df * (1.0 - sig), axis=0, keepdims=True), new_ds)

    def body(hq_ref, hf_ref, hi_ref, lb_ref, st_ref, do_ref, dq_ref, dhf_ref, dv_ref, dlb_ref, ds_sc, b_sc, k_sc):
        @pl.when(pl.program_id(0) == 0)
        def _():
            ds_sc[...] = jnp.zeros_like(ds_sc)
            dlb_ref[...] = jnp.zeros_like(dlb_ref)

        ds_all = ds_sc[...]
        dlb = jnp.zeros((1, H * K), F32)
        for j in reversed(range(P)):
            rows = slice(C * j, C * (j + 1))
            st_all = st_ref[j]
            res = []
            for h in range(H):
                sl = slice(K * h, K * (h + 1))
                pre = _hg_prep(hq_ref[rows, sl], hf_ref[rows, sl], lb_ref[:, sl], b_sc.at[j, h], k_sc.at[j, h])
                res.append(one_head(pre, hi_ref[rows, sl], lb_ref[:, sl], st_all[:, sl], ds_all[:, sl], do_ref[rows, sl],
                                    b_sc.at[j, h], k_sc.at[j, h]))
            cat = lambda i: jnp.concatenate([r[i] for r in res], axis=1)
            dq_ref[rows, :] = cat(0).astype(dq_ref.dtype)
            dhf_ref[rows, :] = cat(1).astype(dhf_ref.dtype)
            dv_ref[rows, :] = cat(2).astype(dv_ref.dtype)
            dlb = dlb + cat(3)
            ds_all = cat(4)
        dlb_ref[...] += dlb
        ds_sc[...] = ds_all

    P = HG_CHUNKS_PER_STEP
    NS = NC // P
    blk = pl.BlockSpec((P * C, H * K), lambda c: (NS - 1 - c, 0))
    par = pl.BlockSpec((1, H * K), lambda c: (0, 0))
    return pl.pallas_call(
        body, name=name, grid=(NS,),
        in_specs=[blk, blk, blk, par, pl.BlockSpec((P, K, H * K), lambda c: (NS - 1 - c, 0, 0)), blk],
        out_specs=[blk, blk, blk, par],
        out_shape=[_sds((T, H * K), BF16)] * 3 + [_sds((1, H * K), F32)],
        scratch_shapes=[pltpu.VMEM((K, H * K), F32), pltpu.VMEM((P, H, C, K), F32), pltpu.VMEM((P, H, C, K), F32)],
        compiler_params=_cp(("arbitrary",)),
    )(hq, hf, hi, lb, states, do)


ATT_STACK = ATT_GROUP


def _att_valid(n):
    R, B = ATT_STACK * ATT_BLOCK, ATT_BLOCK
    j = lax.broadcasted_iota(jnp.int32, (2 * B, R), 0)
    t = lax.broadcasted_iota(jnp.int32, (2 * B, R), 1) % B
    dist = t + B - j
    first_key = jnp.where(n > 0, 0, B)
    return jnp.logical_and(jnp.logical_and(dist >= 0, dist < B), j >= first_key)


def _att_load(cur_ref, prev_ref, ba_ref, h0):
    hd = ATT_HD
    kv = h0 // ATT_GROUP
    def cols(ref, c0):
        return ref[:, c0:c0 + hd] + ba_ref[:, c0:c0 + hd]
    qs = jnp.concatenate([cols(cur_ref, hd * (h0 + g)) for g in range(ATT_STACK)], axis=0)
    kc = jnp.concatenate([cols(prev_ref, ATT_Q_W + hd * kv), cols(cur_ref, ATT_Q_W + hd * kv)], axis=0)
    vc = jnp.concatenate([cols(prev_ref, ATT_Q_W + ATT_KV_W + hd * kv), cols(cur_ref, ATT_Q_W + ATT_KV_W + hd * kv)], axis=0)
    return qs, kc, vc


def _att_probs(qs, kc, valid, sink_ref, h0):
    scale = 1.0 / math.sqrt(ATT_HD)
    s = jnp.where(valid, _dot_nt(kc, qs) * scale, NEG)
    sink = jnp.concatenate([jnp.full((1, ATT_BLOCK), sink_ref[0, h0 + g], F32) for g in range(ATT_STACK)], axis=1)
    m = jnp.maximum(jnp.max(s, axis=0, keepdims=True), sink)
    p = jnp.exp(s - m)
    ps = jnp.exp(sink - m)
    inv = 1.0 / (jnp.sum(p, axis=0, keepdims=True) + ps)
    return p * inv, ps * inv


def _attn_fwd(att, b_attn, sinks, *, name, after=None):
    T = att.shape[0]
    B = ATT_BLOCK
    NB = T // B
    lead = [] if after is None else [after]

    def body(*refs):
        sink_ref, cur_ref, prev_ref, ba_ref, o_ref = refs[len(lead):]
        valid = _att_valid(pl.program_id(0))
        outs = []
        for h0 in range(0, ATT_HEADS, ATT_STACK):
            qs, kc, vc = _att_load(cur_ref, prev_ref, ba_ref, h0)
            prob, _ = _att_probs(qs, kc, valid, sink_ref, h0)
            o = _dot_tn(prob, vc)
            outs += [o[B * g:B * (g + 1)] for g in range(ATT_STACK)]
        o_ref[...] = jnp.concatenate(outs, axis=1)

    return pl.pallas_call(
        body, name=name, grid=(NB,),
        in_specs=[pl.BlockSpec(memory_space=pl.ANY)] * len(lead) + [
            pl.BlockSpec(memory_space=pltpu.SMEM),
            pl.BlockSpec((B, ATT_COLS), lambda n: (n, 0)),
            pl.BlockSpec((B, ATT_COLS), lambda n: (jnp.maximum(n - 1, 0), 0)),
            pl.BlockSpec((1, ATT_COLS), lambda n: (0, 0))],
        out_specs=pl.BlockSpec((B, ATT_Q_W), lambda n: (n, 0)),
        out_shape=_sds((T, ATT_Q_W), F32),
        compiler_params=_cp(("parallel",)),
    )(*lead, sinks, att, att, b_attn)


def _attn_bwd(att, b_attn, sinks, dmix, *, name):
    T = att.shape[0]
    B, hd = ATT_BLOCK, ATT_HD
    NB = T // B
    scale = 1.0 / math.sqrt(hd)

    def body(sink_ref, cur_ref, prev_ref, ba_ref, do_ref, daq_ref, dakv_ref, dsink_ref, dbq_ref, dbkv_ref, carry_sc):
        n = pl.program_id(0)

        @pl.when(n == 0)
        def _():
            carry_sc[...] = jnp.zeros_like(carry_sc)
            dsink_ref[...] = jnp.zeros_like(dsink_ref)
            dbq_ref[...] = jnp.zeros_like(dbq_ref)
            dbkv_ref[...] = jnp.zeros_like(dbkv_ref)

        @pl.when(n < NB)
        def _():
            valid = _att_valid(n)
            hrow = lax.broadcasted_iota(jnp.int32, (SUBLANES, 128), 0)
            dsink = jnp.zeros((SUBLANES, 128), F32)
            dqs = []
            dks = [jnp.zeros((2 * B, hd), F32)] * ATT_KV
            dvs = [jnp.zeros((2 * B, hd), F32)] * ATT_KV
            for h0 in range(0, ATT_HEADS, ATT_STACK):
                kv = h0 // ATT_GROUP
                qs, kc, vc = _att_load(cur_ref, prev_ref, ba_ref, h0)
                prob, psink = _att_probs(qs, kc, valid, sink_ref, h0)
                dout = jnp.concatenate([do_ref[:, hd * (h0 + g):hd * (h0 + g + 1)] for g in range(ATT_STACK)], axis=0)
                dp = _dot_nt(vc, dout)
                delta = jnp.sum(prob * dp, axis=0, keepdims=True)
                dsc = prob * (dp - delta) * scale
                dq = _dot_tn(dsc, kc)
                dks[kv] = dks[kv] + _dot_nn(dsc, qs)
                dvs[kv] = dvs[kv] + _dot_nn(prob, dout)
                dsk = psink * delta
                for g in range(ATT_STACK):
                    dqs.append(dq[B * g:B * (g + 1)])
                    tot = jnp.sum(dsk[:, B * g:B * (g + 1)], axis=1, keepdims=True)
                    dsink = dsink - jnp.where(hrow == h0 + g, tot, 0.0)
            daq = jnp.concatenate(dqs, axis=1).astype(daq_ref.dtype)
            daq_ref[...] = daq
            dsink_ref[...] += dsink
            dbq_ref[...] += jnp.sum(daq.astype(F32), axis=0, keepdims=True)
            done = carry_sc[...] + jnp.concatenate([d[:B] for d in dks + dvs], axis=1)
            dakv_ref[...] = done.astype(dakv_ref.dtype)
            dbkv_ref[...] += jnp.sum(done.astype(dakv_ref.dtype).astype(F32), axis=0, keepdims=True)
            carry_sc[...] = jnp.concatenate([d[B:] for d in dks + dvs], axis=1)

        @pl.when(n == NB)
        def _():
            done = carry_sc[...]
            dakv_ref[...] = done.astype(dakv_ref.dtype)
            dbkv_ref[...] += jnp.sum(done.astype(dakv_ref.dtype).astype(F32), axis=0, keepdims=True)

    cl = lambda n: jnp.minimum(n, NB - 1)
    return pl.pallas_call(
        body, name=name, grid=(NB + 1,),
        in_specs=[pl.BlockSpec(memory_space=pltpu.SMEM),
                  pl.BlockSpec((B, ATT_COLS), lambda n: (cl(n), 0)),
                  pl.BlockSpec((B, ATT_COLS), lambda n: (jnp.maximum(cl(n) - 1, 0), 0)),
                  pl.BlockSpec((1, ATT_COLS), lambda n: (0, 0)),
                  pl.BlockSpec((B, ATT_Q_W), lambda n: (cl(n), 0))],
        out_specs=[pl.BlockSpec((B, ATT_Q_W), lambda n: (cl(n), 0)),
                   pl.BlockSpec((B, 2 * ATT_KV_W), lambda n: (jnp.maximum(n - 1, 0), 0)),
                   pl.BlockSpec((SUBLANES, 128), lambda n: (0, 0)),
                   pl.BlockSpec((1, ATT_Q_W), lambda n: (0, 0)),
                   pl.BlockSpec((1, 2 * ATT_KV_W), lambda n: (0, 0))],
        out_shape=[_sds((T, ATT_Q_W), BF16), _sds((T, 2 * ATT_KV_W), BF16), _sds((SUBLANES, 128), F32),
                   _sds((1, ATT_Q_W), F32), _sds((1, 2 * ATT_KV_W), F32)],
        scratch_shapes=[pltpu.VMEM((B, 2 * ATT_KV_W), F32)],
        compiler_params=_cp(("arbitrary",)),
    )(sinks, att, att, b_attn, dmix)


def _silu_and_grad(x):
    sg = _sigmoid(x)
    return x * sg, sg * (1.0 + x * (1.0 - sg))


def _mix_fwd_fn(o_raw, hg, o_att, hgw):
    outs = []
    for h in range(HG_HEADS):
        sl = slice(HG_DK * h, HG_DK * (h + 1))
        silu, _ = _silu_and_grad(hg[:, sl])
        outs.append(_rms_fwd(o_raw[:, sl], hgw) * silu)
    outs.append(o_att)
    return (jnp.concatenate(outs, axis=1),)


def _mix_bwd_fn(o_raw, hg, dmix, hgw):
    dos, dhgs = [], []
    dw = jnp.zeros((1, HG_DK), F32)
    for h in range(HG_HEADS):
        sl = slice(HG_DK * h, HG_DK * (h + 1))
        silu, dsilu = _silu_and_grad(hg[:, sl])
        dy = dmix[:, sl]
        dhgs.append(dy * _rms_fwd(o_raw[:, sl], hgw) * dsilu)
        dx, dwh = _rms_bwd(o_raw[:, sl], hgw, dy * silu)
        dos.append(dx)
        dw = dw + dwh
    return jnp.concatenate(dos, axis=1), jnp.concatenate(dhgs, axis=1), dw


def _final_fn(h2, tgt, wf):
    d = h2.shape[1]
    err = _rms_fwd(h2, wf) - tgt
    loss_cols = (0.5 / d) * jnp.sum(err * err, axis=0, keepdims=True)
    dh2, dwf = _rms_bwd(h2, wf, err * (1.0 / d))
    return dh2, dh2, loss_cols, dwf


class _NoExchange:
    def __init__(self, weights):
        self.weights = weights

    def start(self):
        return None

    def w_in(self, after):
        return self.weights["w_in_t"]

    def mid(self, after):
        return None

    def w_out(self, after):
        return {k: self.weights[k] for k in ("w_out", "conv_w8")}

    def rest(self, after):
        return {k: self.weights[k] for k in ("w_gate_t", "w_up_t", "w_down")}

    def ffn_grads(self, gs):
        return None

    def ffn_grads_send(self, after):
        return None


def _local_step(x, tgt, p, ex):
    T, D = x.shape
    row = lambda n, dt: _sds((T, n), dt)
    acc = lambda n: _sds((1, n), F32)

    (u,) = _rowwise(lambda xv, w: (_rms_fwd(xv, w),), [_full(x)], [p["norm_mix_w"]], [row(D, BF16)], [], name="rms_mix",
                    after=ex.start())
    p = dict(p, w_in_t=ex.w_in(u))
    hq, hf, hi, hg, att = _mm_nt(u, p["w_in_t"], splits=[HG_W] * 4 + [ATT_COLS], out_dtype=F32, name="in_proj")
    o_raw, states = _hgrn_fwd(hq, hf, hi, p["lb"], name="hgrn_fwd")
    o_att = _attn_fwd(att, p["b_attn"], p["sinks"], name="attn_fwd", after=ex.mid(o_raw))
    p = dict(p, **ex.w_out(o_att))
    def out_epilogue(prod, xv, w):
        h1v = prod + xv
        return h1v, _rms_fwd(h1v, w)

    h1, v, mix = _mm_nn(None, [p["w_out"]], name="mix_out_proj",
                        prologue=(lambda *a: _mix_fwd_fn(*a)[0], [o_raw, hg, o_att], [p["hg_norm_w"]], row(D, BF16)),
                        epilogue=(out_epilogue, [x], [p["norm_ffn_w"]], [row(D, F32), row(D, BF16)], []))
    p = dict(p, **ex.rest(v))
    gp, up, gate, act = _ffn_in(v, p["w_gate_t"], p["w_up_t"], p["conv_w8"], p["conv_b"], name="ffn_in")
    def down_epilogue(prod, h1v, tgtv, wf):
        return _final_fn(prod + h1v, tgtv, wf)

    dh2, dh2_b, loss_cols, d_final = _mm_nn(
        [[act]], [p["w_down"]], name="down_proj_loss",
        epilogue=(down_epilogue, [h1, tgt], [p["final_norm_w"]], [row(D, F32), row(D, BF16)], [acc(D), acc(D)]))

    g_down = _mm_tn([act], dh2_b, name="g_down")
    dgp, dup, d_conv_w8, d_conv_b = _ffn_back(dh2_b, p["w_down"], gp, up, gate, p["conv_w8"], name="ffn_back")
    g_gate_t = _mm_tn([dgp], v, name="g_gate")
    g_up_t = _mm_tn([dup], v, name="g_up")
    swapping = ex.ffn_grads([g_gate_t, g_up_t, g_down])

    def ffn_norm_bwd(dvv, hv, dh2v, w):
        dx, dw = _rms_bwd(hv, w, dvv)
        dh1v = dx + dh2v
        return dh1v, dh1v, dw

    dh1, dh1_b, d_norm_ffn = _mm_nn(
        [[dgp], [dup]], [p["w_gate_t"], p["w_up_t"]], name="d_v_norm", after=swapping,
        epilogue=(ffn_norm_bwd, [h1, dh2], [p["norm_ffn_w"]], [row(D, F32), row(D, BF16)], [acc(D)]))
    sent = ex.ffn_grads_send(dh1_b)
    def mix_bwd(dmixv, o_rawv, hgv, hgw):
        do_rawv, dhgv, dw = _mix_bwd_fn(o_rawv, hgv, dmixv[:, :HG_W], hgw)
        return do_rawv, dhgv, dmixv[:, HG_W:], dw

    do_raw, dhg, do_att, d_hg_norm = _mm_nn(
        [[dh1_b]], [p["w_out"]], name="d_mix_bwd", w_transposed=True, after=sent,
        epilogue=(mix_bwd, [o_raw, hg], [p["hg_norm_w"]], [row(HG_W, F32), row(HG_W, BF16), row(ATT_Q_W, F32)], [acc(HG_DK)]))
    g_out = _mm_tn([mix], dh1_b, name="g_out")
    daq, dakv, d_sinks8, d_bq, d_bkv = _attn_bwd(att, p["b_attn"], p["sinks"], do_att, name="attn_bwd")
    dhq, dhf, dhi, d_lb = _hgrn_bwd(hq, hf, hi, p["lb"], states, do_raw, name="hgrn_bwd")
    pieces = [dhq, dhf, dhi, dhg, daq, dakv]
    g_in_t = _mm_tn(pieces, u, name="g_in")

    def mix_norm_bwd(duv, xv, dh1v, w):
        dx, dw = _rms_bwd(xv, w, duv)
        return dx + dh1v, dw

    dx, d_norm_mix = _mm_nn([pieces], [p["w_in_t"]], name="d_u_norm",
                            epilogue=(mix_norm_bwd, [x, dh1], [p["norm_mix_w"]], [row(D, F32)], [acc(D)]))
    grads = dict(g_in_t=g_in_t, g_out=g_out, g_gate_t=g_gate_t, g_up_t=g_up_t, g_down=g_down,
                 norm_mix_w=d_norm_mix, b_attn=jnp.concatenate([d_bq, d_bkv], axis=1), lb=d_lb, hg_norm_w=d_hg_norm,
                 sinks8=d_sinks8, norm_ffn_w=d_norm_ffn, conv_w8=d_conv_w8, conv_b=d_conv_b, final_norm_w=d_final)
    return loss_cols, dx, grads


SLAB = (IN_COLS // N_CHIPS, D_FF // N_CHIPS, D_FF // N_CHIPS, D_FF // N_CHIPS, D_MODEL // N_CHIPS)
N_W = len(SLAB)
PACK_OFF = tuple(sum(SLAB[:i]) for i in range(N_W))
PACK_ROWS = sum(SLAB)
FULL_OFF = tuple(N_CHIPS * o for o in PACK_OFF)
FULL_ROWS = N_CHIPS * PACK_ROWS
HALF = tuple(s // 2 for s in SLAB)
HPACK_OFF = tuple(sum(HALF[:i]) for i in range(N_W))
HPACK_ROWS = sum(HALF)
HFULL_OFF = tuple(N_CHIPS * o for o in HPACK_OFF)
HFULL_ROWS = N_CHIPS * HPACK_ROWS
CHIP_FLIPS = ((1, 0), (0, 1), (1, 1))
N_DEV = 8
BF16_ROWS = 16
ANY = pl.BlockSpec(memory_space=pl.ANY)


def _pos():
    return lax.axis_index("x"), lax.axis_index("y"), lax.axis_index("c")


def _flip(v, f):
    return 1 - v if f else v


def _rcopy(src, dst, ssem, rsem, dev):
    return pltpu.make_async_remote_copy(src_ref=src, dst_ref=dst, send_sem=ssem, recv_sem=rsem, device_id=dev,
                                        device_id_type=pl.DeviceIdType.MESH)


def _rows(ref, start, n, align=None):
    if not isinstance(start, int):
        if align is None:
            align = SUBLANES * (4 // jnp.dtype(ref.dtype).itemsize)
        start = pl.multiple_of(start, align)
    return ref.at[pl.ds(start, n), :]


FFN_W = (1, 2, 3)
N_PEER = 1 + len(CHIP_FLIPS)
HBM = pl.BlockSpec(memory_space=pltpu.HBM)
SEM = pl.BlockSpec(memory_space=pltpu.SEMAPHORE)
EFFECT = pltpu.SideEffectType.DATAFLOW_SIDE_EFFECTING
LANES = 128


def _sent_rows(k, w, c):
    return (0, SLAB[w]) if k == 0 else (c * HALF[w], HALF[w])


def _gather_start(pack, cw8):
    D = pack.shape[1]
    lands = [lax.empty((N_CHIPS * SLAB[0], D), pack.dtype), lax.empty((3 * N_CHIPS * SLAB[1], D), pack.dtype),
             lax.empty((N_CHIPS * SLAB[4], D), pack.dtype), lax.empty((N_CHIPS,) + cw8.shape, cw8.dtype)]
    bufs = [pack, cw8] + lands

    def body(pack_ref, cw_ref, l_in, l_ffn, l_out, l_cw, *rest):
        in_send, in_recv, out_send, out_recv, ffn_send, ffn_recv = rest[:6]
        token = rest[-1]
        x, y, c = _pos()
        q = 2 * x + y
        peers = _gather_peers(x, y, c)

        def send(k, peer, w, land, base, ssem, rsem):
            r0, n = _sent_rows(k, w, c)
            _rcopy(_rows(pack_ref, PACK_OFF[w] + r0, n), _rows(land, base + q * SLAB[w] + r0, n), ssem, rsem, peer).start()

        for k, peer in enumerate(peers):
            send(k, peer, 0, l_in, 0, in_send.at[k], in_recv.at[k])
        for k, peer in enumerate(peers):
            send(k, peer, 4, l_out, 0, out_send.at[k], out_recv.at[k])
            _rcopy(cw_ref, l_cw.at[q], out_send.at[N_PEER + k], out_recv.at[N_PEER + k], peer).start()
        for j, w in enumerate(FFN_W):
            for k, peer in enumerate(peers):
                send(k, peer, w, l_ffn, j * N_CHIPS * SLAB[w], ffn_send.at[k], ffn_recv.at[k])
        token[...] = jnp.zeros_like(token)

    n_sem = (N_PEER, N_PEER, 2 * N_PEER, 2 * N_PEER, N_PEER, N_PEER)
    outs = pl.pallas_call(
        body, name="gather_start", in_specs=[HBM] * len(bufs),
        out_specs=[SEM] * len(n_sem) + [HBM] * len(bufs) + [pl.BlockSpec(memory_space=pltpu.VMEM)],
        out_shape=[pltpu.SemaphoreType.DMA((n,)) for n in n_sem]
        + [pltpu.HBM(b.shape, b.dtype) for b in bufs] + [TOKEN],
        input_output_aliases={i: len(n_sem) + i for i in range(len(bufs))},
        compiler_params=pltpu.CompilerParams(has_side_effects=EFFECT),
    )(*[pltpu.with_memory_space_constraint(b, pltpu.HBM) for b in bufs])
    bufs_out = outs[len(n_sem):]
    return dict(in_sems=outs[0:2], out_sems=outs[2:4], ffn_sems=outs[4:6], pack=bufs_out[0], cw=bufs_out[1], l_in=bufs_out[2],
                l_ffn=bufs_out[3], l_out=bufs_out[4], l_cw=bufs_out[5], token=bufs_out[6])


def _gather_peers(x, y, c):
    return [(x, y, 1 - c)] + [(_flip(x, fx), _flip(y, fy), c) for fx, fy in CHIP_FLIPS]


def _gather_wait_in(g, after):
    def body(pack_ref, l_in, send, recv, after_ref, pack_out, l_out):
        for k, peer in enumerate(_gather_peers(*_pos())):
            n = _sent_rows(k, 0, 0)[1]
            cp = _rcopy(_rows(pack_ref, PACK_OFF[0], n), _rows(l_in, 0, n), send.at[k], recv.at[k], peer)
            cp.wait_send()
            cp.wait_recv()

    return pl.pallas_call(
        body, name="gather_wait_in", in_specs=[HBM, HBM, SEM, SEM, ANY], out_specs=[HBM, HBM],
        out_shape=[pltpu.HBM(g["pack"].shape, g["pack"].dtype), pltpu.HBM(g["l_in"].shape, g["l_in"].dtype)],
        input_output_aliases={0: 0, 1: 1}, compiler_params=pltpu.CompilerParams(has_side_effects=EFFECT),
    )(g["pack"], g["l_in"], *g["in_sems"], after)


def _gather_wait_out(g, pack, after):
    def body(pack_ref, cw_ref, l_out, l_cw, o_send, o_recv, after_ref, o_out, o_cw):
        for k, peer in enumerate(_gather_peers(*_pos())):
            n_out = _sent_rows(k, 4, 0)[1]
            for cp in (_rcopy(_rows(pack_ref, PACK_OFF[4], n_out), _rows(l_out, 0, n_out), o_send.at[k], o_recv.at[k], peer),
                       _rcopy(cw_ref, l_cw.at[0], o_send.at[N_PEER + k], o_recv.at[N_PEER + k], peer)):
                cp.wait_send()
                cp.wait_recv()

    ins = [pack, g["cw"], g["l_out"], g["l_cw"]]
    return pl.pallas_call(
        body, name="gather_wait_out", in_specs=[HBM] * 4 + [SEM] * 2 + [ANY], out_specs=[HBM] * 2,
        out_shape=[pltpu.HBM(b.shape, b.dtype) for b in ins[2:]],
        input_output_aliases={2: 0, 3: 1}, compiler_params=pltpu.CompilerParams(has_side_effects=EFFECT),
    )(*ins, *g["out_sems"], after)


def _gather_wait_ffn(g, pack, after):
    def body(pack_ref, l_ffn, f_send, f_recv, after_ref, o_ffn):
        for k, peer in enumerate(_gather_peers(*_pos())):
            n_ffn = len(FFN_W) * _sent_rows(k, FFN_W[0], 0)[1]
            cp = _rcopy(_rows(pack_ref, PACK_OFF[FFN_W[0]], n_ffn), _rows(l_ffn, 0, n_ffn), f_send.at[k], f_recv.at[k], peer)
            cp.wait_send()
            cp.wait_recv()

    return pl.pallas_call(
        body, name="gather_wait_ffn", in_specs=[HBM] * 2 + [SEM] * 2 + [ANY], out_specs=HBM,
        out_shape=pltpu.HBM(g["l_ffn"].shape, g["l_ffn"].dtype),
        input_output_aliases={1: 0}, compiler_params=pltpu.CompilerParams(has_side_effects=EFFECT),
    )(pack, g["l_ffn"], *g["ffn_sems"], after)


FWD_IN = ((0, 0, 0),)
FWD_OUT = ((0, 4, 0),)
FWD_FFN = tuple((0, w, j * N_CHIPS * SLAB[w]) for j, w in enumerate(FFN_W))


def _forward_copies(layout, src, dst, send_sems, recv_sems):
    x, y, c = _pos()
    sib = (x, y, 1 - c)
    cps = []
    for fx, fy in CHIP_FLIPS:
        qa = 2 * _flip(x, fx) + _flip(y, fy)
        for bi, w, base in layout:
            r0 = base + qa * SLAB[w] + c * HALF[w]
            cps.append(_rcopy(_rows(src[bi], r0, HALF[w]), _rows(dst[bi], r0, HALF[w]),
                              send_sems.at[len(cps)], recv_sems.at[len(cps)], sib))
    return cps


def _forward_in(l_in):
    n = len(CHIP_FLIPS) * len(FWD_IN)

    def body(in_ref, out_ref, send_sems, recv_sems):
        cps = _forward_copies(FWD_IN, [in_ref], [out_ref], send_sems, recv_sems)
        for cp in cps:
            cp.start()
        for cp in cps:
            cp.wait_recv()
        for cp in cps:
            cp.wait_send()

    return pl.pallas_call(
        body, name="forward_in", in_specs=[ANY], out_specs=ANY, out_shape=_sds(l_in.shape, l_in.dtype),
        input_output_aliases={0: 0},
        scratch_shapes=[pltpu.SemaphoreType.DMA((n,)), pltpu.SemaphoreType.DMA((n,))],
    )(l_in)


def _forward_start(layout, land, *, name):
    n = len(CHIP_FLIPS) * len(layout)

    def body(in_ref, send_sems, recv_sems, out_ref, token):
        for cp in _forward_copies(layout, [in_ref], [in_ref], send_sems, recv_sems):
            cp.start()
        token[...] = jnp.zeros_like(token)

    outs = pl.pallas_call(
        body, name=name, in_specs=[HBM],
        out_specs=[SEM, SEM, HBM, pl.BlockSpec(memory_space=pltpu.VMEM)],
        out_shape=[pltpu.SemaphoreType.DMA((n,)), pltpu.SemaphoreType.DMA((n,)), pltpu.HBM(land.shape, land.dtype), TOKEN],
        input_output_aliases={0: 2}, compiler_params=pltpu.CompilerParams(has_side_effects=EFFECT),
    )(pltpu.with_memory_space_constraint(land, pltpu.HBM))
    return dict(sems=outs[0:2], land=outs[2], token=outs[3])


def _forward_wait(layout, s, after, *, name):
    def body(in_ref, send_sems, recv_sems, after_ref, out_ref):
        for cp in _forward_copies(layout, [in_ref], [in_ref], send_sems, recv_sems):
            cp.wait_send()
            cp.wait_recv()

    return pl.pallas_call(
        body, name=name, in_specs=[HBM, SEM, SEM, ANY], out_specs=HBM,
        out_shape=pltpu.HBM(s["land"].shape, s["land"].dtype),
        input_output_aliases={0: 0}, compiler_params=pltpu.CompilerParams(has_side_effects=EFFECT),
    )(s["land"], *s["sems"], after)


def _exchange_halves(ws, gs, small, *, name):
    D = gs[0].shape[1]
    n = len(ws)
    has_small = small is not None

    def body(*refs):
        g = refs[:n]
        t = refs[n + has_small:2 * n + has_small]
        sems = refs[2 * n + 2 * has_small:]
        d2d_send, d2d_recv = sems[0], sems[1]
        x, y, c = _pos()
        sib = (x, y, 1 - c)
        drains = []
        for i, w in enumerate(ws):
            h = HALF[w]
            for qq in range(N_CHIPS):
                _rcopy(_rows(g[i], qq * SLAB[w] + (1 - c) * h, h), _rows(t[i], qq * h, h),
                       d2d_send.at[i], d2d_recv.at[i], sib).start()
            drains.append(_rcopy(t[i], t[i], d2d_send.at[i], d2d_recv.at[i], sib))
        if has_small:
            small_ref, sall_ref = refs[n], refs[2 * n + 1]
            sm_send, sm_recv, loc_sem = sems[2], sems[3], sems[4]
            me = 4 * x + 2 * y + c
            own_small = pltpu.make_async_copy(small_ref, sall_ref.at[me], loc_sem)
            own_small.start()
            for f in range(1, N_DEV):
                peer = (_flip(x, f & 4), _flip(y, f & 2), _flip(c, f & 1))
                cp = _rcopy(small_ref, sall_ref.at[me], sm_send.at[f - 1], sm_recv.at[f - 1], peer)
                cp.start()
                drains.append(cp)
        for d in drains:
            d.wait_recv()
        for d in drains:
            d.wait_send()
        if has_small:
            own_small.wait()

    out_shape = [_sds((N_CHIPS * HALF[w], D), gs[0].dtype) for w in ws]
    scratch = [pltpu.SemaphoreType.DMA((n,)), pltpu.SemaphoreType.DMA((n,))]
    if has_small:
        out_shape.append(_sds((N_DEV,) + small.shape, F32))
        scratch += [pltpu.SemaphoreType.DMA((N_DEV - 1,)), pltpu.SemaphoreType.DMA((N_DEV - 1,)), pltpu.SemaphoreType.DMA]
    return pl.pallas_call(
        body, name=name, in_specs=[ANY] * (n + has_small), out_specs=[ANY] * (n + has_small),
        out_shape=out_shape, scratch_shapes=scratch,
    )(*gs, *([small] if has_small else []))


def _halves_copies(ws, g, t, send_sems, recv_sems):
    x, y, c = _pos()
    sib = (x, y, 1 - c)
    cps = []
    for i, w in enumerate(ws):
        h = HALF[w]
        for qq in range(N_CHIPS):
            cps.append(_rcopy(_rows(g[i], qq * SLAB[w] + (1 - c) * h, h), _rows(t[i], qq * h, h),
                              send_sems.at[N_CHIPS * i + qq], recv_sems.at[N_CHIPS * i + qq], sib))
    return cps


def _halves_start(ws, gs, *, name):
    D = gs[0].shape[1]
    n = len(ws)
    bufs = list(gs) + [lax.empty((N_CHIPS * HALF[w], D), gs[0].dtype) for w in ws]

    def body(*refs):
        for cp in _halves_copies(ws, refs[:n], refs[n:2 * n], refs[2 * n], refs[2 * n + 1]):
            cp.start()
        refs[-1][...] = jnp.zeros_like(refs[-1])

    outs = pl.pallas_call(
        body, name=name, in_specs=[HBM] * (2 * n),
        out_specs=[SEM, SEM] + [HBM] * (2 * n) + [pl.BlockSpec(memory_space=pltpu.VMEM)],
        out_shape=[pltpu.SemaphoreType.DMA((N_CHIPS * n,)), pltpu.SemaphoreType.DMA((N_CHIPS * n,))]
        + [pltpu.HBM(b.shape, b.dtype) for b in bufs] + [TOKEN],
        input_output_aliases={i: 2 + i for i in range(2 * n)},
        compiler_params=pltpu.CompilerParams(has_side_effects=EFFECT),
    )(*[pltpu.with_memory_space_constraint(b, pltpu.HBM) for b in bufs])
    return dict(sems=outs[0:2], gs=outs[2:2 + n], theirs=outs[2 + n:2 + 2 * n], token=outs[-1])


def _halves_wait(ws, s, after, *, name):
    n = len(ws)

    def body(*refs):
        for cp in _halves_copies(ws, refs[:n], refs[n:2 * n], refs[2 * n], refs[2 * n + 1]):
            cp.wait_send()
            cp.wait_recv()

    bufs = list(s["gs"]) + list(s["theirs"])
    outs = pl.pallas_call(
        body, name=name, in_specs=[HBM] * (2 * n) + [SEM, SEM, ANY], out_specs=[HBM] * (2 * n),
        out_shape=[pltpu.HBM(b.shape, b.dtype) for b in bufs],
        input_output_aliases={i: i for i in range(2 * n)},
        compiler_params=pltpu.CompilerParams(has_side_effects=EFFECT),
    )(*bufs, *s["sems"], after)
    return outs[:n], outs[n:]


REDUCE_SPLIT = 2


def _chip_partial(ws, gs, theirs, *, name, out_dtype=F32):
    D = gs[0].shape[1]
    n = len(ws)

    def body(*refs):
        for i in range(n):
            refs[2 * n + i][...] = (refs[i][...].astype(F32) + refs[n + i][...].astype(F32)).astype(out_dtype)

    blk = [HALF[w] // REDUCE_SPLIT for w in ws]
    mine = [pl.BlockSpec((b, D), lambda qq, j: ((2 * qq + lax.axis_index("c")) * REDUCE_SPLIT + j, 0)) for b in blk]
    flat = [pl.BlockSpec((b, D), lambda qq, j: (qq * REDUCE_SPLIT + j, 0)) for b in blk]
    return pl.pallas_call(
        body, name=name, grid=(N_CHIPS, REDUCE_SPLIT), in_specs=mine + flat, out_specs=flat,
        out_shape=[_sds((N_CHIPS * HALF[w], D), out_dtype) for w in ws],
        compiler_params=_cp(("parallel", "parallel")),
    )(*gs, *theirs)


def _partial_copies(ws, part, got, send_sems, recv_sems):
    x, y, c = _pos()
    cps = []
    for k, (fx, fy) in enumerate(CHIP_FLIPS):
        peer = (_flip(x, fx), _flip(y, fy), c)
        qp = 2 * _flip(x, fx) + _flip(y, fy)
        for i, w in enumerate(ws):
            cps.append(_rcopy(_rows(part[i], qp * HALF[w], HALF[w]), _rows(got[i], k * HALF[w], HALF[w]),
                              send_sems.at[len(ws) * k + i], recv_sems.at[len(ws) * k + i], peer))
    return cps


def _send_chip_partials(ws, parts, *, name):
    D = parts[0].shape[1]
    n = len(ws)

    def body(*refs):
        cps = _partial_copies(ws, refs[:n], refs[n:2 * n], refs[2 * n], refs[2 * n + 1])
        for cp in cps:
            cp.start()
        for cp in cps:
            cp.wait_recv()
        for cp in cps:
            cp.wait_send()

    return pl.pallas_call(
        body, name=name, in_specs=[ANY] * n, out_specs=[ANY] * n,
        out_shape=[_sds((len(CHIP_FLIPS) * HALF[w], D), parts[0].dtype) for w in ws],
        scratch_shapes=[pltpu.SemaphoreType.DMA((len(CHIP_FLIPS) * n,)), pltpu.SemaphoreType.DMA((len(CHIP_FLIPS) * n,))],
    )(*parts)


def _send_start(ws, parts, *, name):
    D = parts[0].shape[1]
    n = len(ws)
    bufs = list(parts) + [lax.empty((len(CHIP_FLIPS) * HALF[w], D), parts[0].dtype) for w in ws]

    def body(*refs):
        send_sems, recv_sems = refs[2 * n], refs[2 * n + 1]
        for cp in _partial_copies(ws, refs[:n], refs[n:2 * n], send_sems, recv_sems):
            cp.start()
        refs[-1][...] = jnp.zeros_like(refs[-1])

    outs = pl.pallas_call(
        body, name=name, in_specs=[HBM] * (2 * n),
        out_specs=[SEM, SEM] + [HBM] * (2 * n) + [pl.BlockSpec(memory_space=pltpu.VMEM)],
        out_shape=[pltpu.SemaphoreType.DMA((len(CHIP_FLIPS) * n,)), pltpu.SemaphoreType.DMA((len(CHIP_FLIPS) * n,))]
        + [pltpu.HBM(b.shape, b.dtype) for b in bufs] + [TOKEN],
        input_output_aliases={i: 2 + i for i in range(2 * n)},
        compiler_params=pltpu.CompilerParams(has_side_effects=EFFECT),
    )(*[pltpu.with_memory_space_constraint(b, pltpu.HBM) for b in bufs])
    return dict(sems=outs[0:2], parts=outs[2:2 + n], got=outs[2 + n:2 + 2 * n], token=outs[-1])


def _send_wait(ws, s, after, *, name):
    n = len(ws)

    def body(*refs):
        for cp in _partial_copies(ws, refs[:n], refs[n:2 * n], refs[2 * n], refs[2 * n + 1]):
            cp.wait_send()
            cp.wait_recv()

    bufs = list(s["parts"]) + list(s["got"])
    outs = pl.pallas_call(
        body, name=name, in_specs=[HBM] * (2 * n) + [SEM, SEM] + [ANY] * len(after), out_specs=[HBM] * (2 * n),
        out_shape=[pltpu.HBM(b.shape, b.dtype) for b in bufs],
        input_output_aliases={i: i for i in range(2 * n)},
        compiler_params=pltpu.CompilerParams(has_side_effects=EFFECT),
    )(*bufs, *s["sems"], *after)
    return outs[:n], outs[n:]


def _chip_reduce(ws, parts, got, *, name, after=None):
    D = parts[0].shape[1]
    nk = len(CHIP_FLIPS)
    n = len(ws)
    extra = [] if after is None else [after]

    def body(*refs):
        refs = refs[len(extra):]
        outs = refs[(1 + nk) * n:]
        for i in range(n):
            acc = refs[i][...].astype(F32)
            for k in range(nk):
                acc = acc + refs[n * (1 + k) + i][...].astype(F32)
            outs[i][...] = acc

    blk = [HALF[w] // REDUCE_SPLIT for w in ws]

    def q_idx(j):
        return (2 * lax.axis_index("x") + lax.axis_index("y")) * REDUCE_SPLIT + j

    in_specs = [pl.BlockSpec((b, D), lambda j: (q_idx(j), 0)) for b in blk]
    for k in range(nk):
        in_specs += [pl.BlockSpec((b, D), functools.partial(lambda j, k: (k * REDUCE_SPLIT + j, 0), k=k)) for b in blk]
    out_specs = [pl.BlockSpec((b, D), lambda j: (lax.axis_index("c") * REDUCE_SPLIT + j, 0)) for b in blk]
    return pl.pallas_call(
        body, name=name, grid=(REDUCE_SPLIT,), in_specs=[ANY] * len(extra) + in_specs, out_specs=out_specs,
        out_shape=[_sds((SLAB[w], D), F32) for w in ws],
        compiler_params=_cp(("parallel",)),
    )(*extra, *parts, *[g for _ in range(nk) for g in got])


def _exchange_reduced(ws, shards, *, name):
    n = len(ws)

    def body(*refs):
        ins, outs = refs[:n], refs[n:2 * n]
        send_sems, recv_sems = refs[2 * n], refs[2 * n + 1]
        x, y, c = _pos()
        sib = (x, y, 1 - c)
        cps = []
        for i, w in enumerate(ws):
            cp = _rcopy(_rows(ins[i], c * HALF[w], HALF[w]), _rows(outs[i], c * HALF[w], HALF[w]),
                        send_sems.at[i], recv_sems.at[i], sib)
            cp.start()
            cps.append(cp)
        for cp in cps:
            cp.wait_recv()
        for cp in cps:
            cp.wait_send()

    return pl.pallas_call(
        body, name=name, in_specs=[ANY] * n, out_specs=[ANY] * n,
        out_shape=[_sds(s.shape, s.dtype) for s in shards], input_output_aliases={i: i for i in range(n)},
        scratch_shapes=[pltpu.SemaphoreType.DMA((n,)), pltpu.SemaphoreType.DMA((n,))],
    )(*shards)


def _adamw_fn(w, g, m, v):
    m2 = ADAM_B1 * m + (1.0 - ADAM_B1) * g
    v2 = ADAM_B2 * v + (1.0 - ADAM_B2) * (g * g)
    m_hat = m2 / (1.0 - ADAM_B1 ** ADAM_STEP)
    v_hat = v2 / (1.0 - ADAM_B2 ** ADAM_STEP)
    return -ADAM_LR * (m_hat / (jnp.sqrt(v_hat) + ADAM_EPS) + ADAM_WD * w), m2, v2


def _adamw(w, g, m, v, *, name):
    shp = _sds(w.shape, F32)
    rows = w.shape[0]
    tm = max(t for t in range(SUBLANES, 512 + 1, SUBLANES) if rows % t == 0)
    return _rowwise(lambda wv, gv, mv, vv: (gv, *_adamw_fn(wv, gv, mv, vv)), [_full(w), _full(g), _full(m), _full(v)], [],
                    [shp] * 4, [], name=name, tm=tm)


SMALL_SEGS = (("loss", 8), ("norm_mix_w", 8), ("b_attn", 8), ("lb_logits", 8), ("hg_norm_w", 8), ("sinks", 8),
              ("norm_ffn_w", 8), ("conv_w", 72), ("conv_b", 24), ("final_norm_w", 8))
SMALL_OFF = {n: sum(r for _, r in SMALL_SEGS[:i]) for i, (n, _) in enumerate(SMALL_SEGS)}
SMALL_ROWS = sum(r for _, r in SMALL_SEGS)
LANES = 128


def _pack_small(parts):
    segs = []
    for n, r in SMALL_SEGS:
        a = parts.get(n)
        flat = jnp.zeros((0,), F32) if a is None else a.reshape(-1).astype(F32)
        segs.append(jnp.pad(flat, (0, r * LANES - flat.shape[0])).reshape(r, LANES))
    return jnp.concatenate(segs, axis=0)


def _unpack_small(pack, n, shape):
    size = math.prod(shape)
    r0 = SMALL_OFF[n]
    return pack[r0:r0 + dict(SMALL_SEGS)[n]].reshape(-1)[:size].reshape(shape)


def _small_update(sall, wp, mp, vp, *, after):
    R = SMALL_ROWS
    r_lb = SMALL_OFF["lb_logits"]

    def body(after_ref, s_ref, w_ref, m_ref, v_ref, g_ref, d_ref, m2_ref, v2_ref, loss_ref):
        g = s_ref[0]
        for i in range(1, N_DEV):
            g = g + s_ref[i]
        tot = jnp.sum(jnp.sum(g[0:8], axis=1, keepdims=True), axis=0, keepdims=True)
        loss_ref[...] = jnp.broadcast_to(tot, loss_ref.shape)
        lg = w_ref[r_lb:r_lb + 8, :]
        p0 = _sigmoid(lg - pltpu.roll(lg, 4, 0))
        d = g[r_lb:r_lb + 8]
        d = d + pltpu.roll(d, 4, 0)
        sign = jnp.where(lax.broadcasted_iota(jnp.int32, d.shape, 0) < 4, 1.0, -1.0)
        g = jnp.concatenate([g[:r_lb], sign * d * p0 * (1.0 - p0), g[r_lb + 8:]], axis=0)
        g_ref[...] = g
        d_ref[...], m2_ref[...], v2_ref[...] = _adamw_fn(w_ref[...], g, m_ref[...], v_ref[...])

    full = pl.BlockSpec((R, LANES), lambda: (0, 0))
    return pl.pallas_call(
        body, name="small_update",
        in_specs=[ANY, pl.BlockSpec((N_DEV, R, LANES), lambda: (0, 0, 0)), full, full, full],
        out_specs=[full, full, full, full, pl.BlockSpec((8, LANES), lambda: (0, 0))],
        out_shape=[_sds((R, LANES), F32)] * 4 + [_sds((8, LANES), F32)],
        compiler_params=_cp(),
    )(after, sall, wp, mp, vp)


def _lb_fwd(lb_logits):
    n = lb_logits.shape[1]

    def body(l_ref, o_ref):
        o_ref[...] = _sigmoid(l_ref[0:1, :] - l_ref[1:2, :])

    return pl.pallas_call(body, name="lb_fwd", out_shape=jax.ShapeDtypeStruct((1, n), F32), compiler_params=_cp())(lb_logits)


class _MeshExchange:
    def __init__(self, pack, cw8):
        self.gather = _gather_start(pack, cw8)
        self.sent = None
        self.conv_w8 = None

    def start(self):
        return self.gather["token"]

    def w_in(self, after):
        self.pack, l_in = _gather_wait_in(self.gather, after)
        return (_forward_in(l_in), N_CHIPS * SLAB[0], 0)

    def mid(self, after):
        l_out, l_cw = _gather_wait_out(self.gather, self.pack, after)
        self.conv_w8 = jnp.concatenate([l_cw[i] for i in range(N_CHIPS)], axis=1)
        self.passing_out = _forward_start(FWD_OUT, l_out, name="forward_out_start")
        return self.passing_out["token"]

    def w_out(self, after):
        l_ffn = _gather_wait_ffn(self.gather, self.pack, after)
        self.passing_ffn = _forward_start(FWD_FFN, l_ffn, name="forward_ffn_start")
        l_out = _forward_wait(FWD_OUT, self.passing_out, self.passing_ffn["token"], name="forward_out_wait")
        return dict(w_out=(l_out, N_CHIPS * SLAB[4], 0), conv_w8=self.conv_w8)

    def rest(self, after):
        l_ffn = _forward_wait(FWD_FFN, self.passing_ffn, after, name="forward_ffn_wait")
        rows = N_CHIPS * SLAB[FFN_W[0]]
        return dict(w_gate_t=(l_ffn, rows, 0), w_up_t=(l_ffn, rows, 1), w_down=(l_ffn, rows, 2))

    def ffn_grads(self, gs):
        self.swap = _halves_start(FFN_W, gs, name="halves_ffn_start")
        return self.swap["token"]

    def ffn_grads_send(self, after):
        gs, theirs = _halves_wait(FFN_W, self.swap, after, name="halves_ffn_wait")
        parts = _chip_partial(FFN_W, gs, theirs, name="chip_partial_ffn", out_dtype=BF16)
        self.sent = _send_start(FFN_W, parts, name="send_ffn_start")
        return self.sent["token"]


def kernel(x, norm_mix_w, w_in, b_attn, lb_logits, hg_norm_w, sinks, w_out, norm_ffn_w, w_gate, w_up, conv_w, conv_b, w_down, final_norm_w, loss_target, m_norm_mix_w, m_w_in, m_b_attn, m_lb_logits, m_hg_norm_w, m_sinks, m_w_out, m_norm_ffn_w, m_w_gate, m_w_up, m_conv_w, m_conv_b, m_w_down, m_final_norm_w, v_norm_mix_w, v_w_in, v_b_attn, v_lb_logits, v_hg_norm_w, v_sinks, v_w_out, v_norm_ffn_w, v_w_gate, v_w_up, v_conv_w, v_conv_b, v_w_down, v_final_norm_w):
    D = D_MODEL
    q = 2 * lax.axis_index("x") + lax.axis_index("y")
    ccols = D_FF // N_CHIPS

    pack = jnp.concatenate([w_in[0].T, w_gate[0].T, w_up[0].T, w_down[0], w_out[0]], axis=0).astype(BF16)
    cw8 = jnp.concatenate([conv_w[0], jnp.zeros((SUBLANES - 3, ccols), F32)], axis=0)
    ex = _MeshExchange(pack, cw8)
    p = dict(norm_mix_w=norm_mix_w, b_attn=b_attn, lb=_lb_fwd(lb_logits), hg_norm_w=hg_norm_w, sinks=sinks,
             norm_ffn_w=norm_ffn_w, conv_b=conv_b, final_norm_w=final_norm_w.reshape(1, D))
    loss_cols, dx, g = _local_step(x[0], loss_target[0], p, ex)
    conv_w8 = ex.conv_w8

    small = _pack_small(dict(loss=loss_cols, norm_mix_w=g["norm_mix_w"], b_attn=g["b_attn"], lb_logits=g["lb"],
                             hg_norm_w=g["hg_norm_w"], sinks=g["sinks8"], norm_ffn_w=g["norm_ffn_w"],
                             conv_w=g["conv_w8"][:3], conv_b=g["conv_b"], final_norm_w=g["final_norm_w"]))
    parts_ffn, got_ffn = _send_wait(FFN_W, ex.sent, [dx], name="send_ffn_wait")
    late = (0, 4)
    gs = [g["g_in_t"], g["g_out"]]
    *theirs, sall = _exchange_halves(late, gs, small, name="exchange_halves_late")
    parts_late = _chip_partial(late, gs, theirs, name="chip_partial_late", out_dtype=BF16)
    sent_late = _send_start(late, parts_late, name="send_late_start")
    big = {}

    def finish(ws, parts, got, specs, tag, after):
        shards = _exchange_reduced(ws, _chip_reduce(ws, parts, got, name="chip_reduce_" + tag, after=after),
                                   name="exchange_reduced_" + tag)
        deltas = []
        for gw, (n, w, m, v, tr) in zip(shards, specs):
            view = (lambda a: a[0].T) if tr else (lambda a: a[0])
            back = (lambda a: a.T[None]) if tr else (lambda a: a[None])
            res = _adamw(view(w), gw, view(m), view(v), name="adamw_" + n)
            big[n] = tuple(back(r) for r in res)
            deltas.append(res[1])
        return deltas

    done_ffn = finish(FFN_W, parts_ffn, got_ffn, (("w_gate", w_gate, m_w_gate, v_w_gate, True),
                                                  ("w_up", w_up, m_w_up, v_w_up, True),
                                                  ("w_down", w_down, m_w_down, v_w_down, False)), "ffn", sent_late["token"])

    def place(a):
        return lax.dynamic_update_slice(jnp.zeros((3, D_FF), F32), a[0], (0, q * ccols))

    def small_pack(ws, cw):
        nm, ba, lbl, hg, sk, nf, cb, fn = ws
        return _pack_small(dict(norm_mix_w=nm, b_attn=ba, lb_logits=lbl, hg_norm_w=hg,
                                sinks=jnp.broadcast_to(sk.reshape(ATT_HEADS, 1), (ATT_HEADS, LANES)), norm_ffn_w=nf,
                                conv_w=cw, conv_b=cb, final_norm_w=fn))

    wp = small_pack((norm_mix_w, b_attn, lb_logits, hg_norm_w, sinks, norm_ffn_w, conv_b, final_norm_w), conv_w8[:3])
    mp = small_pack((m_norm_mix_w, m_b_attn, m_lb_logits, m_hg_norm_w, m_sinks, m_norm_ffn_w, m_conv_b, m_final_norm_w),
                    place(m_conv_w))
    vp = small_pack((v_norm_mix_w, v_b_attn, v_lb_logits, v_hg_norm_w, v_sinks, v_norm_ffn_w, v_conv_b, v_final_norm_w),
                    place(v_conv_w))
    outs = _small_update(sall, wp, mp, vp, after=sent_late["token"])
    loss = outs[4][0, 0]
    parts_late, got_late = _send_wait(late, sent_late, [*done_ffn, outs[4]], name="send_late_wait")
    finish(late, parts_late, got_late, (("w_in", w_in, m_w_in, v_w_in, True), ("w_out", w_out, m_w_out, v_w_out, False)),
           "late", None)

    def small_out(pk, n, ref):
        if n == "sinks":
            return pk[SMALL_OFF[n]:SMALL_OFF[n] + ATT_HEADS, 0].reshape(ref.shape)
        if n == "conv_w":
            full = _unpack_small(pk, n, (3, D_FF))
            return lax.dynamic_slice(full, (0, q * ccols), (3, ccols))[None]
        return _unpack_small(pk, n, ref.shape)

    refs = dict(norm_mix_w=norm_mix_w, b_attn=b_attn, lb_logits=lb_logits, hg_norm_w=hg_norm_w, sinks=sinks,
                norm_ffn_w=norm_ffn_w, conv_w=conv_w, conv_b=conv_b, final_norm_w=final_norm_w)
    order = ("norm_mix_w", "w_in", "b_attn", "lb_logits", "hg_norm_w", "sinks", "w_out", "norm_ffn_w", "w_gate", "w_up",
             "conv_w", "conv_b", "w_down", "final_norm_w")
    res = [loss, dx[None]]
    for k in range(4):
        for n in order:
            res.append(big[n][k] if n in big else small_out(outs[k], n, refs[n]))
    return tuple(res)
```

```python
import functools
import math

import jax
import jax.numpy as jnp
from jax import lax
from jax.experimental import pallas as pl
from jax.experimental.pallas import tpu as pltpu

F32 = jnp.float32
BF16 = jnp.bfloat16

D_MODEL = 1024
HG_HEADS = 4
HG_DK = 128
HG_W = HG_HEADS * HG_DK
HG_CHUNK = 64
HG_SUB = 8
HG_FWD_CHUNKS_PER_STEP = 8
HG_CHUNKS_PER_STEP = 4
ATT_HEADS = 8
ATT_KV = 2
ATT_GROUP = ATT_HEADS // ATT_KV
ATT_HD = 64
ATT_BLOCK = 128
ATT_Q_W = ATT_HEADS * ATT_HD
ATT_KV_W = ATT_KV * ATT_HD
ATT_COLS = ATT_Q_W + 2 * ATT_KV_W
IN_COLS = 4 * HG_W + ATT_COLS
D_FF = 2816
EPS = 1e-6
ADAM_LR, ADAM_B1, ADAM_B2, ADAM_EPS, ADAM_WD, ADAM_STEP = 0.001, 0.9, 0.999, 1e-08, 0.01, 10
NEG = -1e30

V7X_VMEM_BYTES = 64 * 1024 * 1024
VMEM_LIMIT = 48 * 1024 * 1024
SUBLANES = 8

N_CHIPS = 4


def _cp(sem=None, **kw):
    return pltpu.CompilerParams(dimension_semantics=sem, vmem_limit_bytes=VMEM_LIMIT, **kw)


def _sds(shape, dtype):
    return jax.ShapeDtypeStruct(shape, dtype)


TOKEN = jax.ShapeDtypeStruct((8, 128), jnp.float32)


def _wspec(w):
    arr, rows, blk = w
    return pl.BlockSpec((rows, arr.shape[1]), lambda i: (blk, 0))


def _mm_nt(a, w, *, splits, out_dtype, name, after=None, tm=512):
    M, K = a.shape
    N = w[1]
    tm = min(tm, M)
    assert sum(splits) == N and M % tm == 0
    offs = [sum(splits[:i]) for i in range(len(splits))]
    n_in = 2 if after is None else 3

    def body(*refs):
        a_ref, w_ref = refs[0], refs[1]
        acc = lax.dot_general(a_ref[...], w_ref[...], (((1,), (1,)), ((), ())), preferred_element_type=F32)
        for o_ref, c0, n in zip(refs[n_in:], offs, splits):
            o_ref[...] = acc[:, c0:c0 + n].astype(out_dtype)

    in_specs = [pl.BlockSpec((tm, K), lambda i: (i, 0)), _wspec(w)]
    args = [a, w[0]]
    if after is not None:
        in_specs.append(pl.BlockSpec(memory_space=pl.ANY))
        args.append(after)
    outs = pl.pallas_call(
        body, name=name, grid=(M // tm,), in_specs=in_specs,
        out_specs=[pl.BlockSpec((tm, n), lambda i: (i, 0)) for n in splits],
        out_shape=[_sds((M, n), out_dtype) for n in splits],
        compiler_params=_cp(("parallel",)),
    )(*args)
    return outs


def _mm_nn(pieces, ws, *, name, out_dtype=F32, residual=None, epilogue=None, prologue=None, after=None,
           w_transposed=False, tm=512):
    pro_fn, pro_rows, pro_bc, pro_out = prologue or (None, [], [], None)
    if prologue is not None:
        assert pieces is None and len(ws) == 1
        pieces = [[pro_out]]
    M = pieces[0][0].shape[0]
    K = ws[0][1] if w_transposed else ws[0][0].shape[1]
    tm = min(tm, M)
    flat = [] if prologue is not None else [p for grp in pieces for p in grp]
    n_p = len(flat)
    n_w = len(ws)
    n_pr, n_pb = len(pro_rows), len(pro_bc)
    fn, row_ins, bc_ins, row_outs, acc_outs = epilogue or (None, [], [], [_sds((M, K), out_dtype)], [])
    if residual is not None:
        assert epilogue is None
        row_ins = [residual]
    n_r, n_b, n_o = len(row_ins), len(bc_ins), len(row_outs)
    lead = [] if after is None else [after]

    def body(*refs):
        refs = refs[len(lead):]
        p_refs = refs[:n_p]
        w_refs = refs[n_p:n_p + n_w]
        extra = [r[...] for r in refs[n_p + n_w:n_p + n_w + n_r + n_b]]
        base = n_p + n_w + n_r + n_b
        pro = [r[...] for r in refs[base:base + n_pr + n_pb]]
        base += n_pr + n_pb
        o_refs = refs[base:base + n_o]
        a_refs = refs[base + n_o:base + n_o + len(acc_outs)]
        if pro_fn is not None:
            lhs = pro_fn(*pro).astype(pro_out.dtype)
            refs[-1][...] = lhs
            tiles = [lhs]
        else:
            tiles = [r[...] for r in p_refs]
        acc = None
        k = 0
        for gi, grp in enumerate(pieces):
            c0 = 0
            for p in grp:
                n = p.shape[1]
                if w_transposed:
                    t = lax.dot_general(tiles[k], w_refs[gi][...], (((1,), (1,)), ((), ())), preferred_element_type=F32)
                else:
                    t = jnp.dot(tiles[k], w_refs[gi][c0:c0 + n, :], preferred_element_type=F32)
                acc = t if acc is None else acc + t
                c0 += n
                k += 1
        if fn is None:
            res = (acc + extra[0] if residual is not None else acc,)
        else:
            res = fn(acc, *extra)
        for o_ref, val in zip(o_refs, res[:n_o]):
            o_ref[...] = val.astype(o_ref.dtype)
        if acc_outs:
            @pl.when(pl.program_id(0) == 0)
            def _():
                for a_ref in a_refs:
                    a_ref[...] = jnp.zeros_like(a_ref)
            for a_ref, val in zip(a_refs, res[n_o:]):
                a_ref[...] += val

    in_specs = [pl.BlockSpec((tm, p.shape[1]), lambda i: (i, 0)) for p in flat]
    in_specs += [_wspec(w) for w in ws]
    in_specs += [pl.BlockSpec((tm, r.shape[1]), lambda i: (i, 0)) for r in row_ins]
    in_specs += [pl.BlockSpec(b.shape, lambda i: (0, 0)) for b in bc_ins]
    in_specs += [pl.BlockSpec((tm, r.shape[1]), lambda i: (i, 0)) for r in pro_rows]
    in_specs += [pl.BlockSpec(b.shape, lambda i: (0, 0)) for b in pro_bc]
    out_specs = [pl.BlockSpec((tm, s.shape[1]), lambda i: (i, 0)) for s in row_outs]
    out_specs += [pl.BlockSpec(s.shape, lambda i: (0, 0)) for s in acc_outs]
    pro_outs = [] if prologue is None else [pro_out]
    out_specs += [pl.BlockSpec((tm, s.shape[1]), lambda i: (i, 0)) for s in pro_outs]
    outs = pl.pallas_call(
        body, name=name, grid=(M // tm,), in_specs=[pl.BlockSpec(memory_space=pl.ANY)] * len(lead) + in_specs,
        out_specs=out_specs, out_shape=list(row_outs) + list(acc_outs) + pro_outs,
        compiler_params=_cp(("arbitrary",) if acc_outs else ("parallel",)),
    )(*lead, *flat, *[w[0] for w in ws], *row_ins, *bc_ins, *pro_rows, *pro_bc)
    return outs if (epilogue is not None or prologue is not None) else outs[0]


def _mm_tn(pieces, x, *, name, out_dtype=BF16, tt=1024):
    M, K = x.shape
    tt = min(tt, M)
    ns = [p.shape[1] for p in pieces]
    offs = [sum(ns[:i]) for i in range(len(ns))]
    N = sum(ns)
    n_p = len(pieces)
    last = M // tt - 1

    def body(*refs):
        p_refs = refs[:n_p]
        x_ref = refs[n_p]
        o_ref, acc_ref = refs[n_p + 1], refs[n_p + 2]

        @pl.when(pl.program_id(0) == 0)
        def _():
            acc_ref[...] = jnp.zeros_like(acc_ref)

        xv = x_ref[...]
        for p_ref, c0, n in zip(p_refs, offs, ns):
            acc_ref[c0:c0 + n, :] += lax.dot_general(p_ref[...], xv, (((0,), (0,)), ((), ())),
                                                      preferred_element_type=F32)

        @pl.when(pl.program_id(0) == last)
        def _():
            o_ref[...] = acc_ref[...].astype(o_ref.dtype)

    in_specs = [pl.BlockSpec((tt, n), lambda i: (i, 0)) for n in ns]
    in_specs.append(pl.BlockSpec((tt, K), lambda i: (i, 0)))
    return pl.pallas_call(
        body, name=name, grid=(M // tt,), in_specs=in_specs,
        out_specs=pl.BlockSpec((N, K), lambda i: (0, 0)),
        out_shape=_sds((N, K), out_dtype),
        scratch_shapes=[pltpu.VMEM((N, K), F32)],
        compiler_params=_cp(("arbitrary",)),
    )(*pieces, x)


def _rms_fwd(xf, w):
    inv = lax.rsqrt(jnp.mean(xf * xf, axis=-1, keepdims=True) + EPS)
    return xf * inv * w


def _rms_bwd(xf, w, dy):
    inv = lax.rsqrt(jnp.mean(xf * xf, axis=-1, keepdims=True) + EPS)
    xhat = xf * inv
    dxhat = dy * w
    dx = inv * (dxhat - xhat * jnp.mean(dxhat * xhat, axis=-1, keepdims=True))
    dw = jnp.sum(dy * xhat, axis=0, keepdims=True)
    return dx, dw


def _sigmoid(x):
    return 1.0 / (1.0 + jnp.exp(-x))


def _rowwise(fn, row_ins, bc_ins, row_outs, acc_outs, *, name, tm=256, after=None):
    M = row_outs[0].shape[0] if row_outs else row_ins[0][0].shape[0]
    assert M % tm == 0 and tm % SUBLANES == 0, (name, M, tm)
    n_r, n_b, n_o, n_a = len(row_ins), len(bc_ins), len(row_outs), len(acc_outs)
    n_after = 0 if after is None else 1

    def body(*refs):
        refs = refs[n_after:]
        ins = [r[...] for r in refs[:n_r + n_b]]
        o_refs = refs[n_r + n_b:n_r + n_b + n_o]
        a_refs = refs[n_r + n_b + n_o:]
        res = fn(*ins)
        for o_ref, val in zip(o_refs, res[:n_o]):
            o_ref[...] = val.astype(o_ref.dtype)
        if n_a:
            @pl.when(pl.program_id(0) == 0)
            def _():
                for a_ref in a_refs:
                    a_ref[...] = jnp.zeros_like(a_ref)
            for a_ref, val in zip(a_refs, res[n_o:]):
                a_ref[...] += val

    in_specs = [pl.BlockSpec((tm, cw), functools.partial(lambda i, cb, r0: (i + r0, cb), cb=cb, r0=r0))
                for (_, cw, cb, r0) in row_ins]
    in_specs += [pl.BlockSpec(b.shape, lambda i: (0, 0)) for b in bc_ins]
    out_specs = [pl.BlockSpec((tm, s.shape[1]), lambda i: (i, 0)) for s in row_outs]
    out_specs += [pl.BlockSpec(s.shape, lambda i: (0, 0)) for s in acc_outs]
    if n_after:
        in_specs = [pl.BlockSpec(memory_space=pl.ANY)] + in_specs
    return pl.pallas_call(
        body, name=name, grid=(M // tm,), in_specs=in_specs, out_specs=out_specs,
        out_shape=list(row_outs) + list(acc_outs),
        compiler_params=_cp(("arbitrary",) if n_a else ("parallel",)),
    )(*([after] if n_after else []), *[r[0] for r in row_ins], *bc_ins)


def _full(a, first_row_block=0):
    return (a, a.shape[1], 0, first_row_block)


def _conv_rows(ext, w_ref_val, lo):
    s1 = pltpu.roll(ext, 1, 0)
    s2 = pltpu.roll(ext, 2, 0)
    y = w_ref_val[0:1, :] * s2 + w_ref_val[1:2, :] * s1 + w_ref_val[2:3, :] * ext
    return y[SUBLANES:, :]


def _ffn_in(v, w_gate, w_up, conv_w8, conv_b, *, name, tm=256):
    T, K = v.shape
    N = w_gate[1]
    tm = min(tm, T)

    def body(v_ref, wg_ref, wu_ref, cw_ref, cb_ref, gp_ref, up_ref, gate_ref, act_ref, carry_sc):
        @pl.when(pl.program_id(0) == 0)
        def _():
            carry_sc[...] = jnp.zeros_like(carry_sc)

        vv = v_ref[...]
        dn = (((1,), (1,)), ((), ()))
        gp = lax.dot_general(vv, wg_ref[...], dn, preferred_element_type=F32)
        up = lax.dot_general(vv, wu_ref[...], dn, preferred_element_type=F32)
        gp_ref[...] = gp.astype(gp_ref.dtype)
        up_ref[...] = up.astype(up_ref.dtype)
        gate = _conv_rows(jnp.concatenate([carry_sc[...], gp], axis=0), cw_ref[...], 0) + cb_ref[...]
        gate_ref[...] = gate
        act_ref[...] = (gate * _sigmoid(gate) * up).astype(act_ref.dtype)
        carry_sc[...] = gp[tm - SUBLANES:, :]

    tile = pl.BlockSpec((tm, N), lambda i: (i, 0))
    return pl.pallas_call(
        body, name=name, grid=(T // tm,),
        in_specs=[pl.BlockSpec((tm, K), lambda i: (i, 0)), _wspec(w_gate), _wspec(w_up),
                  pl.BlockSpec((SUBLANES, N), lambda i: (0, 0)), pl.BlockSpec((1, N), lambda i: (0, 0))],
        out_specs=[tile] * 4,
        out_shape=[_sds((T, N), BF16), _sds((T, N), BF16), _sds((T, N), F32), _sds((T, N), BF16)],
        scratch_shapes=[pltpu.VMEM((SUBLANES, N), F32)],
        compiler_params=_cp(("arbitrary",)),
    )(v, w_gate[0], w_up[0], conv_w8, conv_b)


def _ffn_back(dh2, w_down, gp, up, gate, conv_w8, *, name, tr=512, tc=1408):
    T, C = gp.shape
    K = dh2.shape[1]
    warr, _, wblk = w_down
    tr = min(tr, T)
    nr = T // tr
    ncb = C // tc

    def body(dh_ref, wd_ref, gp_ref, up_ref, gate_ref, w_ref, dgp_ref, dup_ref, dw_ref, db_ref, carry_sc):
        @pl.when(pl.program_id(1) == 0)
        def _():
            carry_sc[...] = jnp.zeros_like(carry_sc)
            dw_ref[...] = jnp.zeros_like(dw_ref)
            db_ref[...] = jnp.zeros_like(db_ref)

        w = w_ref[...]
        dact = lax.dot_general(dh_ref[...], wd_ref[...], (((1,), (1,)), ((), ())), preferred_element_type=F32)
        gpc = gp_ref[...].astype(F32)
        gate = gate_ref[...]
        sg = _sigmoid(gate)
        silu = gate * sg
        dup_ref[...] = (dact * silu).astype(dup_ref.dtype)
        dgate = dact * up_ref[...].astype(F32) * (sg + silu * (1.0 - sg))
        ext = jnp.concatenate([dgate, carry_sc[...]], axis=0)
        n = tr + SUBLANES
        g1 = pltpu.roll(ext, n - 1, 0)[:tr]
        g2 = pltpu.roll(ext, n - 2, 0)[:tr]
        dgp_ref[...] = (w[2:3, :] * dgate + w[1:2, :] * g1 + w[0:1, :] * g2).astype(dgp_ref.dtype)
        dw0 = jnp.sum(gpc * g2, axis=0, keepdims=True)
        dw1 = jnp.sum(gpc * g1, axis=0, keepdims=True)
        dw2 = jnp.sum(gpc * dgate, axis=0, keepdims=True)
        z = jnp.zeros((SUBLANES - 3, gpc.shape[1]), F32)
        dw_ref[...] += jnp.concatenate([dw0, dw1, dw2, z], axis=0)
        db_ref[...] += jnp.sum(dgate, axis=0, keepdims=True)
        carry_sc[...] = dgate[:SUBLANES]

    rev = lambda i: nr - 1 - i
    cur = pl.BlockSpec((tr, tc), lambda j, i: (rev(i), j))
    return pl.pallas_call(
        body, name=name, grid=(ncb, nr),
        in_specs=[pl.BlockSpec((tr, K), lambda j, i: (rev(i), 0)),
                  pl.BlockSpec((tc, K), lambda j, i: (wblk * ncb + j, 0)),
                  cur, cur, cur,
                  pl.BlockSpec((SUBLANES, tc), lambda j, i: (0, j))],
        out_specs=[cur, cur,
                   pl.BlockSpec((SUBLANES, tc), lambda j, i: (0, j)),
                   pl.BlockSpec((1, tc), lambda j, i: (0, j))],
        out_shape=[_sds((T, C), BF16), _sds((T, C), BF16), _sds((SUBLANES, C), F32), _sds((1, C), F32)],
        scratch_shapes=[pltpu.VMEM((SUBLANES, tc), F32)],
        compiler_params=_cp(("parallel", "arbitrary")),
    )(dh2, warr, gp, up, gate, conv_w8)


def _cumsum_rows(x):
    n = x.shape[0]
    row = lax.broadcasted_iota(jnp.int32, x.shape, 0)
    s = 1
    while s < n:
        x = x + jnp.where(row >= s, pltpu.roll(x, s, 0), 0.0)
        s *= 2
    return x


def _rcumsum_rows(x):
    n = x.shape[0]
    row = lax.broadcasted_iota(jnp.int32, x.shape, 0)
    s = 1
    while s < n:
        x = x + jnp.where(row < n - s, pltpu.roll(x, n - s, 0), 0.0)
        s *= 2
    return x


def _dot_nt(a, b):
    return lax.dot_general(a.astype(BF16), b.astype(BF16), (((1,), (1,)), ((), ())), preferred_element_type=F32)


def _dot_tn(a, b):
    return lax.dot_general(a.astype(BF16), b.astype(BF16), (((0,), (0,)), ((), ())), preferred_element_type=F32)


def _dot_nn(a, b):
    return jnp.dot(a.astype(BF16), b.astype(BF16), preferred_element_type=F32)


def _dot3(a, b, contract):
    def split(x):
        hi = x.astype(BF16)
        return hi, (x - hi.astype(F32)).astype(BF16)

    a_hi, a_lo = split(a)
    b_hi, b_lo = split(b)
    dot = lambda x, y: lax.dot_general(x, y, (contract, ((), ())), preferred_element_type=F32)
    return dot(a_hi, b_hi) + (dot(a_hi, b_lo) + dot(a_lo, b_hi))


NT, TN, NN = ((1,), (1,)), ((0,), (0,)), ((1,), (0,))


def _hg_gates(hq, hf, lbv):
    sig = _sigmoid(hf)
    f = lbv + (1.0 - lbv) * sig
    return sig, f, jnp.log(f), 1.0 - f, hq * (HG_DK ** -0.5)


def _hg_sel_rows(ref, sp):
    return jnp.concatenate(
        [jnp.broadcast_to(ref[pl.ds(HG_SUB * i + sp, 1), :], (HG_SUB, HG_DK)) for i in range(HG_CHUNK // HG_SUB)], axis=0)


def _hg_masks():
    C = HG_CHUNK
    row = lax.broadcasted_iota(jnp.int32, (C, C), 0)
    col = lax.broadcasted_iota(jnp.int32, (C, C), 1)
    d = col - (row // HG_SUB) * HG_SUB
    tmod = row % HG_SUB
    diag_valid = jnp.logical_and(d >= 0, d <= tmod)
    return row, col, d, diag_valid


def _hg_strip_keys(k, b, r, n):
    ek = jnp.exp(r - b[:n])
    return ek, jnp.concatenate([k[:n] * ek, jnp.zeros((HG_CHUNK - n, k.shape[1]), F32)], axis=0)


def _hg_scores(q, k, b, b_sc, k_sc):
    C, S = HG_CHUNK, HG_SUB
    row, col, d, diag_valid = _hg_masks()
    blocks = [jnp.zeros((S, C), F32)]
    for i in range(1, C // S):
        r = b_sc[pl.ds(S * i - 1, 1), :]
        qi = q[S * i:S * (i + 1)] * jnp.exp(b[S * i:S * (i + 1)] - r)
        blocks.append(_dot_nt(qi, _hg_strip_keys(k, b, r, S * i)[1]))
    a_off = jnp.concatenate(blocks, axis=0)
    a_d = jnp.zeros((C, C), F32)
    for sp in range(S):
        bs = _hg_sel_rows(b_sc, sp)
        ks = _hg_sel_rows(k_sc, sp)
        e = jnp.exp(jnp.minimum(b - bs, 0.0))
        colv = jnp.sum(q * ks * e, axis=-1, keepdims=True)
        a_d = jnp.where(d == sp, colv, a_d)
    return a_off + jnp.where(diag_valid, a_d, 0.0)


def _hg_prep(hq_v, hf_v, lbv, b_sc, k_sc):
    sig, f, g, k, q = _hg_gates(hq_v, hf_v, lbv)
    b = _cumsum_rows(g)
    b_sc[...] = b
    k_sc[...] = k
    return sig, f, k, q, b, b_sc[pl.ds(HG_CHUNK - 1, 1), :]


def _hgrn_fwd(hq, hf, hi, lb, *, name):
    T = hq.shape[0]
    C, H, K = HG_CHUNK, HG_HEADS, HG_DK
    NC = T // C

    def body(hq_ref, hf_ref, hi_ref, lb_ref, o_ref, st_ref, s_sc, b_sc, k_sc):
        @pl.when(pl.program_id(0) == 0)
        def _():
            s_sc[...] = jnp.zeros_like(s_sc)

        st_all = s_sc[...]
        for j in range(P):
            rows = slice(C * j, C * (j + 1))
            st_ref[j] = st_all
            outs, news = [], []
            for h in range(H):
                sl = slice(K * h, K * (h + 1))
                _, _, k, q, b, bc = _hg_prep(hq_ref[rows, sl], hf_ref[rows, sl], lb_ref[:, sl], b_sc.at[j, h], k_sc.at[j, h])
                v = hi_ref[rows, sl]
                st0 = st_all[:, sl]
                a = _hg_scores(q, k, b, b_sc.at[j, h], k_sc.at[j, h])
                outs.append(_dot_nn(a, v) + _dot_nt(q * jnp.exp(b), st0))
                news.append(st0 * jnp.exp(bc) + _dot_tn(v, k * jnp.exp(bc - b)))
            o_ref[rows, :] = jnp.concatenate(outs, axis=1)
            st_all = jnp.concatenate(news, axis=1)
        s_sc[...] = st_all

    P = HG_FWD_CHUNKS_PER_STEP
    blk = pl.BlockSpec((P * C, H * K), lambda c: (c, 0))
    return pl.pallas_call(
        body, name=name, grid=(NC // P,),
        in_specs=[blk, blk, blk, pl.BlockSpec((1, H * K), lambda c: (0, 0))],
        out_specs=[blk, pl.BlockSpec((P, K, H * K), lambda c: (c, 0, 0))],
        out_shape=[_sds((T, H * K), F32), _sds((NC, K, H * K), F32)],
        scratch_shapes=[pltpu.VMEM((K, H * K), F32), pltpu.VMEM((P, H, C, K), F32), pltpu.VMEM((P, H, C, K), F32)],
        compiler_params=_cp(("arbitrary",)),
    )(hq, hf, hi, lb)


def _hgrn_bwd(hq, hf, hi, lb, states, do, *, name):
    T = hq.shape[0]
    C, H, K, S = HG_CHUNK, HG_HEADS, HG_DK, HG_SUB
    NC = T // C

    def intra_slow(q, k, b, da, b_sc, k_sc):
        row, col, d, diag_valid = _hg_masks()
        a_blocks = [jnp.zeros((S, C), F32)]
        dq_blocks = [jnp.zeros((S, K), F32)]
        dk = jnp.zeros((C, K), F32)
        for i in range(1, C // S):
            r = b_sc[pl.ds(S * i - 1, 1), :]
            eq = jnp.exp(b[S * i:S * (i + 1)] - r)
            ek = jnp.exp(jnp.minimum(r - b, 0.0))
            qi = q[S * i:S * (i + 1)] * eq
            kk = k * ek
            a_blocks.append(_dot_nt(qi, kk))
            dai = jnp.where(col[S * i:S * (i + 1)] < S * i, da[S * i:S * (i + 1)], 0.0)
            dq_blocks.append(_dot_nn(dai, kk) * eq)
            dk = dk + _dot_tn(dai, qi) * ek
        dq = jnp.concatenate(dq_blocks, axis=0)
        a_off = jnp.where(col < (row // S) * S, jnp.concatenate(a_blocks, axis=0), 0.0)
        same_blk = (row // S == col // S).astype(BF16)
        tmod = (lax.broadcasted_iota(jnp.int32, (C, K), 0)) % S
        a_d = jnp.zeros((C, C), F32)
        dk_d = jnp.zeros((C, K), F32)
        for sp in range(S):
            bs = _hg_sel_rows(b_sc, sp)
            ks = _hg_sel_rows(k_sc, sp)
            e = jnp.exp(jnp.minimum(b - bs, 0.0))
            eks = e * ks
            a_d = jnp.where(d == sp, jnp.sum(q * eks, axis=-1, keepdims=True), a_d)
            dacol = jnp.sum(jnp.where(d == sp, da, 0.0), axis=-1, keepdims=True)
            dq = dq + dacol * eks
            wq = dacol * e * q
            wq_hi = wq.astype(BF16)
            wq_lo = (wq - wq_hi.astype(F32)).astype(BF16)
            blk_sum = (jnp.dot(same_blk, wq_hi, preferred_element_type=F32)
                       + jnp.dot(same_blk, wq_lo, preferred_element_type=F32))
            dk_d = jnp.where(tmod == sp, blk_sum, dk_d)
        return a_off + jnp.where(diag_valid, a_d, 0.0), dq, dk + dk_d

    def one_head(pre, v, lbv, st0, dst1, dout, b_sc, k_sc):
        sig, f, k, q, b, bc = pre
        ebc = jnp.exp(bc)
        eb = jnp.exp(b)
        ekb = jnp.exp(bc - b)
        qt = q * eb
        kb = k * ekb
        row = lax.broadcasted_iota(jnp.int32, (C, C), 0)
        col = lax.broadcasted_iota(jnp.int32, (C, C), 1)
        da = jnp.where(col <= row, _dot_nt(dout, v), 0.0)
        dkb = _dot_nn(v, dst1)
        new_ds = _dot_tn(dout, qt) + dst1 * ebc
        a, dq_i, dk_i = intra_slow(q, k, b, da, b_sc, k_sc)
        dq = _dot_nn(dout, st0) * eb + dq_i
        dk = dkb * ekb + dk_i
        dv = _dot_tn(a, dout) + _dot_nt(kb, dst1)
        extra = jnp.sum(dkb * kb, axis=0, keepdims=True) + ebc * jnp.sum(st0 * dst1, axis=0, keepdims=True)
        rowk = lax.broadcasted_iota(jnp.int32, (C, K), 0)
        db = q * dq - k * dk + jnp.where(rowk == C - 1, extra, 0.0)
        dg = _rcumsum_rows(db)
        df = dg / f - dk
        return (dq * (K ** -0.5), df * (1.0 - lbv) * sig * (1.0 - sig), dv,
                jnp.sum(df * (1.0 - sig), axis=0, keepdims=True), new_ds)

    def body(hq_ref, hf_ref, hi_ref, lb_ref, st_ref, do_ref, dq_ref, dhf_ref, dv_ref, dlb_ref, ds_sc, b_sc, k_sc):
        @pl.when(pl.program_id(0) == 0)
        def _():
            ds_sc[...] = jnp.zeros_like(ds_sc)
            dlb_ref[...] = jnp.zeros_like(dlb_ref)

        ds_all = ds_sc[...]
        dlb = jnp.zeros((1, H * K), F32)
        for j in reversed(range(P)):
            rows = slice(C * j, C * (j + 1))
            st_all = st_ref[j]
            res = []
            for h in range(H):
                sl = slice(K * h, K * (h + 1))
                pre = _hg_prep(hq_ref[rows, sl], hf_ref[rows, sl], lb_ref[:, sl], b_sc.at[j, h], k_sc.at[j, h])
                res.append(one_head(pre, hi_ref[rows, sl], lb_ref[:, sl], st_all[:, sl], ds_all[:, sl], do_ref[rows, sl],
                                    b_sc.at[j, h], k_sc.at[j, h]))
            cat = lambda i: jnp.concatenate([r[i] for r in res], axis=1)
            dq_ref[rows, :] = cat(0).astype(dq_ref.dtype)
            dhf_ref[rows, :] = cat(1).astype(dhf_ref.dtype)
            dv_ref[rows, :] = cat(2).astype(dv_ref.dtype)
            dlb = dlb + cat(3)
            ds_all = cat(4)
        dlb_ref[...] += dlb
        ds_sc[...] = ds_all

    P = HG_CHUNKS_PER_STEP
    NS = NC // P
    blk = pl.BlockSpec((P * C, H * K), lambda c: (NS - 1 - c, 0))
    par = pl.BlockSpec((1, H * K), lambda c: (0, 0))
    return pl.pallas_call(
        body, name=name, grid=(NS,),
        in_specs=[blk, blk, blk, par, pl.BlockSpec((P, K, H * K), lambda c: (NS - 1 - c, 0, 0)), blk],
        out_specs=[blk, blk, blk, par],
        out_shape=[_sds((T, H * K), BF16)] * 3 + [_sds((1, H * K), F32)],
        scratch_shapes=[pltpu.VMEM((K, H * K), F32), pltpu.VMEM((P, H, C, K), F32), pltpu.VMEM((P, H, C, K), F32)],
        compiler_params=_cp(("arbitrary",)),
    )(hq, hf, hi, lb, states, do)


ATT_STACK = ATT_GROUP


def _att_valid(n):
    R, B = ATT_STACK * ATT_BLOCK, ATT_BLOCK
    j = lax.broadcasted_iota(jnp.int32, (2 * B, R), 0)
    t = lax.broadcasted_iota(jnp.int32, (2 * B, R), 1) % B
    dist = t + B - j
    first_key = jnp.where(n > 0, 0, B)
    return jnp.logical_and(jnp.logical_and(dist >= 0, dist < B), j >= first_key)


def _att_load(cur_ref, prev_ref, ba_ref, h0):
    hd = ATT_HD
    kv = h0 // ATT_GROUP
    def cols(ref, c0):
        return ref[:, c0:c0 + hd] + ba_ref[:, c0:c0 + hd]
    qs = jnp.concatenate([cols(cur_ref, hd * (h0 + g)) for g in range(ATT_STACK)], axis=0)
    kc = jnp.concatenate([cols(prev_ref, ATT_Q_W + hd * kv), cols(cur_ref, ATT_Q_W + hd * kv)], axis=0)
    vc = jnp.concatenate([cols(prev_ref, ATT_Q_W + ATT_KV_W + hd * kv), cols(cur_ref, ATT_Q_W + ATT_KV_W + hd * kv)], axis=0)
    return qs, kc, vc


def _att_probs(qs, kc, valid, sink_ref, h0):
    scale = 1.0 / math.sqrt(ATT_HD)
    s = jnp.where(valid, _dot_nt(kc, qs) * scale, NEG)
    sink = jnp.concatenate([jnp.full((1, ATT_BLOCK), sink_ref[0, h0 + g], F32) for g in range(ATT_STACK)], axis=1)
    m = jnp.maximum(jnp.max(s, axis=0, keepdims=True), sink)
    p = jnp.exp(s - m)
    ps = jnp.exp(sink - m)
    inv = 1.0 / (jnp.sum(p, axis=0, keepdims=True) + ps)
    return p * inv, ps * inv


def _attn_fwd(att, b_attn, sinks, *, name, after=None):
    T = att.shape[0]
    B = ATT_BLOCK
    NB = T // B
    lead = [] if after is None else [after]

    def body(*refs):
        sink_ref, cur_ref, prev_ref, ba_ref, o_ref = refs[len(lead):]
        valid = _att_valid(pl.program_id(0))
        outs = []
        for h0 in range(0, ATT_HEADS, ATT_STACK):
            qs, kc, vc = _att_load(cur_ref, prev_ref, ba_ref, h0)
            prob, _ = _att_probs(qs, kc, valid, sink_ref, h0)
            o = _dot_tn(prob, vc)
            outs += [o[B * g:B * (g + 1)] for g in range(ATT_STACK)]
        o_ref[...] = jnp.concatenate(outs, axis=1)

    return pl.pallas_call(
        body, name=name, grid=(NB,),
        in_specs=[pl.BlockSpec(memory_space=pl.ANY)] * len(lead) + [
            pl.BlockSpec(memory_space=pltpu.SMEM),
            pl.BlockSpec((B, ATT_COLS), lambda n: (n, 0)),
            pl.BlockSpec((B, ATT_COLS), lambda n: (jnp.maximum(n - 1, 0), 0)),
            pl.BlockSpec((1, ATT_COLS), lambda n: (0, 0))],
        out_specs=pl.BlockSpec((B, ATT_Q_W), lambda n: (n, 0)),
        out_shape=_sds((T, ATT_Q_W), F32),
        compiler_params=_cp(("parallel",)),
    )(*lead, sinks, att, att, b_attn)


def _attn_bwd(att, b_attn, sinks, dmix, *, name):
    T = att.shape[0]
    B, hd = ATT_BLOCK, ATT_HD
    NB = T // B
    scale = 1.0 / math.sqrt(hd)

    def body(sink_ref, cur_ref, prev_ref, ba_ref, do_ref, daq_ref, dakv_ref, dsink_ref, dbq_ref, dbkv_ref, carry_sc):
        n = pl.program_id(0)

        @pl.when(n == 0)
        def _():
            carry_sc[...] = jnp.zeros_like(carry_sc)
            dsink_ref[...] = jnp.zeros_like(dsink_ref)
            dbq_ref[...] = jnp.zeros_like(dbq_ref)
            dbkv_ref[...] = jnp.zeros_like(dbkv_ref)

        @pl.when(n < NB)
        def _():
            valid = _att_valid(n)
            hrow = lax.broadcasted_iota(jnp.int32, (SUBLANES, 128), 0)
            dsink = jnp.zeros((SUBLANES, 128), F32)
            dqs = []
            dks = [jnp.zeros((2 * B, hd), F32)] * ATT_KV
            dvs = [jnp.zeros((2 * B, hd), F32)] * ATT_KV
            for h0 in range(0, ATT_HEADS, ATT_STACK):
                kv = h0 // ATT_GROUP
                qs, kc, vc = _att_load(cur_ref, prev_ref, ba_ref, h0)
                prob, psink = _att_probs(qs, kc, valid, sink_ref, h0)
                dout = jnp.concatenate([do_ref[:, hd * (h0 + g):hd * (h0 + g + 1)] for g in range(ATT_STACK)], axis=0)
                dp = _dot_nt(vc, dout)
                delta = jnp.sum(prob * dp, axis=0, keepdims=True)
                dsc = prob * (dp - delta) * scale
                dq = _dot_tn(dsc, kc)
                dks[kv] = dks[kv] + _dot_nn(dsc, qs)
                dvs[kv] = dvs[kv] + _dot_nn(prob, dout)
                dsk = psink * delta
                for g in range(ATT_STACK):
                    dqs.append(dq[B * g:B * (g + 1)])
                    tot = jnp.sum(dsk[:, B * g:B * (g + 1)], axis=1, keepdims=True)
                    dsink = dsink - jnp.where(hrow == h0 + g, tot, 0.0)
            daq = jnp.concatenate(dqs, axis=1).astype(daq_ref.dtype)
            daq_ref[...] = daq
            dsink_ref[...] += dsink
            dbq_ref[...] += jnp.sum(daq.astype(F32), axis=0, keepdims=True)
            done = carry_sc[...] + jnp.concatenate([d[:B] for d in dks + dvs], axis=1)
            dakv_ref[...] = done.astype(dakv_ref.dtype)
            dbkv_ref[...] += jnp.sum(done.astype(dakv_ref.dtype).astype(F32), axis=0, keepdims=True)
            carry_sc[...] = jnp.concatenate([d[B:] for d in dks + dvs], axis=1)

        @pl.when(n == NB)
        def _():
            done = carry_sc[...]
            dakv_ref[...] = done.astype(dakv_ref.dtype)
            dbkv_ref[...] += jnp.sum(done.astype(dakv_ref.dtype).astype(F32), axis=0, keepdims=True)

    cl = lambda n: jnp.minimum(n, NB - 1)
    return pl.pallas_call(
        body, name=name, grid=(NB + 1,),
        in_specs=[pl.BlockSpec(memory_space=pltpu.SMEM),
                  pl.BlockSpec((B, ATT_COLS), lambda n: (cl(n), 0)),
                  pl.BlockSpec((B, ATT_COLS), lambda n: (jnp.maximum(cl(n) - 1, 0), 0)),
                  pl.BlockSpec((1, ATT_COLS), lambda n: (0, 0)),
                  pl.BlockSpec((B, ATT_Q_W), lambda n: (cl(n), 0))],
        out_specs=[pl.BlockSpec((B, ATT_Q_W), lambda n: (cl(n), 0)),
                   pl.BlockSpec((B, 2 * ATT_KV_W), lambda n: (jnp.maximum(n - 1, 0), 0)),
                   pl.BlockSpec((SUBLANES, 128), lambda n: (0, 0)),
                   pl.BlockSpec((1, ATT_Q_W), lambda n: (0, 0)),
                   pl.BlockSpec((1, 2 * ATT_KV_W), lambda n: (0, 0))],
        out_shape=[_sds((T, ATT_Q_W), BF16), _sds((T, 2 * ATT_KV_W), BF16), _sds((SUBLANES, 128), F32),
                   _sds((1, ATT_Q_W), F32), _sds((1, 2 * ATT_KV_W), F32)],
        scratch_shapes=[pltpu.VMEM((B, 2 * ATT_KV_W), F32)],
        compiler_params=_cp(("arbitrary",)),
    )(sinks, att, att, b_attn, dmix)


def _silu_and_grad(x):
    sg = _sigmoid(x)
    return x * sg, sg * (1.0 + x * (1.0 - sg))


def _mix_fwd_fn(o_raw, hg, o_att, hgw):
    outs = []
    for h in range(HG_HEADS):
        sl = slice(HG_DK * h, HG_DK * (h + 1))
        silu, _ = _silu_and_grad(hg[:, sl])
        outs.append(_rms_fwd(o_raw[:, sl], hgw) * silu)
    outs.append(o_att)
    return (jnp.concatenate(outs, axis=1),)


def _mix_bwd_fn(o_raw, hg, dmix, hgw):
    dos, dhgs = [], []
    dw = jnp.zeros((1, HG_DK), F32)
    for h in range(HG_HEADS):
        sl = slice(HG_DK * h, HG_DK * (h + 1))
        silu, dsilu = _silu_and_grad(hg[:, sl])
        dy = dmix[:, sl]
        dhgs.append(dy * _rms_fwd(o_raw[:, sl], hgw) * dsilu)
        dx, dwh = _rms_bwd(o_raw[:, sl], hgw, dy * silu)
        dos.append(dx)
        dw = dw + dwh
    return jnp.concatenate(dos, axis=1), jnp.concatenate(dhgs, axis=1), dw


def _final_fn(h2, tgt, wf):
    d = h2.shape[1]
    err = _rms_fwd(h2, wf) - tgt
    loss_cols = (0.5 / d) * jnp.sum(err * err, axis=0, keepdims=True)
    dh2, dwf = _rms_bwd(h2, wf, err * (1.0 / d))
    return dh2, dh2, loss_cols, dwf


class _NoExchange:
    def __init__(self, weights):
        self.weights = weights

    def start(self):
        return None

    def w_in(self, after):
        return self.weights["w_in_t"]

    def mid(self, after):
        return None

    def w_out(self, after):
        return {k: self.weights[k] for k in ("w_out", "conv_w8")}

    def rest(self, after):
        return {k: self.weights[k] for k in ("w_gate_t", "w_up_t", "w_down")}

    def ffn_grads(self, gs):
        return None

    def ffn_grads_send(self, after):
        return None


def _local_step(x, tgt, p, ex):
    T, D = x.shape
    row = lambda n, dt: _sds((T, n), dt)
    acc = lambda n: _sds((1, n), F32)

    (u,) = _rowwise(lambda xv, w: (_rms_fwd(xv, w),), [_full(x)], [p["norm_mix_w"]], [row(D, BF16)], [], name="rms_mix",
                    after=ex.start())
    p = dict(p, w_in_t=ex.w_in(u))
    hq, hf, hi, hg, att = _mm_nt(u, p["w_in_t"], splits=[HG_W] * 4 + [ATT_COLS], out_dtype=F32, name="in_proj")
    o_raw, states = _hgrn_fwd(hq, hf, hi, p["lb"], name="hgrn_fwd")
    o_att = _attn_fwd(att, p["b_attn"], p["sinks"], name="attn_fwd", after=ex.mid(o_raw))
    p = dict(p, **ex.w_out(o_att))
    def out_epilogue(prod, xv, w):
        h1v = prod + xv
        return h1v, _rms_fwd(h1v, w)

    h1, v, mix = _mm_nn(None, [p["w_out"]], name="mix_out_proj",
                        prologue=(lambda *a: _mix_fwd_fn(*a)[0], [o_raw, hg, o_att], [p["hg_norm_w"]], row(D, BF16)),
                        epilogue=(out_epilogue, [x], [p["norm_ffn_w"]], [row(D, F32), row(D, BF16)], []))
    p = dict(p, **ex.rest(v))
    gp, up, gate, act = _ffn_in(v, p["w_gate_t"], p["w_up_t"], p["conv_w8"], p["conv_b"], name="ffn_in")
    def down_epilogue(prod, h1v, tgtv, wf):
        return _final_fn(prod + h1v, tgtv, wf)

    dh2, dh2_b, loss_cols, d_final = _mm_nn(
        [[act]], [p["w_down"]], name="down_proj_loss",
        epilogue=(down_epilogue, [h1, tgt], [p["final_norm_w"]], [row(D, F32), row(D, BF16)], [acc(D), acc(D)]))

    g_down = _mm_tn([act], dh2_b, name="g_down")
    dgp, dup, d_conv_w8, d_conv_b = _ffn_back(dh2_b, p["w_down"], gp, up, gate, p["conv_w8"], name="ffn_back")
    g_gate_t = _mm_tn([dgp], v, name="g_gate")
    g_up_t = _mm_tn([dup], v, name="g_up")
    swapping = ex.ffn_grads([g_gate_t, g_up_t, g_down])

    def ffn_norm_bwd(dvv, hv, dh2v, w):
        dx, dw = _rms_bwd(hv, w, dvv)
        dh1v = dx + dh2v
        return dh1v, dh1v, dw

    dh1, dh1_b, d_norm_ffn = _mm_nn(
        [[dgp], [dup]], [p["w_gate_t"], p["w_up_t"]], name="d_v_norm", after=swapping,
        epilogue=(ffn_norm_bwd, [h1, dh2], [p["norm_ffn_w"]], [row(D, F32), row(D, BF16)], [acc(D)]))
    sent = ex.ffn_grads_send(dh1_b)
    def mix_bwd(dmixv, o_rawv, hgv, hgw):
        do_rawv, dhgv, dw = _mix_bwd_fn(o_rawv, hgv, dmixv[:, :HG_W], hgw)
        return do_rawv, dhgv, dmixv[:, HG_W:], dw

    do_raw, dhg, do_att, d_hg_norm = _mm_nn(
        [[dh1_b]], [p["w_out"]], name="d_mix_bwd", w_transposed=True, after=sent,
        epilogue=(mix_bwd, [o_raw, hg], [p["hg_norm_w"]], [row(HG_W, F32), row(HG_W, BF16), row(ATT_Q_W, F32)], [acc(HG_DK)]))
    g_out = _mm_tn([mix], dh1_b, name="g_out")
    daq, dakv, d_sinks8, d_bq, d_bkv = _attn_bwd(att, p["b_attn"], p["sinks"], do_att, name="attn_bwd")
    dhq, dhf, dhi, d_lb = _hgrn_bwd(hq, hf, hi, p["lb"], states, do_raw, name="hgrn_bwd")
    pieces = [dhq, dhf, dhi, dhg, daq, dakv]
    g_in_t = _mm_tn(pieces, u, name="g_in")

    def mix_norm_bwd(duv, xv, dh1v, w):
        dx, dw = _rms_bwd(xv, w, duv)
        return dx + dh1v, dw

    dx, d_norm_mix = _mm_nn([pieces], [p["w_in_t"]], name="d_u_norm",
                            epilogue=(mix_norm_bwd, [x, dh1], [p["norm_mix_w"]], [row(D, F32)], [acc(D)]))
    grads = dict(g_in_t=g_in_t, g_out=g_out, g_gate_t=g_gate_t, g_up_t=g_up_t, g_down=g_down,
                 norm_mix_w=d_norm_mix, b_attn=jnp.concatenate([d_bq, d_bkv], axis=1), lb=d_lb, hg_norm_w=d_hg_norm,
                 sinks8=d_sinks8, norm_ffn_w=d_norm_ffn, conv_w8=d_conv_w8, conv_b=d_conv_b, final_norm_w=d_final)
    return loss_cols, dx, grads


SLAB = (IN_COLS // N_CHIPS, D_FF // N_CHIPS, D_FF // N_CHIPS, D_FF // N_CHIPS, D_MODEL // N_CHIPS)
N_W = len(SLAB)
PACK_OFF = tuple(sum(SLAB[:i]) for i in range(N_W))
PACK_ROWS = sum(SLAB)
FULL_OFF = tuple(N_CHIPS * o for o in PACK_OFF)
FULL_ROWS = N_CHIPS * PACK_ROWS
HALF = tuple(s // 2 for s in SLAB)
HPACK_OFF = tuple(sum(HALF[:i]) for i in range(N_W))
HPACK_ROWS = sum(HALF)
HFULL_OFF = tuple(N_CHIPS * o for o in HPACK_OFF)
HFULL_ROWS = N_CHIPS * HPACK_ROWS
CHIP_FLIPS = ((1, 0), (0, 1), (1, 1))
N_DEV = 8
BF16_ROWS = 16
ANY = pl.BlockSpec(memory_space=pl.ANY)


def _pos():
    return lax.axis_index("x"), lax.axis_index("y"), lax.axis_index("c")


def _flip(v, f):
    return 1 - v if f else v


def _rcopy(src, dst, ssem, rsem, dev):
    return pltpu.make_async_remote_copy(src_ref=src, dst_ref=dst, send_sem=ssem, recv_sem=rsem, device_id=dev,
                                        device_id_type=pl.DeviceIdType.MESH)


def _rows(ref, start, n, align=None):
    if not isinstance(start, int):
        if align is None:
            align = SUBLANES * (4 // jnp.dtype(ref.dtype).itemsize)
        start = pl.multiple_of(start, align)
    return ref.at[pl.ds(start, n), :]


FFN_W = (1, 2, 3)
N_PEER = 1 + len(CHIP_FLIPS)
HBM = pl.BlockSpec(memory_space=pltpu.HBM)
SEM = pl.BlockSpec(memory_space=pltpu.SEMAPHORE)
EFFECT = pltpu.SideEffectType.DATAFLOW_SIDE_EFFECTING
LANES = 128


def _sent_rows(k, w, c):
    return (0, SLAB[w]) if k == 0 else (c * HALF[w], HALF[w])


def _gather_start(pack, cw8):
    D = pack.shape[1]
    lands = [lax.empty((N_CHIPS * SLAB[0], D), pack.dtype), lax.empty((3 * N_CHIPS * SLAB[1], D), pack.dtype),
             lax.empty((N_CHIPS * SLAB[4], D), pack.dtype), lax.empty((N_CHIPS,) + cw8.shape, cw8.dtype)]
    bufs = [pack, cw8] + lands

    def body(pack_ref, cw_ref, l_in, l_ffn, l_out, l_cw, *rest):
        in_send, in_recv, out_send, out_recv, ffn_send, ffn_recv = rest[:6]
        token = rest[-1]
        x, y, c = _pos()
        q = 2 * x + y
        peers = _gather_peers(x, y, c)

        def send(k, peer, w, land, base, ssem, rsem):
            r0, n = _sent_rows(k, w, c)
            _rcopy(_rows(pack_ref, PACK_OFF[w] + r0, n), _rows(land, base + q * SLAB[w] + r0, n), ssem, rsem, peer).start()

        for k, peer in enumerate(peers):
            send(k, peer, 0, l_in, 0, in_send.at[k], in_recv.at[k])
        for k, peer in enumerate(peers):
            send(k, peer, 4, l_out, 0, out_send.at[k], out_recv.at[k])
            _rcopy(cw_ref, l_cw.at[q], out_send.at[N_PEER + k], out_recv.at[N_PEER + k], peer).start()
        for j, w in enumerate(FFN_W):
            for k, peer in enumerate(peers):
                send(k, peer, w, l_ffn, j * N_CHIPS * SLAB[w], ffn_send.at[k], ffn_recv.at[k])
        token[...] = jnp.zeros_like(token)

    n_sem = (N_PEER, N_PEER, 2 * N_PEER, 2 * N_PEER, N_PEER, N_PEER)
    outs = pl.pallas_call(
        body, name="gather_start", in_specs=[HBM] * len(bufs),
        out_specs=[SEM] * len(n_sem) + [HBM] * len(bufs) + [pl.BlockSpec(memory_space=pltpu.VMEM)],
        out_shape=[pltpu.SemaphoreType.DMA((n,)) for n in n_sem]
        + [pltpu.HBM(b.shape, b.dtype) for b in bufs] + [TOKEN],
        input_output_aliases={i: len(n_sem) + i for i in range(len(bufs))},
        compiler_params=pltpu.CompilerParams(has_side_effects=EFFECT),
    )(*[pltpu.with_memory_space_constraint(b, pltpu.HBM) for b in bufs])
    bufs_out = outs[len(n_sem):]
    return dict(in_sems=outs[0:2], out_sems=outs[2:4], ffn_sems=outs[4:6], pack=bufs_out[0], cw=bufs_out[1], l_in=bufs_out[2],
                l_ffn=bufs_out[3], l_out=bufs_out[4], l_cw=bufs_out[5], token=bufs_out[6])


def _gather_peers(x, y, c):
    return [(x, y, 1 - c)] + [(_flip(x, fx), _flip(y, fy), c) for fx, fy in CHIP_FLIPS]


def _gather_wait_in(g, after):
    def body(pack_ref, l_in, send, recv, after_ref, pack_out, l_out):
        for k, peer in enumerate(_gather_peers(*_pos())):
            n = _sent_rows(k, 0, 0)[1]
            cp = _rcopy(_rows(pack_ref, PACK_OFF[0], n), _rows(l_in, 0, n), send.at[k], recv.at[k], peer)
            cp.wait_send()
            cp.wait_recv()

    return pl.pallas_call(
        body, name="gather_wait_in", in_specs=[HBM, HBM, SEM, SEM, ANY], out_specs=[HBM, HBM],
        out_shape=[pltpu.HBM(g["pack"].shape, g["pack"].dtype), pltpu.HBM(g["l_in"].shape, g["l_in"].dtype)],
        input_output_aliases={0: 0, 1: 1}, compiler_params=pltpu.CompilerParams(has_side_effects=EFFECT),
    )(g["pack"], g["l_in"], *g["in_sems"], after)


def _gather_wait_out(g, pack, after):
    def body(pack_ref, cw_ref, l_out, l_cw, o_send, o_recv, after_ref, o_out, o_cw):
        for k, peer in enumerate(_gather_peers(*_pos())):
            n_out = _sent_rows(k, 4, 0)[1]
            for cp in (_rcopy(_rows(pack_ref, PACK_OFF[4], n_out), _rows(l_out, 0, n_out), o_send.at[k], o_recv.at[k], peer),
                       _rcopy(cw_ref, l_cw.at[0], o_send.at[N_PEER + k], o_recv.at[N_PEER + k], peer)):
                cp.wait_send()
                cp.wait_recv()

    ins = [pack, g["cw"], g["l_out"], g["l_cw"]]
    return pl.pallas_call(
        body, name="gather_wait_out", in_specs=[HBM] * 4 + [SEM] * 2 + [ANY], out_specs=[HBM] * 2,
        out_shape=[pltpu.HBM(b.shape, b.dtype) for b in ins[2:]],
        input_output_aliases={2: 0, 3: 1}, compiler_params=pltpu.CompilerParams(has_side_effects=EFFECT),
    )(*ins, *g["out_sems"], after)


def _gather_wait_ffn(g, pack, after):
    def body(pack_ref, l_ffn, f_send, f_recv, after_ref, o_ffn):
        for k, peer in enumerate(_gather_peers(*_pos())):
            n_ffn = len(FFN_W) * _sent_rows(k, FFN_W[0], 0)[1]
            cp = _rcopy(_rows(pack_ref, PACK_OFF[FFN_W[0]], n_ffn), _rows(l_ffn, 0, n_ffn), f_send.at[k], f_recv.at[k], peer)
            cp.wait_send()
            cp.wait_recv()

    return pl.pallas_call(
        body, name="gather_wait_ffn", in_specs=[HBM] * 2 + [SEM] * 2 + [ANY], out_specs=HBM,
        out_shape=pltpu.HBM(g["l_ffn"].shape, g["l_ffn"].dtype),
        input_output_aliases={1: 0}, compiler_params=pltpu.CompilerParams(has_side_effects=EFFECT),
    )(pack, g["l_ffn"], *g["ffn_sems"], after)


FWD_IN = ((0, 0, 0),)
FWD_OUT = ((0, 4, 0),)
FWD_FFN = tuple((0, w, j * N_CHIPS * SLAB[w]) for j, w in enumerate(FFN_W))


def _forward_copies(layout, src, dst, send_sems, recv_sems):
    x, y, c = _pos()
    sib = (x, y, 1 - c)
    cps = []
    for fx, fy in CHIP_FLIPS:
        qa = 2 * _flip(x, fx) + _flip(y, fy)
        for bi, w, base in layout:
            r0 = base + qa * SLAB[w] + c * HALF[w]
            cps.append(_rcopy(_rows(src[bi], r0, HALF[w]), _rows(dst[bi], r0, HALF[w]),
                              send_sems.at[len(cps)], recv_sems.at[len(cps)], sib))
    return cps


def _forward_in(l_in):
    n = len(CHIP_FLIPS) * len(FWD_IN)

    def body(in_ref, out_ref, send_sems, recv_sems):
        cps = _forward_copies(FWD_IN, [in_ref], [out_ref], send_sems, recv_sems)
        for cp in cps:
            cp.start()
        for cp in cps:
            cp.wait_recv()
        for cp in cps:
            cp.wait_send()

    return pl.pallas_call(
        body, name="forward_in", in_specs=[ANY], out_specs=ANY, out_shape=_sds(l_in.shape, l_in.dtype),
        input_output_aliases={0: 0},
        scratch_shapes=[pltpu.SemaphoreType.DMA((n,)), pltpu.SemaphoreType.DMA((n,))],
    )(l_in)


def _forward_start(layout, land, *, name):
    n = len(CHIP_FLIPS) * len(layout)

    def body(in_ref, send_sems, recv_sems, out_ref, token):
        for cp in _forward_copies(layout, [in_ref], [in_ref], send_sems, recv_sems):
            cp.start()
        token[...] = jnp.zeros_like(token)

    outs = pl.pallas_call(
        body, name=name, in_specs=[HBM],
        out_specs=[SEM, SEM, HBM, pl.BlockSpec(memory_space=pltpu.VMEM)],
        out_shape=[pltpu.SemaphoreType.DMA((n,)), pltpu.SemaphoreType.DMA((n,)), pltpu.HBM(land.shape, land.dtype), TOKEN],
        input_output_aliases={0: 2}, compiler_params=pltpu.CompilerParams(has_side_effects=EFFECT),
    )(pltpu.with_memory_space_constraint(land, pltpu.HBM))
    return dict(sems=outs[0:2], land=outs[2], token=outs[3])


def _forward_wait(layout, s, after, *, name):
    def body(in_ref, send_sems, recv_sems, after_ref, out_ref):
        for cp in _forward_copies(layout, [in_ref], [in_ref], send_sems, recv_sems):
            cp.wait_send()
            cp.wait_recv()

    return pl.pallas_call(
        body, name=name, in_specs=[HBM, SEM, SEM, ANY], out_specs=HBM,
        out_shape=pltpu.HBM(s["land"].shape, s["land"].dtype),
        input_output_aliases={0: 0}, compiler_params=pltpu.CompilerParams(has_side_effects=EFFECT),
    )(s["land"], *s["sems"], after)


def _exchange_halves(ws, gs, small, *, name):
    D = gs[0].shape[1]
    n = len(ws)
    has_small = small is not None

    def body(*refs):
        g = refs[:n]
        t = refs[n + has_small:2 * n + has_small]
        sems = refs[2 * n + 2 * has_small:]
        d2d_send, d2d_recv = sems[0], sems[1]
        x, y, c = _pos()
        sib = (x, y, 1 - c)
        drains = []
        for i, w in enumerate(ws):
            h = HALF[w]
            for qq in range(N_CHIPS):
                _rcopy(_rows(g[i], qq * SLAB[w] + (1 - c) * h, h), _rows(t[i], qq * h, h),
                       d2d_send.at[i], d2d_recv.at[i], sib).start()
            drains.append(_rcopy(t[i], t[i], d2d_send.at[i], d2d_recv.at[i], sib))
        if has_small:
            small_ref, sall_ref = refs[n], refs[2 * n + 1]
            sm_send, sm_recv, loc_sem = sems[2], sems[3], sems[4]
            me = 4 * x + 2 * y + c
            own_small = pltpu.make_async_copy(small_ref, sall_ref.at[me], loc_sem)
            own_small.start()
            for f in range(1, N_DEV):
                peer = (_flip(x, f & 4), _flip(y, f & 2), _flip(c, f & 1))
                cp = _rcopy(small_ref, sall_ref.at[me], sm_send.at[f - 1], sm_recv.at[f - 1], peer)
                cp.start()
                drains.append(cp)
        for d in drains:
            d.wait_recv()
        for d in drains:
            d.wait_send()
        if has_small:
            own_small.wait()

    out_shape = [_sds((N_CHIPS * HALF[w], D), gs[0].dtype) for w in ws]
    scratch = [pltpu.SemaphoreType.DMA((n,)), pltpu.SemaphoreType.DMA((n,))]
    if has_small:
        out_shape.append(_sds((N_DEV,) + small.shape, F32))
        scratch += [pltpu.SemaphoreType.DMA((N_DEV - 1,)), pltpu.SemaphoreType.DMA((N_DEV - 1,)), pltpu.SemaphoreType.DMA]
    return pl.pallas_call(
        body, name=name, in_specs=[ANY] * (n + has_small), out_specs=[ANY] * (n + has_small),
        out_shape=out_shape, scratch_shapes=scratch,
    )(*gs, *([small] if has_small else []))


def _halves_copies(ws, g, t, send_sems, recv_sems):
    x, y, c = _pos()
    sib = (x, y, 1 - c)
    cps = []
    for i, w in enumerate(ws):
        h = HALF[w]
        for qq in range(N_CHIPS):
            cps.append(_rcopy(_rows(g[i], qq * SLAB[w] + (1 - c) * h, h), _rows(t[i], qq * h, h),
                              send_sems.at[N_CHIPS * i + qq], recv_sems.at[N_CHIPS * i + qq], sib))
    return cps


def _halves_start(ws, gs, *, name):
    D = gs[0].shape[1]
    n = len(ws)
    bufs = list(gs) + [lax.empty((N_CHIPS * HALF[w], D), gs[0].dtype) for w in ws]

    def body(*refs):
        for cp in _halves_copies(ws, refs[:n], refs[n:2 * n], refs[2 * n], refs[2 * n + 1]):
            cp.start()
        refs[-1][...] = jnp.zeros_like(refs[-1])

    outs = pl.pallas_call(
        body, name=name, in_specs=[HBM] * (2 * n),
        out_specs=[SEM, SEM] + [HBM] * (2 * n) + [pl.BlockSpec(memory_space=pltpu.VMEM)],
        out_shape=[pltpu.SemaphoreType.DMA((N_CHIPS * n,)), pltpu.SemaphoreType.DMA((N_CHIPS * n,))]
        + [pltpu.HBM(b.shape, b.dtype) for b in bufs] + [TOKEN],
        input_output_aliases={i: 2 + i for i in range(2 * n)},
        compiler_params=pltpu.CompilerParams(has_side_effects=EFFECT),
    )(*[pltpu.with_memory_space_constraint(b, pltpu.HBM) for b in bufs])
    return dict(sems=outs[0:2], gs=outs[2:2 + n], theirs=outs[2 + n:2 + 2 * n], token=outs[-1])


def _halves_wait(ws, s, after, *, name):
    n = len(ws)

    def body(*refs):
        for cp in _halves_copies(ws, refs[:n], refs[n:2 * n], refs[2 * n], refs[2 * n + 1]):
            cp.wait_send()
            cp.wait_recv()

    bufs = list(s["gs"]) + list(s["theirs"])
    outs = pl.pallas_call(
        body, name=name, in_specs=[HBM] * (2 * n) + [SEM, SEM, ANY], out_specs=[HBM] * (2 * n),
        out_shape=[pltpu.HBM(b.shape, b.dtype) for b in bufs],
        input_output_aliases={i: i for i in range(2 * n)},
        compiler_params=pltpu.CompilerParams(has_side_effects=EFFECT),
    )(*bufs, *s["sems"], after)
    return outs[:n], outs[n:]


REDUCE_SPLIT = 2


def _chip_partial(ws, gs, theirs, *, name, out_dtype=F32):
    D = gs[0].shape[1]
    n = len(ws)

    def body(*refs):
        for i in range(n):
            refs[2 * n + i][...] = (refs[i][...].astype(F32) + refs[n + i][...].astype(F32)).astype(out_dtype)

    blk = [HALF[w] // REDUCE_SPLIT for w in ws]
    mine = [pl.BlockSpec((b, D), lambda qq, j: ((2 * qq + lax.axis_index("c")) * REDUCE_SPLIT + j, 0)) for b in blk]
    flat = [pl.BlockSpec((b, D), lambda qq, j: (qq * REDUCE_SPLIT + j, 0)) for b in blk]
    return pl.pallas_call(
        body, name=name, grid=(N_CHIPS, REDUCE_SPLIT), in_specs=mine + flat, out_specs=flat,
        out_shape=[_sds((N_CHIPS * HALF[w], D), out_dtype) for w in ws],
        compiler_params=_cp(("parallel", "parallel")),
    )(*gs, *theirs)


def _partial_copies(ws, part, got, send_sems, recv_sems):
    x, y, c = _pos()
    cps = []
    for k, (fx, fy) in enumerate(CHIP_FLIPS):
        peer = (_flip(x, fx), _flip(y, fy), c)
        qp = 2 * _flip(x, fx) + _flip(y, fy)
        for i, w in enumerate(ws):
            cps.append(_rcopy(_rows(part[i], qp * HALF[w], HALF[w]), _rows(got[i], k * HALF[w], HALF[w]),
                              send_sems.at[len(ws) * k + i], recv_sems.at[len(ws) * k + i], peer))
    return cps


def _send_chip_partials(ws, parts, *, name):
    D = parts[0].shape[1]
    n = len(ws)

    def body(*refs):
        cps = _partial_copies(ws, refs[:n], refs[n:2 * n], refs[2 * n], refs[2 * n + 1])
        for cp in cps:
            cp.start()
        for cp in cps:
            cp.wait_recv()
        for cp in cps:
            cp.wait_send()

    return pl.pallas_call(
        body, name=name, in_specs=[ANY] * n, out_specs=[ANY] * n,
        out_shape=[_sds((len(CHIP_FLIPS) * HALF[w], D), parts[0].dtype) for w in ws],
        scratch_shapes=[pltpu.SemaphoreType.DMA((len(CHIP_FLIPS) * n,)), pltpu.SemaphoreType.DMA((len(CHIP_FLIPS) * n,))],
    )(*parts)


def _send_start(ws, parts, *, name):
    D = parts[0].shape[1]
    n = len(ws)
    bufs = list(parts) + [lax.empty((len(CHIP_FLIPS) * HALF[w], D), parts[0].dtype) for w in ws]

    def body(*refs):
        send_sems, recv_sems = refs[2 * n], refs[2 * n + 1]
        for cp in _partial_copies(ws, refs[:n], refs[n:2 * n], send_sems, recv_sems):
            cp.start()
        refs[-1][...] = jnp.zeros_like(refs[-1])

    outs = pl.pallas_call(
        body, name=name, in_specs=[HBM] * (2 * n),
        out_specs=[SEM, SEM] + [HBM] * (2 * n) + [pl.BlockSpec(memory_space=pltpu.VMEM)],
        out_shape=[pltpu.SemaphoreType.DMA((len(CHIP_FLIPS) * n,)), pltpu.SemaphoreType.DMA((len(CHIP_FLIPS) * n,))]
        + [pltpu.HBM(b.shape, b.dtype) for b in bufs] + [TOKEN],
        input_output_aliases={i: 2 + i for i in range(2 * n)},
        compiler_params=pltpu.CompilerParams(has_side_effects=EFFECT),
    )(*[pltpu.with_memory_space_constraint(b, pltpu.HBM) for b in bufs])
    return dict(sems=outs[0:2], parts=outs[2:2 + n], got=outs[2 + n:2 + 2 * n], token=outs[-1])


def _send_wait(ws, s, after, *, name):
    n = len(ws)

    def body(*refs):
        for cp in _partial_copies(ws, refs[:n], refs[n:2 * n], refs[2 * n], refs[2 * n + 1]):
            cp.wait_send()
            cp.wait_recv()

    bufs = list(s["parts"]) + list(s["got"])
    outs = pl.pallas_call(
        body, name=name, in_specs=[HBM] * (2 * n) + [SEM, SEM] + [ANY] * len(after), out_specs=[HBM] * (2 * n),
        out_shape=[pltpu.HBM(b.shape, b.dtype) for b in bufs],
        input_output_aliases={i: i for i in range(2 * n)},
        compiler_params=pltpu.CompilerParams(has_side_effects=EFFECT),
    )(*bufs, *s["sems"], *after)
    return outs[:n], outs[n:]


def _chip_reduce(ws, parts, got, *, name, after=None):
    D = parts[0].shape[1]
    nk = len(CHIP_FLIPS)
    n = len(ws)
    extra = [] if after is None else [after]

    def body(*refs):
        refs = refs[len(extra):]
        outs = refs[(1 + nk) * n:]
        for i in range(n):
            acc = refs[i][...].astype(F32)
            for k in range(nk):
                acc = acc + refs[n * (1 + k) + i][...].astype(F32)
            outs[i][...] = acc

    blk = [HALF[w] // REDUCE_SPLIT for w in ws]

    def q_idx(j):
        return (2 * lax.axis_index("x") + lax.axis_index("y")) * REDUCE_SPLIT + j

    in_specs = [pl.BlockSpec((b, D), lambda j: (q_idx(j), 0)) for b in blk]
    for k in range(nk):
        in_specs += [pl.BlockSpec((b, D), functools.partial(lambda j, k: (k * REDUCE_SPLIT + j, 0), k=k)) for b in blk]
    out_specs = [pl.BlockSpec((b, D), lambda j: (lax.axis_index("c") * REDUCE_SPLIT + j, 0)) for b in blk]
    return pl.pallas_call(
        body, name=name, grid=(REDUCE_SPLIT,), in_specs=[ANY] * len(extra) + in_specs, out_specs=out_specs,
        out_shape=[_sds((SLAB[w], D), F32) for w in ws],
        compiler_params=_cp(("parallel",)),
    )(*extra, *parts, *[g for _ in range(nk) for g in got])


def _exchange_reduced(ws, shards, *, name):
    n = len(ws)

    def body(*refs):
        ins, outs = refs[:n], refs[n:2 * n]
        send_sems, recv_sems = refs[2 * n], refs[2 * n + 1]
        x, y, c = _pos()
        sib = (x, y, 1 - c)
        cps = []
        for i, w in enumerate(ws):
            cp = _rcopy(_rows(ins[i], c * HALF[w], HALF[w]), _rows(outs[i], c * HALF[w], HALF[w]),
                        send_sems.at[i], recv_sems.at[i], sib)
            cp.start()
            cps.append(cp)
        for cp in cps:
            cp.wait_recv()
        for cp in cps:
            cp.wait_send()

    return pl.pallas_call(
        body, name=name, in_specs=[ANY] * n, out_specs=[ANY] * n,
        out_shape=[_sds(s.shape, s.dtype) for s in shards], input_output_aliases={i: i for i in range(n)},
        scratch_shapes=[pltpu.SemaphoreType.DMA((n,)), pltpu.SemaphoreType.DMA((n,))],
    )(*shards)


def _adamw_fn(w, g, m, v):
    m2 = ADAM_B1 * m + (1.0 - ADAM_B1) * g
    v2 = ADAM_B2 * v + (1.0 - ADAM_B2) * (g * g)
    m_hat = m2 / (1.0 - ADAM_B1 ** ADAM_STEP)
    v_hat = v2 / (1.0 - ADAM_B2 ** ADAM_STEP)
    return -ADAM_LR * (m_hat / (jnp.sqrt(v_hat) + ADAM_EPS) + ADAM_WD * w), m2, v2


def _adamw(w, g, m, v, *, name):
    shp = _sds(w.shape, F32)
    rows = w.shape[0]
    tm = max(t for t in range(SUBLANES, 512 + 1, SUBLANES) if rows % t == 0)
    return _rowwise(lambda wv, gv, mv, vv: (gv, *_adamw_fn(wv, gv, mv, vv)), [_full(w), _full(g), _full(m), _full(v)], [],
                    [shp] * 4, [], name=name, tm=tm)


SMALL_SEGS = (("loss", 8), ("norm_mix_w", 8), ("b_attn", 8), ("lb_logits", 8), ("hg_norm_w", 8), ("sinks", 8),
              ("norm_ffn_w", 8), ("conv_w", 72), ("conv_b", 24), ("final_norm_w", 8))
SMALL_OFF = {n: sum(r for _, r in SMALL_SEGS[:i]) for i, (n, _) in enumerate(SMALL_SEGS)}
SMALL_ROWS = sum(r for _, r in SMALL_SEGS)
LANES = 128


def _pack_small(parts):
    segs = []
    for n, r in SMALL_SEGS:
        a = parts.get(n)
        flat = jnp.zeros((0,), F32) if a is None else a.reshape(-1).astype(F32)
        segs.append(jnp.pad(flat, (0, r * LANES - flat.shape[0])).reshape(r, LANES))
    return jnp.concatenate(segs, axis=0)


def _unpack_small(pack, n, shape):
    size = math.prod(shape)
    r0 = SMALL_OFF[n]
    return pack[r0:r0 + dict(SMALL_SEGS)[n]].reshape(-1)[:size].reshape(shape)


def _small_update(sall, wp, mp, vp, *, after):
    R = SMALL_ROWS
    r_lb = SMALL_OFF["lb_logits"]

    def body(after_ref, s_ref, w_ref, m_ref, v_ref, g_ref, d_ref, m2_ref, v2_ref, loss_ref):
        g = s_ref[0]
        for i in range(1, N_DEV):
            g = g + s_ref[i]
        tot = jnp.sum(jnp.sum(g[0:8], axis=1, keepdims=True), axis=0, keepdims=True)
        loss_ref[...] = jnp.broadcast_to(tot, loss_ref.shape)
        lg = w_ref[r_lb:r_lb + 8, :]
        p0 = _sigmoid(lg - pltpu.roll(lg, 4, 0))
        d = g[r_lb:r_lb + 8]
        d = d + pltpu.roll(d, 4, 0)
        sign = jnp.where(lax.broadcasted_iota(jnp.int32, d.shape, 0) < 4, 1.0, -1.0)
        g = jnp.concatenate([g[:r_lb], sign * d * p0 * (1.0 - p0), g[r_lb + 8:]], axis=0)
        g_ref[...] = g
        d_ref[...], m2_ref[...], v2_ref[...] = _adamw_fn(w_ref[...], g, m_ref[...], v_ref[...])

    full = pl.BlockSpec((R, LANES), lambda: (0, 0))
    return pl.pallas_call(
        body, name="small_update",
        in_specs=[ANY, pl.BlockSpec((N_DEV, R, LANES), lambda: (0, 0, 0)), full, full, full],
        out_specs=[full, full, full, full, pl.BlockSpec((8, LANES), lambda: (0, 0))],
        out_shape=[_sds((R, LANES), F32)] * 4 + [_sds((8, LANES), F32)],
        compiler_params=_cp(),
    )(after, sall, wp, mp, vp)


def _lb_fwd(lb_logits):
    n = lb_logits.shape[1]

    def body(l_ref, o_ref):
        o_ref[...] = _sigmoid(l_ref[0:1, :] - l_ref[1:2, :])

    return pl.pallas_call(body, name="lb_fwd", out_shape=jax.ShapeDtypeStruct((1, n), F32), compiler_params=_cp())(lb_logits)


class _MeshExchange:
    def __init__(self, pack, cw8):
        self.gather = _gather_start(pack, cw8)
        self.sent = None
        self.conv_w8 = None

    def start(self):
        return self.gather["token"]

    def w_in(self, after):
        self.pack, l_in = _gather_wait_in(self.gather, after)
        return (_forward_in(l_in), N_CHIPS * SLAB[0], 0)

    def mid(self, after):
        l_out, l_cw = _gather_wait_out(self.gather, self.pack, after)
        self.conv_w8 = jnp.concatenate([l_cw[i] for i in range(N_CHIPS)], axis=1)
        self.passing_out = _forward_start(FWD_OUT, l_out, name="forward_out_start")
        return self.passing_out["token"]

    def w_out(self, after):
        l_ffn = _gather_wait_ffn(self.gather, self.pack, after)
        self.passing_ffn = _forward_start(FWD_FFN, l_ffn, name="forward_ffn_start")
        l_out = _forward_wait(FWD_OUT, self.passing_out, self.passing_ffn["token"], name="forward_out_wait")
        return dict(w_out=(l_out, N_CHIPS * SLAB[4], 0), conv_w8=self.conv_w8)

    def rest(self, after):
        l_ffn = _forward_wait(FWD_FFN, self.passing_ffn, after, name="forward_ffn_wait")
        rows = N_CHIPS * SLAB[FFN_W[0]]
        return dict(w_gate_t=(l_ffn, rows, 0), w_up_t=(l_ffn, rows, 1), w_down=(l_ffn, rows, 2))

    def ffn_grads(self, gs):
        self.swap = _halves_start(FFN_W, gs, name="halves_ffn_start")
        return self.swap["token"]

    def ffn_grads_send(self, after):
        gs, theirs = _halves_wait(FFN_W, self.swap, after, name="halves_ffn_wait")
        parts = _chip_partial(FFN_W, gs, theirs, name="chip_partial_ffn", out_dtype=BF16)
        self.sent = _send_start(FFN_W, parts, name="send_ffn_start")
        return self.sent["token"]


def kernel(x, norm_mix_w, w_in, b_attn, lb_logits, hg_norm_w, sinks, w_out, norm_ffn_w, w_gate, w_up, conv_w, conv_b, w_down, final_norm_w, loss_target, m_norm_mix_w, m_w_in, m_b_attn, m_lb_logits, m_hg_norm_w, m_sinks, m_w_out, m_norm_ffn_w, m_w_gate, m_w_up, m_conv_w, m_conv_b, m_w_down, m_final_norm_w, v_norm_mix_w, v_w_in, v_b_attn, v_lb_logits, v_hg_norm_w, v_sinks, v_w_out, v_norm_ffn_w, v_w_gate, v_w_up, v_conv_w, v_conv_b, v_w_down, v_final_norm_w):
    D = D_MODEL
    q = 2 * lax.axis_index("x") + lax.axis_index("y")
    ccols = D_FF // N_CHIPS

    pack = jnp.concatenate([w_in[0].T, w_gate[0].T, w_up[0].T, w_down[0], w_out[0]], axis=0).astype(BF16)
    cw8 = jnp.concatenate([conv_w[0], jnp.zeros((SUBLANES - 3, ccols), F32)], axis=0)
    ex = _MeshExchange(pack, cw8)
    p = dict(norm_mix_w=norm_mix_w, b_attn=b_attn, lb=_lb_fwd(lb_logits), hg_norm_w=hg_norm_w, sinks=sinks,
             norm_ffn_w=norm_ffn_w, conv_b=conv_b, final_norm_w=final_norm_w.reshape(1, D))
    loss_cols, dx, g = _local_step(x[0], loss_target[0], p, ex)
    conv_w8 = ex.conv_w8

    small = _pack_small(dict(loss=loss_cols, norm_mix_w=g["norm_mix_w"], b_attn=g["b_attn"], lb_logits=g["lb"],
                             hg_norm_w=g["hg_norm_w"], sinks=g["sinks8"], norm_ffn_w=g["norm_ffn_w"],
                             conv_w=g["conv_w8"][:3], conv_b=g["conv_b"], final_norm_w=g["final_norm_w"]))
    parts_ffn, got_ffn = _send_wait(FFN_W, ex.sent, [dx], name="send_ffn_wait")
    late = (0, 4)
    gs = [g["g_in_t"], g["g_out"]]
    *theirs, sall = _exchange_halves(late, gs, small, name="exchange_halves_late")
    parts_late = _chip_partial(late, gs, theirs, name="chip_partial_late", out_dtype=BF16)
    sent_late = _send_start(late, parts_late, name="send_late_start")
    big = {}

    def finish(ws, parts, got, specs, tag, after):
        shards = _exchange_reduced(ws, _chip_reduce(ws, parts, got, name="chip_reduce_" + tag, after=after),
                                   name="exchange_reduced_" + tag)
        deltas = []
        for gw, (n, w, m, v, tr) in zip(shards, specs):
            view = (lambda a: a[0].T) if tr else (lambda a: a[0])
            back = (lambda a: a.T[None]) if tr else (lambda a: a[None])
            res = _adamw(view(w), gw, view(m), view(v), name="adamw_" + n)
            big[n] = tuple(back(r) for r in res)
            deltas.append(res[1])
        return deltas

    done_ffn = finish(FFN_W, parts_ffn, got_ffn, (("w_gate", w_gate, m_w_gate, v_w_gate, True),
                                                  ("w_up", w_up, m_w_up, v_w_up, True),
                                                  ("w_down", w_down, m_w_down, v_w_down, False)), "ffn", sent_late["token"])

    def place(a):
        return lax.dynamic_update_slice(jnp.zeros((3, D_FF), F32), a[0], (0, q * ccols))

    def small_pack(ws, cw):
        nm, ba, lbl, hg, sk, nf, cb, fn = ws
        return _pack_small(dict(norm_mix_w=nm, b_attn=ba, lb_logits=lbl, hg_norm_w=hg,
                                sinks=jnp.broadcast_to(sk.reshape(ATT_HEADS, 1), (ATT_HEADS, LANES)), norm_ffn_w=nf,
                                conv_w=cw, conv_b=cb, final_norm_w=fn))

    wp = small_pack((norm_mix_w, b_attn, lb_logits, hg_norm_w, sinks, norm_ffn_w, conv_b, final_norm_w), conv_w8[:3])
    mp = small_pack((m_norm_mix_w, m_b_attn, m_lb_logits, m_hg_norm_w, m_sinks, m_norm_ffn_w, m_conv_b, m_final_norm_w),
                    place(m_conv_w))
    vp = small_pack((v_norm_mix_w, v_b_attn, v_lb_logits, v_hg_norm_w, v_sinks, v_norm_ffn_w, v_conv_b, v_final_norm_w),
                    place(v_conv_w))
    outs = _small_update(sall, wp, mp, vp, after=sent_late["token"])
    loss = outs[4][0, 0]
    parts_late, got_late = _send_wait(late, sent_late, [*done_ffn, outs[4]], name="send_late_wait")
    finish(late, parts_late, got_late, (("w_in", w_in, m_w_in, v_w_in, True), ("w_out", w_out, m_w_out, v_w_out, False)),
           "late", None)

    def small_out(pk, n, ref):
        if n == "sinks":
            return pk[SMALL_OFF[n]:SMALL_OFF[n] + ATT_HEADS, 0].reshape(ref.shape)
        if n == "conv_w":
            full = _unpack_small(pk, n, (3, D_FF))
            return lax.dynamic_slice(full, (0, q * ccols), (3, ccols))[None]
        return _unpack_small(pk, n, ref.shape)

    refs = dict(norm_mix_w=norm_mix_w, b_attn=b_attn, lb_logits=lb_logits, hg_norm_w=hg_norm_w, sinks=sinks,
                norm_ffn_w=norm_ffn_w, conv_w=conv_w, conv_b=conv_b, final_norm_w=final_norm_w)
    order = ("norm_mix_w", "w_in", "b_attn", "lb_logits", "hg_norm_w", "sinks", "w_out", "norm_ffn_w", "w_gate", "w_up",
             "conv_w", "conv_b", "w_down", "final_norm_w")
    res = [loss, dx[None]]
    for k in range(4):
        for n in order:
            res.append(big[n][k] if n in big else small_out(outs[k], n, refs[n]))
    return tuple(res)
```

```python
import functools
import math

import jax
import jax.numpy as jnp
from jax import lax
from jax.experimental import pallas as pl
from jax.experimental.pallas import tpu as pltpu

F32 = jnp.float32
BF16 = jnp.bfloat16

D_MODEL = 1024
HG_HEADS = 4
HG_DK = 128
HG_W = HG_HEADS * HG_DK
HG_CHUNK = 64
HG_SUB = 8
HG_FWD_CHUNKS_PER_STEP = 8
HG_CHUNKS_PER_STEP = 4
ATT_HEADS = 8
ATT_KV = 2
ATT_GROUP = ATT_HEADS // ATT_KV
ATT_HD = 64
ATT_BLOCK = 128
ATT_Q_W = ATT_HEADS * ATT_HD
ATT_KV_W = ATT_KV * ATT_HD
ATT_COLS = ATT_Q_W + 2 * ATT_KV_W
IN_COLS = 4 * HG_W + ATT_COLS
D_FF = 2816
EPS = 1e-6
ADAM_LR, ADAM_B1, ADAM_B2, ADAM_EPS, ADAM_WD, ADAM_STEP = 0.001, 0.9, 0.999, 1e-08, 0.01, 10
NEG = -1e30

V7X_VMEM_BYTES = 64 * 1024 * 1024
VMEM_LIMIT = 48 * 1024 * 1024
SUBLANES = 8

N_CHIPS = 4


def _cp(sem=None, **kw):
    return pltpu.CompilerParams(dimension_semantics=sem, vmem_limit_bytes=VMEM_LIMIT, **kw)


def _sds(shape, dtype):
    return jax.ShapeDtypeStruct(shape, dtype)


TOKEN = jax.ShapeDtypeStruct((8, 128), jnp.float32)


def _wspec(w):
    arr, rows, blk = w
    return pl.BlockSpec((rows, arr.shape[1]), lambda i: (blk, 0))


def _mm_nt(a, w, *, splits, out_dtype, name, after=None, tm=512):
    M, K = a.shape
    N = w[1]
    tm = min(tm, M)
    assert sum(splits) == N and M % tm == 0
    offs = [sum(splits[:i]) for i in range(len(splits))]
    n_in = 2 if after is None else 3

    def body(*refs):
        a_ref, w_ref = refs[0], refs[1]
        acc = lax.dot_general(a_ref[...], w_ref[...], (((1,), (1,)), ((), ())), preferred_element_type=F32)
        for o_ref, c0, n in zip(refs[n_in:], offs, splits):
            o_ref[...] = acc[:, c0:c0 + n].astype(out_dtype)

    in_specs = [pl.BlockSpec((tm, K), lambda i: (i, 0)), _wspec(w)]
    args = [a, w[0]]
    if after is not None:
        in_specs.append(pl.BlockSpec(memory_space=pl.ANY))
        args.append(after)
    outs = pl.pallas_call(
        body, name=name, grid=(M // tm,), in_specs=in_specs,
        out_specs=[pl.BlockSpec((tm, n), lambda i: (i, 0)) for n in splits],
        out_shape=[_sds((M, n), out_dtype) for n in splits],
        compiler_params=_cp(("parallel",)),
    )(*args)
    return outs


def _mm_nn(pieces, ws, *, name, out_dtype=F32, residual=None, epilogue=None, prologue=None, after=None,
           w_transposed=False, tm=512):
    pro_fn, pro_rows, pro_bc, pro_out = prologue or (None, [], [], None)
    if prologue is not None:
        assert pieces is None and len(ws) == 1
        pieces = [[pro_out]]
    M = pieces[0][0].shape[0]
    K = ws[0][1] if w_transposed else ws[0][0].shape[1]
    tm = min(tm, M)
    flat = [] if prologue is not None else [p for grp in pieces for p in grp]
    n_p = len(flat)
    n_w = len(ws)
    n_pr, n_pb = len(pro_rows), len(pro_bc)
    fn, row_ins, bc_ins, row_outs, acc_outs = epilogue or (None, [], [], [_sds((M, K), out_dtype)], [])
    if residual is not None:
        assert epilogue is None
        row_ins = [residual]
    n_r, n_b, n_o = len(row_ins), len(bc_ins), len(row_outs)
    lead = [] if after is None else [after]

    def body(*refs):
        refs = refs[len(lead):]
        p_refs = refs[:n_p]
        w_refs = refs[n_p:n_p + n_w]
        extra = [r[...] for r in refs[n_p + n_w:n_p + n_w + n_r + n_b]]
        base = n_p + n_w + n_r + n_b
        pro = [r[...] for r in refs[base:base + n_pr + n_pb]]
        base += n_pr + n_pb
        o_refs = refs[base:base + n_o]
        a_refs = refs[base + n_o:base + n_o + len(acc_outs)]
        if pro_fn is not None:
            lhs = pro_fn(*pro).astype(pro_out.dtype)
            refs[-1][...] = lhs
            tiles = [lhs]
        else:
            tiles = [r[...] for r in p_refs]
        acc = None
        k = 0
        for gi, grp in enumerate(pieces):
            c0 = 0
            for p in grp:
                n = p.shape[1]
                if w_transposed:
                    t = lax.dot_general(tiles[k], w_refs[gi][...], (((1,), (1,)), ((), ())), preferred_element_type=F32)
                else:
                    t = jnp.dot(tiles[k], w_refs[gi][c0:c0 + n, :], preferred_element_type=F32)
                acc = t if acc is None else acc + t
                c0 += n
                k += 1
        if fn is None:
            res = (acc + extra[0] if residual is not None else acc,)
        else:
            res = fn(acc, *extra)
        for o_ref, val in zip(o_refs, res[:n_o]):
            o_ref[...] = val.astype(o_ref.dtype)
        if acc_outs:
            @pl.when(pl.program_id(0) == 0)
            def _():
                for a_ref in a_refs:
                    a_ref[...] = jnp.zeros_like(a_ref)
            for a_ref, val in zip(a_refs, res[n_o:]):
                a_ref[...] += val

    in_specs = [pl.BlockSpec((tm, p.shape[1]), lambda i: (i, 0)) for p in flat]
    in_specs += [_wspec(w) for w in ws]
    in_specs += [pl.BlockSpec((tm, r.shape[1]), lambda i: (i, 0)) for r in row_ins]
    in_specs += [pl.BlockSpec(b.shape, lambda i: (0, 0)) for b in bc_ins]
    in_specs += [pl.BlockSpec((tm, r.shape[1]), lambda i: (i, 0)) for r in pro_rows]
    in_specs += [pl.BlockSpec(b.shape, lambda i: (0, 0)) for b in pro_bc]
    out_specs = [pl.BlockSpec((tm, s.shape[1]), lambda i: (i, 0)) for s in row_outs]
    out_specs += [pl.BlockSpec(s.shape, lambda i: (0, 0)) for s in acc_outs]
    pro_outs = [] if prologue is None else [pro_out]
    out_specs += [pl.BlockSpec((tm, s.shape[1]), lambda i: (i, 0)) for s in pro_outs]
    outs = pl.pallas_call(
        body, name=name, grid=(M // tm,), in_specs=[pl.BlockSpec(memory_space=pl.ANY)] * len(lead) + in_specs,
        out_specs=out_specs, out_shape=list(row_outs) + list(acc_outs) + pro_outs,
        compiler_params=_cp(("arbitrary",) if acc_outs else ("parallel",)),
    )(*lead, *flat, *[w[0] for w in ws], *row_ins, *bc_ins, *pro_rows, *pro_bc)
    return outs if (epilogue is not None or prologue is not None) else outs[0]


def _mm_tn(pieces, x, *, name, out_dtype=BF16, tt=1024):
    M, K = x.shape
    tt = min(tt, M)
    ns = [p.shape[1] for p in pieces]
    offs = [sum(ns[:i]) for i in range(len(ns))]
    N = sum(ns)
    n_p = len(pieces)
    last = M // tt - 1

    def body(*refs):
        p_refs = refs[:n_p]
        x_ref = refs[n_p]
        o_ref, acc_ref = refs[n_p + 1], refs[n_p + 2]

        @pl.when(pl.program_id(0) == 0)
        def _():
            acc_ref[...] = jnp.zeros_like(acc_ref)

        xv = x_ref[...]
        for p_ref, c0, n in zip(p_refs, offs, ns):
            acc_ref[c0:c0 + n, :] += lax.dot_general(p_ref[...], xv, (((0,), (0,)), ((), ())),
                                                      preferred_element_type=F32)

        @pl.when(pl.program_id(0) == last)
        def _():
            o_ref[...] = acc_ref[...].astype(o_ref.dtype)

    in_specs = [pl.BlockSpec((tt, n), lambda i: (i, 0)) for n in ns]
    in_specs.append(pl.BlockSpec((tt, K), lambda i: (i, 0)))
    return pl.pallas_call(
        body, name=name, grid=(M // tt,), in_specs=in_specs,
        out_specs=pl.BlockSpec((N, K), lambda i: (0, 0)),
        out_shape=_sds((N, K), out_dtype),
        scratch_shapes=[pltpu.VMEM((N, K), F32)],
        compiler_params=_cp(("arbitrary",)),
    )(*pieces, x)


def _rms_fwd(xf, w):
    inv = lax.rsqrt(jnp.mean(xf * xf, axis=-1, keepdims=True) + EPS)
    return xf * inv * w


def _rms_bwd(xf, w, dy):
    inv = lax.rsqrt(jnp.mean(xf * xf, axis=-1, keepdims=True) + EPS)
    xhat = xf * inv
    dxhat = dy * w
    dx = inv * (dxhat - xhat * jnp.mean(dxhat * xhat, axis=-1, keepdims=True))
    dw = jnp.sum(dy * xhat, axis=0, keepdims=True)
    return dx, dw


def _sigmoid(x):
    return 1.0 / (1.0 + jnp.exp(-x))


def _rowwise(fn, row_ins, bc_ins, row_outs, acc_outs, *, name, tm=256, after=None):
    M = row_outs[0].shape[0] if row_outs else row_ins[0][0].shape[0]
    assert M % tm == 0 and tm % SUBLANES == 0, (name, M, tm)
    n_r, n_b, n_o, n_a = len(row_ins), len(bc_ins), len(row_outs), len(acc_outs)
    n_after = 0 if after is None else 1

    def body(*refs):
        refs = refs[n_after:]
        ins = [r[...] for r in refs[:n_r + n_b]]
        o_refs = refs[n_r + n_b:n_r + n_b + n_o]
        a_refs = refs[n_r + n_b + n_o:]
        res = fn(*ins)
        for o_ref, val in zip(o_refs, res[:n_o]):
            o_ref[...] = val.astype(o_ref.dtype)
        if n_a:
            @pl.when(pl.program_id(0) == 0)
            def _():
                for a_ref in a_refs:
                    a_ref[...] = jnp.zeros_like(a_ref)
            for a_ref, val in zip(a_refs, res[n_o:]):
                a_ref[...] += val

    in_specs = [pl.BlockSpec((tm, cw), functools.partial(lambda i, cb, r0: (i + r0, cb), cb=cb, r0=r0))
                for (_, cw, cb, r0) in row_ins]
    in_specs += [pl.BlockSpec(b.shape, lambda i: (0, 0)) for b in bc_ins]
    out_specs = [pl.BlockSpec((tm, s.shape[1]), lambda i: (i, 0)) for s in row_outs]
    out_specs += [pl.BlockSpec(s.shape, lambda i: (0, 0)) for s in acc_outs]
    if n_after:
        in_specs = [pl.BlockSpec(memory_space=pl.ANY)] + in_specs
    return pl.pallas_call(
        body, name=name, grid=(M // tm,), in_specs=in_specs, out_specs=out_specs,
        out_shape=list(row_outs) + list(acc_outs),
        compiler_params=_cp(("arbitrary",) if n_a else ("parallel",)),
    )(*([after] if n_after else []), *[r[0] for r in row_ins], *bc_ins)


def _full(a, first_row_block=0):
    return (a, a.shape[1], 0, first_row_block)


def _conv_rows(ext, w_ref_val, lo):
    s1 = pltpu.roll(ext, 1, 0)
    s2 = pltpu.roll(ext, 2, 0)
    y = w_ref_val[0:1, :] * s2 + w_ref_val[1:2, :] * s1 + w_ref_val[2:3, :] * ext
    return y[SUBLANES:, :]


def _ffn_in(v, w_gate, w_up, conv_w8, conv_b, *, name, tm=256):
    T, K = v.shape
    N = w_gate[1]
    tm = min(tm, T)

    def body(v_ref, wg_ref, wu_ref, cw_ref, cb_ref, gp_ref, up_ref, gate_ref, act_ref, carry_sc):
        @pl.when(pl.program_id(0) == 0)
        def _():
            carry_sc[...] = jnp.zeros_like(carry_sc)

        vv = v_ref[...]
        dn = (((1,), (1,)), ((), ()))
        gp = lax.dot_general(vv, wg_ref[...], dn, preferred_element_type=F32)
        up = lax.dot_general(vv, wu_ref[...], dn, preferred_element_type=F32)
        gp_ref[...] = gp.astype(gp_ref.dtype)
        up_ref[...] = up.astype(up_ref.dtype)
        gate = _conv_rows(jnp.concatenate([carry_sc[...], gp], axis=0), cw_ref[...], 0) + cb_ref[...]
        gate_ref[...] = gate
        act_ref[...] = (gate * _sigmoid(gate) * up).astype(act_ref.dtype)
        carry_sc[...] = gp[tm - SUBLANES:, :]

    tile = pl.BlockSpec((tm, N), lambda i: (i, 0))
    return pl.pallas_call(
        body, name=name, grid=(T // tm,),
        in_specs=[pl.BlockSpec((tm, K), lambda i: (i, 0)), _wspec(w_gate), _wspec(w_up),
                  pl.BlockSpec((SUBLANES, N), lambda i: (0, 0)), pl.BlockSpec((1, N), lambda i: (0, 0))],
        out_specs=[tile] * 4,
        out_shape=[_sds((T, N), BF16), _sds((T, N), BF16), _sds((T, N), F32), _sds((T, N), BF16)],
        scratch_shapes=[pltpu.VMEM((SUBLANES, N), F32)],
        compiler_params=_cp(("arbitrary",)),
    )(v, w_gate[0], w_up[0], conv_w8, conv_b)


def _ffn_back(dh2, w_down, gp, up, gate, conv_w8, *, name, tr=512, tc=1408):
    T, C = gp.shape
    K = dh2.shape[1]
    warr, _, wblk = w_down
    tr = min(tr, T)
    nr = T // tr
    ncb = C // tc

    def body(dh_ref, wd_ref, gp_ref, up_ref, gate_ref, w_ref, dgp_ref, dup_ref, dw_ref, db_ref, carry_sc):
        @pl.when(pl.program_id(1) == 0)
        def _():
            carry_sc[...] = jnp.zeros_like(carry_sc)
            dw_ref[...] = jnp.zeros_like(dw_ref)
            db_ref[...] = jnp.zeros_like(db_ref)

        w = w_ref[...]
        dact = lax.dot_general(dh_ref[...], wd_ref[...], (((1,), (1,)), ((), ())), preferred_element_type=F32)
        gpc = gp_ref[...].astype(F32)
        gate = gate_ref[...]
        sg = _sigmoid(gate)
        silu = gate * sg
        dup_ref[...] = (dact * silu).astype(dup_ref.dtype)
        dgate = dact * up_ref[...].astype(F32) * (sg + silu * (1.0 - sg))
        ext = jnp.concatenate([dgate, carry_sc[...]], axis=0)
        n = tr + SUBLANES
        g1 = pltpu.roll(ext, n - 1, 0)[:tr]
        g2 = pltpu.roll(ext, n - 2, 0)[:tr]
        dgp_ref[...] = (w[2:3, :] * dgate + w[1:2, :] * g1 + w[0:1, :] * g2).astype(dgp_ref.dtype)
        dw0 = jnp.sum(gpc * g2, axis=0, keepdims=True)
        dw1 = jnp.sum(gpc * g1, axis=0, keepdims=True)
        dw2 = jnp.sum(gpc * dgate, axis=0, keepdims=True)
        z = jnp.zeros((SUBLANES - 3, gpc.shape[1]), F32)
        dw_ref[...] += jnp.concatenate([dw0, dw1, dw2, z], axis=0)
        db_ref[...] += jnp.sum(dgate, axis=0, keepdims=True)
        carry_sc[...] = dgate[:SUBLANES]

    rev = lambda i: nr - 1 - i
    cur = pl.BlockSpec((tr, tc), lambda j, i: (rev(i), j))
    return pl.pallas_call(
        body, name=name, grid=(ncb, nr),
        in_specs=[pl.BlockSpec((tr, K), lambda j, i: (rev(i), 0)),
                  pl.BlockSpec((tc, K), lambda j, i: (wblk * ncb + j, 0)),
                  cur, cur, cur,
                  pl.BlockSpec((SUBLANES, tc), lambda j, i: (0, j))],
        out_specs=[cur, cur,
                   pl.BlockSpec((SUBLANES, tc), lambda j, i: (0, j)),
                   pl.BlockSpec((1, tc), lambda j, i: (0, j))],
        out_shape=[_sds((T, C), BF16), _sds((T, C), BF16), _sds((SUBLANES, C), F32), _sds((1, C), F32)],
        scratch_shapes=[pltpu.VMEM((SUBLANES, tc), F32)],
        compiler_params=_cp(("parallel", "arbitrary")),
    )(dh2, warr, gp, up, gate, conv_w8)


def _cumsum_rows(x):
    n = x.shape[0]
    row = lax.broadcasted_iota(jnp.int32, x.shape, 0)
    s = 1
    while s < n:
        x = x + jnp.where(row >= s, pltpu.roll(x, s, 0), 0.0)
        s *= 2
    return x


def _rcumsum_rows(x):
    n = x.shape[0]
    row = lax.broadcasted_iota(jnp.int32, x.shape, 0)
    s = 1
    while s < n:
        x = x + jnp.where(row < n - s, pltpu.roll(x, n - s, 0), 0.0)
        s *= 2
    return x


def _dot_nt(a, b):
    return lax.dot_general(a.astype(BF16), b.astype(BF16), (((1,), (1,)), ((), ())), preferred_element_type=F32)


def _dot_tn(a, b):
    return lax.dot_general(a.astype(BF16), b.astype(BF16), (((0,), (0,)), ((), ())), preferred_element_type=F32)


def _dot_nn(a, b):
    return jnp.dot(a.astype(BF16), b.astype(BF16), preferred_element_type=F32)


def _dot3(a, b, contract):
    def split(x):
        hi = x.astype(BF16)
        return hi, (x - hi.astype(F32)).astype(BF16)

    a_hi, a_lo = split(a)
    b_hi, b_lo = split(b)
    dot = lambda x, y: lax.dot_general(x, y, (contract, ((), ())), preferred_element_type=F32)
    return dot(a_hi, b_hi) + (dot(a_hi, b_lo) + dot(a_lo, b_hi))


NT, TN, NN = ((1,), (1,)), ((0,), (0,)), ((1,), (0,))


def _hg_gates(hq, hf, lbv):
    sig = _sigmoid(hf)
    f = lbv + (1.0 - lbv) * sig
    return sig, f, jnp.log(f), 1.0 - f, hq * (HG_DK ** -0.5)


def _hg_sel_rows(ref, sp):
    return jnp.concatenate(
        [jnp.broadcast_to(ref[pl.ds(HG_SUB * i + sp, 1), :], (HG_SUB, HG_DK)) for i in range(HG_CHUNK // HG_SUB)], axis=0)


def _hg_masks():
    C = HG_CHUNK
    row = lax.broadcasted_iota(jnp.int32, (C, C), 0)
    col = lax.broadcasted_iota(jnp.int32, (C, C), 1)
    d = col - (row // HG_SUB) * HG_SUB
    tmod = row % HG_SUB
    diag_valid = jnp.logical_and(d >= 0, d <= tmod)
    return row, col, d, diag_valid


def _hg_strip_keys(k, b, r, n):
    ek = jnp.exp(r - b[:n])
    return ek, jnp.concatenate([k[:n] * ek, jnp.zeros((HG_CHUNK - n, k.shape[1]), F32)], axis=0)


def _hg_scores(q, k, b, b_sc, k_sc):
    C, S = HG_CHUNK, HG_SUB
    row, col, d, diag_valid = _hg_masks()
    blocks = [jnp.zeros((S, C), F32)]
    for i in range(1, C // S):
        r = b_sc[pl.ds(S * i - 1, 1), :]
        qi = q[S * i:S * (i + 1)] * jnp.exp(b[S * i:S * (i + 1)] - r)
        blocks.append(_dot_nt(qi, _hg_strip_keys(k, b, r, S * i)[1]))
    a_off = jnp.concatenate(blocks, axis=0)
    a_d = jnp.zeros((C, C), F32)
    for sp in range(S):
        bs = _hg_sel_rows(b_sc, sp)
        ks = _hg_sel_rows(k_sc, sp)
        e = jnp.exp(jnp.minimum(b - bs, 0.0))
        colv = jnp.sum(q * ks * e, axis=-1, keepdims=True)
        a_d = jnp.where(d == sp, colv, a_d)
    return a_off + jnp.where(diag_valid, a_d, 0.0)


def _hg_prep(hq_v, hf_v, lbv, b_sc, k_sc):
    sig, f, g, k, q = _hg_gates(hq_v, hf_v, lbv)
    b = _cumsum_rows(g)
    b_sc[...] = b
    k_sc[...] = k
    return sig, f, k, q, b, b_sc[pl.ds(HG_CHUNK - 1, 1), :]


def _hgrn_fwd(hq, hf, hi, lb, *, name):
    T = hq.shape[0]
    C, H, K = HG_CHUNK, HG_HEADS, HG_DK
    NC = T // C

    def body(hq_ref, hf_ref, hi_ref, lb_ref, o_ref, st_ref, s_sc, b_sc, k_sc):
        @pl.when(pl.program_id(0) == 0)
        def _():
            s_sc[...] = jnp.zeros_like(s_sc)

        st_all = s_sc[...]
        for j in range(P):
            rows = slice(C * j, C * (j + 1))
            st_ref[j] = st_all
            outs, news = [], []
            for h in range(H):
                sl = slice(K * h, K * (h + 1))
                _, _, k, q, b, bc = _hg_prep(hq_ref[rows, sl], hf_ref[rows, sl], lb_ref[:, sl], b_sc.at[j, h], k_sc.at[j, h])
                v = hi_ref[rows, sl]
                st0 = st_all[:, sl]
                a = _hg_scores(q, k, b, b_sc.at[j, h], k_sc.at[j, h])
                outs.append(_dot_nn(a, v) + _dot_nt(q * jnp.exp(b), st0))
                news.append(st0 * jnp.exp(bc) + _dot_tn(v, k * jnp.exp(bc - b)))
            o_ref[rows, :] = jnp.concatenate(outs, axis=1)
            st_all = jnp.concatenate(news, axis=1)
        s_sc[...] = st_all

    P = HG_FWD_CHUNKS_PER_STEP
    blk = pl.BlockSpec((P * C, H * K), lambda c: (c, 0))
    return pl.pallas_call(
        body, name=name, grid=(NC // P,),
        in_specs=[blk, blk, blk, pl.BlockSpec((1, H * K), lambda c: (0, 0))],
        out_specs=[blk, pl.BlockSpec((P, K, H * K), lambda c: (c, 0, 0))],
        out_shape=[_sds((T, H * K), F32), _sds((NC, K, H * K), F32)],
        scratch_shapes=[pltpu.VMEM((K, H * K), F32), pltpu.VMEM((P, H, C, K), F32), pltpu.VMEM((P, H, C, K), F32)],
        compiler_params=_cp(("arbitrary",)),
    )(hq, hf, hi, lb)


def _hgrn_bwd(hq, hf, hi, lb, states, do, *, name):
    T = hq.shape[0]
    C, H, K, S = HG_CHUNK, HG_HEADS, HG_DK, HG_SUB
    NC = T // C

    def intra_slow(q, k, b, da, b_sc, k_sc):
        row, col, d, diag_valid = _hg_masks()
        a_blocks = [jnp.zeros((S, C), F32)]
        dq_blocks = [jnp.zeros((S, K), F32)]
        dk = jnp.zeros((C, K), F32)
        for i in range(1, C // S):
            r = b_sc[pl.ds(S * i - 1, 1), :]
            eq = jnp.exp(b[S * i:S * (i + 1)] - r)
            ek = jnp.exp(jnp.minimum(r - b, 0.0))
            qi = q[S * i:S * (i + 1)] * eq
            kk = k * ek
            a_blocks.append(_dot_nt(qi, kk))
            dai = jnp.where(col[S * i:S * (i + 1)] < S * i, da[S * i:S * (i + 1)], 0.0)
            dq_blocks.append(_dot_nn(dai, kk) * eq)
            dk = dk + _dot_tn(dai, qi) * ek
        dq = jnp.concatenate(dq_blocks, axis=0)
        a_off = jnp.where(col < (row // S) * S, jnp.concatenate(a_blocks, axis=0), 0.0)
        same_blk = (row // S == col // S).astype(BF16)
        tmod = (lax.broadcasted_iota(jnp.int32, (C, K), 0)) % S
        a_d = jnp.zeros((C, C), F32)
        dk_d = jnp.zeros((C, K), F32)
        for sp in range(S):
            bs = _hg_sel_rows(b_sc, sp)
            ks = _hg_sel_rows(k_sc, sp)
            e = jnp.exp(jnp.minimum(b - bs, 0.0))
            eks = e * ks
            a_d = jnp.where(d == sp, jnp.sum(q * eks, axis=-1, keepdims=True), a_d)
            dacol = jnp.sum(jnp.where(d == sp, da, 0.0), axis=-1, keepdims=True)
            dq = dq + dacol * eks
            wq = dacol * e * q
            wq_hi = wq.astype(BF16)
            wq_lo = (wq - wq_hi.astype(F32)).astype(BF16)
            blk_sum = (jnp.dot(same_blk, wq_hi, preferred_element_type=F32)
                       + jnp.dot(same_blk, wq_lo, preferred_element_type=F32))
            dk_d = jnp.where(tmod == sp, blk_sum, dk_d)
        return a_off + jnp.where(diag_valid, a_d, 0.0), dq, dk + dk_d

    def one_head(pre, v, lbv, st0, dst1, dout, b_sc, k_sc):
        sig, f, k, q, b, bc = pre
        ebc = jnp.exp(bc)
        eb = jnp.exp(b)
        ekb = jnp.exp(bc - b)
        qt = q * eb
        kb = k * ekb
        row = lax.broadcasted_iota(jnp.int32, (C, C), 0)
        col = lax.broadcasted_iota(jnp.int32, (C, C), 1)
        da = jnp.where(col <= row, _dot_nt(dout, v), 0.0)
        dkb = _dot_nn(v, dst1)
        new_ds = _dot_tn(dout, qt) + dst1 * ebc
        a, dq_i, dk_i = intra_slow(q, k, b, da, b_sc, k_sc)
        dq = _dot_nn(dout, st0) * eb + dq_i
        dk = dkb * ekb + dk_i
        dv = _dot_tn(a, dout) + _dot_nt(kb, dst1)
        extra = jnp.sum(dkb * kb, axis=0, keepdims=True) + ebc * jnp.sum(st0 * dst1, axis=0, keepdims=True)
        rowk = lax.broadcasted_iota(jnp.int32, (C, K), 0)
        db = q * dq - k * dk + jnp.where(rowk == C - 1, extra, 0.0)
        dg = _rcumsum_rows(db)
        df = dg / f - dk
        return (dq * (K ** -0.5), df * (1.0 - lbv) * sig * (1.0 - sig), dv,
                jnp.sum(df * (1.0 - sig), axis=0, keepdims=True), new_ds)

    def body(hq_ref, hf_ref, hi_ref, lb_ref, st_ref, do_ref, dq_ref, dhf_ref, dv_ref, dlb_ref, ds_sc, b_sc, k_sc):
        @pl.when(pl.program_id(0) == 0)
        def _():
            ds_sc[...] = jnp.zeros_like(ds_sc)
            dlb_ref[...] = jnp.zeros_like(dlb_ref)

        ds_all = ds_sc[...]
        dlb = jnp.zeros((1, H * K), F32)
        for j in reversed(range(P)):
            rows = slice(C * j, C * (j + 1))
            st_all = st_ref[j]
            res = []
            for h in range(H):
                sl = slice(K * h, K * (h + 1))
                pre = _hg_prep(hq_ref[rows, sl], hf_ref[rows, sl], lb_ref[:, sl], b_sc.at[j, h], k_sc.at[j, h])
                res.append(one_head(pre, hi_ref[rows, sl], lb_ref[:, sl], st_all[:, sl], ds_all[:, sl], do_ref[rows, sl],
                                    b_sc.at[j, h], k_sc.at[j, h]))
            cat = lambda i: jnp.concatenate([r[i] for r in res], axis=1)
            dq_ref[rows, :] = cat(0).astype(dq_ref.dtype)
            dhf_ref[rows, :] = cat(1).astype(dhf_ref.dtype)
            dv_ref[rows, :] = cat(2).astype(dv_ref.dtype)
            dlb = dlb + cat(3)
            ds_all = cat(4)
        dlb_ref[...] += dlb
        ds_sc[...] = ds_all

    P = HG_CHUNKS_PER_STEP
    NS = NC // P
    blk = pl.BlockSpec((P * C, H * K), lambda c: (NS - 1 - c, 0))
    par = pl.BlockSpec((1, H * K), lambda c: (0, 0))
    return pl.pallas_call(
        body, name=name, grid=(NS,),
        in_specs=[blk, blk, blk, par, pl.BlockSpec((P, K, H * K), lambda c: (NS - 1 - c, 0, 0)), blk],
        out_specs=[blk, blk, blk, par],
        out_shape=[_sds((T, H * K), BF16)] * 3 + [_sds((1, H * K), F32)],
        scratch_shapes=[pltpu.VMEM((K, H * K), F32), pltpu.VMEM((P, H, C, K), F32), pltpu.VMEM((P, H, C, K), F32)],
        compiler_params=_cp(("arbitrary",)),
    )(hq, hf, hi, lb, states, do)


ATT_STACK = ATT_GROUP
ATT_FWD_QROWS = ATT_BLOCK // 2


def _att_valid(n, a=0, qrows=ATT_BLOCK):
    R, B = ATT_STACK * qrows, ATT_BLOCK
    j = lax.broadcasted_iota(jnp.int32, (B + qrows, R), 0)
    t = lax.broadcasted_iota(jnp.int32, (B + qrows, R), 1) % qrows
    dist = t + B - j
    first_key = jnp.where(n > 0, 0, B)
    return jnp.logical_and(jnp.logical_and(dist >= 0, dist < B), j + qrows * a >= first_key)


def _att_rows(x, a, qrows):
    return jnp.concatenate([x[ATT_BLOCK * g + qrows * a:ATT_BLOCK * g + qrows * (a + 1)] for g in range(ATT_STACK)], axis=0)


def _att_load(cur_ref, prev_ref, ba_ref, h0):
    hd = ATT_HD
    kv = h0 // ATT_GROUP
    def cols(ref, c0):
        return ref[:, c0:c0 + hd] + ba_ref[:, c0:c0 + hd]
    qs = jnp.concatenate([cols(cur_ref, hd * (h0 + g)) for g in range(ATT_STACK)], axis=0)
    kc = jnp.concatenate([cols(prev_ref, ATT_Q_W + hd * kv), cols(cur_ref, ATT_Q_W + hd * kv)], axis=0)
    vc = jnp.concatenate([cols(prev_ref, ATT_Q_W + ATT_KV_W + hd * kv), cols(cur_ref, ATT_Q_W + ATT_KV_W + hd * kv)], axis=0)
    return qs, kc, vc


def _att_probs(qs, kc, valid, sink_ref, h0):
    scale = 1.0 / math.sqrt(ATT_HD)
    s = jnp.where(valid, _dot_nt(kc, qs) * scale, NEG)
    nq = qs.shape[0] // ATT_STACK
    sink = jnp.concatenate([jnp.full((1, nq), sink_ref[0, h0 + g], F32) for g in range(ATT_STACK)], axis=1)
    m = jnp.maximum(jnp.max(s, axis=0, keepdims=True), sink)
    p = jnp.exp(s - m)
    ps = jnp.exp(sink - m)
    inv = 1.0 / (jnp.sum(p, axis=0, keepdims=True) + ps)
    return p * inv, ps * inv


def _attn_fwd(att, b_attn, sinks, *, name, after=None):
    T = att.shape[0]
    B = ATT_BLOCK
    NB = T // B
    lead = [] if after is None else [after]

    def body(*refs):
        sink_ref, cur_ref, prev_ref, ba_ref, o_ref = refs[len(lead):]
        Q = ATT_FWD_QROWS
        parts = range(B // Q)
        valid = [_att_valid(pl.program_id(0), a, Q) for a in parts]
        outs = [[None] * len(parts) for _ in range(ATT_HEADS)]
        for h0 in range(0, ATT_HEADS, ATT_STACK):
            qs, kc, vc = _att_load(cur_ref, prev_ref, ba_ref, h0)
            for a in parts:
                keys = slice(Q * a, Q * a + B + Q)
                prob, _ = _att_probs(_att_rows(qs, a, Q), kc[keys], valid[a], sink_ref, h0)
                o = _dot_tn(prob, vc[keys])
                for g in range(ATT_STACK):
                    outs[h0 + g][a] = o[Q * g:Q * (g + 1)]
        o_ref[...] = jnp.concatenate([jnp.concatenate(p, axis=0) for p in outs], axis=1)

    return pl.pallas_call(
        body, name=name, grid=(NB,),
        in_specs=[pl.BlockSpec(memory_space=pl.ANY)] * len(lead) + [
            pl.BlockSpec(memory_space=pltpu.SMEM),
            pl.BlockSpec((B, ATT_COLS), lambda n: (n, 0)),
            pl.BlockSpec((B, ATT_COLS), lambda n: (jnp.maximum(n - 1, 0), 0)),
            pl.BlockSpec((1, ATT_COLS), lambda n: (0, 0))],
        out_specs=pl.BlockSpec((B, ATT_Q_W), lambda n: (n, 0)),
        out_shape=_sds((T, ATT_Q_W), F32),
        compiler_params=_cp(("parallel",)),
    )(*lead, sinks, att, att, b_attn)


def _attn_bwd(att, b_attn, sinks, dmix, *, name):
    T = att.shape[0]
    B, hd = ATT_BLOCK, ATT_HD
    NB = T // B
    scale = 1.0 / math.sqrt(hd)

    def body(sink_ref, cur_ref, prev_ref, ba_ref, do_ref, daq_ref, dakv_ref, dsink_ref, dbq_ref, dbkv_ref, carry_sc):
        n = pl.program_id(0)

        @pl.when(n == 0)
        def _():
            carry_sc[...] = jnp.zeros_like(carry_sc)
            dsink_ref[...] = jnp.zeros_like(dsink_ref)
            dbq_ref[...] = jnp.zeros_like(dbq_ref)
            dbkv_ref[...] = jnp.zeros_like(dbkv_ref)

        @pl.when(n < NB)
        def _():
            valid = _att_valid(n)
            hrow = lax.broadcasted_iota(jnp.int32, (SUBLANES, 128), 0)
            dsink = jnp.zeros((SUBLANES, 128), F32)
            dqs = []
            dks = [jnp.zeros((2 * B, hd), F32)] * ATT_KV
            dvs = [jnp.zeros((2 * B, hd), F32)] * ATT_KV
            for h0 in range(0, ATT_HEADS, ATT_STACK):
                kv = h0 // ATT_GROUP
                qs, kc, vc = _att_load(cur_ref, prev_ref, ba_ref, h0)
                prob, psink = _att_probs(qs, kc, valid, sink_ref, h0)
                dout = jnp.concatenate([do_ref[:, hd * (h0 + g):hd * (h0 + g + 1)] for g in range(ATT_STACK)], axis=0)
                dp = _dot_nt(vc, dout)
                delta = jnp.sum(prob * dp, axis=0, keepdims=True)
                dsc = prob * (dp - delta) * scale
                dq = _dot_tn(dsc, kc)
                dks[kv] = dks[kv] + _dot_nn(dsc, qs)
                dvs[kv] = dvs[kv] + _dot_nn(prob, dout)
                dsk = psink * delta
                for g in range(ATT_STACK):
                    dqs.append(dq[B * g:B * (g + 1)])
                    tot = jnp.sum(dsk[:, B * g:B * (g + 1)], axis=1, keepdims=True)
                    dsink = dsink - jnp.where(hrow == h0 + g, tot, 0.0)
            daq = jnp.concatenate(dqs, axis=1).astype(daq_ref.dtype)
            daq_ref[...] = daq
            dsink_ref[...] += dsink
            dbq_ref[...] += jnp.sum(daq.astype(F32), axis=0, keepdims=True)
            done = carry_sc[...] + jnp.concatenate([d[:B] for d in dks + dvs], axis=1)
            dakv_ref[...] = done.astype(dakv_ref.dtype)
            dbkv_ref[...] += jnp.sum(done.astype(dakv_ref.dtype).astype(F32), axis=0, keepdims=True)
            carry_sc[...] = jnp.concatenate([d[B:] for d in dks + dvs], axis=1)

        @pl.when(n == NB)
        def _():
            done = carry_sc[...]
            dakv_ref[...] = done.astype(dakv_ref.dtype)
            dbkv_ref[...] += jnp.sum(done.astype(dakv_ref.dtype).astype(F32), axis=0, keepdims=True)

    cl = lambda n: jnp.minimum(n, NB - 1)
    return pl.pallas_call(
        body, name=name, grid=(NB + 1,),
        in_specs=[pl.BlockSpec(memory_space=pltpu.SMEM),
                  pl.BlockSpec((B, ATT_COLS), lambda n: (cl(n), 0)),
                  pl.BlockSpec((B, ATT_COLS), lambda n: (jnp.maximum(cl(n) - 1, 0), 0)),
                  pl.BlockSpec((1, ATT_COLS), lambda n: (0, 0)),
                  pl.BlockSpec((B, ATT_Q_W), lambda n: (cl(n), 0))],
        out_specs=[pl.BlockSpec((B, ATT_Q_W), lambda n: (cl(n), 0)),
                   pl.BlockSpec((B, 2 * ATT_KV_W), lambda n: (jnp.maximum(n - 1, 0), 0)),
                   pl.BlockSpec((SUBLANES, 128), lambda n: (0, 0)),
                   pl.BlockSpec((1, ATT_Q_W), lambda n: (0, 0)),
                   pl.BlockSpec((1, 2 * ATT_KV_W), lambda n: (0, 0))],
        out_shape=[_sds((T, ATT_Q_W), BF16), _sds((T, 2 * ATT_KV_W), BF16), _sds((SUBLANES, 128), F32),
                   _sds((1, ATT_Q_W), F32), _sds((1, 2 * ATT_KV_W), F32)],
        scratch_shapes=[pltpu.VMEM((B, 2 * ATT_KV_W), F32)],
        compiler_params=_cp(("arbitrary",)),
    )(sinks, att, att, b_attn, dmix)


def _silu_and_grad(x):
    sg = _sigmoid(x)
    return x * sg, sg * (1.0 + x * (1.0 - sg))


def _mix_fwd_fn(o_raw, hg, o_att, hgw):
    outs = []
    for h in range(HG_HEADS):
        sl = slice(HG_DK * h, HG_DK * (h + 1))
        silu, _ = _silu_and_grad(hg[:, sl])
        outs.append(_rms_fwd(o_raw[:, sl], hgw) * silu)
    outs.append(o_att)
    return (jnp.concatenate(outs, axis=1),)


def _mix_bwd_fn(o_raw, hg, dmix, hgw):
    dos, dhgs = [], []
    dw = jnp.zeros((1, HG_DK), F32)
    for h in range(HG_HEADS):
        sl = slice(HG_DK * h, HG_DK * (h + 1))
        silu, dsilu = _silu_and_grad(hg[:, sl])
        dy = dmix[:, sl]
        dhgs.append(dy * _rms_fwd(o_raw[:, sl], hgw) * dsilu)
        dx, dwh = _rms_bwd(o_raw[:, sl], hgw, dy * silu)
        dos.append(dx)
        dw = dw + dwh
    return jnp.concatenate(dos, axis=1), jnp.concatenate(dhgs, axis=1), dw


def _final_fn(h2, tgt, wf):
    d = h2.shape[1]
    err = _rms_fwd(h2, wf) - tgt
    loss_cols = (0.5 / d) * jnp.sum(err * err, axis=0, keepdims=True)
    dh2, dwf = _rms_bwd(h2, wf, err * (1.0 / d))
    return dh2, dh2, loss_cols, dwf


class _NoExchange:
    def __init__(self, weights):
        self.weights = weights

    def start(self):
        return None

    def w_in(self, after):
        return self.weights["w_in_t"]

    def mid(self, after):
        return None

    def w_out(self, after):
        return {k: self.weights[k] for k in ("w_out", "conv_w8")}

    def rest(self, after):
        return {k: self.weights[k] for k in ("w_gate_t", "w_up_t", "w_down")}

    def ffn_grads(self, gs):
        return None

    def ffn_grads_send(self, after):
        return None


def _local_step(x, tgt, p, ex):
    T, D = x.shape
    row = lambda n, dt: _sds((T, n), dt)
    acc = lambda n: _sds((1, n), F32)

    (u,) = _rowwise(lambda xv, w: (_rms_fwd(xv, w),), [_full(x)], [p["norm_mix_w"]], [row(D, BF16)], [], name="rms_mix",
                    after=ex.start())
    p = dict(p, w_in_t=ex.w_in(u))
    hq, hf, hi, hg, att = _mm_nt(u, p["w_in_t"], splits=[HG_W] * 4 + [ATT_COLS], out_dtype=F32, name="in_proj")
    o_raw, states = _hgrn_fwd(hq, hf, hi, p["lb"], name="hgrn_fwd")
    o_att = _attn_fwd(att, p["b_attn"], p["sinks"], name="attn_fwd", after=ex.mid(o_raw))
    p = dict(p, **ex.w_out(o_att))
    def out_epilogue(prod, xv, w):
        h1v = prod + xv
        return h1v, _rms_fwd(h1v, w)

    h1, v, mix = _mm_nn(None, [p["w_out"]], name="mix_out_proj",
                        prologue=(lambda *a: _mix_fwd_fn(*a)[0], [o_raw, hg, o_att], [p["hg_norm_w"]], row(D, BF16)),
                        epilogue=(out_epilogue, [x], [p["norm_ffn_w"]], [row(D, F32), row(D, BF16)], []))
    p = dict(p, **ex.rest(v))
    gp, up, gate, act = _ffn_in(v, p["w_gate_t"], p["w_up_t"], p["conv_w8"], p["conv_b"], name="ffn_in")
    def down_epilogue(prod, h1v, tgtv, wf):
        return _final_fn(prod + h1v, tgtv, wf)

    dh2, dh2_b, loss_cols, d_final = _mm_nn(
        [[act]], [p["w_down"]], name="down_proj_loss",
        epilogue=(down_epilogue, [h1, tgt], [p["final_norm_w"]], [row(D, F32), row(D, BF16)], [acc(D), acc(D)]))

    g_down = _mm_tn([act], dh2_b, name="g_down")
    dgp, dup, d_conv_w8, d_conv_b = _ffn_back(dh2_b, p["w_down"], gp, up, gate, p["conv_w8"], name="ffn_back")
    g_gate_t = _mm_tn([dgp], v, name="g_gate")
    g_up_t = _mm_tn([dup], v, name="g_up")
    swapping = ex.ffn_grads([g_gate_t, g_up_t, g_down])

    def ffn_norm_bwd(dvv, hv, dh2v, w):
        dx, dw = _rms_bwd(hv, w, dvv)
        dh1v = dx + dh2v
        return dh1v, dh1v, dw

    dh1, dh1_b, d_norm_ffn = _mm_nn(
        [[dgp], [dup]], [p["w_gate_t"], p["w_up_t"]], name="d_v_norm", after=swapping,
        epilogue=(ffn_norm_bwd, [h1, dh2], [p["norm_ffn_w"]], [row(D, F32), row(D, BF16)], [acc(D)]))
    sent = ex.ffn_grads_send(dh1_b)
    def mix_bwd(dmixv, o_rawv, hgv, hgw):
        do_rawv, dhgv, dw = _mix_bwd_fn(o_rawv, hgv, dmixv[:, :HG_W], hgw)
        return do_rawv, dhgv, dmixv[:, HG_W:], dw

    do_raw, dhg, do_att, d_hg_norm = _mm_nn(
        [[dh1_b]], [p["w_out"]], name="d_mix_bwd", w_transposed=True, after=sent,
        epilogue=(mix_bwd, [o_raw, hg], [p["hg_norm_w"]], [row(HG_W, F32), row(HG_W, BF16), row(ATT_Q_W, F32)], [acc(HG_DK)]))
    g_out = _mm_tn([mix], dh1_b, name="g_out")
    daq, dakv, d_sinks8, d_bq, d_bkv = _attn_bwd(att, p["b_attn"], p["sinks"], do_att, name="attn_bwd")
    dhq, dhf, dhi, d_lb = _hgrn_bwd(hq, hf, hi, p["lb"], states, do_raw, name="hgrn_bwd")
    pieces = [dhq, dhf, dhi, dhg, daq, dakv]
    g_in_t = _mm_tn(pieces, u, name="g_in")

    def mix_norm_bwd(duv, xv, dh1v, w):
        dx, dw = _rms_bwd(xv, w, duv)
        return dx + dh1v, dw

    dx, d_norm_mix = _mm_nn([pieces], [p["w_in_t"]], name="d_u_norm",
                            epilogue=(mix_norm_bwd, [x, dh1], [p["norm_mix_w"]], [row(D, F32)], [acc(D)]))
    grads = dict(g_in_t=g_in_t, g_out=g_out, g_gate_t=g_gate_t, g_up_t=g_up_t, g_down=g_down,
                 norm_mix_w=d_norm_mix, b_attn=jnp.concatenate([d_bq, d_bkv], axis=1), lb=d_lb, hg_norm_w=d_hg_norm,
                 sinks8=d_sinks8, norm_ffn_w=d_norm_ffn, conv_w8=d_conv_w8, conv_b=d_conv_b, final_norm_w=d_final)
    return loss_cols, dx, grads


SLAB = (IN_COLS // N_CHIPS, D_FF // N_CHIPS, D_FF // N_CHIPS, D_FF // N_CHIPS, D_MODEL // N_CHIPS)
N_W = len(SLAB)
PACK_OFF = tuple(sum(SLAB[:i]) for i in range(N_W))
PACK_ROWS = sum(SLAB)
FULL_OFF = tuple(N_CHIPS * o for o in PACK_OFF)
FULL_ROWS = N_CHIPS * PACK_ROWS
HALF = tuple(s // 2 for s in SLAB)
HPACK_OFF = tuple(sum(HALF[:i]) for i in range(N_W))
HPACK_ROWS = sum(HALF)
HFULL_OFF = tuple(N_CHIPS * o for o in HPACK_OFF)
HFULL_ROWS = N_CHIPS * HPACK_ROWS
CHIP_FLIPS = ((1, 0), (0, 1), (1, 1))
N_DEV = 8
BF16_ROWS = 16
ANY = pl.BlockSpec(memory_space=pl.ANY)


def _pos():
    return lax.axis_index("x"), lax.axis_index("y"), lax.axis_index("c")


def _flip(v, f):
    return 1 - v if f else v


def _rcopy(src, dst, ssem, rsem, dev):
    return pltpu.make_async_remote_copy(src_ref=src, dst_ref=dst, send_sem=ssem, recv_sem=rsem, device_id=dev,
                                        device_id_type=pl.DeviceIdType.MESH)


def _rows(ref, start, n, align=None):
    if not isinstance(start, int):
        if align is None:
            align = SUBLANES * (4 // jnp.dtype(ref.dtype).itemsize)
        start = pl.multiple_of(start, align)
    return ref.at[pl.ds(start, n), :]


FFN_W = (1, 2, 3)
N_PEER = 1 + len(CHIP_FLIPS)
HBM = pl.BlockSpec(memory_space=pltpu.HBM)
SEM = pl.BlockSpec(memory_space=pltpu.SEMAPHORE)
EFFECT = pltpu.SideEffectType.DATAFLOW_SIDE_EFFECTING
LANES = 128


def _sent_rows(k, w, c):
    return (0, SLAB[w]) if k == 0 else (c * HALF[w], HALF[w])


def _gather_start(pack, cw8):
    D = pack.shape[1]
    lands = [lax.empty((N_CHIPS * SLAB[0], D), pack.dtype), lax.empty((3 * N_CHIPS * SLAB[1], D), pack.dtype),
             lax.empty((N_CHIPS * SLAB[4], D), pack.dtype), lax.empty((N_CHIPS,) + cw8.shape, cw8.dtype)]
    bufs = [pack, cw8] + lands

    def body(pack_ref, cw_ref, l_in, l_ffn, l_out, l_cw, *rest):
        in_send, in_recv, out_send, out_recv, ffn_send, ffn_recv = rest[:6]
        token = rest[-1]
        x, y, c = _pos()
        q = 2 * x + y
        peers = _gather_peers(x, y, c)

        def send(k, peer, w, land, base, ssem, rsem):
            r0, n = _sent_rows(k, w, c)
            _rcopy(_rows(pack_ref, PACK_OFF[w] + r0, n), _rows(land, base + q * SLAB[w] + r0, n), ssem, rsem, peer).start()

        for k, peer in enumerate(peers):
            send(k, peer, 0, l_in, 0, in_send.at[k], in_recv.at[k])
        for k, peer in enumerate(peers):
            send(k, peer, 4, l_out, 0, out_send.at[k], out_recv.at[k])
            _rcopy(cw_ref, l_cw.at[q], out_send.at[N_PEER + k], out_recv.at[N_PEER + k], peer).start()
        for j, w in enumerate(FFN_W):
            for k, peer in enumerate(peers):
                send(k, peer, w, l_ffn, j * N_CHIPS * SLAB[w], ffn_send.at[k], ffn_recv.at[k])
        token[...] = jnp.zeros_like(token)

    n_sem = (N_PEER, N_PEER, 2 * N_PEER, 2 * N_PEER, N_PEER, N_PEER)
    outs = pl.pallas_call(
        body, name="gather_start", in_specs=[HBM] * len(bufs),
        out_specs=[SEM] * len(n_sem) + [HBM] * len(bufs) + [pl.BlockSpec(memory_space=pltpu.VMEM)],
        out_shape=[pltpu.SemaphoreType.DMA((n,)) for n in n_sem]
        + [pltpu.HBM(b.shape, b.dtype) for b in bufs] + [TOKEN],
        input_output_aliases={i: len(n_sem) + i for i in range(len(bufs))},
        compiler_params=pltpu.CompilerParams(has_side_effects=EFFECT),
    )(*[pltpu.with_memory_space_constraint(b, pltpu.HBM) for b in bufs])
    bufs_out = outs[len(n_sem):]
    return dict(in_sems=outs[0:2], out_sems=outs[2:4], ffn_sems=outs[4:6], pack=bufs_out[0], cw=bufs_out[1], l_in=bufs_out[2],
                l_ffn=bufs_out[3], l_out=bufs_out[4], l_cw=bufs_out[5], token=bufs_out[6])


def _gather_peers(x, y, c):
    return [(x, y, 1 - c)] + [(_flip(x, fx), _flip(y, fy), c) for fx, fy in CHIP_FLIPS]


def _gather_wait_in(g, after):
    def body(pack_ref, l_in, send, recv, after_ref, pack_out, l_out):
        for k, peer in enumerate(_gather_peers(*_pos())):
            n = _sent_rows(k, 0, 0)[1]
            cp = _rcopy(_rows(pack_ref, PACK_OFF[0], n), _rows(l_in, 0, n), send.at[k], recv.at[k], peer)
            cp.wait_send()
            cp.wait_recv()

    return pl.pallas_call(
        body, name="gather_wait_in", in_specs=[HBM, HBM, SEM, SEM, ANY], out_specs=[HBM, HBM],
        out_shape=[pltpu.HBM(g["pack"].shape, g["pack"].dtype), pltpu.HBM(g["l_in"].shape, g["l_in"].dtype)],
        input_output_aliases={0: 0, 1: 1}, compiler_params=pltpu.CompilerParams(has_side_effects=EFFECT),
    )(g["pack"], g["l_in"], *g["in_sems"], after)


def _gather_wait_out(g, pack, after):
    def body(pack_ref, cw_ref, l_out, l_cw, o_send, o_recv, after_ref, o_out, o_cw):
        for k, peer in enumerate(_gather_peers(*_pos())):
            n_out = _sent_rows(k, 4, 0)[1]
            for cp in (_rcopy(_rows(pack_ref, PACK_OFF[4], n_out), _rows(l_out, 0, n_out), o_send.at[k], o_recv.at[k], peer),
                       _rcopy(cw_ref, l_cw.at[0], o_send.at[N_PEER + k], o_recv.at[N_PEER + k], peer)):
                cp.wait_send()
                cp.wait_recv()

    ins = [pack, g["cw"], g["l_out"], g["l_cw"]]
    return pl.pallas_call(
        body, name="gather_wait_out", in_specs=[HBM] * 4 + [SEM] * 2 + [ANY], out_specs=[HBM] * 2,
        out_shape=[pltpu.HBM(b.shape, b.dtype) for b in ins[2:]],
        input_output_aliases={2: 0, 3: 1}, compiler_params=pltpu.CompilerParams(has_side_effects=EFFECT),
    )(*ins, *g["out_sems"], after)


def _gather_wait_ffn(g, pack, after):
    def body(pack_ref, l_ffn, f_send, f_recv, after_ref, o_ffn):
        for k, peer in enumerate(_gather_peers(*_pos())):
            n_ffn = len(FFN_W) * _sent_rows(k, FFN_W[0], 0)[1]
            cp = _rcopy(_rows(pack_ref, PACK_OFF[FFN_W[0]], n_ffn), _rows(l_ffn, 0, n_ffn), f_send.at[k], f_recv.at[k], peer)
            cp.wait_send()
            cp.wait_recv()

    return pl.pallas_call(
        body, name="gather_wait_ffn", in_specs=[HBM] * 2 + [SEM] * 2 + [ANY], out_specs=HBM,
        out_shape=pltpu.HBM(g["l_ffn"].shape, g["l_ffn"].dtype),
        input_output_aliases={1: 0}, compiler_params=pltpu.CompilerParams(has_side_effects=EFFECT),
    )(pack, g["l_ffn"], *g["ffn_sems"], after)


FWD_IN = ((0, 0, 0),)
FWD_OUT = ((0, 4, 0),)
FWD_FFN = tuple((0, w, j * N_CHIPS * SLAB[w]) for j, w in enumerate(FFN_W))


def _forward_copies(layout, src, dst, send_sems, recv_sems):
    x, y, c = _pos()
    sib = (x, y, 1 - c)
    cps = []
    for fx, fy in CHIP_FLIPS:
        qa = 2 * _flip(x, fx) + _flip(y, fy)
        for bi, w, base in layout:
            r0 = base + qa * SLAB[w] + c * HALF[w]
            cps.append(_rcopy(_rows(src[bi], r0, HALF[w]), _rows(dst[bi], r0, HALF[w]),
                              send_sems.at[len(cps)], recv_sems.at[len(cps)], sib))
    return cps


def _forward_in(l_in):
    n = len(CHIP_FLIPS) * len(FWD_IN)

    def body(in_ref, out_ref, send_sems, recv_sems):
        cps = _forward_copies(FWD_IN, [in_ref], [out_ref], send_sems, recv_sems)
        for cp in cps:
            cp.start()
        for cp in cps:
            cp.wait_recv()
        for cp in cps:
            cp.wait_send()

    return pl.pallas_call(
        body, name="forward_in", in_specs=[ANY], out_specs=ANY, out_shape=_sds(l_in.shape, l_in.dtype),
        input_output_aliases={0: 0},
        scratch_shapes=[pltpu.SemaphoreType.DMA((n,)), pltpu.SemaphoreType.DMA((n,))],
    )(l_in)


def _forward_start(layout, land, *, name):
    n = len(CHIP_FLIPS) * len(layout)

    def body(in_ref, send_sems, recv_sems, out_ref, token):
        for cp in _forward_copies(layout, [in_ref], [in_ref], send_sems, recv_sems):
            cp.start()
        token[...] = jnp.zeros_like(token)

    outs = pl.pallas_call(
        body, name=name, in_specs=[HBM],
        out_specs=[SEM, SEM, HBM, pl.BlockSpec(memory_space=pltpu.VMEM)],
        out_shape=[pltpu.SemaphoreType.DMA((n,)), pltpu.SemaphoreType.DMA((n,)), pltpu.HBM(land.shape, land.dtype), TOKEN],
        input_output_aliases={0: 2}, compiler_params=pltpu.CompilerParams(has_side_effects=EFFECT),
    )(pltpu.with_memory_space_constraint(land, pltpu.HBM))
    return dict(sems=outs[0:2], land=outs[2], token=outs[3])


def _forward_wait(layout, s, after, *, name):
    def body(in_ref, send_sems, recv_sems, after_ref, out_ref):
        for cp in _forward_copies(layout, [in_ref], [in_ref], send_sems, recv_sems):
            cp.wait_send()
            cp.wait_recv()

    return pl.pallas_call(
        body, name=name, in_specs=[HBM, SEM, SEM, ANY], out_specs=HBM,
        out_shape=pltpu.HBM(s["land"].shape, s["land"].dtype),
        input_output_aliases={0: 0}, compiler_params=pltpu.CompilerParams(has_side_effects=EFFECT),
    )(s["land"], *s["sems"], after)


def _exchange_halves(ws, gs, small, *, name):
    D = gs[0].shape[1]
    n = len(ws)
    has_small = small is not None

    def body(*refs):
        g = refs[:n]
        t = refs[n + has_small:2 * n + has_small]
        sems = refs[2 * n + 2 * has_small:]
        d2d_send, d2d_recv = sems[0], sems[1]
        x, y, c = _pos()
        sib = (x, y, 1 - c)
        drains = []
        for i, w in enumerate(ws):
            h = HALF[w]
            for qq in range(N_CHIPS):
                _rcopy(_rows(g[i], qq * SLAB[w] + (1 - c) * h, h), _rows(t[i], qq * h, h),
                       d2d_send.at[i], d2d_recv.at[i], sib).start()
            drains.append(_rcopy(t[i], t[i], d2d_send.at[i], d2d_recv.at[i], sib))
        if has_small:
            small_ref, sall_ref = refs[n], refs[2 * n + 1]
            sm_send, sm_recv, loc_sem = sems[2], sems[3], sems[4]
            me = 4 * x + 2 * y + c
            own_small = pltpu.make_async_copy(small_ref, sall_ref.at[me], loc_sem)
            own_small.start()
            for f in range(1, N_DEV):
                peer = (_flip(x, f & 4), _flip(y, f & 2), _flip(c, f & 1))
                cp = _rcopy(small_ref, sall_ref.at[me], sm_send.at[f - 1], sm_recv.at[f - 1], peer)
                cp.start()
                drains.append(cp)
        for d in drains:
            d.wait_recv()
        for d in drains:
            d.wait_send()
        if has_small:
            own_small.wait()

    out_shape = [_sds((N_CHIPS * HALF[w], D), gs[0].dtype) for w in ws]
    scratch = [pltpu.SemaphoreType.DMA((n,)), pltpu.SemaphoreType.DMA((n,))]
    if has_small:
        out_shape.append(_sds((N_DEV,) + small.shape, F32))
        scratch += [pltpu.SemaphoreType.DMA((N_DEV - 1,)), pltpu.SemaphoreType.DMA((N_DEV - 1,)), pltpu.SemaphoreType.DMA]
    return pl.pallas_call(
        body, name=name, in_specs=[ANY] * (n + has_small), out_specs=[ANY] * (n + has_small),
        out_shape=out_shape, scratch_shapes=scratch,
    )(*gs, *([small] if has_small else []))


def _halves_copies(ws, g, t, send_sems, recv_sems):
    x, y, c = _pos()
    sib = (x, y, 1 - c)
    cps = []
    for i, w in enumerate(ws):
        h = HALF[w]
        for qq in range(N_CHIPS):
            cps.append(_rcopy(_rows(g[i], qq * SLAB[w] + (1 - c) * h, h), _rows(t[i], qq * h, h),
                              send_sems.at[N_CHIPS * i + qq], recv_sems.at[N_CHIPS * i + qq], sib))
    return cps


def _halves_start(ws, gs, *, name):
    D = gs[0].shape[1]
    n = len(ws)
    bufs = list(gs) + [lax.empty((N_CHIPS * HALF[w], D), gs[0].dtype) for w in ws]

    def body(*refs):
        for cp in _halves_copies(ws, refs[:n], refs[n:2 * n], refs[2 * n], refs[2 * n + 1]):
            cp.start()
        refs[-1][...] = jnp.zeros_like(refs[-1])

    outs = pl.pallas_call(
        body, name=name, in_specs=[HBM] * (2 * n),
        out_specs=[SEM, SEM] + [HBM] * (2 * n) + [pl.BlockSpec(memory_space=pltpu.VMEM)],
        out_shape=[pltpu.SemaphoreType.DMA((N_CHIPS * n,)), pltpu.SemaphoreType.DMA((N_CHIPS * n,))]
        + [pltpu.HBM(b.shape, b.dtype) for b in bufs] + [TOKEN],
        input_output_aliases={i: 2 + i for i in range(2 * n)},
        compiler_params=pltpu.CompilerParams(has_side_effects=EFFECT),
    )(*[pltpu.with_memory_space_constraint(b, pltpu.HBM) for b in bufs])
    return dict(sems=outs[0:2], gs=outs[2:2 + n], theirs=outs[2 + n:2 + 2 * n], token=outs[-1])


def _halves_wait(ws, s, after, *, name):
    n = len(ws)

    def body(*refs):
        for cp in _halves_copies(ws, refs[:n], refs[n:2 * n], refs[2 * n], refs[2 * n + 1]):
            cp.wait_send()
            cp.wait_recv()

    bufs = list(s["gs"]) + list(s["theirs"])
    outs = pl.pallas_call(
        body, name=name, in_specs=[HBM] * (2 * n) + [SEM, SEM, ANY], out_specs=[HBM] * (2 * n),
        out_shape=[pltpu.HBM(b.shape, b.dtype) for b in bufs],
        input_output_aliases={i: i for i in range(2 * n)},
        compiler_params=pltpu.CompilerParams(has_side_effects=EFFECT),
    )(*bufs, *s["sems"], after)
    return outs[:n], outs[n:]


REDUCE_SPLIT = 2


def _chip_partial(ws, gs, theirs, *, name, out_dtype=F32):
    D = gs[0].shape[1]
    n = len(ws)

    def body(*refs):
        for i in range(n):
            refs[2 * n + i][...] = (refs[i][...].astype(F32) + refs[n + i][...].astype(F32)).astype(out_dtype)

    blk = [HALF[w] // REDUCE_SPLIT for w in ws]
    mine = [pl.BlockSpec((b, D), lambda qq, j: ((2 * qq + lax.axis_index("c")) * REDUCE_SPLIT + j, 0)) for b in blk]
    flat = [pl.BlockSpec((b, D), lambda qq, j: (qq * REDUCE_SPLIT + j, 0)) for b in blk]
    return pl.pallas_call(
        body, name=name, grid=(N_CHIPS, REDUCE_SPLIT), in_specs=mine + flat, out_specs=flat,
        out_shape=[_sds((N_CHIPS * HALF[w], D), out_dtype) for w in ws],
        compiler_params=_cp(("parallel", "parallel")),
    )(*gs, *theirs)


def _partial_copies(ws, part, got, send_sems, recv_sems):
    x, y, c = _pos()
    cps = []
    for k, (fx, fy) in enumerate(CHIP_FLIPS):
        peer = (_flip(x, fx), _flip(y, fy), c)
        qp = 2 * _flip(x, fx) + _flip(y, fy)
        for i, w in enumerate(ws):
            cps.append(_rcopy(_rows(part[i], qp * HALF[w], HALF[w]), _rows(got[i], k * HALF[w], HALF[w]),
                              send_sems.at[len(ws) * k + i], recv_sems.at[len(ws) * k + i], peer))
    return cps


def _send_chip_partials(ws, parts, *, name):
    D = parts[0].shape[1]
    n = len(ws)

    def body(*refs):
        cps = _partial_copies(ws, refs[:n], refs[n:2 * n], refs[2 * n], refs[2 * n + 1])
        for cp in cps:
            cp.start()
        for cp in cps:
            cp.wait_recv()
        for cp in cps:
            cp.wait_send()

    return pl.pallas_call(
        body, name=name, in_specs=[ANY] * n, out_specs=[ANY] * n,
        out_shape=[_sds((len(CHIP_FLIPS) * HALF[w], D), parts[0].dtype) for w in ws],
        scratch_shapes=[pltpu.SemaphoreType.DMA((len(CHIP_FLIPS) * n,)), pltpu.SemaphoreType.DMA((len(CHIP_FLIPS) * n,))],
    )(*parts)


def _send_start(ws, parts, *, name):
    D = parts[0].shape[1]
    n = len(ws)
    bufs = list(parts) + [lax.empty((len(CHIP_FLIPS) * HALF[w], D), parts[0].dtype) for w in ws]

    def body(*refs):
        send_sems, recv_sems = refs[2 * n], refs[2 * n + 1]
        for cp in _partial_copies(ws, refs[:n], refs[n:2 * n], send_sems, recv_sems):
            cp.start()
        refs[-1][...] = jnp.zeros_like(refs[-1])

    outs = pl.pallas_call(
        body, name=name, in_specs=[HBM] * (2 * n),
        out_specs=[SEM, SEM] + [HBM] * (2 * n) + [pl.BlockSpec(memory_space=pltpu.VMEM)],
        out_shape=[pltpu.SemaphoreType.DMA((len(CHIP_FLIPS) * n,)), pltpu.SemaphoreType.DMA((len(CHIP_FLIPS) * n,))]
        + [pltpu.HBM(b.shape, b.dtype) for b in bufs] + [TOKEN],
        input_output_aliases={i: 2 + i for i in range(2 * n)},
        compiler_params=pltpu.CompilerParams(has_side_effects=EFFECT),
    )(*[pltpu.with_memory_space_constraint(b, pltpu.HBM) for b in bufs])
    return dict(sems=outs[0:2], parts=outs[2:2 + n], got=outs[2 + n:2 + 2 * n], token=outs[-1])


def _send_wait(ws, s, after, *, name):
    n = len(ws)

    def body(*refs):
        for cp in _partial_copies(ws, refs[:n], refs[n:2 * n], refs[2 * n], refs[2 * n + 1]):
            cp.wait_send()
            cp.wait_recv()

    bufs = list(s["parts"]) + list(s["got"])
    outs = pl.pallas_call(
        body, name=name, in_specs=[HBM] * (2 * n) + [SEM, SEM] + [ANY] * len(after), out_specs=[HBM] * (2 * n),
        out_shape=[pltpu.HBM(b.shape, b.dtype) for b in bufs],
        input_output_aliases={i: i for i in range(2 * n)},
        compiler_params=pltpu.CompilerParams(has_side_effects=EFFECT),
    )(*bufs, *s["sems"], *after)
    return outs[:n], outs[n:]


def _chip_reduce(ws, parts, got, *, name, after=None):
    D = parts[0].shape[1]
    nk = len(CHIP_FLIPS)
    n = len(ws)
    extra = [] if after is None else [after]

    def body(*refs):
        refs = refs[len(extra):]
        outs = refs[(1 + nk) * n:]
        for i in range(n):
            acc = refs[i][...].astype(F32)
            for k in range(nk):
                acc = acc + refs[n * (1 + k) + i][...].astype(F32)
            outs[i][...] = acc

    blk = [HALF[w] // REDUCE_SPLIT for w in ws]

    def q_idx(j):
        return (2 * lax.axis_index("x") + lax.axis_index("y")) * REDUCE_SPLIT + j

    in_specs = [pl.BlockSpec((b, D), lambda j: (q_idx(j), 0)) for b in blk]
    for k in range(nk):
        in_specs += [pl.BlockSpec((b, D), functools.partial(lambda j, k: (k * REDUCE_SPLIT + j, 0), k=k)) for b in blk]
    out_specs = [pl.BlockSpec((b, D), lambda j: (lax.axis_index("c") * REDUCE_SPLIT + j, 0)) for b in blk]
    return pl.pallas_call(
        body, name=name, grid=(REDUCE_SPLIT,), in_specs=[ANY] * len(extra) + in_specs, out_specs=out_specs,
        out_shape=[_sds((SLAB[w], D), F32) for w in ws],
        compiler_params=_cp(("parallel",)),
    )(*extra, *parts, *[g for _ in range(nk) for g in got])


def _exchange_reduced(ws, shards, *, name):
    n = len(ws)

    def body(*refs):
        ins, outs = refs[:n], refs[n:2 * n]
        send_sems, recv_sems = refs[2 * n], refs[2 * n + 1]
        x, y, c = _pos()
        sib = (x, y, 1 - c)
        cps = []
        for i, w in enumerate(ws):
            cp = _rcopy(_rows(ins[i], c * HALF[w], HALF[w]), _rows(outs[i], c * HALF[w], HALF[w]),
                        send_sems.at[i], recv_sems.at[i], sib)
            cp.start()
            cps.append(cp)
        for cp in cps:
            cp.wait_recv()
        for cp in cps:
            cp.wait_send()

    return pl.pallas_call(
        body, name=name, in_specs=[ANY] * n, out_specs=[ANY] * n,
        out_shape=[_sds(s.shape, s.dtype) for s in shards], input_output_aliases={i: i for i in range(n)},
        scratch_shapes=[pltpu.SemaphoreType.DMA((n,)), pltpu.SemaphoreType.DMA((n,))],
    )(*shards)


def _adamw_fn(w, g, m, v):
    m2 = ADAM_B1 * m + (1.0 - ADAM_B1) * g
    v2 = ADAM_B2 * v + (1.0 - ADAM_B2) * (g * g)
    m_hat = m2 / (1.0 - ADAM_B1 ** ADAM_STEP)
    v_hat = v2 / (1.0 - ADAM_B2 ** ADAM_STEP)
    return -ADAM_LR * (m_hat / (jnp.sqrt(v_hat) + ADAM_EPS) + ADAM_WD * w), m2, v2


def _adamw(w, g, m, v, *, name):
    shp = _sds(w.shape, F32)
    rows = w.shape[0]
    tm = max(t for t in range(SUBLANES, 512 + 1, SUBLANES) if rows % t == 0)
    return _rowwise(lambda wv, gv, mv, vv: (gv, *_adamw_fn(wv, gv, mv, vv)), [_full(w), _full(g), _full(m), _full(v)], [],
                    [shp] * 4, [], name=name, tm=tm)


SMALL_SEGS = (("loss", 8), ("norm_mix_w", 8), ("b_attn", 8), ("lb_logits", 8), ("hg_norm_w", 8), ("sinks", 8),
              ("norm_ffn_w", 8), ("conv_w", 72), ("conv_b", 24), ("final_norm_w", 8))
SMALL_OFF = {n: sum(r for _, r in SMALL_SEGS[:i]) for i, (n, _) in enumerate(SMALL_SEGS)}
SMALL_ROWS = sum(r for _, r in SMALL_SEGS)
LANES = 128


def _pack_small(parts):
    segs = []
    for n, r in SMALL_SEGS:
        a = parts.get(n)
        flat = jnp.zeros((0,), F32) if a is None else a.reshape(-1).astype(F32)
        segs.append(jnp.pad(flat, (0, r * LANES - flat.shape[0])).reshape(r, LANES))
    return jnp.concatenate(segs, axis=0)


def _unpack_small(pack, n, shape):
    size = math.prod(shape)
    r0 = SMALL_OFF[n]
    return pack[r0:r0 + dict(SMALL_SEGS)[n]].reshape(-1)[:size].reshape(shape)


def _small_update(sall, wp, mp, vp, *, after):
    R = SMALL_ROWS
    r_lb = SMALL_OFF["lb_logits"]

    def body(after_ref, s_ref, w_ref, m_ref, v_ref, g_ref, d_ref, m2_ref, v2_ref, loss_ref):
        g = s_ref[0]
        for i in range(1, N_DEV):
            g = g + s_ref[i]
        tot = jnp.sum(jnp.sum(g[0:8], axis=1, keepdims=True), axis=0, keepdims=True)
        loss_ref[...] = jnp.broadcast_to(tot, loss_ref.shape)
        lg = w_ref[r_lb:r_lb + 8, :]
        p0 = _sigmoid(lg - pltpu.roll(lg, 4, 0))
        d = g[r_lb:r_lb + 8]
        d = d + pltpu.roll(d, 4, 0)
        sign = jnp.where(lax.broadcasted_iota(jnp.int32, d.shape, 0) < 4, 1.0, -1.0)
        g = jnp.concatenate([g[:r_lb], sign * d * p0 * (1.0 - p0), g[r_lb + 8:]], axis=0)
        g_ref[...] = g
        d_ref[...], m2_ref[...], v2_ref[...] = _adamw_fn(w_ref[...], g, m_ref[...], v_ref[...])

    full = pl.BlockSpec((R, LANES), lambda: (0, 0))
    return pl.pallas_call(
        body, name="small_update",
        in_specs=[ANY, pl.BlockSpec((N_DEV, R, LANES), lambda: (0, 0, 0)), full, full, full],
        out_specs=[full, full, full, full, pl.BlockSpec((8, LANES), lambda: (0, 0))],
        out_shape=[_sds((R, LANES), F32)] * 4 + [_sds((8, LANES), F32)],
        compiler_params=_cp(),
    )(after, sall, wp, mp, vp)


def _lb_fwd(lb_logits):
    n = lb_logits.shape[1]

    def body(l_ref, o_ref):
        o_ref[...] = _sigmoid(l_ref[0:1, :] - l_ref[1:2, :])

    return pl.pallas_call(body, name="lb_fwd", out_shape=jax.ShapeDtypeStruct((1, n), F32), compiler_params=_cp())(lb_logits)


class _MeshExchange:
    def __init__(self, pack, cw8):
        self.gather = _gather_start(pack, cw8)
        self.sent = None
        self.conv_w8 = None

    def start(self):
        return self.gather["token"]

    def w_in(self, after):
        self.pack, l_in = _gather_wait_in(self.gather, after)
        return (_forward_in(l_in), N_CHIPS * SLAB[0], 0)

    def mid(self, after):
        l_out, l_cw = _gather_wait_out(self.gather, self.pack, after)
        self.conv_w8 = jnp.concatenate([l_cw[i] for i in range(N_CHIPS)], axis=1)
        self.passing_out = _forward_start(FWD_OUT, l_out, name="forward_out_start")
        return self.passing_out["token"]

    def w_out(self, after):
        l_ffn = _gather_wait_ffn(self.gather, self.pack, after)
        self.passing_ffn = _forward_start(FWD_FFN, l_ffn, name="forward_ffn_start")
        l_out = _forward_wait(FWD_OUT, self.passing_out, self.passing_ffn["token"], name="forward_out_wait")
        return dict(w_out=(l_out, N_CHIPS * SLAB[4], 0), conv_w8=self.conv_w8)

    def rest(self, after):
        l_ffn = _forward_wait(FWD_FFN, self.passing_ffn, after, name="forward_ffn_wait")
        rows = N_CHIPS * SLAB[FFN_W[0]]
        return dict(w_gate_t=(l_ffn, rows, 0), w_up_t=(l_ffn, rows, 1), w_down=(l_ffn, rows, 2))

    def ffn_grads(self, gs):
        self.swap = _halves_start(FFN_W, gs, name="halves_ffn_start")
        return self.swap["token"]

    def ffn_grads_send(self, after):
        gs, theirs = _halves_wait(FFN_W, self.swap, after, name="halves_ffn_wait")
        parts = _chip_partial(FFN_W, gs, theirs, name="chip_partial_ffn", out_dtype=BF16)
        self.sent = _send_start(FFN_W, parts, name="send_ffn_start")
        return self.sent["token"]


def kernel(x, norm_mix_w, w_in, b_attn, lb_logits, hg_norm_w, sinks, w_out, norm_ffn_w, w_gate, w_up, conv_w, conv_b, w_down, final_norm_w, loss_target, m_norm_mix_w, m_w_in, m_b_attn, m_lb_logits, m_hg_norm_w, m_sinks, m_w_out, m_norm_ffn_w, m_w_gate, m_w_up, m_conv_w, m_conv_b, m_w_down, m_final_norm_w, v_norm_mix_w, v_w_in, v_b_attn, v_lb_logits, v_hg_norm_w, v_sinks, v_w_out, v_norm_ffn_w, v_w_gate, v_w_up, v_conv_w, v_conv_b, v_w_down, v_final_norm_w):
    D = D_MODEL
    q = 2 * lax.axis_index("x") + lax.axis_index("y")
    ccols = D_FF // N_CHIPS

    pack = jnp.concatenate([w_in[0].T, w_gate[0].T, w_up[0].T, w_down[0], w_out[0]], axis=0).astype(BF16)
    cw8 = jnp.concatenate([conv_w[0], jnp.zeros((SUBLANES - 3, ccols), F32)], axis=0)
    ex = _MeshExchange(pack, cw8)
    p = dict(norm_mix_w=norm_mix_w, b_attn=b_attn, lb=_lb_fwd(lb_logits), hg_norm_w=hg_norm_w, sinks=sinks,
             norm_ffn_w=norm_ffn_w, conv_b=conv_b, final_norm_w=final_norm_w.reshape(1, D))
    loss_cols, dx, g = _local_step(x[0], loss_target[0], p, ex)
    conv_w8 = ex.conv_w8

    small = _pack_small(dict(loss=loss_cols, norm_mix_w=g["norm_mix_w"], b_attn=g["b_attn"], lb_logits=g["lb"],
                             hg_norm_w=g["hg_norm_w"], sinks=g["sinks8"], norm_ffn_w=g["norm_ffn_w"],
                             conv_w=g["conv_w8"][:3], conv_b=g["conv_b"], final_norm_w=g["final_norm_w"]))
    parts_ffn, got_ffn = _send_wait(FFN_W, ex.sent, [dx], name="send_ffn_wait")
    late = (0, 4)
    gs = [g["g_in_t"], g["g_out"]]
    *theirs, sall = _exchange_halves(late, gs, small, name="exchange_halves_late")
    parts_late = _chip_partial(late, gs, theirs, name="chip_partial_late", out_dtype=BF16)
    sent_late = _send_start(late, parts_late, name="send_late_start")
    big = {}

    def finish(ws, parts, got, specs, tag, after):
        shards = _exchange_reduced(ws, _chip_reduce(ws, parts, got, name="chip_reduce_" + tag, after=after),
                                   name="exchange_reduced_" + tag)
        deltas = []
        for gw, (n, w, m, v, tr) in zip(shards, specs):
            view = (lambda a: a[0].T) if tr else (lambda a: a[0])
            back = (lambda a: a.T[None]) if tr else (lambda a: a[None])
            res = _adamw(view(w), gw, view(m), view(v), name="adamw_" + n)
            big[n] = tuple(back(r) for r in res)
            deltas.append(res[1])
        return deltas

    done_ffn = finish(FFN_W, parts_ffn, got_ffn, (("w_gate", w_gate, m_w_gate, v_w_gate, True),
                                                  ("w_up", w_up, m_w_up, v_w_up, True),
                                                  ("w_down", w_down, m_w_down, v_w_down, False)), "ffn", sent_late["token"])

    def place(a):
        return lax.dynamic_update_slice(jnp.zeros((3, D_FF), F32), a[0], (0, q * ccols))

    def small_pack(ws, cw):
        nm, ba, lbl, hg, sk, nf, cb, fn = ws
        return _pack_small(dict(norm_mix_w=nm, b_attn=ba, lb_logits=lbl, hg_norm_w=hg,
                                sinks=jnp.broadcast_to(sk.reshape(ATT_HEADS, 1), (ATT_HEADS, LANES)), norm_ffn_w=nf,
                                conv_w=cw, conv_b=cb, final_norm_w=fn))

    wp = small_pack((norm_mix_w, b_attn, lb_logits, hg_norm_w, sinks, norm_ffn_w, conv_b, final_norm_w), conv_w8[:3])
    mp = small_pack((m_norm_mix_w, m_b_attn, m_lb_logits, m_hg_norm_w, m_sinks, m_norm_ffn_w, m_conv_b, m_final_norm_w),
                    place(m_conv_w))
    vp = small_pack((v_norm_mix_w, v_b_attn, v_lb_logits, v_hg_norm_w, v_sinks, v_norm_ffn_w, v_conv_b, v_final_norm_w),
                    place(v_conv_w))
    outs = _small_update(sall, wp, mp, vp, after=sent_late["token"])
    loss = outs[4][0, 0]
    parts_late, got_late = _send_wait(late, sent_late, [*done_ffn, outs[4]], name="send_late_wait")
    finish(late, parts_late, got_late, (("w_in", w_in, m_w_in, v_w_in, True), ("w_out", w_out, m_w_out, v_w_out, False)),
           "late", None)

    def small_out(pk, n, ref):
        if n == "sinks":
            return pk[SMALL_OFF[n]:SMALL_OFF[n] + ATT_HEADS, 0].reshape(ref.shape)
        if n == "conv_w":
            full = _unpack_small(pk, n, (3, D_FF))
            return lax.dynamic_slice(full, (0, q * ccols), (3, ccols))[None]
        return _unpack_small(pk, n, ref.shape)

    refs = dict(norm_mix_w=norm_mix_w, b_attn=b_attn, lb_logits=lb_logits, hg_norm_w=hg_norm_w, sinks=sinks,
                norm_ffn_w=norm_ffn_w, conv_w=conv_w, conv_b=conv_b, final_norm_w=final_norm_w)
    order = ("norm_mix_w", "w_in", "b_attn", "lb_logits", "hg_norm_w", "sinks", "w_out", "norm_ffn_w", "w_gate", "w_up",
             "conv_w", "conv_b", "w_down", "final_norm_w")
    res = [loss, dx[None]]
    for k in range(4):
        for n in order:
            res.append(big[n][k] if n in big else small_out(outs[k], n, refs[n]))
    return tuple(res)
```

```python
import functools
import math

import jax
import jax.numpy as jnp
from jax import lax
from jax.experimental import pallas as pl
from jax.experimental.pallas import tpu as pltpu

F32 = jnp.float32
BF16 = jnp.bfloat16

D_MODEL = 1024
HG_HEADS = 4
HG_DK = 128
HG_W = HG_HEADS * HG_DK
HG_CHUNK = 64
HG_SUB = 8
HG_FWD_CHUNKS_PER_STEP = 8
HG_CHUNKS_PER_STEP = 4
ATT_HEADS = 8
ATT_KV = 2
ATT_GROUP = ATT_HEADS // ATT_KV
ATT_HD = 64
ATT_BLOCK = 128
ATT_Q_W = ATT_HEADS * ATT_HD
ATT_KV_W = ATT_KV * ATT_HD
ATT_COLS = ATT_Q_W + 2 * ATT_KV_W
IN_COLS = 4 * HG_W + ATT_COLS
D_FF = 2816
EPS = 1e-6
ADAM_LR, ADAM_B1, ADAM_B2, ADAM_EPS, ADAM_WD, ADAM_STEP = 0.001, 0.9, 0.999, 1e-08, 0.01, 10
NEG = -1e30

V7X_VMEM_BYTES = 64 * 1024 * 1024
VMEM_LIMIT = 48 * 1024 * 1024
SUBLANES = 8

N_CHIPS = 4


def _cp(sem=None, **kw):
    return pltpu.CompilerParams(dimension_semantics=sem, vmem_limit_bytes=VMEM_LIMIT, **kw)


def _sds(shape, dtype):
    return jax.ShapeDtypeStruct(shape, dtype)


TOKEN = jax.ShapeDtypeStruct((8, 128), jnp.float32)


def _wspec(w):
    arr, rows, blk = w
    return pl.BlockSpec((rows, arr.shape[1]), lambda i: (blk, 0))


def _mm_nt(a, w, *, splits, out_dtype, name, after=None, tm=512):
    M, K = a.shape
    N = w[1]
    tm = min(tm, M)
    assert sum(splits) == N and M % tm == 0
    offs = [sum(splits[:i]) for i in range(len(splits))]
    n_in = 2 if after is None else 3

    def body(*refs):
        a_ref, w_ref = refs[0], refs[1]
        acc = lax.dot_general(a_ref[...], w_ref[...], (((1,), (1,)), ((), ())), preferred_element_type=F32)
        for o_ref, c0, n in zip(refs[n_in:], offs, splits):
            o_ref[...] = acc[:, c0:c0 + n].astype(out_dtype)

    in_specs = [pl.BlockSpec((tm, K), lambda i: (i, 0)), _wspec(w)]
    args = [a, w[0]]
    if after is not None:
        in_specs.append(pl.BlockSpec(memory_space=pl.ANY))
        args.append(after)
    outs = pl.pallas_call(
        body, name=name, grid=(M // tm,), in_specs=in_specs,
        out_specs=[pl.BlockSpec((tm, n), lambda i: (i, 0)) for n in splits],
        out_shape=[_sds((M, n), out_dtype) for n in splits],
        compiler_params=_cp(("parallel",)),
    )(*args)
    return outs


def _mm_nn(pieces, ws, *, name, out_dtype=F32, residual=None, epilogue=None, prologue=None, after=None,
           w_transposed=False, tm=512):
    pro_fn, pro_rows, pro_bc, pro_out = prologue or (None, [], [], None)
    if prologue is not None:
        assert pieces is None and len(ws) == 1
        pieces = [[pro_out]]
    M = pieces[0][0].shape[0]
    K = ws[0][1] if w_transposed else ws[0][0].shape[1]
    tm = min(tm, M)
    flat = [] if prologue is not None else [p for grp in pieces for p in grp]
    n_p = len(flat)
    n_w = len(ws)
    n_pr, n_pb = len(pro_rows), len(pro_bc)
    fn, row_ins, bc_ins, row_outs, acc_outs = epilogue or (None, [], [], [_sds((M, K), out_dtype)], [])
    if residual is not None:
        assert epilogue is None
        row_ins = [residual]
    n_r, n_b, n_o = len(row_ins), len(bc_ins), len(row_outs)
    lead = [] if after is None else [after]

    def body(*refs):
        refs = refs[len(lead):]
        p_refs = refs[:n_p]
        w_refs = refs[n_p:n_p + n_w]
        extra = [r[...] for r in refs[n_p + n_w:n_p + n_w + n_r + n_b]]
        base = n_p + n_w + n_r + n_b
        pro = [r[...] for r in refs[base:base + n_pr + n_pb]]
        base += n_pr + n_pb
        o_refs = refs[base:base + n_o]
        a_refs = refs[base + n_o:base + n_o + len(acc_outs)]
        if pro_fn is not None:
            lhs = pro_fn(*pro).astype(pro_out.dtype)
            refs[-1][...] = lhs
            tiles = [lhs]
        else:
            tiles = [r[...] for r in p_refs]
        acc = None
        k = 0
        for gi, grp in enumerate(pieces):
            c0 = 0
            for p in grp:
                n = p.shape[1]
                if w_transposed:
                    t = lax.dot_general(tiles[k], w_refs[gi][...], (((1,), (1,)), ((), ())), preferred_element_type=F32)
                else:
                    t = jnp.dot(tiles[k], w_refs[gi][c0:c0 + n, :], preferred_element_type=F32)
                acc = t if acc is None else acc + t
                c0 += n
                k += 1
        if fn is None:
            res = (acc + extra[0] if residual is not None else acc,)
        else:
            res = fn(acc, *extra)
        for o_ref, val in zip(o_refs, res[:n_o]):
            o_ref[...] = val.astype(o_ref.dtype)
        if acc_outs:
            @pl.when(pl.program_id(0) == 0)
            def _():
                for a_ref in a_refs:
                    a_ref[...] = jnp.zeros_like(a_ref)
            for a_ref, val in zip(a_refs, res[n_o:]):
                a_ref[...] += val

    in_specs = [pl.BlockSpec((tm, p.shape[1]), lambda i: (i, 0)) for p in flat]
    in_specs += [_wspec(w) for w in ws]
    in_specs += [pl.BlockSpec((tm, r.shape[1]), lambda i: (i, 0)) for r in row_ins]
    in_specs += [pl.BlockSpec(b.shape, lambda i: (0, 0)) for b in bc_ins]
    in_specs += [pl.BlockSpec((tm, r.shape[1]), lambda i: (i, 0)) for r in pro_rows]
    in_specs += [pl.BlockSpec(b.shape, lambda i: (0, 0)) for b in pro_bc]
    out_specs = [pl.BlockSpec((tm, s.shape[1]), lambda i: (i, 0)) for s in row_outs]
    out_specs += [pl.BlockSpec(s.shape, lambda i: (0, 0)) for s in acc_outs]
    pro_outs = [] if prologue is None else [pro_out]
    out_specs += [pl.BlockSpec((tm, s.shape[1]), lambda i: (i, 0)) for s in pro_outs]
    outs = pl.pallas_call(
        body, name=name, grid=(M // tm,), in_specs=[pl.BlockSpec(memory_space=pl.ANY)] * len(lead) + in_specs,
        out_specs=out_specs, out_shape=list(row_outs) + list(acc_outs) + pro_outs,
        compiler_params=_cp(("arbitrary",) if acc_outs else ("parallel",)),
    )(*lead, *flat, *[w[0] for w in ws], *row_ins, *bc_ins, *pro_rows, *pro_bc)
    return outs if (epilogue is not None or prologue is not None) else outs[0]


def _mm_tn(pieces, x, *, name, out_dtype=BF16, tt=1024, after=None):
    M, K = x.shape
    tt = min(tt, M)
    ns = [p.shape[1] for p in pieces]
    offs = [sum(ns[:i]) for i in range(len(ns))]
    N = sum(ns)
    n_p = len(pieces)
    last = M // tt - 1
    lead = [] if after is None else [after]

    def body(*refs):
        refs = refs[len(lead):]
        p_refs = refs[:n_p]
        x_ref = refs[n_p]
        o_ref, acc_ref = refs[n_p + 1], refs[n_p + 2]

        @pl.when(pl.program_id(0) == 0)
        def _():
            acc_ref[...] = jnp.zeros_like(acc_ref)

        xv = x_ref[...]
        for p_ref, c0, n in zip(p_refs, offs, ns):
            acc_ref[c0:c0 + n, :] += lax.dot_general(p_ref[...], xv, (((0,), (0,)), ((), ())),
                                                      preferred_element_type=F32)

        @pl.when(pl.program_id(0) == last)
        def _():
            o_ref[...] = acc_ref[...].astype(o_ref.dtype)

    in_specs = [pl.BlockSpec(memory_space=pl.ANY)] * len(lead) + [pl.BlockSpec((tt, n), lambda i: (i, 0)) for n in ns]
    in_specs.append(pl.BlockSpec((tt, K), lambda i: (i, 0)))
    return pl.pallas_call(
        body, name=name, grid=(M // tt,), in_specs=in_specs,
        out_specs=pl.BlockSpec((N, K), lambda i: (0, 0)),
        out_shape=_sds((N, K), out_dtype),
        scratch_shapes=[pltpu.VMEM((N, K), F32)],
        compiler_params=_cp(("arbitrary",)),
    )(*lead, *pieces, x)


def _rms_fwd(xf, w):
    inv = lax.rsqrt(jnp.mean(xf * xf, axis=-1, keepdims=True) + EPS)
    return xf * inv * w


def _rms_bwd(xf, w, dy):
    inv = lax.rsqrt(jnp.mean(xf * xf, axis=-1, keepdims=True) + EPS)
    xhat = xf * inv
    dxhat = dy * w
    dx = inv * (dxhat - xhat * jnp.mean(dxhat * xhat, axis=-1, keepdims=True))
    dw = jnp.sum(dy * xhat, axis=0, keepdims=True)
    return dx, dw


def _sigmoid(x):
    return 1.0 / (1.0 + jnp.exp(-x))


def _rowwise(fn, row_ins, bc_ins, row_outs, acc_outs, *, name, tm=256, after=None):
    M = row_outs[0].shape[0] if row_outs else row_ins[0][0].shape[0]
    assert M % tm == 0 and tm % SUBLANES == 0, (name, M, tm)
    n_r, n_b, n_o, n_a = len(row_ins), len(bc_ins), len(row_outs), len(acc_outs)
    n_after = 0 if after is None else 1

    def body(*refs):
        refs = refs[n_after:]
        ins = [r[...] for r in refs[:n_r + n_b]]
        o_refs = refs[n_r + n_b:n_r + n_b + n_o]
        a_refs = refs[n_r + n_b + n_o:]
        res = fn(*ins)
        for o_ref, val in zip(o_refs, res[:n_o]):
            o_ref[...] = val.astype(o_ref.dtype)
        if n_a:
            @pl.when(pl.program_id(0) == 0)
            def _():
                for a_ref in a_refs:
                    a_ref[...] = jnp.zeros_like(a_ref)
            for a_ref, val in zip(a_refs, res[n_o:]):
                a_ref[...] += val

    in_specs = [pl.BlockSpec((tm, cw), functools.partial(lambda i, cb, r0: (i + r0, cb), cb=cb, r0=r0))
                for (_, cw, cb, r0) in row_ins]
    in_specs += [pl.BlockSpec(b.shape, lambda i: (0, 0)) for b in bc_ins]
    out_specs = [pl.BlockSpec((tm, s.shape[1]), lambda i: (i, 0)) for s in row_outs]
    out_specs += [pl.BlockSpec(s.shape, lambda i: (0, 0)) for s in acc_outs]
    if n_after:
        in_specs = [pl.BlockSpec(memory_space=pl.ANY)] + in_specs
    return pl.pallas_call(
        body, name=name, grid=(M // tm,), in_specs=in_specs, out_specs=out_specs,
        out_shape=list(row_outs) + list(acc_outs),
        compiler_params=_cp(("arbitrary",) if n_a else ("parallel",)),
    )(*([after] if n_after else []), *[r[0] for r in row_ins], *bc_ins)


def _full(a, first_row_block=0):
    return (a, a.shape[1], 0, first_row_block)


def _conv_rows(ext, w_ref_val):
    s1 = pltpu.roll(ext, 1, 0)
    s2 = pltpu.roll(ext, 2, 0)
    y = w_ref_val[0:1, :] * s2 + w_ref_val[1:2, :] * s1 + w_ref_val[2:3, :] * ext
    return y[SUBLANES:, :]


def _ffn_in(v, w_gate, w_up, conv_w8, conv_b, *, name, tm=256):
    T, K = v.shape
    N = w_gate[1]
    tm = min(tm, T)

    def body(v_ref, wg_ref, wu_ref, cw_ref, cb_ref, gp_ref, up_ref, gate_ref, act_ref, carry_sc):
        @pl.when(pl.program_id(0) == 0)
        def _():
            carry_sc[...] = jnp.zeros_like(carry_sc)

        vv = v_ref[...]
        dn = (((1,), (1,)), ((), ()))
        gp = lax.dot_general(vv, wg_ref[...], dn, preferred_element_type=F32)
        up = lax.dot_general(vv, wu_ref[...], dn, preferred_element_type=F32)
        gp_ref[...] = gp.astype(gp_ref.dtype)
        up_ref[...] = up.astype(up_ref.dtype)
        gate = _conv_rows(jnp.concatenate([carry_sc[...], gp], axis=0), cw_ref[...]) + cb_ref[...]
        gate_ref[...] = gate
        act_ref[...] = (gate * _sigmoid(gate) * up).astype(act_ref.dtype)
        carry_sc[...] = gp[tm - SUBLANES:, :]

    tile = pl.BlockSpec((tm, N), lambda i: (i, 0))
    return pl.pallas_call(
        body, name=name, grid=(T // tm,),
        in_specs=[pl.BlockSpec((tm, K), lambda i: (i, 0)), _wspec(w_gate), _wspec(w_up),
                  pl.BlockSpec((SUBLANES, N), lambda i: (0, 0)), pl.BlockSpec((1, N), lambda i: (0, 0))],
        out_specs=[tile] * 4,
        out_shape=[_sds((T, N), BF16), _sds((T, N), BF16), _sds((T, N), F32), _sds((T, N), BF16)],
        scratch_shapes=[pltpu.VMEM((SUBLANES, N), F32)],
        compiler_params=_cp(("arbitrary",)),
    )(v, w_gate[0], w_up[0], conv_w8, conv_b)


def _ffn_back(dh2, w_down, gp, up, gate, conv_w8, *, name, tr=512, tc=1408):
    T, C = gp.shape
    K = dh2.shape[1]
    warr, _, wblk = w_down
    tr = min(tr, T)
    nr = T // tr
    ncb = C // tc

    def body(dh_ref, wd_ref, gp_ref, up_ref, gate_ref, w_ref, dgp_ref, dup_ref, dw_ref, db_ref, carry_sc):
        @pl.when(pl.program_id(1) == 0)
        def _():
            carry_sc[...] = jnp.zeros_like(carry_sc)
            dw_ref[...] = jnp.zeros_like(dw_ref)
            db_ref[...] = jnp.zeros_like(db_ref)

        w = w_ref[...]
        dact = lax.dot_general(dh_ref[...], wd_ref[...], (((1,), (1,)), ((), ())), preferred_element_type=F32)
        gpc = gp_ref[...].astype(F32)
        gate = gate_ref[...]
        sg = _sigmoid(gate)
        silu = gate * sg
        dup_ref[...] = (dact * silu).astype(dup_ref.dtype)
        dgate = dact * up_ref[...].astype(F32) * (sg + silu * (1.0 - sg))
        ext = jnp.concatenate([dgate, carry_sc[...]], axis=0)
        n = tr + SUBLANES
        g1 = pltpu.roll(ext, n - 1, 0)[:tr]
        g2 = pltpu.roll(ext, n - 2, 0)[:tr]
        dgp_ref[...] = (w[2:3, :] * dgate + w[1:2, :] * g1 + w[0:1, :] * g2).astype(dgp_ref.dtype)
        dw0 = jnp.sum(gpc * g2, axis=0, keepdims=True)
        dw1 = jnp.sum(gpc * g1, axis=0, keepdims=True)
        dw2 = jnp.sum(gpc * dgate, axis=0, keepdims=True)
        z = jnp.zeros((SUBLANES - 3, gpc.shape[1]), F32)
        dw_ref[...] += jnp.concatenate([dw0, dw1, dw2, z], axis=0)
        db_ref[...] += jnp.sum(dgate, axis=0, keepdims=True)
        carry_sc[...] = dgate[:SUBLANES]

    rev = lambda i: nr - 1 - i
    cur = pl.BlockSpec((tr, tc), lambda j, i: (rev(i), j))
    return pl.pallas_call(
        body, name=name, grid=(ncb, nr),
        in_specs=[pl.BlockSpec((tr, K), lambda j, i: (rev(i), 0)),
                  pl.BlockSpec((tc, K), lambda j, i: (wblk * ncb + j, 0)),
                  cur, cur, cur,
                  pl.BlockSpec((SUBLANES, tc), lambda j, i: (0, j))],
        out_specs=[cur, cur,
                   pl.BlockSpec((SUBLANES, tc), lambda j, i: (0, j)),
                   pl.BlockSpec((1, tc), lambda j, i: (0, j))],
        out_shape=[_sds((T, C), BF16), _sds((T, C), BF16), _sds((SUBLANES, C), F32), _sds((1, C), F32)],
        scratch_shapes=[pltpu.VMEM((SUBLANES, tc), F32)],
        compiler_params=_cp(("parallel", "arbitrary")),
    )(dh2, warr, gp, up, gate, conv_w8)


def _cumsum_rows(x):
    n = x.shape[0]
    row = lax.broadcasted_iota(jnp.int32, x.shape, 0)
    s = 1
    while s < n:
        x = x + jnp.where(row >= s, pltpu.roll(x, s, 0), 0.0)
        s *= 2
    return x


def _rcumsum_rows(x):
    n = x.shape[0]
    row = lax.broadcasted_iota(jnp.int32, x.shape, 0)
    s = 1
    while s < n:
        x = x + jnp.where(row < n - s, pltpu.roll(x, n - s, 0), 0.0)
        s *= 2
    return x


def _dot_nt(a, b):
    return lax.dot_general(a.astype(BF16), b.astype(BF16), (((1,), (1,)), ((), ())), preferred_element_type=F32)


def _dot_tn(a, b):
    return lax.dot_general(a.astype(BF16), b.astype(BF16), (((0,), (0,)), ((), ())), preferred_element_type=F32)


def _dot_nn(a, b):
    return jnp.dot(a.astype(BF16), b.astype(BF16), preferred_element_type=F32)


def _hg_gates(hq, hf, lbv):
    sig = _sigmoid(hf)
    f = lbv + (1.0 - lbv) * sig
    return sig, f, jnp.log(f), 1.0 - f, hq * (HG_DK ** -0.5)


def _hg_sel_rows(ref, sp):
    return jnp.concatenate(
        [jnp.broadcast_to(ref[pl.ds(HG_SUB * i + sp, 1), :], (HG_SUB, HG_DK)) for i in range(HG_CHUNK // HG_SUB)], axis=0)


def _hg_masks():
    C = HG_CHUNK
    row = lax.broadcasted_iota(jnp.int32, (C, C), 0)
    col = lax.broadcasted_iota(jnp.int32, (C, C), 1)
    d = col - (row // HG_SUB) * HG_SUB
    tmod = row % HG_SUB
    diag_valid = jnp.logical_and(d >= 0, d <= tmod)
    return row, col, d, diag_valid


def _hg_strip_keys(k, b, r, n):
    ek = jnp.exp(r - b[:n])
    return ek, jnp.concatenate([k[:n] * ek, jnp.zeros((HG_CHUNK - n, k.shape[1]), F32)], axis=0)


def _hg_scores(q, k, b, b_sc, k_sc):
    C, S = HG_CHUNK, HG_SUB
    row, col, d, diag_valid = _hg_masks()
    blocks = [jnp.zeros((S, C), F32)]
    for i in range(1, C // S):
        r = b_sc[pl.ds(S * i - 1, 1), :]
        qi = q[S * i:S * (i + 1)] * jnp.exp(b[S * i:S * (i + 1)] - r)
        blocks.append(_dot_nt(qi, _hg_strip_keys(k, b, r, S * i)[1]))
    a_off = jnp.concatenate(blocks, axis=0)
    a_d = jnp.zeros((C, C), F32)
    for sp in range(S):
        bs = _hg_sel_rows(b_sc, sp)
        ks = _hg_sel_rows(k_sc, sp)
        e = jnp.exp(jnp.minimum(b - bs, 0.0))
        colv = jnp.sum(q * ks * e, axis=-1, keepdims=True)
        a_d = jnp.where(d == sp, colv, a_d)
    return a_off + jnp.where(diag_valid, a_d, 0.0)


def _hg_prep(hq_v, hf_v, lbv, b_sc, k_sc):
    sig, f, g, k, q = _hg_gates(hq_v, hf_v, lbv)
    b = _cumsum_rows(g)
    b_sc[...] = b
    k_sc[...] = k
    return sig, f, k, q, b, b_sc[pl.ds(HG_CHUNK - 1, 1), :]


def _hgrn_fwd(hq, hf, hi, lb, *, name):
    T = hq.shape[0]
    C, H, K = HG_CHUNK, HG_HEADS, HG_DK
    NC = T // C

    def body(hq_ref, hf_ref, hi_ref, lb_ref, o_ref, st_ref, s_sc, b_sc, k_sc):
        @pl.when(pl.program_id(0) == 0)
        def _():
            s_sc[...] = jnp.zeros_like(s_sc)

        st_all = s_sc[...]
        for j in range(P):
            rows = slice(C * j, C * (j + 1))
            st_ref[j] = st_all
            outs, news = [], []
            for h in range(H):
                sl = slice(K * h, K * (h + 1))
                _, _, k, q, b, bc = _hg_prep(hq_ref[rows, sl], hf_ref[rows, sl], lb_ref[:, sl], b_sc.at[j, h], k_sc.at[j, h])
                v = hi_ref[rows, sl]
                st0 = st_all[:, sl]
                a = _hg_scores(q, k, b, b_sc.at[j, h], k_sc.at[j, h])
                outs.append(_dot_nn(a, v) + _dot_nt(q * jnp.exp(b), st0))
                news.append(st0 * jnp.exp(bc) + _dot_tn(v, k * jnp.exp(bc - b)))
            o_ref[rows, :] = jnp.concatenate(outs, axis=1)
            st_all = jnp.concatenate(news, axis=1)
        s_sc[...] = st_all

    P = HG_FWD_CHUNKS_PER_STEP
    blk = pl.BlockSpec((P * C, H * K), lambda c: (c, 0))
    return pl.pallas_call(
        body, name=name, grid=(NC // P,),
        in_specs=[blk, blk, blk, pl.BlockSpec((1, H * K), lambda c: (0, 0))],
        out_specs=[blk, pl.BlockSpec((P, K, H * K), lambda c: (c, 0, 0))],
        out_shape=[_sds((T, H * K), F32), _sds((NC, K, H * K), F32)],
        scratch_shapes=[pltpu.VMEM((K, H * K), F32), pltpu.VMEM((P, H, C, K), F32), pltpu.VMEM((P, H, C, K), F32)],
        compiler_params=_cp(("arbitrary",)),
    )(hq, hf, hi, lb)


def _hgrn_bwd(hq, hf, hi, lb, states, do, *, name):
    T = hq.shape[0]
    C, H, K, S = HG_CHUNK, HG_HEADS, HG_DK, HG_SUB
    NC = T // C

    def intra_slow(q, k, b, da, b_sc, k_sc):
        row, col, d, diag_valid = _hg_masks()
        a_blocks = [jnp.zeros((S, C), F32)]
        dq_blocks = [jnp.zeros((S, K), F32)]
        dk = jnp.zeros((C, K), F32)
        for i in range(1, C // S):
            r = b_sc[pl.ds(S * i - 1, 1), :]
            eq = jnp.exp(b[S * i:S * (i + 1)] - r)
            ek = jnp.exp(jnp.minimum(r - b, 0.0))
            qi = q[S * i:S * (i + 1)] * eq
            kk = k * ek
            a_blocks.append(_dot_nt(qi, kk))
            dai = jnp.where(col[S * i:S * (i + 1)] < S * i, da[S * i:S * (i + 1)], 0.0)
            dq_blocks.append(_dot_nn(dai, kk) * eq)
            dk = dk + _dot_tn(dai, qi) * ek
        dq = jnp.concatenate(dq_blocks, axis=0)
        a_off = jnp.where(col < (row // S) * S, jnp.concatenate(a_blocks, axis=0), 0.0)
        same_blk = (row // S == col // S).astype(BF16)
        tmod = (lax.broadcasted_iota(jnp.int32, (C, K), 0)) % S
        a_d = jnp.zeros((C, C), F32)
        dk_d = jnp.zeros((C, K), F32)
        for sp in range(S):
            bs = _hg_sel_rows(b_sc, sp)
            ks = _hg_sel_rows(k_sc, sp)
            e = jnp.exp(jnp.minimum(b - bs, 0.0))
            eks = e * ks
            a_d = jnp.where(d == sp, jnp.sum(q * eks, axis=-1, keepdims=True), a_d)
            dacol = jnp.sum(jnp.where(d == sp, da, 0.0), axis=-1, keepdims=True)
            dq = dq + dacol * eks
            wq = dacol * e * q
            wq_hi = wq.astype(BF16)
            wq_lo = (wq - wq_hi.astype(F32)).astype(BF16)
            blk_sum = (jnp.dot(same_blk, wq_hi, preferred_element_type=F32)
                       + jnp.dot(same_blk, wq_lo, preferred_element_type=F32))
            dk_d = jnp.where(tmod == sp, blk_sum, dk_d)
        return a_off + jnp.where(diag_valid, a_d, 0.0), dq, dk + dk_d

    def one_head(pre, v, lbv, st0, dst1, dout, b_sc, k_sc):
        sig, f, k, q, b, bc = pre
        ebc = jnp.exp(bc)
        eb = jnp.exp(b)
        ekb = jnp.exp(bc - b)
        qt = q * eb
        kb = k * ekb
        row = lax.broadcasted_iota(jnp.int32, (C, C), 0)
        col = lax.broadcasted_iota(jnp.int32, (C, C), 1)
        da = jnp.where(col <= row, _dot_nt(dout, v), 0.0)
        dkb = _dot_nn(v, dst1)
        new_ds = _dot_tn(dout, qt) + dst1 * ebc
        a, dq_i, dk_i = intra_slow(q, k, b, da, b_sc, k_sc)
        dq = _dot_nn(dout, st0) * eb + dq_i
        dk = dkb * ekb + dk_i
        dv = _dot_tn(a, dout) + _dot_nt(kb, dst1)
        extra = jnp.sum(dkb * kb, axis=0, keepdims=True) + ebc * jnp.sum(st0 * dst1, axis=0, keepdims=True)
        rowk = lax.broadcasted_iota(jnp.int32, (C, K), 0)
        db = q * dq - k * dk + jnp.where(rowk == C - 1, extra, 0.0)
        dg = _rcumsum_rows(db)
        df = dg / f - dk
        return (dq * (K ** -0.5), df * (1.0 - lbv) * sig * (1.0 - sig), dv,
                jnp.sum(df * (1.0 - sig), axis=0, keepdims=True), new_ds)

    def body(hq_ref, hf_ref, hi_ref, lb_ref, st_ref, do_ref, dq_ref, dhf_ref, dv_ref, dlb_ref, ds_sc, b_sc, k_sc):
        @pl.when(pl.program_id(0) == 0)
        def _():
            ds_sc[...] = jnp.zeros_like(ds_sc)
            dlb_ref[...] = jnp.zeros_like(dlb_ref)

        ds_all = ds_sc[...]
        dlb = jnp.zeros((1, H * K), F32)
        for j in reversed(range(P)):
            rows = slice(C * j, C * (j + 1))
            st_all = st_ref[j]
            res = []
            for h in range(H):
                sl = slice(K * h, K * (h + 1))
                pre = _hg_prep(hq_ref[rows, sl], hf_ref[rows, sl], lb_ref[:, sl], b_sc.at[j, h], k_sc.at[j, h])
                res.append(one_head(pre, hi_ref[rows, sl], lb_ref[:, sl], st_all[:, sl], ds_all[:, sl], do_ref[rows, sl],
                                    b_sc.at[j, h], k_sc.at[j, h]))
            cat = lambda i: jnp.concatenate([r[i] for r in res], axis=1)
            dq_ref[rows, :] = cat(0).astype(dq_ref.dtype)
            dhf_ref[rows, :] = cat(1).astype(dhf_ref.dtype)
            dv_ref[rows, :] = cat(2).astype(dv_ref.dtype)
            dlb = dlb + cat(3)
            ds_all = cat(4)
        dlb_ref[...] += dlb
        ds_sc[...] = ds_all

    P = HG_CHUNKS_PER_STEP
    NS = NC // P
    blk = pl.BlockSpec((P * C, H * K), lambda c: (NS - 1 - c, 0))
    par = pl.BlockSpec((1, H * K), lambda c: (0, 0))
    return pl.pallas_call(
        body, name=name, grid=(NS,),
        in_specs=[blk, blk, blk, par, pl.BlockSpec((P, K, H * K), lambda c: (NS - 1 - c, 0, 0)), blk],
        out_specs=[blk, blk, blk, par],
        out_shape=[_sds((T, H * K), BF16)] * 3 + [_sds((1, H * K), F32)],
        scratch_shapes=[pltpu.VMEM((K, H * K), F32), pltpu.VMEM((P, H, C, K), F32), pltpu.VMEM((P, H, C, K), F32)],
        compiler_params=_cp(("arbitrary",)),
    )(hq, hf, hi, lb, states, do)


ATT_STACK = ATT_GROUP
ATT_FWD_QROWS = ATT_BLOCK // 2


def _att_valid(n, a=0, qrows=ATT_BLOCK):
    R, B = ATT_STACK * qrows, ATT_BLOCK
    j = lax.broadcasted_iota(jnp.int32, (B + qrows, R), 0)
    t = lax.broadcasted_iota(jnp.int32, (B + qrows, R), 1) % qrows
    dist = t + B - j
    first_key = jnp.where(n > 0, 0, B)
    return jnp.logical_and(jnp.logical_and(dist >= 0, dist < B), j + qrows * a >= first_key)


def _att_rows(x, a, qrows):
    return jnp.concatenate([x[ATT_BLOCK * g + qrows * a:ATT_BLOCK * g + qrows * (a + 1)] for g in range(ATT_STACK)], axis=0)


def _att_load(cur_ref, prev_ref, ba_ref, h0):
    hd = ATT_HD
    kv = h0 // ATT_GROUP
    def cols(ref, c0):
        return ref[:, c0:c0 + hd] + ba_ref[:, c0:c0 + hd]
    qs = jnp.concatenate([cols(cur_ref, hd * (h0 + g)) for g in range(ATT_STACK)], axis=0)
    kc = jnp.concatenate([cols(prev_ref, ATT_Q_W + hd * kv), cols(cur_ref, ATT_Q_W + hd * kv)], axis=0)
    vc = jnp.concatenate([cols(prev_ref, ATT_Q_W + ATT_KV_W + hd * kv), cols(cur_ref, ATT_Q_W + ATT_KV_W + hd * kv)], axis=0)
    return qs, kc, vc


def _att_probs(qs, kc, valid, sink_ref, h0):
    scale = 1.0 / math.sqrt(ATT_HD)
    s = jnp.where(valid, _dot_nt(kc, qs) * scale, NEG)
    nq = qs.shape[0] // ATT_STACK
    sink = jnp.concatenate([jnp.full((1, nq), sink_ref[0, h0 + g], F32) for g in range(ATT_STACK)], axis=1)
    m = jnp.maximum(jnp.max(s, axis=0, keepdims=True), sink)
    p = jnp.exp(s - m)
    ps = jnp.exp(sink - m)
    inv = 1.0 / (jnp.sum(p, axis=0, keepdims=True) + ps)
    return p * inv, ps * inv


def _attn_fwd(att, b_attn, sinks, *, name, after=None):
    T = att.shape[0]
    B = ATT_BLOCK
    NB = T // B
    lead = [] if after is None else [after]

    def body(*refs):
        sink_ref, cur_ref, prev_ref, ba_ref, o_ref = refs[len(lead):]
        Q = ATT_FWD_QROWS
        parts = range(B // Q)
        valid = [_att_valid(pl.program_id(0), a, Q) for a in parts]
        outs = [[None] * len(parts) for _ in range(ATT_HEADS)]
        for h0 in range(0, ATT_HEADS, ATT_STACK):
            qs, kc, vc = _att_load(cur_ref, prev_ref, ba_ref, h0)
            for a in parts:
                keys = slice(Q * a, Q * a + B + Q)
                prob, _ = _att_probs(_att_rows(qs, a, Q), kc[keys], valid[a], sink_ref, h0)
                o = _dot_tn(prob, vc[keys])
                for g in range(ATT_STACK):
                    outs[h0 + g][a] = o[Q * g:Q * (g + 1)]
        o_ref[...] = jnp.concatenate([jnp.concatenate(p, axis=0) for p in outs], axis=1)

    return pl.pallas_call(
        body, name=name, grid=(NB,),
        in_specs=[pl.BlockSpec(memory_space=pl.ANY)] * len(lead) + [
            pl.BlockSpec(memory_space=pltpu.SMEM),
            pl.BlockSpec((B, ATT_COLS), lambda n: (n, 0)),
            pl.BlockSpec((B, ATT_COLS), lambda n: (jnp.maximum(n - 1, 0), 0)),
            pl.BlockSpec((1, ATT_COLS), lambda n: (0, 0))],
        out_specs=pl.BlockSpec((B, ATT_Q_W), lambda n: (n, 0)),
        out_shape=_sds((T, ATT_Q_W), F32),
        compiler_params=_cp(("parallel",)),
    )(*lead, sinks, att, att, b_attn)


def _attn_bwd(att, b_attn, sinks, dmix, *, name):
    T = att.shape[0]
    B, hd = ATT_BLOCK, ATT_HD
    NB = T // B
    scale = 1.0 / math.sqrt(hd)

    def body(sink_ref, cur_ref, prev_ref, ba_ref, do_ref, daq_ref, dakv_ref, dsink_ref, dbq_ref, dbkv_ref, carry_sc):
        n = pl.program_id(0)

        @pl.when(n == 0)
        def _():
            carry_sc[...] = jnp.zeros_like(carry_sc)
            dsink_ref[...] = jnp.zeros_like(dsink_ref)
            dbq_ref[...] = jnp.zeros_like(dbq_ref)
            dbkv_ref[...] = jnp.zeros_like(dbkv_ref)

        @pl.when(n < NB)
        def _():
            valid = _att_valid(n)
            hrow = lax.broadcasted_iota(jnp.int32, (SUBLANES, 128), 0)
            dsink = jnp.zeros((SUBLANES, 128), F32)
            dqs = []
            dks = [jnp.zeros((2 * B, hd), F32)] * ATT_KV
            dvs = [jnp.zeros((2 * B, hd), F32)] * ATT_KV
            for h0 in range(0, ATT_HEADS, ATT_STACK):
                kv = h0 // ATT_GROUP
                qs, kc, vc = _att_load(cur_ref, prev_ref, ba_ref, h0)
                prob, psink = _att_probs(qs, kc, valid, sink_ref, h0)
                dout = jnp.concatenate([do_ref[:, hd * (h0 + g):hd * (h0 + g + 1)] for g in range(ATT_STACK)], axis=0)
                dp = _dot_nt(vc, dout)
                delta = jnp.sum(prob * dp, axis=0, keepdims=True)
                dsc = prob * (dp - delta) * scale
                dq = _dot_tn(dsc, kc)
                dks[kv] = dks[kv] + _dot_nn(dsc, qs)
                dvs[kv] = dvs[kv] + _dot_nn(prob, dout)
                dsk = psink * delta
                for g in range(ATT_STACK):
                    dqs.append(dq[B * g:B * (g + 1)])
                    tot = jnp.sum(dsk[:, B * g:B * (g + 1)], axis=1, keepdims=True)
                    dsink = dsink - jnp.where(hrow == h0 + g, tot, 0.0)
            daq = jnp.concatenate(dqs, axis=1).astype(daq_ref.dtype)
            daq_ref[...] = daq
            dsink_ref[...] += dsink
            dbq_ref[...] += jnp.sum(daq.astype(F32), axis=0, keepdims=True)
            done = carry_sc[...] + jnp.concatenate([d[:B] for d in dks + dvs], axis=1)
            dakv_ref[...] = done.astype(dakv_ref.dtype)
            dbkv_ref[...] += jnp.sum(done.astype(dakv_ref.dtype).astype(F32), axis=0, keepdims=True)
            carry_sc[...] = jnp.concatenate([d[B:] for d in dks + dvs], axis=1)

        @pl.when(n == NB)
        def _():
            done = carry_sc[...]
            dakv_ref[...] = done.astype(dakv_ref.dtype)
            dbkv_ref[...] += jnp.sum(done.astype(dakv_ref.dtype).astype(F32), axis=0, keepdims=True)

    cl = lambda n: jnp.minimum(n, NB - 1)
    return pl.pallas_call(
        body, name=name, grid=(NB + 1,),
        in_specs=[pl.BlockSpec(memory_space=pltpu.SMEM),
                  pl.BlockSpec((B, ATT_COLS), lambda n: (cl(n), 0)),
                  pl.BlockSpec((B, ATT_COLS), lambda n: (jnp.maximum(cl(n) - 1, 0), 0)),
                  pl.BlockSpec((1, ATT_COLS), lambda n: (0, 0)),
                  pl.BlockSpec((B, ATT_Q_W), lambda n: (cl(n), 0))],
        out_specs=[pl.BlockSpec((B, ATT_Q_W), lambda n: (cl(n), 0)),
                   pl.BlockSpec((B, 2 * ATT_KV_W), lambda n: (jnp.maximum(n - 1, 0), 0)),
                   pl.BlockSpec((SUBLANES, 128), lambda n: (0, 0)),
                   pl.BlockSpec((1, ATT_Q_W), lambda n: (0, 0)),
                   pl.BlockSpec((1, 2 * ATT_KV_W), lambda n: (0, 0))],
        out_shape=[_sds((T, ATT_Q_W), BF16), _sds((T, 2 * ATT_KV_W), BF16), _sds((SUBLANES, 128), F32),
                   _sds((1, ATT_Q_W), F32), _sds((1, 2 * ATT_KV_W), F32)],
        scratch_shapes=[pltpu.VMEM((B, 2 * ATT_KV_W), F32)],
        compiler_params=_cp(("arbitrary",)),
    )(sinks, att, att, b_attn, dmix)


def _silu_and_grad(x):
    sg = _sigmoid(x)
    return x * sg, sg * (1.0 + x * (1.0 - sg))


def _mix_fwd_fn(o_raw, hg, o_att, hgw):
    outs = []
    for h in range(HG_HEADS):
        sl = slice(HG_DK * h, HG_DK * (h + 1))
        silu, _ = _silu_and_grad(hg[:, sl])
        outs.append(_rms_fwd(o_raw[:, sl], hgw) * silu)
    outs.append(o_att)
    return (jnp.concatenate(outs, axis=1),)


def _mix_bwd_fn(o_raw, hg, dmix, hgw):
    dos, dhgs = [], []
    dw = jnp.zeros((1, HG_DK), F32)
    for h in range(HG_HEADS):
        sl = slice(HG_DK * h, HG_DK * (h + 1))
        silu, dsilu = _silu_and_grad(hg[:, sl])
        dy = dmix[:, sl]
        dhgs.append(dy * _rms_fwd(o_raw[:, sl], hgw) * dsilu)
        dx, dwh = _rms_bwd(o_raw[:, sl], hgw, dy * silu)
        dos.append(dx)
        dw = dw + dwh
    return jnp.concatenate(dos, axis=1), jnp.concatenate(dhgs, axis=1), dw


def _final_fn(h2, tgt, wf):
    d = h2.shape[1]
    err = _rms_fwd(h2, wf) - tgt
    loss_cols = (0.5 / d) * jnp.sum(err * err, axis=0, keepdims=True)
    dh2, dwf = _rms_bwd(h2, wf, err * (1.0 / d))
    return dh2, dh2, loss_cols, dwf


class _NoExchange:
    def __init__(self, weights):
        self.weights = weights

    def start(self):
        return None

    def w_in(self, after):
        return self.weights["w_in_t"]

    def mid(self, after):
        return None

    def w_out(self, after):
        return {k: self.weights[k] for k in ("w_out", "conv_w8")}

    def rest(self, after):
        return {k: self.weights[k] for k in ("w_gate_t", "w_up_t", "w_down")}

    def ffn_grads(self, gs):
        return None

    def ffn_grads_send(self, after):
        return None

    def late_grads(self, g_in):
        return None


def _local_step(x, tgt, p, ex):
    T, D = x.shape
    row = lambda n, dt: _sds((T, n), dt)
    acc = lambda n: _sds((1, n), F32)

    (u,) = _rowwise(lambda xv, w: (_rms_fwd(xv, w),), [_full(x)], [p["norm_mix_w"]], [row(D, BF16)], [], name="rms_mix",
                    after=ex.start())
    p = dict(p, w_in_t=ex.w_in(u))
    hq, hf, hi, hg, att = _mm_nt(u, p["w_in_t"], splits=[HG_W] * 4 + [ATT_COLS], out_dtype=F32, name="in_proj")
    o_raw, states = _hgrn_fwd(hq, hf, hi, p["lb"], name="hgrn_fwd")
    o_att = _attn_fwd(att, p["b_attn"], p["sinks"], name="attn_fwd", after=ex.mid(o_raw))
    p = dict(p, **ex.w_out(o_att))
    def out_epilogue(prod, xv, w):
        h1v = prod + xv
        return h1v, _rms_fwd(h1v, w)

    h1, v, mix = _mm_nn(None, [p["w_out"]], name="mix_out_proj",
                        prologue=(lambda *a: _mix_fwd_fn(*a)[0], [o_raw, hg, o_att], [p["hg_norm_w"]], row(D, BF16)),
                        epilogue=(out_epilogue, [x], [p["norm_ffn_w"]], [row(D, F32), row(D, BF16)], []))
    p = dict(p, **ex.rest(v))
    gp, up, gate, act = _ffn_in(v, p["w_gate_t"], p["w_up_t"], p["conv_w8"], p["conv_b"], name="ffn_in")
    def down_epilogue(prod, h1v, tgtv, wf):
        return _final_fn(prod + h1v, tgtv, wf)

    dh2, dh2_b, loss_cols, d_final = _mm_nn(
        [[act]], [p["w_down"]], name="down_proj_loss",
        epilogue=(down_epilogue, [h1, tgt], [p["final_norm_w"]], [row(D, F32), row(D, BF16)], [acc(D), acc(D)]))

    g_down = _mm_tn([act], dh2_b, name="g_down")
    dgp, dup, d_conv_w8, d_conv_b = _ffn_back(dh2_b, p["w_down"], gp, up, gate, p["conv_w8"], name="ffn_back")
    g_gate_t = _mm_tn([dgp], v, name="g_gate")
    g_up_t = _mm_tn([dup], v, name="g_up")
    swapping = ex.ffn_grads([g_gate_t, g_up_t, g_down])

    def ffn_norm_bwd(dvv, hv, dh2v, w):
        dx, dw = _rms_bwd(hv, w, dvv)
        dh1v = dx + dh2v
        return dh1v, dh1v, dw

    dh1, dh1_b, d_norm_ffn = _mm_nn(
        [[dgp], [dup]], [p["w_gate_t"], p["w_up_t"]], name="d_v_norm", after=swapping,
        epilogue=(ffn_norm_bwd, [h1, dh2], [p["norm_ffn_w"]], [row(D, F32), row(D, BF16)], [acc(D)]))
    sent = ex.ffn_grads_send(dh1_b)
    def mix_bwd(dmixv, o_rawv, hgv, hgw):
        do_rawv, dhgv, dw = _mix_bwd_fn(o_rawv, hgv, dmixv[:, :HG_W], hgw)
        return do_rawv, dhgv, dmixv[:, HG_W:], dw

    do_raw, dhg, do_att, d_hg_norm = _mm_nn(
        [[dh1_b]], [p["w_out"]], name="d_mix_bwd", w_transposed=True, after=sent,
        epilogue=(mix_bwd, [o_raw, hg], [p["hg_norm_w"]], [row(HG_W, F32), row(HG_W, BF16), row(ATT_Q_W, F32)], [acc(HG_DK)]))
    daq, dakv, d_sinks8, d_bq, d_bkv = _attn_bwd(att, p["b_attn"], p["sinks"], do_att, name="attn_bwd")
    dhq, dhf, dhi, d_lb = _hgrn_bwd(hq, hf, hi, p["lb"], states, do_raw, name="hgrn_bwd")
    pieces = [dhq, dhf, dhi, dhg, daq, dakv]
    g_in_t = _mm_tn(pieces, u, name="g_in")
    g_out = _mm_tn([mix], dh1_b, name="g_out", after=ex.late_grads(g_in_t))

    def mix_norm_bwd(duv, xv, dh1v, w):
        dx, dw = _rms_bwd(xv, w, duv)
        return dx + dh1v, dw

    dx, d_norm_mix = _mm_nn([pieces], [p["w_in_t"]], name="d_u_norm",
                            epilogue=(mix_norm_bwd, [x, dh1], [p["norm_mix_w"]], [row(D, F32)], [acc(D)]))
    grads = dict(g_in_t=g_in_t, g_out=g_out, g_gate_t=g_gate_t, g_up_t=g_up_t, g_down=g_down,
                 norm_mix_w=d_norm_mix, b_attn=jnp.concatenate([d_bq, d_bkv], axis=1), lb=d_lb, hg_norm_w=d_hg_norm,
                 sinks8=d_sinks8, norm_ffn_w=d_norm_ffn, conv_w8=d_conv_w8, conv_b=d_conv_b, final_norm_w=d_final)
    return loss_cols, dx, grads


SLAB = (IN_COLS // N_CHIPS, D_FF // N_CHIPS, D_FF // N_CHIPS, D_FF // N_CHIPS, D_MODEL // N_CHIPS)
N_W = len(SLAB)
PACK_OFF = tuple(sum(SLAB[:i]) for i in range(N_W))
PACK_ROWS = sum(SLAB)
FULL_OFF = tuple(N_CHIPS * o for o in PACK_OFF)
FULL_ROWS = N_CHIPS * PACK_ROWS
HALF = tuple(s // 2 for s in SLAB)
HPACK_OFF = tuple(sum(HALF[:i]) for i in range(N_W))
HPACK_ROWS = sum(HALF)
HFULL_OFF = tuple(N_CHIPS * o for o in HPACK_OFF)
HFULL_ROWS = N_CHIPS * HPACK_ROWS
CHIP_FLIPS = ((1, 0), (0, 1), (1, 1))
N_DEV = 8
BF16_ROWS = 16
ANY = pl.BlockSpec(memory_space=pl.ANY)


def _pos():
    return lax.axis_index("x"), lax.axis_index("y"), lax.axis_index("c")


def _flip(v, f):
    return 1 - v if f else v


def _rcopy(src, dst, ssem, rsem, dev):
    return pltpu.make_async_remote_copy(src_ref=src, dst_ref=dst, send_sem=ssem, recv_sem=rsem, device_id=dev,
                                        device_id_type=pl.DeviceIdType.MESH)


def _rows(ref, start, n, align=None):
    if not isinstance(start, int):
        if align is None:
            align = SUBLANES * (4 // jnp.dtype(ref.dtype).itemsize)
        start = pl.multiple_of(start, align)
    return ref.at[pl.ds(start, n), :]


FFN_W = (1, 2, 3)
N_PEER = 1 + len(CHIP_FLIPS)
HBM = pl.BlockSpec(memory_space=pltpu.HBM)
SEM = pl.BlockSpec(memory_space=pltpu.SEMAPHORE)
EFFECT = pltpu.SideEffectType.DATAFLOW_SIDE_EFFECTING
LANES = 128


def _sent_rows(k, w, c):
    return (0, SLAB[w]) if k == 0 else (c * HALF[w], HALF[w])


def _gather_start(pack, cw8):
    D = pack.shape[1]
    lands = [lax.empty((N_CHIPS * SLAB[0], D), pack.dtype), lax.empty((3 * N_CHIPS * SLAB[1], D), pack.dtype),
             lax.empty((N_CHIPS * SLAB[4], D), pack.dtype), lax.empty((N_CHIPS,) + cw8.shape, cw8.dtype)]
    bufs = [pack, cw8] + lands

    def body(pack_ref, cw_ref, l_in, l_ffn, l_out, l_cw, *rest):
        in_send, in_recv, out_send, out_recv, ffn_send, ffn_recv = rest[:6]
        token = rest[-1]
        x, y, c = _pos()
        q = 2 * x + y
        peers = _gather_peers(x, y, c)

        def send(k, peer, w, land, base, ssem, rsem):
            r0, n = _sent_rows(k, w, c)
            _rcopy(_rows(pack_ref, PACK_OFF[w] + r0, n), _rows(land, base + q * SLAB[w] + r0, n), ssem, rsem, peer).start()

        for k, peer in enumerate(peers):
            send(k, peer, 0, l_in, 0, in_send.at[k], in_recv.at[k])
        for k, peer in enumerate(peers):
            send(k, peer, 4, l_out, 0, out_send.at[k], out_recv.at[k])
            _rcopy(cw_ref, l_cw.at[q], out_send.at[N_PEER + k], out_recv.at[N_PEER + k], peer).start()
        for j, w in enumerate(FFN_W):
            for k, peer in enumerate(peers):
                send(k, peer, w, l_ffn, j * N_CHIPS * SLAB[w], ffn_send.at[k], ffn_recv.at[k])
        token[...] = jnp.zeros_like(token)

    n_sem = (N_PEER, N_PEER, 2 * N_PEER, 2 * N_PEER, N_PEER, N_PEER)
    outs = pl.pallas_call(
        body, name="gather_start", in_specs=[HBM] * len(bufs),
        out_specs=[SEM] * len(n_sem) + [HBM] * len(bufs) + [pl.BlockSpec(memory_space=pltpu.VMEM)],
        out_shape=[pltpu.SemaphoreType.DMA((n,)) for n in n_sem]
        + [pltpu.HBM(b.shape, b.dtype) for b in bufs] + [TOKEN],
        input_output_aliases={i: len(n_sem) + i for i in range(len(bufs))},
        compiler_params=pltpu.CompilerParams(has_side_effects=EFFECT),
    )(*[pltpu.with_memory_space_constraint(b, pltpu.HBM) for b in bufs])
    bufs_out = outs[len(n_sem):]
    return dict(in_sems=outs[0:2], out_sems=outs[2:4], ffn_sems=outs[4:6], pack=bufs_out[0], cw=bufs_out[1], l_in=bufs_out[2],
                l_ffn=bufs_out[3], l_out=bufs_out[4], l_cw=bufs_out[5], token=bufs_out[6])


def _gather_peers(x, y, c):
    return [(x, y, 1 - c)] + [(_flip(x, fx), _flip(y, fy), c) for fx, fy in CHIP_FLIPS]


def _gather_wait_in(g, after):
    def body(pack_ref, l_in, send, recv, after_ref, pack_out, l_out):
        for k, peer in enumerate(_gather_peers(*_pos())):
            n = _sent_rows(k, 0, 0)[1]
            cp = _rcopy(_rows(pack_ref, PACK_OFF[0], n), _rows(l_in, 0, n), send.at[k], recv.at[k], peer)
            cp.wait_send()
            cp.wait_recv()

    return pl.pallas_call(
        body, name="gather_wait_in", in_specs=[HBM, HBM, SEM, SEM, ANY], out_specs=[HBM, HBM],
        out_shape=[pltpu.HBM(g["pack"].shape, g["pack"].dtype), pltpu.HBM(g["l_in"].shape, g["l_in"].dtype)],
        input_output_aliases={0: 0, 1: 1}, compiler_params=pltpu.CompilerParams(has_side_effects=EFFECT),
    )(g["pack"], g["l_in"], *g["in_sems"], after)


def _gather_wait_out(g, pack, after):
    def body(pack_ref, cw_ref, l_out, l_cw, o_send, o_recv, after_ref, o_out, o_cw):
        for k, peer in enumerate(_gather_peers(*_pos())):
            n_out = _sent_rows(k, 4, 0)[1]
            for cp in (_rcopy(_rows(pack_ref, PACK_OFF[4], n_out), _rows(l_out, 0, n_out), o_send.at[k], o_recv.at[k], peer),
                       _rcopy(cw_ref, l_cw.at[0], o_send.at[N_PEER + k], o_recv.at[N_PEER + k], peer)):
                cp.wait_send()
                cp.wait_recv()

    ins = [pack, g["cw"], g["l_out"], g["l_cw"]]
    return pl.pallas_call(
        body, name="gather_wait_out", in_specs=[HBM] * 4 + [SEM] * 2 + [ANY], out_specs=[HBM] * 2,
        out_shape=[pltpu.HBM(b.shape, b.dtype) for b in ins[2:]],
        input_output_aliases={2: 0, 3: 1}, compiler_params=pltpu.CompilerParams(has_side_effects=EFFECT),
    )(*ins, *g["out_sems"], after)


def _gather_wait_ffn(g, pack, after):
    def body(pack_ref, l_ffn, f_send, f_recv, after_ref, o_ffn):
        for k, peer in enumerate(_gather_peers(*_pos())):
            n_ffn = len(FFN_W) * _sent_rows(k, FFN_W[0], 0)[1]
            cp = _rcopy(_rows(pack_ref, PACK_OFF[FFN_W[0]], n_ffn), _rows(l_ffn, 0, n_ffn), f_send.at[k], f_recv.at[k], peer)
            cp.wait_send()
            cp.wait_recv()

    return pl.pallas_call(
        body, name="gather_wait_ffn", in_specs=[HBM] * 2 + [SEM] * 2 + [ANY], out_specs=HBM,
        out_shape=pltpu.HBM(g["l_ffn"].shape, g["l_ffn"].dtype),
        input_output_aliases={1: 0}, compiler_params=pltpu.CompilerParams(has_side_effects=EFFECT),
    )(pack, g["l_ffn"], *g["ffn_sems"], after)


FWD_IN = ((0, 0, 0),)
FWD_OUT = ((0, 4, 0),)
FWD_FFN = tuple((0, w, j * N_CHIPS * SLAB[w]) for j, w in enumerate(FFN_W))


def _forward_copies(layout, src, dst, send_sems, recv_sems):
    x, y, c = _pos()
    sib = (x, y, 1 - c)
    cps = []
    for fx, fy in CHIP_FLIPS:
        qa = 2 * _flip(x, fx) + _flip(y, fy)
        for bi, w, base in layout:
            r0 = base + qa * SLAB[w] + c * HALF[w]
            cps.append(_rcopy(_rows(src[bi], r0, HALF[w]), _rows(dst[bi], r0, HALF[w]),
                              send_sems.at[len(cps)], recv_sems.at[len(cps)], sib))
    return cps


def _forward_in(l_in):
    n = len(CHIP_FLIPS) * len(FWD_IN)

    def body(in_ref, out_ref, send_sems, recv_sems):
        cps = _forward_copies(FWD_IN, [in_ref], [out_ref], send_sems, recv_sems)
        for cp in cps:
            cp.start()
        for cp in cps:
            cp.wait_recv()
        for cp in cps:
            cp.wait_send()

    return pl.pallas_call(
        body, name="forward_in", in_specs=[ANY], out_specs=ANY, out_shape=_sds(l_in.shape, l_in.dtype),
        input_output_aliases={0: 0},
        scratch_shapes=[pltpu.SemaphoreType.DMA((n,)), pltpu.SemaphoreType.DMA((n,))],
    )(l_in)


def _forward_start(layout, land, *, name):
    n = len(CHIP_FLIPS) * len(layout)

    def body(in_ref, send_sems, recv_sems, out_ref, token):
        for cp in _forward_copies(layout, [in_ref], [in_ref], send_sems, recv_sems):
            cp.start()
        token[...] = jnp.zeros_like(token)

    outs = pl.pallas_call(
        body, name=name, in_specs=[HBM],
        out_specs=[SEM, SEM, HBM, pl.BlockSpec(memory_space=pltpu.VMEM)],
        out_shape=[pltpu.SemaphoreType.DMA((n,)), pltpu.SemaphoreType.DMA((n,)), pltpu.HBM(land.shape, land.dtype), TOKEN],
        input_output_aliases={0: 2}, compiler_params=pltpu.CompilerParams(has_side_effects=EFFECT),
    )(pltpu.with_memory_space_constraint(land, pltpu.HBM))
    return dict(sems=outs[0:2], land=outs[2], token=outs[3])


def _forward_wait(layout, s, after, *, name):
    def body(in_ref, send_sems, recv_sems, after_ref, out_ref):
        for cp in _forward_copies(layout, [in_ref], [in_ref], send_sems, recv_sems):
            cp.wait_send()
            cp.wait_recv()

    return pl.pallas_call(
        body, name=name, in_specs=[HBM, SEM, SEM, ANY], out_specs=HBM,
        out_shape=pltpu.HBM(s["land"].shape, s["land"].dtype),
        input_output_aliases={0: 0}, compiler_params=pltpu.CompilerParams(has_side_effects=EFFECT),
    )(s["land"], *s["sems"], after)


def _exchange_halves(ws, gs, small, *, name):
    D = gs[0].shape[1]
    n = len(ws)
    has_small = small is not None

    def body(*refs):
        g = refs[:n]
        t = refs[n + has_small:2 * n + has_small]
        sems = refs[2 * n + 2 * has_small:]
        d2d_send, d2d_recv = sems[0], sems[1]
        x, y, c = _pos()
        sib = (x, y, 1 - c)
        drains = []
        for i, w in enumerate(ws):
            h = HALF[w]
            for qq in range(N_CHIPS):
                _rcopy(_rows(g[i], qq * SLAB[w] + (1 - c) * h, h), _rows(t[i], qq * h, h),
                       d2d_send.at[i], d2d_recv.at[i], sib).start()
            drains.append(_rcopy(t[i], t[i], d2d_send.at[i], d2d_recv.at[i], sib))
        if has_small:
            small_ref, sall_ref = refs[n], refs[2 * n + 1]
            sm_send, sm_recv, loc_sem = sems[2], sems[3], sems[4]
            me = 4 * x + 2 * y + c
            own_small = pltpu.make_async_copy(small_ref, sall_ref.at[me], loc_sem)
            own_small.start()
            for f in range(1, N_DEV):
                peer = (_flip(x, f & 4), _flip(y, f & 2), _flip(c, f & 1))
                cp = _rcopy(small_ref, sall_ref.at[me], sm_send.at[f - 1], sm_recv.at[f - 1], peer)
                cp.start()
                drains.append(cp)
        for d in drains:
            d.wait_recv()
        for d in drains:
            d.wait_send()
        if has_small:
            own_small.wait()

    out_shape = [_sds((N_CHIPS * HALF[w], D), gs[0].dtype) for w in ws]
    scratch = [pltpu.SemaphoreType.DMA((n,)), pltpu.SemaphoreType.DMA((n,))]
    if has_small:
        out_shape.append(_sds((N_DEV,) + small.shape, F32))
        scratch += [pltpu.SemaphoreType.DMA((N_DEV - 1,)), pltpu.SemaphoreType.DMA((N_DEV - 1,)), pltpu.SemaphoreType.DMA]
    return pl.pallas_call(
        body, name=name, in_specs=[ANY] * (n + has_small), out_specs=[ANY] * (n + has_small),
        out_shape=out_shape, scratch_shapes=scratch,
    )(*gs, *([small] if has_small else []))


def _halves_copies(ws, g, t, send_sems, recv_sems):
    x, y, c = _pos()
    sib = (x, y, 1 - c)
    cps = []
    for i, w in enumerate(ws):
        h = HALF[w]
        for qq in range(N_CHIPS):
            cps.append(_rcopy(_rows(g[i], qq * SLAB[w] + (1 - c) * h, h), _rows(t[i], qq * h, h),
                              send_sems.at[N_CHIPS * i + qq], recv_sems.at[N_CHIPS * i + qq], sib))
    return cps


def _halves_start(ws, gs, *, name):
    D = gs[0].shape[1]
    n = len(ws)
    bufs = list(gs) + [lax.empty((N_CHIPS * HALF[w], D), gs[0].dtype) for w in ws]

    def body(*refs):
        for cp in _halves_copies(ws, refs[:n], refs[n:2 * n], refs[2 * n], refs[2 * n + 1]):
            cp.start()
        refs[-1][...] = jnp.zeros_like(refs[-1])

    outs = pl.pallas_call(
        body, name=name, in_specs=[HBM] * (2 * n),
        out_specs=[SEM, SEM] + [HBM] * (2 * n) + [pl.BlockSpec(memory_space=pltpu.VMEM)],
        out_shape=[pltpu.SemaphoreType.DMA((N_CHIPS * n,)), pltpu.SemaphoreType.DMA((N_CHIPS * n,))]
        + [pltpu.HBM(b.shape, b.dtype) for b in bufs] + [TOKEN],
        input_output_aliases={i: 2 + i for i in range(2 * n)},
        compiler_params=pltpu.CompilerParams(has_side_effects=EFFECT),
    )(*[pltpu.with_memory_space_constraint(b, pltpu.HBM) for b in bufs])
    return dict(sems=outs[0:2], gs=outs[2:2 + n], theirs=outs[2 + n:2 + 2 * n], token=outs[-1])


def _halves_wait(ws, s, after, *, name):
    n = len(ws)

    def body(*refs):
        for cp in _halves_copies(ws, refs[:n], refs[n:2 * n], refs[2 * n], refs[2 * n + 1]):
            cp.wait_send()
            cp.wait_recv()

    bufs = list(s["gs"]) + list(s["theirs"])
    outs = pl.pallas_call(
        body, name=name, in_specs=[HBM] * (2 * n) + [SEM, SEM, ANY], out_specs=[HBM] * (2 * n),
        out_shape=[pltpu.HBM(b.shape, b.dtype) for b in bufs],
        input_output_aliases={i: i for i in range(2 * n)},
        compiler_params=pltpu.CompilerParams(has_side_effects=EFFECT),
    )(*bufs, *s["sems"], after)
    return outs[:n], outs[n:]


REDUCE_SPLIT = 2


def _chip_partial(ws, gs, theirs, *, name, out_dtype=F32):
    D = gs[0].shape[1]
    n = len(ws)

    def body(*refs):
        for i in range(n):
            refs[2 * n + i][...] = (refs[i][...].astype(F32) + refs[n + i][...].astype(F32)).astype(out_dtype)

    blk = [HALF[w] // REDUCE_SPLIT for w in ws]
    mine = [pl.BlockSpec((b, D), lambda qq, j: ((2 * qq + lax.axis_index("c")) * REDUCE_SPLIT + j, 0)) for b in blk]
    flat = [pl.BlockSpec((b, D), lambda qq, j: (qq * REDUCE_SPLIT + j, 0)) for b in blk]
    return pl.pallas_call(
        body, name=name, grid=(N_CHIPS, REDUCE_SPLIT), in_specs=mine + flat, out_specs=flat,
        out_shape=[_sds((N_CHIPS * HALF[w], D), out_dtype) for w in ws],
        compiler_params=_cp(("parallel", "parallel")),
    )(*gs, *theirs)


def _partial_copies(ws, part, got, send_sems, recv_sems):
    x, y, c = _pos()
    cps = []
    for k, (fx, fy) in enumerate(CHIP_FLIPS):
        peer = (_flip(x, fx), _flip(y, fy), c)
        qp = 2 * _flip(x, fx) + _flip(y, fy)
        for i, w in enumerate(ws):
            cps.append(_rcopy(_rows(part[i], qp * HALF[w], HALF[w]), _rows(got[i], k * HALF[w], HALF[w]),
                              send_sems.at[len(ws) * k + i], recv_sems.at[len(ws) * k + i], peer))
    return cps


def _send_start(ws, parts, *, name):
    D = parts[0].shape[1]
    n = len(ws)
    bufs = list(parts) + [lax.empty((len(CHIP_FLIPS) * HALF[w], D), parts[0].dtype) for w in ws]

    def body(*refs):
        send_sems, recv_sems = refs[2 * n], refs[2 * n + 1]
        for cp in _partial_copies(ws, refs[:n], refs[n:2 * n], send_sems, recv_sems):
            cp.start()
        refs[-1][...] = jnp.zeros_like(refs[-1])

    outs = pl.pallas_call(
        body, name=name, in_specs=[HBM] * (2 * n),
        out_specs=[SEM, SEM] + [HBM] * (2 * n) + [pl.BlockSpec(memory_space=pltpu.VMEM)],
        out_shape=[pltpu.SemaphoreType.DMA((len(CHIP_FLIPS) * n,)), pltpu.SemaphoreType.DMA((len(CHIP_FLIPS) * n,))]
        + [pltpu.HBM(b.shape, b.dtype) for b in bufs] + [TOKEN],
        input_output_aliases={i: 2 + i for i in range(2 * n)},
        compiler_params=pltpu.CompilerParams(has_side_effects=EFFECT),
    )(*[pltpu.with_memory_space_constraint(b, pltpu.HBM) for b in bufs])
    return dict(sems=outs[0:2], parts=outs[2:2 + n], got=outs[2 + n:2 + 2 * n], token=outs[-1])


def _send_wait(ws, s, after, *, name):
    n = len(ws)

    def body(*refs):
        for cp in _partial_copies(ws, refs[:n], refs[n:2 * n], refs[2 * n], refs[2 * n + 1]):
            cp.wait_send()
            cp.wait_recv()

    bufs = list(s["parts"]) + list(s["got"])
    outs = pl.pallas_call(
        body, name=name, in_specs=[HBM] * (2 * n) + [SEM, SEM] + [ANY] * len(after), out_specs=[HBM] * (2 * n),
        out_shape=[pltpu.HBM(b.shape, b.dtype) for b in bufs],
        input_output_aliases={i: i for i in range(2 * n)},
        compiler_params=pltpu.CompilerParams(has_side_effects=EFFECT),
    )(*bufs, *s["sems"], *after)
    return outs[:n], outs[n:]


def _chip_reduce(ws, parts, got, *, name, after=None):
    D = parts[0].shape[1]
    nk = len(CHIP_FLIPS)
    n = len(ws)
    extra = [] if after is None else [after]

    def body(*refs):
        refs = refs[len(extra):]
        outs = refs[(1 + nk) * n:]
        for i in range(n):
            acc = refs[i][...].astype(F32)
            for k in range(nk):
                acc = acc + refs[n * (1 + k) + i][...].astype(F32)
            outs[i][...] = acc

    blk = [HALF[w] // REDUCE_SPLIT for w in ws]

    def q_idx(j):
        return (2 * lax.axis_index("x") + lax.axis_index("y")) * REDUCE_SPLIT + j

    in_specs = [pl.BlockSpec((b, D), lambda j: (q_idx(j), 0)) for b in blk]
    for k in range(nk):
        in_specs += [pl.BlockSpec((b, D), functools.partial(lambda j, k: (k * REDUCE_SPLIT + j, 0), k=k)) for b in blk]
    out_specs = [pl.BlockSpec((b, D), lambda j: (lax.axis_index("c") * REDUCE_SPLIT + j, 0)) for b in blk]
    return pl.pallas_call(
        body, name=name, grid=(REDUCE_SPLIT,), in_specs=[ANY] * len(extra) + in_specs, out_specs=out_specs,
        out_shape=[_sds((SLAB[w], D), F32) for w in ws],
        compiler_params=_cp(("parallel",)),
    )(*extra, *parts, *[g for _ in range(nk) for g in got])


def _exchange_reduced(ws, shards, *, name):
    n = len(ws)

    def body(*refs):
        ins, outs = refs[:n], refs[n:2 * n]
        send_sems, recv_sems = refs[2 * n], refs[2 * n + 1]
        x, y, c = _pos()
        sib = (x, y, 1 - c)
        cps = []
        for i, w in enumerate(ws):
            cp = _rcopy(_rows(ins[i], c * HALF[w], HALF[w]), _rows(outs[i], c * HALF[w], HALF[w]),
                        send_sems.at[i], recv_sems.at[i], sib)
            cp.start()
            cps.append(cp)
        for cp in cps:
            cp.wait_recv()
        for cp in cps:
            cp.wait_send()

    return pl.pallas_call(
        body, name=name, in_specs=[ANY] * n, out_specs=[ANY] * n,
        out_shape=[_sds(s.shape, s.dtype) for s in shards], input_output_aliases={i: i for i in range(n)},
        scratch_shapes=[pltpu.SemaphoreType.DMA((n,)), pltpu.SemaphoreType.DMA((n,))],
    )(*shards)


def _adamw_fn(w, g, m, v):
    m2 = ADAM_B1 * m + (1.0 - ADAM_B1) * g
    v2 = ADAM_B2 * v + (1.0 - ADAM_B2) * (g * g)
    m_hat = m2 / (1.0 - ADAM_B1 ** ADAM_STEP)
    v_hat = v2 / (1.0 - ADAM_B2 ** ADAM_STEP)
    return -ADAM_LR * (m_hat / (jnp.sqrt(v_hat) + ADAM_EPS) + ADAM_WD * w), m2, v2


def _adamw(w, g, m, v, *, name):
    shp = _sds(w.shape, F32)
    rows = w.shape[0]
    tm = max(t for t in range(SUBLANES, 512 + 1, SUBLANES) if rows % t == 0)
    return _rowwise(lambda wv, gv, mv, vv: (gv, *_adamw_fn(wv, gv, mv, vv)), [_full(w), _full(g), _full(m), _full(v)], [],
                    [shp] * 4, [], name=name, tm=tm)


SMALL_SEGS = (("loss", 8), ("norm_mix_w", 8), ("b_attn", 8), ("lb_logits", 8), ("hg_norm_w", 8), ("sinks", 8),
              ("norm_ffn_w", 8), ("conv_w", 72), ("conv_b", 24), ("final_norm_w", 8))
SMALL_OFF = {n: sum(r for _, r in SMALL_SEGS[:i]) for i, (n, _) in enumerate(SMALL_SEGS)}
SMALL_ROWS = sum(r for _, r in SMALL_SEGS)


def _pack_small(parts):
    segs = []
    for n, r in SMALL_SEGS:
        a = parts.get(n)
        flat = jnp.zeros((0,), F32) if a is None else a.reshape(-1).astype(F32)
        segs.append(jnp.pad(flat, (0, r * LANES - flat.shape[0])).reshape(r, LANES))
    return jnp.concatenate(segs, axis=0)


def _unpack_small(pack, n, shape):
    size = math.prod(shape)
    r0 = SMALL_OFF[n]
    return pack[r0:r0 + dict(SMALL_SEGS)[n]].reshape(-1)[:size].reshape(shape)


def _small_update(sall, wp, mp, vp, *, after):
    R = SMALL_ROWS
    r_lb = SMALL_OFF["lb_logits"]

    def body(after_ref, s_ref, w_ref, m_ref, v_ref, g_ref, d_ref, m2_ref, v2_ref, loss_ref):
        g = s_ref[0]
        for i in range(1, N_DEV):
            g = g + s_ref[i]
        tot = jnp.sum(jnp.sum(g[0:8], axis=1, keepdims=True), axis=0, keepdims=True)
        loss_ref[...] = jnp.broadcast_to(tot, loss_ref.shape)
        lg = w_ref[r_lb:r_lb + 8, :]
        p0 = _sigmoid(lg - pltpu.roll(lg, 4, 0))
        d = g[r_lb:r_lb + 8]
        d = d + pltpu.roll(d, 4, 0)
        sign = jnp.where(lax.broadcasted_iota(jnp.int32, d.shape, 0) < 4, 1.0, -1.0)
        g = jnp.concatenate([g[:r_lb], sign * d * p0 * (1.0 - p0), g[r_lb + 8:]], axis=0)
        g_ref[...] = g
        d_ref[...], m2_ref[...], v2_ref[...] = _adamw_fn(w_ref[...], g, m_ref[...], v_ref[...])

    full = pl.BlockSpec((R, LANES), lambda: (0, 0))
    return pl.pallas_call(
        body, name="small_update",
        in_specs=[ANY, pl.BlockSpec((N_DEV, R, LANES), lambda: (0, 0, 0)), full, full, full],
        out_specs=[full, full, full, full, pl.BlockSpec((8, LANES), lambda: (0, 0))],
        out_shape=[_sds((R, LANES), F32)] * 4 + [_sds((8, LANES), F32)],
        compiler_params=_cp(),
    )(after, sall, wp, mp, vp)


def _lb_fwd(lb_logits):
    n = lb_logits.shape[1]

    def body(l_ref, o_ref):
        o_ref[...] = _sigmoid(l_ref[0:1, :] - l_ref[1:2, :])

    return pl.pallas_call(body, name="lb_fwd", out_shape=jax.ShapeDtypeStruct((1, n), F32), compiler_params=_cp())(lb_logits)


class _MeshExchange:
    def __init__(self, pack, cw8):
        self.gather = _gather_start(pack, cw8)
        self.sent = None
        self.conv_w8 = None

    def start(self):
        return self.gather["token"]

    def w_in(self, after):
        self.pack, l_in = _gather_wait_in(self.gather, after)
        return (_forward_in(l_in), N_CHIPS * SLAB[0], 0)

    def mid(self, after):
        l_out, l_cw = _gather_wait_out(self.gather, self.pack, after)
        self.conv_w8 = jnp.concatenate([l_cw[i] for i in range(N_CHIPS)], axis=1)
        self.passing_out = _forward_start(FWD_OUT, l_out, name="forward_out_start")
        return self.passing_out["token"]

    def w_out(self, after):
        l_ffn = _gather_wait_ffn(self.gather, self.pack, after)
        self.passing_ffn = _forward_start(FWD_FFN, l_ffn, name="forward_ffn_start")
        l_out = _forward_wait(FWD_OUT, self.passing_out, self.passing_ffn["token"], name="forward_out_wait")
        return dict(w_out=(l_out, N_CHIPS * SLAB[4], 0), conv_w8=self.conv_w8)

    def rest(self, after):
        l_ffn = _forward_wait(FWD_FFN, self.passing_ffn, after, name="forward_ffn_wait")
        rows = N_CHIPS * SLAB[FFN_W[0]]
        return dict(w_gate_t=(l_ffn, rows, 0), w_up_t=(l_ffn, rows, 1), w_down=(l_ffn, rows, 2))

    def ffn_grads(self, gs):
        self.swap = _halves_start(FFN_W, gs, name="halves_ffn_start")
        return self.swap["token"]

    def late_grads(self, g_in):
        self.swap_in = _halves_start((0,), [g_in], name="halves_in_start")
        return self.swap_in["token"]

    def ffn_grads_send(self, after):
        gs, theirs = _halves_wait(FFN_W, self.swap, after, name="halves_ffn_wait")
        parts = _chip_partial(FFN_W, gs, theirs, name="chip_partial_ffn", out_dtype=BF16)
        self.sent = _send_start(FFN_W, parts, name="send_ffn_start")
        return self.sent["token"]


def kernel(x, norm_mix_w, w_in, b_attn, lb_logits, hg_norm_w, sinks, w_out, norm_ffn_w, w_gate, w_up, conv_w, conv_b, w_down, final_norm_w, loss_target, m_norm_mix_w, m_w_in, m_b_attn, m_lb_logits, m_hg_norm_w, m_sinks, m_w_out, m_norm_ffn_w, m_w_gate, m_w_up, m_conv_w, m_conv_b, m_w_down, m_final_norm_w, v_norm_mix_w, v_w_in, v_b_attn, v_lb_logits, v_hg_norm_w, v_sinks, v_w_out, v_norm_ffn_w, v_w_gate, v_w_up, v_conv_w, v_conv_b, v_w_down, v_final_norm_w):
    D = D_MODEL
    q = 2 * lax.axis_index("x") + lax.axis_index("y")
    ccols = D_FF // N_CHIPS

    pack = jnp.concatenate([w_in[0].T, w_gate[0].T, w_up[0].T, w_down[0], w_out[0]], axis=0).astype(BF16)
    cw8 = jnp.concatenate([conv_w[0], jnp.zeros((SUBLANES - 3, ccols), F32)], axis=0)
    ex = _MeshExchange(pack, cw8)
    p = dict(norm_mix_w=norm_mix_w, b_attn=b_attn, lb=_lb_fwd(lb_logits), hg_norm_w=hg_norm_w, sinks=sinks,
             norm_ffn_w=norm_ffn_w, conv_b=conv_b, final_norm_w=final_norm_w.reshape(1, D))
    loss_cols, dx, g = _local_step(x[0], loss_target[0], p, ex)
    conv_w8 = ex.conv_w8

    small = _pack_small(dict(loss=loss_cols, norm_mix_w=g["norm_mix_w"], b_attn=g["b_attn"], lb_logits=g["lb"],
                             hg_norm_w=g["hg_norm_w"], sinks=g["sinks8"], norm_ffn_w=g["norm_ffn_w"],
                             conv_w=g["conv_w8"][:3], conv_b=g["conv_b"], final_norm_w=g["final_norm_w"]))
    parts_ffn, got_ffn = _send_wait(FFN_W, ex.sent, [dx], name="send_ffn_wait")
    late = (0, 4)
    (g_in_t,), (theirs_in,) = _halves_wait((0,), ex.swap_in, g["g_out"], name="halves_in_wait")
    theirs_out, sall = _exchange_halves((4,), [g["g_out"]], small, name="exchange_halves_late")
    gs, theirs = [g_in_t, g["g_out"]], [theirs_in, theirs_out]
    parts_late = _chip_partial(late, gs, theirs, name="chip_partial_late", out_dtype=BF16)
    sent_late = _send_start(late, parts_late, name="send_late_start")
    big = {}

    def finish(ws, parts, got, specs, tag, after):
        shards = _exchange_reduced(ws, _chip_reduce(ws, parts, got, name="chip_reduce_" + tag, after=after),
                                   name="exchange_reduced_" + tag)
        deltas = []
        for gw, (n, w, m, v, tr) in zip(shards, specs):
            view = (lambda a: a[0].T) if tr else (lambda a: a[0])
            back = (lambda a: a.T[None]) if tr else (lambda a: a[None])
            res = _adamw(view(w), gw, view(m), view(v), name="adamw_" + n)
            big[n] = tuple(back(r) for r in res)
            deltas.append(res[1])
        return deltas

    done_ffn = finish(FFN_W, parts_ffn, got_ffn, (("w_gate", w_gate, m_w_gate, v_w_gate, True),
                                                  ("w_up", w_up, m_w_up, v_w_up, True),
                                                  ("w_down", w_down, m_w_down, v_w_down, False)), "ffn", sent_late["token"])

    def place(a):
        return lax.dynamic_update_slice(jnp.zeros((3, D_FF), F32), a[0], (0, q * ccols))

    def small_pack(ws, cw):
        nm, ba, lbl, hg, sk, nf, cb, fn = ws
        return _pack_small(dict(norm_mix_w=nm, b_attn=ba, lb_logits=lbl, hg_norm_w=hg,
                                sinks=jnp.broadcast_to(sk.reshape(ATT_HEADS, 1), (ATT_HEADS, LANES)), norm_ffn_w=nf,
                                conv_w=cw, conv_b=cb, final_norm_w=fn))

    wp = small_pack((norm_mix_w, b_attn, lb_logits, hg_norm_w, sinks, norm_ffn_w, conv_b, final_norm_w), conv_w8[:3])
    mp = small_pack((m_norm_mix_w, m_b_attn, m_lb_logits, m_hg_norm_w, m_sinks, m_norm_ffn_w, m_conv_b, m_final_norm_w),
                    place(m_conv_w))
    vp = small_pack((v_norm_mix_w, v_b_attn, v_lb_logits, v_hg_norm_w, v_sinks, v_norm_ffn_w, v_conv_b, v_final_norm_w),
                    place(v_conv_w))
    outs = _small_update(sall, wp, mp, vp, after=sent_late["token"])
    loss = outs[4][0, 0]
    parts_late, got_late = _send_wait(late, sent_late, [*done_ffn, outs[4]], name="send_late_wait")
    finish(late, parts_late, got_late, (("w_in", w_in, m_w_in, v_w_in, True), ("w_out", w_out, m_w_out, v_w_out, False)),
           "late", None)

    def small_out(pk, n, ref):
        if n == "sinks":
            return pk[SMALL_OFF[n]:SMALL_OFF[n] + ATT_HEADS, 0].reshape(ref.shape)
        if n == "conv_w":
            full = _unpack_small(pk, n, (3, D_FF))
            return lax.dynamic_slice(full, (0, q * ccols), (3, ccols))[None]
        return _unpack_small(pk, n, ref.shape)

    refs = dict(norm_mix_w=norm_mix_w, b_attn=b_attn, lb_logits=lb_logits, hg_norm_w=hg_norm_w, sinks=sinks,
                norm_ffn_w=norm_ffn_w, conv_w=conv_w, conv_b=conv_b, final_norm_w=final_norm_w)
    order = ("norm_mix_w", "w_in", "b_attn", "lb_logits", "hg_norm_w", "sinks", "w_out", "norm_ffn_w", "w_gate", "w_up",
             "conv_w", "conv_b", "w_down", "final_norm_w")
    res = [loss, dx[None]]
    for k in range(4):
        for n in order:
            res.append(big[n][k] if n in big else small_out(outs[k], n, refs[n]))
    return tuple(res)
```

```python
import functools
import math

import jax
import jax.numpy as jnp
from jax import lax
from jax.experimental import pallas as pl
from jax.experimental.pallas import tpu as pltpu

F32 = jnp.float32
BF16 = jnp.bfloat16

D_MODEL = 1024
HG_HEADS = 4
HG_DK = 128
HG_W = HG_HEADS * HG_DK
HG_CHUNK = 64
HG_SUB = 8
HG_FWD_CHUNKS_PER_STEP = 8
HG_CHUNKS_PER_STEP = 4
ATT_HEADS = 8
ATT_KV = 2
ATT_GROUP = ATT_HEADS // ATT_KV
ATT_HD = 64
ATT_BLOCK = 128
ATT_Q_W = ATT_HEADS * ATT_HD
ATT_KV_W = ATT_KV * ATT_HD
ATT_COLS = ATT_Q_W + 2 * ATT_KV_W
IN_COLS = 4 * HG_W + ATT_COLS
D_FF = 2816
EPS = 1e-6
ADAM_LR, ADAM_B1, ADAM_B2, ADAM_EPS, ADAM_WD, ADAM_STEP = 0.001, 0.9, 0.999, 1e-08, 0.01, 10
NEG = -1e30

V7X_VMEM_BYTES = 64 * 1024 * 1024
VMEM_LIMIT = 48 * 1024 * 1024
SUBLANES = 8

N_CHIPS = 4


def _cp(sem=None, **kw):
    return pltpu.CompilerParams(dimension_semantics=sem, vmem_limit_bytes=VMEM_LIMIT, **kw)


def _sds(shape, dtype):
    return jax.ShapeDtypeStruct(shape, dtype)


TOKEN = jax.ShapeDtypeStruct((8, 128), jnp.float32)


def _wspec(w):
    arr, rows, blk = w
    return pl.BlockSpec((rows, arr.shape[1]), lambda i: (blk, 0))


def _mm_nt(a, w, *, splits, out_dtype, name, after=None, tm=512):
    M, K = a.shape
    N = w[1]
    tm = min(tm, M)
    assert sum(splits) == N and M % tm == 0
    offs = [sum(splits[:i]) for i in range(len(splits))]
    n_in = 2 if after is None else 3

    def body(*refs):
        a_ref, w_ref = refs[0], refs[1]
        acc = lax.dot_general(a_ref[...], w_ref[...], (((1,), (1,)), ((), ())), preferred_element_type=F32)
        for o_ref, c0, n in zip(refs[n_in:], offs, splits):
            o_ref[...] = acc[:, c0:c0 + n].astype(out_dtype)

    in_specs = [pl.BlockSpec((tm, K), lambda i: (i, 0)), _wspec(w)]
    args = [a, w[0]]
    if after is not None:
        in_specs.append(pl.BlockSpec(memory_space=pl.ANY))
        args.append(after)
    outs = pl.pallas_call(
        body, name=name, grid=(M // tm,), in_specs=in_specs,
        out_specs=[pl.BlockSpec((tm, n), lambda i: (i, 0)) for n in splits],
        out_shape=[_sds((M, n), out_dtype) for n in splits],
        compiler_params=_cp(("parallel",)),
    )(*args)
    return outs


def _mm_nn(pieces, ws, *, name, out_dtype=F32, residual=None, epilogue=None, prologue=None, after=None,
           w_transposed=False, tm=512):
    pro_fn, pro_rows, pro_bc, pro_out = prologue or (None, [], [], None)
    if prologue is not None:
        assert pieces is None and len(ws) == 1
        pieces = [[pro_out]]
    M = pieces[0][0].shape[0]
    K = ws[0][1] if w_transposed else ws[0][0].shape[1]
    tm = min(tm, M)
    flat = [] if prologue is not None else [p for grp in pieces for p in grp]
    n_p = len(flat)
    n_w = len(ws)
    n_pr, n_pb = len(pro_rows), len(pro_bc)
    fn, row_ins, bc_ins, row_outs, acc_outs = epilogue or (None, [], [], [_sds((M, K), out_dtype)], [])
    if residual is not None:
        assert epilogue is None
        row_ins = [residual]
    n_r, n_b, n_o = len(row_ins), len(bc_ins), len(row_outs)
    lead = [] if after is None else [after]

    def body(*refs):
        refs = refs[len(lead):]
        p_refs = refs[:n_p]
        w_refs = refs[n_p:n_p + n_w]
        extra = [r[...] for r in refs[n_p + n_w:n_p + n_w + n_r + n_b]]
        base = n_p + n_w + n_r + n_b
        pro = [r[...] for r in refs[base:base + n_pr + n_pb]]
        base += n_pr + n_pb
        o_refs = refs[base:base + n_o]
        a_refs = refs[base + n_o:base + n_o + len(acc_outs)]
        if pro_fn is not None:
            lhs = pro_fn(*pro).astype(pro_out.dtype)
            refs[-1][...] = lhs
            tiles = [lhs]
        else:
            tiles = [r[...] for r in p_refs]
        acc = None
        k = 0
        for gi, grp in enumerate(pieces):
            c0 = 0
            for p in grp:
                n = p.shape[1]
                if w_transposed:
                    t = lax.dot_general(tiles[k], w_refs[gi][...], (((1,), (1,)), ((), ())), preferred_element_type=F32)
                else:
                    t = jnp.dot(tiles[k], w_refs[gi][c0:c0 + n, :], preferred_element_type=F32)
                acc = t if acc is None else acc + t
                c0 += n
                k += 1
        if fn is None:
            res = (acc + extra[0] if residual is not None else acc,)
        else:
            res = fn(acc, *extra)
        for o_ref, val in zip(o_refs, res[:n_o]):
            o_ref[...] = val.astype(o_ref.dtype)
        if acc_outs:
            @pl.when(pl.program_id(0) == 0)
            def _():
                for a_ref in a_refs:
                    a_ref[...] = jnp.zeros_like(a_ref)
            for a_ref, val in zip(a_refs, res[n_o:]):
                a_ref[...] += val

    in_specs = [pl.BlockSpec((tm, p.shape[1]), lambda i: (i, 0)) for p in flat]
    in_specs += [_wspec(w) for w in ws]
    in_specs += [pl.BlockSpec((tm, r.shape[1]), lambda i: (i, 0)) for r in row_ins]
    in_specs += [pl.BlockSpec(b.shape, lambda i: (0, 0)) for b in bc_ins]
    in_specs += [pl.BlockSpec((tm, r.shape[1]), lambda i: (i, 0)) for r in pro_rows]
    in_specs += [pl.BlockSpec(b.shape, lambda i: (0, 0)) for b in pro_bc]
    out_specs = [pl.BlockSpec((tm, s.shape[1]), lambda i: (i, 0)) for s in row_outs]
    out_specs += [pl.BlockSpec(s.shape, lambda i: (0, 0)) for s in acc_outs]
    pro_outs = [] if prologue is None else [pro_out]
    out_specs += [pl.BlockSpec((tm, s.shape[1]), lambda i: (i, 0)) for s in pro_outs]
    outs = pl.pallas_call(
        body, name=name, grid=(M // tm,), in_specs=[pl.BlockSpec(memory_space=pl.ANY)] * len(lead) + in_specs,
        out_specs=out_specs, out_shape=list(row_outs) + list(acc_outs) + pro_outs,
        compiler_params=_cp(("arbitrary",) if acc_outs else ("parallel",)),
    )(*lead, *flat, *[w[0] for w in ws], *row_ins, *bc_ins, *pro_rows, *pro_bc)
    return outs if (epilogue is not None or prologue is not None) else outs[0]


def _mm_tn(pieces, x, *, name, out_dtype=BF16, tt=1024, after=None):
    M, K = x.shape
    tt = min(tt, M)
    ns = [p.shape[1] for p in pieces]
    offs = [sum(ns[:i]) for i in range(len(ns))]
    N = sum(ns)
    n_p = len(pieces)
    last = M // tt - 1
    lead = [] if after is None else [after]

    def body(*refs):
        refs = refs[len(lead):]
        p_refs = refs[:n_p]
        x_ref = refs[n_p]
        o_ref, acc_ref = refs[n_p + 1], refs[n_p + 2]

        @pl.when(pl.program_id(0) == 0)
        def _():
            acc_ref[...] = jnp.zeros_like(acc_ref)

        xv = x_ref[...]
        for p_ref, c0, n in zip(p_refs, offs, ns):
            acc_ref[c0:c0 + n, :] += lax.dot_general(p_ref[...], xv, (((0,), (0,)), ((), ())),
                                                      preferred_element_type=F32)

        @pl.when(pl.program_id(0) == last)
        def _():
            o_ref[...] = acc_ref[...].astype(o_ref.dtype)

    in_specs = [pl.BlockSpec(memory_space=pl.ANY)] * len(lead) + [pl.BlockSpec((tt, n), lambda i: (i, 0)) for n in ns]
    in_specs.append(pl.BlockSpec((tt, K), lambda i: (i, 0)))
    return pl.pallas_call(
        body, name=name, grid=(M // tt,), in_specs=in_specs,
        out_specs=pl.BlockSpec((N, K), lambda i: (0, 0)),
        out_shape=_sds((N, K), out_dtype),
        scratch_shapes=[pltpu.VMEM((N, K), F32)],
        compiler_params=_cp(("arbitrary",)),
    )(*lead, *pieces, x)


def _rms_fwd(xf, w):
    inv = lax.rsqrt(jnp.mean(xf * xf, axis=-1, keepdims=True) + EPS)
    return xf * inv * w


def _rms_bwd(xf, w, dy):
    inv = lax.rsqrt(jnp.mean(xf * xf, axis=-1, keepdims=True) + EPS)
    xhat = xf * inv
    dxhat = dy * w
    dx = inv * (dxhat - xhat * jnp.mean(dxhat * xhat, axis=-1, keepdims=True))
    dw = jnp.sum(dy * xhat, axis=0, keepdims=True)
    return dx, dw


def _sigmoid(x):
    return 1.0 / (1.0 + jnp.exp(-x))


def _rowwise(fn, row_ins, bc_ins, row_outs, acc_outs, *, name, tm=256, after=None):
    M = row_outs[0].shape[0] if row_outs else row_ins[0][0].shape[0]
    assert M % tm == 0 and tm % SUBLANES == 0, (name, M, tm)
    n_r, n_b, n_o, n_a = len(row_ins), len(bc_ins), len(row_outs), len(acc_outs)
    n_after = 0 if after is None else 1

    def body(*refs):
        refs = refs[n_after:]
        ins = [r[...] for r in refs[:n_r + n_b]]
        o_refs = refs[n_r + n_b:n_r + n_b + n_o]
        a_refs = refs[n_r + n_b + n_o:]
        res = fn(*ins)
        for o_ref, val in zip(o_refs, res[:n_o]):
            o_ref[...] = val.astype(o_ref.dtype)
        if n_a:
            @pl.when(pl.program_id(0) == 0)
            def _():
                for a_ref in a_refs:
                    a_ref[...] = jnp.zeros_like(a_ref)
            for a_ref, val in zip(a_refs, res[n_o:]):
                a_ref[...] += val

    in_specs = [pl.BlockSpec((tm, cw), functools.partial(lambda i, cb, r0: (i + r0, cb), cb=cb, r0=r0))
                for (_, cw, cb, r0) in row_ins]
    in_specs += [pl.BlockSpec(b.shape, lambda i: (0, 0)) for b in bc_ins]
    out_specs = [pl.BlockSpec((tm, s.shape[1]), lambda i: (i, 0)) for s in row_outs]
    out_specs += [pl.BlockSpec(s.shape, lambda i: (0, 0)) for s in acc_outs]
    if n_after:
        in_specs = [pl.BlockSpec(memory_space=pl.ANY)] + in_specs
    return pl.pallas_call(
        body, name=name, grid=(M // tm,), in_specs=in_specs, out_specs=out_specs,
        out_shape=list(row_outs) + list(acc_outs),
        compiler_params=_cp(("arbitrary",) if n_a else ("parallel",)),
    )(*([after] if n_after else []), *[r[0] for r in row_ins], *bc_ins)


def _full(a, first_row_block=0):
    return (a, a.shape[1], 0, first_row_block)


def _conv_rows(ext, w_ref_val):
    s1 = pltpu.roll(ext, 1, 0)
    s2 = pltpu.roll(ext, 2, 0)
    y = w_ref_val[0:1, :] * s2 + w_ref_val[1:2, :] * s1 + w_ref_val[2:3, :] * ext
    return y[SUBLANES:, :]


def _ffn_in(v, w_gate, w_up, conv_w8, conv_b, *, name, tm=256):
    T, K = v.shape
    N = w_gate[1]
    tm = min(tm, T)

    def body(v_ref, wg_ref, wu_ref, cw_ref, cb_ref, gp_ref, up_ref, gate_ref, act_ref, carry_sc):
        @pl.when(pl.program_id(0) == 0)
        def _():
            carry_sc[...] = jnp.zeros_like(carry_sc)

        vv = v_ref[...]
        dn = (((1,), (1,)), ((), ()))
        gp = lax.dot_general(vv, wg_ref[...], dn, preferred_element_type=F32)
        up = lax.dot_general(vv, wu_ref[...], dn, preferred_element_type=F32)
        gp_ref[...] = gp.astype(gp_ref.dtype)
        up_ref[...] = up.astype(up_ref.dtype)
        gate = _conv_rows(jnp.concatenate([carry_sc[...], gp], axis=0), cw_ref[...]) + cb_ref[...]
        gate_ref[...] = gate
        act_ref[...] = (gate * _sigmoid(gate) * up).astype(act_ref.dtype)
        carry_sc[...] = gp[tm - SUBLANES:, :]

    tile = pl.BlockSpec((tm, N), lambda i: (i, 0))
    return pl.pallas_call(
        body, name=name, grid=(T // tm,),
        in_specs=[pl.BlockSpec((tm, K), lambda i: (i, 0)), _wspec(w_gate), _wspec(w_up),
                  pl.BlockSpec((SUBLANES, N), lambda i: (0, 0)), pl.BlockSpec((1, N), lambda i: (0, 0))],
        out_specs=[tile] * 4,
        out_shape=[_sds((T, N), BF16), _sds((T, N), BF16), _sds((T, N), F32), _sds((T, N), BF16)],
        scratch_shapes=[pltpu.VMEM((SUBLANES, N), F32)],
        compiler_params=_cp(("arbitrary",)),
    )(v, w_gate[0], w_up[0], conv_w8, conv_b)


def _ffn_back(dh2, w_down, gp, up, gate, conv_w8, *, name, tr=512, tc=1408):
    T, C = gp.shape
    K = dh2.shape[1]
    warr, _, wblk = w_down
    tr = min(tr, T)
    nr = T // tr
    ncb = C // tc

    def body(dh_ref, wd_ref, gp_ref, up_ref, gate_ref, w_ref, dgp_ref, dup_ref, dw_ref, db_ref, carry_sc):
        @pl.when(pl.program_id(1) == 0)
        def _():
            carry_sc[...] = jnp.zeros_like(carry_sc)
            dw_ref[...] = jnp.zeros_like(dw_ref)
            db_ref[...] = jnp.zeros_like(db_ref)

        w = w_ref[...]
        dact = lax.dot_general(dh_ref[...], wd_ref[...], (((1,), (1,)), ((), ())), preferred_element_type=F32)
        gpc = gp_ref[...].astype(F32)
        gate = gate_ref[...]
        sg = _sigmoid(gate)
        silu = gate * sg
        dup_ref[...] = (dact * silu).astype(dup_ref.dtype)
        dgate = dact * up_ref[...].astype(F32) * (sg + silu * (1.0 - sg))
        ext = jnp.concatenate([dgate, carry_sc[...]], axis=0)
        n = tr + SUBLANES
        g1 = pltpu.roll(ext, n - 1, 0)[:tr]
        g2 = pltpu.roll(ext, n - 2, 0)[:tr]
        dgp_ref[...] = (w[2:3, :] * dgate + w[1:2, :] * g1 + w[0:1, :] * g2).astype(dgp_ref.dtype)
        dw0 = jnp.sum(gpc * g2, axis=0, keepdims=True)
        dw1 = jnp.sum(gpc * g1, axis=0, keepdims=True)
        dw2 = jnp.sum(gpc * dgate, axis=0, keepdims=True)
        z = jnp.zeros((SUBLANES - 3, gpc.shape[1]), F32)
        dw_ref[...] += jnp.concatenate([dw0, dw1, dw2, z], axis=0)
        db_ref[...] += jnp.sum(dgate, axis=0, keepdims=True)
        carry_sc[...] = dgate[:SUBLANES]

    rev = lambda i: nr - 1 - i
    cur = pl.BlockSpec((tr, tc), lambda j, i: (rev(i), j))
    return pl.pallas_call(
        body, name=name, grid=(ncb, nr),
        in_specs=[pl.BlockSpec((tr, K), lambda j, i: (rev(i), 0)),
                  pl.BlockSpec((tc, K), lambda j, i: (wblk * ncb + j, 0)),
                  cur, cur, cur,
                  pl.BlockSpec((SUBLANES, tc), lambda j, i: (0, j))],
        out_specs=[cur, cur,
                   pl.BlockSpec((SUBLANES, tc), lambda j, i: (0, j)),
                   pl.BlockSpec((1, tc), lambda j, i: (0, j))],
        out_shape=[_sds((T, C), BF16), _sds((T, C), BF16), _sds((SUBLANES, C), F32), _sds((1, C), F32)],
        scratch_shapes=[pltpu.VMEM((SUBLANES, tc), F32)],
        compiler_params=_cp(("parallel", "arbitrary")),
    )(dh2, warr, gp, up, gate, conv_w8)


def _cumsum_rows(x):
    n = x.shape[0]
    row = lax.broadcasted_iota(jnp.int32, x.shape, 0)
    s = 1
    while s < n:
        x = x + jnp.where(row >= s, pltpu.roll(x, s, 0), 0.0)
        s *= 2
    return x


def _rcumsum_rows(x):
    n = x.shape[0]
    row = lax.broadcasted_iota(jnp.int32, x.shape, 0)
    s = 1
    while s < n:
        x = x + jnp.where(row < n - s, pltpu.roll(x, n - s, 0), 0.0)
        s *= 2
    return x


def _dot_nt(a, b):
    return lax.dot_general(a.astype(BF16), b.astype(BF16), (((1,), (1,)), ((), ())), preferred_element_type=F32)


def _dot_tn(a, b):
    return lax.dot_general(a.astype(BF16), b.astype(BF16), (((0,), (0,)), ((), ())), preferred_element_type=F32)


def _dot_nn(a, b):
    return jnp.dot(a.astype(BF16), b.astype(BF16), preferred_element_type=F32)


def _hg_gates(hq, hf, lbv):
    sig = _sigmoid(hf)
    f = lbv + (1.0 - lbv) * sig
    return sig, f, jnp.log(f), 1.0 - f, hq * (HG_DK ** -0.5)


def _hg_sel_rows(ref, sp):
    return jnp.concatenate(
        [jnp.broadcast_to(ref[pl.ds(HG_SUB * i + sp, 1), :], (HG_SUB, HG_DK)) for i in range(HG_CHUNK // HG_SUB)], axis=0)


def _hg_masks():
    C = HG_CHUNK
    row = lax.broadcasted_iota(jnp.int32, (C, C), 0)
    col = lax.broadcasted_iota(jnp.int32, (C, C), 1)
    d = col - (row // HG_SUB) * HG_SUB
    tmod = row % HG_SUB
    diag_valid = jnp.logical_and(d >= 0, d <= tmod)
    return row, col, d, diag_valid


def _hg_strip_keys(k, b, r, n):
    ek = jnp.exp(r - b[:n])
    return ek, jnp.concatenate([k[:n] * ek, jnp.zeros((HG_CHUNK - n, k.shape[1]), F32)], axis=0)


def _hg_scores(q, k, b, b_sc, k_sc):
    C, S = HG_CHUNK, HG_SUB
    row, col, d, diag_valid = _hg_masks()
    blocks = [jnp.zeros((S, C), F32)]
    for i in range(1, C // S):
        r = b_sc[pl.ds(S * i - 1, 1), :]
        qi = q[S * i:S * (i + 1)] * jnp.exp(b[S * i:S * (i + 1)] - r)
        blocks.append(_dot_nt(qi, _hg_strip_keys(k, b, r, S * i)[1]))
    a_off = jnp.concatenate(blocks, axis=0)
    a_d = jnp.zeros((C, C), F32)
    for sp in range(S):
        bs = _hg_sel_rows(b_sc, sp)
        ks = _hg_sel_rows(k_sc, sp)
        e = jnp.exp(jnp.minimum(b - bs, 0.0))
        colv = jnp.sum(q * ks * e, axis=-1, keepdims=True)
        a_d = jnp.where(d == sp, colv, a_d)
    return a_off + jnp.where(diag_valid, a_d, 0.0)


def _hg_prep(hq_v, hf_v, lbv, b_sc, k_sc):
    sig, f, g, k, q = _hg_gates(hq_v, hf_v, lbv)
    b = _cumsum_rows(g)
    b_sc[...] = b
    k_sc[...] = k
    return sig, f, k, q, b, b_sc[pl.ds(HG_CHUNK - 1, 1), :]


def _hgrn_fwd(hq, hf, hi, lb, *, name):
    T = hq.shape[0]
    C, H, K = HG_CHUNK, HG_HEADS, HG_DK
    NC = T // C

    def body(hq_ref, hf_ref, hi_ref, lb_ref, o_ref, st_ref, s_sc, b_sc, k_sc):
        @pl.when(pl.program_id(0) == 0)
        def _():
            s_sc[...] = jnp.zeros_like(s_sc)

        st_all = s_sc[...]
        for j in range(P):
            rows = slice(C * j, C * (j + 1))
            st_ref[j] = st_all
            outs, news = [], []
            for h in range(H):
                sl = slice(K * h, K * (h + 1))
                _, _, k, q, b, bc = _hg_prep(hq_ref[rows, sl], hf_ref[rows, sl], lb_ref[:, sl], b_sc.at[j, h], k_sc.at[j, h])
                v = hi_ref[rows, sl]
                st0 = st_all[:, sl]
                a = _hg_scores(q, k, b, b_sc.at[j, h], k_sc.at[j, h])
                outs.append(_dot_nn(a, v) + _dot_nt(q * jnp.exp(b), st0))
                news.append(st0 * jnp.exp(bc) + _dot_tn(v, k * jnp.exp(bc - b)))
            o_ref[rows, :] = jnp.concatenate(outs, axis=1)
            st_all = jnp.concatenate(news, axis=1)
        s_sc[...] = st_all

    P = HG_FWD_CHUNKS_PER_STEP
    blk = pl.BlockSpec((P * C, H * K), lambda c: (c, 0))
    return pl.pallas_call(
        body, name=name, grid=(NC // P,),
        in_specs=[blk, blk, blk, pl.BlockSpec((1, H * K), lambda c: (0, 0))],
        out_specs=[blk, pl.BlockSpec((P, K, H * K), lambda c: (c, 0, 0))],
        out_shape=[_sds((T, H * K), F32), _sds((NC, K, H * K), F32)],
        scratch_shapes=[pltpu.VMEM((K, H * K), F32), pltpu.VMEM((P, H, C, K), F32), pltpu.VMEM((P, H, C, K), F32)],
        compiler_params=_cp(("arbitrary",)),
    )(hq, hf, hi, lb)


def _hgrn_bwd(hq, hf, hi, lb, states, do, *, name):
    T = hq.shape[0]
    C, H, K, S = HG_CHUNK, HG_HEADS, HG_DK, HG_SUB
    NC = T // C

    def intra_slow(q, k, b, da, b_sc, k_sc):
        row, col, d, diag_valid = _hg_masks()
        a_blocks = [jnp.zeros((S, C), F32)]
        dq_blocks = [jnp.zeros((S, K), F32)]
        dk = jnp.zeros((C, K), F32)
        for i in range(1, C // S):
            r = b_sc[pl.ds(S * i - 1, 1), :]
            eq = jnp.exp(b[S * i:S * (i + 1)] - r)
            ek = jnp.exp(jnp.minimum(r - b, 0.0))
            qi = q[S * i:S * (i + 1)] * eq
            kk = k * ek
            a_blocks.append(_dot_nt(qi, kk))
            dai = jnp.where(col[S * i:S * (i + 1)] < S * i, da[S * i:S * (i + 1)], 0.0)
            dq_blocks.append(_dot_nn(dai, kk) * eq)
            dk = dk + _dot_tn(dai, qi) * ek
        dq = jnp.concatenate(dq_blocks, axis=0)
        a_off = jnp.where(col < (row // S) * S, jnp.concatenate(a_blocks, axis=0), 0.0)
        same_blk = (row // S == col // S).astype(BF16)
        tmod = (lax.broadcasted_iota(jnp.int32, (C, K), 0)) % S
        a_d = jnp.zeros((C, C), F32)
        dk_d = jnp.zeros((C, K), F32)
        for sp in range(S):
            bs = _hg_sel_rows(b_sc, sp)
            ks = _hg_sel_rows(k_sc, sp)
            e = jnp.exp(jnp.minimum(b - bs, 0.0))
            eks = e * ks
            a_d = jnp.where(d == sp, jnp.sum(q * eks, axis=-1, keepdims=True), a_d)
            dacol = jnp.sum(jnp.where(d == sp, da, 0.0), axis=-1, keepdims=True)
            dq = dq + dacol * eks
            wq = dacol * e * q
            wq_hi = wq.astype(BF16)
            wq_lo = (wq - wq_hi.astype(F32)).astype(BF16)
            blk_sum = (jnp.dot(same_blk, wq_hi, preferred_element_type=F32)
                       + jnp.dot(same_blk, wq_lo, preferred_element_type=F32))
            dk_d = jnp.where(tmod == sp, blk_sum, dk_d)
        return a_off + jnp.where(diag_valid, a_d, 0.0), dq, dk + dk_d

    def one_head(pre, v, lbv, st0, dst1, dout, b_sc, k_sc):
        sig, f, k, q, b, bc = pre
        ebc = jnp.exp(bc)
        eb = jnp.exp(b)
        ekb = jnp.exp(bc - b)
        qt = q * eb
        kb = k * ekb
        row = lax.broadcasted_iota(jnp.int32, (C, C), 0)
        col = lax.broadcasted_iota(jnp.int32, (C, C), 1)
        da = jnp.where(col <= row, _dot_nt(dout, v), 0.0)
        dkb = _dot_nn(v, dst1)
        new_ds = _dot_tn(dout, qt) + dst1 * ebc
        a, dq_i, dk_i = intra_slow(q, k, b, da, b_sc, k_sc)
        dq = _dot_nn(dout, st0) * eb + dq_i
        dk = dkb * ekb + dk_i
        dv = _dot_tn(a, dout) + _dot_nt(kb, dst1)
        extra = jnp.sum(dkb * kb, axis=0, keepdims=True) + ebc * jnp.sum(st0 * dst1, axis=0, keepdims=True)
        rowk = lax.broadcasted_iota(jnp.int32, (C, K), 0)
        db = q * dq - k * dk + jnp.where(rowk == C - 1, extra, 0.0)
        dg = _rcumsum_rows(db)
        df = dg / f - dk
        return (dq * (K ** -0.5), df * (1.0 - lbv) * sig * (1.0 - sig), dv,
                jnp.sum(df * (1.0 - sig), axis=0, keepdims=True), new_ds)

    def body(hq_ref, hf_ref, hi_ref, lb_ref, st_ref, do_ref, dq_ref, dhf_ref, dv_ref, dlb_ref, ds_sc, b_sc, k_sc):
        @pl.when(pl.program_id(0) == 0)
        def _():
            ds_sc[...] = jnp.zeros_like(ds_sc)
            dlb_ref[...] = jnp.zeros_like(dlb_ref)

        ds_all = ds_sc[...]
        dlb = jnp.zeros((1, H * K), F32)
        for j in reversed(range(P)):
            rows = slice(C * j, C * (j + 1))
            st_all = st_ref[j]
            res = []
            for h in range(H):
                sl = slice(K * h, K * (h + 1))
                pre = _hg_prep(hq_ref[rows, sl], hf_ref[rows, sl], lb_ref[:, sl], b_sc.at[j, h], k_sc.at[j, h])
                res.append(one_head(pre, hi_ref[rows, sl], lb_ref[:, sl], st_all[:, sl], ds_all[:, sl], do_ref[rows, sl],
                                    b_sc.at[j, h], k_sc.at[j, h]))
            cat = lambda i: jnp.concatenate([r[i] for r in res], axis=1)
            dq_ref[rows, :] = cat(0).astype(dq_ref.dtype)
            dhf_ref[rows, :] = cat(1).astype(dhf_ref.dtype)
            dv_ref[rows, :] = cat(2).astype(dv_ref.dtype)
            dlb = dlb + cat(3)
            ds_all = cat(4)
        dlb_ref[...] += dlb
        ds_sc[...] = ds_all

    P = HG_CHUNKS_PER_STEP
    NS = NC // P
    blk = pl.BlockSpec((P * C, H * K), lambda c: (NS - 1 - c, 0))
    par = pl.BlockSpec((1, H * K), lambda c: (0, 0))
    return pl.pallas_call(
        body, name=name, grid=(NS,),
        in_specs=[blk, blk, blk, par, pl.BlockSpec((P, K, H * K), lambda c: (NS - 1 - c, 0, 0)), blk],
        out_specs=[blk, blk, blk, par],
        out_shape=[_sds((T, H * K), BF16)] * 3 + [_sds((1, H * K), F32)],
        scratch_shapes=[pltpu.VMEM((K, H * K), F32), pltpu.VMEM((P, H, C, K), F32), pltpu.VMEM((P, H, C, K), F32)],
        compiler_params=_cp(("arbitrary",)),
    )(hq, hf, hi, lb, states, do)


ATT_STACK = ATT_GROUP
ATT_FWD_QROWS = ATT_BLOCK // 2


def _att_valid(n, a=0, qrows=ATT_BLOCK):
    R, B = ATT_STACK * qrows, ATT_BLOCK
    j = lax.broadcasted_iota(jnp.int32, (B + qrows, R), 0)
    t = lax.broadcasted_iota(jnp.int32, (B + qrows, R), 1) % qrows
    dist = t + B - j
    first_key = jnp.where(n > 0, 0, B)
    return jnp.logical_and(jnp.logical_and(dist >= 0, dist < B), j + qrows * a >= first_key)


def _att_rows(x, a, qrows):
    return jnp.concatenate([x[ATT_BLOCK * g + qrows * a:ATT_BLOCK * g + qrows * (a + 1)] for g in range(ATT_STACK)], axis=0)


def _att_load(cur_ref, prev_ref, ba_ref, h0):
    hd = ATT_HD
    kv = h0 // ATT_GROUP
    def cols(ref, c0):
        return ref[:, c0:c0 + hd] + ba_ref[:, c0:c0 + hd]
    qs = jnp.concatenate([cols(cur_ref, hd * (h0 + g)) for g in range(ATT_STACK)], axis=0)
    kc = jnp.concatenate([cols(prev_ref, ATT_Q_W + hd * kv), cols(cur_ref, ATT_Q_W + hd * kv)], axis=0)
    vc = jnp.concatenate([cols(prev_ref, ATT_Q_W + ATT_KV_W + hd * kv), cols(cur_ref, ATT_Q_W + ATT_KV_W + hd * kv)], axis=0)
    return qs, kc, vc


def _att_probs(qs, kc, valid, sink_ref, h0):
    scale = 1.0 / math.sqrt(ATT_HD)
    s = jnp.where(valid, _dot_nt(kc, qs) * scale, NEG)
    nq = qs.shape[0] // ATT_STACK
    sink = jnp.concatenate([jnp.full((1, nq), sink_ref[0, h0 + g], F32) for g in range(ATT_STACK)], axis=1)
    m = jnp.maximum(jnp.max(s, axis=0, keepdims=True), sink)
    p = jnp.exp(s - m)
    ps = jnp.exp(sink - m)
    inv = 1.0 / (jnp.sum(p, axis=0, keepdims=True) + ps)
    return p * inv, ps * inv


def _attn_fwd(att, b_attn, sinks, *, name, after=None):
    T = att.shape[0]
    B = ATT_BLOCK
    NB = T // B
    lead = [] if after is None else [after]

    def body(*refs):
        sink_ref, cur_ref, prev_ref, ba_ref, o_ref = refs[len(lead):]
        Q = ATT_FWD_QROWS
        parts = range(B // Q)
        valid = [_att_valid(pl.program_id(0), a, Q) for a in parts]
        outs = [[None] * len(parts) for _ in range(ATT_HEADS)]
        for h0 in range(0, ATT_HEADS, ATT_STACK):
            qs, kc, vc = _att_load(cur_ref, prev_ref, ba_ref, h0)
            for a in parts:
                keys = slice(Q * a, Q * a + B + Q)
                prob, _ = _att_probs(_att_rows(qs, a, Q), kc[keys], valid[a], sink_ref, h0)
                o = _dot_tn(prob, vc[keys])
                for g in range(ATT_STACK):
                    outs[h0 + g][a] = o[Q * g:Q * (g + 1)]
        o_ref[...] = jnp.concatenate([jnp.concatenate(p, axis=0) for p in outs], axis=1)

    return pl.pallas_call(
        body, name=name, grid=(NB,),
        in_specs=[pl.BlockSpec(memory_space=pl.ANY)] * len(lead) + [
            pl.BlockSpec(memory_space=pltpu.SMEM),
            pl.BlockSpec((B, ATT_COLS), lambda n: (n, 0)),
            pl.BlockSpec((B, ATT_COLS), lambda n: (jnp.maximum(n - 1, 0), 0)),
            pl.BlockSpec((1, ATT_COLS), lambda n: (0, 0))],
        out_specs=pl.BlockSpec((B, ATT_Q_W), lambda n: (n, 0)),
        out_shape=_sds((T, ATT_Q_W), F32),
        compiler_params=_cp(("parallel",)),
    )(*lead, sinks, att, att, b_attn)


def _attn_bwd(att, b_attn, sinks, dmix, *, name):
    T = att.shape[0]
    B, hd = ATT_BLOCK, ATT_HD
    NB = T // B
    scale = 1.0 / math.sqrt(hd)

    def body(sink_ref, cur_ref, prev_ref, ba_ref, do_ref, daq_ref, dakv_ref, dsink_ref, dbq_ref, dbkv_ref, carry_sc):
        n = pl.program_id(0)

        @pl.when(n == 0)
        def _():
            carry_sc[...] = jnp.zeros_like(carry_sc)
            dsink_ref[...] = jnp.zeros_like(dsink_ref)
            dbq_ref[...] = jnp.zeros_like(dbq_ref)
            dbkv_ref[...] = jnp.zeros_like(dbkv_ref)

        @pl.when(n < NB)
        def _():
            valid = _att_valid(n)
            hrow = lax.broadcasted_iota(jnp.int32, (SUBLANES, 128), 0)
            dsink = jnp.zeros((SUBLANES, 128), F32)
            dqs = []
            dks = [jnp.zeros((2 * B, hd), F32)] * ATT_KV
            dvs = [jnp.zeros((2 * B, hd), F32)] * ATT_KV
            for h0 in range(0, ATT_HEADS, ATT_STACK):
                kv = h0 // ATT_GROUP
                qs, kc, vc = _att_load(cur_ref, prev_ref, ba_ref, h0)
                prob, psink = _att_probs(qs, kc, valid, sink_ref, h0)
                dout = jnp.concatenate([do_ref[:, hd * (h0 + g):hd * (h0 + g + 1)] for g in range(ATT_STACK)], axis=0)
                dp = _dot_nt(vc, dout)
                delta = jnp.sum(prob * dp, axis=0, keepdims=True)
                dsc = prob * (dp - delta) * scale
                dq = _dot_tn(dsc, kc)
                dks[kv] = dks[kv] + _dot_nn(dsc, qs)
                dvs[kv] = dvs[kv] + _dot_nn(prob, dout)
                dsk = psink * delta
                for g in range(ATT_STACK):
                    dqs.append(dq[B * g:B * (g + 1)])
                    tot = jnp.sum(dsk[:, B * g:B * (g + 1)], axis=1, keepdims=True)
                    dsink = dsink - jnp.where(hrow == h0 + g, tot, 0.0)
            daq = jnp.concatenate(dqs, axis=1).astype(daq_ref.dtype)
            daq_ref[...] = daq
            dsink_ref[...] += dsink
            dbq_ref[...] += jnp.sum(daq.astype(F32), axis=0, keepdims=True)
            done = carry_sc[...] + jnp.concatenate([d[:B] for d in dks + dvs], axis=1)
            dakv_ref[...] = done.astype(dakv_ref.dtype)
            dbkv_ref[...] += jnp.sum(done.astype(dakv_ref.dtype).astype(F32), axis=0, keepdims=True)
            carry_sc[...] = jnp.concatenate([d[B:] for d in dks + dvs], axis=1)

        @pl.when(n == NB)
        def _():
            done = carry_sc[...]
            dakv_ref[...] = done.astype(dakv_ref.dtype)
            dbkv_ref[...] += jnp.sum(done.astype(dakv_ref.dtype).astype(F32), axis=0, keepdims=True)

    cl = lambda n: jnp.minimum(n, NB - 1)
    return pl.pallas_call(
        body, name=name, grid=(NB + 1,),
        in_specs=[pl.BlockSpec(memory_space=pltpu.SMEM),
                  pl.BlockSpec((B, ATT_COLS), lambda n: (cl(n), 0)),
                  pl.BlockSpec((B, ATT_COLS), lambda n: (jnp.maximum(cl(n) - 1, 0), 0)),
                  pl.BlockSpec((1, ATT_COLS), lambda n: (0, 0)),
                  pl.BlockSpec((B, ATT_Q_W), lambda n: (cl(n), 0))],
        out_specs=[pl.BlockSpec((B, ATT_Q_W), lambda n: (cl(n), 0)),
                   pl.BlockSpec((B, 2 * ATT_KV_W), lambda n: (jnp.maximum(n - 1, 0), 0)),
                   pl.BlockSpec((SUBLANES, 128), lambda n: (0, 0)),
                   pl.BlockSpec((1, ATT_Q_W), lambda n: (0, 0)),
                   pl.BlockSpec((1, 2 * ATT_KV_W), lambda n: (0, 0))],
        out_shape=[_sds((T, ATT_Q_W), BF16), _sds((T, 2 * ATT_KV_W), BF16), _sds((SUBLANES, 128), F32),
                   _sds((1, ATT_Q_W), F32), _sds((1, 2 * ATT_KV_W), F32)],
        scratch_shapes=[pltpu.VMEM((B, 2 * ATT_KV_W), F32)],
        compiler_params=_cp(("arbitrary",)),
    )(sinks, att, att, b_attn, dmix)


def _silu_and_grad(x):
    sg = _sigmoid(x)
    return x * sg, sg * (1.0 + x * (1.0 - sg))


def _mix_fwd_fn(o_raw, hg, o_att, hgw):
    outs = []
    for h in range(HG_HEADS):
        sl = slice(HG_DK * h, HG_DK * (h + 1))
        silu, _ = _silu_and_grad(hg[:, sl])
        outs.append(_rms_fwd(o_raw[:, sl], hgw) * silu)
    outs.append(o_att)
    return (jnp.concatenate(outs, axis=1),)


def _mix_bwd_fn(o_raw, hg, dmix, hgw):
    dos, dhgs = [], []
    dw = jnp.zeros((1, HG_DK), F32)
    for h in range(HG_HEADS):
        sl = slice(HG_DK * h, HG_DK * (h + 1))
        silu, dsilu = _silu_and_grad(hg[:, sl])
        dy = dmix[:, sl]
        dhgs.append(dy * _rms_fwd(o_raw[:, sl], hgw) * dsilu)
        dx, dwh = _rms_bwd(o_raw[:, sl], hgw, dy * silu)
        dos.append(dx)
        dw = dw + dwh
    return jnp.concatenate(dos, axis=1), jnp.concatenate(dhgs, axis=1), dw


def _final_fn(h2, tgt, wf):
    d = h2.shape[1]
    err = _rms_fwd(h2, wf) - tgt
    loss_cols = (0.5 / d) * jnp.sum(err * err, axis=0, keepdims=True)
    dh2, dwf = _rms_bwd(h2, wf, err * (1.0 / d))
    return dh2, dh2, loss_cols, dwf


class _NoExchange:
    def __init__(self, weights):
        self.weights = weights

    def start(self):
        return None

    def w_in(self, after):
        return self.weights["w_in_t"]

    def mid(self, after):
        return None

    def w_out(self, after):
        return {k: self.weights[k] for k in ("w_out", "conv_w8")}

    def rest(self, after):
        return {k: self.weights[k] for k in ("w_gate_t", "w_up_t", "w_down")}

    def ffn_grads(self, gs):
        return None

    def ffn_grads_send(self, after):
        return None

    def small_grads(self, loss_cols, g):
        return None

    def late_grads(self, g_in):
        return None


def _local_step(x, tgt, p, ex):
    T, D = x.shape
    row = lambda n, dt: _sds((T, n), dt)
    acc = lambda n: _sds((1, n), F32)

    (u,) = _rowwise(lambda xv, w: (_rms_fwd(xv, w),), [_full(x)], [p["norm_mix_w"]], [row(D, BF16)], [], name="rms_mix",
                    after=ex.start())
    p = dict(p, w_in_t=ex.w_in(u))
    hq, hf, hi, hg, att = _mm_nt(u, p["w_in_t"], splits=[HG_W] * 4 + [ATT_COLS], out_dtype=F32, name="in_proj")
    o_raw, states = _hgrn_fwd(hq, hf, hi, p["lb"], name="hgrn_fwd")
    o_att = _attn_fwd(att, p["b_attn"], p["sinks"], name="attn_fwd", after=ex.mid(o_raw))
    p = dict(p, **ex.w_out(o_att))
    def out_epilogue(prod, xv, w):
        h1v = prod + xv
        return h1v, _rms_fwd(h1v, w)

    h1, v, mix = _mm_nn(None, [p["w_out"]], name="mix_out_proj",
                        prologue=(lambda *a: _mix_fwd_fn(*a)[0], [o_raw, hg, o_att], [p["hg_norm_w"]], row(D, BF16)),
                        epilogue=(out_epilogue, [x], [p["norm_ffn_w"]], [row(D, F32), row(D, BF16)], []))
    p = dict(p, **ex.rest(v))
    gp, up, gate, act = _ffn_in(v, p["w_gate_t"], p["w_up_t"], p["conv_w8"], p["conv_b"], name="ffn_in")
    def down_epilogue(prod, h1v, tgtv, wf):
        return _final_fn(prod + h1v, tgtv, wf)

    dh2, dh2_b, loss_cols, d_final = _mm_nn(
        [[act]], [p["w_down"]], name="down_proj_loss",
        epilogue=(down_epilogue, [h1, tgt], [p["final_norm_w"]], [row(D, F32), row(D, BF16)], [acc(D), acc(D)]))

    g_down = _mm_tn([act], dh2_b, name="g_down")
    dgp, dup, d_conv_w8, d_conv_b = _ffn_back(dh2_b, p["w_down"], gp, up, gate, p["conv_w8"], name="ffn_back")
    g_gate_t = _mm_tn([dgp], v, name="g_gate")
    g_up_t = _mm_tn([dup], v, name="g_up")
    swapping = ex.ffn_grads([g_gate_t, g_up_t, g_down])

    def ffn_norm_bwd(dvv, hv, dh2v, w):
        dx, dw = _rms_bwd(hv, w, dvv)
        dh1v = dx + dh2v
        return dh1v, dh1v, dw

    dh1, dh1_b, d_norm_ffn = _mm_nn(
        [[dgp], [dup]], [p["w_gate_t"], p["w_up_t"]], name="d_v_norm", after=swapping,
        epilogue=(ffn_norm_bwd, [h1, dh2], [p["norm_ffn_w"]], [row(D, F32), row(D, BF16)], [acc(D)]))
    sent = ex.ffn_grads_send(dh1_b)
    def mix_bwd(dmixv, o_rawv, hgv, hgw):
        do_rawv, dhgv, dw = _mix_bwd_fn(o_rawv, hgv, dmixv[:, :HG_W], hgw)
        return do_rawv, dhgv, dmixv[:, HG_W:], dw

    do_raw, dhg, do_att, d_hg_norm = _mm_nn(
        [[dh1_b]], [p["w_out"]], name="d_mix_bwd", w_transposed=True, after=sent,
        epilogue=(mix_bwd, [o_raw, hg], [p["hg_norm_w"]], [row(HG_W, F32), row(HG_W, BF16), row(ATT_Q_W, F32)], [acc(HG_DK)]))
    daq, dakv, d_sinks8, d_bq, d_bkv = _attn_bwd(att, p["b_attn"], p["sinks"], do_att, name="attn_bwd")
    dhq, dhf, dhi, d_lb = _hgrn_bwd(hq, hf, hi, p["lb"], states, do_raw, name="hgrn_bwd")
    pieces = [dhq, dhf, dhi, dhg, daq, dakv]
    def mix_norm_bwd(duv, xv, dh1v, w):
        dx, dw = _rms_bwd(xv, w, duv)
        return dx + dh1v, dw

    dx, d_norm_mix = _mm_nn([pieces], [p["w_in_t"]], name="d_u_norm",
                            epilogue=(mix_norm_bwd, [x, dh1], [p["norm_mix_w"]], [row(D, F32)], [acc(D)]))
    grads = dict(g_gate_t=g_gate_t, g_up_t=g_up_t, g_down=g_down,
                 norm_mix_w=d_norm_mix, b_attn=jnp.concatenate([d_bq, d_bkv], axis=1), lb=d_lb, hg_norm_w=d_hg_norm,
                 sinks8=d_sinks8, norm_ffn_w=d_norm_ffn, conv_w8=d_conv_w8, conv_b=d_conv_b, final_norm_w=d_final)
    g_in_t = _mm_tn(pieces, u, name="g_in", after=ex.small_grads(loss_cols, grads))
    g_out = _mm_tn([mix], dh1_b, name="g_out", after=ex.late_grads(g_in_t))
    return loss_cols, dx, dict(grads, g_in_t=g_in_t, g_out=g_out)


SLAB = (IN_COLS // N_CHIPS, D_FF // N_CHIPS, D_FF // N_CHIPS, D_FF // N_CHIPS, D_MODEL // N_CHIPS)
N_W = len(SLAB)
PACK_OFF = tuple(sum(SLAB[:i]) for i in range(N_W))
PACK_ROWS = sum(SLAB)
FULL_OFF = tuple(N_CHIPS * o for o in PACK_OFF)
FULL_ROWS = N_CHIPS * PACK_ROWS
HALF = tuple(s // 2 for s in SLAB)
HPACK_OFF = tuple(sum(HALF[:i]) for i in range(N_W))
HPACK_ROWS = sum(HALF)
HFULL_OFF = tuple(N_CHIPS * o for o in HPACK_OFF)
HFULL_ROWS = N_CHIPS * HPACK_ROWS
CHIP_FLIPS = ((1, 0), (0, 1), (1, 1))
N_DEV = 8
BF16_ROWS = 16
ANY = pl.BlockSpec(memory_space=pl.ANY)


def _pos():
    return lax.axis_index("x"), lax.axis_index("y"), lax.axis_index("c")


def _flip(v, f):
    return 1 - v if f else v


def _rcopy(src, dst, ssem, rsem, dev):
    return pltpu.make_async_remote_copy(src_ref=src, dst_ref=dst, send_sem=ssem, recv_sem=rsem, device_id=dev,
                                        device_id_type=pl.DeviceIdType.MESH)


def _rows(ref, start, n, align=None):
    if not isinstance(start, int):
        if align is None:
            align = SUBLANES * (4 // jnp.dtype(ref.dtype).itemsize)
        start = pl.multiple_of(start, align)
    return ref.at[pl.ds(start, n), :]


FFN_W = (1, 2, 3)
N_PEER = 1 + len(CHIP_FLIPS)
HBM = pl.BlockSpec(memory_space=pltpu.HBM)
SEM = pl.BlockSpec(memory_space=pltpu.SEMAPHORE)
EFFECT = pltpu.SideEffectType.DATAFLOW_SIDE_EFFECTING
LANES = 128


def _sent_rows(k, w, c):
    return (0, SLAB[w]) if k == 0 else (c * HALF[w], HALF[w])


def _gather_start(pack, cw8):
    D = pack.shape[1]
    lands = [lax.empty((N_CHIPS * SLAB[0], D), pack.dtype), lax.empty((3 * N_CHIPS * SLAB[1], D), pack.dtype),
             lax.empty((N_CHIPS * SLAB[4], D), pack.dtype), lax.empty((N_CHIPS,) + cw8.shape, cw8.dtype)]
    bufs = [pack, cw8] + lands

    def body(pack_ref, cw_ref, l_in, l_ffn, l_out, l_cw, *rest):
        in_send, in_recv, out_send, out_recv, ffn_send, ffn_recv = rest[:6]
        token = rest[-1]
        x, y, c = _pos()
        q = 2 * x + y
        peers = _gather_peers(x, y, c)

        def send(k, peer, w, land, base, ssem, rsem):
            r0, n = _sent_rows(k, w, c)
            _rcopy(_rows(pack_ref, PACK_OFF[w] + r0, n), _rows(land, base + q * SLAB[w] + r0, n), ssem, rsem, peer).start()

        for k, peer in enumerate(peers):
            send(k, peer, 0, l_in, 0, in_send.at[k], in_recv.at[k])
        for k, peer in enumerate(peers):
            send(k, peer, 4, l_out, 0, out_send.at[k], out_recv.at[k])
            _rcopy(cw_ref, l_cw.at[q], out_send.at[N_PEER + k], out_recv.at[N_PEER + k], peer).start()
        for j, w in enumerate(FFN_W):
            for k, peer in enumerate(peers):
                send(k, peer, w, l_ffn, j * N_CHIPS * SLAB[w], ffn_send.at[k], ffn_recv.at[k])
        token[...] = jnp.zeros_like(token)

    n_sem = (N_PEER, N_PEER, 2 * N_PEER, 2 * N_PEER, N_PEER, N_PEER)
    outs = pl.pallas_call(
        body, name="gather_start", in_specs=[HBM] * len(bufs),
        out_specs=[SEM] * len(n_sem) + [HBM] * len(bufs) + [pl.BlockSpec(memory_space=pltpu.VMEM)],
        out_shape=[pltpu.SemaphoreType.DMA((n,)) for n in n_sem]
        + [pltpu.HBM(b.shape, b.dtype) for b in bufs] + [TOKEN],
        input_output_aliases={i: len(n_sem) + i for i in range(len(bufs))},
        compiler_params=pltpu.CompilerParams(has_side_effects=EFFECT),
    )(*[pltpu.with_memory_space_constraint(b, pltpu.HBM) for b in bufs])
    bufs_out = outs[len(n_sem):]
    return dict(in_sems=outs[0:2], out_sems=outs[2:4], ffn_sems=outs[4:6], pack=bufs_out[0], cw=bufs_out[1], l_in=bufs_out[2],
                l_ffn=bufs_out[3], l_out=bufs_out[4], l_cw=bufs_out[5], token=bufs_out[6])


def _gather_peers(x, y, c):
    return [(x, y, 1 - c)] + [(_flip(x, fx), _flip(y, fy), c) for fx, fy in CHIP_FLIPS]


def _gather_wait_in(g, after):
    def body(pack_ref, l_in, send, recv, after_ref, pack_out, l_out):
        for k, peer in enumerate(_gather_peers(*_pos())):
            n = _sent_rows(k, 0, 0)[1]
            cp = _rcopy(_rows(pack_ref, PACK_OFF[0], n), _rows(l_in, 0, n), send.at[k], recv.at[k], peer)
            cp.wait_send()
            cp.wait_recv()

    return pl.pallas_call(
        body, name="gather_wait_in", in_specs=[HBM, HBM, SEM, SEM, ANY], out_specs=[HBM, HBM],
        out_shape=[pltpu.HBM(g["pack"].shape, g["pack"].dtype), pltpu.HBM(g["l_in"].shape, g["l_in"].dtype)],
        input_output_aliases={0: 0, 1: 1}, compiler_params=pltpu.CompilerParams(has_side_effects=EFFECT),
    )(g["pack"], g["l_in"], *g["in_sems"], after)


def _gather_wait_out(g, pack, after):
    def body(pack_ref, cw_ref, l_out, l_cw, o_send, o_recv, after_ref, o_out, o_cw):
        for k, peer in enumerate(_gather_peers(*_pos())):
            n_out = _sent_rows(k, 4, 0)[1]
            for cp in (_rcopy(_rows(pack_ref, PACK_OFF[4], n_out), _rows(l_out, 0, n_out), o_send.at[k], o_recv.at[k], peer),
                       _rcopy(cw_ref, l_cw.at[0], o_send.at[N_PEER + k], o_recv.at[N_PEER + k], peer)):
                cp.wait_send()
                cp.wait_recv()

    ins = [pack, g["cw"], g["l_out"], g["l_cw"]]
    return pl.pallas_call(
        body, name="gather_wait_out", in_specs=[HBM] * 4 + [SEM] * 2 + [ANY], out_specs=[HBM] * 2,
        out_shape=[pltpu.HBM(b.shape, b.dtype) for b in ins[2:]],
        input_output_aliases={2: 0, 3: 1}, compiler_params=pltpu.CompilerParams(has_side_effects=EFFECT),
    )(*ins, *g["out_sems"], after)


def _gather_wait_ffn(g, pack, after):
    def body(pack_ref, l_ffn, f_send, f_recv, after_ref, o_ffn):
        for k, peer in enumerate(_gather_peers(*_pos())):
            n_ffn = len(FFN_W) * _sent_rows(k, FFN_W[0], 0)[1]
            cp = _rcopy(_rows(pack_ref, PACK_OFF[FFN_W[0]], n_ffn), _rows(l_ffn, 0, n_ffn), f_send.at[k], f_recv.at[k], peer)
            cp.wait_send()
            cp.wait_recv()

    return pl.pallas_call(
        body, name="gather_wait_ffn", in_specs=[HBM] * 2 + [SEM] * 2 + [ANY], out_specs=HBM,
        out_shape=pltpu.HBM(g["l_ffn"].shape, g["l_ffn"].dtype),
        input_output_aliases={1: 0}, compiler_params=pltpu.CompilerParams(has_side_effects=EFFECT),
    )(pack, g["l_ffn"], *g["ffn_sems"], after)


FWD_IN = ((0, 0, 0),)
FWD_OUT = ((0, 4, 0),)
FWD_FFN = tuple((0, w, j * N_CHIPS * SLAB[w]) for j, w in enumerate(FFN_W))


def _forward_copies(layout, src, dst, send_sems, recv_sems):
    x, y, c = _pos()
    sib = (x, y, 1 - c)
    cps = []
    for fx, fy in CHIP_FLIPS:
        qa = 2 * _flip(x, fx) + _flip(y, fy)
        for bi, w, base in layout:
            r0 = base + qa * SLAB[w] + c * HALF[w]
            cps.append(_rcopy(_rows(src[bi], r0, HALF[w]), _rows(dst[bi], r0, HALF[w]),
                              send_sems.at[len(cps)], recv_sems.at[len(cps)], sib))
    return cps


def _forward_in(l_in):
    n = len(CHIP_FLIPS) * len(FWD_IN)

    def body(in_ref, out_ref, send_sems, recv_sems):
        cps = _forward_copies(FWD_IN, [in_ref], [out_ref], send_sems, recv_sems)
        for cp in cps:
            cp.start()
        for cp in cps:
            cp.wait_recv()
        for cp in cps:
            cp.wait_send()

    return pl.pallas_call(
        body, name="forward_in", in_specs=[ANY], out_specs=ANY, out_shape=_sds(l_in.shape, l_in.dtype),
        input_output_aliases={0: 0},
        scratch_shapes=[pltpu.SemaphoreType.DMA((n,)), pltpu.SemaphoreType.DMA((n,))],
    )(l_in)


def _forward_start(layout, land, *, name):
    n = len(CHIP_FLIPS) * len(layout)

    def body(in_ref, send_sems, recv_sems, out_ref, token):
        for cp in _forward_copies(layout, [in_ref], [in_ref], send_sems, recv_sems):
            cp.start()
        token[...] = jnp.zeros_like(token)

    outs = pl.pallas_call(
        body, name=name, in_specs=[HBM],
        out_specs=[SEM, SEM, HBM, pl.BlockSpec(memory_space=pltpu.VMEM)],
        out_shape=[pltpu.SemaphoreType.DMA((n,)), pltpu.SemaphoreType.DMA((n,)), pltpu.HBM(land.shape, land.dtype), TOKEN],
        input_output_aliases={0: 2}, compiler_params=pltpu.CompilerParams(has_side_effects=EFFECT),
    )(pltpu.with_memory_space_constraint(land, pltpu.HBM))
    return dict(sems=outs[0:2], land=outs[2], token=outs[3])


def _forward_wait(layout, s, after, *, name):
    def body(in_ref, send_sems, recv_sems, after_ref, out_ref):
        for cp in _forward_copies(layout, [in_ref], [in_ref], send_sems, recv_sems):
            cp.wait_send()
            cp.wait_recv()

    return pl.pallas_call(
        body, name=name, in_specs=[HBM, SEM, SEM, ANY], out_specs=HBM,
        out_shape=pltpu.HBM(s["land"].shape, s["land"].dtype),
        input_output_aliases={0: 0}, compiler_params=pltpu.CompilerParams(has_side_effects=EFFECT),
    )(s["land"], *s["sems"], after)


def _exchange_halves(ws, gs, small, *, name):
    D = gs[0].shape[1]
    n = len(ws)
    has_small = small is not None

    def body(*refs):
        g = refs[:n]
        t = refs[n + has_small:2 * n + has_small]
        sems = refs[2 * n + 2 * has_small:]
        d2d_send, d2d_recv = sems[0], sems[1]
        x, y, c = _pos()
        sib = (x, y, 1 - c)
        drains = []
        for i, w in enumerate(ws):
            h = HALF[w]
            for qq in range(N_CHIPS):
                _rcopy(_rows(g[i], qq * SLAB[w] + (1 - c) * h, h), _rows(t[i], qq * h, h),
                       d2d_send.at[i], d2d_recv.at[i], sib).start()
            drains.append(_rcopy(t[i], t[i], d2d_send.at[i], d2d_recv.at[i], sib))
        if has_small:
            small_ref, sall_ref = refs[n], refs[2 * n + 1]
            sm_send, sm_recv, loc_sem = sems[2], sems[3], sems[4]
            me = 4 * x + 2 * y + c
            own_small = pltpu.make_async_copy(small_ref, sall_ref.at[me], loc_sem)
            own_small.start()
            for f in range(1, N_DEV):
                peer = (_flip(x, f & 4), _flip(y, f & 2), _flip(c, f & 1))
                cp = _rcopy(small_ref, sall_ref.at[me], sm_send.at[f - 1], sm_recv.at[f - 1], peer)
                cp.start()
                drains.append(cp)
        for d in drains:
            d.wait_recv()
        for d in drains:
            d.wait_send()
        if has_small:
            own_small.wait()

    out_shape = [_sds((N_CHIPS * HALF[w], D), gs[0].dtype) for w in ws]
    scratch = [pltpu.SemaphoreType.DMA((n,)), pltpu.SemaphoreType.DMA((n,))]
    if has_small:
        out_shape.append(_sds((N_DEV,) + small.shape, F32))
        scratch += [pltpu.SemaphoreType.DMA((N_DEV - 1,)), pltpu.SemaphoreType.DMA((N_DEV - 1,)), pltpu.SemaphoreType.DMA]
    return pl.pallas_call(
        body, name=name, in_specs=[ANY] * (n + has_small), out_specs=[ANY] * (n + has_small),
        out_shape=out_shape, scratch_shapes=scratch,
    )(*gs, *([small] if has_small else []))


def _halves_copies(ws, g, t, send_sems, recv_sems):
    x, y, c = _pos()
    sib = (x, y, 1 - c)
    cps = []
    for i, w in enumerate(ws):
        h = HALF[w]
        for qq in range(N_CHIPS):
            cps.append(_rcopy(_rows(g[i], qq * SLAB[w] + (1 - c) * h, h), _rows(t[i], qq * h, h),
                              send_sems.at[N_CHIPS * i + qq], recv_sems.at[N_CHIPS * i + qq], sib))
    return cps


def _halves_start(ws, gs, *, name):
    D = gs[0].shape[1]
    n = len(ws)
    bufs = list(gs) + [lax.empty((N_CHIPS * HALF[w], D), gs[0].dtype) for w in ws]

    def body(*refs):
        for cp in _halves_copies(ws, refs[:n], refs[n:2 * n], refs[2 * n], refs[2 * n + 1]):
            cp.start()
        refs[-1][...] = jnp.zeros_like(refs[-1])

    outs = pl.pallas_call(
        body, name=name, in_specs=[HBM] * (2 * n),
        out_specs=[SEM, SEM] + [HBM] * (2 * n) + [pl.BlockSpec(memory_space=pltpu.VMEM)],
        out_shape=[pltpu.SemaphoreType.DMA((N_CHIPS * n,)), pltpu.SemaphoreType.DMA((N_CHIPS * n,))]
        + [pltpu.HBM(b.shape, b.dtype) for b in bufs] + [TOKEN],
        input_output_aliases={i: 2 + i for i in range(2 * n)},
        compiler_params=pltpu.CompilerParams(has_side_effects=EFFECT),
    )(*[pltpu.with_memory_space_constraint(b, pltpu.HBM) for b in bufs])
    return dict(sems=outs[0:2], gs=outs[2:2 + n], theirs=outs[2 + n:2 + 2 * n], token=outs[-1])


def _halves_wait(ws, s, after, *, name):
    n = len(ws)

    def body(*refs):
        for cp in _halves_copies(ws, refs[:n], refs[n:2 * n], refs[2 * n], refs[2 * n + 1]):
            cp.wait_send()
            cp.wait_recv()

    bufs = list(s["gs"]) + list(s["theirs"])
    outs = pl.pallas_call(
        body, name=name, in_specs=[HBM] * (2 * n) + [SEM, SEM, ANY], out_specs=[HBM] * (2 * n),
        out_shape=[pltpu.HBM(b.shape, b.dtype) for b in bufs],
        input_output_aliases={i: i for i in range(2 * n)},
        compiler_params=pltpu.CompilerParams(has_side_effects=EFFECT),
    )(*bufs, *s["sems"], after)
    return outs[:n], outs[n:]


def _small_peers():
    x, y, c = _pos()
    return 4 * x + 2 * y + c, [(_flip(x, f & 4), _flip(y, f & 2), _flip(c, f & 1)) for f in range(1, N_DEV)]


def _small_start(small):
    sall = lax.empty((N_DEV,) + small.shape, F32)

    def body(small_ref, sall_ref, sm_send, sm_recv, small_out, sall_out, token, loc_sem):
        me, peers = _small_peers()
        own = pltpu.make_async_copy(small_ref, sall_ref.at[me], loc_sem)
        own.start()
        for k, peer in enumerate(peers):
            _rcopy(small_ref, sall_ref.at[me], sm_send.at[k], sm_recv.at[k], peer).start()
        own.wait()
        token[...] = jnp.zeros_like(token)

    outs = pl.pallas_call(
        body, name="small_start", in_specs=[HBM] * 2,
        out_specs=[SEM, SEM, HBM, HBM, pl.BlockSpec(memory_space=pltpu.VMEM)],
        out_shape=[pltpu.SemaphoreType.DMA((N_DEV - 1,)), pltpu.SemaphoreType.DMA((N_DEV - 1,)),
                   pltpu.HBM(small.shape, F32), pltpu.HBM(sall.shape, F32), TOKEN],
        input_output_aliases={0: 2, 1: 3}, scratch_shapes=[pltpu.SemaphoreType.DMA],
        compiler_params=pltpu.CompilerParams(has_side_effects=EFFECT),
    )(pltpu.with_memory_space_constraint(small, pltpu.HBM), pltpu.with_memory_space_constraint(sall, pltpu.HBM))
    return dict(sems=outs[0:2], small=outs[2], sall=outs[3], token=outs[4])


def _small_wait(s, after):
    def body(small_ref, sall_ref, sm_send, sm_recv, after_ref, small_out, sall_out):
        me, peers = _small_peers()
        for k, peer in enumerate(peers):
            cp = _rcopy(small_ref, sall_ref.at[me], sm_send.at[k], sm_recv.at[k], peer)
            cp.wait_send()
            cp.wait_recv()

    outs = pl.pallas_call(
        body, name="small_wait", in_specs=[HBM, HBM, SEM, SEM, ANY], out_specs=[HBM, HBM],
        out_shape=[pltpu.HBM(s["small"].shape, F32), pltpu.HBM(s["sall"].shape, F32)],
        input_output_aliases={0: 0, 1: 1}, compiler_params=pltpu.CompilerParams(has_side_effects=EFFECT),
    )(s["small"], s["sall"], *s["sems"], after)
    return outs[1]


REDUCE_SPLIT = 2


def _chip_partial(ws, gs, theirs, *, name, out_dtype=F32):
    D = gs[0].shape[1]
    n = len(ws)

    def body(*refs):
        for i in range(n):
            refs[2 * n + i][...] = (refs[i][...].astype(F32) + refs[n + i][...].astype(F32)).astype(out_dtype)

    blk = [HALF[w] // REDUCE_SPLIT for w in ws]
    mine = [pl.BlockSpec((b, D), lambda qq, j: ((2 * qq + lax.axis_index("c")) * REDUCE_SPLIT + j, 0)) for b in blk]
    flat = [pl.BlockSpec((b, D), lambda qq, j: (qq * REDUCE_SPLIT + j, 0)) for b in blk]
    return pl.pallas_call(
        body, name=name, grid=(N_CHIPS, REDUCE_SPLIT), in_specs=mine + flat, out_specs=flat,
        out_shape=[_sds((N_CHIPS * HALF[w], D), out_dtype) for w in ws],
        compiler_params=_cp(("parallel", "parallel")),
    )(*gs, *theirs)


def _partial_copies(ws, part, got, send_sems, recv_sems):
    x, y, c = _pos()
    cps = []
    for k, (fx, fy) in enumerate(CHIP_FLIPS):
        peer = (_flip(x, fx), _flip(y, fy), c)
        qp = 2 * _flip(x, fx) + _flip(y, fy)
        for i, w in enumerate(ws):
            cps.append(_rcopy(_rows(part[i], qp * HALF[w], HALF[w]), _rows(got[i], k * HALF[w], HALF[w]),
                              send_sems.at[len(ws) * k + i], recv_sems.at[len(ws) * k + i], peer))
    return cps


def _send_start(ws, parts, *, name):
    D = parts[0].shape[1]
    n = len(ws)
    bufs = list(parts) + [lax.empty((len(CHIP_FLIPS) * HALF[w], D), parts[0].dtype) for w in ws]

    def body(*refs):
        send_sems, recv_sems = refs[2 * n], refs[2 * n + 1]
        for cp in _partial_copies(ws, refs[:n], refs[n:2 * n], send_sems, recv_sems):
            cp.start()
        refs[-1][...] = jnp.zeros_like(refs[-1])

    outs = pl.pallas_call(
        body, name=name, in_specs=[HBM] * (2 * n),
        out_specs=[SEM, SEM] + [HBM] * (2 * n) + [pl.BlockSpec(memory_space=pltpu.VMEM)],
        out_shape=[pltpu.SemaphoreType.DMA((len(CHIP_FLIPS) * n,)), pltpu.SemaphoreType.DMA((len(CHIP_FLIPS) * n,))]
        + [pltpu.HBM(b.shape, b.dtype) for b in bufs] + [TOKEN],
        input_output_aliases={i: 2 + i for i in range(2 * n)},
        compiler_params=pltpu.CompilerParams(has_side_effects=EFFECT),
    )(*[pltpu.with_memory_space_constraint(b, pltpu.HBM) for b in bufs])
    return dict(sems=outs[0:2], parts=outs[2:2 + n], got=outs[2 + n:2 + 2 * n], token=outs[-1])


def _send_wait(ws, s, after, *, name):
    n = len(ws)

    def body(*refs):
        for cp in _partial_copies(ws, refs[:n], refs[n:2 * n], refs[2 * n], refs[2 * n + 1]):
            cp.wait_send()
            cp.wait_recv()

    bufs = list(s["parts"]) + list(s["got"])
    outs = pl.pallas_call(
        body, name=name, in_specs=[HBM] * (2 * n) + [SEM, SEM] + [ANY] * len(after), out_specs=[HBM] * (2 * n),
        out_shape=[pltpu.HBM(b.shape, b.dtype) for b in bufs],
        input_output_aliases={i: i for i in range(2 * n)},
        compiler_params=pltpu.CompilerParams(has_side_effects=EFFECT),
    )(*bufs, *s["sems"], *after)
    return outs[:n], outs[n:]


def _chip_reduce(ws, parts, got, *, name, after=None):
    D = parts[0].shape[1]
    nk = len(CHIP_FLIPS)
    n = len(ws)
    extra = [] if after is None else [after]

    def body(*refs):
        refs = refs[len(extra):]
        outs = refs[(1 + nk) * n:]
        for i in range(n):
            acc = refs[i][...].astype(F32)
            for k in range(nk):
                acc = acc + refs[n * (1 + k) + i][...].astype(F32)
            outs[i][...] = acc

    blk = [HALF[w] // REDUCE_SPLIT for w in ws]

    def q_idx(j):
        return (2 * lax.axis_index("x") + lax.axis_index("y")) * REDUCE_SPLIT + j

    in_specs = [pl.BlockSpec((b, D), lambda j: (q_idx(j), 0)) for b in blk]
    for k in range(nk):
        in_specs += [pl.BlockSpec((b, D), functools.partial(lambda j, k: (k * REDUCE_SPLIT + j, 0), k=k)) for b in blk]
    out_specs = [pl.BlockSpec((b, D), lambda j: (lax.axis_index("c") * REDUCE_SPLIT + j, 0)) for b in blk]
    return pl.pallas_call(
        body, name=name, grid=(REDUCE_SPLIT,), in_specs=[ANY] * len(extra) + in_specs, out_specs=out_specs,
        out_shape=[_sds((SLAB[w], D), F32) for w in ws],
        compiler_params=_cp(("parallel",)),
    )(*extra, *parts, *[g for _ in range(nk) for g in got])


def _exchange_reduced(ws, shards, *, name):
    n = len(ws)

    def body(*refs):
        ins, outs = refs[:n], refs[n:2 * n]
        send_sems, recv_sems = refs[2 * n], refs[2 * n + 1]
        x, y, c = _pos()
        sib = (x, y, 1 - c)
        cps = []
        for i, w in enumerate(ws):
            cp = _rcopy(_rows(ins[i], c * HALF[w], HALF[w]), _rows(outs[i], c * HALF[w], HALF[w]),
                        send_sems.at[i], recv_sems.at[i], sib)
            cp.start()
            cps.append(cp)
        for cp in cps:
            cp.wait_recv()
        for cp in cps:
            cp.wait_send()

    return pl.pallas_call(
        body, name=name, in_specs=[ANY] * n, out_specs=[ANY] * n,
        out_shape=[_sds(s.shape, s.dtype) for s in shards], input_output_aliases={i: i for i in range(n)},
        scratch_shapes=[pltpu.SemaphoreType.DMA((n,)), pltpu.SemaphoreType.DMA((n,))],
    )(*shards)


def _adamw_fn(w, g, m, v):
    m2 = ADAM_B1 * m + (1.0 - ADAM_B1) * g
    v2 = ADAM_B2 * v + (1.0 - ADAM_B2) * (g * g)
    m_hat = m2 / (1.0 - ADAM_B1 ** ADAM_STEP)
    v_hat = v2 / (1.0 - ADAM_B2 ** ADAM_STEP)
    return -ADAM_LR * (m_hat / (jnp.sqrt(v_hat) + ADAM_EPS) + ADAM_WD * w), m2, v2


def _adamw(w, g, m, v, *, name):
    shp = _sds(w.shape, F32)
    rows = w.shape[0]
    tm = max(t for t in range(SUBLANES, 512 + 1, SUBLANES) if rows % t == 0)
    return _rowwise(lambda wv, gv, mv, vv: (gv, *_adamw_fn(wv, gv, mv, vv)), [_full(w), _full(g), _full(m), _full(v)], [],
                    [shp] * 4, [], name=name, tm=tm)


SMALL_SEGS = (("loss", 8), ("norm_mix_w", 8), ("b_attn", 8), ("lb_logits", 8), ("hg_norm_w", 8), ("sinks", 8),
              ("norm_ffn_w", 8), ("conv_w", 72), ("conv_b", 24), ("final_norm_w", 8))
SMALL_OFF = {n: sum(r for _, r in SMALL_SEGS[:i]) for i, (n, _) in enumerate(SMALL_SEGS)}
SMALL_ROWS = sum(r for _, r in SMALL_SEGS)


def _pack_small(parts):
    segs = []
    for n, r in SMALL_SEGS:
        a = parts.get(n)
        flat = jnp.zeros((0,), F32) if a is None else a.reshape(-1).astype(F32)
        segs.append(jnp.pad(flat, (0, r * LANES - flat.shape[0])).reshape(r, LANES))
    return jnp.concatenate(segs, axis=0)


def _unpack_small(pack, n, shape):
    size = math.prod(shape)
    r0 = SMALL_OFF[n]
    return pack[r0:r0 + dict(SMALL_SEGS)[n]].reshape(-1)[:size].reshape(shape)


def _small_update(sall, wp, mp, vp, *, after):
    R = SMALL_ROWS
    r_lb = SMALL_OFF["lb_logits"]

    def body(after_ref, s_ref, w_ref, m_ref, v_ref, g_ref, d_ref, m2_ref, v2_ref, loss_ref):
        g = s_ref[0]
        for i in range(1, N_DEV):
            g = g + s_ref[i]
        tot = jnp.sum(jnp.sum(g[0:8], axis=1, keepdims=True), axis=0, keepdims=True)
        loss_ref[...] = jnp.broadcast_to(tot, loss_ref.shape)
        lg = w_ref[r_lb:r_lb + 8, :]
        p0 = _sigmoid(lg - pltpu.roll(lg, 4, 0))
        d = g[r_lb:r_lb + 8]
        d = d + pltpu.roll(d, 4, 0)
        sign = jnp.where(lax.broadcasted_iota(jnp.int32, d.shape, 0) < 4, 1.0, -1.0)
        g = jnp.concatenate([g[:r_lb], sign * d * p0 * (1.0 - p0), g[r_lb + 8:]], axis=0)
        g_ref[...] = g
        d_ref[...], m2_ref[...], v2_ref[...] = _adamw_fn(w_ref[...], g, m_ref[...], v_ref[...])

    full = pl.BlockSpec((R, LANES), lambda: (0, 0))
    return pl.pallas_call(
        body, name="small_update",
        in_specs=[ANY, pl.BlockSpec((N_DEV, R, LANES), lambda: (0, 0, 0)), full, full, full],
        out_specs=[full, full, full, full, pl.BlockSpec((8, LANES), lambda: (0, 0))],
        out_shape=[_sds((R, LANES), F32)] * 4 + [_sds((8, LANES), F32)],
        compiler_params=_cp(),
    )(after, sall, wp, mp, vp)


def _lb_fwd(lb_logits):
    n = lb_logits.shape[1]

    def body(l_ref, o_ref):
        o_ref[...] = _sigmoid(l_ref[0:1, :] - l_ref[1:2, :])

    return pl.pallas_call(body, name="lb_fwd", out_shape=jax.ShapeDtypeStruct((1, n), F32), compiler_params=_cp())(lb_logits)


class _MeshExchange:
    def __init__(self, pack, cw8):
        self.gather = _gather_start(pack, cw8)
        self.sent = None
        self.conv_w8 = None

    def start(self):
        return self.gather["token"]

    def w_in(self, after):
        self.pack, l_in = _gather_wait_in(self.gather, after)
        return (_forward_in(l_in), N_CHIPS * SLAB[0], 0)

    def mid(self, after):
        l_out, l_cw = _gather_wait_out(self.gather, self.pack, after)
        self.conv_w8 = jnp.concatenate([l_cw[i] for i in range(N_CHIPS)], axis=1)
        self.passing_out = _forward_start(FWD_OUT, l_out, name="forward_out_start")
        return self.passing_out["token"]

    def w_out(self, after):
        l_ffn = _gather_wait_ffn(self.gather, self.pack, after)
        self.passing_ffn = _forward_start(FWD_FFN, l_ffn, name="forward_ffn_start")
        l_out = _forward_wait(FWD_OUT, self.passing_out, self.passing_ffn["token"], name="forward_out_wait")
        return dict(w_out=(l_out, N_CHIPS * SLAB[4], 0), conv_w8=self.conv_w8)

    def rest(self, after):
        l_ffn = _forward_wait(FWD_FFN, self.passing_ffn, after, name="forward_ffn_wait")
        rows = N_CHIPS * SLAB[FFN_W[0]]
        return dict(w_gate_t=(l_ffn, rows, 0), w_up_t=(l_ffn, rows, 1), w_down=(l_ffn, rows, 2))

    def ffn_grads(self, gs):
        self.swap = _halves_start(FFN_W, gs, name="halves_ffn_start")
        return self.swap["token"]

    def small_grads(self, loss_cols, g):
        small = _pack_small(dict(loss=loss_cols, norm_mix_w=g["norm_mix_w"], b_attn=g["b_attn"], lb_logits=g["lb"],
                                 hg_norm_w=g["hg_norm_w"], sinks=g["sinks8"], norm_ffn_w=g["norm_ffn_w"],
                                 conv_w=g["conv_w8"][:3], conv_b=g["conv_b"], final_norm_w=g["final_norm_w"]))
        self.small_sent = _small_start(small)
        return self.small_sent["token"]

    def late_grads(self, g_in):
        self.swap_in = _halves_start((0,), [g_in], name="halves_in_start")
        return self.swap_in["token"]

    def ffn_grads_send(self, after):
        gs, theirs = _halves_wait(FFN_W, self.swap, after, name="halves_ffn_wait")
        parts = _chip_partial(FFN_W, gs, theirs, name="chip_partial_ffn", out_dtype=BF16)
        self.sent = _send_start(FFN_W, parts, name="send_ffn_start")
        return self.sent["token"]


def kernel(x, norm_mix_w, w_in, b_attn, lb_logits, hg_norm_w, sinks, w_out, norm_ffn_w, w_gate, w_up, conv_w, conv_b, w_down, final_norm_w, loss_target, m_norm_mix_w, m_w_in, m_b_attn, m_lb_logits, m_hg_norm_w, m_sinks, m_w_out, m_norm_ffn_w, m_w_gate, m_w_up, m_conv_w, m_conv_b, m_w_down, m_final_norm_w, v_norm_mix_w, v_w_in, v_b_attn, v_lb_logits, v_hg_norm_w, v_sinks, v_w_out, v_norm_ffn_w, v_w_gate, v_w_up, v_conv_w, v_conv_b, v_w_down, v_final_norm_w):
    D = D_MODEL
    q = 2 * lax.axis_index("x") + lax.axis_index("y")
    ccols = D_FF // N_CHIPS

    pack = jnp.concatenate([w_in[0].T, w_gate[0].T, w_up[0].T, w_down[0], w_out[0]], axis=0).astype(BF16)
    cw8 = jnp.concatenate([conv_w[0], jnp.zeros((SUBLANES - 3, ccols), F32)], axis=0)
    ex = _MeshExchange(pack, cw8)
    p = dict(norm_mix_w=norm_mix_w, b_attn=b_attn, lb=_lb_fwd(lb_logits), hg_norm_w=hg_norm_w, sinks=sinks,
             norm_ffn_w=norm_ffn_w, conv_b=conv_b, final_norm_w=final_norm_w.reshape(1, D))
    loss_cols, dx, g = _local_step(x[0], loss_target[0], p, ex)
    conv_w8 = ex.conv_w8

    parts_ffn, got_ffn = _send_wait(FFN_W, ex.sent, [dx], name="send_ffn_wait")
    late = (0, 4)
    (g_in_t,), (theirs_in,) = _halves_wait((0,), ex.swap_in, g["g_out"], name="halves_in_wait")
    (theirs_out,) = _exchange_halves((4,), [g["g_out"]], None, name="exchange_halves_late")
    sall = _small_wait(ex.small_sent, theirs_out)
    gs, theirs = [g_in_t, g["g_out"]], [theirs_in, theirs_out]
    parts_late = _chip_partial(late, gs, theirs, name="chip_partial_late", out_dtype=BF16)
    sent_late = _send_start(late, parts_late, name="send_late_start")
    big = {}

    def finish(ws, parts, got, specs, tag, after):
        shards = _exchange_reduced(ws, _chip_reduce(ws, parts, got, name="chip_reduce_" + tag, after=after),
                                   name="exchange_reduced_" + tag)
        deltas = []
        for gw, (n, w, m, v, tr) in zip(shards, specs):
            view = (lambda a: a[0].T) if tr else (lambda a: a[0])
            back = (lambda a: a.T[None]) if tr else (lambda a: a[None])
            res = _adamw(view(w), gw, view(m), view(v), name="adamw_" + n)
            big[n] = tuple(back(r) for r in res)
            deltas.append(res[1])
        return deltas

    done_ffn = finish(FFN_W, parts_ffn, got_ffn, (("w_gate", w_gate, m_w_gate, v_w_gate, True),
                                                  ("w_up", w_up, m_w_up, v_w_up, True),
                                                  ("w_down", w_down, m_w_down, v_w_down, False)), "ffn", sent_late["token"])

    def place(a):
        return lax.dynamic_update_slice(jnp.zeros((3, D_FF), F32), a[0], (0, q * ccols))

    def small_pack(ws, cw):
        nm, ba, lbl, hg, sk, nf, cb, fn = ws
        return _pack_small(dict(norm_mix_w=nm, b_attn=ba, lb_logits=lbl, hg_norm_w=hg,
                                sinks=jnp.broadcast_to(sk.reshape(ATT_HEADS, 1), (ATT_HEADS, LANES)), norm_ffn_w=nf,
                                conv_w=cw, conv_b=cb, final_norm_w=fn))

    wp = small_pack((norm_mix_w, b_attn, lb_logits, hg_norm_w, sinks, norm_ffn_w, conv_b, final_norm_w), conv_w8[:3])
    mp = small_pack((m_norm_mix_w, m_b_attn, m_lb_logits, m_hg_norm_w, m_sinks, m_norm_ffn_w, m_conv_b, m_final_norm_w),
                    place(m_conv_w))
    vp = small_pack((v_norm_mix_w, v_b_attn, v_lb_logits, v_hg_norm_w, v_sinks, v_norm_ffn_w, v_conv_b, v_final_norm_w),
                    place(v_conv_w))
    outs = _small_update(sall, wp, mp, vp, after=sent_late["token"])
    loss = outs[4][0, 0]
    parts_late, got_late = _send_wait(late, sent_late, [*done_ffn, outs[4]], name="send_late_wait")
    finish(late, parts_late, got_late, (("w_in", w_in, m_w_in, v_w_in, True), ("w_out", w_out, m_w_out, v_w_out, False)),
           "late", None)

    def small_out(pk, n, ref):
        if n == "sinks":
            return pk[SMALL_OFF[n]:SMALL_OFF[n] + ATT_HEADS, 0].reshape(ref.shape)
        if n == "conv_w":
            full = _unpack_small(pk, n, (3, D_FF))
            return lax.dynamic_slice(full, (0, q * ccols), (3, ccols))[None]
        return _unpack_small(pk, n, ref.shape)

    refs = dict(norm_mix_w=norm_mix_w, b_attn=b_attn, lb_logits=lb_logits, hg_norm_w=hg_norm_w, sinks=sinks,
                norm_ffn_w=norm_ffn_w, conv_w=conv_w, conv_b=conv_b, final_norm_w=final_norm_w)
    order = ("norm_mix_w", "w_in", "b_attn", "lb_logits", "hg_norm_w", "sinks", "w_out", "norm_ffn_w", "w_gate", "w_up",
             "conv_w", "conv_b", "w_down", "final_norm_w")
    res = [loss, dx[None]]
    for k in range(4):
        for n in order:
            res.append(big[n][k] if n in big else small_out(outs[k], n, refs[n]))
    return tuple(res)
```

```python
import functools
import math

import jax
import jax.numpy as jnp
from jax import lax
from jax.experimental import pallas as pl
from jax.experimental.pallas import tpu as pltpu

F32 = jnp.float32
BF16 = jnp.bfloat16

D_MODEL = 1024
HG_HEADS = 4
HG_DK = 128
HG_W = HG_HEADS * HG_DK
HG_CHUNK = 64
HG_SUB = 8
HG_FWD_CHUNKS_PER_STEP = 8
HG_CHUNKS_PER_STEP = 4
ATT_HEADS = 8
ATT_KV = 2
ATT_GROUP = ATT_HEADS // ATT_KV
ATT_HD = 64
ATT_BLOCK = 128
ATT_Q_W = ATT_HEADS * ATT_HD
ATT_KV_W = ATT_KV * ATT_HD
ATT_COLS = ATT_Q_W + 2 * ATT_KV_W
IN_COLS = 4 * HG_W + ATT_COLS
D_FF = 2816
EPS = 1e-6
ADAM_LR, ADAM_B1, ADAM_B2, ADAM_EPS, ADAM_WD, ADAM_STEP = 0.001, 0.9, 0.999, 1e-08, 0.01, 10
NEG = -1e30

V7X_VMEM_BYTES = 64 * 1024 * 1024
VMEM_LIMIT = 48 * 1024 * 1024
SUBLANES = 8

N_CHIPS = 4


def _cp(sem=None, **kw):
    return pltpu.CompilerParams(dimension_semantics=sem, vmem_limit_bytes=VMEM_LIMIT, **kw)


def _sds(shape, dtype):
    return jax.ShapeDtypeStruct(shape, dtype)


TOKEN = jax.ShapeDtypeStruct((8, 128), jnp.float32)


def _wspec(w):
    arr, rows, blk = w
    return pl.BlockSpec((rows, arr.shape[1]), lambda i: (blk, 0))


def _mm_nt(a, w, *, splits, out_dtype, name, after=None, tm=512):
    M, K = a.shape
    N = w[1]
    tm = min(tm, M)
    assert sum(splits) == N and M % tm == 0
    offs = [sum(splits[:i]) for i in range(len(splits))]
    n_in = 2 if after is None else 3

    def body(*refs):
        a_ref, w_ref = refs[0], refs[1]
        acc = lax.dot_general(a_ref[...], w_ref[...], (((1,), (1,)), ((), ())), preferred_element_type=F32)
        for o_ref, c0, n in zip(refs[n_in:], offs, splits):
            o_ref[...] = acc[:, c0:c0 + n].astype(out_dtype)

    in_specs = [pl.BlockSpec((tm, K), lambda i: (i, 0)), _wspec(w)]
    args = [a, w[0]]
    if after is not None:
        in_specs.append(pl.BlockSpec(memory_space=pl.ANY))
        args.append(after)
    outs = pl.pallas_call(
        body, name=name, grid=(M // tm,), in_specs=in_specs,
        out_specs=[pl.BlockSpec((tm, n), lambda i: (i, 0)) for n in splits],
        out_shape=[_sds((M, n), out_dtype) for n in splits],
        compiler_params=_cp(("parallel",)),
    )(*args)
    return outs


def _mm_nn(pieces, ws, *, name, out_dtype=F32, residual=None, epilogue=None, prologue=None, after=None,
           w_transposed=False, tm=512):
    pro_fn, pro_rows, pro_bc, pro_out = prologue or (None, [], [], None)
    if prologue is not None:
        assert pieces is None and len(ws) == 1
        pieces = [[pro_out]]
    M = pieces[0][0].shape[0]
    K = ws[0][1] if w_transposed else ws[0][0].shape[1]
    tm = min(tm, M)
    flat = [] if prologue is not None else [p for grp in pieces for p in grp]
    n_p = len(flat)
    n_w = len(ws)
    n_pr, n_pb = len(pro_rows), len(pro_bc)
    fn, row_ins, bc_ins, row_outs, acc_outs = epilogue or (None, [], [], [_sds((M, K), out_dtype)], [])
    if residual is not None:
        assert epilogue is None
        row_ins = [residual]
    n_r, n_b, n_o = len(row_ins), len(bc_ins), len(row_outs)
    lead = [] if after is None else [after]

    def body(*refs):
        refs = refs[len(lead):]
        p_refs = refs[:n_p]
        w_refs = refs[n_p:n_p + n_w]
        extra = [r[...] for r in refs[n_p + n_w:n_p + n_w + n_r + n_b]]
        base = n_p + n_w + n_r + n_b
        pro = [r[...] for r in refs[base:base + n_pr + n_pb]]
        base += n_pr + n_pb
        o_refs = refs[base:base + n_o]
        a_refs = refs[base + n_o:base + n_o + len(acc_outs)]
        if pro_fn is not None:
            lhs = pro_fn(*pro).astype(pro_out.dtype)
            refs[-1][...] = lhs
            tiles = [lhs]
        else:
            tiles = [r[...] for r in p_refs]
        acc = None
        k = 0
        for gi, grp in enumerate(pieces):
            c0 = 0
            for p in grp:
                n = p.shape[1]
                if w_transposed:
                    t = lax.dot_general(tiles[k], w_refs[gi][...], (((1,), (1,)), ((), ())), preferred_element_type=F32)
                else:
                    t = jnp.dot(tiles[k], w_refs[gi][c0:c0 + n, :], preferred_element_type=F32)
                acc = t if acc is None else acc + t
                c0 += n
                k += 1
        if fn is None:
            res = (acc + extra[0] if residual is not None else acc,)
        else:
            res = fn(acc, *extra)
        for o_ref, val in zip(o_refs, res[:n_o]):
            o_ref[...] = val.astype(o_ref.dtype)
        if acc_outs:
            @pl.when(pl.program_id(0) == 0)
            def _():
                for a_ref in a_refs:
                    a_ref[...] = jnp.zeros_like(a_ref)
            for a_ref, val in zip(a_refs, res[n_o:]):
                a_ref[...] += val

    in_specs = [pl.BlockSpec((tm, p.shape[1]), lambda i: (i, 0)) for p in flat]
    in_specs += [_wspec(w) for w in ws]
    in_specs += [pl.BlockSpec((tm, r.shape[1]), lambda i: (i, 0)) for r in row_ins]
    in_specs += [pl.BlockSpec(b.shape, lambda i: (0, 0)) for b in bc_ins]
    in_specs += [pl.BlockSpec((tm, r.shape[1]), lambda i: (i, 0)) for r in pro_rows]
    in_specs += [pl.BlockSpec(b.shape, lambda i: (0, 0)) for b in pro_bc]
    out_specs = [pl.BlockSpec((tm, s.shape[1]), lambda i: (i, 0)) for s in row_outs]
    out_specs += [pl.BlockSpec(s.shape, lambda i: (0, 0)) for s in acc_outs]
    pro_outs = [] if prologue is None else [pro_out]
    out_specs += [pl.BlockSpec((tm, s.shape[1]), lambda i: (i, 0)) for s in pro_outs]
    outs = pl.pallas_call(
        body, name=name, grid=(M // tm,), in_specs=[pl.BlockSpec(memory_space=pl.ANY)] * len(lead) + in_specs,
        out_specs=out_specs, out_shape=list(row_outs) + list(acc_outs) + pro_outs,
        compiler_params=_cp(("arbitrary",) if acc_outs else ("parallel",)),
    )(*lead, *flat, *[w[0] for w in ws], *row_ins, *bc_ins, *pro_rows, *pro_bc)
    return outs if (epilogue is not None or prologue is not None) else outs[0]


def _mm_tn(pieces, x, *, name, out_dtype=BF16, tt=1024, after=None):
    M, K = x.shape
    tt = min(tt, M)
    ns = [p.shape[1] for p in pieces]
    offs = [sum(ns[:i]) for i in range(len(ns))]
    N = sum(ns)
    n_p = len(pieces)
    last = M // tt - 1
    lead = [] if after is None else [after]

    def body(*refs):
        refs = refs[len(lead):]
        p_refs = refs[:n_p]
        x_ref = refs[n_p]
        o_ref, acc_ref = refs[n_p + 1], refs[n_p + 2]

        @pl.when(pl.program_id(0) == 0)
        def _():
            acc_ref[...] = jnp.zeros_like(acc_ref)

        xv = x_ref[...]
        for p_ref, c0, n in zip(p_refs, offs, ns):
            acc_ref[c0:c0 + n, :] += lax.dot_general(p_ref[...], xv, (((0,), (0,)), ((), ())),
                                                      preferred_element_type=F32)

        @pl.when(pl.program_id(0) == last)
        def _():
            o_ref[...] = acc_ref[...].astype(o_ref.dtype)

    in_specs = [pl.BlockSpec(memory_space=pl.ANY)] * len(lead) + [pl.BlockSpec((tt, n), lambda i: (i, 0)) for n in ns]
    in_specs.append(pl.BlockSpec((tt, K), lambda i: (i, 0)))
    return pl.pallas_call(
        body, name=name, grid=(M // tt,), in_specs=in_specs,
        out_specs=pl.BlockSpec((N, K), lambda i: (0, 0)),
        out_shape=_sds((N, K), out_dtype),
        scratch_shapes=[pltpu.VMEM((N, K), F32)],
        compiler_params=_cp(("arbitrary",)),
    )(*lead, *pieces, x)


def _rms_fwd(xf, w):
    inv = lax.rsqrt(jnp.mean(xf * xf, axis=-1, keepdims=True) + EPS)
    return xf * inv * w


def _rms_bwd(xf, w, dy):
    inv = lax.rsqrt(jnp.mean(xf * xf, axis=-1, keepdims=True) + EPS)
    xhat = xf * inv
    dxhat = dy * w
    dx = inv * (dxhat - xhat * jnp.mean(dxhat * xhat, axis=-1, keepdims=True))
    dw = jnp.sum(dy * xhat, axis=0, keepdims=True)
    return dx, dw


def _sigmoid(x):
    return 1.0 / (1.0 + jnp.exp(-x))


def _rowwise(fn, row_ins, bc_ins, row_outs, acc_outs, *, name, tm=256, after=None):
    M = row_outs[0].shape[0] if row_outs else row_ins[0][0].shape[0]
    assert M % tm == 0 and tm % SUBLANES == 0, (name, M, tm)
    n_r, n_b, n_o, n_a = len(row_ins), len(bc_ins), len(row_outs), len(acc_outs)
    n_after = 0 if after is None else 1

    def body(*refs):
        refs = refs[n_after:]
        ins = [r[...] for r in refs[:n_r + n_b]]
        o_refs = refs[n_r + n_b:n_r + n_b + n_o]
        a_refs = refs[n_r + n_b + n_o:]
        res = fn(*ins)
        for o_ref, val in zip(o_refs, res[:n_o]):
            o_ref[...] = val.astype(o_ref.dtype)
        if n_a:
            @pl.when(pl.program_id(0) == 0)
            def _():
                for a_ref in a_refs:
                    a_ref[...] = jnp.zeros_like(a_ref)
            for a_ref, val in zip(a_refs, res[n_o:]):
                a_ref[...] += val

    in_specs = [pl.BlockSpec((tm, cw), functools.partial(lambda i, cb, r0: (i + r0, cb), cb=cb, r0=r0))
                for (_, cw, cb, r0) in row_ins]
    in_specs += [pl.BlockSpec(b.shape, lambda i: (0, 0)) for b in bc_ins]
    out_specs = [pl.BlockSpec((tm, s.shape[1]), lambda i: (i, 0)) for s in row_outs]
    out_specs += [pl.BlockSpec(s.shape, lambda i: (0, 0)) for s in acc_outs]
    if n_after:
        in_specs = [pl.BlockSpec(memory_space=pl.ANY)] + in_specs
    return pl.pallas_call(
        body, name=name, grid=(M // tm,), in_specs=in_specs, out_specs=out_specs,
        out_shape=list(row_outs) + list(acc_outs),
        compiler_params=_cp(("arbitrary",) if n_a else ("parallel",)),
    )(*([after] if n_after else []), *[r[0] for r in row_ins], *bc_ins)


def _full(a, first_row_block=0):
    return (a, a.shape[1], 0, first_row_block)


def _conv_rows(ext, w_ref_val):
    s1 = pltpu.roll(ext, 1, 0)
    s2 = pltpu.roll(ext, 2, 0)
    y = w_ref_val[0:1, :] * s2 + w_ref_val[1:2, :] * s1 + w_ref_val[2:3, :] * ext
    return y[SUBLANES:, :]


def _ffn_in(v, w_gate, w_up, conv_w8, conv_b, *, name, tm=256):
    T, K = v.shape
    N = w_gate[1]
    tm = min(tm, T)

    def body(v_ref, wg_ref, wu_ref, cw_ref, cb_ref, gp_ref, up_ref, gate_ref, act_ref, carry_sc):
        @pl.when(pl.program_id(0) == 0)
        def _():
            carry_sc[...] = jnp.zeros_like(carry_sc)

        vv = v_ref[...]
        dn = (((1,), (1,)), ((), ()))
        gp = lax.dot_general(vv, wg_ref[...], dn, preferred_element_type=F32)
        up = lax.dot_general(vv, wu_ref[...], dn, preferred_element_type=F32)
        gp_ref[...] = gp.astype(gp_ref.dtype)
        up_ref[...] = up.astype(up_ref.dtype)
        gate = _conv_rows(jnp.concatenate([carry_sc[...], gp], axis=0), cw_ref[...]) + cb_ref[...]
        gate_ref[...] = gate
        act_ref[...] = (gate * _sigmoid(gate) * up).astype(act_ref.dtype)
        carry_sc[...] = gp[tm - SUBLANES:, :]

    tile = pl.BlockSpec((tm, N), lambda i: (i, 0))
    return pl.pallas_call(
        body, name=name, grid=(T // tm,),
        in_specs=[pl.BlockSpec((tm, K), lambda i: (i, 0)), _wspec(w_gate), _wspec(w_up),
                  pl.BlockSpec((SUBLANES, N), lambda i: (0, 0)), pl.BlockSpec((1, N), lambda i: (0, 0))],
        out_specs=[tile] * 4,
        out_shape=[_sds((T, N), BF16), _sds((T, N), BF16), _sds((T, N), F32), _sds((T, N), BF16)],
        scratch_shapes=[pltpu.VMEM((SUBLANES, N), F32)],
        compiler_params=_cp(("arbitrary",)),
    )(v, w_gate[0], w_up[0], conv_w8, conv_b)


def _ffn_back(dh2, w_down, gp, up, gate, conv_w8, *, name, tr=512, tc=1408):
    T, C = gp.shape
    K = dh2.shape[1]
    warr, _, wblk = w_down
    tr = min(tr, T)
    nr = T // tr
    ncb = C // tc

    def body(dh_ref, wd_ref, gp_ref, up_ref, gate_ref, w_ref, dgp_ref, dup_ref, dw_ref, db_ref, carry_sc):
        @pl.when(pl.program_id(1) == 0)
        def _():
            carry_sc[...] = jnp.zeros_like(carry_sc)
            dw_ref[...] = jnp.zeros_like(dw_ref)
            db_ref[...] = jnp.zeros_like(db_ref)

        w = w_ref[...]
        dact = lax.dot_general(dh_ref[...], wd_ref[...], (((1,), (1,)), ((), ())), preferred_element_type=F32)
        gpc = gp_ref[...].astype(F32)
        gate = gate_ref[...]
        sg = _sigmoid(gate)
        silu = gate * sg
        dup_ref[...] = (dact * silu).astype(dup_ref.dtype)
        dgate = dact * up_ref[...].astype(F32) * (sg + silu * (1.0 - sg))
        ext = jnp.concatenate([dgate, carry_sc[...]], axis=0)
        n = tr + SUBLANES
        g1 = pltpu.roll(ext, n - 1, 0)[:tr]
        g2 = pltpu.roll(ext, n - 2, 0)[:tr]
        dgp_ref[...] = (w[2:3, :] * dgate + w[1:2, :] * g1 + w[0:1, :] * g2).astype(dgp_ref.dtype)
        dw0 = jnp.sum(gpc * g2, axis=0, keepdims=True)
        dw1 = jnp.sum(gpc * g1, axis=0, keepdims=True)
        dw2 = jnp.sum(gpc * dgate, axis=0, keepdims=True)
        z = jnp.zeros((SUBLANES - 3, gpc.shape[1]), F32)
        dw_ref[...] += jnp.concatenate([dw0, dw1, dw2, z], axis=0)
        db_ref[...] += jnp.sum(dgate, axis=0, keepdims=True)
        carry_sc[...] = dgate[:SUBLANES]

    rev = lambda i: nr - 1 - i
    cur = pl.BlockSpec((tr, tc), lambda j, i: (rev(i), j))
    return pl.pallas_call(
        body, name=name, grid=(ncb, nr),
        in_specs=[pl.BlockSpec((tr, K), lambda j, i: (rev(i), 0)),
                  pl.BlockSpec((tc, K), lambda j, i: (wblk * ncb + j, 0)),
                  cur, cur, cur,
                  pl.BlockSpec((SUBLANES, tc), lambda j, i: (0, j))],
        out_specs=[cur, cur,
                   pl.BlockSpec((SUBLANES, tc), lambda j, i: (0, j)),
                   pl.BlockSpec((1, tc), lambda j, i: (0, j))],
        out_shape=[_sds((T, C), BF16), _sds((T, C), BF16), _sds((SUBLANES, C), F32), _sds((1, C), F32)],
        scratch_shapes=[pltpu.VMEM((SUBLANES, tc), F32)],
        compiler_params=_cp(("parallel", "arbitrary")),
    )(dh2, warr, gp, up, gate, conv_w8)


def _cumsum_rows(x):
    n = x.shape[0]
    row = lax.broadcasted_iota(jnp.int32, x.shape, 0)
    s = 1
    while s < n:
        x = x + jnp.where(row >= s, pltpu.roll(x, s, 0), 0.0)
        s *= 2
    return x


def _rcumsum_rows(x):
    n = x.shape[0]
    row = lax.broadcasted_iota(jnp.int32, x.shape, 0)
    s = 1
    while s < n:
        x = x + jnp.where(row < n - s, pltpu.roll(x, n - s, 0), 0.0)
        s *= 2
    return x


def _dot_nt(a, b):
    return lax.dot_general(a.astype(BF16), b.astype(BF16), (((1,), (1,)), ((), ())), preferred_element_type=F32)


def _dot_tn(a, b):
    return lax.dot_general(a.astype(BF16), b.astype(BF16), (((0,), (0,)), ((), ())), preferred_element_type=F32)


def _dot_nn(a, b):
    return jnp.dot(a.astype(BF16), b.astype(BF16), preferred_element_type=F32)


def _hg_gates(hq, hf, lbv):
    sig = _sigmoid(hf)
    f = lbv + (1.0 - lbv) * sig
    return sig, f, jnp.log(f), 1.0 - f, hq * (HG_DK ** -0.5)


def _hg_sel_rows(ref, sp):
    return jnp.concatenate(
        [jnp.broadcast_to(ref[pl.ds(HG_SUB * i + sp, 1), :], (HG_SUB, HG_DK)) for i in range(HG_CHUNK // HG_SUB)], axis=0)


def _hg_masks():
    C = HG_CHUNK
    row = lax.broadcasted_iota(jnp.int32, (C, C), 0)
    col = lax.broadcasted_iota(jnp.int32, (C, C), 1)
    d = col - (row // HG_SUB) * HG_SUB
    tmod = row % HG_SUB
    diag_valid = jnp.logical_and(d >= 0, d <= tmod)
    return row, col, d, diag_valid


def _hg_strip_keys(k, b, r, n):
    ek = jnp.exp(r - b[:n])
    return ek, jnp.concatenate([k[:n] * ek, jnp.zeros((HG_CHUNK - n, k.shape[1]), F32)], axis=0)


def _hg_scores(q, k, b, b_sc, k_sc):
    C, S = HG_CHUNK, HG_SUB
    row, col, d, diag_valid = _hg_masks()
    blocks = [jnp.zeros((S, C), F32)]
    for i in range(1, C // S):
        r = b_sc[pl.ds(S * i - 1, 1), :]
        qi = q[S * i:S * (i + 1)] * jnp.exp(b[S * i:S * (i + 1)] - r)
        blocks.append(_dot_nt(qi, _hg_strip_keys(k, b, r, S * i)[1]))
    a_off = jnp.concatenate(blocks, axis=0)
    a_d = jnp.zeros((C, C), F32)
    for sp in range(S):
        bs = _hg_sel_rows(b_sc, sp)
        ks = _hg_sel_rows(k_sc, sp)
        e = jnp.exp(jnp.minimum(b - bs, 0.0))
        colv = jnp.sum(q * ks * e, axis=-1, keepdims=True)
        a_d = jnp.where(d == sp, colv, a_d)
    return a_off + jnp.where(diag_valid, a_d, 0.0)


def _hg_prep(hq_v, hf_v, lbv, b_sc, k_sc):
    sig, f, g, k, q = _hg_gates(hq_v, hf_v, lbv)
    b = _cumsum_rows(g)
    b_sc[...] = b
    k_sc[...] = k
    return sig, f, k, q, b, b_sc[pl.ds(HG_CHUNK - 1, 1), :]


def _hgrn_fwd(hq, hf, hi, lb, *, name):
    T = hq.shape[0]
    C, H, K = HG_CHUNK, HG_HEADS, HG_DK
    NC = T // C

    def body(hq_ref, hf_ref, hi_ref, lb_ref, o_ref, st_ref, s_sc, b_sc, k_sc):
        @pl.when(pl.program_id(0) == 0)
        def _():
            s_sc[...] = jnp.zeros_like(s_sc)

        st_all = s_sc[...]
        for j in range(P):
            rows = slice(C * j, C * (j + 1))
            st_ref[j] = st_all
            outs, news = [], []
            for h in range(H):
                sl = slice(K * h, K * (h + 1))
                _, _, k, q, b, bc = _hg_prep(hq_ref[rows, sl], hf_ref[rows, sl], lb_ref[:, sl], b_sc.at[j, h], k_sc.at[j, h])
                v = hi_ref[rows, sl]
                st0 = st_all[:, sl]
                a = _hg_scores(q, k, b, b_sc.at[j, h], k_sc.at[j, h])
                outs.append(_dot_nn(a, v) + _dot_nt(q * jnp.exp(b), st0))
                news.append(st0 * jnp.exp(bc) + _dot_tn(v, k * jnp.exp(bc - b)))
            o_ref[rows, :] = jnp.concatenate(outs, axis=1)
            st_all = jnp.concatenate(news, axis=1)
        s_sc[...] = st_all

    P = HG_FWD_CHUNKS_PER_STEP
    blk = pl.BlockSpec((P * C, H * K), lambda c: (c, 0))
    return pl.pallas_call(
        body, name=name, grid=(NC // P,),
        in_specs=[blk, blk, blk, pl.BlockSpec((1, H * K), lambda c: (0, 0))],
        out_specs=[blk, pl.BlockSpec((P, K, H * K), lambda c: (c, 0, 0))],
        out_shape=[_sds((T, H * K), F32), _sds((NC, K, H * K), F32)],
        scratch_shapes=[pltpu.VMEM((K, H * K), F32), pltpu.VMEM((P, H, C, K), F32), pltpu.VMEM((P, H, C, K), F32)],
        compiler_params=_cp(("arbitrary",)),
    )(hq, hf, hi, lb)


def _hgrn_bwd(hq, hf, hi, lb, states, do, *, name):
    T = hq.shape[0]
    C, H, K, S = HG_CHUNK, HG_HEADS, HG_DK, HG_SUB
    NC = T // C

    def intra_slow(q, k, b, da, b_sc, k_sc):
        row, col, d, diag_valid = _hg_masks()
        a_blocks = [jnp.zeros((S, C), F32)]
        dq_blocks = [jnp.zeros((S, K), F32)]
        dk = jnp.zeros((C, K), F32)
        for i in range(1, C // S):
            r = b_sc[pl.ds(S * i - 1, 1), :]
            eq = jnp.exp(b[S * i:S * (i + 1)] - r)
            ek = jnp.exp(jnp.minimum(r - b, 0.0))
            qi = q[S * i:S * (i + 1)] * eq
            kk = k * ek
            a_blocks.append(_dot_nt(qi, kk))
            dai = jnp.where(col[S * i:S * (i + 1)] < S * i, da[S * i:S * (i + 1)], 0.0)
            dq_blocks.append(_dot_nn(dai, kk) * eq)
            dk = dk + _dot_tn(dai, qi) * ek
        dq = jnp.concatenate(dq_blocks, axis=0)
        a_off = jnp.where(col < (row // S) * S, jnp.concatenate(a_blocks, axis=0), 0.0)
        same_blk = (row // S == col // S).astype(BF16)
        tmod = (lax.broadcasted_iota(jnp.int32, (C, K), 0)) % S
        a_d = jnp.zeros((C, C), F32)
        dk_d = jnp.zeros((C, K), F32)
        for sp in range(S):
            bs = _hg_sel_rows(b_sc, sp)
            ks = _hg_sel_rows(k_sc, sp)
            e = jnp.exp(jnp.minimum(b - bs, 0.0))
            eks = e * ks
            a_d = jnp.where(d == sp, jnp.sum(q * eks, axis=-1, keepdims=True), a_d)
            dacol = jnp.sum(jnp.where(d == sp, da, 0.0), axis=-1, keepdims=True)
            dq = dq + dacol * eks
            wq = dacol * e * q
            wq_hi = wq.astype(BF16)
            wq_lo = (wq - wq_hi.astype(F32)).astype(BF16)
            blk_sum = (jnp.dot(same_blk, wq_hi, preferred_element_type=F32)
                       + jnp.dot(same_blk, wq_lo, preferred_element_type=F32))
            dk_d = jnp.where(tmod == sp, blk_sum, dk_d)
        return a_off + jnp.where(diag_valid, a_d, 0.0), dq, dk + dk_d

    def one_head(pre, v, lbv, st0, dst1, dout, b_sc, k_sc):
        sig, f, k, q, b, bc = pre
        ebc = jnp.exp(bc)
        eb = jnp.exp(b)
        ekb = jnp.exp(bc - b)
        qt = q * eb
        kb = k * ekb
        row = lax.broadcasted_iota(jnp.int32, (C, C), 0)
        col = lax.broadcasted_iota(jnp.int32, (C, C), 1)
        da = jnp.where(col <= row, _dot_nt(dout, v), 0.0)
        dkb = _dot_nn(v, dst1)
        new_ds = _dot_tn(dout, qt) + dst1 * ebc
        a, dq_i, dk_i = intra_slow(q, k, b, da, b_sc, k_sc)
        dq = _dot_nn(dout, st0) * eb + dq_i
        dk = dkb * ekb + dk_i
        dv = _dot_tn(a, dout) + _dot_nt(kb, dst1)
        extra = jnp.sum(dkb * kb, axis=0, keepdims=True) + ebc * jnp.sum(st0 * dst1, axis=0, keepdims=True)
        rowk = lax.broadcasted_iota(jnp.int32, (C, K), 0)
        db = q * dq - k * dk + jnp.where(rowk == C - 1, extra, 0.0)
        dg = _rcumsum_rows(db)
        df = dg / f - dk
        return (dq * (K ** -0.5), df * (1.0 - lbv) * sig * (1.0 - sig), dv,
                jnp.sum(df * (1.0 - sig), axis=0, keepdims=True), new_ds)

    def body(hq_ref, hf_ref, hi_ref, lb_ref, st_ref, do_ref, dq_ref, dhf_ref, dv_ref, dlb_ref, ds_sc, b_sc, k_sc):
        @pl.when(pl.program_id(0) == 0)
        def _():
            ds_sc[...] = jnp.zeros_like(ds_sc)
            dlb_ref[...] = jnp.zeros_like(dlb_ref)

        ds_all = ds_sc[...]
        dlb = jnp.zeros((1, H * K), F32)
        for j in reversed(range(P)):
            rows = slice(C * j, C * (j + 1))
            st_all = st_ref[j]
            res = []
            for h in range(H):
                sl = slice(K * h, K * (h + 1))
                pre = _hg_prep(hq_ref[rows, sl], hf_ref[rows, sl], lb_ref[:, sl], b_sc.at[j, h], k_sc.at[j, h])
                res.append(one_head(pre, hi_ref[rows, sl], lb_ref[:, sl], st_all[:, sl], ds_all[:, sl], do_ref[rows, sl],
                                    b_sc.at[j, h], k_sc.at[j, h]))
            cat = lambda i: jnp.concatenate([r[i] for r in res], axis=1)
            dq_ref[rows, :] = cat(0).astype(dq_ref.dtype)
            dhf_ref[rows, :] = cat(1).astype(dhf_ref.dtype)
            dv_ref[rows, :] = cat(2).astype(dv_ref.dtype)
            dlb = dlb + cat(3)
            ds_all = cat(4)
        dlb_ref[...] += dlb
        ds_sc[...] = ds_all

    P = HG_CHUNKS_PER_STEP
    NS = NC // P
    blk = pl.BlockSpec((P * C, H * K), lambda c: (NS - 1 - c, 0))
    par = pl.BlockSpec((1, H * K), lambda c: (0, 0))
    return pl.pallas_call(
        body, name=name, grid=(NS,),
        in_specs=[blk, blk, blk, par, pl.BlockSpec((P, K, H * K), lambda c: (NS - 1 - c, 0, 0)), blk],
        out_specs=[blk, blk, blk, par],
        out_shape=[_sds((T, H * K), BF16)] * 3 + [_sds((1, H * K), F32)],
        scratch_shapes=[pltpu.VMEM((K, H * K), F32), pltpu.VMEM((P, H, C, K), F32), pltpu.VMEM((P, H, C, K), F32)],
        compiler_params=_cp(("arbitrary",)),
    )(hq, hf, hi, lb, states, do)


ATT_STACK = ATT_GROUP
ATT_FWD_QROWS = ATT_BLOCK // 2


def _att_valid(n, a=0, qrows=ATT_BLOCK):
    R, B = ATT_STACK * qrows, ATT_BLOCK
    j = lax.broadcasted_iota(jnp.int32, (B + qrows, R), 0)
    t = lax.broadcasted_iota(jnp.int32, (B + qrows, R), 1) % qrows
    dist = t + B - j
    first_key = jnp.where(n > 0, 0, B)
    return jnp.logical_and(jnp.logical_and(dist >= 0, dist < B), j + qrows * a >= first_key)


def _att_rows(x, a, qrows):
    return jnp.concatenate([x[ATT_BLOCK * g + qrows * a:ATT_BLOCK * g + qrows * (a + 1)] for g in range(ATT_STACK)], axis=0)


def _att_load(cur_ref, prev_ref, ba_ref, h0):
    hd = ATT_HD
    kv = h0 // ATT_GROUP
    def cols(ref, c0):
        return ref[:, c0:c0 + hd] + ba_ref[:, c0:c0 + hd]
    qs = jnp.concatenate([cols(cur_ref, hd * (h0 + g)) for g in range(ATT_STACK)], axis=0)
    kc = jnp.concatenate([cols(prev_ref, ATT_Q_W + hd * kv), cols(cur_ref, ATT_Q_W + hd * kv)], axis=0)
    vc = jnp.concatenate([cols(prev_ref, ATT_Q_W + ATT_KV_W + hd * kv), cols(cur_ref, ATT_Q_W + ATT_KV_W + hd * kv)], axis=0)
    return qs, kc, vc


def _att_probs(qs, kc, valid, sink_ref, h0):
    scale = 1.0 / math.sqrt(ATT_HD)
    s = jnp.where(valid, _dot_nt(kc, qs) * scale, NEG)
    nq = qs.shape[0] // ATT_STACK
    sink = jnp.concatenate([jnp.full((1, nq), sink_ref[0, h0 + g], F32) for g in range(ATT_STACK)], axis=1)
    m = jnp.maximum(jnp.max(s, axis=0, keepdims=True), sink)
    p = jnp.exp(s - m)
    ps = jnp.exp(sink - m)
    inv = 1.0 / (jnp.sum(p, axis=0, keepdims=True) + ps)
    return p * inv, ps * inv


def _attn_fwd(att, b_attn, sinks, *, name, after=None):
    T = att.shape[0]
    B = ATT_BLOCK
    NB = T // B
    lead = [] if after is None else [after]

    def body(*refs):
        sink_ref, cur_ref, prev_ref, ba_ref, o_ref = refs[len(lead):]
        Q = ATT_FWD_QROWS
        parts = range(B // Q)
        valid = [_att_valid(pl.program_id(0), a, Q) for a in parts]
        outs = [[None] * len(parts) for _ in range(ATT_HEADS)]
        for h0 in range(0, ATT_HEADS, ATT_STACK):
            qs, kc, vc = _att_load(cur_ref, prev_ref, ba_ref, h0)
            for a in parts:
                keys = slice(Q * a, Q * a + B + Q)
                prob, _ = _att_probs(_att_rows(qs, a, Q), kc[keys], valid[a], sink_ref, h0)
                o = _dot_tn(prob, vc[keys])
                for g in range(ATT_STACK):
                    outs[h0 + g][a] = o[Q * g:Q * (g + 1)]
        o_ref[...] = jnp.concatenate([jnp.concatenate(p, axis=0) for p in outs], axis=1)

    return pl.pallas_call(
        body, name=name, grid=(NB,),
        in_specs=[pl.BlockSpec(memory_space=pl.ANY)] * len(lead) + [
            pl.BlockSpec(memory_space=pltpu.SMEM),
            pl.BlockSpec((B, ATT_COLS), lambda n: (n, 0)),
            pl.BlockSpec((B, ATT_COLS), lambda n: (jnp.maximum(n - 1, 0), 0)),
            pl.BlockSpec((1, ATT_COLS), lambda n: (0, 0))],
        out_specs=pl.BlockSpec((B, ATT_Q_W), lambda n: (n, 0)),
        out_shape=_sds((T, ATT_Q_W), F32),
        compiler_params=_cp(("parallel",)),
    )(*lead, sinks, att, att, b_attn)


def _attn_bwd(att, b_attn, sinks, dmix, *, name):
    T = att.shape[0]
    B, hd = ATT_BLOCK, ATT_HD
    NB = T // B
    scale = 1.0 / math.sqrt(hd)

    def body(sink_ref, cur_ref, prev_ref, ba_ref, do_ref, daq_ref, dakv_ref, dsink_ref, dbq_ref, dbkv_ref, carry_sc):
        n = pl.program_id(0)

        @pl.when(n == 0)
        def _():
            carry_sc[...] = jnp.zeros_like(carry_sc)
            dsink_ref[...] = jnp.zeros_like(dsink_ref)
            dbq_ref[...] = jnp.zeros_like(dbq_ref)
            dbkv_ref[...] = jnp.zeros_like(dbkv_ref)

        @pl.when(n < NB)
        def _():
            valid = _att_valid(n)
            hrow = lax.broadcasted_iota(jnp.int32, (SUBLANES, 128), 0)
            dsink = jnp.zeros((SUBLANES, 128), F32)
            dqs = []
            dks = [jnp.zeros((2 * B, hd), F32)] * ATT_KV
            dvs = [jnp.zeros((2 * B, hd), F32)] * ATT_KV
            for h0 in range(0, ATT_HEADS, ATT_STACK):
                kv = h0 // ATT_GROUP
                qs, kc, vc = _att_load(cur_ref, prev_ref, ba_ref, h0)
                prob, psink = _att_probs(qs, kc, valid, sink_ref, h0)
                dout = jnp.concatenate([do_ref[:, hd * (h0 + g):hd * (h0 + g + 1)] for g in range(ATT_STACK)], axis=0)
                dp = _dot_nt(vc, dout)
                delta = jnp.sum(prob * dp, axis=0, keepdims=True)
                dsc = prob * (dp - delta) * scale
                dq = _dot_tn(dsc, kc)
                dks[kv] = dks[kv] + _dot_nn(dsc, qs)
                dvs[kv] = dvs[kv] + _dot_nn(prob, dout)
                dsk = psink * delta
                for g in range(ATT_STACK):
                    dqs.append(dq[B * g:B * (g + 1)])
                    tot = jnp.sum(dsk[:, B * g:B * (g + 1)], axis=1, keepdims=True)
                    dsink = dsink - jnp.where(hrow == h0 + g, tot, 0.0)
            daq = jnp.concatenate(dqs, axis=1).astype(daq_ref.dtype)
            daq_ref[...] = daq
            dsink_ref[...] += dsink
            dbq_ref[...] += jnp.sum(daq.astype(F32), axis=0, keepdims=True)
            done = carry_sc[...] + jnp.concatenate([d[:B] for d in dks + dvs], axis=1)
            dakv_ref[...] = done.astype(dakv_ref.dtype)
            dbkv_ref[...] += jnp.sum(done.astype(dakv_ref.dtype).astype(F32), axis=0, keepdims=True)
            carry_sc[...] = jnp.concatenate([d[B:] for d in dks + dvs], axis=1)

        @pl.when(n == NB)
        def _():
            done = carry_sc[...]
            dakv_ref[...] = done.astype(dakv_ref.dtype)
            dbkv_ref[...] += jnp.sum(done.astype(dakv_ref.dtype).astype(F32), axis=0, keepdims=True)

    cl = lambda n: jnp.minimum(n, NB - 1)
    return pl.pallas_call(
        body, name=name, grid=(NB + 1,),
        in_specs=[pl.BlockSpec(memory_space=pltpu.SMEM),
                  pl.BlockSpec((B, ATT_COLS), lambda n: (cl(n), 0)),
                  pl.BlockSpec((B, ATT_COLS), lambda n: (jnp.maximum(cl(n) - 1, 0), 0)),
                  pl.BlockSpec((1, ATT_COLS), lambda n: (0, 0)),
                  pl.BlockSpec((B, ATT_Q_W), lambda n: (cl(n), 0))],
        out_specs=[pl.BlockSpec((B, ATT_Q_W), lambda n: (cl(n), 0)),
                   pl.BlockSpec((B, 2 * ATT_KV_W), lambda n: (jnp.maximum(n - 1, 0), 0)),
                   pl.BlockSpec((SUBLANES, 128), lambda n: (0, 0)),
                   pl.BlockSpec((1, ATT_Q_W), lambda n: (0, 0)),
                   pl.BlockSpec((1, 2 * ATT_KV_W), lambda n: (0, 0))],
        out_shape=[_sds((T, ATT_Q_W), BF16), _sds((T, 2 * ATT_KV_W), BF16), _sds((SUBLANES, 128), F32),
                   _sds((1, ATT_Q_W), F32), _sds((1, 2 * ATT_KV_W), F32)],
        scratch_shapes=[pltpu.VMEM((B, 2 * ATT_KV_W), F32)],
        compiler_params=_cp(("arbitrary",)),
    )(sinks, att, att, b_attn, dmix)


def _silu_and_grad(x):
    sg = _sigmoid(x)
    return x * sg, sg * (1.0 + x * (1.0 - sg))


def _mix_fwd_fn(o_raw, hg, o_att, hgw):
    outs = []
    for h in range(HG_HEADS):
        sl = slice(HG_DK * h, HG_DK * (h + 1))
        silu, _ = _silu_and_grad(hg[:, sl])
        outs.append(_rms_fwd(o_raw[:, sl], hgw) * silu)
    outs.append(o_att)
    return (jnp.concatenate(outs, axis=1),)


def _mix_bwd_fn(o_raw, hg, dmix, hgw):
    dos, dhgs = [], []
    dw = jnp.zeros((1, HG_DK), F32)
    for h in range(HG_HEADS):
        sl = slice(HG_DK * h, HG_DK * (h + 1))
        silu, dsilu = _silu_and_grad(hg[:, sl])
        dy = dmix[:, sl]
        dhgs.append(dy * _rms_fwd(o_raw[:, sl], hgw) * dsilu)
        dx, dwh = _rms_bwd(o_raw[:, sl], hgw, dy * silu)
        dos.append(dx)
        dw = dw + dwh
    return jnp.concatenate(dos, axis=1), jnp.concatenate(dhgs, axis=1), dw


def _final_fn(h2, tgt, wf):
    d = h2.shape[1]
    err = _rms_fwd(h2, wf) - tgt
    loss_cols = (0.5 / d) * jnp.sum(err * err, axis=0, keepdims=True)
    dh2, dwf = _rms_bwd(h2, wf, err * (1.0 / d))
    return dh2, dh2, loss_cols, dwf


class _NoExchange:
    def __init__(self, weights):
        self.weights = weights

    def start(self):
        return None

    def w_in(self, after):
        return self.weights["w_in_t"]

    def mid(self, after):
        return None

    def w_out(self, after):
        return {k: self.weights[k] for k in ("w_out", "conv_w8")}

    def rest(self, after):
        return {k: self.weights[k] for k in ("w_gate_t", "w_up_t", "w_down")}

    def ffn_grads(self, gs):
        return None

    def ffn_grads_send(self, after):
        return None

    def small_grads(self, loss_cols, g):
        return None

    def late_grads(self, g_in):
        return None


def _local_step(x, tgt, p, ex):
    T, D = x.shape
    row = lambda n, dt: _sds((T, n), dt)
    acc = lambda n: _sds((1, n), F32)

    (u,) = _rowwise(lambda xv, w: (_rms_fwd(xv, w),), [_full(x)], [p["norm_mix_w"]], [row(D, BF16)], [], name="rms_mix",
                    after=ex.start())
    p = dict(p, w_in_t=ex.w_in(u))
    hq, hf, hi, hg, att = _mm_nt(u, p["w_in_t"], splits=[HG_W] * 4 + [ATT_COLS], out_dtype=F32, name="in_proj")
    o_raw, states = _hgrn_fwd(hq, hf, hi, p["lb"], name="hgrn_fwd")
    o_att = _attn_fwd(att, p["b_attn"], p["sinks"], name="attn_fwd", after=ex.mid(o_raw))
    p = dict(p, **ex.w_out(o_att))
    def out_epilogue(prod, xv, w):
        h1v = prod + xv
        return h1v, _rms_fwd(h1v, w)

    h1, v, mix = _mm_nn(None, [p["w_out"]], name="mix_out_proj",
                        prologue=(lambda *a: _mix_fwd_fn(*a)[0], [o_raw, hg, o_att], [p["hg_norm_w"]], row(D, BF16)),
                        epilogue=(out_epilogue, [x], [p["norm_ffn_w"]], [row(D, F32), row(D, BF16)], []))
    p = dict(p, **ex.rest(v))
    gp, up, gate, act = _ffn_in(v, p["w_gate_t"], p["w_up_t"], p["conv_w8"], p["conv_b"], name="ffn_in")
    def down_epilogue(prod, h1v, tgtv, wf):
        return _final_fn(prod + h1v, tgtv, wf)

    dh2, dh2_b, loss_cols, d_final = _mm_nn(
        [[act]], [p["w_down"]], name="down_proj_loss",
        epilogue=(down_epilogue, [h1, tgt], [p["final_norm_w"]], [row(D, F32), row(D, BF16)], [acc(D), acc(D)]))

    g_down = _mm_tn([act], dh2_b, name="g_down")
    dgp, dup, d_conv_w8, d_conv_b = _ffn_back(dh2_b, p["w_down"], gp, up, gate, p["conv_w8"], name="ffn_back")
    g_gate_t = _mm_tn([dgp], v, name="g_gate")
    g_up_t = _mm_tn([dup], v, name="g_up")
    swapping = ex.ffn_grads([g_gate_t, g_up_t, g_down])

    def ffn_norm_bwd(dvv, hv, dh2v, w):
        dx, dw = _rms_bwd(hv, w, dvv)
        dh1v = dx + dh2v
        return dh1v, dh1v, dw

    dh1, dh1_b, d_norm_ffn = _mm_nn(
        [[dgp], [dup]], [p["w_gate_t"], p["w_up_t"]], name="d_v_norm", after=swapping,
        epilogue=(ffn_norm_bwd, [h1, dh2], [p["norm_ffn_w"]], [row(D, F32), row(D, BF16)], [acc(D)]))
    sent = ex.ffn_grads_send(dh1_b)
    def mix_bwd(dmixv, o_rawv, hgv, hgw):
        do_rawv, dhgv, dw = _mix_bwd_fn(o_rawv, hgv, dmixv[:, :HG_W], hgw)
        return do_rawv, dhgv, dmixv[:, HG_W:], dw

    do_raw, dhg, do_att, d_hg_norm = _mm_nn(
        [[dh1_b]], [p["w_out"]], name="d_mix_bwd", w_transposed=True, after=sent,
        epilogue=(mix_bwd, [o_raw, hg], [p["hg_norm_w"]], [row(HG_W, F32), row(HG_W, BF16), row(ATT_Q_W, F32)], [acc(HG_DK)]))
    daq, dakv, d_sinks8, d_bq, d_bkv = _attn_bwd(att, p["b_attn"], p["sinks"], do_att, name="attn_bwd")
    dhq, dhf, dhi, d_lb = _hgrn_bwd(hq, hf, hi, p["lb"], states, do_raw, name="hgrn_bwd")
    pieces = [dhq, dhf, dhi, dhg, daq, dakv]
    def mix_norm_bwd(duv, xv, dh1v, w):
        dx, dw = _rms_bwd(xv, w, duv)
        return dx + dh1v, dw

    dx, d_norm_mix = _mm_nn([pieces], [p["w_in_t"]], name="d_u_norm",
                            epilogue=(mix_norm_bwd, [x, dh1], [p["norm_mix_w"]], [row(D, F32)], [acc(D)]))
    grads = dict(g_gate_t=g_gate_t, g_up_t=g_up_t, g_down=g_down,
                 norm_mix_w=d_norm_mix, b_attn=jnp.concatenate([d_bq, d_bkv], axis=1), lb=d_lb, hg_norm_w=d_hg_norm,
                 sinks8=d_sinks8, norm_ffn_w=d_norm_ffn, conv_w8=d_conv_w8, conv_b=d_conv_b, final_norm_w=d_final)
    g_in_t = _mm_tn(pieces, u, name="g_in", after=ex.small_grads(loss_cols, grads))
    g_out = _mm_tn([mix], dh1_b, name="g_out", after=ex.late_grads(g_in_t))
    return loss_cols, dx, dict(grads, g_in_t=g_in_t, g_out=g_out)


SLAB = (IN_COLS // N_CHIPS, D_FF // N_CHIPS, D_FF // N_CHIPS, D_FF // N_CHIPS, D_MODEL // N_CHIPS)
N_W = len(SLAB)
PACK_OFF = tuple(sum(SLAB[:i]) for i in range(N_W))
PACK_ROWS = sum(SLAB)
FULL_OFF = tuple(N_CHIPS * o for o in PACK_OFF)
FULL_ROWS = N_CHIPS * PACK_ROWS
HALF = tuple(s // 2 for s in SLAB)
HPACK_OFF = tuple(sum(HALF[:i]) for i in range(N_W))
HPACK_ROWS = sum(HALF)
HFULL_OFF = tuple(N_CHIPS * o for o in HPACK_OFF)
HFULL_ROWS = N_CHIPS * HPACK_ROWS
CHIP_FLIPS = ((1, 0), (0, 1), (1, 1))
N_DEV = 8
BF16_ROWS = 16
ANY = pl.BlockSpec(memory_space=pl.ANY)


def _pos():
    return lax.axis_index("x"), lax.axis_index("y"), lax.axis_index("c")


def _flip(v, f):
    return 1 - v if f else v


def _rcopy(src, dst, ssem, rsem, dev):
    return pltpu.make_async_remote_copy(src_ref=src, dst_ref=dst, send_sem=ssem, recv_sem=rsem, device_id=dev,
                                        device_id_type=pl.DeviceIdType.MESH)


def _rows(ref, start, n, align=None):
    if not isinstance(start, int):
        if align is None:
            align = SUBLANES * (4 // jnp.dtype(ref.dtype).itemsize)
        start = pl.multiple_of(start, align)
    return ref.at[pl.ds(start, n), :]


FFN_W = (1, 2, 3)
N_PEER = 1 + len(CHIP_FLIPS)
HBM = pl.BlockSpec(memory_space=pltpu.HBM)
SEM = pl.BlockSpec(memory_space=pltpu.SEMAPHORE)
EFFECT = pltpu.SideEffectType.DATAFLOW_SIDE_EFFECTING
LANES = 128


def _sent_rows(k, w, c):
    return (0, SLAB[w]) if k == 0 else (c * HALF[w], HALF[w])


def _gather_start(pack, cw8):
    D = pack.shape[1]
    lands = [lax.empty((N_CHIPS * SLAB[0], D), pack.dtype), lax.empty((3 * N_CHIPS * SLAB[1], D), pack.dtype),
             lax.empty((N_CHIPS * SLAB[4], D), pack.dtype), lax.empty((N_CHIPS,) + cw8.shape, cw8.dtype)]
    bufs = [pack, cw8] + lands

    def body(pack_ref, cw_ref, l_in, l_ffn, l_out, l_cw, *rest):
        in_send, in_recv, out_send, out_recv, ffn_send, ffn_recv = rest[:6]
        token = rest[-1]
        x, y, c = _pos()
        q = 2 * x + y
        peers = _gather_peers(x, y, c)

        def send(k, peer, w, land, base, ssem, rsem):
            r0, n = _sent_rows(k, w, c)
            _rcopy(_rows(pack_ref, PACK_OFF[w] + r0, n), _rows(land, base + q * SLAB[w] + r0, n), ssem, rsem, peer).start()

        for k, peer in enumerate(peers):
            send(k, peer, 0, l_in, 0, in_send.at[k], in_recv.at[k])
        for k, peer in enumerate(peers):
            send(k, peer, 4, l_out, 0, out_send.at[k], out_recv.at[k])
            _rcopy(cw_ref, l_cw.at[q], out_send.at[N_PEER + k], out_recv.at[N_PEER + k], peer).start()
        for j, w in enumerate(FFN_W):
            for k, peer in enumerate(peers):
                send(k, peer, w, l_ffn, j * N_CHIPS * SLAB[w], ffn_send.at[k], ffn_recv.at[k])
        token[...] = jnp.zeros_like(token)

    n_sem = (N_PEER, N_PEER, 2 * N_PEER, 2 * N_PEER, N_PEER, N_PEER)
    outs = pl.pallas_call(
        body, name="gather_start", in_specs=[HBM] * len(bufs),
        out_specs=[SEM] * len(n_sem) + [HBM] * len(bufs) + [pl.BlockSpec(memory_space=pltpu.VMEM)],
        out_shape=[pltpu.SemaphoreType.DMA((n,)) for n in n_sem]
        + [pltpu.HBM(b.shape, b.dtype) for b in bufs] + [TOKEN],
        input_output_aliases={i: len(n_sem) + i for i in range(len(bufs))},
        compiler_params=pltpu.CompilerParams(has_side_effects=EFFECT),
    )(*[pltpu.with_memory_space_constraint(b, pltpu.HBM) for b in bufs])
    bufs_out = outs[len(n_sem):]
    return dict(in_sems=outs[0:2], out_sems=outs[2:4], ffn_sems=outs[4:6], pack=bufs_out[0], cw=bufs_out[1], l_in=bufs_out[2],
                l_ffn=bufs_out[3], l_out=bufs_out[4], l_cw=bufs_out[5], token=bufs_out[6])


def _gather_peers(x, y, c):
    return [(x, y, 1 - c)] + [(_flip(x, fx), _flip(y, fy), c) for fx, fy in CHIP_FLIPS]


def _gather_wait_in(g, after):
    def body(pack_ref, l_in, send, recv, after_ref, pack_out, l_out):
        for k, peer in enumerate(_gather_peers(*_pos())):
            n = _sent_rows(k, 0, 0)[1]
            cp = _rcopy(_rows(pack_ref, PACK_OFF[0], n), _rows(l_in, 0, n), send.at[k], recv.at[k], peer)
            cp.wait_send()
            cp.wait_recv()

    return pl.pallas_call(
        body, name="gather_wait_in", in_specs=[HBM, HBM, SEM, SEM, ANY], out_specs=[HBM, HBM],
        out_shape=[pltpu.HBM(g["pack"].shape, g["pack"].dtype), pltpu.HBM(g["l_in"].shape, g["l_in"].dtype)],
        input_output_aliases={0: 0, 1: 1}, compiler_params=pltpu.CompilerParams(has_side_effects=EFFECT),
    )(g["pack"], g["l_in"], *g["in_sems"], after)


def _gather_wait_out(g, pack, after):
    def body(pack_ref, cw_ref, l_out, l_cw, o_send, o_recv, after_ref, o_out, o_cw):
        for k, peer in enumerate(_gather_peers(*_pos())):
            n_out = _sent_rows(k, 4, 0)[1]
            for cp in (_rcopy(_rows(pack_ref, PACK_OFF[4], n_out), _rows(l_out, 0, n_out), o_send.at[k], o_recv.at[k], peer),
                       _rcopy(cw_ref, l_cw.at[0], o_send.at[N_PEER + k], o_recv.at[N_PEER + k], peer)):
                cp.wait_send()
                cp.wait_recv()

    ins = [pack, g["cw"], g["l_out"], g["l_cw"]]
    return pl.pallas_call(
        body, name="gather_wait_out", in_specs=[HBM] * 4 + [SEM] * 2 + [ANY], out_specs=[HBM] * 2,
        out_shape=[pltpu.HBM(b.shape, b.dtype) for b in ins[2:]],
        input_output_aliases={2: 0, 3: 1}, compiler_params=pltpu.CompilerParams(has_side_effects=EFFECT),
    )(*ins, *g["out_sems"], after)


def _gather_wait_ffn(g, pack, after):
    def body(pack_ref, l_ffn, f_send, f_recv, after_ref, o_ffn):
        for k, peer in enumerate(_gather_peers(*_pos())):
            n_ffn = len(FFN_W) * _sent_rows(k, FFN_W[0], 0)[1]
            cp = _rcopy(_rows(pack_ref, PACK_OFF[FFN_W[0]], n_ffn), _rows(l_ffn, 0, n_ffn), f_send.at[k], f_recv.at[k], peer)
            cp.wait_send()
            cp.wait_recv()

    return pl.pallas_call(
        body, name="gather_wait_ffn", in_specs=[HBM] * 2 + [SEM] * 2 + [ANY], out_specs=HBM,
        out_shape=pltpu.HBM(g["l_ffn"].shape, g["l_ffn"].dtype),
        input_output_aliases={1: 0}, compiler_params=pltpu.CompilerParams(has_side_effects=EFFECT),
    )(pack, g["l_ffn"], *g["ffn_sems"], after)


FWD_IN = ((0, 0, 0),)
FWD_OUT = ((0, 4, 0),)
FWD_FFN = tuple((0, w, j * N_CHIPS * SLAB[w]) for j, w in enumerate(FFN_W))


def _forward_copies(layout, src, dst, send_sems, recv_sems):
    x, y, c = _pos()
    sib = (x, y, 1 - c)
    cps = []
    for fx, fy in CHIP_FLIPS:
        qa = 2 * _flip(x, fx) + _flip(y, fy)
        for bi, w, base in layout:
            r0 = base + qa * SLAB[w] + c * HALF[w]
            cps.append(_rcopy(_rows(src[bi], r0, HALF[w]), _rows(dst[bi], r0, HALF[w]),
                              send_sems.at[len(cps)], recv_sems.at[len(cps)], sib))
    return cps


def _forward_in(l_in):
    n = len(CHIP_FLIPS) * len(FWD_IN)

    def body(in_ref, out_ref, send_sems, recv_sems):
        cps = _forward_copies(FWD_IN, [in_ref], [out_ref], send_sems, recv_sems)
        for cp in cps:
            cp.start()
        for cp in cps:
            cp.wait_recv()
        for cp in cps:
            cp.wait_send()

    return pl.pallas_call(
        body, name="forward_in", in_specs=[ANY], out_specs=ANY, out_shape=_sds(l_in.shape, l_in.dtype),
        input_output_aliases={0: 0},
        scratch_shapes=[pltpu.SemaphoreType.DMA((n,)), pltpu.SemaphoreType.DMA((n,))],
    )(l_in)


def _forward_start(layout, land, *, name):
    n = len(CHIP_FLIPS) * len(layout)

    def body(in_ref, send_sems, recv_sems, out_ref, token):
        for cp in _forward_copies(layout, [in_ref], [in_ref], send_sems, recv_sems):
            cp.start()
        token[...] = jnp.zeros_like(token)

    outs = pl.pallas_call(
        body, name=name, in_specs=[HBM],
        out_specs=[SEM, SEM, HBM, pl.BlockSpec(memory_space=pltpu.VMEM)],
        out_shape=[pltpu.SemaphoreType.DMA((n,)), pltpu.SemaphoreType.DMA((n,)), pltpu.HBM(land.shape, land.dtype), TOKEN],
        input_output_aliases={0: 2}, compiler_params=pltpu.CompilerParams(has_side_effects=EFFECT),
    )(pltpu.with_memory_space_constraint(land, pltpu.HBM))
    return dict(sems=outs[0:2], land=outs[2], token=outs[3])


def _forward_wait(layout, s, after, *, name):
    def body(in_ref, send_sems, recv_sems, after_ref, out_ref):
        for cp in _forward_copies(layout, [in_ref], [in_ref], send_sems, recv_sems):
            cp.wait_send()
            cp.wait_recv()

    return pl.pallas_call(
        body, name=name, in_specs=[HBM, SEM, SEM, ANY], out_specs=HBM,
        out_shape=pltpu.HBM(s["land"].shape, s["land"].dtype),
        input_output_aliases={0: 0}, compiler_params=pltpu.CompilerParams(has_side_effects=EFFECT),
    )(s["land"], *s["sems"], after)


def _exchange_halves(ws, gs, small, *, name):
    D = gs[0].shape[1]
    n = len(ws)
    has_small = small is not None

    def body(*refs):
        g = refs[:n]
        t = refs[n + has_small:2 * n + has_small]
        sems = refs[2 * n + 2 * has_small:]
        d2d_send, d2d_recv = sems[0], sems[1]
        x, y, c = _pos()
        sib = (x, y, 1 - c)
        drains = []
        for i, w in enumerate(ws):
            h = HALF[w]
            for qq in range(N_CHIPS):
                _rcopy(_rows(g[i], qq * SLAB[w] + (1 - c) * h, h), _rows(t[i], qq * h, h),
                       d2d_send.at[i], d2d_recv.at[i], sib).start()
            drains.append(_rcopy(t[i], t[i], d2d_send.at[i], d2d_recv.at[i], sib))
        if has_small:
            small_ref, sall_ref = refs[n], refs[2 * n + 1]
            sm_send, sm_recv, loc_sem = sems[2], sems[3], sems[4]
            me = 4 * x + 2 * y + c
            own_small = pltpu.make_async_copy(small_ref, sall_ref.at[me], loc_sem)
            own_small.start()
            for f in range(1, N_DEV):
                peer = (_flip(x, f & 4), _flip(y, f & 2), _flip(c, f & 1))
                cp = _rcopy(small_ref, sall_ref.at[me], sm_send.at[f - 1], sm_recv.at[f - 1], peer)
                cp.start()
                drains.append(cp)
        for d in drains:
            d.wait_recv()
        for d in drains:
            d.wait_send()
        if has_small:
            own_small.wait()

    out_shape = [_sds((N_CHIPS * HALF[w], D), gs[0].dtype) for w in ws]
    scratch = [pltpu.SemaphoreType.DMA((n,)), pltpu.SemaphoreType.DMA((n,))]
    if has_small:
        out_shape.append(_sds((N_DEV,) + small.shape, F32))
        scratch += [pltpu.SemaphoreType.DMA((N_DEV - 1,)), pltpu.SemaphoreType.DMA((N_DEV - 1,)), pltpu.SemaphoreType.DMA]
    return pl.pallas_call(
        body, name=name, in_specs=[ANY] * (n + has_small), out_specs=[ANY] * (n + has_small),
        out_shape=out_shape, scratch_shapes=scratch,
    )(*gs, *([small] if has_small else []))


def _halves_copies(ws, g, t, send_sems, recv_sems):
    x, y, c = _pos()
    sib = (x, y, 1 - c)
    cps = []
    for i, w in enumerate(ws):
        h = HALF[w]
        for qq in range(N_CHIPS):
            cps.append(_rcopy(_rows(g[i], qq * SLAB[w] + (1 - c) * h, h), _rows(t[i], qq * h, h),
                              send_sems.at[N_CHIPS * i + qq], recv_sems.at[N_CHIPS * i + qq], sib))
    return cps


def _halves_start(ws, gs, *, name):
    D = gs[0].shape[1]
    n = len(ws)
    bufs = list(gs) + [lax.empty((N_CHIPS * HALF[w], D), gs[0].dtype) for w in ws]

    def body(*refs):
        for cp in _halves_copies(ws, refs[:n], refs[n:2 * n], refs[2 * n], refs[2 * n + 1]):
            cp.start()
        refs[-1][...] = jnp.zeros_like(refs[-1])

    outs = pl.pallas_call(
        body, name=name, in_specs=[HBM] * (2 * n),
        out_specs=[SEM, SEM] + [HBM] * (2 * n) + [pl.BlockSpec(memory_space=pltpu.VMEM)],
        out_shape=[pltpu.SemaphoreType.DMA((N_CHIPS * n,)), pltpu.SemaphoreType.DMA((N_CHIPS * n,))]
        + [pltpu.HBM(b.shape, b.dtype) for b in bufs] + [TOKEN],
        input_output_aliases={i: 2 + i for i in range(2 * n)},
        compiler_params=pltpu.CompilerParams(has_side_effects=EFFECT),
    )(*[pltpu.with_memory_space_constraint(b, pltpu.HBM) for b in bufs])
    return dict(sems=outs[0:2], gs=outs[2:2 + n], theirs=outs[2 + n:2 + 2 * n], token=outs[-1])


def _halves_wait(ws, s, after, *, name):
    n = len(ws)

    def body(*refs):
        for cp in _halves_copies(ws, refs[:n], refs[n:2 * n], refs[2 * n], refs[2 * n + 1]):
            cp.wait_send()
            cp.wait_recv()

    bufs = list(s["gs"]) + list(s["theirs"])
    outs = pl.pallas_call(
        body, name=name, in_specs=[HBM] * (2 * n) + [SEM, SEM, ANY], out_specs=[HBM] * (2 * n),
        out_shape=[pltpu.HBM(b.shape, b.dtype) for b in bufs],
        input_output_aliases={i: i for i in range(2 * n)},
        compiler_params=pltpu.CompilerParams(has_side_effects=EFFECT),
    )(*bufs, *s["sems"], after)
    return outs[:n], outs[n:]


def _small_peers():
    x, y, c = _pos()
    return 4 * x + 2 * y + c, [(_flip(x, f & 4), _flip(y, f & 2), _flip(c, f & 1)) for f in range(1, N_DEV)]


def _small_start(small):
    sall = lax.empty((N_DEV,) + small.shape, F32)

    def body(small_ref, sall_ref, sm_send, sm_recv, loc_sem, small_out, sall_out, token):
        me, peers = _small_peers()
        for k, peer in enumerate(peers):
            _rcopy(small_ref, sall_ref.at[me], sm_send.at[k], sm_recv.at[k], peer).start()
        pltpu.make_async_copy(small_ref, sall_ref.at[me], loc_sem.at[0]).start()
        token[...] = jnp.zeros_like(token)

    outs = pl.pallas_call(
        body, name="small_start", in_specs=[HBM] * 2,
        out_specs=[SEM, SEM, SEM, HBM, HBM, pl.BlockSpec(memory_space=pltpu.VMEM)],
        out_shape=[pltpu.SemaphoreType.DMA((N_DEV - 1,)), pltpu.SemaphoreType.DMA((N_DEV - 1,)), pltpu.SemaphoreType.DMA((1,)),
                   pltpu.HBM(small.shape, F32), pltpu.HBM(sall.shape, F32), TOKEN],
        input_output_aliases={0: 3, 1: 4},
        compiler_params=pltpu.CompilerParams(has_side_effects=EFFECT),
    )(pltpu.with_memory_space_constraint(small, pltpu.HBM), pltpu.with_memory_space_constraint(sall, pltpu.HBM))
    return dict(sems=outs[0:3], small=outs[3], sall=outs[4], token=outs[5])


def _small_wait(s, after):
    def body(small_ref, sall_ref, sm_send, sm_recv, loc_sem, after_ref, small_out, sall_out):
        me, peers = _small_peers()
        for k, peer in enumerate(peers):
            cp = _rcopy(small_ref, sall_ref.at[me], sm_send.at[k], sm_recv.at[k], peer)
            cp.wait_send()
            cp.wait_recv()
        pltpu.make_async_copy(small_ref, sall_ref.at[me], loc_sem.at[0]).wait()

    outs = pl.pallas_call(
        body, name="small_wait", in_specs=[HBM, HBM, SEM, SEM, SEM, ANY], out_specs=[HBM, HBM],
        out_shape=[pltpu.HBM(s["small"].shape, F32), pltpu.HBM(s["sall"].shape, F32)],
        input_output_aliases={0: 0, 1: 1}, compiler_params=pltpu.CompilerParams(has_side_effects=EFFECT),
    )(s["small"], s["sall"], *s["sems"], after)
    return outs[1]


REDUCE_SPLIT = 2


def _chip_partial(ws, gs, theirs, *, name, out_dtype=F32):
    D = gs[0].shape[1]
    n = len(ws)

    def body(*refs):
        for i in range(n):
            refs[2 * n + i][...] = (refs[i][...].astype(F32) + refs[n + i][...].astype(F32)).astype(out_dtype)

    blk = [HALF[w] // REDUCE_SPLIT for w in ws]
    mine = [pl.BlockSpec((b, D), lambda qq, j: ((2 * qq + lax.axis_index("c")) * REDUCE_SPLIT + j, 0)) for b in blk]
    flat = [pl.BlockSpec((b, D), lambda qq, j: (qq * REDUCE_SPLIT + j, 0)) for b in blk]
    return pl.pallas_call(
        body, name=name, grid=(N_CHIPS, REDUCE_SPLIT), in_specs=mine + flat, out_specs=flat,
        out_shape=[_sds((N_CHIPS * HALF[w], D), out_dtype) for w in ws],
        compiler_params=_cp(("parallel", "parallel")),
    )(*gs, *theirs)


def _partial_copies(ws, part, got, send_sems, recv_sems):
    x, y, c = _pos()
    cps = []
    for k, (fx, fy) in enumerate(CHIP_FLIPS):
        peer = (_flip(x, fx), _flip(y, fy), c)
        qp = 2 * _flip(x, fx) + _flip(y, fy)
        for i, w in enumerate(ws):
            cps.append(_rcopy(_rows(part[i], qp * HALF[w], HALF[w]), _rows(got[i], k * HALF[w], HALF[w]),
                              send_sems.at[len(ws) * k + i], recv_sems.at[len(ws) * k + i], peer))
    return cps


def _send_start(ws, parts, *, name):
    D = parts[0].shape[1]
    n = len(ws)
    bufs = list(parts) + [lax.empty((len(CHIP_FLIPS) * HALF[w], D), parts[0].dtype) for w in ws]

    def body(*refs):
        send_sems, recv_sems = refs[2 * n], refs[2 * n + 1]
        for cp in _partial_copies(ws, refs[:n], refs[n:2 * n], send_sems, recv_sems):
            cp.start()
        refs[-1][...] = jnp.zeros_like(refs[-1])

    outs = pl.pallas_call(
        body, name=name, in_specs=[HBM] * (2 * n),
        out_specs=[SEM, SEM] + [HBM] * (2 * n) + [pl.BlockSpec(memory_space=pltpu.VMEM)],
        out_shape=[pltpu.SemaphoreType.DMA((len(CHIP_FLIPS) * n,)), pltpu.SemaphoreType.DMA((len(CHIP_FLIPS) * n,))]
        + [pltpu.HBM(b.shape, b.dtype) for b in bufs] + [TOKEN],
        input_output_aliases={i: 2 + i for i in range(2 * n)},
        compiler_params=pltpu.CompilerParams(has_side_effects=EFFECT),
    )(*[pltpu.with_memory_space_constraint(b, pltpu.HBM) for b in bufs])
    return dict(sems=outs[0:2], parts=outs[2:2 + n], got=outs[2 + n:2 + 2 * n], token=outs[-1])


def _send_wait(ws, s, after, *, name):
    n = len(ws)

    def body(*refs):
        for cp in _partial_copies(ws, refs[:n], refs[n:2 * n], refs[2 * n], refs[2 * n + 1]):
            cp.wait_send()
            cp.wait_recv()

    bufs = list(s["parts"]) + list(s["got"])
    outs = pl.pallas_call(
        body, name=name, in_specs=[HBM] * (2 * n) + [SEM, SEM] + [ANY] * len(after), out_specs=[HBM] * (2 * n),
        out_shape=[pltpu.HBM(b.shape, b.dtype) for b in bufs],
        input_output_aliases={i: i for i in range(2 * n)},
        compiler_params=pltpu.CompilerParams(has_side_effects=EFFECT),
    )(*bufs, *s["sems"], *after)
    return outs[:n], outs[n:]


def _chip_reduce(ws, parts, got, *, name, after=None):
    D = parts[0].shape[1]
    nk = len(CHIP_FLIPS)
    n = len(ws)
    extra = [] if after is None else [after]

    def body(*refs):
        refs = refs[len(extra):]
        outs = refs[(1 + nk) * n:]
        for i in range(n):
            acc = refs[i][...].astype(F32)
            for k in range(nk):
                acc = acc + refs[n * (1 + k) + i][...].astype(F32)
            outs[i][...] = acc

    blk = [HALF[w] // REDUCE_SPLIT for w in ws]

    def q_idx(j):
        return (2 * lax.axis_index("x") + lax.axis_index("y")) * REDUCE_SPLIT + j

    in_specs = [pl.BlockSpec((b, D), lambda j: (q_idx(j), 0)) for b in blk]
    for k in range(nk):
        in_specs += [pl.BlockSpec((b, D), functools.partial(lambda j, k: (k * REDUCE_SPLIT + j, 0), k=k)) for b in blk]
    out_specs = [pl.BlockSpec((b, D), lambda j: (lax.axis_index("c") * REDUCE_SPLIT + j, 0)) for b in blk]
    return pl.pallas_call(
        body, name=name, grid=(REDUCE_SPLIT,), in_specs=[ANY] * len(extra) + in_specs, out_specs=out_specs,
        out_shape=[_sds((SLAB[w], D), F32) for w in ws],
        compiler_params=_cp(("parallel",)),
    )(*extra, *parts, *[g for _ in range(nk) for g in got])


def _exchange_reduced(ws, shards, *, name):
    n = len(ws)

    def body(*refs):
        ins, outs = refs[:n], refs[n:2 * n]
        send_sems, recv_sems = refs[2 * n], refs[2 * n + 1]
        x, y, c = _pos()
        sib = (x, y, 1 - c)
        cps = []
        for i, w in enumerate(ws):
            cp = _rcopy(_rows(ins[i], c * HALF[w], HALF[w]), _rows(outs[i], c * HALF[w], HALF[w]),
                        send_sems.at[i], recv_sems.at[i], sib)
            cp.start()
            cps.append(cp)
        for cp in cps:
            cp.wait_recv()
        for cp in cps:
            cp.wait_send()

    return pl.pallas_call(
        body, name=name, in_specs=[ANY] * n, out_specs=[ANY] * n,
        out_shape=[_sds(s.shape, s.dtype) for s in shards], input_output_aliases={i: i for i in range(n)},
        scratch_shapes=[pltpu.SemaphoreType.DMA((n,)), pltpu.SemaphoreType.DMA((n,))],
    )(*shards)


def _adamw_fn(w, g, m, v):
    m2 = ADAM_B1 * m + (1.0 - ADAM_B1) * g
    v2 = ADAM_B2 * v + (1.0 - ADAM_B2) * (g * g)
    m_hat = m2 / (1.0 - ADAM_B1 ** ADAM_STEP)
    v_hat = v2 / (1.0 - ADAM_B2 ** ADAM_STEP)
    return -ADAM_LR * (m_hat / (jnp.sqrt(v_hat) + ADAM_EPS) + ADAM_WD * w), m2, v2


def _adamw(w, g, m, v, *, name):
    shp = _sds(w.shape, F32)
    rows = w.shape[0]
    tm = max(t for t in range(SUBLANES, 512 + 1, SUBLANES) if rows % t == 0)
    return _rowwise(lambda wv, gv, mv, vv: (gv, *_adamw_fn(wv, gv, mv, vv)), [_full(w), _full(g), _full(m), _full(v)], [],
                    [shp] * 4, [], name=name, tm=tm)


SMALL_SEGS = (("loss", 8), ("norm_mix_w", 8), ("b_attn", 8), ("lb_logits", 8), ("hg_norm_w", 8), ("sinks", 8),
              ("norm_ffn_w", 8), ("conv_w", 72), ("conv_b", 24), ("final_norm_w", 8))
SMALL_OFF = {n: sum(r for _, r in SMALL_SEGS[:i]) for i, (n, _) in enumerate(SMALL_SEGS)}
SMALL_ROWS = sum(r for _, r in SMALL_SEGS)


def _pack_small(parts):
    segs = []
    for n, r in SMALL_SEGS:
        a = parts.get(n)
        flat = jnp.zeros((0,), F32) if a is None else a.reshape(-1).astype(F32)
        segs.append(jnp.pad(flat, (0, r * LANES - flat.shape[0])).reshape(r, LANES))
    return jnp.concatenate(segs, axis=0)


def _unpack_small(pack, n, shape):
    size = math.prod(shape)
    r0 = SMALL_OFF[n]
    return pack[r0:r0 + dict(SMALL_SEGS)[n]].reshape(-1)[:size].reshape(shape)


def _small_update(sall, wp, mp, vp, *, after):
    R = SMALL_ROWS
    r_lb = SMALL_OFF["lb_logits"]

    def body(after_ref, s_ref, w_ref, m_ref, v_ref, g_ref, d_ref, m2_ref, v2_ref, loss_ref):
        g = s_ref[0]
        for i in range(1, N_DEV):
            g = g + s_ref[i]
        tot = jnp.sum(jnp.sum(g[0:8], axis=1, keepdims=True), axis=0, keepdims=True)
        loss_ref[...] = jnp.broadcast_to(tot, loss_ref.shape)
        lg = w_ref[r_lb:r_lb + 8, :]
        p0 = _sigmoid(lg - pltpu.roll(lg, 4, 0))
        d = g[r_lb:r_lb + 8]
        d = d + pltpu.roll(d, 4, 0)
        sign = jnp.where(lax.broadcasted_iota(jnp.int32, d.shape, 0) < 4, 1.0, -1.0)
        g = jnp.concatenate([g[:r_lb], sign * d * p0 * (1.0 - p0), g[r_lb + 8:]], axis=0)
        g_ref[...] = g
        d_ref[...], m2_ref[...], v2_ref[...] = _adamw_fn(w_ref[...], g, m_ref[...], v_ref[...])

    full = pl.BlockSpec((R, LANES), lambda: (0, 0))
    return pl.pallas_call(
        body, name="small_update",
        in_specs=[ANY, pl.BlockSpec((N_DEV, R, LANES), lambda: (0, 0, 0)), full, full, full],
        out_specs=[full, full, full, full, pl.BlockSpec((8, LANES), lambda: (0, 0))],
        out_shape=[_sds((R, LANES), F32)] * 4 + [_sds((8, LANES), F32)],
        compiler_params=_cp(),
    )(after, sall, wp, mp, vp)


def _lb_fwd(lb_logits):
    n = lb_logits.shape[1]

    def body(l_ref, o_ref):
        o_ref[...] = _sigmoid(l_ref[0:1, :] - l_ref[1:2, :])

    return pl.pallas_call(body, name="lb_fwd", out_shape=jax.ShapeDtypeStruct((1, n), F32), compiler_params=_cp())(lb_logits)


class _MeshExchange:
    def __init__(self, pack, cw8):
        self.gather = _gather_start(pack, cw8)
        self.sent = None
        self.conv_w8 = None

    def start(self):
        return self.gather["token"]

    def w_in(self, after):
        self.pack, l_in = _gather_wait_in(self.gather, after)
        return (_forward_in(l_in), N_CHIPS * SLAB[0], 0)

    def mid(self, after):
        l_out, l_cw = _gather_wait_out(self.gather, self.pack, after)
        self.conv_w8 = jnp.concatenate([l_cw[i] for i in range(N_CHIPS)], axis=1)
        self.passing_out = _forward_start(FWD_OUT, l_out, name="forward_out_start")
        return self.passing_out["token"]

    def w_out(self, after):
        l_ffn = _gather_wait_ffn(self.gather, self.pack, after)
        self.passing_ffn = _forward_start(FWD_FFN, l_ffn, name="forward_ffn_start")
        l_out = _forward_wait(FWD_OUT, self.passing_out, self.passing_ffn["token"], name="forward_out_wait")
        return dict(w_out=(l_out, N_CHIPS * SLAB[4], 0), conv_w8=self.conv_w8)

    def rest(self, after):
        l_ffn = _forward_wait(FWD_FFN, self.passing_ffn, after, name="forward_ffn_wait")
        rows = N_CHIPS * SLAB[FFN_W[0]]
        return dict(w_gate_t=(l_ffn, rows, 0), w_up_t=(l_ffn, rows, 1), w_down=(l_ffn, rows, 2))

    def ffn_grads(self, gs):
        self.swap = _halves_start(FFN_W, gs, name="halves_ffn_start")
        return self.swap["token"]

    def small_grads(self, loss_cols, g):
        small = _pack_small(dict(loss=loss_cols, norm_mix_w=g["norm_mix_w"], b_attn=g["b_attn"], lb_logits=g["lb"],
                                 hg_norm_w=g["hg_norm_w"], sinks=g["sinks8"], norm_ffn_w=g["norm_ffn_w"],
                                 conv_w=g["conv_w8"][:3], conv_b=g["conv_b"], final_norm_w=g["final_norm_w"]))
        self.small_sent = _small_start(small)
        return self.small_sent["token"]

    def late_grads(self, g_in):
        self.swap_in = _halves_start((0,), [g_in], name="halves_in_start")
        return self.swap_in["token"]

    def ffn_grads_send(self, after):
        gs, theirs = _halves_wait(FFN_W, self.swap, after, name="halves_ffn_wait")
        parts = _chip_partial(FFN_W, gs, theirs, name="chip_partial_ffn", out_dtype=BF16)
        self.sent = _send_start(FFN_W, parts, name="send_ffn_start")
        return self.sent["token"]


def kernel(x, norm_mix_w, w_in, b_attn, lb_logits, hg_norm_w, sinks, w_out, norm_ffn_w, w_gate, w_up, conv_w, conv_b, w_down, final_norm_w, loss_target, m_norm_mix_w, m_w_in, m_b_attn, m_lb_logits, m_hg_norm_w, m_sinks, m_w_out, m_norm_ffn_w, m_w_gate, m_w_up, m_conv_w, m_conv_b, m_w_down, m_final_norm_w, v_norm_mix_w, v_w_in, v_b_attn, v_lb_logits, v_hg_norm_w, v_sinks, v_w_out, v_norm_ffn_w, v_w_gate, v_w_up, v_conv_w, v_conv_b, v_w_down, v_final_norm_w):
    D = D_MODEL
    q = 2 * lax.axis_index("x") + lax.axis_index("y")
    ccols = D_FF // N_CHIPS

    pack = jnp.concatenate([w_in[0].T, w_gate[0].T, w_up[0].T, w_down[0], w_out[0]], axis=0).astype(BF16)
    cw8 = jnp.concatenate([conv_w[0], jnp.zeros((SUBLANES - 3, ccols), F32)], axis=0)
    ex = _MeshExchange(pack, cw8)
    p = dict(norm_mix_w=norm_mix_w, b_attn=b_attn, lb=_lb_fwd(lb_logits), hg_norm_w=hg_norm_w, sinks=sinks,
             norm_ffn_w=norm_ffn_w, conv_b=conv_b, final_norm_w=final_norm_w.reshape(1, D))
    loss_cols, dx, g = _local_step(x[0], loss_target[0], p, ex)
    conv_w8 = ex.conv_w8

    parts_ffn, got_ffn = _send_wait(FFN_W, ex.sent, [dx], name="send_ffn_wait")
    late = (0, 4)
    (g_in_t,), (theirs_in,) = _halves_wait((0,), ex.swap_in, g["g_out"], name="halves_in_wait")
    (theirs_out,) = _exchange_halves((4,), [g["g_out"]], None, name="exchange_halves_late")
    sall = _small_wait(ex.small_sent, theirs_out)
    gs, theirs = [g_in_t, g["g_out"]], [theirs_in, theirs_out]
    parts_late = _chip_partial(late, gs, theirs, name="chip_partial_late", out_dtype=BF16)
    sent_late = _send_start(late, parts_late, name="send_late_start")
    big = {}

    def finish(ws, parts, got, specs, tag, after):
        shards = _exchange_reduced(ws, _chip_reduce(ws, parts, got, name="chip_reduce_" + tag, after=after),
                                   name="exchange_reduced_" + tag)
        deltas = []
        for gw, (n, w, m, v, tr) in zip(shards, specs):
            view = (lambda a: a[0].T) if tr else (lambda a: a[0])
            back = (lambda a: a.T[None]) if tr else (lambda a: a[None])
            res = _adamw(view(w), gw, view(m), view(v), name="adamw_" + n)
            big[n] = tuple(back(r) for r in res)
            deltas.append(res[1])
        return deltas

    done_ffn = finish(FFN_W, parts_ffn, got_ffn, (("w_gate", w_gate, m_w_gate, v_w_gate, True),
                                                  ("w_up", w_up, m_w_up, v_w_up, True),
                                                  ("w_down", w_down, m_w_down, v_w_down, False)), "ffn", sent_late["token"])

    def place(a):
        return lax.dynamic_update_slice(jnp.zeros((3, D_FF), F32), a[0], (0, q * ccols))

    def small_pack(ws, cw):
        nm, ba, lbl, hg, sk, nf, cb, fn = ws
        return _pack_small(dict(norm_mix_w=nm, b_attn=ba, lb_logits=lbl, hg_norm_w=hg,
                                sinks=jnp.broadcast_to(sk.reshape(ATT_HEADS, 1), (ATT_HEADS, LANES)), norm_ffn_w=nf,
                                conv_w=cw, conv_b=cb, final_norm_w=fn))

    wp = small_pack((norm_mix_w, b_attn, lb_logits, hg_norm_w, sinks, norm_ffn_w, conv_b, final_norm_w), conv_w8[:3])
    mp = small_pack((m_norm_mix_w, m_b_attn, m_lb_logits, m_hg_norm_w, m_sinks, m_norm_ffn_w, m_conv_b, m_final_norm_w),
                    place(m_conv_w))
    vp = small_pack((v_norm_mix_w, v_b_attn, v_lb_logits, v_hg_norm_w, v_sinks, v_norm_ffn_w, v_conv_b, v_final_norm_w),
                    place(v_conv_w))
    outs = _small_update(sall, wp, mp, vp, after=sent_late["token"])
    loss = outs[4][0, 0]
    parts_late, got_late = _send_wait(late, sent_late, [*done_ffn, outs[4]], name="send_late_wait")
    finish(late, parts_late, got_late, (("w_in", w_in, m_w_in, v_w_in, True), ("w_out", w_out, m_w_out, v_w_out, False)),
           "late", None)

    def small_out(pk, n, ref):
        if n == "sinks":
            return pk[SMALL_OFF[n]:SMALL_OFF[n] + ATT_HEADS, 0].reshape(ref.shape)
        if n == "conv_w":
            full = _unpack_small(pk, n, (3, D_FF))
            return lax.dynamic_slice(full, (0, q * ccols), (3, ccols))[None]
        return _unpack_small(pk, n, ref.shape)

    refs = dict(norm_mix_w=norm_mix_w, b_attn=b_attn, lb_logits=lb_logits, hg_norm_w=hg_norm_w, sinks=sinks,
                norm_ffn_w=norm_ffn_w, conv_w=conv_w, conv_b=conv_b, final_norm_w=final_norm_w)
    order = ("norm_mix_w", "w_in", "b_attn", "lb_logits", "hg_norm_w", "sinks", "w_out", "norm_ffn_w", "w_gate", "w_up",
             "conv_w", "conv_b", "w_down", "final_norm_w")
    res = [loss, dx[None]]
    for k in range(4):
        for n in order:
            res.append(big[n][k] if n in big else small_out(outs[k], n, refs[n]))
    return tuple(res)
```
